```python
import jax, jax.numpy as jnp
from jax import lax
import numpy as np

D_MODEL = 1024
BATCH = 8
SEQ = 4096
DEPTH = 1

N_MEM = 256
HEAD_DIM = 64
ATTN_WIDTH = D_MODEL // 2
CONV_WIDTH = D_MODEL // 4
XATTN_WIDTH = D_MODEL // 4
N_ATTN_HEADS = ATTN_WIDTH // HEAD_DIM
N_XATTN_HEADS = 4
XATTN_HEAD_DIM = XATTN_WIDTH // N_XATTN_HEADS
MIX_WIDTH = ATTN_WIDTH + CONV_WIDTH + XATTN_WIDTH
IN_PROJ_WIDTH = 3 * ATTN_WIDTH + 3 * CONV_WIDTH + XATTN_WIDTH
DILATED_PATTERNS = ((128, 1), (512, 4), (2048, 16))
CONV_K = 3
D_FF = 4 * D_MODEL
ROPE_THETA = 10000.0
EPS = 1e-6
NEG_INF = -1e30

kernel_name = "hybrid_dilated_attn_shortconv_memxattn_block"


def rms_norm(x, g):
    xf = x.astype(jnp.float32)
    y = xf * lax.rsqrt(jnp.mean(xf * xf, axis=-1, keepdims=True) + EPS)
    return (y * g.astype(jnp.float32)).astype(x.dtype)


def apply_rope(t, positions):
    dh = t.shape[-1]
    half = dh // 2
    inv_freq = jnp.float32(ROPE_THETA) ** (-(jnp.arange(half, dtype=jnp.float32) * 2.0 / dh))
    ang = positions.astype(jnp.float32)[..., None] * inv_freq
    cos = jnp.cos(ang)[:, :, None, :]
    sin = jnp.sin(ang)[:, :, None, :]
    tf = t.astype(jnp.float32)
    t1, t2 = tf[..., :half], tf[..., half:]
    out = jnp.concatenate([t1 * cos - t2 * sin, t1 * sin + t2 * cos], axis=-1)
    return out.astype(t.dtype)


def dilated_window_attention(q, k, v, window, dilation):
    B, S, H, Dh = q.shape
    L = S // dilation
    n_back = window // dilation
    blk = n_back
    nb = -(-L // blk)
    Lp = nb * blk

    def to_blocks(t):
        t = t.reshape(B, L, dilation, H, Dh).transpose(0, 2, 1, 3, 4)
        t = jnp.pad(t, ((0, 0), (0, 0), (0, Lp - L), (0, 0), (0, 0)))
        return t.reshape(B, dilation, nb, blk, H, Dh)

    qb, kb, vb = to_blocks(q), to_blocks(k), to_blocks(v)

    def with_prev(t):
        prev = jnp.pad(t, ((0, 0), (0, 0), (1, 0), (0, 0), (0, 0), (0, 0)))[:, :, :-1]
        return jnp.concatenate([prev, t], axis=3)

    kw, vw = with_prev(kb), with_prev(vb)
    scale = Dh ** -0.5
    s = jnp.einsum('bdnqhc,bdnkhc->bdnhqk', qb, kw,
                   preferred_element_type=jnp.float32) * scale
    qi = jnp.arange(blk)[:, None]
    kj = jnp.arange(2 * blk)[None, :]
    band = (kj >= qi) & (kj <= qi + n_back)
    valid = band[None] & ((jnp.arange(nb)[:, None, None] > 0) | (kj[None] >= blk))
    s = jnp.where(valid[None, None, :, None], s, NEG_INF)
    lse = jax.nn.logsumexp(s, axis=-1)
    p = jnp.exp(s - lse[..., None])
    o = jnp.einsum('bdnhqk,bdnkhc->bdnqhc', p.astype(v.dtype), vw,
                   preferred_element_type=jnp.float32)
    o = o.reshape(B, dilation, Lp, H, Dh)[:, :, :L]
    o = o.transpose(0, 2, 1, 3, 4).reshape(B, S, H, Dh)
    lse = lse.transpose(0, 1, 2, 4, 3).reshape(B, dilation, Lp, H)[:, :, :L]
    lse = lse.transpose(0, 2, 1, 3).reshape(B, S, H)
    return o, lse


def dilated_mixture_attention(q, k, v):
    outs, lses = [], []
    for window, dilation in DILATED_PATTERNS:
        o, lse = dilated_window_attention(q, k, v, window, dilation)
        outs.append(o)
        lses.append(lse)
    w = jax.nn.softmax(jnp.stack(lses, axis=0), axis=0)
    o = jnp.sum(w[..., None] * jnp.stack(outs, axis=0), axis=0)
    return o.astype(q.dtype)


def short_gated_conv(b_gate, c_gate, u, conv_w):
    z = c_gate * u
    S = z.shape[1]
    zp = jnp.pad(z, ((0, 0), (CONV_K - 1, 0), (0, 0)))
    y = zp[:, 0:S] * conv_w[0]
    for tap in range(1, CONV_K):
        y = y + zp[:, tap:tap + S] * conv_w[tap]
    return b_gate * y


def memory_cross_attention(qx, mem_kv):
    B, S, _ = qx.shape
    q = qx.reshape(B, S, N_XATTN_HEADS, XATTN_HEAD_DIM)
    km, vm = jnp.split(mem_kv, 2, axis=-1)
    km = km.reshape(B, -1, N_XATTN_HEADS, XATTN_HEAD_DIM)
    vm = vm.reshape(B, -1, N_XATTN_HEADS, XATTN_HEAD_DIM)
    s = jnp.einsum('bshc,bmhc->bhsm', q, km,
                   preferred_element_type=jnp.float32) * (XATTN_HEAD_DIM ** -0.5)
    p = jax.nn.softmax(s, axis=-1)
    o = jnp.einsum('bhsm,bmhc->bshc', p.astype(vm.dtype), vm)
    return o.reshape(B, S, XATTN_WIDTH)


def _fwd_setup_inputs(seed: int = 0) -> dict:
    key = jax.random.key(seed)
    ks = jax.random.split(key, 20)
    f32 = jnp.float32

    def w(k, shape, fan_in):
        return jax.random.normal(k, shape, f32) * (fan_in ** -0.5)

    def gain(k, width):
        return 1.0 + 0.05 * jax.random.normal(k, (DEPTH, width), f32)

    x = jax.random.normal(ks[0], (BATCH, SEQ, D_MODEL), f32)
    mem = jax.random.normal(ks[1], (BATCH, N_MEM, D_MODEL), f32)
    offset = jax.random.randint(ks[2], (BATCH, 1), 0, 1024, dtype=jnp.int32)
    positions = offset + jnp.arange(SEQ, dtype=jnp.int32)[None, :]
    return {
        "x": x,
        "mem": mem,
        "positions": positions,
        "g_pre_mix": gain(ks[3], D_MODEL),
        "g_mem": gain(ks[4], D_MODEL),
        "w_in": w(ks[5], (DEPTH, D_MODEL, IN_PROJ_WIDTH), D_MODEL),
        "w_mem_kv": w(ks[6], (DEPTH, D_MODEL, 2 * XATTN_WIDTH), D_MODEL),
        "conv_w": w(ks[7], (DEPTH, CONV_K, CONV_WIDTH), CONV_K),
        "g_attn_out": gain(ks[8], ATTN_WIDTH),
        "g_conv_out": gain(ks[9], CONV_WIDTH),
        "g_xattn_out": gain(ks[10], XATTN_WIDTH),
        "w_out": w(ks[11], (DEPTH, MIX_WIDTH, D_MODEL), MIX_WIDTH),
        "g_post_mix": gain(ks[12], D_MODEL),
        "g_pre_mlp": gain(ks[13], D_MODEL),
        "w_up": w(ks[14], (DEPTH, D_MODEL, D_FF), D_MODEL),
        "w_down": w(ks[15], (DEPTH, D_FF, D_MODEL), D_FF),
        "g_post_mlp": gain(ks[16], D_MODEL),
    }


def _fwd_reference(x, mem, positions, g_pre_mix, g_mem, w_in, w_mem_kv, conv_w,
              g_attn_out, g_conv_out, g_xattn_out, w_out, g_post_mix,
              g_pre_mlp, w_up, w_down, g_post_mlp):
    B, S, _ = x.shape
    a0 = ATTN_WIDTH
    c0 = 3 * ATTN_WIDTH
    x0 = 3 * ATTN_WIDTH + 3 * CONV_WIDTH
    for l in range(DEPTH):
        h = rms_norm(x, g_pre_mix[l])
        proj = jnp.einsum('bsd,de->bse', h, w_in[l])

        q = proj[..., 0:a0].reshape(B, S, N_ATTN_HEADS, HEAD_DIM)
        k = proj[..., a0:2 * a0].reshape(B, S, N_ATTN_HEADS, HEAD_DIM)
        v = proj[..., 2 * a0:3 * a0].reshape(B, S, N_ATTN_HEADS, HEAD_DIM)
        q = apply_rope(q, positions)
        k = apply_rope(k, positions)
        y_attn = dilated_mixture_attention(q, k, v).reshape(B, S, ATTN_WIDTH)

        b_gate = proj[..., c0:c0 + CONV_WIDTH]
        c_gate = proj[..., c0 + CONV_WIDTH:c0 + 2 * CONV_WIDTH]
        u = proj[..., c0 + 2 * CONV_WIDTH:c0 + 3 * CONV_WIDTH]
        y_conv = short_gated_conv(b_gate, c_gate, u, conv_w[l])

        qx = proj[..., x0:x0 + XATTN_WIDTH]
        mem_kv = jnp.einsum('bmd,de->bme', rms_norm(mem, g_mem[l]), w_mem_kv[l])
        y_x = memory_cross_attention(qx, mem_kv)

        y = jnp.concatenate([rms_norm(y_attn, g_attn_out[l]),
                             rms_norm(y_conv, g_conv_out[l]),
                             rms_norm(y_x, g_xattn_out[l])], axis=-1)
        y = jnp.einsum('bse,ed->bsd', y, w_out[l])
        x = x + rms_norm(y, g_post_mix[l])

        h2 = rms_norm(x, g_pre_mlp[l])
        f = jnp.square(jax.nn.relu(jnp.einsum('bsd,df->bsf', h2, w_up[l])))
        f = jnp.einsum('bsf,fd->bsd', f, w_down[l])
        x = x + rms_norm(f, g_post_mlp[l])
    return x


import jax as _jax
import jax.numpy as _jnp

TWIN_FORMAT = 'train_step'
FWD_PARAMS = ['x', 'mem', 'positions', 'g_pre_mix', 'g_mem', 'w_in', 'w_mem_kv', 'conv_w', 'g_attn_out', 'g_conv_out', 'g_xattn_out', 'w_out', 'g_post_mix', 'g_pre_mlp', 'w_up', 'w_down', 'g_post_mlp']
TWIN_WEIGHTS = ['g_pre_mix', 'g_mem', 'w_in', 'w_mem_kv', 'conv_w', 'g_attn_out', 'g_conv_out', 'g_xattn_out', 'w_out', 'g_post_mix', 'g_pre_mlp', 'w_up', 'w_down', 'g_post_mlp']
TWIN_DIFF_INPUT = 'x'
TWIN_INPUTS = ['x', 'mem', 'positions', 'g_pre_mix', 'g_mem', 'w_in', 'w_mem_kv', 'conv_w', 'g_attn_out', 'g_conv_out', 'g_xattn_out', 'w_out', 'g_post_mix', 'g_pre_mlp', 'w_up', 'w_down', 'g_post_mlp', 'loss_target', 'm_g_pre_mix', 'm_g_mem', 'm_w_in', 'm_w_mem_kv', 'm_conv_w', 'm_g_attn_out', 'm_g_conv_out', 'm_g_xattn_out', 'm_w_out', 'm_g_post_mix', 'm_g_pre_mlp', 'm_w_up', 'm_w_down', 'm_g_post_mlp', 'v_g_pre_mix', 'v_g_mem', 'v_w_in', 'v_w_mem_kv', 'v_conv_w', 'v_g_attn_out', 'v_g_conv_out', 'v_g_xattn_out', 'v_w_out', 'v_g_post_mix', 'v_g_pre_mlp', 'v_w_up', 'v_w_down', 'v_g_post_mlp']
TWIN_OUTPUTS = ['loss', 'grad_x', 'grad_g_pre_mix', 'grad_g_mem', 'grad_w_in', 'grad_w_mem_kv', 'grad_conv_w', 'grad_g_attn_out', 'grad_g_conv_out', 'grad_g_xattn_out', 'grad_w_out', 'grad_g_post_mix', 'grad_g_pre_mlp', 'grad_w_up', 'grad_w_down', 'grad_g_post_mlp', 'delta_g_pre_mix', 'delta_g_mem', 'delta_w_in', 'delta_w_mem_kv', 'delta_conv_w', 'delta_g_attn_out', 'delta_g_conv_out', 'delta_g_xattn_out', 'delta_w_out', 'delta_g_post_mix', 'delta_g_pre_mlp', 'delta_w_up', 'delta_w_down', 'delta_g_post_mlp', 'new_m_g_pre_mix', 'new_m_g_mem', 'new_m_w_in', 'new_m_w_mem_kv', 'new_m_conv_w', 'new_m_g_attn_out', 'new_m_g_conv_out', 'new_m_g_xattn_out', 'new_m_w_out', 'new_m_g_post_mix', 'new_m_g_pre_mlp', 'new_m_w_up', 'new_m_w_down', 'new_m_g_post_mlp', 'new_v_g_pre_mix', 'new_v_g_mem', 'new_v_w_in', 'new_v_w_mem_kv', 'new_v_conv_w', 'new_v_g_attn_out', 'new_v_g_conv_out', 'new_v_g_xattn_out', 'new_v_w_out', 'new_v_g_post_mix', 'new_v_g_pre_mlp', 'new_v_w_up', 'new_v_w_down', 'new_v_g_post_mlp']
TWIN_LEAF_KINDS = {'loss': 'loss', 'grad_x': 'grad_x', 'grad_g_pre_mix': 'grad_w', 'grad_g_mem': 'grad_w', 'grad_w_in': 'grad_w', 'grad_w_mem_kv': 'grad_w', 'grad_conv_w': 'grad_w', 'grad_g_attn_out': 'grad_w', 'grad_g_conv_out': 'grad_w', 'grad_g_xattn_out': 'grad_w', 'grad_w_out': 'grad_w', 'grad_g_post_mix': 'grad_w', 'grad_g_pre_mlp': 'grad_w', 'grad_w_up': 'grad_w', 'grad_w_down': 'grad_w', 'grad_g_post_mlp': 'grad_w', 'delta_g_pre_mix': 'delta_w', 'delta_g_mem': 'delta_w', 'delta_w_in': 'delta_w', 'delta_w_mem_kv': 'delta_w', 'delta_conv_w': 'delta_w', 'delta_g_attn_out': 'delta_w', 'delta_g_conv_out': 'delta_w', 'delta_g_xattn_out': 'delta_w', 'delta_w_out': 'delta_w', 'delta_g_post_mix': 'delta_w', 'delta_g_pre_mlp': 'delta_w', 'delta_w_up': 'delta_w', 'delta_w_down': 'delta_w', 'delta_g_post_mlp': 'delta_w', 'new_m_g_pre_mix': 'new_m', 'new_m_g_mem': 'new_m', 'new_m_w_in': 'new_m', 'new_m_w_mem_kv': 'new_m', 'new_m_conv_w': 'new_m', 'new_m_g_attn_out': 'new_m', 'new_m_g_conv_out': 'new_m', 'new_m_g_xattn_out': 'new_m', 'new_m_w_out': 'new_m', 'new_m_g_post_mix': 'new_m', 'new_m_g_pre_mlp': 'new_m', 'new_m_w_up': 'new_m', 'new_m_w_down': 'new_m', 'new_m_g_post_mlp': 'new_m', 'new_v_g_pre_mix': 'new_v', 'new_v_g_mem': 'new_v', 'new_v_w_in': 'new_v', 'new_v_w_mem_kv': 'new_v', 'new_v_conv_w': 'new_v', 'new_v_g_attn_out': 'new_v', 'new_v_g_conv_out': 'new_v', 'new_v_g_xattn_out': 'new_v', 'new_v_w_out': 'new_v', 'new_v_g_post_mix': 'new_v', 'new_v_g_pre_mlp': 'new_v', 'new_v_w_up': 'new_v', 'new_v_w_down': 'new_v', 'new_v_g_post_mlp': 'new_v'}


def _forward(args):
    return _fwd_reference(*[args[k] for k in FWD_PARAMS])


def _output_shape():
    def fwd():
        inp = _fwd_setup_inputs(0)
        return _fwd_reference(*[inp[k] for k in FWD_PARAMS])
    out = _jax.eval_shape(fwd)
    return out.shape, out.dtype

N_MICROBATCH = 1
ADAM_LR = 0.001
ADAM_B1 = 0.9
ADAM_B2 = 0.999
ADAM_EPS = 1e-08
ADAM_WD = 0.01
ADAM_STEP = 10
PER_EXAMPLE_BATCH_AXIS = {'x': 0, 'mem': 0, 'positions': 0, 'loss_target': 0}
SHARED_INPUTS = []
_WEIGHT_DTYPES = {'g_pre_mix': _jnp.float32, 'g_mem': _jnp.float32, 'w_in': _jnp.float32, 'w_mem_kv': _jnp.float32, 'conv_w': _jnp.float32, 'g_attn_out': _jnp.float32, 'g_conv_out': _jnp.float32, 'g_xattn_out': _jnp.float32, 'w_out': _jnp.float32, 'g_post_mix': _jnp.float32, 'g_pre_mlp': _jnp.float32, 'w_up': _jnp.float32, 'w_down': _jnp.float32, 'g_post_mlp': _jnp.float32}
MOMENT_SCALE = {'g_pre_mix': 9.518781e-01, 'g_mem': 1.599816e+00, 'w_in': 5.788843e-01, 'w_mem_kv': 2.182616e+00, 'conv_w': 5.367505e-01, 'g_attn_out': 1.355594e+00, 'g_conv_out': 1.033780e+00, 'g_xattn_out': 3.168966e+00, 'w_out': 1.658820e+00, 'g_post_mix': 3.219711e+01, 'g_pre_mlp': 1.048534e+00, 'w_up': 5.374061e-01, 'w_down': 1.793160e+00, 'g_post_mlp': 3.307655e+01}


def _to_microbatches(a, axis):
    t = _jnp.moveaxis(a, axis, 0)
    t = t.reshape((N_MICROBATCH, t.shape[0] // N_MICROBATCH) + t.shape[1:])
    return _jnp.moveaxis(t, 1, axis + 1)


def setup_inputs(seed: int = 0) -> dict:
    inp = _fwd_setup_inputs(seed)
    key = _jax.random.fold_in(_jax.random.key(seed), 7919)
    shape, _ = _output_shape()
    out = dict(inp)
    out["loss_target"] = _jax.random.normal(_jax.random.fold_in(key, 0), shape, _jnp.float32)
    for i, name in enumerate(TWIN_WEIGHTS):
        w = inp[name].astype(_jnp.float32)
        if MOMENT_SCALE is None:
            s = _jnp.sqrt(_jnp.mean(_jnp.square(w)) + 1e-30)
        else:
            s = MOMENT_SCALE[name]
        km, kv = _jax.random.split(_jax.random.fold_in(key, i + 1))
        out[name] = w
        out["m_" + name] = s * _jax.random.normal(km, w.shape, _jnp.float32)
        out["v_" + name] = (s * s) * _jax.random.uniform(kv, w.shape, _jnp.float32, 0.5, 1.5)
    if N_MICROBATCH > 1:
        for name, axis in PER_EXAMPLE_BATCH_AXIS.items():
            out[name] = _to_microbatches(out[name], axis)
    return {'x': out['x'], 'mem': out['mem'], 'positions': out['positions'], 'g_pre_mix': out['g_pre_mix'], 'g_mem': out['g_mem'], 'w_in': out['w_in'], 'w_mem_kv': out['w_mem_kv'], 'conv_w': out['conv_w'], 'g_attn_out': out['g_attn_out'], 'g_conv_out': out['g_conv_out'], 'g_xattn_out': out['g_xattn_out'], 'w_out': out['w_out'], 'g_post_mix': out['g_post_mix'], 'g_pre_mlp': out['g_pre_mlp'], 'w_up': out['w_up'], 'w_down': out['w_down'], 'g_post_mlp': out['g_post_mlp'], 'loss_target': out['loss_target'], 'm_g_pre_mix': out['m_g_pre_mix'], 'm_g_mem': out['m_g_mem'], 'm_w_in': out['m_w_in'], 'm_w_mem_kv': out['m_w_mem_kv'], 'm_conv_w': out['m_conv_w'], 'm_g_attn_out': out['m_g_attn_out'], 'm_g_conv_out': out['m_g_conv_out'], 'm_g_xattn_out': out['m_g_xattn_out'], 'm_w_out': out['m_w_out'], 'm_g_post_mix': out['m_g_post_mix'], 'm_g_pre_mlp': out['m_g_pre_mlp'], 'm_w_up': out['m_w_up'], 'm_w_down': out['m_w_down'], 'm_g_post_mlp': out['m_g_post_mlp'], 'v_g_pre_mix': out['v_g_pre_mix'], 'v_g_mem': out['v_g_mem'], 'v_w_in': out['v_w_in'], 'v_w_mem_kv': out['v_w_mem_kv'], 'v_conv_w': out['v_conv_w'], 'v_g_attn_out': out['v_g_attn_out'], 'v_g_conv_out': out['v_g_conv_out'], 'v_g_xattn_out': out['v_g_xattn_out'], 'v_w_out': out['v_w_out'], 'v_g_post_mix': out['v_g_post_mix'], 'v_g_pre_mlp': out['v_g_pre_mlp'], 'v_w_up': out['v_w_up'], 'v_w_down': out['v_w_down'], 'v_g_post_mlp': out['v_g_post_mlp']}


def _loss(weights, diff, rest, loss_target):
    with _jax.named_scope("forward"):
        args = {**rest, TWIN_DIFF_INPUT: diff, **{k: w.astype(_WEIGHT_DTYPES[k]) for k, w in weights.items()}}
        y = _forward(args)
    with _jax.named_scope("loss_head"):
        err = _jnp.square(y.astype(_jnp.float32) - loss_target)
        return 0.5 * _jnp.sum(_jnp.mean(err, axis=-1)) if err.ndim else 0.5 * err


def _adamw(w, g, m, v):
    m = ADAM_B1 * m + (1.0 - ADAM_B1) * g
    v = ADAM_B2 * v + (1.0 - ADAM_B2) * _jnp.square(g)
    m_hat = m / (1.0 - ADAM_B1 ** ADAM_STEP)
    v_hat = v / (1.0 - ADAM_B2 ** ADAM_STEP)
    delta = -ADAM_LR * (m_hat / (_jnp.sqrt(v_hat) + ADAM_EPS) + ADAM_WD * w)
    return delta, m, v


def reference(x, mem, positions, g_pre_mix, g_mem, w_in, w_mem_kv, conv_w, g_attn_out, g_conv_out, g_xattn_out, w_out, g_post_mix, g_pre_mlp, w_up, w_down, g_post_mlp, loss_target, m_g_pre_mix, m_g_mem, m_w_in, m_w_mem_kv, m_conv_w, m_g_attn_out, m_g_conv_out, m_g_xattn_out, m_w_out, m_g_post_mix, m_g_pre_mlp, m_w_up, m_w_down, m_g_post_mlp, v_g_pre_mix, v_g_mem, v_w_in, v_w_mem_kv, v_conv_w, v_g_attn_out, v_g_conv_out, v_g_xattn_out, v_w_out, v_g_post_mix, v_g_pre_mlp, v_w_up, v_w_down, v_g_post_mlp):
    given = dict(x=x, mem=mem, positions=positions, g_pre_mix=g_pre_mix, g_mem=g_mem, w_in=w_in, w_mem_kv=w_mem_kv, conv_w=conv_w, g_attn_out=g_attn_out, g_conv_out=g_conv_out, g_xattn_out=g_xattn_out, w_out=w_out, g_post_mix=g_post_mix, g_pre_mlp=g_pre_mlp, w_up=w_up, w_down=w_down, g_post_mlp=g_post_mlp, loss_target=loss_target, m_g_pre_mix=m_g_pre_mix, m_g_mem=m_g_mem, m_w_in=m_w_in, m_w_mem_kv=m_w_mem_kv, m_conv_w=m_conv_w, m_g_attn_out=m_g_attn_out, m_g_conv_out=m_g_conv_out, m_g_xattn_out=m_g_xattn_out, m_w_out=m_w_out, m_g_post_mix=m_g_post_mix, m_g_pre_mlp=m_g_pre_mlp, m_w_up=m_w_up, m_w_down=m_w_down, m_g_post_mlp=m_g_post_mlp, v_g_pre_mix=v_g_pre_mix, v_g_mem=v_g_mem, v_w_in=v_w_in, v_w_mem_kv=v_w_mem_kv, v_conv_w=v_conv_w, v_g_attn_out=v_g_attn_out, v_g_conv_out=v_g_conv_out, v_g_xattn_out=v_g_xattn_out, v_w_out=v_w_out, v_g_post_mix=v_g_post_mix, v_g_pre_mlp=v_g_pre_mlp, v_w_up=v_w_up, v_w_down=v_w_down, v_g_post_mlp=v_g_post_mlp)
    weights = {n: given[n] for n in TWIN_WEIGHTS}
    shared = {n: given[n] for n in SHARED_INPUTS}
    per_example = {n: given[n] for n in ['x', 'mem', 'positions']}
    grad_fn = _jax.value_and_grad(_loss, argnums=(0, 1))

    def one_microbatch(ex, loss_target):
        ex = dict(ex)
        diff = ex.pop(TWIN_DIFF_INPUT)
        return grad_fn(weights, diff, {**shared, **ex}, loss_target)

    if N_MICROBATCH == 1:
        loss, (grad_w, grad_x) = one_microbatch(per_example, given["loss_target"])
    else:
        def body(carry, xs):
            loss_sum, grad_sum = carry
            l_k, (gw_k, gx_k) = one_microbatch(xs[0], xs[1])
            with _jax.named_scope("update"):
                return (loss_sum + l_k, _jax.tree.map(_jnp.add, grad_sum, gw_k)), gx_k

        init = (_jnp.zeros((), _jnp.float32), _jax.tree.map(_jnp.zeros_like, weights))
        (loss, grad_w), grad_x = _jax.lax.scan(body, init, (per_example, given["loss_target"]))
    with _jax.named_scope("update"):
        delta_w, new_m, new_v = {}, {}, {}
        for n in TWIN_WEIGHTS:
            delta_w[n], new_m[n], new_v[n] = _adamw(weights[n], grad_w[n], given["m_" + n], given["v_" + n])
    return (loss, grad_x, *[grad_w[n] for n in TWIN_WEIGHTS], *[delta_w[n] for n in TWIN_WEIGHTS],
            *[new_m[n] for n in TWIN_WEIGHTS], *[new_v[n] for n in TWIN_WEIGHTS])
```

```python
import functools

import numpy as np
import jax
import jax.numpy as jnp
from jax import lax
from jax.experimental import pallas as pl
from jax.experimental.pallas import tpu as pltpu

F32, BF16 = jnp.float32, jnp.bfloat16
MESH = pl.DeviceIdType.MESH
ANY = pl.BlockSpec(memory_space=pl.ANY)

N_DEV = 8
D = 1024
S = 4096
N_MEM = 256
HEAD = 64
AW, CW, XW = 512, 256, 256
PW = 3 * AW + 3 * CW + XW
FF = 4096
FF_BLK = FF // N_DEV
PATTERNS = ((128, 1), (512, 4), (2048, 16))
QB = 128
EPS = 1e-6
NEG = -1e30
SCALE = HEAD ** -0.5
ROPE_THETA = 10000.0
LANES = 128
SUBLANES = 8

ADAM_LR, ADAM_B1, ADAM_B2, ADAM_EPS, ADAM_WD, ADAM_STEP = 0.001, 0.9, 0.999, 1e-08, 0.01, 10

TQ = 512
TQ_MLP = 256
NT = S // TQ


def _cparams(vmem_mb, n_grid=1):
    return pltpu.CompilerParams(dimension_semantics=("arbitrary",) * n_grid, vmem_limit_bytes=vmem_mb << 20)


def _const(shape):
    nd = len(shape)
    return pl.BlockSpec(shape, lambda *_: (0,) * nd, pipeline_mode=pl.Buffered(1))


def _acc(shape):
    nd = len(shape)
    return pl.BlockSpec(shape, lambda *_: (0,) * nd)


def _dot(a, b):
    return jnp.dot(a, b, preferred_element_type=F32)


def _dot_nt(a, b):
    return lax.dot_general(a, b, (((1,), (1,)), ((), ())), preferred_element_type=F32)


def _dot_tn(a, b):
    return lax.dot_general(a, b, (((0,), (0,)), ((), ())), preferred_element_type=F32)


def _rms(x, g):
    r = lax.rsqrt(jnp.mean(x * x, axis=-1, keepdims=True) + EPS)
    n = x * r
    return n * g, n, r


def _rms_bwd(dy, n, r, g):
    dn = dy * g
    dx = r * (dn - n * jnp.mean(dn * n, axis=-1, keepdims=True))
    return dx, jnp.sum(dy * n, axis=0, keepdims=True)


def _rot_half(t):
    lane = lax.broadcasted_iota(jnp.int32, t.shape, 1)
    n = t.shape[1]
    return jnp.where((lane % HEAD) < HEAD // 2, pltpu.roll(t, n - HEAD // 2, 1), pltpu.roll(t, HEAD // 2, 1))


def _rope_table(pos_col, invf, sgn):
    def body(p_ref, f_ref, s_ref, c_out, s_out):
        ang = p_ref[...] * f_ref[...]
        c_out[...] = jnp.cos(ang)
        s_out[...] = jnp.sin(ang) * s_ref[...]

    tile = pl.BlockSpec((TQ, AW), lambda i: (i, 0))
    return pl.pallas_call(
        body, grid=(NT,), name="rope_table",
        in_specs=[pl.BlockSpec((TQ, 1), lambda i: (i, 0)), _const((1, AW)), _const((1, AW))],
        out_specs=[tile, tile], out_shape=[jax.ShapeDtypeStruct((S, AW), F32)] * 2,
        compiler_params=_cparams(32))(pos_col, invf, sgn)


def _mem_fwd(mem, g_mem, wkv16):
    def body(m_ref, g_ref, w_ref, n16_ref, kv_ref):
        y, _, _ = _rms(m_ref[...], g_ref[...])
        y16 = y.astype(BF16)
        n16_ref[...] = y16
        kv_ref[...] = _dot(y16, w_ref[...]).astype(BF16)

    return pl.pallas_call(
        body, name="mem_fwd",
        out_shape=[jax.ShapeDtypeStruct((N_MEM, D), BF16), jax.ShapeDtypeStruct((N_MEM, 2 * XW), BF16)],
        compiler_params=pltpu.CompilerParams(vmem_limit_bytes=32 << 20))(mem, g_mem, wkv16)


def _in_proj(x, g, w16, cos, sins):
    def body(x_ref, g_ref, w_ref, c_ref, s_ref, q_ref, k_ref, v_ref, bcu_ref, qx_ref, h_ref):
        y, _, _ = _rms(x_ref[...], g_ref[...])
        h = y.astype(BF16)
        h_ref[...] = h
        proj = _dot(h, w_ref[...])
        cos, sn = c_ref[...], s_ref[...]
        q, k = proj[:, 0:AW], proj[:, AW:2 * AW]
        q_ref[...] = (q * cos + _rot_half(q) * sn) * SCALE
        k_ref[...] = k * cos + _rot_half(k) * sn
        v_ref[...] = proj[:, 2 * AW:3 * AW]
        bcu_ref[...] = proj[:, 3 * AW:3 * AW + 3 * CW]
        qx_ref[...] = (proj[:, 3 * AW + 3 * CW:] * SCALE).astype(BF16)

    def tile(w):
        return pl.BlockSpec((TQ, w), lambda i: (i, 0))

    return pl.pallas_call(
        body, grid=(NT,), name="in_proj",
        in_specs=[tile(D), _const((1, D)), _const((D, PW)), tile(AW), tile(AW)],
        out_specs=[tile(AW), tile(AW), tile(AW), tile(3 * CW), tile(XW), tile(D)],
        out_shape=[jax.ShapeDtypeStruct((S, AW), F32)] * 3 + [
            jax.ShapeDtypeStruct((S, 3 * CW), F32), jax.ShapeDtypeStruct((S, XW), BF16),
            jax.ShapeDtypeStruct((S, D), BF16)],
        compiler_params=_cparams(56))(x, g, w16, cos, sins)


def _band_bias():
    qi = lax.broadcasted_iota(jnp.int32, (QB, QB), 0)
    kj = lax.broadcasted_iota(jnp.int32, (QB, QB), 1)
    return jnp.where(kj >= qi, 0.0, NEG).astype(F32), jnp.where(kj <= qi, 0.0, NEG).astype(F32)


def _attn_fwd(q, k, v):
    def body(q_ref, k_ref, v_ref, y_ref, lt_ref, og0, og1, og2, lg0, lg1, lg2):
        og, lg = (og0, og1, og2), (lg0, lg1, lg2)
        lane = lax.broadcasted_iota(jnp.int32, (QB, LANES), 1)
        heads = (lane < HEAD, lane >= HEAD)
        bias_prev, bias_own = _band_bias()
        for gi, (window, d) in enumerate(PATTERNS):
            nb = S // d // QB
            for r in range(d):
                def block(n, carry, gi=gi, d=d, r=r):
                    own = pl.ds(r + n * (QB * d), QB, stride=d)
                    prev = pl.ds(r + jnp.maximum(n - 1, 0) * (QB * d), QB, stride=d)
                    bp = jnp.where(n > 0, bias_prev, NEG)
                    qb = q_ref[own, :].astype(BF16)
                    ko, vo = k_ref[own, :].astype(BF16), v_ref[own, :].astype(BF16)
                    kp, vp = k_ref[prev, :].astype(BF16), v_ref[prev, :].astype(BF16)
                    o_pair = jnp.zeros((QB, LANES), F32)
                    l_pair = jnp.zeros((QB, LANES), F32)
                    for hm in heads:
                        qm = jnp.where(hm, qb, jnp.zeros_like(qb))
                        so = _dot_nt(qm, ko) + bias_own
                        sp = _dot_nt(qm, kp) + bp
                        m = jnp.maximum(jnp.max(so, axis=1, keepdims=True), jnp.max(sp, axis=1, keepdims=True))
                        po, pp = jnp.exp(so - m), jnp.exp(sp - m)
                        l = jnp.sum(po, axis=1, keepdims=True) + jnp.sum(pp, axis=1, keepdims=True)
                        o = (_dot(po.astype(BF16), vo) + _dot(pp.astype(BF16), vp)) / l
                        o_pair = jnp.where(hm, o, o_pair)
                        l_pair = jnp.where(hm, m + jnp.log(l), l_pair)
                    og[gi][own, :] = o_pair
                    lg[gi][own, :] = l_pair
                    return carry
                lax.fori_loop(0, nb, block, 0)

        for t in range(S // TQ):
            rows = pl.ds(t * TQ, TQ)
            l0, l1, l2 = lg0[rows, :], lg1[rows, :], lg2[rows, :]
            lm = jnp.maximum(jnp.maximum(l0, l1), l2)
            e0, e1, e2 = jnp.exp(l0 - lm), jnp.exp(l1 - lm), jnp.exp(l2 - lm)
            den = e0 + e1 + e2
            y_ref[rows, :] = (e0 * og0[rows, :] + e1 * og1[rows, :] + e2 * og2[rows, :]) / den
            lt_ref[rows, :] = lm + jnp.log(den)

    col = pl.BlockSpec((S, LANES), lambda h: (0, h))
    return pl.pallas_call(
        body, grid=(AW // LANES,), name="attn_fwd",
        in_specs=[col, col, col], out_specs=[col, col],
        out_shape=[jax.ShapeDtypeStruct((S, AW), F32)] * 2,
        scratch_shapes=[pltpu.VMEM((S, LANES), F32)] * 6,
        compiler_params=_cparams(56))(q, k, v)


def _conv_taps(z, zprev, row):
    z1 = jnp.where(row == 0, zprev[7:8, :], pltpu.roll(z, 1, 0))
    z2 = jnp.where(row == 0, zprev[6:7, :], jnp.where(row == 1, zprev[7:8, :], pltpu.roll(z, 2, 0)))
    return z1, z2


def _xattn_scores(qm, km):
    s = _dot_nt(qm, km)
    m = jnp.max(s, axis=1, keepdims=True)
    e = jnp.exp(s - m)
    return e, jnp.sum(e, axis=1, keepdims=True)


def _mix_out(y_attn, bcu, qx16, kv16, cw8, g_attn, g_conv, g_x, g_post, wout16, x):
    def body(ya_ref, bcu_ref, halo_ref, qx_ref, kv_ref, cw_ref, ga_ref, gc_ref, gx_ref, gp_ref, w_ref, x_ref,
             ypre_ref, y16_ref, y2_ref, x1_ref):
        i = pl.program_id(0)
        bcu = bcu_ref[...]
        b, c, u = bcu[:, 0:CW], bcu[:, CW:2 * CW], bcu[:, 2 * CW:]
        z = c * u
        halo = halo_ref[...]
        zprev = jnp.where(i > 0, halo[:, CW:2 * CW] * halo[:, 2 * CW:], 0.0)
        row = lax.broadcasted_iota(jnp.int32, z.shape, 0)
        z1, z2 = _conv_taps(z, zprev, row)
        cw = cw_ref[...]
        y_conv = b * (z2 * cw[0:1, :] + z1 * cw[1:2, :] + z * cw[2:3, :])

        qx = qx_ref[...]
        kv = kv_ref[...]
        km, vm = kv[:, 0:XW], kv[:, XW:]
        lane = lax.broadcasted_iota(jnp.int32, qx.shape, 1)
        y_x = jnp.zeros(qx.shape, F32)
        for h in range(XW // HEAD):
            hm = (lane >= h * HEAD) & (lane < (h + 1) * HEAD)
            e, l = _xattn_scores(jnp.where(hm, qx, jnp.zeros_like(qx)), km)
            y_x = jnp.where(hm, _dot(e.astype(BF16), vm) / l, y_x)

        y_attn = ya_ref[...]
        ypre_ref[:, 0:AW] = y_attn
        ypre_ref[:, AW:AW + CW] = y_conv
        ypre_ref[:, AW + CW:] = y_x
        y = jnp.concatenate([_rms(y_attn, ga_ref[...])[0], _rms(y_conv, gc_ref[...])[0],
                             _rms(y_x, gx_ref[...])[0]], axis=1).astype(BF16)
        y16_ref[...] = y
        y2 = _dot(y, w_ref[...])
        y2_ref[...] = y2
        x1_ref[...] = x_ref[...] + _rms(y2, gp_ref[...])[0]

    def tile(w):
        return pl.BlockSpec((TQ, w), lambda i: (i, 0))

    halo = pl.BlockSpec((SUBLANES, 3 * CW), lambda i: (jnp.maximum(i * (TQ // SUBLANES) - 1, 0), 0))
    return pl.pallas_call(
        body, grid=(NT,), name="mix_out",
        in_specs=[tile(AW), tile(3 * CW), halo, tile(XW), _const((N_MEM, 2 * XW)), _const((SUBLANES, CW)),
                  _const((1, AW)), _const((1, CW)), _const((1, XW)), _const((1, D)), _const((D, D)), tile(D)],
        out_specs=[tile(D), tile(D), tile(D), tile(D)],
        out_shape=[jax.ShapeDtypeStruct((S, D), F32), jax.ShapeDtypeStruct((S, D), BF16),
                   jax.ShapeDtypeStruct((S, D), F32), jax.ShapeDtypeStruct((S, D), F32)],
        compiler_params=_cparams(56))(y_attn, bcu, bcu, qx16, kv16, cw8, g_attn, g_conv, g_x, g_post, wout16, x)


def _mlp(x1, tgt, g_pre, g_post, wup8, wdn16):
    tq = TQ_MLP

    def body(x1_ref, t_ref, g1_ref, g2_ref, wu_ref, wd_ref,
             a16_ref, du_ref, h2_ref, df2_ref, dx1_ref, loss_ref, dg_ref, a32):
        @pl.when(pl.program_id(0) == 0)
        def _():
            loss_ref[...] = jnp.zeros_like(loss_ref)
            dg_ref[...] = jnp.zeros_like(dg_ref)

        x1 = x1_ref[...]
        g1, g2 = g1_ref[...], g2_ref[...]
        y1, n1, r1 = _rms(x1, g1)
        h2 = y1.astype(BF16)
        h2_ref[...] = h2
        f2 = jnp.zeros((tq, D), F32)
        for j in range(N_DEV):
            cols = slice(j * FF_BLK, (j + 1) * FF_BLK)
            a = jnp.maximum(_dot(h2, wu_ref[j]), 0.0)
            a32[:, cols] = a
            a16_ref[:, cols] = a.astype(BF16)
            f2 = f2 + _dot((a * a).astype(BF16), wd_ref[cols, :])
        y2, n2, r2 = _rms(f2, g2)
        e = x1 + y2 - t_ref[...]
        sq = jnp.sum(jnp.sum(e * e, axis=1, keepdims=True), axis=0, keepdims=True)
        loss_ref[...] += jnp.broadcast_to(sq * (0.5 / D), loss_ref.shape)
        dout = e * (1.0 / D)
        df2, dg2 = _rms_bwd(dout, n2, r2, g2)
        df2_16 = df2.astype(BF16)
        df2_ref[...] = df2_16
        dh2 = jnp.zeros((tq, D), F32)
        for j in range(N_DEV):
            cols = slice(j * FF_BLK, (j + 1) * FF_BLK)
            du = (_dot_nt(df2_16, wd_ref[cols, :]) * (2.0 * a32[:, cols])).astype(BF16)
            du_ref[:, cols] = du
            dh2 = dh2 + _dot_nt(du, wu_ref[j])
        dx, dg1 = _rms_bwd(dh2, n1, r1, g1)
        dx1_ref[...] = dout + dx
        dg_ref[0:1, :] += dg2
        dg_ref[1:2, :] += dg1

    def tile(w):
        return pl.BlockSpec((tq, w), lambda i: (i, 0))

    return pl.pallas_call(
        body, grid=(S // tq,), name="mlp",
        in_specs=[tile(D), tile(D), _const((1, D)), _const((1, D)), _const((N_DEV, D, FF_BLK)), _const((FF, D))],
        out_specs=[tile(FF), tile(FF), tile(D), tile(D), tile(D), _acc((SUBLANES, LANES)), _acc((SUBLANES, D))],
        out_shape=[jax.ShapeDtypeStruct((S, FF), BF16), jax.ShapeDtypeStruct((S, FF), BF16),
                   jax.ShapeDtypeStruct((S, D), BF16), jax.ShapeDtypeStruct((S, D), BF16),
                   jax.ShapeDtypeStruct((S, D), F32), jax.ShapeDtypeStruct((SUBLANES, LANES), F32),
                   jax.ShapeDtypeStruct((SUBLANES, D), F32)],
        scratch_shapes=[pltpu.VMEM((tq, FF), F32)],
        compiler_params=_cparams(56))(x1, tgt, g_pre, g_post, wup8, wdn16)


def _mix_out_bwd(dx1, y2, ypre, bcu, qx16, kv16, cw8, g_post, g_attn, g_conv, g_x, wout16):
    def body(dx1_ref, y2_ref, ypre_ref, bcu_ref, halo_ref, qx_ref, kv_ref, cw_ref, gp_ref, ga_ref, gc_ref, gx_ref,
             w_ref, dy2_ref, dya_ref, dbcu_ref, dqx_ref, dgs_ref, dcw_ref, dkv_ref, carry):
        i = pl.program_id(0)

        @pl.when(i == 0)
        def _():
            dgs_ref[...] = jnp.zeros_like(dgs_ref)
            dcw_ref[...] = jnp.zeros_like(dcw_ref)
            dkv_ref[...] = jnp.zeros_like(dkv_ref)
            carry[...] = jnp.zeros_like(carry)

        gp = gp_ref[...]
        _, n, r = _rms(y2_ref[...], gp)
        dy2, dgp = _rms_bwd(dx1_ref[...], n, r, gp)
        dy2_16 = dy2.astype(BF16)
        dy2_ref[...] = dy2_16
        dy = _dot_nt(dy2_16, w_ref[...])

        ypre = ypre_ref[...]
        ga, gc, gx = ga_ref[...], gc_ref[...], gx_ref[...]
        _, na, ra = _rms(ypre[:, 0:AW], ga)
        dya, dga = _rms_bwd(dy[:, 0:AW], na, ra, ga)
        _, nc, rc = _rms(ypre[:, AW:AW + CW], gc)
        dyc, dgc = _rms_bwd(dy[:, AW:AW + CW], nc, rc, gc)
        y_x = ypre[:, AW + CW:]
        _, nx, rx = _rms(y_x, gx)
        dyx, dgx = _rms_bwd(dy[:, AW + CW:], nx, rx, gx)
        dya_ref[...] = dya
        dgs_ref[0:1, :] += dgp
        dgs_ref[1:2, :] += jnp.concatenate([dga, dgc, dgx], axis=1)

        bcu = bcu_ref[...]
        b, c, u = bcu[:, 0:CW], bcu[:, CW:2 * CW], bcu[:, 2 * CW:]
        z = c * u
        halo = halo_ref[...]
        zprev = jnp.where(i < NT - 1, halo[:, CW:2 * CW] * halo[:, 2 * CW:], 0.0)
        row = lax.broadcasted_iota(jnp.int32, z.shape, 0)
        z1, z2 = _conv_taps(z, zprev, row)
        cw = cw_ref[...]
        conv = z2 * cw[0:1, :] + z1 * cw[1:2, :] + z * cw[2:3, :]
        dconv = dyc * b
        nxt = carry[...]
        dn1 = jnp.where(row == TQ - 1, nxt[0:1, :], pltpu.roll(dconv, TQ - 1, 0))
        dn2 = jnp.where(row == TQ - 1, nxt[1:2, :], jnp.where(row == TQ - 2, nxt[0:1, :], pltpu.roll(dconv, TQ - 2, 0)))
        carry[...] = dconv[0:SUBLANES, :]
        dz = dconv * cw[2:3, :] + dn1 * cw[1:2, :] + dn2 * cw[0:1, :]
        dbcu_ref[:, 0:CW] = dyc * conv
        dbcu_ref[:, CW:2 * CW] = dz * u
        dbcu_ref[:, 2 * CW:] = dz * c
        dcw_ref[0:1, :] += jnp.sum(z2 * dconv, axis=0, keepdims=True)
        dcw_ref[1:2, :] += jnp.sum(z1 * dconv, axis=0, keepdims=True)
        dcw_ref[2:3, :] += jnp.sum(z * dconv, axis=0, keepdims=True)

        qx = qx_ref[...]
        kv = kv_ref[...]
        km, vm = kv[:, 0:XW], kv[:, XW:]
        lane = lax.broadcasted_iota(jnp.int32, qx.shape, 1)
        dqx = jnp.zeros(qx.shape, F32)
        dkm = jnp.zeros((N_MEM, XW), F32)
        dvm = jnp.zeros((N_MEM, XW), F32)
        for h in range(XW // HEAD):
            hm = (lane >= h * HEAD) & (lane < (h + 1) * HEAD)
            qm = jnp.where(hm, qx, jnp.zeros_like(qx))
            e, l = _xattn_scores(qm, km)
            p = e / l
            dom = jnp.where(hm, dyx, 0.0)
            do16 = dom.astype(BF16)
            dsum = jnp.sum(dom * y_x, axis=1, keepdims=True)
            ds = (p * (_dot_nt(do16, vm) - dsum)).astype(BF16)
            dqx = jnp.where(hm, _dot(ds, km), dqx)
            dkm = dkm + _dot_tn(ds, qm)
            dvm = dvm + _dot_tn(p.astype(BF16), do16)
        dqx_ref[...] = dqx * SCALE
        dkv_ref[:, 0:XW] += dkm
        dkv_ref[:, XW:] += dvm

    def tile(w):
        return pl.BlockSpec((TQ, w), lambda i: (NT - 1 - i, 0))

    halo = pl.BlockSpec((SUBLANES, 3 * CW), lambda i: (jnp.maximum((NT - 1 - i) * (TQ // SUBLANES) - 1, 0), 0))
    return pl.pallas_call(
        body, grid=(NT,), name="mix_out_bwd",
        in_specs=[tile(D), tile(D), tile(D), tile(3 * CW), halo, tile(XW), _const((N_MEM, 2 * XW)),
                  _const((SUBLANES, CW)), _const((1, D)), _const((1, AW)), _const((1, CW)), _const((1, XW)),
                  _const((D, D))],
        out_specs=[tile(D), tile(AW), tile(3 * CW), tile(XW), _acc((SUBLANES, D)), _acc((SUBLANES, CW)),
                   _acc((N_MEM, 2 * XW))],
        out_shape=[jax.ShapeDtypeStruct((S, D), BF16), jax.ShapeDtypeStruct((S, AW), F32),
                   jax.ShapeDtypeStruct((S, 3 * CW), F32), jax.ShapeDtypeStruct((S, XW), F32),
                   jax.ShapeDtypeStruct((SUBLANES, D), F32), jax.ShapeDtypeStruct((SUBLANES, CW), F32),
                   jax.ShapeDtypeStruct((N_MEM, 2 * XW), F32)],
        scratch_shapes=[pltpu.VMEM((SUBLANES, CW), F32)],
        compiler_params=_cparams(56))(dx1, y2, ypre, bcu, bcu, qx16, kv16, cw8, g_post, g_attn, g_conv, g_x, wout16)


def _attn_bwd(q, k, v, dya, ypre, ltot):
    def body(q_ref, k_ref, v_ref, do_ref, o_ref, lt_ref, dq_ref, dk_ref, dv_ref):
        lane = lax.broadcasted_iota(jnp.int32, (QB, LANES), 1)
        heads = ((lane < HEAD, 0), (lane >= HEAD, HEAD))
        bias_prev, bias_own = _band_bias()
        dq_ref[...] = jnp.zeros_like(dq_ref)
        dk_ref[...] = jnp.zeros_like(dk_ref)
        dv_ref[...] = jnp.zeros_like(dv_ref)
        for window, d in PATTERNS:
            nb = S // d // QB
            for r in range(d):
                def block(n, carry, d=d, r=r):
                    own = pl.ds(r + n * (QB * d), QB, stride=d)
                    prev = pl.ds(r + jnp.maximum(n - 1, 0) * (QB * d), QB, stride=d)
                    bp = jnp.where(n > 0, bias_prev, NEG)
                    qb = q_ref[own, :].astype(BF16)
                    ko, vo = k_ref[own, :].astype(BF16), v_ref[own, :].astype(BF16)
                    kp, vp = k_ref[prev, :].astype(BF16), v_ref[prev, :].astype(BF16)
                    do, o, lt = do_ref[own, :], o_ref[own, :], lt_ref[own, :]
                    dq = jnp.zeros((QB, LANES), F32)
                    dko = jnp.zeros((QB, LANES), F32)
                    dkp = jnp.zeros((QB, LANES), F32)
                    dvo = jnp.zeros((QB, LANES), F32)
                    dvp = jnp.zeros((QB, LANES), F32)
                    for hm, first in heads:
                        qm = jnp.where(hm, qb, jnp.zeros_like(qb))
                        dom = jnp.where(hm, do, 0.0)
                        do16 = dom.astype(BF16)
                        dsum = jnp.sum(dom * o, axis=1, keepdims=True)
                        lth = lt[:, first:first + 1]
                        po = jnp.exp(_dot_nt(qm, ko) + bias_own - lth)
                        pp = jnp.exp(_dot_nt(qm, kp) + bp - lth)
                        dso = (po * (_dot_nt(do16, vo) - dsum)).astype(BF16)
                        dsp = (pp * (_dot_nt(do16, vp) - dsum)).astype(BF16)
                        dq = jnp.where(hm, _dot(dso, ko) + _dot(dsp, kp), dq)
                        dko = dko + _dot_tn(dso, qm)
                        dkp = dkp + _dot_tn(dsp, qm)
                        dvo = dvo + _dot_tn(po.astype(BF16), do16)
                        dvp = dvp + _dot_tn(pp.astype(BF16), do16)
                    dq_ref[own, :] += dq
                    dk_ref[own, :] += dko
                    dv_ref[own, :] += dvo
                    dk_ref[prev, :] += dkp
                    dv_ref[prev, :] += dvp
                    return carry
                lax.fori_loop(0, nb, block, 0)

    col = pl.BlockSpec((S, LANES), lambda h: (0, h))
    return pl.pallas_call(
        body, grid=(AW // LANES,), name="attn_bwd",
        in_specs=[col] * 6, out_specs=[col] * 3,
        out_shape=[jax.ShapeDtypeStruct((S, AW), F32)] * 3,
        compiler_params=_cparams(56))(q, k, v, dya, ypre, ltot)


def _in_proj_bwd(dq, dk, dv, dbcu, dqx, cos, sins, w16, x, g, dx1):
    def body(dq_ref, dk_ref, dv_ref, dbcu_ref, dqx_ref, c_ref, s_ref, w_ref, x_ref, g_ref, dx1_ref,
             dp_ref, gx_ref, dg_ref):
        @pl.when(pl.program_id(0) == 0)
        def _():
            dg_ref[...] = jnp.zeros_like(dg_ref)

        cos, sn = c_ref[...], s_ref[...]
        dqr = dq_ref[...] * SCALE
        dkr = dk_ref[...]
        dp = jnp.concatenate([dqr * cos + _rot_half(dqr * sn), dkr * cos + _rot_half(dkr * sn), dv_ref[...],
                              dbcu_ref[...], dqx_ref[...]], axis=1).astype(BF16)
        dp_ref[...] = dp
        dh = _dot_nt(dp, w_ref[...])
        g = g_ref[...]
        _, n, r = _rms(x_ref[...], g)
        dx, dg = _rms_bwd(dh, n, r, g)
        gx_ref[...] = dx1_ref[...] + dx
        dg_ref[0:1, :] += dg

    def tile(w):
        return pl.BlockSpec((TQ, w), lambda i: (i, 0))

    return pl.pallas_call(
        body, grid=(NT,), name="in_proj_bwd",
        in_specs=[tile(AW), tile(AW), tile(AW), tile(3 * CW), tile(XW), tile(AW), tile(AW), _const((D, PW)),
                  tile(D), _const((1, D)), tile(D)],
        out_specs=[tile(PW), tile(D), _acc((SUBLANES, D))],
        out_shape=[jax.ShapeDtypeStruct((S, PW), BF16), jax.ShapeDtypeStruct((S, D), F32),
                   jax.ShapeDtypeStruct((SUBLANES, D), F32)],
        compiler_params=_cparams(56))(dq, dk, dv, dbcu, dqx, cos, sins, w16, x, g, dx1)


def _mem_bwd(mem, g_mem, wkv16, dkv):
    def body(m_ref, g_ref, w_ref, dkv_ref, dkv16_ref, dg_ref):
        dkv16 = dkv_ref[...].astype(BF16)
        dkv16_ref[...] = dkv16
        _, n, _ = _rms(m_ref[...], g_ref[...])
        dg = jnp.sum(_dot_nt(dkv16, w_ref[...]) * n, axis=0, keepdims=True)
        dg_ref[...] = jnp.broadcast_to(dg, dg_ref.shape)

    return pl.pallas_call(
        body, name="mem_bwd",
        out_shape=[jax.ShapeDtypeStruct((N_MEM, 2 * XW), BF16), jax.ShapeDtypeStruct((SUBLANES, D), F32)],
        compiler_params=pltpu.CompilerParams(vmem_limit_bytes=32 << 20))(mem, g_mem, wkv16, dkv)


def _wgrad(a16, b16, tn, name, square_b=False, transpose_out=False):
    kk, m = a16.shape
    n_tiles = b16.shape[1] // tn
    chunk = min(kk, 512)
    oshape = (tn, m) if transpose_out else (m, tn)

    def body(a_ref, b_ref, o32_ref, o16_ref, at):
        @pl.when(pl.program_id(0) == 0)
        def _():
            for c in range(kk // chunk):
                at[:, c * chunk:(c + 1) * chunk] = a_ref[c * chunk:(c + 1) * chunk, :].T

        b = b_ref[...]
        if square_b:
            b = b * b
        acc = _dot(at[...], b)
        if transpose_out:
            acc = acc.T
        o32_ref[0] = acc
        o16_ref[0] = acc.astype(BF16)

    oblk = pl.BlockSpec((1,) + oshape, lambda j: (j, 0, 0))
    return pl.pallas_call(
        body, grid=(n_tiles,), name=name,
        in_specs=[_const((kk, m)), pl.BlockSpec((kk, tn), lambda j: (0, j))],
        out_specs=[oblk, oblk],
        out_shape=[jax.ShapeDtypeStruct((n_tiles,) + oshape, F32), jax.ShapeDtypeStruct((n_tiles,) + oshape, BF16)],
        scratch_shapes=[pltpu.VMEM((m, kk), BF16)],
        compiler_params=_cparams(56))(a16, b16)


def _adamw_math(w, g, m, v):
    m = ADAM_B1 * m + (1.0 - ADAM_B1) * g
    v = ADAM_B2 * v + (1.0 - ADAM_B2) * jnp.square(g)
    m_hat = m / (1.0 - ADAM_B1 ** ADAM_STEP)
    v_hat = v / (1.0 - ADAM_B2 ** ADAM_STEP)
    delta = -ADAM_LR * (m_hat / (jnp.sqrt(v_hat) + ADAM_EPS) + ADAM_WD * w)
    return delta, m, v


def _adamw_shard(own32, rb16, w, m, v, name):
    def body(o_ref, r_ref, w_ref, m_ref, v_ref, g_out, d_out, m_out, v_out):
        g = o_ref[...] + r_ref[0].astype(F32) + r_ref[1].astype(F32) + r_ref[2].astype(F32)
        g_out[...] = g
        d_out[...], m_out[...], v_out[...] = _adamw_math(w_ref[...], g, m_ref[...], v_ref[...])

    return pl.pallas_call(
        body, name=name, out_shape=[jax.ShapeDtypeStruct(w.shape, F32)] * 4,
        compiler_params=pltpu.CompilerParams(vmem_limit_bytes=48 << 20))(own32, rb16, w, m, v)


def _adamw_small(g, w, m, v):
    def body(g_ref, w_ref, m_ref, v_ref, d_out, m_out, v_out):
        d_out[...], m_out[...], v_out[...] = _adamw_math(w_ref[...], g_ref[...], m_ref[...], v_ref[...])

    return pl.pallas_call(body, name="adamw_small", out_shape=[jax.ShapeDtypeStruct(w.shape, F32)] * 3)(g, w, m, v)


def _place():
    x, y, c = lax.axis_index("x"), lax.axis_index("y"), lax.axis_index("c")
    chips = [(1 - x, y), (x, 1 - y), (1 - x, 1 - y)]
    return x, y, c, chips


def _all_gather(shards):
    nt = len(shards)

    def body(*refs):
        ins, outs = refs[:nt], refs[nt:2 * nt]
        send, recv, lsem = refs[2 * nt:]
        x, y, c, chips = _place()
        me, sib = (x, y, c), (x, y, 1 - c)

        def slot(t, px, py, pc):
            return outs[t].at[4 * px + 2 * py + pc]

        def copy(t, k, block, to, src=None):
            return pltpu.make_async_remote_copy(
                src_ref=slot(t, *block) if src is None else src, dst_ref=slot(t, *block),
                send_sem=send.at[t, k], recv_sem=recv.at[t, k], device_id=to, device_id_type=MESH)

        mine = [pltpu.make_async_copy(ins[t], slot(t, *me), lsem.at[t]) for t in range(nt)]
        for cp in mine:
            cp.start()
        first = []
        for t in range(nt):
            first.append(copy(t, 0, me, sib, src=ins[t]))
            first += [copy(t, 1 + j, me, (*chip, c), src=ins[t]) for j, chip in enumerate(chips)]
        for cp in first:
            cp.start()
        passed = []
        for j, chip in enumerate(chips):
            for t in range(nt):
                copy(t, 1 + j, (*chip, c), me).wait_recv()
                fwd = copy(t, 4 + j, (*chip, c), sib)
                fwd.start()
                passed.append(fwd)
        for t in range(nt):
            copy(t, 0, sib, me).wait_recv()
            for j, chip in enumerate(chips):
                copy(t, 4 + j, (*chip, 1 - c), me).wait_recv()
        for cp in first + passed:
            cp.wait_send()
        for cp in mine:
            cp.wait()

    return pl.pallas_call(
        body, name="all_gather_weights",
        in_specs=[ANY] * nt, out_specs=[ANY] * nt,
        out_shape=[jax.ShapeDtypeStruct((N_DEV,) + s.shape, s.dtype) for s in shards],
        scratch_shapes=[pltpu.SemaphoreType.DMA((nt, 7)), pltpu.SemaphoreType.DMA((nt, 7)),
                        pltpu.SemaphoreType.DMA((nt,))])(*shards)


def _rs_pair(g16s):
    nt = len(g16s)

    def body(*refs):
        ins, outs = refs[:nt], refs[nt:2 * nt]
        send, recv = refs[2 * nt:]
        x, y, c, _ = _place()
        copies = [pltpu.make_async_remote_copy(
            src_ref=ins[t].at[2 * p + (1 - c)], dst_ref=outs[t].at[p], send_sem=send.at[t, p], recv_sem=recv.at[t, p],
            device_id=(x, y, 1 - c), device_id_type=MESH) for t in range(nt) for p in range(4)]
        for cp in copies:
            cp.start()
        for cp in copies:
            cp.wait()

    return pl.pallas_call(
        body, name="reduce_scatter_pair",
        in_specs=[ANY] * nt, out_specs=[ANY] * nt,
        out_shape=[jax.ShapeDtypeStruct((4,) + g.shape[1:], g.dtype) for g in g16s],
        scratch_shapes=[pltpu.SemaphoreType.DMA((nt, 4)), pltpu.SemaphoreType.DMA((nt, 4))])(*g16s)


def _rs_pair_add(place, g32, ra16, name):
    shp = g32.shape[1:]

    def body(pl_ref, g_ref, r_ref, cs_ref, own_ref):
        s = g_ref[0] + r_ref[0].astype(F32)
        cs_ref[0] = s.astype(BF16)

        @pl.when(pl.program_id(0) == pl_ref[1])
        def _():
            own_ref[...] = s

    blk = (1,) + shp
    return pl.pallas_call(
        body, name=name,
        grid_spec=pltpu.PrefetchScalarGridSpec(
            num_scalar_prefetch=1, grid=(4,),
            in_specs=[pl.BlockSpec(blk, lambda p, s: (2 * p + s[0], 0, 0)), pl.BlockSpec(blk, lambda p, s: (p, 0, 0))],
            out_specs=[pl.BlockSpec(blk, lambda p, s: (p, 0, 0)), pl.BlockSpec(shp, lambda p, s: (0, 0))]),
        out_shape=[jax.ShapeDtypeStruct((4,) + shp, BF16), jax.ShapeDtypeStruct(shp, F32)],
        compiler_params=_cparams(48))(place, g32, ra16)


def _rs_chips(cs16s):
    nt = len(cs16s)

    def body(*refs):
        ins, outs = refs[:nt], refs[nt:2 * nt]
        send, recv = refs[2 * nt:]
        x, y, c, chips = _place()
        copies = [pltpu.make_async_remote_copy(
            src_ref=ins[t].at[2 * px + py], dst_ref=outs[t].at[j], send_sem=send.at[t, j], recv_sem=recv.at[t, j],
            device_id=(px, py, c), device_id_type=MESH) for t in range(nt) for j, (px, py) in enumerate(chips)]
        for cp in copies:
            cp.start()
        for cp in copies:
            cp.wait()

    return pl.pallas_call(
        body, name="reduce_scatter_chips",
        in_specs=[ANY] * nt, out_specs=[ANY] * nt,
        out_shape=[jax.ShapeDtypeStruct((3,) + g.shape[1:], g.dtype) for g in cs16s],
        scratch_shapes=[pltpu.SemaphoreType.DMA((nt, 3)), pltpu.SemaphoreType.DMA((nt, 3))])(*cs16s)


def _all_reduce_small(pack):
    def body(p_ref, o_ref, land, send, recv):
        x, y, c, _ = _place()
        me = 4 * x + 2 * y + c
        land[me] = p_ref[...]
        copies = []
        for k in range(1, N_DEV):
            kx, ky, kc = (k >> 2) & 1, (k >> 1) & 1, k & 1
            peer = (1 - x if kx else x, 1 - y if ky else y, 1 - c if kc else c)
            copies.append(pltpu.make_async_remote_copy(
                src_ref=p_ref, dst_ref=land.at[me], send_sem=send.at[k - 1], recv_sem=recv.at[k - 1],
                device_id=peer, device_id_type=MESH))
        for cp in copies:
            cp.start()
        for cp in copies:
            cp.wait()
        tot = land[0]
        for s in range(1, N_DEV):
            tot = tot + land[s]
        o_ref[...] = tot

    return pl.pallas_call(
        body, name="all_reduce_small", out_shape=jax.ShapeDtypeStruct(pack.shape, F32),
        in_specs=[pl.BlockSpec(memory_space=pltpu.VMEM)], out_specs=pl.BlockSpec(memory_space=pltpu.VMEM),
        scratch_shapes=[pltpu.VMEM((N_DEV,) + pack.shape, F32), pltpu.SemaphoreType.DMA((N_DEV - 1,)),
                        pltpu.SemaphoreType.DMA((N_DEV - 1,))])(pack)


GAINS = (("g_pre_mix", D), ("g_mem", D), ("g_attn_out", AW), ("g_conv_out", CW), ("g_xattn_out", XW),
         ("g_post_mix", D), ("g_pre_mlp", D), ("g_post_mlp", D))
GAIN_ROWS = sum(w for _, w in GAINS) // LANES
CONV_ROWS = 3 * CW // LANES
PACK_ROWS = 56


def _rows(a):
    return a.reshape(-1, LANES)


def _local_step(x, mem, pos, gains, cw_full, win16, wkv16, wout16, wup8, wdn16, tgt):
    half = HEAD // 2
    inv_freq = jnp.float32(ROPE_THETA) ** (-(jnp.arange(half, dtype=F32) * 2.0 / HEAD))
    invf = jnp.tile(inv_freq, AW // half)[None, :]
    sgn = jnp.tile(jnp.concatenate([-jnp.ones((half,), F32), jnp.ones((half,), F32)]), AW // HEAD)[None, :]
    cos, sins = _rope_table(pos.astype(F32).reshape(S, 1), invf, sgn)
    cw8 = jnp.zeros((SUBLANES, CW), F32).at[0:3].set(cw_full)

    memn16, kv16 = _mem_fwd(mem, gains["g_mem"], wkv16)
    q, k, v, bcu, qx16, h16 = _in_proj(x, gains["g_pre_mix"], win16, cos, sins)
    y_attn, ltot = _attn_fwd(q, k, v)
    ypre, y16, y2, x1 = _mix_out(y_attn, bcu, qx16, kv16, cw8, gains["g_attn_out"], gains["g_conv_out"],
                                 gains["g_xattn_out"], gains["g_post_mix"], wout16, x)
    a16, du16, h2_16, df2_16, dx1, loss8, dg_mlp = _mlp(x1, tgt, gains["g_pre_mlp"], gains["g_post_mlp"], wup8, wdn16)

    dy2_16, dya, dbcu, dqx, dgs, dcw, dkv = _mix_out_bwd(
        dx1, y2, ypre, bcu, qx16, kv16, cw8, gains["g_post_mix"], gains["g_attn_out"], gains["g_conv_out"],
        gains["g_xattn_out"], wout16)
    dq, dk, dv = _attn_bwd(q, k, v, dya, ypre, ltot)
    dproj16, grad_x, dg_in = _in_proj_bwd(dq, dk, dv, dbcu, dqx, cos, sins, win16, x, gains["g_pre_mix"], dx1)
    dkv16, dg_mem = _mem_bwd(mem, gains["g_mem"], wkv16, dkv)

    gw_in = _wgrad(h16, dproj16, 512, "wgrad_in")
    gw_out = _wgrad(y16, dy2_16, D, "wgrad_out")
    gw_up = _wgrad(h2_16, du16, FF_BLK, "wgrad_up")
    gw_dn = _wgrad(df2_16, a16, FF_BLK, "wgrad_down", square_b=True, transpose_out=True)
    gw_kv = _wgrad(memn16, dkv16, 2 * XW, "wgrad_mem_kv")

    def by_owner_in(g):
        return g.transpose(1, 0, 2).reshape(D, N_DEV, PW // N_DEV).transpose(1, 0, 2)

    wgrads = {
        "w_in": tuple(by_owner_in(g) for g in gw_in),
        "w_mem_kv": tuple(g.reshape(N_DEV, D // N_DEV, 2 * XW) for g in gw_kv),
        "w_out": tuple(g.reshape(N_DEV, D // N_DEV, D) for g in gw_out),
        "w_up": gw_up,
        "w_down": gw_dn,
    }
    small = {
        "g_pre_mix": dg_in[0], "g_mem": dg_mem[0], "g_attn_out": dgs[1, 0:AW], "g_conv_out": dgs[1, AW:AW + CW],
        "g_xattn_out": dgs[1, AW + CW:], "g_post_mix": dgs[0], "g_pre_mlp": dg_mlp[1], "g_post_mlp": dg_mlp[0],
    }
    return loss8[0, 0], grad_x, wgrads, small, dcw[0:3]


BIG = ("w_in", "w_mem_kv", "w_out", "w_up", "w_down")
ORDER = ("g_pre_mix", "g_mem", "w_in", "w_mem_kv", "conv_w", "g_attn_out", "g_conv_out", "g_xattn_out", "w_out",
         "g_post_mix", "g_pre_mlp", "w_up", "w_down", "g_post_mlp")


def kernel(x, mem, positions, g_pre_mix, g_mem, w_in, w_mem_kv, conv_w, g_attn_out, g_conv_out, g_xattn_out, w_out, g_post_mix, g_pre_mlp, w_up, w_down, g_post_mlp, loss_target, m_g_pre_mix, m_g_mem, m_w_in, m_w_mem_kv, m_conv_w, m_g_attn_out, m_g_conv_out, m_g_xattn_out, m_w_out, m_g_post_mix, m_g_pre_mlp, m_w_up, m_w_down, m_g_post_mlp, v_g_pre_mix, v_g_mem, v_w_in, v_w_mem_kv, v_conv_w, v_g_attn_out, v_g_conv_out, v_g_xattn_out, v_w_out, v_g_post_mix, v_g_pre_mlp, v_w_up, v_w_down, v_g_post_mlp):
    w = dict(g_pre_mix=g_pre_mix, g_mem=g_mem, w_in=w_in, w_mem_kv=w_mem_kv, conv_w=conv_w, g_attn_out=g_attn_out,
             g_conv_out=g_conv_out, g_xattn_out=g_xattn_out, w_out=w_out, g_post_mix=g_post_mix, g_pre_mlp=g_pre_mlp,
             w_up=w_up, w_down=w_down, g_post_mlp=g_post_mlp)
    mo = dict(g_pre_mix=m_g_pre_mix, g_mem=m_g_mem, w_in=m_w_in, w_mem_kv=m_w_mem_kv, conv_w=m_conv_w,
              g_attn_out=m_g_attn_out, g_conv_out=m_g_conv_out, g_xattn_out=m_g_xattn_out, w_out=m_w_out,
              g_post_mix=m_g_post_mix, g_pre_mlp=m_g_pre_mlp, w_up=m_w_up, w_down=m_w_down, g_post_mlp=m_g_post_mlp)
    vo = dict(g_pre_mix=v_g_pre_mix, g_mem=v_g_mem, w_in=v_w_in, w_mem_kv=v_w_mem_kv, conv_w=v_conv_w,
              g_attn_out=v_g_attn_out, g_conv_out=v_g_conv_out, g_xattn_out=v_g_xattn_out, w_out=v_w_out,
              g_post_mix=v_g_post_mix, g_pre_mlp=v_g_pre_mlp, w_up=v_w_up, w_down=v_w_down, g_post_mlp=v_g_post_mlp)

    xi, yi, ci = lax.axis_index("x"), lax.axis_index("y"), lax.axis_index("c")
    me = 4 * xi + 2 * yi + ci
    place = jnp.stack([ci, 2 * xi + yi]).astype(jnp.int32)

    conv_tile = jnp.zeros((SUBLANES, LANES), F32).at[0:3, 0:CW // N_DEV].set(conv_w[0])
    shards = [w[n][0].astype(BF16) for n in BIG] + [conv_tile]
    win8, wkv8, wout8, wup8, wdn8, conv8 = _all_gather(shards)
    win16 = win8.transpose(1, 0, 2).reshape(D, PW)
    wkv16 = wkv8.reshape(D, 2 * XW)
    wout16 = wout8.reshape(D, D)
    wdn16 = wdn8.reshape(FF, D)
    cw_full = conv8[:, 0:3, 0:CW // N_DEV].transpose(1, 0, 2).reshape(3, CW)

    gains = {n: w[n] for n, _ in GAINS}
    loss, grad_x, wgrads, small, dcw = _local_step(
        x[0], mem[0], positions[0], gains, cw_full, win16, wkv16, wout16, wup8, wdn16, loss_target[0])
    loss = lax.psum(loss, ("x", "y", "c"))

    from_sib = _rs_pair([wgrads[n][1] for n in BIG])
    sums = [_rs_pair_add(place, wgrads[n][0], from_sib[t], "pair_add_" + n) for t, n in enumerate(BIG)]
    from_chips = _rs_chips([s[0] for s in sums])
    grad, delta, new_m, new_v = {}, {}, {}, {}
    for t, n in enumerate(BIG):
        g, d_, m_, v_ = _adamw_shard(sums[t][1], from_chips[t], w[n][0], mo[n][0], vo[n][0], "adamw_" + n)
        grad[n], delta[n], new_m[n], new_v[n] = g[None], d_[None], m_[None], v_[None]

    pack = jnp.concatenate([_rows(small[n]) for n, _ in GAINS] + [_rows(dcw), jnp.zeros((PACK_ROWS - GAIN_ROWS - CONV_ROWS, LANES), F32)])
    tot = _all_reduce_small(pack)
    conv_grad = lax.dynamic_slice(tot[GAIN_ROWS:GAIN_ROWS + CONV_ROWS].reshape(3, CW), (0, me * (CW // N_DEV)), (3, CW // N_DEV))

    def small_pack(get):
        conv_row = jnp.zeros((LANES,), F32).at[0:3 * CW // N_DEV].set(get("conv_w").reshape(-1))[None, :]
        return jnp.concatenate([_rows(get(n)) for n, _ in GAINS] + [conv_row, jnp.zeros((PACK_ROWS - GAIN_ROWS - 1, LANES), F32)])

    gpack = jnp.concatenate([tot[0:GAIN_ROWS], jnp.zeros((LANES,), F32).at[0:3 * CW // N_DEV].set(conv_grad.reshape(-1))[None, :],
                             jnp.zeros((PACK_ROWS - GAIN_ROWS - 1, LANES), F32)])
    dpack, mpack, vnew = _adamw_small(gpack, small_pack(lambda n: w[n][0]), small_pack(lambda n: mo[n][0]),
                                      small_pack(lambda n: vo[n][0]))

    row = 0
    for n, width in GAINS:
        nr = width // LANES
        grad[n] = tot[row:row + nr].reshape(1, width)
        delta[n], new_m[n], new_v[n] = (p[row:row + nr].reshape(1, width) for p in (dpack, mpack, vnew))
        row += nr
    grad["conv_w"] = conv_grad[None]
    delta["conv_w"], new_m["conv_w"], new_v["conv_w"] = (
        p[GAIN_ROWS, 0:3 * CW // N_DEV].reshape(1, 3, CW // N_DEV) for p in (dpack, mpack, vnew))

    return (loss, grad_x[None], *[grad[n] for n in ORDER], *[delta[n] for n in ORDER],
            *[new_m[n] for n in ORDER], *[new_v[n] for n in ORDER])
```

```python
import functools

import numpy as np
import jax
import jax.numpy as jnp
from jax import lax
from jax.experimental import pallas as pl
from jax.experimental.pallas import tpu as pltpu

F32, BF16 = jnp.float32, jnp.bfloat16
MESH = pl.DeviceIdType.MESH
ANY = pl.BlockSpec(memory_space=pl.ANY)

N_DEV = 8
D = 1024
S = 4096
N_MEM = 256
HEAD = 64
AW, CW, XW = 512, 256, 256
PW = 3 * AW + 3 * CW + XW
FF = 4096
FF_BLK = FF // N_DEV
PATTERNS = ((128, 1), (512, 4), (2048, 16))
QB = 128
EPS = 1e-6
NEG = -1e30
SCALE = HEAD ** -0.5
ROPE_THETA = 10000.0
LANES = 128
SUBLANES = 8

ADAM_LR, ADAM_B1, ADAM_B2, ADAM_EPS, ADAM_WD, ADAM_STEP = 0.001, 0.9, 0.999, 1e-08, 0.01, 10

TQ = 512
TQ_MLP = 256
NT = S // TQ


def _cparams(vmem_mb, n_grid=1):
    return pltpu.CompilerParams(dimension_semantics=("arbitrary",) * n_grid, vmem_limit_bytes=vmem_mb << 20)


def _const(shape):
    nd = len(shape)
    return pl.BlockSpec(shape, lambda *_: (0,) * nd, pipeline_mode=pl.Buffered(1))


def _acc(shape):
    nd = len(shape)
    return pl.BlockSpec(shape, lambda *_: (0,) * nd)


def _dot(a, b):
    return jnp.dot(a, b, preferred_element_type=F32)


def _dot_nt(a, b):
    return lax.dot_general(a, b, (((1,), (1,)), ((), ())), preferred_element_type=F32)


def _dot_tn(a, b):
    return lax.dot_general(a, b, (((0,), (0,)), ((), ())), preferred_element_type=F32)


def _rms(x, g):
    r = lax.rsqrt(jnp.mean(x * x, axis=-1, keepdims=True) + EPS)
    n = x * r
    return n * g, n, r


def _rms_bwd(dy, n, r, g):
    dn = dy * g
    dx = r * (dn - n * jnp.mean(dn * n, axis=-1, keepdims=True))
    return dx, jnp.sum(dy * n, axis=0, keepdims=True)


def _rot_half(t):
    lane = lax.broadcasted_iota(jnp.int32, t.shape, 1)
    n = t.shape[1]
    return jnp.where((lane % HEAD) < HEAD // 2, pltpu.roll(t, n - HEAD // 2, 1), pltpu.roll(t, HEAD // 2, 1))


def _rope_table(pos_col, invf, sgn):
    def body(p_ref, f_ref, s_ref, c_out, s_out):
        ang = p_ref[...] * f_ref[...]
        c_out[...] = jnp.tile(jnp.cos(ang), (1, AW // LANES))
        s_out[...] = jnp.tile(jnp.sin(ang) * s_ref[...], (1, AW // LANES))

    tile = pl.BlockSpec((TQ, AW), lambda i: (i, 0))
    return pl.pallas_call(
        body, grid=(NT,), name="rope_table",
        in_specs=[pl.BlockSpec((TQ, 1), lambda i: (i, 0)), _const((1, LANES)), _const((1, LANES))],
        out_specs=[tile, tile], out_shape=[jax.ShapeDtypeStruct((S, AW), F32)] * 2,
        compiler_params=_cparams(32))(pos_col, invf, sgn)


def _mem_fwd(mem, g_mem, wkv16):
    def body(m_ref, g_ref, w_ref, n16_ref, kv_ref):
        y, _, _ = _rms(m_ref[...], g_ref[...])
        y16 = y.astype(BF16)
        n16_ref[...] = y16
        kv_ref[...] = _dot(y16, w_ref[...]).astype(BF16)

    return pl.pallas_call(
        body, name="mem_fwd",
        out_shape=[jax.ShapeDtypeStruct((N_MEM, D), BF16), jax.ShapeDtypeStruct((N_MEM, 2 * XW), BF16)],
        compiler_params=pltpu.CompilerParams(vmem_limit_bytes=32 << 20))(mem, g_mem, wkv16)


def _in_proj(x, g, w16, cos, sins):
    def body(x_ref, g_ref, w_ref, c_ref, s_ref, q_ref, k_ref, v_ref, bcu_ref, qx_ref, h_ref):
        y, _, _ = _rms(x_ref[...], g_ref[...])
        h = y.astype(BF16)
        h_ref[...] = h
        proj = _dot(h, w_ref[...])
        cos, sn = c_ref[...], s_ref[...]
        q, k = proj[:, 0:AW], proj[:, AW:2 * AW]
        q_ref[...] = (q * cos + _rot_half(q) * sn) * SCALE
        k_ref[...] = k * cos + _rot_half(k) * sn
        v_ref[...] = proj[:, 2 * AW:3 * AW]
        bcu_ref[...] = proj[:, 3 * AW:3 * AW + 3 * CW]
        qx_ref[...] = (proj[:, 3 * AW + 3 * CW:] * SCALE).astype(BF16)

    def tile(w):
        return pl.BlockSpec((TQ, w), lambda i: (i, 0))

    return pl.pallas_call(
        body, grid=(NT,), name="in_proj",
        in_specs=[tile(D), _const((1, D)), _const((D, PW)), tile(AW), tile(AW)],
        out_specs=[tile(AW), tile(AW), tile(AW), tile(3 * CW), tile(XW), tile(D)],
        out_shape=[jax.ShapeDtypeStruct((S, AW), F32)] * 3 + [
            jax.ShapeDtypeStruct((S, 3 * CW), F32), jax.ShapeDtypeStruct((S, XW), BF16),
            jax.ShapeDtypeStruct((S, D), BF16)],
        compiler_params=_cparams(56))(x, g, w16, cos, sins)


ATTN_PLANS = (("p1", 1, 128, 32), ("p4", 8, 64, 8), ("p16", 16, 128, 2))
PAD = 128
WIN = 256


ATTN_UNROLL = 4


def _fill_bias(tab, qblk, partner):
    qi = lax.broadcasted_iota(jnp.int32, (2 * qblk, WIN), 0) & (qblk - 1)
    kj = lax.broadcasted_iota(jnp.int32, (2 * qblk, WIN), 1)
    piece = kj >> (qblk.bit_length() - 1)
    kk = kj & (qblk - 1)
    prev = (piece & 1) == 0
    of_partner = piece >= 2
    for first in (0, 1):
        for par in (0, 1):
            lo = jnp.where(prev, (qblk if first else qi) + jnp.where(of_partner, par, 0), 0)
            hi = jnp.where(prev, qblk, qi + jnp.where(of_partner, par - 1, 0))
            tab[2 * first + par] = jnp.where((kk >= lo) & (kk <= hi), 0.0, NEG).astype(F32)


def _block_rows(g, qblk, nbc, partner):
    own = pl.ds(pl.multiple_of(PAD + g * qblk, qblk), qblk)
    first = ((g & (nbc - 1)) == 0).astype(jnp.int32)
    if partner:
        gp = jnp.bitwise_xor(g, 4 * nbc)
        wins = (pl.ds(pl.multiple_of(PAD + (g - 1) * qblk, qblk), 2 * qblk),
                pl.ds(pl.multiple_of(PAD + (gp - 1) * qblk, qblk), 2 * qblk))
        return own, wins, 2 * first + ((g >> ((4 * nbc).bit_length() - 1)) & 1)
    return own, (pl.ds(pl.multiple_of(PAD + (g - 1) * qblk, qblk), 2 * qblk),), 2 * first


def _window(ref, wins):
    parts = [ref[w, :].astype(BF16) for w in wins]
    return parts[0] if len(parts) == 1 else jnp.concatenate(parts, axis=0)


def _stack_heads(t, lane):
    zero = jnp.zeros_like(t)
    return jnp.concatenate([jnp.where(lane < HEAD, t, zero), jnp.where(lane >= HEAD, t, zero)], axis=0)


def _unstack_heads(t2, lane):
    half = t2.shape[0] // 2
    return jnp.where(lane < HEAD, t2[0:half, :], t2[half:, :])


def _gather_classes(views, bufs, sems, lanes):
    waits = []
    for i, (view, buf) in enumerate(zip(views, bufs)):
        if view.ndim == 2:
            pltpu.make_async_copy(view.at[:, lanes], buf.at[pl.ds(PAD, S), :], sems.at[i]).start()
        else:
            n_cls, per = view.shape[1], view.shape[0]
            for c in range(n_cls):
                pltpu.make_async_copy(view.at[:, c, lanes], buf.at[pl.ds(PAD + c * per, per), :], sems.at[i]).start()
        whole = buf.at[pl.ds(PAD, S), :]
        waits.append(pltpu.make_async_copy(whole, whole, sems.at[i]))
    return waits


def _scatter_classes(bufs, lands, sems):
    waits = []
    for i, (buf, land) in enumerate(zip(bufs, lands)):
        per, n_cls = land.shape[0], land.shape[1]
        for c in range(n_cls):
            pltpu.make_async_copy(buf.at[pl.ds(PAD + c * per, per), :], land.at[:, c, :], sems.at[i]).start()
        waits.append(pltpu.make_async_copy(land, land, sems.at[i]))
    return waits


def _attn_fwd(q, k, v):
    views = [[a] + [a.reshape(S // n, n, AW) for _, n, _, _ in ATTN_PLANS[1:]] for a in (q, k, v)]
    flat = [views[a][p] for p in range(3) for a in range(3)]

    def body(*refs):
        hbm = [refs[3 * p:3 * p + 3] for p in range(3)]
        y_ref, lt_ref = refs[9:11]
        bufs = [refs[11 + 3 * p:14 + 3 * p] for p in range(3)]
        oc, lc, o4n, l4n, o16n, l16n, tab128, tab4, sem_in, sem_out = refs[20:]
        lanes = pl.ds(pl.multiple_of(pl.program_id(0) * LANES, LANES), LANES)
        waits = [_gather_classes(hbm[p], bufs[p], sem_in.at[p], lanes) for p in range(3)]

        @pl.when(pl.program_id(0) == 0)
        def _():
            for p in range(3):
                for b in bufs[p]:
                    b[0:PAD, :] = jnp.zeros((PAD, LANES), F32)
            _fill_bias(tab128, 128, False)
            _fill_bias(tab4, 64, True)

        lane = lax.broadcasted_iota(jnp.int32, (1, LANES), 1)
        ones = jnp.ones((WIN, LANES), BF16)

        def run(plan, bq, bk, bv, tab, o_dst, l_dst, dst_pad):
            _, n_cls, qblk, nbc = plan
            partner = n_cls == 8

            def block(g, carry):
                own, wins, mask = _block_rows(g, qblk, nbc, partner)
                q2 = _stack_heads(bq[own, :].astype(BF16), lane)
                kw = _window(bk, wins)
                vw = jnp.concatenate([_window(bv, wins), ones], axis=1)
                s = _dot_nt(q2, kw) + tab[mask]
                m = jnp.max(s, axis=1, keepdims=True)
                oe = _dot(jnp.exp(s - m).astype(BF16), vw)
                den = oe[:, LANES:]
                dst = pl.ds(pl.multiple_of(dst_pad + g * qblk, qblk), qblk)
                o_dst[dst, :] = _unstack_heads(oe[:, 0:LANES] / den, lane)
                l_dst[dst, :] = _unstack_heads(m + jnp.log(den), lane)
                return carry
            lax.fori_loop(0, n_cls * nbc, block, 0, unroll=ATTN_UNROLL)

        for w in waits[0]:
            w.wait()
        run(ATTN_PLANS[0], *bufs[0], tab128, y_ref, lt_ref, 0)
        for w in waits[1]:
            w.wait()
        run(ATTN_PLANS[1], *bufs[1], tab4, oc, lc, PAD)
        for w in _scatter_classes((oc, lc), (o4n, l4n), sem_out.at[0]):
            w.wait()
        for w in waits[2]:
            w.wait()
        run(ATTN_PLANS[2], *bufs[2], tab128, oc, lc, PAD)
        for w in _scatter_classes((oc, lc), (o16n, l16n), sem_out.at[1]):
            w.wait()

        for t in range(S // TQ):
            rows = pl.ds(t * TQ, TQ)
            r4, r16 = pl.ds(t * (TQ // 8), TQ // 8), pl.ds(t * (TQ // 16), TQ // 16)
            l0, l1, l2 = lt_ref[rows, :], l4n[r4, :, :].reshape(TQ, LANES), l16n[r16, :, :].reshape(TQ, LANES)
            lm = jnp.maximum(jnp.maximum(l0, l1), l2)
            e0, e1, e2 = jnp.exp(l0 - lm), jnp.exp(l1 - lm), jnp.exp(l2 - lm)
            den = e0 + e1 + e2
            y_ref[rows, :] = (e0 * y_ref[rows, :] + e1 * o4n[r4, :, :].reshape(TQ, LANES)
                              + e2 * o16n[r16, :, :].reshape(TQ, LANES)) / den
            lt_ref[rows, :] = lm + jnp.log(den)

    col = pl.BlockSpec((S, LANES), lambda h: (0, h))
    padded = pltpu.VMEM((PAD + S, LANES), F32)
    return pl.pallas_call(
        body, grid=(AW // LANES,), name="attn_fwd",
        in_specs=[ANY] * 9, out_specs=[col, col],
        out_shape=[jax.ShapeDtypeStruct((S, AW), F32)] * 2,
        scratch_shapes=[padded] * 9 + [padded, padded,
                        pltpu.VMEM((S // 8, 8, LANES), F32), pltpu.VMEM((S // 8, 8, LANES), F32),
                        pltpu.VMEM((S // 16, 16, LANES), F32), pltpu.VMEM((S // 16, 16, LANES), F32),
                        pltpu.VMEM((4, 256, WIN), F32), pltpu.VMEM((4, 128, WIN), F32),
                        pltpu.SemaphoreType.DMA((3, 3)), pltpu.SemaphoreType.DMA((2, 2))],
        compiler_params=_cparams(56))(*flat)


def _conv_taps(z, zprev, row):
    z1 = jnp.where(row == 0, zprev[7:8, :], pltpu.roll(z, 1, 0))
    z2 = jnp.where(row == 0, zprev[6:7, :], jnp.where(row == 1, zprev[7:8, :], pltpu.roll(z, 2, 0)))
    return z1, z2


def _xattn_scores(qm, km):
    s = _dot_nt(qm, km)
    m = jnp.max(s, axis=1, keepdims=True)
    e = jnp.exp(s - m)
    return e, jnp.sum(e, axis=1, keepdims=True)


def _mix_out(y_attn, bcu, qx16, kv16, cw8, g_attn, g_conv, g_x, g_post, wout16, x):
    def body(ya_ref, bcu_ref, halo_ref, qx_ref, kv_ref, cw_ref, ga_ref, gc_ref, gx_ref, gp_ref, w_ref, x_ref,
             ypre_ref, y16_ref, y2_ref, x1_ref):
        i = pl.program_id(0)
        bcu = bcu_ref[...]
        b, c, u = bcu[:, 0:CW], bcu[:, CW:2 * CW], bcu[:, 2 * CW:]
        z = c * u
        halo = halo_ref[...]
        zprev = jnp.where(i > 0, halo[:, CW:2 * CW] * halo[:, 2 * CW:], 0.0)
        row = lax.broadcasted_iota(jnp.int32, z.shape, 0)
        z1, z2 = _conv_taps(z, zprev, row)
        cw = cw_ref[...]
        y_conv = b * (z2 * cw[0:1, :] + z1 * cw[1:2, :] + z * cw[2:3, :])

        qx = qx_ref[...]
        kv = kv_ref[...]
        km, vm = kv[:, 0:XW], kv[:, XW:]
        lane = lax.broadcasted_iota(jnp.int32, qx.shape, 1)
        y_x = jnp.zeros(qx.shape, F32)
        for h in range(XW // HEAD):
            hm = (lane >= h * HEAD) & (lane < (h + 1) * HEAD)
            e, l = _xattn_scores(jnp.where(hm, qx, jnp.zeros_like(qx)), km)
            y_x = jnp.where(hm, _dot(e.astype(BF16), vm) / l, y_x)

        y_attn = ya_ref[...]
        ypre_ref[:, 0:AW] = y_attn
        ypre_ref[:, AW:AW + CW] = y_conv
        ypre_ref[:, AW + CW:] = y_x
        y = jnp.concatenate([_rms(y_attn, ga_ref[...])[0], _rms(y_conv, gc_ref[...])[0],
                             _rms(y_x, gx_ref[...])[0]], axis=1).astype(BF16)
        y16_ref[...] = y
        y2 = _dot(y, w_ref[...])
        y2_ref[...] = y2
        x1_ref[...] = x_ref[...] + _rms(y2, gp_ref[...])[0]

    def tile(w):
        return pl.BlockSpec((TQ, w), lambda i: (i, 0))

    halo = pl.BlockSpec((SUBLANES, 3 * CW), lambda i: (jnp.maximum(i * (TQ // SUBLANES) - 1, 0), 0))
    return pl.pallas_call(
        body, grid=(NT,), name="mix_out",
        in_specs=[tile(AW), tile(3 * CW), halo, tile(XW), _const((N_MEM, 2 * XW)), _const((SUBLANES, CW)),
                  _const((1, AW)), _const((1, CW)), _const((1, XW)), _const((1, D)), _const((D, D)), tile(D)],
        out_specs=[tile(D), tile(D), tile(D), tile(D)],
        out_shape=[jax.ShapeDtypeStruct((S, D), F32), jax.ShapeDtypeStruct((S, D), BF16),
                   jax.ShapeDtypeStruct((S, D), F32), jax.ShapeDtypeStruct((S, D), F32)],
        compiler_params=_cparams(56))(y_attn, bcu, bcu, qx16, kv16, cw8, g_attn, g_conv, g_x, g_post, wout16, x)


def _mlp(x1, tgt, g_pre, g_post, wup8, wdn16):
    tq = TQ_MLP

    def body(x1_ref, t_ref, g1_ref, g2_ref, wu_ref, wd_ref,
             a16_ref, du_ref, h2_ref, df2_ref, dx1_ref, loss_ref, dg_ref, a32):
        @pl.when(pl.program_id(0) == 0)
        def _():
            loss_ref[...] = jnp.zeros_like(loss_ref)
            dg_ref[...] = jnp.zeros_like(dg_ref)

        x1 = x1_ref[...]
        g1, g2 = g1_ref[...], g2_ref[...]
        y1, n1, r1 = _rms(x1, g1)
        h2 = y1.astype(BF16)
        h2_ref[...] = h2
        f2 = jnp.zeros((tq, D), F32)
        for j in range(N_DEV):
            cols = slice(j * FF_BLK, (j + 1) * FF_BLK)
            a = jnp.maximum(_dot(h2, wu_ref[j]), 0.0)
            a32[:, cols] = a
            a16_ref[:, cols] = a.astype(BF16)
            f2 = f2 + _dot((a * a).astype(BF16), wd_ref[cols, :])
        y2, n2, r2 = _rms(f2, g2)
        e = x1 + y2 - t_ref[...]
        sq = jnp.sum(jnp.sum(e * e, axis=1, keepdims=True), axis=0, keepdims=True)
        loss_ref[...] += jnp.broadcast_to(sq * (0.5 / D), loss_ref.shape)
        dout = e * (1.0 / D)
        df2, dg2 = _rms_bwd(dout, n2, r2, g2)
        df2_16 = df2.astype(BF16)
        df2_ref[...] = df2_16
        dh2 = jnp.zeros((tq, D), F32)
        for j in range(N_DEV):
            cols = slice(j * FF_BLK, (j + 1) * FF_BLK)
            du = (_dot_nt(df2_16, wd_ref[cols, :]) * (2.0 * a32[:, cols])).astype(BF16)
            du_ref[:, cols] = du
            dh2 = dh2 + _dot_nt(du, wu_ref[j])
        dx, dg1 = _rms_bwd(dh2, n1, r1, g1)
        dx1_ref[...] = dout + dx
        dg_ref[0:1, :] += dg2
        dg_ref[1:2, :] += dg1

    def tile(w):
        return pl.BlockSpec((tq, w), lambda i: (i, 0))

    return pl.pallas_call(
        body, grid=(S // tq,), name="mlp",
        in_specs=[tile(D), tile(D), _const((1, D)), _const((1, D)), _const((N_DEV, D, FF_BLK)), _const((FF, D))],
        out_specs=[tile(FF), tile(FF), tile(D), tile(D), tile(D), _acc((SUBLANES, LANES)), _acc((SUBLANES, D))],
        out_shape=[jax.ShapeDtypeStruct((S, FF), BF16), jax.ShapeDtypeStruct((S, FF), BF16),
                   jax.ShapeDtypeStruct((S, D), BF16), jax.ShapeDtypeStruct((S, D), BF16),
                   jax.ShapeDtypeStruct((S, D), F32), jax.ShapeDtypeStruct((SUBLANES, LANES), F32),
                   jax.ShapeDtypeStruct((SUBLANES, D), F32)],
        scratch_shapes=[pltpu.VMEM((tq, FF), F32)],
        compiler_params=_cparams(56))(x1, tgt, g_pre, g_post, wup8, wdn16)


def _mix_out_bwd(dx1, y2, ypre, ltot, head_ones, bcu, qx16, kv16, cw8, g_post, g_attn, g_conv, g_x, wout16):
    def body(dx1_ref, y2_ref, ypre_ref, lt_ref, e_ref, bcu_ref, halo_ref, qx_ref, kv_ref, cw_ref, gp_ref, ga_ref,
             gc_ref, gx_ref, w_ref, dy2_ref, dya_ref, ld_ref, dbcu_ref, dqx_ref, dgs_ref, dcw_ref, dkv_ref, carry):
        i = pl.program_id(0)

        @pl.when(i == 0)
        def _():
            dgs_ref[...] = jnp.zeros_like(dgs_ref)
            dcw_ref[...] = jnp.zeros_like(dcw_ref)
            dkv_ref[...] = jnp.zeros_like(dkv_ref)
            carry[...] = jnp.zeros_like(carry)

        gp = gp_ref[...]
        _, n, r = _rms(y2_ref[...], gp)
        dy2, dgp = _rms_bwd(dx1_ref[...], n, r, gp)
        dy2_16 = dy2.astype(BF16)
        dy2_ref[...] = dy2_16
        dy = _dot_nt(dy2_16, w_ref[...])

        ypre = ypre_ref[...]
        ga, gc, gx = ga_ref[...], gc_ref[...], gx_ref[...]
        _, na, ra = _rms(ypre[:, 0:AW], ga)
        dya, dga = _rms_bwd(dy[:, 0:AW], na, ra, ga)
        _, nc, rc = _rms(ypre[:, AW:AW + CW], gc)
        dyc, dgc = _rms_bwd(dy[:, AW:AW + CW], nc, rc, gc)
        y_x = ypre[:, AW + CW:]
        _, nx, rx = _rms(y_x, gx)
        dyx, dgx = _rms_bwd(dy[:, AW + CW:], nx, rx, gx)
        dya_ref[...] = dya
        prod = dya * ypre[:, 0:AW]
        hi = prod.astype(BF16)
        lo = (prod - hi.astype(F32)).astype(BF16)
        head_sum = _dot(hi, e_ref[...]) + _dot(lo, e_ref[...])
        lane_a = lax.broadcasted_iota(jnp.int32, prod.shape, 1)
        ld_ref[...] = jnp.where((lane_a % HEAD) < HEAD // 2, lt_ref[...], head_sum)
        dgs_ref[0:1, :] += dgp
        dgs_ref[1:2, :] += jnp.concatenate([dga, dgc, dgx], axis=1)

        bcu = bcu_ref[...]
        b, c, u = bcu[:, 0:CW], bcu[:, CW:2 * CW], bcu[:, 2 * CW:]
        z = c * u
        halo = halo_ref[...]
        zprev = jnp.where(i < NT - 1, halo[:, CW:2 * CW] * halo[:, 2 * CW:], 0.0)
        row = lax.broadcasted_iota(jnp.int32, z.shape, 0)
        z1, z2 = _conv_taps(z, zprev, row)
        cw = cw_ref[...]
        conv = z2 * cw[0:1, :] + z1 * cw[1:2, :] + z * cw[2:3, :]
        dconv = dyc * b
        nxt = carry[...]
        dn1 = jnp.where(row == TQ - 1, nxt[0:1, :], pltpu.roll(dconv, TQ - 1, 0))
        dn2 = jnp.where(row == TQ - 1, nxt[1:2, :], jnp.where(row == TQ - 2, nxt[0:1, :], pltpu.roll(dconv, TQ - 2, 0)))
        carry[...] = dconv[0:SUBLANES, :]
        dz = dconv * cw[2:3, :] + dn1 * cw[1:2, :] + dn2 * cw[0:1, :]
        dbcu_ref[:, 0:CW] = dyc * conv
        dbcu_ref[:, CW:2 * CW] = dz * u
        dbcu_ref[:, 2 * CW:] = dz * c
        dcw_ref[0:1, :] += jnp.sum(z2 * dconv, axis=0, keepdims=True)
        dcw_ref[1:2, :] += jnp.sum(z1 * dconv, axis=0, keepdims=True)
        dcw_ref[2:3, :] += jnp.sum(z * dconv, axis=0, keepdims=True)

        qx = qx_ref[...]
        kv = kv_ref[...]
        km, vm = kv[:, 0:XW], kv[:, XW:]
        lane = lax.broadcasted_iota(jnp.int32, qx.shape, 1)
        dqx = jnp.zeros(qx.shape, F32)
        dkm = jnp.zeros((N_MEM, XW), F32)
        dvm = jnp.zeros((N_MEM, XW), F32)
        for h in range(XW // HEAD):
            hm = (lane >= h * HEAD) & (lane < (h + 1) * HEAD)
            qm = jnp.where(hm, qx, jnp.zeros_like(qx))
            e, l = _xattn_scores(qm, km)
            p = e / l
            dom = jnp.where(hm, dyx, 0.0)
            do16 = dom.astype(BF16)
            dsum = jnp.sum(dom * y_x, axis=1, keepdims=True)
            ds = (p * (_dot_nt(do16, vm) - dsum)).astype(BF16)
            dqx = jnp.where(hm, _dot(ds, km), dqx)
            dkm = dkm + _dot_tn(ds, qm)
            dvm = dvm + _dot_tn(p.astype(BF16), do16)
        dqx_ref[...] = dqx * SCALE
        dkv_ref[:, 0:XW] += dkm
        dkv_ref[:, XW:] += dvm

    def tile(w):
        return pl.BlockSpec((TQ, w), lambda i: (NT - 1 - i, 0))

    halo = pl.BlockSpec((SUBLANES, 3 * CW), lambda i: (jnp.maximum((NT - 1 - i) * (TQ // SUBLANES) - 1, 0), 0))
    return pl.pallas_call(
        body, grid=(NT,), name="mix_out_bwd",
        in_specs=[tile(D), tile(D), tile(D), tile(AW), _const((AW, AW)), tile(3 * CW), halo, tile(XW),
                  _const((N_MEM, 2 * XW)), _const((SUBLANES, CW)), _const((1, D)), _const((1, AW)), _const((1, CW)),
                  _const((1, XW)), _const((D, D))],
        out_specs=[tile(D), tile(AW), tile(AW), tile(3 * CW), tile(XW), _acc((SUBLANES, D)), _acc((SUBLANES, CW)),
                   _acc((N_MEM, 2 * XW))],
        out_shape=[jax.ShapeDtypeStruct((S, D), BF16), jax.ShapeDtypeStruct((S, AW), F32),
                   jax.ShapeDtypeStruct((S, AW), F32),
                   jax.ShapeDtypeStruct((S, 3 * CW), F32), jax.ShapeDtypeStruct((S, XW), F32),
                   jax.ShapeDtypeStruct((SUBLANES, D), F32), jax.ShapeDtypeStruct((SUBLANES, CW), F32),
                   jax.ShapeDtypeStruct((N_MEM, 2 * XW), F32)],
        scratch_shapes=[pltpu.VMEM((SUBLANES, CW), F32)],
        compiler_params=_cparams(56))(dx1, y2, ypre, ltot, head_ones, bcu, bcu, qx16, kv16, cw8, g_post, g_attn,
                                      g_conv, g_x, wout16)


def _attn_bwd(q, k, v, dya, ld):
    n_in = 5
    views = [[a] + [a.reshape(S // n, n, AW) for _, n, _, _ in ATTN_PLANS[1:]] for a in (q, k, v, dya, ld)]
    flat = [views[a][p] for p in range(3) for a in range(n_in)]

    def body(*refs):
        hbm = [refs[n_in * p:n_in * p + n_in] for p in range(3)]
        outs = refs[3 * n_in:3 * n_in + 3]
        sc = refs[3 * n_in + 3:]
        buf_a, buf_b, res, acc = sc[0:5], sc[5:10], sc[10:13], sc[13:16]
        land8, land16 = sc[16:19], sc[19:22]
        tab128, tab4, sem_in, sem_out = sc[22:]
        lanes = pl.ds(pl.multiple_of(pl.program_id(0) * LANES, LANES), LANES)
        w_p1 = _gather_classes(hbm[0], buf_a, sem_in.at[0], lanes)
        w_p4 = _gather_classes(hbm[1], buf_b, sem_in.at[1], lanes)

        @pl.when(pl.program_id(0) == 0)
        def _():
            for b in buf_a + buf_b:
                b[0:PAD, :] = jnp.zeros((PAD, LANES), F32)
            _fill_bias(tab128, 128, False)
            _fill_bias(tab4, 64, True)

        for b in res + acc:
            b[...] = jnp.zeros_like(b)
        lane = lax.broadcasted_iota(jnp.int32, (1, LANES), 1)

        def run(plan, bufs, tab, dst):
            _, n_cls, qblk, nbc = plan
            partner = n_cls == 8
            bq, bk, bv, bdo, bld = bufs
            rq, rk, rv = dst

            def block(g, carry):
                own, wins, mask = _block_rows(g, qblk, nbc, partner)
                q2 = _stack_heads(bq[own, :].astype(BF16), lane)
                do2 = _stack_heads(bdo[own, :].astype(BF16), lane)
                kw, vw = _window(bk, wins), _window(bv, wins)
                ldv = bld[own, :]
                half = HEAD // 2
                lt2 = jnp.concatenate([ldv[:, 0:1], ldv[:, HEAD:HEAD + 1]], axis=0)
                dsum2 = jnp.concatenate([ldv[:, half:half + 1], ldv[:, HEAD + half:HEAD + half + 1]], axis=0)
                p = jnp.exp(_dot_nt(q2, kw) + tab[mask] - lt2)
                ds = (p * (_dot_nt(do2, vw) - dsum2)).astype(BF16)
                rq[own, :] = _unstack_heads(_dot(ds, kw), lane)
                dkw = _dot_tn(ds, q2)
                dvw = _dot_tn(p.astype(BF16), do2)
                n_w = WIN // len(wins)
                for i, w in enumerate(wins):
                    rk[w, :] += dkw[i * n_w:(i + 1) * n_w, :]
                    rv[w, :] += dvw[i * n_w:(i + 1) * n_w, :]
                return carry
            lax.fori_loop(0, n_cls * nbc, block, 0, unroll=ATTN_UNROLL)

        def add_landed(lands, per_tile):
            for a, land in zip(acc, lands):
                for t in range(S // TQ):
                    a[pl.ds(PAD + t * TQ, TQ), :] += land[pl.ds(t * per_tile, per_tile), :, :].reshape(TQ, LANES)

        for w in w_p1:
            w.wait()
        run(ATTN_PLANS[0], buf_a, tab128, acc)
        w_p16 = _gather_classes(hbm[2], buf_a, sem_in.at[2], lanes)
        for w in w_p4:
            w.wait()
        run(ATTN_PLANS[1], buf_b, tab4, res)
        for w in _scatter_classes(res, land8, sem_out.at[0]):
            w.wait()
        add_landed(land8, TQ // 8)
        for b in res:
            b[...] = jnp.zeros_like(b)
        for w in w_p16:
            w.wait()
        run(ATTN_PLANS[2], buf_a, tab128, res)
        for w in _scatter_classes(res, land16, sem_out.at[1]):
            w.wait()
        add_landed(land16, TQ // 16)
        done = [pltpu.make_async_copy(a.at[pl.ds(PAD, S), :], o.at[:, lanes], sem_out.at[0, i])
                for i, (a, o) in enumerate(zip(acc, outs))]
        for cp in done:
            cp.start()
        for cp in done:
            cp.wait()

    padded = pltpu.VMEM((PAD + S, LANES), F32)
    return pl.pallas_call(
        body, grid=(AW // LANES,), name="attn_bwd",
        in_specs=[ANY] * (3 * n_in), out_specs=[ANY] * 3,
        out_shape=[jax.ShapeDtypeStruct((S, AW), F32)] * 3,
        scratch_shapes=[padded] * 16 + [pltpu.VMEM((S // 8, 8, LANES), F32)] * 3
        + [pltpu.VMEM((S // 16, 16, LANES), F32)] * 3
        + [pltpu.VMEM((4, 256, WIN), F32), pltpu.VMEM((4, 128, WIN), F32),
           pltpu.SemaphoreType.DMA((3, n_in)), pltpu.SemaphoreType.DMA((2, 3))],
        compiler_params=_cparams(60))(*flat)


def _in_proj_bwd(dq, dk, dv, dbcu, dqx, cos, sins, w16, x, g, dx1):
    def body(dq_ref, dk_ref, dv_ref, dbcu_ref, dqx_ref, c_ref, s_ref, w_ref, x_ref, g_ref, dx1_ref,
             dp_ref, gx_ref, dg_ref):
        @pl.when(pl.program_id(0) == 0)
        def _():
            dg_ref[...] = jnp.zeros_like(dg_ref)

        cos, sn = c_ref[...], s_ref[...]
        dqr = dq_ref[...] * SCALE
        dkr = dk_ref[...]
        dp = jnp.concatenate([dqr * cos + _rot_half(dqr * sn), dkr * cos + _rot_half(dkr * sn), dv_ref[...],
                              dbcu_ref[...], dqx_ref[...]], axis=1).astype(BF16)
        dp_ref[...] = dp
        dh = _dot_nt(dp, w_ref[...])
        g = g_ref[...]
        _, n, r = _rms(x_ref[...], g)
        dx, dg = _rms_bwd(dh, n, r, g)
        gx_ref[...] = dx1_ref[...] + dx
        dg_ref[0:1, :] += dg

    def tile(w):
        return pl.BlockSpec((TQ, w), lambda i: (i, 0))

    return pl.pallas_call(
        body, grid=(NT,), name="in_proj_bwd",
        in_specs=[tile(AW), tile(AW), tile(AW), tile(3 * CW), tile(XW), tile(AW), tile(AW), _const((D, PW)),
                  tile(D), _const((1, D)), tile(D)],
        out_specs=[tile(PW), tile(D), _acc((SUBLANES, D))],
        out_shape=[jax.ShapeDtypeStruct((S, PW), BF16), jax.ShapeDtypeStruct((S, D), F32),
                   jax.ShapeDtypeStruct((SUBLANES, D), F32)],
        compiler_params=_cparams(56))(dq, dk, dv, dbcu, dqx, cos, sins, w16, x, g, dx1)


def _mem_bwd(mem, g_mem, wkv16, dkv):
    def body(m_ref, g_ref, w_ref, dkv_ref, dkv16_ref, dg_ref):
        dkv16 = dkv_ref[...].astype(BF16)
        dkv16_ref[...] = dkv16
        _, n, _ = _rms(m_ref[...], g_ref[...])
        dg = jnp.sum(_dot_nt(dkv16, w_ref[...]) * n, axis=0, keepdims=True)
        dg_ref[...] = jnp.broadcast_to(dg, dg_ref.shape)

    return pl.pallas_call(
        body, name="mem_bwd",
        out_shape=[jax.ShapeDtypeStruct((N_MEM, 2 * XW), BF16), jax.ShapeDtypeStruct((SUBLANES, D), F32)],
        compiler_params=pltpu.CompilerParams(vmem_limit_bytes=32 << 20))(mem, g_mem, wkv16, dkv)


def _wgrad(a16, b16, tn, name, square_b=False, transpose_out=False):
    kk, m = a16.shape
    n_tiles = b16.shape[1] // tn
    chunk = min(kk, 512)
    oshape = (tn, m) if transpose_out else (m, tn)

    def body(a_ref, b_ref, o32_ref, o16_ref, at):
        @pl.when(pl.program_id(0) == 0)
        def _():
            for c in range(kk // chunk):
                at[:, c * chunk:(c + 1) * chunk] = a_ref[c * chunk:(c + 1) * chunk, :].T

        b = b_ref[...]
        if square_b:
            b = b * b
        acc = _dot(at[...], b)
        if transpose_out:
            acc = acc.T
        o32_ref[0] = acc
        o16_ref[0] = acc.astype(BF16)

    oblk = pl.BlockSpec((1,) + oshape, lambda j: (j, 0, 0))
    return pl.pallas_call(
        body, grid=(n_tiles,), name=name,
        in_specs=[_const((kk, m)), pl.BlockSpec((kk, tn), lambda j: (0, j))],
        out_specs=[oblk, oblk],
        out_shape=[jax.ShapeDtypeStruct((n_tiles,) + oshape, F32), jax.ShapeDtypeStruct((n_tiles,) + oshape, BF16)],
        scratch_shapes=[pltpu.VMEM((m, kk), BF16)],
        compiler_params=_cparams(56))(a16, b16)


def _adamw_math(w, g, m, v):
    m = ADAM_B1 * m + (1.0 - ADAM_B1) * g
    v = ADAM_B2 * v + (1.0 - ADAM_B2) * jnp.square(g)
    m_hat = m / (1.0 - ADAM_B1 ** ADAM_STEP)
    v_hat = v / (1.0 - ADAM_B2 ** ADAM_STEP)
    delta = -ADAM_LR * (m_hat / (jnp.sqrt(v_hat) + ADAM_EPS) + ADAM_WD * w)
    return delta, m, v


def _adamw_shard(own32, rb16, w, m, v, name):
    def body(o_ref, r_ref, w_ref, m_ref, v_ref, g_out, d_out, m_out, v_out):
        g = o_ref[...] + r_ref[0].astype(F32) + r_ref[1].astype(F32) + r_ref[2].astype(F32)
        g_out[...] = g
        d_out[...], m_out[...], v_out[...] = _adamw_math(w_ref[...], g, m_ref[...], v_ref[...])

    return pl.pallas_call(
        body, name=name, out_shape=[jax.ShapeDtypeStruct(w.shape, F32)] * 4,
        compiler_params=pltpu.CompilerParams(vmem_limit_bytes=48 << 20))(own32, rb16, w, m, v)


def _adamw_small(g, w, m, v):
    def body(g_ref, w_ref, m_ref, v_ref, d_out, m_out, v_out):
        d_out[...], m_out[...], v_out[...] = _adamw_math(w_ref[...], g_ref[...], m_ref[...], v_ref[...])

    return pl.pallas_call(body, name="adamw_small", out_shape=[jax.ShapeDtypeStruct(w.shape, F32)] * 3)(g, w, m, v)


def _place():
    x, y, c = lax.axis_index("x"), lax.axis_index("y"), lax.axis_index("c")
    chips = [(1 - x, y), (x, 1 - y), (1 - x, 1 - y)]
    return x, y, c, chips


def _all_gather(shards):
    nt = len(shards)

    def body(*refs):
        ins, outs = refs[:nt], refs[nt:2 * nt]
        send, recv, lsem = refs[2 * nt:]
        x, y, c, chips = _place()
        me, sib = (x, y, c), (x, y, 1 - c)

        def slot(t, px, py, pc):
            return outs[t].at[4 * px + 2 * py + pc]

        def copy(t, k, block, to, src=None):
            return pltpu.make_async_remote_copy(
                src_ref=slot(t, *block) if src is None else src, dst_ref=slot(t, *block),
                send_sem=send.at[t, k], recv_sem=recv.at[t, k], device_id=to, device_id_type=MESH)

        mine = [pltpu.make_async_copy(ins[t], slot(t, *me), lsem.at[t]) for t in range(nt)]
        for cp in mine:
            cp.start()
        first = []
        for t in range(nt):
            first.append(copy(t, 0, me, sib, src=ins[t]))
            first += [copy(t, 1 + j, me, (*chip, c), src=ins[t]) for j, chip in enumerate(chips)]
        for cp in first:
            cp.start()
        passed = []
        for j, chip in enumerate(chips):
            for t in range(nt):
                copy(t, 1 + j, (*chip, c), me).wait_recv()
                fwd = copy(t, 4 + j, (*chip, c), sib)
                fwd.start()
                passed.append(fwd)
        for t in range(nt):
            copy(t, 0, sib, me).wait_recv()
            for j, chip in enumerate(chips):
                copy(t, 4 + j, (*chip, 1 - c), me).wait_recv()
        for cp in first + passed:
            cp.wait_send()
        for cp in mine:
            cp.wait()

    return pl.pallas_call(
        body, name="all_gather_weights",
        in_specs=[ANY] * nt, out_specs=[ANY] * nt,
        out_shape=[jax.ShapeDtypeStruct((N_DEV,) + s.shape, s.dtype) for s in shards],
        scratch_shapes=[pltpu.SemaphoreType.DMA((nt, 7)), pltpu.SemaphoreType.DMA((nt, 7)),
                        pltpu.SemaphoreType.DMA((nt,))])(*shards)


def _rs_pair(g16s):
    nt = len(g16s)

    def body(*refs):
        ins, outs = refs[:nt], refs[nt:2 * nt]
        send, recv = refs[2 * nt:]
        x, y, c, _ = _place()
        copies = [pltpu.make_async_remote_copy(
            src_ref=ins[t].at[2 * p + (1 - c)], dst_ref=outs[t].at[p], send_sem=send.at[t, p], recv_sem=recv.at[t, p],
            device_id=(x, y, 1 - c), device_id_type=MESH) for t in range(nt) for p in range(4)]
        for cp in copies:
            cp.start()
        for cp in copies:
            cp.wait()

    return pl.pallas_call(
        body, name="reduce_scatter_pair",
        in_specs=[ANY] * nt, out_specs=[ANY] * nt,
        out_shape=[jax.ShapeDtypeStruct((4,) + g.shape[1:], g.dtype) for g in g16s],
        scratch_shapes=[pltpu.SemaphoreType.DMA((nt, 4)), pltpu.SemaphoreType.DMA((nt, 4))])(*g16s)


def _rs_pair_add(place, g32, ra16, name):
    shp = g32.shape[1:]

    def body(pl_ref, g_ref, r_ref, cs_ref, own_ref):
        s = g_ref[0] + r_ref[0].astype(F32)
        cs_ref[0] = s.astype(BF16)

        @pl.when(pl.program_id(0) == pl_ref[1])
        def _():
            own_ref[...] = s

    blk = (1,) + shp
    return pl.pallas_call(
        body, name=name,
        grid_spec=pltpu.PrefetchScalarGridSpec(
            num_scalar_prefetch=1, grid=(4,),
            in_specs=[pl.BlockSpec(blk, lambda p, s: (2 * p + s[0], 0, 0)), pl.BlockSpec(blk, lambda p, s: (p, 0, 0))],
            out_specs=[pl.BlockSpec(blk, lambda p, s: (p, 0, 0)), pl.BlockSpec(shp, lambda p, s: (0, 0))]),
        out_shape=[jax.ShapeDtypeStruct((4,) + shp, BF16), jax.ShapeDtypeStruct(shp, F32)],
        compiler_params=_cparams(48))(place, g32, ra16)


def _rs_chips(cs16s):
    nt = len(cs16s)

    def body(*refs):
        ins, outs = refs[:nt], refs[nt:2 * nt]
        send, recv = refs[2 * nt:]
        x, y, c, chips = _place()
        copies = [pltpu.make_async_remote_copy(
            src_ref=ins[t].at[2 * px + py], dst_ref=outs[t].at[j], send_sem=send.at[t, j], recv_sem=recv.at[t, j],
            device_id=(px, py, c), device_id_type=MESH) for t in range(nt) for j, (px, py) in enumerate(chips)]
        for cp in copies:
            cp.start()
        for cp in copies:
            cp.wait()

    return pl.pallas_call(
        body, name="reduce_scatter_chips",
        in_specs=[ANY] * nt, out_specs=[ANY] * nt,
        out_shape=[jax.ShapeDtypeStruct((3,) + g.shape[1:], g.dtype) for g in cs16s],
        scratch_shapes=[pltpu.SemaphoreType.DMA((nt, 3)), pltpu.SemaphoreType.DMA((nt, 3))])(*cs16s)


def _all_reduce_small(pack):
    def body(p_ref, o_ref, land, send, recv):
        x, y, c, _ = _place()
        me = 4 * x + 2 * y + c
        land[me] = p_ref[...]
        copies = []
        for k in range(1, N_DEV):
            kx, ky, kc = (k >> 2) & 1, (k >> 1) & 1, k & 1
            peer = (1 - x if kx else x, 1 - y if ky else y, 1 - c if kc else c)
            copies.append(pltpu.make_async_remote_copy(
                src_ref=p_ref, dst_ref=land.at[me], send_sem=send.at[k - 1], recv_sem=recv.at[k - 1],
                device_id=peer, device_id_type=MESH))
        for cp in copies:
            cp.start()
        for cp in copies:
            cp.wait()
        tot = land[0]
        for s in range(1, N_DEV):
            tot = tot + land[s]
        o_ref[...] = tot

    return pl.pallas_call(
        body, name="all_reduce_small", out_shape=jax.ShapeDtypeStruct(pack.shape, F32),
        in_specs=[pl.BlockSpec(memory_space=pltpu.VMEM)], out_specs=pl.BlockSpec(memory_space=pltpu.VMEM),
        scratch_shapes=[pltpu.VMEM((N_DEV,) + pack.shape, F32), pltpu.SemaphoreType.DMA((N_DEV - 1,)),
                        pltpu.SemaphoreType.DMA((N_DEV - 1,))])(pack)


GAINS = (("g_pre_mix", D), ("g_mem", D), ("g_attn_out", AW), ("g_conv_out", CW), ("g_xattn_out", XW),
         ("g_post_mix", D), ("g_pre_mlp", D), ("g_post_mlp", D))
GAIN_ROWS = sum(w for _, w in GAINS) // LANES
CONV_ROWS = 3 * CW // LANES
PACK_ROWS = 56


def _rows(a):
    return a.reshape(-1, LANES)


def _local_step(x, mem, pos, gains, cw_full, win16, wkv16, wout16, wup8, wdn16, tgt):
    half = HEAD // 2
    inv_freq = jnp.float32(ROPE_THETA) ** (-(jnp.arange(half, dtype=F32) * 2.0 / HEAD))
    invf = jnp.tile(inv_freq, LANES // half)[None, :]
    sgn = jnp.tile(jnp.concatenate([-jnp.ones((half,), F32), jnp.ones((half,), F32)]), LANES // HEAD)[None, :]
    cos, sins = _rope_table(pos.astype(F32).reshape(S, 1), invf, sgn)
    cw8 = jnp.zeros((SUBLANES, CW), F32).at[0:3].set(cw_full)

    memn16, kv16 = _mem_fwd(mem, gains["g_mem"], wkv16)
    q, k, v, bcu, qx16, h16 = _in_proj(x, gains["g_pre_mix"], win16, cos, sins)
    y_attn, ltot = _attn_fwd(q, k, v)
    ypre, y16, y2, x1 = _mix_out(y_attn, bcu, qx16, kv16, cw8, gains["g_attn_out"], gains["g_conv_out"],
                                 gains["g_xattn_out"], gains["g_post_mix"], wout16, x)
    a16, du16, h2_16, df2_16, dx1, loss8, dg_mlp = _mlp(x1, tgt, gains["g_pre_mlp"], gains["g_post_mlp"], wup8, wdn16)

    head_id = jnp.arange(AW, dtype=jnp.int32) // HEAD
    head_ones = (head_id[:, None] == head_id[None, :]).astype(BF16)
    dy2_16, dya, ld, dbcu, dqx, dgs, dcw, dkv = _mix_out_bwd(
        dx1, y2, ypre, ltot, head_ones, bcu, qx16, kv16, cw8, gains["g_post_mix"], gains["g_attn_out"],
        gains["g_conv_out"], gains["g_xattn_out"], wout16)
    dq, dk, dv = _attn_bwd(q, k, v, dya, ld)
    dproj16, grad_x, dg_in = _in_proj_bwd(dq, dk, dv, dbcu, dqx, cos, sins, win16, x, gains["g_pre_mix"], dx1)
    dkv16, dg_mem = _mem_bwd(mem, gains["g_mem"], wkv16, dkv)

    gw_in = _wgrad(h16, dproj16, 512, "wgrad_in")
    gw_out = _wgrad(y16, dy2_16, D, "wgrad_out")
    gw_up = _wgrad(h2_16, du16, FF_BLK, "wgrad_up")
    gw_dn = _wgrad(df2_16, a16, FF_BLK, "wgrad_down", square_b=True, transpose_out=True)
    gw_kv = _wgrad(memn16, dkv16, 2 * XW, "wgrad_mem_kv")

    def by_owner_in(g):
        return g.transpose(1, 0, 2).reshape(D, N_DEV, PW // N_DEV).transpose(1, 0, 2)

    wgrads = {
        "w_in": tuple(by_owner_in(g) for g in gw_in),
        "w_mem_kv": tuple(g.reshape(N_DEV, D // N_DEV, 2 * XW) for g in gw_kv),
        "w_out": tuple(g.reshape(N_DEV, D // N_DEV, D) for g in gw_out),
        "w_up": gw_up,
        "w_down": gw_dn,
    }
    small = {
        "g_pre_mix": dg_in[0], "g_mem": dg_mem[0], "g_attn_out": dgs[1, 0:AW], "g_conv_out": dgs[1, AW:AW + CW],
        "g_xattn_out": dgs[1, AW + CW:], "g_post_mix": dgs[0], "g_pre_mlp": dg_mlp[1], "g_post_mlp": dg_mlp[0],
    }
    return loss8[0, 0], grad_x, wgrads, small, dcw[0:3]


BIG = ("w_in", "w_mem_kv", "w_out", "w_up", "w_down")
ORDER = ("g_pre_mix", "g_mem", "w_in", "w_mem_kv", "conv_w", "g_attn_out", "g_conv_out", "g_xattn_out", "w_out",
         "g_post_mix", "g_pre_mlp", "w_up", "w_down", "g_post_mlp")


def kernel(x, mem, positions, g_pre_mix, g_mem, w_in, w_mem_kv, conv_w, g_attn_out, g_conv_out, g_xattn_out, w_out, g_post_mix, g_pre_mlp, w_up, w_down, g_post_mlp, loss_target, m_g_pre_mix, m_g_mem, m_w_in, m_w_mem_kv, m_conv_w, m_g_attn_out, m_g_conv_out, m_g_xattn_out, m_w_out, m_g_post_mix, m_g_pre_mlp, m_w_up, m_w_down, m_g_post_mlp, v_g_pre_mix, v_g_mem, v_w_in, v_w_mem_kv, v_conv_w, v_g_attn_out, v_g_conv_out, v_g_xattn_out, v_w_out, v_g_post_mix, v_g_pre_mlp, v_w_up, v_w_down, v_g_post_mlp):
    w = dict(g_pre_mix=g_pre_mix, g_mem=g_mem, w_in=w_in, w_mem_kv=w_mem_kv, conv_w=conv_w, g_attn_out=g_attn_out,
             g_conv_out=g_conv_out, g_xattn_out=g_xattn_out, w_out=w_out, g_post_mix=g_post_mix, g_pre_mlp=g_pre_mlp,
             w_up=w_up, w_down=w_down, g_post_mlp=g_post_mlp)
    mo = dict(g_pre_mix=m_g_pre_mix, g_mem=m_g_mem, w_in=m_w_in, w_mem_kv=m_w_mem_kv, conv_w=m_conv_w,
              g_attn_out=m_g_attn_out, g_conv_out=m_g_conv_out, g_xattn_out=m_g_xattn_out, w_out=m_w_out,
              g_post_mix=m_g_post_mix, g_pre_mlp=m_g_pre_mlp, w_up=m_w_up, w_down=m_w_down, g_post_mlp=m_g_post_mlp)
    vo = dict(g_pre_mix=v_g_pre_mix, g_mem=v_g_mem, w_in=v_w_in, w_mem_kv=v_w_mem_kv, conv_w=v_conv_w,
              g_attn_out=v_g_attn_out, g_conv_out=v_g_conv_out, g_xattn_out=v_g_xattn_out, w_out=v_w_out,
              g_post_mix=v_g_post_mix, g_pre_mlp=v_g_pre_mlp, w_up=v_w_up, w_down=v_w_down, g_post_mlp=v_g_post_mlp)

    xi, yi, ci = lax.axis_index("x"), lax.axis_index("y"), lax.axis_index("c")
    me = 4 * xi + 2 * yi + ci
    place = jnp.stack([ci, 2 * xi + yi]).astype(jnp.int32)

    conv_tile = jnp.zeros((SUBLANES, LANES), F32).at[0:3, 0:CW // N_DEV].set(conv_w[0])
    shards = [w[n][0].astype(BF16) for n in BIG] + [conv_tile]
    win8, wkv8, wout8, wup8, wdn8, conv8 = _all_gather(shards)
    win16 = win8.transpose(1, 0, 2).reshape(D, PW)
    wkv16 = wkv8.reshape(D, 2 * XW)
    wout16 = wout8.reshape(D, D)
    wdn16 = wdn8.reshape(FF, D)
    cw_full = conv8[:, 0:3, 0:CW // N_DEV].transpose(1, 0, 2).reshape(3, CW)

    gains = {n: w[n] for n, _ in GAINS}
    loss, grad_x, wgrads, small, dcw = _local_step(
        x[0], mem[0], positions[0], gains, cw_full, win16, wkv16, wout16, wup8, wdn16, loss_target[0])
    loss = lax.psum(loss, ("x", "y", "c"))

    from_sib = _rs_pair([wgrads[n][1] for n in BIG])
    sums = [_rs_pair_add(place, wgrads[n][0], from_sib[t], "pair_add_" + n) for t, n in enumerate(BIG)]
    from_chips = _rs_chips([s[0] for s in sums])
    grad, delta, new_m, new_v = {}, {}, {}, {}
    for t, n in enumerate(BIG):
        g, d_, m_, v_ = _adamw_shard(sums[t][1], from_chips[t], w[n][0], mo[n][0], vo[n][0], "adamw_" + n)
        grad[n], delta[n], new_m[n], new_v[n] = g[None], d_[None], m_[None], v_[None]

    pack = jnp.concatenate([_rows(small[n]) for n, _ in GAINS] + [_rows(dcw), jnp.zeros((PACK_ROWS - GAIN_ROWS - CONV_ROWS, LANES), F32)])
    tot = _all_reduce_small(pack)
    conv_grad = lax.dynamic_slice(tot[GAIN_ROWS:GAIN_ROWS + CONV_ROWS].reshape(3, CW), (0, me * (CW // N_DEV)), (3, CW // N_DEV))

    def small_pack(get):
        conv_row = jnp.zeros((LANES,), F32).at[0:3 * CW // N_DEV].set(get("conv_w").reshape(-1))[None, :]
        return jnp.concatenate([_rows(get(n)) for n, _ in GAINS] + [conv_row, jnp.zeros((PACK_ROWS - GAIN_ROWS - 1, LANES), F32)])

    gpack = jnp.concatenate([tot[0:GAIN_ROWS], jnp.zeros((LANES,), F32).at[0:3 * CW // N_DEV].set(conv_grad.reshape(-1))[None, :],
                             jnp.zeros((PACK_ROWS - GAIN_ROWS - 1, LANES), F32)])
    dpack, mpack, vnew = _adamw_small(gpack, small_pack(lambda n: w[n][0]), small_pack(lambda n: mo[n][0]),
                                      small_pack(lambda n: vo[n][0]))

    row = 0
    for n, width in GAINS:
        nr = width // LANES
        grad[n] = tot[row:row + nr].reshape(1, width)
        delta[n], new_m[n], new_v[n] = (p[row:row + nr].reshape(1, width) for p in (dpack, mpack, vnew))
        row += nr
    grad["conv_w"] = conv_grad[None]
    delta["conv_w"], new_m["conv_w"], new_v["conv_w"] = (
        p[GAIN_ROWS, 0:3 * CW // N_DEV].reshape(1, 3, CW // N_DEV) for p in (dpack, mpack, vnew))

    return (loss, grad_x[None], *[grad[n] for n in ORDER], *[delta[n] for n in ORDER],
            *[new_m[n] for n in ORDER], *[new_v[n] for n in ORDER])
```

```python
import functools

import numpy as np
import jax
import jax.numpy as jnp
from jax import lax
from jax.experimental import pallas as pl
from jax.experimental.pallas import tpu as pltpu

F32, BF16 = jnp.float32, jnp.bfloat16
MESH = pl.DeviceIdType.MESH
ANY = pl.BlockSpec(memory_space=pl.ANY)

N_DEV = 8
D = 1024
S = 4096
N_MEM = 256
HEAD = 64
AW, CW, XW = 512, 256, 256
PW = 3 * AW + 3 * CW + XW
FF = 4096
FF_BLK = FF // N_DEV
PATTERNS = ((128, 1), (512, 4), (2048, 16))
QB = 128
EPS = 1e-6
NEG = -1e30
SCALE = HEAD ** -0.5
ROPE_THETA = 10000.0
LANES = 128
SUBLANES = 8

ADAM_LR, ADAM_B1, ADAM_B2, ADAM_EPS, ADAM_WD, ADAM_STEP = 0.001, 0.9, 0.999, 1e-08, 0.01, 10

TQ = 512
TQ_MLP = 256
NT = S // TQ


def _cparams(vmem_mb, n_grid=1):
    return pltpu.CompilerParams(dimension_semantics=("arbitrary",) * n_grid, vmem_limit_bytes=vmem_mb << 20)


def _const(shape):
    nd = len(shape)
    return pl.BlockSpec(shape, lambda *_: (0,) * nd, pipeline_mode=pl.Buffered(1))


def _acc(shape):
    nd = len(shape)
    return pl.BlockSpec(shape, lambda *_: (0,) * nd)


def _dot(a, b):
    return jnp.dot(a, b, preferred_element_type=F32)


def _dot_nt(a, b):
    return lax.dot_general(a, b, (((1,), (1,)), ((), ())), preferred_element_type=F32)


def _dot_tn(a, b):
    return lax.dot_general(a, b, (((0,), (0,)), ((), ())), preferred_element_type=F32)


def _rms(x, g):
    r = lax.rsqrt(jnp.mean(x * x, axis=-1, keepdims=True) + EPS)
    n = x * r
    return n * g, n, r


def _rms_bwd(dy, n, r, g):
    dn = dy * g
    dx = r * (dn - n * jnp.mean(dn * n, axis=-1, keepdims=True))
    return dx, jnp.sum(dy * n, axis=0, keepdims=True)


def _rot_half(t):
    lane = lax.broadcasted_iota(jnp.int32, t.shape, 1)
    n = t.shape[1]
    return jnp.where((lane % HEAD) < HEAD // 2, pltpu.roll(t, n - HEAD // 2, 1), pltpu.roll(t, HEAD // 2, 1))


def _rope_table(pos_col, invf, sgn):
    def body(p_ref, f_ref, s_ref, c_out, s_out):
        ang = p_ref[...] * f_ref[...]
        c_out[...] = jnp.tile(jnp.cos(ang), (1, AW // LANES))
        s_out[...] = jnp.tile(jnp.sin(ang) * s_ref[...], (1, AW // LANES))

    tile = pl.BlockSpec((TQ, AW), lambda i: (i, 0))
    return pl.pallas_call(
        body, grid=(NT,), name="rope_table",
        in_specs=[pl.BlockSpec((TQ, 1), lambda i: (i, 0)), _const((1, LANES)), _const((1, LANES))],
        out_specs=[tile, tile], out_shape=[jax.ShapeDtypeStruct((S, AW), F32)] * 2,
        compiler_params=_cparams(32))(pos_col, invf, sgn)


def _mem_fwd(mem, g_mem, wkv16):
    def body(m_ref, g_ref, w_ref, n16_ref, kv_ref):
        y, _, _ = _rms(m_ref[...], g_ref[...])
        y16 = y.astype(BF16)
        n16_ref[...] = y16
        kv_ref[...] = _dot(y16, w_ref[...]).astype(BF16)

    return pl.pallas_call(
        body, name="mem_fwd",
        out_shape=[jax.ShapeDtypeStruct((N_MEM, D), BF16), jax.ShapeDtypeStruct((N_MEM, 2 * XW), BF16)],
        compiler_params=pltpu.CompilerParams(vmem_limit_bytes=32 << 20))(mem, g_mem, wkv16)


def _in_proj(x, g, w16, cos, sins):
    def body(x_ref, g_ref, w_ref, c_ref, s_ref, q_ref, k_ref, v_ref, bcu_ref, qx_ref, h_ref):
        y, _, _ = _rms(x_ref[...], g_ref[...])
        h = y.astype(BF16)
        h_ref[...] = h
        proj = _dot(h, w_ref[...])
        cos, sn = c_ref[...], s_ref[...]
        q, k = proj[:, 0:AW], proj[:, AW:2 * AW]
        q_ref[...] = (q * cos + _rot_half(q) * sn) * SCALE
        k_ref[...] = k * cos + _rot_half(k) * sn
        v_ref[...] = proj[:, 2 * AW:3 * AW]
        bcu_ref[...] = proj[:, 3 * AW:3 * AW + 3 * CW]
        qx_ref[...] = (proj[:, 3 * AW + 3 * CW:] * SCALE).astype(BF16)

    def tile(w):
        return pl.BlockSpec((TQ, w), lambda i: (i, 0))

    return pl.pallas_call(
        body, grid=(NT,), name="in_proj",
        in_specs=[tile(D), _const((1, D)), _const((D, PW)), tile(AW), tile(AW)],
        out_specs=[tile(AW), tile(AW), tile(AW), tile(3 * CW), tile(XW), tile(D)],
        out_shape=[jax.ShapeDtypeStruct((S, AW), F32)] * 3 + [
            jax.ShapeDtypeStruct((S, 3 * CW), F32), jax.ShapeDtypeStruct((S, XW), BF16),
            jax.ShapeDtypeStruct((S, D), BF16)],
        compiler_params=_cparams(56))(x, g, w16, cos, sins)


ATTN_PLANS = (("p1", 1, 128, 32), ("p4", 8, 64, 8), ("p16", 16, 128, 2))
PAD = 128
WIN = 256


ATTN_UNROLL = 4


def _fill_bias(tab, qblk, partner):
    qi = lax.broadcasted_iota(jnp.int32, (2 * qblk, WIN), 0) & (qblk - 1)
    kj = lax.broadcasted_iota(jnp.int32, (2 * qblk, WIN), 1)
    piece = kj >> (qblk.bit_length() - 1)
    kk = kj & (qblk - 1)
    prev = (piece & 1) == 0
    of_partner = piece >= 2
    for first in (0, 1):
        for par in (0, 1):
            lo = jnp.where(prev, (qblk if first else qi) + jnp.where(of_partner, par, 0), 0)
            hi = jnp.where(prev, qblk, qi + jnp.where(of_partner, par - 1, 0))
            tab[2 * first + par] = jnp.where((kk >= lo) & (kk <= hi), 0.0, NEG).astype(F32)


def _block_rows(g, qblk, nbc, partner):
    own = pl.ds(pl.multiple_of(PAD + g * qblk, qblk), qblk)
    first = ((g & (nbc - 1)) == 0).astype(jnp.int32)
    if partner:
        gp = jnp.bitwise_xor(g, 4 * nbc)
        wins = (pl.ds(pl.multiple_of(PAD + (g - 1) * qblk, qblk), 2 * qblk),
                pl.ds(pl.multiple_of(PAD + (gp - 1) * qblk, qblk), 2 * qblk))
        return own, wins, 2 * first + ((g >> ((4 * nbc).bit_length() - 1)) & 1)
    return own, (pl.ds(pl.multiple_of(PAD + (g - 1) * qblk, qblk), 2 * qblk),), 2 * first


def _window(ref, wins):
    parts = [ref[w, :].astype(BF16) for w in wins]
    return parts[0] if len(parts) == 1 else jnp.concatenate(parts, axis=0)


def _stack_heads(t, lane):
    zero = jnp.zeros_like(t)
    return jnp.concatenate([jnp.where(lane < HEAD, t, zero), jnp.where(lane >= HEAD, t, zero)], axis=0)


def _unstack_heads(t2, lane):
    half = t2.shape[0] // 2
    return jnp.where(lane < HEAD, t2[0:half, :], t2[half:, :])


def _gather_classes(views, bufs, sems, lanes):
    waits = []
    for i, (view, buf) in enumerate(zip(views, bufs)):
        if view.ndim == 2:
            pltpu.make_async_copy(view.at[:, lanes], buf.at[pl.ds(PAD, S), :], sems.at[i]).start()
        else:
            n_cls, per = view.shape[1], view.shape[0]
            for c in range(n_cls):
                pltpu.make_async_copy(view.at[:, c, lanes], buf.at[pl.ds(PAD + c * per, per), :], sems.at[i]).start()
        whole = buf.at[pl.ds(PAD, S), :]
        waits.append(pltpu.make_async_copy(whole, whole, sems.at[i]))
    return waits


def _scatter_classes(bufs, lands, sems):
    waits = []
    for i, (buf, land) in enumerate(zip(bufs, lands)):
        per, n_cls = land.shape[0], land.shape[1]
        for c in range(n_cls):
            pltpu.make_async_copy(buf.at[pl.ds(PAD + c * per, per), :], land.at[:, c, :], sems.at[i]).start()
        waits.append(pltpu.make_async_copy(land, land, sems.at[i]))
    return waits


def _attn_fwd(q, k, v, shards=()):
    views = [[a] + [a.reshape(S // n, n, AW) for _, n, _, _ in ATTN_PLANS[1:]] for a in (q, k, v)]
    flat = [views[a][p] for p in range(3) for a in range(3)]
    ng = len(shards)
    n_grid = AW // LANES

    def body(*refs):
        hbm = [refs[3 * p:3 * p + 3] for p in range(3)]
        refs = refs[9:]
        shard_refs, refs = refs[:ng], refs[ng:]
        y_ref, lt_ref = refs[0:2]
        whole_refs, refs = refs[2:2 + ng], refs[2 + ng:]
        bufs = [refs[3 * p:3 * p + 3] for p in range(3)]
        oc, lc, o4n, l4n, o16n, l16n, tab128, tab4, sem_in, sem_out = refs[9:19]
        if ng:
            start_gather, finish_gather = _gather_steps(shard_refs, whole_refs, *refs[19:])
            pl.when(pl.program_id(0) == 0)(start_gather)
        lanes = pl.ds(pl.multiple_of(pl.program_id(0) * LANES, LANES), LANES)
        waits = [_gather_classes(hbm[p], bufs[p], sem_in.at[p], lanes) for p in range(3)]

        @pl.when(pl.program_id(0) == 0)
        def _():
            for p in range(3):
                for b in bufs[p]:
                    b[0:PAD, :] = jnp.zeros((PAD, LANES), F32)
            _fill_bias(tab128, 128, False)
            _fill_bias(tab4, 64, True)

        lane = lax.broadcasted_iota(jnp.int32, (1, LANES), 1)
        ones = jnp.ones((WIN, LANES), BF16)

        def run(plan, bq, bk, bv, tab, o_dst, l_dst, dst_pad):
            _, n_cls, qblk, nbc = plan
            partner = n_cls == 8

            def block(g, carry):
                own, wins, mask = _block_rows(g, qblk, nbc, partner)
                q2 = _stack_heads(bq[own, :].astype(BF16), lane)
                kw = _window(bk, wins)
                vw = jnp.concatenate([_window(bv, wins), ones], axis=1)
                s = _dot_nt(q2, kw) + tab[mask]
                m = jnp.max(s, axis=1, keepdims=True)
                oe = _dot(jnp.exp(s - m).astype(BF16), vw)
                den = oe[:, LANES:]
                dst = pl.ds(pl.multiple_of(dst_pad + g * qblk, qblk), qblk)
                o_dst[dst, :] = _unstack_heads(oe[:, 0:LANES] / den, lane)
                l_dst[dst, :] = _unstack_heads(m + jnp.log(den), lane)
                return carry
            lax.fori_loop(0, n_cls * nbc, block, 0, unroll=ATTN_UNROLL)

        for w in waits[0]:
            w.wait()
        run(ATTN_PLANS[0], *bufs[0], tab128, y_ref, lt_ref, 0)
        for w in waits[1]:
            w.wait()
        run(ATTN_PLANS[1], *bufs[1], tab4, oc, lc, PAD)
        for w in _scatter_classes((oc, lc), (o4n, l4n), sem_out.at[0]):
            w.wait()
        for w in waits[2]:
            w.wait()
        run(ATTN_PLANS[2], *bufs[2], tab128, oc, lc, PAD)
        for w in _scatter_classes((oc, lc), (o16n, l16n), sem_out.at[1]):
            w.wait()

        for t in range(S // TQ):
            rows = pl.ds(t * TQ, TQ)
            r4, r16 = pl.ds(t * (TQ // 8), TQ // 8), pl.ds(t * (TQ // 16), TQ // 16)
            l0, l1, l2 = lt_ref[rows, :], l4n[r4, :, :].reshape(TQ, LANES), l16n[r16, :, :].reshape(TQ, LANES)
            lm = jnp.maximum(jnp.maximum(l0, l1), l2)
            e0, e1, e2 = jnp.exp(l0 - lm), jnp.exp(l1 - lm), jnp.exp(l2 - lm)
            den = e0 + e1 + e2
            y_ref[rows, :] = (e0 * y_ref[rows, :] + e1 * o4n[r4, :, :].reshape(TQ, LANES)
                              + e2 * o16n[r16, :, :].reshape(TQ, LANES)) / den
            lt_ref[rows, :] = lm + jnp.log(den)

        if ng:
            pl.when(pl.program_id(0) == n_grid - 1)(finish_gather)

    col = pl.BlockSpec((S, LANES), lambda h: (0, h))
    padded = pltpu.VMEM((PAD + S, LANES), F32)
    return pl.pallas_call(
        body, grid=(n_grid,), name="attn_fwd",
        in_specs=[ANY] * (9 + ng), out_specs=[col, col] + [ANY] * ng,
        out_shape=[jax.ShapeDtypeStruct((S, AW), F32)] * 2 + _gathered_shapes(shards),
        scratch_shapes=[padded] * 9 + [padded, padded,
                        pltpu.VMEM((S // 8, 8, LANES), F32), pltpu.VMEM((S // 8, 8, LANES), F32),
                        pltpu.VMEM((S // 16, 16, LANES), F32), pltpu.VMEM((S // 16, 16, LANES), F32),
                        pltpu.VMEM((4, 256, WIN), F32), pltpu.VMEM((4, 128, WIN), F32),
                        pltpu.SemaphoreType.DMA((3, 3)), pltpu.SemaphoreType.DMA((2, 2))]
        + (_gather_scratch(ng) if ng else []),
        compiler_params=_cparams(56))(*flat, *shards)


def _conv_taps(z, zprev, row):
    z1 = jnp.where(row == 0, zprev[7:8, :], pltpu.roll(z, 1, 0))
    z2 = jnp.where(row == 0, zprev[6:7, :], jnp.where(row == 1, zprev[7:8, :], pltpu.roll(z, 2, 0)))
    return z1, z2


def _xattn_scores(qm, km):
    s = _dot_nt(qm, km)
    m = jnp.max(s, axis=1, keepdims=True)
    e = jnp.exp(s - m)
    return e, jnp.sum(e, axis=1, keepdims=True)


def _mix_out(y_attn, bcu, qx16, kv16, cw8, g_attn, g_conv, g_x, g_post, wout16, x):
    def body(ya_ref, bcu_ref, halo_ref, qx_ref, kv_ref, cw_ref, ga_ref, gc_ref, gx_ref, gp_ref, w_ref, x_ref,
             ypre_ref, y16_ref, y2_ref, x1_ref):
        i = pl.program_id(0)
        bcu = bcu_ref[...]
        b, c, u = bcu[:, 0:CW], bcu[:, CW:2 * CW], bcu[:, 2 * CW:]
        z = c * u
        halo = halo_ref[...]
        zprev = jnp.where(i > 0, halo[:, CW:2 * CW] * halo[:, 2 * CW:], 0.0)
        row = lax.broadcasted_iota(jnp.int32, z.shape, 0)
        z1, z2 = _conv_taps(z, zprev, row)
        cw = cw_ref[...]
        y_conv = b * (z2 * cw[0:1, :] + z1 * cw[1:2, :] + z * cw[2:3, :])

        qx = qx_ref[...]
        kv = kv_ref[...]
        km, vm = kv[:, 0:XW], kv[:, XW:]
        lane = lax.broadcasted_iota(jnp.int32, qx.shape, 1)
        y_x = jnp.zeros(qx.shape, F32)
        for h in range(XW // HEAD):
            hm = (lane >= h * HEAD) & (lane < (h + 1) * HEAD)
            e, l = _xattn_scores(jnp.where(hm, qx, jnp.zeros_like(qx)), km)
            y_x = jnp.where(hm, _dot(e.astype(BF16), vm) / l, y_x)

        y_attn = ya_ref[...]
        ypre_ref[:, 0:AW] = y_attn
        ypre_ref[:, AW:AW + CW] = y_conv
        ypre_ref[:, AW + CW:] = y_x
        y = jnp.concatenate([_rms(y_attn, ga_ref[...])[0], _rms(y_conv, gc_ref[...])[0],
                             _rms(y_x, gx_ref[...])[0]], axis=1).astype(BF16)
        y16_ref[...] = y
        y2 = _dot(y, w_ref[...])
        y2_ref[...] = y2
        x1_ref[...] = x_ref[...] + _rms(y2, gp_ref[...])[0]

    def tile(w):
        return pl.BlockSpec((TQ, w), lambda i: (i, 0))

    halo = pl.BlockSpec((SUBLANES, 3 * CW), lambda i: (jnp.maximum(i * (TQ // SUBLANES) - 1, 0), 0))
    return pl.pallas_call(
        body, grid=(NT,), name="mix_out",
        in_specs=[tile(AW), tile(3 * CW), halo, tile(XW), _const((N_MEM, 2 * XW)), _const((SUBLANES, CW)),
                  _const((1, AW)), _const((1, CW)), _const((1, XW)), _const((1, D)), _const((D, D)), tile(D)],
        out_specs=[tile(D), tile(D), tile(D), tile(D)],
        out_shape=[jax.ShapeDtypeStruct((S, D), F32), jax.ShapeDtypeStruct((S, D), BF16),
                   jax.ShapeDtypeStruct((S, D), F32), jax.ShapeDtypeStruct((S, D), F32)],
        compiler_params=_cparams(56))(y_attn, bcu, bcu, qx16, kv16, cw8, g_attn, g_conv, g_x, g_post, wout16, x)


def _mlp(x1, tgt, g_pre, g_post, wup8, wdn16):
    tq = TQ_MLP

    def body(x1_ref, t_ref, g1_ref, g2_ref, wu_ref, wd_ref,
             a16_ref, du_ref, h2_ref, df2_ref, dx1_ref, loss_ref, dg_ref, a32):
        @pl.when(pl.program_id(0) == 0)
        def _():
            loss_ref[...] = jnp.zeros_like(loss_ref)
            dg_ref[...] = jnp.zeros_like(dg_ref)

        x1 = x1_ref[...]
        g1, g2 = g1_ref[...], g2_ref[...]
        y1, n1, r1 = _rms(x1, g1)
        h2 = y1.astype(BF16)
        h2_ref[...] = h2
        f2 = jnp.zeros((tq, D), F32)
        for j in range(N_DEV):
            cols = slice(j * FF_BLK, (j + 1) * FF_BLK)
            a = jnp.maximum(_dot(h2, wu_ref[j]), 0.0)
            a32[:, cols] = a
            a16_ref[:, cols] = a.astype(BF16)
            f2 = f2 + _dot((a * a).astype(BF16), wd_ref[cols, :])
        y2, n2, r2 = _rms(f2, g2)
        e = x1 + y2 - t_ref[...]
        sq = jnp.sum(jnp.sum(e * e, axis=1, keepdims=True), axis=0, keepdims=True)
        loss_ref[...] += jnp.broadcast_to(sq * (0.5 / D), loss_ref.shape)
        dout = e * (1.0 / D)
        df2, dg2 = _rms_bwd(dout, n2, r2, g2)
        df2_16 = df2.astype(BF16)
        df2_ref[...] = df2_16
        dh2 = jnp.zeros((tq, D), F32)
        for j in range(N_DEV):
            cols = slice(j * FF_BLK, (j + 1) * FF_BLK)
            du = (_dot_nt(df2_16, wd_ref[cols, :]) * (2.0 * a32[:, cols])).astype(BF16)
            du_ref[:, cols] = du
            dh2 = dh2 + _dot_nt(du, wu_ref[j])
        dx, dg1 = _rms_bwd(dh2, n1, r1, g1)
        dx1_ref[...] = dout + dx
        dg_ref[0:1, :] += dg2
        dg_ref[1:2, :] += dg1

    def tile(w):
        return pl.BlockSpec((tq, w), lambda i: (i, 0))

    return pl.pallas_call(
        body, grid=(S // tq,), name="mlp",
        in_specs=[tile(D), tile(D), _const((1, D)), _const((1, D)), _const((N_DEV, D, FF_BLK)), _const((FF, D))],
        out_specs=[tile(FF), tile(FF), tile(D), tile(D), tile(D), _acc((SUBLANES, LANES)), _acc((SUBLANES, D))],
        out_shape=[jax.ShapeDtypeStruct((S, FF), BF16), jax.ShapeDtypeStruct((S, FF), BF16),
                   jax.ShapeDtypeStruct((S, D), BF16), jax.ShapeDtypeStruct((S, D), BF16),
                   jax.ShapeDtypeStruct((S, D), F32), jax.ShapeDtypeStruct((SUBLANES, LANES), F32),
                   jax.ShapeDtypeStruct((SUBLANES, D), F32)],
        scratch_shapes=[pltpu.VMEM((tq, FF), F32)],
        compiler_params=_cparams(56))(x1, tgt, g_pre, g_post, wup8, wdn16)


def _mix_out_bwd(dx1, y2, ypre, ltot, head_ones, bcu, qx16, kv16, cw8, g_post, g_attn, g_conv, g_x, wout16):
    def body(dx1_ref, y2_ref, ypre_ref, lt_ref, e_ref, bcu_ref, halo_ref, qx_ref, kv_ref, cw_ref, gp_ref, ga_ref,
             gc_ref, gx_ref, w_ref, dy2_ref, dya_ref, ld_ref, dbcu_ref, dqx_ref, dgs_ref, dcw_ref, dkv_ref, carry):
        i = pl.program_id(0)

        @pl.when(i == 0)
        def _():
            dgs_ref[...] = jnp.zeros_like(dgs_ref)
            dcw_ref[...] = jnp.zeros_like(dcw_ref)
            dkv_ref[...] = jnp.zeros_like(dkv_ref)
            carry[...] = jnp.zeros_like(carry)

        gp = gp_ref[...]
        _, n, r = _rms(y2_ref[...], gp)
        dy2, dgp = _rms_bwd(dx1_ref[...], n, r, gp)
        dy2_16 = dy2.astype(BF16)
        dy2_ref[...] = dy2_16
        dy = _dot_nt(dy2_16, w_ref[...])

        ypre = ypre_ref[...]
        ga, gc, gx = ga_ref[...], gc_ref[...], gx_ref[...]
        _, na, ra = _rms(ypre[:, 0:AW], ga)
        dya, dga = _rms_bwd(dy[:, 0:AW], na, ra, ga)
        _, nc, rc = _rms(ypre[:, AW:AW + CW], gc)
        dyc, dgc = _rms_bwd(dy[:, AW:AW + CW], nc, rc, gc)
        y_x = ypre[:, AW + CW:]
        _, nx, rx = _rms(y_x, gx)
        dyx, dgx = _rms_bwd(dy[:, AW + CW:], nx, rx, gx)
        dya_ref[...] = dya
        prod = dya * ypre[:, 0:AW]
        hi = prod.astype(BF16)
        lo = (prod - hi.astype(F32)).astype(BF16)
        head_sum = _dot(hi, e_ref[...]) + _dot(lo, e_ref[...])
        lane_a = lax.broadcasted_iota(jnp.int32, prod.shape, 1)
        ld_ref[...] = jnp.where((lane_a % HEAD) < HEAD // 2, lt_ref[...], head_sum)
        dgs_ref[0:1, :] += dgp
        dgs_ref[1:2, :] += jnp.concatenate([dga, dgc, dgx], axis=1)

        bcu = bcu_ref[...]
        b, c, u = bcu[:, 0:CW], bcu[:, CW:2 * CW], bcu[:, 2 * CW:]
        z = c * u
        halo = halo_ref[...]
        zprev = jnp.where(i < NT - 1, halo[:, CW:2 * CW] * halo[:, 2 * CW:], 0.0)
        row = lax.broadcasted_iota(jnp.int32, z.shape, 0)
        z1, z2 = _conv_taps(z, zprev, row)
        cw = cw_ref[...]
        conv = z2 * cw[0:1, :] + z1 * cw[1:2, :] + z * cw[2:3, :]
        dconv = dyc * b
        nxt = carry[...]
        dn1 = jnp.where(row == TQ - 1, nxt[0:1, :], pltpu.roll(dconv, TQ - 1, 0))
        dn2 = jnp.where(row == TQ - 1, nxt[1:2, :], jnp.where(row == TQ - 2, nxt[0:1, :], pltpu.roll(dconv, TQ - 2, 0)))
        carry[...] = dconv[0:SUBLANES, :]
        dz = dconv * cw[2:3, :] + dn1 * cw[1:2, :] + dn2 * cw[0:1, :]
        dbcu_ref[:, 0:CW] = dyc * conv
        dbcu_ref[:, CW:2 * CW] = dz * u
        dbcu_ref[:, 2 * CW:] = dz * c
        dcw_ref[0:1, :] += jnp.sum(z2 * dconv, axis=0, keepdims=True)
        dcw_ref[1:2, :] += jnp.sum(z1 * dconv, axis=0, keepdims=True)
        dcw_ref[2:3, :] += jnp.sum(z * dconv, axis=0, keepdims=True)

        qx = qx_ref[...]
        kv = kv_ref[...]
        km, vm = kv[:, 0:XW], kv[:, XW:]
        lane = lax.broadcasted_iota(jnp.int32, qx.shape, 1)
        dqx = jnp.zeros(qx.shape, F32)
        dkm = jnp.zeros((N_MEM, XW), F32)
        dvm = jnp.zeros((N_MEM, XW), F32)
        for h in range(XW // HEAD):
            hm = (lane >= h * HEAD) & (lane < (h + 1) * HEAD)
            qm = jnp.where(hm, qx, jnp.zeros_like(qx))
            e, l = _xattn_scores(qm, km)
            p = e / l
            dom = jnp.where(hm, dyx, 0.0)
            do16 = dom.astype(BF16)
            dsum = jnp.sum(dom * y_x, axis=1, keepdims=True)
            ds = (p * (_dot_nt(do16, vm) - dsum)).astype(BF16)
            dqx = jnp.where(hm, _dot(ds, km), dqx)
            dkm = dkm + _dot_tn(ds, qm)
            dvm = dvm + _dot_tn(p.astype(BF16), do16)
        dqx_ref[...] = dqx * SCALE
        dkv_ref[:, 0:XW] += dkm
        dkv_ref[:, XW:] += dvm

    def tile(w):
        return pl.BlockSpec((TQ, w), lambda i: (NT - 1 - i, 0))

    halo = pl.BlockSpec((SUBLANES, 3 * CW), lambda i: (jnp.maximum((NT - 1 - i) * (TQ // SUBLANES) - 1, 0), 0))
    return pl.pallas_call(
        body, grid=(NT,), name="mix_out_bwd",
        in_specs=[tile(D), tile(D), tile(D), tile(AW), _const((AW, AW)), tile(3 * CW), halo, tile(XW),
                  _const((N_MEM, 2 * XW)), _const((SUBLANES, CW)), _const((1, D)), _const((1, AW)), _const((1, CW)),
                  _const((1, XW)), _const((D, D))],
        out_specs=[tile(D), tile(AW), tile(AW), tile(3 * CW), tile(XW), _acc((SUBLANES, D)), _acc((SUBLANES, CW)),
                   _acc((N_MEM, 2 * XW))],
        out_shape=[jax.ShapeDtypeStruct((S, D), BF16), jax.ShapeDtypeStruct((S, AW), F32),
                   jax.ShapeDtypeStruct((S, AW), F32),
                   jax.ShapeDtypeStruct((S, 3 * CW), F32), jax.ShapeDtypeStruct((S, XW), F32),
                   jax.ShapeDtypeStruct((SUBLANES, D), F32), jax.ShapeDtypeStruct((SUBLANES, CW), F32),
                   jax.ShapeDtypeStruct((N_MEM, 2 * XW), F32)],
        scratch_shapes=[pltpu.VMEM((SUBLANES, CW), F32)],
        compiler_params=_cparams(56))(dx1, y2, ypre, ltot, head_ones, bcu, bcu, qx16, kv16, cw8, g_post, g_attn,
                                      g_conv, g_x, wout16)


def _attn_bwd(q, k, v, dya, ld, chip_sums=()):
    n_in = 5
    views = [[a] + [a.reshape(S // n, n, AW) for _, n, _, _ in ATTN_PLANS[1:]] for a in (q, k, v, dya, ld)]
    flat = [views[a][p] for p in range(3) for a in range(n_in)]
    ns = len(chip_sums)
    n_grid = AW // LANES

    def body(*refs):
        hbm = [refs[n_in * p:n_in * p + n_in] for p in range(3)]
        refs = refs[3 * n_in:]
        sum_refs, refs = refs[:ns], refs[ns:]
        outs = refs[0:3]
        landed_refs, sc = refs[3:3 + ns], refs[3 + ns:]
        buf_a, buf_b, res, acc = sc[0:5], sc[5:10], sc[10:13], sc[13:16]
        land8, land16 = sc[16:19], sc[19:22]
        tab128, tab4, sem_in, sem_out = sc[22:26]
        if ns:
            start_chips, finish_chips = _chips_steps(sum_refs, landed_refs, *sc[26:])
            pl.when(pl.program_id(0) == 0)(start_chips)
        lanes = pl.ds(pl.multiple_of(pl.program_id(0) * LANES, LANES), LANES)
        w_p1 = _gather_classes(hbm[0], buf_a, sem_in.at[0], lanes)
        w_p4 = _gather_classes(hbm[1], buf_b, sem_in.at[1], lanes)

        @pl.when(pl.program_id(0) == 0)
        def _():
            for b in buf_a + buf_b:
                b[0:PAD, :] = jnp.zeros((PAD, LANES), F32)
            _fill_bias(tab128, 128, False)
            _fill_bias(tab4, 64, True)

        for b in res + acc:
            b[...] = jnp.zeros_like(b)
        lane = lax.broadcasted_iota(jnp.int32, (1, LANES), 1)

        def run(plan, bufs, tab, dst):
            _, n_cls, qblk, nbc = plan
            partner = n_cls == 8
            bq, bk, bv, bdo, bld = bufs
            rq, rk, rv = dst

            def block(g, carry):
                own, wins, mask = _block_rows(g, qblk, nbc, partner)
                q2 = _stack_heads(bq[own, :].astype(BF16), lane)
                do2 = _stack_heads(bdo[own, :].astype(BF16), lane)
                kw, vw = _window(bk, wins), _window(bv, wins)
                ldv = bld[own, :]
                half = HEAD // 2
                lt2 = jnp.concatenate([ldv[:, 0:1], ldv[:, HEAD:HEAD + 1]], axis=0)
                dsum2 = jnp.concatenate([ldv[:, half:half + 1], ldv[:, HEAD + half:HEAD + half + 1]], axis=0)
                p = jnp.exp(_dot_nt(q2, kw) + tab[mask] - lt2)
                ds = (p * (_dot_nt(do2, vw) - dsum2)).astype(BF16)
                rq[own, :] = _unstack_heads(_dot(ds, kw), lane)
                dkw = _dot_tn(ds, q2)
                dvw = _dot_tn(p.astype(BF16), do2)
                n_w = WIN // len(wins)
                for i, w in enumerate(wins):
                    rk[w, :] += dkw[i * n_w:(i + 1) * n_w, :]
                    rv[w, :] += dvw[i * n_w:(i + 1) * n_w, :]
                return carry
            lax.fori_loop(0, n_cls * nbc, block, 0, unroll=ATTN_UNROLL)

        def add_landed(lands, per_tile):
            for a, land in zip(acc, lands):
                for t in range(S // TQ):
                    a[pl.ds(PAD + t * TQ, TQ), :] += land[pl.ds(t * per_tile, per_tile), :, :].reshape(TQ, LANES)

        for w in w_p1:
            w.wait()
        run(ATTN_PLANS[0], buf_a, tab128, acc)
        w_p16 = _gather_classes(hbm[2], buf_a, sem_in.at[2], lanes)
        for w in w_p4:
            w.wait()
        run(ATTN_PLANS[1], buf_b, tab4, res)
        for w in _scatter_classes(res, land8, sem_out.at[0]):
            w.wait()
        add_landed(land8, TQ // 8)
        for b in res:
            b[...] = jnp.zeros_like(b)
        for w in w_p16:
            w.wait()
        run(ATTN_PLANS[2], buf_a, tab128, res)
        for w in _scatter_classes(res, land16, sem_out.at[1]):
            w.wait()
        add_landed(land16, TQ // 16)
        done = [pltpu.make_async_copy(a.at[pl.ds(PAD, S), :], o.at[:, lanes], sem_out.at[0, i])
                for i, (a, o) in enumerate(zip(acc, outs))]
        for cp in done:
            cp.start()
        for cp in done:
            cp.wait()
        if ns:
            pl.when(pl.program_id(0) == n_grid - 1)(finish_chips)

    padded = pltpu.VMEM((PAD + S, LANES), F32)
    return pl.pallas_call(
        body, grid=(n_grid,), name="attn_bwd",
        in_specs=[ANY] * (3 * n_in + ns), out_specs=[ANY] * (3 + ns),
        out_shape=[jax.ShapeDtypeStruct((S, AW), F32)] * 3 + _chips_shapes(chip_sums),
        scratch_shapes=[padded] * 16 + [pltpu.VMEM((S // 8, 8, LANES), F32)] * 3
        + [pltpu.VMEM((S // 16, 16, LANES), F32)] * 3
        + [pltpu.VMEM((4, 256, WIN), F32), pltpu.VMEM((4, 128, WIN), F32),
           pltpu.SemaphoreType.DMA((3, n_in)), pltpu.SemaphoreType.DMA((2, 3))]
        + (_chips_scratch(ns) if ns else []),
        compiler_params=_cparams(60))(*flat, *chip_sums)


def _in_proj_bwd(dq, dk, dv, dbcu, dqx, cos, sins, w16, x, g, dx1):
    def body(dq_ref, dk_ref, dv_ref, dbcu_ref, dqx_ref, c_ref, s_ref, w_ref, x_ref, g_ref, dx1_ref,
             dp_ref, gx_ref, dg_ref):
        @pl.when(pl.program_id(0) == 0)
        def _():
            dg_ref[...] = jnp.zeros_like(dg_ref)

        cos, sn = c_ref[...], s_ref[...]
        dqr = dq_ref[...] * SCALE
        dkr = dk_ref[...]
        dp = jnp.concatenate([dqr * cos + _rot_half(dqr * sn), dkr * cos + _rot_half(dkr * sn), dv_ref[...],
                              dbcu_ref[...], dqx_ref[...]], axis=1).astype(BF16)
        dp_ref[...] = dp
        dh = _dot_nt(dp, w_ref[...])
        g = g_ref[...]
        _, n, r = _rms(x_ref[...], g)
        dx, dg = _rms_bwd(dh, n, r, g)
        gx_ref[...] = dx1_ref[...] + dx
        dg_ref[0:1, :] += dg

    def tile(w):
        return pl.BlockSpec((TQ, w), lambda i: (i, 0))

    return pl.pallas_call(
        body, grid=(NT,), name="in_proj_bwd",
        in_specs=[tile(AW), tile(AW), tile(AW), tile(3 * CW), tile(XW), tile(AW), tile(AW), _const((D, PW)),
                  tile(D), _const((1, D)), tile(D)],
        out_specs=[tile(PW), tile(D), _acc((SUBLANES, D))],
        out_shape=[jax.ShapeDtypeStruct((S, PW), BF16), jax.ShapeDtypeStruct((S, D), F32),
                   jax.ShapeDtypeStruct((SUBLANES, D), F32)],
        compiler_params=_cparams(56))(dq, dk, dv, dbcu, dqx, cos, sins, w16, x, g, dx1)


def _mem_bwd(mem, g_mem, wkv16, dkv):
    def body(m_ref, g_ref, w_ref, dkv_ref, dkv16_ref, dg_ref):
        dkv16 = dkv_ref[...].astype(BF16)
        dkv16_ref[...] = dkv16
        _, n, _ = _rms(m_ref[...], g_ref[...])
        dg = jnp.sum(_dot_nt(dkv16, w_ref[...]) * n, axis=0, keepdims=True)
        dg_ref[...] = jnp.broadcast_to(dg, dg_ref.shape)

    return pl.pallas_call(
        body, name="mem_bwd",
        out_shape=[jax.ShapeDtypeStruct((N_MEM, 2 * XW), BF16), jax.ShapeDtypeStruct((SUBLANES, D), F32)],
        compiler_params=pltpu.CompilerParams(vmem_limit_bytes=32 << 20))(mem, g_mem, wkv16, dkv)


def _wgrad(a16, b16, tn, name, square_b=False, transpose_out=False):
    kk, m = a16.shape
    n_tiles = b16.shape[1] // tn
    chunk = min(kk, 512)
    oshape = (tn, m) if transpose_out else (m, tn)

    def body(a_ref, b_ref, o32_ref, o16_ref, at):
        @pl.when(pl.program_id(0) == 0)
        def _():
            for c in range(kk // chunk):
                at[:, c * chunk:(c + 1) * chunk] = a_ref[c * chunk:(c + 1) * chunk, :].T

        b = b_ref[...]
        if square_b:
            b = b * b
        acc = _dot(at[...], b)
        if transpose_out:
            acc = acc.T
        o32_ref[0] = acc
        o16_ref[0] = acc.astype(BF16)

    oblk = pl.BlockSpec((1,) + oshape, lambda j: (j, 0, 0))
    return pl.pallas_call(
        body, grid=(n_tiles,), name=name,
        in_specs=[_const((kk, m)), pl.BlockSpec((kk, tn), lambda j: (0, j))],
        out_specs=[oblk, oblk],
        out_shape=[jax.ShapeDtypeStruct((n_tiles,) + oshape, F32), jax.ShapeDtypeStruct((n_tiles,) + oshape, BF16)],
        scratch_shapes=[pltpu.VMEM((m, kk), BF16)],
        compiler_params=_cparams(56))(a16, b16)


def _adamw_math(w, g, m, v):
    m = ADAM_B1 * m + (1.0 - ADAM_B1) * g
    v = ADAM_B2 * v + (1.0 - ADAM_B2) * jnp.square(g)
    m_hat = m / (1.0 - ADAM_B1 ** ADAM_STEP)
    v_hat = v / (1.0 - ADAM_B2 ** ADAM_STEP)
    delta = -ADAM_LR * (m_hat / (jnp.sqrt(v_hat) + ADAM_EPS) + ADAM_WD * w)
    return delta, m, v


def _adamw_shard(own32, rb16, w, m, v, name):
    def body(o_ref, r_ref, w_ref, m_ref, v_ref, g_out, d_out, m_out, v_out):
        g = o_ref[...] + r_ref[0].astype(F32) + r_ref[1].astype(F32) + r_ref[2].astype(F32)
        g_out[...] = g
        d_out[...], m_out[...], v_out[...] = _adamw_math(w_ref[...], g, m_ref[...], v_ref[...])

    return pl.pallas_call(
        body, name=name, out_shape=[jax.ShapeDtypeStruct(w.shape, F32)] * 4,
        compiler_params=pltpu.CompilerParams(vmem_limit_bytes=48 << 20))(own32, rb16, w, m, v)


def _adamw_small(g, w, m, v):
    def body(g_ref, w_ref, m_ref, v_ref, d_out, m_out, v_out):
        d_out[...], m_out[...], v_out[...] = _adamw_math(w_ref[...], g_ref[...], m_ref[...], v_ref[...])

    return pl.pallas_call(body, name="adamw_small", out_shape=[jax.ShapeDtypeStruct(w.shape, F32)] * 3)(g, w, m, v)


def _place():
    x, y, c = lax.axis_index("x"), lax.axis_index("y"), lax.axis_index("c")
    chips = [(1 - x, y), (x, 1 - y), (1 - x, 1 - y)]
    return x, y, c, chips


def _gather_steps(ins, outs, send, recv, lsem):
    nt = len(ins)
    x, y, c, chips = _place()
    me, sib = (x, y, c), (x, y, 1 - c)

    def slot(t, px, py, pc):
        return outs[t].at[4 * px + 2 * py + pc]

    def copy(t, k, block, to, src=None):
        return pltpu.make_async_remote_copy(
            src_ref=slot(t, *block) if src is None else src, dst_ref=slot(t, *block),
            send_sem=send.at[t, k], recv_sem=recv.at[t, k], device_id=to, device_id_type=MESH)

    mine = [pltpu.make_async_copy(ins[t], slot(t, *me), lsem.at[t]) for t in range(nt)]
    first = []
    for t in range(nt):
        first.append(copy(t, 0, me, sib, src=ins[t]))
        first += [copy(t, 1 + j, me, (*chip, c), src=ins[t]) for j, chip in enumerate(chips)]

    def start():
        for cp in mine + first:
            cp.start()

    def finish():
        passed = []
        for j, chip in enumerate(chips):
            for t in range(nt):
                copy(t, 1 + j, (*chip, c), me).wait_recv()
                fwd = copy(t, 4 + j, (*chip, c), sib)
                fwd.start()
                passed.append(fwd)
        for t in range(nt):
            copy(t, 0, sib, me).wait_recv()
            for j, chip in enumerate(chips):
                copy(t, 4 + j, (*chip, 1 - c), me).wait_recv()
        for cp in first + passed:
            cp.wait_send()
        for cp in mine:
            cp.wait()

    return start, finish


def _gather_scratch(nt):
    return [pltpu.SemaphoreType.DMA((nt, 7)), pltpu.SemaphoreType.DMA((nt, 7)), pltpu.SemaphoreType.DMA((nt,))]


def _gathered_shapes(shards):
    return [jax.ShapeDtypeStruct((N_DEV,) + s.shape, s.dtype) for s in shards]


def _all_gather(shards):
    nt = len(shards)

    def body(*refs):
        start, finish = _gather_steps(refs[:nt], refs[nt:2 * nt], *refs[2 * nt:])
        start()
        finish()

    return pl.pallas_call(
        body, name="all_gather_weights", in_specs=[ANY] * nt, out_specs=[ANY] * nt,
        out_shape=_gathered_shapes(shards), scratch_shapes=_gather_scratch(nt))(*shards)


def _rs_pair(g16s, name):
    nt = len(g16s)

    def body(*refs):
        ins, outs = refs[:nt], refs[nt:2 * nt]
        send, recv = refs[2 * nt:]
        x, y, c, _ = _place()
        copies = [pltpu.make_async_remote_copy(
            src_ref=ins[t].at[2 * p + (1 - c)], dst_ref=outs[t].at[p], send_sem=send.at[t, p], recv_sem=recv.at[t, p],
            device_id=(x, y, 1 - c), device_id_type=MESH) for t in range(nt) for p in range(4)]
        for cp in copies:
            cp.start()
        for cp in copies:
            cp.wait()

    return pl.pallas_call(
        body, name=name,
        in_specs=[ANY] * nt, out_specs=[ANY] * nt,
        out_shape=[jax.ShapeDtypeStruct((4,) + g.shape[1:], g.dtype) for g in g16s],
        scratch_shapes=[pltpu.SemaphoreType.DMA((nt, 4)), pltpu.SemaphoreType.DMA((nt, 4))])(*g16s)


def _rs_pair_add(place, g32, ra16, name):
    shp = g32.shape[1:]

    def body(pl_ref, g_ref, r_ref, cs_ref, own_ref):
        s = g_ref[0] + r_ref[0].astype(F32)
        cs_ref[0] = s.astype(BF16)

        @pl.when(pl.program_id(0) == pl_ref[1])
        def _():
            own_ref[...] = s

    blk = (1,) + shp
    return pl.pallas_call(
        body, name=name,
        grid_spec=pltpu.PrefetchScalarGridSpec(
            num_scalar_prefetch=1, grid=(4,),
            in_specs=[pl.BlockSpec(blk, lambda p, s: (2 * p + s[0], 0, 0)), pl.BlockSpec(blk, lambda p, s: (p, 0, 0))],
            out_specs=[pl.BlockSpec(blk, lambda p, s: (p, 0, 0)), pl.BlockSpec(shp, lambda p, s: (0, 0))]),
        out_shape=[jax.ShapeDtypeStruct((4,) + shp, BF16), jax.ShapeDtypeStruct(shp, F32)],
        compiler_params=_cparams(48))(place, g32, ra16)


def _chips_steps(ins, outs, send, recv):
    _, _, c, chips = _place()
    copies = [pltpu.make_async_remote_copy(
        src_ref=ins[t].at[2 * px + py], dst_ref=outs[t].at[j], send_sem=send.at[t, j], recv_sem=recv.at[t, j],
        device_id=(px, py, c), device_id_type=MESH) for t in range(len(ins)) for j, (px, py) in enumerate(chips)]

    def start():
        for cp in copies:
            cp.start()

    def finish():
        for cp in copies:
            cp.wait()

    return start, finish


def _chips_scratch(nt):
    return [pltpu.SemaphoreType.DMA((nt, 3)), pltpu.SemaphoreType.DMA((nt, 3))]


def _chips_shapes(cs16s):
    return [jax.ShapeDtypeStruct((3,) + g.shape[1:], g.dtype) for g in cs16s]


def _rs_chips(cs16s):
    nt = len(cs16s)

    def body(*refs):
        start, finish = _chips_steps(refs[:nt], refs[nt:2 * nt], *refs[2 * nt:])
        start()
        finish()

    return pl.pallas_call(
        body, name="reduce_scatter_chips", in_specs=[ANY] * nt, out_specs=[ANY] * nt,
        out_shape=_chips_shapes(cs16s), scratch_shapes=_chips_scratch(nt))(*cs16s)


def _all_reduce_small(pack):
    def body(p_ref, o_ref, land, send, recv):
        x, y, c, _ = _place()
        me = 4 * x + 2 * y + c
        land[me] = p_ref[...]
        copies = []
        for k in range(1, N_DEV):
            kx, ky, kc = (k >> 2) & 1, (k >> 1) & 1, k & 1
            peer = (1 - x if kx else x, 1 - y if ky else y, 1 - c if kc else c)
            copies.append(pltpu.make_async_remote_copy(
                src_ref=p_ref, dst_ref=land.at[me], send_sem=send.at[k - 1], recv_sem=recv.at[k - 1],
                device_id=peer, device_id_type=MESH))
        for cp in copies:
            cp.start()
        for cp in copies:
            cp.wait()
        tot = land[0]
        for s in range(1, N_DEV):
            tot = tot + land[s]
        o_ref[...] = tot

    return pl.pallas_call(
        body, name="all_reduce_small", out_shape=jax.ShapeDtypeStruct(pack.shape, F32),
        in_specs=[pl.BlockSpec(memory_space=pltpu.VMEM)], out_specs=pl.BlockSpec(memory_space=pltpu.VMEM),
        scratch_shapes=[pltpu.VMEM((N_DEV,) + pack.shape, F32), pltpu.SemaphoreType.DMA((N_DEV - 1,)),
                        pltpu.SemaphoreType.DMA((N_DEV - 1,))])(pack)


GAINS = (("g_pre_mix", D), ("g_mem", D), ("g_attn_out", AW), ("g_conv_out", CW), ("g_xattn_out", XW),
         ("g_post_mix", D), ("g_pre_mlp", D), ("g_post_mlp", D))
GAIN_ROWS = sum(w for _, w in GAINS) // LANES
CONV_ROWS = 3 * CW // LANES
PACK_ROWS = 56


def _rows(a):
    return a.reshape(-1, LANES)


def _reduce_to_chip_sums(place, grads):
    from_sib = _rs_pair([g16 for _, g16 in grads.values()], "reduce_scatter_pair_" + "_".join(grads))
    return {n: _rs_pair_add(place, g32, from_sib[t], "pair_add_" + n) for t, (n, (g32, _)) in enumerate(grads.items())}


def _local_step(x, mem, pos, gains, cw_full, win16, wkv16, late_shards, tgt, place):
    half = HEAD // 2
    inv_freq = jnp.float32(ROPE_THETA) ** (-(jnp.arange(half, dtype=F32) * 2.0 / HEAD))
    invf = jnp.tile(inv_freq, LANES // half)[None, :]
    sgn = jnp.tile(jnp.concatenate([-jnp.ones((half,), F32), jnp.ones((half,), F32)]), LANES // HEAD)[None, :]
    cos, sins = _rope_table(pos.astype(F32).reshape(S, 1), invf, sgn)
    cw8 = jnp.zeros((SUBLANES, CW), F32).at[0:3].set(cw_full)

    memn16, kv16 = _mem_fwd(mem, gains["g_mem"], wkv16)
    q, k, v, bcu, qx16, h16 = _in_proj(x, gains["g_pre_mix"], win16, cos, sins)
    y_attn, ltot, wout8, wup8, wdn8 = _attn_fwd(q, k, v, late_shards)
    wout16, wdn16 = wout8.reshape(D, D), wdn8.reshape(FF, D)
    ypre, y16, y2, x1 = _mix_out(y_attn, bcu, qx16, kv16, cw8, gains["g_attn_out"], gains["g_conv_out"],
                                 gains["g_xattn_out"], gains["g_post_mix"], wout16, x)
    a16, du16, h2_16, df2_16, dx1, loss8, dg_mlp = _mlp(x1, tgt, gains["g_pre_mlp"], gains["g_post_mlp"], wup8, wdn16)

    mlp_sums = _reduce_to_chip_sums(place, {
        "w_up": _wgrad(h2_16, du16, FF_BLK, "wgrad_up"),
        "w_down": _wgrad(df2_16, a16, FF_BLK, "wgrad_down", square_b=True, transpose_out=True)})

    head_id = jnp.arange(AW, dtype=jnp.int32) // HEAD
    head_ones = (head_id[:, None] == head_id[None, :]).astype(BF16)
    dy2_16, dya, ld, dbcu, dqx, dgs, dcw, dkv = _mix_out_bwd(
        dx1, y2, ypre, ltot, head_ones, bcu, qx16, kv16, cw8, gains["g_post_mix"], gains["g_attn_out"],
        gains["g_conv_out"], gains["g_xattn_out"], wout16)
    dq, dk, dv, up_chips, dn_chips = _attn_bwd(q, k, v, dya, ld, [mlp_sums["w_up"][0], mlp_sums["w_down"][0]])
    dproj16, grad_x, dg_in = _in_proj_bwd(dq, dk, dv, dbcu, dqx, cos, sins, win16, x, gains["g_pre_mix"], dx1)
    dkv16, dg_mem = _mem_bwd(mem, gains["g_mem"], wkv16, dkv)

    def by_owner_in(g):
        return g.transpose(1, 0, 2).reshape(D, N_DEV, PW // N_DEV).transpose(1, 0, 2)

    mix_sums = _reduce_to_chip_sums(place, {
        "w_in": tuple(by_owner_in(g) for g in _wgrad(h16, dproj16, 512, "wgrad_in")),
        "w_mem_kv": tuple(g.reshape(N_DEV, D // N_DEV, 2 * XW) for g in _wgrad(memn16, dkv16, 2 * XW, "wgrad_mem_kv")),
        "w_out": tuple(g.reshape(N_DEV, D // N_DEV, D) for g in _wgrad(y16, dy2_16, D, "wgrad_out"))})
    mix_chips = _rs_chips([s[0] for s in mix_sums.values()])
    reduced = {n: (s[1], mix_chips[t]) for t, (n, s) in enumerate(mix_sums.items())}
    reduced["w_up"] = (mlp_sums["w_up"][1], up_chips)
    reduced["w_down"] = (mlp_sums["w_down"][1], dn_chips)
    small = {
        "g_pre_mix": dg_in[0], "g_mem": dg_mem[0], "g_attn_out": dgs[1, 0:AW], "g_conv_out": dgs[1, AW:AW + CW],
        "g_xattn_out": dgs[1, AW + CW:], "g_post_mix": dgs[0], "g_pre_mlp": dg_mlp[1], "g_post_mlp": dg_mlp[0],
    }
    return loss8[0, 0], grad_x, reduced, small, dcw[0:3]


BIG = ("w_in", "w_mem_kv", "w_out", "w_up", "w_down")
ORDER = ("g_pre_mix", "g_mem", "w_in", "w_mem_kv", "conv_w", "g_attn_out", "g_conv_out", "g_xattn_out", "w_out",
         "g_post_mix", "g_pre_mlp", "w_up", "w_down", "g_post_mlp")


def kernel(x, mem, positions, g_pre_mix, g_mem, w_in, w_mem_kv, conv_w, g_attn_out, g_conv_out, g_xattn_out, w_out, g_post_mix, g_pre_mlp, w_up, w_down, g_post_mlp, loss_target, m_g_pre_mix, m_g_mem, m_w_in, m_w_mem_kv, m_conv_w, m_g_attn_out, m_g_conv_out, m_g_xattn_out, m_w_out, m_g_post_mix, m_g_pre_mlp, m_w_up, m_w_down, m_g_post_mlp, v_g_pre_mix, v_g_mem, v_w_in, v_w_mem_kv, v_conv_w, v_g_attn_out, v_g_conv_out, v_g_xattn_out, v_w_out, v_g_post_mix, v_g_pre_mlp, v_w_up, v_w_down, v_g_post_mlp):
    w = dict(g_pre_mix=g_pre_mix, g_mem=g_mem, w_in=w_in, w_mem_kv=w_mem_kv, conv_w=conv_w, g_attn_out=g_attn_out,
             g_conv_out=g_conv_out, g_xattn_out=g_xattn_out, w_out=w_out, g_post_mix=g_post_mix, g_pre_mlp=g_pre_mlp,
             w_up=w_up, w_down=w_down, g_post_mlp=g_post_mlp)
    mo = dict(g_pre_mix=m_g_pre_mix, g_mem=m_g_mem, w_in=m_w_in, w_mem_kv=m_w_mem_kv, conv_w=m_conv_w,
              g_attn_out=m_g_attn_out, g_conv_out=m_g_conv_out, g_xattn_out=m_g_xattn_out, w_out=m_w_out,
              g_post_mix=m_g_post_mix, g_pre_mlp=m_g_pre_mlp, w_up=m_w_up, w_down=m_w_down, g_post_mlp=m_g_post_mlp)
    vo = dict(g_pre_mix=v_g_pre_mix, g_mem=v_g_mem, w_in=v_w_in, w_mem_kv=v_w_mem_kv, conv_w=v_conv_w,
              g_attn_out=v_g_attn_out, g_conv_out=v_g_conv_out, g_xattn_out=v_g_xattn_out, w_out=v_w_out,
              g_post_mix=v_g_post_mix, g_pre_mlp=v_g_pre_mlp, w_up=v_w_up, w_down=v_w_down, g_post_mlp=v_g_post_mlp)

    xi, yi, ci = lax.axis_index("x"), lax.axis_index("y"), lax.axis_index("c")
    me = 4 * xi + 2 * yi + ci
    place = jnp.stack([ci, 2 * xi + yi]).astype(jnp.int32)

    conv_tile = jnp.zeros((SUBLANES, LANES), F32).at[0:3, 0:CW // N_DEV].set(conv_w[0])
    win8, wkv8, conv8 = _all_gather([w["w_in"][0].astype(BF16), w["w_mem_kv"][0].astype(BF16), conv_tile])
    win16 = win8.transpose(1, 0, 2).reshape(D, PW)
    wkv16 = wkv8.reshape(D, 2 * XW)
    cw_full = conv8[:, 0:3, 0:CW // N_DEV].transpose(1, 0, 2).reshape(3, CW)
    late_shards = [w[n][0].astype(BF16) for n in ("w_out", "w_up", "w_down")]

    gains = {n: w[n] for n, _ in GAINS}
    loss, grad_x, reduced, small, dcw = _local_step(
        x[0], mem[0], positions[0], gains, cw_full, win16, wkv16, late_shards, loss_target[0], place)
    loss = lax.psum(loss, ("x", "y", "c"))

    grad, delta, new_m, new_v = {}, {}, {}, {}
    for n in BIG:
        g, d_, m_, v_ = _adamw_shard(*reduced[n], w[n][0], mo[n][0], vo[n][0], "adamw_" + n)
        grad[n], delta[n], new_m[n], new_v[n] = g[None], d_[None], m_[None], v_[None]

    pack = jnp.concatenate([_rows(small[n]) for n, _ in GAINS] + [_rows(dcw), jnp.zeros((PACK_ROWS - GAIN_ROWS - CONV_ROWS, LANES), F32)])
    tot = _all_reduce_small(pack)
    conv_grad = lax.dynamic_slice(tot[GAIN_ROWS:GAIN_ROWS + CONV_ROWS].reshape(3, CW), (0, me * (CW // N_DEV)), (3, CW // N_DEV))

    def small_pack(get):
        conv_row = jnp.zeros((LANES,), F32).at[0:3 * CW // N_DEV].set(get("conv_w").reshape(-1))[None, :]
        return jnp.concatenate([_rows(get(n)) for n, _ in GAINS] + [conv_row, jnp.zeros((PACK_ROWS - GAIN_ROWS - 1, LANES), F32)])

    gpack = jnp.concatenate([tot[0:GAIN_ROWS], jnp.zeros((LANES,), F32).at[0:3 * CW // N_DEV].set(conv_grad.reshape(-1))[None, :],
                             jnp.zeros((PACK_ROWS - GAIN_ROWS - 1, LANES), F32)])
    dpack, mpack, vnew = _adamw_small(gpack, small_pack(lambda n: w[n][0]), small_pack(lambda n: mo[n][0]),
                                      small_pack(lambda n: vo[n][0]))

    row = 0
    for n, width in GAINS:
        nr = width // LANES
        grad[n] = tot[row:row + nr].reshape(1, width)
        delta[n], new_m[n], new_v[n] = (p[row:row + nr].reshape(1, width) for p in (dpack, mpack, vnew))
        row += nr
    grad["conv_w"] = conv_grad[None]
    delta["conv_w"], new_m["conv_w"], new_v["conv_w"] = (
        p[GAIN_ROWS, 0:3 * CW // N_DEV].reshape(1, 3, CW // N_DEV) for p in (dpack, mpack, vnew))

    return (loss, grad_x[None], *[grad[n] for n in ORDER], *[delta[n] for n in ORDER],
            *[new_m[n] for n in ORDER], *[new_v[n] for n in ORDER])
```

```python
import functools

import numpy as np
import jax
import jax.numpy as jnp
from jax import lax
from jax.experimental import pallas as pl
from jax.experimental.pallas import tpu as pltpu

F32, BF16 = jnp.float32, jnp.bfloat16
MESH = pl.DeviceIdType.MESH
ANY = pl.BlockSpec(memory_space=pl.ANY)

N_DEV = 8
D = 1024
S = 4096
N_MEM = 256
HEAD = 64
AW, CW, XW = 512, 256, 256
PW = 3 * AW + 3 * CW + XW
FF = 4096
FF_BLK = FF // N_DEV
PATTERNS = ((128, 1), (512, 4), (2048, 16))
QB = 128
EPS = 1e-6
NEG = -1e30
SCALE = HEAD ** -0.5
ROPE_THETA = 10000.0
LANES = 128
SUBLANES = 8

ADAM_LR, ADAM_B1, ADAM_B2, ADAM_EPS, ADAM_WD, ADAM_STEP = 0.001, 0.9, 0.999, 1e-08, 0.01, 10

TQ = 512
TQ_MLP = 256
NT = S // TQ


def _cparams(vmem_mb, n_grid=1):
    return pltpu.CompilerParams(dimension_semantics=("arbitrary",) * n_grid, vmem_limit_bytes=vmem_mb << 20)


def _const(shape):
    nd = len(shape)
    return pl.BlockSpec(shape, lambda *_: (0,) * nd, pipeline_mode=pl.Buffered(1))


def _acc(shape):
    nd = len(shape)
    return pl.BlockSpec(shape, lambda *_: (0,) * nd)


def _dot(a, b):
    return jnp.dot(a, b, preferred_element_type=F32)


def _dot_nt(a, b):
    return lax.dot_general(a, b, (((1,), (1,)), ((), ())), preferred_element_type=F32)


def _dot_tn(a, b):
    return lax.dot_general(a, b, (((0,), (0,)), ((), ())), preferred_element_type=F32)


def _rms(x, g):
    r = lax.rsqrt(jnp.mean(x * x, axis=-1, keepdims=True) + EPS)
    n = x * r
    return n * g, n, r


def _rms_bwd(dy, n, r, g):
    dn = dy * g
    dx = r * (dn - n * jnp.mean(dn * n, axis=-1, keepdims=True))
    return dx, jnp.sum(dy * n, axis=0, keepdims=True)


def _rot_half(t):
    lane = lax.broadcasted_iota(jnp.int32, t.shape, 1)
    n = t.shape[1]
    return jnp.where((lane % HEAD) < HEAD // 2, pltpu.roll(t, n - HEAD // 2, 1), pltpu.roll(t, HEAD // 2, 1))


def _rope_table(pos_col, invf, sgn):
    def body(p_ref, f_ref, s_ref, c_out, s_out):
        ang = p_ref[...] * f_ref[...]
        c_out[...] = jnp.tile(jnp.cos(ang), (1, AW // LANES))
        s_out[...] = jnp.tile(jnp.sin(ang) * s_ref[...], (1, AW // LANES))

    tile = pl.BlockSpec((TQ, AW), lambda i: (i, 0))
    return pl.pallas_call(
        body, grid=(NT,), name="rope_table",
        in_specs=[pl.BlockSpec((TQ, 1), lambda i: (i, 0)), _const((1, LANES)), _const((1, LANES))],
        out_specs=[tile, tile], out_shape=[jax.ShapeDtypeStruct((S, AW), F32)] * 2,
        compiler_params=_cparams(32))(pos_col, invf, sgn)


def _mem_fwd(mem, g_mem, wkv16):
    def body(m_ref, g_ref, w_ref, n16_ref, kv_ref):
        y, _, _ = _rms(m_ref[...], g_ref[...])
        y16 = y.astype(BF16)
        n16_ref[...] = y16
        kv_ref[...] = _dot(y16, w_ref[...]).astype(BF16)

    return pl.pallas_call(
        body, name="mem_fwd",
        out_shape=[jax.ShapeDtypeStruct((N_MEM, D), BF16), jax.ShapeDtypeStruct((N_MEM, 2 * XW), BF16)],
        compiler_params=pltpu.CompilerParams(vmem_limit_bytes=32 << 20))(mem, g_mem, wkv16)


def _in_proj(x, g, w16, cos, sins):
    def body(x_ref, g_ref, w_ref, c_ref, s_ref, q_ref, k_ref, v_ref, bcu_ref, qx_ref, h_ref):
        y, _, _ = _rms(x_ref[...], g_ref[...])
        h = y.astype(BF16)
        h_ref[...] = h
        proj = _dot(h, w_ref[...])
        cos, sn = c_ref[...], s_ref[...]
        q, k = proj[:, 0:AW], proj[:, AW:2 * AW]
        q_ref[...] = (q * cos + _rot_half(q) * sn) * SCALE
        k_ref[...] = k * cos + _rot_half(k) * sn
        v_ref[...] = proj[:, 2 * AW:3 * AW]
        bcu_ref[...] = proj[:, 3 * AW:3 * AW + 3 * CW]
        qx_ref[...] = (proj[:, 3 * AW + 3 * CW:] * SCALE).astype(BF16)

    def tile(w):
        return pl.BlockSpec((TQ, w), lambda i: (i, 0))

    return pl.pallas_call(
        body, grid=(NT,), name="in_proj",
        in_specs=[tile(D), _const((1, D)), _const((D, PW)), tile(AW), tile(AW)],
        out_specs=[tile(AW), tile(AW), tile(AW), tile(3 * CW), tile(XW), tile(D)],
        out_shape=[jax.ShapeDtypeStruct((S, AW), F32)] * 3 + [
            jax.ShapeDtypeStruct((S, 3 * CW), F32), jax.ShapeDtypeStruct((S, XW), BF16),
            jax.ShapeDtypeStruct((S, D), BF16)],
        compiler_params=_cparams(56))(x, g, w16, cos, sins)


ATTN_PLANS = (("p1", 1, 128, 32), ("p4", 8, 64, 8), ("p16", 16, 128, 2))
PAD = 128
WIN = 256


ATTN_UNROLL = 8


def _fill_bias(tab, qblk, partner):
    qi = lax.broadcasted_iota(jnp.int32, (2 * qblk, WIN), 0) & (qblk - 1)
    kj = lax.broadcasted_iota(jnp.int32, (2 * qblk, WIN), 1)
    piece = kj >> (qblk.bit_length() - 1)
    kk = kj & (qblk - 1)
    prev = (piece & 1) == 0
    of_partner = piece >= 2
    for first in (0, 1):
        for par in (0, 1):
            lo = jnp.where(prev, (qblk if first else qi) + jnp.where(of_partner, par, 0), 0)
            hi = jnp.where(prev, qblk, qi + jnp.where(of_partner, par - 1, 0))
            tab[2 * first + par] = jnp.where((kk >= lo) & (kk <= hi), 0.0, NEG).astype(F32)


def _block_rows(g, qblk, nbc, partner):
    own = pl.ds(pl.multiple_of(PAD + g * qblk, qblk), qblk)
    first = ((g & (nbc - 1)) == 0).astype(jnp.int32)
    if partner:
        gp = jnp.bitwise_xor(g, 4 * nbc)
        wins = (pl.ds(pl.multiple_of(PAD + (g - 1) * qblk, qblk), 2 * qblk),
                pl.ds(pl.multiple_of(PAD + (gp - 1) * qblk, qblk), 2 * qblk))
        return own, wins, 2 * first + ((g >> ((4 * nbc).bit_length() - 1)) & 1)
    return own, (pl.ds(pl.multiple_of(PAD + (g - 1) * qblk, qblk), 2 * qblk),), 2 * first


def _window(ref, wins):
    parts = [ref[w, :].astype(BF16) for w in wins]
    return parts[0] if len(parts) == 1 else jnp.concatenate(parts, axis=0)


def _stack_heads(t, lane):
    zero = jnp.zeros_like(t)
    return jnp.concatenate([jnp.where(lane < HEAD, t, zero), jnp.where(lane >= HEAD, t, zero)], axis=0)


def _unstack_heads(t2, lane):
    half = t2.shape[0] // 2
    return jnp.where(lane < HEAD, t2[0:half, :], t2[half:, :])


def _gather_classes(views, bufs, sems, lanes):
    waits = []
    for i, (view, buf) in enumerate(zip(views, bufs)):
        if view.ndim == 2:
            pltpu.make_async_copy(view.at[:, lanes], buf.at[pl.ds(PAD, S), :], sems.at[i]).start()
        else:
            n_cls, per = view.shape[1], view.shape[0]
            for c in range(n_cls):
                pltpu.make_async_copy(view.at[:, c, lanes], buf.at[pl.ds(PAD + c * per, per), :], sems.at[i]).start()
        whole = buf.at[pl.ds(PAD, S), :]
        waits.append(pltpu.make_async_copy(whole, whole, sems.at[i]))
    return waits


def _scatter_classes(bufs, lands, sems):
    waits = []
    for i, (buf, land) in enumerate(zip(bufs, lands)):
        per, n_cls = land.shape[0], land.shape[1]
        for c in range(n_cls):
            pltpu.make_async_copy(buf.at[pl.ds(PAD + c * per, per), :], land.at[:, c, :], sems.at[i]).start()
        waits.append(pltpu.make_async_copy(land, land, sems.at[i]))
    return waits


def _attn_fwd(q, k, v, shards=()):
    views = [[a] + [a.reshape(S // n, n, AW) for _, n, _, _ in ATTN_PLANS[1:]] for a in (q, k, v)]
    flat = [views[a][p] for p in range(3) for a in range(3)]
    ng = len(shards)
    n_grid = AW // LANES

    def body(*refs):
        hbm = [refs[3 * p:3 * p + 3] for p in range(3)]
        refs = refs[9:]
        shard_refs, refs = refs[:ng], refs[ng:]
        y_ref, lt_ref = refs[0:2]
        whole_refs, refs = refs[2:2 + ng], refs[2 + ng:]
        bufs = [refs[3 * p:3 * p + 3] for p in range(3)]
        oc, lc, o4n, l4n, o16n, l16n, tab128, tab4, sem_in, sem_out = refs[9:19]
        if ng:
            start_gather, finish_gather = _gather_steps(shard_refs, whole_refs, *refs[19:])
            pl.when(pl.program_id(0) == 0)(start_gather)
        lanes = pl.ds(pl.multiple_of(pl.program_id(0) * LANES, LANES), LANES)
        waits = [_gather_classes(hbm[p], bufs[p], sem_in.at[p], lanes) for p in range(3)]

        @pl.when(pl.program_id(0) == 0)
        def _():
            for p in range(3):
                for b in bufs[p]:
                    b[0:PAD, :] = jnp.zeros((PAD, LANES), F32)
            _fill_bias(tab128, 128, False)
            _fill_bias(tab4, 64, True)

        lane = lax.broadcasted_iota(jnp.int32, (1, LANES), 1)
        ones = jnp.ones((WIN, LANES), BF16)

        def run(plan, bq, bk, bv, tab, o_dst, l_dst, dst_pad):
            _, n_cls, qblk, nbc = plan
            partner = n_cls == 8

            def block(g, carry):
                own, wins, mask = _block_rows(g, qblk, nbc, partner)
                q2 = _stack_heads(bq[own, :].astype(BF16), lane)
                kw = _window(bk, wins)
                vw = jnp.concatenate([_window(bv, wins), ones], axis=1)
                s = _dot_nt(q2, kw) + tab[mask]
                m = jnp.max(s, axis=1, keepdims=True)
                oe = _dot(jnp.exp(s - m).astype(BF16), vw)
                den = oe[:, LANES:]
                dst = pl.ds(pl.multiple_of(dst_pad + g * qblk, qblk), qblk)
                o_dst[dst, :] = _unstack_heads(oe[:, 0:LANES] / den, lane)
                l_dst[dst, :] = _unstack_heads(m + jnp.log(den), lane)
                return carry
            lax.fori_loop(0, n_cls * nbc, block, 0, unroll=ATTN_UNROLL)

        for w in waits[0]:
            w.wait()
        run(ATTN_PLANS[0], *bufs[0], tab128, y_ref, lt_ref, 0)
        for w in waits[1]:
            w.wait()
        run(ATTN_PLANS[1], *bufs[1], tab4, oc, lc, PAD)
        for w in _scatter_classes((oc, lc), (o4n, l4n), sem_out.at[0]):
            w.wait()
        for w in waits[2]:
            w.wait()
        run(ATTN_PLANS[2], *bufs[2], tab128, oc, lc, PAD)
        for w in _scatter_classes((oc, lc), (o16n, l16n), sem_out.at[1]):
            w.wait()

        for t in range(S // TQ):
            rows = pl.ds(t * TQ, TQ)
            r4, r16 = pl.ds(t * (TQ // 8), TQ // 8), pl.ds(t * (TQ // 16), TQ // 16)
            l0, l1, l2 = lt_ref[rows, :], l4n[r4, :, :].reshape(TQ, LANES), l16n[r16, :, :].reshape(TQ, LANES)
            lm = jnp.maximum(jnp.maximum(l0, l1), l2)
            e0, e1, e2 = jnp.exp(l0 - lm), jnp.exp(l1 - lm), jnp.exp(l2 - lm)
            den = e0 + e1 + e2
            y_ref[rows, :] = (e0 * y_ref[rows, :] + e1 * o4n[r4, :, :].reshape(TQ, LANES)
                              + e2 * o16n[r16, :, :].reshape(TQ, LANES)) / den
            lt_ref[rows, :] = lm + jnp.log(den)

        if ng:
            pl.when(pl.program_id(0) == n_grid - 1)(finish_gather)

    col = pl.BlockSpec((S, LANES), lambda h: (0, h))
    padded = pltpu.VMEM((PAD + S, LANES), F32)
    return pl.pallas_call(
        body, grid=(n_grid,), name="attn_fwd",
        in_specs=[ANY] * (9 + ng), out_specs=[col, col] + [ANY] * ng,
        out_shape=[jax.ShapeDtypeStruct((S, AW), F32)] * 2 + _gathered_shapes(shards),
        scratch_shapes=[padded] * 9 + [padded, padded,
                        pltpu.VMEM((S // 8, 8, LANES), F32), pltpu.VMEM((S // 8, 8, LANES), F32),
                        pltpu.VMEM((S // 16, 16, LANES), F32), pltpu.VMEM((S // 16, 16, LANES), F32),
                        pltpu.VMEM((4, 256, WIN), F32), pltpu.VMEM((4, 128, WIN), F32),
                        pltpu.SemaphoreType.DMA((3, 3)), pltpu.SemaphoreType.DMA((2, 2))]
        + (_gather_scratch(ng) if ng else []),
        compiler_params=_cparams(56))(*flat, *shards)


def _conv_taps(z, zprev, row):
    z1 = jnp.where(row == 0, zprev[7:8, :], pltpu.roll(z, 1, 0))
    z2 = jnp.where(row == 0, zprev[6:7, :], jnp.where(row == 1, zprev[7:8, :], pltpu.roll(z, 2, 0)))
    return z1, z2


def _xattn_scores(qm, km):
    s = _dot_nt(qm, km)
    m = jnp.max(s, axis=1, keepdims=True)
    e = jnp.exp(s - m)
    return e, jnp.sum(e, axis=1, keepdims=True)


def _mix_out(y_attn, bcu, qx16, kv16, cw8, g_attn, g_conv, g_x, g_post, wout16, x):
    def body(ya_ref, bcu_ref, halo_ref, qx_ref, kv_ref, cw_ref, ga_ref, gc_ref, gx_ref, gp_ref, w_ref, x_ref,
             ypre_ref, y16_ref, y2_ref, x1_ref):
        i = pl.program_id(0)
        bcu = bcu_ref[...]
        b, c, u = bcu[:, 0:CW], bcu[:, CW:2 * CW], bcu[:, 2 * CW:]
        z = c * u
        halo = halo_ref[...]
        zprev = jnp.where(i > 0, halo[:, CW:2 * CW] * halo[:, 2 * CW:], 0.0)
        row = lax.broadcasted_iota(jnp.int32, z.shape, 0)
        z1, z2 = _conv_taps(z, zprev, row)
        cw = cw_ref[...]
        y_conv = b * (z2 * cw[0:1, :] + z1 * cw[1:2, :] + z * cw[2:3, :])

        qx = qx_ref[...]
        kv = kv_ref[...]
        km, vm = kv[:, 0:XW], kv[:, XW:]
        lane = lax.broadcasted_iota(jnp.int32, qx.shape, 1)
        y_x = jnp.zeros(qx.shape, F32)
        for h in range(XW // HEAD):
            hm = (lane >= h * HEAD) & (lane < (h + 1) * HEAD)
            e, l = _xattn_scores(jnp.where(hm, qx, jnp.zeros_like(qx)), km)
            y_x = jnp.where(hm, _dot(e.astype(BF16), vm) / l, y_x)

        y_attn = ya_ref[...]
        ypre_ref[:, 0:AW] = y_attn
        ypre_ref[:, AW:AW + CW] = y_conv
        ypre_ref[:, AW + CW:] = y_x
        y = jnp.concatenate([_rms(y_attn, ga_ref[...])[0], _rms(y_conv, gc_ref[...])[0],
                             _rms(y_x, gx_ref[...])[0]], axis=1).astype(BF16)
        y16_ref[...] = y
        y2 = _dot(y, w_ref[...])
        y2_ref[...] = y2
        x1_ref[...] = x_ref[...] + _rms(y2, gp_ref[...])[0]

    def tile(w):
        return pl.BlockSpec((TQ, w), lambda i: (i, 0))

    halo = pl.BlockSpec((SUBLANES, 3 * CW), lambda i: (jnp.maximum(i * (TQ // SUBLANES) - 1, 0), 0))
    return pl.pallas_call(
        body, grid=(NT,), name="mix_out",
        in_specs=[tile(AW), tile(3 * CW), halo, tile(XW), _const((N_MEM, 2 * XW)), _const((SUBLANES, CW)),
                  _const((1, AW)), _const((1, CW)), _const((1, XW)), _const((1, D)), _const((D, D)), tile(D)],
        out_specs=[tile(D), tile(D), tile(D), tile(D)],
        out_shape=[jax.ShapeDtypeStruct((S, D), F32), jax.ShapeDtypeStruct((S, D), BF16),
                   jax.ShapeDtypeStruct((S, D), F32), jax.ShapeDtypeStruct((S, D), F32)],
        compiler_params=_cparams(56))(y_attn, bcu, bcu, qx16, kv16, cw8, g_attn, g_conv, g_x, g_post, wout16, x)


def _mlp(x1, tgt, g_pre, g_post, wup8, wdn16):
    tq = TQ_MLP

    def body(x1_ref, t_ref, g1_ref, g2_ref, wu_ref, wd_ref,
             a16_ref, du_ref, h2_ref, df2_ref, dx1_ref, loss_ref, dg_ref, a32):
        @pl.when(pl.program_id(0) == 0)
        def _():
            loss_ref[...] = jnp.zeros_like(loss_ref)
            dg_ref[...] = jnp.zeros_like(dg_ref)

        x1 = x1_ref[...]
        g1, g2 = g1_ref[...], g2_ref[...]
        y1, n1, r1 = _rms(x1, g1)
        h2 = y1.astype(BF16)
        h2_ref[...] = h2
        f2 = jnp.zeros((tq, D), F32)
        for j in range(N_DEV):
            cols = slice(j * FF_BLK, (j + 1) * FF_BLK)
            a = jnp.maximum(_dot(h2, wu_ref[j]), 0.0)
            a32[:, cols] = a
            a16_ref[:, cols] = a.astype(BF16)
            f2 = f2 + _dot((a * a).astype(BF16), wd_ref[cols, :])
        y2, n2, r2 = _rms(f2, g2)
        e = x1 + y2 - t_ref[...]
        sq = jnp.sum(jnp.sum(e * e, axis=1, keepdims=True), axis=0, keepdims=True)
        loss_ref[...] += jnp.broadcast_to(sq * (0.5 / D), loss_ref.shape)
        dout = e * (1.0 / D)
        df2, dg2 = _rms_bwd(dout, n2, r2, g2)
        df2_16 = df2.astype(BF16)
        df2_ref[...] = df2_16
        dh2 = jnp.zeros((tq, D), F32)
        for j in range(N_DEV):
            cols = slice(j * FF_BLK, (j + 1) * FF_BLK)
            du = (_dot_nt(df2_16, wd_ref[cols, :]) * (2.0 * a32[:, cols])).astype(BF16)
            du_ref[:, cols] = du
            dh2 = dh2 + _dot_nt(du, wu_ref[j])
        dx, dg1 = _rms_bwd(dh2, n1, r1, g1)
        dx1_ref[...] = dout + dx
        dg_ref[0:1, :] += dg2
        dg_ref[1:2, :] += dg1

    def tile(w):
        return pl.BlockSpec((tq, w), lambda i: (i, 0))

    return pl.pallas_call(
        body, grid=(S // tq,), name="mlp",
        in_specs=[tile(D), tile(D), _const((1, D)), _const((1, D)), _const((N_DEV, D, FF_BLK)), _const((FF, D))],
        out_specs=[tile(FF), tile(FF), tile(D), tile(D), tile(D), _acc((SUBLANES, LANES)), _acc((SUBLANES, D))],
        out_shape=[jax.ShapeDtypeStruct((S, FF), BF16), jax.ShapeDtypeStruct((S, FF), BF16),
                   jax.ShapeDtypeStruct((S, D), BF16), jax.ShapeDtypeStruct((S, D), BF16),
                   jax.ShapeDtypeStruct((S, D), F32), jax.ShapeDtypeStruct((SUBLANES, LANES), F32),
                   jax.ShapeDtypeStruct((SUBLANES, D), F32)],
        scratch_shapes=[pltpu.VMEM((tq, FF), F32)],
        compiler_params=_cparams(56))(x1, tgt, g_pre, g_post, wup8, wdn16)


def _mix_out_bwd(dx1, y2, ypre, ltot, head_ones, bcu, qx16, kv16, cw8, g_post, g_attn, g_conv, g_x, wout16):
    def body(dx1_ref, y2_ref, ypre_ref, lt_ref, e_ref, bcu_ref, halo_ref, qx_ref, kv_ref, cw_ref, gp_ref, ga_ref,
             gc_ref, gx_ref, w_ref, dy2_ref, dya_ref, ld_ref, dbcu_ref, dqx_ref, dgs_ref, dcw_ref, dkv_ref, carry):
        i = pl.program_id(0)

        @pl.when(i == 0)
        def _():
            dgs_ref[...] = jnp.zeros_like(dgs_ref)
            dcw_ref[...] = jnp.zeros_like(dcw_ref)
            dkv_ref[...] = jnp.zeros_like(dkv_ref)
            carry[...] = jnp.zeros_like(carry)

        gp = gp_ref[...]
        _, n, r = _rms(y2_ref[...], gp)
        dy2, dgp = _rms_bwd(dx1_ref[...], n, r, gp)
        dy2_16 = dy2.astype(BF16)
        dy2_ref[...] = dy2_16
        dy = _dot_nt(dy2_16, w_ref[...])

        ypre = ypre_ref[...]
        ga, gc, gx = ga_ref[...], gc_ref[...], gx_ref[...]
        _, na, ra = _rms(ypre[:, 0:AW], ga)
        dya, dga = _rms_bwd(dy[:, 0:AW], na, ra, ga)
        _, nc, rc = _rms(ypre[:, AW:AW + CW], gc)
        dyc, dgc = _rms_bwd(dy[:, AW:AW + CW], nc, rc, gc)
        y_x = ypre[:, AW + CW:]
        _, nx, rx = _rms(y_x, gx)
        dyx, dgx = _rms_bwd(dy[:, AW + CW:], nx, rx, gx)
        dya_ref[...] = dya
        prod = dya * ypre[:, 0:AW]
        hi = prod.astype(BF16)
        lo = (prod - hi.astype(F32)).astype(BF16)
        head_sum = _dot(hi, e_ref[...]) + _dot(lo, e_ref[...])
        lane_a = lax.broadcasted_iota(jnp.int32, prod.shape, 1)
        ld_ref[...] = jnp.where((lane_a % HEAD) < HEAD // 2, lt_ref[...], head_sum)
        dgs_ref[0:1, :] += dgp
        dgs_ref[1:2, :] += jnp.concatenate([dga, dgc, dgx], axis=1)

        bcu = bcu_ref[...]
        b, c, u = bcu[:, 0:CW], bcu[:, CW:2 * CW], bcu[:, 2 * CW:]
        z = c * u
        halo = halo_ref[...]
        zprev = jnp.where(i < NT - 1, halo[:, CW:2 * CW] * halo[:, 2 * CW:], 0.0)
        row = lax.broadcasted_iota(jnp.int32, z.shape, 0)
        z1, z2 = _conv_taps(z, zprev, row)
        cw = cw_ref[...]
        conv = z2 * cw[0:1, :] + z1 * cw[1:2, :] + z * cw[2:3, :]
        dconv = dyc * b
        nxt = carry[...]
        dn1 = jnp.where(row == TQ - 1, nxt[0:1, :], pltpu.roll(dconv, TQ - 1, 0))
        dn2 = jnp.where(row == TQ - 1, nxt[1:2, :], jnp.where(row == TQ - 2, nxt[0:1, :], pltpu.roll(dconv, TQ - 2, 0)))
        carry[...] = dconv[0:SUBLANES, :]
        dz = dconv * cw[2:3, :] + dn1 * cw[1:2, :] + dn2 * cw[0:1, :]
        dbcu_ref[:, 0:CW] = dyc * conv
        dbcu_ref[:, CW:2 * CW] = dz * u
        dbcu_ref[:, 2 * CW:] = dz * c
        dcw_ref[0:1, :] += jnp.sum(z2 * dconv, axis=0, keepdims=True)
        dcw_ref[1:2, :] += jnp.sum(z1 * dconv, axis=0, keepdims=True)
        dcw_ref[2:3, :] += jnp.sum(z * dconv, axis=0, keepdims=True)

        qx = qx_ref[...]
        kv = kv_ref[...]
        km, vm = kv[:, 0:XW], kv[:, XW:]
        lane = lax.broadcasted_iota(jnp.int32, qx.shape, 1)
        dqx = jnp.zeros(qx.shape, F32)
        dkm = jnp.zeros((N_MEM, XW), F32)
        dvm = jnp.zeros((N_MEM, XW), F32)
        for h in range(XW // HEAD):
            hm = (lane >= h * HEAD) & (lane < (h + 1) * HEAD)
            qm = jnp.where(hm, qx, jnp.zeros_like(qx))
            e, l = _xattn_scores(qm, km)
            p = e / l
            dom = jnp.where(hm, dyx, 0.0)
            do16 = dom.astype(BF16)
            dsum = jnp.sum(dom * y_x, axis=1, keepdims=True)
            ds = (p * (_dot_nt(do16, vm) - dsum)).astype(BF16)
            dqx = jnp.where(hm, _dot(ds, km), dqx)
            dkm = dkm + _dot_tn(ds, qm)
            dvm = dvm + _dot_tn(p.astype(BF16), do16)
        dqx_ref[...] = dqx * SCALE
        dkv_ref[:, 0:XW] += dkm
        dkv_ref[:, XW:] += dvm

    def tile(w):
        return pl.BlockSpec((TQ, w), lambda i: (NT - 1 - i, 0))

    halo = pl.BlockSpec((SUBLANES, 3 * CW), lambda i: (jnp.maximum((NT - 1 - i) * (TQ // SUBLANES) - 1, 0), 0))
    return pl.pallas_call(
        body, grid=(NT,), name="mix_out_bwd",
        in_specs=[tile(D), tile(D), tile(D), tile(AW), _const((AW, AW)), tile(3 * CW), halo, tile(XW),
                  _const((N_MEM, 2 * XW)), _const((SUBLANES, CW)), _const((1, D)), _const((1, AW)), _const((1, CW)),
                  _const((1, XW)), _const((D, D))],
        out_specs=[tile(D), tile(AW), tile(AW), tile(3 * CW), tile(XW), _acc((SUBLANES, D)), _acc((SUBLANES, CW)),
                   _acc((N_MEM, 2 * XW))],
        out_shape=[jax.ShapeDtypeStruct((S, D), BF16), jax.ShapeDtypeStruct((S, AW), F32),
                   jax.ShapeDtypeStruct((S, AW), F32),
                   jax.ShapeDtypeStruct((S, 3 * CW), F32), jax.ShapeDtypeStruct((S, XW), F32),
                   jax.ShapeDtypeStruct((SUBLANES, D), F32), jax.ShapeDtypeStruct((SUBLANES, CW), F32),
                   jax.ShapeDtypeStruct((N_MEM, 2 * XW), F32)],
        scratch_shapes=[pltpu.VMEM((SUBLANES, CW), F32)],
        compiler_params=_cparams(56))(dx1, y2, ypre, ltot, head_ones, bcu, bcu, qx16, kv16, cw8, g_post, g_attn,
                                      g_conv, g_x, wout16)


def _attn_bwd(q, k, v, dya, ld, chip_sums=()):
    n_in = 5
    views = [[a] + [a.reshape(S // n, n, AW) for _, n, _, _ in ATTN_PLANS[1:]] for a in (q, k, v, dya, ld)]
    flat = [views[a][p] for p in range(3) for a in range(n_in)]
    ns = len(chip_sums)
    n_grid = AW // LANES

    def body(*refs):
        hbm = [refs[n_in * p:n_in * p + n_in] for p in range(3)]
        refs = refs[3 * n_in:]
        sum_refs, refs = refs[:ns], refs[ns:]
        outs = refs[0:3]
        landed_refs, sc = refs[3:3 + ns], refs[3 + ns:]
        buf_a, buf_b, res, acc = sc[0:5], sc[5:10], sc[10:13], sc[13:16]
        land8, land16 = sc[16:19], sc[19:22]
        tab128, tab4, sem_in, sem_out = sc[22:26]
        if ns:
            start_chips, finish_chips = _chips_steps(sum_refs, landed_refs, *sc[26:])
            pl.when(pl.program_id(0) == 0)(start_chips)
        lanes = pl.ds(pl.multiple_of(pl.program_id(0) * LANES, LANES), LANES)
        w_p1 = _gather_classes(hbm[0], buf_a, sem_in.at[0], lanes)
        w_p4 = _gather_classes(hbm[1], buf_b, sem_in.at[1], lanes)

        @pl.when(pl.program_id(0) == 0)
        def _():
            for b in buf_a + buf_b:
                b[0:PAD, :] = jnp.zeros((PAD, LANES), F32)
            _fill_bias(tab128, 128, False)
            _fill_bias(tab4, 64, True)

        for b in res + acc:
            b[...] = jnp.zeros_like(b)
        lane = lax.broadcasted_iota(jnp.int32, (1, LANES), 1)

        def run(plan, bufs, tab, dst):
            _, n_cls, qblk, nbc = plan
            partner = n_cls == 8
            bq, bk, bv, bdo, bld = bufs
            rq, rk, rv = dst

            def block(g, carry):
                own, wins, mask = _block_rows(g, qblk, nbc, partner)
                q2 = _stack_heads(bq[own, :].astype(BF16), lane)
                do2 = _stack_heads(bdo[own, :].astype(BF16), lane)
                kw, vw = _window(bk, wins), _window(bv, wins)
                ldv = bld[own, :]
                half = HEAD // 2
                lt2 = jnp.concatenate([ldv[:, 0:1], ldv[:, HEAD:HEAD + 1]], axis=0)
                dsum2 = jnp.concatenate([ldv[:, half:half + 1], ldv[:, HEAD + half:HEAD + half + 1]], axis=0)
                p = jnp.exp(_dot_nt(q2, kw) + tab[mask] - lt2)
                ds = (p * (_dot_nt(do2, vw) - dsum2)).astype(BF16)
                rq[own, :] = _unstack_heads(_dot(ds, kw), lane)
                dkw = _dot_tn(ds, q2)
                dvw = _dot_tn(p.astype(BF16), do2)
                n_w = WIN // len(wins)
                for i, w in enumerate(wins):
                    rk[w, :] += dkw[i * n_w:(i + 1) * n_w, :]
                    rv[w, :] += dvw[i * n_w:(i + 1) * n_w, :]
                return carry
            lax.fori_loop(0, n_cls * nbc, block, 0, unroll=ATTN_UNROLL)

        def add_landed(lands, per_tile):
            for a, land in zip(acc, lands):
                for t in range(S // TQ):
                    a[pl.ds(PAD + t * TQ, TQ), :] += land[pl.ds(t * per_tile, per_tile), :, :].reshape(TQ, LANES)

        for w in w_p1:
            w.wait()
        run(ATTN_PLANS[0], buf_a, tab128, acc)
        w_p16 = _gather_classes(hbm[2], buf_a, sem_in.at[2], lanes)
        for w in w_p4:
            w.wait()
        run(ATTN_PLANS[1], buf_b, tab4, res)
        for w in _scatter_classes(res, land8, sem_out.at[0]):
            w.wait()
        add_landed(land8, TQ // 8)
        for b in res:
            b[...] = jnp.zeros_like(b)
        for w in w_p16:
            w.wait()
        run(ATTN_PLANS[2], buf_a, tab128, res)
        for w in _scatter_classes(res, land16, sem_out.at[1]):
            w.wait()
        add_landed(land16, TQ // 16)
        done = [pltpu.make_async_copy(a.at[pl.ds(PAD, S), :], o.at[:, lanes], sem_out.at[0, i])
                for i, (a, o) in enumerate(zip(acc, outs))]
        for cp in done:
            cp.start()
        for cp in done:
            cp.wait()
        if ns:
            pl.when(pl.program_id(0) == n_grid - 1)(finish_chips)

    padded = pltpu.VMEM((PAD + S, LANES), F32)
    return pl.pallas_call(
        body, grid=(n_grid,), name="attn_bwd",
        in_specs=[ANY] * (3 * n_in + ns), out_specs=[ANY] * (3 + ns),
        out_shape=[jax.ShapeDtypeStruct((S, AW), F32)] * 3 + _chips_shapes(chip_sums),
        scratch_shapes=[padded] * 16 + [pltpu.VMEM((S // 8, 8, LANES), F32)] * 3
        + [pltpu.VMEM((S // 16, 16, LANES), F32)] * 3
        + [pltpu.VMEM((4, 256, WIN), F32), pltpu.VMEM((4, 128, WIN), F32),
           pltpu.SemaphoreType.DMA((3, n_in)), pltpu.SemaphoreType.DMA((2, 3))]
        + (_chips_scratch(ns) if ns else []),
        compiler_params=_cparams(60))(*flat, *chip_sums)


def _in_proj_bwd(dq, dk, dv, dbcu, dqx, cos, sins, w16, x, g, dx1):
    def body(dq_ref, dk_ref, dv_ref, dbcu_ref, dqx_ref, c_ref, s_ref, w_ref, x_ref, g_ref, dx1_ref,
             dp_ref, gx_ref, dg_ref):
        @pl.when(pl.program_id(0) == 0)
        def _():
            dg_ref[...] = jnp.zeros_like(dg_ref)

        cos, sn = c_ref[...], s_ref[...]
        dqr = dq_ref[...] * SCALE
        dkr = dk_ref[...]
        dp = jnp.concatenate([dqr * cos + _rot_half(dqr * sn), dkr * cos + _rot_half(dkr * sn), dv_ref[...],
                              dbcu_ref[...], dqx_ref[...]], axis=1).astype(BF16)
        dp_ref[...] = dp
        dh = _dot_nt(dp, w_ref[...])
        g = g_ref[...]
        _, n, r = _rms(x_ref[...], g)
        dx, dg = _rms_bwd(dh, n, r, g)
        gx_ref[...] = dx1_ref[...] + dx
        dg_ref[0:1, :] += dg

    def tile(w):
        return pl.BlockSpec((TQ, w), lambda i: (i, 0))

    return pl.pallas_call(
        body, grid=(NT,), name="in_proj_bwd",
        in_specs=[tile(AW), tile(AW), tile(AW), tile(3 * CW), tile(XW), tile(AW), tile(AW), _const((D, PW)),
                  tile(D), _const((1, D)), tile(D)],
        out_specs=[tile(PW), tile(D), _acc((SUBLANES, D))],
        out_shape=[jax.ShapeDtypeStruct((S, PW), BF16), jax.ShapeDtypeStruct((S, D), F32),
                   jax.ShapeDtypeStruct((SUBLANES, D), F32)],
        compiler_params=_cparams(56))(dq, dk, dv, dbcu, dqx, cos, sins, w16, x, g, dx1)


def _mem_bwd(mem, g_mem, wkv16, dkv):
    def body(m_ref, g_ref, w_ref, dkv_ref, dkv16_ref, dg_ref):
        dkv16 = dkv_ref[...].astype(BF16)
        dkv16_ref[...] = dkv16
        _, n, _ = _rms(m_ref[...], g_ref[...])
        dg = jnp.sum(_dot_nt(dkv16, w_ref[...]) * n, axis=0, keepdims=True)
        dg_ref[...] = jnp.broadcast_to(dg, dg_ref.shape)

    return pl.pallas_call(
        body, name="mem_bwd",
        out_shape=[jax.ShapeDtypeStruct((N_MEM, 2 * XW), BF16), jax.ShapeDtypeStruct((SUBLANES, D), F32)],
        compiler_params=pltpu.CompilerParams(vmem_limit_bytes=32 << 20))(mem, g_mem, wkv16, dkv)


def _wgrad(a16, b16, tn, name, square_b=False, transpose_out=False):
    kk, m = a16.shape
    n_tiles = b16.shape[1] // tn
    chunk = min(kk, 512)
    oshape = (tn, m) if transpose_out else (m, tn)

    def body(a_ref, b_ref, o32_ref, o16_ref, at):
        @pl.when(pl.program_id(0) == 0)
        def _():
            for c in range(kk // chunk):
                at[:, c * chunk:(c + 1) * chunk] = a_ref[c * chunk:(c + 1) * chunk, :].T

        b = b_ref[...]
        if square_b:
            b = b * b
        acc = _dot(at[...], b)
        if transpose_out:
            acc = acc.T
        o32_ref[0] = acc
        o16_ref[0] = acc.astype(BF16)

    oblk = pl.BlockSpec((1,) + oshape, lambda j: (j, 0, 0))
    return pl.pallas_call(
        body, grid=(n_tiles,), name=name,
        in_specs=[_const((kk, m)), pl.BlockSpec((kk, tn), lambda j: (0, j))],
        out_specs=[oblk, oblk],
        out_shape=[jax.ShapeDtypeStruct((n_tiles,) + oshape, F32), jax.ShapeDtypeStruct((n_tiles,) + oshape, BF16)],
        scratch_shapes=[pltpu.VMEM((m, kk), BF16)],
        compiler_params=_cparams(56))(a16, b16)


def _adamw_math(w, g, m, v):
    m = ADAM_B1 * m + (1.0 - ADAM_B1) * g
    v = ADAM_B2 * v + (1.0 - ADAM_B2) * jnp.square(g)
    m_hat = m / (1.0 - ADAM_B1 ** ADAM_STEP)
    v_hat = v / (1.0 - ADAM_B2 ** ADAM_STEP)
    delta = -ADAM_LR * (m_hat / (jnp.sqrt(v_hat) + ADAM_EPS) + ADAM_WD * w)
    return delta, m, v


def _adamw_shard(own32, rb16, w, m, v, name):
    def body(o_ref, r_ref, w_ref, m_ref, v_ref, g_out, d_out, m_out, v_out):
        g = o_ref[...] + r_ref[0].astype(F32) + r_ref[1].astype(F32) + r_ref[2].astype(F32)
        g_out[...] = g
        d_out[...], m_out[...], v_out[...] = _adamw_math(w_ref[...], g, m_ref[...], v_ref[...])

    return pl.pallas_call(
        body, name=name, out_shape=[jax.ShapeDtypeStruct(w.shape, F32)] * 4,
        compiler_params=pltpu.CompilerParams(vmem_limit_bytes=48 << 20))(own32, rb16, w, m, v)


def _place():
    x, y, c = lax.axis_index("x"), lax.axis_index("y"), lax.axis_index("c")
    chips = [(1 - x, y), (x, 1 - y), (1 - x, 1 - y)]
    return x, y, c, chips


def _gather_steps(ins, outs, send, recv, lsem):
    nt = len(ins)
    x, y, c, chips = _place()
    me, sib = (x, y, c), (x, y, 1 - c)

    def slot(t, px, py, pc):
        return outs[t].at[4 * px + 2 * py + pc]

    def copy(t, k, block, to, src=None):
        return pltpu.make_async_remote_copy(
            src_ref=slot(t, *block) if src is None else src, dst_ref=slot(t, *block),
            send_sem=send.at[t, k], recv_sem=recv.at[t, k], device_id=to, device_id_type=MESH)

    mine = [pltpu.make_async_copy(ins[t], slot(t, *me), lsem.at[t]) for t in range(nt)]
    first = []
    for t in range(nt):
        first.append(copy(t, 0, me, sib, src=ins[t]))
        first += [copy(t, 1 + j, me, (*chip, c), src=ins[t]) for j, chip in enumerate(chips)]

    def start():
        for cp in mine + first:
            cp.start()

    def finish():
        passed = []
        for j, chip in enumerate(chips):
            for t in range(nt):
                copy(t, 1 + j, (*chip, c), me).wait_recv()
                fwd = copy(t, 4 + j, (*chip, c), sib)
                fwd.start()
                passed.append(fwd)
        for t in range(nt):
            copy(t, 0, sib, me).wait_recv()
            for j, chip in enumerate(chips):
                copy(t, 4 + j, (*chip, 1 - c), me).wait_recv()
        for cp in first + passed:
            cp.wait_send()
        for cp in mine:
            cp.wait()

    return start, finish


def _gather_scratch(nt):
    return [pltpu.SemaphoreType.DMA((nt, 7)), pltpu.SemaphoreType.DMA((nt, 7)), pltpu.SemaphoreType.DMA((nt,))]


def _gathered_shapes(shards):
    return [jax.ShapeDtypeStruct((N_DEV,) + s.shape, s.dtype) for s in shards]


def _all_gather(shards):
    nt = len(shards)

    def body(*refs):
        start, finish = _gather_steps(refs[:nt], refs[nt:2 * nt], *refs[2 * nt:])
        start()
        finish()

    return pl.pallas_call(
        body, name="all_gather_weights", in_specs=[ANY] * nt, out_specs=[ANY] * nt,
        out_shape=_gathered_shapes(shards), scratch_shapes=_gather_scratch(nt))(*shards)


def _rs_pair(g16s, name):
    nt = len(g16s)

    def body(*refs):
        ins, outs = refs[:nt], refs[nt:2 * nt]
        send, recv = refs[2 * nt:]
        x, y, c, _ = _place()
        copies = [pltpu.make_async_remote_copy(
            src_ref=ins[t].at[2 * p + (1 - c)], dst_ref=outs[t].at[p], send_sem=send.at[t, p], recv_sem=recv.at[t, p],
            device_id=(x, y, 1 - c), device_id_type=MESH) for t in range(nt) for p in range(4)]
        for cp in copies:
            cp.start()
        for cp in copies:
            cp.wait()

    return pl.pallas_call(
        body, name=name,
        in_specs=[ANY] * nt, out_specs=[ANY] * nt,
        out_shape=[jax.ShapeDtypeStruct((4,) + g.shape[1:], g.dtype) for g in g16s],
        scratch_shapes=[pltpu.SemaphoreType.DMA((nt, 4)), pltpu.SemaphoreType.DMA((nt, 4))])(*g16s)


def _rs_pair_add(place, g32, ra16, name):
    shp = g32.shape[1:]

    def body(pl_ref, g_ref, r_ref, cs_ref, own_ref):
        s = g_ref[0] + r_ref[0].astype(F32)
        cs_ref[0] = s.astype(BF16)

        @pl.when(pl.program_id(0) == pl_ref[1])
        def _():
            own_ref[...] = s

    blk = (1,) + shp
    return pl.pallas_call(
        body, name=name,
        grid_spec=pltpu.PrefetchScalarGridSpec(
            num_scalar_prefetch=1, grid=(4,),
            in_specs=[pl.BlockSpec(blk, lambda p, s: (2 * p + s[0], 0, 0)), pl.BlockSpec(blk, lambda p, s: (p, 0, 0))],
            out_specs=[pl.BlockSpec(blk, lambda p, s: (p, 0, 0)), pl.BlockSpec(shp, lambda p, s: (0, 0))]),
        out_shape=[jax.ShapeDtypeStruct((4,) + shp, BF16), jax.ShapeDtypeStruct(shp, F32)],
        compiler_params=_cparams(48))(place, g32, ra16)


def _chips_steps(ins, outs, send, recv):
    _, _, c, chips = _place()
    copies = [pltpu.make_async_remote_copy(
        src_ref=ins[t].at[2 * px + py], dst_ref=outs[t].at[j], send_sem=send.at[t, j], recv_sem=recv.at[t, j],
        device_id=(px, py, c), device_id_type=MESH) for t in range(len(ins)) for j, (px, py) in enumerate(chips)]

    def start():
        for cp in copies:
            cp.start()

    def finish():
        for cp in copies:
            cp.wait()

    return start, finish


def _chips_scratch(nt):
    return [pltpu.SemaphoreType.DMA((nt, 3)), pltpu.SemaphoreType.DMA((nt, 3))]


def _chips_shapes(cs16s):
    return [jax.ShapeDtypeStruct((3,) + g.shape[1:], g.dtype) for g in cs16s]


def _rs_chips(cs16s):
    nt = len(cs16s)

    def body(*refs):
        start, finish = _chips_steps(refs[:nt], refs[nt:2 * nt], *refs[2 * nt:])
        start()
        finish()

    return pl.pallas_call(
        body, name="reduce_scatter_chips", in_specs=[ANY] * nt, out_specs=[ANY] * nt,
        out_shape=_chips_shapes(cs16s), scratch_shapes=_chips_scratch(nt))(*cs16s)


SMALL = (("g_pre_mix", 0, 0, D), ("g_mem", 1, 0, D), ("g_post_mix", 2, 0, D), ("g_attn_out", 3, 0, AW),
         ("g_conv_out", 3, AW, CW), ("g_xattn_out", 3, AW + CW, XW), ("g_post_mlp", 4, 0, D), ("g_pre_mlp", 5, 0, D))
CONV_ROW = 8
PACK_ROWS = 16


def _small_params_step(dg_in, dg_mem, dgs, dg_mlp, dcw, params):
    flat = [a for n, _, _, _ in SMALL for a in params[n]] + list(params["conv_w"])
    n_par = len(SMALL) + 1
    tap_cols = CW // N_DEV

    def body(*refs):
        acc_in, acc_mem, acc_mix, acc_mlp, acc_cw = refs[0:5]
        ins = refs[5:5 + 3 * n_par]
        outs = refs[5 + 3 * n_par:5 + 7 * n_par]
        pack, land, send, recv = refs[5 + 7 * n_par:]
        x, y, c, _ = _place()
        me = 4 * x + 2 * y + c
        pack[...] = jnp.zeros_like(pack)
        pack[0:1, :] = acc_in[0:1, :]
        pack[1:2, :] = acc_mem[0:1, :]
        pack[2:4, :] = acc_mix[0:2, :]
        pack[4:6, :] = acc_mlp[0:2, :]
        pack[CONV_ROW:CONV_ROW + 3, 0:CW] = acc_cw[0:3, :]
        land[me] = pack[...]
        copies = []
        for k in range(1, N_DEV):
            kx, ky, kc = (k >> 2) & 1, (k >> 1) & 1, k & 1
            peer = (1 - x if kx else x, 1 - y if ky else y, 1 - c if kc else c)
            copies.append(pltpu.make_async_remote_copy(
                src_ref=pack, dst_ref=land.at[me], send_sem=send.at[k - 1], recv_sem=recv.at[k - 1],
                device_id=peer, device_id_type=MESH))
        for cp in copies:
            cp.start()
        for cp in copies:
            cp.wait()
        tot = land[0]
        for s in range(1, N_DEV):
            tot = tot + land[s]

        def update(i, g):
            w_ref, m_ref, v_ref = ins[3 * i:3 * i + 3]
            g_out, d_out, m_out, v_out = outs[4 * i:4 * i + 4]
            g_out[...] = g
            d_out[...], m_out[...], v_out[...] = _adamw_math(w_ref[...], g, m_ref[...], v_ref[...])

        for i, (_, row, lane0, width) in enumerate(SMALL):
            update(i, tot[row:row + 1, lane0:lane0 + width])
        taps = pltpu.roll(tot[CONV_ROW:CONV_ROW + SUBLANES, 0:CW], jnp.where(me == 0, 0, CW - me * tap_cols), 1)
        update(n_par - 1, taps[0:3, 0:tap_cols])

    shapes = [jax.ShapeDtypeStruct(params[n][0].shape, F32) for n, _, _, _ in SMALL] + [
        jax.ShapeDtypeStruct(params["conv_w"][0].shape, F32)]
    out = pl.pallas_call(
        body, name="small_params_step", out_shape=[s for s in shapes for _ in range(4)],
        scratch_shapes=[pltpu.VMEM((PACK_ROWS, D), F32), pltpu.VMEM((N_DEV, PACK_ROWS, D), F32),
                        pltpu.SemaphoreType.DMA((N_DEV - 1,)), pltpu.SemaphoreType.DMA((N_DEV - 1,))],
    )(dg_in, dg_mem, dgs, dg_mlp, dcw, *flat)
    names = [n for n, _, _, _ in SMALL] + ["conv_w"]
    return {n: out[4 * i:4 * i + 4] for i, n in enumerate(names)}


def _reduce_to_chip_sums(place, grads):
    from_sib = _rs_pair([g16 for _, g16 in grads.values()], "reduce_scatter_pair_" + "_".join(grads))
    return {n: _rs_pair_add(place, g32, from_sib[t], "pair_add_" + n) for t, (n, (g32, _)) in enumerate(grads.items())}


def _local_step(x, mem, pos, gains, cw_full, win16, wkv16, late_shards, tgt, place):
    half = HEAD // 2
    inv_freq = jnp.float32(ROPE_THETA) ** (-(jnp.arange(half, dtype=F32) * 2.0 / HEAD))
    invf = jnp.tile(inv_freq, LANES // half)[None, :]
    sgn = jnp.tile(jnp.concatenate([-jnp.ones((half,), F32), jnp.ones((half,), F32)]), LANES // HEAD)[None, :]
    cos, sins = _rope_table(pos.astype(F32).reshape(S, 1), invf, sgn)
    cw8 = jnp.zeros((SUBLANES, CW), F32).at[0:3].set(cw_full)

    memn16, kv16 = _mem_fwd(mem, gains["g_mem"], wkv16)
    q, k, v, bcu, qx16, h16 = _in_proj(x, gains["g_pre_mix"], win16, cos, sins)
    y_attn, ltot, wout8, wup8, wdn8 = _attn_fwd(q, k, v, late_shards)
    wout16, wdn16 = wout8.reshape(D, D), wdn8.reshape(FF, D)
    ypre, y16, y2, x1 = _mix_out(y_attn, bcu, qx16, kv16, cw8, gains["g_attn_out"], gains["g_conv_out"],
                                 gains["g_xattn_out"], gains["g_post_mix"], wout16, x)
    a16, du16, h2_16, df2_16, dx1, loss8, dg_mlp = _mlp(x1, tgt, gains["g_pre_mlp"], gains["g_post_mlp"], wup8, wdn16)

    mlp_sums = _reduce_to_chip_sums(place, {
        "w_up": _wgrad(h2_16, du16, FF_BLK, "wgrad_up"),
        "w_down": _wgrad(df2_16, a16, FF_BLK, "wgrad_down", square_b=True, transpose_out=True)})

    head_id = jnp.arange(AW, dtype=jnp.int32) // HEAD
    head_ones = (head_id[:, None] == head_id[None, :]).astype(BF16)
    dy2_16, dya, ld, dbcu, dqx, dgs, dcw, dkv = _mix_out_bwd(
        dx1, y2, ypre, ltot, head_ones, bcu, qx16, kv16, cw8, gains["g_post_mix"], gains["g_attn_out"],
        gains["g_conv_out"], gains["g_xattn_out"], wout16)
    dq, dk, dv, up_chips, dn_chips = _attn_bwd(q, k, v, dya, ld, [mlp_sums["w_up"][0], mlp_sums["w_down"][0]])
    dproj16, grad_x, dg_in = _in_proj_bwd(dq, dk, dv, dbcu, dqx, cos, sins, win16, x, gains["g_pre_mix"], dx1)
    dkv16, dg_mem = _mem_bwd(mem, gains["g_mem"], wkv16, dkv)

    def by_owner_in(g):
        return g.transpose(1, 0, 2).reshape(D, N_DEV, PW // N_DEV).transpose(1, 0, 2)

    mix_sums = _reduce_to_chip_sums(place, {
        "w_in": tuple(by_owner_in(g) for g in _wgrad(h16, dproj16, 512, "wgrad_in")),
        "w_mem_kv": tuple(g.reshape(N_DEV, D // N_DEV, 2 * XW) for g in _wgrad(memn16, dkv16, 2 * XW, "wgrad_mem_kv")),
        "w_out": tuple(g.reshape(N_DEV, D // N_DEV, D) for g in _wgrad(y16, dy2_16, D, "wgrad_out"))})
    mix_chips = _rs_chips([s[0] for s in mix_sums.values()])
    reduced = {n: (s[1], mix_chips[t]) for t, (n, s) in enumerate(mix_sums.items())}
    reduced["w_up"] = (mlp_sums["w_up"][1], up_chips)
    reduced["w_down"] = (mlp_sums["w_down"][1], dn_chips)
    return loss8[0, 0], grad_x, reduced, (dg_in, dg_mem, dgs, dg_mlp, dcw)


BIG = ("w_in", "w_mem_kv", "w_out", "w_up", "w_down")
ORDER = ("g_pre_mix", "g_mem", "w_in", "w_mem_kv", "conv_w", "g_attn_out", "g_conv_out", "g_xattn_out", "w_out",
         "g_post_mix", "g_pre_mlp", "w_up", "w_down", "g_post_mlp")


def kernel(x, mem, positions, g_pre_mix, g_mem, w_in, w_mem_kv, conv_w, g_attn_out, g_conv_out, g_xattn_out, w_out, g_post_mix, g_pre_mlp, w_up, w_down, g_post_mlp, loss_target, m_g_pre_mix, m_g_mem, m_w_in, m_w_mem_kv, m_conv_w, m_g_attn_out, m_g_conv_out, m_g_xattn_out, m_w_out, m_g_post_mix, m_g_pre_mlp, m_w_up, m_w_down, m_g_post_mlp, v_g_pre_mix, v_g_mem, v_w_in, v_w_mem_kv, v_conv_w, v_g_attn_out, v_g_conv_out, v_g_xattn_out, v_w_out, v_g_post_mix, v_g_pre_mlp, v_w_up, v_w_down, v_g_post_mlp):
    w = dict(g_pre_mix=g_pre_mix, g_mem=g_mem, w_in=w_in, w_mem_kv=w_mem_kv, conv_w=conv_w, g_attn_out=g_attn_out,
             g_conv_out=g_conv_out, g_xattn_out=g_xattn_out, w_out=w_out, g_post_mix=g_post_mix, g_pre_mlp=g_pre_mlp,
             w_up=w_up, w_down=w_down, g_post_mlp=g_post_mlp)
    mo = dict(g_pre_mix=m_g_pre_mix, g_mem=m_g_mem, w_in=m_w_in, w_mem_kv=m_w_mem_kv, conv_w=m_conv_w,
              g_attn_out=m_g_attn_out, g_conv_out=m_g_conv_out, g_xattn_out=m_g_xattn_out, w_out=m_w_out,
              g_post_mix=m_g_post_mix, g_pre_mlp=m_g_pre_mlp, w_up=m_w_up, w_down=m_w_down, g_post_mlp=m_g_post_mlp)
    vo = dict(g_pre_mix=v_g_pre_mix, g_mem=v_g_mem, w_in=v_w_in, w_mem_kv=v_w_mem_kv, conv_w=v_conv_w,
              g_attn_out=v_g_attn_out, g_conv_out=v_g_conv_out, g_xattn_out=v_g_xattn_out, w_out=v_w_out,
              g_post_mix=v_g_post_mix, g_pre_mlp=v_g_pre_mlp, w_up=v_w_up, w_down=v_w_down, g_post_mlp=v_g_post_mlp)

    xi, yi, ci = lax.axis_index("x"), lax.axis_index("y"), lax.axis_index("c")
    me = 4 * xi + 2 * yi + ci
    place = jnp.stack([ci, 2 * xi + yi]).astype(jnp.int32)

    conv_tile = jnp.zeros((SUBLANES, LANES), F32).at[0:3, 0:CW // N_DEV].set(conv_w[0])
    win8, wkv8, conv8 = _all_gather([w["w_in"][0].astype(BF16), w["w_mem_kv"][0].astype(BF16), conv_tile])
    win16 = win8.transpose(1, 0, 2).reshape(D, PW)
    wkv16 = wkv8.reshape(D, 2 * XW)
    cw_full = conv8[:, 0:3, 0:CW // N_DEV].transpose(1, 0, 2).reshape(3, CW)
    late_shards = [w[n][0].astype(BF16) for n in ("w_out", "w_up", "w_down")]

    gains = {n: w[n] for n, _, _, _ in SMALL}
    loss, grad_x, reduced, small_acc = _local_step(
        x[0], mem[0], positions[0], gains, cw_full, win16, wkv16, late_shards, loss_target[0], place)
    loss = lax.psum(loss, ("x", "y", "c"))

    grad, delta, new_m, new_v = {}, {}, {}, {}
    for n in BIG:
        g, d_, m_, v_ = _adamw_shard(*reduced[n], w[n][0], mo[n][0], vo[n][0], "adamw_" + n)
        grad[n], delta[n], new_m[n], new_v[n] = g[None], d_[None], m_[None], v_[None]

    params = {n: (w[n], mo[n], vo[n]) for n, _, _, _ in SMALL}
    params["conv_w"] = (w["conv_w"][0], mo["conv_w"][0], vo["conv_w"][0])
    for n, (g, d_, m_, v_) in _small_params_step(*small_acc, params).items():
        lead = (lambda a: a[None]) if n == "conv_w" else (lambda a: a)
        grad[n], delta[n], new_m[n], new_v[n] = lead(g), lead(d_), lead(m_), lead(v_)

    return (loss, grad_x[None], *[grad[n] for n in ORDER], *[delta[n] for n in ORDER],
            *[new_m[n] for n in ORDER], *[new_v[n] for n in ORDER])
```

```python
import functools

import numpy as np
import jax
import jax.numpy as jnp
from jax import lax
from jax.experimental import pallas as pl
from jax.experimental.pallas import tpu as pltpu

F32, BF16 = jnp.float32, jnp.bfloat16
MESH = pl.DeviceIdType.MESH
ANY = pl.BlockSpec(memory_space=pl.ANY)

N_DEV = 8
D = 1024
S = 4096
N_MEM = 256
HEAD = 64
AW, CW, XW = 512, 256, 256
PW = 3 * AW + 3 * CW + XW
FF = 4096
FF_BLK = FF // N_DEV
PATTERNS = ((128, 1), (512, 4), (2048, 16))
QB = 128
EPS = 1e-6
NEG = -1e30
SCALE = HEAD ** -0.5
ROPE_THETA = 10000.0
LANES = 128
SUBLANES = 8

ADAM_LR, ADAM_B1, ADAM_B2, ADAM_EPS, ADAM_WD, ADAM_STEP = 0.001, 0.9, 0.999, 1e-08, 0.01, 10

TQ = 512
TQ_MLP = 256
NT = S // TQ


def _cparams(vmem_mb, n_grid=1):
    return pltpu.CompilerParams(dimension_semantics=("arbitrary",) * n_grid, vmem_limit_bytes=vmem_mb << 20)


def _const(shape):
    nd = len(shape)
    return pl.BlockSpec(shape, lambda *_: (0,) * nd, pipeline_mode=pl.Buffered(1))


def _acc(shape):
    nd = len(shape)
    return pl.BlockSpec(shape, lambda *_: (0,) * nd)


def _dot(a, b):
    return jnp.dot(a, b, preferred_element_type=F32)


def _dot_nt(a, b):
    return lax.dot_general(a, b, (((1,), (1,)), ((), ())), preferred_element_type=F32)


def _dot_tn(a, b):
    return lax.dot_general(a, b, (((0,), (0,)), ((), ())), preferred_element_type=F32)


def _rms(x, g):
    r = lax.rsqrt(jnp.mean(x * x, axis=-1, keepdims=True) + EPS)
    n = x * r
    return n * g, n, r


def _rms_bwd(dy, n, r, g):
    dn = dy * g
    dx = r * (dn - n * jnp.mean(dn * n, axis=-1, keepdims=True))
    return dx, jnp.sum(dy * n, axis=0, keepdims=True)


def _rot_half(t):
    lane = lax.broadcasted_iota(jnp.int32, t.shape, 1)
    n = t.shape[1]
    return jnp.where((lane % HEAD) < HEAD // 2, pltpu.roll(t, n - HEAD // 2, 1), pltpu.roll(t, HEAD // 2, 1))


def _rope_table(pos_col, invf, sgn):
    def body(p_ref, f_ref, s_ref, c_out, s_out):
        ang = p_ref[...] * f_ref[...]
        c_out[...] = jnp.tile(jnp.cos(ang), (1, AW // LANES))
        s_out[...] = jnp.tile(jnp.sin(ang) * s_ref[...], (1, AW // LANES))

    tile = pl.BlockSpec((TQ, AW), lambda i: (i, 0))
    return pl.pallas_call(
        body, grid=(NT,), name="rope_table",
        in_specs=[pl.BlockSpec((TQ, 1), lambda i: (i, 0)), _const((1, LANES)), _const((1, LANES))],
        out_specs=[tile, tile], out_shape=[jax.ShapeDtypeStruct((S, AW), F32)] * 2,
        compiler_params=_cparams(32))(pos_col, invf, sgn)


def _mem_fwd(mem, g_mem, wkv16):
    def body(m_ref, g_ref, w_ref, n16_ref, kv_ref):
        y, _, _ = _rms(m_ref[...], g_ref[...])
        y16 = y.astype(BF16)
        n16_ref[...] = y16
        kv_ref[...] = _dot(y16, w_ref[...]).astype(BF16)

    return pl.pallas_call(
        body, name="mem_fwd",
        out_shape=[jax.ShapeDtypeStruct((N_MEM, D), BF16), jax.ShapeDtypeStruct((N_MEM, 2 * XW), BF16)],
        compiler_params=pltpu.CompilerParams(vmem_limit_bytes=32 << 20))(mem, g_mem, wkv16)


def _in_proj(x, g, w16, cos, sins):
    def body(x_ref, g_ref, w_ref, c_ref, s_ref, q_ref, kv_ref, bcu_ref, qx_ref, h_ref):
        y, _, _ = _rms(x_ref[...], g_ref[...])
        h = y.astype(BF16)
        h_ref[...] = h
        proj = _dot(h, w_ref[...])
        cos, sn = c_ref[...], s_ref[...]
        q, k = proj[:, 0:AW], proj[:, AW:2 * AW]
        q_ref[...] = (q * cos + _rot_half(q) * sn) * SCALE
        kv_ref[...] = _pack_pair(k * cos + _rot_half(k) * sn, proj[:, 2 * AW:3 * AW])
        bcu_ref[...] = proj[:, 3 * AW:3 * AW + 3 * CW]
        qx_ref[...] = (proj[:, 3 * AW + 3 * CW:] * SCALE).astype(BF16)

    def tile(w):
        return pl.BlockSpec((TQ, w), lambda i: (i, 0))

    return pl.pallas_call(
        body, grid=(NT,), name="in_proj",
        in_specs=[tile(D), _const((1, D)), _const((D, PW)), tile(AW), tile(AW)],
        out_specs=[tile(AW), tile(AW), tile(3 * CW), tile(XW), tile(D)],
        out_shape=[jax.ShapeDtypeStruct((S, AW), F32)] * 2 + [
            jax.ShapeDtypeStruct((S, 3 * CW), F32), jax.ShapeDtypeStruct((S, XW), BF16),
            jax.ShapeDtypeStruct((S, D), BF16)],
        compiler_params=_cparams(56))(x, g, w16, cos, sins)


ATTN_PLANS = (("p1", 1, 128, 32), ("p4", 8, 64, 8), ("p16", 16, 128, 2))
PAD = 128
WIN = 256


ATTN_UNROLL = 8


def _fill_bias(tab, qblk, partner):
    qi = lax.broadcasted_iota(jnp.int32, (2 * qblk, WIN), 0) & (qblk - 1)
    kj = lax.broadcasted_iota(jnp.int32, (2 * qblk, WIN), 1)
    piece = kj >> (qblk.bit_length() - 1)
    kk = kj & (qblk - 1)
    prev = (piece & 1) == 0
    of_partner = piece >= 2
    for first in (0, 1):
        for par in (0, 1):
            lo = jnp.where(prev, (qblk if first else qi) + jnp.where(of_partner, par, 0), 0)
            hi = jnp.where(prev, qblk, qi + jnp.where(of_partner, par - 1, 0))
            tab[2 * first + par] = jnp.where((kk >= lo) & (kk <= hi), 0.0, NEG).astype(F32)


def _block_rows(g, qblk, nbc, partner):
    own = pl.ds(pl.multiple_of(PAD + g * qblk, qblk), qblk)
    first = ((g & (nbc - 1)) == 0).astype(jnp.int32)
    if partner:
        gp = jnp.bitwise_xor(g, 4 * nbc)
        wins = (pl.ds(pl.multiple_of(PAD + (g - 1) * qblk, qblk), 2 * qblk),
                pl.ds(pl.multiple_of(PAD + (gp - 1) * qblk, qblk), 2 * qblk))
        return own, wins, 2 * first + ((g >> ((4 * nbc).bit_length() - 1)) & 1)
    return own, (pl.ds(pl.multiple_of(PAD + (g - 1) * qblk, qblk), 2 * qblk),), 2 * first


def _pack_pair(lo, hi):
    lo_bits = lax.bitcast_convert_type(lo.astype(BF16).astype(F32), jnp.uint32) >> 16
    hi_bits = lax.bitcast_convert_type(hi.astype(BF16).astype(F32), jnp.uint32) & jnp.uint32(0xFFFF0000)
    return lax.bitcast_convert_type(hi_bits | lo_bits, F32)


def _unpack_pair(c):
    bits = lax.bitcast_convert_type(c, jnp.uint32)
    lo = lax.bitcast_convert_type(bits << 16, F32).astype(BF16)
    hi = lax.bitcast_convert_type(bits & jnp.uint32(0xFFFF0000), F32).astype(BF16)
    return lo, hi


def _window(ref, wins):
    parts = [ref[w, :] for w in wins]
    return parts[0] if len(parts) == 1 else jnp.concatenate(parts, axis=0)


def _stack_heads(t, lane):
    zero = jnp.zeros_like(t)
    return jnp.concatenate([jnp.where(lane < HEAD, t, zero), jnp.where(lane >= HEAD, t, zero)], axis=0)


def _unstack_heads(t2, lane):
    half = t2.shape[0] // 2
    return jnp.where(lane < HEAD, t2[0:half, :], t2[half:, :])


def _lanes_of(step):
    return pl.ds(pl.multiple_of(step * LANES, LANES), LANES)


def _whole_wait(buf, sem):
    whole = buf.at[pl.ds(PAD, S), :]
    return pltpu.make_async_copy(whole, whole, sem)


def _whole_waits(bufs, sems):
    return [_whole_wait(buf, sems.at[i]) for i, buf in enumerate(bufs)]


def _class_gather(views, bufs, sems, lanes):
    copies = []
    for i, (view, buf) in enumerate(zip(views, bufs)):
        if view.ndim == 2:
            copies.append(pltpu.make_async_copy(view.at[:, lanes], buf.at[pl.ds(PAD, S), :], sems.at[i]))
        else:
            per, n_cls = view.shape[0], view.shape[1]
            copies += [pltpu.make_async_copy(view.at[:, c, lanes], buf.at[pl.ds(PAD + c * per, per), :], sems.at[i])
                       for c in range(n_cls)]
    return copies


def _class_scatter(bufs, dsts, sems, lanes=None):
    copies = []
    for i, (buf, dst) in enumerate(zip(bufs, dsts)):
        if dst.ndim == 2:
            copies.append(pltpu.make_async_copy(buf.at[pl.ds(PAD, S), :], dst.at[:, lanes], sems.at[i]))
            continue
        per, n_cls = dst.shape[0], dst.shape[1]
        for c in range(n_cls):
            to = dst.at[:, c, :] if lanes is None else dst.at[:, c, lanes]
            copies.append(pltpu.make_async_copy(buf.at[pl.ds(PAD + c * per, per), :], to, sems.at[i]))
    return copies


def _start(copies):
    for cp in copies:
        cp.start()


def _wait(waits):
    for w in waits:
        w.wait()


def _attn_fwd(q, kvp, shards=()):
    views = [[a] + [a.reshape(S // n, n, AW) for _, n, _, _ in ATTN_PLANS[1:]] for a in (q, kvp)]
    flat = [views[a][p] for p in range(3) for a in range(2)]
    ng = len(shards)
    n_grid = AW // LANES

    def body(*refs):
        hbm = [refs[2 * p:2 * p + 2] for p in range(3)]
        refs = refs[6:]
        shard_refs, refs = refs[:ng], refs[ng:]
        y_ref, lt_ref = refs[0:2]
        whole_refs, refs = refs[2:2 + ng], refs[2 + ng:]
        bufs = [refs[2 * p:2 * p + 2] for p in range(3)]
        oc4, lc4, oc16, lc16, o4n, l4n, o16n, l16n, tab128, tab4, sem_in, sem_out = refs[6:18]
        step = pl.program_id(0)
        if ng:
            start_gather, finish_gather = _gather_steps(shard_refs, whole_refs, *refs[18:])
            pl.when(step == 0)(start_gather)
        now = [_class_gather(hbm[p], bufs[p], sem_in.at[p], _lanes_of(step)) for p in range(3)]
        nxt = [_class_gather(hbm[p], bufs[p], sem_in.at[p], _lanes_of(step + 1)) for p in range(3)]

        @pl.when(step == 0)
        def _():
            for p in range(3):
                _start(now[p])
                for b in bufs[p]:
                    b[0:PAD, :] = jnp.zeros((PAD, LANES), F32)
            _fill_bias(tab128, 128, False)
            _fill_bias(tab4, 64, True)

        def prefetch(p):
            pl.when(step + 1 < n_grid)(lambda: _start(nxt[p]))

        lane = lax.broadcasted_iota(jnp.int32, (1, LANES), 1)
        ones = jnp.ones((WIN, LANES), BF16)

        def run(plan, bq, bkv, tab, o_dst, l_dst, dst_pad):
            _, n_cls, qblk, nbc = plan
            partner = n_cls == 8

            def block(g, carry):
                own, wins, mask = _block_rows(g, qblk, nbc, partner)
                q2 = _stack_heads(bq[own, :].astype(BF16), lane)
                kw, vwin = _unpack_pair(_window(bkv, wins))
                vw = jnp.concatenate([vwin, ones], axis=1)
                s = _dot_nt(q2, kw) + tab[mask]
                m = jnp.max(s, axis=1, keepdims=True)
                oe = _dot(jnp.exp(s - m).astype(BF16), vw)
                den = oe[:, LANES:]
                dst = pl.ds(pl.multiple_of(dst_pad + g * qblk, qblk), qblk)
                o_dst[dst, :] = _unstack_heads(oe[:, 0:LANES] / den, lane)
                l_dst[dst, :] = _unstack_heads(m + jnp.log(den), lane)
                return carry
            lax.fori_loop(0, n_cls * nbc, block, 0, unroll=ATTN_UNROLL)

        _wait(_whole_waits(bufs[0], sem_in.at[0]))
        run(ATTN_PLANS[0], *bufs[0], tab128, y_ref, lt_ref, 0)
        prefetch(0)
        _wait(_whole_waits(bufs[1], sem_in.at[1]))
        run(ATTN_PLANS[1], *bufs[1], tab4, oc4, lc4, PAD)
        prefetch(1)
        _start(_class_scatter((oc4, lc4), (o4n, l4n), sem_out.at[0]))
        _wait(_whole_waits(bufs[2], sem_in.at[2]))
        run(ATTN_PLANS[2], *bufs[2], tab128, oc16, lc16, PAD)
        prefetch(2)
        _start(_class_scatter((oc16, lc16), (o16n, l16n), sem_out.at[1]))
        _wait(_whole_waits((oc4, lc4), sem_out.at[0]) + _whole_waits((oc16, lc16), sem_out.at[1]))

        for t in range(S // TQ):
            rows = pl.ds(t * TQ, TQ)
            r4, r16 = pl.ds(t * (TQ // 8), TQ // 8), pl.ds(t * (TQ // 16), TQ // 16)
            l0, l1, l2 = lt_ref[rows, :], l4n[r4, :, :].reshape(TQ, LANES), l16n[r16, :, :].reshape(TQ, LANES)
            lm = jnp.maximum(jnp.maximum(l0, l1), l2)
            e0, e1, e2 = jnp.exp(l0 - lm), jnp.exp(l1 - lm), jnp.exp(l2 - lm)
            den = e0 + e1 + e2
            y_ref[rows, :] = (e0 * y_ref[rows, :] + e1 * o4n[r4, :, :].reshape(TQ, LANES)
                              + e2 * o16n[r16, :, :].reshape(TQ, LANES)) / den
            lt_ref[rows, :] = lm + jnp.log(den)

        if ng:
            pl.when(step == n_grid - 1)(finish_gather)

    col = pl.BlockSpec((S, LANES), lambda h: (0, h))
    padded = pltpu.VMEM((PAD + S, LANES), F32)
    return pl.pallas_call(
        body, grid=(n_grid,), name="attn_fwd",
        in_specs=[ANY] * (6 + ng), out_specs=[col, col] + [ANY] * ng,
        out_shape=[jax.ShapeDtypeStruct((S, AW), F32)] * 2 + _gathered_shapes(shards),
        scratch_shapes=[padded] * 10 + [
            pltpu.VMEM((S // 8, 8, LANES), F32), pltpu.VMEM((S // 8, 8, LANES), F32),
            pltpu.VMEM((S // 16, 16, LANES), F32), pltpu.VMEM((S // 16, 16, LANES), F32),
            pltpu.VMEM((4, 256, WIN), F32), pltpu.VMEM((4, 128, WIN), F32),
            pltpu.SemaphoreType.DMA((3, 2)), pltpu.SemaphoreType.DMA((2, 2))]
        + (_gather_scratch(ng) if ng else []),
        compiler_params=_cparams(56))(*flat, *shards)


def _conv_taps(z, zprev, row):
    z1 = jnp.where(row == 0, zprev[7:8, :], pltpu.roll(z, 1, 0))
    z2 = jnp.where(row == 0, zprev[6:7, :], jnp.where(row == 1, zprev[7:8, :], pltpu.roll(z, 2, 0)))
    return z1, z2


def _xattn_scores(qm, km):
    s = _dot_nt(qm, km)
    m = jnp.max(s, axis=1, keepdims=True)
    e = jnp.exp(s - m)
    return e, jnp.sum(e, axis=1, keepdims=True)


def _mix_out(y_attn, bcu, qx16, kv16, cw8, g_attn, g_conv, g_x, g_post, wout16, x):
    def body(ya_ref, bcu_ref, halo_ref, qx_ref, kv_ref, cw_ref, ga_ref, gc_ref, gx_ref, gp_ref, w_ref, x_ref,
             ypre_ref, y16_ref, y2_ref, x1_ref):
        i = pl.program_id(0)
        bcu = bcu_ref[...]
        b, c, u = bcu[:, 0:CW], bcu[:, CW:2 * CW], bcu[:, 2 * CW:]
        z = c * u
        halo = halo_ref[...]
        zprev = jnp.where(i > 0, halo[:, CW:2 * CW] * halo[:, 2 * CW:], 0.0)
        row = lax.broadcasted_iota(jnp.int32, z.shape, 0)
        z1, z2 = _conv_taps(z, zprev, row)
        cw = cw_ref[...]
        y_conv = b * (z2 * cw[0:1, :] + z1 * cw[1:2, :] + z * cw[2:3, :])

        qx = qx_ref[...]
        kv = kv_ref[...]
        km, vm = kv[:, 0:XW], kv[:, XW:]
        lane = lax.broadcasted_iota(jnp.int32, qx.shape, 1)
        y_x = jnp.zeros(qx.shape, F32)
        for h in range(XW // HEAD):
            hm = (lane >= h * HEAD) & (lane < (h + 1) * HEAD)
            e, l = _xattn_scores(jnp.where(hm, qx, jnp.zeros_like(qx)), km)
            y_x = jnp.where(hm, _dot(e.astype(BF16), vm) / l, y_x)

        y_attn = ya_ref[...]
        ypre_ref[:, 0:AW] = y_attn
        ypre_ref[:, AW:AW + CW] = y_conv
        ypre_ref[:, AW + CW:] = y_x
        y = jnp.concatenate([_rms(y_attn, ga_ref[...])[0], _rms(y_conv, gc_ref[...])[0],
                             _rms(y_x, gx_ref[...])[0]], axis=1).astype(BF16)
        y16_ref[...] = y
        y2 = _dot(y, w_ref[...])
        y2_ref[...] = y2
        x1_ref[...] = x_ref[...] + _rms(y2, gp_ref[...])[0]

    def tile(w):
        return pl.BlockSpec((TQ, w), lambda i: (i, 0))

    halo = pl.BlockSpec((SUBLANES, 3 * CW), lambda i: (jnp.maximum(i * (TQ // SUBLANES) - 1, 0), 0))
    return pl.pallas_call(
        body, grid=(NT,), name="mix_out",
        in_specs=[tile(AW), tile(3 * CW), halo, tile(XW), _const((N_MEM, 2 * XW)), _const((SUBLANES, CW)),
                  _const((1, AW)), _const((1, CW)), _const((1, XW)), _const((1, D)), _const((D, D)), tile(D)],
        out_specs=[tile(D), tile(D), tile(D), tile(D)],
        out_shape=[jax.ShapeDtypeStruct((S, D), F32), jax.ShapeDtypeStruct((S, D), BF16),
                   jax.ShapeDtypeStruct((S, D), F32), jax.ShapeDtypeStruct((S, D), F32)],
        compiler_params=_cparams(56))(y_attn, bcu, bcu, qx16, kv16, cw8, g_attn, g_conv, g_x, g_post, wout16, x)


def _mlp(x1, tgt, g_pre, g_post, wup8, wdn16):
    tq = TQ_MLP

    def body(x1_ref, t_ref, g1_ref, g2_ref, wu_ref, wd_ref,
             a16_ref, du_ref, h2_ref, df2_ref, dx1_ref, loss_ref, dg_ref, a32):
        @pl.when(pl.program_id(0) == 0)
        def _():
            loss_ref[...] = jnp.zeros_like(loss_ref)
            dg_ref[...] = jnp.zeros_like(dg_ref)

        x1 = x1_ref[...]
        g1, g2 = g1_ref[...], g2_ref[...]
        y1, n1, r1 = _rms(x1, g1)
        h2 = y1.astype(BF16)
        h2_ref[...] = h2
        f2 = jnp.zeros((tq, D), F32)
        for j in range(N_DEV):
            cols = slice(j * FF_BLK, (j + 1) * FF_BLK)
            a = jnp.maximum(_dot(h2, wu_ref[j]), 0.0)
            a32[:, cols] = a
            a16_ref[:, cols] = a.astype(BF16)
            f2 = f2 + _dot((a * a).astype(BF16), wd_ref[cols, :])
        y2, n2, r2 = _rms(f2, g2)
        e = x1 + y2 - t_ref[...]
        sq = jnp.sum(jnp.sum(e * e, axis=1, keepdims=True), axis=0, keepdims=True)
        loss_ref[...] += jnp.broadcast_to(sq * (0.5 / D), loss_ref.shape)
        dout = e * (1.0 / D)
        df2, dg2 = _rms_bwd(dout, n2, r2, g2)
        df2_16 = df2.astype(BF16)
        df2_ref[...] = df2_16
        dh2 = jnp.zeros((tq, D), F32)
        for j in range(N_DEV):
            cols = slice(j * FF_BLK, (j + 1) * FF_BLK)
            du = (_dot_nt(df2_16, wd_ref[cols, :]) * (2.0 * a32[:, cols])).astype(BF16)
            du_ref[:, cols] = du
            dh2 = dh2 + _dot_nt(du, wu_ref[j])
        dx, dg1 = _rms_bwd(dh2, n1, r1, g1)
        dx1_ref[...] = dout + dx
        dg_ref[0:1, :] += dg2
        dg_ref[1:2, :] += dg1

    def tile(w):
        return pl.BlockSpec((tq, w), lambda i: (i, 0))

    return pl.pallas_call(
        body, grid=(S // tq,), name="mlp",
        in_specs=[tile(D), tile(D), _const((1, D)), _const((1, D)), _const((N_DEV, D, FF_BLK)), _const((FF, D))],
        out_specs=[tile(FF), tile(FF), tile(D), tile(D), tile(D), _acc((SUBLANES, LANES)), _acc((SUBLANES, D))],
        out_shape=[jax.ShapeDtypeStruct((S, FF), BF16), jax.ShapeDtypeStruct((S, FF), BF16),
                   jax.ShapeDtypeStruct((S, D), BF16), jax.ShapeDtypeStruct((S, D), BF16),
                   jax.ShapeDtypeStruct((S, D), F32), jax.ShapeDtypeStruct((SUBLANES, LANES), F32),
                   jax.ShapeDtypeStruct((SUBLANES, D), F32)],
        scratch_shapes=[pltpu.VMEM((tq, FF), F32)],
        compiler_params=_cparams(56))(x1, tgt, g_pre, g_post, wup8, wdn16)


def _mix_out_bwd(dx1, y2, ypre, ltot, head_ones, q, bcu, qx16, kv16, cw8, g_post, g_attn, g_conv, g_x, wout16):
    def body(dx1_ref, y2_ref, ypre_ref, lt_ref, e_ref, q_ref, bcu_ref, halo_ref, qx_ref, kv_ref, cw_ref, gp_ref,
             ga_ref, gc_ref, gx_ref, w_ref, dy2_ref, qdo_ref, ld_ref, dbcu_ref, dqx_ref, dgs_ref, dcw_ref, dkv_ref,
             carry):
        i = pl.program_id(0)

        @pl.when(i == 0)
        def _():
            dgs_ref[...] = jnp.zeros_like(dgs_ref)
            dcw_ref[...] = jnp.zeros_like(dcw_ref)
            dkv_ref[...] = jnp.zeros_like(dkv_ref)
            carry[...] = jnp.zeros_like(carry)

        gp = gp_ref[...]
        _, n, r = _rms(y2_ref[...], gp)
        dy2, dgp = _rms_bwd(dx1_ref[...], n, r, gp)
        dy2_16 = dy2.astype(BF16)
        dy2_ref[...] = dy2_16
        dy = _dot_nt(dy2_16, w_ref[...])

        ypre = ypre_ref[...]
        ga, gc, gx = ga_ref[...], gc_ref[...], gx_ref[...]
        _, na, ra = _rms(ypre[:, 0:AW], ga)
        dya, dga = _rms_bwd(dy[:, 0:AW], na, ra, ga)
        _, nc, rc = _rms(ypre[:, AW:AW + CW], gc)
        dyc, dgc = _rms_bwd(dy[:, AW:AW + CW], nc, rc, gc)
        y_x = ypre[:, AW + CW:]
        _, nx, rx = _rms(y_x, gx)
        dyx, dgx = _rms_bwd(dy[:, AW + CW:], nx, rx, gx)
        qdo_ref[...] = _pack_pair(q_ref[...], dya)
        prod = dya * ypre[:, 0:AW]
        hi = prod.astype(BF16)
        lo = (prod - hi.astype(F32)).astype(BF16)
        head_sum = _dot(hi, e_ref[...]) + _dot(lo, e_ref[...])
        lane_a = lax.broadcasted_iota(jnp.int32, prod.shape, 1)
        ld_ref[...] = jnp.where((lane_a % HEAD) < HEAD // 2, lt_ref[...], head_sum)
        dgs_ref[0:1, :] += dgp
        dgs_ref[1:2, :] += jnp.concatenate([dga, dgc, dgx], axis=1)

        bcu = bcu_ref[...]
        b, c, u = bcu[:, 0:CW], bcu[:, CW:2 * CW], bcu[:, 2 * CW:]
        z = c * u
        halo = halo_ref[...]
        zprev = jnp.where(i < NT - 1, halo[:, CW:2 * CW] * halo[:, 2 * CW:], 0.0)
        row = lax.broadcasted_iota(jnp.int32, z.shape, 0)
        z1, z2 = _conv_taps(z, zprev, row)
        cw = cw_ref[...]
        conv = z2 * cw[0:1, :] + z1 * cw[1:2, :] + z * cw[2:3, :]
        dconv = dyc * b
        nxt = carry[...]
        dn1 = jnp.where(row == TQ - 1, nxt[0:1, :], pltpu.roll(dconv, TQ - 1, 0))
        dn2 = jnp.where(row == TQ - 1, nxt[1:2, :], jnp.where(row == TQ - 2, nxt[0:1, :], pltpu.roll(dconv, TQ - 2, 0)))
        carry[...] = dconv[0:SUBLANES, :]
        dz = dconv * cw[2:3, :] + dn1 * cw[1:2, :] + dn2 * cw[0:1, :]
        dbcu_ref[:, 0:CW] = dyc * conv
        dbcu_ref[:, CW:2 * CW] = dz * u
        dbcu_ref[:, 2 * CW:] = dz * c
        dcw_ref[0:1, :] += jnp.sum(z2 * dconv, axis=0, keepdims=True)
        dcw_ref[1:2, :] += jnp.sum(z1 * dconv, axis=0, keepdims=True)
        dcw_ref[2:3, :] += jnp.sum(z * dconv, axis=0, keepdims=True)

        qx = qx_ref[...]
        kv = kv_ref[...]
        km, vm = kv[:, 0:XW], kv[:, XW:]
        lane = lax.broadcasted_iota(jnp.int32, qx.shape, 1)
        dqx = jnp.zeros(qx.shape, F32)
        dkm = jnp.zeros((N_MEM, XW), F32)
        dvm = jnp.zeros((N_MEM, XW), F32)
        for h in range(XW // HEAD):
            hm = (lane >= h * HEAD) & (lane < (h + 1) * HEAD)
            qm = jnp.where(hm, qx, jnp.zeros_like(qx))
            e, l = _xattn_scores(qm, km)
            p = e / l
            dom = jnp.where(hm, dyx, 0.0)
            do16 = dom.astype(BF16)
            dsum = jnp.sum(dom * y_x, axis=1, keepdims=True)
            ds = (p * (_dot_nt(do16, vm) - dsum)).astype(BF16)
            dqx = jnp.where(hm, _dot(ds, km), dqx)
            dkm = dkm + _dot_tn(ds, qm)
            dvm = dvm + _dot_tn(p.astype(BF16), do16)
        dqx_ref[...] = dqx * SCALE
        dkv_ref[:, 0:XW] += dkm
        dkv_ref[:, XW:] += dvm

    def tile(w):
        return pl.BlockSpec((TQ, w), lambda i: (NT - 1 - i, 0))

    halo = pl.BlockSpec((SUBLANES, 3 * CW), lambda i: (jnp.maximum((NT - 1 - i) * (TQ // SUBLANES) - 1, 0), 0))
    return pl.pallas_call(
        body, grid=(NT,), name="mix_out_bwd",
        in_specs=[tile(D), tile(D), tile(D), tile(AW), _const((AW, AW)), tile(AW), tile(3 * CW), halo, tile(XW),
                  _const((N_MEM, 2 * XW)), _const((SUBLANES, CW)), _const((1, D)), _const((1, AW)), _const((1, CW)),
                  _const((1, XW)), _const((D, D))],
        out_specs=[tile(D), tile(AW), tile(AW), tile(3 * CW), tile(XW), _acc((SUBLANES, D)), _acc((SUBLANES, CW)),
                   _acc((N_MEM, 2 * XW))],
        out_shape=[jax.ShapeDtypeStruct((S, D), BF16), jax.ShapeDtypeStruct((S, AW), F32),
                   jax.ShapeDtypeStruct((S, AW), F32),
                   jax.ShapeDtypeStruct((S, 3 * CW), F32), jax.ShapeDtypeStruct((S, XW), F32),
                   jax.ShapeDtypeStruct((SUBLANES, D), F32), jax.ShapeDtypeStruct((SUBLANES, CW), F32),
                   jax.ShapeDtypeStruct((N_MEM, 2 * XW), F32)],
        scratch_shapes=[pltpu.VMEM((SUBLANES, CW), F32)],
        compiler_params=_cparams(56))(dx1, y2, ypre, ltot, head_ones, q, bcu, bcu, qx16, kv16, cw8, g_post, g_attn,
                                      g_conv, g_x, wout16)


def _attn_bwd(qdo, kvp, ld, chip_sums=()):
    n_in = 3
    views = [[a] + [a.reshape(S // n, n, AW) for _, n, _, _ in ATTN_PLANS[1:]] for a in (qdo, kvp, ld)]
    flat = [views[a][p] for p in range(3) for a in range(n_in)]
    ns = len(chip_sums)
    n_grid = AW // LANES

    def body(*refs):
        hbm = [refs[n_in * p:n_in * p + n_in] for p in range(3)]
        refs = refs[3 * n_in:]
        sum_refs, refs = refs[:ns], refs[ns:]
        outs = [refs[3 * p:3 * p + 3] for p in range(3)]
        landed_refs, sc = refs[9:9 + ns], refs[9 + ns:]
        bufs = [sc[3 * p:3 * p + 3] for p in range(3)]
        res = [sc[9 + 3 * p:12 + 3 * p] for p in range(3)]
        tab128, tab4, sem_in, sem_out = sc[18:22]
        step = pl.program_id(0)
        if ns:
            start_chips, finish_chips = _chips_steps(sum_refs, landed_refs, *sc[22:])
            pl.when(step == 0)(start_chips)
        now = [_class_gather(hbm[p], bufs[p], sem_in.at[p], _lanes_of(step)) for p in range(3)]
        nxt = [_class_gather(hbm[p], bufs[p], sem_in.at[p], _lanes_of(step + 1)) for p in range(3)]

        @pl.when(step == 0)
        def _():
            for p in range(3):
                _start(now[p])
                for b in bufs[p]:
                    b[0:PAD, :] = jnp.zeros((PAD, LANES), F32)
            _fill_bias(tab128, 128, False)
            _fill_bias(tab4, 64, True)

        def prefetch(p):
            pl.when(step + 1 < n_grid)(lambda: _start(nxt[p]))

        for p in range(3):
            for b in res[p]:
                b[...] = jnp.zeros_like(b)
        lane = lax.broadcasted_iota(jnp.int32, (1, LANES), 1)

        def run(plan, plan_bufs, tab, dst):
            _, n_cls, qblk, nbc = plan
            partner = n_cls == 8
            bqdo, bkv, bld = plan_bufs
            rq, rk, rv = dst

            def block(g, carry):
                own, wins, mask = _block_rows(g, qblk, nbc, partner)
                qb, dob = _unpack_pair(bqdo[own, :])
                q2, do2 = _stack_heads(qb, lane), _stack_heads(dob, lane)
                kw, vw = _unpack_pair(_window(bkv, wins))
                ldv = bld[own, :]
                half = HEAD // 2
                lt2 = jnp.concatenate([ldv[:, 0:1], ldv[:, HEAD:HEAD + 1]], axis=0)
                dsum2 = jnp.concatenate([ldv[:, half:half + 1], ldv[:, HEAD + half:HEAD + half + 1]], axis=0)
                p = jnp.exp(_dot_nt(q2, kw) + tab[mask] - lt2)
                ds = (p * (_dot_nt(do2, vw) - dsum2)).astype(BF16)
                rq[own, :] = _unstack_heads(_dot(ds, kw), lane)
                dkw = _dot_tn(ds, q2)
                dvw = _dot_tn(p.astype(BF16), do2)
                n_w = WIN // len(wins)
                for i, w in enumerate(wins):
                    rk[w, :] += dkw[i * n_w:(i + 1) * n_w, :]
                    rv[w, :] += dvw[i * n_w:(i + 1) * n_w, :]
                return carry
            lax.fori_loop(0, n_cls * nbc, block, 0, unroll=ATTN_UNROLL)

        tabs = (tab128, tab4, tab128)
        for p in range(3):
            _wait(_whole_waits(bufs[p], sem_in.at[p]))
            run(ATTN_PLANS[p], bufs[p], tabs[p], res[p])
            prefetch(p)
            _start(_class_scatter(res[p], outs[p], sem_out.at[p], _lanes_of(step)))
        for p in range(3):
            _wait(_whole_waits(res[p], sem_out.at[p]))
        if ns:
            pl.when(step == n_grid - 1)(finish_chips)

    padded = pltpu.VMEM((PAD + S, LANES), F32)
    shapes = [jax.ShapeDtypeStruct(views[0][p].shape, F32) for p in range(3) for _ in range(3)]
    out = pl.pallas_call(
        body, grid=(n_grid,), name="attn_bwd",
        in_specs=[ANY] * (3 * n_in + ns), out_specs=[ANY] * (9 + ns),
        out_shape=shapes + _chips_shapes(chip_sums),
        scratch_shapes=[padded] * 18
        + [pltpu.VMEM((4, 256, WIN), F32), pltpu.VMEM((4, 128, WIN), F32),
           pltpu.SemaphoreType.DMA((3, n_in)), pltpu.SemaphoreType.DMA((3, 3))]
        + (_chips_scratch(ns) if ns else []),
        compiler_params=_cparams(56))(*flat, *chip_sums)
    return [o.reshape(S, AW) for o in out[:9]] + list(out[9:])


def _in_proj_bwd(dqkv, dbcu, dqx, cos, sins, w16, x, g, dx1):
    tq = TQ // 2

    def body(*refs):
        parts = refs[0:9]
        dbcu_ref, dqx_ref, c_ref, s_ref, w_ref, x_ref, g_ref, dx1_ref, dp_ref, gx_ref, dg_ref = refs[9:]

        @pl.when(pl.program_id(0) == 0)
        def _():
            dg_ref[...] = jnp.zeros_like(dg_ref)

        dq, dk, dv = (parts[i][...] + parts[3 + i][...] + parts[6 + i][...] for i in range(3))
        cos, sn = c_ref[...], s_ref[...]
        dqr = dq * SCALE
        dkr = dk
        dp = jnp.concatenate([dqr * cos + _rot_half(dqr * sn), dkr * cos + _rot_half(dkr * sn), dv,
                              dbcu_ref[...], dqx_ref[...]], axis=1).astype(BF16)
        dp_ref[...] = dp
        dh = _dot_nt(dp, w_ref[...])
        g = g_ref[...]
        _, n, r = _rms(x_ref[...], g)
        dx, dg = _rms_bwd(dh, n, r, g)
        gx_ref[...] = dx1_ref[...] + dx
        dg_ref[0:1, :] += dg

    def tile(w):
        return pl.BlockSpec((tq, w), lambda i: (i, 0))

    return pl.pallas_call(
        body, grid=(S // tq,), name="in_proj_bwd",
        in_specs=[tile(AW)] * 9 + [tile(3 * CW), tile(XW), tile(AW), tile(AW), _const((D, PW)),
                                   tile(D), _const((1, D)), tile(D)],
        out_specs=[tile(PW), tile(D), _acc((SUBLANES, D))],
        out_shape=[jax.ShapeDtypeStruct((S, PW), BF16), jax.ShapeDtypeStruct((S, D), F32),
                   jax.ShapeDtypeStruct((SUBLANES, D), F32)],
        compiler_params=_cparams(56))(*dqkv, dbcu, dqx, cos, sins, w16, x, g, dx1)


def _mem_bwd(mem, g_mem, wkv16, dkv):
    def body(m_ref, g_ref, w_ref, dkv_ref, dkv16_ref, dg_ref):
        dkv16 = dkv_ref[...].astype(BF16)
        dkv16_ref[...] = dkv16
        _, n, _ = _rms(m_ref[...], g_ref[...])
        dg = jnp.sum(_dot_nt(dkv16, w_ref[...]) * n, axis=0, keepdims=True)
        dg_ref[...] = jnp.broadcast_to(dg, dg_ref.shape)

    return pl.pallas_call(
        body, name="mem_bwd",
        out_shape=[jax.ShapeDtypeStruct((N_MEM, 2 * XW), BF16), jax.ShapeDtypeStruct((SUBLANES, D), F32)],
        compiler_params=pltpu.CompilerParams(vmem_limit_bytes=32 << 20))(mem, g_mem, wkv16, dkv)


def _wgrad(a16, b16, tn, name, square_b=False, transpose_out=False):
    kk, m = a16.shape
    n_tiles = b16.shape[1] // tn
    chunk = min(kk, 512)
    oshape = (tn, m) if transpose_out else (m, tn)

    def body(a_ref, b_ref, o32_ref, o16_ref, at):
        @pl.when(pl.program_id(0) == 0)
        def _():
            for c in range(kk // chunk):
                at[:, c * chunk:(c + 1) * chunk] = a_ref[c * chunk:(c + 1) * chunk, :].T

        b = b_ref[...]
        if square_b:
            b = b * b
        acc = _dot(at[...], b)
        if transpose_out:
            acc = acc.T
        o32_ref[0] = acc
        o16_ref[0] = acc.astype(BF16)

    oblk = pl.BlockSpec((1,) + oshape, lambda j: (j, 0, 0))
    return pl.pallas_call(
        body, grid=(n_tiles,), name=name,
        in_specs=[_const((kk, m)), pl.BlockSpec((kk, tn), lambda j: (0, j))],
        out_specs=[oblk, oblk],
        out_shape=[jax.ShapeDtypeStruct((n_tiles,) + oshape, F32), jax.ShapeDtypeStruct((n_tiles,) + oshape, BF16)],
        scratch_shapes=[pltpu.VMEM((m, kk), BF16)],
        compiler_params=_cparams(56))(a16, b16)


def _adamw_math(w, g, m, v):
    m = ADAM_B1 * m + (1.0 - ADAM_B1) * g
    v = ADAM_B2 * v + (1.0 - ADAM_B2) * jnp.square(g)
    m_hat = m / (1.0 - ADAM_B1 ** ADAM_STEP)
    v_hat = v / (1.0 - ADAM_B2 ** ADAM_STEP)
    delta = -ADAM_LR * (m_hat / (jnp.sqrt(v_hat) + ADAM_EPS) + ADAM_WD * w)
    return delta, m, v


def _adamw_shard(own32, rb16, w, m, v, name):
    def body(o_ref, r_ref, w_ref, m_ref, v_ref, g_out, d_out, m_out, v_out):
        g = o_ref[...] + r_ref[0].astype(F32) + r_ref[1].astype(F32) + r_ref[2].astype(F32)
        g_out[...] = g
        d_out[...], m_out[...], v_out[...] = _adamw_math(w_ref[...], g, m_ref[...], v_ref[...])

    return pl.pallas_call(
        body, name=name, out_shape=[jax.ShapeDtypeStruct(w.shape, F32)] * 4,
        compiler_params=pltpu.CompilerParams(vmem_limit_bytes=48 << 20))(own32, rb16, w, m, v)


def _place():
    x, y, c = lax.axis_index("x"), lax.axis_index("y"), lax.axis_index("c")
    chips = [(1 - x, y), (x, 1 - y), (1 - x, 1 - y)]
    return x, y, c, chips


def _gather_steps(ins, outs, send, recv, lsem):
    nt = len(ins)
    x, y, c, chips = _place()
    me, sib = (x, y, c), (x, y, 1 - c)

    def slot(t, px, py, pc):
        return outs[t].at[4 * px + 2 * py + pc]

    def copy(t, k, block, to, src=None):
        return pltpu.make_async_remote_copy(
            src_ref=slot(t, *block) if src is None else src, dst_ref=slot(t, *block),
            send_sem=send.at[t, k], recv_sem=recv.at[t, k], device_id=to, device_id_type=MESH)

    mine = [pltpu.make_async_copy(ins[t], slot(t, *me), lsem.at[t]) for t in range(nt)]
    first = []
    for t in range(nt):
        first.append(copy(t, 0, me, sib, src=ins[t]))
        first += [copy(t, 1 + j, me, (*chip, c), src=ins[t]) for j, chip in enumerate(chips)]

    def start():
        for cp in mine + first:
            cp.start()

    def finish():
        passed = []
        for j, chip in enumerate(chips):
            for t in range(nt):
                copy(t, 1 + j, (*chip, c), me).wait_recv()
                fwd = copy(t, 4 + j, (*chip, c), sib)
                fwd.start()
                passed.append(fwd)
        for t in range(nt):
            copy(t, 0, sib, me).wait_recv()
            for j, chip in enumerate(chips):
                copy(t, 4 + j, (*chip, 1 - c), me).wait_recv()
        for cp in first + passed:
            cp.wait_send()
        for cp in mine:
            cp.wait()

    return start, finish


def _gather_scratch(nt):
    return [pltpu.SemaphoreType.DMA((nt, 7)), pltpu.SemaphoreType.DMA((nt, 7)), pltpu.SemaphoreType.DMA((nt,))]


def _gathered_shapes(shards):
    return [jax.ShapeDtypeStruct((N_DEV,) + s.shape, s.dtype) for s in shards]


def _all_gather(shards):
    nt = len(shards)

    def body(*refs):
        start, finish = _gather_steps(refs[:nt], refs[nt:2 * nt], *refs[2 * nt:])
        start()
        finish()

    return pl.pallas_call(
        body, name="all_gather_weights", in_specs=[ANY] * nt, out_specs=[ANY] * nt,
        out_shape=_gathered_shapes(shards), scratch_shapes=_gather_scratch(nt))(*shards)


def _rs_pair(g16s, name):
    nt = len(g16s)

    def body(*refs):
        ins, outs = refs[:nt], refs[nt:2 * nt]
        send, recv = refs[2 * nt:]
        x, y, c, _ = _place()
        copies = [pltpu.make_async_remote_copy(
            src_ref=ins[t].at[2 * p + (1 - c)], dst_ref=outs[t].at[p], send_sem=send.at[t, p], recv_sem=recv.at[t, p],
            device_id=(x, y, 1 - c), device_id_type=MESH) for t in range(nt) for p in range(4)]
        for cp in copies:
            cp.start()
        for cp in copies:
            cp.wait()

    return pl.pallas_call(
        body, name=name,
        in_specs=[ANY] * nt, out_specs=[ANY] * nt,
        out_shape=[jax.ShapeDtypeStruct((4,) + g.shape[1:], g.dtype) for g in g16s],
        scratch_shapes=[pltpu.SemaphoreType.DMA((nt, 4)), pltpu.SemaphoreType.DMA((nt, 4))])(*g16s)


def _rs_pair_add(place, g32, ra16, name):
    shp = g32.shape[1:]

    def body(pl_ref, g_ref, r_ref, cs_ref, own_ref):
        s = g_ref[0] + r_ref[0].astype(F32)
        cs_ref[0] = s.astype(BF16)

        @pl.when(pl.program_id(0) == pl_ref[1])
        def _():
            own_ref[...] = s

    blk = (1,) + shp
    return pl.pallas_call(
        body, name=name,
        grid_spec=pltpu.PrefetchScalarGridSpec(
            num_scalar_prefetch=1, grid=(4,),
            in_specs=[pl.BlockSpec(blk, lambda p, s: (2 * p + s[0], 0, 0)), pl.BlockSpec(blk, lambda p, s: (p, 0, 0))],
            out_specs=[pl.BlockSpec(blk, lambda p, s: (p, 0, 0)), pl.BlockSpec(shp, lambda p, s: (0, 0))]),
        out_shape=[jax.ShapeDtypeStruct((4,) + shp, BF16), jax.ShapeDtypeStruct(shp, F32)],
        compiler_params=_cparams(48))(place, g32, ra16)


def _chips_steps(ins, outs, send, recv):
    _, _, c, chips = _place()
    copies = [pltpu.make_async_remote_copy(
        src_ref=ins[t].at[2 * px + py], dst_ref=outs[t].at[j], send_sem=send.at[t, j], recv_sem=recv.at[t, j],
        device_id=(px, py, c), device_id_type=MESH) for t in range(len(ins)) for j, (px, py) in enumerate(chips)]

    def start():
        for cp in copies:
            cp.start()

    def finish():
        for cp in copies:
            cp.wait()

    return start, finish


def _chips_scratch(nt):
    return [pltpu.SemaphoreType.DMA((nt, 3)), pltpu.SemaphoreType.DMA((nt, 3))]


def _chips_shapes(cs16s):
    return [jax.ShapeDtypeStruct((3,) + g.shape[1:], g.dtype) for g in cs16s]


def _rs_chips(cs16s):
    nt = len(cs16s)

    def body(*refs):
        start, finish = _chips_steps(refs[:nt], refs[nt:2 * nt], *refs[2 * nt:])
        start()
        finish()

    return pl.pallas_call(
        body, name="reduce_scatter_chips", in_specs=[ANY] * nt, out_specs=[ANY] * nt,
        out_shape=_chips_shapes(cs16s), scratch_shapes=_chips_scratch(nt))(*cs16s)


SMALL = (("g_pre_mix", 0, 0, D), ("g_mem", 1, 0, D), ("g_post_mix", 2, 0, D), ("g_attn_out", 3, 0, AW),
         ("g_conv_out", 3, AW, CW), ("g_xattn_out", 3, AW + CW, XW), ("g_post_mlp", 4, 0, D), ("g_pre_mlp", 5, 0, D))
CONV_ROW = 8
PACK_ROWS = 16


LOSS_ROW = 15


def _small_params_step(dg_in, dg_mem, dgs, dg_mlp, dcw, loss8, params):
    flat = [a for n, _, _, _ in SMALL for a in params[n]] + list(params["conv_w"])
    n_par = len(SMALL) + 1
    tap_cols = CW // N_DEV

    def body(*refs):
        acc_in, acc_mem, acc_mix, acc_mlp, acc_cw, acc_loss = refs[0:6]
        refs = refs[6:]
        ins = refs[0:3 * n_par]
        loss_out = refs[3 * n_par]
        outs = refs[3 * n_par + 1:7 * n_par + 1]
        pack, land, send, recv = refs[7 * n_par + 1:]
        x, y, c, _ = _place()
        me = 4 * x + 2 * y + c
        pack[...] = jnp.zeros_like(pack)
        pack[0:1, :] = acc_in[0:1, :]
        pack[1:2, :] = acc_mem[0:1, :]
        pack[2:4, :] = acc_mix[0:2, :]
        pack[4:6, :] = acc_mlp[0:2, :]
        pack[CONV_ROW:CONV_ROW + 3, 0:CW] = acc_cw[0:3, :]
        pack[LOSS_ROW:LOSS_ROW + 1, 0:LANES] = acc_loss[0:1, :]
        land[me] = pack[...]
        copies = []
        for k in range(1, N_DEV):
            kx, ky, kc = (k >> 2) & 1, (k >> 1) & 1, k & 1
            peer = (1 - x if kx else x, 1 - y if ky else y, 1 - c if kc else c)
            copies.append(pltpu.make_async_remote_copy(
                src_ref=pack, dst_ref=land.at[me], send_sem=send.at[k - 1], recv_sem=recv.at[k - 1],
                device_id=peer, device_id_type=MESH))
        for cp in copies:
            cp.start()
        for cp in copies:
            cp.wait()
        tot = land[0]
        for s in range(1, N_DEV):
            tot = tot + land[s]
        loss_out[...] = jnp.broadcast_to(tot[LOSS_ROW:LOSS_ROW + 1, 0:LANES], loss_out.shape)

        def update(i, g):
            w_ref, m_ref, v_ref = ins[3 * i:3 * i + 3]
            g_out, d_out, m_out, v_out = outs[4 * i:4 * i + 4]
            g_out[...] = g
            d_out[...], m_out[...], v_out[...] = _adamw_math(w_ref[...], g, m_ref[...], v_ref[...])

        for i, (_, row, lane0, width) in enumerate(SMALL):
            update(i, tot[row:row + 1, lane0:lane0 + width])
        taps = pltpu.roll(tot[CONV_ROW:CONV_ROW + SUBLANES, 0:CW], jnp.where(me == 0, 0, CW - me * tap_cols), 1)
        update(n_par - 1, taps[0:3, 0:tap_cols])

    shapes = [jax.ShapeDtypeStruct(params[n][0].shape, F32) for n, _, _, _ in SMALL] + [
        jax.ShapeDtypeStruct(params["conv_w"][0].shape, F32)]
    loss, *out = pl.pallas_call(
        body, name="small_params_step",
        out_shape=[jax.ShapeDtypeStruct((SUBLANES, LANES), F32)] + [s for s in shapes for _ in range(4)],
        scratch_shapes=[pltpu.VMEM((PACK_ROWS, D), F32), pltpu.VMEM((N_DEV, PACK_ROWS, D), F32),
                        pltpu.SemaphoreType.DMA((N_DEV - 1,)), pltpu.SemaphoreType.DMA((N_DEV - 1,))],
    )(dg_in, dg_mem, dgs, dg_mlp, dcw, loss8, *flat)
    names = [n for n, _, _, _ in SMALL] + ["conv_w"]
    return loss[0, 0], {n: out[4 * i:4 * i + 4] for i, n in enumerate(names)}


def _reduce_to_chip_sums(place, grads):
    from_sib = _rs_pair([g16 for _, g16 in grads.values()], "reduce_scatter_pair_" + "_".join(grads))
    return {n: _rs_pair_add(place, g32, from_sib[t], "pair_add_" + n) for t, (n, (g32, _)) in enumerate(grads.items())}


def _local_step(x, mem, pos, gains, win16, late_shards, tgt, place):
    half = HEAD // 2
    inv_freq = jnp.float32(ROPE_THETA) ** (-(jnp.arange(half, dtype=F32) * 2.0 / HEAD))
    invf = jnp.tile(inv_freq, LANES // half)[None, :]
    sgn = jnp.tile(jnp.concatenate([-jnp.ones((half,), F32), jnp.ones((half,), F32)]), LANES // HEAD)[None, :]
    cos, sins = _rope_table(pos.astype(F32).reshape(S, 1), invf, sgn)

    q, kvp, bcu, qx16, h16 = _in_proj(x, gains["g_pre_mix"], win16, cos, sins)
    y_attn, ltot, wout8, wup8, wdn8, wkv8, conv8 = _attn_fwd(q, kvp, late_shards)
    wout16, wdn16, wkv16 = wout8.reshape(D, D), wdn8.reshape(FF, D), wkv8.reshape(D, 2 * XW)
    cw_full = conv8[:, 0:3, 0:CW // N_DEV].transpose(1, 0, 2).reshape(3, CW)
    cw8 = jnp.zeros((SUBLANES, CW), F32).at[0:3].set(cw_full)
    memn16, kv16 = _mem_fwd(mem, gains["g_mem"], wkv16)
    ypre, y16, y2, x1 = _mix_out(y_attn, bcu, qx16, kv16, cw8, gains["g_attn_out"], gains["g_conv_out"],
                                 gains["g_xattn_out"], gains["g_post_mix"], wout16, x)
    a16, du16, h2_16, df2_16, dx1, loss8, dg_mlp = _mlp(x1, tgt, gains["g_pre_mlp"], gains["g_post_mlp"], wup8, wdn16)

    mlp_sums = _reduce_to_chip_sums(place, {
        "w_up": _wgrad(h2_16, du16, FF_BLK, "wgrad_up"),
        "w_down": _wgrad(df2_16, a16, FF_BLK, "wgrad_down", square_b=True, transpose_out=True)})

    head_id = jnp.arange(AW, dtype=jnp.int32) // HEAD
    head_ones = (head_id[:, None] == head_id[None, :]).astype(BF16)
    dy2_16, qdo, ld, dbcu, dqx, dgs, dcw, dkv = _mix_out_bwd(
        dx1, y2, ypre, ltot, head_ones, q, bcu, qx16, kv16, cw8, gains["g_post_mix"], gains["g_attn_out"],
        gains["g_conv_out"], gains["g_xattn_out"], wout16)
    *dqkv, up_chips, dn_chips = _attn_bwd(qdo, kvp, ld, [mlp_sums["w_up"][0], mlp_sums["w_down"][0]])
    dproj16, grad_x, dg_in = _in_proj_bwd(dqkv, dbcu, dqx, cos, sins, win16, x, gains["g_pre_mix"], dx1)
    dkv16, dg_mem = _mem_bwd(mem, gains["g_mem"], wkv16, dkv)

    def by_owner_in(g):
        return g.transpose(1, 0, 2).reshape(D, N_DEV, PW // N_DEV).transpose(1, 0, 2)

    mix_sums = _reduce_to_chip_sums(place, {
        "w_in": tuple(by_owner_in(g) for g in _wgrad(h16, dproj16, 512, "wgrad_in")),
        "w_mem_kv": tuple(g.reshape(N_DEV, D // N_DEV, 2 * XW) for g in _wgrad(memn16, dkv16, 2 * XW, "wgrad_mem_kv")),
        "w_out": tuple(g.reshape(N_DEV, D // N_DEV, D) for g in _wgrad(y16, dy2_16, D, "wgrad_out"))})
    mix_chips = _rs_chips([s[0] for s in mix_sums.values()])
    reduced = {n: (s[1], mix_chips[t]) for t, (n, s) in enumerate(mix_sums.items())}
    reduced["w_up"] = (mlp_sums["w_up"][1], up_chips)
    reduced["w_down"] = (mlp_sums["w_down"][1], dn_chips)
    return grad_x, reduced, (dg_in, dg_mem, dgs, dg_mlp, dcw, loss8)


BIG = ("w_in", "w_mem_kv", "w_out", "w_up", "w_down")
ORDER = ("g_pre_mix", "g_mem", "w_in", "w_mem_kv", "conv_w", "g_attn_out", "g_conv_out", "g_xattn_out", "w_out",
         "g_post_mix", "g_pre_mlp", "w_up", "w_down", "g_post_mlp")


def kernel(x, mem, positions, g_pre_mix, g_mem, w_in, w_mem_kv, conv_w, g_attn_out, g_conv_out, g_xattn_out, w_out, g_post_mix, g_pre_mlp, w_up, w_down, g_post_mlp, loss_target, m_g_pre_mix, m_g_mem, m_w_in, m_w_mem_kv, m_conv_w, m_g_attn_out, m_g_conv_out, m_g_xattn_out, m_w_out, m_g_post_mix, m_g_pre_mlp, m_w_up, m_w_down, m_g_post_mlp, v_g_pre_mix, v_g_mem, v_w_in, v_w_mem_kv, v_conv_w, v_g_attn_out, v_g_conv_out, v_g_xattn_out, v_w_out, v_g_post_mix, v_g_pre_mlp, v_w_up, v_w_down, v_g_post_mlp):
    w = dict(g_pre_mix=g_pre_mix, g_mem=g_mem, w_in=w_in, w_mem_kv=w_mem_kv, conv_w=conv_w, g_attn_out=g_attn_out,
             g_conv_out=g_conv_out, g_xattn_out=g_xattn_out, w_out=w_out, g_post_mix=g_post_mix, g_pre_mlp=g_pre_mlp,
             w_up=w_up, w_down=w_down, g_post_mlp=g_post_mlp)
    mo = dict(g_pre_mix=m_g_pre_mix, g_mem=m_g_mem, w_in=m_w_in, w_mem_kv=m_w_mem_kv, conv_w=m_conv_w,
              g_attn_out=m_g_attn_out, g_conv_out=m_g_conv_out, g_xattn_out=m_g_xattn_out, w_out=m_w_out,
              g_post_mix=m_g_post_mix, g_pre_mlp=m_g_pre_mlp, w_up=m_w_up, w_down=m_w_down, g_post_mlp=m_g_post_mlp)
    vo = dict(g_pre_mix=v_g_pre_mix, g_mem=v_g_mem, w_in=v_w_in, w_mem_kv=v_w_mem_kv, conv_w=v_conv_w,
              g_attn_out=v_g_attn_out, g_conv_out=v_g_conv_out, g_xattn_out=v_g_xattn_out, w_out=v_w_out,
              g_post_mix=v_g_post_mix, g_pre_mlp=v_g_pre_mlp, w_up=v_w_up, w_down=v_w_down, g_post_mlp=v_g_post_mlp)

    xi, yi, ci = lax.axis_index("x"), lax.axis_index("y"), lax.axis_index("c")
    me = 4 * xi + 2 * yi + ci
    place = jnp.stack([ci, 2 * xi + yi]).astype(jnp.int32)

    win8, = _all_gather([w["w_in"][0].astype(BF16)])
    win16 = win8.transpose(1, 0, 2).reshape(D, PW)
    conv_tile = jnp.zeros((SUBLANES, LANES), F32).at[0:3, 0:CW // N_DEV].set(conv_w[0])
    late_shards = [w[n][0].astype(BF16) for n in ("w_out", "w_up", "w_down", "w_mem_kv")] + [conv_tile]

    gains = {n: w[n] for n, _, _, _ in SMALL}
    grad_x, reduced, small_acc = _local_step(
        x[0], mem[0], positions[0], gains, win16, late_shards, loss_target[0], place)

    grad, delta, new_m, new_v = {}, {}, {}, {}
    for n in BIG:
        g, d_, m_, v_ = _adamw_shard(*reduced[n], w[n][0], mo[n][0], vo[n][0], "adamw_" + n)
        grad[n], delta[n], new_m[n], new_v[n] = g[None], d_[None], m_[None], v_[None]

    params = {n: (w[n], mo[n], vo[n]) for n, _, _, _ in SMALL}
    params["conv_w"] = (w["conv_w"][0], mo["conv_w"][0], vo["conv_w"][0])
    loss, small = _small_params_step(*small_acc, params)
    for n, (g, d_, m_, v_) in small.items():
        lead = (lambda a: a[None]) if n == "conv_w" else (lambda a: a)
        grad[n], delta[n], new_m[n], new_v[n] = lead(g), lead(d_), lead(m_), lead(v_)

    return (loss, grad_x[None], *[grad[n] for n in ORDER], *[delta[n] for n in ORDER],
            *[new_m[n] for n in ORDER], *[new_v[n] for n in ORDER])
```

```python
import functools

import numpy as np
import jax
import jax.numpy as jnp
from jax import lax
from jax.experimental import pallas as pl
from jax.experimental.pallas import tpu as pltpu

F32, BF16 = jnp.float32, jnp.bfloat16
MESH = pl.DeviceIdType.MESH
ANY = pl.BlockSpec(memory_space=pl.ANY)

N_DEV = 8
D = 1024
S = 4096
N_MEM = 256
HEAD = 64
AW, CW, XW = 512, 256, 256
PW = 3 * AW + 3 * CW + XW
FF = 4096
FF_BLK = FF // N_DEV
PATTERNS = ((128, 1), (512, 4), (2048, 16))
QB = 128
EPS = 1e-6
NEG = -1e30
SCALE = HEAD ** -0.5
ROPE_THETA = 10000.0
LANES = 128
SUBLANES = 8

ADAM_LR, ADAM_B1, ADAM_B2, ADAM_EPS, ADAM_WD, ADAM_STEP = 0.001, 0.9, 0.999, 1e-08, 0.01, 10

TQ = 512
TQ_MLP = 256
NT = S // TQ


def _cparams(vmem_mb, n_grid=1):
    return pltpu.CompilerParams(dimension_semantics=("arbitrary",) * n_grid, vmem_limit_bytes=vmem_mb << 20)


def _const(shape):
    nd = len(shape)
    return pl.BlockSpec(shape, lambda *_: (0,) * nd, pipeline_mode=pl.Buffered(1))


def _acc(shape):
    nd = len(shape)
    return pl.BlockSpec(shape, lambda *_: (0,) * nd)


def _dot(a, b):
    return jnp.dot(a, b, preferred_element_type=F32)


def _dot_nt(a, b):
    return lax.dot_general(a, b, (((1,), (1,)), ((), ())), preferred_element_type=F32)


def _dot_tn(a, b):
    return lax.dot_general(a, b, (((0,), (0,)), ((), ())), preferred_element_type=F32)


def _rms(x, g):
    r = lax.rsqrt(jnp.mean(x * x, axis=-1, keepdims=True) + EPS)
    n = x * r
    return n * g, n, r


def _rms_bwd(dy, n, r, g):
    dn = dy * g
    dx = r * (dn - n * jnp.mean(dn * n, axis=-1, keepdims=True))
    return dx, jnp.sum(dy * n, axis=0, keepdims=True)


def _rot_half(t):
    lane = lax.broadcasted_iota(jnp.int32, t.shape, 1)
    n = t.shape[1]
    return jnp.where((lane % HEAD) < HEAD // 2, pltpu.roll(t, n - HEAD // 2, 1), pltpu.roll(t, HEAD // 2, 1))


def _rope_table(pos_col, invf, sgn):
    def body(p_ref, f_ref, s_ref, c_out, s_out):
        ang = p_ref[...] * f_ref[...]
        c_out[...] = jnp.tile(jnp.cos(ang), (1, AW // LANES))
        s_out[...] = jnp.tile(jnp.sin(ang) * s_ref[...], (1, AW // LANES))

    tile = pl.BlockSpec((TQ, AW), lambda i: (i, 0))
    return pl.pallas_call(
        body, grid=(NT,), name="rope_table",
        in_specs=[pl.BlockSpec((TQ, 1), lambda i: (i, 0)), _const((1, LANES)), _const((1, LANES))],
        out_specs=[tile, tile], out_shape=[jax.ShapeDtypeStruct((S, AW), F32)] * 2,
        compiler_params=_cparams(32))(pos_col, invf, sgn)


def _mem_fwd(mem, g_mem, wkv16):
    def body(m_ref, g_ref, w_ref, n16_ref, kv_ref):
        y, _, _ = _rms(m_ref[...], g_ref[...])
        y16 = y.astype(BF16)
        n16_ref[...] = y16
        kv_ref[...] = _dot(y16, w_ref[...]).astype(BF16)

    return pl.pallas_call(
        body, name="mem_fwd",
        out_shape=[jax.ShapeDtypeStruct((N_MEM, D), BF16), jax.ShapeDtypeStruct((N_MEM, 2 * XW), BF16)],
        compiler_params=pltpu.CompilerParams(vmem_limit_bytes=32 << 20))(mem, g_mem, wkv16)


def _in_proj(x, g, w16, cos, sins, shards):
    def body(x_ref, g_ref, w_ref, c_ref, s_ref, q_ref, kv_ref, bcu_ref, qx_ref, h_ref):
        y, _, _ = _rms(x_ref[...], g_ref[...])
        h = y.astype(BF16)
        h_ref[...] = h
        proj = _dot(h, w_ref[...])
        cos, sn = c_ref[...], s_ref[...]
        q, k = proj[:, 0:AW], proj[:, AW:2 * AW]
        q_ref[...] = (q * cos + _rot_half(q) * sn) * SCALE
        kv_ref[...] = _pack_pair(k * cos + _rot_half(k) * sn, proj[:, 2 * AW:3 * AW])
        bcu_ref[...] = proj[:, 3 * AW:3 * AW + 3 * CW]
        qx_ref[...] = (proj[:, 3 * AW + 3 * CW:] * SCALE).astype(BF16)

    def tile(w):
        return pl.BlockSpec((TQ, w), lambda i: (i, 0))

    return _call_with_gather(
        body, NT, shards, name="in_proj",
        in_specs=[tile(D), _const((1, D)), _const((D, PW)), tile(AW), tile(AW)],
        out_specs=[tile(AW), tile(AW), tile(3 * CW), tile(XW), tile(D)],
        out_shape=[jax.ShapeDtypeStruct((S, AW), F32)] * 2 + [
            jax.ShapeDtypeStruct((S, 3 * CW), F32), jax.ShapeDtypeStruct((S, XW), BF16),
            jax.ShapeDtypeStruct((S, D), BF16)],
        scratch_shapes=[], vmem_mb=56, args=(x, g, w16, cos, sins))


ATTN_PLANS = (("p1", 1, 128, 32), ("p4", 8, 64, 8), ("p16", 16, 128, 2))
PAD = 128
WIN = 256


ATTN_UNROLL = 8


def _fill_bias(tab, qblk, partner):
    qi = lax.broadcasted_iota(jnp.int32, (2 * qblk, WIN), 0) & (qblk - 1)
    kj = lax.broadcasted_iota(jnp.int32, (2 * qblk, WIN), 1)
    piece = kj >> (qblk.bit_length() - 1)
    kk = kj & (qblk - 1)
    prev = (piece & 1) == 0
    of_partner = piece >= 2
    for first in (0, 1):
        for par in (0, 1):
            lo = jnp.where(prev, (qblk if first else qi) + jnp.where(of_partner, par, 0), 0)
            hi = jnp.where(prev, qblk, qi + jnp.where(of_partner, par - 1, 0))
            tab[2 * first + par] = jnp.where((kk >= lo) & (kk <= hi), 0.0, NEG).astype(F32)


def _block_rows(g, qblk, nbc, partner):
    own = pl.ds(pl.multiple_of(PAD + g * qblk, qblk), qblk)
    first = ((g & (nbc - 1)) == 0).astype(jnp.int32)
    if partner:
        gp = jnp.bitwise_xor(g, 4 * nbc)
        wins = (pl.ds(pl.multiple_of(PAD + (g - 1) * qblk, qblk), 2 * qblk),
                pl.ds(pl.multiple_of(PAD + (gp - 1) * qblk, qblk), 2 * qblk))
        return own, wins, 2 * first + ((g >> ((4 * nbc).bit_length() - 1)) & 1)
    return own, (pl.ds(pl.multiple_of(PAD + (g - 1) * qblk, qblk), 2 * qblk),), 2 * first


def _pack_pair(lo, hi):
    lo_bits = lax.bitcast_convert_type(lo.astype(BF16).astype(F32), jnp.uint32) >> 16
    hi_bits = lax.bitcast_convert_type(hi.astype(BF16).astype(F32), jnp.uint32) & jnp.uint32(0xFFFF0000)
    return lax.bitcast_convert_type(hi_bits | lo_bits, F32)


def _unpack_pair(c):
    bits = lax.bitcast_convert_type(c, jnp.uint32)
    lo = lax.bitcast_convert_type(bits << 16, F32).astype(BF16)
    hi = lax.bitcast_convert_type(bits & jnp.uint32(0xFFFF0000), F32).astype(BF16)
    return lo, hi


def _window(ref, wins):
    parts = [ref[w, :] for w in wins]
    return parts[0] if len(parts) == 1 else jnp.concatenate(parts, axis=0)


def _stack_heads(t, lane):
    zero = jnp.zeros_like(t)
    return jnp.concatenate([jnp.where(lane < HEAD, t, zero), jnp.where(lane >= HEAD, t, zero)], axis=0)


def _unstack_heads(t2, lane):
    half = t2.shape[0] // 2
    return jnp.where(lane < HEAD, t2[0:half, :], t2[half:, :])


def _lanes_of(step):
    return pl.ds(pl.multiple_of(step * LANES, LANES), LANES)


def _whole_wait(buf, sem):
    whole = buf.at[pl.ds(PAD, S), :]
    return pltpu.make_async_copy(whole, whole, sem)


def _whole_waits(bufs, sems):
    return [_whole_wait(buf, sems.at[i]) for i, buf in enumerate(bufs)]


def _class_gather(views, bufs, sems, lanes):
    copies = []
    for i, (view, buf) in enumerate(zip(views, bufs)):
        if view.ndim == 2:
            copies.append(pltpu.make_async_copy(view.at[:, lanes], buf.at[pl.ds(PAD, S), :], sems.at[i]))
        else:
            per, n_cls = view.shape[0], view.shape[1]
            copies += [pltpu.make_async_copy(view.at[:, c, lanes], buf.at[pl.ds(PAD + c * per, per), :], sems.at[i])
                       for c in range(n_cls)]
    return copies


def _class_scatter(bufs, dsts, sems, lanes=None):
    copies = []
    for i, (buf, dst) in enumerate(zip(bufs, dsts)):
        if dst.ndim == 2:
            copies.append(pltpu.make_async_copy(buf.at[pl.ds(PAD, S), :], dst.at[:, lanes], sems.at[i]))
            continue
        per, n_cls = dst.shape[0], dst.shape[1]
        for c in range(n_cls):
            to = dst.at[:, c, :] if lanes is None else dst.at[:, c, lanes]
            copies.append(pltpu.make_async_copy(buf.at[pl.ds(PAD + c * per, per), :], to, sems.at[i]))
    return copies


def _start(copies):
    for cp in copies:
        cp.start()


def _wait(waits):
    for w in waits:
        w.wait()


def _attn_fwd(q, kvp, shards=()):
    views = [[a] + [a.reshape(S // n, n, AW) for _, n, _, _ in ATTN_PLANS[1:]] for a in (q, kvp)]
    flat = [views[a][p] for p in range(3) for a in range(2)]
    ng = len(shards)
    n_grid = AW // LANES

    def body(*refs):
        hbm = [refs[2 * p:2 * p + 2] for p in range(3)]
        refs = refs[6:]
        shard_refs, refs = refs[:ng], refs[ng:]
        y_ref, lt_ref = refs[0:2]
        whole_refs, refs = refs[2:2 + ng], refs[2 + ng:]
        bufs = [refs[2 * p:2 * p + 2] for p in range(3)]
        oc4, lc4, oc16, lc16, o4n, l4n, o16n, l16n, tab128, tab4, sem_in, sem_out = refs[6:18]
        step = pl.program_id(0)
        if ng:
            start_gather, finish_gather = _gather_steps(shard_refs, whole_refs, *refs[18:])
            pl.when(step == 0)(start_gather)
        now = [_class_gather(hbm[p], bufs[p], sem_in.at[p], _lanes_of(step)) for p in range(3)]
        nxt = [_class_gather(hbm[p], bufs[p], sem_in.at[p], _lanes_of(step + 1)) for p in range(3)]

        @pl.when(step == 0)
        def _():
            for p in range(3):
                _start(now[p])
                for b in bufs[p]:
                    b[0:PAD, :] = jnp.zeros((PAD, LANES), F32)
            _fill_bias(tab128, 128, False)
            _fill_bias(tab4, 64, True)

        def prefetch(p):
            pl.when(step + 1 < n_grid)(lambda: _start(nxt[p]))

        lane = lax.broadcasted_iota(jnp.int32, (1, LANES), 1)
        ones = jnp.ones((WIN, LANES), BF16)

        def run(plan, bq, bkv, tab, o_dst, l_dst, dst_pad):
            _, n_cls, qblk, nbc = plan
            partner = n_cls == 8

            def block(g, carry):
                own, wins, mask = _block_rows(g, qblk, nbc, partner)
                q2 = _stack_heads(bq[own, :].astype(BF16), lane)
                kw, vwin = _unpack_pair(_window(bkv, wins))
                vw = jnp.concatenate([vwin, ones], axis=1)
                s = _dot_nt(q2, kw) + tab[mask]
                m = jnp.max(s, axis=1, keepdims=True)
                oe = _dot(jnp.exp(s - m).astype(BF16), vw)
                den = oe[:, LANES:]
                dst = pl.ds(pl.multiple_of(dst_pad + g * qblk, qblk), qblk)
                o_dst[dst, :] = _unstack_heads(oe[:, 0:LANES] / den, lane)
                l_dst[dst, :] = _unstack_heads(m + jnp.log(den), lane)
                return carry
            lax.fori_loop(0, n_cls * nbc, block, 0, unroll=ATTN_UNROLL)

        _wait(_whole_waits(bufs[0], sem_in.at[0]))
        run(ATTN_PLANS[0], *bufs[0], tab128, y_ref, lt_ref, 0)
        prefetch(0)
        _wait(_whole_waits(bufs[1], sem_in.at[1]))
        run(ATTN_PLANS[1], *bufs[1], tab4, oc4, lc4, PAD)
        prefetch(1)
        _start(_class_scatter((oc4, lc4), (o4n, l4n), sem_out.at[0]))
        _wait(_whole_waits(bufs[2], sem_in.at[2]))
        run(ATTN_PLANS[2], *bufs[2], tab128, oc16, lc16, PAD)
        prefetch(2)
        _start(_class_scatter((oc16, lc16), (o16n, l16n), sem_out.at[1]))
        _wait(_whole_waits((oc4, lc4), sem_out.at[0]) + _whole_waits((oc16, lc16), sem_out.at[1]))

        for t in range(S // TQ):
            rows = pl.ds(t * TQ, TQ)
            r4, r16 = pl.ds(t * (TQ // 8), TQ // 8), pl.ds(t * (TQ // 16), TQ // 16)
            l0, l1, l2 = lt_ref[rows, :], l4n[r4, :, :].reshape(TQ, LANES), l16n[r16, :, :].reshape(TQ, LANES)
            lm = jnp.maximum(jnp.maximum(l0, l1), l2)
            e0, e1, e2 = jnp.exp(l0 - lm), jnp.exp(l1 - lm), jnp.exp(l2 - lm)
            den = e0 + e1 + e2
            y_ref[rows, :] = (e0 * y_ref[rows, :] + e1 * o4n[r4, :, :].reshape(TQ, LANES)
                              + e2 * o16n[r16, :, :].reshape(TQ, LANES)) / den
            lt_ref[rows, :] = lm + jnp.log(den)

        if ng:
            pl.when(step == n_grid - 1)(finish_gather)

    col = pl.BlockSpec((S, LANES), lambda h: (0, h))
    padded = pltpu.VMEM((PAD + S, LANES), F32)
    return pl.pallas_call(
        body, grid=(n_grid,), name="attn_fwd",
        in_specs=[ANY] * (6 + ng), out_specs=[col, col] + [ANY] * ng,
        out_shape=[jax.ShapeDtypeStruct((S, AW), F32)] * 2 + _gathered_shapes(shards),
        scratch_shapes=[padded] * 10 + [
            pltpu.VMEM((S // 8, 8, LANES), F32), pltpu.VMEM((S // 8, 8, LANES), F32),
            pltpu.VMEM((S // 16, 16, LANES), F32), pltpu.VMEM((S // 16, 16, LANES), F32),
            pltpu.VMEM((4, 256, WIN), F32), pltpu.VMEM((4, 128, WIN), F32),
            pltpu.SemaphoreType.DMA((3, 2)), pltpu.SemaphoreType.DMA((2, 2))]
        + (_gather_scratch(ng) if ng else []),
        compiler_params=_cparams(56))(*flat, *shards)


def _conv_taps(z, zprev, row):
    z1 = jnp.where(row == 0, zprev[7:8, :], pltpu.roll(z, 1, 0))
    z2 = jnp.where(row == 0, zprev[6:7, :], jnp.where(row == 1, zprev[7:8, :], pltpu.roll(z, 2, 0)))
    return z1, z2


def _xattn_scores(qm, km):
    s = _dot_nt(qm, km)
    m = jnp.max(s, axis=1, keepdims=True)
    e = jnp.exp(s - m)
    return e, jnp.sum(e, axis=1, keepdims=True)


def _mix_out(y_attn, bcu, qx16, kv16, cw8, g_attn, g_conv, g_x, g_post, wout16, x, shards):
    def body(ya_ref, bcu_ref, halo_ref, qx_ref, kv_ref, cw_ref, ga_ref, gc_ref, gx_ref, gp_ref, w_ref, x_ref,
             ypre_ref, y16_ref, y2_ref, x1_ref):
        i = pl.program_id(0)
        bcu = bcu_ref[...]
        b, c, u = bcu[:, 0:CW], bcu[:, CW:2 * CW], bcu[:, 2 * CW:]
        z = c * u
        halo = halo_ref[...]
        zprev = jnp.where(i > 0, halo[:, CW:2 * CW] * halo[:, 2 * CW:], 0.0)
        row = lax.broadcasted_iota(jnp.int32, z.shape, 0)
        z1, z2 = _conv_taps(z, zprev, row)
        cw = cw_ref[...]
        y_conv = b * (z2 * cw[0:1, :] + z1 * cw[1:2, :] + z * cw[2:3, :])

        qx = qx_ref[...]
        kv = kv_ref[...]
        km, vm = kv[:, 0:XW], kv[:, XW:]
        lane = lax.broadcasted_iota(jnp.int32, qx.shape, 1)
        y_x = jnp.zeros(qx.shape, F32)
        for h in range(XW // HEAD):
            hm = (lane >= h * HEAD) & (lane < (h + 1) * HEAD)
            e, l = _xattn_scores(jnp.where(hm, qx, jnp.zeros_like(qx)), km)
            y_x = jnp.where(hm, _dot(e.astype(BF16), vm) / l, y_x)

        y_attn = ya_ref[...]
        ypre_ref[:, 0:AW] = y_attn
        ypre_ref[:, AW:AW + CW] = y_conv
        ypre_ref[:, AW + CW:] = y_x
        y = jnp.concatenate([_rms(y_attn, ga_ref[...])[0], _rms(y_conv, gc_ref[...])[0],
                             _rms(y_x, gx_ref[...])[0]], axis=1).astype(BF16)
        y16_ref[...] = y
        y2 = _dot(y, w_ref[...])
        y2_ref[...] = y2
        x1_ref[...] = x_ref[...] + _rms(y2, gp_ref[...])[0]

    def tile(w):
        return pl.BlockSpec((TQ, w), lambda i: (i, 0))

    halo = pl.BlockSpec((SUBLANES, 3 * CW), lambda i: (jnp.maximum(i * (TQ // SUBLANES) - 1, 0), 0))
    return _call_with_gather(
        body, NT, shards, name="mix_out",
        in_specs=[tile(AW), tile(3 * CW), halo, tile(XW), _const((N_MEM, 2 * XW)), _const((SUBLANES, CW)),
                  _const((1, AW)), _const((1, CW)), _const((1, XW)), _const((1, D)), _const((D, D)), tile(D)],
        out_specs=[tile(D), tile(D), tile(D), tile(D)],
        out_shape=[jax.ShapeDtypeStruct((S, D), F32), jax.ShapeDtypeStruct((S, D), BF16),
                   jax.ShapeDtypeStruct((S, D), F32), jax.ShapeDtypeStruct((S, D), F32)],
        scratch_shapes=[], vmem_mb=56,
        args=(y_attn, bcu, bcu, qx16, kv16, cw8, g_attn, g_conv, g_x, g_post, wout16, x))


def _mlp(x1, tgt, g_pre, g_post, wup8, wdn16):
    tq = TQ_MLP

    def body(x1_ref, t_ref, g1_ref, g2_ref, wu_ref, wd_ref,
             a16_ref, du_ref, h2_ref, df2_ref, dx1_ref, loss_ref, dg_ref, a32):
        @pl.when(pl.program_id(0) == 0)
        def _():
            loss_ref[...] = jnp.zeros_like(loss_ref)
            dg_ref[...] = jnp.zeros_like(dg_ref)

        x1 = x1_ref[...]
        g1, g2 = g1_ref[...], g2_ref[...]
        y1, n1, r1 = _rms(x1, g1)
        h2 = y1.astype(BF16)
        h2_ref[...] = h2
        f2 = jnp.zeros((tq, D), F32)
        for j in range(N_DEV):
            cols = slice(j * FF_BLK, (j + 1) * FF_BLK)
            a = jnp.maximum(_dot(h2, wu_ref[j]), 0.0)
            a32[:, cols] = a
            a16_ref[:, cols] = a.astype(BF16)
            f2 = f2 + _dot((a * a).astype(BF16), wd_ref[cols, :])
        y2, n2, r2 = _rms(f2, g2)
        e = x1 + y2 - t_ref[...]
        sq = jnp.sum(jnp.sum(e * e, axis=1, keepdims=True), axis=0, keepdims=True)
        loss_ref[...] += jnp.broadcast_to(sq * (0.5 / D), loss_ref.shape)
        dout = e * (1.0 / D)
        df2, dg2 = _rms_bwd(dout, n2, r2, g2)
        df2_16 = df2.astype(BF16)
        df2_ref[...] = df2_16
        dh2 = jnp.zeros((tq, D), F32)
        for j in range(N_DEV):
            cols = slice(j * FF_BLK, (j + 1) * FF_BLK)
            du = (_dot_nt(df2_16, wd_ref[cols, :]) * (2.0 * a32[:, cols])).astype(BF16)
            du_ref[:, cols] = du
            dh2 = dh2 + _dot_nt(du, wu_ref[j])
        dx, dg1 = _rms_bwd(dh2, n1, r1, g1)
        dx1_ref[...] = dout + dx
        dg_ref[0:1, :] += dg2
        dg_ref[1:2, :] += dg1

    def tile(w):
        return pl.BlockSpec((tq, w), lambda i: (i, 0))

    return pl.pallas_call(
        body, grid=(S // tq,), name="mlp",
        in_specs=[tile(D), tile(D), _const((1, D)), _const((1, D)), _const((N_DEV, D, FF_BLK)), _const((FF, D))],
        out_specs=[tile(FF), tile(FF), tile(D), tile(D), tile(D), _acc((SUBLANES, LANES)), _acc((SUBLANES, D))],
        out_shape=[jax.ShapeDtypeStruct((S, FF), BF16), jax.ShapeDtypeStruct((S, FF), BF16),
                   jax.ShapeDtypeStruct((S, D), BF16), jax.ShapeDtypeStruct((S, D), BF16),
                   jax.ShapeDtypeStruct((S, D), F32), jax.ShapeDtypeStruct((SUBLANES, LANES), F32),
                   jax.ShapeDtypeStruct((SUBLANES, D), F32)],
        scratch_shapes=[pltpu.VMEM((tq, FF), F32)],
        compiler_params=_cparams(56))(x1, tgt, g_pre, g_post, wup8, wdn16)


def _mix_out_bwd(dx1, y2, ypre, ltot, head_ones, q, bcu, qx16, kv16, cw8, g_post, g_attn, g_conv, g_x, wout16):
    def body(dx1_ref, y2_ref, ypre_ref, lt_ref, e_ref, q_ref, bcu_ref, halo_ref, qx_ref, kv_ref, cw_ref, gp_ref,
             ga_ref, gc_ref, gx_ref, w_ref, dy2_ref, qdo_ref, ld_ref, dbcu_ref, dqx_ref, dgs_ref, dcw_ref, dkv_ref,
             carry):
        i = pl.program_id(0)

        @pl.when(i == 0)
        def _():
            dgs_ref[...] = jnp.zeros_like(dgs_ref)
            dcw_ref[...] = jnp.zeros_like(dcw_ref)
            dkv_ref[...] = jnp.zeros_like(dkv_ref)
            carry[...] = jnp.zeros_like(carry)

        gp = gp_ref[...]
        _, n, r = _rms(y2_ref[...], gp)
        dy2, dgp = _rms_bwd(dx1_ref[...], n, r, gp)
        dy2_16 = dy2.astype(BF16)
        dy2_ref[...] = dy2_16
        dy = _dot_nt(dy2_16, w_ref[...])

        ypre = ypre_ref[...]
        ga, gc, gx = ga_ref[...], gc_ref[...], gx_ref[...]
        _, na, ra = _rms(ypre[:, 0:AW], ga)
        dya, dga = _rms_bwd(dy[:, 0:AW], na, ra, ga)
        _, nc, rc = _rms(ypre[:, AW:AW + CW], gc)
        dyc, dgc = _rms_bwd(dy[:, AW:AW + CW], nc, rc, gc)
        y_x = ypre[:, AW + CW:]
        _, nx, rx = _rms(y_x, gx)
        dyx, dgx = _rms_bwd(dy[:, AW + CW:], nx, rx, gx)
        qdo_ref[...] = _pack_pair(q_ref[...], dya)
        prod = dya * ypre[:, 0:AW]
        hi = prod.astype(BF16)
        lo = (prod - hi.astype(F32)).astype(BF16)
        head_sum = _dot(hi, e_ref[...]) + _dot(lo, e_ref[...])
        lane_a = lax.broadcasted_iota(jnp.int32, prod.shape, 1)
        ld_ref[...] = jnp.where((lane_a % HEAD) < HEAD // 2, lt_ref[...], head_sum)
        dgs_ref[0:1, :] += dgp
        dgs_ref[1:2, :] += jnp.concatenate([dga, dgc, dgx], axis=1)

        bcu = bcu_ref[...]
        b, c, u = bcu[:, 0:CW], bcu[:, CW:2 * CW], bcu[:, 2 * CW:]
        z = c * u
        halo = halo_ref[...]
        zprev = jnp.where(i < NT - 1, halo[:, CW:2 * CW] * halo[:, 2 * CW:], 0.0)
        row = lax.broadcasted_iota(jnp.int32, z.shape, 0)
        z1, z2 = _conv_taps(z, zprev, row)
        cw = cw_ref[...]
        conv = z2 * cw[0:1, :] + z1 * cw[1:2, :] + z * cw[2:3, :]
        dconv = dyc * b
        nxt = carry[...]
        dn1 = jnp.where(row == TQ - 1, nxt[0:1, :], pltpu.roll(dconv, TQ - 1, 0))
        dn2 = jnp.where(row == TQ - 1, nxt[1:2, :], jnp.where(row == TQ - 2, nxt[0:1, :], pltpu.roll(dconv, TQ - 2, 0)))
        carry[...] = dconv[0:SUBLANES, :]
        dz = dconv * cw[2:3, :] + dn1 * cw[1:2, :] + dn2 * cw[0:1, :]
        dbcu_ref[:, 0:CW] = dyc * conv
        dbcu_ref[:, CW:2 * CW] = dz * u
        dbcu_ref[:, 2 * CW:] = dz * c
        dcw_ref[0:1, :] += jnp.sum(z2 * dconv, axis=0, keepdims=True)
        dcw_ref[1:2, :] += jnp.sum(z1 * dconv, axis=0, keepdims=True)
        dcw_ref[2:3, :] += jnp.sum(z * dconv, axis=0, keepdims=True)

        qx = qx_ref[...]
        kv = kv_ref[...]
        km, vm = kv[:, 0:XW], kv[:, XW:]
        lane = lax.broadcasted_iota(jnp.int32, qx.shape, 1)
        dqx = jnp.zeros(qx.shape, F32)
        dkm = jnp.zeros((N_MEM, XW), F32)
        dvm = jnp.zeros((N_MEM, XW), F32)
        for h in range(XW // HEAD):
            hm = (lane >= h * HEAD) & (lane < (h + 1) * HEAD)
            qm = jnp.where(hm, qx, jnp.zeros_like(qx))
            e, l = _xattn_scores(qm, km)
            p = e / l
            dom = jnp.where(hm, dyx, 0.0)
            do16 = dom.astype(BF16)
            dsum = jnp.sum(dom * y_x, axis=1, keepdims=True)
            ds = (p * (_dot_nt(do16, vm) - dsum)).astype(BF16)
            dqx = jnp.where(hm, _dot(ds, km), dqx)
            dkm = dkm + _dot_tn(ds, qm)
            dvm = dvm + _dot_tn(p.astype(BF16), do16)
        dqx_ref[...] = dqx * SCALE
        dkv_ref[:, 0:XW] += dkm
        dkv_ref[:, XW:] += dvm

    def tile(w):
        return pl.BlockSpec((TQ, w), lambda i: (NT - 1 - i, 0))

    halo = pl.BlockSpec((SUBLANES, 3 * CW), lambda i: (jnp.maximum((NT - 1 - i) * (TQ // SUBLANES) - 1, 0), 0))
    return pl.pallas_call(
        body, grid=(NT,), name="mix_out_bwd",
        in_specs=[tile(D), tile(D), tile(D), tile(AW), _const((AW, AW)), tile(AW), tile(3 * CW), halo, tile(XW),
                  _const((N_MEM, 2 * XW)), _const((SUBLANES, CW)), _const((1, D)), _const((1, AW)), _const((1, CW)),
                  _const((1, XW)), _const((D, D))],
        out_specs=[tile(D), tile(AW), tile(AW), tile(3 * CW), tile(XW), _acc((SUBLANES, D)), _acc((SUBLANES, CW)),
                   _acc((N_MEM, 2 * XW))],
        out_shape=[jax.ShapeDtypeStruct((S, D), BF16), jax.ShapeDtypeStruct((S, AW), F32),
                   jax.ShapeDtypeStruct((S, AW), F32),
                   jax.ShapeDtypeStruct((S, 3 * CW), F32), jax.ShapeDtypeStruct((S, XW), F32),
                   jax.ShapeDtypeStruct((SUBLANES, D), F32), jax.ShapeDtypeStruct((SUBLANES, CW), F32),
                   jax.ShapeDtypeStruct((N_MEM, 2 * XW), F32)],
        scratch_shapes=[pltpu.VMEM((SUBLANES, CW), F32)],
        compiler_params=_cparams(56))(dx1, y2, ypre, ltot, head_ones, q, bcu, bcu, qx16, kv16, cw8, g_post, g_attn,
                                      g_conv, g_x, wout16)


def _attn_bwd(qdo, kvp, ld, chip_sums=()):
    n_in = 3
    views = [[a] + [a.reshape(S // n, n, AW) for _, n, _, _ in ATTN_PLANS[1:]] for a in (qdo, kvp, ld)]
    flat = [views[a][p] for p in range(3) for a in range(n_in)]
    ns = len(chip_sums)
    n_grid = AW // LANES

    def body(*refs):
        hbm = [refs[n_in * p:n_in * p + n_in] for p in range(3)]
        refs = refs[3 * n_in:]
        sum_refs, refs = refs[:ns], refs[ns:]
        outs = [refs[3 * p:3 * p + 3] for p in range(3)]
        landed_refs, sc = refs[9:9 + ns], refs[9 + ns:]
        bufs = [sc[3 * p:3 * p + 3] for p in range(3)]
        res = [sc[9 + 3 * p:12 + 3 * p] for p in range(3)]
        tab128, tab4, sem_in, sem_out = sc[18:22]
        step = pl.program_id(0)
        if ns:
            start_chips, finish_chips = _chips_steps(sum_refs, landed_refs, *sc[22:])
            pl.when(step == 0)(start_chips)
        now = [_class_gather(hbm[p], bufs[p], sem_in.at[p], _lanes_of(step)) for p in range(3)]
        nxt = [_class_gather(hbm[p], bufs[p], sem_in.at[p], _lanes_of(step + 1)) for p in range(3)]

        @pl.when(step == 0)
        def _():
            for p in range(3):
                _start(now[p])
                for b in bufs[p]:
                    b[0:PAD, :] = jnp.zeros((PAD, LANES), F32)
            _fill_bias(tab128, 128, False)
            _fill_bias(tab4, 64, True)

        def prefetch(p):
            pl.when(step + 1 < n_grid)(lambda: _start(nxt[p]))

        for p in range(3):
            for b in res[p]:
                b[...] = jnp.zeros_like(b)
        lane = lax.broadcasted_iota(jnp.int32, (1, LANES), 1)

        def run(plan, plan_bufs, tab, dst):
            _, n_cls, qblk, nbc = plan
            partner = n_cls == 8
            bqdo, bkv, bld = plan_bufs
            rq, rk, rv = dst

            def block(g, carry):
                own, wins, mask = _block_rows(g, qblk, nbc, partner)
                qb, dob = _unpack_pair(bqdo[own, :])
                q2, do2 = _stack_heads(qb, lane), _stack_heads(dob, lane)
                kw, vw = _unpack_pair(_window(bkv, wins))
                ldv = bld[own, :]
                half = HEAD // 2
                lt2 = jnp.concatenate([ldv[:, 0:1], ldv[:, HEAD:HEAD + 1]], axis=0)
                dsum2 = jnp.concatenate([ldv[:, half:half + 1], ldv[:, HEAD + half:HEAD + half + 1]], axis=0)
                p = jnp.exp(_dot_nt(q2, kw) + tab[mask] - lt2)
                ds = (p * (_dot_nt(do2, vw) - dsum2)).astype(BF16)
                rq[own, :] = _unstack_heads(_dot(ds, kw), lane)
                dkw = _dot_tn(ds, q2)
                dvw = _dot_tn(p.astype(BF16), do2)
                n_w = WIN // len(wins)
                for i, w in enumerate(wins):
                    rk[w, :] += dkw[i * n_w:(i + 1) * n_w, :]
                    rv[w, :] += dvw[i * n_w:(i + 1) * n_w, :]
                return carry
            lax.fori_loop(0, n_cls * nbc, block, 0, unroll=ATTN_UNROLL)

        tabs = (tab128, tab4, tab128)
        for p in range(3):
            _wait(_whole_waits(bufs[p], sem_in.at[p]))
            run(ATTN_PLANS[p], bufs[p], tabs[p], res[p])
            prefetch(p)
            _start(_class_scatter(res[p], outs[p], sem_out.at[p], _lanes_of(step)))
        for p in range(3):
            _wait(_whole_waits(res[p], sem_out.at[p]))
        if ns:
            pl.when(step == n_grid - 1)(finish_chips)

    padded = pltpu.VMEM((PAD + S, LANES), F32)
    shapes = [jax.ShapeDtypeStruct(views[0][p].shape, F32) for p in range(3) for _ in range(3)]
    out = pl.pallas_call(
        body, grid=(n_grid,), name="attn_bwd",
        in_specs=[ANY] * (3 * n_in + ns), out_specs=[ANY] * (9 + ns),
        out_shape=shapes + _chips_shapes(chip_sums),
        scratch_shapes=[padded] * 18
        + [pltpu.VMEM((4, 256, WIN), F32), pltpu.VMEM((4, 128, WIN), F32),
           pltpu.SemaphoreType.DMA((3, n_in)), pltpu.SemaphoreType.DMA((3, 3))]
        + (_chips_scratch(ns) if ns else []),
        compiler_params=_cparams(56))(*flat, *chip_sums)
    return [o.reshape(S, AW) for o in out[:9]] + list(out[9:])


def _in_proj_bwd(dqkv, dbcu, dqx, cos, sins, w16, x, g, dx1):
    tq = TQ // 2

    def body(*refs):
        parts = refs[0:9]
        dbcu_ref, dqx_ref, c_ref, s_ref, w_ref, x_ref, g_ref, dx1_ref, dp_ref, gx_ref, dg_ref = refs[9:]

        @pl.when(pl.program_id(0) == 0)
        def _():
            dg_ref[...] = jnp.zeros_like(dg_ref)

        dq, dk, dv = (parts[i][...] + parts[3 + i][...] + parts[6 + i][...] for i in range(3))
        cos, sn = c_ref[...], s_ref[...]
        dqr = dq * SCALE
        dkr = dk
        dp = jnp.concatenate([dqr * cos + _rot_half(dqr * sn), dkr * cos + _rot_half(dkr * sn), dv,
                              dbcu_ref[...], dqx_ref[...]], axis=1).astype(BF16)
        dp_ref[...] = dp
        dh = _dot_nt(dp, w_ref[...])
        g = g_ref[...]
        _, n, r = _rms(x_ref[...], g)
        dx, dg = _rms_bwd(dh, n, r, g)
        gx_ref[...] = dx1_ref[...] + dx
        dg_ref[0:1, :] += dg

    def tile(w):
        return pl.BlockSpec((tq, w), lambda i: (i, 0))

    return pl.pallas_call(
        body, grid=(S // tq,), name="in_proj_bwd",
        in_specs=[tile(AW)] * 9 + [tile(3 * CW), tile(XW), tile(AW), tile(AW), _const((D, PW)),
                                   tile(D), _const((1, D)), tile(D)],
        out_specs=[tile(PW), tile(D), _acc((SUBLANES, D))],
        out_shape=[jax.ShapeDtypeStruct((S, PW), BF16), jax.ShapeDtypeStruct((S, D), F32),
                   jax.ShapeDtypeStruct((SUBLANES, D), F32)],
        compiler_params=_cparams(56))(*dqkv, dbcu, dqx, cos, sins, w16, x, g, dx1)


def _mem_bwd(mem, g_mem, wkv16, dkv):
    def body(m_ref, g_ref, w_ref, dkv_ref, dkv16_ref, dg_ref):
        dkv16 = dkv_ref[...].astype(BF16)
        dkv16_ref[...] = dkv16
        _, n, _ = _rms(m_ref[...], g_ref[...])
        dg = jnp.sum(_dot_nt(dkv16, w_ref[...]) * n, axis=0, keepdims=True)
        dg_ref[...] = jnp.broadcast_to(dg, dg_ref.shape)

    return pl.pallas_call(
        body, name="mem_bwd",
        out_shape=[jax.ShapeDtypeStruct((N_MEM, 2 * XW), BF16), jax.ShapeDtypeStruct((SUBLANES, D), F32)],
        compiler_params=pltpu.CompilerParams(vmem_limit_bytes=32 << 20))(mem, g_mem, wkv16, dkv)


def _wgrad(a16, b16, tn, name, square_b=False, transpose_out=False):
    kk, m = a16.shape
    n_tiles = b16.shape[1] // tn
    chunk = min(kk, 512)
    oshape = (tn, m) if transpose_out else (m, tn)

    def body(a_ref, b_ref, o32_ref, o16_ref, at):
        @pl.when(pl.program_id(0) == 0)
        def _():
            for c in range(kk // chunk):
                at[:, c * chunk:(c + 1) * chunk] = a_ref[c * chunk:(c + 1) * chunk, :].T

        b = b_ref[...]
        if square_b:
            b = b * b
        acc = _dot(at[...], b)
        if transpose_out:
            acc = acc.T
        o32_ref[0] = acc
        o16_ref[0] = acc.astype(BF16)

    oblk = pl.BlockSpec((1,) + oshape, lambda j: (j, 0, 0))
    return pl.pallas_call(
        body, grid=(n_tiles,), name=name,
        in_specs=[_const((kk, m)), pl.BlockSpec((kk, tn), lambda j: (0, j))],
        out_specs=[oblk, oblk],
        out_shape=[jax.ShapeDtypeStruct((n_tiles,) + oshape, F32), jax.ShapeDtypeStruct((n_tiles,) + oshape, BF16)],
        scratch_shapes=[pltpu.VMEM((m, kk), BF16)],
        compiler_params=_cparams(56))(a16, b16)


def _adamw_math(w, g, m, v):
    m = ADAM_B1 * m + (1.0 - ADAM_B1) * g
    v = ADAM_B2 * v + (1.0 - ADAM_B2) * jnp.square(g)
    m_hat = m / (1.0 - ADAM_B1 ** ADAM_STEP)
    v_hat = v / (1.0 - ADAM_B2 ** ADAM_STEP)
    delta = -ADAM_LR * (m_hat / (jnp.sqrt(v_hat) + ADAM_EPS) + ADAM_WD * w)
    return delta, m, v


def _adamw_shards(updates, name, chip_sums=()):
    names, nu, ns = list(updates), len(updates), len(chip_sums)

    def body(*refs):
        ins, sum_refs = refs[:5 * nu], refs[5 * nu:5 * nu + ns]
        outs = refs[5 * nu + ns:9 * nu + ns]
        landed_refs, scratch = refs[9 * nu + ns:9 * nu + 2 * ns], refs[9 * nu + 2 * ns:]
        if ns:
            start_chips, finish_chips = _chips_steps(sum_refs, landed_refs, *scratch)
            start_chips()
        for i in range(nu):
            o_ref, r_ref, w_ref, m_ref, v_ref = ins[5 * i:5 * i + 5]
            g_out, d_out, m_out, v_out = outs[4 * i:4 * i + 4]
            g = o_ref[...] + r_ref[0].astype(F32) + r_ref[1].astype(F32) + r_ref[2].astype(F32)
            g_out[...] = g
            d_out[...], m_out[...], v_out[...] = _adamw_math(w_ref[...], g, m_ref[...], v_ref[...])
        if ns:
            finish_chips()

    vmem = pl.BlockSpec(memory_space=pltpu.VMEM)
    out = pl.pallas_call(
        body, name=name,
        in_specs=[vmem] * (5 * nu) + [ANY] * ns, out_specs=[vmem] * (4 * nu) + [ANY] * ns,
        out_shape=[jax.ShapeDtypeStruct(updates[n][2].shape, F32) for n in names for _ in range(4)]
        + _chips_shapes(chip_sums),
        scratch_shapes=_chips_scratch(ns) if ns else [],
        compiler_params=pltpu.CompilerParams(vmem_limit_bytes=56 << 20),
    )(*[a for n in names for a in updates[n]], *chip_sums)
    return {n: out[4 * i:4 * i + 4] for i, n in enumerate(names)}, list(out[4 * nu:])


def _place():
    x, y, c = lax.axis_index("x"), lax.axis_index("y"), lax.axis_index("c")
    chips = [(1 - x, y), (x, 1 - y), (1 - x, 1 - y)]
    return x, y, c, chips


def _gather_steps(ins, outs, send, recv, lsem):
    nt = len(ins)
    x, y, c, chips = _place()
    me, sib = (x, y, c), (x, y, 1 - c)

    def slot(t, px, py, pc):
        return outs[t].at[4 * px + 2 * py + pc]

    def copy(t, k, block, to, src=None):
        return pltpu.make_async_remote_copy(
            src_ref=slot(t, *block) if src is None else src, dst_ref=slot(t, *block),
            send_sem=send.at[t, k], recv_sem=recv.at[t, k], device_id=to, device_id_type=MESH)

    mine = [pltpu.make_async_copy(ins[t], slot(t, *me), lsem.at[t]) for t in range(nt)]
    first = []
    for t in range(nt):
        first.append(copy(t, 0, me, sib, src=ins[t]))
        first += [copy(t, 1 + j, me, (*chip, c), src=ins[t]) for j, chip in enumerate(chips)]

    def start():
        for cp in mine + first:
            cp.start()

    def finish():
        passed = []
        for j, chip in enumerate(chips):
            for t in range(nt):
                copy(t, 1 + j, (*chip, c), me).wait_recv()
                fwd = copy(t, 4 + j, (*chip, c), sib)
                fwd.start()
                passed.append(fwd)
        for t in range(nt):
            copy(t, 0, sib, me).wait_recv()
            for j, chip in enumerate(chips):
                copy(t, 4 + j, (*chip, 1 - c), me).wait_recv()
        for cp in first + passed:
            cp.wait_send()
        for cp in mine:
            cp.wait()

    return start, finish


def _gather_scratch(nt):
    return [pltpu.SemaphoreType.DMA((nt, 7)), pltpu.SemaphoreType.DMA((nt, 7)), pltpu.SemaphoreType.DMA((nt,))]


def _gathered_shapes(shards):
    return [jax.ShapeDtypeStruct((N_DEV,) + s.shape, s.dtype) for s in shards]


def _call_with_gather(body, n_grid, shards, *, name, in_specs, out_specs, out_shape, scratch_shapes, vmem_mb, args):
    ng, n_in, n_out = len(shards), len(in_specs), len(out_specs)

    def wrapped(*refs):
        ins, shard_refs = refs[:n_in], refs[n_in:n_in + ng]
        outs = refs[n_in + ng:n_in + ng + n_out]
        whole_refs = refs[n_in + ng + n_out:n_in + 2 * ng + n_out]
        scratch = refs[n_in + 2 * ng + n_out:]
        start, finish = _gather_steps(shard_refs, whole_refs, *scratch[len(scratch_shapes):])
        pl.when(pl.program_id(0) == 0)(start)
        body(*ins, *outs, *scratch[:len(scratch_shapes)])
        pl.when(pl.program_id(0) == n_grid - 1)(finish)

    return pl.pallas_call(
        wrapped, grid=(n_grid,), name=name,
        in_specs=list(in_specs) + [ANY] * ng, out_specs=list(out_specs) + [ANY] * ng,
        out_shape=list(out_shape) + _gathered_shapes(shards),
        scratch_shapes=list(scratch_shapes) + _gather_scratch(ng),
        compiler_params=_cparams(vmem_mb))(*args, *shards)


def _all_gather(shards):
    nt = len(shards)

    def body(*refs):
        start, finish = _gather_steps(refs[:nt], refs[nt:2 * nt], *refs[2 * nt:])
        start()
        finish()

    return pl.pallas_call(
        body, name="all_gather_weights", in_specs=[ANY] * nt, out_specs=[ANY] * nt,
        out_shape=_gathered_shapes(shards), scratch_shapes=_gather_scratch(nt))(*shards)


def _rs_pair(g16s, name):
    nt = len(g16s)

    def body(*refs):
        ins, outs = refs[:nt], refs[nt:2 * nt]
        send, recv = refs[2 * nt:]
        x, y, c, _ = _place()
        copies = [pltpu.make_async_remote_copy(
            src_ref=ins[t].at[2 * p + (1 - c)], dst_ref=outs[t].at[p], send_sem=send.at[t, p], recv_sem=recv.at[t, p],
            device_id=(x, y, 1 - c), device_id_type=MESH) for t in range(nt) for p in range(4)]
        for cp in copies:
            cp.start()
        for cp in copies:
            cp.wait()

    return pl.pallas_call(
        body, name=name,
        in_specs=[ANY] * nt, out_specs=[ANY] * nt,
        out_shape=[jax.ShapeDtypeStruct((4,) + g.shape[1:], g.dtype) for g in g16s],
        scratch_shapes=[pltpu.SemaphoreType.DMA((nt, 4)), pltpu.SemaphoreType.DMA((nt, 4))])(*g16s)


def _rs_pair_add(place, g32, ra16, name):
    shp = g32.shape[1:]

    def body(pl_ref, g_ref, r_ref, cs_ref, own_ref):
        s = g_ref[0] + r_ref[0].astype(F32)
        cs_ref[0] = s.astype(BF16)

        @pl.when(pl.program_id(0) == pl_ref[1])
        def _():
            own_ref[...] = s

    blk = (1,) + shp
    return pl.pallas_call(
        body, name=name,
        grid_spec=pltpu.PrefetchScalarGridSpec(
            num_scalar_prefetch=1, grid=(4,),
            in_specs=[pl.BlockSpec(blk, lambda p, s: (2 * p + s[0], 0, 0)), pl.BlockSpec(blk, lambda p, s: (p, 0, 0))],
            out_specs=[pl.BlockSpec(blk, lambda p, s: (p, 0, 0)), pl.BlockSpec(shp, lambda p, s: (0, 0))]),
        out_shape=[jax.ShapeDtypeStruct((4,) + shp, BF16), jax.ShapeDtypeStruct(shp, F32)],
        compiler_params=_cparams(48))(place, g32, ra16)


def _chips_steps(ins, outs, send, recv):
    _, _, c, chips = _place()
    copies = [pltpu.make_async_remote_copy(
        src_ref=ins[t].at[2 * px + py], dst_ref=outs[t].at[j], send_sem=send.at[t, j], recv_sem=recv.at[t, j],
        device_id=(px, py, c), device_id_type=MESH) for t in range(len(ins)) for j, (px, py) in enumerate(chips)]

    def start():
        for cp in copies:
            cp.start()

    def finish():
        for cp in copies:
            cp.wait()

    return start, finish


def _chips_scratch(nt):
    return [pltpu.SemaphoreType.DMA((nt, 3)), pltpu.SemaphoreType.DMA((nt, 3))]


def _chips_shapes(cs16s):
    return [jax.ShapeDtypeStruct((3,) + g.shape[1:], g.dtype) for g in cs16s]


def _rs_chips(cs16s):
    nt = len(cs16s)

    def body(*refs):
        start, finish = _chips_steps(refs[:nt], refs[nt:2 * nt], *refs[2 * nt:])
        start()
        finish()

    return pl.pallas_call(
        body, name="reduce_scatter_chips", in_specs=[ANY] * nt, out_specs=[ANY] * nt,
        out_shape=_chips_shapes(cs16s), scratch_shapes=_chips_scratch(nt))(*cs16s)


SMALL = (("g_pre_mix", 0, 0, D), ("g_mem", 1, 0, D), ("g_post_mix", 2, 0, D), ("g_attn_out", 3, 0, AW),
         ("g_conv_out", 3, AW, CW), ("g_xattn_out", 3, AW + CW, XW), ("g_post_mlp", 4, 0, D), ("g_pre_mlp", 5, 0, D))
CONV_ROW = 8
PACK_ROWS = 16


LOSS_ROW = 15


def _small_params_step(dg_in, dg_mem, dgs, dg_mlp, dcw, loss8, params):
    flat = [a for n, _, _, _ in SMALL for a in params[n]] + list(params["conv_w"])
    n_par = len(SMALL) + 1
    tap_cols = CW // N_DEV

    def body(*refs):
        acc_in, acc_mem, acc_mix, acc_mlp, acc_cw, acc_loss = refs[0:6]
        refs = refs[6:]
        ins = refs[0:3 * n_par]
        loss_out = refs[3 * n_par]
        outs = refs[3 * n_par + 1:7 * n_par + 1]
        pack, land, send, recv = refs[7 * n_par + 1:]
        x, y, c, _ = _place()
        me = 4 * x + 2 * y + c
        pack[...] = jnp.zeros_like(pack)
        pack[0:1, :] = acc_in[0:1, :]
        pack[1:2, :] = acc_mem[0:1, :]
        pack[2:4, :] = acc_mix[0:2, :]
        pack[4:6, :] = acc_mlp[0:2, :]
        pack[CONV_ROW:CONV_ROW + 3, 0:CW] = acc_cw[0:3, :]
        pack[LOSS_ROW:LOSS_ROW + 1, 0:LANES] = acc_loss[0:1, :]
        land[me] = pack[...]
        copies = []
        for k in range(1, N_DEV):
            kx, ky, kc = (k >> 2) & 1, (k >> 1) & 1, k & 1
            peer = (1 - x if kx else x, 1 - y if ky else y, 1 - c if kc else c)
            copies.append(pltpu.make_async_remote_copy(
                src_ref=pack, dst_ref=land.at[me], send_sem=send.at[k - 1], recv_sem=recv.at[k - 1],
                device_id=peer, device_id_type=MESH))
        for cp in copies:
            cp.start()
        for cp in copies:
            cp.wait()
        tot = land[0]
        for s in range(1, N_DEV):
            tot = tot + land[s]
        loss_out[...] = jnp.broadcast_to(tot[LOSS_ROW:LOSS_ROW + 1, 0:LANES], loss_out.shape)

        def update(i, g):
            w_ref, m_ref, v_ref = ins[3 * i:3 * i + 3]
            g_out, d_out, m_out, v_out = outs[4 * i:4 * i + 4]
            g_out[...] = g
            d_out[...], m_out[...], v_out[...] = _adamw_math(w_ref[...], g, m_ref[...], v_ref[...])

        for i, (_, row, lane0, width) in enumerate(SMALL):
            update(i, tot[row:row + 1, lane0:lane0 + width])
        taps = pltpu.roll(tot[CONV_ROW:CONV_ROW + SUBLANES, 0:CW], jnp.where(me == 0, 0, CW - me * tap_cols), 1)
        update(n_par - 1, taps[0:3, 0:tap_cols])

    shapes = [jax.ShapeDtypeStruct(params[n][0].shape, F32) for n, _, _, _ in SMALL] + [
        jax.ShapeDtypeStruct(params["conv_w"][0].shape, F32)]
    loss, *out = pl.pallas_call(
        body, name="small_params_step",
        out_shape=[jax.ShapeDtypeStruct((SUBLANES, LANES), F32)] + [s for s in shapes for _ in range(4)],
        scratch_shapes=[pltpu.VMEM((PACK_ROWS, D), F32), pltpu.VMEM((N_DEV, PACK_ROWS, D), F32),
                        pltpu.SemaphoreType.DMA((N_DEV - 1,)), pltpu.SemaphoreType.DMA((N_DEV - 1,))],
    )(dg_in, dg_mem, dgs, dg_mlp, dcw, loss8, *flat)
    names = [n for n, _, _, _ in SMALL] + ["conv_w"]
    return loss[0, 0], {n: out[4 * i:4 * i + 4] for i, n in enumerate(names)}


def _reduce_to_chip_sums(place, grads):
    from_sib = _rs_pair([g16 for _, g16 in grads.values()], "reduce_scatter_pair_" + "_".join(grads))
    return {n: _rs_pair_add(place, g32, from_sib[t], "pair_add_" + n) for t, (n, (g32, _)) in enumerate(grads.items())}


def _local_step(x, mem, pos, gains, win16, shards, tgt, place):
    half = HEAD // 2
    inv_freq = jnp.float32(ROPE_THETA) ** (-(jnp.arange(half, dtype=F32) * 2.0 / HEAD))
    invf = jnp.tile(inv_freq, LANES // half)[None, :]
    sgn = jnp.tile(jnp.concatenate([-jnp.ones((half,), F32), jnp.ones((half,), F32)]), LANES // HEAD)[None, :]
    cos, sins = _rope_table(pos.astype(F32).reshape(S, 1), invf, sgn)

    q, kvp, bcu, qx16, h16, wout8, wkv8, conv8 = _in_proj(
        x, gains["g_pre_mix"], win16, cos, sins, [shards["w_out"], shards["w_mem_kv"], shards["conv_w"]])
    wout16, wkv16 = wout8.reshape(D, D), wkv8.reshape(D, 2 * XW)
    cw_full = conv8[:, 0:3, 0:CW // N_DEV].transpose(1, 0, 2).reshape(3, CW)
    cw8 = jnp.zeros((SUBLANES, CW), F32).at[0:3].set(cw_full)
    y_attn, ltot, wup8 = _attn_fwd(q, kvp, [shards["w_up"]])
    memn16, kv16 = _mem_fwd(mem, gains["g_mem"], wkv16)
    ypre, y16, y2, x1, wdn8 = _mix_out(y_attn, bcu, qx16, kv16, cw8, gains["g_attn_out"], gains["g_conv_out"],
                                       gains["g_xattn_out"], gains["g_post_mix"], wout16, x, [shards["w_down"]])
    wdn16 = wdn8.reshape(FF, D)
    a16, du16, h2_16, df2_16, dx1, loss8, dg_mlp = _mlp(x1, tgt, gains["g_pre_mlp"], gains["g_post_mlp"], wup8, wdn16)

    mlp_sums = _reduce_to_chip_sums(place, {
        "w_up": _wgrad(h2_16, du16, FF_BLK, "wgrad_up"),
        "w_down": _wgrad(df2_16, a16, FF_BLK, "wgrad_down", square_b=True, transpose_out=True)})

    head_id = jnp.arange(AW, dtype=jnp.int32) // HEAD
    head_ones = (head_id[:, None] == head_id[None, :]).astype(BF16)
    dy2_16, qdo, ld, dbcu, dqx, dgs, dcw, dkv = _mix_out_bwd(
        dx1, y2, ypre, ltot, head_ones, q, bcu, qx16, kv16, cw8, gains["g_post_mix"], gains["g_attn_out"],
        gains["g_conv_out"], gains["g_xattn_out"], wout16)
    dkv16, dg_mem = _mem_bwd(mem, gains["g_mem"], wkv16, dkv)
    sums = dict(mlp_sums, **_reduce_to_chip_sums(place, {
        "w_mem_kv": tuple(g.reshape(N_DEV, D // N_DEV, 2 * XW) for g in _wgrad(memn16, dkv16, 2 * XW, "wgrad_mem_kv")),
        "w_out": tuple(g.reshape(N_DEV, D // N_DEV, D) for g in _wgrad(y16, dy2_16, D, "wgrad_out"))}))
    out = _attn_bwd(qdo, kvp, ld, [s[0] for s in sums.values()])
    dqkv, landed = out[:9], out[9:]
    reduced = {n: (s[1], landed[t]) for t, (n, s) in enumerate(sums.items())}
    dproj16, grad_x, dg_in = _in_proj_bwd(dqkv, dbcu, dqx, cos, sins, win16, x, gains["g_pre_mix"], dx1)

    def by_owner_in(g):
        return g.transpose(1, 0, 2).reshape(D, N_DEV, PW // N_DEV).transpose(1, 0, 2)

    in_sums = _reduce_to_chip_sums(
        place, {"w_in": tuple(by_owner_in(g) for g in _wgrad(h16, dproj16, 512, "wgrad_in"))})["w_in"]
    return grad_x, reduced, in_sums, (dg_in, dg_mem, dgs, dg_mlp, dcw, loss8)


BIG = ("w_in", "w_mem_kv", "w_out", "w_up", "w_down")
ORDER = ("g_pre_mix", "g_mem", "w_in", "w_mem_kv", "conv_w", "g_attn_out", "g_conv_out", "g_xattn_out", "w_out",
         "g_post_mix", "g_pre_mlp", "w_up", "w_down", "g_post_mlp")


def kernel(x, mem, positions, g_pre_mix, g_mem, w_in, w_mem_kv, conv_w, g_attn_out, g_conv_out, g_xattn_out, w_out, g_post_mix, g_pre_mlp, w_up, w_down, g_post_mlp, loss_target, m_g_pre_mix, m_g_mem, m_w_in, m_w_mem_kv, m_conv_w, m_g_attn_out, m_g_conv_out, m_g_xattn_out, m_w_out, m_g_post_mix, m_g_pre_mlp, m_w_up, m_w_down, m_g_post_mlp, v_g_pre_mix, v_g_mem, v_w_in, v_w_mem_kv, v_conv_w, v_g_attn_out, v_g_conv_out, v_g_xattn_out, v_w_out, v_g_post_mix, v_g_pre_mlp, v_w_up, v_w_down, v_g_post_mlp):
    w = dict(g_pre_mix=g_pre_mix, g_mem=g_mem, w_in=w_in, w_mem_kv=w_mem_kv, conv_w=conv_w, g_attn_out=g_attn_out,
             g_conv_out=g_conv_out, g_xattn_out=g_xattn_out, w_out=w_out, g_post_mix=g_post_mix, g_pre_mlp=g_pre_mlp,
             w_up=w_up, w_down=w_down, g_post_mlp=g_post_mlp)
    mo = dict(g_pre_mix=m_g_pre_mix, g_mem=m_g_mem, w_in=m_w_in, w_mem_kv=m_w_mem_kv, conv_w=m_conv_w,
              g_attn_out=m_g_attn_out, g_conv_out=m_g_conv_out, g_xattn_out=m_g_xattn_out, w_out=m_w_out,
              g_post_mix=m_g_post_mix, g_pre_mlp=m_g_pre_mlp, w_up=m_w_up, w_down=m_w_down, g_post_mlp=m_g_post_mlp)
    vo = dict(g_pre_mix=v_g_pre_mix, g_mem=v_g_mem, w_in=v_w_in, w_mem_kv=v_w_mem_kv, conv_w=v_conv_w,
              g_attn_out=v_g_attn_out, g_conv_out=v_g_conv_out, g_xattn_out=v_g_xattn_out, w_out=v_w_out,
              g_post_mix=v_g_post_mix, g_pre_mlp=v_g_pre_mlp, w_up=v_w_up, w_down=v_w_down, g_post_mlp=v_g_post_mlp)

    xi, yi, ci = lax.axis_index("x"), lax.axis_index("y"), lax.axis_index("c")
    me = 4 * xi + 2 * yi + ci
    place = jnp.stack([ci, 2 * xi + yi]).astype(jnp.int32)

    win8, = _all_gather([w["w_in"][0].astype(BF16)])
    win16 = win8.transpose(1, 0, 2).reshape(D, PW)
    shards = {n: w[n][0].astype(BF16) for n in ("w_out", "w_up", "w_down", "w_mem_kv")}
    shards["conv_w"] = jnp.zeros((SUBLANES, LANES), F32).at[0:3, 0:CW // N_DEV].set(conv_w[0])

    gains = {n: w[n] for n, _, _, _ in SMALL}
    grad_x, reduced, in_sums, small_acc = _local_step(
        x[0], mem[0], positions[0], gains, win16, shards, loss_target[0], place)

    state = lambda n: (w[n][0], mo[n][0], vo[n][0])
    updated, in_chips = _adamw_shards({n: (*reduced[n], *state(n)) for n in reduced}, "adamw_shards", [in_sums[0]])
    updated.update(_adamw_shards({"w_in": (in_sums[1], in_chips[0], *state("w_in"))}, "adamw_w_in")[0])
    grad, delta, new_m, new_v = {}, {}, {}, {}
    for n, (g, d_, m_, v_) in updated.items():
        grad[n], delta[n], new_m[n], new_v[n] = g[None], d_[None], m_[None], v_[None]

    params = {n: (w[n], mo[n], vo[n]) for n, _, _, _ in SMALL}
    params["conv_w"] = (w["conv_w"][0], mo["conv_w"][0], vo["conv_w"][0])
    loss, small = _small_params_step(*small_acc, params)
    for n, (g, d_, m_, v_) in small.items():
        lead = (lambda a: a[None]) if n == "conv_w" else (lambda a: a)
        grad[n], delta[n], new_m[n], new_v[n] = lead(g), lead(d_), lead(m_), lead(v_)

    return (loss, grad_x[None], *[grad[n] for n in ORDER], *[delta[n] for n in ORDER],
            *[new_m[n] for n in ORDER], *[new_v[n] for n in ORDER])
```

```python
import functools

import numpy as np
import jax
import jax.numpy as jnp
from jax import lax
from jax.experimental import pallas as pl
from jax.experimental.pallas import tpu as pltpu

F32, BF16 = jnp.float32, jnp.bfloat16
MESH = pl.DeviceIdType.MESH
ANY = pl.BlockSpec(memory_space=pl.ANY)

N_DEV = 8
D = 1024
S = 4096
N_MEM = 256
HEAD = 64
AW, CW, XW = 512, 256, 256
PW = 3 * AW + 3 * CW + XW
FF = 4096
FF_BLK = FF // N_DEV
PATTERNS = ((128, 1), (512, 4), (2048, 16))
QB = 128
EPS = 1e-6
NEG = -1e30
SCALE = HEAD ** -0.5
ROPE_THETA = 10000.0
LANES = 128
SUBLANES = 8

ADAM_LR, ADAM_B1, ADAM_B2, ADAM_EPS, ADAM_WD, ADAM_STEP = 0.001, 0.9, 0.999, 1e-08, 0.01, 10

TQ = 512
TQ_MLP = 256
NT = S // TQ


def _cparams(vmem_mb, n_grid=1):
    return pltpu.CompilerParams(dimension_semantics=("arbitrary",) * n_grid, vmem_limit_bytes=vmem_mb << 20)


def _const(shape):
    nd = len(shape)
    return pl.BlockSpec(shape, lambda *_: (0,) * nd, pipeline_mode=pl.Buffered(1))


def _acc(shape):
    nd = len(shape)
    return pl.BlockSpec(shape, lambda *_: (0,) * nd)


def _dot(a, b):
    return jnp.dot(a, b, preferred_element_type=F32)


def _dot_nt(a, b):
    return lax.dot_general(a, b, (((1,), (1,)), ((), ())), preferred_element_type=F32)


def _dot_tn(a, b):
    return lax.dot_general(a, b, (((0,), (0,)), ((), ())), preferred_element_type=F32)


def _rms(x, g):
    r = lax.rsqrt(jnp.mean(x * x, axis=-1, keepdims=True) + EPS)
    n = x * r
    return n * g, n, r


def _rms_bwd(dy, n, r, g):
    dn = dy * g
    dx = r * (dn - n * jnp.mean(dn * n, axis=-1, keepdims=True))
    return dx, jnp.sum(dy * n, axis=0, keepdims=True)


def _rot_half(t):
    lane = lax.broadcasted_iota(jnp.int32, t.shape, 1)
    n = t.shape[1]
    return jnp.where((lane % HEAD) < HEAD // 2, pltpu.roll(t, n - HEAD // 2, 1), pltpu.roll(t, HEAD // 2, 1))


def _rope_table(pos_col, invf, sgn, shards):
    def body(p_ref, f_ref, s_ref, c_out, s_out):
        ang = p_ref[...] * f_ref[...]
        c_out[...] = jnp.tile(jnp.cos(ang), (1, AW // LANES))
        s_out[...] = jnp.tile(jnp.sin(ang) * s_ref[...], (1, AW // LANES))

    tile = pl.BlockSpec((TQ, AW), lambda i: (i, 0))
    return _call_with_gather(
        body, NT, shards, name="rope_table",
        in_specs=[pl.BlockSpec((TQ, 1), lambda i: (i, 0)), _const((1, LANES)), _const((1, LANES))],
        out_specs=[tile, tile], out_shape=[jax.ShapeDtypeStruct((S, AW), F32)] * 2,
        scratch_shapes=[], vmem_mb=32, args=(pos_col, invf, sgn))


def _mem_fwd(mem, g_mem, wkv16):
    def body(m_ref, g_ref, w_ref, n16_ref, kv_ref):
        y, _, _ = _rms(m_ref[...], g_ref[...])
        y16 = y.astype(BF16)
        n16_ref[...] = y16
        kv_ref[...] = _dot(y16, w_ref[...]).astype(BF16)

    return pl.pallas_call(
        body, name="mem_fwd",
        out_shape=[jax.ShapeDtypeStruct((N_MEM, D), BF16), jax.ShapeDtypeStruct((N_MEM, 2 * XW), BF16)],
        compiler_params=pltpu.CompilerParams(vmem_limit_bytes=32 << 20))(mem, g_mem, wkv16)


def _in_proj(x, g, w16, cos, sins, shards):
    def body(x_ref, g_ref, w_ref, c_ref, s_ref, q_ref, kv_ref, bcu_ref, qx_ref, h_ref):
        y, _, _ = _rms(x_ref[...], g_ref[...])
        h = y.astype(BF16)
        h_ref[...] = h
        proj = _dot(h, w_ref[...])
        cos, sn = c_ref[...], s_ref[...]
        q, k = proj[:, 0:AW], proj[:, AW:2 * AW]
        q_ref[...] = (q * cos + _rot_half(q) * sn) * SCALE
        kv_ref[...] = _pack_pair(k * cos + _rot_half(k) * sn, proj[:, 2 * AW:3 * AW])
        bcu_ref[...] = proj[:, 3 * AW:3 * AW + 3 * CW]
        qx_ref[...] = (proj[:, 3 * AW + 3 * CW:] * SCALE).astype(BF16)

    def tile(w):
        return pl.BlockSpec((TQ, w), lambda i: (i, 0))

    return _call_with_gather(
        body, NT, shards, name="in_proj",
        in_specs=[tile(D), _const((1, D)), _const((D, PW)), tile(AW), tile(AW)],
        out_specs=[tile(AW), tile(AW), tile(3 * CW), tile(XW), tile(D)],
        out_shape=[jax.ShapeDtypeStruct((S, AW), F32)] * 2 + [
            jax.ShapeDtypeStruct((S, 3 * CW), F32), jax.ShapeDtypeStruct((S, XW), BF16),
            jax.ShapeDtypeStruct((S, D), BF16)],
        scratch_shapes=[], vmem_mb=56, args=(x, g, w16, cos, sins))


ATTN_PLANS = (("p1", 1, 128, 32), ("p4", 8, 64, 8), ("p16", 16, 128, 2))
PAD = 128
WIN = 256


ATTN_UNROLL = 8


def _fill_bias(tab, qblk, partner):
    qi = lax.broadcasted_iota(jnp.int32, (2 * qblk, WIN), 0) & (qblk - 1)
    kj = lax.broadcasted_iota(jnp.int32, (2 * qblk, WIN), 1)
    piece = kj >> (qblk.bit_length() - 1)
    kk = kj & (qblk - 1)
    prev = (piece & 1) == 0
    of_partner = piece >= 2
    for first in (0, 1):
        for par in (0, 1):
            lo = jnp.where(prev, (qblk if first else qi) + jnp.where(of_partner, par, 0), 0)
            hi = jnp.where(prev, qblk, qi + jnp.where(of_partner, par - 1, 0))
            tab[2 * first + par] = jnp.where((kk >= lo) & (kk <= hi), 0.0, NEG).astype(F32)


def _block_rows(g, qblk, nbc, partner):
    own = pl.ds(pl.multiple_of(PAD + g * qblk, qblk), qblk)
    first = ((g & (nbc - 1)) == 0).astype(jnp.int32)
    if partner:
        gp = jnp.bitwise_xor(g, 4 * nbc)
        wins = (pl.ds(pl.multiple_of(PAD + (g - 1) * qblk, qblk), 2 * qblk),
                pl.ds(pl.multiple_of(PAD + (gp - 1) * qblk, qblk), 2 * qblk))
        return own, wins, 2 * first + ((g >> ((4 * nbc).bit_length() - 1)) & 1)
    return own, (pl.ds(pl.multiple_of(PAD + (g - 1) * qblk, qblk), 2 * qblk),), 2 * first


def _pack_pair(lo, hi):
    lo_bits = lax.bitcast_convert_type(lo.astype(BF16).astype(F32), jnp.uint32) >> 16
    hi_bits = lax.bitcast_convert_type(hi.astype(BF16).astype(F32), jnp.uint32) & jnp.uint32(0xFFFF0000)
    return lax.bitcast_convert_type(hi_bits | lo_bits, F32)


def _unpack_pair(c):
    bits = lax.bitcast_convert_type(c, jnp.uint32)
    lo = lax.bitcast_convert_type(bits << 16, F32).astype(BF16)
    hi = lax.bitcast_convert_type(bits & jnp.uint32(0xFFFF0000), F32).astype(BF16)
    return lo, hi


def _window(ref, wins):
    parts = [ref[w, :] for w in wins]
    return parts[0] if len(parts) == 1 else jnp.concatenate(parts, axis=0)


def _stack_heads(t, lane):
    zero = jnp.zeros_like(t)
    return jnp.concatenate([jnp.where(lane < HEAD, t, zero), jnp.where(lane >= HEAD, t, zero)], axis=0)


def _unstack_heads(t2, lane):
    half = t2.shape[0] // 2
    return jnp.where(lane < HEAD, t2[0:half, :], t2[half:, :])


def _lanes_of(step):
    return pl.ds(pl.multiple_of(step * LANES, LANES), LANES)


def _whole_wait(buf, sem):
    whole = buf.at[pl.ds(PAD, S), :]
    return pltpu.make_async_copy(whole, whole, sem)


def _whole_waits(bufs, sems):
    return [_whole_wait(buf, sems.at[i]) for i, buf in enumerate(bufs)]


def _class_gather(views, bufs, sems, lanes):
    copies = []
    for i, (view, buf) in enumerate(zip(views, bufs)):
        if view.ndim == 2:
            copies.append(pltpu.make_async_copy(view.at[:, lanes], buf.at[pl.ds(PAD, S), :], sems.at[i]))
        else:
            per, n_cls = view.shape[0], view.shape[1]
            copies += [pltpu.make_async_copy(view.at[:, c, lanes], buf.at[pl.ds(PAD + c * per, per), :], sems.at[i])
                       for c in range(n_cls)]
    return copies


def _class_scatter(bufs, dsts, sems, lanes=None):
    copies = []
    for i, (buf, dst) in enumerate(zip(bufs, dsts)):
        if dst.ndim == 2:
            copies.append(pltpu.make_async_copy(buf.at[pl.ds(PAD, S), :], dst.at[:, lanes], sems.at[i]))
            continue
        per, n_cls = dst.shape[0], dst.shape[1]
        for c in range(n_cls):
            to = dst.at[:, c, :] if lanes is None else dst.at[:, c, lanes]
            copies.append(pltpu.make_async_copy(buf.at[pl.ds(PAD + c * per, per), :], to, sems.at[i]))
    return copies


def _start(copies):
    for cp in copies:
        cp.start()


def _wait(waits):
    for w in waits:
        w.wait()


def _attn_fwd(q, kvp, shards=()):
    views = [[a] + [a.reshape(S // n, n, AW) for _, n, _, _ in ATTN_PLANS[1:]] for a in (q, kvp)]
    flat = [views[a][p] for p in range(3) for a in range(2)]
    ng = len(shards)
    n_grid = AW // LANES

    def body(*refs):
        hbm = [refs[2 * p:2 * p + 2] for p in range(3)]
        refs = refs[6:]
        shard_refs, refs = refs[:ng], refs[ng:]
        y_ref, lt_ref = refs[0:2]
        whole_refs, refs = refs[2:2 + ng], refs[2 + ng:]
        bufs = [refs[2 * p:2 * p + 2] for p in range(3)]
        oc4, lc4, oc16, lc16, o4n, l4n, o16n, l16n, tab128, tab4, sem_in, sem_out = refs[6:18]
        step = pl.program_id(0)
        if ng:
            start_gather, finish_gather = _gather_steps(shard_refs, whole_refs, *refs[18:])
            pl.when(step == 0)(start_gather)
        now = [_class_gather(hbm[p], bufs[p], sem_in.at[p], _lanes_of(step)) for p in range(3)]
        nxt = [_class_gather(hbm[p], bufs[p], sem_in.at[p], _lanes_of(step + 1)) for p in range(3)]

        @pl.when(step == 0)
        def _():
            for p in range(3):
                _start(now[p])
                for b in bufs[p]:
                    b[0:PAD, :] = jnp.zeros((PAD, LANES), F32)
            _fill_bias(tab128, 128, False)
            _fill_bias(tab4, 64, True)

        def prefetch(p):
            pl.when(step + 1 < n_grid)(lambda: _start(nxt[p]))

        lane = lax.broadcasted_iota(jnp.int32, (1, LANES), 1)
        ones = jnp.ones((WIN, LANES), BF16)

        def run(plan, bq, bkv, tab, o_dst, l_dst, dst_pad):
            _, n_cls, qblk, nbc = plan
            partner = n_cls == 8

            def block(g, carry):
                own, wins, mask = _block_rows(g, qblk, nbc, partner)
                q2 = _stack_heads(bq[own, :].astype(BF16), lane)
                kw, vwin = _unpack_pair(_window(bkv, wins))
                vw = jnp.concatenate([vwin, ones], axis=1)
                s = _dot_nt(q2, kw) + tab[mask]
                m = jnp.max(s, axis=1, keepdims=True)
                oe = _dot(jnp.exp(s - m).astype(BF16), vw)
                den = oe[:, LANES:]
                dst = pl.ds(pl.multiple_of(dst_pad + g * qblk, qblk), qblk)
                o_dst[dst, :] = _unstack_heads(oe[:, 0:LANES] / den, lane)
                l_dst[dst, :] = _unstack_heads(m + jnp.log(den), lane)
                return carry
            lax.fori_loop(0, n_cls * nbc, block, 0, unroll=ATTN_UNROLL)

        _wait(_whole_waits(bufs[0], sem_in.at[0]))
        run(ATTN_PLANS[0], *bufs[0], tab128, y_ref, lt_ref, 0)
        prefetch(0)
        _wait(_whole_waits(bufs[1], sem_in.at[1]))
        run(ATTN_PLANS[1], *bufs[1], tab4, oc4, lc4, PAD)
        prefetch(1)
        _start(_class_scatter((oc4, lc4), (o4n, l4n), sem_out.at[0]))
        _wait(_whole_waits(bufs[2], sem_in.at[2]))
        run(ATTN_PLANS[2], *bufs[2], tab128, oc16, lc16, PAD)
        prefetch(2)
        _start(_class_scatter((oc16, lc16), (o16n, l16n), sem_out.at[1]))
        _wait(_whole_waits((oc4, lc4), sem_out.at[0]) + _whole_waits((oc16, lc16), sem_out.at[1]))

        for t in range(S // TQ):
            rows = pl.ds(t * TQ, TQ)
            r4, r16 = pl.ds(t * (TQ // 8), TQ // 8), pl.ds(t * (TQ // 16), TQ // 16)
            l0, l1, l2 = lt_ref[rows, :], l4n[r4, :, :].reshape(TQ, LANES), l16n[r16, :, :].reshape(TQ, LANES)
            lm = jnp.maximum(jnp.maximum(l0, l1), l2)
            e0, e1, e2 = jnp.exp(l0 - lm), jnp.exp(l1 - lm), jnp.exp(l2 - lm)
            den = e0 + e1 + e2
            y_ref[rows, :] = (e0 * y_ref[rows, :] + e1 * o4n[r4, :, :].reshape(TQ, LANES)
                              + e2 * o16n[r16, :, :].reshape(TQ, LANES)) / den
            lt_ref[rows, :] = lm + jnp.log(den)

        if ng:
            pl.when(step == n_grid - 1)(finish_gather)

    col = pl.BlockSpec((S, LANES), lambda h: (0, h))
    padded = pltpu.VMEM((PAD + S, LANES), F32)
    return pl.pallas_call(
        body, grid=(n_grid,), name="attn_fwd",
        in_specs=[ANY] * (6 + ng), out_specs=[col, col] + [ANY] * ng,
        out_shape=[jax.ShapeDtypeStruct((S, AW), F32)] * 2 + _gathered_shapes(shards),
        scratch_shapes=[padded] * 10 + [
            pltpu.VMEM((S // 8, 8, LANES), F32), pltpu.VMEM((S // 8, 8, LANES), F32),
            pltpu.VMEM((S // 16, 16, LANES), F32), pltpu.VMEM((S // 16, 16, LANES), F32),
            pltpu.VMEM((4, 256, WIN), F32), pltpu.VMEM((4, 128, WIN), F32),
            pltpu.SemaphoreType.DMA((3, 2)), pltpu.SemaphoreType.DMA((2, 2))]
        + (_gather_scratch(ng) if ng else []),
        compiler_params=_cparams(56))(*flat, *shards)


def _conv_taps(z, zprev, row):
    z1 = jnp.where(row == 0, zprev[7:8, :], pltpu.roll(z, 1, 0))
    z2 = jnp.where(row == 0, zprev[6:7, :], jnp.where(row == 1, zprev[7:8, :], pltpu.roll(z, 2, 0)))
    return z1, z2


def _xattn_scores(qm, km):
    s = _dot_nt(qm, km)
    m = jnp.max(s, axis=1, keepdims=True)
    e = jnp.exp(s - m)
    return e, jnp.sum(e, axis=1, keepdims=True)


def _mix_out(y_attn, bcu, qx16, kv16, cw8, g_attn, g_conv, g_x, g_post, wout16, x, shards):
    def body(ya_ref, bcu_ref, halo_ref, qx_ref, kv_ref, cw_ref, ga_ref, gc_ref, gx_ref, gp_ref, w_ref, x_ref,
             ypre_ref, y16_ref, y2_ref, x1_ref):
        i = pl.program_id(0)
        bcu = bcu_ref[...]
        b, c, u = bcu[:, 0:CW], bcu[:, CW:2 * CW], bcu[:, 2 * CW:]
        z = c * u
        halo = halo_ref[...]
        zprev = jnp.where(i > 0, halo[:, CW:2 * CW] * halo[:, 2 * CW:], 0.0)
        row = lax.broadcasted_iota(jnp.int32, z.shape, 0)
        z1, z2 = _conv_taps(z, zprev, row)
        cw = cw_ref[...]
        y_conv = b * (z2 * cw[0:1, :] + z1 * cw[1:2, :] + z * cw[2:3, :])

        qx = qx_ref[...]
        kv = kv_ref[...]
        km, vm = kv[:, 0:XW], kv[:, XW:]
        lane = lax.broadcasted_iota(jnp.int32, qx.shape, 1)
        y_x = jnp.zeros(qx.shape, F32)
        for h in range(XW // HEAD):
            hm = (lane >= h * HEAD) & (lane < (h + 1) * HEAD)
            e, l = _xattn_scores(jnp.where(hm, qx, jnp.zeros_like(qx)), km)
            y_x = jnp.where(hm, _dot(e.astype(BF16), vm) / l, y_x)

        y_attn = ya_ref[...]
        ypre_ref[:, 0:AW] = y_attn
        ypre_ref[:, AW:AW + CW] = y_conv
        ypre_ref[:, AW + CW:] = y_x
        y = jnp.concatenate([_rms(y_attn, ga_ref[...])[0], _rms(y_conv, gc_ref[...])[0],
                             _rms(y_x, gx_ref[...])[0]], axis=1).astype(BF16)
        y16_ref[...] = y
        y2 = _dot(y, w_ref[...])
        y2_ref[...] = y2
        x1_ref[...] = x_ref[...] + _rms(y2, gp_ref[...])[0]

    def tile(w):
        return pl.BlockSpec((TQ, w), lambda i: (i, 0))

    halo = pl.BlockSpec((SUBLANES, 3 * CW), lambda i: (jnp.maximum(i * (TQ // SUBLANES) - 1, 0), 0))
    return _call_with_gather(
        body, NT, shards, name="mix_out",
        in_specs=[tile(AW), tile(3 * CW), halo, tile(XW), _const((N_MEM, 2 * XW)), _const((SUBLANES, CW)),
                  _const((1, AW)), _const((1, CW)), _const((1, XW)), _const((1, D)), _const((D, D)), tile(D)],
        out_specs=[tile(D), tile(D), tile(D), tile(D)],
        out_shape=[jax.ShapeDtypeStruct((S, D), F32), jax.ShapeDtypeStruct((S, D), BF16),
                   jax.ShapeDtypeStruct((S, D), F32), jax.ShapeDtypeStruct((S, D), F32)],
        scratch_shapes=[], vmem_mb=56,
        args=(y_attn, bcu, bcu, qx16, kv16, cw8, g_attn, g_conv, g_x, g_post, wout16, x))


def _mlp(x1, tgt, g_pre, g_post, wup8, wdn16):
    tq = TQ_MLP

    def body(x1_ref, t_ref, g1_ref, g2_ref, wu_ref, wd_ref,
             a16_ref, du_ref, h2_ref, df2_ref, dx1_ref, loss_ref, dg_ref, a32):
        @pl.when(pl.program_id(0) == 0)
        def _():
            loss_ref[...] = jnp.zeros_like(loss_ref)
            dg_ref[...] = jnp.zeros_like(dg_ref)

        x1 = x1_ref[...]
        g1, g2 = g1_ref[...], g2_ref[...]
        y1, n1, r1 = _rms(x1, g1)
        h2 = y1.astype(BF16)
        h2_ref[...] = h2
        f2 = jnp.zeros((tq, D), F32)
        for j in range(N_DEV):
            cols = slice(j * FF_BLK, (j + 1) * FF_BLK)
            a = jnp.maximum(_dot(h2, wu_ref[j]), 0.0)
            a32[:, cols] = a
            a16_ref[:, cols] = a.astype(BF16)
            f2 = f2 + _dot((a * a).astype(BF16), wd_ref[cols, :])
        y2, n2, r2 = _rms(f2, g2)
        e = x1 + y2 - t_ref[...]
        sq = jnp.sum(jnp.sum(e * e, axis=1, keepdims=True), axis=0, keepdims=True)
        loss_ref[...] += jnp.broadcast_to(sq * (0.5 / D), loss_ref.shape)
        dout = e * (1.0 / D)
        df2, dg2 = _rms_bwd(dout, n2, r2, g2)
        df2_16 = df2.astype(BF16)
        df2_ref[...] = df2_16
        dh2 = jnp.zeros((tq, D), F32)
        for j in range(N_DEV):
            cols = slice(j * FF_BLK, (j + 1) * FF_BLK)
            du = (_dot_nt(df2_16, wd_ref[cols, :]) * (2.0 * a32[:, cols])).astype(BF16)
            du_ref[:, cols] = du
            dh2 = dh2 + _dot_nt(du, wu_ref[j])
        dx, dg1 = _rms_bwd(dh2, n1, r1, g1)
        dx1_ref[...] = dout + dx
        dg_ref[0:1, :] += dg2
        dg_ref[1:2, :] += dg1

    def tile(w):
        return pl.BlockSpec((tq, w), lambda i: (i, 0))

    return pl.pallas_call(
        body, grid=(S // tq,), name="mlp",
        in_specs=[tile(D), tile(D), _const((1, D)), _const((1, D)), _const((N_DEV, D, FF_BLK)), _const((FF, D))],
        out_specs=[tile(FF), tile(FF), tile(D), tile(D), tile(D), _acc((SUBLANES, LANES)), _acc((SUBLANES, D))],
        out_shape=[jax.ShapeDtypeStruct((S, FF), BF16), jax.ShapeDtypeStruct((S, FF), BF16),
                   jax.ShapeDtypeStruct((S, D), BF16), jax.ShapeDtypeStruct((S, D), BF16),
                   jax.ShapeDtypeStruct((S, D), F32), jax.ShapeDtypeStruct((SUBLANES, LANES), F32),
                   jax.ShapeDtypeStruct((SUBLANES, D), F32)],
        scratch_shapes=[pltpu.VMEM((tq, FF), F32)],
        compiler_params=_cparams(56))(x1, tgt, g_pre, g_post, wup8, wdn16)


def _mix_out_bwd(dx1, y2, ypre, ltot, head_ones, q, bcu, qx16, kv16, cw8, g_post, g_attn, g_conv, g_x, wout16):
    def body(dx1_ref, y2_ref, ypre_ref, lt_ref, e_ref, q_ref, bcu_ref, halo_ref, qx_ref, kv_ref, cw_ref, gp_ref,
             ga_ref, gc_ref, gx_ref, w_ref, dy2_ref, qdo_ref, ld_ref, dbcu_ref, dqx_ref, dgs_ref, dcw_ref, dkv_ref,
             carry):
        i = pl.program_id(0)

        @pl.when(i == 0)
        def _():
            dgs_ref[...] = jnp.zeros_like(dgs_ref)
            dcw_ref[...] = jnp.zeros_like(dcw_ref)
            dkv_ref[...] = jnp.zeros_like(dkv_ref)
            carry[...] = jnp.zeros_like(carry)

        gp = gp_ref[...]
        _, n, r = _rms(y2_ref[...], gp)
        dy2, dgp = _rms_bwd(dx1_ref[...], n, r, gp)
        dy2_16 = dy2.astype(BF16)
        dy2_ref[...] = dy2_16
        dy = _dot_nt(dy2_16, w_ref[...])

        ypre = ypre_ref[...]
        ga, gc, gx = ga_ref[...], gc_ref[...], gx_ref[...]
        _, na, ra = _rms(ypre[:, 0:AW], ga)
        dya, dga = _rms_bwd(dy[:, 0:AW], na, ra, ga)
        _, nc, rc = _rms(ypre[:, AW:AW + CW], gc)
        dyc, dgc = _rms_bwd(dy[:, AW:AW + CW], nc, rc, gc)
        y_x = ypre[:, AW + CW:]
        _, nx, rx = _rms(y_x, gx)
        dyx, dgx = _rms_bwd(dy[:, AW + CW:], nx, rx, gx)
        qdo_ref[...] = _pack_pair(q_ref[...], dya)
        prod = dya * ypre[:, 0:AW]
        hi = prod.astype(BF16)
        lo = (prod - hi.astype(F32)).astype(BF16)
        head_sum = _dot(hi, e_ref[...]) + _dot(lo, e_ref[...])
        lane_a = lax.broadcasted_iota(jnp.int32, prod.shape, 1)
        ld_ref[...] = jnp.where((lane_a % HEAD) < HEAD // 2, lt_ref[...], head_sum)
        dgs_ref[0:1, :] += dgp
        dgs_ref[1:2, :] += jnp.concatenate([dga, dgc, dgx], axis=1)

        bcu = bcu_ref[...]
        b, c, u = bcu[:, 0:CW], bcu[:, CW:2 * CW], bcu[:, 2 * CW:]
        z = c * u
        halo = halo_ref[...]
        zprev = jnp.where(i < NT - 1, halo[:, CW:2 * CW] * halo[:, 2 * CW:], 0.0)
        row = lax.broadcasted_iota(jnp.int32, z.shape, 0)
        z1, z2 = _conv_taps(z, zprev, row)
        cw = cw_ref[...]
        conv = z2 * cw[0:1, :] + z1 * cw[1:2, :] + z * cw[2:3, :]
        dconv = dyc * b
        nxt = carry[...]
        dn1 = jnp.where(row == TQ - 1, nxt[0:1, :], pltpu.roll(dconv, TQ - 1, 0))
        dn2 = jnp.where(row == TQ - 1, nxt[1:2, :], jnp.where(row == TQ - 2, nxt[0:1, :], pltpu.roll(dconv, TQ - 2, 0)))
        carry[...] = dconv[0:SUBLANES, :]
        dz = dconv * cw[2:3, :] + dn1 * cw[1:2, :] + dn2 * cw[0:1, :]
        dbcu_ref[:, 0:CW] = dyc * conv
        dbcu_ref[:, CW:2 * CW] = dz * u
        dbcu_ref[:, 2 * CW:] = dz * c
        dcw_ref[0:1, :] += jnp.sum(z2 * dconv, axis=0, keepdims=True)
        dcw_ref[1:2, :] += jnp.sum(z1 * dconv, axis=0, keepdims=True)
        dcw_ref[2:3, :] += jnp.sum(z * dconv, axis=0, keepdims=True)

        qx = qx_ref[...]
        kv = kv_ref[...]
        km, vm = kv[:, 0:XW], kv[:, XW:]
        lane = lax.broadcasted_iota(jnp.int32, qx.shape, 1)
        dqx = jnp.zeros(qx.shape, F32)
        dkm = jnp.zeros((N_MEM, XW), F32)
        dvm = jnp.zeros((N_MEM, XW), F32)
        for h in range(XW // HEAD):
            hm = (lane >= h * HEAD) & (lane < (h + 1) * HEAD)
            qm = jnp.where(hm, qx, jnp.zeros_like(qx))
            e, l = _xattn_scores(qm, km)
            p = e / l
            dom = jnp.where(hm, dyx, 0.0)
            do16 = dom.astype(BF16)
            dsum = jnp.sum(dom * y_x, axis=1, keepdims=True)
            ds = (p * (_dot_nt(do16, vm) - dsum)).astype(BF16)
            dqx = jnp.where(hm, _dot(ds, km), dqx)
            dkm = dkm + _dot_tn(ds, qm)
            dvm = dvm + _dot_tn(p.astype(BF16), do16)
        dqx_ref[...] = dqx * SCALE
        dkv_ref[:, 0:XW] += dkm
        dkv_ref[:, XW:] += dvm

    def tile(w):
        return pl.BlockSpec((TQ, w), lambda i: (NT - 1 - i, 0))

    halo = pl.BlockSpec((SUBLANES, 3 * CW), lambda i: (jnp.maximum((NT - 1 - i) * (TQ // SUBLANES) - 1, 0), 0))
    return pl.pallas_call(
        body, grid=(NT,), name="mix_out_bwd",
        in_specs=[tile(D), tile(D), tile(D), tile(AW), _const((AW, AW)), tile(AW), tile(3 * CW), halo, tile(XW),
                  _const((N_MEM, 2 * XW)), _const((SUBLANES, CW)), _const((1, D)), _const((1, AW)), _const((1, CW)),
                  _const((1, XW)), _const((D, D))],
        out_specs=[tile(D), tile(AW), tile(AW), tile(3 * CW), tile(XW), _acc((SUBLANES, D)), _acc((SUBLANES, CW)),
                   _acc((N_MEM, 2 * XW))],
        out_shape=[jax.ShapeDtypeStruct((S, D), BF16), jax.ShapeDtypeStruct((S, AW), F32),
                   jax.ShapeDtypeStruct((S, AW), F32),
                   jax.ShapeDtypeStruct((S, 3 * CW), F32), jax.ShapeDtypeStruct((S, XW), F32),
                   jax.ShapeDtypeStruct((SUBLANES, D), F32), jax.ShapeDtypeStruct((SUBLANES, CW), F32),
                   jax.ShapeDtypeStruct((N_MEM, 2 * XW), F32)],
        scratch_shapes=[pltpu.VMEM((SUBLANES, CW), F32)],
        compiler_params=_cparams(56))(dx1, y2, ypre, ltot, head_ones, q, bcu, bcu, qx16, kv16, cw8, g_post, g_attn,
                                      g_conv, g_x, wout16)


def _attn_bwd(qdo, kvp, ld, chip_sums=()):
    n_in = 3
    views = [[a] + [a.reshape(S // n, n, AW) for _, n, _, _ in ATTN_PLANS[1:]] for a in (qdo, kvp, ld)]
    flat = [views[a][p] for p in range(3) for a in range(n_in)]
    ns = len(chip_sums)
    n_grid = AW // LANES

    def body(*refs):
        hbm = [refs[n_in * p:n_in * p + n_in] for p in range(3)]
        refs = refs[3 * n_in:]
        sum_refs, refs = refs[:ns], refs[ns:]
        outs = [refs[3 * p:3 * p + 3] for p in range(3)]
        landed_refs, sc = refs[9:9 + ns], refs[9 + ns:]
        bufs = [sc[3 * p:3 * p + 3] for p in range(3)]
        res = [sc[9 + 3 * p:12 + 3 * p] for p in range(3)]
        tab128, tab4, sem_in, sem_out = sc[18:22]
        step = pl.program_id(0)
        if ns:
            start_chips, finish_chips = _chips_steps(sum_refs, landed_refs, *sc[22:])
            pl.when(step == 0)(start_chips)
        now = [_class_gather(hbm[p], bufs[p], sem_in.at[p], _lanes_of(step)) for p in range(3)]
        nxt = [_class_gather(hbm[p], bufs[p], sem_in.at[p], _lanes_of(step + 1)) for p in range(3)]

        @pl.when(step == 0)
        def _():
            for p in range(3):
                _start(now[p])
                for b in bufs[p]:
                    b[0:PAD, :] = jnp.zeros((PAD, LANES), F32)
            _fill_bias(tab128, 128, False)
            _fill_bias(tab4, 64, True)

        def prefetch(p):
            pl.when(step + 1 < n_grid)(lambda: _start(nxt[p]))

        for p in range(3):
            for b in res[p]:
                b[...] = jnp.zeros_like(b)
        lane = lax.broadcasted_iota(jnp.int32, (1, LANES), 1)

        def run(plan, plan_bufs, tab, dst):
            _, n_cls, qblk, nbc = plan
            partner = n_cls == 8
            bqdo, bkv, bld = plan_bufs
            rq, rk, rv = dst

            def block(g, carry):
                own, wins, mask = _block_rows(g, qblk, nbc, partner)
                qb, dob = _unpack_pair(bqdo[own, :])
                q2, do2 = _stack_heads(qb, lane), _stack_heads(dob, lane)
                kw, vw = _unpack_pair(_window(bkv, wins))
                ldv = bld[own, :]
                half = HEAD // 2
                lt2 = jnp.concatenate([ldv[:, 0:1], ldv[:, HEAD:HEAD + 1]], axis=0)
                dsum2 = jnp.concatenate([ldv[:, half:half + 1], ldv[:, HEAD + half:HEAD + half + 1]], axis=0)
                p = jnp.exp(_dot_nt(q2, kw) + tab[mask] - lt2)
                ds = (p * (_dot_nt(do2, vw) - dsum2)).astype(BF16)
                rq[own, :] = _unstack_heads(_dot(ds, kw), lane)
                dkw = _dot_tn(ds, q2)
                dvw = _dot_tn(p.astype(BF16), do2)
                n_w = WIN // len(wins)
                for i, w in enumerate(wins):
                    rk[w, :] += dkw[i * n_w:(i + 1) * n_w, :]
                    rv[w, :] += dvw[i * n_w:(i + 1) * n_w, :]
                return carry
            lax.fori_loop(0, n_cls * nbc, block, 0, unroll=ATTN_UNROLL)

        tabs = (tab128, tab4, tab128)
        for p in range(3):
            _wait(_whole_waits(bufs[p], sem_in.at[p]))
            run(ATTN_PLANS[p], bufs[p], tabs[p], res[p])
            prefetch(p)
            _start(_class_scatter(res[p], outs[p], sem_out.at[p], _lanes_of(step)))
        for p in range(3):
            _wait(_whole_waits(res[p], sem_out.at[p]))
        if ns:
            pl.when(step == n_grid - 1)(finish_chips)

    padded = pltpu.VMEM((PAD + S, LANES), F32)
    shapes = [jax.ShapeDtypeStruct(views[0][p].shape, F32) for p in range(3) for _ in range(3)]
    out = pl.pallas_call(
        body, grid=(n_grid,), name="attn_bwd",
        in_specs=[ANY] * (3 * n_in + ns), out_specs=[ANY] * (9 + ns),
        out_shape=shapes + _chips_shapes(chip_sums),
        scratch_shapes=[padded] * 18
        + [pltpu.VMEM((4, 256, WIN), F32), pltpu.VMEM((4, 128, WIN), F32),
           pltpu.SemaphoreType.DMA((3, n_in)), pltpu.SemaphoreType.DMA((3, 3))]
        + (_chips_scratch(ns) if ns else []),
        compiler_params=_cparams(56))(*flat, *chip_sums)
    return [o.reshape(S, AW) for o in out[:9]] + list(out[9:])


def _in_proj_bwd(dqkv, dbcu, dqx, cos, sins, w16, x, g, dx1):
    tq = TQ // 2

    def body(*refs):
        parts = refs[0:9]
        dbcu_ref, dqx_ref, c_ref, s_ref, w_ref, x_ref, g_ref, dx1_ref, dp_ref, gx_ref, dg_ref = refs[9:]

        @pl.when(pl.program_id(0) == 0)
        def _():
            dg_ref[...] = jnp.zeros_like(dg_ref)

        dq, dk, dv = (parts[i][...] + parts[3 + i][...] + parts[6 + i][...] for i in range(3))
        cos, sn = c_ref[...], s_ref[...]
        dqr = dq * SCALE
        dkr = dk
        dp = jnp.concatenate([dqr * cos + _rot_half(dqr * sn), dkr * cos + _rot_half(dkr * sn), dv,
                              dbcu_ref[...], dqx_ref[...]], axis=1).astype(BF16)
        dp_ref[...] = dp
        dh = _dot_nt(dp, w_ref[...])
        g = g_ref[...]
        _, n, r = _rms(x_ref[...], g)
        dx, dg = _rms_bwd(dh, n, r, g)
        gx_ref[...] = dx1_ref[...] + dx
        dg_ref[0:1, :] += dg

    def tile(w):
        return pl.BlockSpec((tq, w), lambda i: (i, 0))

    return pl.pallas_call(
        body, grid=(S // tq,), name="in_proj_bwd",
        in_specs=[tile(AW)] * 9 + [tile(3 * CW), tile(XW), tile(AW), tile(AW), _const((D, PW)),
                                   tile(D), _const((1, D)), tile(D)],
        out_specs=[tile(PW), tile(D), _acc((SUBLANES, D))],
        out_shape=[jax.ShapeDtypeStruct((S, PW), BF16), jax.ShapeDtypeStruct((S, D), F32),
                   jax.ShapeDtypeStruct((SUBLANES, D), F32)],
        compiler_params=_cparams(56))(*dqkv, dbcu, dqx, cos, sins, w16, x, g, dx1)


def _mem_bwd(mem, g_mem, wkv16, dkv):
    def body(m_ref, g_ref, w_ref, dkv_ref, dkv16_ref, dg_ref):
        dkv16 = dkv_ref[...].astype(BF16)
        dkv16_ref[...] = dkv16
        _, n, _ = _rms(m_ref[...], g_ref[...])
        dg = jnp.sum(_dot_nt(dkv16, w_ref[...]) * n, axis=0, keepdims=True)
        dg_ref[...] = jnp.broadcast_to(dg, dg_ref.shape)

    return pl.pallas_call(
        body, name="mem_bwd",
        out_shape=[jax.ShapeDtypeStruct((N_MEM, 2 * XW), BF16), jax.ShapeDtypeStruct((SUBLANES, D), F32)],
        compiler_params=pltpu.CompilerParams(vmem_limit_bytes=32 << 20))(mem, g_mem, wkv16, dkv)


def _wgrad(a16, b16, tn, name, square_b=False, transpose_out=False):
    kk, m = a16.shape
    n_tiles = b16.shape[1] // tn
    chunk = min(kk, 512)
    oshape = (tn, m) if transpose_out else (m, tn)

    def body(a_ref, b_ref, o32_ref, o16_ref, at):
        @pl.when(pl.program_id(0) == 0)
        def _():
            for c in range(kk // chunk):
                at[:, c * chunk:(c + 1) * chunk] = a_ref[c * chunk:(c + 1) * chunk, :].T

        b = b_ref[...]
        if square_b:
            b = b * b
        acc = _dot(at[...], b)
        if transpose_out:
            acc = acc.T
        o32_ref[0] = acc
        o16_ref[0] = acc.astype(BF16)

    oblk = pl.BlockSpec((1,) + oshape, lambda j: (j, 0, 0))
    return pl.pallas_call(
        body, grid=(n_tiles,), name=name,
        in_specs=[_const((kk, m)), pl.BlockSpec((kk, tn), lambda j: (0, j))],
        out_specs=[oblk, oblk],
        out_shape=[jax.ShapeDtypeStruct((n_tiles,) + oshape, F32), jax.ShapeDtypeStruct((n_tiles,) + oshape, BF16)],
        scratch_shapes=[pltpu.VMEM((m, kk), BF16)],
        compiler_params=_cparams(56))(a16, b16)


def _adamw_math(w, g, m, v):
    m = ADAM_B1 * m + (1.0 - ADAM_B1) * g
    v = ADAM_B2 * v + (1.0 - ADAM_B2) * jnp.square(g)
    m_hat = m / (1.0 - ADAM_B1 ** ADAM_STEP)
    v_hat = v / (1.0 - ADAM_B2 ** ADAM_STEP)
    delta = -ADAM_LR * (m_hat / (jnp.sqrt(v_hat) + ADAM_EPS) + ADAM_WD * w)
    return delta, m, v


def _adamw_shards(updates, name, chip_sums=()):
    names, nu, ns = list(updates), len(updates), len(chip_sums)

    def body(*refs):
        ins, sum_refs = refs[:5 * nu], refs[5 * nu:5 * nu + ns]
        outs = refs[5 * nu + ns:9 * nu + ns]
        landed_refs, scratch = refs[9 * nu + ns:9 * nu + 2 * ns], refs[9 * nu + 2 * ns:]
        if ns:
            start_chips, finish_chips = _chips_steps(sum_refs, landed_refs, *scratch)
            start_chips()
        for i in range(nu):
            o_ref, r_ref, w_ref, m_ref, v_ref = ins[5 * i:5 * i + 5]
            g_out, d_out, m_out, v_out = outs[4 * i:4 * i + 4]
            g = o_ref[...] + r_ref[0].astype(F32) + r_ref[1].astype(F32) + r_ref[2].astype(F32)
            g_out[...] = g
            d_out[...], m_out[...], v_out[...] = _adamw_math(w_ref[...], g, m_ref[...], v_ref[...])
        if ns:
            finish_chips()

    vmem = pl.BlockSpec(memory_space=pltpu.VMEM)
    out = pl.pallas_call(
        body, name=name,
        in_specs=[vmem] * (5 * nu) + [ANY] * ns, out_specs=[vmem] * (4 * nu) + [ANY] * ns,
        out_shape=[jax.ShapeDtypeStruct(updates[n][2].shape, F32) for n in names for _ in range(4)]
        + _chips_shapes(chip_sums),
        scratch_shapes=_chips_scratch(ns) if ns else [],
        compiler_params=pltpu.CompilerParams(vmem_limit_bytes=56 << 20),
    )(*[a for n in names for a in updates[n]], *chip_sums)
    return {n: out[4 * i:4 * i + 4] for i, n in enumerate(names)}, list(out[4 * nu:])


def _place():
    x, y, c = lax.axis_index("x"), lax.axis_index("y"), lax.axis_index("c")
    chips = [(1 - x, y), (x, 1 - y), (1 - x, 1 - y)]
    return x, y, c, chips


def _gather_steps(ins, outs, send, recv, lsem):
    nt = len(ins)
    x, y, c, chips = _place()
    me, sib = (x, y, c), (x, y, 1 - c)

    def slot(t, px, py, pc):
        return outs[t].at[4 * px + 2 * py + pc]

    def copy(t, k, block, to, src=None):
        return pltpu.make_async_remote_copy(
            src_ref=slot(t, *block) if src is None else src, dst_ref=slot(t, *block),
            send_sem=send.at[t, k], recv_sem=recv.at[t, k], device_id=to, device_id_type=MESH)

    mine = [pltpu.make_async_copy(ins[t], slot(t, *me), lsem.at[t]) for t in range(nt)]
    first = []
    for t in range(nt):
        first.append(copy(t, 0, me, sib, src=ins[t]))
        first += [copy(t, 1 + j, me, (*chip, c), src=ins[t]) for j, chip in enumerate(chips)]

    def start():
        for cp in mine + first:
            cp.start()

    def finish():
        passed = []
        for j, chip in enumerate(chips):
            for t in range(nt):
                copy(t, 1 + j, (*chip, c), me).wait_recv()
                fwd = copy(t, 4 + j, (*chip, c), sib)
                fwd.start()
                passed.append(fwd)
        for t in range(nt):
            copy(t, 0, sib, me).wait_recv()
            for j, chip in enumerate(chips):
                copy(t, 4 + j, (*chip, 1 - c), me).wait_recv()
        for cp in first + passed:
            cp.wait_send()
        for cp in mine:
            cp.wait()

    return start, finish


def _gather_scratch(nt):
    return [pltpu.SemaphoreType.DMA((nt, 7)), pltpu.SemaphoreType.DMA((nt, 7)), pltpu.SemaphoreType.DMA((nt,))]


def _gathered_shapes(shards):
    return [jax.ShapeDtypeStruct((N_DEV,) + s.shape, s.dtype) for s in shards]


def _call_with_gather(body, n_grid, shards, *, name, in_specs, out_specs, out_shape, scratch_shapes, vmem_mb, args):
    ng, n_in, n_out = len(shards), len(in_specs), len(out_specs)

    def wrapped(*refs):
        ins, shard_refs = refs[:n_in], refs[n_in:n_in + ng]
        outs = refs[n_in + ng:n_in + ng + n_out]
        whole_refs = refs[n_in + ng + n_out:n_in + 2 * ng + n_out]
        scratch = refs[n_in + 2 * ng + n_out:]
        if ng:
            start, finish = _gather_steps(shard_refs, whole_refs, *scratch[len(scratch_shapes):])
            pl.when(pl.program_id(0) == 0)(start)
        body(*ins, *outs, *scratch[:len(scratch_shapes)])
        if ng:
            pl.when(pl.program_id(0) == n_grid - 1)(finish)

    return pl.pallas_call(
        wrapped, grid=(n_grid,), name=name,
        in_specs=list(in_specs) + [ANY] * ng, out_specs=list(out_specs) + [ANY] * ng,
        out_shape=list(out_shape) + _gathered_shapes(shards),
        scratch_shapes=list(scratch_shapes) + (_gather_scratch(ng) if ng else []),
        compiler_params=_cparams(vmem_mb))(*args, *shards)


def _all_gather(shards):
    nt = len(shards)

    def body(*refs):
        start, finish = _gather_steps(refs[:nt], refs[nt:2 * nt], *refs[2 * nt:])
        start()
        finish()

    return pl.pallas_call(
        body, name="all_gather_weights", in_specs=[ANY] * nt, out_specs=[ANY] * nt,
        out_shape=_gathered_shapes(shards), scratch_shapes=_gather_scratch(nt))(*shards)


def _rs_pair(g16s, name):
    nt = len(g16s)

    def body(*refs):
        ins, outs = refs[:nt], refs[nt:2 * nt]
        send, recv = refs[2 * nt:]
        x, y, c, _ = _place()
        copies = [pltpu.make_async_remote_copy(
            src_ref=ins[t].at[2 * p + (1 - c)], dst_ref=outs[t].at[p], send_sem=send.at[t, p], recv_sem=recv.at[t, p],
            device_id=(x, y, 1 - c), device_id_type=MESH) for t in range(nt) for p in range(4)]
        for cp in copies:
            cp.start()
        for cp in copies:
            cp.wait()

    return pl.pallas_call(
        body, name=name,
        in_specs=[ANY] * nt, out_specs=[ANY] * nt,
        out_shape=[jax.ShapeDtypeStruct((4,) + g.shape[1:], g.dtype) for g in g16s],
        scratch_shapes=[pltpu.SemaphoreType.DMA((nt, 4)), pltpu.SemaphoreType.DMA((nt, 4))])(*g16s)


def _rs_pair_add(place, g32, ra16, name):
    shp = g32.shape[1:]

    def body(pl_ref, g_ref, r_ref, cs_ref, own_ref):
        s = g_ref[0] + r_ref[0].astype(F32)
        cs_ref[0] = s.astype(BF16)

        @pl.when(pl.program_id(0) == pl_ref[1])
        def _():
            own_ref[...] = s

    blk = (1,) + shp
    return pl.pallas_call(
        body, name=name,
        grid_spec=pltpu.PrefetchScalarGridSpec(
            num_scalar_prefetch=1, grid=(4,),
            in_specs=[pl.BlockSpec(blk, lambda p, s: (2 * p + s[0], 0, 0)), pl.BlockSpec(blk, lambda p, s: (p, 0, 0))],
            out_specs=[pl.BlockSpec(blk, lambda p, s: (p, 0, 0)), pl.BlockSpec(shp, lambda p, s: (0, 0))]),
        out_shape=[jax.ShapeDtypeStruct((4,) + shp, BF16), jax.ShapeDtypeStruct(shp, F32)],
        compiler_params=_cparams(48))(place, g32, ra16)


def _chips_steps(ins, outs, send, recv):
    _, _, c, chips = _place()
    copies = [pltpu.make_async_remote_copy(
        src_ref=ins[t].at[2 * px + py], dst_ref=outs[t].at[j], send_sem=send.at[t, j], recv_sem=recv.at[t, j],
        device_id=(px, py, c), device_id_type=MESH) for t in range(len(ins)) for j, (px, py) in enumerate(chips)]

    def start():
        for cp in copies:
            cp.start()

    def finish():
        for cp in copies:
            cp.wait()

    return start, finish


def _chips_scratch(nt):
    return [pltpu.SemaphoreType.DMA((nt, 3)), pltpu.SemaphoreType.DMA((nt, 3))]


def _chips_shapes(cs16s):
    return [jax.ShapeDtypeStruct((3,) + g.shape[1:], g.dtype) for g in cs16s]


def _rs_chips(cs16s):
    nt = len(cs16s)

    def body(*refs):
        start, finish = _chips_steps(refs[:nt], refs[nt:2 * nt], *refs[2 * nt:])
        start()
        finish()

    return pl.pallas_call(
        body, name="reduce_scatter_chips", in_specs=[ANY] * nt, out_specs=[ANY] * nt,
        out_shape=_chips_shapes(cs16s), scratch_shapes=_chips_scratch(nt))(*cs16s)


SMALL = (("g_pre_mix", 0, 0, D), ("g_mem", 1, 0, D), ("g_post_mix", 2, 0, D), ("g_attn_out", 3, 0, AW),
         ("g_conv_out", 3, AW, CW), ("g_xattn_out", 3, AW + CW, XW), ("g_post_mlp", 4, 0, D), ("g_pre_mlp", 5, 0, D))
CONV_ROW = 8
PACK_ROWS = 16


LOSS_ROW = 15


def _small_params_step(dg_in, dg_mem, dgs, dg_mlp, dcw, loss8, params):
    flat = [a for n, _, _, _ in SMALL for a in params[n]] + list(params["conv_w"])
    n_par = len(SMALL) + 1
    tap_cols = CW // N_DEV

    def body(*refs):
        acc_in, acc_mem, acc_mix, acc_mlp, acc_cw, acc_loss = refs[0:6]
        refs = refs[6:]
        ins = refs[0:3 * n_par]
        loss_out = refs[3 * n_par]
        outs = refs[3 * n_par + 1:7 * n_par + 1]
        pack, land, send, recv = refs[7 * n_par + 1:]
        x, y, c, _ = _place()
        me = 4 * x + 2 * y + c
        pack[...] = jnp.zeros_like(pack)
        pack[0:1, :] = acc_in[0:1, :]
        pack[1:2, :] = acc_mem[0:1, :]
        pack[2:4, :] = acc_mix[0:2, :]
        pack[4:6, :] = acc_mlp[0:2, :]
        pack[CONV_ROW:CONV_ROW + 3, 0:CW] = acc_cw[0:3, :]
        pack[LOSS_ROW:LOSS_ROW + 1, 0:LANES] = acc_loss[0:1, :]
        land[me] = pack[...]
        copies = []
        for k in range(1, N_DEV):
            kx, ky, kc = (k >> 2) & 1, (k >> 1) & 1, k & 1
            peer = (1 - x if kx else x, 1 - y if ky else y, 1 - c if kc else c)
            copies.append(pltpu.make_async_remote_copy(
                src_ref=pack, dst_ref=land.at[me], send_sem=send.at[k - 1], recv_sem=recv.at[k - 1],
                device_id=peer, device_id_type=MESH))
        for cp in copies:
            cp.start()
        for cp in copies:
            cp.wait()
        tot = land[0]
        for s in range(1, N_DEV):
            tot = tot + land[s]
        loss_out[...] = jnp.broadcast_to(tot[LOSS_ROW:LOSS_ROW + 1, 0:LANES], loss_out.shape)

        def update(i, g):
            w_ref, m_ref, v_ref = ins[3 * i:3 * i + 3]
            g_out, d_out, m_out, v_out = outs[4 * i:4 * i + 4]
            g_out[...] = g
            d_out[...], m_out[...], v_out[...] = _adamw_math(w_ref[...], g, m_ref[...], v_ref[...])

        for i, (_, row, lane0, width) in enumerate(SMALL):
            update(i, tot[row:row + 1, lane0:lane0 + width])
        taps = pltpu.roll(tot[CONV_ROW:CONV_ROW + SUBLANES, 0:CW], jnp.where(me == 0, 0, CW - me * tap_cols), 1)
        update(n_par - 1, taps[0:3, 0:tap_cols])

    shapes = [jax.ShapeDtypeStruct(params[n][0].shape, F32) for n, _, _, _ in SMALL] + [
        jax.ShapeDtypeStruct(params["conv_w"][0].shape, F32)]
    loss, *out = pl.pallas_call(
        body, name="small_params_step",
        out_shape=[jax.ShapeDtypeStruct((SUBLANES, LANES), F32)] + [s for s in shapes for _ in range(4)],
        scratch_shapes=[pltpu.VMEM((PACK_ROWS, D), F32), pltpu.VMEM((N_DEV, PACK_ROWS, D), F32),
                        pltpu.SemaphoreType.DMA((N_DEV - 1,)), pltpu.SemaphoreType.DMA((N_DEV - 1,))],
    )(dg_in, dg_mem, dgs, dg_mlp, dcw, loss8, *flat)
    names = [n for n, _, _, _ in SMALL] + ["conv_w"]
    return loss[0, 0], {n: out[4 * i:4 * i + 4] for i, n in enumerate(names)}


def _reduce_to_chip_sums(place, grads):
    from_sib = _rs_pair([g16 for _, g16 in grads.values()], "reduce_scatter_pair_" + "_".join(grads))
    return {n: _rs_pair_add(place, g32, from_sib[t], "pair_add_" + n) for t, (n, (g32, _)) in enumerate(grads.items())}


def _local_step(x, mem, pos, gains, shards, tgt, place):
    half = HEAD // 2
    inv_freq = jnp.float32(ROPE_THETA) ** (-(jnp.arange(half, dtype=F32) * 2.0 / HEAD))
    invf = jnp.tile(inv_freq, LANES // half)[None, :]
    sgn = jnp.tile(jnp.concatenate([-jnp.ones((half,), F32), jnp.ones((half,), F32)]), LANES // HEAD)[None, :]
    cos, sins, win8 = _rope_table(pos.astype(F32).reshape(S, 1), invf, sgn, [shards["w_in"]])
    win16 = win8.transpose(1, 0, 2).reshape(D, PW)

    q, kvp, bcu, qx16, h16, wout8, wkv8, conv8 = _in_proj(
        x, gains["g_pre_mix"], win16, cos, sins, [shards["w_out"], shards["w_mem_kv"], shards["conv_w"]])
    wout16, wkv16 = wout8.reshape(D, D), wkv8.reshape(D, 2 * XW)
    cw_full = conv8[:, 0:3, 0:CW // N_DEV].transpose(1, 0, 2).reshape(3, CW)
    cw8 = jnp.zeros((SUBLANES, CW), F32).at[0:3].set(cw_full)
    y_attn, ltot, wup8, wdn8 = _attn_fwd(q, kvp, [shards["w_up"], shards["w_down"]])
    wdn16 = wdn8.reshape(FF, D)
    memn16, kv16 = _mem_fwd(mem, gains["g_mem"], wkv16)
    ypre, y16, y2, x1 = _mix_out(y_attn, bcu, qx16, kv16, cw8, gains["g_attn_out"], gains["g_conv_out"],
                                 gains["g_xattn_out"], gains["g_post_mix"], wout16, x, [])
    a16, du16, h2_16, df2_16, dx1, loss8, dg_mlp = _mlp(x1, tgt, gains["g_pre_mlp"], gains["g_post_mlp"], wup8, wdn16)

    mlp_sums = _reduce_to_chip_sums(place, {
        "w_up": _wgrad(h2_16, du16, FF_BLK, "wgrad_up"),
        "w_down": _wgrad(df2_16, a16, FF_BLK, "wgrad_down", square_b=True, transpose_out=True)})

    head_id = jnp.arange(AW, dtype=jnp.int32) // HEAD
    head_ones = (head_id[:, None] == head_id[None, :]).astype(BF16)
    dy2_16, qdo, ld, dbcu, dqx, dgs, dcw, dkv = _mix_out_bwd(
        dx1, y2, ypre, ltot, head_ones, q, bcu, qx16, kv16, cw8, gains["g_post_mix"], gains["g_attn_out"],
        gains["g_conv_out"], gains["g_xattn_out"], wout16)
    dkv16, dg_mem = _mem_bwd(mem, gains["g_mem"], wkv16, dkv)
    sums = dict(mlp_sums, **_reduce_to_chip_sums(place, {
        "w_mem_kv": tuple(g.reshape(N_DEV, D // N_DEV, 2 * XW) for g in _wgrad(memn16, dkv16, 2 * XW, "wgrad_mem_kv")),
        "w_out": tuple(g.reshape(N_DEV, D // N_DEV, D) for g in _wgrad(y16, dy2_16, D, "wgrad_out"))}))
    out = _attn_bwd(qdo, kvp, ld, [s[0] for s in sums.values()])
    dqkv, landed = out[:9], out[9:]
    reduced = {n: (s[1], landed[t]) for t, (n, s) in enumerate(sums.items())}
    dproj16, grad_x, dg_in = _in_proj_bwd(dqkv, dbcu, dqx, cos, sins, win16, x, gains["g_pre_mix"], dx1)

    def by_owner_in(g):
        return g.transpose(1, 0, 2).reshape(D, N_DEV, PW // N_DEV).transpose(1, 0, 2)

    in_sums = _reduce_to_chip_sums(
        place, {"w_in": tuple(by_owner_in(g) for g in _wgrad(h16, dproj16, 512, "wgrad_in"))})["w_in"]
    return grad_x, reduced, in_sums, (dg_in, dg_mem, dgs, dg_mlp, dcw, loss8)


BIG = ("w_in", "w_mem_kv", "w_out", "w_up", "w_down")
ORDER = ("g_pre_mix", "g_mem", "w_in", "w_mem_kv", "conv_w", "g_attn_out", "g_conv_out", "g_xattn_out", "w_out",
         "g_post_mix", "g_pre_mlp", "w_up", "w_down", "g_post_mlp")


def kernel(x, mem, positions, g_pre_mix, g_mem, w_in, w_mem_kv, conv_w, g_attn_out, g_conv_out, g_xattn_out, w_out, g_post_mix, g_pre_mlp, w_up, w_down, g_post_mlp, loss_target, m_g_pre_mix, m_g_mem, m_w_in, m_w_mem_kv, m_conv_w, m_g_attn_out, m_g_conv_out, m_g_xattn_out, m_w_out, m_g_post_mix, m_g_pre_mlp, m_w_up, m_w_down, m_g_post_mlp, v_g_pre_mix, v_g_mem, v_w_in, v_w_mem_kv, v_conv_w, v_g_attn_out, v_g_conv_out, v_g_xattn_out, v_w_out, v_g_post_mix, v_g_pre_mlp, v_w_up, v_w_down, v_g_post_mlp):
    w = dict(g_pre_mix=g_pre_mix, g_mem=g_mem, w_in=w_in, w_mem_kv=w_mem_kv, conv_w=conv_w, g_attn_out=g_attn_out,
             g_conv_out=g_conv_out, g_xattn_out=g_xattn_out, w_out=w_out, g_post_mix=g_post_mix, g_pre_mlp=g_pre_mlp,
             w_up=w_up, w_down=w_down, g_post_mlp=g_post_mlp)
    mo = dict(g_pre_mix=m_g_pre_mix, g_mem=m_g_mem, w_in=m_w_in, w_mem_kv=m_w_mem_kv, conv_w=m_conv_w,
              g_attn_out=m_g_attn_out, g_conv_out=m_g_conv_out, g_xattn_out=m_g_xattn_out, w_out=m_w_out,
              g_post_mix=m_g_post_mix, g_pre_mlp=m_g_pre_mlp, w_up=m_w_up, w_down=m_w_down, g_post_mlp=m_g_post_mlp)
    vo = dict(g_pre_mix=v_g_pre_mix, g_mem=v_g_mem, w_in=v_w_in, w_mem_kv=v_w_mem_kv, conv_w=v_conv_w,
              g_attn_out=v_g_attn_out, g_conv_out=v_g_conv_out, g_xattn_out=v_g_xattn_out, w_out=v_w_out,
              g_post_mix=v_g_post_mix, g_pre_mlp=v_g_pre_mlp, w_up=v_w_up, w_down=v_w_down, g_post_mlp=v_g_post_mlp)

    xi, yi, ci = lax.axis_index("x"), lax.axis_index("y"), lax.axis_index("c")
    me = 4 * xi + 2 * yi + ci
    place = jnp.stack([ci, 2 * xi + yi]).astype(jnp.int32)

    shards = {n: w[n][0].astype(BF16) for n in BIG}
    shards["conv_w"] = jnp.zeros((SUBLANES, LANES), F32).at[0:3, 0:CW // N_DEV].set(conv_w[0])

    gains = {n: w[n] for n, _, _, _ in SMALL}
    grad_x, reduced, in_sums, small_acc = _local_step(
        x[0], mem[0], positions[0], gains, shards, loss_target[0], place)

    state = lambda n: (w[n][0], mo[n][0], vo[n][0])
    updated, in_chips = _adamw_shards({n: (*reduced[n], *state(n)) for n in reduced}, "adamw_shards", [in_sums[0]])
    updated.update(_adamw_shards({"w_in": (in_sums[1], in_chips[0], *state("w_in"))}, "adamw_w_in")[0])
    grad, delta, new_m, new_v = {}, {}, {}, {}
    for n, (g, d_, m_, v_) in updated.items():
        grad[n], delta[n], new_m[n], new_v[n] = g[None], d_[None], m_[None], v_[None]

    params = {n: (w[n], mo[n], vo[n]) for n, _, _, _ in SMALL}
    params["conv_w"] = (w["conv_w"][0], mo["conv_w"][0], vo["conv_w"][0])
    loss, small = _small_params_step(*small_acc, params)
    for n, (g, d_, m_, v_) in small.items():
        lead = (lambda a: a[None]) if n == "conv_w" else (lambda a: a)
        grad[n], delta[n], new_m[n], new_v[n] = lead(g), lead(d_), lead(m_), lead(v_)

    return (loss, grad_x[None], *[grad[n] for n in ORDER], *[delta[n] for n in ORDER],
            *[new_m[n] for n in ORDER], *[new_v[n] for n in ORDER])
```

```python
import functools

import numpy as np
import jax
import jax.numpy as jnp
from jax import lax
from jax.experimental import pallas as pl
from jax.experimental.pallas import tpu as pltpu

F32, BF16 = jnp.float32, jnp.bfloat16
MESH = pl.DeviceIdType.MESH
ANY = pl.BlockSpec(memory_space=pl.ANY)

N_DEV = 8
D = 1024
S = 4096
N_MEM = 256
HEAD = 64
AW, CW, XW = 512, 256, 256
PW = 3 * AW + 3 * CW + XW
FF = 4096
FF_BLK = FF // N_DEV
PATTERNS = ((128, 1), (512, 4), (2048, 16))
QB = 128
EPS = 1e-6
NEG = -1e30
SCALE = HEAD ** -0.5
ROPE_THETA = 10000.0
LANES = 128
SUBLANES = 8

ADAM_LR, ADAM_B1, ADAM_B2, ADAM_EPS, ADAM_WD, ADAM_STEP = 0.001, 0.9, 0.999, 1e-08, 0.01, 10

TQ = 512
TQ_MLP = 256
NT = S // TQ


def _cparams(vmem_mb, n_grid=1):
    return pltpu.CompilerParams(dimension_semantics=("arbitrary",) * n_grid, vmem_limit_bytes=vmem_mb << 20)


def _const(shape):
    nd = len(shape)
    return pl.BlockSpec(shape, lambda *_: (0,) * nd, pipeline_mode=pl.Buffered(1))


def _acc(shape):
    nd = len(shape)
    return pl.BlockSpec(shape, lambda *_: (0,) * nd)


def _dot(a, b):
    return jnp.dot(a, b, preferred_element_type=F32)


def _dot_nt(a, b):
    return lax.dot_general(a, b, (((1,), (1,)), ((), ())), preferred_element_type=F32)


def _dot_tn(a, b):
    return lax.dot_general(a, b, (((0,), (0,)), ((), ())), preferred_element_type=F32)


def _rms(x, g):
    r = lax.rsqrt(jnp.mean(x * x, axis=-1, keepdims=True) + EPS)
    n = x * r
    return n * g, n, r


def _rms_bwd(dy, n, r, g):
    dn = dy * g
    dx = r * (dn - n * jnp.mean(dn * n, axis=-1, keepdims=True))
    return dx, jnp.sum(dy * n, axis=0, keepdims=True)


def _rot_half(t):
    lane = lax.broadcasted_iota(jnp.int32, t.shape, 1)
    n = t.shape[1]
    return jnp.where((lane % HEAD) < HEAD // 2, pltpu.roll(t, n - HEAD // 2, 1), pltpu.roll(t, HEAD // 2, 1))


def _rope_table(pos_col, invf, sgn, shards):
    def body(p_ref, f_ref, s_ref, c_out, s_out):
        ang = p_ref[...] * f_ref[...]
        c_out[...] = jnp.tile(jnp.cos(ang), (1, AW // LANES))
        s_out[...] = jnp.tile(jnp.sin(ang) * s_ref[...], (1, AW // LANES))

    tile = pl.BlockSpec((TQ, AW), lambda i: (i, 0))
    return _call_with_gather(
        body, NT, shards, name="rope_table",
        in_specs=[pl.BlockSpec((TQ, 1), lambda i: (i, 0)), _const((1, LANES)), _const((1, LANES))],
        out_specs=[tile, tile], out_shape=[jax.ShapeDtypeStruct((S, AW), F32)] * 2,
        scratch_shapes=[], vmem_mb=32, args=(pos_col, invf, sgn))


def _mem_fwd(mem, g_mem, wkv16):
    def body(m_ref, g_ref, w_ref, n16_ref, kv_ref):
        y, _, _ = _rms(m_ref[...], g_ref[...])
        y16 = y.astype(BF16)
        n16_ref[...] = y16
        kv_ref[...] = _dot(y16, w_ref[...]).astype(BF16)

    return pl.pallas_call(
        body, name="mem_fwd",
        out_shape=[jax.ShapeDtypeStruct((N_MEM, D), BF16), jax.ShapeDtypeStruct((N_MEM, 2 * XW), BF16)],
        compiler_params=pltpu.CompilerParams(vmem_limit_bytes=32 << 20))(mem, g_mem, wkv16)


def _in_proj(x, g, w8, cos, sins, shards):
    blk = PW // N_DEV

    def body(x_ref, g_ref, w8_ref, c_ref, s_ref, q_ref, kv_ref, bcu_ref, qx_ref, h_ref, w_out, w_ref):
        @pl.when(pl.program_id(0) == 0)
        def _():
            for j in range(N_DEV):
                w_ref[:, j * blk:(j + 1) * blk] = w8_ref[j]
            w_out[...] = w_ref[...]

        y, _, _ = _rms(x_ref[...], g_ref[...])
        h = y.astype(BF16)
        h_ref[...] = h
        proj = _dot(h, w_ref[...])
        cos, sn = c_ref[...], s_ref[...]
        q, k = proj[:, 0:AW], proj[:, AW:2 * AW]
        q_ref[...] = (q * cos + _rot_half(q) * sn) * SCALE
        kv_ref[...] = _pack_pair(k * cos + _rot_half(k) * sn, proj[:, 2 * AW:3 * AW])
        bcu_ref[...] = proj[:, 3 * AW:3 * AW + 3 * CW]
        qx_ref[...] = (proj[:, 3 * AW + 3 * CW:] * SCALE).astype(BF16)

    def tile(w):
        return pl.BlockSpec((TQ, w), lambda i: (i, 0))

    return _call_with_gather(
        body, NT, shards, name="in_proj",
        in_specs=[tile(D), _const((1, D)), _const((N_DEV, D, blk)), tile(AW), tile(AW)],
        out_specs=[tile(AW), tile(AW), tile(3 * CW), tile(XW), tile(D), _acc((D, PW))],
        out_shape=[jax.ShapeDtypeStruct((S, AW), F32)] * 2 + [
            jax.ShapeDtypeStruct((S, 3 * CW), F32), jax.ShapeDtypeStruct((S, XW), BF16),
            jax.ShapeDtypeStruct((S, D), BF16), jax.ShapeDtypeStruct((D, PW), BF16)],
        scratch_shapes=[pltpu.VMEM((D, PW), BF16)], vmem_mb=56, args=(x, g, w8, cos, sins))


ATTN_PLANS = (("p1", 1, 128, 32), ("p4", 8, 64, 8), ("p16", 16, 128, 2))
PAD = 128
WIN = 256


ATTN_UNROLL = 8


def _fill_bias(tab, qblk, partner):
    qi = lax.broadcasted_iota(jnp.int32, (2 * qblk, WIN), 0) & (qblk - 1)
    kj = lax.broadcasted_iota(jnp.int32, (2 * qblk, WIN), 1)
    piece = kj >> (qblk.bit_length() - 1)
    kk = kj & (qblk - 1)
    prev = (piece & 1) == 0
    of_partner = piece >= 2
    for first in (0, 1):
        for par in (0, 1):
            lo = jnp.where(prev, (qblk if first else qi) + jnp.where(of_partner, par, 0), 0)
            hi = jnp.where(prev, qblk, qi + jnp.where(of_partner, par - 1, 0))
            tab[2 * first + par] = jnp.where((kk >= lo) & (kk <= hi), 0.0, NEG).astype(F32)


def _block_rows(g, qblk, nbc, partner):
    own = pl.ds(pl.multiple_of(PAD + g * qblk, qblk), qblk)
    first = ((g & (nbc - 1)) == 0).astype(jnp.int32)
    if partner:
        gp = jnp.bitwise_xor(g, 4 * nbc)
        wins = (pl.ds(pl.multiple_of(PAD + (g - 1) * qblk, qblk), 2 * qblk),
                pl.ds(pl.multiple_of(PAD + (gp - 1) * qblk, qblk), 2 * qblk))
        return own, wins, 2 * first + ((g >> ((4 * nbc).bit_length() - 1)) & 1)
    return own, (pl.ds(pl.multiple_of(PAD + (g - 1) * qblk, qblk), 2 * qblk),), 2 * first


def _pack_pair(lo, hi):
    lo_bits = lax.bitcast_convert_type(lo.astype(BF16).astype(F32), jnp.uint32) >> 16
    hi_bits = lax.bitcast_convert_type(hi.astype(BF16).astype(F32), jnp.uint32) & jnp.uint32(0xFFFF0000)
    return lax.bitcast_convert_type(hi_bits | lo_bits, F32)


def _unpack_pair(c):
    bits = lax.bitcast_convert_type(c, jnp.uint32)
    lo = lax.bitcast_convert_type(bits << 16, F32).astype(BF16)
    hi = lax.bitcast_convert_type(bits & jnp.uint32(0xFFFF0000), F32).astype(BF16)
    return lo, hi


def _window(ref, wins):
    parts = [ref[w, :] for w in wins]
    return parts[0] if len(parts) == 1 else jnp.concatenate(parts, axis=0)


def _stack_heads(t, lane):
    zero = jnp.zeros_like(t)
    return jnp.concatenate([jnp.where(lane < HEAD, t, zero), jnp.where(lane >= HEAD, t, zero)], axis=0)


def _unstack_heads(t2, lane):
    half = t2.shape[0] // 2
    return jnp.where(lane < HEAD, t2[0:half, :], t2[half:, :])


def _lanes_of(step):
    return pl.ds(pl.multiple_of(step * LANES, LANES), LANES)


def _whole_wait(buf, sem):
    whole = buf.at[pl.ds(PAD, S), :]
    return pltpu.make_async_copy(whole, whole, sem)


def _whole_waits(bufs, sems):
    return [_whole_wait(buf, sems.at[i]) for i, buf in enumerate(bufs)]


def _class_gather(views, bufs, sems, lanes):
    copies = []
    for i, (view, buf) in enumerate(zip(views, bufs)):
        if view.ndim == 2:
            copies.append(pltpu.make_async_copy(view.at[:, lanes], buf.at[pl.ds(PAD, S), :], sems.at[i]))
        else:
            per, n_cls = view.shape[0], view.shape[1]
            copies += [pltpu.make_async_copy(view.at[:, c, lanes], buf.at[pl.ds(PAD + c * per, per), :], sems.at[i])
                       for c in range(n_cls)]
    return copies


def _class_scatter(bufs, dsts, sems, lanes=None):
    copies = []
    for i, (buf, dst) in enumerate(zip(bufs, dsts)):
        if dst.ndim == 2:
            copies.append(pltpu.make_async_copy(buf.at[pl.ds(PAD, S), :], dst.at[:, lanes], sems.at[i]))
            continue
        per, n_cls = dst.shape[0], dst.shape[1]
        for c in range(n_cls):
            to = dst.at[:, c, :] if lanes is None else dst.at[:, c, lanes]
            copies.append(pltpu.make_async_copy(buf.at[pl.ds(PAD + c * per, per), :], to, sems.at[i]))
    return copies


def _start(copies):
    for cp in copies:
        cp.start()


def _wait(waits):
    for w in waits:
        w.wait()


def _attn_fwd(q, kvp, shards=()):
    views = [[a] + [a.reshape(S // n, n, AW) for _, n, _, _ in ATTN_PLANS[1:]] for a in (q, kvp)]
    flat = [views[a][p] for p in range(3) for a in range(2)]
    ng = len(shards)
    n_grid = AW // LANES

    def body(*refs):
        hbm = [refs[2 * p:2 * p + 2] for p in range(3)]
        refs = refs[6:]
        shard_refs, refs = refs[:ng], refs[ng:]
        y_ref, lt_ref = refs[0:2]
        whole_refs, refs = refs[2:2 + ng], refs[2 + ng:]
        bufs = [refs[2 * p:2 * p + 2] for p in range(3)]
        oc4, lc4, oc16, lc16, o4n, l4n, o16n, l16n, tab128, tab4, sem_in, sem_out = refs[6:18]
        step = pl.program_id(0)
        if ng:
            start_gather, finish_gather = _gather_steps(shard_refs, whole_refs, *refs[18:])
            pl.when(step == 0)(start_gather)
        now = [_class_gather(hbm[p], bufs[p], sem_in.at[p], _lanes_of(step)) for p in range(3)]
        nxt = [_class_gather(hbm[p], bufs[p], sem_in.at[p], _lanes_of(step + 1)) for p in range(3)]

        @pl.when(step == 0)
        def _():
            for p in range(3):
                _start(now[p])
                for b in bufs[p]:
                    b[0:PAD, :] = jnp.zeros((PAD, LANES), F32)
            _fill_bias(tab128, 128, False)
            _fill_bias(tab4, 64, True)

        def prefetch(p):
            pl.when(step + 1 < n_grid)(lambda: _start(nxt[p]))

        lane = lax.broadcasted_iota(jnp.int32, (1, LANES), 1)
        ones = jnp.ones((WIN, LANES), BF16)

        def run(plan, bq, bkv, tab, o_dst, l_dst, dst_pad):
            _, n_cls, qblk, nbc = plan
            partner = n_cls == 8

            def block(g, carry):
                own, wins, mask = _block_rows(g, qblk, nbc, partner)
                q2 = _stack_heads(bq[own, :].astype(BF16), lane)
                kw, vwin = _unpack_pair(_window(bkv, wins))
                vw = jnp.concatenate([vwin, ones], axis=1)
                s = _dot_nt(q2, kw) + tab[mask]
                m = jnp.max(s, axis=1, keepdims=True)
                oe = _dot(jnp.exp(s - m).astype(BF16), vw)
                den = oe[:, LANES:]
                dst = pl.ds(pl.multiple_of(dst_pad + g * qblk, qblk), qblk)
                o_dst[dst, :] = _unstack_heads(oe[:, 0:LANES] / den, lane)
                l_dst[dst, :] = _unstack_heads(m + jnp.log(den), lane)
                return carry
            lax.fori_loop(0, n_cls * nbc, block, 0, unroll=ATTN_UNROLL)

        _wait(_whole_waits(bufs[0], sem_in.at[0]))
        run(ATTN_PLANS[0], *bufs[0], tab128, y_ref, lt_ref, 0)
        prefetch(0)
        _wait(_whole_waits(bufs[1], sem_in.at[1]))
        run(ATTN_PLANS[1], *bufs[1], tab4, oc4, lc4, PAD)
        prefetch(1)
        _start(_class_scatter((oc4, lc4), (o4n, l4n), sem_out.at[0]))
        _wait(_whole_waits(bufs[2], sem_in.at[2]))
        run(ATTN_PLANS[2], *bufs[2], tab128, oc16, lc16, PAD)
        prefetch(2)
        _start(_class_scatter((oc16, lc16), (o16n, l16n), sem_out.at[1]))
        _wait(_whole_waits((oc4, lc4), sem_out.at[0]) + _whole_waits((oc16, lc16), sem_out.at[1]))

        for t in range(S // TQ):
            rows = pl.ds(t * TQ, TQ)
            r4, r16 = pl.ds(t * (TQ // 8), TQ // 8), pl.ds(t * (TQ // 16), TQ // 16)
            l0, l1, l2 = lt_ref[rows, :], l4n[r4, :, :].reshape(TQ, LANES), l16n[r16, :, :].reshape(TQ, LANES)
            lm = jnp.maximum(jnp.maximum(l0, l1), l2)
            e0, e1, e2 = jnp.exp(l0 - lm), jnp.exp(l1 - lm), jnp.exp(l2 - lm)
            den = e0 + e1 + e2
            y_ref[rows, :] = (e0 * y_ref[rows, :] + e1 * o4n[r4, :, :].reshape(TQ, LANES)
                              + e2 * o16n[r16, :, :].reshape(TQ, LANES)) / den
            lt_ref[rows, :] = lm + jnp.log(den)

        if ng:
            pl.when(step == n_grid - 1)(finish_gather)

    col = pl.BlockSpec((S, LANES), lambda h: (0, h))
    padded = pltpu.VMEM((PAD + S, LANES), F32)
    return pl.pallas_call(
        body, grid=(n_grid,), name="attn_fwd",
        in_specs=[ANY] * (6 + ng), out_specs=[col, col] + [ANY] * ng,
        out_shape=[jax.ShapeDtypeStruct((S, AW), F32)] * 2 + _gathered_shapes(shards),
        scratch_shapes=[padded] * 10 + [
            pltpu.VMEM((S // 8, 8, LANES), F32), pltpu.VMEM((S // 8, 8, LANES), F32),
            pltpu.VMEM((S // 16, 16, LANES), F32), pltpu.VMEM((S // 16, 16, LANES), F32),
            pltpu.VMEM((4, 256, WIN), F32), pltpu.VMEM((4, 128, WIN), F32),
            pltpu.SemaphoreType.DMA((3, 2)), pltpu.SemaphoreType.DMA((2, 2))]
        + (_gather_scratch(ng) if ng else []),
        compiler_params=_cparams(56))(*flat, *shards)


def _conv_taps(z, zprev, row):
    z1 = jnp.where(row == 0, zprev[7:8, :], pltpu.roll(z, 1, 0))
    z2 = jnp.where(row == 0, zprev[6:7, :], jnp.where(row == 1, zprev[7:8, :], pltpu.roll(z, 2, 0)))
    return z1, z2


def _xattn_scores(qm, km):
    s = _dot_nt(qm, km)
    m = jnp.max(s, axis=1, keepdims=True)
    e = jnp.exp(s - m)
    return e, jnp.sum(e, axis=1, keepdims=True)


def _mix_out(y_attn, bcu, qx16, kv16, cw8, g_attn, g_conv, g_x, g_post, wout16, x, shards):
    def body(ya_ref, bcu_ref, halo_ref, qx_ref, kv_ref, cw_ref, ga_ref, gc_ref, gx_ref, gp_ref, w_ref, x_ref,
             ypre_ref, y16_ref, y2_ref, x1_ref):
        i = pl.program_id(0)
        bcu = bcu_ref[...]
        b, c, u = bcu[:, 0:CW], bcu[:, CW:2 * CW], bcu[:, 2 * CW:]
        z = c * u
        halo = halo_ref[...]
        zprev = jnp.where(i > 0, halo[:, CW:2 * CW] * halo[:, 2 * CW:], 0.0)
        row = lax.broadcasted_iota(jnp.int32, z.shape, 0)
        z1, z2 = _conv_taps(z, zprev, row)
        cw = cw_ref[...]
        y_conv = b * (z2 * cw[0:1, :] + z1 * cw[1:2, :] + z * cw[2:3, :])

        qx = qx_ref[...]
        kv = kv_ref[...]
        km, vm = kv[:, 0:XW], kv[:, XW:]
        lane = lax.broadcasted_iota(jnp.int32, qx.shape, 1)
        y_x = jnp.zeros(qx.shape, F32)
        for h in range(XW // HEAD):
            hm = (lane >= h * HEAD) & (lane < (h + 1) * HEAD)
            e, l = _xattn_scores(jnp.where(hm, qx, jnp.zeros_like(qx)), km)
            y_x = jnp.where(hm, _dot(e.astype(BF16), vm) / l, y_x)

        y_attn = ya_ref[...]
        ypre_ref[:, 0:AW] = y_attn
        ypre_ref[:, AW:AW + CW] = y_conv
        ypre_ref[:, AW + CW:] = y_x
        y = jnp.concatenate([_rms(y_attn, ga_ref[...])[0], _rms(y_conv, gc_ref[...])[0],
                             _rms(y_x, gx_ref[...])[0]], axis=1).astype(BF16)
        y16_ref[...] = y
        y2 = _dot(y, w_ref[...])
        y2_ref[...] = y2
        x1_ref[...] = x_ref[...] + _rms(y2, gp_ref[...])[0]

    def tile(w):
        return pl.BlockSpec((TQ, w), lambda i: (i, 0))

    halo = pl.BlockSpec((SUBLANES, 3 * CW), lambda i: (jnp.maximum(i * (TQ // SUBLANES) - 1, 0), 0))
    return _call_with_gather(
        body, NT, shards, name="mix_out",
        in_specs=[tile(AW), tile(3 * CW), halo, tile(XW), _const((N_MEM, 2 * XW)), _const((SUBLANES, CW)),
                  _const((1, AW)), _const((1, CW)), _const((1, XW)), _const((1, D)), _const((D, D)), tile(D)],
        out_specs=[tile(D), tile(D), tile(D), tile(D)],
        out_shape=[jax.ShapeDtypeStruct((S, D), F32), jax.ShapeDtypeStruct((S, D), BF16),
                   jax.ShapeDtypeStruct((S, D), F32), jax.ShapeDtypeStruct((S, D), F32)],
        scratch_shapes=[], vmem_mb=56,
        args=(y_attn, bcu, bcu, qx16, kv16, cw8, g_attn, g_conv, g_x, g_post, wout16, x))


def _mlp(x1, tgt, g_pre, g_post, wup8, wdn16):
    tq = TQ_MLP

    def body(x1_ref, t_ref, g1_ref, g2_ref, wu_ref, wd_ref,
             a16_ref, du_ref, h2_ref, df2_ref, dx1_ref, loss_ref, dg_ref, a32):
        @pl.when(pl.program_id(0) == 0)
        def _():
            loss_ref[...] = jnp.zeros_like(loss_ref)
            dg_ref[...] = jnp.zeros_like(dg_ref)

        x1 = x1_ref[...]
        g1, g2 = g1_ref[...], g2_ref[...]
        y1, n1, r1 = _rms(x1, g1)
        h2 = y1.astype(BF16)
        h2_ref[...] = h2
        f2 = jnp.zeros((tq, D), F32)
        for j in range(N_DEV):
            cols = slice(j * FF_BLK, (j + 1) * FF_BLK)
            a = jnp.maximum(_dot(h2, wu_ref[j]), 0.0)
            a32[:, cols] = a
            a16_ref[:, cols] = a.astype(BF16)
            f2 = f2 + _dot((a * a).astype(BF16), wd_ref[cols, :])
        y2, n2, r2 = _rms(f2, g2)
        e = x1 + y2 - t_ref[...]
        sq = jnp.sum(jnp.sum(e * e, axis=1, keepdims=True), axis=0, keepdims=True)
        loss_ref[...] += jnp.broadcast_to(sq * (0.5 / D), loss_ref.shape)
        dout = e * (1.0 / D)
        df2, dg2 = _rms_bwd(dout, n2, r2, g2)
        df2_16 = df2.astype(BF16)
        df2_ref[...] = df2_16
        dh2 = jnp.zeros((tq, D), F32)
        for j in range(N_DEV):
            cols = slice(j * FF_BLK, (j + 1) * FF_BLK)
            du = (_dot_nt(df2_16, wd_ref[cols, :]) * (2.0 * a32[:, cols])).astype(BF16)
            du_ref[:, cols] = du
            dh2 = dh2 + _dot_nt(du, wu_ref[j])
        dx, dg1 = _rms_bwd(dh2, n1, r1, g1)
        dx1_ref[...] = dout + dx
        dg_ref[0:1, :] += dg2
        dg_ref[1:2, :] += dg1

    def tile(w):
        return pl.BlockSpec((tq, w), lambda i: (i, 0))

    return pl.pallas_call(
        body, grid=(S // tq,), name="mlp",
        in_specs=[tile(D), tile(D), _const((1, D)), _const((1, D)), _const((N_DEV, D, FF_BLK)), _const((FF, D))],
        out_specs=[tile(FF), tile(FF), tile(D), tile(D), tile(D), _acc((SUBLANES, LANES)), _acc((SUBLANES, D))],
        out_shape=[jax.ShapeDtypeStruct((S, FF), BF16), jax.ShapeDtypeStruct((S, FF), BF16),
                   jax.ShapeDtypeStruct((S, D), BF16), jax.ShapeDtypeStruct((S, D), BF16),
                   jax.ShapeDtypeStruct((S, D), F32), jax.ShapeDtypeStruct((SUBLANES, LANES), F32),
                   jax.ShapeDtypeStruct((SUBLANES, D), F32)],
        scratch_shapes=[pltpu.VMEM((tq, FF), F32)],
        compiler_params=_cparams(56))(x1, tgt, g_pre, g_post, wup8, wdn16)


def _mix_out_bwd(dx1, y2, ypre, ltot, head_ones, q, bcu, qx16, kv16, cw8, g_post, g_attn, g_conv, g_x, wout16):
    def body(dx1_ref, y2_ref, ypre_ref, lt_ref, e_ref, q_ref, bcu_ref, halo_ref, qx_ref, kv_ref, cw_ref, gp_ref,
             ga_ref, gc_ref, gx_ref, w_ref, dy2_ref, qdo_ref, ld_ref, dbcu_ref, dqx_ref, dgs_ref, dcw_ref, dkv_ref,
             carry):
        i = pl.program_id(0)

        @pl.when(i == 0)
        def _():
            dgs_ref[...] = jnp.zeros_like(dgs_ref)
            dcw_ref[...] = jnp.zeros_like(dcw_ref)
            dkv_ref[...] = jnp.zeros_like(dkv_ref)
            carry[...] = jnp.zeros_like(carry)

        gp = gp_ref[...]
        _, n, r = _rms(y2_ref[...], gp)
        dy2, dgp = _rms_bwd(dx1_ref[...], n, r, gp)
        dy2_16 = dy2.astype(BF16)
        dy2_ref[...] = dy2_16
        dy = _dot_nt(dy2_16, w_ref[...])

        ypre = ypre_ref[...]
        ga, gc, gx = ga_ref[...], gc_ref[...], gx_ref[...]
        _, na, ra = _rms(ypre[:, 0:AW], ga)
        dya, dga = _rms_bwd(dy[:, 0:AW], na, ra, ga)
        _, nc, rc = _rms(ypre[:, AW:AW + CW], gc)
        dyc, dgc = _rms_bwd(dy[:, AW:AW + CW], nc, rc, gc)
        y_x = ypre[:, AW + CW:]
        _, nx, rx = _rms(y_x, gx)
        dyx, dgx = _rms_bwd(dy[:, AW + CW:], nx, rx, gx)
        qdo_ref[...] = _pack_pair(q_ref[...], dya)
        prod = dya * ypre[:, 0:AW]
        hi = prod.astype(BF16)
        lo = (prod - hi.astype(F32)).astype(BF16)
        head_sum = _dot(hi, e_ref[...]) + _dot(lo, e_ref[...])
        lane_a = lax.broadcasted_iota(jnp.int32, prod.shape, 1)
        ld_ref[...] = jnp.where((lane_a % HEAD) < HEAD // 2, lt_ref[...], head_sum)
        dgs_ref[0:1, :] += dgp
        dgs_ref[1:2, :] += jnp.concatenate([dga, dgc, dgx], axis=1)

        bcu = bcu_ref[...]
        b, c, u = bcu[:, 0:CW], bcu[:, CW:2 * CW], bcu[:, 2 * CW:]
        z = c * u
        halo = halo_ref[...]
        zprev = jnp.where(i < NT - 1, halo[:, CW:2 * CW] * halo[:, 2 * CW:], 0.0)
        row = lax.broadcasted_iota(jnp.int32, z.shape, 0)
        z1, z2 = _conv_taps(z, zprev, row)
        cw = cw_ref[...]
        conv = z2 * cw[0:1, :] + z1 * cw[1:2, :] + z * cw[2:3, :]
        dconv = dyc * b
        nxt = carry[...]
        dn1 = jnp.where(row == TQ - 1, nxt[0:1, :], pltpu.roll(dconv, TQ - 1, 0))
        dn2 = jnp.where(row == TQ - 1, nxt[1:2, :], jnp.where(row == TQ - 2, nxt[0:1, :], pltpu.roll(dconv, TQ - 2, 0)))
        carry[...] = dconv[0:SUBLANES, :]
        dz = dconv * cw[2:3, :] + dn1 * cw[1:2, :] + dn2 * cw[0:1, :]
        dbcu_ref[:, 0:CW] = dyc * conv
        dbcu_ref[:, CW:2 * CW] = dz * u
        dbcu_ref[:, 2 * CW:] = dz * c
        dcw_ref[0:1, :] += jnp.sum(z2 * dconv, axis=0, keepdims=True)
        dcw_ref[1:2, :] += jnp.sum(z1 * dconv, axis=0, keepdims=True)
        dcw_ref[2:3, :] += jnp.sum(z * dconv, axis=0, keepdims=True)

        qx = qx_ref[...]
        kv = kv_ref[...]
        km, vm = kv[:, 0:XW], kv[:, XW:]
        lane = lax.broadcasted_iota(jnp.int32, qx.shape, 1)
        dqx = jnp.zeros(qx.shape, F32)
        dkm = jnp.zeros((N_MEM, XW), F32)
        dvm = jnp.zeros((N_MEM, XW), F32)
        for h in range(XW // HEAD):
            hm = (lane >= h * HEAD) & (lane < (h + 1) * HEAD)
            qm = jnp.where(hm, qx, jnp.zeros_like(qx))
            e, l = _xattn_scores(qm, km)
            p = e / l
            dom = jnp.where(hm, dyx, 0.0)
            do16 = dom.astype(BF16)
            dsum = jnp.sum(dom * y_x, axis=1, keepdims=True)
            ds = (p * (_dot_nt(do16, vm) - dsum)).astype(BF16)
            dqx = jnp.where(hm, _dot(ds, km), dqx)
            dkm = dkm + _dot_tn(ds, qm)
            dvm = dvm + _dot_tn(p.astype(BF16), do16)
        dqx_ref[...] = dqx * SCALE
        dkv_ref[:, 0:XW] += dkm
        dkv_ref[:, XW:] += dvm

    def tile(w):
        return pl.BlockSpec((TQ, w), lambda i: (NT - 1 - i, 0))

    halo = pl.BlockSpec((SUBLANES, 3 * CW), lambda i: (jnp.maximum((NT - 1 - i) * (TQ // SUBLANES) - 1, 0), 0))
    return pl.pallas_call(
        body, grid=(NT,), name="mix_out_bwd",
        in_specs=[tile(D), tile(D), tile(D), tile(AW), _const((AW, AW)), tile(AW), tile(3 * CW), halo, tile(XW),
                  _const((N_MEM, 2 * XW)), _const((SUBLANES, CW)), _const((1, D)), _const((1, AW)), _const((1, CW)),
                  _const((1, XW)), _const((D, D))],
        out_specs=[tile(D), tile(AW), tile(AW), tile(3 * CW), tile(XW), _acc((SUBLANES, D)), _acc((SUBLANES, CW)),
                   _acc((N_MEM, 2 * XW))],
        out_shape=[jax.ShapeDtypeStruct((S, D), BF16), jax.ShapeDtypeStruct((S, AW), F32),
                   jax.ShapeDtypeStruct((S, AW), F32),
                   jax.ShapeDtypeStruct((S, 3 * CW), F32), jax.ShapeDtypeStruct((S, XW), F32),
                   jax.ShapeDtypeStruct((SUBLANES, D), F32), jax.ShapeDtypeStruct((SUBLANES, CW), F32),
                   jax.ShapeDtypeStruct((N_MEM, 2 * XW), F32)],
        scratch_shapes=[pltpu.VMEM((SUBLANES, CW), F32)],
        compiler_params=_cparams(56))(dx1, y2, ypre, ltot, head_ones, q, bcu, bcu, qx16, kv16, cw8, g_post, g_attn,
                                      g_conv, g_x, wout16)


def _attn_bwd(qdo, kvp, ld, chip_sums=()):
    n_in = 3
    views = [[a] + [a.reshape(S // n, n, AW) for _, n, _, _ in ATTN_PLANS[1:]] for a in (qdo, kvp, ld)]
    flat = [views[a][p] for p in range(3) for a in range(n_in)]
    ns = len(chip_sums)
    n_grid = AW // LANES

    def body(*refs):
        hbm = [refs[n_in * p:n_in * p + n_in] for p in range(3)]
        refs = refs[3 * n_in:]
        sum_refs, refs = refs[:ns], refs[ns:]
        outs = [refs[3 * p:3 * p + 3] for p in range(3)]
        landed_refs, sc = refs[9:9 + ns], refs[9 + ns:]
        bufs = [sc[3 * p:3 * p + 3] for p in range(3)]
        res = [sc[9 + 3 * p:12 + 3 * p] for p in range(3)]
        tab128, tab4, sem_in, sem_out = sc[18:22]
        step = pl.program_id(0)
        if ns:
            start_chips, finish_chips = _chips_steps(sum_refs, landed_refs, *sc[22:])
            pl.when(step == 0)(start_chips)
        now = [_class_gather(hbm[p], bufs[p], sem_in.at[p], _lanes_of(step)) for p in range(3)]
        nxt = [_class_gather(hbm[p], bufs[p], sem_in.at[p], _lanes_of(step + 1)) for p in range(3)]

        @pl.when(step == 0)
        def _():
            for p in range(3):
                _start(now[p])
                for b in bufs[p]:
                    b[0:PAD, :] = jnp.zeros((PAD, LANES), F32)
            _fill_bias(tab128, 128, False)
            _fill_bias(tab4, 64, True)

        def prefetch(p):
            pl.when(step + 1 < n_grid)(lambda: _start(nxt[p]))

        for p in range(3):
            for b in res[p]:
                b[...] = jnp.zeros_like(b)
        lane = lax.broadcasted_iota(jnp.int32, (1, LANES), 1)

        def run(plan, plan_bufs, tab, dst):
            _, n_cls, qblk, nbc = plan
            partner = n_cls == 8
            bqdo, bkv, bld = plan_bufs
            rq, rk, rv = dst

            def block(g, carry):
                own, wins, mask = _block_rows(g, qblk, nbc, partner)
                qb, dob = _unpack_pair(bqdo[own, :])
                q2, do2 = _stack_heads(qb, lane), _stack_heads(dob, lane)
                kw, vw = _unpack_pair(_window(bkv, wins))
                ldv = bld[own, :]
                half = HEAD // 2
                lt2 = jnp.concatenate([ldv[:, 0:1], ldv[:, HEAD:HEAD + 1]], axis=0)
                dsum2 = jnp.concatenate([ldv[:, half:half + 1], ldv[:, HEAD + half:HEAD + half + 1]], axis=0)
                p = jnp.exp(_dot_nt(q2, kw) + tab[mask] - lt2)
                ds = (p * (_dot_nt(do2, vw) - dsum2)).astype(BF16)
                rq[own, :] = _unstack_heads(_dot(ds, kw), lane)
                dkw = _dot_tn(ds, q2)
                dvw = _dot_tn(p.astype(BF16), do2)
                n_w = WIN // len(wins)
                for i, w in enumerate(wins):
                    rk[w, :] += dkw[i * n_w:(i + 1) * n_w, :]
                    rv[w, :] += dvw[i * n_w:(i + 1) * n_w, :]
                return carry
            lax.fori_loop(0, n_cls * nbc, block, 0, unroll=ATTN_UNROLL)

        tabs = (tab128, tab4, tab128)
        for p in range(3):
            _wait(_whole_waits(bufs[p], sem_in.at[p]))
            run(ATTN_PLANS[p], bufs[p], tabs[p], res[p])
            prefetch(p)
            _start(_class_scatter(res[p], outs[p], sem_out.at[p], _lanes_of(step)))
        for p in range(3):
            _wait(_whole_waits(res[p], sem_out.at[p]))
        if ns:
            pl.when(step == n_grid - 1)(finish_chips)

    padded = pltpu.VMEM((PAD + S, LANES), F32)
    shapes = [jax.ShapeDtypeStruct(views[0][p].shape, F32) for p in range(3) for _ in range(3)]
    out = pl.pallas_call(
        body, grid=(n_grid,), name="attn_bwd",
        in_specs=[ANY] * (3 * n_in + ns), out_specs=[ANY] * (9 + ns),
        out_shape=shapes + _chips_shapes(chip_sums),
        scratch_shapes=[padded] * 18
        + [pltpu.VMEM((4, 256, WIN), F32), pltpu.VMEM((4, 128, WIN), F32),
           pltpu.SemaphoreType.DMA((3, n_in)), pltpu.SemaphoreType.DMA((3, 3))]
        + (_chips_scratch(ns) if ns else []),
        compiler_params=_cparams(56))(*flat, *chip_sums)
    return [o.reshape(S, AW) for o in out[:9]] + list(out[9:])


def _in_proj_bwd(dqkv, dbcu, dqx, cos, sins, w16, x, g, dx1):
    tq = TQ // 2

    def body(*refs):
        parts = refs[0:9]
        dbcu_ref, dqx_ref, c_ref, s_ref, w_ref, x_ref, g_ref, dx1_ref, dp_ref, gx_ref, dg_ref = refs[9:]

        @pl.when(pl.program_id(0) == 0)
        def _():
            dg_ref[...] = jnp.zeros_like(dg_ref)

        dq, dk, dv = (parts[i][...] + parts[3 + i][...] + parts[6 + i][...] for i in range(3))
        cos, sn = c_ref[...], s_ref[...]
        dqr = dq * SCALE
        dkr = dk
        dp = jnp.concatenate([dqr * cos + _rot_half(dqr * sn), dkr * cos + _rot_half(dkr * sn), dv,
                              dbcu_ref[...], dqx_ref[...]], axis=1).astype(BF16)
        dp_ref[...] = dp
        dh = _dot_nt(dp, w_ref[...])
        g = g_ref[...]
        _, n, r = _rms(x_ref[...], g)
        dx, dg = _rms_bwd(dh, n, r, g)
        gx_ref[...] = dx1_ref[...] + dx
        dg_ref[0:1, :] += dg

    def tile(w):
        return pl.BlockSpec((tq, w), lambda i: (i, 0))

    return pl.pallas_call(
        body, grid=(S // tq,), name="in_proj_bwd",
        in_specs=[tile(AW)] * 9 + [tile(3 * CW), tile(XW), tile(AW), tile(AW), _const((D, PW)),
                                   tile(D), _const((1, D)), tile(D)],
        out_specs=[tile(PW), tile(D), _acc((SUBLANES, D))],
        out_shape=[jax.ShapeDtypeStruct((S, PW), BF16), jax.ShapeDtypeStruct((S, D), F32),
                   jax.ShapeDtypeStruct((SUBLANES, D), F32)],
        compiler_params=_cparams(56))(*dqkv, dbcu, dqx, cos, sins, w16, x, g, dx1)


def _mem_bwd(mem, g_mem, wkv16, dkv):
    def body(m_ref, g_ref, w_ref, dkv_ref, dkv16_ref, dg_ref):
        dkv16 = dkv_ref[...].astype(BF16)
        dkv16_ref[...] = dkv16
        _, n, _ = _rms(m_ref[...], g_ref[...])
        dg = jnp.sum(_dot_nt(dkv16, w_ref[...]) * n, axis=0, keepdims=True)
        dg_ref[...] = jnp.broadcast_to(dg, dg_ref.shape)

    return pl.pallas_call(
        body, name="mem_bwd",
        out_shape=[jax.ShapeDtypeStruct((N_MEM, 2 * XW), BF16), jax.ShapeDtypeStruct((SUBLANES, D), F32)],
        compiler_params=pltpu.CompilerParams(vmem_limit_bytes=32 << 20))(mem, g_mem, wkv16, dkv)


def _wgrad(a16, b16, tn, name, square_b=False, transpose_out=False):
    kk, m = a16.shape
    n_tiles = b16.shape[1] // tn
    chunk = min(kk, 512)
    oshape = (tn, m) if transpose_out else (m, tn)

    def body(a_ref, b_ref, o32_ref, o16_ref, at):
        @pl.when(pl.program_id(0) == 0)
        def _():
            for c in range(kk // chunk):
                at[:, c * chunk:(c + 1) * chunk] = a_ref[c * chunk:(c + 1) * chunk, :].T

        b = b_ref[...]
        if square_b:
            b = b * b
        acc = _dot(at[...], b)
        if transpose_out:
            acc = acc.T
        o32_ref[0] = acc
        o16_ref[0] = acc.astype(BF16)

    oblk = pl.BlockSpec((1,) + oshape, lambda j: (j, 0, 0))
    return pl.pallas_call(
        body, grid=(n_tiles,), name=name,
        in_specs=[_const((kk, m)), pl.BlockSpec((kk, tn), lambda j: (0, j))],
        out_specs=[oblk, oblk],
        out_shape=[jax.ShapeDtypeStruct((n_tiles,) + oshape, F32), jax.ShapeDtypeStruct((n_tiles,) + oshape, BF16)],
        scratch_shapes=[pltpu.VMEM((m, kk), BF16)],
        compiler_params=_cparams(56))(a16, b16)


def _wgrad_in(h16, dproj16):
    blk = PW // N_DEV
    wide = -(-(blk + LANES // 2) // LANES) * LANES
    starts = [j * blk // LANES * LANES for j in range(N_DEV)]
    chunk = 512

    def body(a_ref, b_hbm, o32_ref, o16_ref, at, win, sem):
        def fetch(j):
            return pltpu.make_async_copy(b_hbm.at[:, pl.ds(starts[j], wide)], win.at[j % 2], sem.at[j % 2])

        fetch(0).start()
        for c in range(S // chunk):
            at[:, c * chunk:(c + 1) * chunk] = a_ref[c * chunk:(c + 1) * chunk, :].T
        for j in range(N_DEV):
            if j + 1 < N_DEV:
                fetch(j + 1).start()
            fetch(j).wait()
            off = j * blk - starts[j]
            acc = _dot(at[...], win[j % 2])[:, off:off + blk]
            o32_ref[j] = acc
            o16_ref[j] = acc.astype(BF16)

    vmem = pl.BlockSpec(memory_space=pltpu.VMEM)
    return pl.pallas_call(
        body, name="wgrad_in", in_specs=[vmem, ANY], out_specs=[vmem, vmem],
        out_shape=[jax.ShapeDtypeStruct((N_DEV, D, blk), F32), jax.ShapeDtypeStruct((N_DEV, D, blk), BF16)],
        scratch_shapes=[pltpu.VMEM((D, S), BF16), pltpu.VMEM((2, S, wide), BF16), pltpu.SemaphoreType.DMA((2,))],
        compiler_params=pltpu.CompilerParams(vmem_limit_bytes=56 << 20))(h16, dproj16)


def _adamw_math(w, g, m, v):
    m = ADAM_B1 * m + (1.0 - ADAM_B1) * g
    v = ADAM_B2 * v + (1.0 - ADAM_B2) * jnp.square(g)
    m_hat = m / (1.0 - ADAM_B1 ** ADAM_STEP)
    v_hat = v / (1.0 - ADAM_B2 ** ADAM_STEP)
    delta = -ADAM_LR * (m_hat / (jnp.sqrt(v_hat) + ADAM_EPS) + ADAM_WD * w)
    return delta, m, v


def _adamw_shards(updates, name, chip_sums=()):
    names, nu, ns = list(updates), len(updates), len(chip_sums)

    def body(*refs):
        ins, sum_refs = refs[:5 * nu], refs[5 * nu:5 * nu + ns]
        outs = refs[5 * nu + ns:9 * nu + ns]
        landed_refs, scratch = refs[9 * nu + ns:9 * nu + 2 * ns], refs[9 * nu + 2 * ns:]
        if ns:
            start_chips, finish_chips = _chips_steps(sum_refs, landed_refs, *scratch)
            start_chips()
        for i in range(nu):
            o_ref, r_ref, w_ref, m_ref, v_ref = ins[5 * i:5 * i + 5]
            g_out, d_out, m_out, v_out = outs[4 * i:4 * i + 4]
            g = o_ref[...] + r_ref[0].astype(F32) + r_ref[1].astype(F32) + r_ref[2].astype(F32)
            g_out[...] = g
            d_out[...], m_out[...], v_out[...] = _adamw_math(w_ref[...], g, m_ref[...], v_ref[...])
        if ns:
            finish_chips()

    vmem = pl.BlockSpec(memory_space=pltpu.VMEM)
    out = pl.pallas_call(
        body, name=name,
        in_specs=[vmem] * (5 * nu) + [ANY] * ns, out_specs=[vmem] * (4 * nu) + [ANY] * ns,
        out_shape=[jax.ShapeDtypeStruct(updates[n][2].shape, F32) for n in names for _ in range(4)]
        + _chips_shapes(chip_sums),
        scratch_shapes=_chips_scratch(ns) if ns else [],
        compiler_params=pltpu.CompilerParams(vmem_limit_bytes=56 << 20),
    )(*[a for n in names for a in updates[n]], *chip_sums)
    return {n: out[4 * i:4 * i + 4] for i, n in enumerate(names)}, list(out[4 * nu:])


def _place():
    x, y, c = lax.axis_index("x"), lax.axis_index("y"), lax.axis_index("c")
    chips = [(1 - x, y), (x, 1 - y), (1 - x, 1 - y)]
    return x, y, c, chips


def _gather_steps(ins, outs, send, recv, lsem):
    nt = len(ins)
    x, y, c, chips = _place()
    me, sib = (x, y, c), (x, y, 1 - c)

    def slot(t, px, py, pc):
        return outs[t].at[4 * px + 2 * py + pc]

    def copy(t, k, block, to, src=None):
        return pltpu.make_async_remote_copy(
            src_ref=slot(t, *block) if src is None else src, dst_ref=slot(t, *block),
            send_sem=send.at[t, k], recv_sem=recv.at[t, k], device_id=to, device_id_type=MESH)

    mine = [pltpu.make_async_copy(ins[t], slot(t, *me), lsem.at[t]) for t in range(nt)]
    first = []
    for t in range(nt):
        first.append(copy(t, 0, me, sib, src=ins[t]))
        first += [copy(t, 1 + j, me, (*chip, c), src=ins[t]) for j, chip in enumerate(chips)]

    def start():
        for cp in mine + first:
            cp.start()

    def finish():
        passed = []
        for j, chip in enumerate(chips):
            for t in range(nt):
                copy(t, 1 + j, (*chip, c), me).wait_recv()
                fwd = copy(t, 4 + j, (*chip, c), sib)
                fwd.start()
                passed.append(fwd)
        for t in range(nt):
            copy(t, 0, sib, me).wait_recv()
            for j, chip in enumerate(chips):
                copy(t, 4 + j, (*chip, 1 - c), me).wait_recv()
        for cp in first + passed:
            cp.wait_send()
        for cp in mine:
            cp.wait()

    return start, finish


def _gather_scratch(nt):
    return [pltpu.SemaphoreType.DMA((nt, 7)), pltpu.SemaphoreType.DMA((nt, 7)), pltpu.SemaphoreType.DMA((nt,))]


def _gathered_shapes(shards):
    return [jax.ShapeDtypeStruct((N_DEV,) + s.shape, s.dtype) for s in shards]


def _call_with_gather(body, n_grid, shards, *, name, in_specs, out_specs, out_shape, scratch_shapes, vmem_mb, args):
    ng, n_in, n_out = len(shards), len(in_specs), len(out_specs)

    def wrapped(*refs):
        ins, shard_refs = refs[:n_in], refs[n_in:n_in + ng]
        outs = refs[n_in + ng:n_in + ng + n_out]
        whole_refs = refs[n_in + ng + n_out:n_in + 2 * ng + n_out]
        scratch = refs[n_in + 2 * ng + n_out:]
        if ng:
            start, finish = _gather_steps(shard_refs, whole_refs, *scratch[len(scratch_shapes):])
            pl.when(pl.program_id(0) == 0)(start)
        body(*ins, *outs, *scratch[:len(scratch_shapes)])
        if ng:
            pl.when(pl.program_id(0) == n_grid - 1)(finish)

    return pl.pallas_call(
        wrapped, grid=(n_grid,), name=name,
        in_specs=list(in_specs) + [ANY] * ng, out_specs=list(out_specs) + [ANY] * ng,
        out_shape=list(out_shape) + _gathered_shapes(shards),
        scratch_shapes=list(scratch_shapes) + (_gather_scratch(ng) if ng else []),
        compiler_params=_cparams(vmem_mb))(*args, *shards)


def _rs_pair(g16s, name):
    nt = len(g16s)

    def body(*refs):
        ins, outs = refs[:nt], refs[nt:2 * nt]
        send, recv = refs[2 * nt:]
        x, y, c, _ = _place()
        copies = [pltpu.make_async_remote_copy(
            src_ref=ins[t].at[2 * p + (1 - c)], dst_ref=outs[t].at[p], send_sem=send.at[t, p], recv_sem=recv.at[t, p],
            device_id=(x, y, 1 - c), device_id_type=MESH) for t in range(nt) for p in range(4)]
        for cp in copies:
            cp.start()
        for cp in copies:
            cp.wait()

    return pl.pallas_call(
        body, name=name,
        in_specs=[ANY] * nt, out_specs=[ANY] * nt,
        out_shape=[jax.ShapeDtypeStruct((4,) + g.shape[1:], g.dtype) for g in g16s],
        scratch_shapes=[pltpu.SemaphoreType.DMA((nt, 4)), pltpu.SemaphoreType.DMA((nt, 4))])(*g16s)


def _rs_pair_add(place, g32, ra16, name):
    shp = g32.shape[1:]

    def body(pl_ref, g_ref, r_ref, cs_ref, own_ref):
        s = g_ref[0] + r_ref[0].astype(F32)
        cs_ref[0] = s.astype(BF16)

        @pl.when(pl.program_id(0) == pl_ref[1])
        def _():
            own_ref[...] = s

    blk = (1,) + shp
    return pl.pallas_call(
        body, name=name,
        grid_spec=pltpu.PrefetchScalarGridSpec(
            num_scalar_prefetch=1, grid=(4,),
            in_specs=[pl.BlockSpec(blk, lambda p, s: (2 * p + s[0], 0, 0)), pl.BlockSpec(blk, lambda p, s: (p, 0, 0))],
            out_specs=[pl.BlockSpec(blk, lambda p, s: (p, 0, 0)), pl.BlockSpec(shp, lambda p, s: (0, 0))]),
        out_shape=[jax.ShapeDtypeStruct((4,) + shp, BF16), jax.ShapeDtypeStruct(shp, F32)],
        compiler_params=_cparams(48))(place, g32, ra16)


def _chips_steps(ins, outs, send, recv):
    _, _, c, chips = _place()
    copies = [pltpu.make_async_remote_copy(
        src_ref=ins[t].at[2 * px + py], dst_ref=outs[t].at[j], send_sem=send.at[t, j], recv_sem=recv.at[t, j],
        device_id=(px, py, c), device_id_type=MESH) for t in range(len(ins)) for j, (px, py) in enumerate(chips)]

    def start():
        for cp in copies:
            cp.start()

    def finish():
        for cp in copies:
            cp.wait()

    return start, finish


def _chips_scratch(nt):
    return [pltpu.SemaphoreType.DMA((nt, 3)), pltpu.SemaphoreType.DMA((nt, 3))]


def _chips_shapes(cs16s):
    return [jax.ShapeDtypeStruct((3,) + g.shape[1:], g.dtype) for g in cs16s]


SMALL = (("g_pre_mix", 0, 0, D), ("g_mem", 1, 0, D), ("g_post_mix", 2, 0, D), ("g_attn_out", 3, 0, AW),
         ("g_conv_out", 3, AW, CW), ("g_xattn_out", 3, AW + CW, XW), ("g_post_mlp", 4, 0, D), ("g_pre_mlp", 5, 0, D))
CONV_ROW = 8
PACK_ROWS = 16


LOSS_ROW = 15


def _small_all_reduce(dg_in, dg_mem, dgs, dg_mlp, dcw, loss8):
    def body(acc_in, acc_mem, acc_mix, acc_mlp, acc_cw, acc_loss, tot_ref, pack, land, send, recv):
        x, y, c, _ = _place()
        me = 4 * x + 2 * y + c
        pack[...] = jnp.zeros_like(pack)
        pack[0:1, :] = acc_in[0:1, :]
        pack[1:2, :] = acc_mem[0:1, :]
        pack[2:4, :] = acc_mix[0:2, :]
        pack[4:6, :] = acc_mlp[0:2, :]
        pack[CONV_ROW:CONV_ROW + 3, 0:CW] = acc_cw[0:3, :]
        pack[LOSS_ROW:LOSS_ROW + 1, 0:LANES] = acc_loss[0:1, :]
        land[me] = pack[...]
        copies = []
        for k in range(1, N_DEV):
            kx, ky, kc = (k >> 2) & 1, (k >> 1) & 1, k & 1
            peer = (1 - x if kx else x, 1 - y if ky else y, 1 - c if kc else c)
            copies.append(pltpu.make_async_remote_copy(
                src_ref=pack, dst_ref=land.at[me], send_sem=send.at[k - 1], recv_sem=recv.at[k - 1],
                device_id=peer, device_id_type=MESH))
        for cp in copies:
            cp.start()
        for cp in copies:
            cp.wait()
        tot = land[0]
        for s in range(1, N_DEV):
            tot = tot + land[s]
        tot_ref[...] = tot

    return pl.pallas_call(
        body, name="small_all_reduce", out_shape=jax.ShapeDtypeStruct((PACK_ROWS, D), F32),
        scratch_shapes=[pltpu.VMEM((PACK_ROWS, D), F32), pltpu.VMEM((N_DEV, PACK_ROWS, D), F32),
                        pltpu.SemaphoreType.DMA((N_DEV - 1,)), pltpu.SemaphoreType.DMA((N_DEV - 1,))],
    )(dg_in, dg_mem, dgs, dg_mlp, dcw, loss8)


def _small_update(tot, me, params):
    flat = [a for n, _, _, _ in SMALL for a in params[n]] + list(params["conv_w"])
    n_par = len(SMALL) + 1
    tap_cols = CW // N_DEV

    def body(*refs):
        me_ref, tot_ref = refs[0:2]
        ins = refs[2:2 + 3 * n_par]
        loss_out = refs[2 + 3 * n_par]
        outs = refs[3 + 3 * n_par:]
        tot = tot_ref[...]
        loss_out[...] = jnp.broadcast_to(tot[LOSS_ROW:LOSS_ROW + 1, 0:LANES], loss_out.shape)

        def update(i, g):
            w_ref, m_ref, v_ref = ins[3 * i:3 * i + 3]
            g_out, d_out, m_out, v_out = outs[4 * i:4 * i + 4]
            g_out[...] = g
            d_out[...], m_out[...], v_out[...] = _adamw_math(w_ref[...], g, m_ref[...], v_ref[...])

        for i, (_, row, lane0, width) in enumerate(SMALL):
            update(i, tot[row:row + 1, lane0:lane0 + width])
        me = me_ref[0]
        taps = pltpu.roll(tot[CONV_ROW:CONV_ROW + SUBLANES, 0:CW], jnp.where(me == 0, 0, CW - me * tap_cols), 1)
        update(n_par - 1, taps[0:3, 0:tap_cols])

    shapes = [jax.ShapeDtypeStruct(params[n][0].shape, F32) for n, _, _, _ in SMALL] + [
        jax.ShapeDtypeStruct(params["conv_w"][0].shape, F32)]
    vmem = pl.BlockSpec(memory_space=pltpu.VMEM)
    loss, *out = pl.pallas_call(
        body, name="small_update",
        in_specs=[pl.BlockSpec(memory_space=pltpu.SMEM)] + [vmem] * (1 + 3 * n_par),
        out_shape=[jax.ShapeDtypeStruct((SUBLANES, LANES), F32)] + [s for s in shapes for _ in range(4)],
    )(me, tot, *flat)
    names = [n for n, _, _, _ in SMALL] + ["conv_w"]
    return loss[0, 0], {n: out[4 * i:4 * i + 4] for i, n in enumerate(names)}


def _reduce_to_chip_sums(place, grads):
    from_sib = _rs_pair([g16 for _, g16 in grads.values()], "reduce_scatter_pair_" + "_".join(grads))
    return {n: _rs_pair_add(place, g32, from_sib[t], "pair_add_" + n) for t, (n, (g32, _)) in enumerate(grads.items())}


def _local_step(x, mem, pos, gains, shards, tgt, place):
    half = HEAD // 2
    inv_freq = jnp.float32(ROPE_THETA) ** (-(jnp.arange(half, dtype=F32) * 2.0 / HEAD))
    invf = jnp.tile(inv_freq, LANES // half)[None, :]
    sgn = jnp.tile(jnp.concatenate([-jnp.ones((half,), F32), jnp.ones((half,), F32)]), LANES // HEAD)[None, :]
    cos, sins, win8 = _rope_table(pos.astype(F32).reshape(S, 1), invf, sgn, [shards["w_in"]])
    q, kvp, bcu, qx16, h16, win16, wout8, wkv8, conv8 = _in_proj(
        x, gains["g_pre_mix"], win8, cos, sins, [shards["w_out"], shards["w_mem_kv"], shards["conv_w"]])
    wout16, wkv16 = wout8.reshape(D, D), wkv8.reshape(D, 2 * XW)
    cw_full = conv8[:, 0:3, 0:CW // N_DEV].transpose(1, 0, 2).reshape(3, CW)
    cw8 = jnp.zeros((SUBLANES, CW), F32).at[0:3].set(cw_full)
    y_attn, ltot, wup8, wdn8 = _attn_fwd(q, kvp, [shards["w_up"], shards["w_down"]])
    wdn16 = wdn8.reshape(FF, D)
    memn16, kv16 = _mem_fwd(mem, gains["g_mem"], wkv16)
    ypre, y16, y2, x1 = _mix_out(y_attn, bcu, qx16, kv16, cw8, gains["g_attn_out"], gains["g_conv_out"],
                                 gains["g_xattn_out"], gains["g_post_mix"], wout16, x, [])
    a16, du16, h2_16, df2_16, dx1, loss8, dg_mlp = _mlp(x1, tgt, gains["g_pre_mlp"], gains["g_post_mlp"], wup8, wdn16)

    mlp_sums = _reduce_to_chip_sums(place, {
        "w_up": _wgrad(h2_16, du16, FF_BLK, "wgrad_up"),
        "w_down": _wgrad(df2_16, a16, FF_BLK, "wgrad_down", square_b=True, transpose_out=True)})

    head_id = jnp.arange(AW, dtype=jnp.int32) // HEAD
    head_ones = (head_id[:, None] == head_id[None, :]).astype(BF16)
    dy2_16, qdo, ld, dbcu, dqx, dgs, dcw, dkv = _mix_out_bwd(
        dx1, y2, ypre, ltot, head_ones, q, bcu, qx16, kv16, cw8, gains["g_post_mix"], gains["g_attn_out"],
        gains["g_conv_out"], gains["g_xattn_out"], wout16)
    dkv16, dg_mem = _mem_bwd(mem, gains["g_mem"], wkv16, dkv)
    sums = dict(mlp_sums, **_reduce_to_chip_sums(place, {
        "w_mem_kv": tuple(g.reshape(N_DEV, D // N_DEV, 2 * XW) for g in _wgrad(memn16, dkv16, 2 * XW, "wgrad_mem_kv")),
        "w_out": tuple(g.reshape(N_DEV, D // N_DEV, D) for g in _wgrad(y16, dy2_16, D, "wgrad_out"))}))
    out = _attn_bwd(qdo, kvp, ld, [s[0] for s in sums.values()])
    dqkv, landed = out[:9], out[9:]
    reduced = {n: (s[1], landed[t]) for t, (n, s) in enumerate(sums.items())}
    dproj16, grad_x, dg_in = _in_proj_bwd(dqkv, dbcu, dqx, cos, sins, win16, x, gains["g_pre_mix"], dx1)

    in_sums = _reduce_to_chip_sums(place, {"w_in": _wgrad_in(h16, dproj16)})["w_in"]
    return grad_x, reduced, in_sums, (dg_in, dg_mem, dgs, dg_mlp, dcw, loss8)


BIG = ("w_in", "w_mem_kv", "w_out", "w_up", "w_down")
ORDER = ("g_pre_mix", "g_mem", "w_in", "w_mem_kv", "conv_w", "g_attn_out", "g_conv_out", "g_xattn_out", "w_out",
         "g_post_mix", "g_pre_mlp", "w_up", "w_down", "g_post_mlp")


def kernel(x, mem, positions, g_pre_mix, g_mem, w_in, w_mem_kv, conv_w, g_attn_out, g_conv_out, g_xattn_out, w_out, g_post_mix, g_pre_mlp, w_up, w_down, g_post_mlp, loss_target, m_g_pre_mix, m_g_mem, m_w_in, m_w_mem_kv, m_conv_w, m_g_attn_out, m_g_conv_out, m_g_xattn_out, m_w_out, m_g_post_mix, m_g_pre_mlp, m_w_up, m_w_down, m_g_post_mlp, v_g_pre_mix, v_g_mem, v_w_in, v_w_mem_kv, v_conv_w, v_g_attn_out, v_g_conv_out, v_g_xattn_out, v_w_out, v_g_post_mix, v_g_pre_mlp, v_w_up, v_w_down, v_g_post_mlp):
    w = dict(g_pre_mix=g_pre_mix, g_mem=g_mem, w_in=w_in, w_mem_kv=w_mem_kv, conv_w=conv_w, g_attn_out=g_attn_out,
             g_conv_out=g_conv_out, g_xattn_out=g_xattn_out, w_out=w_out, g_post_mix=g_post_mix, g_pre_mlp=g_pre_mlp,
             w_up=w_up, w_down=w_down, g_post_mlp=g_post_mlp)
    mo = dict(g_pre_mix=m_g_pre_mix, g_mem=m_g_mem, w_in=m_w_in, w_mem_kv=m_w_mem_kv, conv_w=m_conv_w,
              g_attn_out=m_g_attn_out, g_conv_out=m_g_conv_out, g_xattn_out=m_g_xattn_out, w_out=m_w_out,
              g_post_mix=m_g_post_mix, g_pre_mlp=m_g_pre_mlp, w_up=m_w_up, w_down=m_w_down, g_post_mlp=m_g_post_mlp)
    vo = dict(g_pre_mix=v_g_pre_mix, g_mem=v_g_mem, w_in=v_w_in, w_mem_kv=v_w_mem_kv, conv_w=v_conv_w,
              g_attn_out=v_g_attn_out, g_conv_out=v_g_conv_out, g_xattn_out=v_g_xattn_out, w_out=v_w_out,
              g_post_mix=v_g_post_mix, g_pre_mlp=v_g_pre_mlp, w_up=v_w_up, w_down=v_w_down, g_post_mlp=v_g_post_mlp)

    xi, yi, ci = lax.axis_index("x"), lax.axis_index("y"), lax.axis_index("c")
    me = 4 * xi + 2 * yi + ci
    place = jnp.stack([ci, 2 * xi + yi]).astype(jnp.int32)

    shards = {n: w[n][0].astype(BF16) for n in BIG}
    shards["conv_w"] = jnp.zeros((SUBLANES, LANES), F32).at[0:3, 0:CW // N_DEV].set(conv_w[0])

    gains = {n: w[n] for n, _, _, _ in SMALL}
    grad_x, reduced, in_sums, small_acc = _local_step(
        x[0], mem[0], positions[0], gains, shards, loss_target[0], place)

    state = lambda n: (w[n][0], mo[n][0], vo[n][0])
    updated, in_chips = _adamw_shards({n: (*reduced[n], *state(n)) for n in reduced}, "adamw_shards", [in_sums[0]])
    updated.update(_adamw_shards({"w_in": (in_sums[1], in_chips[0], *state("w_in"))}, "adamw_w_in")[0])
    grad, delta, new_m, new_v = {}, {}, {}, {}
    for n, (g, d_, m_, v_) in updated.items():
        grad[n], delta[n], new_m[n], new_v[n] = g[None], d_[None], m_[None], v_[None]

    params = {n: (w[n], mo[n], vo[n]) for n, _, _, _ in SMALL}
    params["conv_w"] = (w["conv_w"][0], mo["conv_w"][0], vo["conv_w"][0])
    loss, small = _small_update(_small_all_reduce(*small_acc), me.reshape(1).astype(jnp.int32), params)
    for n, (g, d_, m_, v_) in small.items():
        lead = (lambda a: a[None]) if n == "conv_w" else (lambda a: a)
        grad[n], delta[n], new_m[n], new_v[n] = lead(g), lead(d_), lead(m_), lead(v_)

    return (loss, grad_x[None], *[grad[n] for n in ORDER], *[delta[n] for n in ORDER],
            *[new_m[n] for n in ORDER], *[new_v[n] for n in ORDER])
```

```python
import functools

import numpy as np
import jax
import jax.numpy as jnp
from jax import lax
from jax.experimental import pallas as pl
from jax.experimental.pallas import tpu as pltpu

F32, BF16 = jnp.float32, jnp.bfloat16
MESH = pl.DeviceIdType.MESH
ANY = pl.BlockSpec(memory_space=pl.ANY)

N_DEV = 8
D = 1024
S = 4096
N_MEM = 256
HEAD = 64
AW, CW, XW = 512, 256, 256
PW = 3 * AW + 3 * CW + XW
FF = 4096
FF_BLK = FF // N_DEV
PATTERNS = ((128, 1), (512, 4), (2048, 16))
QB = 128
EPS = 1e-6
NEG = -1e30
SCALE = HEAD ** -0.5
ROPE_THETA = 10000.0
LANES = 128
SUBLANES = 8

ADAM_LR, ADAM_B1, ADAM_B2, ADAM_EPS, ADAM_WD, ADAM_STEP = 0.001, 0.9, 0.999, 1e-08, 0.01, 10

TQ = 512
TQ_MLP = 256
NT = S // TQ


def _cparams(vmem_mb, n_grid=1):
    return pltpu.CompilerParams(dimension_semantics=("arbitrary",) * n_grid, vmem_limit_bytes=vmem_mb << 20)


def _const(shape):
    nd = len(shape)
    return pl.BlockSpec(shape, lambda *_: (0,) * nd, pipeline_mode=pl.Buffered(1))


def _acc(shape):
    nd = len(shape)
    return pl.BlockSpec(shape, lambda *_: (0,) * nd)


def _dot(a, b):
    return jnp.dot(a, b, preferred_element_type=F32)


def _dot_nt(a, b):
    return lax.dot_general(a, b, (((1,), (1,)), ((), ())), preferred_element_type=F32)


def _dot_tn(a, b):
    return lax.dot_general(a, b, (((0,), (0,)), ((), ())), preferred_element_type=F32)


def _rms(x, g):
    r = lax.rsqrt(jnp.mean(x * x, axis=-1, keepdims=True) + EPS)
    n = x * r
    return n * g, n, r


def _rms_bwd(dy, n, r, g):
    dn = dy * g
    dx = r * (dn - n * jnp.mean(dn * n, axis=-1, keepdims=True))
    return dx, jnp.sum(dy * n, axis=0, keepdims=True)


def _rot_half(t):
    lane = lax.broadcasted_iota(jnp.int32, t.shape, 1)
    n = t.shape[1]
    return jnp.where((lane % HEAD) < HEAD // 2, pltpu.roll(t, n - HEAD // 2, 1), pltpu.roll(t, HEAD // 2, 1))


def _rope_table(pos_col, invf, sgn, shards):
    def body(p_ref, f_ref, s_ref, c_out, s_out):
        ang = p_ref[...] * f_ref[...]
        c_out[...] = jnp.tile(jnp.cos(ang), (1, AW // LANES))
        s_out[...] = jnp.tile(jnp.sin(ang) * s_ref[...], (1, AW // LANES))

    tile = pl.BlockSpec((TQ, AW), lambda i: (i, 0))
    return _call_with_gather(
        body, NT, shards, name="rope_table",
        in_specs=[pl.BlockSpec((TQ, 1), lambda i: (i, 0)), _const((1, LANES)), _const((1, LANES))],
        out_specs=[tile, tile], out_shape=[jax.ShapeDtypeStruct((S, AW), F32)] * 2,
        scratch_shapes=[], vmem_mb=32, args=(pos_col, invf, sgn))


def _mem_fwd(mem, g_mem, wkv16):
    def body(m_ref, g_ref, w_ref, n16_ref, kv_ref):
        y, _, _ = _rms(m_ref[...], g_ref[...])
        y16 = y.astype(BF16)
        n16_ref[...] = y16
        kv_ref[...] = _dot(y16, w_ref[...]).astype(BF16)

    return pl.pallas_call(
        body, name="mem_fwd",
        out_shape=[jax.ShapeDtypeStruct((N_MEM, D), BF16), jax.ShapeDtypeStruct((N_MEM, 2 * XW), BF16)],
        compiler_params=pltpu.CompilerParams(vmem_limit_bytes=32 << 20))(mem, g_mem, wkv16)


def _in_proj(x, g, w8, cos, sins, shards):
    blk = PW // N_DEV

    def body(x_ref, g_ref, w8_ref, c_ref, s_ref, q_ref, kv_ref, bcu_ref, qx_ref, h_ref, w_out, w_ref):
        @pl.when(pl.program_id(0) == 0)
        def _():
            for j in range(N_DEV):
                w_ref[:, j * blk:(j + 1) * blk] = w8_ref[j]
            w_out[...] = w_ref[...]

        y, _, _ = _rms(x_ref[...], g_ref[...])
        h = y.astype(BF16)
        h_ref[...] = h
        proj = _dot(h, w_ref[...])
        cos, sn = c_ref[...], s_ref[...]
        q, k = proj[:, 0:AW], proj[:, AW:2 * AW]
        q_ref[...] = (q * cos + _rot_half(q) * sn) * SCALE
        kv_ref[...] = _pack_pair(k * cos + _rot_half(k) * sn, proj[:, 2 * AW:3 * AW])
        bcu_ref[...] = proj[:, 3 * AW:3 * AW + 3 * CW]
        qx_ref[...] = (proj[:, 3 * AW + 3 * CW:] * SCALE).astype(BF16)

    def tile(w):
        return pl.BlockSpec((TQ, w), lambda i: (i, 0))

    return _call_with_gather(
        body, NT, shards, name="in_proj",
        in_specs=[tile(D), _const((1, D)), _const((N_DEV, D, blk)), tile(AW), tile(AW)],
        out_specs=[tile(AW), tile(AW), tile(3 * CW), tile(XW), tile(D), _acc((D, PW))],
        out_shape=[jax.ShapeDtypeStruct((S, AW), F32)] * 2 + [
            jax.ShapeDtypeStruct((S, 3 * CW), F32), jax.ShapeDtypeStruct((S, XW), BF16),
            jax.ShapeDtypeStruct((S, D), BF16), jax.ShapeDtypeStruct((D, PW), BF16)],
        scratch_shapes=[pltpu.VMEM((D, PW), BF16)], vmem_mb=56, args=(x, g, w8, cos, sins))


ATTN_PLANS = (("p1", 1, 128, 32), ("p4", 8, 64, 8), ("p16", 16, 128, 2))
PAD = 128
WIN = 256


ATTN_UNROLL = 8


def _fill_bias(tab, qblk, partner):
    qi = lax.broadcasted_iota(jnp.int32, (2 * qblk, WIN), 0) & (qblk - 1)
    kj = lax.broadcasted_iota(jnp.int32, (2 * qblk, WIN), 1)
    piece = kj >> (qblk.bit_length() - 1)
    kk = kj & (qblk - 1)
    prev = (piece & 1) == 0
    of_partner = piece >= 2
    for first in (0, 1):
        for par in (0, 1):
            lo = jnp.where(prev, (qblk if first else qi) + jnp.where(of_partner, par, 0), 0)
            hi = jnp.where(prev, qblk, qi + jnp.where(of_partner, par - 1, 0))
            tab[2 * first + par] = jnp.where((kk >= lo) & (kk <= hi), 0.0, NEG).astype(F32)


def _block_rows(g, qblk, nbc, partner):
    own = pl.ds(pl.multiple_of(PAD + g * qblk, qblk), qblk)
    first = ((g & (nbc - 1)) == 0).astype(jnp.int32)
    if partner:
        gp = jnp.bitwise_xor(g, 4 * nbc)
        wins = (pl.ds(pl.multiple_of(PAD + (g - 1) * qblk, qblk), 2 * qblk),
                pl.ds(pl.multiple_of(PAD + (gp - 1) * qblk, qblk), 2 * qblk))
        return own, wins, 2 * first + ((g >> ((4 * nbc).bit_length() - 1)) & 1)
    return own, (pl.ds(pl.multiple_of(PAD + (g - 1) * qblk, qblk), 2 * qblk),), 2 * first


def _pack_pair(lo, hi):
    lo_bits = lax.bitcast_convert_type(lo.astype(BF16).astype(F32), jnp.uint32) >> 16
    hi_bits = lax.bitcast_convert_type(hi.astype(BF16).astype(F32), jnp.uint32) & jnp.uint32(0xFFFF0000)
    return lax.bitcast_convert_type(hi_bits | lo_bits, F32)


def _unpack_pair(c):
    bits = lax.bitcast_convert_type(c, jnp.uint32)
    lo = lax.bitcast_convert_type(bits << 16, F32).astype(BF16)
    hi = lax.bitcast_convert_type(bits & jnp.uint32(0xFFFF0000), F32).astype(BF16)
    return lo, hi


def _window(ref, wins):
    parts = [ref[w, :] for w in wins]
    return parts[0] if len(parts) == 1 else jnp.concatenate(parts, axis=0)


def _stack_heads(t, lane):
    zero = jnp.zeros_like(t)
    return jnp.concatenate([jnp.where(lane < HEAD, t, zero), jnp.where(lane >= HEAD, t, zero)], axis=0)


def _unstack_heads(t2, lane):
    half = t2.shape[0] // 2
    return jnp.where(lane < HEAD, t2[0:half, :], t2[half:, :])


def _lanes_of(step):
    return pl.ds(pl.multiple_of(step * LANES, LANES), LANES)


def _whole_wait(buf, sem):
    whole = buf.at[pl.ds(PAD, S), :]
    return pltpu.make_async_copy(whole, whole, sem)


def _whole_waits(bufs, sems):
    return [_whole_wait(buf, sems.at[i]) for i, buf in enumerate(bufs)]


def _class_gather(views, bufs, sems, lanes):
    copies = []
    for i, (view, buf) in enumerate(zip(views, bufs)):
        if view.ndim == 2:
            copies.append(pltpu.make_async_copy(view.at[:, lanes], buf.at[pl.ds(PAD, S), :], sems.at[i]))
        else:
            per, n_cls = view.shape[0], view.shape[1]
            copies += [pltpu.make_async_copy(view.at[:, c, lanes], buf.at[pl.ds(PAD + c * per, per), :], sems.at[i])
                       for c in range(n_cls)]
    return copies


def _class_scatter(bufs, dsts, sems, lanes=None):
    copies = []
    for i, (buf, dst) in enumerate(zip(bufs, dsts)):
        if dst.ndim == 2:
            copies.append(pltpu.make_async_copy(buf.at[pl.ds(PAD, S), :], dst.at[:, lanes], sems.at[i]))
            continue
        per, n_cls = dst.shape[0], dst.shape[1]
        for c in range(n_cls):
            to = dst.at[:, c, :] if lanes is None else dst.at[:, c, lanes]
            copies.append(pltpu.make_async_copy(buf.at[pl.ds(PAD + c * per, per), :], to, sems.at[i]))
    return copies


def _start(copies):
    for cp in copies:
        cp.start()


def _wait(waits):
    for w in waits:
        w.wait()


def _attn_fwd(q, kvp, shards=()):
    views = [[a] + [a.reshape(S // n, n, AW) for _, n, _, _ in ATTN_PLANS[1:]] for a in (q, kvp)]
    flat = [views[a][p] for p in range(3) for a in range(2)]
    ng = len(shards)
    n_grid = AW // LANES

    def body(*refs):
        hbm = [refs[2 * p:2 * p + 2] for p in range(3)]
        refs = refs[6:]
        shard_refs, refs = refs[:ng], refs[ng:]
        y_ref, lt_ref = refs[0:2]
        whole_refs, refs = refs[2:2 + ng], refs[2 + ng:]
        bufs = [refs[2 * p:2 * p + 2] for p in range(3)]
        oc4, lc4, oc16, lc16, o4n, l4n, o16n, l16n, tab128, tab4, sem_in, sem_out = refs[6:18]
        step = pl.program_id(0)
        if ng:
            start_gather, finish_gather = _gather_steps(shard_refs, whole_refs, *refs[18:])
            pl.when(step == 0)(start_gather)
        now = [_class_gather(hbm[p], bufs[p], sem_in.at[p], _lanes_of(step)) for p in range(3)]
        nxt = [_class_gather(hbm[p], bufs[p], sem_in.at[p], _lanes_of(step + 1)) for p in range(3)]

        @pl.when(step == 0)
        def _():
            for p in range(3):
                _start(now[p])
                for b in bufs[p]:
                    b[0:PAD, :] = jnp.zeros((PAD, LANES), F32)
            _fill_bias(tab128, 128, False)
            _fill_bias(tab4, 64, True)

        def prefetch(p):
            pl.when(step + 1 < n_grid)(lambda: _start(nxt[p]))

        lane = lax.broadcasted_iota(jnp.int32, (1, LANES), 1)
        ones = jnp.ones((WIN, LANES), BF16)

        def run(plan, bq, bkv, tab, o_dst, l_dst, dst_pad):
            _, n_cls, qblk, nbc = plan
            partner = n_cls == 8

            def block(g, carry):
                own, wins, mask = _block_rows(g, qblk, nbc, partner)
                q2 = _stack_heads(bq[own, :].astype(BF16), lane)
                kw, vwin = _unpack_pair(_window(bkv, wins))
                vw = jnp.concatenate([vwin, ones], axis=1)
                s = _dot_nt(q2, kw) + tab[mask]
                m = jnp.max(s, axis=1, keepdims=True)
                oe = _dot(jnp.exp(s - m).astype(BF16), vw)
                den = oe[:, LANES:]
                dst = pl.ds(pl.multiple_of(dst_pad + g * qblk, qblk), qblk)
                o_dst[dst, :] = _unstack_heads(oe[:, 0:LANES] / den, lane)
                l_dst[dst, :] = _unstack_heads(m + jnp.log(den), lane)
                return carry
            lax.fori_loop(0, n_cls * nbc, block, 0, unroll=ATTN_UNROLL)

        _wait(_whole_waits(bufs[0], sem_in.at[0]))
        run(ATTN_PLANS[0], *bufs[0], tab128, y_ref, lt_ref, 0)
        prefetch(0)
        _wait(_whole_waits(bufs[1], sem_in.at[1]))
        run(ATTN_PLANS[1], *bufs[1], tab4, oc4, lc4, PAD)
        prefetch(1)
        _start(_class_scatter((oc4, lc4), (o4n, l4n), sem_out.at[0]))
        _wait(_whole_waits(bufs[2], sem_in.at[2]))
        run(ATTN_PLANS[2], *bufs[2], tab128, oc16, lc16, PAD)
        prefetch(2)
        _start(_class_scatter((oc16, lc16), (o16n, l16n), sem_out.at[1]))
        _wait(_whole_waits((oc4, lc4), sem_out.at[0]) + _whole_waits((oc16, lc16), sem_out.at[1]))

        for t in range(S // TQ):
            rows = pl.ds(t * TQ, TQ)
            r4, r16 = pl.ds(t * (TQ // 8), TQ // 8), pl.ds(t * (TQ // 16), TQ // 16)
            l0, l1, l2 = lt_ref[rows, :], l4n[r4, :, :].reshape(TQ, LANES), l16n[r16, :, :].reshape(TQ, LANES)
            lm = jnp.maximum(jnp.maximum(l0, l1), l2)
            e0, e1, e2 = jnp.exp(l0 - lm), jnp.exp(l1 - lm), jnp.exp(l2 - lm)
            den = e0 + e1 + e2
            y_ref[rows, :] = (e0 * y_ref[rows, :] + e1 * o4n[r4, :, :].reshape(TQ, LANES)
                              + e2 * o16n[r16, :, :].reshape(TQ, LANES)) / den
            lt_ref[rows, :] = lm + jnp.log(den)

        if ng:
            pl.when(step == n_grid - 1)(finish_gather)

    col = pl.BlockSpec((S, LANES), lambda h: (0, h))
    padded = pltpu.VMEM((PAD + S, LANES), F32)
    return pl.pallas_call(
        body, grid=(n_grid,), name="attn_fwd",
        in_specs=[ANY] * (6 + ng), out_specs=[col, col] + [ANY] * ng,
        out_shape=[jax.ShapeDtypeStruct((S, AW), F32)] * 2 + _gathered_shapes(shards),
        scratch_shapes=[padded] * 10 + [
            pltpu.VMEM((S // 8, 8, LANES), F32), pltpu.VMEM((S // 8, 8, LANES), F32),
            pltpu.VMEM((S // 16, 16, LANES), F32), pltpu.VMEM((S // 16, 16, LANES), F32),
            pltpu.VMEM((4, 256, WIN), F32), pltpu.VMEM((4, 128, WIN), F32),
            pltpu.SemaphoreType.DMA((3, 2)), pltpu.SemaphoreType.DMA((2, 2))]
        + (_gather_scratch(ng) if ng else []),
        compiler_params=_cparams(56))(*flat, *shards)


def _conv_taps(z, zprev, row):
    z1 = jnp.where(row == 0, zprev[7:8, :], pltpu.roll(z, 1, 0))
    z2 = jnp.where(row == 0, zprev[6:7, :], jnp.where(row == 1, zprev[7:8, :], pltpu.roll(z, 2, 0)))
    return z1, z2


def _xattn_scores(qm, km):
    s = _dot_nt(qm, km)
    m = jnp.max(s, axis=1, keepdims=True)
    e = jnp.exp(s - m)
    return e, jnp.sum(e, axis=1, keepdims=True)


def _mix_out(y_attn, bcu, qx16, kv16, cw8, g_attn, g_conv, g_x, g_post, wout16, x, shards):
    def body(ya_ref, bcu_ref, halo_ref, qx_ref, kv_ref, cw_ref, ga_ref, gc_ref, gx_ref, gp_ref, w_ref, x_ref,
             ypre_ref, y16_ref, y2_ref, x1_ref):
        i = pl.program_id(0)
        bcu = bcu_ref[...]
        b, c, u = bcu[:, 0:CW], bcu[:, CW:2 * CW], bcu[:, 2 * CW:]
        z = c * u
        halo = halo_ref[...]
        zprev = jnp.where(i > 0, halo[:, CW:2 * CW] * halo[:, 2 * CW:], 0.0)
        row = lax.broadcasted_iota(jnp.int32, z.shape, 0)
        z1, z2 = _conv_taps(z, zprev, row)
        cw = cw_ref[...]
        y_conv = b * (z2 * cw[0:1, :] + z1 * cw[1:2, :] + z * cw[2:3, :])

        qx = qx_ref[...]
        kv = kv_ref[...]
        km, vm = kv[:, 0:XW], kv[:, XW:]
        lane = lax.broadcasted_iota(jnp.int32, qx.shape, 1)
        y_x = jnp.zeros(qx.shape, F32)
        for h in range(XW // HEAD):
            hm = (lane >= h * HEAD) & (lane < (h + 1) * HEAD)
            e, l = _xattn_scores(jnp.where(hm, qx, jnp.zeros_like(qx)), km)
            y_x = jnp.where(hm, _dot(e.astype(BF16), vm) / l, y_x)

        y_attn = ya_ref[...]
        ypre_ref[:, 0:AW] = y_attn
        ypre_ref[:, AW:AW + CW] = y_conv
        ypre_ref[:, AW + CW:] = y_x
        y = jnp.concatenate([_rms(y_attn, ga_ref[...])[0], _rms(y_conv, gc_ref[...])[0],
                             _rms(y_x, gx_ref[...])[0]], axis=1).astype(BF16)
        y16_ref[...] = y
        y2 = _dot(y, w_ref[...])
        y2_ref[...] = y2
        x1_ref[...] = x_ref[...] + _rms(y2, gp_ref[...])[0]

    def tile(w):
        return pl.BlockSpec((TQ, w), lambda i: (i, 0))

    halo = pl.BlockSpec((SUBLANES, 3 * CW), lambda i: (jnp.maximum(i * (TQ // SUBLANES) - 1, 0), 0))
    return _call_with_gather(
        body, NT, shards, name="mix_out",
        in_specs=[tile(AW), tile(3 * CW), halo, tile(XW), _const((N_MEM, 2 * XW)), _const((SUBLANES, CW)),
                  _const((1, AW)), _const((1, CW)), _const((1, XW)), _const((1, D)), _const((D, D)), tile(D)],
        out_specs=[tile(D), tile(D), tile(D), tile(D)],
        out_shape=[jax.ShapeDtypeStruct((S, D), F32), jax.ShapeDtypeStruct((S, D), BF16),
                   jax.ShapeDtypeStruct((S, D), F32), jax.ShapeDtypeStruct((S, D), F32)],
        scratch_shapes=[], vmem_mb=56,
        args=(y_attn, bcu, bcu, qx16, kv16, cw8, g_attn, g_conv, g_x, g_post, wout16, x))


def _mlp(x1, tgt, g_pre, g_post, wup8, wdn16):
    tq = TQ_MLP

    def body(x1_ref, t_ref, g1_ref, g2_ref, wu_ref, wd_ref,
             a16_ref, du_ref, h2_ref, df2_ref, dx1_ref, loss_ref, dg_ref, a32):
        @pl.when(pl.program_id(0) == 0)
        def _():
            loss_ref[...] = jnp.zeros_like(loss_ref)
            dg_ref[...] = jnp.zeros_like(dg_ref)

        x1 = x1_ref[...]
        g1, g2 = g1_ref[...], g2_ref[...]
        y1, n1, r1 = _rms(x1, g1)
        h2 = y1.astype(BF16)
        h2_ref[...] = h2
        f2 = jnp.zeros((tq, D), F32)
        for j in range(N_DEV):
            cols = slice(j * FF_BLK, (j + 1) * FF_BLK)
            a = jnp.maximum(_dot(h2, wu_ref[j]), 0.0)
            a32[:, cols] = a
            a16_ref[:, cols] = a.astype(BF16)
            f2 = f2 + _dot((a * a).astype(BF16), wd_ref[cols, :])
        y2, n2, r2 = _rms(f2, g2)
        e = x1 + y2 - t_ref[...]
        sq = jnp.sum(jnp.sum(e * e, axis=1, keepdims=True), axis=0, keepdims=True)
        loss_ref[...] += jnp.broadcast_to(sq * (0.5 / D), loss_ref.shape)
        dout = e * (1.0 / D)
        df2, dg2 = _rms_bwd(dout, n2, r2, g2)
        df2_16 = df2.astype(BF16)
        df2_ref[...] = df2_16
        dh2 = jnp.zeros((tq, D), F32)
        for j in range(N_DEV):
            cols = slice(j * FF_BLK, (j + 1) * FF_BLK)
            du = (_dot_nt(df2_16, wd_ref[cols, :]) * (2.0 * a32[:, cols])).astype(BF16)
            du_ref[:, cols] = du
            dh2 = dh2 + _dot_nt(du, wu_ref[j])
        dx, dg1 = _rms_bwd(dh2, n1, r1, g1)
        dx1_ref[...] = dout + dx
        dg_ref[0:1, :] += dg2
        dg_ref[1:2, :] += dg1

    def tile(w):
        return pl.BlockSpec((tq, w), lambda i: (i, 0))

    return pl.pallas_call(
        body, grid=(S // tq,), name="mlp",
        in_specs=[tile(D), tile(D), _const((1, D)), _const((1, D)), _const((N_DEV, D, FF_BLK)), _const((FF, D))],
        out_specs=[tile(FF), tile(FF), tile(D), tile(D), tile(D), _acc((SUBLANES, LANES)), _acc((SUBLANES, D))],
        out_shape=[jax.ShapeDtypeStruct((S, FF), BF16), jax.ShapeDtypeStruct((S, FF), BF16),
                   jax.ShapeDtypeStruct((S, D), BF16), jax.ShapeDtypeStruct((S, D), BF16),
                   jax.ShapeDtypeStruct((S, D), F32), jax.ShapeDtypeStruct((SUBLANES, LANES), F32),
                   jax.ShapeDtypeStruct((SUBLANES, D), F32)],
        scratch_shapes=[pltpu.VMEM((tq, FF), F32)],
        compiler_params=_cparams(56))(x1, tgt, g_pre, g_post, wup8, wdn16)


def _mix_out_bwd(dx1, y2, ypre, ltot, head_ones, q, bcu, qx16, kv16, cw8, g_post, g_attn, g_conv, g_x, wout16):
    def body(dx1_ref, y2_ref, ypre_ref, lt_ref, e_ref, q_ref, bcu_ref, halo_ref, qx_ref, kv_ref, cw_ref, gp_ref,
             ga_ref, gc_ref, gx_ref, w_ref, dy2_ref, qdo_ref, ld_ref, dbcu_ref, dqx_ref, dgs_ref, dcw_ref, dkv_ref,
             carry):
        i = pl.program_id(0)

        @pl.when(i == 0)
        def _():
            dgs_ref[...] = jnp.zeros_like(dgs_ref)
            dcw_ref[...] = jnp.zeros_like(dcw_ref)
            dkv_ref[...] = jnp.zeros_like(dkv_ref)
            carry[...] = jnp.zeros_like(carry)

        gp = gp_ref[...]
        _, n, r = _rms(y2_ref[...], gp)
        dy2, dgp = _rms_bwd(dx1_ref[...], n, r, gp)
        dy2_16 = dy2.astype(BF16)
        dy2_ref[...] = dy2_16
        dy = _dot_nt(dy2_16, w_ref[...])

        ypre = ypre_ref[...]
        ga, gc, gx = ga_ref[...], gc_ref[...], gx_ref[...]
        _, na, ra = _rms(ypre[:, 0:AW], ga)
        dya, dga = _rms_bwd(dy[:, 0:AW], na, ra, ga)
        _, nc, rc = _rms(ypre[:, AW:AW + CW], gc)
        dyc, dgc = _rms_bwd(dy[:, AW:AW + CW], nc, rc, gc)
        y_x = ypre[:, AW + CW:]
        _, nx, rx = _rms(y_x, gx)
        dyx, dgx = _rms_bwd(dy[:, AW + CW:], nx, rx, gx)
        qdo_ref[...] = _pack_pair(q_ref[...], dya)
        prod = dya * ypre[:, 0:AW]
        hi = prod.astype(BF16)
        lo = (prod - hi.astype(F32)).astype(BF16)
        head_sum = _dot(hi, e_ref[...]) + _dot(lo, e_ref[...])
        lane_a = lax.broadcasted_iota(jnp.int32, prod.shape, 1)
        ld_ref[...] = jnp.where((lane_a % HEAD) < HEAD // 2, lt_ref[...], head_sum)
        dgs_ref[0:1, :] += dgp
        dgs_ref[1:2, :] += jnp.concatenate([dga, dgc, dgx], axis=1)

        bcu = bcu_ref[...]
        b, c, u = bcu[:, 0:CW], bcu[:, CW:2 * CW], bcu[:, 2 * CW:]
        z = c * u
        halo = halo_ref[...]
        zprev = jnp.where(i < NT - 1, halo[:, CW:2 * CW] * halo[:, 2 * CW:], 0.0)
        row = lax.broadcasted_iota(jnp.int32, z.shape, 0)
        z1, z2 = _conv_taps(z, zprev, row)
        cw = cw_ref[...]
        conv = z2 * cw[0:1, :] + z1 * cw[1:2, :] + z * cw[2:3, :]
        dconv = dyc * b
        nxt = carry[...]
        dn1 = jnp.where(row == TQ - 1, nxt[0:1, :], pltpu.roll(dconv, TQ - 1, 0))
        dn2 = jnp.where(row == TQ - 1, nxt[1:2, :], jnp.where(row == TQ - 2, nxt[0:1, :], pltpu.roll(dconv, TQ - 2, 0)))
        carry[...] = dconv[0:SUBLANES, :]
        dz = dconv * cw[2:3, :] + dn1 * cw[1:2, :] + dn2 * cw[0:1, :]
        dbcu_ref[:, 0:CW] = dyc * conv
        dbcu_ref[:, CW:2 * CW] = dz * u
        dbcu_ref[:, 2 * CW:] = dz * c
        dcw_ref[0:1, :] += jnp.sum(z2 * dconv, axis=0, keepdims=True)
        dcw_ref[1:2, :] += jnp.sum(z1 * dconv, axis=0, keepdims=True)
        dcw_ref[2:3, :] += jnp.sum(z * dconv, axis=0, keepdims=True)

        qx = qx_ref[...]
        kv = kv_ref[...]
        km, vm = kv[:, 0:XW], kv[:, XW:]
        lane = lax.broadcasted_iota(jnp.int32, qx.shape, 1)
        dqx = jnp.zeros(qx.shape, F32)
        dkm = jnp.zeros((N_MEM, XW), F32)
        dvm = jnp.zeros((N_MEM, XW), F32)
        for h in range(XW // HEAD):
            hm = (lane >= h * HEAD) & (lane < (h + 1) * HEAD)
            qm = jnp.where(hm, qx, jnp.zeros_like(qx))
            e, l = _xattn_scores(qm, km)
            p = e / l
            dom = jnp.where(hm, dyx, 0.0)
            do16 = dom.astype(BF16)
            dsum = jnp.sum(dom * y_x, axis=1, keepdims=True)
            ds = (p * (_dot_nt(do16, vm) - dsum)).astype(BF16)
            dqx = jnp.where(hm, _dot(ds, km), dqx)
            dkm = dkm + _dot_tn(ds, qm)
            dvm = dvm + _dot_tn(p.astype(BF16), do16)
        dqx_ref[...] = dqx * SCALE
        dkv_ref[:, 0:XW] += dkm
        dkv_ref[:, XW:] += dvm

    def tile(w):
        return pl.BlockSpec((TQ, w), lambda i: (NT - 1 - i, 0))

    halo = pl.BlockSpec((SUBLANES, 3 * CW), lambda i: (jnp.maximum((NT - 1 - i) * (TQ // SUBLANES) - 1, 0), 0))
    return pl.pallas_call(
        body, grid=(NT,), name="mix_out_bwd",
        in_specs=[tile(D), tile(D), tile(D), tile(AW), _const((AW, AW)), tile(AW), tile(3 * CW), halo, tile(XW),
                  _const((N_MEM, 2 * XW)), _const((SUBLANES, CW)), _const((1, D)), _const((1, AW)), _const((1, CW)),
                  _const((1, XW)), _const((D, D))],
        out_specs=[tile(D), tile(AW), tile(AW), tile(3 * CW), tile(XW), _acc((SUBLANES, D)), _acc((SUBLANES, CW)),
                   _acc((N_MEM, 2 * XW))],
        out_shape=[jax.ShapeDtypeStruct((S, D), BF16), jax.ShapeDtypeStruct((S, AW), F32),
                   jax.ShapeDtypeStruct((S, AW), F32),
                   jax.ShapeDtypeStruct((S, 3 * CW), F32), jax.ShapeDtypeStruct((S, XW), F32),
                   jax.ShapeDtypeStruct((SUBLANES, D), F32), jax.ShapeDtypeStruct((SUBLANES, CW), F32),
                   jax.ShapeDtypeStruct((N_MEM, 2 * XW), F32)],
        scratch_shapes=[pltpu.VMEM((SUBLANES, CW), F32)],
        compiler_params=_cparams(56))(dx1, y2, ypre, ltot, head_ones, q, bcu, bcu, qx16, kv16, cw8, g_post, g_attn,
                                      g_conv, g_x, wout16)


def _attn_bwd(qdo, kvp, ld, chip_sums=()):
    n_in = 3
    views = [[a] + [a.reshape(S // n, n, AW) for _, n, _, _ in ATTN_PLANS[1:]] for a in (qdo, kvp, ld)]
    flat = [views[a][p] for p in range(3) for a in range(n_in)]
    ns = len(chip_sums)
    n_grid = AW // LANES

    def body(*refs):
        hbm = [refs[n_in * p:n_in * p + n_in] for p in range(3)]
        refs = refs[3 * n_in:]
        sum_refs, refs = refs[:ns], refs[ns:]
        outs = [refs[3 * p:3 * p + 3] for p in range(3)]
        landed_refs, sc = refs[9:9 + ns], refs[9 + ns:]
        bufs = [sc[3 * p:3 * p + 3] for p in range(3)]
        res = [sc[9 + 3 * p:12 + 3 * p] for p in range(3)]
        tab128, tab4, sem_in, sem_out = sc[18:22]
        step = pl.program_id(0)
        if ns:
            start_chips, finish_chips = _chips_steps(sum_refs, landed_refs, *sc[22:])
            pl.when(step == 0)(start_chips)
        now = [_class_gather(hbm[p], bufs[p], sem_in.at[p], _lanes_of(step)) for p in range(3)]
        nxt = [_class_gather(hbm[p], bufs[p], sem_in.at[p], _lanes_of(step + 1)) for p in range(3)]

        @pl.when(step == 0)
        def _():
            for p in range(3):
                _start(now[p])
                for b in bufs[p]:
                    b[0:PAD, :] = jnp.zeros((PAD, LANES), F32)
            _fill_bias(tab128, 128, False)
            _fill_bias(tab4, 64, True)

        def prefetch(p):
            pl.when(step + 1 < n_grid)(lambda: _start(nxt[p]))

        for p in range(3):
            for b in res[p]:
                b[...] = jnp.zeros_like(b)
        lane = lax.broadcasted_iota(jnp.int32, (1, LANES), 1)

        def run(plan, plan_bufs, tab, dst):
            _, n_cls, qblk, nbc = plan
            partner = n_cls == 8
            bqdo, bkv, bld = plan_bufs
            rq, rk, rv = dst

            def block(g, carry):
                own, wins, mask = _block_rows(g, qblk, nbc, partner)
                qb, dob = _unpack_pair(bqdo[own, :])
                q2, do2 = _stack_heads(qb, lane), _stack_heads(dob, lane)
                kw, vw = _unpack_pair(_window(bkv, wins))
                ldv = bld[own, :]
                half = HEAD // 2
                lt2 = jnp.concatenate([ldv[:, 0:1], ldv[:, HEAD:HEAD + 1]], axis=0)
                dsum2 = jnp.concatenate([ldv[:, half:half + 1], ldv[:, HEAD + half:HEAD + half + 1]], axis=0)
                p = jnp.exp(_dot_nt(q2, kw) + tab[mask] - lt2)
                ds = (p * (_dot_nt(do2, vw) - dsum2)).astype(BF16)
                rq[own, :] = _unstack_heads(_dot(ds, kw), lane)
                dkw = _dot_tn(ds, q2)
                dvw = _dot_tn(p.astype(BF16), do2)
                n_w = WIN // len(wins)
                for i, w in enumerate(wins):
                    rk[w, :] += dkw[i * n_w:(i + 1) * n_w, :]
                    rv[w, :] += dvw[i * n_w:(i + 1) * n_w, :]
                return carry
            lax.fori_loop(0, n_cls * nbc, block, 0, unroll=ATTN_UNROLL)

        tabs = (tab128, tab4, tab128)
        for p in range(3):
            _wait(_whole_waits(bufs[p], sem_in.at[p]))
            run(ATTN_PLANS[p], bufs[p], tabs[p], res[p])
            prefetch(p)
            _start(_class_scatter(res[p], outs[p], sem_out.at[p], _lanes_of(step)))
        for p in range(3):
            _wait(_whole_waits(res[p], sem_out.at[p]))
        if ns:
            pl.when(step == n_grid - 1)(finish_chips)

    padded = pltpu.VMEM((PAD + S, LANES), F32)
    shapes = [jax.ShapeDtypeStruct(views[0][p].shape, F32) for p in range(3) for _ in range(3)]
    out = pl.pallas_call(
        body, grid=(n_grid,), name="attn_bwd",
        in_specs=[ANY] * (3 * n_in + ns), out_specs=[ANY] * (9 + ns),
        out_shape=shapes + _chips_shapes(chip_sums),
        scratch_shapes=[padded] * 18
        + [pltpu.VMEM((4, 256, WIN), F32), pltpu.VMEM((4, 128, WIN), F32),
           pltpu.SemaphoreType.DMA((3, n_in)), pltpu.SemaphoreType.DMA((3, 3))]
        + (_chips_scratch(ns) if ns else []),
        compiler_params=_cparams(56))(*flat, *chip_sums)
    return [o.reshape(S, AW) for o in out[:9]] + list(out[9:])


def _in_proj_bwd(dqkv, dbcu, dqx, cos, sins, w16, x, g, dx1):
    tq = TQ // 2

    def body(*refs):
        parts = refs[0:9]
        dbcu_ref, dqx_ref, c_ref, s_ref, w_ref, x_ref, g_ref, dx1_ref, dp_ref, gx_ref, dg_ref = refs[9:]

        @pl.when(pl.program_id(0) == 0)
        def _():
            dg_ref[...] = jnp.zeros_like(dg_ref)

        dq, dk, dv = (parts[i][...] + parts[3 + i][...] + parts[6 + i][...] for i in range(3))
        cos, sn = c_ref[...], s_ref[...]
        dqr = dq * SCALE
        dkr = dk
        dp = jnp.concatenate([dqr * cos + _rot_half(dqr * sn), dkr * cos + _rot_half(dkr * sn), dv,
                              dbcu_ref[...], dqx_ref[...]], axis=1).astype(BF16)
        dp_ref[...] = dp
        dh = _dot_nt(dp, w_ref[...])
        g = g_ref[...]
        _, n, r = _rms(x_ref[...], g)
        dx, dg = _rms_bwd(dh, n, r, g)
        gx_ref[...] = dx1_ref[...] + dx
        dg_ref[0:1, :] += dg

    def tile(w):
        return pl.BlockSpec((tq, w), lambda i: (i, 0))

    return pl.pallas_call(
        body, grid=(S // tq,), name="in_proj_bwd",
        in_specs=[tile(AW)] * 9 + [tile(3 * CW), tile(XW), tile(AW), tile(AW), _const((D, PW)),
                                   tile(D), _const((1, D)), tile(D)],
        out_specs=[tile(PW), tile(D), _acc((SUBLANES, D))],
        out_shape=[jax.ShapeDtypeStruct((S, PW), BF16), jax.ShapeDtypeStruct((S, D), F32),
                   jax.ShapeDtypeStruct((SUBLANES, D), F32)],
        compiler_params=_cparams(56))(*dqkv, dbcu, dqx, cos, sins, w16, x, g, dx1)


def _mem_bwd(mem, g_mem, wkv16, dkv):
    def body(m_ref, g_ref, w_ref, dkv_ref, dkv16_ref, dg_ref):
        dkv16 = dkv_ref[...].astype(BF16)
        dkv16_ref[...] = dkv16
        _, n, _ = _rms(m_ref[...], g_ref[...])
        dg = jnp.sum(_dot_nt(dkv16, w_ref[...]) * n, axis=0, keepdims=True)
        dg_ref[...] = jnp.broadcast_to(dg, dg_ref.shape)

    return pl.pallas_call(
        body, name="mem_bwd",
        out_shape=[jax.ShapeDtypeStruct((N_MEM, 2 * XW), BF16), jax.ShapeDtypeStruct((SUBLANES, D), F32)],
        compiler_params=pltpu.CompilerParams(vmem_limit_bytes=32 << 20))(mem, g_mem, wkv16, dkv)


N_CHIPS = N_DEV // 2


def _transpose_into(at, a_ref):
    kk = a_ref.shape[0]
    chunk = min(kk, 512)
    for c in range(kk // chunk):
        at[:, c * chunk:(c + 1) * chunk] = a_ref[c * chunk:(c + 1) * chunk, :].T


def _pair_scratch(block):
    return [pltpu.VMEM((N_CHIPS,) + block, BF16), pltpu.VMEM((N_CHIPS,) + block, BF16),
            pltpu.SemaphoreType.DMA((N_CHIPS,)), pltpu.SemaphoreType.DMA((N_CHIPS,))]


def _swap_with_sibling(p, stage, land, send, recv):
    x, y, c = lax.axis_index("x"), lax.axis_index("y"), lax.axis_index("c")
    return pltpu.make_async_remote_copy(src_ref=stage.at[p], dst_ref=land.at[p], send_sem=send.at[p],
                                        recv_sem=recv.at[p], device_id=(x, y, 1 - c), device_id_type=MESH)


def _wgrad_cols(place, a16, b16, blk, name, square_b=False, transpose_out=False):
    kk, m = a16.shape
    aligned = blk % LANES == 0
    wide = blk if aligned else -(-(blk + LANES // 2) // LANES) * LANES
    block = (blk, m) if transpose_out else (m, blk)

    def body(pl_ref, a_ref, *refs):
        if aligned:
            b_sib, b_mine, cs_ref, own_ref, at, stage, land, send, recv = refs
        else:
            b_hbm, cs_ref, own_ref, at, stage, land, send, recv, win, wsem = refs
        p = pl.program_id(0)
        c = pl_ref[0]

        def fetch(step, mine):
            j = 2 * step + jnp.where(mine, c, 1 - c)
            first = pl.multiple_of(((j * blk) >> 7) << 7, LANES)
            slot = 2 * (step & 1) + mine
            return pltpu.make_async_copy(b_hbm.at[:, pl.ds(first, wide)], win.at[slot], wsem.at[slot])

        @pl.when(p == 0)
        def _():
            if not aligned:
                fetch(0, 0).start()
                fetch(0, 1).start()
            _transpose_into(at, a_ref)

        if not aligned:
            @pl.when(p + 1 < N_CHIPS)
            def _():
                fetch(p + 1, 0).start()
                fetch(p + 1, 1).start()

        def partial(mine):
            if aligned:
                b = (b_mine if mine else b_sib)[...]
                if square_b:
                    b = b * b
                acc = _dot(at[...], b)
            else:
                fetch(p, mine).wait()
                acc = _dot(at[...], win[2 * (p & 1) + mine])
                odd = c if mine else 1 - c
                acc = pltpu.roll(acc, jnp.where(odd == 0, 0, wide - LANES // 2), 1)[:, 0:blk]
            return acc.T if transpose_out else acc

        stage[p] = partial(0).astype(BF16)
        swap = _swap_with_sibling(p, stage, land, send, recv)
        swap.start()
        mine = partial(1)
        swap.wait()
        total = mine + land[p].astype(F32)
        cs_ref[0] = total.astype(BF16)

        @pl.when(p == pl_ref[1])
        def _():
            own_ref[...] = total

    if aligned:
        b_specs = [pl.BlockSpec((kk, blk), lambda p, s: (0, 2 * p + 1 - s[0])),
                   pl.BlockSpec((kk, blk), lambda p, s: (0, 2 * p + s[0]))]
        b_args, extra = (b16, b16), []
    else:
        b_specs, b_args = [ANY], (b16,)
        extra = [pltpu.VMEM((4, kk, wide), BF16), pltpu.SemaphoreType.DMA((4,))]
    return pl.pallas_call(
        body, name=name,
        grid_spec=pltpu.PrefetchScalarGridSpec(
            num_scalar_prefetch=1, grid=(N_CHIPS,),
            in_specs=[pl.BlockSpec((kk, m), lambda p, s: (0, 0), pipeline_mode=pl.Buffered(1))] + b_specs,
            out_specs=[pl.BlockSpec((1,) + block, lambda p, s: (p, 0, 0)), pl.BlockSpec(block, lambda p, s: (0, 0))],
            scratch_shapes=[pltpu.VMEM((m, kk), BF16)] + _pair_scratch(block) + extra),
        out_shape=[jax.ShapeDtypeStruct((N_CHIPS,) + block, BF16), jax.ShapeDtypeStruct(block, F32)],
        compiler_params=_cparams(56))(place, a16, *b_args)


def _wgrad_rows(place, a16, b16, name):
    kk, m = a16.shape
    n = b16.shape[1]
    block = (m // N_DEV, n)

    def body(pl_ref, a_ref, b_ref, cs_ref, own_ref, at, acc, stage, land, send, recv):
        c = pl_ref[0]
        _transpose_into(at, a_ref)
        acc[...] = _dot(at[...], b_ref[...])

        def rows(owner):
            return pl.ds(pl.multiple_of(owner * block[0], block[0]), block[0])

        swaps = []
        for p in range(N_CHIPS):
            stage[p] = acc[rows(2 * p + 1 - c), :].astype(BF16)
            swaps.append(_swap_with_sibling(p, stage, land, send, recv))
            swaps[-1].start()
        for p in range(N_CHIPS):
            swaps[p].wait()
            total = acc[rows(2 * p + c), :] + land[p].astype(F32)
            cs_ref[p] = total.astype(BF16)

            @pl.when(p == pl_ref[1])
            def _():
                own_ref[...] = total

    vmem = pl.BlockSpec(memory_space=pltpu.VMEM)
    return pl.pallas_call(
        body, name=name,
        in_specs=[pl.BlockSpec(memory_space=pltpu.SMEM), vmem, vmem], out_specs=[vmem, vmem],
        out_shape=[jax.ShapeDtypeStruct((N_CHIPS,) + block, BF16), jax.ShapeDtypeStruct(block, F32)],
        scratch_shapes=[pltpu.VMEM((m, kk), BF16), pltpu.VMEM((m, n), F32)] + _pair_scratch(block),
        compiler_params=pltpu.CompilerParams(vmem_limit_bytes=56 << 20))(place, a16, b16)


def _adamw_math(w, g, m, v):
    m = ADAM_B1 * m + (1.0 - ADAM_B1) * g
    v = ADAM_B2 * v + (1.0 - ADAM_B2) * jnp.square(g)
    m_hat = m / (1.0 - ADAM_B1 ** ADAM_STEP)
    v_hat = v / (1.0 - ADAM_B2 ** ADAM_STEP)
    delta = -ADAM_LR * (m_hat / (jnp.sqrt(v_hat) + ADAM_EPS) + ADAM_WD * w)
    return delta, m, v


def _adamw_shards(updates, name, chip_sums=()):
    names, nu, ns = list(updates), len(updates), len(chip_sums)

    def body(*refs):
        ins, sum_refs = refs[:5 * nu], refs[5 * nu:5 * nu + ns]
        outs = refs[5 * nu + ns:9 * nu + ns]
        landed_refs, scratch = refs[9 * nu + ns:9 * nu + 2 * ns], refs[9 * nu + 2 * ns:]
        if ns:
            start_chips, finish_chips = _chips_steps(sum_refs, landed_refs, *scratch)
            start_chips()
        for i in range(nu):
            o_ref, r_ref, w_ref, m_ref, v_ref = ins[5 * i:5 * i + 5]
            g_out, d_out, m_out, v_out = outs[4 * i:4 * i + 4]
            g = o_ref[...] + r_ref[0].astype(F32) + r_ref[1].astype(F32) + r_ref[2].astype(F32)
            g_out[...] = g
            d_out[...], m_out[...], v_out[...] = _adamw_math(w_ref[...], g, m_ref[...], v_ref[...])
        if ns:
            finish_chips()

    vmem = pl.BlockSpec(memory_space=pltpu.VMEM)
    out = pl.pallas_call(
        body, name=name,
        in_specs=[vmem] * (5 * nu) + [ANY] * ns, out_specs=[vmem] * (4 * nu) + [ANY] * ns,
        out_shape=[jax.ShapeDtypeStruct(updates[n][2].shape, F32) for n in names for _ in range(4)]
        + _chips_shapes(chip_sums),
        scratch_shapes=_chips_scratch(ns) if ns else [],
        compiler_params=pltpu.CompilerParams(vmem_limit_bytes=56 << 20),
    )(*[a for n in names for a in updates[n]], *chip_sums)
    return {n: out[4 * i:4 * i + 4] for i, n in enumerate(names)}, list(out[4 * nu:])


def _place():
    x, y, c = lax.axis_index("x"), lax.axis_index("y"), lax.axis_index("c")
    chips = [(1 - x, y), (x, 1 - y), (1 - x, 1 - y)]
    return x, y, c, chips


def _gather_steps(ins, outs, send, recv, lsem):
    nt = len(ins)
    x, y, c, chips = _place()
    me, sib = (x, y, c), (x, y, 1 - c)

    def slot(t, px, py, pc):
        return outs[t].at[4 * px + 2 * py + pc]

    def copy(t, k, block, to, src=None):
        return pltpu.make_async_remote_copy(
            src_ref=slot(t, *block) if src is None else src, dst_ref=slot(t, *block),
            send_sem=send.at[t, k], recv_sem=recv.at[t, k], device_id=to, device_id_type=MESH)

    mine = [pltpu.make_async_copy(ins[t], slot(t, *me), lsem.at[t]) for t in range(nt)]
    first = []
    for t in range(nt):
        first.append(copy(t, 0, me, sib, src=ins[t]))
        first += [copy(t, 1 + j, me, (*chip, c), src=ins[t]) for j, chip in enumerate(chips)]

    def start():
        for cp in mine + first:
            cp.start()

    def finish():
        passed = []
        for j, chip in enumerate(chips):
            for t in range(nt):
                copy(t, 1 + j, (*chip, c), me).wait_recv()
                fwd = copy(t, 4 + j, (*chip, c), sib)
                fwd.start()
                passed.append(fwd)
        for t in range(nt):
            copy(t, 0, sib, me).wait_recv()
            for j, chip in enumerate(chips):
                copy(t, 4 + j, (*chip, 1 - c), me).wait_recv()
        for cp in first + passed:
            cp.wait_send()
        for cp in mine:
            cp.wait()

    return start, finish


def _gather_scratch(nt):
    return [pltpu.SemaphoreType.DMA((nt, 7)), pltpu.SemaphoreType.DMA((nt, 7)), pltpu.SemaphoreType.DMA((nt,))]


def _gathered_shapes(shards):
    return [jax.ShapeDtypeStruct((N_DEV,) + s.shape, s.dtype) for s in shards]


def _call_with_gather(body, n_grid, shards, *, name, in_specs, out_specs, out_shape, scratch_shapes, vmem_mb, args):
    ng, n_in, n_out = len(shards), len(in_specs), len(out_specs)

    def wrapped(*refs):
        ins, shard_refs = refs[:n_in], refs[n_in:n_in + ng]
        outs = refs[n_in + ng:n_in + ng + n_out]
        whole_refs = refs[n_in + ng + n_out:n_in + 2 * ng + n_out]
        scratch = refs[n_in + 2 * ng + n_out:]
        if ng:
            start, finish = _gather_steps(shard_refs, whole_refs, *scratch[len(scratch_shapes):])
            pl.when(pl.program_id(0) == 0)(start)
        body(*ins, *outs, *scratch[:len(scratch_shapes)])
        if ng:
            pl.when(pl.program_id(0) == n_grid - 1)(finish)

    return pl.pallas_call(
        wrapped, grid=(n_grid,), name=name,
        in_specs=list(in_specs) + [ANY] * ng, out_specs=list(out_specs) + [ANY] * ng,
        out_shape=list(out_shape) + _gathered_shapes(shards),
        scratch_shapes=list(scratch_shapes) + (_gather_scratch(ng) if ng else []),
        compiler_params=_cparams(vmem_mb))(*args, *shards)


def _chips_steps(ins, outs, send, recv):
    _, _, c, chips = _place()
    copies = [pltpu.make_async_remote_copy(
        src_ref=ins[t].at[2 * px + py], dst_ref=outs[t].at[j], send_sem=send.at[t, j], recv_sem=recv.at[t, j],
        device_id=(px, py, c), device_id_type=MESH) for t in range(len(ins)) for j, (px, py) in enumerate(chips)]

    def start():
        for cp in copies:
            cp.start()

    def finish():
        for cp in copies:
            cp.wait()

    return start, finish


def _chips_scratch(nt):
    return [pltpu.SemaphoreType.DMA((nt, 3)), pltpu.SemaphoreType.DMA((nt, 3))]


def _chips_shapes(cs16s):
    return [jax.ShapeDtypeStruct((3,) + g.shape[1:], g.dtype) for g in cs16s]


SMALL = (("g_pre_mix", 0, 0, D), ("g_mem", 1, 0, D), ("g_post_mix", 2, 0, D), ("g_attn_out", 3, 0, AW),
         ("g_conv_out", 3, AW, CW), ("g_xattn_out", 3, AW + CW, XW), ("g_post_mlp", 4, 0, D), ("g_pre_mlp", 5, 0, D))
CONV_ROW = 8
PACK_ROWS = 16


LOSS_ROW = 15


def _small_all_reduce(dg_in, dg_mem, dgs, dg_mlp, dcw, loss8):
    def body(acc_in, acc_mem, acc_mix, acc_mlp, acc_cw, acc_loss, tot_ref, pack, land, send, recv):
        x, y, c, _ = _place()
        me = 4 * x + 2 * y + c
        pack[...] = jnp.zeros_like(pack)
        pack[0:1, :] = acc_in[0:1, :]
        pack[1:2, :] = acc_mem[0:1, :]
        pack[2:4, :] = acc_mix[0:2, :]
        pack[4:6, :] = acc_mlp[0:2, :]
        pack[CONV_ROW:CONV_ROW + 3, 0:CW] = acc_cw[0:3, :]
        pack[LOSS_ROW:LOSS_ROW + 1, 0:LANES] = acc_loss[0:1, :]
        land[me] = pack[...]
        copies = []
        for k in range(1, N_DEV):
            kx, ky, kc = (k >> 2) & 1, (k >> 1) & 1, k & 1
            peer = (1 - x if kx else x, 1 - y if ky else y, 1 - c if kc else c)
            copies.append(pltpu.make_async_remote_copy(
                src_ref=pack, dst_ref=land.at[me], send_sem=send.at[k - 1], recv_sem=recv.at[k - 1],
                device_id=peer, device_id_type=MESH))
        for cp in copies:
            cp.start()
        for cp in copies:
            cp.wait()
        tot = land[0]
        for s in range(1, N_DEV):
            tot = tot + land[s]
        tot_ref[...] = tot

    return pl.pallas_call(
        body, name="small_all_reduce", out_shape=jax.ShapeDtypeStruct((PACK_ROWS, D), F32),
        scratch_shapes=[pltpu.VMEM((PACK_ROWS, D), F32), pltpu.VMEM((N_DEV, PACK_ROWS, D), F32),
                        pltpu.SemaphoreType.DMA((N_DEV - 1,)), pltpu.SemaphoreType.DMA((N_DEV - 1,))],
    )(dg_in, dg_mem, dgs, dg_mlp, dcw, loss8)


def _small_update(tot, me, params):
    flat = [a for n, _, _, _ in SMALL for a in params[n]] + list(params["conv_w"])
    n_par = len(SMALL) + 1
    tap_cols = CW // N_DEV

    def body(*refs):
        me_ref, tot_ref = refs[0:2]
        ins = refs[2:2 + 3 * n_par]
        loss_out = refs[2 + 3 * n_par]
        outs = refs[3 + 3 * n_par:]
        tot = tot_ref[...]
        loss_out[...] = jnp.broadcast_to(tot[LOSS_ROW:LOSS_ROW + 1, 0:LANES], loss_out.shape)

        def update(i, g):
            w_ref, m_ref, v_ref = ins[3 * i:3 * i + 3]
            g_out, d_out, m_out, v_out = outs[4 * i:4 * i + 4]
            g_out[...] = g
            d_out[...], m_out[...], v_out[...] = _adamw_math(w_ref[...], g, m_ref[...], v_ref[...])

        for i, (_, row, lane0, width) in enumerate(SMALL):
            update(i, tot[row:row + 1, lane0:lane0 + width])
        me = me_ref[0]
        taps = pltpu.roll(tot[CONV_ROW:CONV_ROW + SUBLANES, 0:CW], jnp.where(me == 0, 0, CW - me * tap_cols), 1)
        update(n_par - 1, taps[0:3, 0:tap_cols])

    shapes = [jax.ShapeDtypeStruct(params[n][0].shape, F32) for n, _, _, _ in SMALL] + [
        jax.ShapeDtypeStruct(params["conv_w"][0].shape, F32)]
    vmem = pl.BlockSpec(memory_space=pltpu.VMEM)
    loss, *out = pl.pallas_call(
        body, name="small_update",
        in_specs=[pl.BlockSpec(memory_space=pltpu.SMEM)] + [vmem] * (1 + 3 * n_par),
        out_shape=[jax.ShapeDtypeStruct((SUBLANES, LANES), F32)] + [s for s in shapes for _ in range(4)],
    )(me, tot, *flat)
    names = [n for n, _, _, _ in SMALL] + ["conv_w"]
    return loss[0, 0], {n: out[4 * i:4 * i + 4] for i, n in enumerate(names)}


def _local_step(x, mem, pos, gains, shards, tgt, place):
    half = HEAD // 2
    inv_freq = jnp.float32(ROPE_THETA) ** (-(jnp.arange(half, dtype=F32) * 2.0 / HEAD))
    invf = jnp.tile(inv_freq, LANES // half)[None, :]
    sgn = jnp.tile(jnp.concatenate([-jnp.ones((half,), F32), jnp.ones((half,), F32)]), LANES // HEAD)[None, :]
    cos, sins, win8 = _rope_table(pos.astype(F32).reshape(S, 1), invf, sgn, [shards["w_in"]])
    q, kvp, bcu, qx16, h16, win16, wout8, wkv8, conv8 = _in_proj(
        x, gains["g_pre_mix"], win8, cos, sins, [shards["w_out"], shards["w_mem_kv"], shards["conv_w"]])
    wout16, wkv16 = wout8.reshape(D, D), wkv8.reshape(D, 2 * XW)
    cw_full = conv8[:, 0:3, 0:CW // N_DEV].transpose(1, 0, 2).reshape(3, CW)
    cw8 = jnp.zeros((SUBLANES, CW), F32).at[0:3].set(cw_full)
    y_attn, ltot, wup8, wdn8 = _attn_fwd(q, kvp, [shards["w_up"], shards["w_down"]])
    wdn16 = wdn8.reshape(FF, D)
    memn16, kv16 = _mem_fwd(mem, gains["g_mem"], wkv16)
    ypre, y16, y2, x1 = _mix_out(y_attn, bcu, qx16, kv16, cw8, gains["g_attn_out"], gains["g_conv_out"],
                                 gains["g_xattn_out"], gains["g_post_mix"], wout16, x, [])
    a16, du16, h2_16, df2_16, dx1, loss8, dg_mlp = _mlp(x1, tgt, gains["g_pre_mlp"], gains["g_post_mlp"], wup8, wdn16)

    sums = {"w_up": _wgrad_cols(place, h2_16, du16, FF_BLK, "wgrad_up"),
            "w_down": _wgrad_cols(place, df2_16, a16, FF_BLK, "wgrad_down", square_b=True, transpose_out=True)}

    head_id = jnp.arange(AW, dtype=jnp.int32) // HEAD
    head_ones = (head_id[:, None] == head_id[None, :]).astype(BF16)
    dy2_16, qdo, ld, dbcu, dqx, dgs, dcw, dkv = _mix_out_bwd(
        dx1, y2, ypre, ltot, head_ones, q, bcu, qx16, kv16, cw8, gains["g_post_mix"], gains["g_attn_out"],
        gains["g_conv_out"], gains["g_xattn_out"], wout16)
    dkv16, dg_mem = _mem_bwd(mem, gains["g_mem"], wkv16, dkv)
    sums["w_mem_kv"] = _wgrad_rows(place, memn16, dkv16, "wgrad_mem_kv")
    sums["w_out"] = _wgrad_rows(place, y16, dy2_16, "wgrad_out")
    out = _attn_bwd(qdo, kvp, ld, [s[0] for s in sums.values()])
    dqkv, landed = out[:9], out[9:]
    reduced = {n: (s[1], landed[t]) for t, (n, s) in enumerate(sums.items())}
    dproj16, grad_x, dg_in = _in_proj_bwd(dqkv, dbcu, dqx, cos, sins, win16, x, gains["g_pre_mix"], dx1)

    in_sums = _wgrad_cols(place, h16, dproj16, PW // N_DEV, "wgrad_in")
    return grad_x, reduced, in_sums, (dg_in, dg_mem, dgs, dg_mlp, dcw, loss8)


BIG = ("w_in", "w_mem_kv", "w_out", "w_up", "w_down")
ORDER = ("g_pre_mix", "g_mem", "w_in", "w_mem_kv", "conv_w", "g_attn_out", "g_conv_out", "g_xattn_out", "w_out",
         "g_post_mix", "g_pre_mlp", "w_up", "w_down", "g_post_mlp")


def kernel(x, mem, positions, g_pre_mix, g_mem, w_in, w_mem_kv, conv_w, g_attn_out, g_conv_out, g_xattn_out, w_out, g_post_mix, g_pre_mlp, w_up, w_down, g_post_mlp, loss_target, m_g_pre_mix, m_g_mem, m_w_in, m_w_mem_kv, m_conv_w, m_g_attn_out, m_g_conv_out, m_g_xattn_out, m_w_out, m_g_post_mix, m_g_pre_mlp, m_w_up, m_w_down, m_g_post_mlp, v_g_pre_mix, v_g_mem, v_w_in, v_w_mem_kv, v_conv_w, v_g_attn_out, v_g_conv_out, v_g_xattn_out, v_w_out, v_g_post_mix, v_g_pre_mlp, v_w_up, v_w_down, v_g_post_mlp):
    w = dict(g_pre_mix=g_pre_mix, g_mem=g_mem, w_in=w_in, w_mem_kv=w_mem_kv, conv_w=conv_w, g_attn_out=g_attn_out,
             g_conv_out=g_conv_out, g_xattn_out=g_xattn_out, w_out=w_out, g_post_mix=g_post_mix, g_pre_mlp=g_pre_mlp,
             w_up=w_up, w_down=w_down, g_post_mlp=g_post_mlp)
    mo = dict(g_pre_mix=m_g_pre_mix, g_mem=m_g_mem, w_in=m_w_in, w_mem_kv=m_w_mem_kv, conv_w=m_conv_w,
              g_attn_out=m_g_attn_out, g_conv_out=m_g_conv_out, g_xattn_out=m_g_xattn_out, w_out=m_w_out,
              g_post_mix=m_g_post_mix, g_pre_mlp=m_g_pre_mlp, w_up=m_w_up, w_down=m_w_down, g_post_mlp=m_g_post_mlp)
    vo = dict(g_pre_mix=v_g_pre_mix, g_mem=v_g_mem, w_in=v_w_in, w_mem_kv=v_w_mem_kv, conv_w=v_conv_w,
              g_attn_out=v_g_attn_out, g_conv_out=v_g_conv_out, g_xattn_out=v_g_xattn_out, w_out=v_w_out,
              g_post_mix=v_g_post_mix, g_pre_mlp=v_g_pre_mlp, w_up=v_w_up, w_down=v_w_down, g_post_mlp=v_g_post_mlp)

    xi, yi, ci = lax.axis_index("x"), lax.axis_index("y"), lax.axis_index("c")
    me = 4 * xi + 2 * yi + ci
    place = jnp.stack([ci, 2 * xi + yi]).astype(jnp.int32)

    shards = {n: w[n][0].astype(BF16) for n in BIG}
    shards["conv_w"] = jnp.zeros((SUBLANES, LANES), F32).at[0:3, 0:CW // N_DEV].set(conv_w[0])

    gains = {n: w[n] for n, _, _, _ in SMALL}
    grad_x, reduced, in_sums, small_acc = _local_step(
        x[0], mem[0], positions[0], gains, shards, loss_target[0], place)

    state = lambda n: (w[n][0], mo[n][0], vo[n][0])
    updated, in_chips = _adamw_shards({n: (*reduced[n], *state(n)) for n in reduced}, "adamw_shards", [in_sums[0]])
    updated.update(_adamw_shards({"w_in": (in_sums[1], in_chips[0], *state("w_in"))}, "adamw_w_in")[0])
    grad, delta, new_m, new_v = {}, {}, {}, {}
    for n, (g, d_, m_, v_) in updated.items():
        grad[n], delta[n], new_m[n], new_v[n] = g[None], d_[None], m_[None], v_[None]

    params = {n: (w[n], mo[n], vo[n]) for n, _, _, _ in SMALL}
    params["conv_w"] = (w["conv_w"][0], mo["conv_w"][0], vo["conv_w"][0])
    loss, small = _small_update(_small_all_reduce(*small_acc), me.reshape(1).astype(jnp.int32), params)
    for n, (g, d_, m_, v_) in small.items():
        lead = (lambda a: a[None]) if n == "conv_w" else (lambda a: a)
        grad[n], delta[n], new_m[n], new_v[n] = lead(g), lead(d_), lead(m_), lead(v_)

    return (loss, grad_x[None], *[grad[n] for n in ORDER], *[delta[n] for n in ORDER],
            *[new_m[n] for n in ORDER], *[new_v[n] for n in ORDER])
```

```python
import functools

import numpy as np
import jax
import jax.numpy as jnp
from jax import lax
from jax.experimental import pallas as pl
from jax.experimental.pallas import tpu as pltpu

F32, BF16 = jnp.float32, jnp.bfloat16
MESH = pl.DeviceIdType.MESH
ANY = pl.BlockSpec(memory_space=pl.ANY)

N_DEV = 8
D = 1024
S = 4096
N_MEM = 256
HEAD = 64
AW, CW, XW = 512, 256, 256
PW = 3 * AW + 3 * CW + XW
FF = 4096
FF_BLK = FF // N_DEV
PATTERNS = ((128, 1), (512, 4), (2048, 16))
QB = 128
EPS = 1e-6
NEG = -1e30
SCALE = HEAD ** -0.5
ROPE_THETA = 10000.0
LANES = 128
SUBLANES = 8

ADAM_LR, ADAM_B1, ADAM_B2, ADAM_EPS, ADAM_WD, ADAM_STEP = 0.001, 0.9, 0.999, 1e-08, 0.01, 10

TQ = 512
TQ_MLP = 256
NT = S // TQ


def _cparams(vmem_mb, n_grid=1):
    return pltpu.CompilerParams(dimension_semantics=("arbitrary",) * n_grid, vmem_limit_bytes=vmem_mb << 20)


def _const(shape):
    nd = len(shape)
    return pl.BlockSpec(shape, lambda *_: (0,) * nd, pipeline_mode=pl.Buffered(1))


def _acc(shape):
    nd = len(shape)
    return pl.BlockSpec(shape, lambda *_: (0,) * nd)


def _dot(a, b):
    return jnp.dot(a, b, preferred_element_type=F32)


def _dot_nt(a, b):
    return lax.dot_general(a, b, (((1,), (1,)), ((), ())), preferred_element_type=F32)


def _dot_tn(a, b):
    return lax.dot_general(a, b, (((0,), (0,)), ((), ())), preferred_element_type=F32)


def _rms(x, g):
    r = lax.rsqrt(jnp.mean(x * x, axis=-1, keepdims=True) + EPS)
    n = x * r
    return n * g, n, r


def _rms_bwd(dy, n, r, g):
    dn = dy * g
    dx = r * (dn - n * jnp.mean(dn * n, axis=-1, keepdims=True))
    return dx, jnp.sum(dy * n, axis=0, keepdims=True)


def _rot_half(t):
    lane = lax.broadcasted_iota(jnp.int32, t.shape, 1)
    n = t.shape[1]
    return jnp.where((lane % HEAD) < HEAD // 2, pltpu.roll(t, n - HEAD // 2, 1), pltpu.roll(t, HEAD // 2, 1))


def _rope_table(pos_col, invf, sgn, shards):
    def body(p_ref, f_ref, s_ref, c_out, s_out):
        ang = p_ref[...] * f_ref[...]
        c_out[...] = jnp.cos(ang)
        s_out[...] = jnp.sin(ang) * s_ref[...]

    tile = pl.BlockSpec((TQ, LANES), lambda i: (i, 0))
    return _call_with_gather(
        body, NT, shards, name="rope_table",
        in_specs=[pl.BlockSpec((TQ, 1), lambda i: (i, 0)), _const((1, LANES)), _const((1, LANES))],
        out_specs=[tile, tile], out_shape=[jax.ShapeDtypeStruct((S, LANES), F32)] * 2,
        scratch_shapes=[], vmem_mb=32, args=(pos_col, invf, sgn))


def _all_heads(t):
    return jnp.tile(t, (1, AW // LANES))


def _mem_fwd(mem, g_mem, wkv16):
    def body(m_ref, g_ref, w_ref, n16_ref, kv_ref):
        y, _, _ = _rms(m_ref[...], g_ref[...])
        y16 = y.astype(BF16)
        n16_ref[...] = y16
        kv_ref[...] = _dot(y16, w_ref[...]).astype(BF16)

    return pl.pallas_call(
        body, name="mem_fwd",
        out_shape=[jax.ShapeDtypeStruct((N_MEM, D), BF16), jax.ShapeDtypeStruct((N_MEM, 2 * XW), BF16)],
        compiler_params=pltpu.CompilerParams(vmem_limit_bytes=32 << 20))(mem, g_mem, wkv16)


def _in_proj(x, g, w8, cos, sins, shards):
    blk = PW // N_DEV

    def body(x_ref, g_ref, w8_ref, c_ref, s_ref, q_ref, kv_ref, bcu_ref, qx_ref, h_ref, w_out, w_ref):
        @pl.when(pl.program_id(0) == 0)
        def _():
            for j in range(N_DEV):
                w_ref[:, j * blk:(j + 1) * blk] = w8_ref[j]
            w_out[...] = w_ref[...]

        y, _, _ = _rms(x_ref[...], g_ref[...])
        h = y.astype(BF16)
        h_ref[...] = h
        proj = _dot(h, w_ref[...])
        cos, sn = _all_heads(c_ref[...]), _all_heads(s_ref[...])
        q, k = proj[:, 0:AW], proj[:, AW:2 * AW]
        q_ref[...] = (q * cos + _rot_half(q) * sn) * SCALE
        kv_ref[...] = _pack_pair(k * cos + _rot_half(k) * sn, proj[:, 2 * AW:3 * AW])
        bcu_ref[...] = proj[:, 3 * AW:3 * AW + 3 * CW]
        qx_ref[...] = (proj[:, 3 * AW + 3 * CW:] * SCALE).astype(BF16)

    def tile(w):
        return pl.BlockSpec((TQ, w), lambda i: (i, 0))

    return _call_with_gather(
        body, NT, shards, name="in_proj",
        in_specs=[tile(D), _const((1, D)), _const((N_DEV, D, blk)), tile(LANES), tile(LANES)],
        out_specs=[tile(AW), tile(AW), tile(3 * CW), tile(XW), tile(D), _acc((D, PW))],
        out_shape=[jax.ShapeDtypeStruct((S, AW), F32)] * 2 + [
            jax.ShapeDtypeStruct((S, 3 * CW), F32), jax.ShapeDtypeStruct((S, XW), BF16),
            jax.ShapeDtypeStruct((S, D), BF16), jax.ShapeDtypeStruct((D, PW), BF16)],
        scratch_shapes=[pltpu.VMEM((D, PW), BF16)], vmem_mb=56, args=(x, g, w8, cos, sins))


ATTN_PLANS = (("p1", 1, 128, 32), ("p4", 8, 64, 8), ("p16", 16, 128, 2))
PAD = 128
WIN = 256


ATTN_UNROLL = 8


def _fill_bias(tab, qblk, partner):
    qi = lax.broadcasted_iota(jnp.int32, (2 * qblk, WIN), 0) & (qblk - 1)
    kj = lax.broadcasted_iota(jnp.int32, (2 * qblk, WIN), 1)
    piece = kj >> (qblk.bit_length() - 1)
    kk = kj & (qblk - 1)
    prev = (piece & 1) == 0
    of_partner = piece >= 2
    for first in (0, 1):
        for par in (0, 1):
            lo = jnp.where(prev, (qblk if first else qi) + jnp.where(of_partner, par, 0), 0)
            hi = jnp.where(prev, qblk, qi + jnp.where(of_partner, par - 1, 0))
            tab[2 * first + par] = jnp.where((kk >= lo) & (kk <= hi), 0.0, NEG).astype(F32)


def _block_rows(g, qblk, nbc, partner):
    own = pl.ds(pl.multiple_of(PAD + g * qblk, qblk), qblk)
    first = ((g & (nbc - 1)) == 0).astype(jnp.int32)
    if partner:
        gp = jnp.bitwise_xor(g, 4 * nbc)
        wins = (pl.ds(pl.multiple_of(PAD + (g - 1) * qblk, qblk), 2 * qblk),
                pl.ds(pl.multiple_of(PAD + (gp - 1) * qblk, qblk), 2 * qblk))
        return own, wins, 2 * first + ((g >> ((4 * nbc).bit_length() - 1)) & 1)
    return own, (pl.ds(pl.multiple_of(PAD + (g - 1) * qblk, qblk), 2 * qblk),), 2 * first


def _pack_pair(lo, hi):
    lo_bits = lax.bitcast_convert_type(lo.astype(BF16).astype(F32), jnp.uint32) >> 16
    hi_bits = lax.bitcast_convert_type(hi.astype(BF16).astype(F32), jnp.uint32) & jnp.uint32(0xFFFF0000)
    return lax.bitcast_convert_type(hi_bits | lo_bits, F32)


def _unpack_pair(c):
    bits = lax.bitcast_convert_type(c, jnp.uint32)
    lo = lax.bitcast_convert_type(bits << 16, F32).astype(BF16)
    hi = lax.bitcast_convert_type(bits & jnp.uint32(0xFFFF0000), F32).astype(BF16)
    return lo, hi


def _window(ref, wins):
    parts = [ref[w, :] for w in wins]
    return parts[0] if len(parts) == 1 else jnp.concatenate(parts, axis=0)


def _stack_heads(t, lane):
    zero = jnp.zeros_like(t)
    return jnp.concatenate([jnp.where(lane < HEAD, t, zero), jnp.where(lane >= HEAD, t, zero)], axis=0)


def _unstack_heads(t2, lane):
    half = t2.shape[0] // 2
    return jnp.where(lane < HEAD, t2[0:half, :], t2[half:, :])


def _lanes_of(step):
    return pl.ds(pl.multiple_of(step * LANES, LANES), LANES)


def _whole_wait(buf, sem):
    whole = buf.at[pl.ds(PAD, S), :]
    return pltpu.make_async_copy(whole, whole, sem)


def _whole_waits(bufs, sems):
    return [_whole_wait(buf, sems.at[i]) for i, buf in enumerate(bufs)]


def _class_gather(views, bufs, sems, lanes):
    copies = []
    for i, (view, buf) in enumerate(zip(views, bufs)):
        if view.ndim == 2:
            copies.append(pltpu.make_async_copy(view.at[:, lanes], buf.at[pl.ds(PAD, S), :], sems.at[i]))
        else:
            per, n_cls = view.shape[0], view.shape[1]
            copies += [pltpu.make_async_copy(view.at[:, c, lanes], buf.at[pl.ds(PAD + c * per, per), :], sems.at[i])
                       for c in range(n_cls)]
    return copies


def _class_scatter(bufs, dsts, sems, lanes=None):
    copies = []
    for i, (buf, dst) in enumerate(zip(bufs, dsts)):
        if dst.ndim == 2:
            copies.append(pltpu.make_async_copy(buf.at[pl.ds(PAD, S), :], dst.at[:, lanes], sems.at[i]))
            continue
        per, n_cls = dst.shape[0], dst.shape[1]
        for c in range(n_cls):
            to = dst.at[:, c, :] if lanes is None else dst.at[:, c, lanes]
            copies.append(pltpu.make_async_copy(buf.at[pl.ds(PAD + c * per, per), :], to, sems.at[i]))
    return copies


def _start(copies):
    for cp in copies:
        cp.start()


def _wait(waits):
    for w in waits:
        w.wait()


def _attn_fwd(q, kvp, shards=()):
    views = [[a] + [a.reshape(S // n, n, AW) for _, n, _, _ in ATTN_PLANS[1:]] for a in (q, kvp)]
    flat = [views[a][p] for p in range(3) for a in range(2)]
    ng = len(shards)
    n_grid = AW // LANES

    def body(*refs):
        hbm = [refs[2 * p:2 * p + 2] for p in range(3)]
        refs = refs[6:]
        shard_refs, refs = refs[:ng], refs[ng:]
        y_ref, lt_ref = refs[0:2]
        whole_refs, refs = refs[2:2 + ng], refs[2 + ng:]
        bufs = [refs[2 * p:2 * p + 2] for p in range(3)]
        oc4, lc4, oc16, lc16, o4n, l4n, o16n, l16n, tab128, tab4, sem_in, sem_out = refs[6:18]
        step = pl.program_id(0)
        if ng:
            start_gather, finish_gather = _gather_steps(shard_refs, whole_refs, *refs[18:])
            pl.when(step == 0)(start_gather)
        now = [_class_gather(hbm[p], bufs[p], sem_in.at[p], _lanes_of(step)) for p in range(3)]
        nxt = [_class_gather(hbm[p], bufs[p], sem_in.at[p], _lanes_of(step + 1)) for p in range(3)]

        @pl.when(step == 0)
        def _():
            for p in range(3):
                _start(now[p])
                for b in bufs[p]:
                    b[0:PAD, :] = jnp.zeros((PAD, LANES), F32)
            _fill_bias(tab128, 128, False)
            _fill_bias(tab4, 64, True)

        def prefetch(p):
            pl.when(step + 1 < n_grid)(lambda: _start(nxt[p]))

        lane = lax.broadcasted_iota(jnp.int32, (1, LANES), 1)
        ones = jnp.ones((WIN, LANES), BF16)

        def run(plan, bq, bkv, tab, o_dst, l_dst, dst_pad):
            _, n_cls, qblk, nbc = plan
            partner = n_cls == 8

            def block(g, carry):
                own, wins, mask = _block_rows(g, qblk, nbc, partner)
                q2 = _stack_heads(bq[own, :].astype(BF16), lane)
                kw, vwin = _unpack_pair(_window(bkv, wins))
                vw = jnp.concatenate([vwin, ones], axis=1)
                s = _dot_nt(q2, kw) + tab[mask]
                m = jnp.max(s, axis=1, keepdims=True)
                oe = _dot(jnp.exp(s - m).astype(BF16), vw)
                den = oe[:, LANES:]
                dst = pl.ds(pl.multiple_of(dst_pad + g * qblk, qblk), qblk)
                o_dst[dst, :] = _unstack_heads(oe[:, 0:LANES] / den, lane)
                l_dst[dst, :] = _unstack_heads(m + jnp.log(den), lane)
                return carry
            lax.fori_loop(0, n_cls * nbc, block, 0, unroll=ATTN_UNROLL)

        _wait(_whole_waits(bufs[0], sem_in.at[0]))
        run(ATTN_PLANS[0], *bufs[0], tab128, y_ref, lt_ref, 0)
        prefetch(0)
        _wait(_whole_waits(bufs[1], sem_in.at[1]))
        run(ATTN_PLANS[1], *bufs[1], tab4, oc4, lc4, PAD)
        prefetch(1)
        _start(_class_scatter((oc4, lc4), (o4n, l4n), sem_out.at[0]))
        _wait(_whole_waits(bufs[2], sem_in.at[2]))
        run(ATTN_PLANS[2], *bufs[2], tab128, oc16, lc16, PAD)
        prefetch(2)
        _start(_class_scatter((oc16, lc16), (o16n, l16n), sem_out.at[1]))
        _wait(_whole_waits((oc4, lc4), sem_out.at[0]) + _whole_waits((oc16, lc16), sem_out.at[1]))

        for t in range(S // TQ):
            rows = pl.ds(t * TQ, TQ)
            r4, r16 = pl.ds(t * (TQ // 8), TQ // 8), pl.ds(t * (TQ // 16), TQ // 16)
            l0, l1, l2 = lt_ref[rows, :], l4n[r4, :, :].reshape(TQ, LANES), l16n[r16, :, :].reshape(TQ, LANES)
            lm = jnp.maximum(jnp.maximum(l0, l1), l2)
            e0, e1, e2 = jnp.exp(l0 - lm), jnp.exp(l1 - lm), jnp.exp(l2 - lm)
            den = e0 + e1 + e2
            y_ref[rows, :] = (e0 * y_ref[rows, :] + e1 * o4n[r4, :, :].reshape(TQ, LANES)
                              + e2 * o16n[r16, :, :].reshape(TQ, LANES)) / den
            lt_ref[rows, :] = lm + jnp.log(den)

        if ng:
            pl.when(step == n_grid - 1)(finish_gather)

    col = pl.BlockSpec((S, LANES), lambda h: (0, h))
    padded = pltpu.VMEM((PAD + S, LANES), F32)
    return pl.pallas_call(
        body, grid=(n_grid,), name="attn_fwd",
        in_specs=[ANY] * (6 + ng), out_specs=[col, col] + [ANY] * ng,
        out_shape=[jax.ShapeDtypeStruct((S, AW), F32)] * 2 + _gathered_shapes(shards),
        scratch_shapes=[padded] * 10 + [
            pltpu.VMEM((S // 8, 8, LANES), F32), pltpu.VMEM((S // 8, 8, LANES), F32),
            pltpu.VMEM((S // 16, 16, LANES), F32), pltpu.VMEM((S // 16, 16, LANES), F32),
            pltpu.VMEM((4, 256, WIN), F32), pltpu.VMEM((4, 128, WIN), F32),
            pltpu.SemaphoreType.DMA((3, 2)), pltpu.SemaphoreType.DMA((2, 2))]
        + (_gather_scratch(ng) if ng else []),
        compiler_params=_cparams(56))(*flat, *shards)


def _conv_taps(z, zprev, row):
    z1 = jnp.where(row == 0, zprev[7:8, :], pltpu.roll(z, 1, 0))
    z2 = jnp.where(row == 0, zprev[6:7, :], jnp.where(row == 1, zprev[7:8, :], pltpu.roll(z, 2, 0)))
    return z1, z2


def _xattn_scores(qm, km):
    s = _dot_nt(qm, km)
    m = jnp.max(s, axis=1, keepdims=True)
    e = jnp.exp(s - m)
    return e, jnp.sum(e, axis=1, keepdims=True)


def _mix_out(y_attn, bcu, qx16, kv16, cw8, g_attn, g_conv, g_x, g_post, wout16, x, shards):
    def body(ya_ref, bcu_ref, halo_ref, qx_ref, kv_ref, cw_ref, ga_ref, gc_ref, gx_ref, gp_ref, w_ref, x_ref,
             ypre_ref, y16_ref, y2_ref, x1_ref):
        i = pl.program_id(0)
        bcu = bcu_ref[...]
        b, c, u = bcu[:, 0:CW], bcu[:, CW:2 * CW], bcu[:, 2 * CW:]
        z = c * u
        halo = halo_ref[...]
        zprev = jnp.where(i > 0, halo[:, CW:2 * CW] * halo[:, 2 * CW:], 0.0)
        row = lax.broadcasted_iota(jnp.int32, z.shape, 0)
        z1, z2 = _conv_taps(z, zprev, row)
        cw = cw_ref[...]
        y_conv = b * (z2 * cw[0:1, :] + z1 * cw[1:2, :] + z * cw[2:3, :])

        qx = qx_ref[...]
        kv = kv_ref[...]
        km, vm = kv[:, 0:XW], kv[:, XW:]
        lane = lax.broadcasted_iota(jnp.int32, qx.shape, 1)
        y_x = jnp.zeros(qx.shape, F32)
        for h in range(XW // HEAD):
            hm = (lane >= h * HEAD) & (lane < (h + 1) * HEAD)
            e, l = _xattn_scores(jnp.where(hm, qx, jnp.zeros_like(qx)), km)
            y_x = jnp.where(hm, _dot(e.astype(BF16), vm) / l, y_x)

        y_attn = ya_ref[...]
        ypre_ref[:, 0:AW] = y_attn
        ypre_ref[:, AW:AW + CW] = y_conv
        ypre_ref[:, AW + CW:] = y_x
        y = jnp.concatenate([_rms(y_attn, ga_ref[...])[0], _rms(y_conv, gc_ref[...])[0],
                             _rms(y_x, gx_ref[...])[0]], axis=1).astype(BF16)
        y16_ref[...] = y
        y2 = _dot(y, w_ref[...])
        y2_ref[...] = y2
        x1_ref[...] = x_ref[...] + _rms(y2, gp_ref[...])[0]

    def tile(w):
        return pl.BlockSpec((TQ, w), lambda i: (i, 0))

    halo = pl.BlockSpec((SUBLANES, 3 * CW), lambda i: (jnp.maximum(i * (TQ // SUBLANES) - 1, 0), 0))
    return _call_with_gather(
        body, NT, shards, name="mix_out",
        in_specs=[tile(AW), tile(3 * CW), halo, tile(XW), _const((N_MEM, 2 * XW)), _const((SUBLANES, CW)),
                  _const((1, AW)), _const((1, CW)), _const((1, XW)), _const((1, D)), _const((D, D)), tile(D)],
        out_specs=[tile(D), tile(D), tile(D), tile(D)],
        out_shape=[jax.ShapeDtypeStruct((S, D), F32), jax.ShapeDtypeStruct((S, D), BF16),
                   jax.ShapeDtypeStruct((S, D), F32), jax.ShapeDtypeStruct((S, D), F32)],
        scratch_shapes=[], vmem_mb=56,
        args=(y_attn, bcu, bcu, qx16, kv16, cw8, g_attn, g_conv, g_x, g_post, wout16, x))


def _mlp(x1, tgt, g_pre, g_post, wup8, wdn16):
    tq = TQ_MLP

    def body(x1_ref, t_ref, g1_ref, g2_ref, wu_ref, wd_ref,
             a16_ref, du_ref, h2_ref, df2_ref, dx1_ref, loss_ref, dg_ref, a32):
        @pl.when(pl.program_id(0) == 0)
        def _():
            loss_ref[...] = jnp.zeros_like(loss_ref)
            dg_ref[...] = jnp.zeros_like(dg_ref)

        x1 = x1_ref[...]
        g1, g2 = g1_ref[...], g2_ref[...]
        y1, n1, r1 = _rms(x1, g1)
        h2 = y1.astype(BF16)
        h2_ref[...] = h2
        f2 = jnp.zeros((tq, D), F32)
        for j in range(N_DEV):
            cols = slice(j * FF_BLK, (j + 1) * FF_BLK)
            a = jnp.maximum(_dot(h2, wu_ref[j]), 0.0)
            a32[:, cols] = a
            a16_ref[:, cols] = a.astype(BF16)
            f2 = f2 + _dot((a * a).astype(BF16), wd_ref[cols, :])
        y2, n2, r2 = _rms(f2, g2)
        e = x1 + y2 - t_ref[...]
        sq = jnp.sum(jnp.sum(e * e, axis=1, keepdims=True), axis=0, keepdims=True)
        loss_ref[...] += jnp.broadcast_to(sq * (0.5 / D), loss_ref.shape)
        dout = e * (1.0 / D)
        df2, dg2 = _rms_bwd(dout, n2, r2, g2)
        df2_16 = df2.astype(BF16)
        df2_ref[...] = df2_16
        dh2 = jnp.zeros((tq, D), F32)
        for j in range(N_DEV):
            cols = slice(j * FF_BLK, (j + 1) * FF_BLK)
            du = (_dot_nt(df2_16, wd_ref[cols, :]) * (2.0 * a32[:, cols])).astype(BF16)
            du_ref[:, cols] = du
            dh2 = dh2 + _dot_nt(du, wu_ref[j])
        dx, dg1 = _rms_bwd(dh2, n1, r1, g1)
        dx1_ref[...] = dout + dx
        dg_ref[0:1, :] += dg2
        dg_ref[1:2, :] += dg1

    def tile(w):
        return pl.BlockSpec((tq, w), lambda i: (i, 0))

    return pl.pallas_call(
        body, grid=(S // tq,), name="mlp",
        in_specs=[tile(D), tile(D), _const((1, D)), _const((1, D)), _const((N_DEV, D, FF_BLK)), _const((FF, D))],
        out_specs=[tile(FF), tile(FF), tile(D), tile(D), tile(D), _acc((SUBLANES, LANES)), _acc((SUBLANES, D))],
        out_shape=[jax.ShapeDtypeStruct((S, FF), BF16), jax.ShapeDtypeStruct((S, FF), BF16),
                   jax.ShapeDtypeStruct((S, D), BF16), jax.ShapeDtypeStruct((S, D), BF16),
                   jax.ShapeDtypeStruct((S, D), F32), jax.ShapeDtypeStruct((SUBLANES, LANES), F32),
                   jax.ShapeDtypeStruct((SUBLANES, D), F32)],
        scratch_shapes=[pltpu.VMEM((tq, FF), F32)],
        compiler_params=_cparams(56))(x1, tgt, g_pre, g_post, wup8, wdn16)


def _mix_out_bwd(dx1, y2, ypre, ltot, head_ones, q, bcu, qx16, kv16, cw8, g_post, g_attn, g_conv, g_x, wout16):
    def body(dx1_ref, y2_ref, ypre_ref, lt_ref, e_ref, q_ref, bcu_ref, halo_ref, qx_ref, kv_ref, cw_ref, gp_ref,
             ga_ref, gc_ref, gx_ref, w_ref, dy2_ref, qdo_ref, ld_ref, dbcu_ref, dqx_ref, dgs_ref, dcw_ref, dkv_ref,
             carry):
        i = pl.program_id(0)

        @pl.when(i == 0)
        def _():
            dgs_ref[...] = jnp.zeros_like(dgs_ref)
            dcw_ref[...] = jnp.zeros_like(dcw_ref)
            dkv_ref[...] = jnp.zeros_like(dkv_ref)
            carry[...] = jnp.zeros_like(carry)

        gp = gp_ref[...]
        _, n, r = _rms(y2_ref[...], gp)
        dy2, dgp = _rms_bwd(dx1_ref[...], n, r, gp)
        dy2_16 = dy2.astype(BF16)
        dy2_ref[...] = dy2_16
        dy = _dot_nt(dy2_16, w_ref[...])

        ypre = ypre_ref[...]
        ga, gc, gx = ga_ref[...], gc_ref[...], gx_ref[...]
        _, na, ra = _rms(ypre[:, 0:AW], ga)
        dya, dga = _rms_bwd(dy[:, 0:AW], na, ra, ga)
        _, nc, rc = _rms(ypre[:, AW:AW + CW], gc)
        dyc, dgc = _rms_bwd(dy[:, AW:AW + CW], nc, rc, gc)
        y_x = ypre[:, AW + CW:]
        _, nx, rx = _rms(y_x, gx)
        dyx, dgx = _rms_bwd(dy[:, AW + CW:], nx, rx, gx)
        qdo_ref[...] = _pack_pair(q_ref[...], dya)
        prod = dya * ypre[:, 0:AW]
        hi = prod.astype(BF16)
        lo = (prod - hi.astype(F32)).astype(BF16)
        head_sum = _dot(hi, e_ref[...]) + _dot(lo, e_ref[...])
        lane_a = lax.broadcasted_iota(jnp.int32, prod.shape, 1)
        ld_ref[...] = jnp.where((lane_a % HEAD) < HEAD // 2, lt_ref[...], head_sum)
        dgs_ref[0:1, :] += dgp
        dgs_ref[1:2, :] += jnp.concatenate([dga, dgc, dgx], axis=1)

        bcu = bcu_ref[...]
        b, c, u = bcu[:, 0:CW], bcu[:, CW:2 * CW], bcu[:, 2 * CW:]
        z = c * u
        halo = halo_ref[...]
        zprev = jnp.where(i < NT - 1, halo[:, CW:2 * CW] * halo[:, 2 * CW:], 0.0)
        row = lax.broadcasted_iota(jnp.int32, z.shape, 0)
        z1, z2 = _conv_taps(z, zprev, row)
        cw = cw_ref[...]
        conv = z2 * cw[0:1, :] + z1 * cw[1:2, :] + z * cw[2:3, :]
        dconv = dyc * b
        nxt = carry[...]
        dn1 = jnp.where(row == TQ - 1, nxt[0:1, :], pltpu.roll(dconv, TQ - 1, 0))
        dn2 = jnp.where(row == TQ - 1, nxt[1:2, :], jnp.where(row == TQ - 2, nxt[0:1, :], pltpu.roll(dconv, TQ - 2, 0)))
        carry[...] = dconv[0:SUBLANES, :]
        dz = dconv * cw[2:3, :] + dn1 * cw[1:2, :] + dn2 * cw[0:1, :]
        dbcu_ref[:, 0:CW] = (dyc * conv).astype(BF16)
        dbcu_ref[:, CW:2 * CW] = (dz * u).astype(BF16)
        dbcu_ref[:, 2 * CW:] = (dz * c).astype(BF16)
        dcw_ref[0:1, :] += jnp.sum(z2 * dconv, axis=0, keepdims=True)
        dcw_ref[1:2, :] += jnp.sum(z1 * dconv, axis=0, keepdims=True)
        dcw_ref[2:3, :] += jnp.sum(z * dconv, axis=0, keepdims=True)

        qx = qx_ref[...]
        kv = kv_ref[...]
        km, vm = kv[:, 0:XW], kv[:, XW:]
        lane = lax.broadcasted_iota(jnp.int32, qx.shape, 1)
        dqx = jnp.zeros(qx.shape, F32)
        dkm = jnp.zeros((N_MEM, XW), F32)
        dvm = jnp.zeros((N_MEM, XW), F32)
        for h in range(XW // HEAD):
            hm = (lane >= h * HEAD) & (lane < (h + 1) * HEAD)
            qm = jnp.where(hm, qx, jnp.zeros_like(qx))
            e, l = _xattn_scores(qm, km)
            p = e / l
            dom = jnp.where(hm, dyx, 0.0)
            do16 = dom.astype(BF16)
            dsum = jnp.sum(dom * y_x, axis=1, keepdims=True)
            ds = (p * (_dot_nt(do16, vm) - dsum)).astype(BF16)
            dqx = jnp.where(hm, _dot(ds, km), dqx)
            dkm = dkm + _dot_tn(ds, qm)
            dvm = dvm + _dot_tn(p.astype(BF16), do16)
        dqx_ref[...] = (dqx * SCALE).astype(BF16)
        dkv_ref[:, 0:XW] += dkm
        dkv_ref[:, XW:] += dvm

    def tile(w):
        return pl.BlockSpec((TQ, w), lambda i: (NT - 1 - i, 0))

    halo = pl.BlockSpec((SUBLANES, 3 * CW), lambda i: (jnp.maximum((NT - 1 - i) * (TQ // SUBLANES) - 1, 0), 0))
    return pl.pallas_call(
        body, grid=(NT,), name="mix_out_bwd",
        in_specs=[tile(D), tile(D), tile(D), tile(AW), _const((AW, AW)), tile(AW), tile(3 * CW), halo, tile(XW),
                  _const((N_MEM, 2 * XW)), _const((SUBLANES, CW)), _const((1, D)), _const((1, AW)), _const((1, CW)),
                  _const((1, XW)), _const((D, D))],
        out_specs=[tile(D), tile(AW), tile(AW), tile(3 * CW), tile(XW), _acc((SUBLANES, D)), _acc((SUBLANES, CW)),
                   _acc((N_MEM, 2 * XW))],
        out_shape=[jax.ShapeDtypeStruct((S, D), BF16), jax.ShapeDtypeStruct((S, AW), F32),
                   jax.ShapeDtypeStruct((S, AW), F32),
                   jax.ShapeDtypeStruct((S, 3 * CW), BF16), jax.ShapeDtypeStruct((S, XW), BF16),
                   jax.ShapeDtypeStruct((SUBLANES, D), F32), jax.ShapeDtypeStruct((SUBLANES, CW), F32),
                   jax.ShapeDtypeStruct((N_MEM, 2 * XW), F32)],
        scratch_shapes=[pltpu.VMEM((SUBLANES, CW), F32)],
        compiler_params=_cparams(56))(dx1, y2, ypre, ltot, head_ones, q, bcu, bcu, qx16, kv16, cw8, g_post, g_attn,
                                      g_conv, g_x, wout16)


def _attn_bwd(qdo, kvp, ld, chip_sums=()):
    n_in = 3
    views = [[a] + [a.reshape(S // n, n, AW) for _, n, _, _ in ATTN_PLANS[1:]] for a in (qdo, kvp, ld)]
    flat = [views[a][p] for p in range(3) for a in range(n_in)]
    ns = len(chip_sums)
    n_grid = AW // LANES

    def body(*refs):
        hbm = [refs[n_in * p:n_in * p + n_in] for p in range(3)]
        refs = refs[3 * n_in:]
        sum_refs, refs = refs[:ns], refs[ns:]
        outs = [refs[3 * p:3 * p + 3] for p in range(3)]
        landed_refs, sc = refs[9:9 + ns], refs[9 + ns:]
        bufs = [sc[3 * p:3 * p + 3] for p in range(3)]
        res = [sc[9 + 3 * p:12 + 3 * p] for p in range(3)]
        tab128, tab4, sem_in, sem_out = sc[18:22]
        step = pl.program_id(0)
        if ns:
            start_chips, finish_chips = _chips_steps(sum_refs, landed_refs, *sc[22:])
            pl.when(step == 0)(start_chips)
        now = [_class_gather(hbm[p], bufs[p], sem_in.at[p], _lanes_of(step)) for p in range(3)]
        nxt = [_class_gather(hbm[p], bufs[p], sem_in.at[p], _lanes_of(step + 1)) for p in range(3)]

        @pl.when(step == 0)
        def _():
            for p in range(3):
                _start(now[p])
                for b in bufs[p]:
                    b[0:PAD, :] = jnp.zeros((PAD, LANES), F32)
            _fill_bias(tab128, 128, False)
            _fill_bias(tab4, 64, True)

        def prefetch(p):
            pl.when(step + 1 < n_grid)(lambda: _start(nxt[p]))

        for p in range(3):
            for b in res[p]:
                b[...] = jnp.zeros_like(b)
        lane = lax.broadcasted_iota(jnp.int32, (1, LANES), 1)

        def run(plan, plan_bufs, tab, dst):
            _, n_cls, qblk, nbc = plan
            partner = n_cls == 8
            bqdo, bkv, bld = plan_bufs
            rq, rk, rv = dst

            def block(g, carry):
                own, wins, mask = _block_rows(g, qblk, nbc, partner)
                qb, dob = _unpack_pair(bqdo[own, :])
                q2, do2 = _stack_heads(qb, lane), _stack_heads(dob, lane)
                kw, vw = _unpack_pair(_window(bkv, wins))
                ldv = bld[own, :]
                half = HEAD // 2
                lt2 = jnp.concatenate([ldv[:, 0:1], ldv[:, HEAD:HEAD + 1]], axis=0)
                dsum2 = jnp.concatenate([ldv[:, half:half + 1], ldv[:, HEAD + half:HEAD + half + 1]], axis=0)
                p = jnp.exp(_dot_nt(q2, kw) + tab[mask] - lt2)
                ds = (p * (_dot_nt(do2, vw) - dsum2)).astype(BF16)
                rq[own, :] = _unstack_heads(_dot(ds, kw), lane)
                dkw = _dot_tn(ds, q2)
                dvw = _dot_tn(p.astype(BF16), do2)
                n_w = WIN // len(wins)
                for i, w in enumerate(wins):
                    rk[w, :] += dkw[i * n_w:(i + 1) * n_w, :]
                    rv[w, :] += dvw[i * n_w:(i + 1) * n_w, :]
                return carry
            lax.fori_loop(0, n_cls * nbc, block, 0, unroll=ATTN_UNROLL)

        tabs = (tab128, tab4, tab128)
        for p in range(3):
            _wait(_whole_waits(bufs[p], sem_in.at[p]))
            run(ATTN_PLANS[p], bufs[p], tabs[p], res[p])
            prefetch(p)
            _start(_class_scatter(res[p], outs[p], sem_out.at[p], _lanes_of(step)))
        for p in range(3):
            _wait(_whole_waits(res[p], sem_out.at[p]))
        if ns:
            pl.when(step == n_grid - 1)(finish_chips)

    padded = pltpu.VMEM((PAD + S, LANES), F32)
    shapes = [jax.ShapeDtypeStruct(views[0][p].shape, F32) for p in range(3) for _ in range(3)]
    out = pl.pallas_call(
        body, grid=(n_grid,), name="attn_bwd",
        in_specs=[ANY] * (3 * n_in + ns), out_specs=[ANY] * (9 + ns),
        out_shape=shapes + _chips_shapes(chip_sums),
        scratch_shapes=[padded] * 18
        + [pltpu.VMEM((4, 256, WIN), F32), pltpu.VMEM((4, 128, WIN), F32),
           pltpu.SemaphoreType.DMA((3, n_in)), pltpu.SemaphoreType.DMA((3, 3))]
        + (_chips_scratch(ns) if ns else []),
        compiler_params=_cparams(56))(*flat, *chip_sums)
    return [o.reshape(S, AW) for o in out[:9]] + list(out[9:])


def _in_proj_bwd(dqkv, dbcu, dqx, cos, sins, w16, x, g, dx1):
    tq = TQ // 2

    def body(*refs):
        parts = refs[0:9]
        dbcu_ref, dqx_ref, c_ref, s_ref, w_ref, x_ref, g_ref, dx1_ref, dp_ref, gx_ref, dg_ref = refs[9:]

        @pl.when(pl.program_id(0) == 0)
        def _():
            dg_ref[...] = jnp.zeros_like(dg_ref)

        dq, dk, dv = (parts[i][...] + parts[3 + i][...] + parts[6 + i][...] for i in range(3))
        cos, sn = _all_heads(c_ref[...]), _all_heads(s_ref[...])
        dqr = dq * SCALE
        dkr = dk
        dp = jnp.concatenate([(dqr * cos + _rot_half(dqr * sn)).astype(BF16),
                              (dkr * cos + _rot_half(dkr * sn)).astype(BF16), dv.astype(BF16),
                              dbcu_ref[...], dqx_ref[...]], axis=1)
        dp_ref[...] = dp
        dh = _dot_nt(dp, w_ref[...])
        g = g_ref[...]
        _, n, r = _rms(x_ref[...], g)
        dx, dg = _rms_bwd(dh, n, r, g)
        gx_ref[...] = dx1_ref[...] + dx
        dg_ref[0:1, :] += dg

    def tile(w):
        return pl.BlockSpec((tq, w), lambda i: (i, 0))

    return pl.pallas_call(
        body, grid=(S // tq,), name="in_proj_bwd",
        in_specs=[tile(AW)] * 9 + [tile(3 * CW), tile(XW), tile(LANES), tile(LANES), _const((D, PW)),
                                   tile(D), _const((1, D)), tile(D)],
        out_specs=[tile(PW), tile(D), _acc((SUBLANES, D))],
        out_shape=[jax.ShapeDtypeStruct((S, PW), BF16), jax.ShapeDtypeStruct((S, D), F32),
                   jax.ShapeDtypeStruct((SUBLANES, D), F32)],
        compiler_params=_cparams(56))(*dqkv, dbcu, dqx, cos, sins, w16, x, g, dx1)


def _mem_bwd(mem, g_mem, wkv16, dkv):
    def body(m_ref, g_ref, w_ref, dkv_ref, dkv16_ref, dg_ref):
        dkv16 = dkv_ref[...].astype(BF16)
        dkv16_ref[...] = dkv16
        _, n, _ = _rms(m_ref[...], g_ref[...])
        dg = jnp.sum(_dot_nt(dkv16, w_ref[...]) * n, axis=0, keepdims=True)
        dg_ref[...] = jnp.broadcast_to(dg, dg_ref.shape)

    return pl.pallas_call(
        body, name="mem_bwd",
        out_shape=[jax.ShapeDtypeStruct((N_MEM, 2 * XW), BF16), jax.ShapeDtypeStruct((SUBLANES, D), F32)],
        compiler_params=pltpu.CompilerParams(vmem_limit_bytes=32 << 20))(mem, g_mem, wkv16, dkv)


N_CHIPS = N_DEV // 2


def _transpose_into(at, a_ref):
    kk = a_ref.shape[0]
    chunk = min(kk, 512)
    for c in range(kk // chunk):
        at[:, c * chunk:(c + 1) * chunk] = a_ref[c * chunk:(c + 1) * chunk, :].T


def _pair_scratch(block):
    return [pltpu.VMEM((N_CHIPS,) + block, BF16), pltpu.VMEM((N_CHIPS,) + block, BF16),
            pltpu.SemaphoreType.DMA((N_CHIPS,)), pltpu.SemaphoreType.DMA((N_CHIPS,))]


def _swap_with_sibling(p, stage, land, send, recv):
    x, y, c = lax.axis_index("x"), lax.axis_index("y"), lax.axis_index("c")
    return pltpu.make_async_remote_copy(src_ref=stage.at[p], dst_ref=land.at[p], send_sem=send.at[p],
                                        recv_sem=recv.at[p], device_id=(x, y, 1 - c), device_id_type=MESH)


def _wgrad_cols(place, a16, b16, blk, name, square_b=False, transpose_out=False):
    kk, m = a16.shape
    aligned = blk % LANES == 0
    wide = blk if aligned else -(-(blk + LANES // 2) // LANES) * LANES
    block = (blk, m) if transpose_out else (m, blk)

    def body(pl_ref, a_ref, *refs):
        if aligned:
            b_sib, b_mine, cs_ref, own_ref, at, stage, land, send, recv = refs
        else:
            b_hbm, cs_ref, own_ref, at, stage, land, send, recv, win, wsem = refs
        p = pl.program_id(0)
        c = pl_ref[0]

        def fetch(step, mine):
            j = 2 * step + jnp.where(mine, c, 1 - c)
            first = pl.multiple_of(((j * blk) >> 7) << 7, LANES)
            slot = 2 * (step & 1) + mine
            return pltpu.make_async_copy(b_hbm.at[:, pl.ds(first, wide)], win.at[slot], wsem.at[slot])

        @pl.when(p == 0)
        def _():
            if not aligned:
                fetch(0, 0).start()
                fetch(0, 1).start()
            _transpose_into(at, a_ref)

        if not aligned:
            @pl.when(p + 1 < N_CHIPS)
            def _():
                fetch(p + 1, 0).start()
                fetch(p + 1, 1).start()

        def partial(mine):
            if aligned:
                b = (b_mine if mine else b_sib)[...]
                if square_b:
                    b = b * b
                acc = _dot(at[...], b)
            else:
                fetch(p, mine).wait()
                acc = _dot(at[...], win[2 * (p & 1) + mine])
                odd = c if mine else 1 - c
                acc = pltpu.roll(acc, jnp.where(odd == 0, 0, wide - LANES // 2), 1)[:, 0:blk]
            return acc.T if transpose_out else acc

        stage[p] = partial(0).astype(BF16)
        swap = _swap_with_sibling(p, stage, land, send, recv)
        swap.start()
        mine = partial(1)
        swap.wait()
        total = mine + land[p].astype(F32)
        cs_ref[0] = total.astype(BF16)

        @pl.when(p == pl_ref[1])
        def _():
            own_ref[...] = total

    if aligned:
        b_specs = [pl.BlockSpec((kk, blk), lambda p, s: (0, 2 * p + 1 - s[0])),
                   pl.BlockSpec((kk, blk), lambda p, s: (0, 2 * p + s[0]))]
        b_args, extra = (b16, b16), []
    else:
        b_specs, b_args = [ANY], (b16,)
        extra = [pltpu.VMEM((4, kk, wide), BF16), pltpu.SemaphoreType.DMA((4,))]
    return pl.pallas_call(
        body, name=name,
        grid_spec=pltpu.PrefetchScalarGridSpec(
            num_scalar_prefetch=1, grid=(N_CHIPS,),
            in_specs=[pl.BlockSpec((kk, m), lambda p, s: (0, 0), pipeline_mode=pl.Buffered(1))] + b_specs,
            out_specs=[pl.BlockSpec((1,) + block, lambda p, s: (p, 0, 0)), pl.BlockSpec(block, lambda p, s: (0, 0))],
            scratch_shapes=[pltpu.VMEM((m, kk), BF16)] + _pair_scratch(block) + extra),
        out_shape=[jax.ShapeDtypeStruct((N_CHIPS,) + block, BF16), jax.ShapeDtypeStruct(block, F32)],
        compiler_params=_cparams(56))(place, a16, *b_args)


def _wgrad_rows(place, a16, b16, name):
    kk, m = a16.shape
    n = b16.shape[1]
    block = (m // N_DEV, n)

    def body(pl_ref, a_ref, b_ref, cs_ref, own_ref, at, acc, stage, land, send, recv):
        c = pl_ref[0]
        _transpose_into(at, a_ref)
        acc[...] = _dot(at[...], b_ref[...])

        def rows(owner):
            return pl.ds(pl.multiple_of(owner * block[0], block[0]), block[0])

        swaps = []
        for p in range(N_CHIPS):
            stage[p] = acc[rows(2 * p + 1 - c), :].astype(BF16)
            swaps.append(_swap_with_sibling(p, stage, land, send, recv))
            swaps[-1].start()
        for p in range(N_CHIPS):
            swaps[p].wait()
            total = acc[rows(2 * p + c), :] + land[p].astype(F32)
            cs_ref[p] = total.astype(BF16)

            @pl.when(p == pl_ref[1])
            def _():
                own_ref[...] = total

    vmem = pl.BlockSpec(memory_space=pltpu.VMEM)
    return pl.pallas_call(
        body, name=name,
        in_specs=[pl.BlockSpec(memory_space=pltpu.SMEM), vmem, vmem], out_specs=[vmem, vmem],
        out_shape=[jax.ShapeDtypeStruct((N_CHIPS,) + block, BF16), jax.ShapeDtypeStruct(block, F32)],
        scratch_shapes=[pltpu.VMEM((m, kk), BF16), pltpu.VMEM((m, n), F32)] + _pair_scratch(block),
        compiler_params=pltpu.CompilerParams(vmem_limit_bytes=56 << 20))(place, a16, b16)


def _adamw_math(w, g, m, v):
    m = ADAM_B1 * m + (1.0 - ADAM_B1) * g
    v = ADAM_B2 * v + (1.0 - ADAM_B2) * jnp.square(g)
    m_hat = m / (1.0 - ADAM_B1 ** ADAM_STEP)
    v_hat = v / (1.0 - ADAM_B2 ** ADAM_STEP)
    delta = -ADAM_LR * (m_hat / (jnp.sqrt(v_hat) + ADAM_EPS) + ADAM_WD * w)
    return delta, m, v


def _adamw_shards(updates, name, chip_sums=()):
    names, nu, ns = list(updates), len(updates), len(chip_sums)

    def body(*refs):
        ins, sum_refs = refs[:5 * nu], refs[5 * nu:5 * nu + ns]
        outs = refs[5 * nu + ns:9 * nu + ns]
        landed_refs, scratch = refs[9 * nu + ns:9 * nu + 2 * ns], refs[9 * nu + 2 * ns:]
        if ns:
            start_chips, finish_chips = _chips_steps(sum_refs, landed_refs, *scratch)
            start_chips()
        for i in range(nu):
            o_ref, r_ref, w_ref, m_ref, v_ref = ins[5 * i:5 * i + 5]
            g_out, d_out, m_out, v_out = outs[4 * i:4 * i + 4]
            g = o_ref[...] + r_ref[0].astype(F32) + r_ref[1].astype(F32) + r_ref[2].astype(F32)
            g_out[...] = g
            d_out[...], m_out[...], v_out[...] = _adamw_math(w_ref[...], g, m_ref[...], v_ref[...])
        if ns:
            finish_chips()

    vmem = pl.BlockSpec(memory_space=pltpu.VMEM)
    out = pl.pallas_call(
        body, name=name,
        in_specs=[vmem] * (5 * nu) + [ANY] * ns, out_specs=[vmem] * (4 * nu) + [ANY] * ns,
        out_shape=[jax.ShapeDtypeStruct(updates[n][2].shape, F32) for n in names for _ in range(4)]
        + _chips_shapes(chip_sums),
        scratch_shapes=_chips_scratch(ns) if ns else [],
        compiler_params=pltpu.CompilerParams(vmem_limit_bytes=56 << 20),
    )(*[a for n in names for a in updates[n]], *chip_sums)
    return {n: out[4 * i:4 * i + 4] for i, n in enumerate(names)}, list(out[4 * nu:])


def _place():
    x, y, c = lax.axis_index("x"), lax.axis_index("y"), lax.axis_index("c")
    chips = [(1 - x, y), (x, 1 - y), (1 - x, 1 - y)]
    return x, y, c, chips


def _gather_steps(ins, outs, send, recv, lsem):
    nt = len(ins)
    x, y, c, chips = _place()
    me, sib = (x, y, c), (x, y, 1 - c)

    def slot(t, px, py, pc):
        return outs[t].at[4 * px + 2 * py + pc]

    def copy(t, k, block, to, src=None):
        return pltpu.make_async_remote_copy(
            src_ref=slot(t, *block) if src is None else src, dst_ref=slot(t, *block),
            send_sem=send.at[t, k], recv_sem=recv.at[t, k], device_id=to, device_id_type=MESH)

    mine = [pltpu.make_async_copy(ins[t], slot(t, *me), lsem.at[t]) for t in range(nt)]
    first = []
    for t in range(nt):
        first.append(copy(t, 0, me, sib, src=ins[t]))
        first += [copy(t, 1 + j, me, (*chip, c), src=ins[t]) for j, chip in enumerate(chips)]

    def start():
        for cp in mine + first:
            cp.start()

    def finish():
        passed = []
        for j, chip in enumerate(chips):
            for t in range(nt):
                copy(t, 1 + j, (*chip, c), me).wait_recv()
                fwd = copy(t, 4 + j, (*chip, c), sib)
                fwd.start()
                passed.append(fwd)
        for t in range(nt):
            copy(t, 0, sib, me).wait_recv()
            for j, chip in enumerate(chips):
                copy(t, 4 + j, (*chip, 1 - c), me).wait_recv()
        for cp in first + passed:
            cp.wait_send()
        for cp in mine:
            cp.wait()

    return start, finish


def _gather_scratch(nt):
    return [pltpu.SemaphoreType.DMA((nt, 7)), pltpu.SemaphoreType.DMA((nt, 7)), pltpu.SemaphoreType.DMA((nt,))]


def _gathered_shapes(shards):
    return [jax.ShapeDtypeStruct((N_DEV,) + s.shape, s.dtype) for s in shards]


def _call_with_gather(body, n_grid, shards, *, name, in_specs, out_specs, out_shape, scratch_shapes, vmem_mb, args):
    ng, n_in, n_out = len(shards), len(in_specs), len(out_specs)

    def wrapped(*refs):
        ins, shard_refs = refs[:n_in], refs[n_in:n_in + ng]
        outs = refs[n_in + ng:n_in + ng + n_out]
        whole_refs = refs[n_in + ng + n_out:n_in + 2 * ng + n_out]
        scratch = refs[n_in + 2 * ng + n_out:]
        if ng:
            start, finish = _gather_steps(shard_refs, whole_refs, *scratch[len(scratch_shapes):])
            pl.when(pl.program_id(0) == 0)(start)
        body(*ins, *outs, *scratch[:len(scratch_shapes)])
        if ng:
            pl.when(pl.program_id(0) == n_grid - 1)(finish)

    return pl.pallas_call(
        wrapped, grid=(n_grid,), name=name,
        in_specs=list(in_specs) + [ANY] * ng, out_specs=list(out_specs) + [ANY] * ng,
        out_shape=list(out_shape) + _gathered_shapes(shards),
        scratch_shapes=list(scratch_shapes) + (_gather_scratch(ng) if ng else []),
        compiler_params=_cparams(vmem_mb))(*args, *shards)


def _chips_steps(ins, outs, send, recv):
    _, _, c, chips = _place()
    copies = [pltpu.make_async_remote_copy(
        src_ref=ins[t].at[2 * px + py], dst_ref=outs[t].at[j], send_sem=send.at[t, j], recv_sem=recv.at[t, j],
        device_id=(px, py, c), device_id_type=MESH) for t in range(len(ins)) for j, (px, py) in enumerate(chips)]

    def start():
        for cp in copies:
            cp.start()

    def finish():
        for cp in copies:
            cp.wait()

    return start, finish


def _chips_scratch(nt):
    return [pltpu.SemaphoreType.DMA((nt, 3)), pltpu.SemaphoreType.DMA((nt, 3))]


def _chips_shapes(cs16s):
    return [jax.ShapeDtypeStruct((3,) + g.shape[1:], g.dtype) for g in cs16s]


SMALL = (("g_pre_mix", 0, 0, D), ("g_mem", 1, 0, D), ("g_post_mix", 2, 0, D), ("g_attn_out", 3, 0, AW),
         ("g_conv_out", 3, AW, CW), ("g_xattn_out", 3, AW + CW, XW), ("g_post_mlp", 4, 0, D), ("g_pre_mlp", 5, 0, D))
CONV_ROW = 8
PACK_ROWS = 16


LOSS_ROW = 15


def _small_all_reduce(dg_in, dg_mem, dgs, dg_mlp, dcw, loss8):
    def body(acc_in, acc_mem, acc_mix, acc_mlp, acc_cw, acc_loss, tot_ref, pack, land, send, recv):
        x, y, c, _ = _place()
        me = 4 * x + 2 * y + c
        pack[...] = jnp.zeros_like(pack)
        pack[0:1, :] = acc_in[0:1, :]
        pack[1:2, :] = acc_mem[0:1, :]
        pack[2:4, :] = acc_mix[0:2, :]
        pack[4:6, :] = acc_mlp[0:2, :]
        pack[CONV_ROW:CONV_ROW + 3, 0:CW] = acc_cw[0:3, :]
        pack[LOSS_ROW:LOSS_ROW + 1, 0:LANES] = acc_loss[0:1, :]
        land[me] = pack[...]
        copies = []
        for k in range(1, N_DEV):
            kx, ky, kc = (k >> 2) & 1, (k >> 1) & 1, k & 1
            peer = (1 - x if kx else x, 1 - y if ky else y, 1 - c if kc else c)
            copies.append(pltpu.make_async_remote_copy(
                src_ref=pack, dst_ref=land.at[me], send_sem=send.at[k - 1], recv_sem=recv.at[k - 1],
                device_id=peer, device_id_type=MESH))
        for cp in copies:
            cp.start()
        for cp in copies:
            cp.wait()
        tot = land[0]
        for s in range(1, N_DEV):
            tot = tot + land[s]
        tot_ref[...] = tot

    return pl.pallas_call(
        body, name="small_all_reduce", out_shape=jax.ShapeDtypeStruct((PACK_ROWS, D), F32),
        scratch_shapes=[pltpu.VMEM((PACK_ROWS, D), F32), pltpu.VMEM((N_DEV, PACK_ROWS, D), F32),
                        pltpu.SemaphoreType.DMA((N_DEV - 1,)), pltpu.SemaphoreType.DMA((N_DEV - 1,))],
    )(dg_in, dg_mem, dgs, dg_mlp, dcw, loss8)


def _small_update(tot, me, params):
    flat = [a for n, _, _, _ in SMALL for a in params[n]] + list(params["conv_w"])
    n_par = len(SMALL) + 1
    tap_cols = CW // N_DEV

    def body(*refs):
        me_ref, tot_ref = refs[0:2]
        ins = refs[2:2 + 3 * n_par]
        loss_out = refs[2 + 3 * n_par]
        outs = refs[3 + 3 * n_par:]
        tot = tot_ref[...]
        loss_out[...] = jnp.broadcast_to(tot[LOSS_ROW:LOSS_ROW + 1, 0:LANES], loss_out.shape)

        def update(i, g):
            w_ref, m_ref, v_ref = ins[3 * i:3 * i + 3]
            g_out, d_out, m_out, v_out = outs[4 * i:4 * i + 4]
            g_out[...] = g
            d_out[...], m_out[...], v_out[...] = _adamw_math(w_ref[...], g, m_ref[...], v_ref[...])

        for i, (_, row, lane0, width) in enumerate(SMALL):
            update(i, tot[row:row + 1, lane0:lane0 + width])
        me = me_ref[0]
        taps = pltpu.roll(tot[CONV_ROW:CONV_ROW + SUBLANES, 0:CW], jnp.where(me == 0, 0, CW - me * tap_cols), 1)
        update(n_par - 1, taps[0:3, 0:tap_cols])

    shapes = [jax.ShapeDtypeStruct(params[n][0].shape, F32) for n, _, _, _ in SMALL] + [
        jax.ShapeDtypeStruct(params["conv_w"][0].shape, F32)]
    vmem = pl.BlockSpec(memory_space=pltpu.VMEM)
    loss, *out = pl.pallas_call(
        body, name="small_update",
        in_specs=[pl.BlockSpec(memory_space=pltpu.SMEM)] + [vmem] * (1 + 3 * n_par),
        out_shape=[jax.ShapeDtypeStruct((SUBLANES, LANES), F32)] + [s for s in shapes for _ in range(4)],
    )(me, tot, *flat)
    names = [n for n, _, _, _ in SMALL] + ["conv_w"]
    return loss[0, 0], {n: out[4 * i:4 * i + 4] for i, n in enumerate(names)}


def _local_step(x, mem, pos, gains, shards, tgt, place):
    half = HEAD // 2
    inv_freq = jnp.float32(ROPE_THETA) ** (-(jnp.arange(half, dtype=F32) * 2.0 / HEAD))
    invf = jnp.tile(inv_freq, LANES // half)[None, :]
    sgn = jnp.tile(jnp.concatenate([-jnp.ones((half,), F32), jnp.ones((half,), F32)]), LANES // HEAD)[None, :]
    cos, sins, win8 = _rope_table(pos.astype(F32).reshape(S, 1), invf, sgn, [shards["w_in"]])
    q, kvp, bcu, qx16, h16, win16, wout8, wkv8, conv8 = _in_proj(
        x, gains["g_pre_mix"], win8, cos, sins, [shards["w_out"], shards["w_mem_kv"], shards["conv_w"]])
    wout16, wkv16 = wout8.reshape(D, D), wkv8.reshape(D, 2 * XW)
    cw_full = conv8[:, 0:3, 0:CW // N_DEV].transpose(1, 0, 2).reshape(3, CW)
    cw8 = jnp.zeros((SUBLANES, CW), F32).at[0:3].set(cw_full)
    y_attn, ltot, wup8, wdn8 = _attn_fwd(q, kvp, [shards["w_up"], shards["w_down"]])
    wdn16 = wdn8.reshape(FF, D)
    memn16, kv16 = _mem_fwd(mem, gains["g_mem"], wkv16)
    ypre, y16, y2, x1 = _mix_out(y_attn, bcu, qx16, kv16, cw8, gains["g_attn_out"], gains["g_conv_out"],
                                 gains["g_xattn_out"], gains["g_post_mix"], wout16, x, [])
    a16, du16, h2_16, df2_16, dx1, loss8, dg_mlp = _mlp(x1, tgt, gains["g_pre_mlp"], gains["g_post_mlp"], wup8, wdn16)

    sums = {"w_up": _wgrad_cols(place, h2_16, du16, FF_BLK, "wgrad_up"),
            "w_down": _wgrad_cols(place, df2_16, a16, FF_BLK, "wgrad_down", square_b=True, transpose_out=True)}

    head_id = jnp.arange(AW, dtype=jnp.int32) // HEAD
    head_ones = (head_id[:, None] == head_id[None, :]).astype(BF16)
    dy2_16, qdo, ld, dbcu, dqx, dgs, dcw, dkv = _mix_out_bwd(
        dx1, y2, ypre, ltot, head_ones, q, bcu, qx16, kv16, cw8, gains["g_post_mix"], gains["g_attn_out"],
        gains["g_conv_out"], gains["g_xattn_out"], wout16)
    dkv16, dg_mem = _mem_bwd(mem, gains["g_mem"], wkv16, dkv)
    sums["w_mem_kv"] = _wgrad_rows(place, memn16, dkv16, "wgrad_mem_kv")
    sums["w_out"] = _wgrad_rows(place, y16, dy2_16, "wgrad_out")
    out = _attn_bwd(qdo, kvp, ld, [s[0] for s in sums.values()])
    dqkv, landed = out[:9], out[9:]
    reduced = {n: (s[1], landed[t]) for t, (n, s) in enumerate(sums.items())}
    dproj16, grad_x, dg_in = _in_proj_bwd(dqkv, dbcu, dqx, cos, sins, win16, x, gains["g_pre_mix"], dx1)

    in_sums = _wgrad_cols(place, h16, dproj16, PW // N_DEV, "wgrad_in")
    return grad_x, reduced, in_sums, (dg_in, dg_mem, dgs, dg_mlp, dcw, loss8)


BIG = ("w_in", "w_mem_kv", "w_out", "w_up", "w_down")
ORDER = ("g_pre_mix", "g_mem", "w_in", "w_mem_kv", "conv_w", "g_attn_out", "g_conv_out", "g_xattn_out", "w_out",
         "g_post_mix", "g_pre_mlp", "w_up", "w_down", "g_post_mlp")


def kernel(x, mem, positions, g_pre_mix, g_mem, w_in, w_mem_kv, conv_w, g_attn_out, g_conv_out, g_xattn_out, w_out, g_post_mix, g_pre_mlp, w_up, w_down, g_post_mlp, loss_target, m_g_pre_mix, m_g_mem, m_w_in, m_w_mem_kv, m_conv_w, m_g_attn_out, m_g_conv_out, m_g_xattn_out, m_w_out, m_g_post_mix, m_g_pre_mlp, m_w_up, m_w_down, m_g_post_mlp, v_g_pre_mix, v_g_mem, v_w_in, v_w_mem_kv, v_conv_w, v_g_attn_out, v_g_conv_out, v_g_xattn_out, v_w_out, v_g_post_mix, v_g_pre_mlp, v_w_up, v_w_down, v_g_post_mlp):
    w = dict(g_pre_mix=g_pre_mix, g_mem=g_mem, w_in=w_in, w_mem_kv=w_mem_kv, conv_w=conv_w, g_attn_out=g_attn_out,
             g_conv_out=g_conv_out, g_xattn_out=g_xattn_out, w_out=w_out, g_post_mix=g_post_mix, g_pre_mlp=g_pre_mlp,
             w_up=w_up, w_down=w_down, g_post_mlp=g_post_mlp)
    mo = dict(g_pre_mix=m_g_pre_mix, g_mem=m_g_mem, w_in=m_w_in, w_mem_kv=m_w_mem_kv, conv_w=m_conv_w,
              g_attn_out=m_g_attn_out, g_conv_out=m_g_conv_out, g_xattn_out=m_g_xattn_out, w_out=m_w_out,
              g_post_mix=m_g_post_mix, g_pre_mlp=m_g_pre_mlp, w_up=m_w_up, w_down=m_w_down, g_post_mlp=m_g_post_mlp)
    vo = dict(g_pre_mix=v_g_pre_mix, g_mem=v_g_mem, w_in=v_w_in, w_mem_kv=v_w_mem_kv, conv_w=v_conv_w,
              g_attn_out=v_g_attn_out, g_conv_out=v_g_conv_out, g_xattn_out=v_g_xattn_out, w_out=v_w_out,
              g_post_mix=v_g_post_mix, g_pre_mlp=v_g_pre_mlp, w_up=v_w_up, w_down=v_w_down, g_post_mlp=v_g_post_mlp)

    xi, yi, ci = lax.axis_index("x"), lax.axis_index("y"), lax.axis_index("c")
    me = 4 * xi + 2 * yi + ci
    place = jnp.stack([ci, 2 * xi + yi]).astype(jnp.int32)

    shards = {n: w[n][0].astype(BF16) for n in BIG}
    shards["conv_w"] = jnp.zeros((SUBLANES, LANES), F32).at[0:3, 0:CW // N_DEV].set(conv_w[0])

    gains = {n: w[n] for n, _, _, _ in SMALL}
    grad_x, reduced, in_sums, small_acc = _local_step(
        x[0], mem[0], positions[0], gains, shards, loss_target[0], place)

    state = lambda n: (w[n][0], mo[n][0], vo[n][0])
    updated, in_chips = _adamw_shards({n: (*reduced[n], *state(n)) for n in reduced}, "adamw_shards", [in_sums[0]])
    updated.update(_adamw_shards({"w_in": (in_sums[1], in_chips[0], *state("w_in"))}, "adamw_w_in")[0])
    grad, delta, new_m, new_v = {}, {}, {}, {}
    for n, (g, d_, m_, v_) in updated.items():
        grad[n], delta[n], new_m[n], new_v[n] = g[None], d_[None], m_[None], v_[None]

    params = {n: (w[n], mo[n], vo[n]) for n, _, _, _ in SMALL}
    params["conv_w"] = (w["conv_w"][0], mo["conv_w"][0], vo["conv_w"][0])
    loss, small = _small_update(_small_all_reduce(*small_acc), me.reshape(1).astype(jnp.int32), params)
    for n, (g, d_, m_, v_) in small.items():
        lead = (lambda a: a[None]) if n == "conv_w" else (lambda a: a)
        grad[n], delta[n], new_m[n], new_v[n] = lead(g), lead(d_), lead(m_), lead(v_)

    return (loss, grad_x[None], *[grad[n] for n in ORDER], *[delta[n] for n in ORDER],
            *[new_m[n] for n in ORDER], *[new_v[n] for n in ORDER])
```

```python
import functools

import numpy as np
import jax
import jax.numpy as jnp
from jax import lax
from jax.experimental import pallas as pl
from jax.experimental.pallas import tpu as pltpu

F32, BF16 = jnp.float32, jnp.bfloat16
MESH = pl.DeviceIdType.MESH
ANY = pl.BlockSpec(memory_space=pl.ANY)

N_DEV = 8
D = 1024
S = 4096
N_MEM = 256
HEAD = 64
AW, CW, XW = 512, 256, 256
PW = 3 * AW + 3 * CW + XW
FF = 4096
FF_BLK = FF // N_DEV
PATTERNS = ((128, 1), (512, 4), (2048, 16))
QB = 128
EPS = 1e-6
NEG = -1e30
SCALE = HEAD ** -0.5
ROPE_THETA = 10000.0
LANES = 128
SUBLANES = 8

ADAM_LR, ADAM_B1, ADAM_B2, ADAM_EPS, ADAM_WD, ADAM_STEP = 0.001, 0.9, 0.999, 1e-08, 0.01, 10

TQ = 512
TQ_MLP = 256
NT = S // TQ


def _cparams(vmem_mb, n_grid=1):
    return pltpu.CompilerParams(dimension_semantics=("arbitrary",) * n_grid, vmem_limit_bytes=vmem_mb << 20)


def _const(shape):
    nd = len(shape)
    return pl.BlockSpec(shape, lambda *_: (0,) * nd, pipeline_mode=pl.Buffered(1))


def _acc(shape):
    nd = len(shape)
    return pl.BlockSpec(shape, lambda *_: (0,) * nd)


def _dot(a, b):
    return jnp.dot(a, b, preferred_element_type=F32)


def _dot_nt(a, b):
    return lax.dot_general(a, b, (((1,), (1,)), ((), ())), preferred_element_type=F32)


def _dot_tn(a, b):
    return lax.dot_general(a, b, (((0,), (0,)), ((), ())), preferred_element_type=F32)


def _rms(x, g):
    r = lax.rsqrt(jnp.mean(x * x, axis=-1, keepdims=True) + EPS)
    n = x * r
    return n * g, n, r


def _rms_bwd(dy, n, r, g):
    dn = dy * g
    dx = r * (dn - n * jnp.mean(dn * n, axis=-1, keepdims=True))
    return dx, jnp.sum(dy * n, axis=0, keepdims=True)


def _rot_half(t):
    lane = lax.broadcasted_iota(jnp.int32, t.shape, 1)
    n = t.shape[1]
    return jnp.where((lane % HEAD) < HEAD // 2, pltpu.roll(t, n - HEAD // 2, 1), pltpu.roll(t, HEAD // 2, 1))


def _rope_table(pos_col, invf, sgn, shards):
    def body(p_ref, f_ref, s_ref, c_out, s_out):
        ang = p_ref[...] * f_ref[...]
        c_out[...] = jnp.cos(ang)
        s_out[...] = jnp.sin(ang) * s_ref[...]

    tile = pl.BlockSpec((TQ, LANES), lambda i: (i, 0))
    return _call_with_gather(
        body, NT, shards, name="rope_table",
        in_specs=[pl.BlockSpec((TQ, 1), lambda i: (i, 0)), _const((1, LANES)), _const((1, LANES))],
        out_specs=[tile, tile], out_shape=[jax.ShapeDtypeStruct((S, LANES), F32)] * 2,
        scratch_shapes=[], vmem_mb=32, args=(pos_col, invf, sgn))


def _all_heads(t):
    return jnp.tile(t, (1, AW // LANES))


def _mem_fwd(mem, g_mem, wkv16):
    def body(m_ref, g_ref, w_ref, n16_ref, kv_ref):
        y, _, _ = _rms(m_ref[...], g_ref[...])
        y16 = y.astype(BF16)
        n16_ref[...] = y16
        kv_ref[...] = _dot(y16, w_ref[...]).astype(BF16)

    return pl.pallas_call(
        body, name="mem_fwd",
        out_shape=[jax.ShapeDtypeStruct((N_MEM, D), BF16), jax.ShapeDtypeStruct((N_MEM, 2 * XW), BF16)],
        compiler_params=pltpu.CompilerParams(vmem_limit_bytes=32 << 20))(mem, g_mem, wkv16)


def _in_proj(x, g, w8, cos, sins, shards):
    blk = PW // N_DEV

    def body(x_ref, g_ref, w8_ref, c_ref, s_ref, q_ref, kv_ref, bcu_ref, qx_ref, h_ref, w_out, w_ref):
        @pl.when(pl.program_id(0) == 0)
        def _():
            for j in range(N_DEV):
                w_ref[:, j * blk:(j + 1) * blk] = w8_ref[j]
            w_out[...] = w_ref[...]

        y, _, _ = _rms(x_ref[...], g_ref[...])
        h = y.astype(BF16)
        h_ref[...] = h
        proj = _dot(h, w_ref[...])
        cos, sn = _all_heads(c_ref[...]), _all_heads(s_ref[...])
        q, k = proj[:, 0:AW], proj[:, AW:2 * AW]
        q_ref[...] = (q * cos + _rot_half(q) * sn) * SCALE
        kv_ref[...] = _pack_pair(k * cos + _rot_half(k) * sn, proj[:, 2 * AW:3 * AW])
        bcu_ref[...] = proj[:, 3 * AW:3 * AW + 3 * CW]
        qx_ref[...] = (proj[:, 3 * AW + 3 * CW:] * SCALE).astype(BF16)

    def tile(w):
        return pl.BlockSpec((TQ, w), lambda i: (i, 0))

    return _call_with_gather(
        body, NT, shards, name="in_proj",
        in_specs=[tile(D), _const((1, D)), _const((N_DEV, D, blk)), tile(LANES), tile(LANES)],
        out_specs=[tile(AW), tile(AW), tile(3 * CW), tile(XW), tile(D), _acc((D, PW))],
        out_shape=[jax.ShapeDtypeStruct((S, AW), F32)] * 2 + [
            jax.ShapeDtypeStruct((S, 3 * CW), F32), jax.ShapeDtypeStruct((S, XW), BF16),
            jax.ShapeDtypeStruct((S, D), BF16), jax.ShapeDtypeStruct((D, PW), BF16)],
        scratch_shapes=[pltpu.VMEM((D, PW), BF16)], vmem_mb=56, args=(x, g, w8, cos, sins))


ATTN_PLANS = (("p1", 1, 128, 32), ("p4", 8, 64, 8), ("p16", 16, 128, 2))
PAD = 128
WIN = 256


ATTN_UNROLL = 8


def _fill_bias(tab, qblk, partner):
    qi = lax.broadcasted_iota(jnp.int32, (2 * qblk, WIN), 0) & (qblk - 1)
    kj = lax.broadcasted_iota(jnp.int32, (2 * qblk, WIN), 1)
    piece = kj >> (qblk.bit_length() - 1)
    kk = kj & (qblk - 1)
    prev = (piece & 1) == 0
    of_partner = piece >= 2
    for first in (0, 1):
        for par in (0, 1):
            lo = jnp.where(prev, (qblk if first else qi) + jnp.where(of_partner, par, 0), 0)
            hi = jnp.where(prev, qblk, qi + jnp.where(of_partner, par - 1, 0))
            tab[2 * first + par] = jnp.where((kk >= lo) & (kk <= hi), 0.0, NEG).astype(F32)


def _block_rows(g, qblk, nbc, partner):
    own = pl.ds(pl.multiple_of(PAD + g * qblk, qblk), qblk)
    first = ((g & (nbc - 1)) == 0).astype(jnp.int32)
    if partner:
        gp = jnp.bitwise_xor(g, 4 * nbc)
        wins = (pl.ds(pl.multiple_of(PAD + (g - 1) * qblk, qblk), 2 * qblk),
                pl.ds(pl.multiple_of(PAD + (gp - 1) * qblk, qblk), 2 * qblk))
        return own, wins, 2 * first + ((g >> ((4 * nbc).bit_length() - 1)) & 1)
    return own, (pl.ds(pl.multiple_of(PAD + (g - 1) * qblk, qblk), 2 * qblk),), 2 * first


def _pack_pair(lo, hi):
    lo_bits = lax.bitcast_convert_type(lo.astype(BF16).astype(F32), jnp.uint32) >> 16
    hi_bits = lax.bitcast_convert_type(hi.astype(BF16).astype(F32), jnp.uint32) & jnp.uint32(0xFFFF0000)
    return lax.bitcast_convert_type(hi_bits | lo_bits, F32)


def _unpack_pair(c):
    bits = lax.bitcast_convert_type(c, jnp.uint32)
    lo = lax.bitcast_convert_type(bits << 16, F32).astype(BF16)
    hi = lax.bitcast_convert_type(bits & jnp.uint32(0xFFFF0000), F32).astype(BF16)
    return lo, hi


def _window(ref, wins):
    parts = [ref[w, :] for w in wins]
    return parts[0] if len(parts) == 1 else jnp.concatenate(parts, axis=0)


def _stack_heads(t, lane):
    zero = jnp.zeros_like(t)
    return jnp.concatenate([jnp.where(lane < HEAD, t, zero), jnp.where(lane >= HEAD, t, zero)], axis=0)


def _unstack_heads(t2, lane):
    half = t2.shape[0] // 2
    return jnp.where(lane < HEAD, t2[0:half, :], t2[half:, :])


def _lanes_of(step):
    return pl.ds(pl.multiple_of(step * LANES, LANES), LANES)


def _whole_wait(buf, sem):
    whole = buf.at[pl.ds(PAD, S), :]
    return pltpu.make_async_copy(whole, whole, sem)


def _whole_waits(bufs, sems):
    return [_whole_wait(buf, sems.at[i]) for i, buf in enumerate(bufs)]


def _class_gather(views, bufs, sems, lanes):
    copies = []
    for i, (view, buf) in enumerate(zip(views, bufs)):
        if view.ndim == 2:
            copies.append(pltpu.make_async_copy(view.at[:, lanes], buf.at[pl.ds(PAD, S), :], sems.at[i]))
        else:
            per, n_cls = view.shape[0], view.shape[1]
            copies += [pltpu.make_async_copy(view.at[:, c, lanes], buf.at[pl.ds(PAD + c * per, per), :], sems.at[i])
                       for c in range(n_cls)]
    return copies


def _class_scatter(bufs, dsts, sems, lanes=None):
    copies = []
    for i, (buf, dst) in enumerate(zip(bufs, dsts)):
        if dst.ndim == 2:
            copies.append(pltpu.make_async_copy(buf.at[pl.ds(PAD, S), :], dst.at[:, lanes], sems.at[i]))
            continue
        per, n_cls = dst.shape[0], dst.shape[1]
        for c in range(n_cls):
            to = dst.at[:, c, :] if lanes is None else dst.at[:, c, lanes]
            copies.append(pltpu.make_async_copy(buf.at[pl.ds(PAD + c * per, per), :], to, sems.at[i]))
    return copies


def _start(copies):
    for cp in copies:
        cp.start()


def _wait(waits):
    for w in waits:
        w.wait()


def _attn_fwd(q, kvp, shards=()):
    views = [[a] + [a.reshape(S // n, n, AW) for _, n, _, _ in ATTN_PLANS[1:]] for a in (q, kvp)]
    flat = [views[a][p] for p in range(3) for a in range(2)]
    ng = len(shards)
    n_grid = AW // LANES

    def body(*refs):
        hbm = [refs[2 * p:2 * p + 2] for p in range(3)]
        refs = refs[6:]
        shard_refs, refs = refs[:ng], refs[ng:]
        y_ref, lt_ref = refs[0:2]
        whole_refs, refs = refs[2:2 + ng], refs[2 + ng:]
        bufs = [refs[2 * p:2 * p + 2] for p in range(3)]
        oc4, lc4, oc16, lc16, o4n, l4n, o16n, l16n, tab128, tab4, sem_in, sem_out = refs[6:18]
        step = pl.program_id(0)
        if ng:
            start_gather, finish_gather = _gather_steps(shard_refs, whole_refs, *refs[18:])
            pl.when(step == 0)(start_gather)
        now = [_class_gather(hbm[p], bufs[p], sem_in.at[p], _lanes_of(step)) for p in range(3)]
        nxt = [_class_gather(hbm[p], bufs[p], sem_in.at[p], _lanes_of(step + 1)) for p in range(3)]

        @pl.when(step == 0)
        def _():
            for p in range(3):
                _start(now[p])
                for b in bufs[p]:
                    b[0:PAD, :] = jnp.zeros((PAD, LANES), F32)
            _fill_bias(tab128, 128, False)
            _fill_bias(tab4, 64, True)

        def prefetch(p):
            pl.when(step + 1 < n_grid)(lambda: _start(nxt[p]))

        lane = lax.broadcasted_iota(jnp.int32, (1, LANES), 1)
        ones = jnp.ones((WIN, LANES), BF16)

        def run(plan, bq, bkv, tab, o_dst, l_dst, dst_pad):
            _, n_cls, qblk, nbc = plan
            partner = n_cls == 8

            def block(g, carry):
                own, wins, mask = _block_rows(g, qblk, nbc, partner)
                q2 = _stack_heads(bq[own, :].astype(BF16), lane)
                kw, vwin = _unpack_pair(_window(bkv, wins))
                vw = jnp.concatenate([vwin, ones], axis=1)
                s = _dot_nt(q2, kw) + tab[mask]
                m = jnp.max(s, axis=1, keepdims=True)
                oe = _dot(jnp.exp(s - m).astype(BF16), vw)
                den = oe[:, LANES:]
                dst = pl.ds(pl.multiple_of(dst_pad + g * qblk, qblk), qblk)
                o_dst[dst, :] = _unstack_heads(oe[:, 0:LANES] / den, lane)
                l_dst[dst, :] = _unstack_heads(m + jnp.log(den), lane)
                return carry
            lax.fori_loop(0, n_cls * nbc, block, 0, unroll=ATTN_UNROLL)

        _wait(_whole_waits(bufs[0], sem_in.at[0]))
        run(ATTN_PLANS[0], *bufs[0], tab128, y_ref, lt_ref, 0)
        prefetch(0)
        _wait(_whole_waits(bufs[1], sem_in.at[1]))
        run(ATTN_PLANS[1], *bufs[1], tab4, oc4, lc4, PAD)
        prefetch(1)
        _start(_class_scatter((oc4, lc4), (o4n, l4n), sem_out.at[0]))
        _wait(_whole_waits(bufs[2], sem_in.at[2]))
        run(ATTN_PLANS[2], *bufs[2], tab128, oc16, lc16, PAD)
        prefetch(2)
        _start(_class_scatter((oc16, lc16), (o16n, l16n), sem_out.at[1]))
        _wait(_whole_waits((oc4, lc4), sem_out.at[0]) + _whole_waits((oc16, lc16), sem_out.at[1]))

        for t in range(S // TQ):
            rows = pl.ds(t * TQ, TQ)
            r4, r16 = pl.ds(t * (TQ // 8), TQ // 8), pl.ds(t * (TQ // 16), TQ // 16)
            l0, l1, l2 = lt_ref[rows, :], l4n[r4, :, :].reshape(TQ, LANES), l16n[r16, :, :].reshape(TQ, LANES)
            lm = jnp.maximum(jnp.maximum(l0, l1), l2)
            e0, e1, e2 = jnp.exp(l0 - lm), jnp.exp(l1 - lm), jnp.exp(l2 - lm)
            den = e0 + e1 + e2
            y_ref[rows, :] = (e0 * y_ref[rows, :] + e1 * o4n[r4, :, :].reshape(TQ, LANES)
                              + e2 * o16n[r16, :, :].reshape(TQ, LANES)) / den
            lt_ref[rows, :] = lm + jnp.log(den)

        if ng:
            pl.when(step == n_grid - 1)(finish_gather)

    col = pl.BlockSpec((S, LANES), lambda h: (0, h))
    padded = pltpu.VMEM((PAD + S, LANES), F32)
    return pl.pallas_call(
        body, grid=(n_grid,), name="attn_fwd",
        in_specs=[ANY] * (6 + ng), out_specs=[col, col] + [ANY] * ng,
        out_shape=[jax.ShapeDtypeStruct((S, AW), F32)] * 2 + _gathered_shapes(shards),
        scratch_shapes=[padded] * 10 + [
            pltpu.VMEM((S // 8, 8, LANES), F32), pltpu.VMEM((S // 8, 8, LANES), F32),
            pltpu.VMEM((S // 16, 16, LANES), F32), pltpu.VMEM((S // 16, 16, LANES), F32),
            pltpu.VMEM((4, 256, WIN), F32), pltpu.VMEM((4, 128, WIN), F32),
            pltpu.SemaphoreType.DMA((3, 2)), pltpu.SemaphoreType.DMA((2, 2))]
        + (_gather_scratch(ng) if ng else []),
        compiler_params=_cparams(56))(*flat, *shards)


def _conv_taps(z, zprev, row):
    z1 = jnp.where(row == 0, zprev[7:8, :], pltpu.roll(z, 1, 0))
    z2 = jnp.where(row == 0, zprev[6:7, :], jnp.where(row == 1, zprev[7:8, :], pltpu.roll(z, 2, 0)))
    return z1, z2


def _xattn_scores(qm, km):
    s = _dot_nt(qm, km)
    m = jnp.max(s, axis=1, keepdims=True)
    e = jnp.exp(s - m)
    return e, jnp.sum(e, axis=1, keepdims=True)


def _mix_out(y_attn, bcu, qx16, kv16, cw8, g_attn, g_conv, g_x, g_post, wout16, x, shards):
    def body(ya_ref, bcu_ref, halo_ref, qx_ref, kv_ref, cw_ref, ga_ref, gc_ref, gx_ref, gp_ref, w_ref, x_ref,
             ypre_ref, y16_ref, y2_ref, x1_ref):
        i = pl.program_id(0)
        bcu = bcu_ref[...]
        b, c, u = bcu[:, 0:CW], bcu[:, CW:2 * CW], bcu[:, 2 * CW:]
        z = c * u
        halo = halo_ref[...]
        zprev = jnp.where(i > 0, halo[:, CW:2 * CW] * halo[:, 2 * CW:], 0.0)
        row = lax.broadcasted_iota(jnp.int32, z.shape, 0)
        z1, z2 = _conv_taps(z, zprev, row)
        cw = cw_ref[...]
        y_conv = b * (z2 * cw[0:1, :] + z1 * cw[1:2, :] + z * cw[2:3, :])

        qx = qx_ref[...]
        kv = kv_ref[...]
        km, vm = kv[:, 0:XW], kv[:, XW:]
        lane = lax.broadcasted_iota(jnp.int32, qx.shape, 1)
        y_x = jnp.zeros(qx.shape, F32)
        for h in range(XW // HEAD):
            hm = (lane >= h * HEAD) & (lane < (h + 1) * HEAD)
            e, l = _xattn_scores(jnp.where(hm, qx, jnp.zeros_like(qx)), km)
            y_x = jnp.where(hm, _dot(e.astype(BF16), vm) / l, y_x)

        y_attn = ya_ref[...]
        ypre_ref[:, 0:AW] = y_attn
        ypre_ref[:, AW:AW + CW] = y_conv
        ypre_ref[:, AW + CW:] = y_x
        y = jnp.concatenate([_rms(y_attn, ga_ref[...])[0], _rms(y_conv, gc_ref[...])[0],
                             _rms(y_x, gx_ref[...])[0]], axis=1).astype(BF16)
        y16_ref[...] = y
        y2 = _dot(y, w_ref[...])
        y2_ref[...] = y2
        x1_ref[...] = x_ref[...] + _rms(y2, gp_ref[...])[0]

    def tile(w):
        return pl.BlockSpec((TQ, w), lambda i: (i, 0))

    halo = pl.BlockSpec((SUBLANES, 3 * CW), lambda i: (jnp.maximum(i * (TQ // SUBLANES) - 1, 0), 0))
    return _call_with_gather(
        body, NT, shards, name="mix_out",
        in_specs=[tile(AW), tile(3 * CW), halo, tile(XW), _const((N_MEM, 2 * XW)), _const((SUBLANES, CW)),
                  _const((1, AW)), _const((1, CW)), _const((1, XW)), _const((1, D)), _const((D, D)), tile(D)],
        out_specs=[tile(D), tile(D), tile(D), tile(D)],
        out_shape=[jax.ShapeDtypeStruct((S, D), F32), jax.ShapeDtypeStruct((S, D), BF16),
                   jax.ShapeDtypeStruct((S, D), F32), jax.ShapeDtypeStruct((S, D), F32)],
        scratch_shapes=[], vmem_mb=56,
        args=(y_attn, bcu, bcu, qx16, kv16, cw8, g_attn, g_conv, g_x, g_post, wout16, x))


def _mlp(x1, tgt, g_pre, g_post, wup8, wdn16):
    tq = TQ_MLP

    def body(x1_ref, t_ref, g1_ref, g2_ref, wu_ref, wd_ref,
             a16_ref, du_ref, h2_ref, df2_ref, dx1_ref, loss_ref, dg_ref, a32):
        @pl.when(pl.program_id(0) == 0)
        def _():
            loss_ref[...] = jnp.zeros_like(loss_ref)
            dg_ref[...] = jnp.zeros_like(dg_ref)

        x1 = x1_ref[...]
        g1, g2 = g1_ref[...], g2_ref[...]
        y1, n1, r1 = _rms(x1, g1)
        h2 = y1.astype(BF16)
        h2_ref[...] = h2
        f2 = jnp.zeros((tq, D), F32)
        for j in range(N_DEV):
            cols = slice(j * FF_BLK, (j + 1) * FF_BLK)
            a = jnp.maximum(_dot(h2, wu_ref[j]), 0.0)
            a32[:, cols] = a
            a16_ref[:, cols] = a.astype(BF16)
            f2 = f2 + _dot((a * a).astype(BF16), wd_ref[cols, :])
        y2, n2, r2 = _rms(f2, g2)
        e = x1 + y2 - t_ref[...]
        sq = jnp.sum(jnp.sum(e * e, axis=1, keepdims=True), axis=0, keepdims=True)
        loss_ref[...] += jnp.broadcast_to(sq * (0.5 / D), loss_ref.shape)
        dout = e * (1.0 / D)
        df2, dg2 = _rms_bwd(dout, n2, r2, g2)
        df2_16 = df2.astype(BF16)
        df2_ref[...] = df2_16
        dh2 = jnp.zeros((tq, D), F32)
        for j in range(N_DEV):
            cols = slice(j * FF_BLK, (j + 1) * FF_BLK)
            du = (_dot_nt(df2_16, wd_ref[cols, :]) * (2.0 * a32[:, cols])).astype(BF16)
            du_ref[:, cols] = du
            dh2 = dh2 + _dot_nt(du, wu_ref[j])
        dx, dg1 = _rms_bwd(dh2, n1, r1, g1)
        dx1_ref[...] = dout + dx
        dg_ref[0:1, :] += dg2
        dg_ref[1:2, :] += dg1

    def tile(w):
        return pl.BlockSpec((tq, w), lambda i: (i, 0))

    return pl.pallas_call(
        body, grid=(S // tq,), name="mlp",
        in_specs=[tile(D), tile(D), _const((1, D)), _const((1, D)), _const((N_DEV, D, FF_BLK)), _const((FF, D))],
        out_specs=[tile(FF), tile(FF), tile(D), tile(D), tile(D), _acc((SUBLANES, LANES)), _acc((SUBLANES, D))],
        out_shape=[jax.ShapeDtypeStruct((S, FF), BF16), jax.ShapeDtypeStruct((S, FF), BF16),
                   jax.ShapeDtypeStruct((S, D), BF16), jax.ShapeDtypeStruct((S, D), BF16),
                   jax.ShapeDtypeStruct((S, D), F32), jax.ShapeDtypeStruct((SUBLANES, LANES), F32),
                   jax.ShapeDtypeStruct((SUBLANES, D), F32)],
        scratch_shapes=[pltpu.VMEM((tq, FF), F32)],
        compiler_params=_cparams(56))(x1, tgt, g_pre, g_post, wup8, wdn16)


def _mix_out_bwd(dx1, y2, ypre, ltot, head_ones, q, bcu, qx16, kv16, cw8, g_post, g_attn, g_conv, g_x, wout16):
    def body(dx1_ref, y2_ref, ypre_ref, lt_ref, e_ref, q_ref, bcu_ref, halo_ref, qx_ref, kv_ref, cw_ref, gp_ref,
             ga_ref, gc_ref, gx_ref, w_ref, dy2_ref, qdo_ref, ld_ref, dbcu_ref, dqx_ref, dgs_ref, dcw_ref, dkv_ref,
             carry):
        i = pl.program_id(0)

        @pl.when(i == 0)
        def _():
            dgs_ref[...] = jnp.zeros_like(dgs_ref)
            dcw_ref[...] = jnp.zeros_like(dcw_ref)
            dkv_ref[...] = jnp.zeros_like(dkv_ref)
            carry[...] = jnp.zeros_like(carry)

        gp = gp_ref[...]
        _, n, r = _rms(y2_ref[...], gp)
        dy2, dgp = _rms_bwd(dx1_ref[...], n, r, gp)
        dy2_16 = dy2.astype(BF16)
        dy2_ref[...] = dy2_16
        dy = _dot_nt(dy2_16, w_ref[...])

        ypre = ypre_ref[...]
        ga, gc, gx = ga_ref[...], gc_ref[...], gx_ref[...]
        _, na, ra = _rms(ypre[:, 0:AW], ga)
        dya, dga = _rms_bwd(dy[:, 0:AW], na, ra, ga)
        _, nc, rc = _rms(ypre[:, AW:AW + CW], gc)
        dyc, dgc = _rms_bwd(dy[:, AW:AW + CW], nc, rc, gc)
        y_x = ypre[:, AW + CW:]
        _, nx, rx = _rms(y_x, gx)
        dyx, dgx = _rms_bwd(dy[:, AW + CW:], nx, rx, gx)
        qdo_ref[...] = _pack_pair(q_ref[...], dya)
        prod = dya * ypre[:, 0:AW]
        hi = prod.astype(BF16)
        lo = (prod - hi.astype(F32)).astype(BF16)
        head_sum = _dot(hi, e_ref[...]) + _dot(lo, e_ref[...])
        lane_a = lax.broadcasted_iota(jnp.int32, prod.shape, 1)
        ld_ref[...] = jnp.where((lane_a % HEAD) < HEAD // 2, lt_ref[...], head_sum)
        dgs_ref[0:1, :] += dgp
        dgs_ref[1:2, :] += jnp.concatenate([dga, dgc, dgx], axis=1)

        bcu = bcu_ref[...]
        b, c, u = bcu[:, 0:CW], bcu[:, CW:2 * CW], bcu[:, 2 * CW:]
        z = c * u
        halo = halo_ref[...]
        zprev = jnp.where(i < NT - 1, halo[:, CW:2 * CW] * halo[:, 2 * CW:], 0.0)
        row = lax.broadcasted_iota(jnp.int32, z.shape, 0)
        z1, z2 = _conv_taps(z, zprev, row)
        cw = cw_ref[...]
        conv = z2 * cw[0:1, :] + z1 * cw[1:2, :] + z * cw[2:3, :]
        dconv = dyc * b
        nxt = carry[...]
        dn1 = jnp.where(row == TQ - 1, nxt[0:1, :], pltpu.roll(dconv, TQ - 1, 0))
        dn2 = jnp.where(row == TQ - 1, nxt[1:2, :], jnp.where(row == TQ - 2, nxt[0:1, :], pltpu.roll(dconv, TQ - 2, 0)))
        carry[...] = dconv[0:SUBLANES, :]
        dz = dconv * cw[2:3, :] + dn1 * cw[1:2, :] + dn2 * cw[0:1, :]
        dbcu_ref[:, 0:CW] = (dyc * conv).astype(BF16)
        dbcu_ref[:, CW:2 * CW] = (dz * u).astype(BF16)
        dbcu_ref[:, 2 * CW:] = (dz * c).astype(BF16)
        dcw_ref[0:1, :] += jnp.sum(z2 * dconv, axis=0, keepdims=True)
        dcw_ref[1:2, :] += jnp.sum(z1 * dconv, axis=0, keepdims=True)
        dcw_ref[2:3, :] += jnp.sum(z * dconv, axis=0, keepdims=True)

        qx = qx_ref[...]
        kv = kv_ref[...]
        km, vm = kv[:, 0:XW], kv[:, XW:]
        lane = lax.broadcasted_iota(jnp.int32, qx.shape, 1)
        dqx = jnp.zeros(qx.shape, F32)
        dkm = jnp.zeros((N_MEM, XW), F32)
        dvm = jnp.zeros((N_MEM, XW), F32)
        for h in range(XW // HEAD):
            hm = (lane >= h * HEAD) & (lane < (h + 1) * HEAD)
            qm = jnp.where(hm, qx, jnp.zeros_like(qx))
            e, l = _xattn_scores(qm, km)
            p = e / l
            dom = jnp.where(hm, dyx, 0.0)
            do16 = dom.astype(BF16)
            dsum = jnp.sum(dom * y_x, axis=1, keepdims=True)
            ds = (p * (_dot_nt(do16, vm) - dsum)).astype(BF16)
            dqx = jnp.where(hm, _dot(ds, km), dqx)
            dkm = dkm + _dot_tn(ds, qm)
            dvm = dvm + _dot_tn(p.astype(BF16), do16)
        dqx_ref[...] = (dqx * SCALE).astype(BF16)
        dkv_ref[:, 0:XW] += dkm
        dkv_ref[:, XW:] += dvm

    def tile(w):
        return pl.BlockSpec((TQ, w), lambda i: (NT - 1 - i, 0))

    halo = pl.BlockSpec((SUBLANES, 3 * CW), lambda i: (jnp.maximum((NT - 1 - i) * (TQ // SUBLANES) - 1, 0), 0))
    return pl.pallas_call(
        body, grid=(NT,), name="mix_out_bwd",
        in_specs=[tile(D), tile(D), tile(D), tile(AW), _const((AW, AW)), tile(AW), tile(3 * CW), halo, tile(XW),
                  _const((N_MEM, 2 * XW)), _const((SUBLANES, CW)), _const((1, D)), _const((1, AW)), _const((1, CW)),
                  _const((1, XW)), _const((D, D))],
        out_specs=[tile(D), tile(AW), tile(AW), tile(3 * CW), tile(XW), _acc((SUBLANES, D)), _acc((SUBLANES, CW)),
                   _acc((N_MEM, 2 * XW))],
        out_shape=[jax.ShapeDtypeStruct((S, D), BF16), jax.ShapeDtypeStruct((S, AW), F32),
                   jax.ShapeDtypeStruct((S, AW), F32),
                   jax.ShapeDtypeStruct((S, 3 * CW), BF16), jax.ShapeDtypeStruct((S, XW), BF16),
                   jax.ShapeDtypeStruct((SUBLANES, D), F32), jax.ShapeDtypeStruct((SUBLANES, CW), F32),
                   jax.ShapeDtypeStruct((N_MEM, 2 * XW), F32)],
        scratch_shapes=[pltpu.VMEM((SUBLANES, CW), F32)],
        compiler_params=_cparams(56))(dx1, y2, ypre, ltot, head_ones, q, bcu, bcu, qx16, kv16, cw8, g_post, g_attn,
                                      g_conv, g_x, wout16)


def _attn_bwd(qdo, kvp, ld, chip_sums=()):
    n_in = 3
    views = [[a] + [a.reshape(S // n, n, AW) for _, n, _, _ in ATTN_PLANS[1:]] for a in (qdo, kvp, ld)]
    flat = [views[a][p] for p in range(3) for a in range(n_in)]
    ns = len(chip_sums)
    n_grid = AW // LANES

    def body(*refs):
        hbm = [refs[n_in * p:n_in * p + n_in] for p in range(3)]
        refs = refs[3 * n_in:]
        sum_refs, refs = refs[:ns], refs[ns:]
        outs = [refs[3 * p:3 * p + 3] for p in range(3)]
        landed_refs, sc = refs[9:9 + ns], refs[9 + ns:]
        bufs = [sc[3 * p:3 * p + 3] for p in range(3)]
        res = [sc[9 + 3 * p:12 + 3 * p] for p in range(3)]
        tab128, tab4, sem_in, sem_out = sc[18:22]
        step = pl.program_id(0)
        if ns:
            start_chips, finish_chips = _chips_steps(sum_refs, landed_refs, *sc[22:])
            pl.when(step == 0)(start_chips)
        now = [_class_gather(hbm[p], bufs[p], sem_in.at[p], _lanes_of(step)) for p in range(3)]
        nxt = [_class_gather(hbm[p], bufs[p], sem_in.at[p], _lanes_of(step + 1)) for p in range(3)]

        @pl.when(step == 0)
        def _():
            for p in range(3):
                _start(now[p])
                for b in bufs[p]:
                    b[0:PAD, :] = jnp.zeros((PAD, LANES), F32)
            _fill_bias(tab128, 128, False)
            _fill_bias(tab4, 64, True)

        def prefetch(p):
            pl.when(step + 1 < n_grid)(lambda: _start(nxt[p]))

        for p in range(3):
            for b in res[p]:
                b[...] = jnp.zeros_like(b)
        lane = lax.broadcasted_iota(jnp.int32, (1, LANES), 1)

        def run(plan, plan_bufs, tab, dst):
            _, n_cls, qblk, nbc = plan
            partner = n_cls == 8
            bqdo, bkv, bld = plan_bufs
            rq, rk, rv = dst

            def block(g, carry):
                own, wins, mask = _block_rows(g, qblk, nbc, partner)
                qb, dob = _unpack_pair(bqdo[own, :])
                q2, do2 = _stack_heads(qb, lane), _stack_heads(dob, lane)
                kw, vw = _unpack_pair(_window(bkv, wins))
                ldv = bld[own, :]
                half = HEAD // 2
                lt2 = jnp.concatenate([ldv[:, 0:1], ldv[:, HEAD:HEAD + 1]], axis=0)
                dsum2 = jnp.concatenate([ldv[:, half:half + 1], ldv[:, HEAD + half:HEAD + half + 1]], axis=0)
                p = jnp.exp(_dot_nt(q2, kw) + tab[mask] - lt2)
                ds = (p * (_dot_nt(do2, vw) - dsum2)).astype(BF16)
                rq[own, :] = _unstack_heads(_dot(ds, kw), lane)
                dkw = _dot_tn(ds, q2)
                dvw = _dot_tn(p.astype(BF16), do2)
                n_w = WIN // len(wins)
                for i, w in enumerate(wins):
                    rk[w, :] += dkw[i * n_w:(i + 1) * n_w, :]
                    rv[w, :] += dvw[i * n_w:(i + 1) * n_w, :]
                return carry
            lax.fori_loop(0, n_cls * nbc, block, 0, unroll=ATTN_UNROLL)

        tabs = (tab128, tab4, tab128)
        for p in range(3):
            _wait(_whole_waits(bufs[p], sem_in.at[p]))
            run(ATTN_PLANS[p], bufs[p], tabs[p], res[p])
            prefetch(p)
            _start(_class_scatter(res[p], outs[p], sem_out.at[p], _lanes_of(step)))
        for p in range(3):
            _wait(_whole_waits(res[p], sem_out.at[p]))
        if ns:
            pl.when(step == n_grid - 1)(finish_chips)

    padded = pltpu.VMEM((PAD + S, LANES), F32)
    shapes = [jax.ShapeDtypeStruct(views[0][p].shape, F32) for p in range(3) for _ in range(3)]
    out = pl.pallas_call(
        body, grid=(n_grid,), name="attn_bwd",
        in_specs=[ANY] * (3 * n_in + ns), out_specs=[ANY] * (9 + ns),
        out_shape=shapes + _chips_shapes(chip_sums),
        scratch_shapes=[padded] * 18
        + [pltpu.VMEM((4, 256, WIN), F32), pltpu.VMEM((4, 128, WIN), F32),
           pltpu.SemaphoreType.DMA((3, n_in)), pltpu.SemaphoreType.DMA((3, 3))]
        + (_chips_scratch(ns) if ns else []),
        compiler_params=_cparams(56))(*flat, *chip_sums)
    return [o.reshape(S, AW) for o in out[:9]] + list(out[9:])


def _in_proj_bwd(dqkv, dbcu, dqx, cos, sins, w16, x, g, dx1):
    tq = TQ // 2

    def body(*refs):
        parts = refs[0:9]
        dbcu_ref, dqx_ref, c_ref, s_ref, w_ref, x_ref, g_ref, dx1_ref, dp_ref, gx_ref, dg_ref = refs[9:]

        @pl.when(pl.program_id(0) == 0)
        def _():
            dg_ref[...] = jnp.zeros_like(dg_ref)

        dq, dk, dv = (parts[i][...] + parts[3 + i][...] + parts[6 + i][...] for i in range(3))
        cos, sn = _all_heads(c_ref[...]), _all_heads(s_ref[...])
        dqr = dq * SCALE
        dkr = dk
        dp = jnp.concatenate([(dqr * cos + _rot_half(dqr * sn)).astype(BF16),
                              (dkr * cos + _rot_half(dkr * sn)).astype(BF16), dv.astype(BF16),
                              dbcu_ref[...], dqx_ref[...]], axis=1)
        dp_ref[...] = dp
        dh = _dot_nt(dp, w_ref[...])
        g = g_ref[...]
        _, n, r = _rms(x_ref[...], g)
        dx, dg = _rms_bwd(dh, n, r, g)
        gx_ref[...] = dx1_ref[...] + dx
        dg_ref[0:1, :] += dg

    def tile(w):
        return pl.BlockSpec((tq, w), lambda i: (i, 0))

    return pl.pallas_call(
        body, grid=(S // tq,), name="in_proj_bwd",
        in_specs=[tile(AW)] * 9 + [tile(3 * CW), tile(XW), tile(LANES), tile(LANES), _const((D, PW)),
                                   tile(D), _const((1, D)), tile(D)],
        out_specs=[tile(PW), tile(D), _acc((SUBLANES, D))],
        out_shape=[jax.ShapeDtypeStruct((S, PW), BF16), jax.ShapeDtypeStruct((S, D), F32),
                   jax.ShapeDtypeStruct((SUBLANES, D), F32)],
        compiler_params=_cparams(56))(*dqkv, dbcu, dqx, cos, sins, w16, x, g, dx1)


def _mem_bwd(mem, g_mem, wkv16, dkv):
    def body(m_ref, g_ref, w_ref, dkv_ref, dkv16_ref, dg_ref):
        dkv16 = dkv_ref[...].astype(BF16)
        dkv16_ref[...] = dkv16
        _, n, _ = _rms(m_ref[...], g_ref[...])
        dg = jnp.sum(_dot_nt(dkv16, w_ref[...]) * n, axis=0, keepdims=True)
        dg_ref[...] = jnp.broadcast_to(dg, dg_ref.shape)

    return pl.pallas_call(
        body, name="mem_bwd",
        out_shape=[jax.ShapeDtypeStruct((N_MEM, 2 * XW), BF16), jax.ShapeDtypeStruct((SUBLANES, D), F32)],
        compiler_params=pltpu.CompilerParams(vmem_limit_bytes=32 << 20))(mem, g_mem, wkv16, dkv)


N_CHIPS = N_DEV // 2


def _transpose_into(at, a_ref):
    kk = a_ref.shape[0]
    chunk = min(kk, 512)
    for c in range(kk // chunk):
        at[:, c * chunk:(c + 1) * chunk] = a_ref[c * chunk:(c + 1) * chunk, :].T


def _pair_scratch(block):
    return [pltpu.VMEM((N_CHIPS,) + block, BF16), pltpu.VMEM((N_CHIPS,) + block, BF16),
            pltpu.SemaphoreType.DMA((N_CHIPS,)), pltpu.SemaphoreType.DMA((N_CHIPS,))]


def _swap_with_sibling(p, stage, land, send, recv):
    x, y, c = lax.axis_index("x"), lax.axis_index("y"), lax.axis_index("c")
    return pltpu.make_async_remote_copy(src_ref=stage.at[p], dst_ref=land.at[p], send_sem=send.at[p],
                                        recv_sem=recv.at[p], device_id=(x, y, 1 - c), device_id_type=MESH)


def _wgrad_cols(place, a16, b16, blk, name, square_b=False, transpose_out=False, to_chips=False):
    kk, m = a16.shape
    aligned = blk % LANES == 0
    wide = blk if aligned else -(-(blk + LANES // 2) // LANES) * LANES
    block = (blk, m) if transpose_out else (m, blk)

    def chip_of(step, my_chip):
        return (my_chip + 1 + step) & (N_CHIPS - 1) if to_chips else step

    def body(pl_ref, a_ref, *refs):
        b_refs, refs = refs[:2 if aligned else 1], refs[2 if aligned else 1:]
        (cs_ref, own_ref), refs = refs[:2], refs[2:]
        if to_chips:
            landed, refs = refs[0], refs[1:]
        (at, stage, land, send, recv), refs = refs[:5], refs[5:]
        if not aligned:
            (win, wsem), refs = refs[:2], refs[2:]
        step = pl.program_id(0)
        x, y, c = lax.axis_index("x"), lax.axis_index("y"), lax.axis_index("c")
        my_chip = 2 * x + y
        p = chip_of(step, my_chip)

        def fetch(at_step, mine):
            j = 2 * chip_of(at_step, my_chip) + (c if mine else 1 - c)
            first = pl.multiple_of(((j * blk) >> 7) << 7, LANES)
            slot = 2 * (at_step & 1) + mine
            return pltpu.make_async_copy(b_refs[0].at[:, pl.ds(first, wide)], win.at[slot], wsem.at[slot])

        @pl.when(step == 0)
        def _():
            if not aligned:
                fetch(0, 0).start()
                fetch(0, 1).start()
            _transpose_into(at, a_ref)

        if not aligned:
            @pl.when(step + 1 < N_CHIPS)
            def _():
                fetch(step + 1, 0).start()
                fetch(step + 1, 1).start()

        def partial(mine):
            if aligned:
                b = b_refs[mine][...]
                if square_b:
                    b = b * b
                acc = _dot(at[...], b)
            else:
                fetch(step, mine).wait()
                acc = _dot(at[...], win[2 * (step & 1) + mine])
                odd = c if mine else 1 - c
                acc = pltpu.roll(acc, jnp.where(odd == 0, 0, wide - LANES // 2), 1)[:, 0:blk]
            return acc.T if transpose_out else acc

        stage[p] = partial(0).astype(BF16)
        swap = _swap_with_sibling(p, stage, land, send, recv)
        swap.start()
        mine = partial(1)
        swap.wait()
        total = mine + land[p].astype(F32)
        cs_ref[0] = total.astype(BF16)

        @pl.when(p == my_chip)
        def _():
            own_ref[...] = total

        if to_chips:
            stage2, send2, recv2 = refs
            flipped = jnp.bitwise_xor(p, my_chip)
            k = jnp.where(flipped == 2, 0, jnp.where(flipped == 1, 1, 2))

            def to_owner(src, k_, px, py):
                return pltpu.make_async_remote_copy(src_ref=src, dst_ref=landed.at[k_], send_sem=send2.at[k_],
                                                    recv_sem=recv2.at[k_], device_id=(px, py, c), device_id_type=MESH)

            @pl.when(p != my_chip)
            def _():
                stage2[p] = total.astype(BF16)
                to_owner(stage2.at[p], k, p >> 1, p & 1).start()

            @pl.when(step == N_CHIPS - 1)
            def _():
                for k_ in range(N_CHIPS - 1):
                    to_owner(stage2.at[0], k_, x, y).wait()

    def b_spec(mine):
        return pl.BlockSpec((kk, blk), lambda i, s: (0, 2 * chip_of(i, s[1]) + (s[0] if mine else 1 - s[0])))

    b_specs, b_args = ([b_spec(0), b_spec(1)], (b16, b16)) if aligned else ([ANY], (b16,))
    scratch = [pltpu.VMEM((m, kk), BF16)] + _pair_scratch(block)
    if not aligned:
        scratch += [pltpu.VMEM((4, kk, wide), BF16), pltpu.SemaphoreType.DMA((4,))]
    out_specs = [pl.BlockSpec((1,) + block, lambda i, s: (chip_of(i, s[1]), 0, 0)), pl.BlockSpec(block, lambda i, s: (0, 0))]
    out_shape = [jax.ShapeDtypeStruct((N_CHIPS,) + block, BF16), jax.ShapeDtypeStruct(block, F32)]
    if to_chips:
        out_specs.append(ANY)
        out_shape.append(jax.ShapeDtypeStruct((N_CHIPS - 1,) + block, BF16))
        scratch += [pltpu.VMEM((N_CHIPS,) + block, BF16), pltpu.SemaphoreType.DMA((N_CHIPS - 1,)),
                    pltpu.SemaphoreType.DMA((N_CHIPS - 1,))]
    return pl.pallas_call(
        body, name=name,
        grid_spec=pltpu.PrefetchScalarGridSpec(
            num_scalar_prefetch=1, grid=(N_CHIPS,),
            in_specs=[pl.BlockSpec((kk, m), lambda i, s: (0, 0), pipeline_mode=pl.Buffered(1))] + b_specs,
            out_specs=out_specs, scratch_shapes=scratch),
        out_shape=out_shape, compiler_params=_cparams(56))(place, a16, *b_args)


def _wgrad_rows(place, a16, b16, name):
    kk, m = a16.shape
    n = b16.shape[1]
    block = (m // N_DEV, n)

    def body(pl_ref, a_ref, b_ref, cs_ref, own_ref, at, acc, stage, land, send, recv):
        c = pl_ref[0]
        _transpose_into(at, a_ref)
        acc[...] = _dot(at[...], b_ref[...])

        def rows(owner):
            return pl.ds(pl.multiple_of(owner * block[0], block[0]), block[0])

        swaps = []
        for p in range(N_CHIPS):
            stage[p] = acc[rows(2 * p + 1 - c), :].astype(BF16)
            swaps.append(_swap_with_sibling(p, stage, land, send, recv))
            swaps[-1].start()
        for p in range(N_CHIPS):
            swaps[p].wait()
            total = acc[rows(2 * p + c), :] + land[p].astype(F32)
            cs_ref[p] = total.astype(BF16)

            @pl.when(p == pl_ref[1])
            def _():
                own_ref[...] = total

    vmem = pl.BlockSpec(memory_space=pltpu.VMEM)
    return pl.pallas_call(
        body, name=name,
        in_specs=[pl.BlockSpec(memory_space=pltpu.SMEM), vmem, vmem], out_specs=[vmem, vmem],
        out_shape=[jax.ShapeDtypeStruct((N_CHIPS,) + block, BF16), jax.ShapeDtypeStruct(block, F32)],
        scratch_shapes=[pltpu.VMEM((m, kk), BF16), pltpu.VMEM((m, n), F32)] + _pair_scratch(block),
        compiler_params=pltpu.CompilerParams(vmem_limit_bytes=56 << 20))(place, a16, b16)


def _adamw_math(w, g, m, v):
    m = ADAM_B1 * m + (1.0 - ADAM_B1) * g
    v = ADAM_B2 * v + (1.0 - ADAM_B2) * jnp.square(g)
    m_hat = m / (1.0 - ADAM_B1 ** ADAM_STEP)
    v_hat = v / (1.0 - ADAM_B2 ** ADAM_STEP)
    delta = -ADAM_LR * (m_hat / (jnp.sqrt(v_hat) + ADAM_EPS) + ADAM_WD * w)
    return delta, m, v


def _adamw_shards(updates, name, chip_sums=()):
    names, nu, ns = list(updates), len(updates), len(chip_sums)

    def body(*refs):
        ins, sum_refs = refs[:5 * nu], refs[5 * nu:5 * nu + ns]
        outs = refs[5 * nu + ns:9 * nu + ns]
        landed_refs, scratch = refs[9 * nu + ns:9 * nu + 2 * ns], refs[9 * nu + 2 * ns:]
        if ns:
            start_chips, finish_chips = _chips_steps(sum_refs, landed_refs, *scratch)
            start_chips()
        for i in range(nu):
            o_ref, r_ref, w_ref, m_ref, v_ref = ins[5 * i:5 * i + 5]
            g_out, d_out, m_out, v_out = outs[4 * i:4 * i + 4]
            g = o_ref[...] + r_ref[0].astype(F32) + r_ref[1].astype(F32) + r_ref[2].astype(F32)
            g_out[...] = g
            d_out[...], m_out[...], v_out[...] = _adamw_math(w_ref[...], g, m_ref[...], v_ref[...])
        if ns:
            finish_chips()

    vmem = pl.BlockSpec(memory_space=pltpu.VMEM)
    out = pl.pallas_call(
        body, name=name,
        in_specs=[vmem] * (5 * nu) + [ANY] * ns, out_specs=[vmem] * (4 * nu) + [ANY] * ns,
        out_shape=[jax.ShapeDtypeStruct(updates[n][2].shape, F32) for n in names for _ in range(4)]
        + _chips_shapes(chip_sums),
        scratch_shapes=_chips_scratch(ns) if ns else [],
        compiler_params=pltpu.CompilerParams(vmem_limit_bytes=56 << 20),
    )(*[a for n in names for a in updates[n]], *chip_sums)
    return {n: out[4 * i:4 * i + 4] for i, n in enumerate(names)}, list(out[4 * nu:])


def _place():
    x, y, c = lax.axis_index("x"), lax.axis_index("y"), lax.axis_index("c")
    chips = [(1 - x, y), (x, 1 - y), (1 - x, 1 - y)]
    return x, y, c, chips


def _gather_steps(ins, outs, send, recv, lsem):
    nt = len(ins)
    x, y, c, chips = _place()
    me, sib = (x, y, c), (x, y, 1 - c)

    def slot(t, px, py, pc):
        return outs[t].at[4 * px + 2 * py + pc]

    def copy(t, k, block, to, src=None):
        return pltpu.make_async_remote_copy(
            src_ref=slot(t, *block) if src is None else src, dst_ref=slot(t, *block),
            send_sem=send.at[t, k], recv_sem=recv.at[t, k], device_id=to, device_id_type=MESH)

    mine = [pltpu.make_async_copy(ins[t], slot(t, *me), lsem.at[t]) for t in range(nt)]
    first = []
    for t in range(nt):
        first.append(copy(t, 0, me, sib, src=ins[t]))
        first += [copy(t, 1 + j, me, (*chip, c), src=ins[t]) for j, chip in enumerate(chips)]

    def start():
        for cp in mine + first:
            cp.start()

    def finish():
        passed = []
        for j, chip in enumerate(chips):
            for t in range(nt):
                copy(t, 1 + j, (*chip, c), me).wait_recv()
                fwd = copy(t, 4 + j, (*chip, c), sib)
                fwd.start()
                passed.append(fwd)
        for t in range(nt):
            copy(t, 0, sib, me).wait_recv()
            for j, chip in enumerate(chips):
                copy(t, 4 + j, (*chip, 1 - c), me).wait_recv()
        for cp in first + passed:
            cp.wait_send()
        for cp in mine:
            cp.wait()

    return start, finish


def _gather_scratch(nt):
    return [pltpu.SemaphoreType.DMA((nt, 7)), pltpu.SemaphoreType.DMA((nt, 7)), pltpu.SemaphoreType.DMA((nt,))]


def _gathered_shapes(shards):
    return [jax.ShapeDtypeStruct((N_DEV,) + s.shape, s.dtype) for s in shards]


def _call_with_gather(body, n_grid, shards, *, name, in_specs, out_specs, out_shape, scratch_shapes, vmem_mb, args):
    ng, n_in, n_out = len(shards), len(in_specs), len(out_specs)

    def wrapped(*refs):
        ins, shard_refs = refs[:n_in], refs[n_in:n_in + ng]
        outs = refs[n_in + ng:n_in + ng + n_out]
        whole_refs = refs[n_in + ng + n_out:n_in + 2 * ng + n_out]
        scratch = refs[n_in + 2 * ng + n_out:]
        if ng:
            start, finish = _gather_steps(shard_refs, whole_refs, *scratch[len(scratch_shapes):])
            pl.when(pl.program_id(0) == 0)(start)
        body(*ins, *outs, *scratch[:len(scratch_shapes)])
        if ng:
            pl.when(pl.program_id(0) == n_grid - 1)(finish)

    return pl.pallas_call(
        wrapped, grid=(n_grid,), name=name,
        in_specs=list(in_specs) + [ANY] * ng, out_specs=list(out_specs) + [ANY] * ng,
        out_shape=list(out_shape) + _gathered_shapes(shards),
        scratch_shapes=list(scratch_shapes) + (_gather_scratch(ng) if ng else []),
        compiler_params=_cparams(vmem_mb))(*args, *shards)


def _chips_steps(ins, outs, send, recv):
    _, _, c, chips = _place()
    copies = [pltpu.make_async_remote_copy(
        src_ref=ins[t].at[2 * px + py], dst_ref=outs[t].at[j], send_sem=send.at[t, j], recv_sem=recv.at[t, j],
        device_id=(px, py, c), device_id_type=MESH) for t in range(len(ins)) for j, (px, py) in enumerate(chips)]

    def start():
        for cp in copies:
            cp.start()

    def finish():
        for cp in copies:
            cp.wait()

    return start, finish


def _chips_scratch(nt):
    return [pltpu.SemaphoreType.DMA((nt, 3)), pltpu.SemaphoreType.DMA((nt, 3))]


def _chips_shapes(cs16s):
    return [jax.ShapeDtypeStruct((3,) + g.shape[1:], g.dtype) for g in cs16s]


SMALL = (("g_pre_mix", 0, 0, D), ("g_mem", 1, 0, D), ("g_post_mix", 2, 0, D), ("g_attn_out", 3, 0, AW),
         ("g_conv_out", 3, AW, CW), ("g_xattn_out", 3, AW + CW, XW), ("g_post_mlp", 4, 0, D), ("g_pre_mlp", 5, 0, D))
CONV_ROW = 8
PACK_ROWS = 16


LOSS_ROW = 15


def _small_all_reduce(dg_in, dg_mem, dgs, dg_mlp, dcw, loss8):
    def body(acc_in, acc_mem, acc_mix, acc_mlp, acc_cw, acc_loss, tot_ref, pack, land, send, recv):
        x, y, c, _ = _place()
        me = 4 * x + 2 * y + c
        pack[...] = jnp.zeros_like(pack)
        pack[0:1, :] = acc_in[0:1, :]
        pack[1:2, :] = acc_mem[0:1, :]
        pack[2:4, :] = acc_mix[0:2, :]
        pack[4:6, :] = acc_mlp[0:2, :]
        pack[CONV_ROW:CONV_ROW + 3, 0:CW] = acc_cw[0:3, :]
        pack[LOSS_ROW:LOSS_ROW + 1, 0:LANES] = acc_loss[0:1, :]
        land[me] = pack[...]
        copies = []
        for k in range(1, N_DEV):
            kx, ky, kc = (k >> 2) & 1, (k >> 1) & 1, k & 1
            peer = (1 - x if kx else x, 1 - y if ky else y, 1 - c if kc else c)
            copies.append(pltpu.make_async_remote_copy(
                src_ref=pack, dst_ref=land.at[me], send_sem=send.at[k - 1], recv_sem=recv.at[k - 1],
                device_id=peer, device_id_type=MESH))
        for cp in copies:
            cp.start()
        for cp in copies:
            cp.wait()
        tot = land[0]
        for s in range(1, N_DEV):
            tot = tot + land[s]
        tot_ref[...] = tot

    return pl.pallas_call(
        body, name="small_all_reduce", out_shape=jax.ShapeDtypeStruct((PACK_ROWS, D), F32),
        scratch_shapes=[pltpu.VMEM((PACK_ROWS, D), F32), pltpu.VMEM((N_DEV, PACK_ROWS, D), F32),
                        pltpu.SemaphoreType.DMA((N_DEV - 1,)), pltpu.SemaphoreType.DMA((N_DEV - 1,))],
    )(dg_in, dg_mem, dgs, dg_mlp, dcw, loss8)


def _small_update(tot, me, params):
    flat = [a for n, _, _, _ in SMALL for a in params[n]] + list(params["conv_w"])
    n_par = len(SMALL) + 1
    tap_cols = CW // N_DEV

    def body(*refs):
        me_ref, tot_ref = refs[0:2]
        ins = refs[2:2 + 3 * n_par]
        loss_out = refs[2 + 3 * n_par]
        outs = refs[3 + 3 * n_par:]
        tot = tot_ref[...]
        loss_out[...] = jnp.broadcast_to(tot[LOSS_ROW:LOSS_ROW + 1, 0:LANES], loss_out.shape)

        def update(i, g):
            w_ref, m_ref, v_ref = ins[3 * i:3 * i + 3]
            g_out, d_out, m_out, v_out = outs[4 * i:4 * i + 4]
            g_out[...] = g
            d_out[...], m_out[...], v_out[...] = _adamw_math(w_ref[...], g, m_ref[...], v_ref[...])

        for i, (_, row, lane0, width) in enumerate(SMALL):
            update(i, tot[row:row + 1, lane0:lane0 + width])
        me = me_ref[0]
        taps = pltpu.roll(tot[CONV_ROW:CONV_ROW + SUBLANES, 0:CW], jnp.where(me == 0, 0, CW - me * tap_cols), 1)
        update(n_par - 1, taps[0:3, 0:tap_cols])

    shapes = [jax.ShapeDtypeStruct(params[n][0].shape, F32) for n, _, _, _ in SMALL] + [
        jax.ShapeDtypeStruct(params["conv_w"][0].shape, F32)]
    vmem = pl.BlockSpec(memory_space=pltpu.VMEM)
    loss, *out = pl.pallas_call(
        body, name="small_update",
        in_specs=[pl.BlockSpec(memory_space=pltpu.SMEM)] + [vmem] * (1 + 3 * n_par),
        out_shape=[jax.ShapeDtypeStruct((SUBLANES, LANES), F32)] + [s for s in shapes for _ in range(4)],
    )(me, tot, *flat)
    names = [n for n, _, _, _ in SMALL] + ["conv_w"]
    return loss[0, 0], {n: out[4 * i:4 * i + 4] for i, n in enumerate(names)}


def _local_step(x, mem, pos, gains, shards, tgt, place):
    half = HEAD // 2
    inv_freq = jnp.float32(ROPE_THETA) ** (-(jnp.arange(half, dtype=F32) * 2.0 / HEAD))
    invf = jnp.tile(inv_freq, LANES // half)[None, :]
    sgn = jnp.tile(jnp.concatenate([-jnp.ones((half,), F32), jnp.ones((half,), F32)]), LANES // HEAD)[None, :]
    cos, sins, win8 = _rope_table(pos.astype(F32).reshape(S, 1), invf, sgn, [shards["w_in"]])
    q, kvp, bcu, qx16, h16, win16, wout8, wkv8, conv8 = _in_proj(
        x, gains["g_pre_mix"], win8, cos, sins, [shards["w_out"], shards["w_mem_kv"], shards["conv_w"]])
    wout16, wkv16 = wout8.reshape(D, D), wkv8.reshape(D, 2 * XW)
    cw_full = conv8[:, 0:3, 0:CW // N_DEV].transpose(1, 0, 2).reshape(3, CW)
    cw8 = jnp.zeros((SUBLANES, CW), F32).at[0:3].set(cw_full)
    y_attn, ltot, wup8, wdn8 = _attn_fwd(q, kvp, [shards["w_up"], shards["w_down"]])
    wdn16 = wdn8.reshape(FF, D)
    memn16, kv16 = _mem_fwd(mem, gains["g_mem"], wkv16)
    ypre, y16, y2, x1 = _mix_out(y_attn, bcu, qx16, kv16, cw8, gains["g_attn_out"], gains["g_conv_out"],
                                 gains["g_xattn_out"], gains["g_post_mix"], wout16, x, [])
    a16, du16, h2_16, df2_16, dx1, loss8, dg_mlp = _mlp(x1, tgt, gains["g_pre_mlp"], gains["g_post_mlp"], wup8, wdn16)

    sums = {"w_up": _wgrad_cols(place, h2_16, du16, FF_BLK, "wgrad_up"),
            "w_down": _wgrad_cols(place, df2_16, a16, FF_BLK, "wgrad_down", square_b=True, transpose_out=True)}

    head_id = jnp.arange(AW, dtype=jnp.int32) // HEAD
    head_ones = (head_id[:, None] == head_id[None, :]).astype(BF16)
    dy2_16, qdo, ld, dbcu, dqx, dgs, dcw, dkv = _mix_out_bwd(
        dx1, y2, ypre, ltot, head_ones, q, bcu, qx16, kv16, cw8, gains["g_post_mix"], gains["g_attn_out"],
        gains["g_conv_out"], gains["g_xattn_out"], wout16)
    dkv16, dg_mem = _mem_bwd(mem, gains["g_mem"], wkv16, dkv)
    sums["w_mem_kv"] = _wgrad_rows(place, memn16, dkv16, "wgrad_mem_kv")
    sums["w_out"] = _wgrad_rows(place, y16, dy2_16, "wgrad_out")
    out = _attn_bwd(qdo, kvp, ld, [s[0] for s in sums.values()])
    dqkv, landed = out[:9], out[9:]
    reduced = {n: (s[1], landed[t]) for t, (n, s) in enumerate(sums.items())}
    dproj16, grad_x, dg_in = _in_proj_bwd(dqkv, dbcu, dqx, cos, sins, win16, x, gains["g_pre_mix"], dx1)

    _, in_own, in_landed = _wgrad_cols(place, h16, dproj16, PW // N_DEV, "wgrad_in", to_chips=True)
    reduced["w_in"] = (in_own, in_landed)
    return grad_x, reduced, (dg_in, dg_mem, dgs, dg_mlp, dcw, loss8)


BIG = ("w_in", "w_mem_kv", "w_out", "w_up", "w_down")
ORDER = ("g_pre_mix", "g_mem", "w_in", "w_mem_kv", "conv_w", "g_attn_out", "g_conv_out", "g_xattn_out", "w_out",
         "g_post_mix", "g_pre_mlp", "w_up", "w_down", "g_post_mlp")


def kernel(x, mem, positions, g_pre_mix, g_mem, w_in, w_mem_kv, conv_w, g_attn_out, g_conv_out, g_xattn_out, w_out, g_post_mix, g_pre_mlp, w_up, w_down, g_post_mlp, loss_target, m_g_pre_mix, m_g_mem, m_w_in, m_w_mem_kv, m_conv_w, m_g_attn_out, m_g_conv_out, m_g_xattn_out, m_w_out, m_g_post_mix, m_g_pre_mlp, m_w_up, m_w_down, m_g_post_mlp, v_g_pre_mix, v_g_mem, v_w_in, v_w_mem_kv, v_conv_w, v_g_attn_out, v_g_conv_out, v_g_xattn_out, v_w_out, v_g_post_mix, v_g_pre_mlp, v_w_up, v_w_down, v_g_post_mlp):
    w = dict(g_pre_mix=g_pre_mix, g_mem=g_mem, w_in=w_in, w_mem_kv=w_mem_kv, conv_w=conv_w, g_attn_out=g_attn_out,
             g_conv_out=g_conv_out, g_xattn_out=g_xattn_out, w_out=w_out, g_post_mix=g_post_mix, g_pre_mlp=g_pre_mlp,
             w_up=w_up, w_down=w_down, g_post_mlp=g_post_mlp)
    mo = dict(g_pre_mix=m_g_pre_mix, g_mem=m_g_mem, w_in=m_w_in, w_mem_kv=m_w_mem_kv, conv_w=m_conv_w,
              g_attn_out=m_g_attn_out, g_conv_out=m_g_conv_out, g_xattn_out=m_g_xattn_out, w_out=m_w_out,
              g_post_mix=m_g_post_mix, g_pre_mlp=m_g_pre_mlp, w_up=m_w_up, w_down=m_w_down, g_post_mlp=m_g_post_mlp)
    vo = dict(g_pre_mix=v_g_pre_mix, g_mem=v_g_mem, w_in=v_w_in, w_mem_kv=v_w_mem_kv, conv_w=v_conv_w,
              g_attn_out=v_g_attn_out, g_conv_out=v_g_conv_out, g_xattn_out=v_g_xattn_out, w_out=v_w_out,
              g_post_mix=v_g_post_mix, g_pre_mlp=v_g_pre_mlp, w_up=v_w_up, w_down=v_w_down, g_post_mlp=v_g_post_mlp)

    xi, yi, ci = lax.axis_index("x"), lax.axis_index("y"), lax.axis_index("c")
    me = 4 * xi + 2 * yi + ci
    place = jnp.stack([ci, 2 * xi + yi]).astype(jnp.int32)

    shards = {n: w[n][0].astype(BF16) for n in BIG}
    shards["conv_w"] = jnp.zeros((SUBLANES, LANES), F32).at[0:3, 0:CW // N_DEV].set(conv_w[0])

    gains = {n: w[n] for n, _, _, _ in SMALL}
    grad_x, reduced, small_acc = _local_step(x[0], mem[0], positions[0], gains, shards, loss_target[0], place)

    updated = {}
    for group in (("w_up", "w_down"), ("w_in", "w_out", "w_mem_kv")):
        updated.update(_adamw_shards({n: (*reduced[n], w[n][0], mo[n][0], vo[n][0]) for n in group},
                                     "adamw_" + "_".join(group))[0])
    grad, delta, new_m, new_v = {}, {}, {}, {}
    for n, (g, d_, m_, v_) in updated.items():
        grad[n], delta[n], new_m[n], new_v[n] = g[None], d_[None], m_[None], v_[None]

    params = {n: (w[n], mo[n], vo[n]) for n, _, _, _ in SMALL}
    params["conv_w"] = (w["conv_w"][0], mo["conv_w"][0], vo["conv_w"][0])
    loss, small = _small_update(_small_all_reduce(*small_acc), me.reshape(1).astype(jnp.int32), params)
    for n, (g, d_, m_, v_) in small.items():
        lead = (lambda a: a[None]) if n == "conv_w" else (lambda a: a)
        grad[n], delta[n], new_m[n], new_v[n] = lead(g), lead(d_), lead(m_), lead(v_)

    return (loss, grad_x[None], *[grad[n] for n in ORDER], *[delta[n] for n in ORDER],
            *[new_m[n] for n in ORDER], *[new_v[n] for n in ORDER])
```

```python
import functools

import numpy as np
import jax
import jax.numpy as jnp
from jax import lax
from jax.experimental import pallas as pl
from jax.experimental.pallas import tpu as pltpu

F32, BF16 = jnp.float32, jnp.bfloat16
MESH = pl.DeviceIdType.MESH
ANY = pl.BlockSpec(memory_space=pl.ANY)

N_DEV = 8
D = 1024
S = 4096
N_MEM = 256
HEAD = 64
AW, CW, XW = 512, 256, 256
PW = 3 * AW + 3 * CW + XW
FF = 4096
FF_BLK = FF // N_DEV
PATTERNS = ((128, 1), (512, 4), (2048, 16))
QB = 128
EPS = 1e-6
NEG = -1e30
SCALE = HEAD ** -0.5
ROPE_THETA = 10000.0
LANES = 128
SUBLANES = 8

ADAM_LR, ADAM_B1, ADAM_B2, ADAM_EPS, ADAM_WD, ADAM_STEP = 0.001, 0.9, 0.999, 1e-08, 0.01, 10

TQ = 512
TQ_MLP = 256
NT = S // TQ


def _cparams(vmem_mb, n_grid=1):
    return pltpu.CompilerParams(dimension_semantics=("arbitrary",) * n_grid, vmem_limit_bytes=vmem_mb << 20)


def _const(shape):
    nd = len(shape)
    return pl.BlockSpec(shape, lambda *_: (0,) * nd, pipeline_mode=pl.Buffered(1))


def _acc(shape):
    nd = len(shape)
    return pl.BlockSpec(shape, lambda *_: (0,) * nd)


def _dot(a, b):
    return jnp.dot(a, b, preferred_element_type=F32)


def _dot_nt(a, b):
    return lax.dot_general(a, b, (((1,), (1,)), ((), ())), preferred_element_type=F32)


def _dot_tn(a, b):
    return lax.dot_general(a, b, (((0,), (0,)), ((), ())), preferred_element_type=F32)


def _rms(x, g):
    r = lax.rsqrt(jnp.mean(x * x, axis=-1, keepdims=True) + EPS)
    n = x * r
    return n * g, n, r


def _rms_bwd(dy, n, r, g):
    dn = dy * g
    dx = r * (dn - n * jnp.mean(dn * n, axis=-1, keepdims=True))
    return dx, jnp.sum(dy * n, axis=0, keepdims=True)


def _rot_half(t):
    lane = lax.broadcasted_iota(jnp.int32, t.shape, 1)
    n = t.shape[1]
    return jnp.where((lane % HEAD) < HEAD // 2, pltpu.roll(t, n - HEAD // 2, 1), pltpu.roll(t, HEAD // 2, 1))


def _rope_table(pos_col, invf, sgn, shards):
    def body(p_ref, f_ref, s_ref, c_out, s_out):
        ang = p_ref[...] * f_ref[...]
        c_out[...] = jnp.cos(ang)
        s_out[...] = jnp.sin(ang) * s_ref[...]

    tile = pl.BlockSpec((TQ, LANES), lambda i: (i, 0))
    return _call_with_gather(
        body, NT, shards, name="rope_table",
        in_specs=[pl.BlockSpec((TQ, 1), lambda i: (i, 0)), _const((1, LANES)), _const((1, LANES))],
        out_specs=[tile, tile], out_shape=[jax.ShapeDtypeStruct((S, LANES), F32)] * 2,
        scratch_shapes=[], vmem_mb=32, args=(pos_col, invf, sgn))


def _all_heads(t):
    return jnp.tile(t, (1, AW // LANES))


def _mem_fwd(mem, g_mem, wkv16):
    def body(m_ref, g_ref, w_ref, n16_ref, kv_ref):
        y, _, _ = _rms(m_ref[...], g_ref[...])
        y16 = y.astype(BF16)
        n16_ref[...] = y16
        kv_ref[...] = _dot(y16, w_ref[...]).astype(BF16)

    return pl.pallas_call(
        body, name="mem_fwd",
        out_shape=[jax.ShapeDtypeStruct((N_MEM, D), BF16), jax.ShapeDtypeStruct((N_MEM, 2 * XW), BF16)],
        compiler_params=pltpu.CompilerParams(vmem_limit_bytes=32 << 20))(mem, g_mem, wkv16)


def _in_proj(x, g, w8, cos, sins, shards):
    blk = PW // N_DEV

    def body(x_ref, g_ref, w8_ref, c_ref, s_ref, q_ref, kv_ref, bcu_ref, qx_ref, h_ref, w_out, w_ref):
        @pl.when(pl.program_id(0) == 0)
        def _():
            for j in range(N_DEV):
                w_ref[:, j * blk:(j + 1) * blk] = w8_ref[j]
            w_out[...] = w_ref[...]

        y, _, _ = _rms(x_ref[...], g_ref[...])
        h = y.astype(BF16)
        h_ref[...] = h
        proj = _dot(h, w_ref[...])
        cos, sn = _all_heads(c_ref[...]), _all_heads(s_ref[...])
        q, k = proj[:, 0:AW], proj[:, AW:2 * AW]
        q_ref[...] = (q * cos + _rot_half(q) * sn) * SCALE
        kv_ref[...] = _pack_pair(k * cos + _rot_half(k) * sn, proj[:, 2 * AW:3 * AW])
        bcu_ref[...] = proj[:, 3 * AW:3 * AW + 3 * CW]
        qx_ref[...] = (proj[:, 3 * AW + 3 * CW:] * SCALE).astype(BF16)

    def tile(w):
        return pl.BlockSpec((TQ, w), lambda i: (i, 0))

    return _call_with_gather(
        body, NT, shards, name="in_proj",
        in_specs=[tile(D), _const((1, D)), _const((N_DEV, D, blk)), tile(LANES), tile(LANES)],
        out_specs=[tile(AW), tile(AW), tile(3 * CW), tile(XW), tile(D), _acc((D, PW))],
        out_shape=[jax.ShapeDtypeStruct((S, AW), F32)] * 2 + [
            jax.ShapeDtypeStruct((S, 3 * CW), F32), jax.ShapeDtypeStruct((S, XW), BF16),
            jax.ShapeDtypeStruct((S, D), BF16), jax.ShapeDtypeStruct((D, PW), BF16)],
        scratch_shapes=[pltpu.VMEM((D, PW), BF16)], vmem_mb=56, args=(x, g, w8, cos, sins))


ATTN_PLANS = (("p1", 1, 128, 32), ("p4", 8, 64, 8), ("p16", 16, 128, 2))
PAD = 128
WIN = 256


ATTN_UNROLL = 8


def _fill_bias(tab, qblk, partner):
    qi = lax.broadcasted_iota(jnp.int32, (2 * qblk, WIN), 0) & (qblk - 1)
    kj = lax.broadcasted_iota(jnp.int32, (2 * qblk, WIN), 1)
    piece = kj >> (qblk.bit_length() - 1)
    kk = kj & (qblk - 1)
    prev = (piece & 1) == 0
    of_partner = piece >= 2
    for first in (0, 1):
        for par in (0, 1):
            lo = jnp.where(prev, (qblk if first else qi) + jnp.where(of_partner, par, 0), 0)
            hi = jnp.where(prev, qblk, qi + jnp.where(of_partner, par - 1, 0))
            tab[2 * first + par] = jnp.where((kk >= lo) & (kk <= hi), 0.0, NEG).astype(F32)


def _block_rows(g, qblk, nbc, partner):
    own = pl.ds(pl.multiple_of(PAD + g * qblk, qblk), qblk)
    first = ((g & (nbc - 1)) == 0).astype(jnp.int32)
    if partner:
        gp = jnp.bitwise_xor(g, 4 * nbc)
        wins = (pl.ds(pl.multiple_of(PAD + (g - 1) * qblk, qblk), 2 * qblk),
                pl.ds(pl.multiple_of(PAD + (gp - 1) * qblk, qblk), 2 * qblk))
        return own, wins, 2 * first + ((g >> ((4 * nbc).bit_length() - 1)) & 1)
    return own, (pl.ds(pl.multiple_of(PAD + (g - 1) * qblk, qblk), 2 * qblk),), 2 * first


def _pack_pair(lo, hi):
    lo_bits = lax.bitcast_convert_type(lo.astype(BF16).astype(F32), jnp.uint32) >> 16
    hi_bits = lax.bitcast_convert_type(hi.astype(BF16).astype(F32), jnp.uint32) & jnp.uint32(0xFFFF0000)
    return lax.bitcast_convert_type(hi_bits | lo_bits, F32)


def _unpack_pair(c):
    bits = lax.bitcast_convert_type(c, jnp.uint32)
    lo = lax.bitcast_convert_type(bits << 16, F32).astype(BF16)
    hi = lax.bitcast_convert_type(bits & jnp.uint32(0xFFFF0000), F32).astype(BF16)
    return lo, hi


def _window(ref, wins):
    parts = [ref[w, :] for w in wins]
    return parts[0] if len(parts) == 1 else jnp.concatenate(parts, axis=0)


def _stack_heads(t, lane):
    zero = jnp.zeros_like(t)
    return jnp.concatenate([jnp.where(lane < HEAD, t, zero), jnp.where(lane >= HEAD, t, zero)], axis=0)


def _unstack_heads(t2, lane):
    half = t2.shape[0] // 2
    return jnp.where(lane < HEAD, t2[0:half, :], t2[half:, :])


def _lanes_of(step):
    return pl.ds(pl.multiple_of(step * LANES, LANES), LANES)


def _whole_wait(buf, sem):
    whole = buf.at[pl.ds(PAD, S), :]
    return pltpu.make_async_copy(whole, whole, sem)


def _whole_waits(bufs, sems):
    return [_whole_wait(buf, sems.at[i]) for i, buf in enumerate(bufs)]


def _class_gather(views, bufs, sems, lanes):
    copies = []
    for i, (view, buf) in enumerate(zip(views, bufs)):
        if view.ndim == 2:
            copies.append(pltpu.make_async_copy(view.at[:, lanes], buf.at[pl.ds(PAD, S), :], sems.at[i]))
        else:
            per, n_cls = view.shape[0], view.shape[1]
            copies += [pltpu.make_async_copy(view.at[:, c, lanes], buf.at[pl.ds(PAD + c * per, per), :], sems.at[i])
                       for c in range(n_cls)]
    return copies


def _class_scatter(bufs, dsts, sems, lanes=None):
    copies = []
    for i, (buf, dst) in enumerate(zip(bufs, dsts)):
        if dst.ndim == 2:
            copies.append(pltpu.make_async_copy(buf.at[pl.ds(PAD, S), :], dst.at[:, lanes], sems.at[i]))
            continue
        per, n_cls = dst.shape[0], dst.shape[1]
        for c in range(n_cls):
            to = dst.at[:, c, :] if lanes is None else dst.at[:, c, lanes]
            copies.append(pltpu.make_async_copy(buf.at[pl.ds(PAD + c * per, per), :], to, sems.at[i]))
    return copies


def _start(copies):
    for cp in copies:
        cp.start()


def _wait(waits):
    for w in waits:
        w.wait()


def _attn_fwd(q, kvp, shards=()):
    views = [[a] + [a.reshape(S // n, n, AW) for _, n, _, _ in ATTN_PLANS[1:]] for a in (q, kvp)]
    flat = [views[a][p] for p in range(3) for a in range(2)]
    ng = len(shards)
    n_grid = AW // LANES

    def body(*refs):
        hbm = [refs[2 * p:2 * p + 2] for p in range(3)]
        refs = refs[6:]
        shard_refs, refs = refs[:ng], refs[ng:]
        y_ref, lt_ref = refs[0:2]
        whole_refs, refs = refs[2:2 + ng], refs[2 + ng:]
        bufs = [refs[2 * p:2 * p + 2] for p in range(3)]
        oc4, lc4, oc16, lc16, o4n, l4n, o16n, l16n, tab128, tab4, sem_in, sem_out = refs[6:18]
        step = pl.program_id(0)
        if ng:
            start_gather, finish_gather = _gather_steps(shard_refs, whole_refs, *refs[18:])
            pl.when(step == 0)(start_gather)
        now = [_class_gather(hbm[p], bufs[p], sem_in.at[p], _lanes_of(step)) for p in range(3)]
        nxt = [_class_gather(hbm[p], bufs[p], sem_in.at[p], _lanes_of(step + 1)) for p in range(3)]

        @pl.when(step == 0)
        def _():
            for p in range(3):
                _start(now[p])
                for b in bufs[p]:
                    b[0:PAD, :] = jnp.zeros((PAD, LANES), F32)
            _fill_bias(tab128, 128, False)
            _fill_bias(tab4, 64, True)

        def prefetch(p):
            pl.when(step + 1 < n_grid)(lambda: _start(nxt[p]))

        lane = lax.broadcasted_iota(jnp.int32, (1, LANES), 1)
        ones = jnp.ones((WIN, LANES), BF16)

        def run(plan, bq, bkv, tab, o_dst, l_dst, dst_pad):
            _, n_cls, qblk, nbc = plan
            partner = n_cls == 8

            def block(g, carry):
                own, wins, mask = _block_rows(g, qblk, nbc, partner)
                q2 = _stack_heads(bq[own, :].astype(BF16), lane)
                kw, vwin = _unpack_pair(_window(bkv, wins))
                vw = jnp.concatenate([vwin, ones], axis=1)
                s = _dot_nt(q2, kw) + tab[mask]
                m = jnp.max(s, axis=1, keepdims=True)
                oe = _dot(jnp.exp(s - m).astype(BF16), vw)
                den = oe[:, LANES:]
                dst = pl.ds(pl.multiple_of(dst_pad + g * qblk, qblk), qblk)
                o_dst[dst, :] = _unstack_heads(oe[:, 0:LANES] / den, lane)
                l_dst[dst, :] = _unstack_heads(m + jnp.log(den), lane)
                return carry
            lax.fori_loop(0, n_cls * nbc, block, 0, unroll=ATTN_UNROLL)

        _wait(_whole_waits(bufs[0], sem_in.at[0]))
        run(ATTN_PLANS[0], *bufs[0], tab128, y_ref, lt_ref, 0)
        prefetch(0)
        _wait(_whole_waits(bufs[1], sem_in.at[1]))
        run(ATTN_PLANS[1], *bufs[1], tab4, oc4, lc4, PAD)
        prefetch(1)
        _start(_class_scatter((oc4, lc4), (o4n, l4n), sem_out.at[0]))
        _wait(_whole_waits(bufs[2], sem_in.at[2]))
        run(ATTN_PLANS[2], *bufs[2], tab128, oc16, lc16, PAD)
        prefetch(2)
        _start(_class_scatter((oc16, lc16), (o16n, l16n), sem_out.at[1]))
        _wait(_whole_waits((oc4, lc4), sem_out.at[0]) + _whole_waits((oc16, lc16), sem_out.at[1]))

        for t in range(S // TQ):
            rows = pl.ds(t * TQ, TQ)
            r4, r16 = pl.ds(t * (TQ // 8), TQ // 8), pl.ds(t * (TQ // 16), TQ // 16)
            l0, l1, l2 = lt_ref[rows, :], l4n[r4, :, :].reshape(TQ, LANES), l16n[r16, :, :].reshape(TQ, LANES)
            lm = jnp.maximum(jnp.maximum(l0, l1), l2)
            e0, e1, e2 = jnp.exp(l0 - lm), jnp.exp(l1 - lm), jnp.exp(l2 - lm)
            den = e0 + e1 + e2
            y_ref[rows, :] = (e0 * y_ref[rows, :] + e1 * o4n[r4, :, :].reshape(TQ, LANES)
                              + e2 * o16n[r16, :, :].reshape(TQ, LANES)) / den
            lt_ref[rows, :] = lm + jnp.log(den)

        if ng:
            pl.when(step == n_grid - 1)(finish_gather)

    col = pl.BlockSpec((S, LANES), lambda h: (0, h))
    padded = pltpu.VMEM((PAD + S, LANES), F32)
    return pl.pallas_call(
        body, grid=(n_grid,), name="attn_fwd",
        in_specs=[ANY] * (6 + ng), out_specs=[col, col] + [ANY] * ng,
        out_shape=[jax.ShapeDtypeStruct((S, AW), F32)] * 2 + _gathered_shapes(shards),
        scratch_shapes=[padded] * 10 + [
            pltpu.VMEM((S // 8, 8, LANES), F32), pltpu.VMEM((S // 8, 8, LANES), F32),
            pltpu.VMEM((S // 16, 16, LANES), F32), pltpu.VMEM((S // 16, 16, LANES), F32),
            pltpu.VMEM((4, 256, WIN), F32), pltpu.VMEM((4, 128, WIN), F32),
            pltpu.SemaphoreType.DMA((3, 2)), pltpu.SemaphoreType.DMA((2, 2))]
        + (_gather_scratch(ng) if ng else []),
        compiler_params=_cparams(56))(*flat, *shards)


def _conv_taps(z, zprev, row):
    z1 = jnp.where(row == 0, zprev[7:8, :], pltpu.roll(z, 1, 0))
    z2 = jnp.where(row == 0, zprev[6:7, :], jnp.where(row == 1, zprev[7:8, :], pltpu.roll(z, 2, 0)))
    return z1, z2


def _xattn_scores(qm, km):
    s = _dot_nt(qm, km)
    m = jnp.max(s, axis=1, keepdims=True)
    e = jnp.exp(s - m)
    return e, jnp.sum(e, axis=1, keepdims=True)


def _mix_out(y_attn, bcu, qx16, kv16, cw8, g_attn, g_conv, g_x, g_post, wout16, x, shards):
    def body(ya_ref, bcu_ref, halo_ref, qx_ref, kv_ref, cw_ref, ga_ref, gc_ref, gx_ref, gp_ref, w_ref, x_ref,
             ypre_ref, y16_ref, y2_ref, x1_ref):
        i = pl.program_id(0)
        bcu = bcu_ref[...]
        b, c, u = bcu[:, 0:CW], bcu[:, CW:2 * CW], bcu[:, 2 * CW:]
        z = c * u
        halo = halo_ref[...]
        zprev = jnp.where(i > 0, halo[:, CW:2 * CW] * halo[:, 2 * CW:], 0.0)
        row = lax.broadcasted_iota(jnp.int32, z.shape, 0)
        z1, z2 = _conv_taps(z, zprev, row)
        cw = cw_ref[...]
        y_conv = b * (z2 * cw[0:1, :] + z1 * cw[1:2, :] + z * cw[2:3, :])

        qx = qx_ref[...]
        kv = kv_ref[...]
        km, vm = kv[:, 0:XW], kv[:, XW:]
        lane = lax.broadcasted_iota(jnp.int32, qx.shape, 1)
        y_x = jnp.zeros(qx.shape, F32)
        for h in range(XW // HEAD):
            hm = (lane >= h * HEAD) & (lane < (h + 1) * HEAD)
            e, l = _xattn_scores(jnp.where(hm, qx, jnp.zeros_like(qx)), km)
            y_x = jnp.where(hm, _dot(e.astype(BF16), vm) / l, y_x)

        y_attn = ya_ref[...]
        ypre_ref[:, 0:AW] = y_attn
        ypre_ref[:, AW:AW + CW] = y_conv
        ypre_ref[:, AW + CW:] = y_x
        y = jnp.concatenate([_rms(y_attn, ga_ref[...])[0], _rms(y_conv, gc_ref[...])[0],
                             _rms(y_x, gx_ref[...])[0]], axis=1).astype(BF16)
        y16_ref[...] = y
        y2 = _dot(y, w_ref[...])
        y2_ref[...] = y2
        x1_ref[...] = x_ref[...] + _rms(y2, gp_ref[...])[0]

    def tile(w):
        return pl.BlockSpec((TQ, w), lambda i: (i, 0))

    halo = pl.BlockSpec((SUBLANES, 3 * CW), lambda i: (jnp.maximum(i * (TQ // SUBLANES) - 1, 0), 0))
    return _call_with_gather(
        body, NT, shards, name="mix_out",
        in_specs=[tile(AW), tile(3 * CW), halo, tile(XW), _const((N_MEM, 2 * XW)), _const((SUBLANES, CW)),
                  _const((1, AW)), _const((1, CW)), _const((1, XW)), _const((1, D)), _const((D, D)), tile(D)],
        out_specs=[tile(D), tile(D), tile(D), tile(D)],
        out_shape=[jax.ShapeDtypeStruct((S, D), F32), jax.ShapeDtypeStruct((S, D), BF16),
                   jax.ShapeDtypeStruct((S, D), F32), jax.ShapeDtypeStruct((S, D), F32)],
        scratch_shapes=[], vmem_mb=56,
        args=(y_attn, bcu, bcu, qx16, kv16, cw8, g_attn, g_conv, g_x, g_post, wout16, x))


def _mlp(x1, tgt, g_pre, g_post, wup8, wdn_halves):
    tq = TQ_MLP
    half = D // 2

    def body(x1_ref, t_ref, g1_ref, g2_ref, wu_ref, wda_ref, wdb_ref,
             a16_ref, du_ref, h2_ref, df2_ref, dx1_ref, loss_ref, dg_ref, a32):
        @pl.when(pl.program_id(0) == 0)
        def _():
            loss_ref[...] = jnp.zeros_like(loss_ref)
            dg_ref[...] = jnp.zeros_like(dg_ref)

        x1 = x1_ref[...]
        g1, g2 = g1_ref[...], g2_ref[...]
        y1, n1, r1 = _rms(x1, g1)
        h2 = y1.astype(BF16)
        h2_ref[...] = h2
        f2a = jnp.zeros((tq, half), F32)
        f2b = jnp.zeros((tq, half), F32)
        for j in range(N_DEV):
            cols = slice(j * FF_BLK, (j + 1) * FF_BLK)
            a = jnp.maximum(_dot(h2, wu_ref[j]), 0.0)
            a32[:, cols] = a
            a16_ref[:, cols] = a.astype(BF16)
            f = (a * a).astype(BF16)
            f2a = f2a + _dot(f, wda_ref[cols, :])
            f2b = f2b + _dot(f, wdb_ref[cols, :])
        f2 = jnp.concatenate([f2a, f2b], axis=1)
        y2, n2, r2 = _rms(f2, g2)
        e = x1 + y2 - t_ref[...]
        sq = jnp.sum(jnp.sum(e * e, axis=1, keepdims=True), axis=0, keepdims=True)
        loss_ref[...] += jnp.broadcast_to(sq * (0.5 / D), loss_ref.shape)
        dout = e * (1.0 / D)
        df2, dg2 = _rms_bwd(dout, n2, r2, g2)
        df2_16 = df2.astype(BF16)
        df2_ref[...] = df2_16
        dh2 = jnp.zeros((tq, D), F32)
        for j in range(N_DEV):
            cols = slice(j * FF_BLK, (j + 1) * FF_BLK)
            df = _dot_nt(df2_16[:, 0:half], wda_ref[cols, :]) + _dot_nt(df2_16[:, half:], wdb_ref[cols, :])
            du = (df * (2.0 * a32[:, cols])).astype(BF16)
            du_ref[:, cols] = du
            dh2 = dh2 + _dot_nt(du, wu_ref[j])
        dx, dg1 = _rms_bwd(dh2, n1, r1, g1)
        dx1_ref[...] = dout + dx
        dg_ref[0:1, :] += dg2
        dg_ref[1:2, :] += dg1

    def tile(w):
        return pl.BlockSpec((tq, w), lambda i: (i, 0))

    return pl.pallas_call(
        body, grid=(S // tq,), name="mlp",
        in_specs=[tile(D), tile(D), _const((1, D)), _const((1, D)), _const((N_DEV, D, FF_BLK)), _const((FF, half)), _const((FF, half))],
        out_specs=[tile(FF), tile(FF), tile(D), tile(D), tile(D), _acc((SUBLANES, LANES)), _acc((SUBLANES, D))],
        out_shape=[jax.ShapeDtypeStruct((S, FF), BF16), jax.ShapeDtypeStruct((S, FF), BF16),
                   jax.ShapeDtypeStruct((S, D), BF16), jax.ShapeDtypeStruct((S, D), BF16),
                   jax.ShapeDtypeStruct((S, D), F32), jax.ShapeDtypeStruct((SUBLANES, LANES), F32),
                   jax.ShapeDtypeStruct((SUBLANES, D), F32)],
        scratch_shapes=[pltpu.VMEM((tq, FF), F32)],
        compiler_params=_cparams(56))(x1, tgt, g_pre, g_post, wup8, *wdn_halves)


def _mix_out_bwd(dx1, y2, ypre, ltot, head_ones, q, bcu, qx16, kv16, cw8, g_post, g_attn, g_conv, g_x, wout16):
    def body(dx1_ref, y2_ref, ypre_ref, lt_ref, e_ref, q_ref, bcu_ref, halo_ref, qx_ref, kv_ref, cw_ref, gp_ref,
             ga_ref, gc_ref, gx_ref, w_ref, dy2_ref, qdo_ref, ld_ref, dbcu_ref, dqx_ref, dgs_ref, dcw_ref, dkv_ref,
             carry):
        i = pl.program_id(0)

        @pl.when(i == 0)
        def _():
            dgs_ref[...] = jnp.zeros_like(dgs_ref)
            dcw_ref[...] = jnp.zeros_like(dcw_ref)
            dkv_ref[...] = jnp.zeros_like(dkv_ref)
            carry[...] = jnp.zeros_like(carry)

        gp = gp_ref[...]
        _, n, r = _rms(y2_ref[...], gp)
        dy2, dgp = _rms_bwd(dx1_ref[...], n, r, gp)
        dy2_16 = dy2.astype(BF16)
        dy2_ref[...] = dy2_16
        dy = _dot_nt(dy2_16, w_ref[...])

        ypre = ypre_ref[...]
        ga, gc, gx = ga_ref[...], gc_ref[...], gx_ref[...]
        _, na, ra = _rms(ypre[:, 0:AW], ga)
        dya, dga = _rms_bwd(dy[:, 0:AW], na, ra, ga)
        _, nc, rc = _rms(ypre[:, AW:AW + CW], gc)
        dyc, dgc = _rms_bwd(dy[:, AW:AW + CW], nc, rc, gc)
        y_x = ypre[:, AW + CW:]
        _, nx, rx = _rms(y_x, gx)
        dyx, dgx = _rms_bwd(dy[:, AW + CW:], nx, rx, gx)
        qdo_ref[...] = _pack_pair(q_ref[...], dya)
        prod = dya * ypre[:, 0:AW]
        hi = prod.astype(BF16)
        lo = (prod - hi.astype(F32)).astype(BF16)
        head_sum = _dot(hi, e_ref[...]) + _dot(lo, e_ref[...])
        lane_a = lax.broadcasted_iota(jnp.int32, prod.shape, 1)
        ld_ref[...] = jnp.where((lane_a % HEAD) < HEAD // 2, lt_ref[...], head_sum)
        dgs_ref[0:1, :] += dgp
        dgs_ref[1:2, :] += jnp.concatenate([dga, dgc, dgx], axis=1)

        bcu = bcu_ref[...]
        b, c, u = bcu[:, 0:CW], bcu[:, CW:2 * CW], bcu[:, 2 * CW:]
        z = c * u
        halo = halo_ref[...]
        zprev = jnp.where(i < NT - 1, halo[:, CW:2 * CW] * halo[:, 2 * CW:], 0.0)
        row = lax.broadcasted_iota(jnp.int32, z.shape, 0)
        z1, z2 = _conv_taps(z, zprev, row)
        cw = cw_ref[...]
        conv = z2 * cw[0:1, :] + z1 * cw[1:2, :] + z * cw[2:3, :]
        dconv = dyc * b
        nxt = carry[...]
        dn1 = jnp.where(row == TQ - 1, nxt[0:1, :], pltpu.roll(dconv, TQ - 1, 0))
        dn2 = jnp.where(row == TQ - 1, nxt[1:2, :], jnp.where(row == TQ - 2, nxt[0:1, :], pltpu.roll(dconv, TQ - 2, 0)))
        carry[...] = dconv[0:SUBLANES, :]
        dz = dconv * cw[2:3, :] + dn1 * cw[1:2, :] + dn2 * cw[0:1, :]
        dbcu_ref[:, 0:CW] = (dyc * conv).astype(BF16)
        dbcu_ref[:, CW:2 * CW] = (dz * u).astype(BF16)
        dbcu_ref[:, 2 * CW:] = (dz * c).astype(BF16)
        dcw_ref[0:1, :] += jnp.sum(z2 * dconv, axis=0, keepdims=True)
        dcw_ref[1:2, :] += jnp.sum(z1 * dconv, axis=0, keepdims=True)
        dcw_ref[2:3, :] += jnp.sum(z * dconv, axis=0, keepdims=True)

        qx = qx_ref[...]
        kv = kv_ref[...]
        km, vm = kv[:, 0:XW], kv[:, XW:]
        lane = lax.broadcasted_iota(jnp.int32, qx.shape, 1)
        dqx = jnp.zeros(qx.shape, F32)
        dkm = jnp.zeros((N_MEM, XW), F32)
        dvm = jnp.zeros((N_MEM, XW), F32)
        for h in range(XW // HEAD):
            hm = (lane >= h * HEAD) & (lane < (h + 1) * HEAD)
            qm = jnp.where(hm, qx, jnp.zeros_like(qx))
            e, l = _xattn_scores(qm, km)
            p = e / l
            dom = jnp.where(hm, dyx, 0.0)
            do16 = dom.astype(BF16)
            dsum = jnp.sum(dom * y_x, axis=1, keepdims=True)
            ds = (p * (_dot_nt(do16, vm) - dsum)).astype(BF16)
            dqx = jnp.where(hm, _dot(ds, km), dqx)
            dkm = dkm + _dot_tn(ds, qm)
            dvm = dvm + _dot_tn(p.astype(BF16), do16)
        dqx_ref[...] = (dqx * SCALE).astype(BF16)
        dkv_ref[:, 0:XW] += dkm
        dkv_ref[:, XW:] += dvm

    def tile(w):
        return pl.BlockSpec((TQ, w), lambda i: (NT - 1 - i, 0))

    halo = pl.BlockSpec((SUBLANES, 3 * CW), lambda i: (jnp.maximum((NT - 1 - i) * (TQ // SUBLANES) - 1, 0), 0))
    return pl.pallas_call(
        body, grid=(NT,), name="mix_out_bwd",
        in_specs=[tile(D), tile(D), tile(D), tile(AW), _const((AW, AW)), tile(AW), tile(3 * CW), halo, tile(XW),
                  _const((N_MEM, 2 * XW)), _const((SUBLANES, CW)), _const((1, D)), _const((1, AW)), _const((1, CW)),
                  _const((1, XW)), _const((D, D))],
        out_specs=[tile(D), tile(AW), tile(AW), tile(3 * CW), tile(XW), _acc((SUBLANES, D)), _acc((SUBLANES, CW)),
                   _acc((N_MEM, 2 * XW))],
        out_shape=[jax.ShapeDtypeStruct((S, D), BF16), jax.ShapeDtypeStruct((S, AW), F32),
                   jax.ShapeDtypeStruct((S, AW), F32),
                   jax.ShapeDtypeStruct((S, 3 * CW), BF16), jax.ShapeDtypeStruct((S, XW), BF16),
                   jax.ShapeDtypeStruct((SUBLANES, D), F32), jax.ShapeDtypeStruct((SUBLANES, CW), F32),
                   jax.ShapeDtypeStruct((N_MEM, 2 * XW), F32)],
        scratch_shapes=[pltpu.VMEM((SUBLANES, CW), F32)],
        compiler_params=_cparams(56))(dx1, y2, ypre, ltot, head_ones, q, bcu, bcu, qx16, kv16, cw8, g_post, g_attn,
                                      g_conv, g_x, wout16)


def _attn_bwd(qdo, kvp, ld, chip_sums=()):
    n_in = 3
    views = [[a] + [a.reshape(S // n, n, AW) for _, n, _, _ in ATTN_PLANS[1:]] for a in (qdo, kvp, ld)]
    flat = [views[a][p] for p in range(3) for a in range(n_in)]
    ns = len(chip_sums)
    n_grid = AW // LANES

    def body(*refs):
        hbm = [refs[n_in * p:n_in * p + n_in] for p in range(3)]
        refs = refs[3 * n_in:]
        sum_refs, refs = refs[:ns], refs[ns:]
        outs = [refs[3 * p:3 * p + 3] for p in range(3)]
        landed_refs, sc = refs[9:9 + ns], refs[9 + ns:]
        bufs = [sc[3 * p:3 * p + 3] for p in range(3)]
        res = [sc[9 + 3 * p:12 + 3 * p] for p in range(3)]
        tab128, tab4, sem_in, sem_out = sc[18:22]
        step = pl.program_id(0)
        if ns:
            start_chips, finish_chips = _chips_steps(sum_refs, landed_refs, *sc[22:])
            pl.when(step == 0)(start_chips)
        now = [_class_gather(hbm[p], bufs[p], sem_in.at[p], _lanes_of(step)) for p in range(3)]
        nxt = [_class_gather(hbm[p], bufs[p], sem_in.at[p], _lanes_of(step + 1)) for p in range(3)]

        @pl.when(step == 0)
        def _():
            for p in range(3):
                _start(now[p])
                for b in bufs[p]:
                    b[0:PAD, :] = jnp.zeros((PAD, LANES), F32)
            _fill_bias(tab128, 128, False)
            _fill_bias(tab4, 64, True)

        def prefetch(p):
            pl.when(step + 1 < n_grid)(lambda: _start(nxt[p]))

        for p in range(3):
            for b in res[p]:
                b[...] = jnp.zeros_like(b)
        lane = lax.broadcasted_iota(jnp.int32, (1, LANES), 1)

        def run(plan, plan_bufs, tab, dst):
            _, n_cls, qblk, nbc = plan
            partner = n_cls == 8
            bqdo, bkv, bld = plan_bufs
            rq, rk, rv = dst

            def block(g, carry):
                own, wins, mask = _block_rows(g, qblk, nbc, partner)
                qb, dob = _unpack_pair(bqdo[own, :])
                q2, do2 = _stack_heads(qb, lane), _stack_heads(dob, lane)
                kw, vw = _unpack_pair(_window(bkv, wins))
                ldv = bld[own, :]
                half = HEAD // 2
                lt2 = jnp.concatenate([ldv[:, 0:1], ldv[:, HEAD:HEAD + 1]], axis=0)
                dsum2 = jnp.concatenate([ldv[:, half:half + 1], ldv[:, HEAD + half:HEAD + half + 1]], axis=0)
                p = jnp.exp(_dot_nt(q2, kw) + tab[mask] - lt2)
                ds = (p * (_dot_nt(do2, vw) - dsum2)).astype(BF16)
                rq[own, :] = _unstack_heads(_dot(ds, kw), lane)
                dkw = _dot_tn(ds, q2)
                dvw = _dot_tn(p.astype(BF16), do2)
                n_w = WIN // len(wins)
                for i, w in enumerate(wins):
                    rk[w, :] += dkw[i * n_w:(i + 1) * n_w, :]
                    rv[w, :] += dvw[i * n_w:(i + 1) * n_w, :]
                return carry
            lax.fori_loop(0, n_cls * nbc, block, 0, unroll=ATTN_UNROLL)

        tabs = (tab128, tab4, tab128)
        for p in range(3):
            _wait(_whole_waits(bufs[p], sem_in.at[p]))
            run(ATTN_PLANS[p], bufs[p], tabs[p], res[p])
            prefetch(p)
            _start(_class_scatter(res[p], outs[p], sem_out.at[p], _lanes_of(step)))
        for p in range(3):
            _wait(_whole_waits(res[p], sem_out.at[p]))
        if ns:
            pl.when(step == n_grid - 1)(finish_chips)

    padded = pltpu.VMEM((PAD + S, LANES), F32)
    shapes = [jax.ShapeDtypeStruct(views[0][p].shape, F32) for p in range(3) for _ in range(3)]
    out = pl.pallas_call(
        body, grid=(n_grid,), name="attn_bwd",
        in_specs=[ANY] * (3 * n_in + ns), out_specs=[ANY] * (9 + ns),
        out_shape=shapes + _chips_shapes(chip_sums),
        scratch_shapes=[padded] * 18
        + [pltpu.VMEM((4, 256, WIN), F32), pltpu.VMEM((4, 128, WIN), F32),
           pltpu.SemaphoreType.DMA((3, n_in)), pltpu.SemaphoreType.DMA((3, 3))]
        + (_chips_scratch(ns) if ns else []),
        compiler_params=_cparams(56))(*flat, *chip_sums)
    return [o.reshape(S, AW) for o in out[:9]] + list(out[9:])


def _in_proj_bwd(dqkv, dbcu, dqx, cos, sins, w16, x, g, dx1):
    tq = TQ // 2

    def body(*refs):
        parts = refs[0:9]
        dbcu_ref, dqx_ref, c_ref, s_ref, w_ref, x_ref, g_ref, dx1_ref, dp_ref, gx_ref, dg_ref = refs[9:]

        @pl.when(pl.program_id(0) == 0)
        def _():
            dg_ref[...] = jnp.zeros_like(dg_ref)

        dq, dk, dv = (parts[i][...] + parts[3 + i][...] + parts[6 + i][...] for i in range(3))
        cos, sn = _all_heads(c_ref[...]), _all_heads(s_ref[...])
        dqr = dq * SCALE
        dkr = dk
        dp = jnp.concatenate([(dqr * cos + _rot_half(dqr * sn)).astype(BF16),
                              (dkr * cos + _rot_half(dkr * sn)).astype(BF16), dv.astype(BF16),
                              dbcu_ref[...], dqx_ref[...]], axis=1)
        dp_ref[...] = dp
        dh = _dot_nt(dp, w_ref[...])
        g = g_ref[...]
        _, n, r = _rms(x_ref[...], g)
        dx, dg = _rms_bwd(dh, n, r, g)
        gx_ref[...] = dx1_ref[...] + dx
        dg_ref[0:1, :] += dg

    def tile(w):
        return pl.BlockSpec((tq, w), lambda i: (i, 0))

    return pl.pallas_call(
        body, grid=(S // tq,), name="in_proj_bwd",
        in_specs=[tile(AW)] * 9 + [tile(3 * CW), tile(XW), tile(LANES), tile(LANES), _const((D, PW)),
                                   tile(D), _const((1, D)), tile(D)],
        out_specs=[tile(PW), tile(D), _acc((SUBLANES, D))],
        out_shape=[jax.ShapeDtypeStruct((S, PW), BF16), jax.ShapeDtypeStruct((S, D), F32),
                   jax.ShapeDtypeStruct((SUBLANES, D), F32)],
        compiler_params=_cparams(56))(*dqkv, dbcu, dqx, cos, sins, w16, x, g, dx1)


def _mem_bwd(mem, g_mem, wkv16, dkv):
    def body(m_ref, g_ref, w_ref, dkv_ref, dkv16_ref, dg_ref):
        dkv16 = dkv_ref[...].astype(BF16)
        dkv16_ref[...] = dkv16
        _, n, _ = _rms(m_ref[...], g_ref[...])
        dg = jnp.sum(_dot_nt(dkv16, w_ref[...]) * n, axis=0, keepdims=True)
        dg_ref[...] = jnp.broadcast_to(dg, dg_ref.shape)

    return pl.pallas_call(
        body, name="mem_bwd",
        out_shape=[jax.ShapeDtypeStruct((N_MEM, 2 * XW), BF16), jax.ShapeDtypeStruct((SUBLANES, D), F32)],
        compiler_params=pltpu.CompilerParams(vmem_limit_bytes=32 << 20))(mem, g_mem, wkv16, dkv)


N_CHIPS = N_DEV // 2


def _transpose_into(at, a_ref):
    kk = a_ref.shape[0]
    chunk = min(kk, 512)
    for c in range(kk // chunk):
        at[:, c * chunk:(c + 1) * chunk] = a_ref[c * chunk:(c + 1) * chunk, :].T


def _pair_scratch(block):
    return [pltpu.VMEM((N_CHIPS,) + block, BF16), pltpu.VMEM((N_CHIPS,) + block, BF16),
            pltpu.SemaphoreType.DMA((N_CHIPS,)), pltpu.SemaphoreType.DMA((N_CHIPS,))]


def _swap_with_sibling(p, stage, land, send, recv):
    x, y, c = lax.axis_index("x"), lax.axis_index("y"), lax.axis_index("c")
    return pltpu.make_async_remote_copy(src_ref=stage.at[p], dst_ref=land.at[p], send_sem=send.at[p],
                                        recv_sem=recv.at[p], device_id=(x, y, 1 - c), device_id_type=MESH)


def _wgrad_cols(place, a16, b16, blk, name, square_b=False, transpose_out=False, to_chips=False):
    kk, m = a16.shape
    aligned = blk % LANES == 0
    wide = blk if aligned else -(-(blk + LANES // 2) // LANES) * LANES
    block = (blk, m) if transpose_out else (m, blk)

    def chip_of(step, my_chip):
        return (my_chip + 1 + step) & (N_CHIPS - 1) if to_chips else step

    def body(pl_ref, a_ref, *refs):
        b_refs, refs = refs[:2 if aligned else 1], refs[2 if aligned else 1:]
        (cs_ref, own_ref), refs = refs[:2], refs[2:]
        if to_chips:
            landed, refs = refs[0], refs[1:]
        (at, stage, land, send, recv), refs = refs[:5], refs[5:]
        if not aligned:
            (win, wsem), refs = refs[:2], refs[2:]
        step = pl.program_id(0)
        x, y, c = lax.axis_index("x"), lax.axis_index("y"), lax.axis_index("c")
        my_chip = 2 * x + y
        p = chip_of(step, my_chip)

        def fetch(at_step, mine):
            j = 2 * chip_of(at_step, my_chip) + (c if mine else 1 - c)
            first = pl.multiple_of(((j * blk) >> 7) << 7, LANES)
            slot = 2 * (at_step & 1) + mine
            return pltpu.make_async_copy(b_refs[0].at[:, pl.ds(first, wide)], win.at[slot], wsem.at[slot])

        @pl.when(step == 0)
        def _():
            if not aligned:
                fetch(0, 0).start()
                fetch(0, 1).start()
            _transpose_into(at, a_ref)

        if not aligned:
            @pl.when(step + 1 < N_CHIPS)
            def _():
                fetch(step + 1, 0).start()
                fetch(step + 1, 1).start()

        def partial(mine):
            if aligned:
                b = b_refs[mine][...]
                if square_b:
                    b = b * b
                acc = _dot(at[...], b)
            else:
                fetch(step, mine).wait()
                acc = _dot(at[...], win[2 * (step & 1) + mine])
                odd = c if mine else 1 - c
                acc = pltpu.roll(acc, jnp.where(odd == 0, 0, wide - LANES // 2), 1)[:, 0:blk]
            return acc.T if transpose_out else acc

        stage[p] = partial(0).astype(BF16)
        swap = _swap_with_sibling(p, stage, land, send, recv)
        swap.start()
        mine = partial(1)
        swap.wait()
        total = mine + land[p].astype(F32)
        cs_ref[0] = total.astype(BF16)

        @pl.when(p == my_chip)
        def _():
            own_ref[...] = total

        if to_chips:
            stage2, send2, recv2 = refs
            flipped = jnp.bitwise_xor(p, my_chip)
            k = jnp.where(flipped == 2, 0, jnp.where(flipped == 1, 1, 2))

            def to_owner(src, k_, px, py):
                return pltpu.make_async_remote_copy(src_ref=src, dst_ref=landed.at[k_], send_sem=send2.at[k_],
                                                    recv_sem=recv2.at[k_], device_id=(px, py, c), device_id_type=MESH)

            @pl.when(p != my_chip)
            def _():
                stage2[p] = total.astype(BF16)
                to_owner(stage2.at[p], k, p >> 1, p & 1).start()

            @pl.when(step == N_CHIPS - 1)
            def _():
                for k_ in range(N_CHIPS - 1):
                    to_owner(stage2.at[0], k_, x, y).wait()

    def b_spec(mine):
        return pl.BlockSpec((kk, blk), lambda i, s: (0, 2 * chip_of(i, s[1]) + (s[0] if mine else 1 - s[0])))

    b_specs, b_args = ([b_spec(0), b_spec(1)], (b16, b16)) if aligned else ([ANY], (b16,))
    scratch = [pltpu.VMEM((m, kk), BF16)] + _pair_scratch(block)
    if not aligned:
        scratch += [pltpu.VMEM((4, kk, wide), BF16), pltpu.SemaphoreType.DMA((4,))]
    out_specs = [pl.BlockSpec((1,) + block, lambda i, s: (chip_of(i, s[1]), 0, 0)), pl.BlockSpec(block, lambda i, s: (0, 0))]
    out_shape = [jax.ShapeDtypeStruct((N_CHIPS,) + block, BF16), jax.ShapeDtypeStruct(block, F32)]
    if to_chips:
        out_specs.append(ANY)
        out_shape.append(jax.ShapeDtypeStruct((N_CHIPS - 1,) + block, BF16))
        scratch += [pltpu.VMEM((N_CHIPS,) + block, BF16), pltpu.SemaphoreType.DMA((N_CHIPS - 1,)),
                    pltpu.SemaphoreType.DMA((N_CHIPS - 1,))]
    return pl.pallas_call(
        body, name=name,
        grid_spec=pltpu.PrefetchScalarGridSpec(
            num_scalar_prefetch=1, grid=(N_CHIPS,),
            in_specs=[pl.BlockSpec((kk, m), lambda i, s: (0, 0), pipeline_mode=pl.Buffered(1))] + b_specs,
            out_specs=out_specs, scratch_shapes=scratch),
        out_shape=out_shape, compiler_params=_cparams(56))(place, a16, *b_args)


def _wgrad_rows(place, a16, b16, name):
    kk, m = a16.shape
    n = b16.shape[1]
    block = (m // N_DEV, n)

    def body(pl_ref, a_ref, b_ref, cs_ref, own_ref, at, acc, stage, land, send, recv):
        c = pl_ref[0]
        _transpose_into(at, a_ref)
        acc[...] = _dot(at[...], b_ref[...])

        def rows(owner):
            return pl.ds(pl.multiple_of(owner * block[0], block[0]), block[0])

        swaps = []
        for p in range(N_CHIPS):
            stage[p] = acc[rows(2 * p + 1 - c), :].astype(BF16)
            swaps.append(_swap_with_sibling(p, stage, land, send, recv))
            swaps[-1].start()
        for p in range(N_CHIPS):
            swaps[p].wait()
            total = acc[rows(2 * p + c), :] + land[p].astype(F32)
            cs_ref[p] = total.astype(BF16)

            @pl.when(p == pl_ref[1])
            def _():
                own_ref[...] = total

    vmem = pl.BlockSpec(memory_space=pltpu.VMEM)
    return pl.pallas_call(
        body, name=name,
        in_specs=[pl.BlockSpec(memory_space=pltpu.SMEM), vmem, vmem], out_specs=[vmem, vmem],
        out_shape=[jax.ShapeDtypeStruct((N_CHIPS,) + block, BF16), jax.ShapeDtypeStruct(block, F32)],
        scratch_shapes=[pltpu.VMEM((m, kk), BF16), pltpu.VMEM((m, n), F32)] + _pair_scratch(block),
        compiler_params=pltpu.CompilerParams(vmem_limit_bytes=56 << 20))(place, a16, b16)


def _adamw_math(w, g, m, v):
    m = ADAM_B1 * m + (1.0 - ADAM_B1) * g
    v = ADAM_B2 * v + (1.0 - ADAM_B2) * jnp.square(g)
    m_hat = m / (1.0 - ADAM_B1 ** ADAM_STEP)
    v_hat = v / (1.0 - ADAM_B2 ** ADAM_STEP)
    delta = -ADAM_LR * (m_hat / (jnp.sqrt(v_hat) + ADAM_EPS) + ADAM_WD * w)
    return delta, m, v


def _adamw_shards(updates, name, chip_sums=()):
    names, nu, ns = list(updates), len(updates), len(chip_sums)

    def body(*refs):
        ins, sum_refs = refs[:5 * nu], refs[5 * nu:5 * nu + ns]
        outs = refs[5 * nu + ns:9 * nu + ns]
        landed_refs, scratch = refs[9 * nu + ns:9 * nu + 2 * ns], refs[9 * nu + 2 * ns:]
        if ns:
            start_chips, finish_chips = _chips_steps(sum_refs, landed_refs, *scratch)
            start_chips()
        for i in range(nu):
            o_ref, r_ref, w_ref, m_ref, v_ref = ins[5 * i:5 * i + 5]
            g_out, d_out, m_out, v_out = outs[4 * i:4 * i + 4]
            g = o_ref[...] + r_ref[0].astype(F32) + r_ref[1].astype(F32) + r_ref[2].astype(F32)
            g_out[...] = g
            d_out[...], m_out[...], v_out[...] = _adamw_math(w_ref[...], g, m_ref[...], v_ref[...])
        if ns:
            finish_chips()

    vmem = pl.BlockSpec(memory_space=pltpu.VMEM)
    out = pl.pallas_call(
        body, name=name,
        in_specs=[vmem] * (5 * nu) + [ANY] * ns, out_specs=[vmem] * (4 * nu) + [ANY] * ns,
        out_shape=[jax.ShapeDtypeStruct(updates[n][2].shape, F32) for n in names for _ in range(4)]
        + _chips_shapes(chip_sums),
        scratch_shapes=_chips_scratch(ns) if ns else [],
        compiler_params=pltpu.CompilerParams(vmem_limit_bytes=56 << 20),
    )(*[a for n in names for a in updates[n]], *chip_sums)
    return {n: out[4 * i:4 * i + 4] for i, n in enumerate(names)}, list(out[4 * nu:])


def _place():
    x, y, c = lax.axis_index("x"), lax.axis_index("y"), lax.axis_index("c")
    chips = [(1 - x, y), (x, 1 - y), (1 - x, 1 - y)]
    return x, y, c, chips


def _gather_steps(ins, outs, send, recv, lsem):
    nt = len(ins)
    x, y, c, chips = _place()
    me, sib = (x, y, c), (x, y, 1 - c)

    def slot(t, px, py, pc):
        return outs[t].at[4 * px + 2 * py + pc]

    def copy(t, k, block, to, src=None):
        return pltpu.make_async_remote_copy(
            src_ref=slot(t, *block) if src is None else src, dst_ref=slot(t, *block),
            send_sem=send.at[t, k], recv_sem=recv.at[t, k], device_id=to, device_id_type=MESH)

    mine = [pltpu.make_async_copy(ins[t], slot(t, *me), lsem.at[t]) for t in range(nt)]
    first = []
    for t in range(nt):
        first.append(copy(t, 0, me, sib, src=ins[t]))
        first += [copy(t, 1 + j, me, (*chip, c), src=ins[t]) for j, chip in enumerate(chips)]

    def start():
        for cp in mine + first:
            cp.start()

    def finish():
        passed = []
        for j, chip in enumerate(chips):
            for t in range(nt):
                copy(t, 1 + j, (*chip, c), me).wait_recv()
                fwd = copy(t, 4 + j, (*chip, c), sib)
                fwd.start()
                passed.append(fwd)
        for t in range(nt):
            copy(t, 0, sib, me).wait_recv()
            for j, chip in enumerate(chips):
                copy(t, 4 + j, (*chip, 1 - c), me).wait_recv()
        for cp in first + passed:
            cp.wait_send()
        for cp in mine:
            cp.wait()

    return start, finish


def _gather_scratch(nt):
    return [pltpu.SemaphoreType.DMA((nt, 7)), pltpu.SemaphoreType.DMA((nt, 7)), pltpu.SemaphoreType.DMA((nt,))]


def _gathered_shapes(shards):
    return [jax.ShapeDtypeStruct((N_DEV,) + s.shape, s.dtype) for s in shards]


def _call_with_gather(body, n_grid, shards, *, name, in_specs, out_specs, out_shape, scratch_shapes, vmem_mb, args):
    ng, n_in, n_out = len(shards), len(in_specs), len(out_specs)

    def wrapped(*refs):
        ins, shard_refs = refs[:n_in], refs[n_in:n_in + ng]
        outs = refs[n_in + ng:n_in + ng + n_out]
        whole_refs = refs[n_in + ng + n_out:n_in + 2 * ng + n_out]
        scratch = refs[n_in + 2 * ng + n_out:]
        if ng:
            start, finish = _gather_steps(shard_refs, whole_refs, *scratch[len(scratch_shapes):])
            pl.when(pl.program_id(0) == 0)(start)
        body(*ins, *outs, *scratch[:len(scratch_shapes)])
        if ng:
            pl.when(pl.program_id(0) == n_grid - 1)(finish)

    return pl.pallas_call(
        wrapped, grid=(n_grid,), name=name,
        in_specs=list(in_specs) + [ANY] * ng, out_specs=list(out_specs) + [ANY] * ng,
        out_shape=list(out_shape) + _gathered_shapes(shards),
        scratch_shapes=list(scratch_shapes) + (_gather_scratch(ng) if ng else []),
        compiler_params=_cparams(vmem_mb))(*args, *shards)


def _chips_steps(ins, outs, send, recv):
    _, _, c, chips = _place()
    copies = [pltpu.make_async_remote_copy(
        src_ref=ins[t].at[2 * px + py], dst_ref=outs[t].at[j], send_sem=send.at[t, j], recv_sem=recv.at[t, j],
        device_id=(px, py, c), device_id_type=MESH) for t in range(len(ins)) for j, (px, py) in enumerate(chips)]

    def start():
        for cp in copies:
            cp.start()

    def finish():
        for cp in copies:
            cp.wait()

    return start, finish


def _chips_scratch(nt):
    return [pltpu.SemaphoreType.DMA((nt, 3)), pltpu.SemaphoreType.DMA((nt, 3))]


def _chips_shapes(cs16s):
    return [jax.ShapeDtypeStruct((3,) + g.shape[1:], g.dtype) for g in cs16s]


SMALL = (("g_pre_mix", 0, 0, D), ("g_mem", 1, 0, D), ("g_post_mix", 2, 0, D), ("g_attn_out", 3, 0, AW),
         ("g_conv_out", 3, AW, CW), ("g_xattn_out", 3, AW + CW, XW), ("g_post_mlp", 4, 0, D), ("g_pre_mlp", 5, 0, D))
CONV_ROW = 8
PACK_ROWS = 16


LOSS_ROW = 15


def _small_all_reduce(dg_in, dg_mem, dgs, dg_mlp, dcw, loss8):
    def body(acc_in, acc_mem, acc_mix, acc_mlp, acc_cw, acc_loss, tot_ref, pack, land, send, recv):
        x, y, c, _ = _place()
        me = 4 * x + 2 * y + c
        pack[...] = jnp.zeros_like(pack)
        pack[0:1, :] = acc_in[0:1, :]
        pack[1:2, :] = acc_mem[0:1, :]
        pack[2:4, :] = acc_mix[0:2, :]
        pack[4:6, :] = acc_mlp[0:2, :]
        pack[CONV_ROW:CONV_ROW + 3, 0:CW] = acc_cw[0:3, :]
        pack[LOSS_ROW:LOSS_ROW + 1, 0:LANES] = acc_loss[0:1, :]
        land[me] = pack[...]
        copies = []
        for k in range(1, N_DEV):
            kx, ky, kc = (k >> 2) & 1, (k >> 1) & 1, k & 1
            peer = (1 - x if kx else x, 1 - y if ky else y, 1 - c if kc else c)
            copies.append(pltpu.make_async_remote_copy(
                src_ref=pack, dst_ref=land.at[me], send_sem=send.at[k - 1], recv_sem=recv.at[k - 1],
                device_id=peer, device_id_type=MESH))
        for cp in copies:
            cp.start()
        for cp in copies:
            cp.wait()
        tot = land[0]
        for s in range(1, N_DEV):
            tot = tot + land[s]
        tot_ref[...] = tot

    return pl.pallas_call(
        body, name="small_all_reduce", out_shape=jax.ShapeDtypeStruct((PACK_ROWS, D), F32),
        scratch_shapes=[pltpu.VMEM((PACK_ROWS, D), F32), pltpu.VMEM((N_DEV, PACK_ROWS, D), F32),
                        pltpu.SemaphoreType.DMA((N_DEV - 1,)), pltpu.SemaphoreType.DMA((N_DEV - 1,))],
    )(dg_in, dg_mem, dgs, dg_mlp, dcw, loss8)


def _small_update(tot, me, params):
    flat = [a for n, _, _, _ in SMALL for a in params[n]] + list(params["conv_w"])
    n_par = len(SMALL) + 1
    tap_cols = CW // N_DEV

    def body(*refs):
        me_ref, tot_ref = refs[0:2]
        ins = refs[2:2 + 3 * n_par]
        loss_out = refs[2 + 3 * n_par]
        outs = refs[3 + 3 * n_par:]
        tot = tot_ref[...]
        loss_out[...] = jnp.broadcast_to(tot[LOSS_ROW:LOSS_ROW + 1, 0:LANES], loss_out.shape)

        def update(i, g):
            w_ref, m_ref, v_ref = ins[3 * i:3 * i + 3]
            g_out, d_out, m_out, v_out = outs[4 * i:4 * i + 4]
            g_out[...] = g
            d_out[...], m_out[...], v_out[...] = _adamw_math(w_ref[...], g, m_ref[...], v_ref[...])

        for i, (_, row, lane0, width) in enumerate(SMALL):
            update(i, tot[row:row + 1, lane0:lane0 + width])
        me = me_ref[0]
        taps = pltpu.roll(tot[CONV_ROW:CONV_ROW + SUBLANES, 0:CW], jnp.where(me == 0, 0, CW - me * tap_cols), 1)
        update(n_par - 1, taps[0:3, 0:tap_cols])

    shapes = [jax.ShapeDtypeStruct(params[n][0].shape, F32) for n, _, _, _ in SMALL] + [
        jax.ShapeDtypeStruct(params["conv_w"][0].shape, F32)]
    vmem = pl.BlockSpec(memory_space=pltpu.VMEM)
    loss, *out = pl.pallas_call(
        body, name="small_update",
        in_specs=[pl.BlockSpec(memory_space=pltpu.SMEM)] + [vmem] * (1 + 3 * n_par),
        out_shape=[jax.ShapeDtypeStruct((SUBLANES, LANES), F32)] + [s for s in shapes for _ in range(4)],
    )(me, tot, *flat)
    names = [n for n, _, _, _ in SMALL] + ["conv_w"]
    return loss[0, 0], {n: out[4 * i:4 * i + 4] for i, n in enumerate(names)}


def _local_step(x, mem, pos, gains, shards, tgt, place):
    half = HEAD // 2
    inv_freq = jnp.float32(ROPE_THETA) ** (-(jnp.arange(half, dtype=F32) * 2.0 / HEAD))
    invf = jnp.tile(inv_freq, LANES // half)[None, :]
    sgn = jnp.tile(jnp.concatenate([-jnp.ones((half,), F32), jnp.ones((half,), F32)]), LANES // HEAD)[None, :]
    cos, sins, win8 = _rope_table(pos.astype(F32).reshape(S, 1), invf, sgn, [shards["w_in"]])
    q, kvp, bcu, qx16, h16, win16, wout8, wkv8, conv8 = _in_proj(
        x, gains["g_pre_mix"], win8, cos, sins, [shards["w_out"], shards["w_mem_kv"], shards["conv_w"]])
    wout16, wkv16 = wout8.reshape(D, D), wkv8.reshape(D, 2 * XW)
    cw_full = conv8[:, 0:3, 0:CW // N_DEV].transpose(1, 0, 2).reshape(3, CW)
    cw8 = jnp.zeros((SUBLANES, CW), F32).at[0:3].set(cw_full)
    wdn_left, wdn_right = shards["w_down"][:, 0:D // 2], shards["w_down"][:, D // 2:]
    y_attn, ltot, wup8, wdn8_left = _attn_fwd(q, kvp, [shards["w_up"], wdn_left])
    memn16, kv16 = _mem_fwd(mem, gains["g_mem"], wkv16)
    ypre, y16, y2, x1, wdn8_right = _mix_out(
        y_attn, bcu, qx16, kv16, cw8, gains["g_attn_out"], gains["g_conv_out"], gains["g_xattn_out"],
        gains["g_post_mix"], wout16, x, [wdn_right])
    wdn_halves = (wdn8_left.reshape(FF, D // 2), wdn8_right.reshape(FF, D // 2))
    a16, du16, h2_16, df2_16, dx1, loss8, dg_mlp = _mlp(
        x1, tgt, gains["g_pre_mlp"], gains["g_post_mlp"], wup8, wdn_halves)

    sums = {"w_up": _wgrad_cols(place, h2_16, du16, FF_BLK, "wgrad_up"),
            "w_down": _wgrad_cols(place, df2_16, a16, FF_BLK, "wgrad_down", square_b=True, transpose_out=True)}

    head_id = jnp.arange(AW, dtype=jnp.int32) // HEAD
    head_ones = (head_id[:, None] == head_id[None, :]).astype(BF16)
    dy2_16, qdo, ld, dbcu, dqx, dgs, dcw, dkv = _mix_out_bwd(
        dx1, y2, ypre, ltot, head_ones, q, bcu, qx16, kv16, cw8, gains["g_post_mix"], gains["g_attn_out"],
        gains["g_conv_out"], gains["g_xattn_out"], wout16)
    dkv16, dg_mem = _mem_bwd(mem, gains["g_mem"], wkv16, dkv)
    sums["w_mem_kv"] = _wgrad_rows(place, memn16, dkv16, "wgrad_mem_kv")
    sums["w_out"] = _wgrad_rows(place, y16, dy2_16, "wgrad_out")
    out = _attn_bwd(qdo, kvp, ld, [s[0] for s in sums.values()])
    dqkv, landed = out[:9], out[9:]
    reduced = {n: (s[1], landed[t]) for t, (n, s) in enumerate(sums.items())}
    dproj16, grad_x, dg_in = _in_proj_bwd(dqkv, dbcu, dqx, cos, sins, win16, x, gains["g_pre_mix"], dx1)

    _, in_own, in_landed = _wgrad_cols(place, h16, dproj16, PW // N_DEV, "wgrad_in", to_chips=True)
    reduced["w_in"] = (in_own, in_landed)
    return grad_x, reduced, (dg_in, dg_mem, dgs, dg_mlp, dcw, loss8)


BIG = ("w_in", "w_mem_kv", "w_out", "w_up", "w_down")
ORDER = ("g_pre_mix", "g_mem", "w_in", "w_mem_kv", "conv_w", "g_attn_out", "g_conv_out", "g_xattn_out", "w_out",
         "g_post_mix", "g_pre_mlp", "w_up", "w_down", "g_post_mlp")


def kernel(x, mem, positions, g_pre_mix, g_mem, w_in, w_mem_kv, conv_w, g_attn_out, g_conv_out, g_xattn_out, w_out, g_post_mix, g_pre_mlp, w_up, w_down, g_post_mlp, loss_target, m_g_pre_mix, m_g_mem, m_w_in, m_w_mem_kv, m_conv_w, m_g_attn_out, m_g_conv_out, m_g_xattn_out, m_w_out, m_g_post_mix, m_g_pre_mlp, m_w_up, m_w_down, m_g_post_mlp, v_g_pre_mix, v_g_mem, v_w_in, v_w_mem_kv, v_conv_w, v_g_attn_out, v_g_conv_out, v_g_xattn_out, v_w_out, v_g_post_mix, v_g_pre_mlp, v_w_up, v_w_down, v_g_post_mlp):
    w = dict(g_pre_mix=g_pre_mix, g_mem=g_mem, w_in=w_in, w_mem_kv=w_mem_kv, conv_w=conv_w, g_attn_out=g_attn_out,
             g_conv_out=g_conv_out, g_xattn_out=g_xattn_out, w_out=w_out, g_post_mix=g_post_mix, g_pre_mlp=g_pre_mlp,
             w_up=w_up, w_down=w_down, g_post_mlp=g_post_mlp)
    mo = dict(g_pre_mix=m_g_pre_mix, g_mem=m_g_mem, w_in=m_w_in, w_mem_kv=m_w_mem_kv, conv_w=m_conv_w,
              g_attn_out=m_g_attn_out, g_conv_out=m_g_conv_out, g_xattn_out=m_g_xattn_out, w_out=m_w_out,
              g_post_mix=m_g_post_mix, g_pre_mlp=m_g_pre_mlp, w_up=m_w_up, w_down=m_w_down, g_post_mlp=m_g_post_mlp)
    vo = dict(g_pre_mix=v_g_pre_mix, g_mem=v_g_mem, w_in=v_w_in, w_mem_kv=v_w_mem_kv, conv_w=v_conv_w,
              g_attn_out=v_g_attn_out, g_conv_out=v_g_conv_out, g_xattn_out=v_g_xattn_out, w_out=v_w_out,
              g_post_mix=v_g_post_mix, g_pre_mlp=v_g_pre_mlp, w_up=v_w_up, w_down=v_w_down, g_post_mlp=v_g_post_mlp)

    xi, yi, ci = lax.axis_index("x"), lax.axis_index("y"), lax.axis_index("c")
    me = 4 * xi + 2 * yi + ci
    place = jnp.stack([ci, 2 * xi + yi]).astype(jnp.int32)

    shards = {n: w[n][0].astype(BF16) for n in BIG}
    shards["conv_w"] = jnp.zeros((SUBLANES, LANES), F32).at[0:3, 0:CW // N_DEV].set(conv_w[0])

    gains = {n: w[n] for n, _, _, _ in SMALL}
    grad_x, reduced, small_acc = _local_step(x[0], mem[0], positions[0], gains, shards, loss_target[0], place)

    updated = {}
    for group in (("w_up", "w_down"), ("w_in", "w_out", "w_mem_kv")):
        updated.update(_adamw_shards({n: (*reduced[n], w[n][0], mo[n][0], vo[n][0]) for n in group},
                                     "adamw_" + "_".join(group))[0])
    grad, delta, new_m, new_v = {}, {}, {}, {}
    for n, (g, d_, m_, v_) in updated.items():
        grad[n], delta[n], new_m[n], new_v[n] = g[None], d_[None], m_[None], v_[None]

    params = {n: (w[n], mo[n], vo[n]) for n, _, _, _ in SMALL}
    params["conv_w"] = (w["conv_w"][0], mo["conv_w"][0], vo["conv_w"][0])
    loss, small = _small_update(_small_all_reduce(*small_acc), me.reshape(1).astype(jnp.int32), params)
    for n, (g, d_, m_, v_) in small.items():
        lead = (lambda a: a[None]) if n == "conv_w" else (lambda a: a)
        grad[n], delta[n], new_m[n], new_v[n] = lead(g), lead(d_), lead(m_), lead(v_)

    return (loss, grad_x[None], *[grad[n] for n in ORDER], *[delta[n] for n in ORDER],
            *[new_m[n] for n in ORDER], *[new_v[n] for n in ORDER])
```

```python
import functools

import numpy as np
import jax
import jax.numpy as jnp
from jax import lax
from jax.experimental import pallas as pl
from jax.experimental.pallas import tpu as pltpu

F32, BF16 = jnp.float32, jnp.bfloat16
MESH = pl.DeviceIdType.MESH
ANY = pl.BlockSpec(memory_space=pl.ANY)

N_DEV = 8
D = 1024
S = 4096
N_MEM = 256
HEAD = 64
AW, CW, XW = 512, 256, 256
PW = 3 * AW + 3 * CW + XW
FF = 4096
FF_BLK = FF // N_DEV
PATTERNS = ((128, 1), (512, 4), (2048, 16))
QB = 128
EPS = 1e-6
NEG = -1e30
SCALE = HEAD ** -0.5
ROPE_THETA = 10000.0
LANES = 128
SUBLANES = 8

ADAM_LR, ADAM_B1, ADAM_B2, ADAM_EPS, ADAM_WD, ADAM_STEP = 0.001, 0.9, 0.999, 1e-08, 0.01, 10

TQ = 512
TQ_MLP = 256
NT = S // TQ


def _cparams(vmem_mb, n_grid=1):
    return pltpu.CompilerParams(dimension_semantics=("arbitrary",) * n_grid, vmem_limit_bytes=vmem_mb << 20)


def _const(shape):
    nd = len(shape)
    return pl.BlockSpec(shape, lambda *_: (0,) * nd, pipeline_mode=pl.Buffered(1))


def _acc(shape):
    nd = len(shape)
    return pl.BlockSpec(shape, lambda *_: (0,) * nd)


def _dot(a, b):
    return jnp.dot(a, b, preferred_element_type=F32)


def _dot_nt(a, b):
    return lax.dot_general(a, b, (((1,), (1,)), ((), ())), preferred_element_type=F32)


def _dot_tn(a, b):
    return lax.dot_general(a, b, (((0,), (0,)), ((), ())), preferred_element_type=F32)


def _rms(x, g):
    r = lax.rsqrt(jnp.mean(x * x, axis=-1, keepdims=True) + EPS)
    n = x * r
    return n * g, n, r


def _rms_bwd(dy, n, r, g):
    dn = dy * g
    dx = r * (dn - n * jnp.mean(dn * n, axis=-1, keepdims=True))
    return dx, jnp.sum(dy * n, axis=0, keepdims=True)


def _rot_half(t):
    lane = lax.broadcasted_iota(jnp.int32, t.shape, 1)
    n = t.shape[1]
    return jnp.where((lane % HEAD) < HEAD // 2, pltpu.roll(t, n - HEAD // 2, 1), pltpu.roll(t, HEAD // 2, 1))


def _rope_table(pos_col, invf, sgn, shards):
    def body(p_ref, f_ref, s_ref, c_out, s_out):
        ang = p_ref[...] * f_ref[...]
        c_out[...] = jnp.cos(ang)
        s_out[...] = jnp.sin(ang) * s_ref[...]

    tile = pl.BlockSpec((TQ, LANES), lambda i: (i, 0))
    return _call_with_gather(
        body, NT, shards, name="rope_table",
        in_specs=[pl.BlockSpec((TQ, 1), lambda i: (i, 0)), _const((1, LANES)), _const((1, LANES))],
        out_specs=[tile, tile], out_shape=[jax.ShapeDtypeStruct((S, LANES), F32)] * 2,
        scratch_shapes=[], vmem_mb=32, args=(pos_col, invf, sgn))


def _all_heads(t):
    return jnp.tile(t, (1, AW // LANES))


def _mem_fwd(mem, g_mem, wkv16):
    def body(m_ref, g_ref, w_ref, n16_ref, kv_ref):
        y, _, _ = _rms(m_ref[...], g_ref[...])
        y16 = y.astype(BF16)
        n16_ref[...] = y16
        kv_ref[...] = _dot(y16, w_ref[...]).astype(BF16)

    return pl.pallas_call(
        body, name="mem_fwd",
        out_shape=[jax.ShapeDtypeStruct((N_MEM, D), BF16), jax.ShapeDtypeStruct((N_MEM, 2 * XW), BF16)],
        compiler_params=pltpu.CompilerParams(vmem_limit_bytes=32 << 20))(mem, g_mem, wkv16)


def _in_proj(x, g, w8, cos, sins, shards):
    blk = PW // N_DEV

    def body(x_ref, g_ref, w8_ref, c_ref, s_ref, q_ref, kv_ref, bcu_ref, qx_ref, h_ref, w_out, w_ref):
        @pl.when(pl.program_id(0) == 0)
        def _():
            for j in range(N_DEV):
                w_ref[:, j * blk:(j + 1) * blk] = w8_ref[j]
            w_out[...] = w_ref[...]

        y, _, _ = _rms(x_ref[...], g_ref[...])
        h = y.astype(BF16)
        h_ref[...] = h
        proj = _dot(h, w_ref[...])
        cos, sn = _all_heads(c_ref[...]), _all_heads(s_ref[...])
        q, k = proj[:, 0:AW], proj[:, AW:2 * AW]
        q_ref[...] = (q * cos + _rot_half(q) * sn) * SCALE
        kv_ref[...] = _pack_pair(k * cos + _rot_half(k) * sn, proj[:, 2 * AW:3 * AW])
        bcu_ref[...] = proj[:, 3 * AW:3 * AW + 3 * CW]
        qx_ref[...] = (proj[:, 3 * AW + 3 * CW:] * SCALE).astype(BF16)

    def tile(w):
        return pl.BlockSpec((TQ, w), lambda i: (i, 0))

    return _call_with_gather(
        body, NT, shards, name="in_proj",
        in_specs=[tile(D), _const((1, D)), _const((N_DEV, D, blk)), tile(LANES), tile(LANES)],
        out_specs=[tile(AW), tile(AW), tile(3 * CW), tile(XW), tile(D), _acc((D, PW))],
        out_shape=[jax.ShapeDtypeStruct((S, AW), F32)] * 2 + [
            jax.ShapeDtypeStruct((S, 3 * CW), F32), jax.ShapeDtypeStruct((S, XW), BF16),
            jax.ShapeDtypeStruct((S, D), BF16), jax.ShapeDtypeStruct((D, PW), BF16)],
        scratch_shapes=[pltpu.VMEM((D, PW), BF16)], vmem_mb=56, args=(x, g, w8, cos, sins))


ATTN_PLANS = (("p1", 1, 128, 32), ("p4", 8, 64, 8), ("p16", 16, 128, 2))
PAD = 128
WIN = 256


ATTN_UNROLL = 8


def _fill_bias(tab, qblk, partner):
    qi = lax.broadcasted_iota(jnp.int32, (2 * qblk, WIN), 0) & (qblk - 1)
    kj = lax.broadcasted_iota(jnp.int32, (2 * qblk, WIN), 1)
    piece = kj >> (qblk.bit_length() - 1)
    kk = kj & (qblk - 1)
    prev = (piece & 1) == 0
    of_partner = piece >= 2
    for first in (0, 1):
        for par in (0, 1):
            lo = jnp.where(prev, (qblk if first else qi) + jnp.where(of_partner, par, 0), 0)
            hi = jnp.where(prev, qblk, qi + jnp.where(of_partner, par - 1, 0))
            tab[2 * first + par] = jnp.where((kk >= lo) & (kk <= hi), 0.0, NEG).astype(F32)


def _block_rows(g, qblk, nbc, partner):
    own = pl.ds(pl.multiple_of(PAD + g * qblk, qblk), qblk)
    first = ((g & (nbc - 1)) == 0).astype(jnp.int32)
    if partner:
        gp = jnp.bitwise_xor(g, 4 * nbc)
        wins = (pl.ds(pl.multiple_of(PAD + (g - 1) * qblk, qblk), 2 * qblk),
                pl.ds(pl.multiple_of(PAD + (gp - 1) * qblk, qblk), 2 * qblk))
        return own, wins, 2 * first + ((g >> ((4 * nbc).bit_length() - 1)) & 1)
    return own, (pl.ds(pl.multiple_of(PAD + (g - 1) * qblk, qblk), 2 * qblk),), 2 * first


def _pack_pair(lo, hi):
    lo_bits = lax.bitcast_convert_type(lo.astype(BF16).astype(F32), jnp.uint32) >> 16
    hi_bits = lax.bitcast_convert_type(hi.astype(BF16).astype(F32), jnp.uint32) & jnp.uint32(0xFFFF0000)
    return lax.bitcast_convert_type(hi_bits | lo_bits, F32)


def _unpack_pair(c):
    bits = lax.bitcast_convert_type(c, jnp.uint32)
    lo = lax.bitcast_convert_type(bits << 16, F32).astype(BF16)
    hi = lax.bitcast_convert_type(bits & jnp.uint32(0xFFFF0000), F32).astype(BF16)
    return lo, hi


def _window(ref, wins):
    parts = [ref[w, :] for w in wins]
    return parts[0] if len(parts) == 1 else jnp.concatenate(parts, axis=0)


def _stack_heads(t, lane):
    zero = jnp.zeros_like(t)
    return jnp.concatenate([jnp.where(lane < HEAD, t, zero), jnp.where(lane >= HEAD, t, zero)], axis=0)


def _unstack_heads(t2, lane):
    half = t2.shape[0] // 2
    return jnp.where(lane < HEAD, t2[0:half, :], t2[half:, :])


def _lanes_of(step):
    return pl.ds(pl.multiple_of(step * LANES, LANES), LANES)


def _whole_wait(buf, sem):
    whole = buf.at[pl.ds(PAD, S), :]
    return pltpu.make_async_copy(whole, whole, sem)


def _whole_waits(bufs, sems):
    return [_whole_wait(buf, sems.at[i]) for i, buf in enumerate(bufs)]


def _class_gather(views, bufs, sems, lanes):
    copies = []
    for i, (view, buf) in enumerate(zip(views, bufs)):
        if view.ndim == 2:
            copies.append(pltpu.make_async_copy(view.at[:, lanes], buf.at[pl.ds(PAD, S), :], sems.at[i]))
        else:
            per, n_cls = view.shape[0], view.shape[1]
            copies += [pltpu.make_async_copy(view.at[:, c, lanes], buf.at[pl.ds(PAD + c * per, per), :], sems.at[i])
                       for c in range(n_cls)]
    return copies


def _class_scatter(bufs, dsts, sems, lanes=None):
    copies = []
    for i, (buf, dst) in enumerate(zip(bufs, dsts)):
        if dst.ndim == 2:
            copies.append(pltpu.make_async_copy(buf.at[pl.ds(PAD, S), :], dst.at[:, lanes], sems.at[i]))
            continue
        per, n_cls = dst.shape[0], dst.shape[1]
        for c in range(n_cls):
            to = dst.at[:, c, :] if lanes is None else dst.at[:, c, lanes]
            copies.append(pltpu.make_async_copy(buf.at[pl.ds(PAD + c * per, per), :], to, sems.at[i]))
    return copies


def _start(copies):
    for cp in copies:
        cp.start()


def _wait(waits):
    for w in waits:
        w.wait()


def _attn_fwd(q, kvp, shards=()):
    views = [[a] + [a.reshape(S // n, n, AW) for _, n, _, _ in ATTN_PLANS[1:]] for a in (q, kvp)]
    flat = [views[a][p] for p in range(3) for a in range(2)]
    ng = len(shards)
    n_grid = AW // LANES

    def body(*refs):
        hbm = [refs[2 * p:2 * p + 2] for p in range(3)]
        refs = refs[6:]
        shard_refs, refs = refs[:ng], refs[ng:]
        y_ref, lt_ref = refs[0:2]
        whole_refs, refs = refs[2:2 + ng], refs[2 + ng:]
        bufs = [refs[2 * p:2 * p + 2] for p in range(3)]
        oc4, lc4, oc16, lc16, tab128, tab4, sem_in = refs[6:13]
        step = pl.program_id(0)
        if ng:
            start_gather, finish_gather = _gather_steps(shard_refs, whole_refs, *refs[13:])
            pl.when(step == 0)(start_gather)
        now = [_class_gather(hbm[p], bufs[p], sem_in.at[p], _lanes_of(step)) for p in range(3)]
        nxt = [_class_gather(hbm[p], bufs[p], sem_in.at[p], _lanes_of(step + 1)) for p in range(3)]

        @pl.when(step == 0)
        def _():
            for p in range(3):
                _start(now[p])
                for b in bufs[p]:
                    b[0:PAD, :] = jnp.zeros((PAD, LANES), F32)
            _fill_bias(tab128, 128, False)
            _fill_bias(tab4, 64, True)

        def prefetch(p):
            pl.when(step + 1 < n_grid)(lambda: _start(nxt[p]))

        lane = lax.broadcasted_iota(jnp.int32, (1, LANES), 1)
        ones = jnp.ones((WIN, LANES), BF16)

        def run(plan, bq, bkv, tab, o_dst, l_dst, dst_pad):
            _, n_cls, qblk, nbc = plan
            partner = n_cls == 8

            def block(g, carry):
                own, wins, mask = _block_rows(g, qblk, nbc, partner)
                q2 = _stack_heads(bq[own, :].astype(BF16), lane)
                kw, vwin = _unpack_pair(_window(bkv, wins))
                vw = jnp.concatenate([vwin, ones], axis=1)
                s = _dot_nt(q2, kw) + tab[mask]
                m = jnp.max(s, axis=1, keepdims=True)
                oe = _dot(jnp.exp(s - m).astype(BF16), vw)
                den = oe[:, LANES:]
                dst = pl.ds(pl.multiple_of(dst_pad + g * qblk, qblk), qblk)
                o_dst[dst, :] = _unstack_heads(oe[:, 0:LANES] / den, lane)
                l_dst[dst, :] = _unstack_heads(m + jnp.log(den), lane)
                return carry
            lax.fori_loop(0, n_cls * nbc, block, 0, unroll=ATTN_UNROLL)

        _wait(_whole_waits(bufs[0], sem_in.at[0]))
        run(ATTN_PLANS[0], *bufs[0], tab128, y_ref, lt_ref, 0)
        prefetch(0)
        _wait(_whole_waits(bufs[1], sem_in.at[1]))
        run(ATTN_PLANS[1], *bufs[1], tab4, oc4, lc4, PAD)
        prefetch(1)
        _wait(_whole_waits(bufs[2], sem_in.at[2]))
        run(ATTN_PLANS[2], *bufs[2], tab128, oc16, lc16, PAD)
        prefetch(2)

        n_rows = 64

        def token_order(buf, t, n_cls):
            per = S // n_cls
            first = PAD + t * (n_rows // n_cls)
            return jnp.concatenate([buf[pl.ds(first + jj, n_cls, stride=per), :] for jj in range(n_rows // n_cls)],
                                   axis=0)

        def combine(t, carry):
            rows = pl.ds(pl.multiple_of(t * n_rows, n_rows), n_rows)
            l0, l1, l2 = lt_ref[rows, :], token_order(lc4, t, 8), token_order(lc16, t, 16)
            lm = jnp.maximum(jnp.maximum(l0, l1), l2)
            e0, e1, e2 = jnp.exp(l0 - lm), jnp.exp(l1 - lm), jnp.exp(l2 - lm)
            den = e0 + e1 + e2
            y_ref[rows, :] = (e0 * y_ref[rows, :] + e1 * token_order(oc4, t, 8)
                              + e2 * token_order(oc16, t, 16)) / den
            lt_ref[rows, :] = lm + jnp.log(den)
            return carry
        lax.fori_loop(0, S // n_rows, combine, 0, unroll=2)

        if ng:
            pl.when(step == n_grid - 1)(finish_gather)

    col = pl.BlockSpec((S, LANES), lambda h: (0, h))
    padded = pltpu.VMEM((PAD + S, LANES), F32)
    return pl.pallas_call(
        body, grid=(n_grid,), name="attn_fwd",
        in_specs=[ANY] * (6 + ng), out_specs=[col, col] + [ANY] * ng,
        out_shape=[jax.ShapeDtypeStruct((S, AW), F32)] * 2 + _gathered_shapes(shards),
        scratch_shapes=[padded] * 10 + [
            pltpu.VMEM((4, 256, WIN), F32), pltpu.VMEM((4, 128, WIN), F32), pltpu.SemaphoreType.DMA((3, 2))]
        + (_gather_scratch(ng) if ng else []),
        compiler_params=_cparams(56))(*flat, *shards)


def _conv_taps(z, zprev, row):
    z1 = jnp.where(row == 0, zprev[7:8, :], pltpu.roll(z, 1, 0))
    z2 = jnp.where(row == 0, zprev[6:7, :], jnp.where(row == 1, zprev[7:8, :], pltpu.roll(z, 2, 0)))
    return z1, z2


def _xattn_scores(qm, km):
    s = _dot_nt(qm, km)
    m = jnp.max(s, axis=1, keepdims=True)
    e = jnp.exp(s - m)
    return e, jnp.sum(e, axis=1, keepdims=True)


def _mix_out(y_attn, bcu, qx16, kv16, cw8, g_attn, g_conv, g_x, g_post, wout16, x, shards):
    def body(ya_ref, bcu_ref, halo_ref, qx_ref, kv_ref, cw_ref, ga_ref, gc_ref, gx_ref, gp_ref, w_ref, x_ref,
             ypre_ref, y16_ref, y2_ref, x1_ref):
        i = pl.program_id(0)
        bcu = bcu_ref[...]
        b, c, u = bcu[:, 0:CW], bcu[:, CW:2 * CW], bcu[:, 2 * CW:]
        z = c * u
        halo = halo_ref[...]
        zprev = jnp.where(i > 0, halo[:, CW:2 * CW] * halo[:, 2 * CW:], 0.0)
        row = lax.broadcasted_iota(jnp.int32, z.shape, 0)
        z1, z2 = _conv_taps(z, zprev, row)
        cw = cw_ref[...]
        y_conv = b * (z2 * cw[0:1, :] + z1 * cw[1:2, :] + z * cw[2:3, :])

        qx = qx_ref[...]
        kv = kv_ref[...]
        km, vm = kv[:, 0:XW], kv[:, XW:]
        lane = lax.broadcasted_iota(jnp.int32, qx.shape, 1)
        y_x = jnp.zeros(qx.shape, F32)
        for h in range(XW // HEAD):
            hm = (lane >= h * HEAD) & (lane < (h + 1) * HEAD)
            e, l = _xattn_scores(jnp.where(hm, qx, jnp.zeros_like(qx)), km)
            y_x = jnp.where(hm, _dot(e.astype(BF16), vm) / l, y_x)

        y_attn = ya_ref[...]
        ypre_ref[:, 0:AW] = y_attn
        ypre_ref[:, AW:AW + CW] = y_conv
        ypre_ref[:, AW + CW:] = y_x
        y = jnp.concatenate([_rms(y_attn, ga_ref[...])[0], _rms(y_conv, gc_ref[...])[0],
                             _rms(y_x, gx_ref[...])[0]], axis=1).astype(BF16)
        y16_ref[...] = y
        y2 = _dot(y, w_ref[...])
        y2_ref[...] = y2
        x1_ref[...] = x_ref[...] + _rms(y2, gp_ref[...])[0]

    def tile(w):
        return pl.BlockSpec((TQ, w), lambda i: (i, 0))

    halo = pl.BlockSpec((SUBLANES, 3 * CW), lambda i: (jnp.maximum(i * (TQ // SUBLANES) - 1, 0), 0))
    return _call_with_gather(
        body, NT, shards, name="mix_out",
        in_specs=[tile(AW), tile(3 * CW), halo, tile(XW), _const((N_MEM, 2 * XW)), _const((SUBLANES, CW)),
                  _const((1, AW)), _const((1, CW)), _const((1, XW)), _const((1, D)), _const((D, D)), tile(D)],
        out_specs=[tile(D), tile(D), tile(D), tile(D)],
        out_shape=[jax.ShapeDtypeStruct((S, D), F32), jax.ShapeDtypeStruct((S, D), BF16),
                   jax.ShapeDtypeStruct((S, D), F32), jax.ShapeDtypeStruct((S, D), F32)],
        scratch_shapes=[], vmem_mb=56,
        args=(y_attn, bcu, bcu, qx16, kv16, cw8, g_attn, g_conv, g_x, g_post, wout16, x))


def _mlp(x1, tgt, g_pre, g_post, wup8, wdn16):
    tq = TQ_MLP

    def body(x1_ref, t_ref, g1_ref, g2_ref, wu_ref, wd_ref,
             a16_ref, du_ref, h2_ref, df2_ref, dx1_ref, loss_ref, dg_ref, a32):
        @pl.when(pl.program_id(0) == 0)
        def _():
            loss_ref[...] = jnp.zeros_like(loss_ref)
            dg_ref[...] = jnp.zeros_like(dg_ref)

        x1 = x1_ref[...]
        g1, g2 = g1_ref[...], g2_ref[...]
        y1, n1, r1 = _rms(x1, g1)
        h2 = y1.astype(BF16)
        h2_ref[...] = h2
        f2 = jnp.zeros((tq, D), F32)
        for j in range(N_DEV):
            cols = slice(j * FF_BLK, (j + 1) * FF_BLK)
            a = jnp.maximum(_dot(h2, wu_ref[j]), 0.0)
            a32[:, cols] = a
            a16_ref[:, cols] = a.astype(BF16)
            f2 = f2 + _dot((a * a).astype(BF16), wd_ref[cols, :])
        y2, n2, r2 = _rms(f2, g2)
        e = x1 + y2 - t_ref[...]
        sq = jnp.sum(jnp.sum(e * e, axis=1, keepdims=True), axis=0, keepdims=True)
        loss_ref[...] += jnp.broadcast_to(sq * (0.5 / D), loss_ref.shape)
        dout = e * (1.0 / D)
        df2, dg2 = _rms_bwd(dout, n2, r2, g2)
        df2_16 = df2.astype(BF16)
        df2_ref[...] = df2_16
        dh2 = jnp.zeros((tq, D), F32)
        for j in range(N_DEV):
            cols = slice(j * FF_BLK, (j + 1) * FF_BLK)
            du = (_dot_nt(df2_16, wd_ref[cols, :]) * (2.0 * a32[:, cols])).astype(BF16)
            du_ref[:, cols] = du
            dh2 = dh2 + _dot_nt(du, wu_ref[j])
        dx, dg1 = _rms_bwd(dh2, n1, r1, g1)
        dx1_ref[...] = dout + dx
        dg_ref[0:1, :] += dg2
        dg_ref[1:2, :] += dg1

    def tile(w):
        return pl.BlockSpec((tq, w), lambda i: (i, 0))

    return pl.pallas_call(
        body, grid=(S // tq,), name="mlp",
        in_specs=[tile(D), tile(D), _const((1, D)), _const((1, D)), _const((N_DEV, D, FF_BLK)), _const((FF, D))],
        out_specs=[tile(FF), tile(FF), tile(D), tile(D), tile(D), _acc((SUBLANES, LANES)), _acc((SUBLANES, D))],
        out_shape=[jax.ShapeDtypeStruct((S, FF), BF16), jax.ShapeDtypeStruct((S, FF), BF16),
                   jax.ShapeDtypeStruct((S, D), BF16), jax.ShapeDtypeStruct((S, D), BF16),
                   jax.ShapeDtypeStruct((S, D), F32), jax.ShapeDtypeStruct((SUBLANES, LANES), F32),
                   jax.ShapeDtypeStruct((SUBLANES, D), F32)],
        scratch_shapes=[pltpu.VMEM((tq, FF), F32)],
        compiler_params=_cparams(56))(x1, tgt, g_pre, g_post, wup8, wdn16)


def _mix_out_bwd(dx1, y2, ypre, ltot, head_ones, q, bcu, qx16, kv16, cw8, g_post, g_attn, g_conv, g_x, wout16):
    def body(dx1_ref, y2_ref, ypre_ref, lt_ref, e_ref, q_ref, bcu_ref, halo_ref, qx_ref, kv_ref, cw_ref, gp_ref,
             ga_ref, gc_ref, gx_ref, w_ref, dy2_ref, qdo_ref, ld_ref, dbcu_ref, dqx_ref, dgs_ref, dcw_ref, dkv_ref,
             carry):
        i = pl.program_id(0)

        @pl.when(i == 0)
        def _():
            dgs_ref[...] = jnp.zeros_like(dgs_ref)
            dcw_ref[...] = jnp.zeros_like(dcw_ref)
            dkv_ref[...] = jnp.zeros_like(dkv_ref)
            carry[...] = jnp.zeros_like(carry)

        gp = gp_ref[...]
        _, n, r = _rms(y2_ref[...], gp)
        dy2, dgp = _rms_bwd(dx1_ref[...], n, r, gp)
        dy2_16 = dy2.astype(BF16)
        dy2_ref[...] = dy2_16
        dy = _dot_nt(dy2_16, w_ref[...])

        ypre = ypre_ref[...]
        ga, gc, gx = ga_ref[...], gc_ref[...], gx_ref[...]
        _, na, ra = _rms(ypre[:, 0:AW], ga)
        dya, dga = _rms_bwd(dy[:, 0:AW], na, ra, ga)
        _, nc, rc = _rms(ypre[:, AW:AW + CW], gc)
        dyc, dgc = _rms_bwd(dy[:, AW:AW + CW], nc, rc, gc)
        y_x = ypre[:, AW + CW:]
        _, nx, rx = _rms(y_x, gx)
        dyx, dgx = _rms_bwd(dy[:, AW + CW:], nx, rx, gx)
        qdo_ref[...] = _pack_pair(q_ref[...], dya)
        prod = dya * ypre[:, 0:AW]
        hi = prod.astype(BF16)
        lo = (prod - hi.astype(F32)).astype(BF16)
        head_sum = _dot(hi, e_ref[...]) + _dot(lo, e_ref[...])
        lane_a = lax.broadcasted_iota(jnp.int32, prod.shape, 1)
        ld_ref[...] = jnp.where((lane_a % HEAD) < HEAD // 2, lt_ref[...], head_sum)
        dgs_ref[0:1, :] += dgp
        dgs_ref[1:2, :] += jnp.concatenate([dga, dgc, dgx], axis=1)

        bcu = bcu_ref[...]
        b, c, u = bcu[:, 0:CW], bcu[:, CW:2 * CW], bcu[:, 2 * CW:]
        z = c * u
        halo = halo_ref[...]
        zprev = jnp.where(i < NT - 1, halo[:, CW:2 * CW] * halo[:, 2 * CW:], 0.0)
        row = lax.broadcasted_iota(jnp.int32, z.shape, 0)
        z1, z2 = _conv_taps(z, zprev, row)
        cw = cw_ref[...]
        conv = z2 * cw[0:1, :] + z1 * cw[1:2, :] + z * cw[2:3, :]
        dconv = dyc * b
        nxt = carry[...]
        dn1 = jnp.where(row == TQ - 1, nxt[0:1, :], pltpu.roll(dconv, TQ - 1, 0))
        dn2 = jnp.where(row == TQ - 1, nxt[1:2, :], jnp.where(row == TQ - 2, nxt[0:1, :], pltpu.roll(dconv, TQ - 2, 0)))
        carry[...] = dconv[0:SUBLANES, :]
        dz = dconv * cw[2:3, :] + dn1 * cw[1:2, :] + dn2 * cw[0:1, :]
        dbcu_ref[:, 0:CW] = (dyc * conv).astype(BF16)
        dbcu_ref[:, CW:2 * CW] = (dz * u).astype(BF16)
        dbcu_ref[:, 2 * CW:] = (dz * c).astype(BF16)
        dcw_ref[0:1, :] += jnp.sum(z2 * dconv, axis=0, keepdims=True)
        dcw_ref[1:2, :] += jnp.sum(z1 * dconv, axis=0, keepdims=True)
        dcw_ref[2:3, :] += jnp.sum(z * dconv, axis=0, keepdims=True)

        qx = qx_ref[...]
        kv = kv_ref[...]
        km, vm = kv[:, 0:XW], kv[:, XW:]
        lane = lax.broadcasted_iota(jnp.int32, qx.shape, 1)
        dqx = jnp.zeros(qx.shape, F32)
        dkm = jnp.zeros((N_MEM, XW), F32)
        dvm = jnp.zeros((N_MEM, XW), F32)
        for h in range(XW // HEAD):
            hm = (lane >= h * HEAD) & (lane < (h + 1) * HEAD)
            qm = jnp.where(hm, qx, jnp.zeros_like(qx))
            e, l = _xattn_scores(qm, km)
            p = e / l
            dom = jnp.where(hm, dyx, 0.0)
            do16 = dom.astype(BF16)
            dsum = jnp.sum(dom * y_x, axis=1, keepdims=True)
            ds = (p * (_dot_nt(do16, vm) - dsum)).astype(BF16)
            dqx = jnp.where(hm, _dot(ds, km), dqx)
            dkm = dkm + _dot_tn(ds, qm)
            dvm = dvm + _dot_tn(p.astype(BF16), do16)
        dqx_ref[...] = (dqx * SCALE).astype(BF16)
        dkv_ref[:, 0:XW] += dkm
        dkv_ref[:, XW:] += dvm

    def tile(w):
        return pl.BlockSpec((TQ, w), lambda i: (NT - 1 - i, 0))

    halo = pl.BlockSpec((SUBLANES, 3 * CW), lambda i: (jnp.maximum((NT - 1 - i) * (TQ // SUBLANES) - 1, 0), 0))
    return pl.pallas_call(
        body, grid=(NT,), name="mix_out_bwd",
        in_specs=[tile(D), tile(D), tile(D), tile(AW), _const((AW, AW)), tile(AW), tile(3 * CW), halo, tile(XW),
                  _const((N_MEM, 2 * XW)), _const((SUBLANES, CW)), _const((1, D)), _const((1, AW)), _const((1, CW)),
                  _const((1, XW)), _const((D, D))],
        out_specs=[tile(D), tile(AW), tile(AW), tile(3 * CW), tile(XW), _acc((SUBLANES, D)), _acc((SUBLANES, CW)),
                   _acc((N_MEM, 2 * XW))],
        out_shape=[jax.ShapeDtypeStruct((S, D), BF16), jax.ShapeDtypeStruct((S, AW), F32),
                   jax.ShapeDtypeStruct((S, AW), F32),
                   jax.ShapeDtypeStruct((S, 3 * CW), BF16), jax.ShapeDtypeStruct((S, XW), BF16),
                   jax.ShapeDtypeStruct((SUBLANES, D), F32), jax.ShapeDtypeStruct((SUBLANES, CW), F32),
                   jax.ShapeDtypeStruct((N_MEM, 2 * XW), F32)],
        scratch_shapes=[pltpu.VMEM((SUBLANES, CW), F32)],
        compiler_params=_cparams(56))(dx1, y2, ypre, ltot, head_ones, q, bcu, bcu, qx16, kv16, cw8, g_post, g_attn,
                                      g_conv, g_x, wout16)


def _attn_bwd(qdo, kvp, ld, chip_sums=()):
    n_in = 3
    views = [[a] + [a.reshape(S // n, n, AW) for _, n, _, _ in ATTN_PLANS[1:]] for a in (qdo, kvp, ld)]
    flat = [views[a][p] for p in range(3) for a in range(n_in)]
    ns = len(chip_sums)
    n_grid = AW // LANES

    def body(*refs):
        hbm = [refs[n_in * p:n_in * p + n_in] for p in range(3)]
        refs = refs[3 * n_in:]
        sum_refs, refs = refs[:ns], refs[ns:]
        outs = [refs[3 * p:3 * p + 3] for p in range(3)]
        landed_refs, sc = refs[9:9 + ns], refs[9 + ns:]
        bufs = [sc[3 * p:3 * p + 3] for p in range(3)]
        res = [sc[9 + 3 * p:12 + 3 * p] for p in range(3)]
        tab128, tab4, sem_in, sem_out = sc[18:22]
        step = pl.program_id(0)
        if ns:
            start_chips, finish_chips = _chips_steps(sum_refs, landed_refs, *sc[22:])
            pl.when(step == 0)(start_chips)
        now = [_class_gather(hbm[p], bufs[p], sem_in.at[p], _lanes_of(step)) for p in range(3)]
        nxt = [_class_gather(hbm[p], bufs[p], sem_in.at[p], _lanes_of(step + 1)) for p in range(3)]

        @pl.when(step == 0)
        def _():
            for p in range(3):
                _start(now[p])
                for b in bufs[p]:
                    b[0:PAD, :] = jnp.zeros((PAD, LANES), F32)
            _fill_bias(tab128, 128, False)
            _fill_bias(tab4, 64, True)

        def prefetch(p):
            pl.when(step + 1 < n_grid)(lambda: _start(nxt[p]))

        for p in range(3):
            for b in res[p]:
                b[...] = jnp.zeros_like(b)
        lane = lax.broadcasted_iota(jnp.int32, (1, LANES), 1)

        def run(plan, plan_bufs, tab, dst):
            _, n_cls, qblk, nbc = plan
            partner = n_cls == 8
            bqdo, bkv, bld = plan_bufs
            rq, rk, rv = dst

            def block(g, carry):
                own, wins, mask = _block_rows(g, qblk, nbc, partner)
                qb, dob = _unpack_pair(bqdo[own, :])
                q2, do2 = _stack_heads(qb, lane), _stack_heads(dob, lane)
                kw, vw = _unpack_pair(_window(bkv, wins))
                ldv = bld[own, :]
                half = HEAD // 2
                lt2 = jnp.concatenate([ldv[:, 0:1], ldv[:, HEAD:HEAD + 1]], axis=0)
                dsum2 = jnp.concatenate([ldv[:, half:half + 1], ldv[:, HEAD + half:HEAD + half + 1]], axis=0)
                p = jnp.exp(_dot_nt(q2, kw) + tab[mask] - lt2)
                ds = (p * (_dot_nt(do2, vw) - dsum2)).astype(BF16)
                rq[own, :] = _unstack_heads(_dot(ds, kw), lane)
                dkw = _dot_tn(ds, q2)
                dvw = _dot_tn(p.astype(BF16), do2)
                n_w = WIN // len(wins)
                for i, w in enumerate(wins):
                    rk[w, :] += dkw[i * n_w:(i + 1) * n_w, :]
                    rv[w, :] += dvw[i * n_w:(i + 1) * n_w, :]
                return carry
            lax.fori_loop(0, n_cls * nbc, block, 0, unroll=ATTN_UNROLL)

        tabs = (tab128, tab4, tab128)
        for p in range(3):
            _wait(_whole_waits(bufs[p], sem_in.at[p]))
            run(ATTN_PLANS[p], bufs[p], tabs[p], res[p])
            prefetch(p)
            _start(_class_scatter(res[p], outs[p], sem_out.at[p], _lanes_of(step)))
        for p in range(3):
            _wait(_whole_waits(res[p], sem_out.at[p]))
        if ns:
            pl.when(step == n_grid - 1)(finish_chips)

    padded = pltpu.VMEM((PAD + S, LANES), F32)
    shapes = [jax.ShapeDtypeStruct(views[0][p].shape, F32) for p in range(3) for _ in range(3)]
    out = pl.pallas_call(
        body, grid=(n_grid,), name="attn_bwd",
        in_specs=[ANY] * (3 * n_in + ns), out_specs=[ANY] * (9 + ns),
        out_shape=shapes + _chips_shapes(chip_sums),
        scratch_shapes=[padded] * 18
        + [pltpu.VMEM((4, 256, WIN), F32), pltpu.VMEM((4, 128, WIN), F32),
           pltpu.SemaphoreType.DMA((3, n_in)), pltpu.SemaphoreType.DMA((3, 3))]
        + (_chips_scratch(ns) if ns else []),
        compiler_params=_cparams(56))(*flat, *chip_sums)
    return [o.reshape(S, AW) for o in out[:9]] + list(out[9:])


def _in_proj_bwd(dqkv, dbcu, dqx, cos, sins, w16, x, g, dx1):
    tq = TQ // 2

    def body(*refs):
        parts = refs[0:9]
        dbcu_ref, dqx_ref, c_ref, s_ref, w_ref, x_ref, g_ref, dx1_ref, dp_ref, gx_ref, dg_ref = refs[9:]

        @pl.when(pl.program_id(0) == 0)
        def _():
            dg_ref[...] = jnp.zeros_like(dg_ref)

        dq, dk, dv = (parts[i][...] + parts[3 + i][...] + parts[6 + i][...] for i in range(3))
        cos, sn = _all_heads(c_ref[...]), _all_heads(s_ref[...])
        dqr = dq * SCALE
        dkr = dk
        dp = jnp.concatenate([(dqr * cos + _rot_half(dqr * sn)).astype(BF16),
                              (dkr * cos + _rot_half(dkr * sn)).astype(BF16), dv.astype(BF16),
                              dbcu_ref[...], dqx_ref[...]], axis=1)
        dp_ref[...] = dp
        dh = _dot_nt(dp, w_ref[...])
        g = g_ref[...]
        _, n, r = _rms(x_ref[...], g)
        dx, dg = _rms_bwd(dh, n, r, g)
        gx_ref[...] = dx1_ref[...] + dx
        dg_ref[0:1, :] += dg

    def tile(w):
        return pl.BlockSpec((tq, w), lambda i: (i, 0))

    return pl.pallas_call(
        body, grid=(S // tq,), name="in_proj_bwd",
        in_specs=[tile(AW)] * 9 + [tile(3 * CW), tile(XW), tile(LANES), tile(LANES), _const((D, PW)),
                                   tile(D), _const((1, D)), tile(D)],
        out_specs=[tile(PW), tile(D), _acc((SUBLANES, D))],
        out_shape=[jax.ShapeDtypeStruct((S, PW), BF16), jax.ShapeDtypeStruct((S, D), F32),
                   jax.ShapeDtypeStruct((SUBLANES, D), F32)],
        compiler_params=_cparams(56))(*dqkv, dbcu, dqx, cos, sins, w16, x, g, dx1)


def _mem_bwd(mem, g_mem, wkv16, dkv):
    def body(m_ref, g_ref, w_ref, dkv_ref, dkv16_ref, dg_ref):
        dkv16 = dkv_ref[...].astype(BF16)
        dkv16_ref[...] = dkv16
        _, n, _ = _rms(m_ref[...], g_ref[...])
        dg = jnp.sum(_dot_nt(dkv16, w_ref[...]) * n, axis=0, keepdims=True)
        dg_ref[...] = jnp.broadcast_to(dg, dg_ref.shape)

    return pl.pallas_call(
        body, name="mem_bwd",
        out_shape=[jax.ShapeDtypeStruct((N_MEM, 2 * XW), BF16), jax.ShapeDtypeStruct((SUBLANES, D), F32)],
        compiler_params=pltpu.CompilerParams(vmem_limit_bytes=32 << 20))(mem, g_mem, wkv16, dkv)


N_CHIPS = N_DEV // 2


def _transpose_into(at, a_ref):
    kk = a_ref.shape[0]
    chunk = min(kk, 512)
    for c in range(kk // chunk):
        at[:, c * chunk:(c + 1) * chunk] = a_ref[c * chunk:(c + 1) * chunk, :].T


def _pair_scratch(block):
    return [pltpu.VMEM((N_CHIPS,) + block, BF16), pltpu.VMEM((N_CHIPS,) + block, BF16),
            pltpu.SemaphoreType.DMA((N_CHIPS,)), pltpu.SemaphoreType.DMA((N_CHIPS,))]


def _swap_with_sibling(p, stage, land, send, recv):
    x, y, c = lax.axis_index("x"), lax.axis_index("y"), lax.axis_index("c")
    return pltpu.make_async_remote_copy(src_ref=stage.at[p], dst_ref=land.at[p], send_sem=send.at[p],
                                        recv_sem=recv.at[p], device_id=(x, y, 1 - c), device_id_type=MESH)


def _wgrad_cols(place, a16, b16, blk, name, square_b=False, transpose_out=False, to_chips=False):
    kk, m = a16.shape
    aligned = blk % LANES == 0
    wide = blk if aligned else -(-(blk + LANES // 2) // LANES) * LANES
    block = (blk, m) if transpose_out else (m, blk)

    def chip_of(step, my_chip):
        return (my_chip + 1 + step) & (N_CHIPS - 1) if to_chips else step

    def body(pl_ref, a_ref, *refs):
        b_refs, refs = refs[:2 if aligned else 1], refs[2 if aligned else 1:]
        (cs_ref, own_ref), refs = refs[:2], refs[2:]
        if to_chips:
            landed, refs = refs[0], refs[1:]
        (at, stage, land, send, recv), refs = refs[:5], refs[5:]
        if not aligned:
            (win, wsem), refs = refs[:2], refs[2:]
        step = pl.program_id(0)
        x, y, c = lax.axis_index("x"), lax.axis_index("y"), lax.axis_index("c")
        my_chip = 2 * x + y
        p = chip_of(step, my_chip)

        def fetch(at_step, mine):
            j = 2 * chip_of(at_step, my_chip) + (c if mine else 1 - c)
            first = pl.multiple_of(((j * blk) >> 7) << 7, LANES)
            slot = 2 * (at_step & 1) + mine
            return pltpu.make_async_copy(b_refs[0].at[:, pl.ds(first, wide)], win.at[slot], wsem.at[slot])

        @pl.when(step == 0)
        def _():
            if not aligned:
                fetch(0, 0).start()
                fetch(0, 1).start()
            _transpose_into(at, a_ref)

        if not aligned:
            @pl.when(step + 1 < N_CHIPS)
            def _():
                fetch(step + 1, 0).start()
                fetch(step + 1, 1).start()

        def partial(mine):
            if aligned:
                b = b_refs[mine][...]
                if square_b:
                    b = b * b
                acc = _dot(at[...], b)
            else:
                fetch(step, mine).wait()
                acc = _dot(at[...], win[2 * (step & 1) + mine])
                odd = c if mine else 1 - c
                acc = pltpu.roll(acc, jnp.where(odd == 0, 0, wide - LANES // 2), 1)[:, 0:blk]
            return acc.T if transpose_out else acc

        stage[p] = partial(0).astype(BF16)
        swap = _swap_with_sibling(p, stage, land, send, recv)
        swap.start()
        mine = partial(1)
        swap.wait()
        total = mine + land[p].astype(F32)
        cs_ref[0] = total.astype(BF16)

        @pl.when(p == my_chip)
        def _():
            own_ref[...] = total

        if to_chips:
            stage2, send2, recv2 = refs
            flipped = jnp.bitwise_xor(p, my_chip)
            k = jnp.where(flipped == 2, 0, jnp.where(flipped == 1, 1, 2))

            def to_owner(src, k_, px, py):
                return pltpu.make_async_remote_copy(src_ref=src, dst_ref=landed.at[k_], send_sem=send2.at[k_],
                                                    recv_sem=recv2.at[k_], device_id=(px, py, c), device_id_type=MESH)

            @pl.when(p != my_chip)
            def _():
                stage2[p] = total.astype(BF16)
                to_owner(stage2.at[p], k, p >> 1, p & 1).start()

            @pl.when(step == N_CHIPS - 1)
            def _():
                for k_ in range(N_CHIPS - 1):
                    to_owner(stage2.at[0], k_, x, y).wait()

    def b_spec(mine):
        return pl.BlockSpec((kk, blk), lambda i, s: (0, 2 * chip_of(i, s[1]) + (s[0] if mine else 1 - s[0])))

    b_specs, b_args = ([b_spec(0), b_spec(1)], (b16, b16)) if aligned else ([ANY], (b16,))
    scratch = [pltpu.VMEM((m, kk), BF16)] + _pair_scratch(block)
    if not aligned:
        scratch += [pltpu.VMEM((4, kk, wide), BF16), pltpu.SemaphoreType.DMA((4,))]
    out_specs = [pl.BlockSpec((1,) + block, lambda i, s: (chip_of(i, s[1]), 0, 0)), pl.BlockSpec(block, lambda i, s: (0, 0))]
    out_shape = [jax.ShapeDtypeStruct((N_CHIPS,) + block, BF16), jax.ShapeDtypeStruct(block, F32)]
    if to_chips:
        out_specs.append(ANY)
        out_shape.append(jax.ShapeDtypeStruct((N_CHIPS - 1,) + block, BF16))
        scratch += [pltpu.VMEM((N_CHIPS,) + block, BF16), pltpu.SemaphoreType.DMA((N_CHIPS - 1,)),
                    pltpu.SemaphoreType.DMA((N_CHIPS - 1,))]
    return pl.pallas_call(
        body, name=name,
        grid_spec=pltpu.PrefetchScalarGridSpec(
            num_scalar_prefetch=1, grid=(N_CHIPS,),
            in_specs=[pl.BlockSpec((kk, m), lambda i, s: (0, 0), pipeline_mode=pl.Buffered(1))] + b_specs,
            out_specs=out_specs, scratch_shapes=scratch),
        out_shape=out_shape, compiler_params=_cparams(56))(place, a16, *b_args)


def _wgrad_rows(place, a16, b16, name):
    kk, m = a16.shape
    n = b16.shape[1]
    block = (m // N_DEV, n)

    def body(pl_ref, a_ref, b_ref, cs_ref, own_ref, at, acc, stage, land, send, recv):
        c = pl_ref[0]
        _transpose_into(at, a_ref)
        acc[...] = _dot(at[...], b_ref[...])

        def rows(owner):
            return pl.ds(pl.multiple_of(owner * block[0], block[0]), block[0])

        swaps = []
        for p in range(N_CHIPS):
            stage[p] = acc[rows(2 * p + 1 - c), :].astype(BF16)
            swaps.append(_swap_with_sibling(p, stage, land, send, recv))
            swaps[-1].start()
        for p in range(N_CHIPS):
            swaps[p].wait()
            total = acc[rows(2 * p + c), :] + land[p].astype(F32)
            cs_ref[p] = total.astype(BF16)

            @pl.when(p == pl_ref[1])
            def _():
                own_ref[...] = total

    vmem = pl.BlockSpec(memory_space=pltpu.VMEM)
    return pl.pallas_call(
        body, name=name,
        in_specs=[pl.BlockSpec(memory_space=pltpu.SMEM), vmem, vmem], out_specs=[vmem, vmem],
        out_shape=[jax.ShapeDtypeStruct((N_CHIPS,) + block, BF16), jax.ShapeDtypeStruct(block, F32)],
        scratch_shapes=[pltpu.VMEM((m, kk), BF16), pltpu.VMEM((m, n), F32)] + _pair_scratch(block),
        compiler_params=pltpu.CompilerParams(vmem_limit_bytes=56 << 20))(place, a16, b16)


def _adamw_math(w, g, m, v):
    m = ADAM_B1 * m + (1.0 - ADAM_B1) * g
    v = ADAM_B2 * v + (1.0 - ADAM_B2) * jnp.square(g)
    m_hat = m / (1.0 - ADAM_B1 ** ADAM_STEP)
    v_hat = v / (1.0 - ADAM_B2 ** ADAM_STEP)
    delta = -ADAM_LR * (m_hat / (jnp.sqrt(v_hat) + ADAM_EPS) + ADAM_WD * w)
    return delta, m, v


def _adamw_shards(updates, name, chip_sums=()):
    names, nu, ns = list(updates), len(updates), len(chip_sums)

    def body(*refs):
        ins, sum_refs = refs[:5 * nu], refs[5 * nu:5 * nu + ns]
        outs = refs[5 * nu + ns:9 * nu + ns]
        landed_refs, scratch = refs[9 * nu + ns:9 * nu + 2 * ns], refs[9 * nu + 2 * ns:]
        if ns:
            start_chips, finish_chips = _chips_steps(sum_refs, landed_refs, *scratch)
            start_chips()
        for i in range(nu):
            o_ref, r_ref, w_ref, m_ref, v_ref = ins[5 * i:5 * i + 5]
            g_out, d_out, m_out, v_out = outs[4 * i:4 * i + 4]
            g = o_ref[...] + r_ref[0].astype(F32) + r_ref[1].astype(F32) + r_ref[2].astype(F32)
            g_out[...] = g
            d_out[...], m_out[...], v_out[...] = _adamw_math(w_ref[...], g, m_ref[...], v_ref[...])
        if ns:
            finish_chips()

    vmem = pl.BlockSpec(memory_space=pltpu.VMEM)
    out = pl.pallas_call(
        body, name=name,
        in_specs=[vmem] * (5 * nu) + [ANY] * ns, out_specs=[vmem] * (4 * nu) + [ANY] * ns,
        out_shape=[jax.ShapeDtypeStruct(updates[n][2].shape, F32) for n in names for _ in range(4)]
        + _chips_shapes(chip_sums),
        scratch_shapes=_chips_scratch(ns) if ns else [],
        compiler_params=pltpu.CompilerParams(vmem_limit_bytes=56 << 20),
    )(*[a for n in names for a in updates[n]], *chip_sums)
    return {n: out[4 * i:4 * i + 4] for i, n in enumerate(names)}, list(out[4 * nu:])


def _place():
    x, y, c = lax.axis_index("x"), lax.axis_index("y"), lax.axis_index("c")
    chips = [(1 - x, y), (x, 1 - y), (1 - x, 1 - y)]
    return x, y, c, chips


def _gather_steps(ins, outs, send, recv, lsem):
    nt = len(ins)
    x, y, c, chips = _place()
    me, sib = (x, y, c), (x, y, 1 - c)

    def slot(t, px, py, pc):
        return outs[t].at[4 * px + 2 * py + pc]

    def copy(t, k, block, to, src=None):
        return pltpu.make_async_remote_copy(
            src_ref=slot(t, *block) if src is None else src, dst_ref=slot(t, *block),
            send_sem=send.at[t, k], recv_sem=recv.at[t, k], device_id=to, device_id_type=MESH)

    mine = [pltpu.make_async_copy(ins[t], slot(t, *me), lsem.at[t]) for t in range(nt)]
    first = []
    for t in range(nt):
        first.append(copy(t, 0, me, sib, src=ins[t]))
        first += [copy(t, 1 + j, me, (*chip, c), src=ins[t]) for j, chip in enumerate(chips)]

    def start():
        for cp in mine + first:
            cp.start()

    def finish():
        passed = []
        for j, chip in enumerate(chips):
            for t in range(nt):
                copy(t, 1 + j, (*chip, c), me).wait_recv()
                fwd = copy(t, 4 + j, (*chip, c), sib)
                fwd.start()
                passed.append(fwd)
        for t in range(nt):
            copy(t, 0, sib, me).wait_recv()
            for j, chip in enumerate(chips):
                copy(t, 4 + j, (*chip, 1 - c), me).wait_recv()
        for cp in first + passed:
            cp.wait_send()
        for cp in mine:
            cp.wait()

    return start, finish


def _gather_scratch(nt):
    return [pltpu.SemaphoreType.DMA((nt, 7)), pltpu.SemaphoreType.DMA((nt, 7)), pltpu.SemaphoreType.DMA((nt,))]


def _gathered_shapes(shards):
    return [jax.ShapeDtypeStruct((N_DEV,) + s.shape, s.dtype) for s in shards]


def _call_with_gather(body, n_grid, shards, *, name, in_specs, out_specs, out_shape, scratch_shapes, vmem_mb, args):
    ng, n_in, n_out = len(shards), len(in_specs), len(out_specs)

    def wrapped(*refs):
        ins, shard_refs = refs[:n_in], refs[n_in:n_in + ng]
        outs = refs[n_in + ng:n_in + ng + n_out]
        whole_refs = refs[n_in + ng + n_out:n_in + 2 * ng + n_out]
        scratch = refs[n_in + 2 * ng + n_out:]
        if ng:
            start, finish = _gather_steps(shard_refs, whole_refs, *scratch[len(scratch_shapes):])
            pl.when(pl.program_id(0) == 0)(start)
        body(*ins, *outs, *scratch[:len(scratch_shapes)])
        if ng:
            pl.when(pl.program_id(0) == n_grid - 1)(finish)

    return pl.pallas_call(
        wrapped, grid=(n_grid,), name=name,
        in_specs=list(in_specs) + [ANY] * ng, out_specs=list(out_specs) + [ANY] * ng,
        out_shape=list(out_shape) + _gathered_shapes(shards),
        scratch_shapes=list(scratch_shapes) + (_gather_scratch(ng) if ng else []),
        compiler_params=_cparams(vmem_mb))(*args, *shards)


def _chips_steps(ins, outs, send, recv):
    _, _, c, chips = _place()
    copies = [pltpu.make_async_remote_copy(
        src_ref=ins[t].at[2 * px + py], dst_ref=outs[t].at[j], send_sem=send.at[t, j], recv_sem=recv.at[t, j],
        device_id=(px, py, c), device_id_type=MESH) for t in range(len(ins)) for j, (px, py) in enumerate(chips)]

    def start():
        for cp in copies:
            cp.start()

    def finish():
        for cp in copies:
            cp.wait()

    return start, finish


def _chips_scratch(nt):
    return [pltpu.SemaphoreType.DMA((nt, 3)), pltpu.SemaphoreType.DMA((nt, 3))]


def _chips_shapes(cs16s):
    return [jax.ShapeDtypeStruct((3,) + g.shape[1:], g.dtype) for g in cs16s]


SMALL = (("g_pre_mix", 0, 0, D), ("g_mem", 1, 0, D), ("g_post_mix", 2, 0, D), ("g_attn_out", 3, 0, AW),
         ("g_conv_out", 3, AW, CW), ("g_xattn_out", 3, AW + CW, XW), ("g_post_mlp", 4, 0, D), ("g_pre_mlp", 5, 0, D))
CONV_ROW = 8
PACK_ROWS = 16


LOSS_ROW = 15


def _small_all_reduce(dg_in, dg_mem, dgs, dg_mlp, dcw, loss8):
    def body(acc_in, acc_mem, acc_mix, acc_mlp, acc_cw, acc_loss, tot_ref, pack, land, send, recv):
        x, y, c, _ = _place()
        me = 4 * x + 2 * y + c
        pack[...] = jnp.zeros_like(pack)
        pack[0:1, :] = acc_in[0:1, :]
        pack[1:2, :] = acc_mem[0:1, :]
        pack[2:4, :] = acc_mix[0:2, :]
        pack[4:6, :] = acc_mlp[0:2, :]
        pack[CONV_ROW:CONV_ROW + 3, 0:CW] = acc_cw[0:3, :]
        pack[LOSS_ROW:LOSS_ROW + 1, 0:LANES] = acc_loss[0:1, :]
        land[me] = pack[...]
        copies = []
        for k in range(1, N_DEV):
            kx, ky, kc = (k >> 2) & 1, (k >> 1) & 1, k & 1
            peer = (1 - x if kx else x, 1 - y if ky else y, 1 - c if kc else c)
            copies.append(pltpu.make_async_remote_copy(
                src_ref=pack, dst_ref=land.at[me], send_sem=send.at[k - 1], recv_sem=recv.at[k - 1],
                device_id=peer, device_id_type=MESH))
        for cp in copies:
            cp.start()
        for cp in copies:
            cp.wait()
        tot = land[0]
        for s in range(1, N_DEV):
            tot = tot + land[s]
        tot_ref[...] = tot

    return pl.pallas_call(
        body, name="small_all_reduce", out_shape=jax.ShapeDtypeStruct((PACK_ROWS, D), F32),
        scratch_shapes=[pltpu.VMEM((PACK_ROWS, D), F32), pltpu.VMEM((N_DEV, PACK_ROWS, D), F32),
                        pltpu.SemaphoreType.DMA((N_DEV - 1,)), pltpu.SemaphoreType.DMA((N_DEV - 1,))],
    )(dg_in, dg_mem, dgs, dg_mlp, dcw, loss8)


def _small_update(tot, me, params):
    flat = [a for n, _, _, _ in SMALL for a in params[n]] + list(params["conv_w"])
    n_par = len(SMALL) + 1
    tap_cols = CW // N_DEV

    def body(*refs):
        me_ref, tot_ref = refs[0:2]
        ins = refs[2:2 + 3 * n_par]
        loss_out = refs[2 + 3 * n_par]
        outs = refs[3 + 3 * n_par:]
        tot = tot_ref[...]
        loss_out[...] = jnp.broadcast_to(tot[LOSS_ROW:LOSS_ROW + 1, 0:LANES], loss_out.shape)

        def update(i, g):
            w_ref, m_ref, v_ref = ins[3 * i:3 * i + 3]
            g_out, d_out, m_out, v_out = outs[4 * i:4 * i + 4]
            g_out[...] = g
            d_out[...], m_out[...], v_out[...] = _adamw_math(w_ref[...], g, m_ref[...], v_ref[...])

        for i, (_, row, lane0, width) in enumerate(SMALL):
            update(i, tot[row:row + 1, lane0:lane0 + width])
        me = me_ref[0]
        taps = pltpu.roll(tot[CONV_ROW:CONV_ROW + SUBLANES, 0:CW], jnp.where(me == 0, 0, CW - me * tap_cols), 1)
        update(n_par - 1, taps[0:3, 0:tap_cols])

    shapes = [jax.ShapeDtypeStruct(params[n][0].shape, F32) for n, _, _, _ in SMALL] + [
        jax.ShapeDtypeStruct(params["conv_w"][0].shape, F32)]
    vmem = pl.BlockSpec(memory_space=pltpu.VMEM)
    loss, *out = pl.pallas_call(
        body, name="small_update",
        in_specs=[pl.BlockSpec(memory_space=pltpu.SMEM)] + [vmem] * (1 + 3 * n_par),
        out_shape=[jax.ShapeDtypeStruct((SUBLANES, LANES), F32)] + [s for s in shapes for _ in range(4)],
    )(me, tot, *flat)
    names = [n for n, _, _, _ in SMALL] + ["conv_w"]
    return loss[0, 0], {n: out[4 * i:4 * i + 4] for i, n in enumerate(names)}


def _local_step(x, mem, pos, gains, shards, tgt, place):
    half = HEAD // 2
    inv_freq = jnp.float32(ROPE_THETA) ** (-(jnp.arange(half, dtype=F32) * 2.0 / HEAD))
    invf = jnp.tile(inv_freq, LANES // half)[None, :]
    sgn = jnp.tile(jnp.concatenate([-jnp.ones((half,), F32), jnp.ones((half,), F32)]), LANES // HEAD)[None, :]
    cos, sins, win8 = _rope_table(pos.astype(F32).reshape(S, 1), invf, sgn, [shards["w_in"]])
    q, kvp, bcu, qx16, h16, win16, wout8, wkv8, conv8 = _in_proj(
        x, gains["g_pre_mix"], win8, cos, sins, [shards["w_out"], shards["w_mem_kv"], shards["conv_w"]])
    wout16, wkv16 = wout8.reshape(D, D), wkv8.reshape(D, 2 * XW)
    cw_full = conv8[:, 0:3, 0:CW // N_DEV].transpose(1, 0, 2).reshape(3, CW)
    cw8 = jnp.zeros((SUBLANES, CW), F32).at[0:3].set(cw_full)
    y_attn, ltot, wup8, wdn8 = _attn_fwd(q, kvp, [shards["w_up"], shards["w_down"]])
    wdn16 = wdn8.reshape(FF, D)
    memn16, kv16 = _mem_fwd(mem, gains["g_mem"], wkv16)
    ypre, y16, y2, x1 = _mix_out(y_attn, bcu, qx16, kv16, cw8, gains["g_attn_out"], gains["g_conv_out"],
                                 gains["g_xattn_out"], gains["g_post_mix"], wout16, x, [])
    a16, du16, h2_16, df2_16, dx1, loss8, dg_mlp = _mlp(x1, tgt, gains["g_pre_mlp"], gains["g_post_mlp"], wup8, wdn16)

    sums = {"w_up": _wgrad_cols(place, h2_16, du16, FF_BLK, "wgrad_up"),
            "w_down": _wgrad_cols(place, df2_16, a16, FF_BLK, "wgrad_down", square_b=True, transpose_out=True)}

    head_id = jnp.arange(AW, dtype=jnp.int32) // HEAD
    head_ones = (head_id[:, None] == head_id[None, :]).astype(BF16)
    dy2_16, qdo, ld, dbcu, dqx, dgs, dcw, dkv = _mix_out_bwd(
        dx1, y2, ypre, ltot, head_ones, q, bcu, qx16, kv16, cw8, gains["g_post_mix"], gains["g_attn_out"],
        gains["g_conv_out"], gains["g_xattn_out"], wout16)
    dkv16, dg_mem = _mem_bwd(mem, gains["g_mem"], wkv16, dkv)
    sums["w_mem_kv"] = _wgrad_rows(place, memn16, dkv16, "wgrad_mem_kv")
    sums["w_out"] = _wgrad_rows(place, y16, dy2_16, "wgrad_out")
    out = _attn_bwd(qdo, kvp, ld, [s[0] for s in sums.values()])
    dqkv, landed = out[:9], out[9:]
    reduced = {n: (s[1], landed[t]) for t, (n, s) in enumerate(sums.items())}
    dproj16, grad_x, dg_in = _in_proj_bwd(dqkv, dbcu, dqx, cos, sins, win16, x, gains["g_pre_mix"], dx1)

    _, in_own, in_landed = _wgrad_cols(place, h16, dproj16, PW // N_DEV, "wgrad_in", to_chips=True)
    reduced["w_in"] = (in_own, in_landed)
    return grad_x, reduced, (dg_in, dg_mem, dgs, dg_mlp, dcw, loss8)


BIG = ("w_in", "w_mem_kv", "w_out", "w_up", "w_down")
ORDER = ("g_pre_mix", "g_mem", "w_in", "w_mem_kv", "conv_w", "g_attn_out", "g_conv_out", "g_xattn_out", "w_out",
         "g_post_mix", "g_pre_mlp", "w_up", "w_down", "g_post_mlp")


def kernel(x, mem, positions, g_pre_mix, g_mem, w_in, w_mem_kv, conv_w, g_attn_out, g_conv_out, g_xattn_out, w_out, g_post_mix, g_pre_mlp, w_up, w_down, g_post_mlp, loss_target, m_g_pre_mix, m_g_mem, m_w_in, m_w_mem_kv, m_conv_w, m_g_attn_out, m_g_conv_out, m_g_xattn_out, m_w_out, m_g_post_mix, m_g_pre_mlp, m_w_up, m_w_down, m_g_post_mlp, v_g_pre_mix, v_g_mem, v_w_in, v_w_mem_kv, v_conv_w, v_g_attn_out, v_g_conv_out, v_g_xattn_out, v_w_out, v_g_post_mix, v_g_pre_mlp, v_w_up, v_w_down, v_g_post_mlp):
    w = dict(g_pre_mix=g_pre_mix, g_mem=g_mem, w_in=w_in, w_mem_kv=w_mem_kv, conv_w=conv_w, g_attn_out=g_attn_out,
             g_conv_out=g_conv_out, g_xattn_out=g_xattn_out, w_out=w_out, g_post_mix=g_post_mix, g_pre_mlp=g_pre_mlp,
             w_up=w_up, w_down=w_down, g_post_mlp=g_post_mlp)
    mo = dict(g_pre_mix=m_g_pre_mix, g_mem=m_g_mem, w_in=m_w_in, w_mem_kv=m_w_mem_kv, conv_w=m_conv_w,
              g_attn_out=m_g_attn_out, g_conv_out=m_g_conv_out, g_xattn_out=m_g_xattn_out, w_out=m_w_out,
              g_post_mix=m_g_post_mix, g_pre_mlp=m_g_pre_mlp, w_up=m_w_up, w_down=m_w_down, g_post_mlp=m_g_post_mlp)
    vo = dict(g_pre_mix=v_g_pre_mix, g_mem=v_g_mem, w_in=v_w_in, w_mem_kv=v_w_mem_kv, conv_w=v_conv_w,
              g_attn_out=v_g_attn_out, g_conv_out=v_g_conv_out, g_xattn_out=v_g_xattn_out, w_out=v_w_out,
              g_post_mix=v_g_post_mix, g_pre_mlp=v_g_pre_mlp, w_up=v_w_up, w_down=v_w_down, g_post_mlp=v_g_post_mlp)

    xi, yi, ci = lax.axis_index("x"), lax.axis_index("y"), lax.axis_index("c")
    me = 4 * xi + 2 * yi + ci
    place = jnp.stack([ci, 2 * xi + yi]).astype(jnp.int32)

    shards = {n: w[n][0].astype(BF16) for n in BIG}
    shards["conv_w"] = jnp.zeros((SUBLANES, LANES), F32).at[0:3, 0:CW // N_DEV].set(conv_w[0])

    gains = {n: w[n] for n, _, _, _ in SMALL}
    grad_x, reduced, small_acc = _local_step(x[0], mem[0], positions[0], gains, shards, loss_target[0], place)

    updated = {}
    for group in (("w_up", "w_down"), ("w_in", "w_out", "w_mem_kv")):
        updated.update(_adamw_shards({n: (*reduced[n], w[n][0], mo[n][0], vo[n][0]) for n in group},
                                     "adamw_" + "_".join(group))[0])
    grad, delta, new_m, new_v = {}, {}, {}, {}
    for n, (g, d_, m_, v_) in updated.items():
        grad[n], delta[n], new_m[n], new_v[n] = g[None], d_[None], m_[None], v_[None]

    params = {n: (w[n], mo[n], vo[n]) for n, _, _, _ in SMALL}
    params["conv_w"] = (w["conv_w"][0], mo["conv_w"][0], vo["conv_w"][0])
    loss, small = _small_update(_small_all_reduce(*small_acc), me.reshape(1).astype(jnp.int32), params)
    for n, (g, d_, m_, v_) in small.items():
        lead = (lambda a: a[None]) if n == "conv_w" else (lambda a: a)
        grad[n], delta[n], new_m[n], new_v[n] = lead(g), lead(d_), lead(m_), lead(v_)

    return (loss, grad_x[None], *[grad[n] for n in ORDER], *[delta[n] for n in ORDER],
            *[new_m[n] for n in ORDER], *[new_v[n] for n in ORDER])
```

```python
import functools

import numpy as np
import jax
import jax.numpy as jnp
from jax import lax
from jax.experimental import pallas as pl
from jax.experimental.pallas import tpu as pltpu

F32, BF16 = jnp.float32, jnp.bfloat16
MESH = pl.DeviceIdType.MESH
ANY = pl.BlockSpec(memory_space=pl.ANY)

N_DEV = 8
D = 1024
S = 4096
N_MEM = 256
HEAD = 64
AW, CW, XW = 512, 256, 256
PW = 3 * AW + 3 * CW + XW
FF = 4096
FF_BLK = FF // N_DEV
PATTERNS = ((128, 1), (512, 4), (2048, 16))
QB = 128
EPS = 1e-6
NEG = -1e30
SCALE = HEAD ** -0.5
ROPE_THETA = 10000.0
LANES = 128
SUBLANES = 8

ADAM_LR, ADAM_B1, ADAM_B2, ADAM_EPS, ADAM_WD, ADAM_STEP = 0.001, 0.9, 0.999, 1e-08, 0.01, 10

TQ = 512
TQ_MLP = 256
NT = S // TQ


def _cparams(vmem_mb, n_grid=1):
    return pltpu.CompilerParams(dimension_semantics=("arbitrary",) * n_grid, vmem_limit_bytes=vmem_mb << 20)


def _const(shape):
    nd = len(shape)
    return pl.BlockSpec(shape, lambda *_: (0,) * nd, pipeline_mode=pl.Buffered(1))


def _acc(shape):
    nd = len(shape)
    return pl.BlockSpec(shape, lambda *_: (0,) * nd)


def _dot(a, b):
    return jnp.dot(a, b, preferred_element_type=F32)


def _dot_nt(a, b):
    return lax.dot_general(a, b, (((1,), (1,)), ((), ())), preferred_element_type=F32)


def _dot_tn(a, b):
    return lax.dot_general(a, b, (((0,), (0,)), ((), ())), preferred_element_type=F32)


def _rms(x, g):
    r = lax.rsqrt(jnp.mean(x * x, axis=-1, keepdims=True) + EPS)
    n = x * r
    return n * g, n, r


def _rms_bwd(dy, n, r, g):
    dn = dy * g
    dx = r * (dn - n * jnp.mean(dn * n, axis=-1, keepdims=True))
    return dx, jnp.sum(dy * n, axis=0, keepdims=True)


def _rot_half(t):
    lane = lax.broadcasted_iota(jnp.int32, t.shape, 1)
    n = t.shape[1]
    return jnp.where((lane % HEAD) < HEAD // 2, pltpu.roll(t, n - HEAD // 2, 1), pltpu.roll(t, HEAD // 2, 1))


def _rope_table(pos_col, invf, sgn, shards):
    def body(p_ref, f_ref, s_ref, c_out, s_out):
        ang = p_ref[...] * f_ref[...]
        c_out[...] = jnp.cos(ang)
        s_out[...] = jnp.sin(ang) * s_ref[...]

    tile = pl.BlockSpec((TQ, LANES), lambda i: (i, 0))
    return _call_with_gather(
        body, NT, shards, name="rope_table",
        in_specs=[pl.BlockSpec((TQ, 1), lambda i: (i, 0)), _const((1, LANES)), _const((1, LANES))],
        out_specs=[tile, tile], out_shape=[jax.ShapeDtypeStruct((S, LANES), F32)] * 2,
        scratch_shapes=[], vmem_mb=32, args=(pos_col, invf, sgn))


def _all_heads(t):
    return jnp.tile(t, (1, AW // LANES))


def _mem_fwd(mem, g_mem, wkv16):
    def body(m_ref, g_ref, w_ref, n16_ref, kv_ref):
        y, _, _ = _rms(m_ref[...], g_ref[...])
        y16 = y.astype(BF16)
        n16_ref[...] = y16
        kv_ref[...] = _dot(y16, w_ref[...]).astype(BF16)

    return pl.pallas_call(
        body, name="mem_fwd",
        out_shape=[jax.ShapeDtypeStruct((N_MEM, D), BF16), jax.ShapeDtypeStruct((N_MEM, 2 * XW), BF16)],
        compiler_params=pltpu.CompilerParams(vmem_limit_bytes=32 << 20))(mem, g_mem, wkv16)


def _in_proj(x, g, w8, cos, sins, shards):
    blk = PW // N_DEV

    def body(x_ref, g_ref, w8_ref, c_ref, s_ref, q_ref, kv_ref, bcu_ref, qx_ref, h_ref, w_out, w_ref):
        @pl.when(pl.program_id(0) == 0)
        def _():
            for j in range(N_DEV):
                w_ref[:, j * blk:(j + 1) * blk] = w8_ref[j]
            w_out[...] = w_ref[...]

        y, _, _ = _rms(x_ref[...], g_ref[...])
        h = y.astype(BF16)
        h_ref[...] = h
        proj = _dot(h, w_ref[...])
        cos, sn = _all_heads(c_ref[...]), _all_heads(s_ref[...])
        q, k = proj[:, 0:AW], proj[:, AW:2 * AW]
        q_ref[...] = (q * cos + _rot_half(q) * sn) * SCALE
        kv_ref[...] = _pack_pair(k * cos + _rot_half(k) * sn, proj[:, 2 * AW:3 * AW])
        bcu_ref[...] = proj[:, 3 * AW:3 * AW + 3 * CW]
        qx_ref[...] = (proj[:, 3 * AW + 3 * CW:] * SCALE).astype(BF16)

    def tile(w):
        return pl.BlockSpec((TQ, w), lambda i: (i, 0))

    return _call_with_gather(
        body, NT, shards, name="in_proj",
        in_specs=[tile(D), _const((1, D)), _const((N_DEV, D, blk)), tile(LANES), tile(LANES)],
        out_specs=[tile(AW), tile(AW), tile(3 * CW), tile(XW), tile(D), _acc((D, PW))],
        out_shape=[jax.ShapeDtypeStruct((S, AW), F32)] * 2 + [
            jax.ShapeDtypeStruct((S, 3 * CW), F32), jax.ShapeDtypeStruct((S, XW), BF16),
            jax.ShapeDtypeStruct((S, D), BF16), jax.ShapeDtypeStruct((D, PW), BF16)],
        scratch_shapes=[pltpu.VMEM((D, PW), BF16)], vmem_mb=56, args=(x, g, w8, cos, sins))


ATTN_PLANS = (("p1", 1, 128, 32), ("p4", 8, 64, 8), ("p16", 16, 128, 2))
PAD = 128
WIN = 256


ATTN_UNROLL = 8


def _fill_bias(tab, qblk, partner):
    qi = lax.broadcasted_iota(jnp.int32, (2 * qblk, WIN), 0) & (qblk - 1)
    kj = lax.broadcasted_iota(jnp.int32, (2 * qblk, WIN), 1)
    piece = kj >> (qblk.bit_length() - 1)
    kk = kj & (qblk - 1)
    prev = (piece & 1) == 0
    of_partner = piece >= 2
    for first in (0, 1):
        for par in (0, 1):
            lo = jnp.where(prev, (qblk if first else qi) + jnp.where(of_partner, par, 0), 0)
            hi = jnp.where(prev, qblk, qi + jnp.where(of_partner, par - 1, 0))
            tab[2 * first + par] = jnp.where((kk >= lo) & (kk <= hi), 0.0, NEG).astype(F32)


def _block_rows(g, qblk, nbc, partner):
    own = pl.ds(pl.multiple_of(PAD + g * qblk, qblk), qblk)
    first = ((g & (nbc - 1)) == 0).astype(jnp.int32)
    if partner:
        gp = jnp.bitwise_xor(g, 4 * nbc)
        wins = (pl.ds(pl.multiple_of(PAD + (g - 1) * qblk, qblk), 2 * qblk),
                pl.ds(pl.multiple_of(PAD + (gp - 1) * qblk, qblk), 2 * qblk))
        return own, wins, 2 * first + ((g >> ((4 * nbc).bit_length() - 1)) & 1)
    return own, (pl.ds(pl.multiple_of(PAD + (g - 1) * qblk, qblk), 2 * qblk),), 2 * first


def _pack_pair(lo, hi):
    lo_bits = lax.bitcast_convert_type(lo.astype(BF16).astype(F32), jnp.uint32) >> 16
    hi_bits = lax.bitcast_convert_type(hi.astype(BF16).astype(F32), jnp.uint32) & jnp.uint32(0xFFFF0000)
    return lax.bitcast_convert_type(hi_bits | lo_bits, F32)


def _unpack_pair(c):
    bits = lax.bitcast_convert_type(c, jnp.uint32)
    lo = lax.bitcast_convert_type(bits << 16, F32).astype(BF16)
    hi = lax.bitcast_convert_type(bits & jnp.uint32(0xFFFF0000), F32).astype(BF16)
    return lo, hi


def _window(ref, wins):
    parts = [ref[w, :] for w in wins]
    return parts[0] if len(parts) == 1 else jnp.concatenate(parts, axis=0)


def _stack_heads(t, lane):
    zero = jnp.zeros_like(t)
    return jnp.concatenate([jnp.where(lane < HEAD, t, zero), jnp.where(lane >= HEAD, t, zero)], axis=0)


def _unstack_heads(t2, lane):
    half = t2.shape[0] // 2
    return jnp.where(lane < HEAD, t2[0:half, :], t2[half:, :])


def _lanes_of(step):
    return pl.ds(pl.multiple_of(step * LANES, LANES), LANES)


def _whole_wait(buf, sem):
    whole = buf.at[pl.ds(PAD, S), :]
    return pltpu.make_async_copy(whole, whole, sem)


def _whole_waits(bufs, sems):
    return [_whole_wait(buf, sems.at[i]) for i, buf in enumerate(bufs)]


def _class_gather(views, bufs, sems, lanes):
    copies = []
    for i, (view, buf) in enumerate(zip(views, bufs)):
        if view.ndim == 2:
            copies.append(pltpu.make_async_copy(view.at[:, lanes], buf.at[pl.ds(PAD, S), :], sems.at[i]))
        else:
            per, n_cls = view.shape[0], view.shape[1]
            copies += [pltpu.make_async_copy(view.at[:, c, lanes], buf.at[pl.ds(PAD + c * per, per), :], sems.at[i])
                       for c in range(n_cls)]
    return copies


def _class_scatter(bufs, dsts, sems, lanes=None):
    copies = []
    for i, (buf, dst) in enumerate(zip(bufs, dsts)):
        if dst.ndim == 2:
            copies.append(pltpu.make_async_copy(buf.at[pl.ds(PAD, S), :], dst.at[:, lanes], sems.at[i]))
            continue
        per, n_cls = dst.shape[0], dst.shape[1]
        for c in range(n_cls):
            to = dst.at[:, c, :] if lanes is None else dst.at[:, c, lanes]
            copies.append(pltpu.make_async_copy(buf.at[pl.ds(PAD + c * per, per), :], to, sems.at[i]))
    return copies


def _start(copies):
    for cp in copies:
        cp.start()


def _wait(waits):
    for w in waits:
        w.wait()


def _attn_fwd(q, kvp, shards=()):
    views = [[a] + [a.reshape(S // n, n, AW) for _, n, _, _ in ATTN_PLANS[1:]] for a in (q, kvp)]
    flat = [views[a][p] for p in range(3) for a in range(2)]
    ng = len(shards)
    n_grid = AW // LANES

    def body(*refs):
        hbm = [refs[2 * p:2 * p + 2] for p in range(3)]
        refs = refs[6:]
        shard_refs, refs = refs[:ng], refs[ng:]
        y_ref, lt_ref = refs[0:2]
        whole_refs, refs = refs[2:2 + ng], refs[2 + ng:]
        bufs = [refs[2 * p:2 * p + 2] for p in range(3)]
        oc4, lc4, oc16, lc16, tab128, tab4, sem_in = refs[6:13]
        step = pl.program_id(0)
        if ng:
            start_gather, finish_gather = _gather_steps(shard_refs, whole_refs, *refs[13:])
            pl.when(step == 0)(start_gather)
        now = [_class_gather(hbm[p], bufs[p], sem_in.at[p], _lanes_of(step)) for p in range(3)]
        nxt = [_class_gather(hbm[p], bufs[p], sem_in.at[p], _lanes_of(step + 1)) for p in range(3)]

        @pl.when(step == 0)
        def _():
            for p in range(3):
                _start(now[p])
                for b in bufs[p]:
                    b[0:PAD, :] = jnp.zeros((PAD, LANES), F32)
            _fill_bias(tab128, 128, False)
            _fill_bias(tab4, 64, True)

        def prefetch(p):
            pl.when(step + 1 < n_grid)(lambda: _start(nxt[p]))

        lane = lax.broadcasted_iota(jnp.int32, (1, LANES), 1)
        ones = jnp.ones((WIN, LANES), BF16)

        def run(plan, bq, bkv, tab, o_dst, l_dst, dst_pad):
            _, n_cls, qblk, nbc = plan
            partner = n_cls == 8

            def block(g, carry):
                own, wins, mask = _block_rows(g, qblk, nbc, partner)
                q2 = _stack_heads(bq[own, :].astype(BF16), lane)
                kw, vwin = _unpack_pair(_window(bkv, wins))
                vw = jnp.concatenate([vwin, ones], axis=1)
                s = _dot_nt(q2, kw) + tab[mask]
                m = jnp.max(s, axis=1, keepdims=True)
                oe = _dot(jnp.exp(s - m).astype(BF16), vw)
                den = oe[:, LANES:]
                dst = pl.ds(pl.multiple_of(dst_pad + g * qblk, qblk), qblk)
                o_dst[dst, :] = _unstack_heads(oe[:, 0:LANES] / den, lane)
                l_dst[dst, :] = _unstack_heads(m + jnp.log(den), lane)
                return carry
            lax.fori_loop(0, n_cls * nbc, block, 0, unroll=ATTN_UNROLL)

        _wait(_whole_waits(bufs[0], sem_in.at[0]))
        run(ATTN_PLANS[0], *bufs[0], tab128, y_ref, lt_ref, 0)
        prefetch(0)
        _wait(_whole_waits(bufs[1], sem_in.at[1]))
        run(ATTN_PLANS[1], *bufs[1], tab4, oc4, lc4, PAD)
        prefetch(1)
        _wait(_whole_waits(bufs[2], sem_in.at[2]))
        run(ATTN_PLANS[2], *bufs[2], tab128, oc16, lc16, PAD)
        prefetch(2)

        n_rows = 64

        def token_order(buf, t, n_cls):
            per = S // n_cls
            first = PAD + t * (n_rows // n_cls)
            return jnp.concatenate([buf[pl.ds(first + jj, n_cls, stride=per), :] for jj in range(n_rows // n_cls)],
                                   axis=0)

        def combine(t, carry):
            rows = pl.ds(pl.multiple_of(t * n_rows, n_rows), n_rows)
            l0, l1, l2 = lt_ref[rows, :], token_order(lc4, t, 8), token_order(lc16, t, 16)
            lm = jnp.maximum(jnp.maximum(l0, l1), l2)
            e0, e1, e2 = jnp.exp(l0 - lm), jnp.exp(l1 - lm), jnp.exp(l2 - lm)
            den = e0 + e1 + e2
            y_ref[rows, :] = (e0 * y_ref[rows, :] + e1 * token_order(oc4, t, 8)
                              + e2 * token_order(oc16, t, 16)) / den
            lt_ref[rows, :] = lm + jnp.log(den)
            return carry
        lax.fori_loop(0, S // n_rows, combine, 0, unroll=2)

        if ng:
            pl.when(step == n_grid - 1)(finish_gather)

    col = pl.BlockSpec((S, LANES), lambda h: (0, h))
    padded = pltpu.VMEM((PAD + S, LANES), F32)
    return pl.pallas_call(
        body, grid=(n_grid,), name="attn_fwd",
        in_specs=[ANY] * (6 + ng), out_specs=[col, col] + [ANY] * ng,
        out_shape=[jax.ShapeDtypeStruct((S, AW), F32)] * 2 + _gathered_shapes(shards),
        scratch_shapes=[padded] * 10 + [
            pltpu.VMEM((4, 256, WIN), F32), pltpu.VMEM((4, 128, WIN), F32), pltpu.SemaphoreType.DMA((3, 2))]
        + (_gather_scratch(ng) if ng else []),
        compiler_params=_cparams(56))(*flat, *shards)


def _conv_taps(z, zprev, row):
    z1 = jnp.where(row == 0, zprev[7:8, :], pltpu.roll(z, 1, 0))
    z2 = jnp.where(row == 0, zprev[6:7, :], jnp.where(row == 1, zprev[7:8, :], pltpu.roll(z, 2, 0)))
    return z1, z2


def _xattn_scores(qm, km):
    s = _dot_nt(qm, km)
    m = jnp.max(s, axis=1, keepdims=True)
    e = jnp.exp(s - m)
    return e, jnp.sum(e, axis=1, keepdims=True)


def _mix_out(y_attn, bcu, qx16, kv16, cw8, g_attn, g_conv, g_x, g_post, wout16, x, shards):
    def body(ya_ref, bcu_ref, halo_ref, qx_ref, kv_ref, cw_ref, ga_ref, gc_ref, gx_ref, gp_ref, w_ref, x_ref,
             ypre_ref, y16_ref, y2_ref, x1_ref):
        i = pl.program_id(0)
        bcu = bcu_ref[...]
        b, c, u = bcu[:, 0:CW], bcu[:, CW:2 * CW], bcu[:, 2 * CW:]
        z = c * u
        halo = halo_ref[...]
        zprev = jnp.where(i > 0, halo[:, CW:2 * CW] * halo[:, 2 * CW:], 0.0)
        row = lax.broadcasted_iota(jnp.int32, z.shape, 0)
        z1, z2 = _conv_taps(z, zprev, row)
        cw = cw_ref[...]
        y_conv = b * (z2 * cw[0:1, :] + z1 * cw[1:2, :] + z * cw[2:3, :])

        qx = qx_ref[...]
        kv = kv_ref[...]
        km, vm = kv[:, 0:XW], kv[:, XW:]
        lane = lax.broadcasted_iota(jnp.int32, qx.shape, 1)
        y_x = jnp.zeros(qx.shape, F32)
        for h in range(XW // HEAD):
            hm = (lane >= h * HEAD) & (lane < (h + 1) * HEAD)
            e, l = _xattn_scores(jnp.where(hm, qx, jnp.zeros_like(qx)), km)
            y_x = jnp.where(hm, _dot(e.astype(BF16), vm) / l, y_x)

        y_attn = ya_ref[...]
        ypre_ref[:, 0:AW] = y_attn
        ypre_ref[:, AW:AW + CW] = y_conv
        ypre_ref[:, AW + CW:] = y_x
        y = jnp.concatenate([_rms(y_attn, ga_ref[...])[0], _rms(y_conv, gc_ref[...])[0],
                             _rms(y_x, gx_ref[...])[0]], axis=1).astype(BF16)
        y16_ref[...] = y
        y2 = _dot(y, w_ref[...])
        y2_ref[...] = y2
        x1_ref[...] = x_ref[...] + _rms(y2, gp_ref[...])[0]

    def tile(w):
        return pl.BlockSpec((TQ, w), lambda i: (i, 0))

    halo = pl.BlockSpec((SUBLANES, 3 * CW), lambda i: (jnp.maximum(i * (TQ // SUBLANES) - 1, 0), 0))
    return _call_with_gather(
        body, NT, shards, name="mix_out",
        in_specs=[tile(AW), tile(3 * CW), halo, tile(XW), _const((N_MEM, 2 * XW)), _const((SUBLANES, CW)),
                  _const((1, AW)), _const((1, CW)), _const((1, XW)), _const((1, D)), _const((D, D)), tile(D)],
        out_specs=[tile(D), tile(D), tile(D), tile(D)],
        out_shape=[jax.ShapeDtypeStruct((S, D), F32), jax.ShapeDtypeStruct((S, D), BF16),
                   jax.ShapeDtypeStruct((S, D), F32), jax.ShapeDtypeStruct((S, D), F32)],
        scratch_shapes=[], vmem_mb=56,
        args=(y_attn, bcu, bcu, qx16, kv16, cw8, g_attn, g_conv, g_x, g_post, wout16, x))


def _mlp(x1, tgt, g_pre, g_post, wup8, wdn_halves):
    tq = TQ_MLP
    half = D // 2

    def body(x1_ref, t_ref, g1_ref, g2_ref, wu_ref, wda_ref, wdb_ref,
             a16_ref, du_ref, h2_ref, df2_ref, dx1_ref, loss_ref, dg_ref, a32):
        @pl.when(pl.program_id(0) == 0)
        def _():
            loss_ref[...] = jnp.zeros_like(loss_ref)
            dg_ref[...] = jnp.zeros_like(dg_ref)

        x1 = x1_ref[...]
        g1, g2 = g1_ref[...], g2_ref[...]
        y1, n1, r1 = _rms(x1, g1)
        h2 = y1.astype(BF16)
        h2_ref[...] = h2
        f2a = jnp.zeros((tq, half), F32)
        f2b = jnp.zeros((tq, half), F32)
        for j in range(N_DEV):
            cols = slice(j * FF_BLK, (j + 1) * FF_BLK)
            a = jnp.maximum(_dot(h2, wu_ref[j]), 0.0)
            a32[:, cols] = a
            a16_ref[:, cols] = a.astype(BF16)
            f = (a * a).astype(BF16)
            f2a = f2a + _dot(f, wda_ref[cols, :])
            f2b = f2b + _dot(f, wdb_ref[cols, :])
        f2 = jnp.concatenate([f2a, f2b], axis=1)
        y2, n2, r2 = _rms(f2, g2)
        e = x1 + y2 - t_ref[...]
        sq = jnp.sum(jnp.sum(e * e, axis=1, keepdims=True), axis=0, keepdims=True)
        loss_ref[...] += jnp.broadcast_to(sq * (0.5 / D), loss_ref.shape)
        dout = e * (1.0 / D)
        df2, dg2 = _rms_bwd(dout, n2, r2, g2)
        df2_16 = df2.astype(BF16)
        df2_ref[...] = df2_16
        dh2 = jnp.zeros((tq, D), F32)
        for j in range(N_DEV):
            cols = slice(j * FF_BLK, (j + 1) * FF_BLK)
            df = _dot_nt(df2_16[:, 0:half], wda_ref[cols, :]) + _dot_nt(df2_16[:, half:], wdb_ref[cols, :])
            du = (df * (2.0 * a32[:, cols])).astype(BF16)
            du_ref[:, cols] = du
            dh2 = dh2 + _dot_nt(du, wu_ref[j])
        dx, dg1 = _rms_bwd(dh2, n1, r1, g1)
        dx1_ref[...] = dout + dx
        dg_ref[0:1, :] += dg2
        dg_ref[1:2, :] += dg1

    def tile(w):
        return pl.BlockSpec((tq, w), lambda i: (i, 0))

    return pl.pallas_call(
        body, grid=(S // tq,), name="mlp",
        in_specs=[tile(D), tile(D), _const((1, D)), _const((1, D)), _const((N_DEV, D, FF_BLK)), _const((FF, half)), _const((FF, half))],
        out_specs=[tile(FF), tile(FF), tile(D), tile(D), tile(D), _acc((SUBLANES, LANES)), _acc((SUBLANES, D))],
        out_shape=[jax.ShapeDtypeStruct((S, FF), BF16), jax.ShapeDtypeStruct((S, FF), BF16),
                   jax.ShapeDtypeStruct((S, D), BF16), jax.ShapeDtypeStruct((S, D), BF16),
                   jax.ShapeDtypeStruct((S, D), F32), jax.ShapeDtypeStruct((SUBLANES, LANES), F32),
                   jax.ShapeDtypeStruct((SUBLANES, D), F32)],
        scratch_shapes=[pltpu.VMEM((tq, FF), F32)],
        compiler_params=_cparams(56))(x1, tgt, g_pre, g_post, wup8, *wdn_halves)


def _mix_out_bwd(dx1, y2, ypre, ltot, head_ones, q, bcu, qx16, kv16, cw8, g_post, g_attn, g_conv, g_x, wout16):
    def body(dx1_ref, y2_ref, ypre_ref, lt_ref, e_ref, q_ref, bcu_ref, halo_ref, qx_ref, kv_ref, cw_ref, gp_ref,
             ga_ref, gc_ref, gx_ref, w_ref, dy2_ref, qdo_ref, ld_ref, dbcu_ref, dqx_ref, dgs_ref, dcw_ref, dkv_ref,
             carry):
        i = pl.program_id(0)

        @pl.when(i == 0)
        def _():
            dgs_ref[...] = jnp.zeros_like(dgs_ref)
            dcw_ref[...] = jnp.zeros_like(dcw_ref)
            dkv_ref[...] = jnp.zeros_like(dkv_ref)
            carry[...] = jnp.zeros_like(carry)

        gp = gp_ref[...]
        _, n, r = _rms(y2_ref[...], gp)
        dy2, dgp = _rms_bwd(dx1_ref[...], n, r, gp)
        dy2_16 = dy2.astype(BF16)
        dy2_ref[...] = dy2_16
        dy = _dot_nt(dy2_16, w_ref[...])

        ypre = ypre_ref[...]
        ga, gc, gx = ga_ref[...], gc_ref[...], gx_ref[...]
        _, na, ra = _rms(ypre[:, 0:AW], ga)
        dya, dga = _rms_bwd(dy[:, 0:AW], na, ra, ga)
        _, nc, rc = _rms(ypre[:, AW:AW + CW], gc)
        dyc, dgc = _rms_bwd(dy[:, AW:AW + CW], nc, rc, gc)
        y_x = ypre[:, AW + CW:]
        _, nx, rx = _rms(y_x, gx)
        dyx, dgx = _rms_bwd(dy[:, AW + CW:], nx, rx, gx)
        qdo_ref[...] = _pack_pair(q_ref[...], dya)
        prod = dya * ypre[:, 0:AW]
        hi = prod.astype(BF16)
        lo = (prod - hi.astype(F32)).astype(BF16)
        head_sum = _dot(hi, e_ref[...]) + _dot(lo, e_ref[...])
        lane_a = lax.broadcasted_iota(jnp.int32, prod.shape, 1)
        ld_ref[...] = jnp.where((lane_a % HEAD) < HEAD // 2, lt_ref[...], head_sum)
        dgs_ref[0:1, :] += dgp
        dgs_ref[1:2, :] += jnp.concatenate([dga, dgc, dgx], axis=1)

        bcu = bcu_ref[...]
        b, c, u = bcu[:, 0:CW], bcu[:, CW:2 * CW], bcu[:, 2 * CW:]
        z = c * u
        halo = halo_ref[...]
        zprev = jnp.where(i < NT - 1, halo[:, CW:2 * CW] * halo[:, 2 * CW:], 0.0)
        row = lax.broadcasted_iota(jnp.int32, z.shape, 0)
        z1, z2 = _conv_taps(z, zprev, row)
        cw = cw_ref[...]
        conv = z2 * cw[0:1, :] + z1 * cw[1:2, :] + z * cw[2:3, :]
        dconv = dyc * b
        nxt = carry[...]
        dn1 = jnp.where(row == TQ - 1, nxt[0:1, :], pltpu.roll(dconv, TQ - 1, 0))
        dn2 = jnp.where(row == TQ - 1, nxt[1:2, :], jnp.where(row == TQ - 2, nxt[0:1, :], pltpu.roll(dconv, TQ - 2, 0)))
        carry[...] = dconv[0:SUBLANES, :]
        dz = dconv * cw[2:3, :] + dn1 * cw[1:2, :] + dn2 * cw[0:1, :]
        dbcu_ref[:, 0:CW] = (dyc * conv).astype(BF16)
        dbcu_ref[:, CW:2 * CW] = (dz * u).astype(BF16)
        dbcu_ref[:, 2 * CW:] = (dz * c).astype(BF16)
        dcw_ref[0:1, :] += jnp.sum(z2 * dconv, axis=0, keepdims=True)
        dcw_ref[1:2, :] += jnp.sum(z1 * dconv, axis=0, keepdims=True)
        dcw_ref[2:3, :] += jnp.sum(z * dconv, axis=0, keepdims=True)

        qx = qx_ref[...]
        kv = kv_ref[...]
        km, vm = kv[:, 0:XW], kv[:, XW:]
        lane = lax.broadcasted_iota(jnp.int32, qx.shape, 1)
        dqx = jnp.zeros(qx.shape, F32)
        dkm = jnp.zeros((N_MEM, XW), F32)
        dvm = jnp.zeros((N_MEM, XW), F32)
        for h in range(XW // HEAD):
            hm = (lane >= h * HEAD) & (lane < (h + 1) * HEAD)
            qm = jnp.where(hm, qx, jnp.zeros_like(qx))
            e, l = _xattn_scores(qm, km)
            p = e / l
            dom = jnp.where(hm, dyx, 0.0)
            do16 = dom.astype(BF16)
            dsum = jnp.sum(dom * y_x, axis=1, keepdims=True)
            ds = (p * (_dot_nt(do16, vm) - dsum)).astype(BF16)
            dqx = jnp.where(hm, _dot(ds, km), dqx)
            dkm = dkm + _dot_tn(ds, qm)
            dvm = dvm + _dot_tn(p.astype(BF16), do16)
        dqx_ref[...] = (dqx * SCALE).astype(BF16)
        dkv_ref[:, 0:XW] += dkm
        dkv_ref[:, XW:] += dvm

    def tile(w):
        return pl.BlockSpec((TQ, w), lambda i: (NT - 1 - i, 0))

    halo = pl.BlockSpec((SUBLANES, 3 * CW), lambda i: (jnp.maximum((NT - 1 - i) * (TQ // SUBLANES) - 1, 0), 0))
    return pl.pallas_call(
        body, grid=(NT,), name="mix_out_bwd",
        in_specs=[tile(D), tile(D), tile(D), tile(AW), _const((AW, AW)), tile(AW), tile(3 * CW), halo, tile(XW),
                  _const((N_MEM, 2 * XW)), _const((SUBLANES, CW)), _const((1, D)), _const((1, AW)), _const((1, CW)),
                  _const((1, XW)), _const((D, D))],
        out_specs=[tile(D), tile(AW), tile(AW), tile(3 * CW), tile(XW), _acc((SUBLANES, D)), _acc((SUBLANES, CW)),
                   _acc((N_MEM, 2 * XW))],
        out_shape=[jax.ShapeDtypeStruct((S, D), BF16), jax.ShapeDtypeStruct((S, AW), F32),
                   jax.ShapeDtypeStruct((S, AW), F32),
                   jax.ShapeDtypeStruct((S, 3 * CW), BF16), jax.ShapeDtypeStruct((S, XW), BF16),
                   jax.ShapeDtypeStruct((SUBLANES, D), F32), jax.ShapeDtypeStruct((SUBLANES, CW), F32),
                   jax.ShapeDtypeStruct((N_MEM, 2 * XW), F32)],
        scratch_shapes=[pltpu.VMEM((SUBLANES, CW), F32)],
        compiler_params=_cparams(56))(dx1, y2, ypre, ltot, head_ones, q, bcu, bcu, qx16, kv16, cw8, g_post, g_attn,
                                      g_conv, g_x, wout16)


def _attn_bwd(qdo, kvp, ld, chip_sums=()):
    n_in = 3
    views = [[a] + [a.reshape(S // n, n, AW) for _, n, _, _ in ATTN_PLANS[1:]] for a in (qdo, kvp, ld)]
    flat = [views[a][p] for p in range(3) for a in range(n_in)]
    ns = len(chip_sums)
    n_grid = AW // LANES

    def body(*refs):
        hbm = [refs[n_in * p:n_in * p + n_in] for p in range(3)]
        refs = refs[3 * n_in:]
        sum_refs, refs = refs[:ns], refs[ns:]
        outs = [refs[3 * p:3 * p + 3] for p in range(3)]
        landed_refs, sc = refs[9:9 + ns], refs[9 + ns:]
        bufs = [sc[3 * p:3 * p + 3] for p in range(3)]
        res = [sc[9 + 3 * p:12 + 3 * p] for p in range(3)]
        tab128, tab4, sem_in, sem_out = sc[18:22]
        step = pl.program_id(0)
        if ns:
            start_chips, finish_chips = _chips_steps(sum_refs, landed_refs, *sc[22:])
            pl.when(step == 0)(start_chips)
        now = [_class_gather(hbm[p], bufs[p], sem_in.at[p], _lanes_of(step)) for p in range(3)]
        nxt = [_class_gather(hbm[p], bufs[p], sem_in.at[p], _lanes_of(step + 1)) for p in range(3)]

        @pl.when(step == 0)
        def _():
            for p in range(3):
                _start(now[p])
                for b in bufs[p]:
                    b[0:PAD, :] = jnp.zeros((PAD, LANES), F32)
            _fill_bias(tab128, 128, False)
            _fill_bias(tab4, 64, True)

        def prefetch(p):
            pl.when(step + 1 < n_grid)(lambda: _start(nxt[p]))

        for p in range(3):
            for b in res[p]:
                b[...] = jnp.zeros_like(b)
        lane = lax.broadcasted_iota(jnp.int32, (1, LANES), 1)

        def run(plan, plan_bufs, tab, dst):
            _, n_cls, qblk, nbc = plan
            partner = n_cls == 8
            bqdo, bkv, bld = plan_bufs
            rq, rk, rv = dst

            def block(g, carry):
                own, wins, mask = _block_rows(g, qblk, nbc, partner)
                qb, dob = _unpack_pair(bqdo[own, :])
                q2, do2 = _stack_heads(qb, lane), _stack_heads(dob, lane)
                kw, vw = _unpack_pair(_window(bkv, wins))
                ldv = bld[own, :]
                half = HEAD // 2
                lt2 = jnp.concatenate([ldv[:, 0:1], ldv[:, HEAD:HEAD + 1]], axis=0)
                dsum2 = jnp.concatenate([ldv[:, half:half + 1], ldv[:, HEAD + half:HEAD + half + 1]], axis=0)
                p = jnp.exp(_dot_nt(q2, kw) + tab[mask] - lt2)
                ds = (p * (_dot_nt(do2, vw) - dsum2)).astype(BF16)
                rq[own, :] = _unstack_heads(_dot(ds, kw), lane)
                dkw = _dot_tn(ds, q2)
                dvw = _dot_tn(p.astype(BF16), do2)
                n_w = WIN // len(wins)
                for i, w in enumerate(wins):
                    rk[w, :] += dkw[i * n_w:(i + 1) * n_w, :]
                    rv[w, :] += dvw[i * n_w:(i + 1) * n_w, :]
                return carry
            lax.fori_loop(0, n_cls * nbc, block, 0, unroll=ATTN_UNROLL)

        tabs = (tab128, tab4, tab128)
        for p in range(3):
            _wait(_whole_waits(bufs[p], sem_in.at[p]))
            run(ATTN_PLANS[p], bufs[p], tabs[p], res[p])
            prefetch(p)
            _start(_class_scatter(res[p], outs[p], sem_out.at[p], _lanes_of(step)))
        for p in range(3):
            _wait(_whole_waits(res[p], sem_out.at[p]))
        if ns:
            pl.when(step == n_grid - 1)(finish_chips)

    padded = pltpu.VMEM((PAD + S, LANES), F32)
    shapes = [jax.ShapeDtypeStruct(views[0][p].shape, F32) for p in range(3) for _ in range(3)]
    out = pl.pallas_call(
        body, grid=(n_grid,), name="attn_bwd",
        in_specs=[ANY] * (3 * n_in + ns), out_specs=[ANY] * (9 + ns),
        out_shape=shapes + _chips_shapes(chip_sums),
        scratch_shapes=[padded] * 18
        + [pltpu.VMEM((4, 256, WIN), F32), pltpu.VMEM((4, 128, WIN), F32),
           pltpu.SemaphoreType.DMA((3, n_in)), pltpu.SemaphoreType.DMA((3, 3))]
        + (_chips_scratch(ns) if ns else []),
        compiler_params=_cparams(56))(*flat, *chip_sums)
    return [o.reshape(S, AW) for o in out[:9]] + list(out[9:])


def _in_proj_bwd(dqkv, dbcu, dqx, cos, sins, w16, x, g, dx1):
    tq = TQ // 2

    def body(*refs):
        parts = refs[0:9]
        dbcu_ref, dqx_ref, c_ref, s_ref, w_ref, x_ref, g_ref, dx1_ref, dp_ref, gx_ref, dg_ref = refs[9:]

        @pl.when(pl.program_id(0) == 0)
        def _():
            dg_ref[...] = jnp.zeros_like(dg_ref)

        dq, dk, dv = (parts[i][...] + parts[3 + i][...] + parts[6 + i][...] for i in range(3))
        cos, sn = _all_heads(c_ref[...]), _all_heads(s_ref[...])
        dqr = dq * SCALE
        dkr = dk
        dp = jnp.concatenate([(dqr * cos + _rot_half(dqr * sn)).astype(BF16),
                              (dkr * cos + _rot_half(dkr * sn)).astype(BF16), dv.astype(BF16),
                              dbcu_ref[...], dqx_ref[...]], axis=1)
        dp_ref[...] = dp
        dh = _dot_nt(dp, w_ref[...])
        g = g_ref[...]
        _, n, r = _rms(x_ref[...], g)
        dx, dg = _rms_bwd(dh, n, r, g)
        gx_ref[...] = dx1_ref[...] + dx
        dg_ref[0:1, :] += dg

    def tile(w):
        return pl.BlockSpec((tq, w), lambda i: (i, 0))

    return pl.pallas_call(
        body, grid=(S // tq,), name="in_proj_bwd",
        in_specs=[tile(AW)] * 9 + [tile(3 * CW), tile(XW), tile(LANES), tile(LANES), _const((D, PW)),
                                   tile(D), _const((1, D)), tile(D)],
        out_specs=[tile(PW), tile(D), _acc((SUBLANES, D))],
        out_shape=[jax.ShapeDtypeStruct((S, PW), BF16), jax.ShapeDtypeStruct((S, D), F32),
                   jax.ShapeDtypeStruct((SUBLANES, D), F32)],
        compiler_params=_cparams(56))(*dqkv, dbcu, dqx, cos, sins, w16, x, g, dx1)


def _mem_bwd(mem, g_mem, wkv16, dkv):
    def body(m_ref, g_ref, w_ref, dkv_ref, dkv16_ref, dg_ref):
        dkv16 = dkv_ref[...].astype(BF16)
        dkv16_ref[...] = dkv16
        _, n, _ = _rms(m_ref[...], g_ref[...])
        dg = jnp.sum(_dot_nt(dkv16, w_ref[...]) * n, axis=0, keepdims=True)
        dg_ref[...] = jnp.broadcast_to(dg, dg_ref.shape)

    return pl.pallas_call(
        body, name="mem_bwd",
        out_shape=[jax.ShapeDtypeStruct((N_MEM, 2 * XW), BF16), jax.ShapeDtypeStruct((SUBLANES, D), F32)],
        compiler_params=pltpu.CompilerParams(vmem_limit_bytes=32 << 20))(mem, g_mem, wkv16, dkv)


N_CHIPS = N_DEV // 2


def _transpose_into(at, a_ref):
    kk = a_ref.shape[0]
    chunk = min(kk, 512)
    for c in range(kk // chunk):
        at[:, c * chunk:(c + 1) * chunk] = a_ref[c * chunk:(c + 1) * chunk, :].T


def _pair_scratch(block):
    return [pltpu.VMEM((N_CHIPS,) + block, BF16), pltpu.VMEM((N_CHIPS,) + block, BF16),
            pltpu.SemaphoreType.DMA((N_CHIPS,)), pltpu.SemaphoreType.DMA((N_CHIPS,))]


def _swap_with_sibling(p, stage, land, send, recv):
    x, y, c = lax.axis_index("x"), lax.axis_index("y"), lax.axis_index("c")
    return pltpu.make_async_remote_copy(src_ref=stage.at[p], dst_ref=land.at[p], send_sem=send.at[p],
                                        recv_sem=recv.at[p], device_id=(x, y, 1 - c), device_id_type=MESH)


def _wgrad_cols(place, a16, b16, blk, name, square_b=False, transpose_out=False, to_chips=False):
    kk, m = a16.shape
    aligned = blk % LANES == 0
    wide = blk if aligned else -(-(blk + LANES // 2) // LANES) * LANES
    block = (blk, m) if transpose_out else (m, blk)

    def chip_of(step, my_chip):
        return (my_chip + 1 + step) & (N_CHIPS - 1) if to_chips else step

    def body(pl_ref, a_ref, *refs):
        b_refs, refs = refs[:2 if aligned else 1], refs[2 if aligned else 1:]
        (cs_ref, own_ref), refs = refs[:2], refs[2:]
        if to_chips:
            landed, refs = refs[0], refs[1:]
        (at, stage, land, send, recv), refs = refs[:5], refs[5:]
        if not aligned:
            (win, wsem), refs = refs[:2], refs[2:]
        step = pl.program_id(0)
        x, y, c = lax.axis_index("x"), lax.axis_index("y"), lax.axis_index("c")
        my_chip = 2 * x + y
        p = chip_of(step, my_chip)

        def fetch(at_step, mine):
            j = 2 * chip_of(at_step, my_chip) + (c if mine else 1 - c)
            first = pl.multiple_of(((j * blk) >> 7) << 7, LANES)
            slot = 2 * (at_step & 1) + mine
            return pltpu.make_async_copy(b_refs[0].at[:, pl.ds(first, wide)], win.at[slot], wsem.at[slot])

        @pl.when(step == 0)
        def _():
            if not aligned:
                fetch(0, 0).start()
                fetch(0, 1).start()
            _transpose_into(at, a_ref)

        if not aligned:
            @pl.when(step + 1 < N_CHIPS)
            def _():
                fetch(step + 1, 0).start()
                fetch(step + 1, 1).start()

        def partial(mine):
            if aligned:
                b = b_refs[mine][...]
                if square_b:
                    b = b * b
                acc = _dot(at[...], b)
            else:
                fetch(step, mine).wait()
                acc = _dot(at[...], win[2 * (step & 1) + mine])
                odd = c if mine else 1 - c
                acc = pltpu.roll(acc, jnp.where(odd == 0, 0, wide - LANES // 2), 1)[:, 0:blk]
            return acc.T if transpose_out else acc

        stage[p] = partial(0).astype(BF16)
        swap = _swap_with_sibling(p, stage, land, send, recv)
        swap.start()
        mine = partial(1)
        swap.wait()
        total = mine + land[p].astype(F32)
        cs_ref[0] = total.astype(BF16)

        @pl.when(p == my_chip)
        def _():
            own_ref[...] = total

        if to_chips:
            stage2, send2, recv2 = refs
            flipped = jnp.bitwise_xor(p, my_chip)
            k = jnp.where(flipped == 2, 0, jnp.where(flipped == 1, 1, 2))

            def to_owner(src, k_, px, py):
                return pltpu.make_async_remote_copy(src_ref=src, dst_ref=landed.at[k_], send_sem=send2.at[k_],
                                                    recv_sem=recv2.at[k_], device_id=(px, py, c), device_id_type=MESH)

            @pl.when(p != my_chip)
            def _():
                stage2[p] = total.astype(BF16)
                to_owner(stage2.at[p], k, p >> 1, p & 1).start()

            @pl.when(step == N_CHIPS - 1)
            def _():
                for k_ in range(N_CHIPS - 1):
                    to_owner(stage2.at[0], k_, x, y).wait()

    def b_spec(mine):
        return pl.BlockSpec((kk, blk), lambda i, s: (0, 2 * chip_of(i, s[1]) + (s[0] if mine else 1 - s[0])))

    b_specs, b_args = ([b_spec(0), b_spec(1)], (b16, b16)) if aligned else ([ANY], (b16,))
    scratch = [pltpu.VMEM((m, kk), BF16)] + _pair_scratch(block)
    if not aligned:
        scratch += [pltpu.VMEM((4, kk, wide), BF16), pltpu.SemaphoreType.DMA((4,))]
    out_specs = [pl.BlockSpec((1,) + block, lambda i, s: (chip_of(i, s[1]), 0, 0)), pl.BlockSpec(block, lambda i, s: (0, 0))]
    out_shape = [jax.ShapeDtypeStruct((N_CHIPS,) + block, BF16), jax.ShapeDtypeStruct(block, F32)]
    if to_chips:
        out_specs.append(ANY)
        out_shape.append(jax.ShapeDtypeStruct((N_CHIPS - 1,) + block, BF16))
        scratch += [pltpu.VMEM((N_CHIPS,) + block, BF16), pltpu.SemaphoreType.DMA((N_CHIPS - 1,)),
                    pltpu.SemaphoreType.DMA((N_CHIPS - 1,))]
    return pl.pallas_call(
        body, name=name,
        grid_spec=pltpu.PrefetchScalarGridSpec(
            num_scalar_prefetch=1, grid=(N_CHIPS,),
            in_specs=[pl.BlockSpec((kk, m), lambda i, s: (0, 0), pipeline_mode=pl.Buffered(1))] + b_specs,
            out_specs=out_specs, scratch_shapes=scratch),
        out_shape=out_shape, compiler_params=_cparams(56))(place, a16, *b_args)


def _wgrad_rows(place, a16, b16, name):
    kk, m = a16.shape
    n = b16.shape[1]
    block = (m // N_DEV, n)

    def body(pl_ref, a_ref, b_ref, cs_ref, own_ref, at, acc, stage, land, send, recv):
        c = pl_ref[0]
        _transpose_into(at, a_ref)
        acc[...] = _dot(at[...], b_ref[...])

        def rows(owner):
            return pl.ds(pl.multiple_of(owner * block[0], block[0]), block[0])

        swaps = []
        for p in range(N_CHIPS):
            stage[p] = acc[rows(2 * p + 1 - c), :].astype(BF16)
            swaps.append(_swap_with_sibling(p, stage, land, send, recv))
            swaps[-1].start()
        for p in range(N_CHIPS):
            swaps[p].wait()
            total = acc[rows(2 * p + c), :] + land[p].astype(F32)
            cs_ref[p] = total.astype(BF16)

            @pl.when(p == pl_ref[1])
            def _():
                own_ref[...] = total

    vmem = pl.BlockSpec(memory_space=pltpu.VMEM)
    return pl.pallas_call(
        body, name=name,
        in_specs=[pl.BlockSpec(memory_space=pltpu.SMEM), vmem, vmem], out_specs=[vmem, vmem],
        out_shape=[jax.ShapeDtypeStruct((N_CHIPS,) + block, BF16), jax.ShapeDtypeStruct(block, F32)],
        scratch_shapes=[pltpu.VMEM((m, kk), BF16), pltpu.VMEM((m, n), F32)] + _pair_scratch(block),
        compiler_params=pltpu.CompilerParams(vmem_limit_bytes=56 << 20))(place, a16, b16)


def _adamw_math(w, g, m, v):
    m = ADAM_B1 * m + (1.0 - ADAM_B1) * g
    v = ADAM_B2 * v + (1.0 - ADAM_B2) * jnp.square(g)
    m_hat = m / (1.0 - ADAM_B1 ** ADAM_STEP)
    v_hat = v / (1.0 - ADAM_B2 ** ADAM_STEP)
    delta = -ADAM_LR * (m_hat / (jnp.sqrt(v_hat) + ADAM_EPS) + ADAM_WD * w)
    return delta, m, v


def _adamw_shards(updates, name, chip_sums=()):
    names, nu, ns = list(updates), len(updates), len(chip_sums)

    def body(*refs):
        ins, sum_refs = refs[:5 * nu], refs[5 * nu:5 * nu + ns]
        outs = refs[5 * nu + ns:9 * nu + ns]
        landed_refs, scratch = refs[9 * nu + ns:9 * nu + 2 * ns], refs[9 * nu + 2 * ns:]
        if ns:
            start_chips, finish_chips = _chips_steps(sum_refs, landed_refs, *scratch)
            start_chips()
        for i in range(nu):
            o_ref, r_ref, w_ref, m_ref, v_ref = ins[5 * i:5 * i + 5]
            g_out, d_out, m_out, v_out = outs[4 * i:4 * i + 4]
            g = o_ref[...] + r_ref[0].astype(F32) + r_ref[1].astype(F32) + r_ref[2].astype(F32)
            g_out[...] = g
            d_out[...], m_out[...], v_out[...] = _adamw_math(w_ref[...], g, m_ref[...], v_ref[...])
        if ns:
            finish_chips()

    vmem = pl.BlockSpec(memory_space=pltpu.VMEM)
    out = pl.pallas_call(
        body, name=name,
        in_specs=[vmem] * (5 * nu) + [ANY] * ns, out_specs=[vmem] * (4 * nu) + [ANY] * ns,
        out_shape=[jax.ShapeDtypeStruct(updates[n][2].shape, F32) for n in names for _ in range(4)]
        + _chips_shapes(chip_sums),
        scratch_shapes=_chips_scratch(ns) if ns else [],
        compiler_params=pltpu.CompilerParams(vmem_limit_bytes=56 << 20),
    )(*[a for n in names for a in updates[n]], *chip_sums)
    return {n: out[4 * i:4 * i + 4] for i, n in enumerate(names)}, list(out[4 * nu:])


def _place():
    x, y, c = lax.axis_index("x"), lax.axis_index("y"), lax.axis_index("c")
    chips = [(1 - x, y), (x, 1 - y), (1 - x, 1 - y)]
    return x, y, c, chips


def _gather_steps(ins, outs, send, recv, lsem):
    nt = len(ins)
    x, y, c, chips = _place()
    me, sib = (x, y, c), (x, y, 1 - c)

    def slot(t, px, py, pc):
        return outs[t].at[4 * px + 2 * py + pc]

    def copy(t, k, block, to, src=None):
        return pltpu.make_async_remote_copy(
            src_ref=slot(t, *block) if src is None else src, dst_ref=slot(t, *block),
            send_sem=send.at[t, k], recv_sem=recv.at[t, k], device_id=to, device_id_type=MESH)

    mine = [pltpu.make_async_copy(ins[t], slot(t, *me), lsem.at[t]) for t in range(nt)]
    first = []
    for t in range(nt):
        first.append(copy(t, 0, me, sib, src=ins[t]))
        first += [copy(t, 1 + j, me, (*chip, c), src=ins[t]) for j, chip in enumerate(chips)]

    def start():
        for cp in mine + first:
            cp.start()

    def finish():
        passed = []
        for j, chip in enumerate(chips):
            for t in range(nt):
                copy(t, 1 + j, (*chip, c), me).wait_recv()
                fwd = copy(t, 4 + j, (*chip, c), sib)
                fwd.start()
                passed.append(fwd)
        for t in range(nt):
            copy(t, 0, sib, me).wait_recv()
            for j, chip in enumerate(chips):
                copy(t, 4 + j, (*chip, 1 - c), me).wait_recv()
        for cp in first + passed:
            cp.wait_send()
        for cp in mine:
            cp.wait()

    return start, finish


def _gather_scratch(nt):
    return [pltpu.SemaphoreType.DMA((nt, 7)), pltpu.SemaphoreType.DMA((nt, 7)), pltpu.SemaphoreType.DMA((nt,))]


def _gathered_shapes(shards):
    return [jax.ShapeDtypeStruct((N_DEV,) + s.shape, s.dtype) for s in shards]


def _call_with_gather(body, n_grid, shards, *, name, in_specs, out_specs, out_shape, scratch_shapes, vmem_mb, args):
    ng, n_in, n_out = len(shards), len(in_specs), len(out_specs)

    def wrapped(*refs):
        ins, shard_refs = refs[:n_in], refs[n_in:n_in + ng]
        outs = refs[n_in + ng:n_in + ng + n_out]
        whole_refs = refs[n_in + ng + n_out:n_in + 2 * ng + n_out]
        scratch = refs[n_in + 2 * ng + n_out:]
        if ng:
            start, finish = _gather_steps(shard_refs, whole_refs, *scratch[len(scratch_shapes):])
            pl.when(pl.program_id(0) == 0)(start)
        body(*ins, *outs, *scratch[:len(scratch_shapes)])
        if ng:
            pl.when(pl.program_id(0) == n_grid - 1)(finish)

    return pl.pallas_call(
        wrapped, grid=(n_grid,), name=name,
        in_specs=list(in_specs) + [ANY] * ng, out_specs=list(out_specs) + [ANY] * ng,
        out_shape=list(out_shape) + _gathered_shapes(shards),
        scratch_shapes=list(scratch_shapes) + (_gather_scratch(ng) if ng else []),
        compiler_params=_cparams(vmem_mb))(*args, *shards)


def _chips_steps(ins, outs, send, recv):
    _, _, c, chips = _place()
    copies = [pltpu.make_async_remote_copy(
        src_ref=ins[t].at[2 * px + py], dst_ref=outs[t].at[j], send_sem=send.at[t, j], recv_sem=recv.at[t, j],
        device_id=(px, py, c), device_id_type=MESH) for t in range(len(ins)) for j, (px, py) in enumerate(chips)]

    def start():
        for cp in copies:
            cp.start()

    def finish():
        for cp in copies:
            cp.wait()

    return start, finish


def _chips_scratch(nt):
    return [pltpu.SemaphoreType.DMA((nt, 3)), pltpu.SemaphoreType.DMA((nt, 3))]


def _chips_shapes(cs16s):
    return [jax.ShapeDtypeStruct((3,) + g.shape[1:], g.dtype) for g in cs16s]


SMALL = (("g_pre_mix", 0, 0, D), ("g_mem", 1, 0, D), ("g_post_mix", 2, 0, D), ("g_attn_out", 3, 0, AW),
         ("g_conv_out", 3, AW, CW), ("g_xattn_out", 3, AW + CW, XW), ("g_post_mlp", 4, 0, D), ("g_pre_mlp", 5, 0, D))
CONV_ROW = 8
PACK_ROWS = 16


LOSS_ROW = 15


def _small_all_reduce(dg_in, dg_mem, dgs, dg_mlp, dcw, loss8):
    def body(acc_in, acc_mem, acc_mix, acc_mlp, acc_cw, acc_loss, tot_ref, pack, land, send, recv):
        x, y, c, _ = _place()
        me = 4 * x + 2 * y + c
        pack[...] = jnp.zeros_like(pack)
        pack[0:1, :] = acc_in[0:1, :]
        pack[1:2, :] = acc_mem[0:1, :]
        pack[2:4, :] = acc_mix[0:2, :]
        pack[4:6, :] = acc_mlp[0:2, :]
        pack[CONV_ROW:CONV_ROW + 3, 0:CW] = acc_cw[0:3, :]
        pack[LOSS_ROW:LOSS_ROW + 1, 0:LANES] = acc_loss[0:1, :]
        land[me] = pack[...]
        copies = []
        for k in range(1, N_DEV):
            kx, ky, kc = (k >> 2) & 1, (k >> 1) & 1, k & 1
            peer = (1 - x if kx else x, 1 - y if ky else y, 1 - c if kc else c)
            copies.append(pltpu.make_async_remote_copy(
                src_ref=pack, dst_ref=land.at[me], send_sem=send.at[k - 1], recv_sem=recv.at[k - 1],
                device_id=peer, device_id_type=MESH))
        for cp in copies:
            cp.start()
        for cp in copies:
            cp.wait()
        tot = land[0]
        for s in range(1, N_DEV):
            tot = tot + land[s]
        tot_ref[...] = tot

    return pl.pallas_call(
        body, name="small_all_reduce", out_shape=jax.ShapeDtypeStruct((PACK_ROWS, D), F32),
        scratch_shapes=[pltpu.VMEM((PACK_ROWS, D), F32), pltpu.VMEM((N_DEV, PACK_ROWS, D), F32),
                        pltpu.SemaphoreType.DMA((N_DEV - 1,)), pltpu.SemaphoreType.DMA((N_DEV - 1,))],
    )(dg_in, dg_mem, dgs, dg_mlp, dcw, loss8)


def _small_update(tot, me, params):
    flat = [a for n, _, _, _ in SMALL for a in params[n]] + list(params["conv_w"])
    n_par = len(SMALL) + 1
    tap_cols = CW // N_DEV

    def body(*refs):
        me_ref, tot_ref = refs[0:2]
        ins = refs[2:2 + 3 * n_par]
        loss_out = refs[2 + 3 * n_par]
        outs = refs[3 + 3 * n_par:]
        tot = tot_ref[...]
        loss_out[...] = jnp.broadcast_to(tot[LOSS_ROW:LOSS_ROW + 1, 0:LANES], loss_out.shape)

        def update(i, g):
            w_ref, m_ref, v_ref = ins[3 * i:3 * i + 3]
            g_out, d_out, m_out, v_out = outs[4 * i:4 * i + 4]
            g_out[...] = g
            d_out[...], m_out[...], v_out[...] = _adamw_math(w_ref[...], g, m_ref[...], v_ref[...])

        for i, (_, row, lane0, width) in enumerate(SMALL):
            update(i, tot[row:row + 1, lane0:lane0 + width])
        me = me_ref[0]
        taps = pltpu.roll(tot[CONV_ROW:CONV_ROW + SUBLANES, 0:CW], jnp.where(me == 0, 0, CW - me * tap_cols), 1)
        update(n_par - 1, taps[0:3, 0:tap_cols])

    shapes = [jax.ShapeDtypeStruct(params[n][0].shape, F32) for n, _, _, _ in SMALL] + [
        jax.ShapeDtypeStruct(params["conv_w"][0].shape, F32)]
    vmem = pl.BlockSpec(memory_space=pltpu.VMEM)
    loss, *out = pl.pallas_call(
        body, name="small_update",
        in_specs=[pl.BlockSpec(memory_space=pltpu.SMEM)] + [vmem] * (1 + 3 * n_par),
        out_shape=[jax.ShapeDtypeStruct((SUBLANES, LANES), F32)] + [s for s in shapes for _ in range(4)],
    )(me, tot, *flat)
    names = [n for n, _, _, _ in SMALL] + ["conv_w"]
    return loss[0, 0], {n: out[4 * i:4 * i + 4] for i, n in enumerate(names)}


def _local_step(x, mem, pos, gains, shards, tgt, place):
    half = HEAD // 2
    inv_freq = jnp.float32(ROPE_THETA) ** (-(jnp.arange(half, dtype=F32) * 2.0 / HEAD))
    invf = jnp.tile(inv_freq, LANES // half)[None, :]
    sgn = jnp.tile(jnp.concatenate([-jnp.ones((half,), F32), jnp.ones((half,), F32)]), LANES // HEAD)[None, :]
    cos, sins, win8 = _rope_table(pos.astype(F32).reshape(S, 1), invf, sgn, [shards["w_in"]])
    wdn_left, wdn_right = shards["w_down"][:, 0:D // 2], shards["w_down"][:, D // 2:]
    q, kvp, bcu, qx16, h16, win16, wout8, wkv8, conv8, wdn8_right = _in_proj(
        x, gains["g_pre_mix"], win8, cos, sins, [shards["w_out"], shards["w_mem_kv"], shards["conv_w"], wdn_right])
    wout16, wkv16 = wout8.reshape(D, D), wkv8.reshape(D, 2 * XW)
    cw_full = conv8[:, 0:3, 0:CW // N_DEV].transpose(1, 0, 2).reshape(3, CW)
    cw8 = jnp.zeros((SUBLANES, CW), F32).at[0:3].set(cw_full)
    y_attn, ltot, wup8, wdn8_left = _attn_fwd(q, kvp, [shards["w_up"], wdn_left])
    wdn_halves = (wdn8_left.reshape(FF, D // 2), wdn8_right.reshape(FF, D // 2))
    memn16, kv16 = _mem_fwd(mem, gains["g_mem"], wkv16)
    ypre, y16, y2, x1 = _mix_out(y_attn, bcu, qx16, kv16, cw8, gains["g_attn_out"], gains["g_conv_out"],
                                 gains["g_xattn_out"], gains["g_post_mix"], wout16, x, [])
    a16, du16, h2_16, df2_16, dx1, loss8, dg_mlp = _mlp(
        x1, tgt, gains["g_pre_mlp"], gains["g_post_mlp"], wup8, wdn_halves)

    sums = {"w_up": _wgrad_cols(place, h2_16, du16, FF_BLK, "wgrad_up"),
            "w_down": _wgrad_cols(place, df2_16, a16, FF_BLK, "wgrad_down", square_b=True, transpose_out=True)}

    head_id = jnp.arange(AW, dtype=jnp.int32) // HEAD
    head_ones = (head_id[:, None] == head_id[None, :]).astype(BF16)
    dy2_16, qdo, ld, dbcu, dqx, dgs, dcw, dkv = _mix_out_bwd(
        dx1, y2, ypre, ltot, head_ones, q, bcu, qx16, kv16, cw8, gains["g_post_mix"], gains["g_attn_out"],
        gains["g_conv_out"], gains["g_xattn_out"], wout16)
    dkv16, dg_mem = _mem_bwd(mem, gains["g_mem"], wkv16, dkv)
    sums["w_mem_kv"] = _wgrad_rows(place, memn16, dkv16, "wgrad_mem_kv")
    sums["w_out"] = _wgrad_rows(place, y16, dy2_16, "wgrad_out")
    out = _attn_bwd(qdo, kvp, ld, [s[0] for s in sums.values()])
    dqkv, landed = out[:9], out[9:]
    reduced = {n: (s[1], landed[t]) for t, (n, s) in enumerate(sums.items())}
    dproj16, grad_x, dg_in = _in_proj_bwd(dqkv, dbcu, dqx, cos, sins, win16, x, gains["g_pre_mix"], dx1)

    _, in_own, in_landed = _wgrad_cols(place, h16, dproj16, PW // N_DEV, "wgrad_in", to_chips=True)
    reduced["w_in"] = (in_own, in_landed)
    return grad_x, reduced, (dg_in, dg_mem, dgs, dg_mlp, dcw, loss8)


BIG = ("w_in", "w_mem_kv", "w_out", "w_up", "w_down")
ORDER = ("g_pre_mix", "g_mem", "w_in", "w_mem_kv", "conv_w", "g_attn_out", "g_conv_out", "g_xattn_out", "w_out",
         "g_post_mix", "g_pre_mlp", "w_up", "w_down", "g_post_mlp")


def kernel(x, mem, positions, g_pre_mix, g_mem, w_in, w_mem_kv, conv_w, g_attn_out, g_conv_out, g_xattn_out, w_out, g_post_mix, g_pre_mlp, w_up, w_down, g_post_mlp, loss_target, m_g_pre_mix, m_g_mem, m_w_in, m_w_mem_kv, m_conv_w, m_g_attn_out, m_g_conv_out, m_g_xattn_out, m_w_out, m_g_post_mix, m_g_pre_mlp, m_w_up, m_w_down, m_g_post_mlp, v_g_pre_mix, v_g_mem, v_w_in, v_w_mem_kv, v_conv_w, v_g_attn_out, v_g_conv_out, v_g_xattn_out, v_w_out, v_g_post_mix, v_g_pre_mlp, v_w_up, v_w_down, v_g_post_mlp):
    w = dict(g_pre_mix=g_pre_mix, g_mem=g_mem, w_in=w_in, w_mem_kv=w_mem_kv, conv_w=conv_w, g_attn_out=g_attn_out,
             g_conv_out=g_conv_out, g_xattn_out=g_xattn_out, w_out=w_out, g_post_mix=g_post_mix, g_pre_mlp=g_pre_mlp,
             w_up=w_up, w_down=w_down, g_post_mlp=g_post_mlp)
    mo = dict(g_pre_mix=m_g_pre_mix, g_mem=m_g_mem, w_in=m_w_in, w_mem_kv=m_w_mem_kv, conv_w=m_conv_w,
              g_attn_out=m_g_attn_out, g_conv_out=m_g_conv_out, g_xattn_out=m_g_xattn_out, w_out=m_w_out,
              g_post_mix=m_g_post_mix, g_pre_mlp=m_g_pre_mlp, w_up=m_w_up, w_down=m_w_down, g_post_mlp=m_g_post_mlp)
    vo = dict(g_pre_mix=v_g_pre_mix, g_mem=v_g_mem, w_in=v_w_in, w_mem_kv=v_w_mem_kv, conv_w=v_conv_w,
              g_attn_out=v_g_attn_out, g_conv_out=v_g_conv_out, g_xattn_out=v_g_xattn_out, w_out=v_w_out,
              g_post_mix=v_g_post_mix, g_pre_mlp=v_g_pre_mlp, w_up=v_w_up, w_down=v_w_down, g_post_mlp=v_g_post_mlp)

    xi, yi, ci = lax.axis_index("x"), lax.axis_index("y"), lax.axis_index("c")
    me = 4 * xi + 2 * yi + ci
    place = jnp.stack([ci, 2 * xi + yi]).astype(jnp.int32)

    shards = {n: w[n][0].astype(BF16) for n in BIG}
    shards["conv_w"] = jnp.zeros((SUBLANES, LANES), F32).at[0:3, 0:CW // N_DEV].set(conv_w[0])

    gains = {n: w[n] for n, _, _, _ in SMALL}
    grad_x, reduced, small_acc = _local_step(x[0], mem[0], positions[0], gains, shards, loss_target[0], place)

    updated = {}
    for group in (("w_up", "w_down"), ("w_in", "w_out", "w_mem_kv")):
        updated.update(_adamw_shards({n: (*reduced[n], w[n][0], mo[n][0], vo[n][0]) for n in group},
                                     "adamw_" + "_".join(group))[0])
    grad, delta, new_m, new_v = {}, {}, {}, {}
    for n, (g, d_, m_, v_) in updated.items():
        grad[n], delta[n], new_m[n], new_v[n] = g[None], d_[None], m_[None], v_[None]

    params = {n: (w[n], mo[n], vo[n]) for n, _, _, _ in SMALL}
    params["conv_w"] = (w["conv_w"][0], mo["conv_w"][0], vo["conv_w"][0])
    loss, small = _small_update(_small_all_reduce(*small_acc), me.reshape(1).astype(jnp.int32), params)
    for n, (g, d_, m_, v_) in small.items():
        lead = (lambda a: a[None]) if n == "conv_w" else (lambda a: a)
        grad[n], delta[n], new_m[n], new_v[n] = lead(g), lead(d_), lead(m_), lead(v_)

    return (loss, grad_x[None], *[grad[n] for n in ORDER], *[delta[n] for n in ORDER],
            *[new_m[n] for n in ORDER], *[new_v[n] for n in ORDER])
```

```python
import functools

import numpy as np
import jax
import jax.numpy as jnp
from jax import lax
from jax.experimental import pallas as pl
from jax.experimental.pallas import tpu as pltpu

F32, BF16 = jnp.float32, jnp.bfloat16
MESH = pl.DeviceIdType.MESH
ANY = pl.BlockSpec(memory_space=pl.ANY)

N_DEV = 8
D = 1024
S = 4096
N_MEM = 256
HEAD = 64
AW, CW, XW = 512, 256, 256
PW = 3 * AW + 3 * CW + XW
FF = 4096
FF_BLK = FF // N_DEV
PATTERNS = ((128, 1), (512, 4), (2048, 16))
QB = 128
EPS = 1e-6
NEG = -1e30
SCALE = HEAD ** -0.5
ROPE_THETA = 10000.0
LANES = 128
SUBLANES = 8

ADAM_LR, ADAM_B1, ADAM_B2, ADAM_EPS, ADAM_WD, ADAM_STEP = 0.001, 0.9, 0.999, 1e-08, 0.01, 10

TQ = 512
TQ_MLP = 256
NT = S // TQ


def _cparams(vmem_mb, n_grid=1):
    return pltpu.CompilerParams(dimension_semantics=("arbitrary",) * n_grid, vmem_limit_bytes=vmem_mb << 20)


def _const(shape):
    nd = len(shape)
    return pl.BlockSpec(shape, lambda *_: (0,) * nd, pipeline_mode=pl.Buffered(1))


def _acc(shape):
    nd = len(shape)
    return pl.BlockSpec(shape, lambda *_: (0,) * nd)


def _dot(a, b):
    return jnp.dot(a, b, preferred_element_type=F32)


def _dot_nt(a, b):
    return lax.dot_general(a, b, (((1,), (1,)), ((), ())), preferred_element_type=F32)


def _dot_tn(a, b):
    return lax.dot_general(a, b, (((0,), (0,)), ((), ())), preferred_element_type=F32)


def _rms(x, g):
    r = lax.rsqrt(jnp.mean(x * x, axis=-1, keepdims=True) + EPS)
    n = x * r
    return n * g, n, r


def _rms_bwd(dy, n, r, g):
    dn = dy * g
    dx = r * (dn - n * jnp.mean(dn * n, axis=-1, keepdims=True))
    return dx, jnp.sum(dy * n, axis=0, keepdims=True)


def _rot_half(t):
    lane = lax.broadcasted_iota(jnp.int32, t.shape, 1)
    n = t.shape[1]
    return jnp.where((lane % HEAD) < HEAD // 2, pltpu.roll(t, n - HEAD // 2, 1), pltpu.roll(t, HEAD // 2, 1))


def _rope_table(pos_col, invf, sgn, shards):
    def body(p_ref, f_ref, s_ref, c_out, s_out):
        ang = p_ref[...] * f_ref[...]
        c_out[...] = jnp.cos(ang)
        s_out[...] = jnp.sin(ang) * s_ref[...]

    tile = pl.BlockSpec((TQ, LANES), lambda i: (i, 0))
    return _call_with_gather(
        body, NT, shards, name="rope_table",
        in_specs=[pl.BlockSpec((TQ, 1), lambda i: (i, 0)), _const((1, LANES)), _const((1, LANES))],
        out_specs=[tile, tile], out_shape=[jax.ShapeDtypeStruct((S, LANES), F32)] * 2,
        scratch_shapes=[], vmem_mb=32, args=(pos_col, invf, sgn))


def _all_heads(t):
    return jnp.tile(t, (1, AW // LANES))


def _mem_fwd(mem, g_mem, wkv16):
    def body(m_ref, g_ref, w_ref, n16_ref, kv_ref):
        y, _, _ = _rms(m_ref[...], g_ref[...])
        y16 = y.astype(BF16)
        n16_ref[...] = y16
        kv_ref[...] = _dot(y16, w_ref[...]).astype(BF16)

    return pl.pallas_call(
        body, name="mem_fwd",
        out_shape=[jax.ShapeDtypeStruct((N_MEM, D), BF16), jax.ShapeDtypeStruct((N_MEM, 2 * XW), BF16)],
        compiler_params=pltpu.CompilerParams(vmem_limit_bytes=32 << 20))(mem, g_mem, wkv16)


def _in_proj(x, g, w8, cos, sins, shards):
    blk = PW // N_DEV

    def body(x_ref, g_ref, w8_ref, c_ref, s_ref, q_ref, kv_ref, bcu_ref, qx_ref, h_ref, w_out, w_ref):
        @pl.when(pl.program_id(0) == 0)
        def _():
            for j in range(N_DEV):
                w_ref[:, j * blk:(j + 1) * blk] = w8_ref[j]
            w_out[...] = w_ref[...]

        y, _, _ = _rms(x_ref[...], g_ref[...])
        h = y.astype(BF16)
        h_ref[...] = h
        proj = _dot(h, w_ref[...])
        cos, sn = _all_heads(c_ref[...]), _all_heads(s_ref[...])
        q, k = proj[:, 0:AW], proj[:, AW:2 * AW]
        q_ref[...] = (q * cos + _rot_half(q) * sn) * SCALE
        kv_ref[...] = _pack_pair(k * cos + _rot_half(k) * sn, proj[:, 2 * AW:3 * AW])
        bcu_ref[...] = proj[:, 3 * AW:3 * AW + 3 * CW]
        qx_ref[...] = (proj[:, 3 * AW + 3 * CW:] * SCALE).astype(BF16)

    def tile(w):
        return pl.BlockSpec((TQ, w), lambda i: (i, 0))

    return _call_with_gather(
        body, NT, shards, name="in_proj",
        in_specs=[tile(D), _const((1, D)), _const((N_DEV, D, blk)), tile(LANES), tile(LANES)],
        out_specs=[tile(AW), tile(AW), tile(3 * CW), tile(XW), tile(D), _acc((D, PW))],
        out_shape=[jax.ShapeDtypeStruct((S, AW), F32)] * 2 + [
            jax.ShapeDtypeStruct((S, 3 * CW), F32), jax.ShapeDtypeStruct((S, XW), BF16),
            jax.ShapeDtypeStruct((S, D), BF16), jax.ShapeDtypeStruct((D, PW), BF16)],
        scratch_shapes=[pltpu.VMEM((D, PW), BF16)], vmem_mb=56, args=(x, g, w8, cos, sins))


ATTN_PLANS = (("p1", 1, 128, 32), ("p4", 8, 64, 8), ("p16", 16, 128, 2))
PAD = 128
WIN = 256


ATTN_UNROLL = 8


def _fill_bias(tab, qblk, partner):
    qi = lax.broadcasted_iota(jnp.int32, (2 * qblk, WIN), 0) & (qblk - 1)
    kj = lax.broadcasted_iota(jnp.int32, (2 * qblk, WIN), 1)
    piece = kj >> (qblk.bit_length() - 1)
    kk = kj & (qblk - 1)
    prev = (piece & 1) == 0
    of_partner = piece >= 2
    for first in (0, 1):
        for par in (0, 1):
            lo = jnp.where(prev, (qblk if first else qi) + jnp.where(of_partner, par, 0), 0)
            hi = jnp.where(prev, qblk, qi + jnp.where(of_partner, par - 1, 0))
            tab[2 * first + par] = jnp.where((kk >= lo) & (kk <= hi), 0.0, NEG).astype(F32)


def _block_rows(g, qblk, nbc, partner):
    own = pl.ds(pl.multiple_of(PAD + g * qblk, qblk), qblk)
    first = ((g & (nbc - 1)) == 0).astype(jnp.int32)
    if partner:
        gp = jnp.bitwise_xor(g, 4 * nbc)
        wins = (pl.ds(pl.multiple_of(PAD + (g - 1) * qblk, qblk), 2 * qblk),
                pl.ds(pl.multiple_of(PAD + (gp - 1) * qblk, qblk), 2 * qblk))
        return own, wins, 2 * first + ((g >> ((4 * nbc).bit_length() - 1)) & 1)
    return own, (pl.ds(pl.multiple_of(PAD + (g - 1) * qblk, qblk), 2 * qblk),), 2 * first


def _pack_pair(lo, hi):
    lo_bits = lax.bitcast_convert_type(lo.astype(BF16).astype(F32), jnp.uint32) >> 16
    hi_bits = lax.bitcast_convert_type(hi.astype(BF16).astype(F32), jnp.uint32) & jnp.uint32(0xFFFF0000)
    return lax.bitcast_convert_type(hi_bits | lo_bits, F32)


def _unpack_pair(c):
    bits = lax.bitcast_convert_type(c, jnp.uint32)
    lo = lax.bitcast_convert_type(bits << 16, F32).astype(BF16)
    hi = lax.bitcast_convert_type(bits & jnp.uint32(0xFFFF0000), F32).astype(BF16)
    return lo, hi


def _window(ref, wins):
    parts = [ref[w, :] for w in wins]
    return parts[0] if len(parts) == 1 else jnp.concatenate(parts, axis=0)


def _stack_heads(t, lane):
    zero = jnp.zeros_like(t)
    return jnp.concatenate([jnp.where(lane < HEAD, t, zero), jnp.where(lane >= HEAD, t, zero)], axis=0)


def _unstack_heads(t2, lane):
    half = t2.shape[0] // 2
    return jnp.where(lane < HEAD, t2[0:half, :], t2[half:, :])


def _lanes_of(step):
    return pl.ds(pl.multiple_of(step * LANES, LANES), LANES)


def _whole_wait(buf, sem):
    whole = buf.at[pl.ds(PAD, S), :]
    return pltpu.make_async_copy(whole, whole, sem)


def _whole_waits(bufs, sems):
    return [_whole_wait(buf, sems.at[i]) for i, buf in enumerate(bufs)]


def _class_gather(views, bufs, sems, lanes):
    copies = []
    for i, (view, buf) in enumerate(zip(views, bufs)):
        if view.ndim == 2:
            copies.append(pltpu.make_async_copy(view.at[:, lanes], buf.at[pl.ds(PAD, S), :], sems.at[i]))
        else:
            per, n_cls = view.shape[0], view.shape[1]
            copies += [pltpu.make_async_copy(view.at[:, c, lanes], buf.at[pl.ds(PAD + c * per, per), :], sems.at[i])
                       for c in range(n_cls)]
    return copies


def _class_scatter(bufs, dsts, sems, lanes=None):
    copies = []
    for i, (buf, dst) in enumerate(zip(bufs, dsts)):
        if dst.ndim == 2:
            copies.append(pltpu.make_async_copy(buf.at[pl.ds(PAD, S), :], dst.at[:, lanes], sems.at[i]))
            continue
        per, n_cls = dst.shape[0], dst.shape[1]
        for c in range(n_cls):
            to = dst.at[:, c, :] if lanes is None else dst.at[:, c, lanes]
            copies.append(pltpu.make_async_copy(buf.at[pl.ds(PAD + c * per, per), :], to, sems.at[i]))
    return copies


def _start(copies):
    for cp in copies:
        cp.start()


def _wait(waits):
    for w in waits:
        w.wait()


def _attn_fwd(q, kvp, shards=()):
    views = [[a] + [a.reshape(S // n, n, AW) for _, n, _, _ in ATTN_PLANS[1:]] for a in (q, kvp)]
    flat = [views[a][p] for p in range(3) for a in range(2)]
    ng = len(shards)
    n_grid = AW // LANES

    def body(*refs):
        hbm = [refs[2 * p:2 * p + 2] for p in range(3)]
        refs = refs[6:]
        shard_refs, refs = refs[:ng], refs[ng:]
        y_ref, lt_ref = refs[0:2]
        whole_refs, refs = refs[2:2 + ng], refs[2 + ng:]
        bufs = [refs[2 * p:2 * p + 2] for p in range(3)]
        oc4, lc4, oc16, lc16, tab128, tab4, sem_in = refs[6:13]
        step = pl.program_id(0)
        if ng:
            start_gather, relay_gather, finish_gather = _gather_steps(shard_refs, whole_refs, *refs[13:])
            pl.when(step == 0)(start_gather)
            pl.when(step == n_grid // 2)(relay_gather)
        now = [_class_gather(hbm[p], bufs[p], sem_in.at[p], _lanes_of(step)) for p in range(3)]
        nxt = [_class_gather(hbm[p], bufs[p], sem_in.at[p], _lanes_of(step + 1)) for p in range(3)]

        @pl.when(step == 0)
        def _():
            for p in range(3):
                _start(now[p])
                for b in bufs[p]:
                    b[0:PAD, :] = jnp.zeros((PAD, LANES), F32)
            _fill_bias(tab128, 128, False)
            _fill_bias(tab4, 64, True)

        def prefetch(p):
            pl.when(step + 1 < n_grid)(lambda: _start(nxt[p]))

        lane = lax.broadcasted_iota(jnp.int32, (1, LANES), 1)
        ones = jnp.ones((WIN, LANES), BF16)

        def run(plan, bq, bkv, tab, o_dst, l_dst, dst_pad):
            _, n_cls, qblk, nbc = plan
            partner = n_cls == 8

            def block(g, carry):
                own, wins, mask = _block_rows(g, qblk, nbc, partner)
                q2 = _stack_heads(bq[own, :].astype(BF16), lane)
                kw, vwin = _unpack_pair(_window(bkv, wins))
                vw = jnp.concatenate([vwin, ones], axis=1)
                s = _dot_nt(q2, kw) + tab[mask]
                m = jnp.max(s, axis=1, keepdims=True)
                oe = _dot(jnp.exp(s - m).astype(BF16), vw)
                den = oe[:, LANES:]
                dst = pl.ds(pl.multiple_of(dst_pad + g * qblk, qblk), qblk)
                o_dst[dst, :] = _unstack_heads(oe[:, 0:LANES] / den, lane)
                l_dst[dst, :] = _unstack_heads(m + jnp.log(den), lane)
                return carry
            lax.fori_loop(0, n_cls * nbc, block, 0, unroll=ATTN_UNROLL)

        _wait(_whole_waits(bufs[0], sem_in.at[0]))
        run(ATTN_PLANS[0], *bufs[0], tab128, y_ref, lt_ref, 0)
        prefetch(0)
        _wait(_whole_waits(bufs[1], sem_in.at[1]))
        run(ATTN_PLANS[1], *bufs[1], tab4, oc4, lc4, PAD)
        prefetch(1)
        _wait(_whole_waits(bufs[2], sem_in.at[2]))
        run(ATTN_PLANS[2], *bufs[2], tab128, oc16, lc16, PAD)
        prefetch(2)

        n_rows = 64

        def token_order(buf, t, n_cls):
            per = S // n_cls
            first = PAD + t * (n_rows // n_cls)
            return jnp.concatenate([buf[pl.ds(first + jj, n_cls, stride=per), :] for jj in range(n_rows // n_cls)],
                                   axis=0)

        def combine(t, carry):
            rows = pl.ds(pl.multiple_of(t * n_rows, n_rows), n_rows)
            l0, l1, l2 = lt_ref[rows, :], token_order(lc4, t, 8), token_order(lc16, t, 16)
            lm = jnp.maximum(jnp.maximum(l0, l1), l2)
            e0, e1, e2 = jnp.exp(l0 - lm), jnp.exp(l1 - lm), jnp.exp(l2 - lm)
            den = e0 + e1 + e2
            y_ref[rows, :] = (e0 * y_ref[rows, :] + e1 * token_order(oc4, t, 8)
                              + e2 * token_order(oc16, t, 16)) / den
            lt_ref[rows, :] = lm + jnp.log(den)
            return carry
        lax.fori_loop(0, S // n_rows, combine, 0, unroll=2)

        if ng:
            pl.when(step == n_grid - 1)(finish_gather)

    col = pl.BlockSpec((S, LANES), lambda h: (0, h))
    padded = pltpu.VMEM((PAD + S, LANES), F32)
    return pl.pallas_call(
        body, grid=(n_grid,), name="attn_fwd",
        in_specs=[ANY] * (6 + ng), out_specs=[col, col] + [ANY] * ng,
        out_shape=[jax.ShapeDtypeStruct((S, AW), F32)] * 2 + _gathered_shapes(shards),
        scratch_shapes=[padded] * 10 + [
            pltpu.VMEM((4, 256, WIN), F32), pltpu.VMEM((4, 128, WIN), F32), pltpu.SemaphoreType.DMA((3, 2))]
        + (_gather_scratch(ng) if ng else []),
        compiler_params=_cparams(56))(*flat, *shards)


def _conv_taps(z, zprev, row):
    z1 = jnp.where(row == 0, zprev[7:8, :], pltpu.roll(z, 1, 0))
    z2 = jnp.where(row == 0, zprev[6:7, :], jnp.where(row == 1, zprev[7:8, :], pltpu.roll(z, 2, 0)))
    return z1, z2


def _xattn_scores(qm, km):
    s = _dot_nt(qm, km)
    m = jnp.max(s, axis=1, keepdims=True)
    e = jnp.exp(s - m)
    return e, jnp.sum(e, axis=1, keepdims=True)


def _mix_out(y_attn, bcu, qx16, kv16, cw8, g_attn, g_conv, g_x, g_post, wout16, x, shards):
    def body(ya_ref, bcu_ref, halo_ref, qx_ref, kv_ref, cw_ref, ga_ref, gc_ref, gx_ref, gp_ref, w_ref, x_ref,
             ypre_ref, y16_ref, y2_ref, x1_ref):
        i = pl.program_id(0)
        bcu = bcu_ref[...]
        b, c, u = bcu[:, 0:CW], bcu[:, CW:2 * CW], bcu[:, 2 * CW:]
        z = c * u
        halo = halo_ref[...]
        zprev = jnp.where(i > 0, halo[:, CW:2 * CW] * halo[:, 2 * CW:], 0.0)
        row = lax.broadcasted_iota(jnp.int32, z.shape, 0)
        z1, z2 = _conv_taps(z, zprev, row)
        cw = cw_ref[...]
        y_conv = b * (z2 * cw[0:1, :] + z1 * cw[1:2, :] + z * cw[2:3, :])

        qx = qx_ref[...]
        kv = kv_ref[...]
        km, vm = kv[:, 0:XW], kv[:, XW:]
        lane = lax.broadcasted_iota(jnp.int32, qx.shape, 1)
        y_x = jnp.zeros(qx.shape, F32)
        for h in range(XW // HEAD):
            hm = (lane >= h * HEAD) & (lane < (h + 1) * HEAD)
            e, l = _xattn_scores(jnp.where(hm, qx, jnp.zeros_like(qx)), km)
            y_x = jnp.where(hm, _dot(e.astype(BF16), vm) / l, y_x)

        y_attn = ya_ref[...]
        ypre_ref[:, 0:AW] = y_attn
        ypre_ref[:, AW:AW + CW] = y_conv
        ypre_ref[:, AW + CW:] = y_x
        y = jnp.concatenate([_rms(y_attn, ga_ref[...])[0], _rms(y_conv, gc_ref[...])[0],
                             _rms(y_x, gx_ref[...])[0]], axis=1).astype(BF16)
        y16_ref[...] = y
        y2 = _dot(y, w_ref[...])
        y2_ref[...] = y2
        x1_ref[...] = x_ref[...] + _rms(y2, gp_ref[...])[0]

    def tile(w):
        return pl.BlockSpec((TQ, w), lambda i: (i, 0))

    halo = pl.BlockSpec((SUBLANES, 3 * CW), lambda i: (jnp.maximum(i * (TQ // SUBLANES) - 1, 0), 0))
    return _call_with_gather(
        body, NT, shards, name="mix_out",
        in_specs=[tile(AW), tile(3 * CW), halo, tile(XW), _const((N_MEM, 2 * XW)), _const((SUBLANES, CW)),
                  _const((1, AW)), _const((1, CW)), _const((1, XW)), _const((1, D)), _const((D, D)), tile(D)],
        out_specs=[tile(D), tile(D), tile(D), tile(D)],
        out_shape=[jax.ShapeDtypeStruct((S, D), F32), jax.ShapeDtypeStruct((S, D), BF16),
                   jax.ShapeDtypeStruct((S, D), F32), jax.ShapeDtypeStruct((S, D), F32)],
        scratch_shapes=[], vmem_mb=56,
        args=(y_attn, bcu, bcu, qx16, kv16, cw8, g_attn, g_conv, g_x, g_post, wout16, x))


def _mlp(x1, tgt, g_pre, g_post, wup8, wdn_halves):
    tq = TQ_MLP
    half = D // 2

    def body(x1_ref, t_ref, g1_ref, g2_ref, wu_ref, wda_ref, wdb_ref,
             a16_ref, du_ref, h2_ref, df2_ref, dx1_ref, loss_ref, dg_ref, a32):
        @pl.when(pl.program_id(0) == 0)
        def _():
            loss_ref[...] = jnp.zeros_like(loss_ref)
            dg_ref[...] = jnp.zeros_like(dg_ref)

        x1 = x1_ref[...]
        g1, g2 = g1_ref[...], g2_ref[...]
        y1, n1, r1 = _rms(x1, g1)
        h2 = y1.astype(BF16)
        h2_ref[...] = h2
        f2a = jnp.zeros((tq, half), F32)
        f2b = jnp.zeros((tq, half), F32)
        for j in range(N_DEV):
            cols = slice(j * FF_BLK, (j + 1) * FF_BLK)
            a = jnp.maximum(_dot(h2, wu_ref[j]), 0.0)
            a32[:, cols] = a
            a16_ref[:, cols] = a.astype(BF16)
            f = (a * a).astype(BF16)
            f2a = f2a + _dot(f, wda_ref[cols, :])
            f2b = f2b + _dot(f, wdb_ref[cols, :])
        f2 = jnp.concatenate([f2a, f2b], axis=1)
        y2, n2, r2 = _rms(f2, g2)
        e = x1 + y2 - t_ref[...]
        sq = jnp.sum(jnp.sum(e * e, axis=1, keepdims=True), axis=0, keepdims=True)
        loss_ref[...] += jnp.broadcast_to(sq * (0.5 / D), loss_ref.shape)
        dout = e * (1.0 / D)
        df2, dg2 = _rms_bwd(dout, n2, r2, g2)
        df2_16 = df2.astype(BF16)
        df2_ref[...] = df2_16
        dh2 = jnp.zeros((tq, D), F32)
        for j in range(N_DEV):
            cols = slice(j * FF_BLK, (j + 1) * FF_BLK)
            df = _dot_nt(df2_16[:, 0:half], wda_ref[cols, :]) + _dot_nt(df2_16[:, half:], wdb_ref[cols, :])
            du = (df * (2.0 * a32[:, cols])).astype(BF16)
            du_ref[:, cols] = du
            dh2 = dh2 + _dot_nt(du, wu_ref[j])
        dx, dg1 = _rms_bwd(dh2, n1, r1, g1)
        dx1_ref[...] = dout + dx
        dg_ref[0:1, :] += dg2
        dg_ref[1:2, :] += dg1

    def tile(w):
        return pl.BlockSpec((tq, w), lambda i: (i, 0))

    return pl.pallas_call(
        body, grid=(S // tq,), name="mlp",
        in_specs=[tile(D), tile(D), _const((1, D)), _const((1, D)), _const((N_DEV, D, FF_BLK)), _const((FF, half)), _const((FF, half))],
        out_specs=[tile(FF), tile(FF), tile(D), tile(D), tile(D), _acc((SUBLANES, LANES)), _acc((SUBLANES, D))],
        out_shape=[jax.ShapeDtypeStruct((S, FF), BF16), jax.ShapeDtypeStruct((S, FF), BF16),
                   jax.ShapeDtypeStruct((S, D), BF16), jax.ShapeDtypeStruct((S, D), BF16),
                   jax.ShapeDtypeStruct((S, D), F32), jax.ShapeDtypeStruct((SUBLANES, LANES), F32),
                   jax.ShapeDtypeStruct((SUBLANES, D), F32)],
        scratch_shapes=[pltpu.VMEM((tq, FF), F32)],
        compiler_params=_cparams(56))(x1, tgt, g_pre, g_post, wup8, *wdn_halves)


def _mix_out_bwd(dx1, y2, ypre, ltot, head_ones, q, bcu, qx16, kv16, cw8, g_post, g_attn, g_conv, g_x, wout16):
    def body(dx1_ref, y2_ref, ypre_ref, lt_ref, e_ref, q_ref, bcu_ref, halo_ref, qx_ref, kv_ref, cw_ref, gp_ref,
             ga_ref, gc_ref, gx_ref, w_ref, dy2_ref, qdo_ref, ld_ref, dbcu_ref, dqx_ref, dgs_ref, dcw_ref, dkv_ref,
             carry):
        i = pl.program_id(0)

        @pl.when(i == 0)
        def _():
            dgs_ref[...] = jnp.zeros_like(dgs_ref)
            dcw_ref[...] = jnp.zeros_like(dcw_ref)
            dkv_ref[...] = jnp.zeros_like(dkv_ref)
            carry[...] = jnp.zeros_like(carry)

        gp = gp_ref[...]
        _, n, r = _rms(y2_ref[...], gp)
        dy2, dgp = _rms_bwd(dx1_ref[...], n, r, gp)
        dy2_16 = dy2.astype(BF16)
        dy2_ref[...] = dy2_16
        dy = _dot_nt(dy2_16, w_ref[...])

        ypre = ypre_ref[...]
        ga, gc, gx = ga_ref[...], gc_ref[...], gx_ref[...]
        _, na, ra = _rms(ypre[:, 0:AW], ga)
        dya, dga = _rms_bwd(dy[:, 0:AW], na, ra, ga)
        _, nc, rc = _rms(ypre[:, AW:AW + CW], gc)
        dyc, dgc = _rms_bwd(dy[:, AW:AW + CW], nc, rc, gc)
        y_x = ypre[:, AW + CW:]
        _, nx, rx = _rms(y_x, gx)
        dyx, dgx = _rms_bwd(dy[:, AW + CW:], nx, rx, gx)
        qdo_ref[...] = _pack_pair(q_ref[...], dya)
        prod = dya * ypre[:, 0:AW]
        hi = prod.astype(BF16)
        lo = (prod - hi.astype(F32)).astype(BF16)
        head_sum = _dot(hi, e_ref[...]) + _dot(lo, e_ref[...])
        lane_a = lax.broadcasted_iota(jnp.int32, prod.shape, 1)
        ld_ref[...] = jnp.where((lane_a % HEAD) < HEAD // 2, lt_ref[...], head_sum)
        dgs_ref[0:1, :] += dgp
        dgs_ref[1:2, :] += jnp.concatenate([dga, dgc, dgx], axis=1)

        bcu = bcu_ref[...]
        b, c, u = bcu[:, 0:CW], bcu[:, CW:2 * CW], bcu[:, 2 * CW:]
        z = c * u
        halo = halo_ref[...]
        zprev = jnp.where(i < NT - 1, halo[:, CW:2 * CW] * halo[:, 2 * CW:], 0.0)
        row = lax.broadcasted_iota(jnp.int32, z.shape, 0)
        z1, z2 = _conv_taps(z, zprev, row)
        cw = cw_ref[...]
        conv = z2 * cw[0:1, :] + z1 * cw[1:2, :] + z * cw[2:3, :]
        dconv = dyc * b
        nxt = carry[...]
        dn1 = jnp.where(row == TQ - 1, nxt[0:1, :], pltpu.roll(dconv, TQ - 1, 0))
        dn2 = jnp.where(row == TQ - 1, nxt[1:2, :], jnp.where(row == TQ - 2, nxt[0:1, :], pltpu.roll(dconv, TQ - 2, 0)))
        carry[...] = dconv[0:SUBLANES, :]
        dz = dconv * cw[2:3, :] + dn1 * cw[1:2, :] + dn2 * cw[0:1, :]
        dbcu_ref[:, 0:CW] = (dyc * conv).astype(BF16)
        dbcu_ref[:, CW:2 * CW] = (dz * u).astype(BF16)
        dbcu_ref[:, 2 * CW:] = (dz * c).astype(BF16)
        dcw_ref[0:1, :] += jnp.sum(z2 * dconv, axis=0, keepdims=True)
        dcw_ref[1:2, :] += jnp.sum(z1 * dconv, axis=0, keepdims=True)
        dcw_ref[2:3, :] += jnp.sum(z * dconv, axis=0, keepdims=True)

        qx = qx_ref[...]
        kv = kv_ref[...]
        km, vm = kv[:, 0:XW], kv[:, XW:]
        lane = lax.broadcasted_iota(jnp.int32, qx.shape, 1)
        dqx = jnp.zeros(qx.shape, F32)
        dkm = jnp.zeros((N_MEM, XW), F32)
        dvm = jnp.zeros((N_MEM, XW), F32)
        for h in range(XW // HEAD):
            hm = (lane >= h * HEAD) & (lane < (h + 1) * HEAD)
            qm = jnp.where(hm, qx, jnp.zeros_like(qx))
            e, l = _xattn_scores(qm, km)
            p = e / l
            dom = jnp.where(hm, dyx, 0.0)
            do16 = dom.astype(BF16)
            dsum = jnp.sum(dom * y_x, axis=1, keepdims=True)
            ds = (p * (_dot_nt(do16, vm) - dsum)).astype(BF16)
            dqx = jnp.where(hm, _dot(ds, km), dqx)
            dkm = dkm + _dot_tn(ds, qm)
            dvm = dvm + _dot_tn(p.astype(BF16), do16)
        dqx_ref[...] = (dqx * SCALE).astype(BF16)
        dkv_ref[:, 0:XW] += dkm
        dkv_ref[:, XW:] += dvm

    def tile(w):
        return pl.BlockSpec((TQ, w), lambda i: (NT - 1 - i, 0))

    halo = pl.BlockSpec((SUBLANES, 3 * CW), lambda i: (jnp.maximum((NT - 1 - i) * (TQ // SUBLANES) - 1, 0), 0))
    return pl.pallas_call(
        body, grid=(NT,), name="mix_out_bwd",
        in_specs=[tile(D), tile(D), tile(D), tile(AW), _const((AW, AW)), tile(AW), tile(3 * CW), halo, tile(XW),
                  _const((N_MEM, 2 * XW)), _const((SUBLANES, CW)), _const((1, D)), _const((1, AW)), _const((1, CW)),
                  _const((1, XW)), _const((D, D))],
        out_specs=[tile(D), tile(AW), tile(AW), tile(3 * CW), tile(XW), _acc((SUBLANES, D)), _acc((SUBLANES, CW)),
                   _acc((N_MEM, 2 * XW))],
        out_shape=[jax.ShapeDtypeStruct((S, D), BF16), jax.ShapeDtypeStruct((S, AW), F32),
                   jax.ShapeDtypeStruct((S, AW), F32),
                   jax.ShapeDtypeStruct((S, 3 * CW), BF16), jax.ShapeDtypeStruct((S, XW), BF16),
                   jax.ShapeDtypeStruct((SUBLANES, D), F32), jax.ShapeDtypeStruct((SUBLANES, CW), F32),
                   jax.ShapeDtypeStruct((N_MEM, 2 * XW), F32)],
        scratch_shapes=[pltpu.VMEM((SUBLANES, CW), F32)],
        compiler_params=_cparams(56))(dx1, y2, ypre, ltot, head_ones, q, bcu, bcu, qx16, kv16, cw8, g_post, g_attn,
                                      g_conv, g_x, wout16)


def _attn_bwd(qdo, kvp, ld, chip_sums=()):
    n_in = 3
    views = [[a] + [a.reshape(S // n, n, AW) for _, n, _, _ in ATTN_PLANS[1:]] for a in (qdo, kvp, ld)]
    flat = [views[a][p] for p in range(3) for a in range(n_in)]
    ns = len(chip_sums)
    n_grid = AW // LANES

    def body(*refs):
        hbm = [refs[n_in * p:n_in * p + n_in] for p in range(3)]
        refs = refs[3 * n_in:]
        sum_refs, refs = refs[:ns], refs[ns:]
        outs = [refs[3 * p:3 * p + 3] for p in range(3)]
        landed_refs, sc = refs[9:9 + ns], refs[9 + ns:]
        bufs = [sc[3 * p:3 * p + 3] for p in range(3)]
        res = [sc[9 + 3 * p:12 + 3 * p] for p in range(3)]
        tab128, tab4, sem_in, sem_out = sc[18:22]
        step = pl.program_id(0)
        if ns:
            start_chips, finish_chips = _chips_steps(sum_refs, landed_refs, *sc[22:])
            pl.when(step == 0)(start_chips)
        now = [_class_gather(hbm[p], bufs[p], sem_in.at[p], _lanes_of(step)) for p in range(3)]
        nxt = [_class_gather(hbm[p], bufs[p], sem_in.at[p], _lanes_of(step + 1)) for p in range(3)]

        @pl.when(step == 0)
        def _():
            for p in range(3):
                _start(now[p])
                for b in bufs[p]:
                    b[0:PAD, :] = jnp.zeros((PAD, LANES), F32)
            _fill_bias(tab128, 128, False)
            _fill_bias(tab4, 64, True)

        def prefetch(p):
            pl.when(step + 1 < n_grid)(lambda: _start(nxt[p]))

        for p in range(3):
            for b in res[p]:
                b[...] = jnp.zeros_like(b)
        lane = lax.broadcasted_iota(jnp.int32, (1, LANES), 1)

        def run(plan, plan_bufs, tab, dst):
            _, n_cls, qblk, nbc = plan
            partner = n_cls == 8
            bqdo, bkv, bld = plan_bufs
            rq, rk, rv = dst

            def block(g, carry):
                own, wins, mask = _block_rows(g, qblk, nbc, partner)
                qb, dob = _unpack_pair(bqdo[own, :])
                q2, do2 = _stack_heads(qb, lane), _stack_heads(dob, lane)
                kw, vw = _unpack_pair(_window(bkv, wins))
                ldv = bld[own, :]
                half = HEAD // 2
                lt2 = jnp.concatenate([ldv[:, 0:1], ldv[:, HEAD:HEAD + 1]], axis=0)
                dsum2 = jnp.concatenate([ldv[:, half:half + 1], ldv[:, HEAD + half:HEAD + half + 1]], axis=0)
                p = jnp.exp(_dot_nt(q2, kw) + tab[mask] - lt2)
                ds = (p * (_dot_nt(do2, vw) - dsum2)).astype(BF16)
                rq[own, :] = _unstack_heads(_dot(ds, kw), lane)
                dkw = _dot_tn(ds, q2)
                dvw = _dot_tn(p.astype(BF16), do2)
                n_w = WIN // len(wins)
                for i, w in enumerate(wins):
                    rk[w, :] += dkw[i * n_w:(i + 1) * n_w, :]
                    rv[w, :] += dvw[i * n_w:(i + 1) * n_w, :]
                return carry
            lax.fori_loop(0, n_cls * nbc, block, 0, unroll=ATTN_UNROLL)

        tabs = (tab128, tab4, tab128)
        for p in range(3):
            _wait(_whole_waits(bufs[p], sem_in.at[p]))
            run(ATTN_PLANS[p], bufs[p], tabs[p], res[p])
            prefetch(p)
            _start(_class_scatter(res[p], outs[p], sem_out.at[p], _lanes_of(step)))
        for p in range(3):
            _wait(_whole_waits(res[p], sem_out.at[p]))
        if ns:
            pl.when(step == n_grid - 1)(finish_chips)

    padded = pltpu.VMEM((PAD + S, LANES), F32)
    shapes = [jax.ShapeDtypeStruct(views[0][p].shape, F32) for p in range(3) for _ in range(3)]
    out = pl.pallas_call(
        body, grid=(n_grid,), name="attn_bwd",
        in_specs=[ANY] * (3 * n_in + ns), out_specs=[ANY] * (9 + ns),
        out_shape=shapes + _chips_shapes(chip_sums),
        scratch_shapes=[padded] * 18
        + [pltpu.VMEM((4, 256, WIN), F32), pltpu.VMEM((4, 128, WIN), F32),
           pltpu.SemaphoreType.DMA((3, n_in)), pltpu.SemaphoreType.DMA((3, 3))]
        + (_chips_scratch(ns) if ns else []),
        compiler_params=_cparams(56))(*flat, *chip_sums)
    return [o.reshape(S, AW) for o in out[:9]] + list(out[9:])


def _in_proj_bwd(dqkv, dbcu, dqx, cos, sins, w16, x, g, dx1):
    tq = TQ // 2

    def body(*refs):
        parts = refs[0:9]
        dbcu_ref, dqx_ref, c_ref, s_ref, w_ref, x_ref, g_ref, dx1_ref, dp_ref, gx_ref, dg_ref = refs[9:]

        @pl.when(pl.program_id(0) == 0)
        def _():
            dg_ref[...] = jnp.zeros_like(dg_ref)

        dq, dk, dv = (parts[i][...] + parts[3 + i][...] + parts[6 + i][...] for i in range(3))
        cos, sn = _all_heads(c_ref[...]), _all_heads(s_ref[...])
        dqr = dq * SCALE
        dkr = dk
        dp = jnp.concatenate([(dqr * cos + _rot_half(dqr * sn)).astype(BF16),
                              (dkr * cos + _rot_half(dkr * sn)).astype(BF16), dv.astype(BF16),
                              dbcu_ref[...], dqx_ref[...]], axis=1)
        dp_ref[...] = dp
        dh = _dot_nt(dp, w_ref[...])
        g = g_ref[...]
        _, n, r = _rms(x_ref[...], g)
        dx, dg = _rms_bwd(dh, n, r, g)
        gx_ref[...] = dx1_ref[...] + dx
        dg_ref[0:1, :] += dg

    def tile(w):
        return pl.BlockSpec((tq, w), lambda i: (i, 0))

    return pl.pallas_call(
        body, grid=(S // tq,), name="in_proj_bwd",
        in_specs=[tile(AW)] * 9 + [tile(3 * CW), tile(XW), tile(LANES), tile(LANES), _const((D, PW)),
                                   tile(D), _const((1, D)), tile(D)],
        out_specs=[tile(PW), tile(D), _acc((SUBLANES, D))],
        out_shape=[jax.ShapeDtypeStruct((S, PW), BF16), jax.ShapeDtypeStruct((S, D), F32),
                   jax.ShapeDtypeStruct((SUBLANES, D), F32)],
        compiler_params=_cparams(56))(*dqkv, dbcu, dqx, cos, sins, w16, x, g, dx1)


def _mem_bwd(mem, g_mem, wkv16, dkv):
    def body(m_ref, g_ref, w_ref, dkv_ref, dkv16_ref, dg_ref):
        dkv16 = dkv_ref[...].astype(BF16)
        dkv16_ref[...] = dkv16
        _, n, _ = _rms(m_ref[...], g_ref[...])
        dg = jnp.sum(_dot_nt(dkv16, w_ref[...]) * n, axis=0, keepdims=True)
        dg_ref[...] = jnp.broadcast_to(dg, dg_ref.shape)

    return pl.pallas_call(
        body, name="mem_bwd",
        out_shape=[jax.ShapeDtypeStruct((N_MEM, 2 * XW), BF16), jax.ShapeDtypeStruct((SUBLANES, D), F32)],
        compiler_params=pltpu.CompilerParams(vmem_limit_bytes=32 << 20))(mem, g_mem, wkv16, dkv)


N_CHIPS = N_DEV // 2


def _transpose_into(at, a_ref):
    kk = a_ref.shape[0]
    chunk = min(kk, 512)
    for c in range(kk // chunk):
        at[:, c * chunk:(c + 1) * chunk] = a_ref[c * chunk:(c + 1) * chunk, :].T


def _pair_scratch(block):
    return [pltpu.VMEM((N_CHIPS,) + block, BF16), pltpu.VMEM((N_CHIPS,) + block, BF16),
            pltpu.SemaphoreType.DMA((N_CHIPS,)), pltpu.SemaphoreType.DMA((N_CHIPS,))]


def _swap_with_sibling(p, stage, land, send, recv):
    x, y, c = lax.axis_index("x"), lax.axis_index("y"), lax.axis_index("c")
    return pltpu.make_async_remote_copy(src_ref=stage.at[p], dst_ref=land.at[p], send_sem=send.at[p],
                                        recv_sem=recv.at[p], device_id=(x, y, 1 - c), device_id_type=MESH)


def _wgrad_cols(place, a16, b16, blk, name, square_b=False, transpose_out=False, to_chips=False):
    kk, m = a16.shape
    aligned = blk % LANES == 0
    wide = blk if aligned else -(-(blk + LANES // 2) // LANES) * LANES
    block = (blk, m) if transpose_out else (m, blk)

    def chip_of(step, my_chip):
        return (my_chip + 1 + step) & (N_CHIPS - 1) if to_chips else step

    def body(pl_ref, a_ref, *refs):
        b_refs, refs = refs[:2 if aligned else 1], refs[2 if aligned else 1:]
        (cs_ref, own_ref), refs = refs[:2], refs[2:]
        if to_chips:
            landed, refs = refs[0], refs[1:]
        (at, stage, land, send, recv), refs = refs[:5], refs[5:]
        if not aligned:
            (win, wsem), refs = refs[:2], refs[2:]
        step = pl.program_id(0)
        x, y, c = lax.axis_index("x"), lax.axis_index("y"), lax.axis_index("c")
        my_chip = 2 * x + y
        p = chip_of(step, my_chip)

        def fetch(at_step, mine):
            j = 2 * chip_of(at_step, my_chip) + (c if mine else 1 - c)
            first = pl.multiple_of(((j * blk) >> 7) << 7, LANES)
            slot = 2 * (at_step & 1) + mine
            return pltpu.make_async_copy(b_refs[0].at[:, pl.ds(first, wide)], win.at[slot], wsem.at[slot])

        @pl.when(step == 0)
        def _():
            if not aligned:
                fetch(0, 0).start()
                fetch(0, 1).start()
            _transpose_into(at, a_ref)

        if not aligned:
            @pl.when(step + 1 < N_CHIPS)
            def _():
                fetch(step + 1, 0).start()
                fetch(step + 1, 1).start()

        def partial(mine):
            if aligned:
                b = b_refs[mine][...]
                if square_b:
                    b = b * b
                acc = _dot(at[...], b)
            else:
                fetch(step, mine).wait()
                acc = _dot(at[...], win[2 * (step & 1) + mine])
                odd = c if mine else 1 - c
                acc = pltpu.roll(acc, jnp.where(odd == 0, 0, wide - LANES // 2), 1)[:, 0:blk]
            return acc.T if transpose_out else acc

        stage[p] = partial(0).astype(BF16)
        swap = _swap_with_sibling(p, stage, land, send, recv)
        swap.start()
        mine = partial(1)
        swap.wait()
        total = mine + land[p].astype(F32)
        cs_ref[0] = total.astype(BF16)

        @pl.when(p == my_chip)
        def _():
            own_ref[...] = total

        if to_chips:
            stage2, send2, recv2 = refs
            flipped = jnp.bitwise_xor(p, my_chip)
            k = jnp.where(flipped == 2, 0, jnp.where(flipped == 1, 1, 2))

            def to_owner(src, k_, px, py):
                return pltpu.make_async_remote_copy(src_ref=src, dst_ref=landed.at[k_], send_sem=send2.at[k_],
                                                    recv_sem=recv2.at[k_], device_id=(px, py, c), device_id_type=MESH)

            @pl.when(p != my_chip)
            def _():
                stage2[p] = total.astype(BF16)
                to_owner(stage2.at[p], k, p >> 1, p & 1).start()

            @pl.when(step == N_CHIPS - 1)
            def _():
                for k_ in range(N_CHIPS - 1):
                    to_owner(stage2.at[0], k_, x, y).wait()

    def b_spec(mine):
        return pl.BlockSpec((kk, blk), lambda i, s: (0, 2 * chip_of(i, s[1]) + (s[0] if mine else 1 - s[0])))

    b_specs, b_args = ([b_spec(0), b_spec(1)], (b16, b16)) if aligned else ([ANY], (b16,))
    scratch = [pltpu.VMEM((m, kk), BF16)] + _pair_scratch(block)
    if not aligned:
        scratch += [pltpu.VMEM((4, kk, wide), BF16), pltpu.SemaphoreType.DMA((4,))]
    out_specs = [pl.BlockSpec((1,) + block, lambda i, s: (chip_of(i, s[1]), 0, 0)), pl.BlockSpec(block, lambda i, s: (0, 0))]
    out_shape = [jax.ShapeDtypeStruct((N_CHIPS,) + block, BF16), jax.ShapeDtypeStruct(block, F32)]
    if to_chips:
        out_specs.append(ANY)
        out_shape.append(jax.ShapeDtypeStruct((N_CHIPS - 1,) + block, BF16))
        scratch += [pltpu.VMEM((N_CHIPS,) + block, BF16), pltpu.SemaphoreType.DMA((N_CHIPS - 1,)),
                    pltpu.SemaphoreType.DMA((N_CHIPS - 1,))]
    return pl.pallas_call(
        body, name=name,
        grid_spec=pltpu.PrefetchScalarGridSpec(
            num_scalar_prefetch=1, grid=(N_CHIPS,),
            in_specs=[pl.BlockSpec((kk, m), lambda i, s: (0, 0), pipeline_mode=pl.Buffered(1))] + b_specs,
            out_specs=out_specs, scratch_shapes=scratch),
        out_shape=out_shape, compiler_params=_cparams(56))(place, a16, *b_args)


def _wgrad_rows(place, a16, b16, name):
    kk, m = a16.shape
    n = b16.shape[1]
    block = (m // N_DEV, n)

    def body(pl_ref, a_ref, b_ref, cs_ref, own_ref, at, acc, stage, land, send, recv):
        c = pl_ref[0]
        _transpose_into(at, a_ref)
        acc[...] = _dot(at[...], b_ref[...])

        def rows(owner):
            return pl.ds(pl.multiple_of(owner * block[0], block[0]), block[0])

        swaps = []
        for p in range(N_CHIPS):
            stage[p] = acc[rows(2 * p + 1 - c), :].astype(BF16)
            swaps.append(_swap_with_sibling(p, stage, land, send, recv))
            swaps[-1].start()
        for p in range(N_CHIPS):
            swaps[p].wait()
            total = acc[rows(2 * p + c), :] + land[p].astype(F32)
            cs_ref[p] = total.astype(BF16)

            @pl.when(p == pl_ref[1])
            def _():
                own_ref[...] = total

    vmem = pl.BlockSpec(memory_space=pltpu.VMEM)
    return pl.pallas_call(
        body, name=name,
        in_specs=[pl.BlockSpec(memory_space=pltpu.SMEM), vmem, vmem], out_specs=[vmem, vmem],
        out_shape=[jax.ShapeDtypeStruct((N_CHIPS,) + block, BF16), jax.ShapeDtypeStruct(block, F32)],
        scratch_shapes=[pltpu.VMEM((m, kk), BF16), pltpu.VMEM((m, n), F32)] + _pair_scratch(block),
        compiler_params=pltpu.CompilerParams(vmem_limit_bytes=56 << 20))(place, a16, b16)


def _adamw_math(w, g, m, v):
    m = ADAM_B1 * m + (1.0 - ADAM_B1) * g
    v = ADAM_B2 * v + (1.0 - ADAM_B2) * jnp.square(g)
    m_hat = m / (1.0 - ADAM_B1 ** ADAM_STEP)
    v_hat = v / (1.0 - ADAM_B2 ** ADAM_STEP)
    delta = -ADAM_LR * (m_hat / (jnp.sqrt(v_hat) + ADAM_EPS) + ADAM_WD * w)
    return delta, m, v


def _adamw_shards(updates, name, chip_sums=()):
    names, nu, ns = list(updates), len(updates), len(chip_sums)

    def body(*refs):
        ins, sum_refs = refs[:5 * nu], refs[5 * nu:5 * nu + ns]
        outs = refs[5 * nu + ns:9 * nu + ns]
        landed_refs, scratch = refs[9 * nu + ns:9 * nu + 2 * ns], refs[9 * nu + 2 * ns:]
        if ns:
            start_chips, finish_chips = _chips_steps(sum_refs, landed_refs, *scratch)
            start_chips()
        for i in range(nu):
            o_ref, r_ref, w_ref, m_ref, v_ref = ins[5 * i:5 * i + 5]
            g_out, d_out, m_out, v_out = outs[4 * i:4 * i + 4]
            g = o_ref[...] + r_ref[0].astype(F32) + r_ref[1].astype(F32) + r_ref[2].astype(F32)
            g_out[...] = g
            d_out[...], m_out[...], v_out[...] = _adamw_math(w_ref[...], g, m_ref[...], v_ref[...])
        if ns:
            finish_chips()

    vmem = pl.BlockSpec(memory_space=pltpu.VMEM)
    out = pl.pallas_call(
        body, name=name,
        in_specs=[vmem] * (5 * nu) + [ANY] * ns, out_specs=[vmem] * (4 * nu) + [ANY] * ns,
        out_shape=[jax.ShapeDtypeStruct(updates[n][2].shape, F32) for n in names for _ in range(4)]
        + _chips_shapes(chip_sums),
        scratch_shapes=_chips_scratch(ns) if ns else [],
        compiler_params=pltpu.CompilerParams(vmem_limit_bytes=56 << 20),
    )(*[a for n in names for a in updates[n]], *chip_sums)
    return {n: out[4 * i:4 * i + 4] for i, n in enumerate(names)}, list(out[4 * nu:])


def _place():
    x, y, c = lax.axis_index("x"), lax.axis_index("y"), lax.axis_index("c")
    chips = [(1 - x, y), (x, 1 - y), (1 - x, 1 - y)]
    return x, y, c, chips


def _gather_steps(ins, outs, send, recv, lsem):
    nt = len(ins)
    x, y, c, (xn, yn, diag) = _place()
    me, sib = (x, y, c), (x, y, 1 - c)

    def slot(t, px, py, pc):
        return outs[t].at[4 * px + 2 * py + pc]

    def copy(t, k, block, to, src=None):
        return pltpu.make_async_remote_copy(
            src_ref=slot(t, *block) if src is None else src, dst_ref=slot(t, *block),
            send_sem=send.at[t, k], recv_sem=recv.at[t, k], device_id=to, device_id_type=MESH)

    mine = [pltpu.make_async_copy(ins[t], slot(t, *me), lsem.at[t]) for t in range(nt)]
    first = [copy(t, k, me, to, src=ins[t]) for t in range(nt) for k, to in ((0, sib), (1, (*xn, c)), (2, (*yn, c)))]

    def start():
        for cp in mine + first:
            cp.start()

    def landed(k, chip, also_to=None):
        for t in range(nt):
            copy(t, k, (*chip, c), me).wait_recv()
            if also_to is not None:
                copy(t, 3, (*chip, c), (*also_to, c)).start()
            copy(t, 3 + k, (*chip, c), sib).start()

    def relay():
        pl.when(c == 0)(lambda: landed(1, xn, also_to=yn))
        pl.when(c == 1)(lambda: landed(2, yn, also_to=xn))

    def finish():
        pl.when(c == 0)(lambda: landed(2, yn))
        pl.when(c == 1)(lambda: landed(1, xn))
        landed(3, diag)
        for t in range(nt):
            copy(t, 0, sib, me).wait_recv()
            for k, chip in ((4, xn), (5, yn), (6, diag)):
                copy(t, k, (*chip, 1 - c), me).wait_recv()
            for k in range(7):
                copy(t, k, me, sib).wait_send()
        for cp in mine:
            cp.wait()

    return start, relay, finish


def _gather_scratch(nt):
    return [pltpu.SemaphoreType.DMA((nt, 7)), pltpu.SemaphoreType.DMA((nt, 7)), pltpu.SemaphoreType.DMA((nt,))]


def _gathered_shapes(shards):
    return [jax.ShapeDtypeStruct((N_DEV,) + s.shape, s.dtype) for s in shards]


def _call_with_gather(body, n_grid, shards, *, name, in_specs, out_specs, out_shape, scratch_shapes, vmem_mb, args):
    ng, n_in, n_out = len(shards), len(in_specs), len(out_specs)

    def wrapped(*refs):
        ins, shard_refs = refs[:n_in], refs[n_in:n_in + ng]
        outs = refs[n_in + ng:n_in + ng + n_out]
        whole_refs = refs[n_in + ng + n_out:n_in + 2 * ng + n_out]
        scratch = refs[n_in + 2 * ng + n_out:]
        if ng:
            start, relay, finish = _gather_steps(shard_refs, whole_refs, *scratch[len(scratch_shapes):])
            pl.when(pl.program_id(0) == 0)(start)
            pl.when(pl.program_id(0) == n_grid // 2)(relay)
        body(*ins, *outs, *scratch[:len(scratch_shapes)])
        if ng:
            pl.when(pl.program_id(0) == n_grid - 1)(finish)

    return pl.pallas_call(
        wrapped, grid=(n_grid,), name=name,
        in_specs=list(in_specs) + [ANY] * ng, out_specs=list(out_specs) + [ANY] * ng,
        out_shape=list(out_shape) + _gathered_shapes(shards),
        scratch_shapes=list(scratch_shapes) + (_gather_scratch(ng) if ng else []),
        compiler_params=_cparams(vmem_mb))(*args, *shards)


def _chips_steps(ins, outs, send, recv):
    _, _, c, chips = _place()
    copies = [pltpu.make_async_remote_copy(
        src_ref=ins[t].at[2 * px + py], dst_ref=outs[t].at[j], send_sem=send.at[t, j], recv_sem=recv.at[t, j],
        device_id=(px, py, c), device_id_type=MESH) for t in range(len(ins)) for j, (px, py) in enumerate(chips)]

    def start():
        for cp in copies:
            cp.start()

    def finish():
        for cp in copies:
            cp.wait()

    return start, finish


def _chips_scratch(nt):
    return [pltpu.SemaphoreType.DMA((nt, 3)), pltpu.SemaphoreType.DMA((nt, 3))]


def _chips_shapes(cs16s):
    return [jax.ShapeDtypeStruct((3,) + g.shape[1:], g.dtype) for g in cs16s]


SMALL = (("g_pre_mix", 0, 0, D), ("g_mem", 1, 0, D), ("g_post_mix", 2, 0, D), ("g_attn_out", 3, 0, AW),
         ("g_conv_out", 3, AW, CW), ("g_xattn_out", 3, AW + CW, XW), ("g_post_mlp", 4, 0, D), ("g_pre_mlp", 5, 0, D))
CONV_ROW = 8
PACK_ROWS = 16


LOSS_ROW = 15


def _small_all_reduce(dg_in, dg_mem, dgs, dg_mlp, dcw, loss8):
    def body(acc_in, acc_mem, acc_mix, acc_mlp, acc_cw, acc_loss, tot_ref, pack, land, send, recv):
        x, y, c, _ = _place()
        me = 4 * x + 2 * y + c
        pack[...] = jnp.zeros_like(pack)
        pack[0:1, :] = acc_in[0:1, :]
        pack[1:2, :] = acc_mem[0:1, :]
        pack[2:4, :] = acc_mix[0:2, :]
        pack[4:6, :] = acc_mlp[0:2, :]
        pack[CONV_ROW:CONV_ROW + 3, 0:CW] = acc_cw[0:3, :]
        pack[LOSS_ROW:LOSS_ROW + 1, 0:LANES] = acc_loss[0:1, :]
        land[me] = pack[...]
        copies = []
        for k in range(1, N_DEV):
            kx, ky, kc = (k >> 2) & 1, (k >> 1) & 1, k & 1
            peer = (1 - x if kx else x, 1 - y if ky else y, 1 - c if kc else c)
            copies.append(pltpu.make_async_remote_copy(
                src_ref=pack, dst_ref=land.at[me], send_sem=send.at[k - 1], recv_sem=recv.at[k - 1],
                device_id=peer, device_id_type=MESH))
        for cp in copies:
            cp.start()
        for cp in copies:
            cp.wait()
        tot = land[0]
        for s in range(1, N_DEV):
            tot = tot + land[s]
        tot_ref[...] = tot

    return pl.pallas_call(
        body, name="small_all_reduce", out_shape=jax.ShapeDtypeStruct((PACK_ROWS, D), F32),
        scratch_shapes=[pltpu.VMEM((PACK_ROWS, D), F32), pltpu.VMEM((N_DEV, PACK_ROWS, D), F32),
                        pltpu.SemaphoreType.DMA((N_DEV - 1,)), pltpu.SemaphoreType.DMA((N_DEV - 1,))],
    )(dg_in, dg_mem, dgs, dg_mlp, dcw, loss8)


def _small_update(tot, me, params):
    flat = [a for n, _, _, _ in SMALL for a in params[n]] + list(params["conv_w"])
    n_par = len(SMALL) + 1
    tap_cols = CW // N_DEV

    def body(*refs):
        me_ref, tot_ref = refs[0:2]
        ins = refs[2:2 + 3 * n_par]
        loss_out = refs[2 + 3 * n_par]
        outs = refs[3 + 3 * n_par:]
        tot = tot_ref[...]
        loss_out[...] = jnp.broadcast_to(tot[LOSS_ROW:LOSS_ROW + 1, 0:LANES], loss_out.shape)

        def update(i, g):
            w_ref, m_ref, v_ref = ins[3 * i:3 * i + 3]
            g_out, d_out, m_out, v_out = outs[4 * i:4 * i + 4]
            g_out[...] = g
            d_out[...], m_out[...], v_out[...] = _adamw_math(w_ref[...], g, m_ref[...], v_ref[...])

        for i, (_, row, lane0, width) in enumerate(SMALL):
            update(i, tot[row:row + 1, lane0:lane0 + width])
        me = me_ref[0]
        taps = pltpu.roll(tot[CONV_ROW:CONV_ROW + SUBLANES, 0:CW], jnp.where(me == 0, 0, CW - me * tap_cols), 1)
        update(n_par - 1, taps[0:3, 0:tap_cols])

    shapes = [jax.ShapeDtypeStruct(params[n][0].shape, F32) for n, _, _, _ in SMALL] + [
        jax.ShapeDtypeStruct(params["conv_w"][0].shape, F32)]
    vmem = pl.BlockSpec(memory_space=pltpu.VMEM)
    loss, *out = pl.pallas_call(
        body, name="small_update",
        in_specs=[pl.BlockSpec(memory_space=pltpu.SMEM)] + [vmem] * (1 + 3 * n_par),
        out_shape=[jax.ShapeDtypeStruct((SUBLANES, LANES), F32)] + [s for s in shapes for _ in range(4)],
    )(me, tot, *flat)
    names = [n for n, _, _, _ in SMALL] + ["conv_w"]
    return loss[0, 0], {n: out[4 * i:4 * i + 4] for i, n in enumerate(names)}


def _local_step(x, mem, pos, gains, shards, tgt, place):
    half = HEAD // 2
    inv_freq = jnp.float32(ROPE_THETA) ** (-(jnp.arange(half, dtype=F32) * 2.0 / HEAD))
    invf = jnp.tile(inv_freq, LANES // half)[None, :]
    sgn = jnp.tile(jnp.concatenate([-jnp.ones((half,), F32), jnp.ones((half,), F32)]), LANES // HEAD)[None, :]
    cos, sins, win8 = _rope_table(pos.astype(F32).reshape(S, 1), invf, sgn, [shards["w_in"]])
    wdn_left, wdn_right = shards["w_down"][:, 0:D // 2], shards["w_down"][:, D // 2:]
    q, kvp, bcu, qx16, h16, win16, wout8, wkv8, conv8, wdn8_right = _in_proj(
        x, gains["g_pre_mix"], win8, cos, sins, [shards["w_out"], shards["w_mem_kv"], shards["conv_w"], wdn_right])
    wout16, wkv16 = wout8.reshape(D, D), wkv8.reshape(D, 2 * XW)
    cw_full = conv8[:, 0:3, 0:CW // N_DEV].transpose(1, 0, 2).reshape(3, CW)
    cw8 = jnp.zeros((SUBLANES, CW), F32).at[0:3].set(cw_full)
    y_attn, ltot, wup8, wdn8_left = _attn_fwd(q, kvp, [shards["w_up"], wdn_left])
    wdn_halves = (wdn8_left.reshape(FF, D // 2), wdn8_right.reshape(FF, D // 2))
    memn16, kv16 = _mem_fwd(mem, gains["g_mem"], wkv16)
    ypre, y16, y2, x1 = _mix_out(y_attn, bcu, qx16, kv16, cw8, gains["g_attn_out"], gains["g_conv_out"],
                                 gains["g_xattn_out"], gains["g_post_mix"], wout16, x, [])
    a16, du16, h2_16, df2_16, dx1, loss8, dg_mlp = _mlp(
        x1, tgt, gains["g_pre_mlp"], gains["g_post_mlp"], wup8, wdn_halves)

    sums = {"w_up": _wgrad_cols(place, h2_16, du16, FF_BLK, "wgrad_up"),
            "w_down": _wgrad_cols(place, df2_16, a16, FF_BLK, "wgrad_down", square_b=True, transpose_out=True)}

    head_id = jnp.arange(AW, dtype=jnp.int32) // HEAD
    head_ones = (head_id[:, None] == head_id[None, :]).astype(BF16)
    dy2_16, qdo, ld, dbcu, dqx, dgs, dcw, dkv = _mix_out_bwd(
        dx1, y2, ypre, ltot, head_ones, q, bcu, qx16, kv16, cw8, gains["g_post_mix"], gains["g_attn_out"],
        gains["g_conv_out"], gains["g_xattn_out"], wout16)
    dkv16, dg_mem = _mem_bwd(mem, gains["g_mem"], wkv16, dkv)
    sums["w_mem_kv"] = _wgrad_rows(place, memn16, dkv16, "wgrad_mem_kv")
    sums["w_out"] = _wgrad_rows(place, y16, dy2_16, "wgrad_out")
    out = _attn_bwd(qdo, kvp, ld, [s[0] for s in sums.values()])
    dqkv, landed = out[:9], out[9:]
    reduced = {n: (s[1], landed[t]) for t, (n, s) in enumerate(sums.items())}
    dproj16, grad_x, dg_in = _in_proj_bwd(dqkv, dbcu, dqx, cos, sins, win16, x, gains["g_pre_mix"], dx1)

    _, in_own, in_landed = _wgrad_cols(place, h16, dproj16, PW // N_DEV, "wgrad_in", to_chips=True)
    reduced["w_in"] = (in_own, in_landed)
    return grad_x, reduced, (dg_in, dg_mem, dgs, dg_mlp, dcw, loss8)


BIG = ("w_in", "w_mem_kv", "w_out", "w_up", "w_down")
ORDER = ("g_pre_mix", "g_mem", "w_in", "w_mem_kv", "conv_w", "g_attn_out", "g_conv_out", "g_xattn_out", "w_out",
         "g_post_mix", "g_pre_mlp", "w_up", "w_down", "g_post_mlp")


def kernel(x, mem, positions, g_pre_mix, g_mem, w_in, w_mem_kv, conv_w, g_attn_out, g_conv_out, g_xattn_out, w_out, g_post_mix, g_pre_mlp, w_up, w_down, g_post_mlp, loss_target, m_g_pre_mix, m_g_mem, m_w_in, m_w_mem_kv, m_conv_w, m_g_attn_out, m_g_conv_out, m_g_xattn_out, m_w_out, m_g_post_mix, m_g_pre_mlp, m_w_up, m_w_down, m_g_post_mlp, v_g_pre_mix, v_g_mem, v_w_in, v_w_mem_kv, v_conv_w, v_g_attn_out, v_g_conv_out, v_g_xattn_out, v_w_out, v_g_post_mix, v_g_pre_mlp, v_w_up, v_w_down, v_g_post_mlp):
    w = dict(g_pre_mix=g_pre_mix, g_mem=g_mem, w_in=w_in, w_mem_kv=w_mem_kv, conv_w=conv_w, g_attn_out=g_attn_out,
             g_conv_out=g_conv_out, g_xattn_out=g_xattn_out, w_out=w_out, g_post_mix=g_post_mix, g_pre_mlp=g_pre_mlp,
             w_up=w_up, w_down=w_down, g_post_mlp=g_post_mlp)
    mo = dict(g_pre_mix=m_g_pre_mix, g_mem=m_g_mem, w_in=m_w_in, w_mem_kv=m_w_mem_kv, conv_w=m_conv_w,
              g_attn_out=m_g_attn_out, g_conv_out=m_g_conv_out, g_xattn_out=m_g_xattn_out, w_out=m_w_out,
              g_post_mix=m_g_post_mix, g_pre_mlp=m_g_pre_mlp, w_up=m_w_up, w_down=m_w_down, g_post_mlp=m_g_post_mlp)
    vo = dict(g_pre_mix=v_g_pre_mix, g_mem=v_g_mem, w_in=v_w_in, w_mem_kv=v_w_mem_kv, conv_w=v_conv_w,
              g_attn_out=v_g_attn_out, g_conv_out=v_g_conv_out, g_xattn_out=v_g_xattn_out, w_out=v_w_out,
              g_post_mix=v_g_post_mix, g_pre_mlp=v_g_pre_mlp, w_up=v_w_up, w_down=v_w_down, g_post_mlp=v_g_post_mlp)

    xi, yi, ci = lax.axis_index("x"), lax.axis_index("y"), lax.axis_index("c")
    me = 4 * xi + 2 * yi + ci
    place = jnp.stack([ci, 2 * xi + yi]).astype(jnp.int32)

    shards = {n: w[n][0].astype(BF16) for n in BIG}
    shards["conv_w"] = jnp.zeros((SUBLANES, LANES), F32).at[0:3, 0:CW // N_DEV].set(conv_w[0])

    gains = {n: w[n] for n, _, _, _ in SMALL}
    grad_x, reduced, small_acc = _local_step(x[0], mem[0], positions[0], gains, shards, loss_target[0], place)

    updated = {}
    for group in (("w_up", "w_down"), ("w_in", "w_out", "w_mem_kv")):
        updated.update(_adamw_shards({n: (*reduced[n], w[n][0], mo[n][0], vo[n][0]) for n in group},
                                     "adamw_" + "_".join(group))[0])
    grad, delta, new_m, new_v = {}, {}, {}, {}
    for n, (g, d_, m_, v_) in updated.items():
        grad[n], delta[n], new_m[n], new_v[n] = g[None], d_[None], m_[None], v_[None]

    params = {n: (w[n], mo[n], vo[n]) for n, _, _, _ in SMALL}
    params["conv_w"] = (w["conv_w"][0], mo["conv_w"][0], vo["conv_w"][0])
    loss, small = _small_update(_small_all_reduce(*small_acc), me.reshape(1).astype(jnp.int32), params)
    for n, (g, d_, m_, v_) in small.items():
        lead = (lambda a: a[None]) if n == "conv_w" else (lambda a: a)
        grad[n], delta[n], new_m[n], new_v[n] = lead(g), lead(d_), lead(m_), lead(v_)

    return (loss, grad_x[None], *[grad[n] for n in ORDER], *[delta[n] for n in ORDER],
            *[new_m[n] for n in ORDER], *[new_v[n] for n in ORDER])
```

```python
import functools

import numpy as np
import jax
import jax.numpy as jnp
from jax import lax
from jax.experimental import pallas as pl
from jax.experimental.pallas import tpu as pltpu

F32, BF16 = jnp.float32, jnp.bfloat16
MESH = pl.DeviceIdType.MESH
ANY = pl.BlockSpec(memory_space=pl.ANY)

N_DEV = 8
D = 1024
S = 4096
N_MEM = 256
HEAD = 64
AW, CW, XW = 512, 256, 256
PW = 3 * AW + 3 * CW + XW
FF = 4096
FF_BLK = FF // N_DEV
PATTERNS = ((128, 1), (512, 4), (2048, 16))
QB = 128
EPS = 1e-6
NEG = -1e30
SCALE = HEAD ** -0.5
ROPE_THETA = 10000.0
LANES = 128
SUBLANES = 8

ADAM_LR, ADAM_B1, ADAM_B2, ADAM_EPS, ADAM_WD, ADAM_STEP = 0.001, 0.9, 0.999, 1e-08, 0.01, 10

TQ = 512
TQ_MLP = 256
NT = S // TQ


def _cparams(vmem_mb, n_grid=1):
    return pltpu.CompilerParams(dimension_semantics=("arbitrary",) * n_grid, vmem_limit_bytes=vmem_mb << 20)


def _const(shape):
    nd = len(shape)
    return pl.BlockSpec(shape, lambda *_: (0,) * nd, pipeline_mode=pl.Buffered(1))


def _acc(shape):
    nd = len(shape)
    return pl.BlockSpec(shape, lambda *_: (0,) * nd)


def _dot(a, b):
    return jnp.dot(a, b, preferred_element_type=F32)


def _dot_nt(a, b):
    return lax.dot_general(a, b, (((1,), (1,)), ((), ())), preferred_element_type=F32)


def _dot_tn(a, b):
    return lax.dot_general(a, b, (((0,), (0,)), ((), ())), preferred_element_type=F32)


def _rms(x, g):
    r = lax.rsqrt(jnp.mean(x * x, axis=-1, keepdims=True) + EPS)
    n = x * r
    return n * g, n, r


def _rms_bwd(dy, n, r, g):
    dn = dy * g
    dx = r * (dn - n * jnp.mean(dn * n, axis=-1, keepdims=True))
    return dx, jnp.sum(dy * n, axis=0, keepdims=True)


def _rot_half(t):
    lane = lax.broadcasted_iota(jnp.int32, t.shape, 1)
    n = t.shape[1]
    return jnp.where((lane % HEAD) < HEAD // 2, pltpu.roll(t, n - HEAD // 2, 1), pltpu.roll(t, HEAD // 2, 1))


def _rope_table(pos_col, invf, sgn, shards):
    def body(p_ref, f_ref, s_ref, c_out, s_out):
        ang = p_ref[...] * f_ref[...]
        c_out[...] = jnp.cos(ang)
        s_out[...] = jnp.sin(ang) * s_ref[...]

    tile = pl.BlockSpec((TQ, LANES), lambda i: (i, 0))
    return _call_with_gather(
        body, NT, shards, name="rope_table",
        in_specs=[pl.BlockSpec((TQ, 1), lambda i: (i, 0)), _const((1, LANES)), _const((1, LANES))],
        out_specs=[tile, tile], out_shape=[jax.ShapeDtypeStruct((S, LANES), F32)] * 2,
        scratch_shapes=[], vmem_mb=32, args=(pos_col, invf, sgn))


def _all_heads(t):
    return jnp.tile(t, (1, AW // LANES))


def _mem_fwd(mem, g_mem, wkv16):
    def body(m_ref, g_ref, w_ref, n16_ref, kv_ref):
        y, _, _ = _rms(m_ref[...], g_ref[...])
        y16 = y.astype(BF16)
        n16_ref[...] = y16
        kv_ref[...] = _dot(y16, w_ref[...]).astype(BF16)

    return pl.pallas_call(
        body, name="mem_fwd",
        out_shape=[jax.ShapeDtypeStruct((N_MEM, D), BF16), jax.ShapeDtypeStruct((N_MEM, 2 * XW), BF16)],
        compiler_params=pltpu.CompilerParams(vmem_limit_bytes=32 << 20))(mem, g_mem, wkv16)


def _in_proj(x, g, w8, cos, sins, shards):
    blk = PW // N_DEV

    def body(x_ref, g_ref, w8_ref, c_ref, s_ref, q_ref, kv_ref, bcu_ref, qx_ref, h_ref, w_out, w_ref):
        @pl.when(pl.program_id(0) == 0)
        def _():
            for j in range(N_DEV):
                w_ref[:, j * blk:(j + 1) * blk] = w8_ref[j]
            w_out[...] = w_ref[...]

        y, _, _ = _rms(x_ref[...], g_ref[...])
        h = y.astype(BF16)
        h_ref[...] = h
        proj = _dot(h, w_ref[...])
        cos, sn = _all_heads(c_ref[...]), _all_heads(s_ref[...])
        q, k = proj[:, 0:AW], proj[:, AW:2 * AW]
        q_ref[...] = (q * cos + _rot_half(q) * sn) * SCALE
        kv_ref[...] = _pack_pair(k * cos + _rot_half(k) * sn, proj[:, 2 * AW:3 * AW])
        bcu_ref[...] = proj[:, 3 * AW:3 * AW + 3 * CW]
        qx_ref[...] = (proj[:, 3 * AW + 3 * CW:] * SCALE).astype(BF16)

    def tile(w):
        return pl.BlockSpec((TQ, w), lambda i: (i, 0))

    return _call_with_gather(
        body, NT, shards, name="in_proj",
        in_specs=[tile(D), _const((1, D)), _const((N_DEV, D, blk)), tile(LANES), tile(LANES)],
        out_specs=[tile(AW), tile(AW), tile(3 * CW), tile(XW), tile(D), _acc((D, PW))],
        out_shape=[jax.ShapeDtypeStruct((S, AW), F32)] * 2 + [
            jax.ShapeDtypeStruct((S, 3 * CW), F32), jax.ShapeDtypeStruct((S, XW), BF16),
            jax.ShapeDtypeStruct((S, D), BF16), jax.ShapeDtypeStruct((D, PW), BF16)],
        scratch_shapes=[pltpu.VMEM((D, PW), BF16)], vmem_mb=56, args=(x, g, w8, cos, sins))


ATTN_PLANS = (("p1", 1, 128, 32), ("p4", 8, 64, 8), ("p16", 16, 128, 2))
PAD = 128
WIN = 256


ATTN_UNROLL = 8


def _fill_bias(tab, qblk, partner):
    qi = lax.broadcasted_iota(jnp.int32, (2 * qblk, WIN), 0) & (qblk - 1)
    kj = lax.broadcasted_iota(jnp.int32, (2 * qblk, WIN), 1)
    piece = kj >> (qblk.bit_length() - 1)
    kk = kj & (qblk - 1)
    prev = (piece & 1) == 0
    of_partner = piece >= 2
    for first in (0, 1):
        for par in (0, 1):
            lo = jnp.where(prev, (qblk if first else qi) + jnp.where(of_partner, par, 0), 0)
            hi = jnp.where(prev, qblk, qi + jnp.where(of_partner, par - 1, 0))
            tab[2 * first + par] = jnp.where((kk >= lo) & (kk <= hi), 0.0, NEG).astype(F32)


def _block_rows(g, qblk, nbc, partner):
    own = pl.ds(pl.multiple_of(PAD + g * qblk, qblk), qblk)
    first = ((g & (nbc - 1)) == 0).astype(jnp.int32)
    if partner:
        gp = jnp.bitwise_xor(g, 4 * nbc)
        wins = (pl.ds(pl.multiple_of(PAD + (g - 1) * qblk, qblk), 2 * qblk),
                pl.ds(pl.multiple_of(PAD + (gp - 1) * qblk, qblk), 2 * qblk))
        return own, wins, 2 * first + ((g >> ((4 * nbc).bit_length() - 1)) & 1)
    return own, (pl.ds(pl.multiple_of(PAD + (g - 1) * qblk, qblk), 2 * qblk),), 2 * first


def _pack_pair(lo, hi):
    lo_bits = lax.bitcast_convert_type(lo.astype(BF16).astype(F32), jnp.uint32) >> 16
    hi_bits = lax.bitcast_convert_type(hi.astype(BF16).astype(F32), jnp.uint32) & jnp.uint32(0xFFFF0000)
    return lax.bitcast_convert_type(hi_bits | lo_bits, F32)


def _unpack_pair(c):
    bits = lax.bitcast_convert_type(c, jnp.uint32)
    lo = lax.bitcast_convert_type(bits << 16, F32).astype(BF16)
    hi = lax.bitcast_convert_type(bits & jnp.uint32(0xFFFF0000), F32).astype(BF16)
    return lo, hi


def _window(ref, wins):
    parts = [ref[w, :] for w in wins]
    return parts[0] if len(parts) == 1 else jnp.concatenate(parts, axis=0)


def _stack_heads(t, lane):
    zero = jnp.zeros_like(t)
    return jnp.concatenate([jnp.where(lane < HEAD, t, zero), jnp.where(lane >= HEAD, t, zero)], axis=0)


def _unstack_heads(t2, lane):
    half = t2.shape[0] // 2
    return jnp.where(lane < HEAD, t2[0:half, :], t2[half:, :])


def _lanes_of(step):
    return pl.ds(pl.multiple_of(step * LANES, LANES), LANES)


def _whole_wait(buf, sem):
    whole = buf.at[pl.ds(PAD, S), :]
    return pltpu.make_async_copy(whole, whole, sem)


def _whole_waits(bufs, sems):
    return [_whole_wait(buf, sems.at[i]) for i, buf in enumerate(bufs)]


def _class_gather(views, bufs, sems, lanes):
    copies = []
    for i, (view, buf) in enumerate(zip(views, bufs)):
        if view.ndim == 2:
            copies.append(pltpu.make_async_copy(view.at[:, lanes], buf.at[pl.ds(PAD, S), :], sems.at[i]))
        else:
            per, n_cls = view.shape[0], view.shape[1]
            copies += [pltpu.make_async_copy(view.at[:, c, lanes], buf.at[pl.ds(PAD + c * per, per), :], sems.at[i])
                       for c in range(n_cls)]
    return copies


def _class_scatter(bufs, dsts, sems, lanes=None):
    copies = []
    for i, (buf, dst) in enumerate(zip(bufs, dsts)):
        if dst.ndim == 2:
            copies.append(pltpu.make_async_copy(buf.at[pl.ds(PAD, S), :], dst.at[:, lanes], sems.at[i]))
            continue
        per, n_cls = dst.shape[0], dst.shape[1]
        for c in range(n_cls):
            to = dst.at[:, c, :] if lanes is None else dst.at[:, c, lanes]
            copies.append(pltpu.make_async_copy(buf.at[pl.ds(PAD + c * per, per), :], to, sems.at[i]))
    return copies


def _start(copies):
    for cp in copies:
        cp.start()


def _wait(waits):
    for w in waits:
        w.wait()


def _attn_fwd(q, kvp, shards=()):
    views = [[a] + [a.reshape(S // n, n, AW) for _, n, _, _ in ATTN_PLANS[1:]] for a in (q, kvp)]
    flat = [views[a][p] for p in range(3) for a in range(2)]
    ng = len(shards)
    n_grid = AW // LANES

    def body(*refs):
        hbm = [refs[2 * p:2 * p + 2] for p in range(3)]
        refs = refs[6:]
        shard_refs, refs = refs[:ng], refs[ng:]
        y_ref, lt_ref = refs[0:2]
        whole_refs, refs = refs[2:2 + ng], refs[2 + ng:]
        bufs = [refs[2 * p:2 * p + 2] for p in range(3)]
        oc4, lc4, oc16, lc16, tab128, tab4, sem_in = refs[6:13]
        step = pl.program_id(0)
        if ng:
            start_gather, relay_gather, finish_gather = _gather_steps(shard_refs, whole_refs, *refs[13:])
            pl.when(step == 0)(start_gather)
            pl.when(step == n_grid // 2)(relay_gather)
        now = [_class_gather(hbm[p], bufs[p], sem_in.at[p], _lanes_of(step)) for p in range(3)]
        nxt = [_class_gather(hbm[p], bufs[p], sem_in.at[p], _lanes_of(step + 1)) for p in range(3)]

        @pl.when(step == 0)
        def _():
            for p in range(3):
                _start(now[p])
                for b in bufs[p]:
                    b[0:PAD, :] = jnp.zeros((PAD, LANES), F32)
            _fill_bias(tab128, 128, False)
            _fill_bias(tab4, 64, True)

        def prefetch(p):
            pl.when(step + 1 < n_grid)(lambda: _start(nxt[p]))

        lane = lax.broadcasted_iota(jnp.int32, (1, LANES), 1)
        ones = jnp.ones((WIN, LANES), BF16)

        def run(plan, bq, bkv, tab, o_dst, l_dst, dst_pad):
            _, n_cls, qblk, nbc = plan
            partner = n_cls == 8

            def block(g, carry):
                own, wins, mask = _block_rows(g, qblk, nbc, partner)
                q2 = _stack_heads(bq[own, :].astype(BF16), lane)
                kw, vwin = _unpack_pair(_window(bkv, wins))
                vw = jnp.concatenate([vwin, ones], axis=1)
                s = _dot_nt(q2, kw) + tab[mask]
                m = jnp.max(s, axis=1, keepdims=True)
                oe = _dot(jnp.exp(s - m).astype(BF16), vw)
                den = oe[:, LANES:]
                dst = pl.ds(pl.multiple_of(dst_pad + g * qblk, qblk), qblk)
                o_dst[dst, :] = _unstack_heads(oe[:, 0:LANES] / den, lane)
                l_dst[dst, :] = _unstack_heads(m + jnp.log(den), lane)
                return carry
            lax.fori_loop(0, n_cls * nbc, block, 0, unroll=ATTN_UNROLL)

        _wait(_whole_waits(bufs[0], sem_in.at[0]))
        run(ATTN_PLANS[0], *bufs[0], tab128, y_ref, lt_ref, 0)
        prefetch(0)
        _wait(_whole_waits(bufs[1], sem_in.at[1]))
        run(ATTN_PLANS[1], *bufs[1], tab4, oc4, lc4, PAD)
        prefetch(1)
        _wait(_whole_waits(bufs[2], sem_in.at[2]))
        run(ATTN_PLANS[2], *bufs[2], tab128, oc16, lc16, PAD)
        prefetch(2)

        n_rows = 64

        def token_order(buf, t, n_cls):
            per = S // n_cls
            first = PAD + t * (n_rows // n_cls)
            return jnp.concatenate([buf[pl.ds(first + jj, n_cls, stride=per), :] for jj in range(n_rows // n_cls)],
                                   axis=0)

        def combine(t, carry):
            rows = pl.ds(pl.multiple_of(t * n_rows, n_rows), n_rows)
            l0, l1, l2 = lt_ref[rows, :], token_order(lc4, t, 8), token_order(lc16, t, 16)
            lm = jnp.maximum(jnp.maximum(l0, l1), l2)
            e0, e1, e2 = jnp.exp(l0 - lm), jnp.exp(l1 - lm), jnp.exp(l2 - lm)
            den = e0 + e1 + e2
            y_ref[rows, :] = (e0 * y_ref[rows, :] + e1 * token_order(oc4, t, 8)
                              + e2 * token_order(oc16, t, 16)) / den
            lt_ref[rows, :] = lm + jnp.log(den)
            return carry
        lax.fori_loop(0, S // n_rows, combine, 0, unroll=2)

        if ng:
            pl.when(step == n_grid - 1)(finish_gather)

    col = pl.BlockSpec((S, LANES), lambda h: (0, h))
    padded = pltpu.VMEM((PAD + S, LANES), F32)
    return pl.pallas_call(
        body, grid=(n_grid,), name="attn_fwd",
        in_specs=[ANY] * (6 + ng), out_specs=[col, col] + [ANY] * ng,
        out_shape=[jax.ShapeDtypeStruct((S, AW), F32)] * 2 + _gathered_shapes(shards),
        scratch_shapes=[padded] * 10 + [
            pltpu.VMEM((4, 256, WIN), F32), pltpu.VMEM((4, 128, WIN), F32), pltpu.SemaphoreType.DMA((3, 2))]
        + (_gather_scratch(ng) if ng else []),
        compiler_params=_cparams(56))(*flat, *shards)


def _conv_taps(z, zprev, row):
    z1 = jnp.where(row == 0, zprev[7:8, :], pltpu.roll(z, 1, 0))
    z2 = jnp.where(row == 0, zprev[6:7, :], jnp.where(row == 1, zprev[7:8, :], pltpu.roll(z, 2, 0)))
    return z1, z2


def _xattn_scores(qm, km):
    s = _dot_nt(qm, km)
    m = jnp.max(s, axis=1, keepdims=True)
    e = jnp.exp(s - m)
    return e, jnp.sum(e, axis=1, keepdims=True)


def _mix_out(y_attn, bcu, qx16, kv16, cw8, g_attn, g_conv, g_x, g_post, wout16, x, shards):
    def body(ya_ref, bcu_ref, halo_ref, qx_ref, kv_ref, cw_ref, ga_ref, gc_ref, gx_ref, gp_ref, w_ref, x_ref,
             ypre_ref, y16_ref, y2_ref, x1_ref):
        i = pl.program_id(0)
        bcu = bcu_ref[...]
        b, c, u = bcu[:, 0:CW], bcu[:, CW:2 * CW], bcu[:, 2 * CW:]
        z = c * u
        halo = halo_ref[...]
        zprev = jnp.where(i > 0, halo[:, CW:2 * CW] * halo[:, 2 * CW:], 0.0)
        row = lax.broadcasted_iota(jnp.int32, z.shape, 0)
        z1, z2 = _conv_taps(z, zprev, row)
        cw = cw_ref[...]
        y_conv = b * (z2 * cw[0:1, :] + z1 * cw[1:2, :] + z * cw[2:3, :])

        qx = qx_ref[...]
        kv = kv_ref[...]
        km, vm = kv[:, 0:XW], kv[:, XW:]
        lane = lax.broadcasted_iota(jnp.int32, qx.shape, 1)
        y_x = jnp.zeros(qx.shape, F32)
        for h in range(XW // HEAD):
            hm = (lane >= h * HEAD) & (lane < (h + 1) * HEAD)
            e, l = _xattn_scores(jnp.where(hm, qx, jnp.zeros_like(qx)), km)
            y_x = jnp.where(hm, _dot(e.astype(BF16), vm) / l, y_x)

        y_attn = ya_ref[...]
        ypre_ref[:, 0:AW] = y_attn
        ypre_ref[:, AW:AW + CW] = y_conv
        ypre_ref[:, AW + CW:] = y_x
        y = jnp.concatenate([_rms(y_attn, ga_ref[...])[0], _rms(y_conv, gc_ref[...])[0],
                             _rms(y_x, gx_ref[...])[0]], axis=1).astype(BF16)
        y16_ref[...] = y
        y2 = _dot(y, w_ref[...])
        y2_ref[...] = y2
        x1_ref[...] = x_ref[...] + _rms(y2, gp_ref[...])[0]

    def tile(w):
        return pl.BlockSpec((TQ, w), lambda i: (i, 0))

    halo = pl.BlockSpec((SUBLANES, 3 * CW), lambda i: (jnp.maximum(i * (TQ // SUBLANES) - 1, 0), 0))
    return _call_with_gather(
        body, NT, shards, name="mix_out",
        in_specs=[tile(AW), tile(3 * CW), halo, tile(XW), _const((N_MEM, 2 * XW)), _const((SUBLANES, CW)),
                  _const((1, AW)), _const((1, CW)), _const((1, XW)), _const((1, D)), _const((D, D)), tile(D)],
        out_specs=[tile(D), tile(D), tile(D), tile(D)],
        out_shape=[jax.ShapeDtypeStruct((S, D), F32), jax.ShapeDtypeStruct((S, D), BF16),
                   jax.ShapeDtypeStruct((S, D), F32), jax.ShapeDtypeStruct((S, D), F32)],
        scratch_shapes=[], vmem_mb=56,
        args=(y_attn, bcu, bcu, qx16, kv16, cw8, g_attn, g_conv, g_x, g_post, wout16, x))


def _mlp(x1, tgt, g_pre, g_post, wup8, wdn16):
    tq = TQ_MLP

    def body(x1_ref, t_ref, g1_ref, g2_ref, wu_ref, wd_ref,
             a16_ref, du_ref, h2_ref, df2_ref, dx1_ref, loss_ref, dg_ref, a32):
        @pl.when(pl.program_id(0) == 0)
        def _():
            loss_ref[...] = jnp.zeros_like(loss_ref)
            dg_ref[...] = jnp.zeros_like(dg_ref)

        x1 = x1_ref[...]
        g1, g2 = g1_ref[...], g2_ref[...]
        y1, n1, r1 = _rms(x1, g1)
        h2 = y1.astype(BF16)
        h2_ref[...] = h2
        f2 = jnp.zeros((tq, D), F32)
        for j in range(N_DEV):
            cols = slice(j * FF_BLK, (j + 1) * FF_BLK)
            a = jnp.maximum(_dot(h2, wu_ref[j]), 0.0)
            a32[:, cols] = a
            a16_ref[:, cols] = a.astype(BF16)
            f2 = f2 + _dot((a * a).astype(BF16), wd_ref[cols, :])
        y2, n2, r2 = _rms(f2, g2)
        e = x1 + y2 - t_ref[...]
        sq = jnp.sum(jnp.sum(e * e, axis=1, keepdims=True), axis=0, keepdims=True)
        loss_ref[...] += jnp.broadcast_to(sq * (0.5 / D), loss_ref.shape)
        dout = e * (1.0 / D)
        df2, dg2 = _rms_bwd(dout, n2, r2, g2)
        df2_16 = df2.astype(BF16)
        df2_ref[...] = df2_16
        dh2 = jnp.zeros((tq, D), F32)
        for j in range(N_DEV):
            cols = slice(j * FF_BLK, (j + 1) * FF_BLK)
            du = (_dot_nt(df2_16, wd_ref[cols, :]) * (2.0 * a32[:, cols])).astype(BF16)
            du_ref[:, cols] = du
            dh2 = dh2 + _dot_nt(du, wu_ref[j])
        dx, dg1 = _rms_bwd(dh2, n1, r1, g1)
        dx1_ref[...] = dout + dx
        dg_ref[0:1, :] += dg2
        dg_ref[1:2, :] += dg1

    def tile(w):
        return pl.BlockSpec((tq, w), lambda i: (i, 0))

    return pl.pallas_call(
        body, grid=(S // tq,), name="mlp",
        in_specs=[tile(D), tile(D), _const((1, D)), _const((1, D)), _const((N_DEV, D, FF_BLK)), _const((FF, D))],
        out_specs=[tile(FF), tile(FF), tile(D), tile(D), tile(D), _acc((SUBLANES, LANES)), _acc((SUBLANES, D))],
        out_shape=[jax.ShapeDtypeStruct((S, FF), BF16), jax.ShapeDtypeStruct((S, FF), BF16),
                   jax.ShapeDtypeStruct((S, D), BF16), jax.ShapeDtypeStruct((S, D), BF16),
                   jax.ShapeDtypeStruct((S, D), F32), jax.ShapeDtypeStruct((SUBLANES, LANES), F32),
                   jax.ShapeDtypeStruct((SUBLANES, D), F32)],
        scratch_shapes=[pltpu.VMEM((tq, FF), F32)],
        compiler_params=_cparams(56))(x1, tgt, g_pre, g_post, wup8, wdn16)


def _mix_out_bwd(dx1, y2, ypre, ltot, head_ones, q, bcu, qx16, kv16, cw8, g_post, g_attn, g_conv, g_x, wout16):
    def body(dx1_ref, y2_ref, ypre_ref, lt_ref, e_ref, q_ref, bcu_ref, halo_ref, qx_ref, kv_ref, cw_ref, gp_ref,
             ga_ref, gc_ref, gx_ref, w_ref, dy2_ref, qdo_ref, ld_ref, dbcu_ref, dqx_ref, dgs_ref, dcw_ref, dkv_ref,
             carry):
        i = pl.program_id(0)

        @pl.when(i == 0)
        def _():
            dgs_ref[...] = jnp.zeros_like(dgs_ref)
            dcw_ref[...] = jnp.zeros_like(dcw_ref)
            dkv_ref[...] = jnp.zeros_like(dkv_ref)
            carry[...] = jnp.zeros_like(carry)

        gp = gp_ref[...]
        _, n, r = _rms(y2_ref[...], gp)
        dy2, dgp = _rms_bwd(dx1_ref[...], n, r, gp)
        dy2_16 = dy2.astype(BF16)
        dy2_ref[...] = dy2_16
        dy = _dot_nt(dy2_16, w_ref[...])

        ypre = ypre_ref[...]
        ga, gc, gx = ga_ref[...], gc_ref[...], gx_ref[...]
        _, na, ra = _rms(ypre[:, 0:AW], ga)
        dya, dga = _rms_bwd(dy[:, 0:AW], na, ra, ga)
        _, nc, rc = _rms(ypre[:, AW:AW + CW], gc)
        dyc, dgc = _rms_bwd(dy[:, AW:AW + CW], nc, rc, gc)
        y_x = ypre[:, AW + CW:]
        _, nx, rx = _rms(y_x, gx)
        dyx, dgx = _rms_bwd(dy[:, AW + CW:], nx, rx, gx)
        qdo_ref[...] = _pack_pair(q_ref[...], dya)
        prod = dya * ypre[:, 0:AW]
        hi = prod.astype(BF16)
        lo = (prod - hi.astype(F32)).astype(BF16)
        head_sum = _dot(hi, e_ref[...]) + _dot(lo, e_ref[...])
        lane_a = lax.broadcasted_iota(jnp.int32, prod.shape, 1)
        ld_ref[...] = jnp.where((lane_a % HEAD) < HEAD // 2, lt_ref[...], head_sum)
        dgs_ref[0:1, :] += dgp
        dgs_ref[1:2, :] += jnp.concatenate([dga, dgc, dgx], axis=1)

        bcu = bcu_ref[...]
        b, c, u = bcu[:, 0:CW], bcu[:, CW:2 * CW], bcu[:, 2 * CW:]
        z = c * u
        halo = halo_ref[...]
        zprev = jnp.where(i < NT - 1, halo[:, CW:2 * CW] * halo[:, 2 * CW:], 0.0)
        row = lax.broadcasted_iota(jnp.int32, z.shape, 0)
        z1, z2 = _conv_taps(z, zprev, row)
        cw = cw_ref[...]
        conv = z2 * cw[0:1, :] + z1 * cw[1:2, :] + z * cw[2:3, :]
        dconv = dyc * b
        nxt = carry[...]
        dn1 = jnp.where(row == TQ - 1, nxt[0:1, :], pltpu.roll(dconv, TQ - 1, 0))
        dn2 = jnp.where(row == TQ - 1, nxt[1:2, :], jnp.where(row == TQ - 2, nxt[0:1, :], pltpu.roll(dconv, TQ - 2, 0)))
        carry[...] = dconv[0:SUBLANES, :]
        dz = dconv * cw[2:3, :] + dn1 * cw[1:2, :] + dn2 * cw[0:1, :]
        dbcu_ref[:, 0:CW] = (dyc * conv).astype(BF16)
        dbcu_ref[:, CW:2 * CW] = (dz * u).astype(BF16)
        dbcu_ref[:, 2 * CW:] = (dz * c).astype(BF16)
        dcw_ref[0:1, :] += jnp.sum(z2 * dconv, axis=0, keepdims=True)
        dcw_ref[1:2, :] += jnp.sum(z1 * dconv, axis=0, keepdims=True)
        dcw_ref[2:3, :] += jnp.sum(z * dconv, axis=0, keepdims=True)

        qx = qx_ref[...]
        kv = kv_ref[...]
        km, vm = kv[:, 0:XW], kv[:, XW:]
        lane = lax.broadcasted_iota(jnp.int32, qx.shape, 1)
        dqx = jnp.zeros(qx.shape, F32)
        dkm = jnp.zeros((N_MEM, XW), F32)
        dvm = jnp.zeros((N_MEM, XW), F32)
        for h in range(XW // HEAD):
            hm = (lane >= h * HEAD) & (lane < (h + 1) * HEAD)
            qm = jnp.where(hm, qx, jnp.zeros_like(qx))
            e, l = _xattn_scores(qm, km)
            p = e / l
            dom = jnp.where(hm, dyx, 0.0)
            do16 = dom.astype(BF16)
            dsum = jnp.sum(dom * y_x, axis=1, keepdims=True)
            ds = (p * (_dot_nt(do16, vm) - dsum)).astype(BF16)
            dqx = jnp.where(hm, _dot(ds, km), dqx)
            dkm = dkm + _dot_tn(ds, qm)
            dvm = dvm + _dot_tn(p.astype(BF16), do16)
        dqx_ref[...] = (dqx * SCALE).astype(BF16)
        dkv_ref[:, 0:XW] += dkm
        dkv_ref[:, XW:] += dvm

    def tile(w):
        return pl.BlockSpec((TQ, w), lambda i: (NT - 1 - i, 0))

    halo = pl.BlockSpec((SUBLANES, 3 * CW), lambda i: (jnp.maximum((NT - 1 - i) * (TQ // SUBLANES) - 1, 0), 0))
    return pl.pallas_call(
        body, grid=(NT,), name="mix_out_bwd",
        in_specs=[tile(D), tile(D), tile(D), tile(AW), _const((AW, AW)), tile(AW), tile(3 * CW), halo, tile(XW),
                  _const((N_MEM, 2 * XW)), _const((SUBLANES, CW)), _const((1, D)), _const((1, AW)), _const((1, CW)),
                  _const((1, XW)), _const((D, D))],
        out_specs=[tile(D), tile(AW), tile(AW), tile(3 * CW), tile(XW), _acc((SUBLANES, D)), _acc((SUBLANES, CW)),
                   _acc((N_MEM, 2 * XW))],
        out_shape=[jax.ShapeDtypeStruct((S, D), BF16), jax.ShapeDtypeStruct((S, AW), F32),
                   jax.ShapeDtypeStruct((S, AW), F32),
                   jax.ShapeDtypeStruct((S, 3 * CW), BF16), jax.ShapeDtypeStruct((S, XW), BF16),
                   jax.ShapeDtypeStruct((SUBLANES, D), F32), jax.ShapeDtypeStruct((SUBLANES, CW), F32),
                   jax.ShapeDtypeStruct((N_MEM, 2 * XW), F32)],
        scratch_shapes=[pltpu.VMEM((SUBLANES, CW), F32)],
        compiler_params=_cparams(56))(dx1, y2, ypre, ltot, head_ones, q, bcu, bcu, qx16, kv16, cw8, g_post, g_attn,
                                      g_conv, g_x, wout16)


def _attn_bwd(qdo, kvp, ld, chip_sums=()):
    n_in = 3
    views = [[a] + [a.reshape(S // n, n, AW) for _, n, _, _ in ATTN_PLANS[1:]] for a in (qdo, kvp, ld)]
    flat = [views[a][p] for p in range(3) for a in range(n_in)]
    ns = len(chip_sums)
    n_grid = AW // LANES

    def body(*refs):
        hbm = [refs[n_in * p:n_in * p + n_in] for p in range(3)]
        refs = refs[3 * n_in:]
        sum_refs, refs = refs[:ns], refs[ns:]
        outs = [refs[3 * p:3 * p + 3] for p in range(3)]
        landed_refs, sc = refs[9:9 + ns], refs[9 + ns:]
        bufs = [sc[3 * p:3 * p + 3] for p in range(3)]
        res = [sc[9 + 3 * p:12 + 3 * p] for p in range(3)]
        tab128, tab4, sem_in, sem_out = sc[18:22]
        step = pl.program_id(0)
        if ns:
            start_chips, finish_chips = _chips_steps(sum_refs, landed_refs, *sc[22:])
            pl.when(step == 0)(start_chips)
        now = [_class_gather(hbm[p], bufs[p], sem_in.at[p], _lanes_of(step)) for p in range(3)]
        nxt = [_class_gather(hbm[p], bufs[p], sem_in.at[p], _lanes_of(step + 1)) for p in range(3)]

        @pl.when(step == 0)
        def _():
            for p in range(3):
                _start(now[p])
                for b in bufs[p]:
                    b[0:PAD, :] = jnp.zeros((PAD, LANES), F32)
            _fill_bias(tab128, 128, False)
            _fill_bias(tab4, 64, True)

        def prefetch(p):
            pl.when(step + 1 < n_grid)(lambda: _start(nxt[p]))

        for p in range(3):
            for b in res[p]:
                b[...] = jnp.zeros_like(b)
        lane = lax.broadcasted_iota(jnp.int32, (1, LANES), 1)

        def run(plan, plan_bufs, tab, dst):
            _, n_cls, qblk, nbc = plan
            partner = n_cls == 8
            bqdo, bkv, bld = plan_bufs
            rq, rk, rv = dst

            def block(g, carry):
                own, wins, mask = _block_rows(g, qblk, nbc, partner)
                qb, dob = _unpack_pair(bqdo[own, :])
                q2, do2 = _stack_heads(qb, lane), _stack_heads(dob, lane)
                kw, vw = _unpack_pair(_window(bkv, wins))
                ldv = bld[own, :]
                half = HEAD // 2
                lt2 = jnp.concatenate([ldv[:, 0:1], ldv[:, HEAD:HEAD + 1]], axis=0)
                dsum2 = jnp.concatenate([ldv[:, half:half + 1], ldv[:, HEAD + half:HEAD + half + 1]], axis=0)
                p = jnp.exp(_dot_nt(q2, kw) + tab[mask] - lt2)
                ds = (p * (_dot_nt(do2, vw) - dsum2)).astype(BF16)
                rq[own, :] = _unstack_heads(_dot(ds, kw), lane)
                dkw = _dot_tn(ds, q2)
                dvw = _dot_tn(p.astype(BF16), do2)
                n_w = WIN // len(wins)
                for i, w in enumerate(wins):
                    rk[w, :] += dkw[i * n_w:(i + 1) * n_w, :]
                    rv[w, :] += dvw[i * n_w:(i + 1) * n_w, :]
                return carry
            lax.fori_loop(0, n_cls * nbc, block, 0, unroll=ATTN_UNROLL)

        tabs = (tab128, tab4, tab128)
        for p in range(3):
            _wait(_whole_waits(bufs[p], sem_in.at[p]))
            run(ATTN_PLANS[p], bufs[p], tabs[p], res[p])
            prefetch(p)
            _start(_class_scatter(res[p], outs[p], sem_out.at[p], _lanes_of(step)))
        for p in range(3):
            _wait(_whole_waits(res[p], sem_out.at[p]))
        if ns:
            pl.when(step == n_grid - 1)(finish_chips)

    padded = pltpu.VMEM((PAD + S, LANES), F32)
    shapes = [jax.ShapeDtypeStruct(views[0][p].shape, F32) for p in range(3) for _ in range(3)]
    out = pl.pallas_call(
        body, grid=(n_grid,), name="attn_bwd",
        in_specs=[ANY] * (3 * n_in + ns), out_specs=[ANY] * (9 + ns),
        out_shape=shapes + _chips_shapes(chip_sums),
        scratch_shapes=[padded] * 18
        + [pltpu.VMEM((4, 256, WIN), F32), pltpu.VMEM((4, 128, WIN), F32),
           pltpu.SemaphoreType.DMA((3, n_in)), pltpu.SemaphoreType.DMA((3, 3))]
        + (_chips_scratch(ns) if ns else []),
        compiler_params=_cparams(56))(*flat, *chip_sums)
    return [o.reshape(S, AW) for o in out[:9]] + list(out[9:])


def _in_proj_bwd(dqkv, dbcu, dqx, cos, sins, w16, x, g, dx1):
    tq = TQ // 2

    def body(*refs):
        parts = refs[0:9]
        dbcu_ref, dqx_ref, c_ref, s_ref, w_ref, x_ref, g_ref, dx1_ref, dp_ref, gx_ref, dg_ref = refs[9:]

        @pl.when(pl.program_id(0) == 0)
        def _():
            dg_ref[...] = jnp.zeros_like(dg_ref)

        dq, dk, dv = (parts[i][...] + parts[3 + i][...] + parts[6 + i][...] for i in range(3))
        cos, sn = _all_heads(c_ref[...]), _all_heads(s_ref[...])
        dqr = dq * SCALE
        dkr = dk
        dp = jnp.concatenate([(dqr * cos + _rot_half(dqr * sn)).astype(BF16),
                              (dkr * cos + _rot_half(dkr * sn)).astype(BF16), dv.astype(BF16),
                              dbcu_ref[...], dqx_ref[...]], axis=1)
        dp_ref[...] = dp
        dh = _dot_nt(dp, w_ref[...])
        g = g_ref[...]
        _, n, r = _rms(x_ref[...], g)
        dx, dg = _rms_bwd(dh, n, r, g)
        gx_ref[...] = dx1_ref[...] + dx
        dg_ref[0:1, :] += dg

    def tile(w):
        return pl.BlockSpec((tq, w), lambda i: (i, 0))

    return pl.pallas_call(
        body, grid=(S // tq,), name="in_proj_bwd",
        in_specs=[tile(AW)] * 9 + [tile(3 * CW), tile(XW), tile(LANES), tile(LANES), _const((D, PW)),
                                   tile(D), _const((1, D)), tile(D)],
        out_specs=[tile(PW), tile(D), _acc((SUBLANES, D))],
        out_shape=[jax.ShapeDtypeStruct((S, PW), BF16), jax.ShapeDtypeStruct((S, D), F32),
                   jax.ShapeDtypeStruct((SUBLANES, D), F32)],
        compiler_params=_cparams(56))(*dqkv, dbcu, dqx, cos, sins, w16, x, g, dx1)


def _mem_bwd(mem, g_mem, wkv16, dkv):
    def body(m_ref, g_ref, w_ref, dkv_ref, dkv16_ref, dg_ref):
        dkv16 = dkv_ref[...].astype(BF16)
        dkv16_ref[...] = dkv16
        _, n, _ = _rms(m_ref[...], g_ref[...])
        dg = jnp.sum(_dot_nt(dkv16, w_ref[...]) * n, axis=0, keepdims=True)
        dg_ref[...] = jnp.broadcast_to(dg, dg_ref.shape)

    return pl.pallas_call(
        body, name="mem_bwd",
        out_shape=[jax.ShapeDtypeStruct((N_MEM, 2 * XW), BF16), jax.ShapeDtypeStruct((SUBLANES, D), F32)],
        compiler_params=pltpu.CompilerParams(vmem_limit_bytes=32 << 20))(mem, g_mem, wkv16, dkv)


N_CHIPS = N_DEV // 2


def _transpose_into(at, a_ref):
    kk = a_ref.shape[0]
    chunk = min(kk, 512)
    for c in range(kk // chunk):
        at[:, c * chunk:(c + 1) * chunk] = a_ref[c * chunk:(c + 1) * chunk, :].T


def _pair_scratch(block):
    return [pltpu.VMEM((N_CHIPS,) + block, BF16), pltpu.VMEM((N_CHIPS,) + block, BF16),
            pltpu.SemaphoreType.DMA((N_CHIPS,)), pltpu.SemaphoreType.DMA((N_CHIPS,))]


def _swap_with_sibling(p, stage, land, send, recv):
    x, y, c = lax.axis_index("x"), lax.axis_index("y"), lax.axis_index("c")
    return pltpu.make_async_remote_copy(src_ref=stage.at[p], dst_ref=land.at[p], send_sem=send.at[p],
                                        recv_sem=recv.at[p], device_id=(x, y, 1 - c), device_id_type=MESH)


def _wgrad_cols(place, a16, b16, blk, name, square_b=False, transpose_out=False, to_chips=False):
    kk, m = a16.shape
    aligned = blk % LANES == 0
    wide = blk if aligned else -(-(blk + LANES // 2) // LANES) * LANES
    block = (blk, m) if transpose_out else (m, blk)

    def chip_of(step, my_chip):
        return (my_chip + 1 + step) & (N_CHIPS - 1) if to_chips else step

    def body(pl_ref, a_ref, *refs):
        b_refs, refs = refs[:2 if aligned else 1], refs[2 if aligned else 1:]
        (cs_ref, own_ref), refs = refs[:2], refs[2:]
        if to_chips:
            landed, refs = refs[0], refs[1:]
        (at, stage, land, send, recv), refs = refs[:5], refs[5:]
        if not aligned:
            (win, wsem), refs = refs[:2], refs[2:]
        step = pl.program_id(0)
        x, y, c = lax.axis_index("x"), lax.axis_index("y"), lax.axis_index("c")
        my_chip = 2 * x + y
        p = chip_of(step, my_chip)

        def fetch(at_step, mine):
            j = 2 * chip_of(at_step, my_chip) + (c if mine else 1 - c)
            first = pl.multiple_of(((j * blk) >> 7) << 7, LANES)
            slot = 2 * (at_step & 1) + mine
            return pltpu.make_async_copy(b_refs[0].at[:, pl.ds(first, wide)], win.at[slot], wsem.at[slot])

        @pl.when(step == 0)
        def _():
            if not aligned:
                fetch(0, 0).start()
                fetch(0, 1).start()
            _transpose_into(at, a_ref)

        if not aligned:
            @pl.when(step + 1 < N_CHIPS)
            def _():
                fetch(step + 1, 0).start()
                fetch(step + 1, 1).start()

        def partial(mine):
            if aligned:
                b = b_refs[mine][...]
                if square_b:
                    b = b * b
                acc = _dot(at[...], b)
            else:
                fetch(step, mine).wait()
                acc = _dot(at[...], win[2 * (step & 1) + mine])
                odd = c if mine else 1 - c
                acc = pltpu.roll(acc, jnp.where(odd == 0, 0, wide - LANES // 2), 1)[:, 0:blk]
            return acc.T if transpose_out else acc

        stage[p] = partial(0).astype(BF16)
        swap = _swap_with_sibling(p, stage, land, send, recv)
        swap.start()
        mine = partial(1)
        swap.wait()
        total = mine + land[p].astype(F32)
        cs_ref[0] = total.astype(BF16)

        @pl.when(p == my_chip)
        def _():
            own_ref[...] = total

        if to_chips:
            stage2, send2, recv2 = refs
            flipped = jnp.bitwise_xor(p, my_chip)
            k = jnp.where(flipped == 2, 0, jnp.where(flipped == 1, 1, 2))

            def to_owner(src, k_, px, py):
                return pltpu.make_async_remote_copy(src_ref=src, dst_ref=landed.at[k_], send_sem=send2.at[k_],
                                                    recv_sem=recv2.at[k_], device_id=(px, py, c), device_id_type=MESH)

            @pl.when(p != my_chip)
            def _():
                stage2[p] = total.astype(BF16)
                to_owner(stage2.at[p], k, p >> 1, p & 1).start()

            @pl.when(step == N_CHIPS - 1)
            def _():
                for k_ in range(N_CHIPS - 1):
                    to_owner(stage2.at[0], k_, x, y).wait()

    def b_spec(mine):
        return pl.BlockSpec((kk, blk), lambda i, s: (0, 2 * chip_of(i, s[1]) + (s[0] if mine else 1 - s[0])))

    b_specs, b_args = ([b_spec(0), b_spec(1)], (b16, b16)) if aligned else ([ANY], (b16,))
    scratch = [pltpu.VMEM((m, kk), BF16)] + _pair_scratch(block)
    if not aligned:
        scratch += [pltpu.VMEM((4, kk, wide), BF16), pltpu.SemaphoreType.DMA((4,))]
    out_specs = [pl.BlockSpec((1,) + block, lambda i, s: (chip_of(i, s[1]), 0, 0)), pl.BlockSpec(block, lambda i, s: (0, 0))]
    out_shape = [jax.ShapeDtypeStruct((N_CHIPS,) + block, BF16), jax.ShapeDtypeStruct(block, F32)]
    if to_chips:
        out_specs.append(ANY)
        out_shape.append(jax.ShapeDtypeStruct((N_CHIPS - 1,) + block, BF16))
        scratch += [pltpu.VMEM((N_CHIPS,) + block, BF16), pltpu.SemaphoreType.DMA((N_CHIPS - 1,)),
                    pltpu.SemaphoreType.DMA((N_CHIPS - 1,))]
    return pl.pallas_call(
        body, name=name,
        grid_spec=pltpu.PrefetchScalarGridSpec(
            num_scalar_prefetch=1, grid=(N_CHIPS,),
            in_specs=[pl.BlockSpec((kk, m), lambda i, s: (0, 0), pipeline_mode=pl.Buffered(1))] + b_specs,
            out_specs=out_specs, scratch_shapes=scratch),
        out_shape=out_shape, compiler_params=_cparams(56))(place, a16, *b_args)


def _wgrad_rows(place, a16, b16, name):
    kk, m = a16.shape
    n = b16.shape[1]
    block = (m // N_DEV, n)

    def body(pl_ref, a_ref, b_ref, cs_ref, own_ref, at, acc, stage, land, send, recv):
        c = pl_ref[0]
        _transpose_into(at, a_ref)
        acc[...] = _dot(at[...], b_ref[...])

        def rows(owner):
            return pl.ds(pl.multiple_of(owner * block[0], block[0]), block[0])

        swaps = []
        for p in range(N_CHIPS):
            stage[p] = acc[rows(2 * p + 1 - c), :].astype(BF16)
            swaps.append(_swap_with_sibling(p, stage, land, send, recv))
            swaps[-1].start()
        for p in range(N_CHIPS):
            swaps[p].wait()
            total = acc[rows(2 * p + c), :] + land[p].astype(F32)
            cs_ref[p] = total.astype(BF16)

            @pl.when(p == pl_ref[1])
            def _():
                own_ref[...] = total

    vmem = pl.BlockSpec(memory_space=pltpu.VMEM)
    return pl.pallas_call(
        body, name=name,
        in_specs=[pl.BlockSpec(memory_space=pltpu.SMEM), vmem, vmem], out_specs=[vmem, vmem],
        out_shape=[jax.ShapeDtypeStruct((N_CHIPS,) + block, BF16), jax.ShapeDtypeStruct(block, F32)],
        scratch_shapes=[pltpu.VMEM((m, kk), BF16), pltpu.VMEM((m, n), F32)] + _pair_scratch(block),
        compiler_params=pltpu.CompilerParams(vmem_limit_bytes=56 << 20))(place, a16, b16)


def _adamw_math(w, g, m, v):
    m = ADAM_B1 * m + (1.0 - ADAM_B1) * g
    v = ADAM_B2 * v + (1.0 - ADAM_B2) * jnp.square(g)
    m_hat = m / (1.0 - ADAM_B1 ** ADAM_STEP)
    v_hat = v / (1.0 - ADAM_B2 ** ADAM_STEP)
    delta = -ADAM_LR * (m_hat / (jnp.sqrt(v_hat) + ADAM_EPS) + ADAM_WD * w)
    return delta, m, v


def _adamw_shards(updates, name, chip_sums=()):
    names, nu, ns = list(updates), len(updates), len(chip_sums)

    def body(*refs):
        ins, sum_refs = refs[:5 * nu], refs[5 * nu:5 * nu + ns]
        outs = refs[5 * nu + ns:9 * nu + ns]
        landed_refs, scratch = refs[9 * nu + ns:9 * nu + 2 * ns], refs[9 * nu + 2 * ns:]
        if ns:
            start_chips, finish_chips = _chips_steps(sum_refs, landed_refs, *scratch)
            start_chips()
        for i in range(nu):
            o_ref, r_ref, w_ref, m_ref, v_ref = ins[5 * i:5 * i + 5]
            g_out, d_out, m_out, v_out = outs[4 * i:4 * i + 4]
            g = o_ref[...] + r_ref[0].astype(F32) + r_ref[1].astype(F32) + r_ref[2].astype(F32)
            g_out[...] = g
            d_out[...], m_out[...], v_out[...] = _adamw_math(w_ref[...], g, m_ref[...], v_ref[...])
        if ns:
            finish_chips()

    vmem = pl.BlockSpec(memory_space=pltpu.VMEM)
    out = pl.pallas_call(
        body, name=name,
        in_specs=[vmem] * (5 * nu) + [ANY] * ns, out_specs=[vmem] * (4 * nu) + [ANY] * ns,
        out_shape=[jax.ShapeDtypeStruct(updates[n][2].shape, F32) for n in names for _ in range(4)]
        + _chips_shapes(chip_sums),
        scratch_shapes=_chips_scratch(ns) if ns else [],
        compiler_params=pltpu.CompilerParams(vmem_limit_bytes=56 << 20),
    )(*[a for n in names for a in updates[n]], *chip_sums)
    return {n: out[4 * i:4 * i + 4] for i, n in enumerate(names)}, list(out[4 * nu:])


def _place():
    x, y, c = lax.axis_index("x"), lax.axis_index("y"), lax.axis_index("c")
    chips = [(1 - x, y), (x, 1 - y), (1 - x, 1 - y)]
    return x, y, c, chips


def _gather_steps(ins, outs, send, recv, lsem):
    nt = len(ins)
    x, y, c, (xn, yn, diag) = _place()
    me, sib = (x, y, c), (x, y, 1 - c)

    def slot(t, px, py, pc):
        return outs[t].at[4 * px + 2 * py + pc]

    def copy(t, k, block, to, src=None):
        return pltpu.make_async_remote_copy(
            src_ref=slot(t, *block) if src is None else src, dst_ref=slot(t, *block),
            send_sem=send.at[t, k], recv_sem=recv.at[t, k], device_id=to, device_id_type=MESH)

    mine = [pltpu.make_async_copy(ins[t], slot(t, *me), lsem.at[t]) for t in range(nt)]
    first = [copy(t, k, me, to, src=ins[t]) for t in range(nt) for k, to in ((0, sib), (1, (*xn, c)), (2, (*yn, c)))]

    def start():
        for cp in mine + first:
            cp.start()

    def landed(k, chip, also_to=None):
        for t in range(nt):
            copy(t, k, (*chip, c), me).wait_recv()
            if also_to is not None:
                copy(t, 3, (*chip, c), (*also_to, c)).start()
            copy(t, 3 + k, (*chip, c), sib).start()

    def relay():
        pl.when(c == 0)(lambda: landed(1, xn, also_to=yn))
        pl.when(c == 1)(lambda: landed(2, yn, also_to=xn))

    def finish():
        pl.when(c == 0)(lambda: landed(2, yn))
        pl.when(c == 1)(lambda: landed(1, xn))
        landed(3, diag)
        for t in range(nt):
            copy(t, 0, sib, me).wait_recv()
            for k, chip in ((4, xn), (5, yn), (6, diag)):
                copy(t, k, (*chip, 1 - c), me).wait_recv()
            for k in range(7):
                copy(t, k, me, sib).wait_send()
        for cp in mine:
            cp.wait()

    return start, relay, finish


def _gather_scratch(nt):
    return [pltpu.SemaphoreType.DMA((nt, 7)), pltpu.SemaphoreType.DMA((nt, 7)), pltpu.SemaphoreType.DMA((nt,))]


def _gathered_shapes(shards):
    return [jax.ShapeDtypeStruct((N_DEV,) + s.shape, s.dtype) for s in shards]


def _call_with_gather(body, n_grid, shards, *, name, in_specs, out_specs, out_shape, scratch_shapes, vmem_mb, args):
    ng, n_in, n_out = len(shards), len(in_specs), len(out_specs)

    def wrapped(*refs):
        ins, shard_refs = refs[:n_in], refs[n_in:n_in + ng]
        outs = refs[n_in + ng:n_in + ng + n_out]
        whole_refs = refs[n_in + ng + n_out:n_in + 2 * ng + n_out]
        scratch = refs[n_in + 2 * ng + n_out:]
        if ng:
            start, relay, finish = _gather_steps(shard_refs, whole_refs, *scratch[len(scratch_shapes):])
            pl.when(pl.program_id(0) == 0)(start)
            pl.when(pl.program_id(0) == n_grid // 2)(relay)
        body(*ins, *outs, *scratch[:len(scratch_shapes)])
        if ng:
            pl.when(pl.program_id(0) == n_grid - 1)(finish)

    return pl.pallas_call(
        wrapped, grid=(n_grid,), name=name,
        in_specs=list(in_specs) + [ANY] * ng, out_specs=list(out_specs) + [ANY] * ng,
        out_shape=list(out_shape) + _gathered_shapes(shards),
        scratch_shapes=list(scratch_shapes) + (_gather_scratch(ng) if ng else []),
        compiler_params=_cparams(vmem_mb))(*args, *shards)


def _chips_steps(ins, outs, send, recv):
    _, _, c, chips = _place()
    copies = [pltpu.make_async_remote_copy(
        src_ref=ins[t].at[2 * px + py], dst_ref=outs[t].at[j], send_sem=send.at[t, j], recv_sem=recv.at[t, j],
        device_id=(px, py, c), device_id_type=MESH) for t in range(len(ins)) for j, (px, py) in enumerate(chips)]

    def start():
        for cp in copies:
            cp.start()

    def finish():
        for cp in copies:
            cp.wait()

    return start, finish


def _chips_scratch(nt):
    return [pltpu.SemaphoreType.DMA((nt, 3)), pltpu.SemaphoreType.DMA((nt, 3))]


def _chips_shapes(cs16s):
    return [jax.ShapeDtypeStruct((3,) + g.shape[1:], g.dtype) for g in cs16s]


SMALL = (("g_pre_mix", 0, 0, D), ("g_mem", 1, 0, D), ("g_post_mix", 2, 0, D), ("g_attn_out", 3, 0, AW),
         ("g_conv_out", 3, AW, CW), ("g_xattn_out", 3, AW + CW, XW), ("g_post_mlp", 4, 0, D), ("g_pre_mlp", 5, 0, D))
CONV_ROW = 8
PACK_ROWS = 16


LOSS_ROW = 15


def _small_all_reduce(dg_in, dg_mem, dgs, dg_mlp, dcw, loss8):
    def body(acc_in, acc_mem, acc_mix, acc_mlp, acc_cw, acc_loss, tot_ref, pack, land, send, recv):
        x, y, c, _ = _place()
        me = 4 * x + 2 * y + c
        pack[...] = jnp.zeros_like(pack)
        pack[0:1, :] = acc_in[0:1, :]
        pack[1:2, :] = acc_mem[0:1, :]
        pack[2:4, :] = acc_mix[0:2, :]
        pack[4:6, :] = acc_mlp[0:2, :]
        pack[CONV_ROW:CONV_ROW + 3, 0:CW] = acc_cw[0:3, :]
        pack[LOSS_ROW:LOSS_ROW + 1, 0:LANES] = acc_loss[0:1, :]
        land[me] = pack[...]
        copies = []
        for k in range(1, N_DEV):
            kx, ky, kc = (k >> 2) & 1, (k >> 1) & 1, k & 1
            peer = (1 - x if kx else x, 1 - y if ky else y, 1 - c if kc else c)
            copies.append(pltpu.make_async_remote_copy(
                src_ref=pack, dst_ref=land.at[me], send_sem=send.at[k - 1], recv_sem=recv.at[k - 1],
                device_id=peer, device_id_type=MESH))
        for cp in copies:
            cp.start()
        for cp in copies:
            cp.wait()
        tot = land[0]
        for s in range(1, N_DEV):
            tot = tot + land[s]
        tot_ref[...] = tot

    return pl.pallas_call(
        body, name="small_all_reduce", out_shape=jax.ShapeDtypeStruct((PACK_ROWS, D), F32),
        scratch_shapes=[pltpu.VMEM((PACK_ROWS, D), F32), pltpu.VMEM((N_DEV, PACK_ROWS, D), F32),
                        pltpu.SemaphoreType.DMA((N_DEV - 1,)), pltpu.SemaphoreType.DMA((N_DEV - 1,))],
    )(dg_in, dg_mem, dgs, dg_mlp, dcw, loss8)


def _small_update(tot, me, params):
    flat = [a for n, _, _, _ in SMALL for a in params[n]] + list(params["conv_w"])
    n_par = len(SMALL) + 1
    tap_cols = CW // N_DEV

    def body(*refs):
        me_ref, tot_ref = refs[0:2]
        ins = refs[2:2 + 3 * n_par]
        loss_out = refs[2 + 3 * n_par]
        outs = refs[3 + 3 * n_par:]
        tot = tot_ref[...]
        loss_out[...] = jnp.broadcast_to(tot[LOSS_ROW:LOSS_ROW + 1, 0:LANES], loss_out.shape)

        def update(i, g):
            w_ref, m_ref, v_ref = ins[3 * i:3 * i + 3]
            g_out, d_out, m_out, v_out = outs[4 * i:4 * i + 4]
            g_out[...] = g
            d_out[...], m_out[...], v_out[...] = _adamw_math(w_ref[...], g, m_ref[...], v_ref[...])

        for i, (_, row, lane0, width) in enumerate(SMALL):
            update(i, tot[row:row + 1, lane0:lane0 + width])
        me = me_ref[0]
        taps = pltpu.roll(tot[CONV_ROW:CONV_ROW + SUBLANES, 0:CW], jnp.where(me == 0, 0, CW - me * tap_cols), 1)
        update(n_par - 1, taps[0:3, 0:tap_cols])

    shapes = [jax.ShapeDtypeStruct(params[n][0].shape, F32) for n, _, _, _ in SMALL] + [
        jax.ShapeDtypeStruct(params["conv_w"][0].shape, F32)]
    vmem = pl.BlockSpec(memory_space=pltpu.VMEM)
    loss, *out = pl.pallas_call(
        body, name="small_update",
        in_specs=[pl.BlockSpec(memory_space=pltpu.SMEM)] + [vmem] * (1 + 3 * n_par),
        out_shape=[jax.ShapeDtypeStruct((SUBLANES, LANES), F32)] + [s for s in shapes for _ in range(4)],
    )(me, tot, *flat)
    names = [n for n, _, _, _ in SMALL] + ["conv_w"]
    return loss[0, 0], {n: out[4 * i:4 * i + 4] for i, n in enumerate(names)}


def _local_step(x, mem, pos, gains, shards, tgt, place):
    half = HEAD // 2
    inv_freq = jnp.float32(ROPE_THETA) ** (-(jnp.arange(half, dtype=F32) * 2.0 / HEAD))
    invf = jnp.tile(inv_freq, LANES // half)[None, :]
    sgn = jnp.tile(jnp.concatenate([-jnp.ones((half,), F32), jnp.ones((half,), F32)]), LANES // HEAD)[None, :]
    cos, sins, win8 = _rope_table(pos.astype(F32).reshape(S, 1), invf, sgn, [shards["w_in"]])
    q, kvp, bcu, qx16, h16, win16, wout8, wkv8, conv8 = _in_proj(
        x, gains["g_pre_mix"], win8, cos, sins, [shards["w_out"], shards["w_mem_kv"], shards["conv_w"]])
    wout16, wkv16 = wout8.reshape(D, D), wkv8.reshape(D, 2 * XW)
    cw_full = conv8[:, 0:3, 0:CW // N_DEV].transpose(1, 0, 2).reshape(3, CW)
    cw8 = jnp.zeros((SUBLANES, CW), F32).at[0:3].set(cw_full)
    y_attn, ltot, wup8, wdn8 = _attn_fwd(q, kvp, [shards["w_up"], shards["w_down"]])
    wdn16 = wdn8.reshape(FF, D)
    memn16, kv16 = _mem_fwd(mem, gains["g_mem"], wkv16)
    ypre, y16, y2, x1 = _mix_out(y_attn, bcu, qx16, kv16, cw8, gains["g_attn_out"], gains["g_conv_out"],
                                 gains["g_xattn_out"], gains["g_post_mix"], wout16, x, [])
    a16, du16, h2_16, df2_16, dx1, loss8, dg_mlp = _mlp(x1, tgt, gains["g_pre_mlp"], gains["g_post_mlp"], wup8, wdn16)

    sums = {"w_up": _wgrad_cols(place, h2_16, du16, FF_BLK, "wgrad_up"),
            "w_down": _wgrad_cols(place, df2_16, a16, FF_BLK, "wgrad_down", square_b=True, transpose_out=True)}

    head_id = jnp.arange(AW, dtype=jnp.int32) // HEAD
    head_ones = (head_id[:, None] == head_id[None, :]).astype(BF16)
    dy2_16, qdo, ld, dbcu, dqx, dgs, dcw, dkv = _mix_out_bwd(
        dx1, y2, ypre, ltot, head_ones, q, bcu, qx16, kv16, cw8, gains["g_post_mix"], gains["g_attn_out"],
        gains["g_conv_out"], gains["g_xattn_out"], wout16)
    dkv16, dg_mem = _mem_bwd(mem, gains["g_mem"], wkv16, dkv)
    sums["w_mem_kv"] = _wgrad_rows(place, memn16, dkv16, "wgrad_mem_kv")
    sums["w_out"] = _wgrad_rows(place, y16, dy2_16, "wgrad_out")
    out = _attn_bwd(qdo, kvp, ld, [s[0] for s in sums.values()])
    dqkv, landed = out[:9], out[9:]
    reduced = {n: (s[1], landed[t]) for t, (n, s) in enumerate(sums.items())}
    dproj16, grad_x, dg_in = _in_proj_bwd(dqkv, dbcu, dqx, cos, sins, win16, x, gains["g_pre_mix"], dx1)

    _, in_own, in_landed = _wgrad_cols(place, h16, dproj16, PW // N_DEV, "wgrad_in", to_chips=True)
    reduced["w_in"] = (in_own, in_landed)
    return grad_x, reduced, (dg_in, dg_mem, dgs, dg_mlp, dcw, loss8)


BIG = ("w_in", "w_mem_kv", "w_out", "w_up", "w_down")
ORDER = ("g_pre_mix", "g_mem", "w_in", "w_mem_kv", "conv_w", "g_attn_out", "g_conv_out", "g_xattn_out", "w_out",
         "g_post_mix", "g_pre_mlp", "w_up", "w_down", "g_post_mlp")


def kernel(x, mem, positions, g_pre_mix, g_mem, w_in, w_mem_kv, conv_w, g_attn_out, g_conv_out, g_xattn_out, w_out, g_post_mix, g_pre_mlp, w_up, w_down, g_post_mlp, loss_target, m_g_pre_mix, m_g_mem, m_w_in, m_w_mem_kv, m_conv_w, m_g_attn_out, m_g_conv_out, m_g_xattn_out, m_w_out, m_g_post_mix, m_g_pre_mlp, m_w_up, m_w_down, m_g_post_mlp, v_g_pre_mix, v_g_mem, v_w_in, v_w_mem_kv, v_conv_w, v_g_attn_out, v_g_conv_out, v_g_xattn_out, v_w_out, v_g_post_mix, v_g_pre_mlp, v_w_up, v_w_down, v_g_post_mlp):
    w = dict(g_pre_mix=g_pre_mix, g_mem=g_mem, w_in=w_in, w_mem_kv=w_mem_kv, conv_w=conv_w, g_attn_out=g_attn_out,
             g_conv_out=g_conv_out, g_xattn_out=g_xattn_out, w_out=w_out, g_post_mix=g_post_mix, g_pre_mlp=g_pre_mlp,
             w_up=w_up, w_down=w_down, g_post_mlp=g_post_mlp)
    mo = dict(g_pre_mix=m_g_pre_mix, g_mem=m_g_mem, w_in=m_w_in, w_mem_kv=m_w_mem_kv, conv_w=m_conv_w,
              g_attn_out=m_g_attn_out, g_conv_out=m_g_conv_out, g_xattn_out=m_g_xattn_out, w_out=m_w_out,
              g_post_mix=m_g_post_mix, g_pre_mlp=m_g_pre_mlp, w_up=m_w_up, w_down=m_w_down, g_post_mlp=m_g_post_mlp)
    vo = dict(g_pre_mix=v_g_pre_mix, g_mem=v_g_mem, w_in=v_w_in, w_mem_kv=v_w_mem_kv, conv_w=v_conv_w,
              g_attn_out=v_g_attn_out, g_conv_out=v_g_conv_out, g_xattn_out=v_g_xattn_out, w_out=v_w_out,
              g_post_mix=v_g_post_mix, g_pre_mlp=v_g_pre_mlp, w_up=v_w_up, w_down=v_w_down, g_post_mlp=v_g_post_mlp)

    xi, yi, ci = lax.axis_index("x"), lax.axis_index("y"), lax.axis_index("c")
    me = 4 * xi + 2 * yi + ci
    place = jnp.stack([ci, 2 * xi + yi]).astype(jnp.int32)

    shards = {n: w[n][0].astype(BF16) for n in BIG}
    shards["conv_w"] = jnp.zeros((SUBLANES, LANES), F32).at[0:3, 0:CW // N_DEV].set(conv_w[0])

    gains = {n: w[n] for n, _, _, _ in SMALL}
    grad_x, reduced, small_acc = _local_step(x[0], mem[0], positions[0], gains, shards, loss_target[0], place)

    updated = {}
    for group in (("w_up", "w_down"), ("w_in", "w_out", "w_mem_kv")):
        updated.update(_adamw_shards({n: (*reduced[n], w[n][0], mo[n][0], vo[n][0]) for n in group},
                                     "adamw_" + "_".join(group))[0])
    grad, delta, new_m, new_v = {}, {}, {}, {}
    for n, (g, d_, m_, v_) in updated.items():
        grad[n], delta[n], new_m[n], new_v[n] = g[None], d_[None], m_[None], v_[None]

    params = {n: (w[n], mo[n], vo[n]) for n, _, _, _ in SMALL}
    params["conv_w"] = (w["conv_w"][0], mo["conv_w"][0], vo["conv_w"][0])
    loss, small = _small_update(_small_all_reduce(*small_acc), me.reshape(1).astype(jnp.int32), params)
    for n, (g, d_, m_, v_) in small.items():
        lead = (lambda a: a[None]) if n == "conv_w" else (lambda a: a)
        grad[n], delta[n], new_m[n], new_v[n] = lead(g), lead(d_), lead(m_), lead(v_)

    return (loss, grad_x[None], *[grad[n] for n in ORDER], *[delta[n] for n in ORDER],
            *[new_m[n] for n in ORDER], *[new_v[n] for n in ORDER])
```

```python
import functools

import numpy as np
import jax
import jax.numpy as jnp
from jax import lax
from jax.experimental import pallas as pl
from jax.experimental.pallas import tpu as pltpu

F32, BF16 = jnp.float32, jnp.bfloat16
MESH = pl.DeviceIdType.MESH
ANY = pl.BlockSpec(memory_space=pl.ANY)

N_DEV = 8
D = 1024
S = 4096
N_MEM = 256
HEAD = 64
AW, CW, XW = 512, 256, 256
PW = 3 * AW + 3 * CW + XW
FF = 4096
FF_BLK = FF // N_DEV
PATTERNS = ((128, 1), (512, 4), (2048, 16))
QB = 128
EPS = 1e-6
NEG = -1e30
SCALE = HEAD ** -0.5
ROPE_THETA = 10000.0
LANES = 128
SUBLANES = 8

ADAM_LR, ADAM_B1, ADAM_B2, ADAM_EPS, ADAM_WD, ADAM_STEP = 0.001, 0.9, 0.999, 1e-08, 0.01, 10

TQ = 512
TQ_MLP = 256
NT = S // TQ


def _cparams(vmem_mb, n_grid=1):
    return pltpu.CompilerParams(dimension_semantics=("arbitrary",) * n_grid, vmem_limit_bytes=vmem_mb << 20)


def _const(shape):
    nd = len(shape)
    return pl.BlockSpec(shape, lambda *_: (0,) * nd, pipeline_mode=pl.Buffered(1))


def _acc(shape):
    nd = len(shape)
    return pl.BlockSpec(shape, lambda *_: (0,) * nd)


def _dot(a, b):
    return jnp.dot(a, b, preferred_element_type=F32)


def _dot_nt(a, b):
    return lax.dot_general(a, b, (((1,), (1,)), ((), ())), preferred_element_type=F32)


def _dot_tn(a, b):
    return lax.dot_general(a, b, (((0,), (0,)), ((), ())), preferred_element_type=F32)


def _rms(x, g):
    r = lax.rsqrt(jnp.mean(x * x, axis=-1, keepdims=True) + EPS)
    n = x * r
    return n * g, n, r


def _rms_bwd(dy, n, r, g):
    dn = dy * g
    dx = r * (dn - n * jnp.mean(dn * n, axis=-1, keepdims=True))
    return dx, jnp.sum(dy * n, axis=0, keepdims=True)


def _rot_half(t):
    lane = lax.broadcasted_iota(jnp.int32, t.shape, 1)
    n = t.shape[1]
    return jnp.where((lane % HEAD) < HEAD // 2, pltpu.roll(t, n - HEAD // 2, 1), pltpu.roll(t, HEAD // 2, 1))


def _rope_table(pos_col, invf, sgn, shards):
    def body(p_ref, f_ref, s_ref, c_out, s_out):
        ang = p_ref[...] * f_ref[...]
        c_out[...] = jnp.cos(ang)
        s_out[...] = jnp.sin(ang) * s_ref[...]

    tile = pl.BlockSpec((TQ, LANES), lambda i: (i, 0))
    return _call_with_gather(
        body, NT, shards, name="rope_table",
        in_specs=[pl.BlockSpec((TQ, 1), lambda i: (i, 0)), _const((1, LANES)), _const((1, LANES))],
        out_specs=[tile, tile], out_shape=[jax.ShapeDtypeStruct((S, LANES), F32)] * 2,
        scratch_shapes=[], vmem_mb=32, args=(pos_col, invf, sgn))


def _all_heads(t):
    return jnp.tile(t, (1, AW // LANES))


def _mem_fwd(mem, g_mem, wkv16):
    def body(m_ref, g_ref, w_ref, n16_ref, kv_ref):
        y, _, _ = _rms(m_ref[...], g_ref[...])
        y16 = y.astype(BF16)
        n16_ref[...] = y16
        kv_ref[...] = _dot(y16, w_ref[...]).astype(BF16)

    return pl.pallas_call(
        body, name="mem_fwd",
        out_shape=[jax.ShapeDtypeStruct((N_MEM, D), BF16), jax.ShapeDtypeStruct((N_MEM, 2 * XW), BF16)],
        compiler_params=pltpu.CompilerParams(vmem_limit_bytes=32 << 20))(mem, g_mem, wkv16)


def _in_proj(x, g, w8, cos, sins, shards):
    blk = PW // N_DEV

    def body(x_ref, g_ref, w8_ref, c_ref, s_ref, q_ref, kv_ref, bcu_ref, qx_ref, h_ref, w_out, w_ref):
        @pl.when(pl.program_id(0) == 0)
        def _():
            for j in range(N_DEV):
                w_ref[:, j * blk:(j + 1) * blk] = w8_ref[j]
            w_out[...] = w_ref[...]

        y, _, _ = _rms(x_ref[...], g_ref[...])
        h = y.astype(BF16)
        h_ref[...] = h
        proj = _dot(h, w_ref[...])
        cos, sn = _all_heads(c_ref[...]), _all_heads(s_ref[...])
        q, k = proj[:, 0:AW], proj[:, AW:2 * AW]
        q_ref[...] = (q * cos + _rot_half(q) * sn) * SCALE
        kv_ref[...] = _pack_pair(k * cos + _rot_half(k) * sn, proj[:, 2 * AW:3 * AW])
        bcu_ref[...] = proj[:, 3 * AW:3 * AW + 3 * CW]
        qx_ref[...] = (proj[:, 3 * AW + 3 * CW:] * SCALE).astype(BF16)

    def tile(w):
        return pl.BlockSpec((TQ, w), lambda i: (i, 0))

    return _call_with_gather(
        body, NT, shards, name="in_proj",
        in_specs=[tile(D), _const((1, D)), _const((N_DEV, D, blk)), tile(LANES), tile(LANES)],
        out_specs=[tile(AW), tile(AW), tile(3 * CW), tile(XW), tile(D), _acc((D, PW))],
        out_shape=[jax.ShapeDtypeStruct((S, AW), F32)] * 2 + [
            jax.ShapeDtypeStruct((S, 3 * CW), F32), jax.ShapeDtypeStruct((S, XW), BF16),
            jax.ShapeDtypeStruct((S, D), BF16), jax.ShapeDtypeStruct((D, PW), BF16)],
        scratch_shapes=[pltpu.VMEM((D, PW), BF16)], vmem_mb=56, args=(x, g, w8, cos, sins))


ATTN_PLANS = (("p1", 1, 128, 32), ("p4", 8, 64, 8), ("p16", 16, 128, 2))
PAD = 128
WIN = 256


ATTN_UNROLL = 8


def _fill_bias(tab, qblk, partner):
    qi = lax.broadcasted_iota(jnp.int32, (2 * qblk, WIN), 0) & (qblk - 1)
    kj = lax.broadcasted_iota(jnp.int32, (2 * qblk, WIN), 1)
    piece = kj >> (qblk.bit_length() - 1)
    kk = kj & (qblk - 1)
    prev = (piece & 1) == 0
    of_partner = piece >= 2
    for first in (0, 1):
        for par in (0, 1):
            lo = jnp.where(prev, (qblk if first else qi) + jnp.where(of_partner, par, 0), 0)
            hi = jnp.where(prev, qblk, qi + jnp.where(of_partner, par - 1, 0))
            tab[2 * first + par] = jnp.where((kk >= lo) & (kk <= hi), 0.0, NEG).astype(F32)


def _block_rows(g, qblk, nbc, partner):
    own = pl.ds(pl.multiple_of(PAD + g * qblk, qblk), qblk)
    first = ((g & (nbc - 1)) == 0).astype(jnp.int32)
    if partner:
        gp = jnp.bitwise_xor(g, 4 * nbc)
        wins = (pl.ds(pl.multiple_of(PAD + (g - 1) * qblk, qblk), 2 * qblk),
                pl.ds(pl.multiple_of(PAD + (gp - 1) * qblk, qblk), 2 * qblk))
        return own, wins, 2 * first + ((g >> ((4 * nbc).bit_length() - 1)) & 1)
    return own, (pl.ds(pl.multiple_of(PAD + (g - 1) * qblk, qblk), 2 * qblk),), 2 * first


def _pack_pair(lo, hi):
    lo_bits = lax.bitcast_convert_type(lo.astype(BF16).astype(F32), jnp.uint32) >> 16
    hi_bits = lax.bitcast_convert_type(hi.astype(BF16).astype(F32), jnp.uint32) & jnp.uint32(0xFFFF0000)
    return lax.bitcast_convert_type(hi_bits | lo_bits, F32)


def _unpack_pair(c):
    bits = lax.bitcast_convert_type(c, jnp.uint32)
    lo = lax.bitcast_convert_type(bits << 16, F32).astype(BF16)
    hi = lax.bitcast_convert_type(bits & jnp.uint32(0xFFFF0000), F32).astype(BF16)
    return lo, hi


def _window(ref, wins):
    parts = [ref[w, :] for w in wins]
    return parts[0] if len(parts) == 1 else jnp.concatenate(parts, axis=0)


def _stack_heads(t, lane):
    zero = jnp.zeros_like(t)
    return jnp.concatenate([jnp.where(lane < HEAD, t, zero), jnp.where(lane >= HEAD, t, zero)], axis=0)


def _unstack_heads(t2, lane):
    half = t2.shape[0] // 2
    return jnp.where(lane < HEAD, t2[0:half, :], t2[half:, :])


def _lanes_of(step):
    return pl.ds(pl.multiple_of(step * LANES, LANES), LANES)


def _whole_wait(buf, sem):
    whole = buf.at[pl.ds(PAD, S), :]
    return pltpu.make_async_copy(whole, whole, sem)


def _whole_waits(bufs, sems):
    return [_whole_wait(buf, sems.at[i]) for i, buf in enumerate(bufs)]


def _class_gather(views, bufs, sems, lanes):
    copies = []
    for i, (view, buf) in enumerate(zip(views, bufs)):
        if view.ndim == 2:
            copies.append(pltpu.make_async_copy(view.at[:, lanes], buf.at[pl.ds(PAD, S), :], sems.at[i]))
        else:
            per, n_cls = view.shape[0], view.shape[1]
            copies += [pltpu.make_async_copy(view.at[:, c, lanes], buf.at[pl.ds(PAD + c * per, per), :], sems.at[i])
                       for c in range(n_cls)]
    return copies


def _class_scatter(bufs, dsts, sems, lanes=None):
    copies = []
    for i, (buf, dst) in enumerate(zip(bufs, dsts)):
        if dst.ndim == 2:
            copies.append(pltpu.make_async_copy(buf.at[pl.ds(PAD, S), :], dst.at[:, lanes], sems.at[i]))
            continue
        per, n_cls = dst.shape[0], dst.shape[1]
        for c in range(n_cls):
            to = dst.at[:, c, :] if lanes is None else dst.at[:, c, lanes]
            copies.append(pltpu.make_async_copy(buf.at[pl.ds(PAD + c * per, per), :], to, sems.at[i]))
    return copies


def _start(copies):
    for cp in copies:
        cp.start()


def _wait(waits):
    for w in waits:
        w.wait()


def _attn_fwd(q, kvp, shards=()):
    views = [[a] + [a.reshape(S // n, n, AW) for _, n, _, _ in ATTN_PLANS[1:]] for a in (q, kvp)]
    flat = [views[a][p] for p in range(3) for a in range(2)]
    ng = len(shards)
    n_grid = AW // LANES

    def body(*refs):
        hbm = [refs[2 * p:2 * p + 2] for p in range(3)]
        refs = refs[6:]
        shard_refs, refs = refs[:ng], refs[ng:]
        y_ref, lt_ref = refs[0:2]
        whole_refs, refs = refs[2:2 + ng], refs[2 + ng:]
        bufs = [refs[2 * p:2 * p + 2] for p in range(3)]
        oc4, lc4, oc16, lc16, tab128, tab4, sem_in = refs[6:13]
        step = pl.program_id(0)
        if ng:
            start_gather, relay_gather, finish_gather = _gather_steps(shard_refs, whole_refs, *refs[13:])
            pl.when(step == 0)(start_gather)
            pl.when(step == n_grid // 2)(relay_gather)
        now = [_class_gather(hbm[p], bufs[p], sem_in.at[p], _lanes_of(step)) for p in range(3)]
        nxt = [_class_gather(hbm[p], bufs[p], sem_in.at[p], _lanes_of(step + 1)) for p in range(3)]

        @pl.when(step == 0)
        def _():
            for p in range(3):
                _start(now[p])
                for b in bufs[p]:
                    b[0:PAD, :] = jnp.zeros((PAD, LANES), F32)
            _fill_bias(tab128, 128, False)
            _fill_bias(tab4, 64, True)

        def prefetch(p):
            pl.when(step + 1 < n_grid)(lambda: _start(nxt[p]))

        lane = lax.broadcasted_iota(jnp.int32, (1, LANES), 1)
        ones = jnp.ones((WIN, LANES), BF16)

        def run(plan, bq, bkv, tab, o_dst, l_dst, dst_pad):
            _, n_cls, qblk, nbc = plan
            partner = n_cls == 8

            def block(g, carry):
                own, wins, mask = _block_rows(g, qblk, nbc, partner)
                q2 = _stack_heads(bq[own, :].astype(BF16), lane)
                kw, vwin = _unpack_pair(_window(bkv, wins))
                vw = jnp.concatenate([vwin, ones], axis=1)
                s = _dot_nt(q2, kw) + tab[mask]
                m = jnp.max(s, axis=1, keepdims=True)
                oe = _dot(jnp.exp(s - m).astype(BF16), vw)
                den = oe[:, LANES:]
                dst = pl.ds(pl.multiple_of(dst_pad + g * qblk, qblk), qblk)
                o_dst[dst, :] = _unstack_heads(oe[:, 0:LANES] / den, lane)
                l_dst[dst, :] = _unstack_heads(m + jnp.log(den), lane)
                return carry
            lax.fori_loop(0, n_cls * nbc, block, 0, unroll=ATTN_UNROLL)

        _wait(_whole_waits(bufs[0], sem_in.at[0]))
        run(ATTN_PLANS[0], *bufs[0], tab128, y_ref, lt_ref, 0)
        prefetch(0)
        _wait(_whole_waits(bufs[1], sem_in.at[1]))
        run(ATTN_PLANS[1], *bufs[1], tab4, oc4, lc4, PAD)
        prefetch(1)
        _wait(_whole_waits(bufs[2], sem_in.at[2]))
        run(ATTN_PLANS[2], *bufs[2], tab128, oc16, lc16, PAD)
        prefetch(2)

        n_rows = 64

        def token_order(buf, t, n_cls):
            per = S // n_cls
            first = PAD + t * (n_rows // n_cls)
            return jnp.concatenate([buf[pl.ds(first + jj, n_cls, stride=per), :] for jj in range(n_rows // n_cls)],
                                   axis=0)

        def combine(t, carry):
            rows = pl.ds(pl.multiple_of(t * n_rows, n_rows), n_rows)
            l0, l1, l2 = lt_ref[rows, :], token_order(lc4, t, 8), token_order(lc16, t, 16)
            lm = jnp.maximum(jnp.maximum(l0, l1), l2)
            e0, e1, e2 = jnp.exp(l0 - lm), jnp.exp(l1 - lm), jnp.exp(l2 - lm)
            den = e0 + e1 + e2
            y_ref[rows, :] = (e0 * y_ref[rows, :] + e1 * token_order(oc4, t, 8)
                              + e2 * token_order(oc16, t, 16)) / den
            lt_ref[rows, :] = lm + jnp.log(den)
            return carry
        lax.fori_loop(0, S // n_rows, combine, 0, unroll=2)

        if ng:
            pl.when(step == n_grid - 1)(finish_gather)

    col = pl.BlockSpec((S, LANES), lambda h: (0, h))
    padded = pltpu.VMEM((PAD + S, LANES), F32)
    return pl.pallas_call(
        body, grid=(n_grid,), name="attn_fwd",
        in_specs=[ANY] * (6 + ng), out_specs=[col, col] + [ANY] * ng,
        out_shape=[jax.ShapeDtypeStruct((S, AW), F32)] * 2 + _gathered_shapes(shards),
        scratch_shapes=[padded] * 10 + [
            pltpu.VMEM((4, 256, WIN), F32), pltpu.VMEM((4, 128, WIN), F32), pltpu.SemaphoreType.DMA((3, 2))]
        + (_gather_scratch(ng) if ng else []),
        compiler_params=_cparams(56))(*flat, *shards)


def _conv_taps(z, zprev, row):
    z1 = jnp.where(row == 0, zprev[7:8, :], pltpu.roll(z, 1, 0))
    z2 = jnp.where(row == 0, zprev[6:7, :], jnp.where(row == 1, zprev[7:8, :], pltpu.roll(z, 2, 0)))
    return z1, z2


def _xattn_scores(qm, km):
    s = _dot_nt(qm, km)
    m = jnp.max(s, axis=1, keepdims=True)
    e = jnp.exp(s - m)
    return e, jnp.sum(e, axis=1, keepdims=True)


def _mix_out(y_attn, bcu, qx16, kv16, cw8, g_attn, g_conv, g_x, g_post, wout16, x, shards):
    def body(ya_ref, bcu_ref, halo_ref, qx_ref, kv_ref, cw_ref, ga_ref, gc_ref, gx_ref, gp_ref, w_ref, x_ref,
             ypre_ref, y16_ref, y2_ref, x1_ref):
        i = pl.program_id(0)
        bcu = bcu_ref[...]
        b, c, u = bcu[:, 0:CW], bcu[:, CW:2 * CW], bcu[:, 2 * CW:]
        z = c * u
        halo = halo_ref[...]
        zprev = jnp.where(i > 0, halo[:, CW:2 * CW] * halo[:, 2 * CW:], 0.0)
        row = lax.broadcasted_iota(jnp.int32, z.shape, 0)
        z1, z2 = _conv_taps(z, zprev, row)
        cw = cw_ref[...]
        y_conv = b * (z2 * cw[0:1, :] + z1 * cw[1:2, :] + z * cw[2:3, :])

        qx = qx_ref[...]
        kv = kv_ref[...]
        km, vm = kv[:, 0:XW], kv[:, XW:]
        lane = lax.broadcasted_iota(jnp.int32, qx.shape, 1)
        y_x = jnp.zeros(qx.shape, F32)
        for h in range(XW // HEAD):
            hm = (lane >= h * HEAD) & (lane < (h + 1) * HEAD)
            e, l = _xattn_scores(jnp.where(hm, qx, jnp.zeros_like(qx)), km)
            y_x = jnp.where(hm, _dot(e.astype(BF16), vm) / l, y_x)

        y_attn = ya_ref[...]
        ypre_ref[:, 0:AW] = y_attn
        ypre_ref[:, AW:AW + CW] = y_conv
        ypre_ref[:, AW + CW:] = y_x
        y = jnp.concatenate([_rms(y_attn, ga_ref[...])[0], _rms(y_conv, gc_ref[...])[0],
                             _rms(y_x, gx_ref[...])[0]], axis=1).astype(BF16)
        y16_ref[...] = y
        y2 = _dot(y, w_ref[...])
        y2_ref[...] = y2
        x1_ref[...] = x_ref[...] + _rms(y2, gp_ref[...])[0]

    def tile(w):
        return pl.BlockSpec((TQ, w), lambda i: (i, 0))

    halo = pl.BlockSpec((SUBLANES, 3 * CW), lambda i: (jnp.maximum(i * (TQ // SUBLANES) - 1, 0), 0))
    return _call_with_gather(
        body, NT, shards, name="mix_out",
        in_specs=[tile(AW), tile(3 * CW), halo, tile(XW), _const((N_MEM, 2 * XW)), _const((SUBLANES, CW)),
                  _const((1, AW)), _const((1, CW)), _const((1, XW)), _const((1, D)), _const((D, D)), tile(D)],
        out_specs=[tile(D), tile(D), tile(D), tile(D)],
        out_shape=[jax.ShapeDtypeStruct((S, D), F32), jax.ShapeDtypeStruct((S, D), BF16),
                   jax.ShapeDtypeStruct((S, D), F32), jax.ShapeDtypeStruct((S, D), F32)],
        scratch_shapes=[], vmem_mb=56,
        args=(y_attn, bcu, bcu, qx16, kv16, cw8, g_attn, g_conv, g_x, g_post, wout16, x))


def _mlp(x1, tgt, g_pre, g_post, wup8, wdn_halves):
    tq = TQ_MLP
    half = D // 2

    def body(x1_ref, t_ref, g1_ref, g2_ref, wu_ref, wda_ref, wdb_ref,
             a16_ref, du_ref, h2_ref, df2_ref, dx1_ref, loss_ref, dg_ref, a32):
        @pl.when(pl.program_id(0) == 0)
        def _():
            loss_ref[...] = jnp.zeros_like(loss_ref)
            dg_ref[...] = jnp.zeros_like(dg_ref)

        x1 = x1_ref[...]
        g1, g2 = g1_ref[...], g2_ref[...]
        y1, n1, r1 = _rms(x1, g1)
        h2 = y1.astype(BF16)
        h2_ref[...] = h2
        f2a = jnp.zeros((tq, half), F32)
        f2b = jnp.zeros((tq, half), F32)
        for j in range(N_DEV):
            cols = slice(j * FF_BLK, (j + 1) * FF_BLK)
            a = jnp.maximum(_dot(h2, wu_ref[j]), 0.0)
            a32[:, cols] = a
            a16_ref[:, cols] = a.astype(BF16)
            f = (a * a).astype(BF16)
            f2a = f2a + _dot(f, wda_ref[cols, :])
            f2b = f2b + _dot(f, wdb_ref[cols, :])
        f2 = jnp.concatenate([f2a, f2b], axis=1)
        y2, n2, r2 = _rms(f2, g2)
        e = x1 + y2 - t_ref[...]
        sq = jnp.sum(jnp.sum(e * e, axis=1, keepdims=True), axis=0, keepdims=True)
        loss_ref[...] += jnp.broadcast_to(sq * (0.5 / D), loss_ref.shape)
        dout = e * (1.0 / D)
        df2, dg2 = _rms_bwd(dout, n2, r2, g2)
        df2_16 = df2.astype(BF16)
        df2_ref[...] = df2_16
        dh2 = jnp.zeros((tq, D), F32)
        for j in range(N_DEV):
            cols = slice(j * FF_BLK, (j + 1) * FF_BLK)
            df = _dot_nt(df2_16[:, 0:half], wda_ref[cols, :]) + _dot_nt(df2_16[:, half:], wdb_ref[cols, :])
            du = (df * (2.0 * a32[:, cols])).astype(BF16)
            du_ref[:, cols] = du
            dh2 = dh2 + _dot_nt(du, wu_ref[j])
        dx, dg1 = _rms_bwd(dh2, n1, r1, g1)
        dx1_ref[...] = dout + dx
        dg_ref[0:1, :] += dg2
        dg_ref[1:2, :] += dg1

    def tile(w):
        return pl.BlockSpec((tq, w), lambda i: (i, 0))

    return pl.pallas_call(
        body, grid=(S // tq,), name="mlp",
        in_specs=[tile(D), tile(D), _const((1, D)), _const((1, D)), _const((N_DEV, D, FF_BLK)), _const((FF, half)), _const((FF, half))],
        out_specs=[tile(FF), tile(FF), tile(D), tile(D), tile(D), _acc((SUBLANES, LANES)), _acc((SUBLANES, D))],
        out_shape=[jax.ShapeDtypeStruct((S, FF), BF16), jax.ShapeDtypeStruct((S, FF), BF16),
                   jax.ShapeDtypeStruct((S, D), BF16), jax.ShapeDtypeStruct((S, D), BF16),
                   jax.ShapeDtypeStruct((S, D), F32), jax.ShapeDtypeStruct((SUBLANES, LANES), F32),
                   jax.ShapeDtypeStruct((SUBLANES, D), F32)],
        scratch_shapes=[pltpu.VMEM((tq, FF), F32)],
        compiler_params=_cparams(56))(x1, tgt, g_pre, g_post, wup8, *wdn_halves)


def _mix_out_bwd(dx1, y2, ypre, ltot, head_ones, q, bcu, qx16, kv16, cw8, g_post, g_attn, g_conv, g_x, wout16):
    def body(dx1_ref, y2_ref, ypre_ref, lt_ref, e_ref, q_ref, bcu_ref, halo_ref, qx_ref, kv_ref, cw_ref, gp_ref,
             ga_ref, gc_ref, gx_ref, w_ref, dy2_ref, qdo_ref, ld_ref, dbcu_ref, dqx_ref, dgs_ref, dcw_ref, dkv_ref,
             carry):
        i = pl.program_id(0)

        @pl.when(i == 0)
        def _():
            dgs_ref[...] = jnp.zeros_like(dgs_ref)
            dcw_ref[...] = jnp.zeros_like(dcw_ref)
            dkv_ref[...] = jnp.zeros_like(dkv_ref)
            carry[...] = jnp.zeros_like(carry)

        gp = gp_ref[...]
        _, n, r = _rms(y2_ref[...], gp)
        dy2, dgp = _rms_bwd(dx1_ref[...], n, r, gp)
        dy2_16 = dy2.astype(BF16)
        dy2_ref[...] = dy2_16
        dy = _dot_nt(dy2_16, w_ref[...])

        ypre = ypre_ref[...]
        ga, gc, gx = ga_ref[...], gc_ref[...], gx_ref[...]
        _, na, ra = _rms(ypre[:, 0:AW], ga)
        dya, dga = _rms_bwd(dy[:, 0:AW], na, ra, ga)
        _, nc, rc = _rms(ypre[:, AW:AW + CW], gc)
        dyc, dgc = _rms_bwd(dy[:, AW:AW + CW], nc, rc, gc)
        y_x = ypre[:, AW + CW:]
        _, nx, rx = _rms(y_x, gx)
        dyx, dgx = _rms_bwd(dy[:, AW + CW:], nx, rx, gx)
        qdo_ref[...] = _pack_pair(q_ref[...], dya)
        prod = dya * ypre[:, 0:AW]
        hi = prod.astype(BF16)
        lo = (prod - hi.astype(F32)).astype(BF16)
        head_sum = _dot(hi, e_ref[...]) + _dot(lo, e_ref[...])
        lane_a = lax.broadcasted_iota(jnp.int32, prod.shape, 1)
        ld_ref[...] = jnp.where((lane_a % HEAD) < HEAD // 2, lt_ref[...], head_sum)
        dgs_ref[0:1, :] += dgp
        dgs_ref[1:2, :] += jnp.concatenate([dga, dgc, dgx], axis=1)

        bcu = bcu_ref[...]
        b, c, u = bcu[:, 0:CW], bcu[:, CW:2 * CW], bcu[:, 2 * CW:]
        z = c * u
        halo = halo_ref[...]
        zprev = jnp.where(i < NT - 1, halo[:, CW:2 * CW] * halo[:, 2 * CW:], 0.0)
        row = lax.broadcasted_iota(jnp.int32, z.shape, 0)
        z1, z2 = _conv_taps(z, zprev, row)
        cw = cw_ref[...]
        conv = z2 * cw[0:1, :] + z1 * cw[1:2, :] + z * cw[2:3, :]
        dconv = dyc * b
        nxt = carry[...]
        dn1 = jnp.where(row == TQ - 1, nxt[0:1, :], pltpu.roll(dconv, TQ - 1, 0))
        dn2 = jnp.where(row == TQ - 1, nxt[1:2, :], jnp.where(row == TQ - 2, nxt[0:1, :], pltpu.roll(dconv, TQ - 2, 0)))
        carry[...] = dconv[0:SUBLANES, :]
        dz = dconv * cw[2:3, :] + dn1 * cw[1:2, :] + dn2 * cw[0:1, :]
        dbcu_ref[:, 0:CW] = (dyc * conv).astype(BF16)
        dbcu_ref[:, CW:2 * CW] = (dz * u).astype(BF16)
        dbcu_ref[:, 2 * CW:] = (dz * c).astype(BF16)
        dcw_ref[0:1, :] += jnp.sum(z2 * dconv, axis=0, keepdims=True)
        dcw_ref[1:2, :] += jnp.sum(z1 * dconv, axis=0, keepdims=True)
        dcw_ref[2:3, :] += jnp.sum(z * dconv, axis=0, keepdims=True)

        qx = qx_ref[...]
        kv = kv_ref[...]
        km, vm = kv[:, 0:XW], kv[:, XW:]
        lane = lax.broadcasted_iota(jnp.int32, qx.shape, 1)
        dqx = jnp.zeros(qx.shape, F32)
        dkm = jnp.zeros((N_MEM, XW), F32)
        dvm = jnp.zeros((N_MEM, XW), F32)
        for h in range(XW // HEAD):
            hm = (lane >= h * HEAD) & (lane < (h + 1) * HEAD)
            qm = jnp.where(hm, qx, jnp.zeros_like(qx))
            e, l = _xattn_scores(qm, km)
            p = e / l
            dom = jnp.where(hm, dyx, 0.0)
            do16 = dom.astype(BF16)
            dsum = jnp.sum(dom * y_x, axis=1, keepdims=True)
            ds = (p * (_dot_nt(do16, vm) - dsum)).astype(BF16)
            dqx = jnp.where(hm, _dot(ds, km), dqx)
            dkm = dkm + _dot_tn(ds, qm)
            dvm = dvm + _dot_tn(p.astype(BF16), do16)
        dqx_ref[...] = (dqx * SCALE).astype(BF16)
        dkv_ref[:, 0:XW] += dkm
        dkv_ref[:, XW:] += dvm

    def tile(w):
        return pl.BlockSpec((TQ, w), lambda i: (NT - 1 - i, 0))

    halo = pl.BlockSpec((SUBLANES, 3 * CW), lambda i: (jnp.maximum((NT - 1 - i) * (TQ // SUBLANES) - 1, 0), 0))
    return pl.pallas_call(
        body, grid=(NT,), name="mix_out_bwd",
        in_specs=[tile(D), tile(D), tile(D), tile(AW), _const((AW, AW)), tile(AW), tile(3 * CW), halo, tile(XW),
                  _const((N_MEM, 2 * XW)), _const((SUBLANES, CW)), _const((1, D)), _const((1, AW)), _const((1, CW)),
                  _const((1, XW)), _const((D, D))],
        out_specs=[tile(D), tile(AW), tile(AW), tile(3 * CW), tile(XW), _acc((SUBLANES, D)), _acc((SUBLANES, CW)),
                   _acc((N_MEM, 2 * XW))],
        out_shape=[jax.ShapeDtypeStruct((S, D), BF16), jax.ShapeDtypeStruct((S, AW), F32),
                   jax.ShapeDtypeStruct((S, AW), F32),
                   jax.ShapeDtypeStruct((S, 3 * CW), BF16), jax.ShapeDtypeStruct((S, XW), BF16),
                   jax.ShapeDtypeStruct((SUBLANES, D), F32), jax.ShapeDtypeStruct((SUBLANES, CW), F32),
                   jax.ShapeDtypeStruct((N_MEM, 2 * XW), F32)],
        scratch_shapes=[pltpu.VMEM((SUBLANES, CW), F32)],
        compiler_params=_cparams(56))(dx1, y2, ypre, ltot, head_ones, q, bcu, bcu, qx16, kv16, cw8, g_post, g_attn,
                                      g_conv, g_x, wout16)


def _attn_bwd(qdo, kvp, ld, chip_sums=()):
    n_in = 3
    views = [[a] + [a.reshape(S // n, n, AW) for _, n, _, _ in ATTN_PLANS[1:]] for a in (qdo, kvp, ld)]
    flat = [views[a][p] for p in range(3) for a in range(n_in)]
    ns = len(chip_sums)
    n_grid = AW // LANES

    def body(*refs):
        hbm = [refs[n_in * p:n_in * p + n_in] for p in range(3)]
        refs = refs[3 * n_in:]
        sum_refs, refs = refs[:ns], refs[ns:]
        outs = [refs[3 * p:3 * p + 3] for p in range(3)]
        landed_refs, sc = refs[9:9 + ns], refs[9 + ns:]
        bufs = [sc[3 * p:3 * p + 3] for p in range(3)]
        res = [sc[9 + 3 * p:12 + 3 * p] for p in range(3)]
        tab128, tab4, sem_in, sem_out = sc[18:22]
        step = pl.program_id(0)
        if ns:
            start_chips, finish_chips = _chips_steps(sum_refs, landed_refs, *sc[22:])
            pl.when(step == 0)(start_chips)
        now = [_class_gather(hbm[p], bufs[p], sem_in.at[p], _lanes_of(step)) for p in range(3)]
        nxt = [_class_gather(hbm[p], bufs[p], sem_in.at[p], _lanes_of(step + 1)) for p in range(3)]

        @pl.when(step == 0)
        def _():
            for p in range(3):
                _start(now[p])
                for b in bufs[p]:
                    b[0:PAD, :] = jnp.zeros((PAD, LANES), F32)
            _fill_bias(tab128, 128, False)
            _fill_bias(tab4, 64, True)

        def prefetch(p):
            pl.when(step + 1 < n_grid)(lambda: _start(nxt[p]))

        for p in range(3):
            for b in res[p]:
                b[...] = jnp.zeros_like(b)
        lane = lax.broadcasted_iota(jnp.int32, (1, LANES), 1)

        def run(plan, plan_bufs, tab, dst):
            _, n_cls, qblk, nbc = plan
            partner = n_cls == 8
            bqdo, bkv, bld = plan_bufs
            rq, rk, rv = dst

            def block(g, carry):
                own, wins, mask = _block_rows(g, qblk, nbc, partner)
                qb, dob = _unpack_pair(bqdo[own, :])
                q2, do2 = _stack_heads(qb, lane), _stack_heads(dob, lane)
                kw, vw = _unpack_pair(_window(bkv, wins))
                ldv = bld[own, :]
                half = HEAD // 2
                lt2 = jnp.concatenate([ldv[:, 0:1], ldv[:, HEAD:HEAD + 1]], axis=0)
                dsum2 = jnp.concatenate([ldv[:, half:half + 1], ldv[:, HEAD + half:HEAD + half + 1]], axis=0)
                p = jnp.exp(_dot_nt(q2, kw) + tab[mask] - lt2)
                ds = (p * (_dot_nt(do2, vw) - dsum2)).astype(BF16)
                rq[own, :] = _unstack_heads(_dot(ds, kw), lane)
                dkw = _dot_tn(ds, q2)
                dvw = _dot_tn(p.astype(BF16), do2)
                n_w = WIN // len(wins)
                for i, w in enumerate(wins):
                    rk[w, :] += dkw[i * n_w:(i + 1) * n_w, :]
                    rv[w, :] += dvw[i * n_w:(i + 1) * n_w, :]
                return carry
            lax.fori_loop(0, n_cls * nbc, block, 0, unroll=ATTN_UNROLL)

        tabs = (tab128, tab4, tab128)
        for p in range(3):
            _wait(_whole_waits(bufs[p], sem_in.at[p]))
            run(ATTN_PLANS[p], bufs[p], tabs[p], res[p])
            prefetch(p)
            _start(_class_scatter(res[p], outs[p], sem_out.at[p], _lanes_of(step)))
        for p in range(3):
            _wait(_whole_waits(res[p], sem_out.at[p]))
        if ns:
            pl.when(step == n_grid - 1)(finish_chips)

    padded = pltpu.VMEM((PAD + S, LANES), F32)
    shapes = [jax.ShapeDtypeStruct(views[0][p].shape, F32) for p in range(3) for _ in range(3)]
    out = pl.pallas_call(
        body, grid=(n_grid,), name="attn_bwd",
        in_specs=[ANY] * (3 * n_in + ns), out_specs=[ANY] * (9 + ns),
        out_shape=shapes + _chips_shapes(chip_sums),
        scratch_shapes=[padded] * 18
        + [pltpu.VMEM((4, 256, WIN), F32), pltpu.VMEM((4, 128, WIN), F32),
           pltpu.SemaphoreType.DMA((3, n_in)), pltpu.SemaphoreType.DMA((3, 3))]
        + (_chips_scratch(ns) if ns else []),
        compiler_params=_cparams(56))(*flat, *chip_sums)
    return [o.reshape(S, AW) for o in out[:9]] + list(out[9:])


def _in_proj_bwd(dqkv, dbcu, dqx, cos, sins, w16, x, g, dx1):
    tq = TQ // 2

    def body(*refs):
        parts = refs[0:9]
        dbcu_ref, dqx_ref, c_ref, s_ref, w_ref, x_ref, g_ref, dx1_ref, dp_ref, gx_ref, dg_ref = refs[9:]

        @pl.when(pl.program_id(0) == 0)
        def _():
            dg_ref[...] = jnp.zeros_like(dg_ref)

        dq, dk, dv = (parts[i][...] + parts[3 + i][...] + parts[6 + i][...] for i in range(3))
        cos, sn = _all_heads(c_ref[...]), _all_heads(s_ref[...])
        dqr = dq * SCALE
        dkr = dk
        dp = jnp.concatenate([(dqr * cos + _rot_half(dqr * sn)).astype(BF16),
                              (dkr * cos + _rot_half(dkr * sn)).astype(BF16), dv.astype(BF16),
                              dbcu_ref[...], dqx_ref[...]], axis=1)
        dp_ref[...] = dp
        dh = _dot_nt(dp, w_ref[...])
        g = g_ref[...]
        _, n, r = _rms(x_ref[...], g)
        dx, dg = _rms_bwd(dh, n, r, g)
        gx_ref[...] = dx1_ref[...] + dx
        dg_ref[0:1, :] += dg

    def tile(w):
        return pl.BlockSpec((tq, w), lambda i: (i, 0))

    return pl.pallas_call(
        body, grid=(S // tq,), name="in_proj_bwd",
        in_specs=[tile(AW)] * 9 + [tile(3 * CW), tile(XW), tile(LANES), tile(LANES), _const((D, PW)),
                                   tile(D), _const((1, D)), tile(D)],
        out_specs=[tile(PW), tile(D), _acc((SUBLANES, D))],
        out_shape=[jax.ShapeDtypeStruct((S, PW), BF16), jax.ShapeDtypeStruct((S, D), F32),
                   jax.ShapeDtypeStruct((SUBLANES, D), F32)],
        compiler_params=_cparams(56))(*dqkv, dbcu, dqx, cos, sins, w16, x, g, dx1)


def _mem_bwd(mem, g_mem, wkv16, dkv):
    def body(m_ref, g_ref, w_ref, dkv_ref, dkv16_ref, dg_ref):
        dkv16 = dkv_ref[...].astype(BF16)
        dkv16_ref[...] = dkv16
        _, n, _ = _rms(m_ref[...], g_ref[...])
        dg = jnp.sum(_dot_nt(dkv16, w_ref[...]) * n, axis=0, keepdims=True)
        dg_ref[...] = jnp.broadcast_to(dg, dg_ref.shape)

    return pl.pallas_call(
        body, name="mem_bwd",
        out_shape=[jax.ShapeDtypeStruct((N_MEM, 2 * XW), BF16), jax.ShapeDtypeStruct((SUBLANES, D), F32)],
        compiler_params=pltpu.CompilerParams(vmem_limit_bytes=32 << 20))(mem, g_mem, wkv16, dkv)


N_CHIPS = N_DEV // 2


def _transpose_into(at, a_ref):
    kk = a_ref.shape[0]
    chunk = min(kk, 512)
    for c in range(kk // chunk):
        at[:, c * chunk:(c + 1) * chunk] = a_ref[c * chunk:(c + 1) * chunk, :].T


def _pair_scratch(block):
    return [pltpu.VMEM((N_CHIPS,) + block, BF16), pltpu.VMEM((N_CHIPS,) + block, BF16),
            pltpu.SemaphoreType.DMA((N_CHIPS,)), pltpu.SemaphoreType.DMA((N_CHIPS,))]


def _swap_with_sibling(p, stage, land, send, recv):
    x, y, c = lax.axis_index("x"), lax.axis_index("y"), lax.axis_index("c")
    return pltpu.make_async_remote_copy(src_ref=stage.at[p], dst_ref=land.at[p], send_sem=send.at[p],
                                        recv_sem=recv.at[p], device_id=(x, y, 1 - c), device_id_type=MESH)


def _wgrad_cols(place, a16, b16, blk, name, square_b=False, transpose_out=False, to_chips=False, small=()):
    kk, m = a16.shape
    aligned = blk % LANES == 0
    wide = blk if aligned else -(-(blk + LANES // 2) // LANES) * LANES
    block = (blk, m) if transpose_out else (m, blk)

    def chip_of(step, my_chip):
        return (my_chip + 1 + step) & (N_CHIPS - 1) if to_chips else step

    def body(pl_ref, a_ref, *refs):
        b_refs, refs = refs[:2 if aligned else 1], refs[2 if aligned else 1:]
        accs, refs = refs[:len(small)], refs[len(small):]
        (cs_ref, own_ref), refs = refs[:2], refs[2:]
        if to_chips:
            landed, refs = refs[0], refs[1:]
        if small:
            tot_ref, refs = refs[0], refs[1:]
        (at, stage, land, send, recv), refs = refs[:5], refs[5:]
        if not aligned:
            (win, wsem), refs = refs[:2], refs[2:]
        if small:
            start_small, finish_small = _small_reduce_steps(accs, tot_ref, *refs[-4:])
            refs = refs[:-4]
        step = pl.program_id(0)
        if small:
            pl.when(step == 0)(start_small)
        x, y, c = lax.axis_index("x"), lax.axis_index("y"), lax.axis_index("c")
        my_chip = 2 * x + y
        p = chip_of(step, my_chip)

        def fetch(at_step, mine):
            j = 2 * chip_of(at_step, my_chip) + (c if mine else 1 - c)
            first = pl.multiple_of(((j * blk) >> 7) << 7, LANES)
            slot = 2 * (at_step & 1) + mine
            return pltpu.make_async_copy(b_refs[0].at[:, pl.ds(first, wide)], win.at[slot], wsem.at[slot])

        @pl.when(step == 0)
        def _():
            if not aligned:
                fetch(0, 0).start()
                fetch(0, 1).start()
            _transpose_into(at, a_ref)

        if not aligned:
            @pl.when(step + 1 < N_CHIPS)
            def _():
                fetch(step + 1, 0).start()
                fetch(step + 1, 1).start()

        def partial(mine):
            if aligned:
                b = b_refs[mine][...]
                if square_b:
                    b = b * b
                acc = _dot(at[...], b)
            else:
                fetch(step, mine).wait()
                acc = _dot(at[...], win[2 * (step & 1) + mine])
                odd = c if mine else 1 - c
                acc = pltpu.roll(acc, jnp.where(odd == 0, 0, wide - LANES // 2), 1)[:, 0:blk]
            return acc.T if transpose_out else acc

        stage[p] = partial(0).astype(BF16)
        swap = _swap_with_sibling(p, stage, land, send, recv)
        swap.start()
        mine = partial(1)
        swap.wait()
        total = mine + land[p].astype(F32)
        cs_ref[0] = total.astype(BF16)

        @pl.when(p == my_chip)
        def _():
            own_ref[...] = total

        if to_chips:
            stage2, send2, recv2 = refs
            flipped = jnp.bitwise_xor(p, my_chip)
            k = jnp.where(flipped == 2, 0, jnp.where(flipped == 1, 1, 2))

            def to_owner(src, k_, px, py):
                return pltpu.make_async_remote_copy(src_ref=src, dst_ref=landed.at[k_], send_sem=send2.at[k_],
                                                    recv_sem=recv2.at[k_], device_id=(px, py, c), device_id_type=MESH)

            @pl.when(p != my_chip)
            def _():
                stage2[p] = total.astype(BF16)
                to_owner(stage2.at[p], k, p >> 1, p & 1).start()

            @pl.when(step == N_CHIPS - 1)
            def _():
                for k_ in range(N_CHIPS - 1):
                    to_owner(stage2.at[0], k_, x, y).wait()

        if small:
            pl.when(step == N_CHIPS - 1)(finish_small)

    def b_spec(mine):
        return pl.BlockSpec((kk, blk), lambda i, s: (0, 2 * chip_of(i, s[1]) + (s[0] if mine else 1 - s[0])))

    b_specs, b_args = ([b_spec(0), b_spec(1)], (b16, b16)) if aligned else ([ANY], (b16,))
    scratch = [pltpu.VMEM((m, kk), BF16)] + _pair_scratch(block)
    if not aligned:
        scratch += [pltpu.VMEM((4, kk, wide), BF16), pltpu.SemaphoreType.DMA((4,))]
    out_specs = [pl.BlockSpec((1,) + block, lambda i, s: (chip_of(i, s[1]), 0, 0)), pl.BlockSpec(block, lambda i, s: (0, 0))]
    out_shape = [jax.ShapeDtypeStruct((N_CHIPS,) + block, BF16), jax.ShapeDtypeStruct(block, F32)]
    if to_chips:
        out_specs.append(ANY)
        out_shape.append(jax.ShapeDtypeStruct((N_CHIPS - 1,) + block, BF16))
        scratch += [pltpu.VMEM((N_CHIPS,) + block, BF16), pltpu.SemaphoreType.DMA((N_CHIPS - 1,)),
                    pltpu.SemaphoreType.DMA((N_CHIPS - 1,))]
    small_specs = [pl.BlockSpec(a.shape, lambda i, s: (0, 0)) for a in small]
    if small:
        out_specs.append(pl.BlockSpec((PACK_ROWS, D), lambda i, s: (0, 0)))
        out_shape.append(jax.ShapeDtypeStruct((PACK_ROWS, D), F32))
        scratch += _small_reduce_scratch()
    return pl.pallas_call(
        body, name=name,
        grid_spec=pltpu.PrefetchScalarGridSpec(
            num_scalar_prefetch=1, grid=(N_CHIPS,),
            in_specs=[pl.BlockSpec((kk, m), lambda i, s: (0, 0), pipeline_mode=pl.Buffered(1))] + b_specs + small_specs,
            out_specs=out_specs, scratch_shapes=scratch),
        out_shape=out_shape, compiler_params=_cparams(56))(place, a16, *b_args, *small)


def _wgrad_rows(place, a16, b16, name):
    kk, m = a16.shape
    n = b16.shape[1]
    block = (m // N_DEV, n)

    def body(pl_ref, a_ref, b_ref, cs_ref, own_ref, at, acc, stage, land, send, recv):
        c = pl_ref[0]
        _transpose_into(at, a_ref)
        acc[...] = _dot(at[...], b_ref[...])

        def rows(owner):
            return pl.ds(pl.multiple_of(owner * block[0], block[0]), block[0])

        swaps = []
        for p in range(N_CHIPS):
            stage[p] = acc[rows(2 * p + 1 - c), :].astype(BF16)
            swaps.append(_swap_with_sibling(p, stage, land, send, recv))
            swaps[-1].start()
        for p in range(N_CHIPS):
            swaps[p].wait()
            total = acc[rows(2 * p + c), :] + land[p].astype(F32)
            cs_ref[p] = total.astype(BF16)

            @pl.when(p == pl_ref[1])
            def _():
                own_ref[...] = total

    vmem = pl.BlockSpec(memory_space=pltpu.VMEM)
    return pl.pallas_call(
        body, name=name,
        in_specs=[pl.BlockSpec(memory_space=pltpu.SMEM), vmem, vmem], out_specs=[vmem, vmem],
        out_shape=[jax.ShapeDtypeStruct((N_CHIPS,) + block, BF16), jax.ShapeDtypeStruct(block, F32)],
        scratch_shapes=[pltpu.VMEM((m, kk), BF16), pltpu.VMEM((m, n), F32)] + _pair_scratch(block),
        compiler_params=pltpu.CompilerParams(vmem_limit_bytes=56 << 20))(place, a16, b16)


def _adamw_math(w, g, m, v):
    m = ADAM_B1 * m + (1.0 - ADAM_B1) * g
    v = ADAM_B2 * v + (1.0 - ADAM_B2) * jnp.square(g)
    m_hat = m / (1.0 - ADAM_B1 ** ADAM_STEP)
    v_hat = v / (1.0 - ADAM_B2 ** ADAM_STEP)
    delta = -ADAM_LR * (m_hat / (jnp.sqrt(v_hat) + ADAM_EPS) + ADAM_WD * w)
    return delta, m, v


def _adamw_shards(updates, name, chip_sums=()):
    names, nu, ns = list(updates), len(updates), len(chip_sums)

    def body(*refs):
        ins, sum_refs = refs[:5 * nu], refs[5 * nu:5 * nu + ns]
        outs = refs[5 * nu + ns:9 * nu + ns]
        landed_refs, scratch = refs[9 * nu + ns:9 * nu + 2 * ns], refs[9 * nu + 2 * ns:]
        if ns:
            start_chips, finish_chips = _chips_steps(sum_refs, landed_refs, *scratch)
            start_chips()
        for i in range(nu):
            o_ref, r_ref, w_ref, m_ref, v_ref = ins[5 * i:5 * i + 5]
            g_out, d_out, m_out, v_out = outs[4 * i:4 * i + 4]
            g = o_ref[...] + r_ref[0].astype(F32) + r_ref[1].astype(F32) + r_ref[2].astype(F32)
            g_out[...] = g
            d_out[...], m_out[...], v_out[...] = _adamw_math(w_ref[...], g, m_ref[...], v_ref[...])
        if ns:
            finish_chips()

    vmem = pl.BlockSpec(memory_space=pltpu.VMEM)
    out = pl.pallas_call(
        body, name=name,
        in_specs=[vmem] * (5 * nu) + [ANY] * ns, out_specs=[vmem] * (4 * nu) + [ANY] * ns,
        out_shape=[jax.ShapeDtypeStruct(updates[n][2].shape, F32) for n in names for _ in range(4)]
        + _chips_shapes(chip_sums),
        scratch_shapes=_chips_scratch(ns) if ns else [],
        compiler_params=pltpu.CompilerParams(vmem_limit_bytes=56 << 20),
    )(*[a for n in names for a in updates[n]], *chip_sums)
    return {n: out[4 * i:4 * i + 4] for i, n in enumerate(names)}, list(out[4 * nu:])


def _place():
    x, y, c = lax.axis_index("x"), lax.axis_index("y"), lax.axis_index("c")
    chips = [(1 - x, y), (x, 1 - y), (1 - x, 1 - y)]
    return x, y, c, chips


def _gather_steps(ins, outs, send, recv, lsem):
    nt = len(ins)
    x, y, c, (xn, yn, diag) = _place()
    me, sib = (x, y, c), (x, y, 1 - c)

    def slot(t, px, py, pc):
        return outs[t].at[4 * px + 2 * py + pc]

    def copy(t, k, block, to, src=None):
        return pltpu.make_async_remote_copy(
            src_ref=slot(t, *block) if src is None else src, dst_ref=slot(t, *block),
            send_sem=send.at[t, k], recv_sem=recv.at[t, k], device_id=to, device_id_type=MESH)

    mine = [pltpu.make_async_copy(ins[t], slot(t, *me), lsem.at[t]) for t in range(nt)]
    first = [copy(t, k, me, to, src=ins[t]) for t in range(nt) for k, to in ((0, sib), (1, (*xn, c)), (2, (*yn, c)))]

    def start():
        for cp in mine + first:
            cp.start()

    def landed(k, chip, also_to=None):
        for t in range(nt):
            copy(t, k, (*chip, c), me).wait_recv()
            if also_to is not None:
                copy(t, 3, (*chip, c), (*also_to, c)).start()
            copy(t, 3 + k, (*chip, c), sib).start()

    def relay():
        @pl.when(c == 0)
        def _():
            landed(1, xn, also_to=yn)
            landed(2, yn)

        @pl.when(c == 1)
        def _():
            landed(2, yn, also_to=xn)
            landed(1, xn)

    def finish():
        landed(3, diag)
        for t in range(nt):
            copy(t, 0, sib, me).wait_recv()
            for k, chip in ((4, xn), (5, yn), (6, diag)):
                copy(t, k, (*chip, 1 - c), me).wait_recv()
            for k in range(7):
                copy(t, k, me, sib).wait_send()
        for cp in mine:
            cp.wait()

    return start, relay, finish


def _gather_scratch(nt):
    return [pltpu.SemaphoreType.DMA((nt, 7)), pltpu.SemaphoreType.DMA((nt, 7)), pltpu.SemaphoreType.DMA((nt,))]


def _gathered_shapes(shards):
    return [jax.ShapeDtypeStruct((N_DEV,) + s.shape, s.dtype) for s in shards]


def _call_with_gather(body, n_grid, shards, *, name, in_specs, out_specs, out_shape, scratch_shapes, vmem_mb, args):
    ng, n_in, n_out = len(shards), len(in_specs), len(out_specs)

    def wrapped(*refs):
        ins, shard_refs = refs[:n_in], refs[n_in:n_in + ng]
        outs = refs[n_in + ng:n_in + ng + n_out]
        whole_refs = refs[n_in + ng + n_out:n_in + 2 * ng + n_out]
        scratch = refs[n_in + 2 * ng + n_out:]
        if ng:
            start, relay, finish = _gather_steps(shard_refs, whole_refs, *scratch[len(scratch_shapes):])
            pl.when(pl.program_id(0) == 0)(start)
            pl.when(pl.program_id(0) == n_grid // 2)(relay)
        body(*ins, *outs, *scratch[:len(scratch_shapes)])
        if ng:
            pl.when(pl.program_id(0) == n_grid - 1)(finish)

    return pl.pallas_call(
        wrapped, grid=(n_grid,), name=name,
        in_specs=list(in_specs) + [ANY] * ng, out_specs=list(out_specs) + [ANY] * ng,
        out_shape=list(out_shape) + _gathered_shapes(shards),
        scratch_shapes=list(scratch_shapes) + (_gather_scratch(ng) if ng else []),
        compiler_params=_cparams(vmem_mb))(*args, *shards)


def _chips_steps(ins, outs, send, recv):
    _, _, c, chips = _place()
    copies = [pltpu.make_async_remote_copy(
        src_ref=ins[t].at[2 * px + py], dst_ref=outs[t].at[j], send_sem=send.at[t, j], recv_sem=recv.at[t, j],
        device_id=(px, py, c), device_id_type=MESH) for t in range(len(ins)) for j, (px, py) in enumerate(chips)]

    def start():
        for cp in copies:
            cp.start()

    def finish():
        for cp in copies:
            cp.wait()

    return start, finish


def _chips_scratch(nt):
    return [pltpu.SemaphoreType.DMA((nt, 3)), pltpu.SemaphoreType.DMA((nt, 3))]


def _chips_shapes(cs16s):
    return [jax.ShapeDtypeStruct((3,) + g.shape[1:], g.dtype) for g in cs16s]


SMALL = (("g_pre_mix", 0, 0, D), ("g_mem", 1, 0, D), ("g_post_mix", 2, 0, D), ("g_attn_out", 3, 0, AW),
         ("g_conv_out", 3, AW, CW), ("g_xattn_out", 3, AW + CW, XW), ("g_post_mlp", 4, 0, D), ("g_pre_mlp", 5, 0, D))
CONV_ROW = 8
PACK_ROWS = 16


LOSS_ROW = 15


def _small_reduce_steps(accs, tot_ref, pack, land, send, recv):
    acc_in, acc_mem, acc_mix, acc_mlp, acc_cw, acc_loss = accs
    x, y, c, _ = _place()
    me = 4 * x + 2 * y + c
    copies = []
    for k in range(1, N_DEV):
        kx, ky, kc = (k >> 2) & 1, (k >> 1) & 1, k & 1
        peer = (1 - x if kx else x, 1 - y if ky else y, 1 - c if kc else c)
        copies.append(pltpu.make_async_remote_copy(
            src_ref=pack, dst_ref=land.at[me], send_sem=send.at[k - 1], recv_sem=recv.at[k - 1],
            device_id=peer, device_id_type=MESH))

    def start():
        pack[...] = jnp.zeros_like(pack)
        pack[0:1, :] = acc_in[0:1, :]
        pack[1:2, :] = acc_mem[0:1, :]
        pack[2:4, :] = acc_mix[0:2, :]
        pack[4:6, :] = acc_mlp[0:2, :]
        pack[CONV_ROW:CONV_ROW + 3, 0:CW] = acc_cw[0:3, :]
        pack[LOSS_ROW:LOSS_ROW + 1, 0:LANES] = acc_loss[0:1, :]
        land[me] = pack[...]
        for cp in copies:
            cp.start()

    def finish():
        for cp in copies:
            cp.wait()
        tot = land[0]
        for s in range(1, N_DEV):
            tot = tot + land[s]
        tot_ref[...] = tot

    return start, finish


def _small_reduce_scratch():
    return [pltpu.VMEM((PACK_ROWS, D), F32), pltpu.VMEM((N_DEV, PACK_ROWS, D), F32),
            pltpu.SemaphoreType.DMA((N_DEV - 1,)), pltpu.SemaphoreType.DMA((N_DEV - 1,))]


def _small_update(tot, me, params):
    flat = [a for n, _, _, _ in SMALL for a in params[n]] + list(params["conv_w"])
    n_par = len(SMALL) + 1
    tap_cols = CW // N_DEV

    def body(*refs):
        me_ref, tot_ref = refs[0:2]
        ins = refs[2:2 + 3 * n_par]
        loss_out = refs[2 + 3 * n_par]
        outs = refs[3 + 3 * n_par:]
        tot = tot_ref[...]
        loss_out[...] = jnp.broadcast_to(tot[LOSS_ROW:LOSS_ROW + 1, 0:LANES], loss_out.shape)

        def update(i, g):
            w_ref, m_ref, v_ref = ins[3 * i:3 * i + 3]
            g_out, d_out, m_out, v_out = outs[4 * i:4 * i + 4]
            g_out[...] = g
            d_out[...], m_out[...], v_out[...] = _adamw_math(w_ref[...], g, m_ref[...], v_ref[...])

        for i, (_, row, lane0, width) in enumerate(SMALL):
            update(i, tot[row:row + 1, lane0:lane0 + width])
        me = me_ref[0]
        taps = pltpu.roll(tot[CONV_ROW:CONV_ROW + SUBLANES, 0:CW], jnp.where(me == 0, 0, CW - me * tap_cols), 1)
        update(n_par - 1, taps[0:3, 0:tap_cols])

    shapes = [jax.ShapeDtypeStruct(params[n][0].shape, F32) for n, _, _, _ in SMALL] + [
        jax.ShapeDtypeStruct(params["conv_w"][0].shape, F32)]
    vmem = pl.BlockSpec(memory_space=pltpu.VMEM)
    loss, *out = pl.pallas_call(
        body, name="small_update",
        in_specs=[pl.BlockSpec(memory_space=pltpu.SMEM)] + [vmem] * (1 + 3 * n_par),
        out_shape=[jax.ShapeDtypeStruct((SUBLANES, LANES), F32)] + [s for s in shapes for _ in range(4)],
    )(me, tot, *flat)
    names = [n for n, _, _, _ in SMALL] + ["conv_w"]
    return loss[0, 0], {n: out[4 * i:4 * i + 4] for i, n in enumerate(names)}


def _local_step(x, mem, pos, gains, shards, tgt, place):
    half = HEAD // 2
    inv_freq = jnp.float32(ROPE_THETA) ** (-(jnp.arange(half, dtype=F32) * 2.0 / HEAD))
    invf = jnp.tile(inv_freq, LANES // half)[None, :]
    sgn = jnp.tile(jnp.concatenate([-jnp.ones((half,), F32), jnp.ones((half,), F32)]), LANES // HEAD)[None, :]
    cos, sins, win8 = _rope_table(pos.astype(F32).reshape(S, 1), invf, sgn, [shards["w_in"]])
    wdn_left, wdn_right = shards["w_down"][:, 0:D // 2], shards["w_down"][:, D // 2:]
    q, kvp, bcu, qx16, h16, win16, wout8, wkv8, conv8, wdn8_right = _in_proj(
        x, gains["g_pre_mix"], win8, cos, sins, [shards["w_out"], shards["w_mem_kv"], shards["conv_w"], wdn_right])
    wout16, wkv16 = wout8.reshape(D, D), wkv8.reshape(D, 2 * XW)
    cw_full = conv8[:, 0:3, 0:CW // N_DEV].transpose(1, 0, 2).reshape(3, CW)
    cw8 = jnp.zeros((SUBLANES, CW), F32).at[0:3].set(cw_full)
    y_attn, ltot, wup8, wdn8_left = _attn_fwd(q, kvp, [shards["w_up"], wdn_left])
    wdn_halves = (wdn8_left.reshape(FF, D // 2), wdn8_right.reshape(FF, D // 2))
    memn16, kv16 = _mem_fwd(mem, gains["g_mem"], wkv16)
    ypre, y16, y2, x1 = _mix_out(y_attn, bcu, qx16, kv16, cw8, gains["g_attn_out"], gains["g_conv_out"],
                                 gains["g_xattn_out"], gains["g_post_mix"], wout16, x, [])
    a16, du16, h2_16, df2_16, dx1, loss8, dg_mlp = _mlp(
        x1, tgt, gains["g_pre_mlp"], gains["g_post_mlp"], wup8, wdn_halves)

    sums = {"w_up": _wgrad_cols(place, h2_16, du16, FF_BLK, "wgrad_up"),
            "w_down": _wgrad_cols(place, df2_16, a16, FF_BLK, "wgrad_down", square_b=True, transpose_out=True)}

    head_id = jnp.arange(AW, dtype=jnp.int32) // HEAD
    head_ones = (head_id[:, None] == head_id[None, :]).astype(BF16)
    dy2_16, qdo, ld, dbcu, dqx, dgs, dcw, dkv = _mix_out_bwd(
        dx1, y2, ypre, ltot, head_ones, q, bcu, qx16, kv16, cw8, gains["g_post_mix"], gains["g_attn_out"],
        gains["g_conv_out"], gains["g_xattn_out"], wout16)
    dkv16, dg_mem = _mem_bwd(mem, gains["g_mem"], wkv16, dkv)
    sums["w_mem_kv"] = _wgrad_rows(place, memn16, dkv16, "wgrad_mem_kv")
    sums["w_out"] = _wgrad_rows(place, y16, dy2_16, "wgrad_out")
    out = _attn_bwd(qdo, kvp, ld, [s[0] for s in sums.values()])
    dqkv, landed = out[:9], out[9:]
    reduced = {n: (s[1], landed[t]) for t, (n, s) in enumerate(sums.items())}
    dproj16, grad_x, dg_in = _in_proj_bwd(dqkv, dbcu, dqx, cos, sins, win16, x, gains["g_pre_mix"], dx1)

    _, in_own, in_landed, small_tot = _wgrad_cols(place, h16, dproj16, PW // N_DEV, "wgrad_in", to_chips=True,
                                                  small=(dg_in, dg_mem, dgs, dg_mlp, dcw, loss8))
    reduced["w_in"] = (in_own, in_landed)
    return grad_x, reduced, small_tot


BIG = ("w_in", "w_mem_kv", "w_out", "w_up", "w_down")
ORDER = ("g_pre_mix", "g_mem", "w_in", "w_mem_kv", "conv_w", "g_attn_out", "g_conv_out", "g_xattn_out", "w_out",
         "g_post_mix", "g_pre_mlp", "w_up", "w_down", "g_post_mlp")


def kernel(x, mem, positions, g_pre_mix, g_mem, w_in, w_mem_kv, conv_w, g_attn_out, g_conv_out, g_xattn_out, w_out, g_post_mix, g_pre_mlp, w_up, w_down, g_post_mlp, loss_target, m_g_pre_mix, m_g_mem, m_w_in, m_w_mem_kv, m_conv_w, m_g_attn_out, m_g_conv_out, m_g_xattn_out, m_w_out, m_g_post_mix, m_g_pre_mlp, m_w_up, m_w_down, m_g_post_mlp, v_g_pre_mix, v_g_mem, v_w_in, v_w_mem_kv, v_conv_w, v_g_attn_out, v_g_conv_out, v_g_xattn_out, v_w_out, v_g_post_mix, v_g_pre_mlp, v_w_up, v_w_down, v_g_post_mlp):
    w = dict(g_pre_mix=g_pre_mix, g_mem=g_mem, w_in=w_in, w_mem_kv=w_mem_kv, conv_w=conv_w, g_attn_out=g_attn_out,
             g_conv_out=g_conv_out, g_xattn_out=g_xattn_out, w_out=w_out, g_post_mix=g_post_mix, g_pre_mlp=g_pre_mlp,
             w_up=w_up, w_down=w_down, g_post_mlp=g_post_mlp)
    mo = dict(g_pre_mix=m_g_pre_mix, g_mem=m_g_mem, w_in=m_w_in, w_mem_kv=m_w_mem_kv, conv_w=m_conv_w,
              g_attn_out=m_g_attn_out, g_conv_out=m_g_conv_out, g_xattn_out=m_g_xattn_out, w_out=m_w_out,
              g_post_mix=m_g_post_mix, g_pre_mlp=m_g_pre_mlp, w_up=m_w_up, w_down=m_w_down, g_post_mlp=m_g_post_mlp)
    vo = dict(g_pre_mix=v_g_pre_mix, g_mem=v_g_mem, w_in=v_w_in, w_mem_kv=v_w_mem_kv, conv_w=v_conv_w,
              g_attn_out=v_g_attn_out, g_conv_out=v_g_conv_out, g_xattn_out=v_g_xattn_out, w_out=v_w_out,
              g_post_mix=v_g_post_mix, g_pre_mlp=v_g_pre_mlp, w_up=v_w_up, w_down=v_w_down, g_post_mlp=v_g_post_mlp)

    xi, yi, ci = lax.axis_index("x"), lax.axis_index("y"), lax.axis_index("c")
    me = 4 * xi + 2 * yi + ci
    place = jnp.stack([ci, 2 * xi + yi]).astype(jnp.int32)

    shards = {n: w[n][0].astype(BF16) for n in BIG}
    shards["conv_w"] = jnp.zeros((SUBLANES, LANES), F32).at[0:3, 0:CW // N_DEV].set(conv_w[0])

    gains = {n: w[n] for n, _, _, _ in SMALL}
    grad_x, reduced, small_tot = _local_step(x[0], mem[0], positions[0], gains, shards, loss_target[0], place)

    updated = {}
    for group in (("w_up", "w_down"), ("w_in", "w_out", "w_mem_kv")):
        updated.update(_adamw_shards({n: (*reduced[n], w[n][0], mo[n][0], vo[n][0]) for n in group},
                                     "adamw_" + "_".join(group))[0])
    grad, delta, new_m, new_v = {}, {}, {}, {}
    for n, (g, d_, m_, v_) in updated.items():
        grad[n], delta[n], new_m[n], new_v[n] = g[None], d_[None], m_[None], v_[None]

    params = {n: (w[n], mo[n], vo[n]) for n, _, _, _ in SMALL}
    params["conv_w"] = (w["conv_w"][0], mo["conv_w"][0], vo["conv_w"][0])
    loss, small = _small_update(small_tot, me.reshape(1).astype(jnp.int32), params)
    for n, (g, d_, m_, v_) in small.items():
        lead = (lambda a: a[None]) if n == "conv_w" else (lambda a: a)
        grad[n], delta[n], new_m[n], new_v[n] = lead(g), lead(d_), lead(m_), lead(v_)

    return (loss, grad_x[None], *[grad[n] for n in ORDER], *[delta[n] for n in ORDER],
            *[new_m[n] for n in ORDER], *[new_v[n] for n in ORDER])
```

```python
import jax
import jax.numpy as jnp
from jax import lax
from jax.experimental import pallas as pl
from jax.experimental.pallas import tpu as pltpu

F32, BF16 = jnp.float32, jnp.bfloat16
MESH = pl.DeviceIdType.MESH
ANY = pl.BlockSpec(memory_space=pl.ANY)

N_DEV = 8
D = 1024
S = 4096
N_MEM = 256
HEAD = 64
AW, CW, XW = 512, 256, 256
PW = 3 * AW + 3 * CW + XW
FF = 4096
FF_BLK = FF // N_DEV
EPS = 1e-6
NEG = -1e30
SCALE = HEAD ** -0.5
ROPE_THETA = 10000.0
LANES = 128
SUBLANES = 8

ADAM_LR, ADAM_B1, ADAM_B2, ADAM_EPS, ADAM_WD, ADAM_STEP = 0.001, 0.9, 0.999, 1e-08, 0.01, 10

TQ = 512
TQ_MLP = 256
NT = S // TQ


def _cparams(vmem_mb, n_grid=1):
    return pltpu.CompilerParams(dimension_semantics=("arbitrary",) * n_grid, vmem_limit_bytes=vmem_mb << 20)


def _const(shape):
    nd = len(shape)
    return pl.BlockSpec(shape, lambda *_: (0,) * nd, pipeline_mode=pl.Buffered(1))


def _acc(shape):
    nd = len(shape)
    return pl.BlockSpec(shape, lambda *_: (0,) * nd)


def _dot(a, b):
    return jnp.dot(a, b, preferred_element_type=F32)


def _dot_nt(a, b):
    return lax.dot_general(a, b, (((1,), (1,)), ((), ())), preferred_element_type=F32)


def _dot_tn(a, b):
    return lax.dot_general(a, b, (((0,), (0,)), ((), ())), preferred_element_type=F32)


def _rms(x, g):
    r = lax.rsqrt(jnp.mean(x * x, axis=-1, keepdims=True) + EPS)
    n = x * r
    return n * g, n, r


def _rms_bwd(dy, n, r, g):
    dn = dy * g
    dx = r * (dn - n * jnp.mean(dn * n, axis=-1, keepdims=True))
    return dx, jnp.sum(dy * n, axis=0, keepdims=True)


def _rot_half(t):
    lane = lax.broadcasted_iota(jnp.int32, t.shape, 1)
    n = t.shape[1]
    return jnp.where((lane % HEAD) < HEAD // 2, pltpu.roll(t, n - HEAD // 2, 1), pltpu.roll(t, HEAD // 2, 1))


def _rope_table(pos_col, invf, sgn, shards):
    def body(p_ref, f_ref, s_ref, c_out, s_out):
        ang = p_ref[...] * f_ref[...]
        c_out[...] = jnp.cos(ang)
        s_out[...] = jnp.sin(ang) * s_ref[...]

    tile = pl.BlockSpec((TQ, LANES), lambda i: (i, 0))
    return _call_with_gather(
        body, NT, shards, name="rope_table",
        in_specs=[pl.BlockSpec((TQ, 1), lambda i: (i, 0)), _const((1, LANES)), _const((1, LANES))],
        out_specs=[tile, tile], out_shape=[jax.ShapeDtypeStruct((S, LANES), F32)] * 2,
        scratch_shapes=[], vmem_mb=32, args=(pos_col, invf, sgn))


def _all_heads(t):
    return jnp.tile(t, (1, AW // LANES))


def _mem_fwd(mem, g_mem, wkv16):
    def body(m_ref, g_ref, w_ref, n16_ref, kv_ref):
        y, _, _ = _rms(m_ref[...], g_ref[...])
        y16 = y.astype(BF16)
        n16_ref[...] = y16
        kv_ref[...] = _dot(y16, w_ref[...]).astype(BF16)

    return pl.pallas_call(
        body, name="mem_fwd",
        out_shape=[jax.ShapeDtypeStruct((N_MEM, D), BF16), jax.ShapeDtypeStruct((N_MEM, 2 * XW), BF16)],
        compiler_params=pltpu.CompilerParams(vmem_limit_bytes=32 << 20))(mem, g_mem, wkv16)


def _in_proj(x, g, w8, cos, sins, shards):
    blk = PW // N_DEV

    def body(x_ref, g_ref, w8_ref, c_ref, s_ref, q_ref, kv_ref, bcu_ref, qx_ref, h_ref, w_out, w_ref):
        @pl.when(pl.program_id(0) == 0)
        def _():
            for j in range(N_DEV):
                w_ref[:, j * blk:(j + 1) * blk] = w8_ref[j]
            w_out[...] = w_ref[...]

        y, _, _ = _rms(x_ref[...], g_ref[...])
        h = y.astype(BF16)
        h_ref[...] = h
        proj = _dot(h, w_ref[...])
        cos, sn = _all_heads(c_ref[...]), _all_heads(s_ref[...])
        q, k = proj[:, 0:AW], proj[:, AW:2 * AW]
        q_ref[...] = (q * cos + _rot_half(q) * sn) * SCALE
        kv_ref[...] = _pack_pair(k * cos + _rot_half(k) * sn, proj[:, 2 * AW:3 * AW])
        bcu_ref[...] = proj[:, 3 * AW:3 * AW + 3 * CW]
        qx_ref[...] = (proj[:, 3 * AW + 3 * CW:] * SCALE).astype(BF16)

    def tile(w):
        return pl.BlockSpec((TQ, w), lambda i: (i, 0))

    return _call_with_gather(
        body, NT, shards, name="in_proj",
        in_specs=[tile(D), _const((1, D)), _const((N_DEV, D, blk)), tile(LANES), tile(LANES)],
        out_specs=[tile(AW), tile(AW), tile(3 * CW), tile(XW), tile(D), _acc((D, PW))],
        out_shape=[jax.ShapeDtypeStruct((S, AW), F32)] * 2 + [
            jax.ShapeDtypeStruct((S, 3 * CW), F32), jax.ShapeDtypeStruct((S, XW), BF16),
            jax.ShapeDtypeStruct((S, D), BF16), jax.ShapeDtypeStruct((D, PW), BF16)],
        scratch_shapes=[pltpu.VMEM((D, PW), BF16)], vmem_mb=56, args=(x, g, w8, cos, sins))


ATTN_PLANS = (("p1", 1, 128, 32), ("p4", 8, 64, 8), ("p16", 16, 128, 2))
PAD = 128
WIN = 256


ATTN_UNROLL = 16


def _fill_bias(tab, qblk, partner):
    qi = lax.broadcasted_iota(jnp.int32, (2 * qblk, WIN), 0) & (qblk - 1)
    kj = lax.broadcasted_iota(jnp.int32, (2 * qblk, WIN), 1)
    piece = kj >> (qblk.bit_length() - 1)
    kk = kj & (qblk - 1)
    prev = (piece & 1) == 0
    of_partner = piece >= 2
    for first in (0, 1):
        for par in (0, 1):
            lo = jnp.where(prev, (qblk if first else qi) + jnp.where(of_partner, par, 0), 0)
            hi = jnp.where(prev, qblk, qi + jnp.where(of_partner, par - 1, 0))
            tab[2 * first + par] = jnp.where((kk >= lo) & (kk <= hi), 0.0, NEG).astype(F32)


def _block_rows(g, qblk, nbc, partner):
    own = pl.ds(pl.multiple_of(PAD + g * qblk, qblk), qblk)
    first = ((g & (nbc - 1)) == 0).astype(jnp.int32)
    if partner:
        gp = jnp.bitwise_xor(g, 4 * nbc)
        wins = (pl.ds(pl.multiple_of(PAD + (g - 1) * qblk, qblk), 2 * qblk),
                pl.ds(pl.multiple_of(PAD + (gp - 1) * qblk, qblk), 2 * qblk))
        return own, wins, 2 * first + ((g >> ((4 * nbc).bit_length() - 1)) & 1)
    return own, (pl.ds(pl.multiple_of(PAD + (g - 1) * qblk, qblk), 2 * qblk),), 2 * first


def _pack_pair(lo, hi):
    lo_bits = lax.bitcast_convert_type(lo.astype(BF16).astype(F32), jnp.uint32) >> 16
    hi_bits = lax.bitcast_convert_type(hi.astype(BF16).astype(F32), jnp.uint32) & jnp.uint32(0xFFFF0000)
    return lax.bitcast_convert_type(hi_bits | lo_bits, F32)


def _unpack_pair(c):
    bits = lax.bitcast_convert_type(c, jnp.uint32)
    lo = lax.bitcast_convert_type(bits << 16, F32).astype(BF16)
    hi = lax.bitcast_convert_type(bits & jnp.uint32(0xFFFF0000), F32).astype(BF16)
    return lo, hi


def _window(ref, wins):
    parts = [ref[w, :] for w in wins]
    return parts[0] if len(parts) == 1 else jnp.concatenate(parts, axis=0)


def _stack_heads(t, lane):
    zero = jnp.zeros_like(t)
    return jnp.concatenate([jnp.where(lane < HEAD, t, zero), jnp.where(lane >= HEAD, t, zero)], axis=0)


def _unstack_heads(t2, lane):
    half = t2.shape[0] // 2
    return jnp.where(lane < HEAD, t2[0:half, :], t2[half:, :])


def _lanes_of(step):
    return pl.ds(pl.multiple_of(step * LANES, LANES), LANES)


def _whole_wait(buf, sem):
    whole = buf.at[pl.ds(PAD, S), :]
    return pltpu.make_async_copy(whole, whole, sem)


def _whole_waits(bufs, sems):
    return [_whole_wait(buf, sems.at[i]) for i, buf in enumerate(bufs)]


def _class_gather(views, bufs, sems, lanes):
    copies = []
    for i, (view, buf) in enumerate(zip(views, bufs)):
        if view.ndim == 2:
            copies.append(pltpu.make_async_copy(view.at[:, lanes], buf.at[pl.ds(PAD, S), :], sems.at[i]))
        else:
            per, n_cls = view.shape[0], view.shape[1]
            copies += [pltpu.make_async_copy(view.at[:, c, lanes], buf.at[pl.ds(PAD + c * per, per), :], sems.at[i])
                       for c in range(n_cls)]
    return copies


def _class_scatter(bufs, dsts, sems, lanes):
    copies = []
    for i, (buf, dst) in enumerate(zip(bufs, dsts)):
        if dst.ndim == 2:
            copies.append(pltpu.make_async_copy(buf.at[pl.ds(PAD, S), :], dst.at[:, lanes], sems.at[i]))
            continue
        per, n_cls = dst.shape[0], dst.shape[1]
        copies += [pltpu.make_async_copy(buf.at[pl.ds(PAD + c * per, per), :], dst.at[:, c, lanes], sems.at[i])
                   for c in range(n_cls)]
    return copies


def _start(copies):
    for cp in copies:
        cp.start()


def _wait(waits):
    for w in waits:
        w.wait()


def _attn_fwd(q, kvp, shards=()):
    views = [[a] + [a.reshape(S // n, n, AW) for _, n, _, _ in ATTN_PLANS[1:]] for a in (q, kvp)]
    flat = [views[a][p] for p in range(3) for a in range(2)]
    ng = len(shards)
    n_grid = AW // LANES

    def body(*refs):
        hbm = [refs[2 * p:2 * p + 2] for p in range(3)]
        refs = refs[6:]
        shard_refs, refs = refs[:ng], refs[ng:]
        y_ref, lt_ref = refs[0:2]
        whole_refs, refs = refs[2:2 + ng], refs[2 + ng:]
        bufs = [refs[2 * p:2 * p + 2] for p in range(3)]
        oc4, lc4, oc16, lc16, tab128, tab4, sem_in = refs[6:13]
        step = pl.program_id(0)
        if ng:
            start_gather, relay_gather, finish_gather = _gather_steps(shard_refs, whole_refs, *refs[13:])
            pl.when(step == 0)(start_gather)
            pl.when(step == n_grid // 2)(relay_gather)
        now = [_class_gather(hbm[p], bufs[p], sem_in.at[p], _lanes_of(step)) for p in range(3)]
        nxt = [_class_gather(hbm[p], bufs[p], sem_in.at[p], _lanes_of(step + 1)) for p in range(3)]

        @pl.when(step == 0)
        def _():
            for p in range(3):
                _start(now[p])
                for b in bufs[p]:
                    b[0:PAD, :] = jnp.zeros((PAD, LANES), F32)
            _fill_bias(tab128, 128, False)
            _fill_bias(tab4, 64, True)

        def prefetch(p):
            pl.when(step + 1 < n_grid)(lambda: _start(nxt[p]))

        lane = lax.broadcasted_iota(jnp.int32, (1, LANES), 1)
        ones = jnp.ones((WIN, LANES), BF16)

        def run(plan, bq, bkv, tab, o_dst, l_dst, dst_pad):
            _, n_cls, qblk, nbc = plan
            partner = n_cls == 8

            def block(g, carry):
                own, wins, mask = _block_rows(g, qblk, nbc, partner)
                q2 = _stack_heads(bq[own, :].astype(BF16), lane)
                kw, vwin = _unpack_pair(_window(bkv, wins))
                vw = jnp.concatenate([vwin, ones], axis=1)
                s = _dot_nt(q2, kw) + tab[mask]
                m = jnp.max(s, axis=1, keepdims=True)
                oe = _dot(jnp.exp(s - m).astype(BF16), vw)
                den = oe[:, LANES:]
                dst = pl.ds(pl.multiple_of(dst_pad + g * qblk, qblk), qblk)
                o_dst[dst, :] = _unstack_heads(oe[:, 0:LANES] / den, lane)
                l_dst[dst, :] = _unstack_heads(m + jnp.log(den), lane)
                return carry
            lax.fori_loop(0, n_cls * nbc, block, 0, unroll=ATTN_UNROLL)

        _wait(_whole_waits(bufs[0], sem_in.at[0]))
        run(ATTN_PLANS[0], *bufs[0], tab128, y_ref, lt_ref, 0)
        prefetch(0)
        _wait(_whole_waits(bufs[1], sem_in.at[1]))
        run(ATTN_PLANS[1], *bufs[1], tab4, oc4, lc4, PAD)
        prefetch(1)
        _wait(_whole_waits(bufs[2], sem_in.at[2]))
        run(ATTN_PLANS[2], *bufs[2], tab128, oc16, lc16, PAD)
        prefetch(2)

        n_rows = 64

        def token_order(buf, t, n_cls):
            per = S // n_cls
            first = PAD + t * (n_rows // n_cls)
            return jnp.concatenate([buf[pl.ds(first + jj, n_cls, stride=per), :] for jj in range(n_rows // n_cls)],
                                   axis=0)

        def combine(t, carry):
            rows = pl.ds(pl.multiple_of(t * n_rows, n_rows), n_rows)
            l0, l1, l2 = lt_ref[rows, :], token_order(lc4, t, 8), token_order(lc16, t, 16)
            lm = jnp.maximum(jnp.maximum(l0, l1), l2)
            e0, e1, e2 = jnp.exp(l0 - lm), jnp.exp(l1 - lm), jnp.exp(l2 - lm)
            den = e0 + e1 + e2
            y_ref[rows, :] = (e0 * y_ref[rows, :] + e1 * token_order(oc4, t, 8)
                              + e2 * token_order(oc16, t, 16)) / den
            lt_ref[rows, :] = lm + jnp.log(den)
            return carry
        lax.fori_loop(0, S // n_rows, combine, 0, unroll=2)

        if ng:
            pl.when(step == n_grid - 1)(finish_gather)

    col = pl.BlockSpec((S, LANES), lambda h: (0, h))
    padded = pltpu.VMEM((PAD + S, LANES), F32)
    return pl.pallas_call(
        body, grid=(n_grid,), name="attn_fwd",
        in_specs=[ANY] * (6 + ng), out_specs=[col, col] + [ANY] * ng,
        out_shape=[jax.ShapeDtypeStruct((S, AW), F32)] * 2 + _gathered_shapes(shards),
        scratch_shapes=[padded] * 10 + [
            pltpu.VMEM((4, 256, WIN), F32), pltpu.VMEM((4, 128, WIN), F32), pltpu.SemaphoreType.DMA((3, 2))]
        + (_gather_scratch(ng) if ng else []),
        compiler_params=_cparams(56))(*flat, *shards)


def _conv_taps(z, zprev, row):
    z1 = jnp.where(row == 0, zprev[7:8, :], pltpu.roll(z, 1, 0))
    z2 = jnp.where(row == 0, zprev[6:7, :], jnp.where(row == 1, zprev[7:8, :], pltpu.roll(z, 2, 0)))
    return z1, z2


def _xattn_scores(qm, km):
    s = _dot_nt(qm, km)
    m = jnp.max(s, axis=1, keepdims=True)
    e = jnp.exp(s - m)
    return e, jnp.sum(e, axis=1, keepdims=True)


def _mix_out(y_attn, bcu, qx16, kv16, cw8, g_attn, g_conv, g_x, g_post, wout16, x, shards):
    def body(ya_ref, bcu_ref, halo_ref, qx_ref, kv_ref, cw_ref, ga_ref, gc_ref, gx_ref, gp_ref, w_ref, x_ref,
             ypre_ref, y16_ref, y2_ref, x1_ref):
        i = pl.program_id(0)
        bcu = bcu_ref[...]
        b, c, u = bcu[:, 0:CW], bcu[:, CW:2 * CW], bcu[:, 2 * CW:]
        z = c * u
        halo = halo_ref[...]
        zprev = jnp.where(i > 0, halo[:, CW:2 * CW] * halo[:, 2 * CW:], 0.0)
        row = lax.broadcasted_iota(jnp.int32, z.shape, 0)
        z1, z2 = _conv_taps(z, zprev, row)
        cw = cw_ref[...]
        y_conv = b * (z2 * cw[0:1, :] + z1 * cw[1:2, :] + z * cw[2:3, :])

        qx = qx_ref[...]
        kv = kv_ref[...]
        km, vm = kv[:, 0:XW], kv[:, XW:]
        lane = lax.broadcasted_iota(jnp.int32, qx.shape, 1)
        y_x = jnp.zeros(qx.shape, F32)
        for h in range(XW // HEAD):
            hm = (lane >= h * HEAD) & (lane < (h + 1) * HEAD)
            e, l = _xattn_scores(jnp.where(hm, qx, jnp.zeros_like(qx)), km)
            y_x = jnp.where(hm, _dot(e.astype(BF16), vm) / l, y_x)

        y_attn = ya_ref[...]
        ypre_ref[:, 0:AW] = y_attn
        ypre_ref[:, AW:AW + CW] = y_conv
        ypre_ref[:, AW + CW:] = y_x
        y = jnp.concatenate([_rms(y_attn, ga_ref[...])[0], _rms(y_conv, gc_ref[...])[0],
                             _rms(y_x, gx_ref[...])[0]], axis=1).astype(BF16)
        y16_ref[...] = y
        y2 = _dot(y, w_ref[...])
        y2_ref[...] = y2
        x1_ref[...] = x_ref[...] + _rms(y2, gp_ref[...])[0]

    def tile(w):
        return pl.BlockSpec((TQ, w), lambda i: (i, 0))

    halo = pl.BlockSpec((SUBLANES, 3 * CW), lambda i: (jnp.maximum(i * (TQ // SUBLANES) - 1, 0), 0))
    return _call_with_gather(
        body, NT, shards, name="mix_out",
        in_specs=[tile(AW), tile(3 * CW), halo, tile(XW), _const((N_MEM, 2 * XW)), _const((SUBLANES, CW)),
                  _const((1, AW)), _const((1, CW)), _const((1, XW)), _const((1, D)), _const((D, D)), tile(D)],
        out_specs=[tile(D), tile(D), tile(D), tile(D)],
        out_shape=[jax.ShapeDtypeStruct((S, D), F32), jax.ShapeDtypeStruct((S, D), BF16),
                   jax.ShapeDtypeStruct((S, D), F32), jax.ShapeDtypeStruct((S, D), F32)],
        scratch_shapes=[], vmem_mb=56,
        args=(y_attn, bcu, bcu, qx16, kv16, cw8, g_attn, g_conv, g_x, g_post, wout16, x))


def _mlp(x1, tgt, g_pre, g_post, wup8, wdn_halves):
    tq = TQ_MLP
    half = D // 2

    def body(x1_ref, t_ref, g1_ref, g2_ref, wu_ref, wda_ref, wdb_ref,
             a16_ref, du_ref, h2_ref, df2_ref, dx1_ref, loss_ref, dg_ref, a32):
        @pl.when(pl.program_id(0) == 0)
        def _():
            loss_ref[...] = jnp.zeros_like(loss_ref)
            dg_ref[...] = jnp.zeros_like(dg_ref)

        x1 = x1_ref[...]
        g1, g2 = g1_ref[...], g2_ref[...]
        y1, n1, r1 = _rms(x1, g1)
        h2 = y1.astype(BF16)
        h2_ref[...] = h2
        f2a = jnp.zeros((tq, half), F32)
        f2b = jnp.zeros((tq, half), F32)
        for j in range(N_DEV):
            cols = slice(j * FF_BLK, (j + 1) * FF_BLK)
            a = jnp.maximum(_dot(h2, wu_ref[j]), 0.0)
            a32[:, cols] = a
            a16_ref[:, cols] = a.astype(BF16)
            f = (a * a).astype(BF16)
            f2a = f2a + _dot(f, wda_ref[cols, :])
            f2b = f2b + _dot(f, wdb_ref[cols, :])
        f2 = jnp.concatenate([f2a, f2b], axis=1)
        y2, n2, r2 = _rms(f2, g2)
        e = x1 + y2 - t_ref[...]
        sq = jnp.sum(jnp.sum(e * e, axis=1, keepdims=True), axis=0, keepdims=True)
        loss_ref[...] += jnp.broadcast_to(sq * (0.5 / D), loss_ref.shape)
        dout = e * (1.0 / D)
        df2, dg2 = _rms_bwd(dout, n2, r2, g2)
        df2_16 = df2.astype(BF16)
        df2_ref[...] = df2_16
        dh2 = jnp.zeros((tq, D), F32)
        for j in range(N_DEV):
            cols = slice(j * FF_BLK, (j + 1) * FF_BLK)
            df = _dot_nt(df2_16[:, 0:half], wda_ref[cols, :]) + _dot_nt(df2_16[:, half:], wdb_ref[cols, :])
            du = (df * (2.0 * a32[:, cols])).astype(BF16)
            du_ref[:, cols] = du
            dh2 = dh2 + _dot_nt(du, wu_ref[j])
        dx, dg1 = _rms_bwd(dh2, n1, r1, g1)
        dx1_ref[...] = dout + dx
        dg_ref[0:1, :] += dg2
        dg_ref[1:2, :] += dg1

    def tile(w):
        return pl.BlockSpec((tq, w), lambda i: (i, 0))

    return pl.pallas_call(
        body, grid=(S // tq,), name="mlp",
        in_specs=[tile(D), tile(D), _const((1, D)), _const((1, D)), _const((N_DEV, D, FF_BLK)), _const((FF, half)), _const((FF, half))],
        out_specs=[tile(FF), tile(FF), tile(D), tile(D), tile(D), _acc((SUBLANES, LANES)), _acc((SUBLANES, D))],
        out_shape=[jax.ShapeDtypeStruct((S, FF), BF16), jax.ShapeDtypeStruct((S, FF), BF16),
                   jax.ShapeDtypeStruct((S, D), BF16), jax.ShapeDtypeStruct((S, D), BF16),
                   jax.ShapeDtypeStruct((S, D), F32), jax.ShapeDtypeStruct((SUBLANES, LANES), F32),
                   jax.ShapeDtypeStruct((SUBLANES, D), F32)],
        scratch_shapes=[pltpu.VMEM((tq, FF), F32)],
        compiler_params=_cparams(56))(x1, tgt, g_pre, g_post, wup8, *wdn_halves)


def _mix_out_bwd(dx1, y2, ypre, ltot, head_ones, q, bcu, qx16, kv16, cw8, g_post, g_attn, g_conv, g_x, wout16):
    def body(dx1_ref, y2_ref, ypre_ref, lt_ref, e_ref, q_ref, bcu_ref, halo_ref, qx_ref, kv_ref, cw_ref, gp_ref,
             ga_ref, gc_ref, gx_ref, w_ref, dy2_ref, qdo_ref, ld_ref, dbcu_ref, dqx_ref, dgs_ref, dcw_ref, dkv_ref,
             carry):
        i = pl.program_id(0)

        @pl.when(i == 0)
        def _():
            dgs_ref[...] = jnp.zeros_like(dgs_ref)
            dcw_ref[...] = jnp.zeros_like(dcw_ref)
            dkv_ref[...] = jnp.zeros_like(dkv_ref)
            carry[...] = jnp.zeros_like(carry)

        gp = gp_ref[...]
        _, n, r = _rms(y2_ref[...], gp)
        dy2, dgp = _rms_bwd(dx1_ref[...], n, r, gp)
        dy2_16 = dy2.astype(BF16)
        dy2_ref[...] = dy2_16
        dy = _dot_nt(dy2_16, w_ref[...])

        ypre = ypre_ref[...]
        ga, gc, gx = ga_ref[...], gc_ref[...], gx_ref[...]
        _, na, ra = _rms(ypre[:, 0:AW], ga)
        dya, dga = _rms_bwd(dy[:, 0:AW], na, ra, ga)
        _, nc, rc = _rms(ypre[:, AW:AW + CW], gc)
        dyc, dgc = _rms_bwd(dy[:, AW:AW + CW], nc, rc, gc)
        y_x = ypre[:, AW + CW:]
        _, nx, rx = _rms(y_x, gx)
        dyx, dgx = _rms_bwd(dy[:, AW + CW:], nx, rx, gx)
        qdo_ref[...] = _pack_pair(q_ref[...], dya)
        prod = dya * ypre[:, 0:AW]
        hi = prod.astype(BF16)
        lo = (prod - hi.astype(F32)).astype(BF16)
        head_sum = _dot(hi, e_ref[...]) + _dot(lo, e_ref[...])
        lane_a = lax.broadcasted_iota(jnp.int32, prod.shape, 1)
        ld_ref[...] = jnp.where((lane_a % HEAD) < HEAD // 2, lt_ref[...], head_sum)
        dgs_ref[0:1, :] += dgp
        dgs_ref[1:2, :] += jnp.concatenate([dga, dgc, dgx], axis=1)

        bcu = bcu_ref[...]
        b, c, u = bcu[:, 0:CW], bcu[:, CW:2 * CW], bcu[:, 2 * CW:]
        z = c * u
        halo = halo_ref[...]
        zprev = jnp.where(i < NT - 1, halo[:, CW:2 * CW] * halo[:, 2 * CW:], 0.0)
        row = lax.broadcasted_iota(jnp.int32, z.shape, 0)
        z1, z2 = _conv_taps(z, zprev, row)
        cw = cw_ref[...]
        conv = z2 * cw[0:1, :] + z1 * cw[1:2, :] + z * cw[2:3, :]
        dconv = dyc * b
        nxt = carry[...]
        dn1 = jnp.where(row == TQ - 1, nxt[0:1, :], pltpu.roll(dconv, TQ - 1, 0))
        dn2 = jnp.where(row == TQ - 1, nxt[1:2, :], jnp.where(row == TQ - 2, nxt[0:1, :], pltpu.roll(dconv, TQ - 2, 0)))
        carry[...] = dconv[0:SUBLANES, :]
        dz = dconv * cw[2:3, :] + dn1 * cw[1:2, :] + dn2 * cw[0:1, :]
        dbcu_ref[:, 0:CW] = (dyc * conv).astype(BF16)
        dbcu_ref[:, CW:2 * CW] = (dz * u).astype(BF16)
        dbcu_ref[:, 2 * CW:] = (dz * c).astype(BF16)
        dcw_ref[0:1, :] += jnp.sum(z2 * dconv, axis=0, keepdims=True)
        dcw_ref[1:2, :] += jnp.sum(z1 * dconv, axis=0, keepdims=True)
        dcw_ref[2:3, :] += jnp.sum(z * dconv, axis=0, keepdims=True)

        qx = qx_ref[...]
        kv = kv_ref[...]
        km, vm = kv[:, 0:XW], kv[:, XW:]
        lane = lax.broadcasted_iota(jnp.int32, qx.shape, 1)
        dqx = jnp.zeros(qx.shape, F32)
        dkm = jnp.zeros((N_MEM, XW), F32)
        dvm = jnp.zeros((N_MEM, XW), F32)
        for h in range(XW // HEAD):
            hm = (lane >= h * HEAD) & (lane < (h + 1) * HEAD)
            qm = jnp.where(hm, qx, jnp.zeros_like(qx))
            e, l = _xattn_scores(qm, km)
            p = e / l
            dom = jnp.where(hm, dyx, 0.0)
            do16 = dom.astype(BF16)
            dsum = jnp.sum(dom * y_x, axis=1, keepdims=True)
            ds = (p * (_dot_nt(do16, vm) - dsum)).astype(BF16)
            dqx = jnp.where(hm, _dot(ds, km), dqx)
            dkm = dkm + _dot_tn(ds, qm)
            dvm = dvm + _dot_tn(p.astype(BF16), do16)
        dqx_ref[...] = (dqx * SCALE).astype(BF16)
        dkv_ref[:, 0:XW] += dkm
        dkv_ref[:, XW:] += dvm

    def tile(w):
        return pl.BlockSpec((TQ, w), lambda i: (NT - 1 - i, 0))

    halo = pl.BlockSpec((SUBLANES, 3 * CW), lambda i: (jnp.maximum((NT - 1 - i) * (TQ // SUBLANES) - 1, 0), 0))
    return pl.pallas_call(
        body, grid=(NT,), name="mix_out_bwd",
        in_specs=[tile(D), tile(D), tile(D), tile(AW), _const((AW, AW)), tile(AW), tile(3 * CW), halo, tile(XW),
                  _const((N_MEM, 2 * XW)), _const((SUBLANES, CW)), _const((1, D)), _const((1, AW)), _const((1, CW)),
                  _const((1, XW)), _const((D, D))],
        out_specs=[tile(D), tile(AW), tile(AW), tile(3 * CW), tile(XW), _acc((SUBLANES, D)), _acc((SUBLANES, CW)),
                   _acc((N_MEM, 2 * XW))],
        out_shape=[jax.ShapeDtypeStruct((S, D), BF16), jax.ShapeDtypeStruct((S, AW), F32),
                   jax.ShapeDtypeStruct((S, AW), F32),
                   jax.ShapeDtypeStruct((S, 3 * CW), BF16), jax.ShapeDtypeStruct((S, XW), BF16),
                   jax.ShapeDtypeStruct((SUBLANES, D), F32), jax.ShapeDtypeStruct((SUBLANES, CW), F32),
                   jax.ShapeDtypeStruct((N_MEM, 2 * XW), F32)],
        scratch_shapes=[pltpu.VMEM((SUBLANES, CW), F32)],
        compiler_params=_cparams(56))(dx1, y2, ypre, ltot, head_ones, q, bcu, bcu, qx16, kv16, cw8, g_post, g_attn,
                                      g_conv, g_x, wout16)


def _attn_bwd(qdo, kvp, ld, chip_sums=()):
    n_in = 3
    views = [[a] + [a.reshape(S // n, n, AW) for _, n, _, _ in ATTN_PLANS[1:]] for a in (qdo, kvp, ld)]
    flat = [views[a][p] for p in range(3) for a in range(n_in)]
    ns = len(chip_sums)
    n_grid = AW // LANES

    def body(*refs):
        hbm = [refs[n_in * p:n_in * p + n_in] for p in range(3)]
        refs = refs[3 * n_in:]
        sum_refs, refs = refs[:ns], refs[ns:]
        outs = [refs[3 * p:3 * p + 3] for p in range(3)]
        landed_refs, sc = refs[9:9 + ns], refs[9 + ns:]
        bufs = [sc[3 * p:3 * p + 3] for p in range(3)]
        res = [sc[9 + 3 * p:12 + 3 * p] for p in range(3)]
        tab128, tab4, sem_in, sem_out = sc[18:22]
        step = pl.program_id(0)
        if ns:
            start_chips, finish_chips = _chips_steps(sum_refs, landed_refs, *sc[22:])
            pl.when(step == 0)(start_chips)
        now = [_class_gather(hbm[p], bufs[p], sem_in.at[p], _lanes_of(step)) for p in range(3)]
        nxt = [_class_gather(hbm[p], bufs[p], sem_in.at[p], _lanes_of(step + 1)) for p in range(3)]

        @pl.when(step == 0)
        def _():
            for p in range(3):
                _start(now[p])
                for b in bufs[p]:
                    b[0:PAD, :] = jnp.zeros((PAD, LANES), F32)
            _fill_bias(tab128, 128, False)
            _fill_bias(tab4, 64, True)

        def prefetch(p):
            pl.when(step + 1 < n_grid)(lambda: _start(nxt[p]))

        for p in range(3):
            for b in res[p]:
                b[...] = jnp.zeros_like(b)
        lane = lax.broadcasted_iota(jnp.int32, (1, LANES), 1)

        def run(plan, plan_bufs, tab, dst):
            _, n_cls, qblk, nbc = plan
            partner = n_cls == 8
            bqdo, bkv, bld = plan_bufs
            rq, rk, rv = dst

            def block(g, carry):
                own, wins, mask = _block_rows(g, qblk, nbc, partner)
                qb, dob = _unpack_pair(bqdo[own, :])
                q2, do2 = _stack_heads(qb, lane), _stack_heads(dob, lane)
                kw, vw = _unpack_pair(_window(bkv, wins))
                ldv = bld[own, :]
                half = HEAD // 2
                lt2 = jnp.concatenate([ldv[:, 0:1], ldv[:, HEAD:HEAD + 1]], axis=0)
                dsum2 = jnp.concatenate([ldv[:, half:half + 1], ldv[:, HEAD + half:HEAD + half + 1]], axis=0)
                p = jnp.exp(_dot_nt(q2, kw) + tab[mask] - lt2)
                ds = (p * (_dot_nt(do2, vw) - dsum2)).astype(BF16)
                rq[own, :] = _unstack_heads(_dot(ds, kw), lane)
                dkw = _dot_tn(ds, q2)
                dvw = _dot_tn(p.astype(BF16), do2)
                n_w = WIN // len(wins)
                for i, w in enumerate(wins):
                    rk[w, :] += dkw[i * n_w:(i + 1) * n_w, :]
                    rv[w, :] += dvw[i * n_w:(i + 1) * n_w, :]
                return carry
            lax.fori_loop(0, n_cls * nbc, block, 0, unroll=ATTN_UNROLL)

        tabs = (tab128, tab4, tab128)
        for p in range(3):
            _wait(_whole_waits(bufs[p], sem_in.at[p]))
            run(ATTN_PLANS[p], bufs[p], tabs[p], res[p])
            prefetch(p)
            _start(_class_scatter(res[p], outs[p], sem_out.at[p], _lanes_of(step)))
        for p in range(3):
            _wait(_whole_waits(res[p], sem_out.at[p]))
        if ns:
            pl.when(step == n_grid - 1)(finish_chips)

    padded = pltpu.VMEM((PAD + S, LANES), F32)
    shapes = [jax.ShapeDtypeStruct(views[0][p].shape, F32) for p in range(3) for _ in range(3)]
    out = pl.pallas_call(
        body, grid=(n_grid,), name="attn_bwd",
        in_specs=[ANY] * (3 * n_in + ns), out_specs=[ANY] * (9 + ns),
        out_shape=shapes + _chips_shapes(chip_sums),
        scratch_shapes=[padded] * 18
        + [pltpu.VMEM((4, 256, WIN), F32), pltpu.VMEM((4, 128, WIN), F32),
           pltpu.SemaphoreType.DMA((3, n_in)), pltpu.SemaphoreType.DMA((3, 3))]
        + (_chips_scratch(ns) if ns else []),
        compiler_params=_cparams(56))(*flat, *chip_sums)
    return [o.reshape(S, AW) for o in out[:9]] + list(out[9:])


def _in_proj_bwd(dqkv, dbcu, dqx, cos, sins, w16, x, g, dx1):
    tq = TQ // 2

    def body(*refs):
        parts = refs[0:9]
        dbcu_ref, dqx_ref, c_ref, s_ref, w_ref, x_ref, g_ref, dx1_ref, dp_ref, gx_ref, dg_ref = refs[9:]

        @pl.when(pl.program_id(0) == 0)
        def _():
            dg_ref[...] = jnp.zeros_like(dg_ref)

        dq, dk, dv = (parts[i][...] + parts[3 + i][...] + parts[6 + i][...] for i in range(3))
        cos, sn = _all_heads(c_ref[...]), _all_heads(s_ref[...])
        dqr = dq * SCALE
        dkr = dk
        dp = jnp.concatenate([(dqr * cos + _rot_half(dqr * sn)).astype(BF16),
                              (dkr * cos + _rot_half(dkr * sn)).astype(BF16), dv.astype(BF16),
                              dbcu_ref[...], dqx_ref[...]], axis=1)
        dp_ref[...] = dp
        dh = _dot_nt(dp, w_ref[...])
        g = g_ref[...]
        _, n, r = _rms(x_ref[...], g)
        dx, dg = _rms_bwd(dh, n, r, g)
        gx_ref[...] = dx1_ref[...] + dx
        dg_ref[0:1, :] += dg

    def tile(w):
        return pl.BlockSpec((tq, w), lambda i: (i, 0))

    return pl.pallas_call(
        body, grid=(S // tq,), name="in_proj_bwd",
        in_specs=[tile(AW)] * 9 + [tile(3 * CW), tile(XW), tile(LANES), tile(LANES), _const((D, PW)),
                                   tile(D), _const((1, D)), tile(D)],
        out_specs=[tile(PW), tile(D), _acc((SUBLANES, D))],
        out_shape=[jax.ShapeDtypeStruct((S, PW), BF16), jax.ShapeDtypeStruct((S, D), F32),
                   jax.ShapeDtypeStruct((SUBLANES, D), F32)],
        compiler_params=_cparams(56))(*dqkv, dbcu, dqx, cos, sins, w16, x, g, dx1)


def _mem_bwd(mem, g_mem, wkv16, dkv):
    def body(m_ref, g_ref, w_ref, dkv_ref, dkv16_ref, dg_ref):
        dkv16 = dkv_ref[...].astype(BF16)
        dkv16_ref[...] = dkv16
        _, n, _ = _rms(m_ref[...], g_ref[...])
        dg = jnp.sum(_dot_nt(dkv16, w_ref[...]) * n, axis=0, keepdims=True)
        dg_ref[...] = jnp.broadcast_to(dg, dg_ref.shape)

    return pl.pallas_call(
        body, name="mem_bwd",
        out_shape=[jax.ShapeDtypeStruct((N_MEM, 2 * XW), BF16), jax.ShapeDtypeStruct((SUBLANES, D), F32)],
        compiler_params=pltpu.CompilerParams(vmem_limit_bytes=32 << 20))(mem, g_mem, wkv16, dkv)


N_CHIPS = N_DEV // 2


def _transpose_into(at, a_ref):
    kk = a_ref.shape[0]
    chunk = min(kk, 512)
    for c in range(kk // chunk):
        at[:, c * chunk:(c + 1) * chunk] = a_ref[c * chunk:(c + 1) * chunk, :].T


def _pair_scratch(block):
    return [pltpu.VMEM((N_CHIPS,) + block, BF16), pltpu.VMEM((N_CHIPS,) + block, BF16),
            pltpu.SemaphoreType.DMA((N_CHIPS,)), pltpu.SemaphoreType.DMA((N_CHIPS,))]


def _swap_with_sibling(p, stage, land, send, recv):
    x, y, c = lax.axis_index("x"), lax.axis_index("y"), lax.axis_index("c")
    return pltpu.make_async_remote_copy(src_ref=stage.at[p], dst_ref=land.at[p], send_sem=send.at[p],
                                        recv_sem=recv.at[p], device_id=(x, y, 1 - c), device_id_type=MESH)


def _wgrad_cols(place, a16, b16, blk, name, square_b=False, transpose_out=False, to_chips=False, small=()):
    kk, m = a16.shape
    aligned = blk % LANES == 0
    wide = blk if aligned else -(-(blk + LANES // 2) // LANES) * LANES
    block = (blk, m) if transpose_out else (m, blk)

    def chip_of(step, my_chip):
        return (my_chip + 1 + step) & (N_CHIPS - 1) if to_chips else step

    def body(pl_ref, a_ref, *refs):
        b_refs, refs = refs[:2 if aligned else 1], refs[2 if aligned else 1:]
        accs, refs = refs[:len(small)], refs[len(small):]
        (cs_ref, own_ref), refs = refs[:2], refs[2:]
        if to_chips:
            landed, refs = refs[0], refs[1:]
        if small:
            tot_ref, refs = refs[0], refs[1:]
        (at, stage, land, send, recv), refs = refs[:5], refs[5:]
        if not aligned:
            (win, wsem), refs = refs[:2], refs[2:]
        if small:
            start_small, finish_small = _small_reduce_steps(accs, tot_ref, *refs[-4:])
            refs = refs[:-4]
        step = pl.program_id(0)
        if small:
            pl.when(step == 0)(start_small)
        x, y, c = lax.axis_index("x"), lax.axis_index("y"), lax.axis_index("c")
        my_chip = 2 * x + y
        p = chip_of(step, my_chip)

        def fetch(at_step, mine):
            j = 2 * chip_of(at_step, my_chip) + (c if mine else 1 - c)
            first = pl.multiple_of(((j * blk) >> 7) << 7, LANES)
            slot = 2 * (at_step & 1) + mine
            return pltpu.make_async_copy(b_refs[0].at[:, pl.ds(first, wide)], win.at[slot], wsem.at[slot])

        @pl.when(step == 0)
        def _():
            if not aligned:
                fetch(0, 0).start()
                fetch(0, 1).start()
            _transpose_into(at, a_ref)

        if not aligned:
            @pl.when(step + 1 < N_CHIPS)
            def _():
                fetch(step + 1, 0).start()
                fetch(step + 1, 1).start()

        def partial(mine):
            if aligned:
                b = b_refs[mine][...]
                if square_b:
                    b = b * b
                acc = _dot(at[...], b)
            else:
                fetch(step, mine).wait()
                acc = _dot(at[...], win[2 * (step & 1) + mine])
                odd = c if mine else 1 - c
                acc = pltpu.roll(acc, jnp.where(odd == 0, 0, wide - LANES // 2), 1)[:, 0:blk]
            return acc.T if transpose_out else acc

        stage[p] = partial(0).astype(BF16)
        swap = _swap_with_sibling(p, stage, land, send, recv)
        swap.start()
        mine = partial(1)
        swap.wait()
        total = mine + land[p].astype(F32)
        cs_ref[0] = total.astype(BF16)

        @pl.when(p == my_chip)
        def _():
            own_ref[...] = total

        if to_chips:
            stage2, send2, recv2 = refs
            flipped = jnp.bitwise_xor(p, my_chip)
            k = jnp.where(flipped == 2, 0, jnp.where(flipped == 1, 1, 2))

            def to_owner(src, k_, px, py):
                return pltpu.make_async_remote_copy(src_ref=src, dst_ref=landed.at[k_], send_sem=send2.at[k_],
                                                    recv_sem=recv2.at[k_], device_id=(px, py, c), device_id_type=MESH)

            @pl.when(p != my_chip)
            def _():
                stage2[p] = total.astype(BF16)
                to_owner(stage2.at[p], k, p >> 1, p & 1).start()

            @pl.when(step == N_CHIPS - 1)
            def _():
                for k_ in range(N_CHIPS - 1):
                    to_owner(stage2.at[0], k_, x, y).wait()

        if small:
            pl.when(step == N_CHIPS - 1)(finish_small)

    def b_spec(mine):
        return pl.BlockSpec((kk, blk), lambda i, s: (0, 2 * chip_of(i, s[1]) + (s[0] if mine else 1 - s[0])))

    b_specs, b_args = ([b_spec(0), b_spec(1)], (b16, b16)) if aligned else ([ANY], (b16,))
    scratch = [pltpu.VMEM((m, kk), BF16)] + _pair_scratch(block)
    if not aligned:
        scratch += [pltpu.VMEM((4, kk, wide), BF16), pltpu.SemaphoreType.DMA((4,))]
    out_specs = [pl.BlockSpec((1,) + block, lambda i, s: (chip_of(i, s[1]), 0, 0)), pl.BlockSpec(block, lambda i, s: (0, 0))]
    out_shape = [jax.ShapeDtypeStruct((N_CHIPS,) + block, BF16), jax.ShapeDtypeStruct(block, F32)]
    if to_chips:
        out_specs.append(ANY)
        out_shape.append(jax.ShapeDtypeStruct((N_CHIPS - 1,) + block, BF16))
        scratch += [pltpu.VMEM((N_CHIPS,) + block, BF16), pltpu.SemaphoreType.DMA((N_CHIPS - 1,)),
                    pltpu.SemaphoreType.DMA((N_CHIPS - 1,))]
    small_specs = [pl.BlockSpec(a.shape, lambda i, s: (0, 0)) for a in small]
    if small:
        out_specs.append(pl.BlockSpec((PACK_ROWS, D), lambda i, s: (0, 0)))
        out_shape.append(jax.ShapeDtypeStruct((PACK_ROWS, D), F32))
        scratch += _small_reduce_scratch()
    return pl.pallas_call(
        body, name=name,
        grid_spec=pltpu.PrefetchScalarGridSpec(
            num_scalar_prefetch=1, grid=(N_CHIPS,),
            in_specs=[pl.BlockSpec((kk, m), lambda i, s: (0, 0), pipeline_mode=pl.Buffered(1))] + b_specs + small_specs,
            out_specs=out_specs, scratch_shapes=scratch),
        out_shape=out_shape, compiler_params=_cparams(56))(place, a16, *b_args, *small)


def _wgrad_rows(place, a16, b16, name):
    kk, m = a16.shape
    n = b16.shape[1]
    block = (m // N_DEV, n)

    def body(pl_ref, a_ref, b_ref, cs_ref, own_ref, at, acc, stage, land, send, recv):
        c = pl_ref[0]
        _transpose_into(at, a_ref)
        acc[...] = _dot(at[...], b_ref[...])

        def rows(owner):
            return pl.ds(pl.multiple_of(owner * block[0], block[0]), block[0])

        swaps = []
        for p in range(N_CHIPS):
            stage[p] = acc[rows(2 * p + 1 - c), :].astype(BF16)
            swaps.append(_swap_with_sibling(p, stage, land, send, recv))
            swaps[-1].start()
        for p in range(N_CHIPS):
            swaps[p].wait()
            total = acc[rows(2 * p + c), :] + land[p].astype(F32)
            cs_ref[p] = total.astype(BF16)

            @pl.when(p == pl_ref[1])
            def _():
                own_ref[...] = total

    vmem = pl.BlockSpec(memory_space=pltpu.VMEM)
    return pl.pallas_call(
        body, name=name,
        in_specs=[pl.BlockSpec(memory_space=pltpu.SMEM), vmem, vmem], out_specs=[vmem, vmem],
        out_shape=[jax.ShapeDtypeStruct((N_CHIPS,) + block, BF16), jax.ShapeDtypeStruct(block, F32)],
        scratch_shapes=[pltpu.VMEM((m, kk), BF16), pltpu.VMEM((m, n), F32)] + _pair_scratch(block),
        compiler_params=pltpu.CompilerParams(vmem_limit_bytes=56 << 20))(place, a16, b16)


def _adamw_math(w, g, m, v):
    m = ADAM_B1 * m + (1.0 - ADAM_B1) * g
    v = ADAM_B2 * v + (1.0 - ADAM_B2) * jnp.square(g)
    m_hat = m / (1.0 - ADAM_B1 ** ADAM_STEP)
    v_hat = v / (1.0 - ADAM_B2 ** ADAM_STEP)
    delta = -ADAM_LR * (m_hat / (jnp.sqrt(v_hat) + ADAM_EPS) + ADAM_WD * w)
    return delta, m, v


def _adamw_shards(updates, name, chip_sums=()):
    names, nu, ns = list(updates), len(updates), len(chip_sums)

    def body(*refs):
        ins, sum_refs = refs[:5 * nu], refs[5 * nu:5 * nu + ns]
        outs = refs[5 * nu + ns:9 * nu + ns]
        landed_refs, scratch = refs[9 * nu + ns:9 * nu + 2 * ns], refs[9 * nu + 2 * ns:]
        if ns:
            start_chips, finish_chips = _chips_steps(sum_refs, landed_refs, *scratch)
            start_chips()
        for i in range(nu):
            o_ref, r_ref, w_ref, m_ref, v_ref = ins[5 * i:5 * i + 5]
            g_out, d_out, m_out, v_out = outs[4 * i:4 * i + 4]
            g = o_ref[...] + r_ref[0].astype(F32) + r_ref[1].astype(F32) + r_ref[2].astype(F32)
            g_out[...] = g
            d_out[...], m_out[...], v_out[...] = _adamw_math(w_ref[...], g, m_ref[...], v_ref[...])
        if ns:
            finish_chips()

    vmem = pl.BlockSpec(memory_space=pltpu.VMEM)
    out = pl.pallas_call(
        body, name=name,
        in_specs=[vmem] * (5 * nu) + [ANY] * ns, out_specs=[vmem] * (4 * nu) + [ANY] * ns,
        out_shape=[jax.ShapeDtypeStruct(updates[n][2].shape, F32) for n in names for _ in range(4)]
        + _chips_shapes(chip_sums),
        scratch_shapes=_chips_scratch(ns) if ns else [],
        compiler_params=pltpu.CompilerParams(vmem_limit_bytes=56 << 20),
    )(*[a for n in names for a in updates[n]], *chip_sums)
    return {n: out[4 * i:4 * i + 4] for i, n in enumerate(names)}, list(out[4 * nu:])


def _place():
    x, y, c = lax.axis_index("x"), lax.axis_index("y"), lax.axis_index("c")
    chips = [(1 - x, y), (x, 1 - y), (1 - x, 1 - y)]
    return x, y, c, chips


def _gather_steps(ins, outs, send, recv, lsem):
    nt = len(ins)
    x, y, c, (xn, yn, diag) = _place()
    me, sib = (x, y, c), (x, y, 1 - c)

    def slot(t, px, py, pc):
        return outs[t].at[4 * px + 2 * py + pc]

    def copy(t, k, block, to, src=None):
        return pltpu.make_async_remote_copy(
            src_ref=slot(t, *block) if src is None else src, dst_ref=slot(t, *block),
            send_sem=send.at[t, k], recv_sem=recv.at[t, k], device_id=to, device_id_type=MESH)

    mine = [pltpu.make_async_copy(ins[t], slot(t, *me), lsem.at[t]) for t in range(nt)]
    first = [copy(t, k, me, to, src=ins[t]) for t in range(nt) for k, to in ((0, sib), (1, (*xn, c)), (2, (*yn, c)))]

    def start():
        for cp in mine + first:
            cp.start()

    def landed(k, chip, also_to=None):
        for t in range(nt):
            copy(t, k, (*chip, c), me).wait_recv()
            if also_to is not None:
                copy(t, 3, (*chip, c), (*also_to, c)).start()
            copy(t, 3 + k, (*chip, c), sib).start()

    def relay():
        @pl.when(c == 0)
        def _():
            landed(1, xn, also_to=yn)
            landed(2, yn)

        @pl.when(c == 1)
        def _():
            landed(2, yn, also_to=xn)
            landed(1, xn)

    def finish():
        landed(3, diag)
        for t in range(nt):
            copy(t, 0, sib, me).wait_recv()
            for k, chip in ((4, xn), (5, yn), (6, diag)):
                copy(t, k, (*chip, 1 - c), me).wait_recv()
            for k in range(7):
                copy(t, k, me, sib).wait_send()
        for cp in mine:
            cp.wait()

    return start, relay, finish


def _gather_scratch(nt):
    return [pltpu.SemaphoreType.DMA((nt, 7)), pltpu.SemaphoreType.DMA((nt, 7)), pltpu.SemaphoreType.DMA((nt,))]


def _gathered_shapes(shards):
    return [jax.ShapeDtypeStruct((N_DEV,) + s.shape, s.dtype) for s in shards]


def _call_with_gather(body, n_grid, shards, *, name, in_specs, out_specs, out_shape, scratch_shapes, vmem_mb, args):
    ng, n_in, n_out = len(shards), len(in_specs), len(out_specs)

    def wrapped(*refs):
        ins, shard_refs = refs[:n_in], refs[n_in:n_in + ng]
        outs = refs[n_in + ng:n_in + ng + n_out]
        whole_refs = refs[n_in + ng + n_out:n_in + 2 * ng + n_out]
        scratch = refs[n_in + 2 * ng + n_out:]
        if ng:
            start, relay, finish = _gather_steps(shard_refs, whole_refs, *scratch[len(scratch_shapes):])
            pl.when(pl.program_id(0) == 0)(start)
            pl.when(pl.program_id(0) == n_grid // 2)(relay)
        body(*ins, *outs, *scratch[:len(scratch_shapes)])
        if ng:
            pl.when(pl.program_id(0) == n_grid - 1)(finish)

    return pl.pallas_call(
        wrapped, grid=(n_grid,), name=name,
        in_specs=list(in_specs) + [ANY] * ng, out_specs=list(out_specs) + [ANY] * ng,
        out_shape=list(out_shape) + _gathered_shapes(shards),
        scratch_shapes=list(scratch_shapes) + (_gather_scratch(ng) if ng else []),
        compiler_params=_cparams(vmem_mb))(*args, *shards)


def _chips_steps(ins, outs, send, recv):
    _, _, c, chips = _place()
    copies = [pltpu.make_async_remote_copy(
        src_ref=ins[t].at[2 * px + py], dst_ref=outs[t].at[j], send_sem=send.at[t, j], recv_sem=recv.at[t, j],
        device_id=(px, py, c), device_id_type=MESH) for t in range(len(ins)) for j, (px, py) in enumerate(chips)]

    def start():
        for cp in copies:
            cp.start()

    def finish():
        for cp in copies:
            cp.wait()

    return start, finish


def _chips_scratch(nt):
    return [pltpu.SemaphoreType.DMA((nt, 3)), pltpu.SemaphoreType.DMA((nt, 3))]


def _chips_shapes(cs16s):
    return [jax.ShapeDtypeStruct((3,) + g.shape[1:], g.dtype) for g in cs16s]


SMALL = (("g_pre_mix", 0, 0, D), ("g_mem", 1, 0, D), ("g_post_mix", 2, 0, D), ("g_attn_out", 3, 0, AW),
         ("g_conv_out", 3, AW, CW), ("g_xattn_out", 3, AW + CW, XW), ("g_post_mlp", 4, 0, D), ("g_pre_mlp", 5, 0, D))
CONV_ROW = 8
PACK_ROWS = 16


LOSS_ROW = 15


def _small_reduce_steps(accs, tot_ref, pack, land, send, recv):
    acc_in, acc_mem, acc_mix, acc_mlp, acc_cw, acc_loss = accs
    x, y, c, _ = _place()
    me = 4 * x + 2 * y + c
    copies = []
    for k in range(1, N_DEV):
        kx, ky, kc = (k >> 2) & 1, (k >> 1) & 1, k & 1
        peer = (1 - x if kx else x, 1 - y if ky else y, 1 - c if kc else c)
        copies.append(pltpu.make_async_remote_copy(
            src_ref=pack, dst_ref=land.at[me], send_sem=send.at[k - 1], recv_sem=recv.at[k - 1],
            device_id=peer, device_id_type=MESH))

    def start():
        pack[...] = jnp.zeros_like(pack)
        pack[0:1, :] = acc_in[0:1, :]
        pack[1:2, :] = acc_mem[0:1, :]
        pack[2:4, :] = acc_mix[0:2, :]
        pack[4:6, :] = acc_mlp[0:2, :]
        pack[CONV_ROW:CONV_ROW + 3, 0:CW] = acc_cw[0:3, :]
        pack[LOSS_ROW:LOSS_ROW + 1, 0:LANES] = acc_loss[0:1, :]
        land[me] = pack[...]
        for cp in copies:
            cp.start()

    def finish():
        for cp in copies:
            cp.wait()
        tot = land[0]
        for s in range(1, N_DEV):
            tot = tot + land[s]
        tot_ref[...] = tot

    return start, finish


def _small_reduce_scratch():
    return [pltpu.VMEM((PACK_ROWS, D), F32), pltpu.VMEM((N_DEV, PACK_ROWS, D), F32),
            pltpu.SemaphoreType.DMA((N_DEV - 1,)), pltpu.SemaphoreType.DMA((N_DEV - 1,))]


def _small_update(tot, me, params):
    flat = [a for n, _, _, _ in SMALL for a in params[n]] + list(params["conv_w"])
    n_par = len(SMALL) + 1
    tap_cols = CW // N_DEV

    def body(*refs):
        me_ref, tot_ref = refs[0:2]
        ins = refs[2:2 + 3 * n_par]
        loss_out = refs[2 + 3 * n_par]
        outs = refs[3 + 3 * n_par:]
        tot = tot_ref[...]
        loss_out[...] = jnp.broadcast_to(tot[LOSS_ROW:LOSS_ROW + 1, 0:LANES], loss_out.shape)

        def update(i, g):
            w_ref, m_ref, v_ref = ins[3 * i:3 * i + 3]
            g_out, d_out, m_out, v_out = outs[4 * i:4 * i + 4]
            g_out[...] = g
            d_out[...], m_out[...], v_out[...] = _adamw_math(w_ref[...], g, m_ref[...], v_ref[...])

        for i, (_, row, lane0, width) in enumerate(SMALL):
            update(i, tot[row:row + 1, lane0:lane0 + width])
        me = me_ref[0]
        taps = pltpu.roll(tot[CONV_ROW:CONV_ROW + SUBLANES, 0:CW], jnp.where(me == 0, 0, CW - me * tap_cols), 1)
        update(n_par - 1, taps[0:3, 0:tap_cols])

    shapes = [jax.ShapeDtypeStruct(params[n][0].shape, F32) for n, _, _, _ in SMALL] + [
        jax.ShapeDtypeStruct(params["conv_w"][0].shape, F32)]
    vmem = pl.BlockSpec(memory_space=pltpu.VMEM)
    loss, *out = pl.pallas_call(
        body, name="small_update",
        in_specs=[pl.BlockSpec(memory_space=pltpu.SMEM)] + [vmem] * (1 + 3 * n_par),
        out_shape=[jax.ShapeDtypeStruct((SUBLANES, LANES), F32)] + [s for s in shapes for _ in range(4)],
    )(me, tot, *flat)
    names = [n for n, _, _, _ in SMALL] + ["conv_w"]
    return loss[0, 0], {n: out[4 * i:4 * i + 4] for i, n in enumerate(names)}


def _local_step(x, mem, pos, gains, shards, tgt, place):
    half = HEAD // 2
    inv_freq = jnp.float32(ROPE_THETA) ** (-(jnp.arange(half, dtype=F32) * 2.0 / HEAD))
    invf = jnp.tile(inv_freq, LANES // half)[None, :]
    sgn = jnp.tile(jnp.concatenate([-jnp.ones((half,), F32), jnp.ones((half,), F32)]), LANES // HEAD)[None, :]
    cos, sins, win8 = _rope_table(pos.astype(F32).reshape(S, 1), invf, sgn, [shards["w_in"]])
    wdn_left, wdn_right = shards["w_down"][:, 0:D // 2], shards["w_down"][:, D // 2:]
    q, kvp, bcu, qx16, h16, win16, wout8, wkv8, conv8, wdn8_right = _in_proj(
        x, gains["g_pre_mix"], win8, cos, sins, [shards["w_out"], shards["w_mem_kv"], shards["conv_w"], wdn_right])
    wout16, wkv16 = wout8.reshape(D, D), wkv8.reshape(D, 2 * XW)
    cw_full = conv8[:, 0:3, 0:CW // N_DEV].transpose(1, 0, 2).reshape(3, CW)
    cw8 = jnp.zeros((SUBLANES, CW), F32).at[0:3].set(cw_full)
    y_attn, ltot, wup8, wdn8_left = _attn_fwd(q, kvp, [shards["w_up"], wdn_left])
    wdn_halves = (wdn8_left.reshape(FF, D // 2), wdn8_right.reshape(FF, D // 2))
    memn16, kv16 = _mem_fwd(mem, gains["g_mem"], wkv16)
    ypre, y16, y2, x1 = _mix_out(y_attn, bcu, qx16, kv16, cw8, gains["g_attn_out"], gains["g_conv_out"],
                                 gains["g_xattn_out"], gains["g_post_mix"], wout16, x, [])
    a16, du16, h2_16, df2_16, dx1, loss8, dg_mlp = _mlp(
        x1, tgt, gains["g_pre_mlp"], gains["g_post_mlp"], wup8, wdn_halves)

    sums = {"w_up": _wgrad_cols(place, h2_16, du16, FF_BLK, "wgrad_up"),
            "w_down": _wgrad_cols(place, df2_16, a16, FF_BLK, "wgrad_down", square_b=True, transpose_out=True)}

    head_id = jnp.arange(AW, dtype=jnp.int32) // HEAD
    head_ones = (head_id[:, None] == head_id[None, :]).astype(BF16)
    dy2_16, qdo, ld, dbcu, dqx, dgs, dcw, dkv = _mix_out_bwd(
        dx1, y2, ypre, ltot, head_ones, q, bcu, qx16, kv16, cw8, gains["g_post_mix"], gains["g_attn_out"],
        gains["g_conv_out"], gains["g_xattn_out"], wout16)
    dkv16, dg_mem = _mem_bwd(mem, gains["g_mem"], wkv16, dkv)
    sums["w_mem_kv"] = _wgrad_rows(place, memn16, dkv16, "wgrad_mem_kv")
    sums["w_out"] = _wgrad_rows(place, y16, dy2_16, "wgrad_out")
    out = _attn_bwd(qdo, kvp, ld, [s[0] for s in sums.values()])
    dqkv, landed = out[:9], out[9:]
    reduced = {n: (s[1], landed[t]) for t, (n, s) in enumerate(sums.items())}
    dproj16, grad_x, dg_in = _in_proj_bwd(dqkv, dbcu, dqx, cos, sins, win16, x, gains["g_pre_mix"], dx1)

    _, in_own, in_landed, small_tot = _wgrad_cols(place, h16, dproj16, PW // N_DEV, "wgrad_in", to_chips=True,
                                                  small=(dg_in, dg_mem, dgs, dg_mlp, dcw, loss8))
    reduced["w_in"] = (in_own, in_landed)
    return grad_x, reduced, small_tot


BIG = ("w_in", "w_mem_kv", "w_out", "w_up", "w_down")
ORDER = ("g_pre_mix", "g_mem", "w_in", "w_mem_kv", "conv_w", "g_attn_out", "g_conv_out", "g_xattn_out", "w_out",
         "g_post_mix", "g_pre_mlp", "w_up", "w_down", "g_post_mlp")


def kernel(x, mem, positions, g_pre_mix, g_mem, w_in, w_mem_kv, conv_w, g_attn_out, g_conv_out, g_xattn_out, w_out, g_post_mix, g_pre_mlp, w_up, w_down, g_post_mlp, loss_target, m_g_pre_mix, m_g_mem, m_w_in, m_w_mem_kv, m_conv_w, m_g_attn_out, m_g_conv_out, m_g_xattn_out, m_w_out, m_g_post_mix, m_g_pre_mlp, m_w_up, m_w_down, m_g_post_mlp, v_g_pre_mix, v_g_mem, v_w_in, v_w_mem_kv, v_conv_w, v_g_attn_out, v_g_conv_out, v_g_xattn_out, v_w_out, v_g_post_mix, v_g_pre_mlp, v_w_up, v_w_down, v_g_post_mlp):
    w = dict(g_pre_mix=g_pre_mix, g_mem=g_mem, w_in=w_in, w_mem_kv=w_mem_kv, conv_w=conv_w, g_attn_out=g_attn_out,
             g_conv_out=g_conv_out, g_xattn_out=g_xattn_out, w_out=w_out, g_post_mix=g_post_mix, g_pre_mlp=g_pre_mlp,
             w_up=w_up, w_down=w_down, g_post_mlp=g_post_mlp)
    mo = dict(g_pre_mix=m_g_pre_mix, g_mem=m_g_mem, w_in=m_w_in, w_mem_kv=m_w_mem_kv, conv_w=m_conv_w,
              g_attn_out=m_g_attn_out, g_conv_out=m_g_conv_out, g_xattn_out=m_g_xattn_out, w_out=m_w_out,
              g_post_mix=m_g_post_mix, g_pre_mlp=m_g_pre_mlp, w_up=m_w_up, w_down=m_w_down, g_post_mlp=m_g_post_mlp)
    vo = dict(g_pre_mix=v_g_pre_mix, g_mem=v_g_mem, w_in=v_w_in, w_mem_kv=v_w_mem_kv, conv_w=v_conv_w,
              g_attn_out=v_g_attn_out, g_conv_out=v_g_conv_out, g_xattn_out=v_g_xattn_out, w_out=v_w_out,
              g_post_mix=v_g_post_mix, g_pre_mlp=v_g_pre_mlp, w_up=v_w_up, w_down=v_w_down, g_post_mlp=v_g_post_mlp)

    xi, yi, ci = lax.axis_index("x"), lax.axis_index("y"), lax.axis_index("c")
    me = 4 * xi + 2 * yi + ci
    place = jnp.stack([ci, 2 * xi + yi]).astype(jnp.int32)

    shards = {n: w[n][0].astype(BF16) for n in BIG}
    shards["conv_w"] = jnp.zeros((SUBLANES, LANES), F32).at[0:3, 0:CW // N_DEV].set(conv_w[0])

    gains = {n: w[n] for n, _, _, _ in SMALL}
    grad_x, reduced, small_tot = _local_step(x[0], mem[0], positions[0], gains, shards, loss_target[0], place)

    updated = {}
    for group in (("w_up", "w_down"), ("w_in", "w_out", "w_mem_kv")):
        updated.update(_adamw_shards({n: (*reduced[n], w[n][0], mo[n][0], vo[n][0]) for n in group},
                                     "adamw_" + "_".join(group))[0])
    grad, delta, new_m, new_v = {}, {}, {}, {}
    for n, (g, d_, m_, v_) in updated.items():
        grad[n], delta[n], new_m[n], new_v[n] = g[None], d_[None], m_[None], v_[None]

    params = {n: (w[n], mo[n], vo[n]) for n, _, _, _ in SMALL}
    params["conv_w"] = (w["conv_w"][0], mo["conv_w"][0], vo["conv_w"][0])
    loss, small = _small_update(small_tot, me.reshape(1).astype(jnp.int32), params)
    for n, (g, d_, m_, v_) in small.items():
        lead = (lambda a: a[None]) if n == "conv_w" else (lambda a: a)
        grad[n], delta[n], new_m[n], new_v[n] = lead(g), lead(d_), lead(m_), lead(v_)

    return (loss, grad_x[None], *[grad[n] for n in ORDER], *[delta[n] for n in ORDER],
            *[new_m[n] for n in ORDER], *[new_v[n] for n in ORDER])
```

```python
import jax
import jax.numpy as jnp
from jax import lax
from jax.experimental import pallas as pl
from jax.experimental.pallas import tpu as pltpu

F32, BF16 = jnp.float32, jnp.bfloat16
MESH = pl.DeviceIdType.MESH
ANY = pl.BlockSpec(memory_space=pl.ANY)

N_DEV = 8
D = 1024
S = 4096
N_MEM = 256
HEAD = 64
AW, CW, XW = 512, 256, 256
PW = 3 * AW + 3 * CW + XW
FF = 4096
FF_BLK = FF // N_DEV
EPS = 1e-6
NEG = -1e30
SCALE = HEAD ** -0.5
ROPE_THETA = 10000.0
LANES = 128
SUBLANES = 8

ADAM_LR, ADAM_B1, ADAM_B2, ADAM_EPS, ADAM_WD, ADAM_STEP = 0.001, 0.9, 0.999, 1e-08, 0.01, 10

TQ = 512
TQ_MLP = 512
NT = S // TQ


def _cparams(vmem_mb, n_grid=1):
    return pltpu.CompilerParams(dimension_semantics=("arbitrary",) * n_grid, vmem_limit_bytes=vmem_mb << 20)


def _const(shape):
    nd = len(shape)
    return pl.BlockSpec(shape, lambda *_: (0,) * nd, pipeline_mode=pl.Buffered(1))


def _acc(shape):
    nd = len(shape)
    return pl.BlockSpec(shape, lambda *_: (0,) * nd)


def _dot(a, b):
    return jnp.dot(a, b, preferred_element_type=F32)


def _dot_nt(a, b):
    return lax.dot_general(a, b, (((1,), (1,)), ((), ())), preferred_element_type=F32)


def _dot_tn(a, b):
    return lax.dot_general(a, b, (((0,), (0,)), ((), ())), preferred_element_type=F32)


def _rms(x, g):
    r = lax.rsqrt(jnp.mean(x * x, axis=-1, keepdims=True) + EPS)
    n = x * r
    return n * g, n, r


def _rms_bwd(dy, n, r, g):
    dn = dy * g
    dx = r * (dn - n * jnp.mean(dn * n, axis=-1, keepdims=True))
    return dx, jnp.sum(dy * n, axis=0, keepdims=True)


def _rot_half(t):
    lane = lax.broadcasted_iota(jnp.int32, t.shape, 1)
    n = t.shape[1]
    return jnp.where((lane % HEAD) < HEAD // 2, pltpu.roll(t, n - HEAD // 2, 1), pltpu.roll(t, HEAD // 2, 1))


def _rope_table(pos_col, invf, sgn, shards):
    def body(p_ref, f_ref, s_ref, c_out, s_out):
        ang = p_ref[...] * f_ref[...]
        c_out[...] = jnp.cos(ang)
        s_out[...] = jnp.sin(ang) * s_ref[...]

    tile = pl.BlockSpec((TQ, LANES), lambda i: (i, 0))
    return _call_with_gather(
        body, NT, shards, name="rope_table",
        in_specs=[pl.BlockSpec((TQ, 1), lambda i: (i, 0)), _const((1, LANES)), _const((1, LANES))],
        out_specs=[tile, tile], out_shape=[jax.ShapeDtypeStruct((S, LANES), F32)] * 2,
        scratch_shapes=[], vmem_mb=32, args=(pos_col, invf, sgn))


def _all_heads(t):
    return jnp.tile(t, (1, AW // LANES))


def _mem_fwd(mem, g_mem, wkv16):
    def body(m_ref, g_ref, w_ref, n16_ref, kv_ref):
        y, _, _ = _rms(m_ref[...], g_ref[...])
        y16 = y.astype(BF16)
        n16_ref[...] = y16
        kv_ref[...] = _dot(y16, w_ref[...]).astype(BF16)

    return pl.pallas_call(
        body, name="mem_fwd",
        out_shape=[jax.ShapeDtypeStruct((N_MEM, D), BF16), jax.ShapeDtypeStruct((N_MEM, 2 * XW), BF16)],
        compiler_params=pltpu.CompilerParams(vmem_limit_bytes=32 << 20))(mem, g_mem, wkv16)


def _in_proj(x, g, w8, cos, sins, shards):
    blk = PW // N_DEV

    def body(x_ref, g_ref, w8_ref, c_ref, s_ref, q_ref, kv_ref, bcu_ref, qx_ref, h_ref, w_out, w_ref):
        @pl.when(pl.program_id(0) == 0)
        def _():
            for j in range(N_DEV):
                w_ref[:, j * blk:(j + 1) * blk] = w8_ref[j]
            w_out[...] = w_ref[...]

        y, _, _ = _rms(x_ref[...], g_ref[...])
        h = y.astype(BF16)
        h_ref[...] = h
        proj = _dot(h, w_ref[...])
        cos, sn = _all_heads(c_ref[...]), _all_heads(s_ref[...])
        q, k = proj[:, 0:AW], proj[:, AW:2 * AW]
        q_ref[...] = (q * cos + _rot_half(q) * sn) * SCALE
        kv_ref[...] = _pack_pair(k * cos + _rot_half(k) * sn, proj[:, 2 * AW:3 * AW])
        bcu_ref[...] = proj[:, 3 * AW:3 * AW + 3 * CW]
        qx_ref[...] = (proj[:, 3 * AW + 3 * CW:] * SCALE).astype(BF16)

    def tile(w):
        return pl.BlockSpec((TQ, w), lambda i: (i, 0))

    return _call_with_gather(
        body, NT, shards, name="in_proj",
        in_specs=[tile(D), _const((1, D)), _const((N_DEV, D, blk)), tile(LANES), tile(LANES)],
        out_specs=[tile(AW), tile(AW), tile(3 * CW), tile(XW), tile(D), _acc((D, PW))],
        out_shape=[jax.ShapeDtypeStruct((S, AW), F32)] * 2 + [
            jax.ShapeDtypeStruct((S, 3 * CW), F32), jax.ShapeDtypeStruct((S, XW), BF16),
            jax.ShapeDtypeStruct((S, D), BF16), jax.ShapeDtypeStruct((D, PW), BF16)],
        scratch_shapes=[pltpu.VMEM((D, PW), BF16)], vmem_mb=56, args=(x, g, w8, cos, sins))


ATTN_PLANS = (("p1", 1, 128, 32), ("p4", 8, 64, 8), ("p16", 16, 128, 2))
PAD = 128
WIN = 256


ATTN_UNROLL = 16


def _fill_bias(tab, qblk, partner):
    qi = lax.broadcasted_iota(jnp.int32, (2 * qblk, WIN), 0) & (qblk - 1)
    kj = lax.broadcasted_iota(jnp.int32, (2 * qblk, WIN), 1)
    piece = kj >> (qblk.bit_length() - 1)
    kk = kj & (qblk - 1)
    prev = (piece & 1) == 0
    of_partner = piece >= 2
    for first in (0, 1):
        for par in (0, 1):
            lo = jnp.where(prev, (qblk if first else qi) + jnp.where(of_partner, par, 0), 0)
            hi = jnp.where(prev, qblk, qi + jnp.where(of_partner, par - 1, 0))
            tab[2 * first + par] = jnp.where((kk >= lo) & (kk <= hi), 0.0, NEG).astype(F32)


def _block_rows(g, qblk, nbc, partner):
    own = pl.ds(pl.multiple_of(PAD + g * qblk, qblk), qblk)
    first = ((g & (nbc - 1)) == 0).astype(jnp.int32)
    if partner:
        gp = jnp.bitwise_xor(g, 4 * nbc)
        wins = (pl.ds(pl.multiple_of(PAD + (g - 1) * qblk, qblk), 2 * qblk),
                pl.ds(pl.multiple_of(PAD + (gp - 1) * qblk, qblk), 2 * qblk))
        return own, wins, 2 * first + ((g >> ((4 * nbc).bit_length() - 1)) & 1)
    return own, (pl.ds(pl.multiple_of(PAD + (g - 1) * qblk, qblk), 2 * qblk),), 2 * first


def _pack_pair(lo, hi):
    lo_bits = lax.bitcast_convert_type(lo.astype(BF16).astype(F32), jnp.uint32) >> 16
    hi_bits = lax.bitcast_convert_type(hi.astype(BF16).astype(F32), jnp.uint32) & jnp.uint32(0xFFFF0000)
    return lax.bitcast_convert_type(hi_bits | lo_bits, F32)


def _unpack_pair(c):
    bits = lax.bitcast_convert_type(c, jnp.uint32)
    lo = lax.bitcast_convert_type(bits << 16, F32).astype(BF16)
    hi = lax.bitcast_convert_type(bits & jnp.uint32(0xFFFF0000), F32).astype(BF16)
    return lo, hi


def _window(ref, wins):
    parts = [ref[w, :] for w in wins]
    return parts[0] if len(parts) == 1 else jnp.concatenate(parts, axis=0)


def _stack_heads(t, lane):
    zero = jnp.zeros_like(t)
    return jnp.concatenate([jnp.where(lane < HEAD, t, zero), jnp.where(lane >= HEAD, t, zero)], axis=0)


def _unstack_heads(t2, lane):
    half = t2.shape[0] // 2
    return jnp.where(lane < HEAD, t2[0:half, :], t2[half:, :])


def _lanes_of(step):
    return pl.ds(pl.multiple_of(step * LANES, LANES), LANES)


def _whole_wait(buf, sem):
    whole = buf.at[pl.ds(PAD, S), :]
    return pltpu.make_async_copy(whole, whole, sem)


def _whole_waits(bufs, sems):
    return [_whole_wait(buf, sems.at[i]) for i, buf in enumerate(bufs)]


def _class_gather(views, bufs, sems, lanes):
    copies = []
    for i, (view, buf) in enumerate(zip(views, bufs)):
        if view.ndim == 2:
            copies.append(pltpu.make_async_copy(view.at[:, lanes], buf.at[pl.ds(PAD, S), :], sems.at[i]))
        else:
            per, n_cls = view.shape[0], view.shape[1]
            copies += [pltpu.make_async_copy(view.at[:, c, lanes], buf.at[pl.ds(PAD + c * per, per), :], sems.at[i])
                       for c in range(n_cls)]
    return copies


def _class_scatter(bufs, dsts, sems, lanes):
    copies = []
    for i, (buf, dst) in enumerate(zip(bufs, dsts)):
        if dst.ndim == 2:
            copies.append(pltpu.make_async_copy(buf.at[pl.ds(PAD, S), :], dst.at[:, lanes], sems.at[i]))
            continue
        per, n_cls = dst.shape[0], dst.shape[1]
        copies += [pltpu.make_async_copy(buf.at[pl.ds(PAD + c * per, per), :], dst.at[:, c, lanes], sems.at[i])
                   for c in range(n_cls)]
    return copies


def _start(copies):
    for cp in copies:
        cp.start()


def _wait(waits):
    for w in waits:
        w.wait()


def _attn_fwd(q, kvp, shards=()):
    views = [[a] + [a.reshape(S // n, n, AW) for _, n, _, _ in ATTN_PLANS[1:]] for a in (q, kvp)]
    flat = [views[a][p] for p in range(3) for a in range(2)]
    ng = len(shards)
    n_grid = AW // LANES

    def body(*refs):
        hbm = [refs[2 * p:2 * p + 2] for p in range(3)]
        refs = refs[6:]
        shard_refs, refs = refs[:ng], refs[ng:]
        y_ref, lt_ref = refs[0:2]
        whole_refs, refs = refs[2:2 + ng], refs[2 + ng:]
        bufs = [refs[2 * p:2 * p + 2] for p in range(3)]
        oc4, lc4, oc16, lc16, tab128, tab4, sem_in = refs[6:13]
        step = pl.program_id(0)
        if ng:
            start_gather, relay_gather, finish_gather = _gather_steps(shard_refs, whole_refs, *refs[13:])
            pl.when(step == 0)(start_gather)
            pl.when(step == n_grid // 2)(relay_gather)
        now = [_class_gather(hbm[p], bufs[p], sem_in.at[p], _lanes_of(step)) for p in range(3)]
        nxt = [_class_gather(hbm[p], bufs[p], sem_in.at[p], _lanes_of(step + 1)) for p in range(3)]

        @pl.when(step == 0)
        def _():
            for p in range(3):
                _start(now[p])
                for b in bufs[p]:
                    b[0:PAD, :] = jnp.zeros((PAD, LANES), F32)
            _fill_bias(tab128, 128, False)
            _fill_bias(tab4, 64, True)

        def prefetch(p):
            pl.when(step + 1 < n_grid)(lambda: _start(nxt[p]))

        lane = lax.broadcasted_iota(jnp.int32, (1, LANES), 1)
        ones = jnp.ones((WIN, LANES), BF16)

        def run(plan, bq, bkv, tab, o_dst, l_dst, dst_pad):
            _, n_cls, qblk, nbc = plan
            partner = n_cls == 8

            def block(g, carry):
                own, wins, mask = _block_rows(g, qblk, nbc, partner)
                q2 = _stack_heads(bq[own, :].astype(BF16), lane)
                kw, vwin = _unpack_pair(_window(bkv, wins))
                vw = jnp.concatenate([vwin, ones], axis=1)
                s = _dot_nt(q2, kw) + tab[mask]
                m = jnp.max(s, axis=1, keepdims=True)
                oe = _dot(jnp.exp(s - m).astype(BF16), vw)
                den = oe[:, LANES:]
                dst = pl.ds(pl.multiple_of(dst_pad + g * qblk, qblk), qblk)
                o_dst[dst, :] = _unstack_heads(oe[:, 0:LANES] / den, lane)
                l_dst[dst, :] = _unstack_heads(m + jnp.log(den), lane)
                return carry
            lax.fori_loop(0, n_cls * nbc, block, 0, unroll=ATTN_UNROLL)

        _wait(_whole_waits(bufs[0], sem_in.at[0]))
        run(ATTN_PLANS[0], *bufs[0], tab128, y_ref, lt_ref, 0)
        prefetch(0)
        _wait(_whole_waits(bufs[1], sem_in.at[1]))
        run(ATTN_PLANS[1], *bufs[1], tab4, oc4, lc4, PAD)
        prefetch(1)
        _wait(_whole_waits(bufs[2], sem_in.at[2]))
        run(ATTN_PLANS[2], *bufs[2], tab128, oc16, lc16, PAD)
        prefetch(2)

        n_rows = 64

        def token_order(buf, t, n_cls):
            per = S // n_cls
            first = PAD + t * (n_rows // n_cls)
            return jnp.concatenate([buf[pl.ds(first + jj, n_cls, stride=per), :] for jj in range(n_rows // n_cls)],
                                   axis=0)

        def combine(t, carry):
            rows = pl.ds(pl.multiple_of(t * n_rows, n_rows), n_rows)
            l0, l1, l2 = lt_ref[rows, :], token_order(lc4, t, 8), token_order(lc16, t, 16)
            lm = jnp.maximum(jnp.maximum(l0, l1), l2)
            e0, e1, e2 = jnp.exp(l0 - lm), jnp.exp(l1 - lm), jnp.exp(l2 - lm)
            den = e0 + e1 + e2
            y_ref[rows, :] = (e0 * y_ref[rows, :] + e1 * token_order(oc4, t, 8)
                              + e2 * token_order(oc16, t, 16)) / den
            lt_ref[rows, :] = lm + jnp.log(den)
            return carry
        lax.fori_loop(0, S // n_rows, combine, 0, unroll=2)

        if ng:
            pl.when(step == n_grid - 1)(finish_gather)

    col = pl.BlockSpec((S, LANES), lambda h: (0, h))
    padded = pltpu.VMEM((PAD + S, LANES), F32)
    return pl.pallas_call(
        body, grid=(n_grid,), name="attn_fwd",
        in_specs=[ANY] * (6 + ng), out_specs=[col, col] + [ANY] * ng,
        out_shape=[jax.ShapeDtypeStruct((S, AW), F32)] * 2 + _gathered_shapes(shards),
        scratch_shapes=[padded] * 10 + [
            pltpu.VMEM((4, 256, WIN), F32), pltpu.VMEM((4, 128, WIN), F32), pltpu.SemaphoreType.DMA((3, 2))]
        + (_gather_scratch(ng) if ng else []),
        compiler_params=_cparams(56))(*flat, *shards)


def _conv_taps(z, zprev, row):
    z1 = jnp.where(row == 0, zprev[7:8, :], pltpu.roll(z, 1, 0))
    z2 = jnp.where(row == 0, zprev[6:7, :], jnp.where(row == 1, zprev[7:8, :], pltpu.roll(z, 2, 0)))
    return z1, z2


def _xattn_scores(qm, km):
    s = _dot_nt(qm, km)
    m = jnp.max(s, axis=1, keepdims=True)
    e = jnp.exp(s - m)
    return e, jnp.sum(e, axis=1, keepdims=True)


def _mix_out(y_attn, bcu, qx16, kv16, cw8, g_attn, g_conv, g_x, g_post, wout16, x, shards):
    def body(ya_ref, bcu_ref, halo_ref, qx_ref, kv_ref, cw_ref, ga_ref, gc_ref, gx_ref, gp_ref, w_ref, x_ref,
             ypre_ref, y16_ref, y2_ref, x1_ref):
        i = pl.program_id(0)
        bcu = bcu_ref[...]
        b, c, u = bcu[:, 0:CW], bcu[:, CW:2 * CW], bcu[:, 2 * CW:]
        z = c * u
        halo = halo_ref[...]
        zprev = jnp.where(i > 0, halo[:, CW:2 * CW] * halo[:, 2 * CW:], 0.0)
        row = lax.broadcasted_iota(jnp.int32, z.shape, 0)
        z1, z2 = _conv_taps(z, zprev, row)
        cw = cw_ref[...]
        y_conv = b * (z2 * cw[0:1, :] + z1 * cw[1:2, :] + z * cw[2:3, :])

        qx = qx_ref[...]
        kv = kv_ref[...]
        km, vm = kv[:, 0:XW], kv[:, XW:]
        lane = lax.broadcasted_iota(jnp.int32, qx.shape, 1)
        y_x = jnp.zeros(qx.shape, F32)
        for h in range(XW // HEAD):
            hm = (lane >= h * HEAD) & (lane < (h + 1) * HEAD)
            e, l = _xattn_scores(jnp.where(hm, qx, jnp.zeros_like(qx)), km)
            y_x = jnp.where(hm, _dot(e.astype(BF16), vm) / l, y_x)

        y_attn = ya_ref[...]
        ypre_ref[:, 0:AW] = y_attn
        ypre_ref[:, AW:AW + CW] = y_conv
        ypre_ref[:, AW + CW:] = y_x
        y = jnp.concatenate([_rms(y_attn, ga_ref[...])[0], _rms(y_conv, gc_ref[...])[0],
                             _rms(y_x, gx_ref[...])[0]], axis=1).astype(BF16)
        y16_ref[...] = y
        y2 = _dot(y, w_ref[...])
        y2_ref[...] = y2
        x1_ref[...] = x_ref[...] + _rms(y2, gp_ref[...])[0]

    def tile(w):
        return pl.BlockSpec((TQ, w), lambda i: (i, 0))

    halo = pl.BlockSpec((SUBLANES, 3 * CW), lambda i: (jnp.maximum(i * (TQ // SUBLANES) - 1, 0), 0))
    return _call_with_gather(
        body, NT, shards, name="mix_out",
        in_specs=[tile(AW), tile(3 * CW), halo, tile(XW), _const((N_MEM, 2 * XW)), _const((SUBLANES, CW)),
                  _const((1, AW)), _const((1, CW)), _const((1, XW)), _const((1, D)), _const((D, D)), tile(D)],
        out_specs=[tile(D), tile(D), tile(D), tile(D)],
        out_shape=[jax.ShapeDtypeStruct((S, D), F32), jax.ShapeDtypeStruct((S, D), BF16),
                   jax.ShapeDtypeStruct((S, D), F32), jax.ShapeDtypeStruct((S, D), F32)],
        scratch_shapes=[], vmem_mb=56,
        args=(y_attn, bcu, bcu, qx16, kv16, cw8, g_attn, g_conv, g_x, g_post, wout16, x))


def _mlp(x1, tgt, g_pre, g_post, wup8, wdn_halves):
    tq = TQ_MLP
    half = D // 2

    def body(x1_ref, t_ref, g1_ref, g2_ref, wu_ref, wda_ref, wdb_ref,
             a16_ref, du_ref, h2_ref, df2_ref, dx1_ref, loss_ref, dg_ref):
        @pl.when(pl.program_id(0) == 0)
        def _():
            loss_ref[...] = jnp.zeros_like(loss_ref)
            dg_ref[...] = jnp.zeros_like(dg_ref)

        x1 = x1_ref[...]
        g1, g2 = g1_ref[...], g2_ref[...]
        y1, n1, r1 = _rms(x1, g1)
        h2 = y1.astype(BF16)
        h2_ref[...] = h2
        f2a = jnp.zeros((tq, half), F32)
        f2b = jnp.zeros((tq, half), F32)
        for j in range(N_DEV):
            cols = slice(j * FF_BLK, (j + 1) * FF_BLK)
            a = jnp.maximum(_dot(h2, wu_ref[j]), 0.0)
            a16_ref[:, cols] = a.astype(BF16)
            f = (a * a).astype(BF16)
            f2a = f2a + _dot(f, wda_ref[cols, :])
            f2b = f2b + _dot(f, wdb_ref[cols, :])
        f2 = jnp.concatenate([f2a, f2b], axis=1)
        y2, n2, r2 = _rms(f2, g2)
        e = x1 + y2 - t_ref[...]
        sq = jnp.sum(jnp.sum(e * e, axis=1, keepdims=True), axis=0, keepdims=True)
        loss_ref[...] += jnp.broadcast_to(sq * (0.5 / D), loss_ref.shape)
        dout = e * (1.0 / D)
        df2, dg2 = _rms_bwd(dout, n2, r2, g2)
        df2_16 = df2.astype(BF16)
        df2_ref[...] = df2_16
        dh2 = jnp.zeros((tq, D), F32)
        for j in range(N_DEV):
            cols = slice(j * FF_BLK, (j + 1) * FF_BLK)
            df = _dot_nt(df2_16[:, 0:half], wda_ref[cols, :]) + _dot_nt(df2_16[:, half:], wdb_ref[cols, :])
            du = (df * (2.0 * a16_ref[:, cols].astype(F32))).astype(BF16)
            du_ref[:, cols] = du
            dh2 = dh2 + _dot_nt(du, wu_ref[j])
        dx, dg1 = _rms_bwd(dh2, n1, r1, g1)
        dx1_ref[...] = dout + dx
        dg_ref[0:1, :] += dg2
        dg_ref[1:2, :] += dg1

    def tile(w):
        return pl.BlockSpec((tq, w), lambda i: (i, 0))

    return pl.pallas_call(
        body, grid=(S // tq,), name="mlp",
        in_specs=[tile(D), tile(D), _const((1, D)), _const((1, D)), _const((N_DEV, D, FF_BLK)), _const((FF, half)), _const((FF, half))],
        out_specs=[tile(FF), tile(FF), tile(D), tile(D), tile(D), _acc((SUBLANES, LANES)), _acc((SUBLANES, D))],
        out_shape=[jax.ShapeDtypeStruct((S, FF), BF16), jax.ShapeDtypeStruct((S, FF), BF16),
                   jax.ShapeDtypeStruct((S, D), BF16), jax.ShapeDtypeStruct((S, D), BF16),
                   jax.ShapeDtypeStruct((S, D), F32), jax.ShapeDtypeStruct((SUBLANES, LANES), F32),
                   jax.ShapeDtypeStruct((SUBLANES, D), F32)],
        compiler_params=_cparams(56))(x1, tgt, g_pre, g_post, wup8, *wdn_halves)


def _mix_out_bwd(dx1, y2, ypre, ltot, head_ones, q, bcu, qx16, kv16, cw8, g_post, g_attn, g_conv, g_x, wout16):
    def body(dx1_ref, y2_ref, ypre_ref, lt_ref, e_ref, q_ref, bcu_ref, halo_ref, qx_ref, kv_ref, cw_ref, gp_ref,
             ga_ref, gc_ref, gx_ref, w_ref, dy2_ref, qdo_ref, ld_ref, dbcu_ref, dqx_ref, dgs_ref, dcw_ref, dkv_ref,
             carry):
        i = pl.program_id(0)

        @pl.when(i == 0)
        def _():
            dgs_ref[...] = jnp.zeros_like(dgs_ref)
            dcw_ref[...] = jnp.zeros_like(dcw_ref)
            dkv_ref[...] = jnp.zeros_like(dkv_ref)
            carry[...] = jnp.zeros_like(carry)

        gp = gp_ref[...]
        _, n, r = _rms(y2_ref[...], gp)
        dy2, dgp = _rms_bwd(dx1_ref[...], n, r, gp)
        dy2_16 = dy2.astype(BF16)
        dy2_ref[...] = dy2_16
        dy = _dot_nt(dy2_16, w_ref[...])

        ypre = ypre_ref[...]
        ga, gc, gx = ga_ref[...], gc_ref[...], gx_ref[...]
        _, na, ra = _rms(ypre[:, 0:AW], ga)
        dya, dga = _rms_bwd(dy[:, 0:AW], na, ra, ga)
        _, nc, rc = _rms(ypre[:, AW:AW + CW], gc)
        dyc, dgc = _rms_bwd(dy[:, AW:AW + CW], nc, rc, gc)
        y_x = ypre[:, AW + CW:]
        _, nx, rx = _rms(y_x, gx)
        dyx, dgx = _rms_bwd(dy[:, AW + CW:], nx, rx, gx)
        qdo_ref[...] = _pack_pair(q_ref[...], dya)
        prod = dya * ypre[:, 0:AW]
        hi = prod.astype(BF16)
        lo = (prod - hi.astype(F32)).astype(BF16)
        head_sum = _dot(hi, e_ref[...]) + _dot(lo, e_ref[...])
        lane_a = lax.broadcasted_iota(jnp.int32, prod.shape, 1)
        ld_ref[...] = jnp.where((lane_a % HEAD) < HEAD // 2, lt_ref[...], head_sum)
        dgs_ref[0:1, :] += dgp
        dgs_ref[1:2, :] += jnp.concatenate([dga, dgc, dgx], axis=1)

        bcu = bcu_ref[...]
        b, c, u = bcu[:, 0:CW], bcu[:, CW:2 * CW], bcu[:, 2 * CW:]
        z = c * u
        halo = halo_ref[...]
        zprev = jnp.where(i < NT - 1, halo[:, CW:2 * CW] * halo[:, 2 * CW:], 0.0)
        row = lax.broadcasted_iota(jnp.int32, z.shape, 0)
        z1, z2 = _conv_taps(z, zprev, row)
        cw = cw_ref[...]
        conv = z2 * cw[0:1, :] + z1 * cw[1:2, :] + z * cw[2:3, :]
        dconv = dyc * b
        nxt = carry[...]
        dn1 = jnp.where(row == TQ - 1, nxt[0:1, :], pltpu.roll(dconv, TQ - 1, 0))
        dn2 = jnp.where(row == TQ - 1, nxt[1:2, :], jnp.where(row == TQ - 2, nxt[0:1, :], pltpu.roll(dconv, TQ - 2, 0)))
        carry[...] = dconv[0:SUBLANES, :]
        dz = dconv * cw[2:3, :] + dn1 * cw[1:2, :] + dn2 * cw[0:1, :]
        dbcu_ref[:, 0:CW] = (dyc * conv).astype(BF16)
        dbcu_ref[:, CW:2 * CW] = (dz * u).astype(BF16)
        dbcu_ref[:, 2 * CW:] = (dz * c).astype(BF16)
        dcw_ref[0:1, :] += jnp.sum(z2 * dconv, axis=0, keepdims=True)
        dcw_ref[1:2, :] += jnp.sum(z1 * dconv, axis=0, keepdims=True)
        dcw_ref[2:3, :] += jnp.sum(z * dconv, axis=0, keepdims=True)

        qx = qx_ref[...]
        kv = kv_ref[...]
        km, vm = kv[:, 0:XW], kv[:, XW:]
        lane = lax.broadcasted_iota(jnp.int32, qx.shape, 1)
        dqx = jnp.zeros(qx.shape, F32)
        dkm = jnp.zeros((N_MEM, XW), F32)
        dvm = jnp.zeros((N_MEM, XW), F32)
        for h in range(XW // HEAD):
            hm = (lane >= h * HEAD) & (lane < (h + 1) * HEAD)
            qm = jnp.where(hm, qx, jnp.zeros_like(qx))
            e, l = _xattn_scores(qm, km)
            p = e / l
            dom = jnp.where(hm, dyx, 0.0)
            do16 = dom.astype(BF16)
            dsum = jnp.sum(dom * y_x, axis=1, keepdims=True)
            ds = (p * (_dot_nt(do16, vm) - dsum)).astype(BF16)
            dqx = jnp.where(hm, _dot(ds, km), dqx)
            dkm = dkm + _dot_tn(ds, qm)
            dvm = dvm + _dot_tn(p.astype(BF16), do16)
        dqx_ref[...] = (dqx * SCALE).astype(BF16)
        dkv_ref[:, 0:XW] += dkm
        dkv_ref[:, XW:] += dvm

    def tile(w):
        return pl.BlockSpec((TQ, w), lambda i: (NT - 1 - i, 0))

    halo = pl.BlockSpec((SUBLANES, 3 * CW), lambda i: (jnp.maximum((NT - 1 - i) * (TQ // SUBLANES) - 1, 0), 0))
    return pl.pallas_call(
        body, grid=(NT,), name="mix_out_bwd",
        in_specs=[tile(D), tile(D), tile(D), tile(AW), _const((AW, AW)), tile(AW), tile(3 * CW), halo, tile(XW),
                  _const((N_MEM, 2 * XW)), _const((SUBLANES, CW)), _const((1, D)), _const((1, AW)), _const((1, CW)),
                  _const((1, XW)), _const((D, D))],
        out_specs=[tile(D), tile(AW), tile(AW), tile(3 * CW), tile(XW), _acc((SUBLANES, D)), _acc((SUBLANES, CW)),
                   _acc((N_MEM, 2 * XW))],
        out_shape=[jax.ShapeDtypeStruct((S, D), BF16), jax.ShapeDtypeStruct((S, AW), F32),
                   jax.ShapeDtypeStruct((S, AW), F32),
                   jax.ShapeDtypeStruct((S, 3 * CW), BF16), jax.ShapeDtypeStruct((S, XW), BF16),
                   jax.ShapeDtypeStruct((SUBLANES, D), F32), jax.ShapeDtypeStruct((SUBLANES, CW), F32),
                   jax.ShapeDtypeStruct((N_MEM, 2 * XW), F32)],
        scratch_shapes=[pltpu.VMEM((SUBLANES, CW), F32)],
        compiler_params=_cparams(56))(dx1, y2, ypre, ltot, head_ones, q, bcu, bcu, qx16, kv16, cw8, g_post, g_attn,
                                      g_conv, g_x, wout16)


def _attn_bwd(qdo, kvp, ld, chip_sums=()):
    n_in = 3
    views = [[a] + [a.reshape(S // n, n, AW) for _, n, _, _ in ATTN_PLANS[1:]] for a in (qdo, kvp, ld)]
    flat = [views[a][p] for p in range(3) for a in range(n_in)]
    ns = len(chip_sums)
    n_grid = AW // LANES

    def body(*refs):
        hbm = [refs[n_in * p:n_in * p + n_in] for p in range(3)]
        refs = refs[3 * n_in:]
        sum_refs, refs = refs[:ns], refs[ns:]
        outs = [refs[3 * p:3 * p + 3] for p in range(3)]
        landed_refs, sc = refs[9:9 + ns], refs[9 + ns:]
        bufs = [sc[3 * p:3 * p + 3] for p in range(3)]
        res = [sc[9 + 3 * p:12 + 3 * p] for p in range(3)]
        tab128, tab4, sem_in, sem_out = sc[18:22]
        step = pl.program_id(0)
        if ns:
            start_chips, finish_chips = _chips_steps(sum_refs, landed_refs, *sc[22:])
            pl.when(step == 0)(start_chips)
        now = [_class_gather(hbm[p], bufs[p], sem_in.at[p], _lanes_of(step)) for p in range(3)]
        nxt = [_class_gather(hbm[p], bufs[p], sem_in.at[p], _lanes_of(step + 1)) for p in range(3)]

        @pl.when(step == 0)
        def _():
            for p in range(3):
                _start(now[p])
                for b in bufs[p]:
                    b[0:PAD, :] = jnp.zeros((PAD, LANES), F32)
            _fill_bias(tab128, 128, False)
            _fill_bias(tab4, 64, True)

        def prefetch(p):
            pl.when(step + 1 < n_grid)(lambda: _start(nxt[p]))

        for p in range(3):
            for b in res[p][1:]:
                b[...] = jnp.zeros_like(b)
        lane = lax.broadcasted_iota(jnp.int32, (1, LANES), 1)

        def run(plan, plan_bufs, tab, dst):
            _, n_cls, qblk, nbc = plan
            partner = n_cls == 8
            bqdo, bkv, bld = plan_bufs
            rq, rk, rv = dst

            def block(g, carry):
                own, wins, mask = _block_rows(g, qblk, nbc, partner)
                qb, dob = _unpack_pair(bqdo[own, :])
                q2, do2 = _stack_heads(qb, lane), _stack_heads(dob, lane)
                kw, vw = _unpack_pair(_window(bkv, wins))
                ldv = bld[own, :]
                half = HEAD // 2
                lt2 = jnp.concatenate([ldv[:, 0:1], ldv[:, HEAD:HEAD + 1]], axis=0)
                dsum2 = jnp.concatenate([ldv[:, half:half + 1], ldv[:, HEAD + half:HEAD + half + 1]], axis=0)
                p = jnp.exp(_dot_nt(q2, kw) + tab[mask] - lt2)
                ds = (p * (_dot_nt(do2, vw) - dsum2)).astype(BF16)
                rq[own, :] = _unstack_heads(_dot(ds, kw), lane)
                dkw = _dot_tn(ds, q2)
                dvw = _dot_tn(p.astype(BF16), do2)
                n_w = WIN // len(wins)
                for i, w in enumerate(wins):
                    rk[w, :] += dkw[i * n_w:(i + 1) * n_w, :]
                    rv[w, :] += dvw[i * n_w:(i + 1) * n_w, :]
                return carry
            lax.fori_loop(0, n_cls * nbc, block, 0, unroll=ATTN_UNROLL)

        tabs = (tab128, tab4, tab128)
        for p in range(3):
            _wait(_whole_waits(bufs[p], sem_in.at[p]))
            run(ATTN_PLANS[p], bufs[p], tabs[p], res[p])
            prefetch(p)
            _start(_class_scatter(res[p], outs[p], sem_out.at[p], _lanes_of(step)))
        for p in range(3):
            _wait(_whole_waits(res[p], sem_out.at[p]))
        if ns:
            pl.when(step == n_grid - 1)(finish_chips)

    padded = pltpu.VMEM((PAD + S, LANES), F32)
    shapes = [jax.ShapeDtypeStruct(views[0][p].shape, F32) for p in range(3) for _ in range(3)]
    out = pl.pallas_call(
        body, grid=(n_grid,), name="attn_bwd",
        in_specs=[ANY] * (3 * n_in + ns), out_specs=[ANY] * (9 + ns),
        out_shape=shapes + _chips_shapes(chip_sums),
        scratch_shapes=[padded] * 18
        + [pltpu.VMEM((4, 256, WIN), F32), pltpu.VMEM((4, 128, WIN), F32),
           pltpu.SemaphoreType.DMA((3, n_in)), pltpu.SemaphoreType.DMA((3, 3))]
        + (_chips_scratch(ns) if ns else []),
        compiler_params=_cparams(56))(*flat, *chip_sums)
    return [o.reshape(S, AW) for o in out[:9]] + list(out[9:])


def _in_proj_bwd(dqkv, dbcu, dqx, cos, sins, w16, x, g, dx1):
    tq = TQ // 2

    def body(*refs):
        parts = refs[0:9]
        dbcu_ref, dqx_ref, c_ref, s_ref, w_ref, x_ref, g_ref, dx1_ref, dp_ref, gx_ref, dg_ref = refs[9:]

        @pl.when(pl.program_id(0) == 0)
        def _():
            dg_ref[...] = jnp.zeros_like(dg_ref)

        dq, dk, dv = (parts[i][...] + parts[3 + i][...] + parts[6 + i][...] for i in range(3))
        cos, sn = _all_heads(c_ref[...]), _all_heads(s_ref[...])
        dqr = dq * SCALE
        dkr = dk
        dp = jnp.concatenate([(dqr * cos + _rot_half(dqr * sn)).astype(BF16),
                              (dkr * cos + _rot_half(dkr * sn)).astype(BF16), dv.astype(BF16),
                              dbcu_ref[...], dqx_ref[...]], axis=1)
        dp_ref[...] = dp
        dh = _dot_nt(dp, w_ref[...])
        g = g_ref[...]
        _, n, r = _rms(x_ref[...], g)
        dx, dg = _rms_bwd(dh, n, r, g)
        gx_ref[...] = dx1_ref[...] + dx
        dg_ref[0:1, :] += dg

    def tile(w):
        return pl.BlockSpec((tq, w), lambda i: (i, 0))

    return pl.pallas_call(
        body, grid=(S // tq,), name="in_proj_bwd",
        in_specs=[tile(AW)] * 9 + [tile(3 * CW), tile(XW), tile(LANES), tile(LANES), _const((D, PW)),
                                   tile(D), _const((1, D)), tile(D)],
        out_specs=[tile(PW), tile(D), _acc((SUBLANES, D))],
        out_shape=[jax.ShapeDtypeStruct((S, PW), BF16), jax.ShapeDtypeStruct((S, D), F32),
                   jax.ShapeDtypeStruct((SUBLANES, D), F32)],
        compiler_params=_cparams(56))(*dqkv, dbcu, dqx, cos, sins, w16, x, g, dx1)


def _mem_bwd(mem, g_mem, wkv16, dkv):
    def body(m_ref, g_ref, w_ref, dkv_ref, dkv16_ref, dg_ref):
        dkv16 = dkv_ref[...].astype(BF16)
        dkv16_ref[...] = dkv16
        _, n, _ = _rms(m_ref[...], g_ref[...])
        dg = jnp.sum(_dot_nt(dkv16, w_ref[...]) * n, axis=0, keepdims=True)
        dg_ref[...] = jnp.broadcast_to(dg, dg_ref.shape)

    return pl.pallas_call(
        body, name="mem_bwd",
        out_shape=[jax.ShapeDtypeStruct((N_MEM, 2 * XW), BF16), jax.ShapeDtypeStruct((SUBLANES, D), F32)],
        compiler_params=pltpu.CompilerParams(vmem_limit_bytes=32 << 20))(mem, g_mem, wkv16, dkv)


N_CHIPS = N_DEV // 2


def _transpose_into(at, a_ref):
    kk = a_ref.shape[0]
    chunk = min(kk, 512)
    for c in range(kk // chunk):
        at[:, c * chunk:(c + 1) * chunk] = a_ref[c * chunk:(c + 1) * chunk, :].T


def _pair_scratch(block):
    return [pltpu.VMEM((N_CHIPS,) + block, BF16), pltpu.VMEM((N_CHIPS,) + block, BF16),
            pltpu.SemaphoreType.DMA((N_CHIPS,)), pltpu.SemaphoreType.DMA((N_CHIPS,))]


def _swap_with_sibling(p, stage, land, send, recv):
    x, y, c = lax.axis_index("x"), lax.axis_index("y"), lax.axis_index("c")
    return pltpu.make_async_remote_copy(src_ref=stage.at[p], dst_ref=land.at[p], send_sem=send.at[p],
                                        recv_sem=recv.at[p], device_id=(x, y, 1 - c), device_id_type=MESH)


def _wgrad_cols(place, a16, b16, blk, name, square_b=False, transpose_out=False, to_chips=False, small=()):
    kk, m = a16.shape
    aligned = blk % LANES == 0
    wide = blk if aligned else -(-(blk + LANES // 2) // LANES) * LANES
    block = (blk, m) if transpose_out else (m, blk)

    def chip_of(step, my_chip):
        return jnp.bitwise_xor(my_chip, N_CHIPS - 1 - step) if to_chips else step

    def body(pl_ref, a_ref, *refs):
        b_refs, refs = refs[:2 if aligned else 1], refs[2 if aligned else 1:]
        accs, refs = refs[:len(small)], refs[len(small):]
        (cs_ref, own_ref), refs = refs[:2], refs[2:]
        if to_chips:
            landed, refs = refs[0], refs[1:]
        if small:
            tot_ref, refs = refs[0], refs[1:]
        (at, stage, land, send, recv), refs = refs[:5], refs[5:]
        if not aligned:
            (win, wsem), refs = refs[:2], refs[2:]
        if small:
            start_small, finish_small = _small_reduce_steps(accs, tot_ref, *refs[-4:])
            refs = refs[:-4]
        step = pl.program_id(0)
        if small:
            pl.when(step == 0)(start_small)
        x, y, c = lax.axis_index("x"), lax.axis_index("y"), lax.axis_index("c")
        my_chip = 2 * x + y
        p = chip_of(step, my_chip)

        def fetch(at_step, mine):
            j = 2 * chip_of(at_step, my_chip) + (c if mine else 1 - c)
            first = pl.multiple_of(((j * blk) >> 7) << 7, LANES)
            slot = 2 * (at_step & 1) + mine
            return pltpu.make_async_copy(b_refs[0].at[:, pl.ds(first, wide)], win.at[slot], wsem.at[slot])

        @pl.when(step == 0)
        def _():
            if not aligned:
                fetch(0, 0).start()
                fetch(0, 1).start()
            _transpose_into(at, a_ref)

        if not aligned:
            @pl.when(step + 1 < N_CHIPS)
            def _():
                fetch(step + 1, 0).start()
                fetch(step + 1, 1).start()

        def partial(mine):
            if aligned:
                b = b_refs[mine][...]
                if square_b:
                    b = b * b
                acc = _dot(at[...], b)
            else:
                fetch(step, mine).wait()
                acc = _dot(at[...], win[2 * (step & 1) + mine])
                odd = c if mine else 1 - c
                acc = pltpu.roll(acc, jnp.where(odd == 0, 0, wide - LANES // 2), 1)[:, 0:blk]
            return acc.T if transpose_out else acc

        stage[p] = partial(0).astype(BF16)
        swap = _swap_with_sibling(p, stage, land, send, recv)
        swap.start()
        mine = partial(1)
        swap.wait()
        total = mine + land[p].astype(F32)
        cs_ref[0] = total.astype(BF16)

        @pl.when(p == my_chip)
        def _():
            own_ref[...] = total

        if to_chips:
            stage2, send2, recv2 = refs
            flipped = jnp.bitwise_xor(p, my_chip)
            k = jnp.where(flipped == 2, 0, jnp.where(flipped == 1, 1, 2))

            def to_owner(src, k_, px, py):
                return pltpu.make_async_remote_copy(src_ref=src, dst_ref=landed.at[k_], send_sem=send2.at[k_],
                                                    recv_sem=recv2.at[k_], device_id=(px, py, c), device_id_type=MESH)

            @pl.when(p != my_chip)
            def _():
                stage2[p] = total.astype(BF16)
                to_owner(stage2.at[p], k, p >> 1, p & 1).start()

            @pl.when(step == N_CHIPS - 1)
            def _():
                for k_ in range(N_CHIPS - 1):
                    to_owner(stage2.at[0], k_, x, y).wait()

        if small:
            pl.when(step == N_CHIPS - 1)(finish_small)

    def b_spec(mine):
        return pl.BlockSpec((kk, blk), lambda i, s: (0, 2 * chip_of(i, s[1]) + (s[0] if mine else 1 - s[0])))

    b_specs, b_args = ([b_spec(0), b_spec(1)], (b16, b16)) if aligned else ([ANY], (b16,))
    scratch = [pltpu.VMEM((m, kk), BF16)] + _pair_scratch(block)
    if not aligned:
        scratch += [pltpu.VMEM((4, kk, wide), BF16), pltpu.SemaphoreType.DMA((4,))]
    out_specs = [pl.BlockSpec((1,) + block, lambda i, s: (chip_of(i, s[1]), 0, 0)), pl.BlockSpec(block, lambda i, s: (0, 0))]
    out_shape = [jax.ShapeDtypeStruct((N_CHIPS,) + block, BF16), jax.ShapeDtypeStruct(block, F32)]
    if to_chips:
        out_specs.append(ANY)
        out_shape.append(jax.ShapeDtypeStruct((N_CHIPS - 1,) + block, BF16))
        scratch += [pltpu.VMEM((N_CHIPS,) + block, BF16), pltpu.SemaphoreType.DMA((N_CHIPS - 1,)),
                    pltpu.SemaphoreType.DMA((N_CHIPS - 1,))]
    small_specs = [pl.BlockSpec(a.shape, lambda i, s: (0, 0)) for a in small]
    if small:
        out_specs.append(pl.BlockSpec((PACK_ROWS, D), lambda i, s: (0, 0)))
        out_shape.append(jax.ShapeDtypeStruct((PACK_ROWS, D), F32))
        scratch += _small_reduce_scratch()
    return pl.pallas_call(
        body, name=name,
        grid_spec=pltpu.PrefetchScalarGridSpec(
            num_scalar_prefetch=1, grid=(N_CHIPS,),
            in_specs=[pl.BlockSpec((kk, m), lambda i, s: (0, 0), pipeline_mode=pl.Buffered(1))] + b_specs + small_specs,
            out_specs=out_specs, scratch_shapes=scratch),
        out_shape=out_shape, compiler_params=_cparams(56))(place, a16, *b_args, *small)


def _wgrad_rows(place, a16, b16, name):
    kk, m = a16.shape
    n = b16.shape[1]
    block = (m // N_DEV, n)

    def body(pl_ref, a_ref, b_ref, cs_ref, own_ref, at, acc, stage, land, send, recv):
        c = pl_ref[0]
        _transpose_into(at, a_ref)
        acc[...] = _dot(at[...], b_ref[...])

        def rows(owner):
            return pl.ds(pl.multiple_of(owner * block[0], block[0]), block[0])

        swaps = []
        for p in range(N_CHIPS):
            stage[p] = acc[rows(2 * p + 1 - c), :].astype(BF16)
            swaps.append(_swap_with_sibling(p, stage, land, send, recv))
            swaps[-1].start()
        for p in range(N_CHIPS):
            swaps[p].wait()
            total = acc[rows(2 * p + c), :] + land[p].astype(F32)
            cs_ref[p] = total.astype(BF16)

            @pl.when(p == pl_ref[1])
            def _():
                own_ref[...] = total

    vmem = pl.BlockSpec(memory_space=pltpu.VMEM)
    return pl.pallas_call(
        body, name=name,
        in_specs=[pl.BlockSpec(memory_space=pltpu.SMEM), vmem, vmem], out_specs=[vmem, vmem],
        out_shape=[jax.ShapeDtypeStruct((N_CHIPS,) + block, BF16), jax.ShapeDtypeStruct(block, F32)],
        scratch_shapes=[pltpu.VMEM((m, kk), BF16), pltpu.VMEM((m, n), F32)] + _pair_scratch(block),
        compiler_params=pltpu.CompilerParams(vmem_limit_bytes=56 << 20))(place, a16, b16)


def _adamw_math(w, g, m, v):
    m = ADAM_B1 * m + (1.0 - ADAM_B1) * g
    v = ADAM_B2 * v + (1.0 - ADAM_B2) * jnp.square(g)
    m_hat = m / (1.0 - ADAM_B1 ** ADAM_STEP)
    v_hat = v / (1.0 - ADAM_B2 ** ADAM_STEP)
    delta = -ADAM_LR * (m_hat / (jnp.sqrt(v_hat) + ADAM_EPS) + ADAM_WD * w)
    return delta, m, v


def _adamw_shards(updates, name, chip_sums=()):
    names, nu, ns = list(updates), len(updates), len(chip_sums)

    def body(*refs):
        ins, sum_refs = refs[:5 * nu], refs[5 * nu:5 * nu + ns]
        outs = refs[5 * nu + ns:9 * nu + ns]
        landed_refs, scratch = refs[9 * nu + ns:9 * nu + 2 * ns], refs[9 * nu + 2 * ns:]
        if ns:
            start_chips, finish_chips = _chips_steps(sum_refs, landed_refs, *scratch)
            start_chips()
        for i in range(nu):
            o_ref, r_ref, w_ref, m_ref, v_ref = ins[5 * i:5 * i + 5]
            g_out, d_out, m_out, v_out = outs[4 * i:4 * i + 4]
            g = o_ref[...] + r_ref[0].astype(F32) + r_ref[1].astype(F32) + r_ref[2].astype(F32)
            g_out[...] = g
            d_out[...], m_out[...], v_out[...] = _adamw_math(w_ref[...], g, m_ref[...], v_ref[...])
        if ns:
            finish_chips()

    vmem = pl.BlockSpec(memory_space=pltpu.VMEM)
    out = pl.pallas_call(
        body, name=name,
        in_specs=[vmem] * (5 * nu) + [ANY] * ns, out_specs=[vmem] * (4 * nu) + [ANY] * ns,
        out_shape=[jax.ShapeDtypeStruct(updates[n][2].shape, F32) for n in names for _ in range(4)]
        + _chips_shapes(chip_sums),
        scratch_shapes=_chips_scratch(ns) if ns else [],
        compiler_params=pltpu.CompilerParams(vmem_limit_bytes=56 << 20),
    )(*[a for n in names for a in updates[n]], *chip_sums)
    return {n: out[4 * i:4 * i + 4] for i, n in enumerate(names)}, list(out[4 * nu:])


def _place():
    x, y, c = lax.axis_index("x"), lax.axis_index("y"), lax.axis_index("c")
    chips = [(1 - x, y), (x, 1 - y), (1 - x, 1 - y)]
    return x, y, c, chips


def _gather_steps(ins, outs, send, recv, lsem):
    nt = len(ins)
    x, y, c, (xn, yn, diag) = _place()
    me, sib = (x, y, c), (x, y, 1 - c)

    def slot(t, px, py, pc):
        return outs[t].at[4 * px + 2 * py + pc]

    def copy(t, k, block, to, src=None):
        return pltpu.make_async_remote_copy(
            src_ref=slot(t, *block) if src is None else src, dst_ref=slot(t, *block),
            send_sem=send.at[t, k], recv_sem=recv.at[t, k], device_id=to, device_id_type=MESH)

    mine = [pltpu.make_async_copy(ins[t], slot(t, *me), lsem.at[t]) for t in range(nt)]
    first = [copy(t, k, me, to, src=ins[t]) for t in range(nt) for k, to in ((0, sib), (1, (*xn, c)), (2, (*yn, c)))]

    def start():
        for cp in mine + first:
            cp.start()

    def landed(k, chip, also_to=None):
        for t in range(nt):
            copy(t, k, (*chip, c), me).wait_recv()
            if also_to is not None:
                copy(t, 3, (*chip, c), (*also_to, c)).start()
            copy(t, 3 + k, (*chip, c), sib).start()

    def relay():
        @pl.when(c == 0)
        def _():
            landed(1, xn, also_to=yn)
            landed(2, yn)

        @pl.when(c == 1)
        def _():
            landed(2, yn, also_to=xn)
            landed(1, xn)

    def finish():
        landed(3, diag)
        for t in range(nt):
            copy(t, 0, sib, me).wait_recv()
            for k, chip in ((4, xn), (5, yn), (6, diag)):
                copy(t, k, (*chip, 1 - c), me).wait_recv()
            for k in range(7):
                copy(t, k, me, sib).wait_send()
        for cp in mine:
            cp.wait()

    return start, relay, finish


def _gather_scratch(nt):
    return [pltpu.SemaphoreType.DMA((nt, 7)), pltpu.SemaphoreType.DMA((nt, 7)), pltpu.SemaphoreType.DMA((nt,))]


def _gathered_shapes(shards):
    return [jax.ShapeDtypeStruct((N_DEV,) + s.shape, s.dtype) for s in shards]


def _call_with_gather(body, n_grid, shards, *, name, in_specs, out_specs, out_shape, scratch_shapes, vmem_mb, args):
    ng, n_in, n_out = len(shards), len(in_specs), len(out_specs)

    def wrapped(*refs):
        ins, shard_refs = refs[:n_in], refs[n_in:n_in + ng]
        outs = refs[n_in + ng:n_in + ng + n_out]
        whole_refs = refs[n_in + ng + n_out:n_in + 2 * ng + n_out]
        scratch = refs[n_in + 2 * ng + n_out:]
        if ng:
            start, relay, finish = _gather_steps(shard_refs, whole_refs, *scratch[len(scratch_shapes):])
            pl.when(pl.program_id(0) == 0)(start)
            pl.when(pl.program_id(0) == n_grid // 2)(relay)
        body(*ins, *outs, *scratch[:len(scratch_shapes)])
        if ng:
            pl.when(pl.program_id(0) == n_grid - 1)(finish)

    return pl.pallas_call(
        wrapped, grid=(n_grid,), name=name,
        in_specs=list(in_specs) + [ANY] * ng, out_specs=list(out_specs) + [ANY] * ng,
        out_shape=list(out_shape) + _gathered_shapes(shards),
        scratch_shapes=list(scratch_shapes) + (_gather_scratch(ng) if ng else []),
        compiler_params=_cparams(vmem_mb))(*args, *shards)


def _chips_steps(ins, outs, send, recv):
    _, _, c, chips = _place()
    copies = [pltpu.make_async_remote_copy(
        src_ref=ins[t].at[2 * px + py], dst_ref=outs[t].at[j], send_sem=send.at[t, j], recv_sem=recv.at[t, j],
        device_id=(px, py, c), device_id_type=MESH) for t in range(len(ins)) for j, (px, py) in enumerate(chips)]

    def start():
        for cp in copies:
            cp.start()

    def finish():
        for cp in copies:
            cp.wait()

    return start, finish


def _chips_scratch(nt):
    return [pltpu.SemaphoreType.DMA((nt, 3)), pltpu.SemaphoreType.DMA((nt, 3))]


def _chips_shapes(cs16s):
    return [jax.ShapeDtypeStruct((3,) + g.shape[1:], g.dtype) for g in cs16s]


SMALL = (("g_pre_mix", 0, 0, D), ("g_mem", 1, 0, D), ("g_post_mix", 2, 0, D), ("g_attn_out", 3, 0, AW),
         ("g_conv_out", 3, AW, CW), ("g_xattn_out", 3, AW + CW, XW), ("g_post_mlp", 4, 0, D), ("g_pre_mlp", 5, 0, D))
CONV_ROW = 8
PACK_ROWS = 16


LOSS_ROW = 15


def _small_reduce_steps(accs, tot_ref, pack, land, send, recv):
    acc_in, acc_mem, acc_mix, acc_mlp, acc_cw, acc_loss = accs
    x, y, c, _ = _place()
    me = 4 * x + 2 * y + c
    copies = []
    for k in range(1, N_DEV):
        kx, ky, kc = (k >> 2) & 1, (k >> 1) & 1, k & 1
        peer = (1 - x if kx else x, 1 - y if ky else y, 1 - c if kc else c)
        copies.append(pltpu.make_async_remote_copy(
            src_ref=pack, dst_ref=land.at[me], send_sem=send.at[k - 1], recv_sem=recv.at[k - 1],
            device_id=peer, device_id_type=MESH))

    def start():
        pack[...] = jnp.zeros_like(pack)
        pack[0:1, :] = acc_in[0:1, :]
        pack[1:2, :] = acc_mem[0:1, :]
        pack[2:4, :] = acc_mix[0:2, :]
        pack[4:6, :] = acc_mlp[0:2, :]
        pack[CONV_ROW:CONV_ROW + 3, 0:CW] = acc_cw[0:3, :]
        pack[LOSS_ROW:LOSS_ROW + 1, 0:LANES] = acc_loss[0:1, :]
        land[me] = pack[...]
        for cp in copies:
            cp.start()

    def finish():
        for cp in copies:
            cp.wait()
        tot = land[0]
        for s in range(1, N_DEV):
            tot = tot + land[s]
        tot_ref[...] = tot

    return start, finish


def _small_reduce_scratch():
    return [pltpu.VMEM((PACK_ROWS, D), F32), pltpu.VMEM((N_DEV, PACK_ROWS, D), F32),
            pltpu.SemaphoreType.DMA((N_DEV - 1,)), pltpu.SemaphoreType.DMA((N_DEV - 1,))]


def _small_update(tot, me, params):
    flat = [a for n, _, _, _ in SMALL for a in params[n]] + list(params["conv_w"])
    n_par = len(SMALL) + 1
    tap_cols = CW // N_DEV

    def body(*refs):
        me_ref, tot_ref = refs[0:2]
        ins = refs[2:2 + 3 * n_par]
        loss_out = refs[2 + 3 * n_par]
        outs = refs[3 + 3 * n_par:]
        tot = tot_ref[...]
        loss_out[...] = jnp.broadcast_to(tot[LOSS_ROW:LOSS_ROW + 1, 0:LANES], loss_out.shape)

        def update(i, g):
            w_ref, m_ref, v_ref = ins[3 * i:3 * i + 3]
            g_out, d_out, m_out, v_out = outs[4 * i:4 * i + 4]
            g_out[...] = g
            d_out[...], m_out[...], v_out[...] = _adamw_math(w_ref[...], g, m_ref[...], v_ref[...])

        for i, (_, row, lane0, width) in enumerate(SMALL):
            update(i, tot[row:row + 1, lane0:lane0 + width])
        me = me_ref[0]
        taps = pltpu.roll(tot[CONV_ROW:CONV_ROW + SUBLANES, 0:CW], jnp.where(me == 0, 0, CW - me * tap_cols), 1)
        update(n_par - 1, taps[0:3, 0:tap_cols])

    shapes = [jax.ShapeDtypeStruct(params[n][0].shape, F32) for n, _, _, _ in SMALL] + [
        jax.ShapeDtypeStruct(params["conv_w"][0].shape, F32)]
    vmem = pl.BlockSpec(memory_space=pltpu.VMEM)
    loss, *out = pl.pallas_call(
        body, name="small_update",
        in_specs=[pl.BlockSpec(memory_space=pltpu.SMEM)] + [vmem] * (1 + 3 * n_par),
        out_shape=[jax.ShapeDtypeStruct((SUBLANES, LANES), F32)] + [s for s in shapes for _ in range(4)],
    )(me, tot, *flat)
    names = [n for n, _, _, _ in SMALL] + ["conv_w"]
    return loss[0, 0], {n: out[4 * i:4 * i + 4] for i, n in enumerate(names)}


def _local_step(x, mem, pos, gains, shards, tgt, place):
    half = HEAD // 2
    inv_freq = jnp.float32(ROPE_THETA) ** (-(jnp.arange(half, dtype=F32) * 2.0 / HEAD))
    invf = jnp.tile(inv_freq, LANES // half)[None, :]
    sgn = jnp.tile(jnp.concatenate([-jnp.ones((half,), F32), jnp.ones((half,), F32)]), LANES // HEAD)[None, :]
    cos, sins, win8 = _rope_table(pos.astype(F32).reshape(S, 1), invf, sgn, [shards["w_in"]])
    wdn_left, wdn_right = shards["w_down"][:, 0:D // 2], shards["w_down"][:, D // 2:]
    q, kvp, bcu, qx16, h16, win16, wout8, wkv8, conv8, wdn8_right = _in_proj(
        x, gains["g_pre_mix"], win8, cos, sins, [shards["w_out"], shards["w_mem_kv"], shards["conv_w"], wdn_right])
    wout16, wkv16 = wout8.reshape(D, D), wkv8.reshape(D, 2 * XW)
    cw_full = conv8[:, 0:3, 0:CW // N_DEV].transpose(1, 0, 2).reshape(3, CW)
    cw8 = jnp.zeros((SUBLANES, CW), F32).at[0:3].set(cw_full)
    y_attn, ltot, wup8, wdn8_left = _attn_fwd(q, kvp, [shards["w_up"], wdn_left])
    wdn_halves = (wdn8_left.reshape(FF, D // 2), wdn8_right.reshape(FF, D // 2))
    memn16, kv16 = _mem_fwd(mem, gains["g_mem"], wkv16)
    ypre, y16, y2, x1 = _mix_out(y_attn, bcu, qx16, kv16, cw8, gains["g_attn_out"], gains["g_conv_out"],
                                 gains["g_xattn_out"], gains["g_post_mix"], wout16, x, [])
    a16, du16, h2_16, df2_16, dx1, loss8, dg_mlp = _mlp(
        x1, tgt, gains["g_pre_mlp"], gains["g_post_mlp"], wup8, wdn_halves)

    sums = {"w_up": _wgrad_cols(place, h2_16, du16, FF_BLK, "wgrad_up"),
            "w_down": _wgrad_cols(place, df2_16, a16, FF_BLK, "wgrad_down", square_b=True, transpose_out=True)}

    head_id = jnp.arange(AW, dtype=jnp.int32) // HEAD
    head_ones = (head_id[:, None] == head_id[None, :]).astype(BF16)
    dy2_16, qdo, ld, dbcu, dqx, dgs, dcw, dkv = _mix_out_bwd(
        dx1, y2, ypre, ltot, head_ones, q, bcu, qx16, kv16, cw8, gains["g_post_mix"], gains["g_attn_out"],
        gains["g_conv_out"], gains["g_xattn_out"], wout16)
    dkv16, dg_mem = _mem_bwd(mem, gains["g_mem"], wkv16, dkv)
    sums["w_mem_kv"] = _wgrad_rows(place, memn16, dkv16, "wgrad_mem_kv")
    sums["w_out"] = _wgrad_rows(place, y16, dy2_16, "wgrad_out")
    out = _attn_bwd(qdo, kvp, ld, [s[0] for s in sums.values()])
    dqkv, landed = out[:9], out[9:]
    reduced = {n: (s[1], landed[t]) for t, (n, s) in enumerate(sums.items())}
    dproj16, grad_x, dg_in = _in_proj_bwd(dqkv, dbcu, dqx, cos, sins, win16, x, gains["g_pre_mix"], dx1)

    _, in_own, in_landed, small_tot = _wgrad_cols(place, h16, dproj16, PW // N_DEV, "wgrad_in", to_chips=True,
                                                  small=(dg_in, dg_mem, dgs, dg_mlp, dcw, loss8))
    reduced["w_in"] = (in_own, in_landed)
    return grad_x, reduced, small_tot


BIG = ("w_in", "w_mem_kv", "w_out", "w_up", "w_down")
ORDER = ("g_pre_mix", "g_mem", "w_in", "w_mem_kv", "conv_w", "g_attn_out", "g_conv_out", "g_xattn_out", "w_out",
         "g_post_mix", "g_pre_mlp", "w_up", "w_down", "g_post_mlp")


def kernel(x, mem, positions, g_pre_mix, g_mem, w_in, w_mem_kv, conv_w, g_attn_out, g_conv_out, g_xattn_out, w_out, g_post_mix, g_pre_mlp, w_up, w_down, g_post_mlp, loss_target, m_g_pre_mix, m_g_mem, m_w_in, m_w_mem_kv, m_conv_w, m_g_attn_out, m_g_conv_out, m_g_xattn_out, m_w_out, m_g_post_mix, m_g_pre_mlp, m_w_up, m_w_down, m_g_post_mlp, v_g_pre_mix, v_g_mem, v_w_in, v_w_mem_kv, v_conv_w, v_g_attn_out, v_g_conv_out, v_g_xattn_out, v_w_out, v_g_post_mix, v_g_pre_mlp, v_w_up, v_w_down, v_g_post_mlp):
    w = dict(g_pre_mix=g_pre_mix, g_mem=g_mem, w_in=w_in, w_mem_kv=w_mem_kv, conv_w=conv_w, g_attn_out=g_attn_out,
             g_conv_out=g_conv_out, g_xattn_out=g_xattn_out, w_out=w_out, g_post_mix=g_post_mix, g_pre_mlp=g_pre_mlp,
             w_up=w_up, w_down=w_down, g_post_mlp=g_post_mlp)
    mo = dict(g_pre_mix=m_g_pre_mix, g_mem=m_g_mem, w_in=m_w_in, w_mem_kv=m_w_mem_kv, conv_w=m_conv_w,
              g_attn_out=m_g_attn_out, g_conv_out=m_g_conv_out, g_xattn_out=m_g_xattn_out, w_out=m_w_out,
              g_post_mix=m_g_post_mix, g_pre_mlp=m_g_pre_mlp, w_up=m_w_up, w_down=m_w_down, g_post_mlp=m_g_post_mlp)
    vo = dict(g_pre_mix=v_g_pre_mix, g_mem=v_g_mem, w_in=v_w_in, w_mem_kv=v_w_mem_kv, conv_w=v_conv_w,
              g_attn_out=v_g_attn_out, g_conv_out=v_g_conv_out, g_xattn_out=v_g_xattn_out, w_out=v_w_out,
              g_post_mix=v_g_post_mix, g_pre_mlp=v_g_pre_mlp, w_up=v_w_up, w_down=v_w_down, g_post_mlp=v_g_post_mlp)

    xi, yi, ci = lax.axis_index("x"), lax.axis_index("y"), lax.axis_index("c")
    me = 4 * xi + 2 * yi + ci
    place = jnp.stack([ci, 2 * xi + yi]).astype(jnp.int32)

    shards = {n: w[n][0].astype(BF16) for n in BIG}
    shards["conv_w"] = jnp.zeros((SUBLANES, LANES), F32).at[0:3, 0:CW // N_DEV].set(conv_w[0])

    gains = {n: w[n] for n, _, _, _ in SMALL}
    grad_x, reduced, small_tot = _local_step(x[0], mem[0], positions[0], gains, shards, loss_target[0], place)

    updated = {}
    for group in (("w_up", "w_down"), ("w_in", "w_out", "w_mem_kv")):
        updated.update(_adamw_shards({n: (*reduced[n], w[n][0], mo[n][0], vo[n][0]) for n in group},
                                     "adamw_" + "_".join(group))[0])
    grad, delta, new_m, new_v = {}, {}, {}, {}
    for n, (g, d_, m_, v_) in updated.items():
        grad[n], delta[n], new_m[n], new_v[n] = g[None], d_[None], m_[None], v_[None]

    params = {n: (w[n], mo[n], vo[n]) for n, _, _, _ in SMALL}
    params["conv_w"] = (w["conv_w"][0], mo["conv_w"][0], vo["conv_w"][0])
    loss, small = _small_update(small_tot, me.reshape(1).astype(jnp.int32), params)
    for n, (g, d_, m_, v_) in small.items():
        lead = (lambda a: a[None]) if n == "conv_w" else (lambda a: a)
        grad[n], delta[n], new_m[n], new_v[n] = lead(g), lead(d_), lead(m_), lead(v_)

    return (loss, grad_x[None], *[grad[n] for n in ORDER], *[delta[n] for n in ORDER],
            *[new_m[n] for n in ORDER], *[new_v[n] for n in ORDER])
```

```python
import jax
import jax.numpy as jnp
from jax import lax
from jax.experimental import pallas as pl
from jax.experimental.pallas import tpu as pltpu

F32, BF16 = jnp.float32, jnp.bfloat16
MESH = pl.DeviceIdType.MESH
ANY = pl.BlockSpec(memory_space=pl.ANY)

N_DEV = 8
D = 1024
S = 4096
N_MEM = 256
HEAD = 64
AW, CW, XW = 512, 256, 256
PW = 3 * AW + 3 * CW + XW
FF = 4096
FF_BLK = FF // N_DEV
EPS = 1e-6
NEG = -1e30
SCALE = HEAD ** -0.5
ROPE_THETA = 10000.0
LANES = 128
SUBLANES = 8

ADAM_LR, ADAM_B1, ADAM_B2, ADAM_EPS, ADAM_WD, ADAM_STEP = 0.001, 0.9, 0.999, 1e-08, 0.01, 10

TQ = 512
TQ_MLP = 512
NT = S // TQ


def _cparams(vmem_mb, n_grid=1):
    return pltpu.CompilerParams(dimension_semantics=("arbitrary",) * n_grid, vmem_limit_bytes=vmem_mb << 20)


def _const(shape):
    nd = len(shape)
    return pl.BlockSpec(shape, lambda *_: (0,) * nd, pipeline_mode=pl.Buffered(1))


def _acc(shape):
    nd = len(shape)
    return pl.BlockSpec(shape, lambda *_: (0,) * nd)


def _dot(a, b):
    return jnp.dot(a, b, preferred_element_type=F32)


def _dot_nt(a, b):
    return lax.dot_general(a, b, (((1,), (1,)), ((), ())), preferred_element_type=F32)


def _dot_tn(a, b):
    return lax.dot_general(a, b, (((0,), (0,)), ((), ())), preferred_element_type=F32)


def _rms(x, g):
    r = lax.rsqrt(jnp.mean(x * x, axis=-1, keepdims=True) + EPS)
    n = x * r
    return n * g, n, r


def _rms_bwd(dy, n, r, g):
    dn = dy * g
    dx = r * (dn - n * jnp.mean(dn * n, axis=-1, keepdims=True))
    return dx, jnp.sum(dy * n, axis=0, keepdims=True)


def _rot_half(t):
    lane = lax.broadcasted_iota(jnp.int32, t.shape, 1)
    n = t.shape[1]
    return jnp.where((lane % HEAD) < HEAD // 2, pltpu.roll(t, n - HEAD // 2, 1), pltpu.roll(t, HEAD // 2, 1))


def _rope_table(pos_col, invf, sgn, shards):
    def body(p_ref, f_ref, s_ref, c_out, s_out):
        ang = p_ref[...] * f_ref[...]
        c_out[...] = jnp.cos(ang)
        s_out[...] = jnp.sin(ang) * s_ref[...]

    tile = pl.BlockSpec((TQ, LANES), lambda i: (i, 0))
    return _call_with_gather(
        body, NT, shards, name="rope_table",
        in_specs=[pl.BlockSpec((TQ, 1), lambda i: (i, 0)), _const((1, LANES)), _const((1, LANES))],
        out_specs=[tile, tile], out_shape=[jax.ShapeDtypeStruct((S, LANES), F32)] * 2,
        scratch_shapes=[], vmem_mb=32, args=(pos_col, invf, sgn))


def _all_heads(t):
    return jnp.tile(t, (1, AW // LANES))


def _mem_fwd(mem, g_mem, wkv16):
    def body(m_ref, g_ref, w_ref, n16_ref, kv_ref):
        y, _, _ = _rms(m_ref[...], g_ref[...])
        y16 = y.astype(BF16)
        n16_ref[...] = y16
        kv_ref[...] = _dot(y16, w_ref[...]).astype(BF16)

    return pl.pallas_call(
        body, name="mem_fwd",
        out_shape=[jax.ShapeDtypeStruct((N_MEM, D), BF16), jax.ShapeDtypeStruct((N_MEM, 2 * XW), BF16)],
        compiler_params=pltpu.CompilerParams(vmem_limit_bytes=32 << 20))(mem, g_mem, wkv16)


def _in_proj(x, g, w8, cos, sins, shards):
    blk = PW // N_DEV

    def body(x_ref, g_ref, w8_ref, c_ref, s_ref, q_ref, kv_ref, bcu_ref, qx_ref, h_ref, w_out, w_ref):
        @pl.when(pl.program_id(0) == 0)
        def _():
            for j in range(N_DEV):
                w_ref[:, j * blk:(j + 1) * blk] = w8_ref[j]
            w_out[...] = w_ref[...]

        y, _, _ = _rms(x_ref[...], g_ref[...])
        h = y.astype(BF16)
        h_ref[...] = h
        proj = _dot(h, w_ref[...])
        cos, sn = _all_heads(c_ref[...]), _all_heads(s_ref[...])
        q, k = proj[:, 0:AW], proj[:, AW:2 * AW]
        q_ref[...] = (q * cos + _rot_half(q) * sn) * SCALE
        kv_ref[...] = _pack_pair(k * cos + _rot_half(k) * sn, proj[:, 2 * AW:3 * AW])
        bcu_ref[...] = proj[:, 3 * AW:3 * AW + 3 * CW]
        qx_ref[...] = (proj[:, 3 * AW + 3 * CW:] * SCALE).astype(BF16)

    def tile(w):
        return pl.BlockSpec((TQ, w), lambda i: (i, 0))

    return _call_with_gather(
        body, NT, shards, name="in_proj",
        in_specs=[tile(D), _const((1, D)), _const((N_DEV, D, blk)), tile(LANES), tile(LANES)],
        out_specs=[tile(AW), tile(AW), tile(3 * CW), tile(XW), tile(D), _acc((D, PW))],
        out_shape=[jax.ShapeDtypeStruct((S, AW), F32)] * 2 + [
            jax.ShapeDtypeStruct((S, 3 * CW), F32), jax.ShapeDtypeStruct((S, XW), BF16),
            jax.ShapeDtypeStruct((S, D), BF16), jax.ShapeDtypeStruct((D, PW), BF16)],
        scratch_shapes=[pltpu.VMEM((D, PW), BF16)], vmem_mb=56, args=(x, g, w8, cos, sins))


ATTN_PLANS = (("p1", 1, 128, 32), ("p4", 8, 64, 8), ("p16", 16, 128, 2))
PAD = 128
WIN = 256


ATTN_UNROLL = 16


def _fill_bias(tab, qblk, partner):
    qi = lax.broadcasted_iota(jnp.int32, (2 * qblk, WIN), 0) & (qblk - 1)
    kj = lax.broadcasted_iota(jnp.int32, (2 * qblk, WIN), 1)
    piece = kj >> (qblk.bit_length() - 1)
    kk = kj & (qblk - 1)
    prev = (piece & 1) == 0
    of_partner = piece >= 2
    for first in (0, 1):
        for par in (0, 1):
            lo = jnp.where(prev, (qblk if first else qi) + jnp.where(of_partner, par, 0), 0)
            hi = jnp.where(prev, qblk, qi + jnp.where(of_partner, par - 1, 0))
            tab[2 * first + par] = jnp.where((kk >= lo) & (kk <= hi), 0.0, NEG).astype(F32)


def _block_rows(g, qblk, nbc, partner):
    own = pl.ds(pl.multiple_of(PAD + g * qblk, qblk), qblk)
    first = ((g & (nbc - 1)) == 0).astype(jnp.int32)
    if partner:
        gp = jnp.bitwise_xor(g, 4 * nbc)
        wins = (pl.ds(pl.multiple_of(PAD + (g - 1) * qblk, qblk), 2 * qblk),
                pl.ds(pl.multiple_of(PAD + (gp - 1) * qblk, qblk), 2 * qblk))
        return own, wins, 2 * first + ((g >> ((4 * nbc).bit_length() - 1)) & 1)
    return own, (pl.ds(pl.multiple_of(PAD + (g - 1) * qblk, qblk), 2 * qblk),), 2 * first


def _pack_pair(lo, hi):
    lo_bits = lax.bitcast_convert_type(lo.astype(BF16).astype(F32), jnp.uint32) >> 16
    hi_bits = lax.bitcast_convert_type(hi.astype(BF16).astype(F32), jnp.uint32) & jnp.uint32(0xFFFF0000)
    return lax.bitcast_convert_type(hi_bits | lo_bits, F32)


def _unpack_pair(c):
    bits = lax.bitcast_convert_type(c, jnp.uint32)
    lo = lax.bitcast_convert_type(bits << 16, F32).astype(BF16)
    hi = lax.bitcast_convert_type(bits & jnp.uint32(0xFFFF0000), F32).astype(BF16)
    return lo, hi


def _window(ref, wins):
    parts = [ref[w, :] for w in wins]
    return parts[0] if len(parts) == 1 else jnp.concatenate(parts, axis=0)


def _stack_heads(t, lane):
    zero = jnp.zeros_like(t)
    return jnp.concatenate([jnp.where(lane < HEAD, t, zero), jnp.where(lane >= HEAD, t, zero)], axis=0)


def _unstack_heads(t2, lane):
    half = t2.shape[0] // 2
    return jnp.where(lane < HEAD, t2[0:half, :], t2[half:, :])


def _lanes_of(step):
    return pl.ds(pl.multiple_of(step * LANES, LANES), LANES)


def _whole_wait(buf, sem):
    whole = buf.at[pl.ds(PAD, S), :]
    return pltpu.make_async_copy(whole, whole, sem)


def _whole_waits(bufs, sems):
    return [_whole_wait(buf, sems.at[i]) for i, buf in enumerate(bufs)]


def _class_gather(views, bufs, sems, lanes):
    copies = []
    for i, (view, buf) in enumerate(zip(views, bufs)):
        if view.ndim == 2:
            copies.append(pltpu.make_async_copy(view.at[:, lanes], buf.at[pl.ds(PAD, S), :], sems.at[i]))
        else:
            per, n_cls = view.shape[0], view.shape[1]
            copies += [pltpu.make_async_copy(view.at[:, c, lanes], buf.at[pl.ds(PAD + c * per, per), :], sems.at[i])
                       for c in range(n_cls)]
    return copies


def _class_scatter(bufs, dsts, sems, lanes):
    copies = []
    for i, (buf, dst) in enumerate(zip(bufs, dsts)):
        if dst.ndim == 2:
            copies.append(pltpu.make_async_copy(buf.at[pl.ds(PAD, S), :], dst.at[:, lanes], sems.at[i]))
            continue
        per, n_cls = dst.shape[0], dst.shape[1]
        copies += [pltpu.make_async_copy(buf.at[pl.ds(PAD + c * per, per), :], dst.at[:, c, lanes], sems.at[i])
                   for c in range(n_cls)]
    return copies


def _start(copies):
    for cp in copies:
        cp.start()


def _wait(waits):
    for w in waits:
        w.wait()


def _attn_fwd(q, kvp, shards=()):
    views = [[a] + [a.reshape(S // n, n, AW) for _, n, _, _ in ATTN_PLANS[1:]] for a in (q, kvp)]
    flat = [views[a][p] for p in range(3) for a in range(2)]
    ng = len(shards)
    n_grid = AW // LANES

    def body(*refs):
        hbm = [refs[2 * p:2 * p + 2] for p in range(3)]
        refs = refs[6:]
        shard_refs, refs = refs[:ng], refs[ng:]
        y_ref, lt_ref = refs[0:2]
        by_class = (refs[2:4], refs[4:6])
        whole_refs, refs = refs[6:6 + ng], refs[6 + ng:]
        bufs = [refs[2 * p:2 * p + 2] for p in range(3)]
        oc4, lc4, oc16, lc16, tab128, tab4, sem_in, sem_out = refs[6:14]
        step = pl.program_id(0)
        if ng:
            start_gather, relay_gather, finish_gather = _gather_steps(shard_refs, whole_refs, *refs[14:])
            pl.when(step == 0)(start_gather)
            pl.when(step == n_grid // 2)(relay_gather)
        now = [_class_gather(hbm[p], bufs[p], sem_in.at[p], _lanes_of(step)) for p in range(3)]
        nxt = [_class_gather(hbm[p], bufs[p], sem_in.at[p], _lanes_of(step + 1)) for p in range(3)]

        @pl.when(step == 0)
        def _():
            for p in range(3):
                _start(now[p])
                for b in bufs[p]:
                    b[0:PAD, :] = jnp.zeros((PAD, LANES), F32)
            _fill_bias(tab128, 128, False)
            _fill_bias(tab4, 64, True)

        def prefetch(p):
            pl.when(step + 1 < n_grid)(lambda: _start(nxt[p]))

        lane = lax.broadcasted_iota(jnp.int32, (1, LANES), 1)
        ones = jnp.ones((WIN, LANES), BF16)

        def run(plan, bq, bkv, tab, o_dst, l_dst, dst_pad):
            _, n_cls, qblk, nbc = plan
            partner = n_cls == 8

            def block(g, carry):
                own, wins, mask = _block_rows(g, qblk, nbc, partner)
                q2 = _stack_heads(bq[own, :].astype(BF16), lane)
                kw, vwin = _unpack_pair(_window(bkv, wins))
                vw = jnp.concatenate([vwin, ones], axis=1)
                s = _dot_nt(q2, kw) + tab[mask]
                m = jnp.max(s, axis=1, keepdims=True)
                oe = _dot(jnp.exp(s - m).astype(BF16), vw)
                den = oe[:, LANES:]
                dst = pl.ds(pl.multiple_of(dst_pad + g * qblk, qblk), qblk)
                o_dst[dst, :] = _unstack_heads(oe[:, 0:LANES] / den, lane)
                l_dst[dst, :] = _unstack_heads(m + jnp.log(den), lane)
                return carry
            lax.fori_loop(0, n_cls * nbc, block, 0, unroll=ATTN_UNROLL)

        _wait(_whole_waits(bufs[0], sem_in.at[0]))
        run(ATTN_PLANS[0], *bufs[0], tab128, y_ref, lt_ref, 0)
        prefetch(0)
        _wait(_whole_waits(bufs[1], sem_in.at[1]))
        run(ATTN_PLANS[1], *bufs[1], tab4, oc4, lc4, PAD)
        prefetch(1)
        _start(_class_scatter((oc4, lc4), by_class[0], sem_out.at[0], _lanes_of(step)))
        _wait(_whole_waits(bufs[2], sem_in.at[2]))
        run(ATTN_PLANS[2], *bufs[2], tab128, oc16, lc16, PAD)
        prefetch(2)
        _start(_class_scatter((oc16, lc16), by_class[1], sem_out.at[1], _lanes_of(step)))
        _wait(_whole_waits((oc4, lc4), sem_out.at[0]) + _whole_waits((oc16, lc16), sem_out.at[1]))
        if ng:
            pl.when(step == n_grid - 1)(finish_gather)

    col = pl.BlockSpec((S, LANES), lambda h: (0, h))
    padded = pltpu.VMEM((PAD + S, LANES), F32)
    by_class_shapes = [jax.ShapeDtypeStruct(views[0][p].shape, F32) for p in (1, 2) for _ in range(2)]
    out = pl.pallas_call(
        body, grid=(n_grid,), name="attn_fwd",
        in_specs=[ANY] * (6 + ng), out_specs=[col, col] + [ANY] * (4 + ng),
        out_shape=[jax.ShapeDtypeStruct((S, AW), F32)] * 2 + by_class_shapes + _gathered_shapes(shards),
        scratch_shapes=[padded] * 10 + [
            pltpu.VMEM((4, 256, WIN), F32), pltpu.VMEM((4, 128, WIN), F32),
            pltpu.SemaphoreType.DMA((3, 2)), pltpu.SemaphoreType.DMA((2, 2))]
        + (_gather_scratch(ng) if ng else []),
        compiler_params=_cparams(56))(*flat, *shards)
    return [o.reshape(S, AW) for o in out[:6]] + list(out[6:])


def _conv_taps(z, zprev, row):
    z1 = jnp.where(row == 0, zprev[7:8, :], pltpu.roll(z, 1, 0))
    z2 = jnp.where(row == 0, zprev[6:7, :], jnp.where(row == 1, zprev[7:8, :], pltpu.roll(z, 2, 0)))
    return z1, z2


def _xattn_scores(qm, km):
    s = _dot_nt(qm, km)
    m = jnp.max(s, axis=1, keepdims=True)
    e = jnp.exp(s - m)
    return e, jnp.sum(e, axis=1, keepdims=True)


def _mix_out(patterns, bcu, qx16, kv16, cw8, g_attn, g_conv, g_x, g_post, wout16, x, shards):
    def body(o1_ref, l1_ref, o4_ref, l4_ref, o16_ref, l16_ref, bcu_ref, halo_ref, qx_ref, kv_ref, cw_ref, ga_ref,
             gc_ref, gx_ref, gp_ref, w_ref, x_ref, ypre_ref, y16_ref, y2_ref, x1_ref, lt_ref):
        i = pl.program_id(0)
        bcu = bcu_ref[...]
        b, c, u = bcu[:, 0:CW], bcu[:, CW:2 * CW], bcu[:, 2 * CW:]
        z = c * u
        halo = halo_ref[...]
        zprev = jnp.where(i > 0, halo[:, CW:2 * CW] * halo[:, 2 * CW:], 0.0)
        row = lax.broadcasted_iota(jnp.int32, z.shape, 0)
        z1, z2 = _conv_taps(z, zprev, row)
        cw = cw_ref[...]
        y_conv = b * (z2 * cw[0:1, :] + z1 * cw[1:2, :] + z * cw[2:3, :])

        qx = qx_ref[...]
        kv = kv_ref[...]
        km, vm = kv[:, 0:XW], kv[:, XW:]
        lane = lax.broadcasted_iota(jnp.int32, qx.shape, 1)
        y_x = jnp.zeros(qx.shape, F32)
        for h in range(XW // HEAD):
            hm = (lane >= h * HEAD) & (lane < (h + 1) * HEAD)
            e, l = _xattn_scores(jnp.where(hm, qx, jnp.zeros_like(qx)), km)
            y_x = jnp.where(hm, _dot(e.astype(BF16), vm) / l, y_x)

        l1, l4, l16 = l1_ref[...], l4_ref[...], l16_ref[...]
        lm = jnp.maximum(jnp.maximum(l1, l4), l16)
        e1, e4, e16 = jnp.exp(l1 - lm), jnp.exp(l4 - lm), jnp.exp(l16 - lm)
        den = e1 + e4 + e16
        y_attn = (e1 * o1_ref[...] + e4 * o4_ref[...] + e16 * o16_ref[...]) / den
        lt_ref[...] = lm + jnp.log(den)
        ypre_ref[:, 0:AW] = y_attn
        ypre_ref[:, AW:AW + CW] = y_conv
        ypre_ref[:, AW + CW:] = y_x
        y = jnp.concatenate([_rms(y_attn, ga_ref[...])[0], _rms(y_conv, gc_ref[...])[0],
                             _rms(y_x, gx_ref[...])[0]], axis=1).astype(BF16)
        y16_ref[...] = y
        y2 = _dot(y, w_ref[...])
        y2_ref[...] = y2
        x1_ref[...] = x_ref[...] + _rms(y2, gp_ref[...])[0]

    def tile(w):
        return pl.BlockSpec((TQ, w), lambda i: (i, 0))

    halo = pl.BlockSpec((SUBLANES, 3 * CW), lambda i: (jnp.maximum(i * (TQ // SUBLANES) - 1, 0), 0))
    return _call_with_gather(
        body, NT, shards, name="mix_out",
        in_specs=[tile(AW)] * 6 + [tile(3 * CW), halo, tile(XW), _const((N_MEM, 2 * XW)), _const((SUBLANES, CW)),
                                   _const((1, AW)), _const((1, CW)), _const((1, XW)), _const((1, D)), _const((D, D)),
                                   tile(D)],
        out_specs=[tile(D), tile(D), tile(D), tile(D), tile(AW)],
        out_shape=[jax.ShapeDtypeStruct((S, D), F32), jax.ShapeDtypeStruct((S, D), BF16),
                   jax.ShapeDtypeStruct((S, D), F32), jax.ShapeDtypeStruct((S, D), F32),
                   jax.ShapeDtypeStruct((S, AW), F32)],
        scratch_shapes=[], vmem_mb=56,
        args=(*patterns, bcu, bcu, qx16, kv16, cw8, g_attn, g_conv, g_x, g_post, wout16, x))


def _mlp(x1, tgt, g_pre, g_post, wup8, wdn_halves):
    tq = TQ_MLP
    half = D // 2

    def body(x1_ref, t_ref, g1_ref, g2_ref, wu_ref, wda_ref, wdb_ref,
             a16_ref, du_ref, h2_ref, df2_ref, dx1_ref, loss_ref, dg_ref):
        @pl.when(pl.program_id(0) == 0)
        def _():
            loss_ref[...] = jnp.zeros_like(loss_ref)
            dg_ref[...] = jnp.zeros_like(dg_ref)

        x1 = x1_ref[...]
        g1, g2 = g1_ref[...], g2_ref[...]
        y1, n1, r1 = _rms(x1, g1)
        h2 = y1.astype(BF16)
        h2_ref[...] = h2
        f2a = jnp.zeros((tq, half), F32)
        f2b = jnp.zeros((tq, half), F32)
        for j in range(N_DEV):
            cols = slice(j * FF_BLK, (j + 1) * FF_BLK)
            a = jnp.maximum(_dot(h2, wu_ref[j]), 0.0)
            a16_ref[:, cols] = a.astype(BF16)
            f = (a * a).astype(BF16)
            f2a = f2a + _dot(f, wda_ref[cols, :])
            f2b = f2b + _dot(f, wdb_ref[cols, :])
        f2 = jnp.concatenate([f2a, f2b], axis=1)
        y2, n2, r2 = _rms(f2, g2)
        e = x1 + y2 - t_ref[...]
        sq = jnp.sum(jnp.sum(e * e, axis=1, keepdims=True), axis=0, keepdims=True)
        loss_ref[...] += jnp.broadcast_to(sq * (0.5 / D), loss_ref.shape)
        dout = e * (1.0 / D)
        df2, dg2 = _rms_bwd(dout, n2, r2, g2)
        df2_16 = df2.astype(BF16)
        df2_ref[...] = df2_16
        dh2 = jnp.zeros((tq, D), F32)
        for j in range(N_DEV):
            cols = slice(j * FF_BLK, (j + 1) * FF_BLK)
            df = _dot_nt(df2_16[:, 0:half], wda_ref[cols, :]) + _dot_nt(df2_16[:, half:], wdb_ref[cols, :])
            du = (df * (2.0 * a16_ref[:, cols].astype(F32))).astype(BF16)
            du_ref[:, cols] = du
            dh2 = dh2 + _dot_nt(du, wu_ref[j])
        dx, dg1 = _rms_bwd(dh2, n1, r1, g1)
        dx1_ref[...] = dout + dx
        dg_ref[0:1, :] += dg2
        dg_ref[1:2, :] += dg1

    def tile(w):
        return pl.BlockSpec((tq, w), lambda i: (i, 0))

    return pl.pallas_call(
        body, grid=(S // tq,), name="mlp",
        in_specs=[tile(D), tile(D), _const((1, D)), _const((1, D)), _const((N_DEV, D, FF_BLK)), _const((FF, half)), _const((FF, half))],
        out_specs=[tile(FF), tile(FF), tile(D), tile(D), tile(D), _acc((SUBLANES, LANES)), _acc((SUBLANES, D))],
        out_shape=[jax.ShapeDtypeStruct((S, FF), BF16), jax.ShapeDtypeStruct((S, FF), BF16),
                   jax.ShapeDtypeStruct((S, D), BF16), jax.ShapeDtypeStruct((S, D), BF16),
                   jax.ShapeDtypeStruct((S, D), F32), jax.ShapeDtypeStruct((SUBLANES, LANES), F32),
                   jax.ShapeDtypeStruct((SUBLANES, D), F32)],
        compiler_params=_cparams(56))(x1, tgt, g_pre, g_post, wup8, *wdn_halves)


def _mix_out_bwd(dx1, y2, ypre, ltot, head_ones, q, bcu, qx16, kv16, cw8, g_post, g_attn, g_conv, g_x, wout16):
    def body(dx1_ref, y2_ref, ypre_ref, lt_ref, e_ref, q_ref, bcu_ref, halo_ref, qx_ref, kv_ref, cw_ref, gp_ref,
             ga_ref, gc_ref, gx_ref, w_ref, dy2_ref, qdo_ref, ld_ref, dbcu_ref, dqx_ref, dgs_ref, dcw_ref, dkv_ref,
             carry):
        i = pl.program_id(0)

        @pl.when(i == 0)
        def _():
            dgs_ref[...] = jnp.zeros_like(dgs_ref)
            dcw_ref[...] = jnp.zeros_like(dcw_ref)
            dkv_ref[...] = jnp.zeros_like(dkv_ref)
            carry[...] = jnp.zeros_like(carry)

        gp = gp_ref[...]
        _, n, r = _rms(y2_ref[...], gp)
        dy2, dgp = _rms_bwd(dx1_ref[...], n, r, gp)
        dy2_16 = dy2.astype(BF16)
        dy2_ref[...] = dy2_16
        dy = _dot_nt(dy2_16, w_ref[...])

        ypre = ypre_ref[...]
        ga, gc, gx = ga_ref[...], gc_ref[...], gx_ref[...]
        _, na, ra = _rms(ypre[:, 0:AW], ga)
        dya, dga = _rms_bwd(dy[:, 0:AW], na, ra, ga)
        _, nc, rc = _rms(ypre[:, AW:AW + CW], gc)
        dyc, dgc = _rms_bwd(dy[:, AW:AW + CW], nc, rc, gc)
        y_x = ypre[:, AW + CW:]
        _, nx, rx = _rms(y_x, gx)
        dyx, dgx = _rms_bwd(dy[:, AW + CW:], nx, rx, gx)
        qdo_ref[...] = _pack_pair(q_ref[...], dya)
        prod = dya * ypre[:, 0:AW]
        hi = prod.astype(BF16)
        lo = (prod - hi.astype(F32)).astype(BF16)
        head_sum = _dot(hi, e_ref[...]) + _dot(lo, e_ref[...])
        lane_a = lax.broadcasted_iota(jnp.int32, prod.shape, 1)
        ld_ref[...] = jnp.where((lane_a % HEAD) < HEAD // 2, lt_ref[...], head_sum)
        dgs_ref[0:1, :] += dgp
        dgs_ref[1:2, :] += jnp.concatenate([dga, dgc, dgx], axis=1)

        bcu = bcu_ref[...]
        b, c, u = bcu[:, 0:CW], bcu[:, CW:2 * CW], bcu[:, 2 * CW:]
        z = c * u
        halo = halo_ref[...]
        zprev = jnp.where(i < NT - 1, halo[:, CW:2 * CW] * halo[:, 2 * CW:], 0.0)
        row = lax.broadcasted_iota(jnp.int32, z.shape, 0)
        z1, z2 = _conv_taps(z, zprev, row)
        cw = cw_ref[...]
        conv = z2 * cw[0:1, :] + z1 * cw[1:2, :] + z * cw[2:3, :]
        dconv = dyc * b
        nxt = carry[...]
        dn1 = jnp.where(row == TQ - 1, nxt[0:1, :], pltpu.roll(dconv, TQ - 1, 0))
        dn2 = jnp.where(row == TQ - 1, nxt[1:2, :], jnp.where(row == TQ - 2, nxt[0:1, :], pltpu.roll(dconv, TQ - 2, 0)))
        carry[...] = dconv[0:SUBLANES, :]
        dz = dconv * cw[2:3, :] + dn1 * cw[1:2, :] + dn2 * cw[0:1, :]
        dbcu_ref[:, 0:CW] = (dyc * conv).astype(BF16)
        dbcu_ref[:, CW:2 * CW] = (dz * u).astype(BF16)
        dbcu_ref[:, 2 * CW:] = (dz * c).astype(BF16)
        dcw_ref[0:1, :] += jnp.sum(z2 * dconv, axis=0, keepdims=True)
        dcw_ref[1:2, :] += jnp.sum(z1 * dconv, axis=0, keepdims=True)
        dcw_ref[2:3, :] += jnp.sum(z * dconv, axis=0, keepdims=True)

        qx = qx_ref[...]
        kv = kv_ref[...]
        km, vm = kv[:, 0:XW], kv[:, XW:]
        lane = lax.broadcasted_iota(jnp.int32, qx.shape, 1)
        dqx = jnp.zeros(qx.shape, F32)
        dkm = jnp.zeros((N_MEM, XW), F32)
        dvm = jnp.zeros((N_MEM, XW), F32)
        for h in range(XW // HEAD):
            hm = (lane >= h * HEAD) & (lane < (h + 1) * HEAD)
            qm = jnp.where(hm, qx, jnp.zeros_like(qx))
            e, l = _xattn_scores(qm, km)
            p = e / l
            dom = jnp.where(hm, dyx, 0.0)
            do16 = dom.astype(BF16)
            dsum = jnp.sum(dom * y_x, axis=1, keepdims=True)
            ds = (p * (_dot_nt(do16, vm) - dsum)).astype(BF16)
            dqx = jnp.where(hm, _dot(ds, km), dqx)
            dkm = dkm + _dot_tn(ds, qm)
            dvm = dvm + _dot_tn(p.astype(BF16), do16)
        dqx_ref[...] = (dqx * SCALE).astype(BF16)
        dkv_ref[:, 0:XW] += dkm
        dkv_ref[:, XW:] += dvm

    def tile(w):
        return pl.BlockSpec((TQ, w), lambda i: (NT - 1 - i, 0))

    halo = pl.BlockSpec((SUBLANES, 3 * CW), lambda i: (jnp.maximum((NT - 1 - i) * (TQ // SUBLANES) - 1, 0), 0))
    return pl.pallas_call(
        body, grid=(NT,), name="mix_out_bwd",
        in_specs=[tile(D), tile(D), tile(D), tile(AW), _const((AW, AW)), tile(AW), tile(3 * CW), halo, tile(XW),
                  _const((N_MEM, 2 * XW)), _const((SUBLANES, CW)), _const((1, D)), _const((1, AW)), _const((1, CW)),
                  _const((1, XW)), _const((D, D))],
        out_specs=[tile(D), tile(AW), tile(AW), tile(3 * CW), tile(XW), _acc((SUBLANES, D)), _acc((SUBLANES, CW)),
                   _acc((N_MEM, 2 * XW))],
        out_shape=[jax.ShapeDtypeStruct((S, D), BF16), jax.ShapeDtypeStruct((S, AW), F32),
                   jax.ShapeDtypeStruct((S, AW), F32),
                   jax.ShapeDtypeStruct((S, 3 * CW), BF16), jax.ShapeDtypeStruct((S, XW), BF16),
                   jax.ShapeDtypeStruct((SUBLANES, D), F32), jax.ShapeDtypeStruct((SUBLANES, CW), F32),
                   jax.ShapeDtypeStruct((N_MEM, 2 * XW), F32)],
        scratch_shapes=[pltpu.VMEM((SUBLANES, CW), F32)],
        compiler_params=_cparams(56))(dx1, y2, ypre, ltot, head_ones, q, bcu, bcu, qx16, kv16, cw8, g_post, g_attn,
                                      g_conv, g_x, wout16)


def _attn_bwd(qdo, kvp, ld, chip_sums=()):
    n_in = 3
    views = [[a] + [a.reshape(S // n, n, AW) for _, n, _, _ in ATTN_PLANS[1:]] for a in (qdo, kvp, ld)]
    flat = [views[a][p] for p in range(3) for a in range(n_in)]
    ns = len(chip_sums)
    n_grid = AW // LANES

    def body(*refs):
        hbm = [refs[n_in * p:n_in * p + n_in] for p in range(3)]
        refs = refs[3 * n_in:]
        sum_refs, refs = refs[:ns], refs[ns:]
        outs = [refs[3 * p:3 * p + 3] for p in range(3)]
        landed_refs, sc = refs[9:9 + ns], refs[9 + ns:]
        bufs = [sc[3 * p:3 * p + 3] for p in range(3)]
        res = [sc[9 + 3 * p:12 + 3 * p] for p in range(3)]
        tab128, tab4, sem_in, sem_out = sc[18:22]
        step = pl.program_id(0)
        if ns:
            start_chips, finish_chips = _chips_steps(sum_refs, landed_refs, *sc[22:])
            pl.when(step == 0)(start_chips)
        now = [_class_gather(hbm[p], bufs[p], sem_in.at[p], _lanes_of(step)) for p in range(3)]
        nxt = [_class_gather(hbm[p], bufs[p], sem_in.at[p], _lanes_of(step + 1)) for p in range(3)]

        @pl.when(step == 0)
        def _():
            for p in range(3):
                _start(now[p])
                for b in bufs[p]:
                    b[0:PAD, :] = jnp.zeros((PAD, LANES), F32)
            _fill_bias(tab128, 128, False)
            _fill_bias(tab4, 64, True)

        def prefetch(p):
            pl.when(step + 1 < n_grid)(lambda: _start(nxt[p]))

        for p in range(3):
            for b in res[p][1:]:
                b[...] = jnp.zeros_like(b)
        lane = lax.broadcasted_iota(jnp.int32, (1, LANES), 1)

        def run(plan, plan_bufs, tab, dst):
            _, n_cls, qblk, nbc = plan
            partner = n_cls == 8
            bqdo, bkv, bld = plan_bufs
            rq, rk, rv = dst

            def block(g, carry):
                own, wins, mask = _block_rows(g, qblk, nbc, partner)
                qb, dob = _unpack_pair(bqdo[own, :])
                q2, do2 = _stack_heads(qb, lane), _stack_heads(dob, lane)
                kw, vw = _unpack_pair(_window(bkv, wins))
                ldv = bld[own, :]
                half = HEAD // 2
                lt2 = jnp.concatenate([ldv[:, 0:1], ldv[:, HEAD:HEAD + 1]], axis=0)
                dsum2 = jnp.concatenate([ldv[:, half:half + 1], ldv[:, HEAD + half:HEAD + half + 1]], axis=0)
                p = jnp.exp(_dot_nt(q2, kw) + tab[mask] - lt2)
                ds = (p * (_dot_nt(do2, vw) - dsum2)).astype(BF16)
                rq[own, :] = _unstack_heads(_dot(ds, kw), lane)
                dkw = _dot_tn(ds, q2)
                dvw = _dot_tn(p.astype(BF16), do2)
                n_w = WIN // len(wins)
                for i, w in enumerate(wins):
                    rk[w, :] += dkw[i * n_w:(i + 1) * n_w, :]
                    rv[w, :] += dvw[i * n_w:(i + 1) * n_w, :]
                return carry
            lax.fori_loop(0, n_cls * nbc, block, 0, unroll=ATTN_UNROLL)

        tabs = (tab128, tab4, tab128)
        for p in range(3):
            _wait(_whole_waits(bufs[p], sem_in.at[p]))
            run(ATTN_PLANS[p], bufs[p], tabs[p], res[p])
            prefetch(p)
            _start(_class_scatter(res[p], outs[p], sem_out.at[p], _lanes_of(step)))
        for p in range(3):
            _wait(_whole_waits(res[p], sem_out.at[p]))
        if ns:
            pl.when(step == n_grid - 1)(finish_chips)

    padded = pltpu.VMEM((PAD + S, LANES), F32)
    shapes = [jax.ShapeDtypeStruct(views[0][p].shape, F32) for p in range(3) for _ in range(3)]
    out = pl.pallas_call(
        body, grid=(n_grid,), name="attn_bwd",
        in_specs=[ANY] * (3 * n_in + ns), out_specs=[ANY] * (9 + ns),
        out_shape=shapes + _chips_shapes(chip_sums),
        scratch_shapes=[padded] * 18
        + [pltpu.VMEM((4, 256, WIN), F32), pltpu.VMEM((4, 128, WIN), F32),
           pltpu.SemaphoreType.DMA((3, n_in)), pltpu.SemaphoreType.DMA((3, 3))]
        + (_chips_scratch(ns) if ns else []),
        compiler_params=_cparams(56))(*flat, *chip_sums)
    return [o.reshape(S, AW) for o in out[:9]] + list(out[9:])


def _in_proj_bwd(dqkv, dbcu, dqx, cos, sins, w16, x, g, dx1):
    tq = TQ // 2

    def body(*refs):
        parts = refs[0:9]
        dbcu_ref, dqx_ref, c_ref, s_ref, w_ref, x_ref, g_ref, dx1_ref, dp_ref, gx_ref, dg_ref = refs[9:]

        @pl.when(pl.program_id(0) == 0)
        def _():
            dg_ref[...] = jnp.zeros_like(dg_ref)

        dq, dk, dv = (parts[i][...] + parts[3 + i][...] + parts[6 + i][...] for i in range(3))
        cos, sn = _all_heads(c_ref[...]), _all_heads(s_ref[...])
        dqr = dq * SCALE
        dkr = dk
        dp = jnp.concatenate([(dqr * cos + _rot_half(dqr * sn)).astype(BF16),
                              (dkr * cos + _rot_half(dkr * sn)).astype(BF16), dv.astype(BF16),
                              dbcu_ref[...], dqx_ref[...]], axis=1)
        dp_ref[...] = dp
        dh = _dot_nt(dp, w_ref[...])
        g = g_ref[...]
        _, n, r = _rms(x_ref[...], g)
        dx, dg = _rms_bwd(dh, n, r, g)
        gx_ref[...] = dx1_ref[...] + dx
        dg_ref[0:1, :] += dg

    def tile(w):
        return pl.BlockSpec((tq, w), lambda i: (i, 0))

    return pl.pallas_call(
        body, grid=(S // tq,), name="in_proj_bwd",
        in_specs=[tile(AW)] * 9 + [tile(3 * CW), tile(XW), tile(LANES), tile(LANES), _const((D, PW)),
                                   tile(D), _const((1, D)), tile(D)],
        out_specs=[tile(PW), tile(D), _acc((SUBLANES, D))],
        out_shape=[jax.ShapeDtypeStruct((S, PW), BF16), jax.ShapeDtypeStruct((S, D), F32),
                   jax.ShapeDtypeStruct((SUBLANES, D), F32)],
        compiler_params=_cparams(56))(*dqkv, dbcu, dqx, cos, sins, w16, x, g, dx1)


def _mem_bwd(mem, g_mem, wkv16, dkv):
    def body(m_ref, g_ref, w_ref, dkv_ref, dkv16_ref, dg_ref):
        dkv16 = dkv_ref[...].astype(BF16)
        dkv16_ref[...] = dkv16
        _, n, _ = _rms(m_ref[...], g_ref[...])
        dg = jnp.sum(_dot_nt(dkv16, w_ref[...]) * n, axis=0, keepdims=True)
        dg_ref[...] = jnp.broadcast_to(dg, dg_ref.shape)

    return pl.pallas_call(
        body, name="mem_bwd",
        out_shape=[jax.ShapeDtypeStruct((N_MEM, 2 * XW), BF16), jax.ShapeDtypeStruct((SUBLANES, D), F32)],
        compiler_params=pltpu.CompilerParams(vmem_limit_bytes=32 << 20))(mem, g_mem, wkv16, dkv)


N_CHIPS = N_DEV // 2


def _transpose_into(at, a_ref):
    kk = a_ref.shape[0]
    chunk = min(kk, 512)
    for c in range(kk // chunk):
        at[:, c * chunk:(c + 1) * chunk] = a_ref[c * chunk:(c + 1) * chunk, :].T


def _pair_scratch(block):
    return [pltpu.VMEM((N_CHIPS,) + block, BF16), pltpu.VMEM((N_CHIPS,) + block, BF16),
            pltpu.SemaphoreType.DMA((N_CHIPS,)), pltpu.SemaphoreType.DMA((N_CHIPS,))]


def _swap_with_sibling(p, stage, land, send, recv):
    x, y, c = lax.axis_index("x"), lax.axis_index("y"), lax.axis_index("c")
    return pltpu.make_async_remote_copy(src_ref=stage.at[p], dst_ref=land.at[p], send_sem=send.at[p],
                                        recv_sem=recv.at[p], device_id=(x, y, 1 - c), device_id_type=MESH)


def _wgrad_cols(place, a16, b16, blk, name, square_b=False, transpose_out=False, to_chips=False, small=()):
    kk, m = a16.shape
    aligned = blk % LANES == 0
    wide = blk if aligned else -(-(blk + LANES // 2) // LANES) * LANES
    block = (blk, m) if transpose_out else (m, blk)

    def chip_of(step, my_chip):
        return jnp.bitwise_xor(my_chip, N_CHIPS - 1 - step) if to_chips else step

    def body(pl_ref, a_ref, *refs):
        b_refs, refs = refs[:2 if aligned else 1], refs[2 if aligned else 1:]
        accs, refs = refs[:len(small)], refs[len(small):]
        (cs_ref, own_ref), refs = refs[:2], refs[2:]
        if to_chips:
            landed, refs = refs[0], refs[1:]
        if small:
            tot_ref, refs = refs[0], refs[1:]
        (at, stage, land, send, recv), refs = refs[:5], refs[5:]
        if not aligned:
            (win, wsem), refs = refs[:2], refs[2:]
        if small:
            start_small, finish_small = _small_reduce_steps(accs, tot_ref, *refs[-4:])
            refs = refs[:-4]
        step = pl.program_id(0)
        if small:
            pl.when(step == 0)(start_small)
        x, y, c = lax.axis_index("x"), lax.axis_index("y"), lax.axis_index("c")
        my_chip = 2 * x + y
        p = chip_of(step, my_chip)

        def fetch(at_step, mine):
            j = 2 * chip_of(at_step, my_chip) + (c if mine else 1 - c)
            first = pl.multiple_of(((j * blk) >> 7) << 7, LANES)
            slot = 2 * (at_step & 1) + mine
            return pltpu.make_async_copy(b_refs[0].at[:, pl.ds(first, wide)], win.at[slot], wsem.at[slot])

        @pl.when(step == 0)
        def _():
            if not aligned:
                fetch(0, 0).start()
                fetch(0, 1).start()
            _transpose_into(at, a_ref)

        if not aligned:
            @pl.when(step + 1 < N_CHIPS)
            def _():
                fetch(step + 1, 0).start()
                fetch(step + 1, 1).start()

        def partial(mine):
            if aligned:
                b = b_refs[mine][...]
                if square_b:
                    b = b * b
                acc = _dot(at[...], b)
            else:
                fetch(step, mine).wait()
                acc = _dot(at[...], win[2 * (step & 1) + mine])
                odd = c if mine else 1 - c
                acc = pltpu.roll(acc, jnp.where(odd == 0, 0, wide - LANES // 2), 1)[:, 0:blk]
            return acc.T if transpose_out else acc

        stage[p] = partial(0).astype(BF16)
        swap = _swap_with_sibling(p, stage, land, send, recv)
        swap.start()
        mine = partial(1)
        swap.wait()
        total = mine + land[p].astype(F32)
        cs_ref[0] = total.astype(BF16)

        @pl.when(p == my_chip)
        def _():
            own_ref[...] = total

        if to_chips:
            stage2, send2, recv2 = refs
            flipped = jnp.bitwise_xor(p, my_chip)
            k = jnp.where(flipped == 2, 0, jnp.where(flipped == 1, 1, 2))

            def to_owner(src, k_, px, py):
                return pltpu.make_async_remote_copy(src_ref=src, dst_ref=landed.at[k_], send_sem=send2.at[k_],
                                                    recv_sem=recv2.at[k_], device_id=(px, py, c), device_id_type=MESH)

            @pl.when(p != my_chip)
            def _():
                stage2[p] = total.astype(BF16)
                to_owner(stage2.at[p], k, p >> 1, p & 1).start()

            @pl.when(step == N_CHIPS - 1)
            def _():
                for k_ in range(N_CHIPS - 1):
                    to_owner(stage2.at[0], k_, x, y).wait()

        if small:
            pl.when(step == N_CHIPS - 1)(finish_small)

    def b_spec(mine):
        return pl.BlockSpec((kk, blk), lambda i, s: (0, 2 * chip_of(i, s[1]) + (s[0] if mine else 1 - s[0])))

    b_specs, b_args = ([b_spec(0), b_spec(1)], (b16, b16)) if aligned else ([ANY], (b16,))
    scratch = [pltpu.VMEM((m, kk), BF16)] + _pair_scratch(block)
    if not aligned:
        scratch += [pltpu.VMEM((4, kk, wide), BF16), pltpu.SemaphoreType.DMA((4,))]
    out_specs = [pl.BlockSpec((1,) + block, lambda i, s: (chip_of(i, s[1]), 0, 0)), pl.BlockSpec(block, lambda i, s: (0, 0))]
    out_shape = [jax.ShapeDtypeStruct((N_CHIPS,) + block, BF16), jax.ShapeDtypeStruct(block, F32)]
    if to_chips:
        out_specs.append(ANY)
        out_shape.append(jax.ShapeDtypeStruct((N_CHIPS - 1,) + block, BF16))
        scratch += [pltpu.VMEM((N_CHIPS,) + block, BF16), pltpu.SemaphoreType.DMA((N_CHIPS - 1,)),
                    pltpu.SemaphoreType.DMA((N_CHIPS - 1,))]
    small_specs = [pl.BlockSpec(a.shape, lambda i, s: (0, 0)) for a in small]
    if small:
        out_specs.append(pl.BlockSpec((PACK_ROWS, D), lambda i, s: (0, 0)))
        out_shape.append(jax.ShapeDtypeStruct((PACK_ROWS, D), F32))
        scratch += _small_reduce_scratch()
    return pl.pallas_call(
        body, name=name,
        grid_spec=pltpu.PrefetchScalarGridSpec(
            num_scalar_prefetch=1, grid=(N_CHIPS,),
            in_specs=[pl.BlockSpec((kk, m), lambda i, s: (0, 0), pipeline_mode=pl.Buffered(1))] + b_specs + small_specs,
            out_specs=out_specs, scratch_shapes=scratch),
        out_shape=out_shape, compiler_params=_cparams(56))(place, a16, *b_args, *small)


def _wgrad_rows(place, a16, b16, name):
    kk, m = a16.shape
    n = b16.shape[1]
    block = (m // N_DEV, n)

    def body(pl_ref, a_ref, b_ref, cs_ref, own_ref, at, acc, stage, land, send, recv):
        c = pl_ref[0]
        _transpose_into(at, a_ref)
        acc[...] = _dot(at[...], b_ref[...])

        def rows(owner):
            return pl.ds(pl.multiple_of(owner * block[0], block[0]), block[0])

        swaps = []
        for p in range(N_CHIPS):
            stage[p] = acc[rows(2 * p + 1 - c), :].astype(BF16)
            swaps.append(_swap_with_sibling(p, stage, land, send, recv))
            swaps[-1].start()
        for p in range(N_CHIPS):
            swaps[p].wait()
            total = acc[rows(2 * p + c), :] + land[p].astype(F32)
            cs_ref[p] = total.astype(BF16)

            @pl.when(p == pl_ref[1])
            def _():
                own_ref[...] = total

    vmem = pl.BlockSpec(memory_space=pltpu.VMEM)
    return pl.pallas_call(
        body, name=name,
        in_specs=[pl.BlockSpec(memory_space=pltpu.SMEM), vmem, vmem], out_specs=[vmem, vmem],
        out_shape=[jax.ShapeDtypeStruct((N_CHIPS,) + block, BF16), jax.ShapeDtypeStruct(block, F32)],
        scratch_shapes=[pltpu.VMEM((m, kk), BF16), pltpu.VMEM((m, n), F32)] + _pair_scratch(block),
        compiler_params=pltpu.CompilerParams(vmem_limit_bytes=56 << 20))(place, a16, b16)


def _adamw_math(w, g, m, v):
    m = ADAM_B1 * m + (1.0 - ADAM_B1) * g
    v = ADAM_B2 * v + (1.0 - ADAM_B2) * jnp.square(g)
    m_hat = m / (1.0 - ADAM_B1 ** ADAM_STEP)
    v_hat = v / (1.0 - ADAM_B2 ** ADAM_STEP)
    delta = -ADAM_LR * (m_hat / (jnp.sqrt(v_hat) + ADAM_EPS) + ADAM_WD * w)
    return delta, m, v


def _adamw_shards(updates, name, chip_sums=()):
    names, nu, ns = list(updates), len(updates), len(chip_sums)

    def body(*refs):
        ins, sum_refs = refs[:5 * nu], refs[5 * nu:5 * nu + ns]
        outs = refs[5 * nu + ns:9 * nu + ns]
        landed_refs, scratch = refs[9 * nu + ns:9 * nu + 2 * ns], refs[9 * nu + 2 * ns:]
        if ns:
            start_chips, finish_chips = _chips_steps(sum_refs, landed_refs, *scratch)
            start_chips()
        for i in range(nu):
            o_ref, r_ref, w_ref, m_ref, v_ref = ins[5 * i:5 * i + 5]
            g_out, d_out, m_out, v_out = outs[4 * i:4 * i + 4]
            g = o_ref[...] + r_ref[0].astype(F32) + r_ref[1].astype(F32) + r_ref[2].astype(F32)
            g_out[...] = g
            d_out[...], m_out[...], v_out[...] = _adamw_math(w_ref[...], g, m_ref[...], v_ref[...])
        if ns:
            finish_chips()

    vmem = pl.BlockSpec(memory_space=pltpu.VMEM)
    out = pl.pallas_call(
        body, name=name,
        in_specs=[vmem] * (5 * nu) + [ANY] * ns, out_specs=[vmem] * (4 * nu) + [ANY] * ns,
        out_shape=[jax.ShapeDtypeStruct(updates[n][2].shape, F32) for n in names for _ in range(4)]
        + _chips_shapes(chip_sums),
        scratch_shapes=_chips_scratch(ns) if ns else [],
        compiler_params=pltpu.CompilerParams(vmem_limit_bytes=56 << 20),
    )(*[a for n in names for a in updates[n]], *chip_sums)
    return {n: out[4 * i:4 * i + 4] for i, n in enumerate(names)}, list(out[4 * nu:])


def _place():
    x, y, c = lax.axis_index("x"), lax.axis_index("y"), lax.axis_index("c")
    chips = [(1 - x, y), (x, 1 - y), (1 - x, 1 - y)]
    return x, y, c, chips


def _gather_steps(ins, outs, send, recv, lsem):
    nt = len(ins)
    x, y, c, (xn, yn, diag) = _place()
    me, sib = (x, y, c), (x, y, 1 - c)

    def slot(t, px, py, pc):
        return outs[t].at[4 * px + 2 * py + pc]

    def copy(t, k, block, to, src=None):
        return pltpu.make_async_remote_copy(
            src_ref=slot(t, *block) if src is None else src, dst_ref=slot(t, *block),
            send_sem=send.at[t, k], recv_sem=recv.at[t, k], device_id=to, device_id_type=MESH)

    mine = [pltpu.make_async_copy(ins[t], slot(t, *me), lsem.at[t]) for t in range(nt)]
    first = [copy(t, k, me, to, src=ins[t]) for t in range(nt) for k, to in ((0, sib), (1, (*xn, c)), (2, (*yn, c)))]

    def start():
        for cp in mine + first:
            cp.start()

    def landed(k, chip, also_to=None):
        for t in range(nt):
            copy(t, k, (*chip, c), me).wait_recv()
            if also_to is not None:
                copy(t, 3, (*chip, c), (*also_to, c)).start()
            copy(t, 3 + k, (*chip, c), sib).start()

    def relay():
        @pl.when(c == 0)
        def _():
            landed(1, xn, also_to=yn)
            landed(2, yn)

        @pl.when(c == 1)
        def _():
            landed(2, yn, also_to=xn)
            landed(1, xn)

    def finish():
        landed(3, diag)
        for t in range(nt):
            copy(t, 0, sib, me).wait_recv()
            for k, chip in ((4, xn), (5, yn), (6, diag)):
                copy(t, k, (*chip, 1 - c), me).wait_recv()
            for k in range(7):
                copy(t, k, me, sib).wait_send()
        for cp in mine:
            cp.wait()

    return start, relay, finish


def _gather_scratch(nt):
    return [pltpu.SemaphoreType.DMA((nt, 7)), pltpu.SemaphoreType.DMA((nt, 7)), pltpu.SemaphoreType.DMA((nt,))]


def _gathered_shapes(shards):
    return [jax.ShapeDtypeStruct((N_DEV,) + s.shape, s.dtype) for s in shards]


def _call_with_gather(body, n_grid, shards, *, name, in_specs, out_specs, out_shape, scratch_shapes, vmem_mb, args):
    ng, n_in, n_out = len(shards), len(in_specs), len(out_specs)

    def wrapped(*refs):
        ins, shard_refs = refs[:n_in], refs[n_in:n_in + ng]
        outs = refs[n_in + ng:n_in + ng + n_out]
        whole_refs = refs[n_in + ng + n_out:n_in + 2 * ng + n_out]
        scratch = refs[n_in + 2 * ng + n_out:]
        if ng:
            start, relay, finish = _gather_steps(shard_refs, whole_refs, *scratch[len(scratch_shapes):])
            pl.when(pl.program_id(0) == 0)(start)
            pl.when(pl.program_id(0) == n_grid // 2)(relay)
        body(*ins, *outs, *scratch[:len(scratch_shapes)])
        if ng:
            pl.when(pl.program_id(0) == n_grid - 1)(finish)

    return pl.pallas_call(
        wrapped, grid=(n_grid,), name=name,
        in_specs=list(in_specs) + [ANY] * ng, out_specs=list(out_specs) + [ANY] * ng,
        out_shape=list(out_shape) + _gathered_shapes(shards),
        scratch_shapes=list(scratch_shapes) + (_gather_scratch(ng) if ng else []),
        compiler_params=_cparams(vmem_mb))(*args, *shards)


def _chips_steps(ins, outs, send, recv):
    _, _, c, chips = _place()
    copies = [pltpu.make_async_remote_copy(
        src_ref=ins[t].at[2 * px + py], dst_ref=outs[t].at[j], send_sem=send.at[t, j], recv_sem=recv.at[t, j],
        device_id=(px, py, c), device_id_type=MESH) for t in range(len(ins)) for j, (px, py) in enumerate(chips)]

    def start():
        for cp in copies:
            cp.start()

    def finish():
        for cp in copies:
            cp.wait()

    return start, finish


def _chips_scratch(nt):
    return [pltpu.SemaphoreType.DMA((nt, 3)), pltpu.SemaphoreType.DMA((nt, 3))]


def _chips_shapes(cs16s):
    return [jax.ShapeDtypeStruct((3,) + g.shape[1:], g.dtype) for g in cs16s]


SMALL = (("g_pre_mix", 0, 0, D), ("g_mem", 1, 0, D), ("g_post_mix", 2, 0, D), ("g_attn_out", 3, 0, AW),
         ("g_conv_out", 3, AW, CW), ("g_xattn_out", 3, AW + CW, XW), ("g_post_mlp", 4, 0, D), ("g_pre_mlp", 5, 0, D))
CONV_ROW = 8
PACK_ROWS = 16


LOSS_ROW = 15


def _small_reduce_steps(accs, tot_ref, pack, land, send, recv):
    acc_in, acc_mem, acc_mix, acc_mlp, acc_cw, acc_loss = accs
    x, y, c, _ = _place()
    me = 4 * x + 2 * y + c
    copies = []
    for k in range(1, N_DEV):
        kx, ky, kc = (k >> 2) & 1, (k >> 1) & 1, k & 1
        peer = (1 - x if kx else x, 1 - y if ky else y, 1 - c if kc else c)
        copies.append(pltpu.make_async_remote_copy(
            src_ref=pack, dst_ref=land.at[me], send_sem=send.at[k - 1], recv_sem=recv.at[k - 1],
            device_id=peer, device_id_type=MESH))

    def start():
        pack[...] = jnp.zeros_like(pack)
        pack[0:1, :] = acc_in[0:1, :]
        pack[1:2, :] = acc_mem[0:1, :]
        pack[2:4, :] = acc_mix[0:2, :]
        pack[4:6, :] = acc_mlp[0:2, :]
        pack[CONV_ROW:CONV_ROW + 3, 0:CW] = acc_cw[0:3, :]
        pack[LOSS_ROW:LOSS_ROW + 1, 0:LANES] = acc_loss[0:1, :]
        land[me] = pack[...]
        for cp in copies:
            cp.start()

    def finish():
        for cp in copies:
            cp.wait()
        tot = land[0]
        for s in range(1, N_DEV):
            tot = tot + land[s]
        tot_ref[...] = tot

    return start, finish


def _small_reduce_scratch():
    return [pltpu.VMEM((PACK_ROWS, D), F32), pltpu.VMEM((N_DEV, PACK_ROWS, D), F32),
            pltpu.SemaphoreType.DMA((N_DEV - 1,)), pltpu.SemaphoreType.DMA((N_DEV - 1,))]


def _small_update(tot, me, params):
    flat = [a for n, _, _, _ in SMALL for a in params[n]] + list(params["conv_w"])
    n_par = len(SMALL) + 1
    tap_cols = CW // N_DEV

    def body(*refs):
        me_ref, tot_ref = refs[0:2]
        ins = refs[2:2 + 3 * n_par]
        loss_out = refs[2 + 3 * n_par]
        outs = refs[3 + 3 * n_par:]
        tot = tot_ref[...]
        loss_out[...] = jnp.broadcast_to(tot[LOSS_ROW:LOSS_ROW + 1, 0:LANES], loss_out.shape)

        def update(i, g):
            w_ref, m_ref, v_ref = ins[3 * i:3 * i + 3]
            g_out, d_out, m_out, v_out = outs[4 * i:4 * i + 4]
            g_out[...] = g
            d_out[...], m_out[...], v_out[...] = _adamw_math(w_ref[...], g, m_ref[...], v_ref[...])

        for i, (_, row, lane0, width) in enumerate(SMALL):
            update(i, tot[row:row + 1, lane0:lane0 + width])
        me = me_ref[0]
        taps = pltpu.roll(tot[CONV_ROW:CONV_ROW + SUBLANES, 0:CW], jnp.where(me == 0, 0, CW - me * tap_cols), 1)
        update(n_par - 1, taps[0:3, 0:tap_cols])

    shapes = [jax.ShapeDtypeStruct(params[n][0].shape, F32) for n, _, _, _ in SMALL] + [
        jax.ShapeDtypeStruct(params["conv_w"][0].shape, F32)]
    vmem = pl.BlockSpec(memory_space=pltpu.VMEM)
    loss, *out = pl.pallas_call(
        body, name="small_update",
        in_specs=[pl.BlockSpec(memory_space=pltpu.SMEM)] + [vmem] * (1 + 3 * n_par),
        out_shape=[jax.ShapeDtypeStruct((SUBLANES, LANES), F32)] + [s for s in shapes for _ in range(4)],
    )(me, tot, *flat)
    names = [n for n, _, _, _ in SMALL] + ["conv_w"]
    return loss[0, 0], {n: out[4 * i:4 * i + 4] for i, n in enumerate(names)}


def _local_step(x, mem, pos, gains, shards, tgt, place):
    half = HEAD // 2
    inv_freq = jnp.float32(ROPE_THETA) ** (-(jnp.arange(half, dtype=F32) * 2.0 / HEAD))
    invf = jnp.tile(inv_freq, LANES // half)[None, :]
    sgn = jnp.tile(jnp.concatenate([-jnp.ones((half,), F32), jnp.ones((half,), F32)]), LANES // HEAD)[None, :]
    cos, sins, win8 = _rope_table(pos.astype(F32).reshape(S, 1), invf, sgn, [shards["w_in"]])
    wdn_left, wdn_right = shards["w_down"][:, 0:D // 2], shards["w_down"][:, D // 2:]
    q, kvp, bcu, qx16, h16, win16, wout8, wkv8, conv8, wdn8_right = _in_proj(
        x, gains["g_pre_mix"], win8, cos, sins, [shards["w_out"], shards["w_mem_kv"], shards["conv_w"], wdn_right])
    wout16, wkv16 = wout8.reshape(D, D), wkv8.reshape(D, 2 * XW)
    cw_full = conv8[:, 0:3, 0:CW // N_DEV].transpose(1, 0, 2).reshape(3, CW)
    cw8 = jnp.zeros((SUBLANES, CW), F32).at[0:3].set(cw_full)
    *patterns, wup8, wdn8_left = _attn_fwd(q, kvp, [shards["w_up"], wdn_left])
    wdn_halves = (wdn8_left.reshape(FF, D // 2), wdn8_right.reshape(FF, D // 2))
    memn16, kv16 = _mem_fwd(mem, gains["g_mem"], wkv16)
    ypre, y16, y2, x1, ltot = _mix_out(patterns, bcu, qx16, kv16, cw8, gains["g_attn_out"], gains["g_conv_out"],
                                       gains["g_xattn_out"], gains["g_post_mix"], wout16, x, [])
    a16, du16, h2_16, df2_16, dx1, loss8, dg_mlp = _mlp(
        x1, tgt, gains["g_pre_mlp"], gains["g_post_mlp"], wup8, wdn_halves)

    sums = {"w_up": _wgrad_cols(place, h2_16, du16, FF_BLK, "wgrad_up"),
            "w_down": _wgrad_cols(place, df2_16, a16, FF_BLK, "wgrad_down", square_b=True, transpose_out=True)}

    head_id = jnp.arange(AW, dtype=jnp.int32) // HEAD
    head_ones = (head_id[:, None] == head_id[None, :]).astype(BF16)
    dy2_16, qdo, ld, dbcu, dqx, dgs, dcw, dkv = _mix_out_bwd(
        dx1, y2, ypre, ltot, head_ones, q, bcu, qx16, kv16, cw8, gains["g_post_mix"], gains["g_attn_out"],
        gains["g_conv_out"], gains["g_xattn_out"], wout16)
    dkv16, dg_mem = _mem_bwd(mem, gains["g_mem"], wkv16, dkv)
    sums["w_mem_kv"] = _wgrad_rows(place, memn16, dkv16, "wgrad_mem_kv")
    sums["w_out"] = _wgrad_rows(place, y16, dy2_16, "wgrad_out")
    out = _attn_bwd(qdo, kvp, ld, [s[0] for s in sums.values()])
    dqkv, landed = out[:9], out[9:]
    reduced = {n: (s[1], landed[t]) for t, (n, s) in enumerate(sums.items())}
    dproj16, grad_x, dg_in = _in_proj_bwd(dqkv, dbcu, dqx, cos, sins, win16, x, gains["g_pre_mix"], dx1)

    _, in_own, in_landed, small_tot = _wgrad_cols(place, h16, dproj16, PW // N_DEV, "wgrad_in", to_chips=True,
                                                  small=(dg_in, dg_mem, dgs, dg_mlp, dcw, loss8))
    reduced["w_in"] = (in_own, in_landed)
    return grad_x, reduced, small_tot


BIG = ("w_in", "w_mem_kv", "w_out", "w_up", "w_down")
ORDER = ("g_pre_mix", "g_mem", "w_in", "w_mem_kv", "conv_w", "g_attn_out", "g_conv_out", "g_xattn_out", "w_out",
         "g_post_mix", "g_pre_mlp", "w_up", "w_down", "g_post_mlp")


def kernel(x, mem, positions, g_pre_mix, g_mem, w_in, w_mem_kv, conv_w, g_attn_out, g_conv_out, g_xattn_out, w_out, g_post_mix, g_pre_mlp, w_up, w_down, g_post_mlp, loss_target, m_g_pre_mix, m_g_mem, m_w_in, m_w_mem_kv, m_conv_w, m_g_attn_out, m_g_conv_out, m_g_xattn_out, m_w_out, m_g_post_mix, m_g_pre_mlp, m_w_up, m_w_down, m_g_post_mlp, v_g_pre_mix, v_g_mem, v_w_in, v_w_mem_kv, v_conv_w, v_g_attn_out, v_g_conv_out, v_g_xattn_out, v_w_out, v_g_post_mix, v_g_pre_mlp, v_w_up, v_w_down, v_g_post_mlp):
    w = dict(g_pre_mix=g_pre_mix, g_mem=g_mem, w_in=w_in, w_mem_kv=w_mem_kv, conv_w=conv_w, g_attn_out=g_attn_out,
             g_conv_out=g_conv_out, g_xattn_out=g_xattn_out, w_out=w_out, g_post_mix=g_post_mix, g_pre_mlp=g_pre_mlp,
             w_up=w_up, w_down=w_down, g_post_mlp=g_post_mlp)
    mo = dict(g_pre_mix=m_g_pre_mix, g_mem=m_g_mem, w_in=m_w_in, w_mem_kv=m_w_mem_kv, conv_w=m_conv_w,
              g_attn_out=m_g_attn_out, g_conv_out=m_g_conv_out, g_xattn_out=m_g_xattn_out, w_out=m_w_out,
              g_post_mix=m_g_post_mix, g_pre_mlp=m_g_pre_mlp, w_up=m_w_up, w_down=m_w_down, g_post_mlp=m_g_post_mlp)
    vo = dict(g_pre_mix=v_g_pre_mix, g_mem=v_g_mem, w_in=v_w_in, w_mem_kv=v_w_mem_kv, conv_w=v_conv_w,
              g_attn_out=v_g_attn_out, g_conv_out=v_g_conv_out, g_xattn_out=v_g_xattn_out, w_out=v_w_out,
              g_post_mix=v_g_post_mix, g_pre_mlp=v_g_pre_mlp, w_up=v_w_up, w_down=v_w_down, g_post_mlp=v_g_post_mlp)

    xi, yi, ci = lax.axis_index("x"), lax.axis_index("y"), lax.axis_index("c")
    me = 4 * xi + 2 * yi + ci
    place = jnp.stack([ci, 2 * xi + yi]).astype(jnp.int32)

    shards = {n: w[n][0].astype(BF16) for n in BIG}
    shards["conv_w"] = jnp.zeros((SUBLANES, LANES), F32).at[0:3, 0:CW // N_DEV].set(conv_w[0])

    gains = {n: w[n] for n, _, _, _ in SMALL}
    grad_x, reduced, small_tot = _local_step(x[0], mem[0], positions[0], gains, shards, loss_target[0], place)

    updated = {}
    for group in (("w_up", "w_down"), ("w_in", "w_out", "w_mem_kv")):
        updated.update(_adamw_shards({n: (*reduced[n], w[n][0], mo[n][0], vo[n][0]) for n in group},
                                     "adamw_" + "_".join(group))[0])
    grad, delta, new_m, new_v = {}, {}, {}, {}
    for n, (g, d_, m_, v_) in updated.items():
        grad[n], delta[n], new_m[n], new_v[n] = g[None], d_[None], m_[None], v_[None]

    params = {n: (w[n], mo[n], vo[n]) for n, _, _, _ in SMALL}
    params["conv_w"] = (w["conv_w"][0], mo["conv_w"][0], vo["conv_w"][0])
    loss, small = _small_update(small_tot, me.reshape(1).astype(jnp.int32), params)
    for n, (g, d_, m_, v_) in small.items():
        lead = (lambda a: a[None]) if n == "conv_w" else (lambda a: a)
        grad[n], delta[n], new_m[n], new_v[n] = lead(g), lead(d_), lead(m_), lead(v_)

    return (loss, grad_x[None], *[grad[n] for n in ORDER], *[delta[n] for n in ORDER],
            *[new_m[n] for n in ORDER], *[new_v[n] for n in ORDER])
```

```python
import jax
import jax.numpy as jnp
from jax import lax
from jax.experimental import pallas as pl
from jax.experimental.pallas import tpu as pltpu

F32, BF16 = jnp.float32, jnp.bfloat16
MESH = pl.DeviceIdType.MESH
ANY = pl.BlockSpec(memory_space=pl.ANY)

N_DEV = 8
D = 1024
S = 4096
N_MEM = 256
HEAD = 64
AW, CW, XW = 512, 256, 256
PW = 3 * AW + 3 * CW + XW
FF = 4096
FF_BLK = FF // N_DEV
EPS = 1e-6
NEG = -1e30
SCALE = HEAD ** -0.5
ROPE_THETA = 10000.0
LANES = 128
SUBLANES = 8

ADAM_LR, ADAM_B1, ADAM_B2, ADAM_EPS, ADAM_WD, ADAM_STEP = 0.001, 0.9, 0.999, 1e-08, 0.01, 10

TQ = 512
TQ_MLP = 512
NT = S // TQ


def _cparams(vmem_mb, n_grid=1):
    return pltpu.CompilerParams(dimension_semantics=("arbitrary",) * n_grid, vmem_limit_bytes=vmem_mb << 20)


def _const(shape):
    nd = len(shape)
    return pl.BlockSpec(shape, lambda *_: (0,) * nd, pipeline_mode=pl.Buffered(1))


def _acc(shape):
    nd = len(shape)
    return pl.BlockSpec(shape, lambda *_: (0,) * nd)


def _dot(a, b):
    return jnp.dot(a, b, preferred_element_type=F32)


def _dot_nt(a, b):
    return lax.dot_general(a, b, (((1,), (1,)), ((), ())), preferred_element_type=F32)


def _dot_tn(a, b):
    return lax.dot_general(a, b, (((0,), (0,)), ((), ())), preferred_element_type=F32)


def _rms(x, g):
    r = lax.rsqrt(jnp.mean(x * x, axis=-1, keepdims=True) + EPS)
    n = x * r
    return n * g, n, r


def _rms_bwd(dy, n, r, g):
    dn = dy * g
    dx = r * (dn - n * jnp.mean(dn * n, axis=-1, keepdims=True))
    return dx, jnp.sum(dy * n, axis=0, keepdims=True)


def _rot_half(t):
    lane = lax.broadcasted_iota(jnp.int32, t.shape, 1)
    n = t.shape[1]
    return jnp.where((lane % HEAD) < HEAD // 2, pltpu.roll(t, n - HEAD // 2, 1), pltpu.roll(t, HEAD // 2, 1))


def _rope_table(pos_col, invf, sgn, shards):
    def body(p_ref, f_ref, s_ref, c_out, s_out):
        ang = p_ref[...] * f_ref[...]
        c_out[...] = jnp.cos(ang)
        s_out[...] = jnp.sin(ang) * s_ref[...]

    tile = pl.BlockSpec((TQ, LANES), lambda i: (i, 0))
    return _call_with_gather(
        body, NT, shards, name="rope_table",
        in_specs=[pl.BlockSpec((TQ, 1), lambda i: (i, 0)), _const((1, LANES)), _const((1, LANES))],
        out_specs=[tile, tile], out_shape=[jax.ShapeDtypeStruct((S, LANES), F32)] * 2,
        scratch_shapes=[], vmem_mb=32, args=(pos_col, invf, sgn))


def _all_heads(t):
    return jnp.tile(t, (1, AW // LANES))


def _mem_fwd(mem, g_mem, wkv16):
    def body(m_ref, g_ref, w_ref, n16_ref, kv_ref):
        y, _, _ = _rms(m_ref[...], g_ref[...])
        y16 = y.astype(BF16)
        n16_ref[...] = y16
        kv_ref[...] = _dot(y16, w_ref[...]).astype(BF16)

    return pl.pallas_call(
        body, name="mem_fwd",
        out_shape=[jax.ShapeDtypeStruct((N_MEM, D), BF16), jax.ShapeDtypeStruct((N_MEM, 2 * XW), BF16)],
        compiler_params=pltpu.CompilerParams(vmem_limit_bytes=32 << 20))(mem, g_mem, wkv16)


def _in_proj(x, g, w8, cos, sins, shards):
    blk = PW // N_DEV

    def body(x_ref, g_ref, w8_ref, c_ref, s_ref, q_ref, kv_ref, bcu_ref, qx_ref, h_ref, w_out, w_ref):
        @pl.when(pl.program_id(0) == 0)
        def _():
            for j in range(N_DEV):
                w_ref[:, j * blk:(j + 1) * blk] = w8_ref[j]
            w_out[...] = w_ref[...]

        y, _, _ = _rms(x_ref[...], g_ref[...])
        h = y.astype(BF16)
        h_ref[...] = h
        proj = _dot(h, w_ref[...])
        cos, sn = _all_heads(c_ref[...]), _all_heads(s_ref[...])
        q, k = proj[:, 0:AW], proj[:, AW:2 * AW]
        q_ref[...] = (q * cos + _rot_half(q) * sn) * SCALE
        kv_ref[...] = _pack_pair(k * cos + _rot_half(k) * sn, proj[:, 2 * AW:3 * AW])
        bcu_ref[...] = proj[:, 3 * AW:3 * AW + 3 * CW]
        qx_ref[...] = (proj[:, 3 * AW + 3 * CW:] * SCALE).astype(BF16)

    def tile(w):
        return pl.BlockSpec((TQ, w), lambda i: (i, 0))

    return _call_with_gather(
        body, NT, shards, name="in_proj",
        in_specs=[tile(D), _const((1, D)), _const((N_DEV, D, blk)), tile(LANES), tile(LANES)],
        out_specs=[tile(AW), tile(AW), tile(3 * CW), tile(XW), tile(D), _acc((D, PW))],
        out_shape=[jax.ShapeDtypeStruct((S, AW), F32)] * 2 + [
            jax.ShapeDtypeStruct((S, 3 * CW), F32), jax.ShapeDtypeStruct((S, XW), BF16),
            jax.ShapeDtypeStruct((S, D), BF16), jax.ShapeDtypeStruct((D, PW), BF16)],
        scratch_shapes=[pltpu.VMEM((D, PW), BF16)], vmem_mb=56, args=(x, g, w8, cos, sins))


ATTN_PLANS = (("p1", 1, 128, 32), ("p4", 8, 64, 8), ("p16", 16, 128, 2))
PAD = 128
WIN = 256


ATTN_UNROLL = 16


def _fill_bias(tab, qblk, partner):
    qi = lax.broadcasted_iota(jnp.int32, (2 * qblk, WIN), 0) & (qblk - 1)
    kj = lax.broadcasted_iota(jnp.int32, (2 * qblk, WIN), 1)
    piece = kj >> (qblk.bit_length() - 1)
    kk = kj & (qblk - 1)
    prev = (piece & 1) == 0
    of_partner = piece >= 2
    for first in (0, 1):
        for par in (0, 1):
            lo = jnp.where(prev, (qblk if first else qi) + jnp.where(of_partner, par, 0), 0)
            hi = jnp.where(prev, qblk, qi + jnp.where(of_partner, par - 1, 0))
            tab[2 * first + par] = jnp.where((kk >= lo) & (kk <= hi), 0.0, NEG).astype(F32)


def _block_rows(g, qblk, nbc, partner):
    own = pl.ds(pl.multiple_of(PAD + g * qblk, qblk), qblk)
    first = ((g & (nbc - 1)) == 0).astype(jnp.int32)
    if partner:
        gp = jnp.bitwise_xor(g, 4 * nbc)
        wins = (pl.ds(pl.multiple_of(PAD + (g - 1) * qblk, qblk), 2 * qblk),
                pl.ds(pl.multiple_of(PAD + (gp - 1) * qblk, qblk), 2 * qblk))
        return own, wins, 2 * first + ((g >> ((4 * nbc).bit_length() - 1)) & 1)
    return own, (pl.ds(pl.multiple_of(PAD + (g - 1) * qblk, qblk), 2 * qblk),), 2 * first


def _pack_pair(lo, hi):
    lo_bits = lax.bitcast_convert_type(lo.astype(BF16).astype(F32), jnp.uint32) >> 16
    hi_bits = lax.bitcast_convert_type(hi.astype(BF16).astype(F32), jnp.uint32) & jnp.uint32(0xFFFF0000)
    return lax.bitcast_convert_type(hi_bits | lo_bits, F32)


def _unpack_pair(c):
    bits = lax.bitcast_convert_type(c, jnp.uint32)
    lo = lax.bitcast_convert_type(bits << 16, F32).astype(BF16)
    hi = lax.bitcast_convert_type(bits & jnp.uint32(0xFFFF0000), F32).astype(BF16)
    return lo, hi


def _window(ref, wins):
    parts = [ref[w, :] for w in wins]
    return parts[0] if len(parts) == 1 else jnp.concatenate(parts, axis=0)


def _stack_heads(t, lane):
    zero = jnp.zeros_like(t)
    return jnp.concatenate([jnp.where(lane < HEAD, t, zero), jnp.where(lane >= HEAD, t, zero)], axis=0)


def _unstack_heads(t2, lane):
    half = t2.shape[0] // 2
    return jnp.where(lane < HEAD, t2[0:half, :], t2[half:, :])


def _lanes_of(step):
    return pl.ds(pl.multiple_of(step * LANES, LANES), LANES)


def _whole_wait(buf, sem):
    whole = buf.at[pl.ds(PAD, S), :]
    return pltpu.make_async_copy(whole, whole, sem)


def _whole_waits(bufs, sems):
    return [_whole_wait(buf, sems.at[i]) for i, buf in enumerate(bufs)]


def _class_gather(views, bufs, sems, lanes):
    copies = []
    for i, (view, buf) in enumerate(zip(views, bufs)):
        if view.ndim == 2:
            copies.append(pltpu.make_async_copy(view.at[:, lanes], buf.at[pl.ds(PAD, S), :], sems.at[i]))
        else:
            per, n_cls = view.shape[0], view.shape[1]
            copies += [pltpu.make_async_copy(view.at[:, c, lanes], buf.at[pl.ds(PAD + c * per, per), :], sems.at[i])
                       for c in range(n_cls)]
    return copies


def _class_scatter(bufs, dsts, sems, lanes):
    copies = []
    for i, (buf, dst) in enumerate(zip(bufs, dsts)):
        if dst.ndim == 2:
            copies.append(pltpu.make_async_copy(buf.at[pl.ds(PAD, S), :], dst.at[:, lanes], sems.at[i]))
            continue
        per, n_cls = dst.shape[0], dst.shape[1]
        copies += [pltpu.make_async_copy(buf.at[pl.ds(PAD + c * per, per), :], dst.at[:, c, lanes], sems.at[i])
                   for c in range(n_cls)]
    return copies


def _start(copies):
    for cp in copies:
        cp.start()


def _wait(waits):
    for w in waits:
        w.wait()


def _attn_fwd(q, kvp, shards=()):
    views = [[a] + [a.reshape(S // n, n, AW) for _, n, _, _ in ATTN_PLANS[1:]] for a in (q, kvp)]
    flat = [views[a][p] for p in range(3) for a in range(2)]
    ng = len(shards)
    n_grid = AW // LANES

    def body(*refs):
        hbm = [refs[2 * p:2 * p + 2] for p in range(3)]
        refs = refs[6:]
        shard_refs, refs = refs[:ng], refs[ng:]
        y_ref, lt_ref = refs[0:2]
        whole_refs, refs = refs[2:2 + ng], refs[2 + ng:]
        bufs = [refs[2 * p:2 * p + 2] for p in range(3)]
        oc4, lc4, oc16, lc16, tab128, tab4, sem_in = refs[6:13]
        step = pl.program_id(0)
        if ng:
            start_gather, relay_gather, finish_gather = _gather_steps(shard_refs, whole_refs, *refs[13:])
            pl.when(step == 0)(start_gather)
            pl.when(step == n_grid // 2)(relay_gather)
        now = [_class_gather(hbm[p], bufs[p], sem_in.at[p], _lanes_of(step)) for p in range(3)]
        nxt = [_class_gather(hbm[p], bufs[p], sem_in.at[p], _lanes_of(step + 1)) for p in range(3)]

        @pl.when(step == 0)
        def _():
            for p in range(3):
                _start(now[p])
                for b in bufs[p]:
                    b[0:PAD, :] = jnp.zeros((PAD, LANES), F32)
            _fill_bias(tab128, 128, False)
            _fill_bias(tab4, 64, True)

        def prefetch(p):
            pl.when(step + 1 < n_grid)(lambda: _start(nxt[p]))

        lane = lax.broadcasted_iota(jnp.int32, (1, LANES), 1)
        ones = jnp.ones((WIN, LANES), BF16)

        def run(plan, bq, bkv, tab, o_dst, l_dst, dst_pad):
            _, n_cls, qblk, nbc = plan
            partner = n_cls == 8

            def block(g, carry):
                own, wins, mask = _block_rows(g, qblk, nbc, partner)
                q2 = _stack_heads(bq[own, :].astype(BF16), lane)
                kw, vwin = _unpack_pair(_window(bkv, wins))
                vw = jnp.concatenate([vwin, ones], axis=1)
                s = _dot_nt(q2, kw) + tab[mask]
                m = jnp.max(s, axis=1, keepdims=True)
                oe = _dot(jnp.exp(s - m).astype(BF16), vw)
                den = oe[:, LANES:]
                dst = pl.ds(pl.multiple_of(dst_pad + g * qblk, qblk), qblk)
                o_dst[dst, :] = _unstack_heads(oe[:, 0:LANES] / den, lane)
                l_dst[dst, :] = _unstack_heads(m + jnp.log(den), lane)
                return carry
            lax.fori_loop(0, n_cls * nbc, block, 0, unroll=ATTN_UNROLL)

        _wait(_whole_waits(bufs[0], sem_in.at[0]))
        run(ATTN_PLANS[0], *bufs[0], tab128, y_ref, lt_ref, 0)
        prefetch(0)
        _wait(_whole_waits(bufs[1], sem_in.at[1]))
        run(ATTN_PLANS[1], *bufs[1], tab4, oc4, lc4, PAD)
        prefetch(1)
        _wait(_whole_waits(bufs[2], sem_in.at[2]))
        run(ATTN_PLANS[2], *bufs[2], tab128, oc16, lc16, PAD)
        prefetch(2)

        n_rows = 64

        def token_order(buf, t, n_cls):
            per = S // n_cls
            first = PAD + t * (n_rows // n_cls)
            return jnp.concatenate([buf[pl.ds(first + jj, n_cls, stride=per), :] for jj in range(n_rows // n_cls)],
                                   axis=0)

        def combine(t, carry):
            rows = pl.ds(pl.multiple_of(t * n_rows, n_rows), n_rows)
            l0, l1, l2 = lt_ref[rows, :], token_order(lc4, t, 8), token_order(lc16, t, 16)
            lm = jnp.maximum(jnp.maximum(l0, l1), l2)
            e0, e1, e2 = jnp.exp(l0 - lm), jnp.exp(l1 - lm), jnp.exp(l2 - lm)
            den = e0 + e1 + e2
            y_ref[rows, :] = (e0 * y_ref[rows, :] + e1 * token_order(oc4, t, 8)
                              + e2 * token_order(oc16, t, 16)) / den
            lt_ref[rows, :] = lm + jnp.log(den)
            return carry
        lax.fori_loop(0, S // n_rows, combine, 0, unroll=2)

        if ng:
            pl.when(step == n_grid - 1)(finish_gather)

    col = pl.BlockSpec((S, LANES), lambda h: (0, h))
    padded = pltpu.VMEM((PAD + S, LANES), F32)
    return pl.pallas_call(
        body, grid=(n_grid,), name="attn_fwd",
        in_specs=[ANY] * (6 + ng), out_specs=[col, col] + [ANY] * ng,
        out_shape=[jax.ShapeDtypeStruct((S, AW), F32)] * 2 + _gathered_shapes(shards),
        scratch_shapes=[padded] * 10 + [
            pltpu.VMEM((4, 256, WIN), F32), pltpu.VMEM((4, 128, WIN), F32), pltpu.SemaphoreType.DMA((3, 2))]
        + (_gather_scratch(ng) if ng else []),
        compiler_params=_cparams(56))(*flat, *shards)


def _conv_taps(z, zprev, row):
    z1 = jnp.where(row == 0, zprev[7:8, :], pltpu.roll(z, 1, 0))
    z2 = jnp.where(row == 0, zprev[6:7, :], jnp.where(row == 1, zprev[7:8, :], pltpu.roll(z, 2, 0)))
    return z1, z2


def _xattn_scores(qm, km):
    s = _dot_nt(qm, km)
    m = jnp.max(s, axis=1, keepdims=True)
    e = jnp.exp(s - m)
    return e, jnp.sum(e, axis=1, keepdims=True)


def _mix_out(y_attn, bcu, qx16, kv16, cw8, g_attn, g_conv, g_x, g_post, wout16, x, shards):
    def body(ya_ref, bcu_ref, halo_ref, qx_ref, kv_ref, cw_ref, ga_ref, gc_ref, gx_ref, gp_ref, w_ref, x_ref,
             ypre_ref, y16_ref, y2_ref, x1_ref):
        i = pl.program_id(0)
        bcu = bcu_ref[...]
        b, c, u = bcu[:, 0:CW], bcu[:, CW:2 * CW], bcu[:, 2 * CW:]
        z = c * u
        halo = halo_ref[...]
        zprev = jnp.where(i > 0, halo[:, CW:2 * CW] * halo[:, 2 * CW:], 0.0)
        row = lax.broadcasted_iota(jnp.int32, z.shape, 0)
        z1, z2 = _conv_taps(z, zprev, row)
        cw = cw_ref[...]
        y_conv = b * (z2 * cw[0:1, :] + z1 * cw[1:2, :] + z * cw[2:3, :])

        qx = qx_ref[...]
        kv = kv_ref[...]
        km, vm = kv[:, 0:XW], kv[:, XW:]
        lane = lax.broadcasted_iota(jnp.int32, qx.shape, 1)
        y_x = jnp.zeros(qx.shape, F32)
        for h in range(XW // HEAD):
            hm = (lane >= h * HEAD) & (lane < (h + 1) * HEAD)
            e, l = _xattn_scores(jnp.where(hm, qx, jnp.zeros_like(qx)), km)
            y_x = jnp.where(hm, _dot(e.astype(BF16), vm) / l, y_x)

        y_attn = ya_ref[...]
        ypre_ref[:, 0:AW] = y_attn
        ypre_ref[:, AW:AW + CW] = y_conv
        ypre_ref[:, AW + CW:] = y_x
        y = jnp.concatenate([_rms(y_attn, ga_ref[...])[0], _rms(y_conv, gc_ref[...])[0],
                             _rms(y_x, gx_ref[...])[0]], axis=1).astype(BF16)
        y16_ref[...] = y
        y2 = _dot(y, w_ref[...])
        y2_ref[...] = y2
        x1_ref[...] = x_ref[...] + _rms(y2, gp_ref[...])[0]

    def tile(w):
        return pl.BlockSpec((TQ, w), lambda i: (i, 0))

    halo = pl.BlockSpec((SUBLANES, 3 * CW), lambda i: (jnp.maximum(i * (TQ // SUBLANES) - 1, 0), 0))
    return _call_with_gather(
        body, NT, shards, name="mix_out",
        in_specs=[tile(AW), tile(3 * CW), halo, tile(XW), _const((N_MEM, 2 * XW)), _const((SUBLANES, CW)),
                  _const((1, AW)), _const((1, CW)), _const((1, XW)), _const((1, D)), _const((D, D)), tile(D)],
        out_specs=[tile(D), tile(D), tile(D), tile(D)],
        out_shape=[jax.ShapeDtypeStruct((S, D), F32), jax.ShapeDtypeStruct((S, D), BF16),
                   jax.ShapeDtypeStruct((S, D), F32), jax.ShapeDtypeStruct((S, D), F32)],
        scratch_shapes=[], vmem_mb=56,
        args=(y_attn, bcu, bcu, qx16, kv16, cw8, g_attn, g_conv, g_x, g_post, wout16, x))


def _mlp(x1, tgt, g_pre, g_post, wup8, wdn_halves):
    tq = TQ_MLP
    half = D // 2

    def body(x1_ref, t_ref, g1_ref, g2_ref, wu_ref, wda_ref, wdb_ref,
             a16_ref, du_ref, h2_ref, df2_ref, dx1_ref, loss_ref, dg_ref):
        @pl.when(pl.program_id(0) == 0)
        def _():
            loss_ref[...] = jnp.zeros_like(loss_ref)
            dg_ref[...] = jnp.zeros_like(dg_ref)

        x1 = x1_ref[...]
        g1, g2 = g1_ref[...], g2_ref[...]
        y1, n1, r1 = _rms(x1, g1)
        h2 = y1.astype(BF16)
        h2_ref[...] = h2
        f2a = jnp.zeros((tq, half), F32)
        f2b = jnp.zeros((tq, half), F32)
        for j in range(N_DEV):
            cols = slice(j * FF_BLK, (j + 1) * FF_BLK)
            a = jnp.maximum(_dot(h2, wu_ref[j]), 0.0)
            a16_ref[:, cols] = a.astype(BF16)
            f = (a * a).astype(BF16)
            f2a = f2a + _dot(f, wda_ref[cols, :])
            f2b = f2b + _dot(f, wdb_ref[cols, :])
        f2 = jnp.concatenate([f2a, f2b], axis=1)
        y2, n2, r2 = _rms(f2, g2)
        e = x1 + y2 - t_ref[...]
        sq = jnp.sum(jnp.sum(e * e, axis=1, keepdims=True), axis=0, keepdims=True)
        loss_ref[...] += jnp.broadcast_to(sq * (0.5 / D), loss_ref.shape)
        dout = e * (1.0 / D)
        df2, dg2 = _rms_bwd(dout, n2, r2, g2)
        df2_16 = df2.astype(BF16)
        df2_ref[...] = df2_16
        dh2 = jnp.zeros((tq, D), F32)
        for j in range(N_DEV):
            cols = slice(j * FF_BLK, (j + 1) * FF_BLK)
            df = _dot_nt(df2_16[:, 0:half], wda_ref[cols, :]) + _dot_nt(df2_16[:, half:], wdb_ref[cols, :])
            du = (df * (2.0 * a16_ref[:, cols].astype(F32))).astype(BF16)
            du_ref[:, cols] = du
            dh2 = dh2 + _dot_nt(du, wu_ref[j])
        dx, dg1 = _rms_bwd(dh2, n1, r1, g1)
        dx1_ref[...] = dout + dx
        dg_ref[0:1, :] += dg2
        dg_ref[1:2, :] += dg1

    def tile(w):
        return pl.BlockSpec((tq, w), lambda i: (i, 0))

    return pl.pallas_call(
        body, grid=(S // tq,), name="mlp",
        in_specs=[tile(D), tile(D), _const((1, D)), _const((1, D)), _const((N_DEV, D, FF_BLK)), _const((FF, half)), _const((FF, half))],
        out_specs=[tile(FF), tile(FF), tile(D), tile(D), tile(D), _acc((SUBLANES, LANES)), _acc((SUBLANES, D))],
        out_shape=[jax.ShapeDtypeStruct((S, FF), BF16), jax.ShapeDtypeStruct((S, FF), BF16),
                   jax.ShapeDtypeStruct((S, D), BF16), jax.ShapeDtypeStruct((S, D), BF16),
                   jax.ShapeDtypeStruct((S, D), F32), jax.ShapeDtypeStruct((SUBLANES, LANES), F32),
                   jax.ShapeDtypeStruct((SUBLANES, D), F32)],
        compiler_params=_cparams(56))(x1, tgt, g_pre, g_post, wup8, *wdn_halves)


def _mix_out_bwd(dx1, y2, ypre, ltot, head_ones, q, bcu, qx16, kv16, cw8, g_post, g_attn, g_conv, g_x, wout16):
    def body(dx1_ref, y2_ref, ypre_ref, lt_ref, e_ref, q_ref, bcu_ref, halo_ref, qx_ref, kv_ref, cw_ref, gp_ref,
             ga_ref, gc_ref, gx_ref, w_ref, dy2_ref, qdo_ref, ld_ref, dbcu_ref, dqx_ref, dgs_ref, dcw_ref, dkv_ref,
             carry):
        i = pl.program_id(0)

        @pl.when(i == 0)
        def _():
            dgs_ref[...] = jnp.zeros_like(dgs_ref)
            dcw_ref[...] = jnp.zeros_like(dcw_ref)
            dkv_ref[...] = jnp.zeros_like(dkv_ref)
            carry[...] = jnp.zeros_like(carry)

        gp = gp_ref[...]
        _, n, r = _rms(y2_ref[...], gp)
        dy2, dgp = _rms_bwd(dx1_ref[...], n, r, gp)
        dy2_16 = dy2.astype(BF16)
        dy2_ref[...] = dy2_16
        dy = _dot_nt(dy2_16, w_ref[...])

        ypre = ypre_ref[...]
        ga, gc, gx = ga_ref[...], gc_ref[...], gx_ref[...]
        _, na, ra = _rms(ypre[:, 0:AW], ga)
        dya, dga = _rms_bwd(dy[:, 0:AW], na, ra, ga)
        _, nc, rc = _rms(ypre[:, AW:AW + CW], gc)
        dyc, dgc = _rms_bwd(dy[:, AW:AW + CW], nc, rc, gc)
        y_x = ypre[:, AW + CW:]
        _, nx, rx = _rms(y_x, gx)
        dyx, dgx = _rms_bwd(dy[:, AW + CW:], nx, rx, gx)
        qdo_ref[...] = _pack_pair(q_ref[...], dya)
        prod = dya * ypre[:, 0:AW]
        hi = prod.astype(BF16)
        lo = (prod - hi.astype(F32)).astype(BF16)
        head_sum = _dot(hi, e_ref[...]) + _dot(lo, e_ref[...])
        lane_a = lax.broadcasted_iota(jnp.int32, prod.shape, 1)
        ld_ref[...] = jnp.where((lane_a % HEAD) < HEAD // 2, lt_ref[...], head_sum)
        dgs_ref[0:1, :] += dgp
        dgs_ref[1:2, :] += jnp.concatenate([dga, dgc, dgx], axis=1)

        bcu = bcu_ref[...]
        b, c, u = bcu[:, 0:CW], bcu[:, CW:2 * CW], bcu[:, 2 * CW:]
        z = c * u
        halo = halo_ref[...]
        zprev = jnp.where(i < NT - 1, halo[:, CW:2 * CW] * halo[:, 2 * CW:], 0.0)
        row = lax.broadcasted_iota(jnp.int32, z.shape, 0)
        z1, z2 = _conv_taps(z, zprev, row)
        cw = cw_ref[...]
        conv = z2 * cw[0:1, :] + z1 * cw[1:2, :] + z * cw[2:3, :]
        dconv = dyc * b
        nxt = carry[...]
        dn1 = jnp.where(row == TQ - 1, nxt[0:1, :], pltpu.roll(dconv, TQ - 1, 0))
        dn2 = jnp.where(row == TQ - 1, nxt[1:2, :], jnp.where(row == TQ - 2, nxt[0:1, :], pltpu.roll(dconv, TQ - 2, 0)))
        carry[...] = dconv[0:SUBLANES, :]
        dz = dconv * cw[2:3, :] + dn1 * cw[1:2, :] + dn2 * cw[0:1, :]
        dbcu_ref[:, 0:CW] = (dyc * conv).astype(BF16)
        dbcu_ref[:, CW:2 * CW] = (dz * u).astype(BF16)
        dbcu_ref[:, 2 * CW:] = (dz * c).astype(BF16)
        dcw_ref[0:1, :] += jnp.sum(z2 * dconv, axis=0, keepdims=True)
        dcw_ref[1:2, :] += jnp.sum(z1 * dconv, axis=0, keepdims=True)
        dcw_ref[2:3, :] += jnp.sum(z * dconv, axis=0, keepdims=True)

        qx = qx_ref[...]
        kv = kv_ref[...]
        km, vm = kv[:, 0:XW], kv[:, XW:]
        lane = lax.broadcasted_iota(jnp.int32, qx.shape, 1)
        dqx = jnp.zeros(qx.shape, F32)
        dkm = jnp.zeros((N_MEM, XW), F32)
        dvm = jnp.zeros((N_MEM, XW), F32)
        for h in range(XW // HEAD):
            hm = (lane >= h * HEAD) & (lane < (h + 1) * HEAD)
            qm = jnp.where(hm, qx, jnp.zeros_like(qx))
            e, l = _xattn_scores(qm, km)
            p = e / l
            dom = jnp.where(hm, dyx, 0.0)
            do16 = dom.astype(BF16)
            dsum = jnp.sum(dom * y_x, axis=1, keepdims=True)
            ds = (p * (_dot_nt(do16, vm) - dsum)).astype(BF16)
            dqx = jnp.where(hm, _dot(ds, km), dqx)
            dkm = dkm + _dot_tn(ds, qm)
            dvm = dvm + _dot_tn(p.astype(BF16), do16)
        dqx_ref[...] = (dqx * SCALE).astype(BF16)
        dkv_ref[:, 0:XW] += dkm
        dkv_ref[:, XW:] += dvm

    def tile(w):
        return pl.BlockSpec((TQ, w), lambda i: (NT - 1 - i, 0))

    halo = pl.BlockSpec((SUBLANES, 3 * CW), lambda i: (jnp.maximum((NT - 1 - i) * (TQ // SUBLANES) - 1, 0), 0))
    return pl.pallas_call(
        body, grid=(NT,), name="mix_out_bwd",
        in_specs=[tile(D), tile(D), tile(D), tile(AW), _const((AW, AW)), tile(AW), tile(3 * CW), halo, tile(XW),
                  _const((N_MEM, 2 * XW)), _const((SUBLANES, CW)), _const((1, D)), _const((1, AW)), _const((1, CW)),
                  _const((1, XW)), _const((D, D))],
        out_specs=[tile(D), tile(AW), tile(AW), tile(3 * CW), tile(XW), _acc((SUBLANES, D)), _acc((SUBLANES, CW)),
                   _acc((N_MEM, 2 * XW))],
        out_shape=[jax.ShapeDtypeStruct((S, D), BF16), jax.ShapeDtypeStruct((S, AW), F32),
                   jax.ShapeDtypeStruct((S, AW), F32),
                   jax.ShapeDtypeStruct((S, 3 * CW), BF16), jax.ShapeDtypeStruct((S, XW), BF16),
                   jax.ShapeDtypeStruct((SUBLANES, D), F32), jax.ShapeDtypeStruct((SUBLANES, CW), F32),
                   jax.ShapeDtypeStruct((N_MEM, 2 * XW), F32)],
        scratch_shapes=[pltpu.VMEM((SUBLANES, CW), F32)],
        compiler_params=_cparams(56))(dx1, y2, ypre, ltot, head_ones, q, bcu, bcu, qx16, kv16, cw8, g_post, g_attn,
                                      g_conv, g_x, wout16)


def _attn_bwd(qdo, kvp, ld, chip_sums=()):
    n_in = 3
    views = [[a] + [a.reshape(S // n, n, AW) for _, n, _, _ in ATTN_PLANS[1:]] for a in (qdo, kvp, ld)]
    flat = [views[a][p] for p in range(3) for a in range(n_in)]
    ns = len(chip_sums)
    n_grid = AW // LANES

    def body(*refs):
        hbm = [refs[n_in * p:n_in * p + n_in] for p in range(3)]
        refs = refs[3 * n_in:]
        sum_refs, refs = refs[:ns], refs[ns:]
        outs = [refs[3 * p:3 * p + 3] for p in range(3)]
        landed_refs, sc = refs[9:9 + ns], refs[9 + ns:]
        bufs = [sc[3 * p:3 * p + 3] for p in range(3)]
        res = [sc[9 + 3 * p:12 + 3 * p] for p in range(3)]
        tab128, tab4, sem_in, sem_out = sc[18:22]
        step = pl.program_id(0)
        if ns:
            start_chips, finish_chips = _chips_steps(sum_refs, landed_refs, *sc[22:])
            pl.when(step == 0)(start_chips)
        now = [_class_gather(hbm[p], bufs[p], sem_in.at[p], _lanes_of(step)) for p in range(3)]
        nxt = [_class_gather(hbm[p], bufs[p], sem_in.at[p], _lanes_of(step + 1)) for p in range(3)]

        @pl.when(step == 0)
        def _():
            for p in range(3):
                _start(now[p])
                for b in bufs[p]:
                    b[0:PAD, :] = jnp.zeros((PAD, LANES), F32)
            _fill_bias(tab128, 128, False)
            _fill_bias(tab4, 64, True)

        def prefetch(p):
            pl.when(step + 1 < n_grid)(lambda: _start(nxt[p]))

        lane = lax.broadcasted_iota(jnp.int32, (1, LANES), 1)

        def run(plan, plan_bufs, tab, dst):
            _, n_cls, qblk, nbc = plan
            partner = n_cls == 8
            bqdo, bkv, bld = plan_bufs
            rq, rk, rv = dst

            def block(g, carry):
                own, wins, mask = _block_rows(g, qblk, nbc, partner)
                qb, dob = _unpack_pair(bqdo[own, :])
                q2, do2 = _stack_heads(qb, lane), _stack_heads(dob, lane)
                kw, vw = _unpack_pair(_window(bkv, wins))
                ldv = bld[own, :]
                half = HEAD // 2
                lt2 = jnp.concatenate([ldv[:, 0:1], ldv[:, HEAD:HEAD + 1]], axis=0)
                dsum2 = jnp.concatenate([ldv[:, half:half + 1], ldv[:, HEAD + half:HEAD + half + 1]], axis=0)
                p = jnp.exp(_dot_nt(q2, kw) + tab[mask] - lt2)
                ds = (p * (_dot_nt(do2, vw) - dsum2)).astype(BF16)
                rq[own, :] = _unstack_heads(_dot(ds, kw), lane)
                dkw = _dot_tn(ds, q2)
                dvw = _dot_tn(p.astype(BF16), do2)
                n_w = WIN // len(wins)
                for i, w in enumerate(wins):
                    rk[w, :] += dkw[i * n_w:(i + 1) * n_w, :]
                    rv[w, :] += dvw[i * n_w:(i + 1) * n_w, :]
                return carry
            lax.fori_loop(0, n_cls * nbc, block, 0, unroll=ATTN_UNROLL)

        tabs = (tab128, tab4, tab128)
        def drained(p):
            return lambda: _wait(_whole_waits(res[p], sem_out.at[p]))

        for p in range(3):
            pl.when(step > 0)(drained(p))
            for b in res[p][1:]:
                b[...] = jnp.zeros_like(b)
            _wait(_whole_waits(bufs[p], sem_in.at[p]))
            run(ATTN_PLANS[p], bufs[p], tabs[p], res[p])
            prefetch(p)
            _start(_class_scatter(res[p], outs[p], sem_out.at[p], _lanes_of(step)))
        for p in range(3):
            pl.when(step == n_grid - 1)(drained(p))
        if ns:
            pl.when(step == n_grid - 1)(finish_chips)

    padded = pltpu.VMEM((PAD + S, LANES), F32)
    shapes = [jax.ShapeDtypeStruct(views[0][p].shape, F32) for p in range(3) for _ in range(3)]
    out = pl.pallas_call(
        body, grid=(n_grid,), name="attn_bwd",
        in_specs=[ANY] * (3 * n_in + ns), out_specs=[ANY] * (9 + ns),
        out_shape=shapes + _chips_shapes(chip_sums),
        scratch_shapes=[padded] * 18
        + [pltpu.VMEM((4, 256, WIN), F32), pltpu.VMEM((4, 128, WIN), F32),
           pltpu.SemaphoreType.DMA((3, n_in)), pltpu.SemaphoreType.DMA((3, 3))]
        + (_chips_scratch(ns) if ns else []),
        compiler_params=_cparams(56))(*flat, *chip_sums)
    return [o.reshape(S, AW) for o in out[:9]] + list(out[9:])


def _in_proj_bwd(dqkv, dbcu, dqx, cos, sins, w16, x, g, dx1):
    tq = TQ // 2

    def body(*refs):
        parts = refs[0:9]
        dbcu_ref, dqx_ref, c_ref, s_ref, w_ref, x_ref, g_ref, dx1_ref, dp_ref, gx_ref, dg_ref = refs[9:]

        @pl.when(pl.program_id(0) == 0)
        def _():
            dg_ref[...] = jnp.zeros_like(dg_ref)

        dq, dk, dv = (parts[i][...] + parts[3 + i][...] + parts[6 + i][...] for i in range(3))
        cos, sn = _all_heads(c_ref[...]), _all_heads(s_ref[...])
        dqr = dq * SCALE
        dkr = dk
        dp = jnp.concatenate([(dqr * cos + _rot_half(dqr * sn)).astype(BF16),
                              (dkr * cos + _rot_half(dkr * sn)).astype(BF16), dv.astype(BF16),
                              dbcu_ref[...], dqx_ref[...]], axis=1)
        dp_ref[...] = dp
        dh = _dot_nt(dp, w_ref[...])
        g = g_ref[...]
        _, n, r = _rms(x_ref[...], g)
        dx, dg = _rms_bwd(dh, n, r, g)
        gx_ref[...] = dx1_ref[...] + dx
        dg_ref[0:1, :] += dg

    def tile(w):
        return pl.BlockSpec((tq, w), lambda i: (i, 0))

    return pl.pallas_call(
        body, grid=(S // tq,), name="in_proj_bwd",
        in_specs=[tile(AW)] * 9 + [tile(3 * CW), tile(XW), tile(LANES), tile(LANES), _const((D, PW)),
                                   tile(D), _const((1, D)), tile(D)],
        out_specs=[tile(PW), tile(D), _acc((SUBLANES, D))],
        out_shape=[jax.ShapeDtypeStruct((S, PW), BF16), jax.ShapeDtypeStruct((S, D), F32),
                   jax.ShapeDtypeStruct((SUBLANES, D), F32)],
        compiler_params=_cparams(56))(*dqkv, dbcu, dqx, cos, sins, w16, x, g, dx1)


def _mem_bwd(mem, g_mem, wkv16, dkv):
    def body(m_ref, g_ref, w_ref, dkv_ref, dkv16_ref, dg_ref):
        dkv16 = dkv_ref[...].astype(BF16)
        dkv16_ref[...] = dkv16
        _, n, _ = _rms(m_ref[...], g_ref[...])
        dg = jnp.sum(_dot_nt(dkv16, w_ref[...]) * n, axis=0, keepdims=True)
        dg_ref[...] = jnp.broadcast_to(dg, dg_ref.shape)

    return pl.pallas_call(
        body, name="mem_bwd",
        out_shape=[jax.ShapeDtypeStruct((N_MEM, 2 * XW), BF16), jax.ShapeDtypeStruct((SUBLANES, D), F32)],
        compiler_params=pltpu.CompilerParams(vmem_limit_bytes=32 << 20))(mem, g_mem, wkv16, dkv)


N_CHIPS = N_DEV // 2


def _transpose_into(at, a_ref):
    kk = a_ref.shape[0]
    chunk = min(kk, 512)
    for c in range(kk // chunk):
        at[:, c * chunk:(c + 1) * chunk] = a_ref[c * chunk:(c + 1) * chunk, :].T


def _pair_scratch(block):
    return [pltpu.VMEM((N_CHIPS,) + block, BF16), pltpu.VMEM((N_CHIPS,) + block, BF16),
            pltpu.SemaphoreType.DMA((N_CHIPS,)), pltpu.SemaphoreType.DMA((N_CHIPS,))]


def _swap_with_sibling(p, stage, land, send, recv):
    x, y, c = lax.axis_index("x"), lax.axis_index("y"), lax.axis_index("c")
    return pltpu.make_async_remote_copy(src_ref=stage.at[p], dst_ref=land.at[p], send_sem=send.at[p],
                                        recv_sem=recv.at[p], device_id=(x, y, 1 - c), device_id_type=MESH)


def _wgrad_cols(place, a16, b16, blk, name, square_b=False, transpose_out=False, to_chips=False, small=()):
    kk, m = a16.shape
    aligned = blk % LANES == 0
    wide = blk if aligned else -(-(blk + LANES // 2) // LANES) * LANES
    block = (blk, m) if transpose_out else (m, blk)

    def chip_of(step, my_chip):
        return jnp.bitwise_xor(my_chip, N_CHIPS - 1 - step) if to_chips else step

    def body(pl_ref, a_ref, *refs):
        b_refs, refs = refs[:2 if aligned else 1], refs[2 if aligned else 1:]
        accs, refs = refs[:len(small)], refs[len(small):]
        (cs_ref, own_ref), refs = refs[:2], refs[2:]
        if to_chips:
            landed, refs = refs[0], refs[1:]
        if small:
            tot_ref, refs = refs[0], refs[1:]
        (at, stage, land, send, recv), refs = refs[:5], refs[5:]
        if not aligned:
            (win, wsem), refs = refs[:2], refs[2:]
        if small:
            start_small, finish_small = _small_reduce_steps(accs, tot_ref, *refs[-4:])
            refs = refs[:-4]
        step = pl.program_id(0)
        if small:
            pl.when(step == 0)(start_small)
        x, y, c = lax.axis_index("x"), lax.axis_index("y"), lax.axis_index("c")
        my_chip = 2 * x + y
        p = chip_of(step, my_chip)

        def fetch(at_step, mine):
            j = 2 * chip_of(at_step, my_chip) + (c if mine else 1 - c)
            first = pl.multiple_of(((j * blk) >> 7) << 7, LANES)
            slot = 2 * (at_step & 1) + mine
            return pltpu.make_async_copy(b_refs[0].at[:, pl.ds(first, wide)], win.at[slot], wsem.at[slot])

        @pl.when(step == 0)
        def _():
            if not aligned:
                fetch(0, 0).start()
                fetch(0, 1).start()
            _transpose_into(at, a_ref)

        if not aligned:
            @pl.when(step + 1 < N_CHIPS)
            def _():
                fetch(step + 1, 0).start()
                fetch(step + 1, 1).start()

        def partial(mine):
            if aligned:
                b = b_refs[mine][...]
                if square_b:
                    b = b * b
                acc = _dot(at[...], b)
            else:
                fetch(step, mine).wait()
                acc = _dot(at[...], win[2 * (step & 1) + mine])
                odd = c if mine else 1 - c
                acc = pltpu.roll(acc, jnp.where(odd == 0, 0, wide - LANES // 2), 1)[:, 0:blk]
            return acc.T if transpose_out else acc

        stage[p] = partial(0).astype(BF16)
        swap = _swap_with_sibling(p, stage, land, send, recv)
        swap.start()
        mine = partial(1)
        swap.wait()
        total = mine + land[p].astype(F32)
        cs_ref[0] = total.astype(BF16)

        @pl.when(p == my_chip)
        def _():
            own_ref[...] = total

        if to_chips:
            stage2, send2, recv2 = refs
            flipped = jnp.bitwise_xor(p, my_chip)
            k = jnp.where(flipped == 2, 0, jnp.where(flipped == 1, 1, 2))

            def to_owner(src, k_, px, py):
                return pltpu.make_async_remote_copy(src_ref=src, dst_ref=landed.at[k_], send_sem=send2.at[k_],
                                                    recv_sem=recv2.at[k_], device_id=(px, py, c), device_id_type=MESH)

            @pl.when(p != my_chip)
            def _():
                stage2[p] = total.astype(BF16)
                to_owner(stage2.at[p], k, p >> 1, p & 1).start()

            @pl.when(step == N_CHIPS - 1)
            def _():
                for k_ in range(N_CHIPS - 1):
                    to_owner(stage2.at[0], k_, x, y).wait()

        if small:
            pl.when(step == N_CHIPS - 1)(finish_small)

    def b_spec(mine):
        return pl.BlockSpec((kk, blk), lambda i, s: (0, 2 * chip_of(i, s[1]) + (s[0] if mine else 1 - s[0])))

    b_specs, b_args = ([b_spec(0), b_spec(1)], (b16, b16)) if aligned else ([ANY], (b16,))
    scratch = [pltpu.VMEM((m, kk), BF16)] + _pair_scratch(block)
    if not aligned:
        scratch += [pltpu.VMEM((4, kk, wide), BF16), pltpu.SemaphoreType.DMA((4,))]
    out_specs = [pl.BlockSpec((1,) + block, lambda i, s: (chip_of(i, s[1]), 0, 0)), pl.BlockSpec(block, lambda i, s: (0, 0))]
    out_shape = [jax.ShapeDtypeStruct((N_CHIPS,) + block, BF16), jax.ShapeDtypeStruct(block, F32)]
    if to_chips:
        out_specs.append(ANY)
        out_shape.append(jax.ShapeDtypeStruct((N_CHIPS - 1,) + block, BF16))
        scratch += [pltpu.VMEM((N_CHIPS,) + block, BF16), pltpu.SemaphoreType.DMA((N_CHIPS - 1,)),
                    pltpu.SemaphoreType.DMA((N_CHIPS - 1,))]
    small_specs = [pl.BlockSpec(a.shape, lambda i, s: (0, 0)) for a in small]
    if small:
        out_specs.append(pl.BlockSpec((PACK_ROWS, D), lambda i, s: (0, 0)))
        out_shape.append(jax.ShapeDtypeStruct((PACK_ROWS, D), F32))
        scratch += _small_reduce_scratch()
    return pl.pallas_call(
        body, name=name,
        grid_spec=pltpu.PrefetchScalarGridSpec(
            num_scalar_prefetch=1, grid=(N_CHIPS,),
            in_specs=[pl.BlockSpec((kk, m), lambda i, s: (0, 0), pipeline_mode=pl.Buffered(1))] + b_specs + small_specs,
            out_specs=out_specs, scratch_shapes=scratch),
        out_shape=out_shape, compiler_params=_cparams(56))(place, a16, *b_args, *small)


def _wgrad_rows(place, a16, b16, name):
    kk, m = a16.shape
    n = b16.shape[1]
    block = (m // N_DEV, n)

    def body(pl_ref, a_ref, b_ref, cs_ref, own_ref, at, acc, stage, land, send, recv):
        c = pl_ref[0]
        _transpose_into(at, a_ref)
        acc[...] = _dot(at[...], b_ref[...])

        def rows(owner):
            return pl.ds(pl.multiple_of(owner * block[0], block[0]), block[0])

        swaps = []
        for p in range(N_CHIPS):
            stage[p] = acc[rows(2 * p + 1 - c), :].astype(BF16)
            swaps.append(_swap_with_sibling(p, stage, land, send, recv))
            swaps[-1].start()
        for p in range(N_CHIPS):
            swaps[p].wait()
            total = acc[rows(2 * p + c), :] + land[p].astype(F32)
            cs_ref[p] = total.astype(BF16)

            @pl.when(p == pl_ref[1])
            def _():
                own_ref[...] = total

    vmem = pl.BlockSpec(memory_space=pltpu.VMEM)
    return pl.pallas_call(
        body, name=name,
        in_specs=[pl.BlockSpec(memory_space=pltpu.SMEM), vmem, vmem], out_specs=[vmem, vmem],
        out_shape=[jax.ShapeDtypeStruct((N_CHIPS,) + block, BF16), jax.ShapeDtypeStruct(block, F32)],
        scratch_shapes=[pltpu.VMEM((m, kk), BF16), pltpu.VMEM((m, n), F32)] + _pair_scratch(block),
        compiler_params=pltpu.CompilerParams(vmem_limit_bytes=56 << 20))(place, a16, b16)


def _adamw_math(w, g, m, v):
    m = ADAM_B1 * m + (1.0 - ADAM_B1) * g
    v = ADAM_B2 * v + (1.0 - ADAM_B2) * jnp.square(g)
    m_hat = m / (1.0 - ADAM_B1 ** ADAM_STEP)
    v_hat = v / (1.0 - ADAM_B2 ** ADAM_STEP)
    delta = -ADAM_LR * (m_hat / (jnp.sqrt(v_hat) + ADAM_EPS) + ADAM_WD * w)
    return delta, m, v


def _adamw_shards(updates, name, chip_sums=()):
    names, nu, ns = list(updates), len(updates), len(chip_sums)

    def body(*refs):
        ins, sum_refs = refs[:5 * nu], refs[5 * nu:5 * nu + ns]
        outs = refs[5 * nu + ns:9 * nu + ns]
        landed_refs, scratch = refs[9 * nu + ns:9 * nu + 2 * ns], refs[9 * nu + 2 * ns:]
        if ns:
            start_chips, finish_chips = _chips_steps(sum_refs, landed_refs, *scratch)
            start_chips()
        for i in range(nu):
            o_ref, r_ref, w_ref, m_ref, v_ref = ins[5 * i:5 * i + 5]
            g_out, d_out, m_out, v_out = outs[4 * i:4 * i + 4]
            g = o_ref[...] + r_ref[0].astype(F32) + r_ref[1].astype(F32) + r_ref[2].astype(F32)
            g_out[...] = g
            d_out[...], m_out[...], v_out[...] = _adamw_math(w_ref[...], g, m_ref[...], v_ref[...])
        if ns:
            finish_chips()

    vmem = pl.BlockSpec(memory_space=pltpu.VMEM)
    out = pl.pallas_call(
        body, name=name,
        in_specs=[vmem] * (5 * nu) + [ANY] * ns, out_specs=[vmem] * (4 * nu) + [ANY] * ns,
        out_shape=[jax.ShapeDtypeStruct(updates[n][2].shape, F32) for n in names for _ in range(4)]
        + _chips_shapes(chip_sums),
        scratch_shapes=_chips_scratch(ns) if ns else [],
        compiler_params=pltpu.CompilerParams(vmem_limit_bytes=56 << 20),
    )(*[a for n in names for a in updates[n]], *chip_sums)
    return {n: out[4 * i:4 * i + 4] for i, n in enumerate(names)}, list(out[4 * nu:])


def _place():
    x, y, c = lax.axis_index("x"), lax.axis_index("y"), lax.axis_index("c")
    chips = [(1 - x, y), (x, 1 - y), (1 - x, 1 - y)]
    return x, y, c, chips


def _gather_steps(ins, outs, send, recv, lsem):
    nt = len(ins)
    x, y, c, (xn, yn, diag) = _place()
    me, sib = (x, y, c), (x, y, 1 - c)

    def slot(t, px, py, pc):
        return outs[t].at[4 * px + 2 * py + pc]

    def copy(t, k, block, to, src=None):
        return pltpu.make_async_remote_copy(
            src_ref=slot(t, *block) if src is None else src, dst_ref=slot(t, *block),
            send_sem=send.at[t, k], recv_sem=recv.at[t, k], device_id=to, device_id_type=MESH)

    mine = [pltpu.make_async_copy(ins[t], slot(t, *me), lsem.at[t]) for t in range(nt)]
    first = [copy(t, k, me, to, src=ins[t]) for t in range(nt) for k, to in ((0, sib), (1, (*xn, c)), (2, (*yn, c)))]

    def start():
        for cp in mine + first:
            cp.start()

    def landed(k, chip, also_to=None):
        for t in range(nt):
            copy(t, k, (*chip, c), me).wait_recv()
            if also_to is not None:
                copy(t, 3, (*chip, c), (*also_to, c)).start()
            copy(t, 3 + k, (*chip, c), sib).start()

    def relay():
        @pl.when(c == 0)
        def _():
            landed(1, xn, also_to=yn)
            landed(2, yn)

        @pl.when(c == 1)
        def _():
            landed(2, yn, also_to=xn)
            landed(1, xn)

    def finish():
        landed(3, diag)
        for t in range(nt):
            copy(t, 0, sib, me).wait_recv()
            for k, chip in ((4, xn), (5, yn), (6, diag)):
                copy(t, k, (*chip, 1 - c), me).wait_recv()
            for k in range(7):
                copy(t, k, me, sib).wait_send()
        for cp in mine:
            cp.wait()

    return start, relay, finish


def _gather_scratch(nt):
    return [pltpu.SemaphoreType.DMA((nt, 7)), pltpu.SemaphoreType.DMA((nt, 7)), pltpu.SemaphoreType.DMA((nt,))]


def _gathered_shapes(shards):
    return [jax.ShapeDtypeStruct((N_DEV,) + s.shape, s.dtype) for s in shards]


def _call_with_gather(body, n_grid, shards, *, name, in_specs, out_specs, out_shape, scratch_shapes, vmem_mb, args):
    ng, n_in, n_out = len(shards), len(in_specs), len(out_specs)

    def wrapped(*refs):
        ins, shard_refs = refs[:n_in], refs[n_in:n_in + ng]
        outs = refs[n_in + ng:n_in + ng + n_out]
        whole_refs = refs[n_in + ng + n_out:n_in + 2 * ng + n_out]
        scratch = refs[n_in + 2 * ng + n_out:]
        if ng:
            start, relay, finish = _gather_steps(shard_refs, whole_refs, *scratch[len(scratch_shapes):])
            pl.when(pl.program_id(0) == 0)(start)
            pl.when(pl.program_id(0) == n_grid // 2)(relay)
        body(*ins, *outs, *scratch[:len(scratch_shapes)])
        if ng:
            pl.when(pl.program_id(0) == n_grid - 1)(finish)

    return pl.pallas_call(
        wrapped, grid=(n_grid,), name=name,
        in_specs=list(in_specs) + [ANY] * ng, out_specs=list(out_specs) + [ANY] * ng,
        out_shape=list(out_shape) + _gathered_shapes(shards),
        scratch_shapes=list(scratch_shapes) + (_gather_scratch(ng) if ng else []),
        compiler_params=_cparams(vmem_mb))(*args, *shards)


def _chips_steps(ins, outs, send, recv):
    _, _, c, chips = _place()
    copies = [pltpu.make_async_remote_copy(
        src_ref=ins[t].at[2 * px + py], dst_ref=outs[t].at[j], send_sem=send.at[t, j], recv_sem=recv.at[t, j],
        device_id=(px, py, c), device_id_type=MESH) for t in range(len(ins)) for j, (px, py) in enumerate(chips)]

    def start():
        for cp in copies:
            cp.start()

    def finish():
        for cp in copies:
            cp.wait()

    return start, finish


def _chips_scratch(nt):
    return [pltpu.SemaphoreType.DMA((nt, 3)), pltpu.SemaphoreType.DMA((nt, 3))]


def _chips_shapes(cs16s):
    return [jax.ShapeDtypeStruct((3,) + g.shape[1:], g.dtype) for g in cs16s]


SMALL = (("g_pre_mix", 0, 0, D), ("g_mem", 1, 0, D), ("g_post_mix", 2, 0, D), ("g_attn_out", 3, 0, AW),
         ("g_conv_out", 3, AW, CW), ("g_xattn_out", 3, AW + CW, XW), ("g_post_mlp", 4, 0, D), ("g_pre_mlp", 5, 0, D))
CONV_ROW = 8
PACK_ROWS = 16


LOSS_ROW = 15


def _small_reduce_steps(accs, tot_ref, pack, land, send, recv):
    acc_in, acc_mem, acc_mix, acc_mlp, acc_cw, acc_loss = accs
    x, y, c, _ = _place()
    me = 4 * x + 2 * y + c
    copies = []
    for k in range(1, N_DEV):
        kx, ky, kc = (k >> 2) & 1, (k >> 1) & 1, k & 1
        peer = (1 - x if kx else x, 1 - y if ky else y, 1 - c if kc else c)
        copies.append(pltpu.make_async_remote_copy(
            src_ref=pack, dst_ref=land.at[me], send_sem=send.at[k - 1], recv_sem=recv.at[k - 1],
            device_id=peer, device_id_type=MESH))

    def start():
        pack[...] = jnp.zeros_like(pack)
        pack[0:1, :] = acc_in[0:1, :]
        pack[1:2, :] = acc_mem[0:1, :]
        pack[2:4, :] = acc_mix[0:2, :]
        pack[4:6, :] = acc_mlp[0:2, :]
        pack[CONV_ROW:CONV_ROW + 3, 0:CW] = acc_cw[0:3, :]
        pack[LOSS_ROW:LOSS_ROW + 1, 0:LANES] = acc_loss[0:1, :]
        land[me] = pack[...]
        for cp in copies:
            cp.start()

    def finish():
        for cp in copies:
            cp.wait()
        tot = land[0]
        for s in range(1, N_DEV):
            tot = tot + land[s]
        tot_ref[...] = tot

    return start, finish


def _small_reduce_scratch():
    return [pltpu.VMEM((PACK_ROWS, D), F32), pltpu.VMEM((N_DEV, PACK_ROWS, D), F32),
            pltpu.SemaphoreType.DMA((N_DEV - 1,)), pltpu.SemaphoreType.DMA((N_DEV - 1,))]


def _small_update(tot, me, params):
    flat = [a for n, _, _, _ in SMALL for a in params[n]] + list(params["conv_w"])
    n_par = len(SMALL) + 1
    tap_cols = CW // N_DEV

    def body(*refs):
        me_ref, tot_ref = refs[0:2]
        ins = refs[2:2 + 3 * n_par]
        loss_out = refs[2 + 3 * n_par]
        outs = refs[3 + 3 * n_par:]
        tot = tot_ref[...]
        loss_out[...] = jnp.broadcast_to(tot[LOSS_ROW:LOSS_ROW + 1, 0:LANES], loss_out.shape)

        def update(i, g):
            w_ref, m_ref, v_ref = ins[3 * i:3 * i + 3]
            g_out, d_out, m_out, v_out = outs[4 * i:4 * i + 4]
            g_out[...] = g
            d_out[...], m_out[...], v_out[...] = _adamw_math(w_ref[...], g, m_ref[...], v_ref[...])

        for i, (_, row, lane0, width) in enumerate(SMALL):
            update(i, tot[row:row + 1, lane0:lane0 + width])
        me = me_ref[0]
        taps = pltpu.roll(tot[CONV_ROW:CONV_ROW + SUBLANES, 0:CW], jnp.where(me == 0, 0, CW - me * tap_cols), 1)
        update(n_par - 1, taps[0:3, 0:tap_cols])

    shapes = [jax.ShapeDtypeStruct(params[n][0].shape, F32) for n, _, _, _ in SMALL] + [
        jax.ShapeDtypeStruct(params["conv_w"][0].shape, F32)]
    vmem = pl.BlockSpec(memory_space=pltpu.VMEM)
    loss, *out = pl.pallas_call(
        body, name="small_update",
        in_specs=[pl.BlockSpec(memory_space=pltpu.SMEM)] + [vmem] * (1 + 3 * n_par),
        out_shape=[jax.ShapeDtypeStruct((SUBLANES, LANES), F32)] + [s for s in shapes for _ in range(4)],
    )(me, tot, *flat)
    names = [n for n, _, _, _ in SMALL] + ["conv_w"]
    return loss[0, 0], {n: out[4 * i:4 * i + 4] for i, n in enumerate(names)}


def _local_step(x, mem, pos, gains, shards, tgt, place):
    half = HEAD // 2
    inv_freq = jnp.float32(ROPE_THETA) ** (-(jnp.arange(half, dtype=F32) * 2.0 / HEAD))
    invf = jnp.tile(inv_freq, LANES // half)[None, :]
    sgn = jnp.tile(jnp.concatenate([-jnp.ones((half,), F32), jnp.ones((half,), F32)]), LANES // HEAD)[None, :]
    cos, sins, win8 = _rope_table(pos.astype(F32).reshape(S, 1), invf, sgn, [shards["w_in"]])
    wdn_left, wdn_right = shards["w_down"][:, 0:D // 2], shards["w_down"][:, D // 2:]
    q, kvp, bcu, qx16, h16, win16, wout8, wkv8, conv8, wdn8_right = _in_proj(
        x, gains["g_pre_mix"], win8, cos, sins, [shards["w_out"], shards["w_mem_kv"], shards["conv_w"], wdn_right])
    wout16, wkv16 = wout8.reshape(D, D), wkv8.reshape(D, 2 * XW)
    cw_full = conv8[:, 0:3, 0:CW // N_DEV].transpose(1, 0, 2).reshape(3, CW)
    cw8 = jnp.zeros((SUBLANES, CW), F32).at[0:3].set(cw_full)
    y_attn, ltot, wup8, wdn8_left = _attn_fwd(q, kvp, [shards["w_up"], wdn_left])
    wdn_halves = (wdn8_left.reshape(FF, D // 2), wdn8_right.reshape(FF, D // 2))
    memn16, kv16 = _mem_fwd(mem, gains["g_mem"], wkv16)
    ypre, y16, y2, x1 = _mix_out(y_attn, bcu, qx16, kv16, cw8, gains["g_attn_out"], gains["g_conv_out"],
                                 gains["g_xattn_out"], gains["g_post_mix"], wout16, x, [])
    a16, du16, h2_16, df2_16, dx1, loss8, dg_mlp = _mlp(
        x1, tgt, gains["g_pre_mlp"], gains["g_post_mlp"], wup8, wdn_halves)

    sums = {"w_up": _wgrad_cols(place, h2_16, du16, FF_BLK, "wgrad_up"),
            "w_down": _wgrad_cols(place, df2_16, a16, FF_BLK, "wgrad_down", square_b=True, transpose_out=True)}

    head_id = jnp.arange(AW, dtype=jnp.int32) // HEAD
    head_ones = (head_id[:, None] == head_id[None, :]).astype(BF16)
    dy2_16, qdo, ld, dbcu, dqx, dgs, dcw, dkv = _mix_out_bwd(
        dx1, y2, ypre, ltot, head_ones, q, bcu, qx16, kv16, cw8, gains["g_post_mix"], gains["g_attn_out"],
        gains["g_conv_out"], gains["g_xattn_out"], wout16)
    dkv16, dg_mem = _mem_bwd(mem, gains["g_mem"], wkv16, dkv)
    sums["w_mem_kv"] = _wgrad_rows(place, memn16, dkv16, "wgrad_mem_kv")
    sums["w_out"] = _wgrad_rows(place, y16, dy2_16, "wgrad_out")
    out = _attn_bwd(qdo, kvp, ld, [s[0] for s in sums.values()])
    dqkv, landed = out[:9], out[9:]
    reduced = {n: (s[1], landed[t]) for t, (n, s) in enumerate(sums.items())}
    dproj16, grad_x, dg_in = _in_proj_bwd(dqkv, dbcu, dqx, cos, sins, win16, x, gains["g_pre_mix"], dx1)

    _, in_own, in_landed, small_tot = _wgrad_cols(place, h16, dproj16, PW // N_DEV, "wgrad_in", to_chips=True,
                                                  small=(dg_in, dg_mem, dgs, dg_mlp, dcw, loss8))
    reduced["w_in"] = (in_own, in_landed)
    return grad_x, reduced, small_tot


BIG = ("w_in", "w_mem_kv", "w_out", "w_up", "w_down")
ORDER = ("g_pre_mix", "g_mem", "w_in", "w_mem_kv", "conv_w", "g_attn_out", "g_conv_out", "g_xattn_out", "w_out",
         "g_post_mix", "g_pre_mlp", "w_up", "w_down", "g_post_mlp")


def kernel(x, mem, positions, g_pre_mix, g_mem, w_in, w_mem_kv, conv_w, g_attn_out, g_conv_out, g_xattn_out, w_out, g_post_mix, g_pre_mlp, w_up, w_down, g_post_mlp, loss_target, m_g_pre_mix, m_g_mem, m_w_in, m_w_mem_kv, m_conv_w, m_g_attn_out, m_g_conv_out, m_g_xattn_out, m_w_out, m_g_post_mix, m_g_pre_mlp, m_w_up, m_w_down, m_g_post_mlp, v_g_pre_mix, v_g_mem, v_w_in, v_w_mem_kv, v_conv_w, v_g_attn_out, v_g_conv_out, v_g_xattn_out, v_w_out, v_g_post_mix, v_g_pre_mlp, v_w_up, v_w_down, v_g_post_mlp):
    w = dict(g_pre_mix=g_pre_mix, g_mem=g_mem, w_in=w_in, w_mem_kv=w_mem_kv, conv_w=conv_w, g_attn_out=g_attn_out,
             g_conv_out=g_conv_out, g_xattn_out=g_xattn_out, w_out=w_out, g_post_mix=g_post_mix, g_pre_mlp=g_pre_mlp,
             w_up=w_up, w_down=w_down, g_post_mlp=g_post_mlp)
    mo = dict(g_pre_mix=m_g_pre_mix, g_mem=m_g_mem, w_in=m_w_in, w_mem_kv=m_w_mem_kv, conv_w=m_conv_w,
              g_attn_out=m_g_attn_out, g_conv_out=m_g_conv_out, g_xattn_out=m_g_xattn_out, w_out=m_w_out,
              g_post_mix=m_g_post_mix, g_pre_mlp=m_g_pre_mlp, w_up=m_w_up, w_down=m_w_down, g_post_mlp=m_g_post_mlp)
    vo = dict(g_pre_mix=v_g_pre_mix, g_mem=v_g_mem, w_in=v_w_in, w_mem_kv=v_w_mem_kv, conv_w=v_conv_w,
              g_attn_out=v_g_attn_out, g_conv_out=v_g_conv_out, g_xattn_out=v_g_xattn_out, w_out=v_w_out,
              g_post_mix=v_g_post_mix, g_pre_mlp=v_g_pre_mlp, w_up=v_w_up, w_down=v_w_down, g_post_mlp=v_g_post_mlp)

    xi, yi, ci = lax.axis_index("x"), lax.axis_index("y"), lax.axis_index("c")
    me = 4 * xi + 2 * yi + ci
    place = jnp.stack([ci, 2 * xi + yi]).astype(jnp.int32)

    shards = {n: w[n][0].astype(BF16) for n in BIG}
    shards["conv_w"] = jnp.zeros((SUBLANES, LANES), F32).at[0:3, 0:CW // N_DEV].set(conv_w[0])

    gains = {n: w[n] for n, _, _, _ in SMALL}
    grad_x, reduced, small_tot = _local_step(x[0], mem[0], positions[0], gains, shards, loss_target[0], place)

    updated = {}
    for group in (("w_up", "w_down"), ("w_in", "w_out", "w_mem_kv")):
        updated.update(_adamw_shards({n: (*reduced[n], w[n][0], mo[n][0], vo[n][0]) for n in group},
                                     "adamw_" + "_".join(group))[0])
    grad, delta, new_m, new_v = {}, {}, {}, {}
    for n, (g, d_, m_, v_) in updated.items():
        grad[n], delta[n], new_m[n], new_v[n] = g[None], d_[None], m_[None], v_[None]

    params = {n: (w[n], mo[n], vo[n]) for n, _, _, _ in SMALL}
    params["conv_w"] = (w["conv_w"][0], mo["conv_w"][0], vo["conv_w"][0])
    loss, small = _small_update(small_tot, me.reshape(1).astype(jnp.int32), params)
    for n, (g, d_, m_, v_) in small.items():
        lead = (lambda a: a[None]) if n == "conv_w" else (lambda a: a)
        grad[n], delta[n], new_m[n], new_v[n] = lead(g), lead(d_), lead(m_), lead(v_)

    return (loss, grad_x[None], *[grad[n] for n in ORDER], *[delta[n] for n in ORDER],
            *[new_m[n] for n in ORDER], *[new_v[n] for n in ORDER])
```

```python
import jax
import jax.numpy as jnp
from jax import lax
from jax.experimental import pallas as pl
from jax.experimental.pallas import tpu as pltpu

F32, BF16 = jnp.float32, jnp.bfloat16
MESH = pl.DeviceIdType.MESH
ANY = pl.BlockSpec(memory_space=pl.ANY)

N_DEV = 8
D = 1024
S = 4096
N_MEM = 256
HEAD = 64
AW, CW, XW = 512, 256, 256
PW = 3 * AW + 3 * CW + XW
FF = 4096
FF_BLK = FF // N_DEV
EPS = 1e-6
NEG = -1e30
SCALE = HEAD ** -0.5
ROPE_THETA = 10000.0
LANES = 128
SUBLANES = 8

ADAM_LR, ADAM_B1, ADAM_B2, ADAM_EPS, ADAM_WD, ADAM_STEP = 0.001, 0.9, 0.999, 1e-08, 0.01, 10

TQ = 512
TQ_MLP = 512
NT = S // TQ


def _cparams(vmem_mb, n_grid=1):
    return pltpu.CompilerParams(dimension_semantics=("arbitrary",) * n_grid, vmem_limit_bytes=vmem_mb << 20)


def _const(shape):
    nd = len(shape)
    return pl.BlockSpec(shape, lambda *_: (0,) * nd, pipeline_mode=pl.Buffered(1))


def _acc(shape):
    nd = len(shape)
    return pl.BlockSpec(shape, lambda *_: (0,) * nd)


def _dot(a, b):
    return jnp.dot(a, b, preferred_element_type=F32)


def _dot_nt(a, b):
    return lax.dot_general(a, b, (((1,), (1,)), ((), ())), preferred_element_type=F32)


def _dot_tn(a, b):
    return lax.dot_general(a, b, (((0,), (0,)), ((), ())), preferred_element_type=F32)


def _rms(x, g):
    r = lax.rsqrt(jnp.mean(x * x, axis=-1, keepdims=True) + EPS)
    n = x * r
    return n * g, n, r


def _rms_bwd(dy, n, r, g):
    dn = dy * g
    dx = r * (dn - n * jnp.mean(dn * n, axis=-1, keepdims=True))
    return dx, jnp.sum(dy * n, axis=0, keepdims=True)


def _rot_half(t):
    lane = lax.broadcasted_iota(jnp.int32, t.shape, 1)
    n = t.shape[1]
    return jnp.where((lane % HEAD) < HEAD // 2, pltpu.roll(t, n - HEAD // 2, 1), pltpu.roll(t, HEAD // 2, 1))


def _rope_table(pos_col, invf, sgn, shards):
    def body(p_ref, f_ref, s_ref, c_out, s_out):
        ang = p_ref[...] * f_ref[...]
        c_out[...] = jnp.cos(ang)
        s_out[...] = jnp.sin(ang) * s_ref[...]

    tile = pl.BlockSpec((TQ, LANES), lambda i: (i, 0))
    return _call_with_gather(
        body, NT, shards, name="rope_table",
        in_specs=[pl.BlockSpec((TQ, 1), lambda i: (i, 0)), _const((1, LANES)), _const((1, LANES))],
        out_specs=[tile, tile], out_shape=[jax.ShapeDtypeStruct((S, LANES), F32)] * 2,
        scratch_shapes=[], vmem_mb=32, args=(pos_col, invf, sgn))


def _all_heads(t):
    return jnp.tile(t, (1, AW // LANES))


def _mem_fwd(mem, g_mem, wkv16):
    def body(m_ref, g_ref, w_ref, n16_ref, kv_ref):
        y, _, _ = _rms(m_ref[...], g_ref[...])
        y16 = y.astype(BF16)
        n16_ref[...] = y16
        kv_ref[...] = _dot(y16, w_ref[...]).astype(BF16)

    return pl.pallas_call(
        body, name="mem_fwd",
        out_shape=[jax.ShapeDtypeStruct((N_MEM, D), BF16), jax.ShapeDtypeStruct((N_MEM, 2 * XW), BF16)],
        compiler_params=pltpu.CompilerParams(vmem_limit_bytes=32 << 20))(mem, g_mem, wkv16)


def _in_proj(x, g, w8, cos, sins, shards):
    blk = PW // N_DEV

    def body(x_ref, g_ref, w8_ref, c_ref, s_ref, q_ref, kv_ref, bcu_ref, qx_ref, h_ref, w_out, w_ref):
        @pl.when(pl.program_id(0) == 0)
        def _():
            for j in range(N_DEV):
                w_ref[:, j * blk:(j + 1) * blk] = w8_ref[j]
            w_out[...] = w_ref[...]

        y, _, _ = _rms(x_ref[...], g_ref[...])
        h = y.astype(BF16)
        h_ref[...] = h
        proj = _dot(h, w_ref[...])
        cos, sn = _all_heads(c_ref[...]), _all_heads(s_ref[...])
        q, k = proj[:, 0:AW], proj[:, AW:2 * AW]
        q_ref[...] = (q * cos + _rot_half(q) * sn) * SCALE
        kv_ref[...] = _pack_pair(k * cos + _rot_half(k) * sn, proj[:, 2 * AW:3 * AW])
        bcu_ref[...] = proj[:, 3 * AW:3 * AW + 3 * CW]
        qx_ref[...] = (proj[:, 3 * AW + 3 * CW:] * SCALE).astype(BF16)

    def tile(w):
        return pl.BlockSpec((TQ, w), lambda i: (i, 0))

    return _call_with_gather(
        body, NT, shards, name="in_proj",
        in_specs=[tile(D), _const((1, D)), _const((N_DEV, D, blk)), tile(LANES), tile(LANES)],
        out_specs=[tile(AW), tile(AW), tile(3 * CW), tile(XW), tile(D), _acc((D, PW))],
        out_shape=[jax.ShapeDtypeStruct((S, AW), F32)] * 2 + [
            jax.ShapeDtypeStruct((S, 3 * CW), F32), jax.ShapeDtypeStruct((S, XW), BF16),
            jax.ShapeDtypeStruct((S, D), BF16), jax.ShapeDtypeStruct((D, PW), BF16)],
        scratch_shapes=[pltpu.VMEM((D, PW), BF16)], vmem_mb=56, args=(x, g, w8, cos, sins))


ATTN_PLANS = (("p1", 1, 128, 32), ("p4", 8, 64, 8), ("p16", 16, 128, 2))
PAD = 128
WIN = 256


ATTN_UNROLL = 16


def _fill_bias(tab, qblk, partner):
    qi = lax.broadcasted_iota(jnp.int32, (2 * qblk, WIN), 0) & (qblk - 1)
    kj = lax.broadcasted_iota(jnp.int32, (2 * qblk, WIN), 1)
    piece = kj >> (qblk.bit_length() - 1)
    kk = kj & (qblk - 1)
    prev = (piece & 1) == 0
    of_partner = piece >= 2
    for first in (0, 1):
        for par in (0, 1):
            lo = jnp.where(prev, (qblk if first else qi) + jnp.where(of_partner, par, 0), 0)
            hi = jnp.where(prev, qblk, qi + jnp.where(of_partner, par - 1, 0))
            tab[2 * first + par] = jnp.where((kk >= lo) & (kk <= hi), 0.0, NEG).astype(F32)


def _block_rows(g, qblk, nbc, partner):
    own = pl.ds(pl.multiple_of(PAD + g * qblk, qblk), qblk)
    first = ((g & (nbc - 1)) == 0).astype(jnp.int32)
    if partner:
        gp = jnp.bitwise_xor(g, 4 * nbc)
        wins = (pl.ds(pl.multiple_of(PAD + (g - 1) * qblk, qblk), 2 * qblk),
                pl.ds(pl.multiple_of(PAD + (gp - 1) * qblk, qblk), 2 * qblk))
        return own, wins, 2 * first + ((g >> ((4 * nbc).bit_length() - 1)) & 1)
    return own, (pl.ds(pl.multiple_of(PAD + (g - 1) * qblk, qblk), 2 * qblk),), 2 * first


def _pack_pair(lo, hi):
    lo_bits = lax.bitcast_convert_type(lo.astype(BF16).astype(F32), jnp.uint32) >> 16
    hi_bits = lax.bitcast_convert_type(hi.astype(BF16).astype(F32), jnp.uint32) & jnp.uint32(0xFFFF0000)
    return lax.bitcast_convert_type(hi_bits | lo_bits, F32)


def _unpack_pair(c):
    bits = lax.bitcast_convert_type(c, jnp.uint32)
    lo = lax.bitcast_convert_type(bits << 16, F32).astype(BF16)
    hi = lax.bitcast_convert_type(bits & jnp.uint32(0xFFFF0000), F32).astype(BF16)
    return lo, hi


def _window(ref, wins):
    parts = [ref[w, :] for w in wins]
    return parts[0] if len(parts) == 1 else jnp.concatenate(parts, axis=0)


def _stack_heads(t, lane):
    zero = jnp.zeros_like(t)
    return jnp.concatenate([jnp.where(lane < HEAD, t, zero), jnp.where(lane >= HEAD, t, zero)], axis=0)


def _unstack_heads(t2, lane):
    half = t2.shape[0] // 2
    return jnp.where(lane < HEAD, t2[0:half, :], t2[half:, :])


def _lanes_of(step):
    return pl.ds(pl.multiple_of(step * LANES, LANES), LANES)


def _whole_wait(buf, sem):
    whole = buf.at[pl.ds(PAD, S), :]
    return pltpu.make_async_copy(whole, whole, sem)


def _whole_waits(bufs, sems):
    return [_whole_wait(buf, sems.at[i]) for i, buf in enumerate(bufs)]


def _class_gather(views, bufs, sems, lanes):
    copies = []
    for i, (view, buf) in enumerate(zip(views, bufs)):
        if view.ndim == 2:
            copies.append(pltpu.make_async_copy(view.at[:, lanes], buf.at[pl.ds(PAD, S), :], sems.at[i]))
        else:
            per, n_cls = view.shape[0], view.shape[1]
            copies += [pltpu.make_async_copy(view.at[:, c, lanes], buf.at[pl.ds(PAD + c * per, per), :], sems.at[i])
                       for c in range(n_cls)]
    return copies


def _class_scatter(bufs, dsts, sems, lanes):
    copies = []
    for i, (buf, dst) in enumerate(zip(bufs, dsts)):
        if dst.ndim == 2:
            copies.append(pltpu.make_async_copy(buf.at[pl.ds(PAD, S), :], dst.at[:, lanes], sems.at[i]))
            continue
        per, n_cls = dst.shape[0], dst.shape[1]
        copies += [pltpu.make_async_copy(buf.at[pl.ds(PAD + c * per, per), :], dst.at[:, c, lanes], sems.at[i])
                   for c in range(n_cls)]
    return copies


def _start(copies):
    for cp in copies:
        cp.start()


def _wait(waits):
    for w in waits:
        w.wait()


def _attn_fwd(q, kvp, shards=()):
    views = [[a] + [a.reshape(S // n, n, AW) for _, n, _, _ in ATTN_PLANS[1:]] for a in (q, kvp)]
    flat = [views[a][p] for p in range(3) for a in range(2)]
    ng = len(shards)
    n_grid = AW // LANES

    def body(*refs):
        hbm = [refs[2 * p:2 * p + 2] for p in range(3)]
        refs = refs[6:]
        shard_refs, refs = refs[:ng], refs[ng:]
        y_ref, lt_ref = refs[0:2]
        whole_refs, refs = refs[2:2 + ng], refs[2 + ng:]
        bufs = [refs[2 * p:2 * p + 2] for p in range(3)]
        oc4, lc4, oc16, lc16, tab128, tab4, sem_in = refs[6:13]
        step = pl.program_id(0)
        if ng:
            start_gather, relay_gather, finish_gather = _gather_steps(shard_refs, whole_refs, *refs[13:])
            pl.when(step == 0)(start_gather)
            pl.when(step == n_grid // 2)(relay_gather)
        now = [_class_gather(hbm[p], bufs[p], sem_in.at[p], _lanes_of(step)) for p in range(3)]
        nxt = [_class_gather(hbm[p], bufs[p], sem_in.at[p], _lanes_of(step + 1)) for p in range(3)]

        @pl.when(step == 0)
        def _():
            for p in range(3):
                _start(now[p])
                for b in bufs[p]:
                    b[0:PAD, :] = jnp.zeros((PAD, LANES), F32)
            _fill_bias(tab128, 128, False)
            _fill_bias(tab4, 64, True)

        def prefetch(p):
            pl.when(step + 1 < n_grid)(lambda: _start(nxt[p]))

        lane = lax.broadcasted_iota(jnp.int32, (1, LANES), 1)
        ones = jnp.ones((WIN, LANES), BF16)

        def run(plan, bq, bkv, tab, o_dst, l_dst, dst_pad):
            _, n_cls, qblk, nbc = plan
            partner = n_cls == 8

            def block(g, carry):
                own, wins, mask = _block_rows(g, qblk, nbc, partner)
                q2 = _stack_heads(bq[own, :].astype(BF16), lane)
                kw, vwin = _unpack_pair(_window(bkv, wins))
                vw = jnp.concatenate([vwin, ones], axis=1)
                s = _dot_nt(q2, kw) + tab[mask]
                m = jnp.max(s, axis=1, keepdims=True)
                oe = _dot(jnp.exp(s - m).astype(BF16), vw)
                den = oe[:, LANES:]
                dst = pl.ds(pl.multiple_of(dst_pad + g * qblk, qblk), qblk)
                o_dst[dst, :] = _unstack_heads(oe[:, 0:LANES] / den, lane)
                l_dst[dst, :] = _unstack_heads(m + jnp.log(den), lane)
                return carry
            lax.fori_loop(0, n_cls * nbc, block, 0, unroll=ATTN_UNROLL)

        _wait(_whole_waits(bufs[0], sem_in.at[0]))
        run(ATTN_PLANS[0], *bufs[0], tab128, y_ref, lt_ref, 0)
        prefetch(0)
        _wait(_whole_waits(bufs[1], sem_in.at[1]))
        run(ATTN_PLANS[1], *bufs[1], tab4, oc4, lc4, PAD)
        prefetch(1)
        _wait(_whole_waits(bufs[2], sem_in.at[2]))
        run(ATTN_PLANS[2], *bufs[2], tab128, oc16, lc16, PAD)
        prefetch(2)

        n_rows = 64

        def token_order(buf, t, n_cls):
            per = S // n_cls
            first = PAD + t * (n_rows // n_cls)
            return jnp.concatenate([buf[pl.ds(first + jj, n_cls, stride=per), :] for jj in range(n_rows // n_cls)],
                                   axis=0)

        def combine(t, carry):
            rows = pl.ds(pl.multiple_of(t * n_rows, n_rows), n_rows)
            l0, l1, l2 = lt_ref[rows, :], token_order(lc4, t, 8), token_order(lc16, t, 16)
            lm = jnp.maximum(jnp.maximum(l0, l1), l2)
            e0, e1, e2 = jnp.exp(l0 - lm), jnp.exp(l1 - lm), jnp.exp(l2 - lm)
            den = e0 + e1 + e2
            y_ref[rows, :] = (e0 * y_ref[rows, :] + e1 * token_order(oc4, t, 8)
                              + e2 * token_order(oc16, t, 16)) / den
            lt_ref[rows, :] = lm + jnp.log(den)
            return carry
        lax.fori_loop(0, S // n_rows, combine, 0, unroll=2)

        if ng:
            pl.when(step == n_grid - 1)(finish_gather)

    col = pl.BlockSpec((S, LANES), lambda h: (0, h))
    padded = pltpu.VMEM((PAD + S, LANES), F32)
    return pl.pallas_call(
        body, grid=(n_grid,), name="attn_fwd",
        in_specs=[ANY] * (6 + ng), out_specs=[col, col] + [ANY] * ng,
        out_shape=[jax.ShapeDtypeStruct((S, AW), F32)] * 2 + _gathered_shapes(shards),
        scratch_shapes=[padded] * 10 + [
            pltpu.VMEM((4, 256, WIN), F32), pltpu.VMEM((4, 128, WIN), F32), pltpu.SemaphoreType.DMA((3, 2))]
        + (_gather_scratch(ng) if ng else []),
        compiler_params=_cparams(56))(*flat, *shards)


def _conv_taps(z, zprev, row):
    z1 = jnp.where(row == 0, zprev[7:8, :], pltpu.roll(z, 1, 0))
    z2 = jnp.where(row == 0, zprev[6:7, :], jnp.where(row == 1, zprev[7:8, :], pltpu.roll(z, 2, 0)))
    return z1, z2


def _xattn_scores(qm, km):
    s = _dot_nt(qm, km)
    m = jnp.max(s, axis=1, keepdims=True)
    e = jnp.exp(s - m)
    return e, jnp.sum(e, axis=1, keepdims=True)


def _mix_out(y_attn, bcu, qx16, kv16, cw8, g_attn, g_conv, g_x, g_post, wout16, x, shards):
    def body(ya_ref, bcu_ref, halo_ref, qx_ref, kv_ref, cw_ref, ga_ref, gc_ref, gx_ref, gp_ref, w_ref, x_ref,
             ypre_ref, y16_ref, y2_ref, x1_ref):
        i = pl.program_id(0)
        bcu = bcu_ref[...]
        b, c, u = bcu[:, 0:CW], bcu[:, CW:2 * CW], bcu[:, 2 * CW:]
        z = c * u
        halo = halo_ref[...]
        zprev = jnp.where(i > 0, halo[:, CW:2 * CW] * halo[:, 2 * CW:], 0.0)
        row = lax.broadcasted_iota(jnp.int32, z.shape, 0)
        z1, z2 = _conv_taps(z, zprev, row)
        cw = cw_ref[...]
        y_conv = b * (z2 * cw[0:1, :] + z1 * cw[1:2, :] + z * cw[2:3, :])

        qx = qx_ref[...]
        kv = kv_ref[...]
        km, vm = kv[:, 0:XW], kv[:, XW:]
        lane = lax.broadcasted_iota(jnp.int32, qx.shape, 1)
        y_x = jnp.zeros(qx.shape, F32)
        for h in range(XW // HEAD):
            hm = (lane >= h * HEAD) & (lane < (h + 1) * HEAD)
            e, l = _xattn_scores(jnp.where(hm, qx, jnp.zeros_like(qx)), km)
            y_x = jnp.where(hm, _dot(e.astype(BF16), vm) / l, y_x)

        y_attn = ya_ref[...]
        ypre_ref[:, 0:AW] = y_attn
        ypre_ref[:, AW:AW + CW] = y_conv
        ypre_ref[:, AW + CW:] = y_x
        y = jnp.concatenate([_rms(y_attn, ga_ref[...])[0], _rms(y_conv, gc_ref[...])[0],
                             _rms(y_x, gx_ref[...])[0]], axis=1).astype(BF16)
        y16_ref[...] = y
        y2 = _dot(y, w_ref[...])
        y2_ref[...] = y2
        x1_ref[...] = x_ref[...] + _rms(y2, gp_ref[...])[0]

    def tile(w):
        return pl.BlockSpec((TQ, w), lambda i: (i, 0))

    halo = pl.BlockSpec((SUBLANES, 3 * CW), lambda i: (jnp.maximum(i * (TQ // SUBLANES) - 1, 0), 0))
    return _call_with_gather(
        body, NT, shards, name="mix_out",
        in_specs=[tile(AW), tile(3 * CW), halo, tile(XW), _const((N_MEM, 2 * XW)), _const((SUBLANES, CW)),
                  _const((1, AW)), _const((1, CW)), _const((1, XW)), _const((1, D)), _const((D, D)), tile(D)],
        out_specs=[tile(D), tile(D), tile(D), tile(D)],
        out_shape=[jax.ShapeDtypeStruct((S, D), F32), jax.ShapeDtypeStruct((S, D), BF16),
                   jax.ShapeDtypeStruct((S, D), F32), jax.ShapeDtypeStruct((S, D), F32)],
        scratch_shapes=[], vmem_mb=56,
        args=(y_attn, bcu, bcu, qx16, kv16, cw8, g_attn, g_conv, g_x, g_post, wout16, x))


def _mlp(x1, tgt, g_pre, g_post, wup8, wdn_halves):
    tq = TQ_MLP
    half = D // 2

    def body(x1_ref, t_ref, g1_ref, g2_ref, wu_ref, wda_ref, wdb_ref,
             a16_ref, du_ref, h2_ref, df2_ref, dx1_ref, loss_ref, dg_ref):
        @pl.when(pl.program_id(0) == 0)
        def _():
            loss_ref[...] = jnp.zeros_like(loss_ref)
            dg_ref[...] = jnp.zeros_like(dg_ref)

        x1 = x1_ref[...]
        g1, g2 = g1_ref[...], g2_ref[...]
        y1, n1, r1 = _rms(x1, g1)
        h2 = y1.astype(BF16)
        h2_ref[...] = h2
        f2a = jnp.zeros((tq, half), F32)
        f2b = jnp.zeros((tq, half), F32)
        for j in range(N_DEV):
            cols = slice(j * FF_BLK, (j + 1) * FF_BLK)
            a = jnp.maximum(_dot(h2, wu_ref[j]), 0.0)
            a16_ref[:, cols] = a.astype(BF16)
            f = (a * a).astype(BF16)
            f2a = f2a + _dot(f, wda_ref[cols, :])
            f2b = f2b + _dot(f, wdb_ref[cols, :])
        f2 = jnp.concatenate([f2a, f2b], axis=1)
        y2, n2, r2 = _rms(f2, g2)
        e = x1 + y2 - t_ref[...]
        sq = jnp.sum(jnp.sum(e * e, axis=1, keepdims=True), axis=0, keepdims=True)
        loss_ref[...] += jnp.broadcast_to(sq * (0.5 / D), loss_ref.shape)
        dout = e * (1.0 / D)
        df2, dg2 = _rms_bwd(dout, n2, r2, g2)
        df2_16 = df2.astype(BF16)
        df2_ref[...] = df2_16
        dh2 = jnp.zeros((tq, D), F32)
        for j in range(N_DEV):
            cols = slice(j * FF_BLK, (j + 1) * FF_BLK)
            df = _dot_nt(df2_16[:, 0:half], wda_ref[cols, :]) + _dot_nt(df2_16[:, half:], wdb_ref[cols, :])
            du = (df * (2.0 * a16_ref[:, cols].astype(F32))).astype(BF16)
            du_ref[:, cols] = du
            dh2 = dh2 + _dot_nt(du, wu_ref[j])
        dx, dg1 = _rms_bwd(dh2, n1, r1, g1)
        dx1_ref[...] = dout + dx
        dg_ref[0:1, :] += dg2
        dg_ref[1:2, :] += dg1

    def tile(w):
        return pl.BlockSpec((tq, w), lambda i: (i, 0))

    return pl.pallas_call(
        body, grid=(S // tq,), name="mlp",
        in_specs=[tile(D), tile(D), _const((1, D)), _const((1, D)), _const((N_DEV, D, FF_BLK)), _const((FF, half)), _const((FF, half))],
        out_specs=[tile(FF), tile(FF), tile(D), tile(D), tile(D), _acc((SUBLANES, LANES)), _acc((SUBLANES, D))],
        out_shape=[jax.ShapeDtypeStruct((S, FF), BF16), jax.ShapeDtypeStruct((S, FF), BF16),
                   jax.ShapeDtypeStruct((S, D), BF16), jax.ShapeDtypeStruct((S, D), BF16),
                   jax.ShapeDtypeStruct((S, D), F32), jax.ShapeDtypeStruct((SUBLANES, LANES), F32),
                   jax.ShapeDtypeStruct((SUBLANES, D), F32)],
        compiler_params=_cparams(56))(x1, tgt, g_pre, g_post, wup8, *wdn_halves)


def _mix_out_bwd(dx1, y2, ypre, ltot, head_ones, q, bcu, qx16, kv16, cw8, g_post, g_attn, g_conv, g_x, wout16):
    def body(dx1_ref, y2_ref, ypre_ref, lt_ref, e_ref, q_ref, bcu_ref, halo_ref, qx_ref, kv_ref, cw_ref, gp_ref,
             ga_ref, gc_ref, gx_ref, w_ref, dy2_ref, qdo_ref, ld_ref, dbcu_ref, dqx_ref, dgs_ref, dcw_ref, dkv_ref,
             carry):
        i = pl.program_id(0)

        @pl.when(i == 0)
        def _():
            dgs_ref[...] = jnp.zeros_like(dgs_ref)
            dcw_ref[...] = jnp.zeros_like(dcw_ref)
            dkv_ref[...] = jnp.zeros_like(dkv_ref)
            carry[...] = jnp.zeros_like(carry)

        gp = gp_ref[...]
        _, n, r = _rms(y2_ref[...], gp)
        dy2, dgp = _rms_bwd(dx1_ref[...], n, r, gp)
        dy2_16 = dy2.astype(BF16)
        dy2_ref[...] = dy2_16
        dy = _dot_nt(dy2_16, w_ref[...])

        ypre = ypre_ref[...]
        ga, gc, gx = ga_ref[...], gc_ref[...], gx_ref[...]
        _, na, ra = _rms(ypre[:, 0:AW], ga)
        dya, dga = _rms_bwd(dy[:, 0:AW], na, ra, ga)
        _, nc, rc = _rms(ypre[:, AW:AW + CW], gc)
        dyc, dgc = _rms_bwd(dy[:, AW:AW + CW], nc, rc, gc)
        y_x = ypre[:, AW + CW:]
        _, nx, rx = _rms(y_x, gx)
        dyx, dgx = _rms_bwd(dy[:, AW + CW:], nx, rx, gx)
        qdo_ref[...] = _pack_pair(q_ref[...], dya)
        prod = dya * ypre[:, 0:AW]
        hi = prod.astype(BF16)
        lo = (prod - hi.astype(F32)).astype(BF16)
        head_sum = _dot(hi, e_ref[...]) + _dot(lo, e_ref[...])
        lane_a = lax.broadcasted_iota(jnp.int32, prod.shape, 1)
        ld_ref[...] = jnp.where((lane_a % HEAD) < HEAD // 2, lt_ref[...], head_sum)
        dgs_ref[0:1, :] += dgp
        dgs_ref[1:2, :] += jnp.concatenate([dga, dgc, dgx], axis=1)

        bcu = bcu_ref[...]
        b, c, u = bcu[:, 0:CW], bcu[:, CW:2 * CW], bcu[:, 2 * CW:]
        z = c * u
        halo = halo_ref[...]
        zprev = jnp.where(i < NT - 1, halo[:, CW:2 * CW] * halo[:, 2 * CW:], 0.0)
        row = lax.broadcasted_iota(jnp.int32, z.shape, 0)
        z1, z2 = _conv_taps(z, zprev, row)
        cw = cw_ref[...]
        conv = z2 * cw[0:1, :] + z1 * cw[1:2, :] + z * cw[2:3, :]
        dconv = dyc * b
        nxt = carry[...]
        dn1 = jnp.where(row == TQ - 1, nxt[0:1, :], pltpu.roll(dconv, TQ - 1, 0))
        dn2 = jnp.where(row == TQ - 1, nxt[1:2, :], jnp.where(row == TQ - 2, nxt[0:1, :], pltpu.roll(dconv, TQ - 2, 0)))
        carry[...] = dconv[0:SUBLANES, :]
        dz = dconv * cw[2:3, :] + dn1 * cw[1:2, :] + dn2 * cw[0:1, :]
        dbcu_ref[:, 0:CW] = (dyc * conv).astype(BF16)
        dbcu_ref[:, CW:2 * CW] = (dz * u).astype(BF16)
        dbcu_ref[:, 2 * CW:] = (dz * c).astype(BF16)
        dcw_ref[0:1, :] += jnp.sum(z2 * dconv, axis=0, keepdims=True)
        dcw_ref[1:2, :] += jnp.sum(z1 * dconv, axis=0, keepdims=True)
        dcw_ref[2:3, :] += jnp.sum(z * dconv, axis=0, keepdims=True)

        qx = qx_ref[...]
        kv = kv_ref[...]
        km, vm = kv[:, 0:XW], kv[:, XW:]
        lane = lax.broadcasted_iota(jnp.int32, qx.shape, 1)
        dqx = jnp.zeros(qx.shape, F32)
        dkm = jnp.zeros((N_MEM, XW), F32)
        dvm = jnp.zeros((N_MEM, XW), F32)
        for h in range(XW // HEAD):
            hm = (lane >= h * HEAD) & (lane < (h + 1) * HEAD)
            qm = jnp.where(hm, qx, jnp.zeros_like(qx))
            e, l = _xattn_scores(qm, km)
            p = e / l
            dom = jnp.where(hm, dyx, 0.0)
            do16 = dom.astype(BF16)
            dsum = jnp.sum(dom * y_x, axis=1, keepdims=True)
            ds = (p * (_dot_nt(do16, vm) - dsum)).astype(BF16)
            dqx = jnp.where(hm, _dot(ds, km), dqx)
            dkm = dkm + _dot_tn(ds, qm)
            dvm = dvm + _dot_tn(p.astype(BF16), do16)
        dqx_ref[...] = (dqx * SCALE).astype(BF16)
        dkv_ref[:, 0:XW] += dkm
        dkv_ref[:, XW:] += dvm

    def tile(w):
        return pl.BlockSpec((TQ, w), lambda i: (NT - 1 - i, 0))

    halo = pl.BlockSpec((SUBLANES, 3 * CW), lambda i: (jnp.maximum((NT - 1 - i) * (TQ // SUBLANES) - 1, 0), 0))
    return pl.pallas_call(
        body, grid=(NT,), name="mix_out_bwd",
        in_specs=[tile(D), tile(D), tile(D), tile(AW), _const((AW, AW)), tile(AW), tile(3 * CW), halo, tile(XW),
                  _const((N_MEM, 2 * XW)), _const((SUBLANES, CW)), _const((1, D)), _const((1, AW)), _const((1, CW)),
                  _const((1, XW)), _const((D, D))],
        out_specs=[tile(D), tile(AW), tile(AW), tile(3 * CW), tile(XW), _acc((SUBLANES, D)), _acc((SUBLANES, CW)),
                   _acc((N_MEM, 2 * XW))],
        out_shape=[jax.ShapeDtypeStruct((S, D), BF16), jax.ShapeDtypeStruct((S, AW), F32),
                   jax.ShapeDtypeStruct((S, AW), F32),
                   jax.ShapeDtypeStruct((S, 3 * CW), BF16), jax.ShapeDtypeStruct((S, XW), BF16),
                   jax.ShapeDtypeStruct((SUBLANES, D), F32), jax.ShapeDtypeStruct((SUBLANES, CW), F32),
                   jax.ShapeDtypeStruct((N_MEM, 2 * XW), F32)],
        scratch_shapes=[pltpu.VMEM((SUBLANES, CW), F32)],
        compiler_params=_cparams(56))(dx1, y2, ypre, ltot, head_ones, q, bcu, bcu, qx16, kv16, cw8, g_post, g_attn,
                                      g_conv, g_x, wout16)


def _attn_bwd(qdo, kvp, ld, chip_sums=()):
    n_in = 3
    views = [[a] + [a.reshape(S // n, n, AW) for _, n, _, _ in ATTN_PLANS[1:]] for a in (qdo, kvp, ld)]
    flat = [views[a][p] for p in range(3) for a in range(n_in)]
    ns = len(chip_sums)
    n_grid = AW // LANES

    def body(*refs):
        hbm = [refs[n_in * p:n_in * p + n_in] for p in range(3)]
        refs = refs[3 * n_in:]
        sum_refs, refs = refs[:ns], refs[ns:]
        outs = [refs[3 * p:3 * p + 3] for p in range(3)]
        landed_refs, sc = refs[9:9 + ns], refs[9 + ns:]
        bufs = [sc[3 * p:3 * p + 3] for p in range(3)]
        res = [sc[9 + 3 * p:12 + 3 * p] for p in range(3)]
        tab128, tab4, sem_in, sem_out = sc[18:22]
        step = pl.program_id(0)
        if ns:
            start_chips, finish_chips = _chips_steps(sum_refs, landed_refs, *sc[22:])
            pl.when(step == 0)(start_chips)
        now = [_class_gather(hbm[p], bufs[p], sem_in.at[p], _lanes_of(step)) for p in range(3)]
        nxt = [_class_gather(hbm[p], bufs[p], sem_in.at[p], _lanes_of(step + 1)) for p in range(3)]

        @pl.when(step == 0)
        def _():
            for p in range(3):
                _start(now[p])
                for b in bufs[p]:
                    b[0:PAD, :] = jnp.zeros((PAD, LANES), F32)
            _fill_bias(tab128, 128, False)
            _fill_bias(tab4, 64, True)

        def prefetch(p):
            pl.when(step + 1 < n_grid)(lambda: _start(nxt[p]))

        lane = lax.broadcasted_iota(jnp.int32, (1, LANES), 1)

        def run(plan, plan_bufs, tab, dst):
            _, n_cls, qblk, nbc = plan
            partner = n_cls == 8
            bqdo, bkv, bld = plan_bufs
            rq, rk, rv = dst

            def block(g, carry):
                own, wins, mask = _block_rows(g, qblk, nbc, partner)
                qb, dob = _unpack_pair(bqdo[own, :])
                q2, do2 = _stack_heads(qb, lane), _stack_heads(dob, lane)
                kw, vw = _unpack_pair(_window(bkv, wins))
                ldv = bld[own, :]
                half = HEAD // 2
                lt2 = jnp.concatenate([ldv[:, 0:1], ldv[:, HEAD:HEAD + 1]], axis=0)
                dsum2 = jnp.concatenate([ldv[:, half:half + 1], ldv[:, HEAD + half:HEAD + half + 1]], axis=0)
                p = jnp.exp(_dot_nt(q2, kw) + tab[mask] - lt2)
                ds = (p * (_dot_nt(do2, vw) - dsum2)).astype(BF16)
                rq[own, :] = _unstack_heads(_dot(ds, kw), lane)
                dkw = _dot_tn(ds, q2)
                dvw = _dot_tn(p.astype(BF16), do2)
                n_w = WIN // len(wins)
                for i, w in enumerate(wins):
                    rk[w, :] += dkw[i * n_w:(i + 1) * n_w, :]
                    rv[w, :] += dvw[i * n_w:(i + 1) * n_w, :]
                return carry
            lax.fori_loop(0, n_cls * nbc, block, 0, unroll=ATTN_UNROLL)

        tabs = (tab128, tab4, tab128)
        def drained(p):
            return lambda: _wait(_whole_waits(res[p], sem_out.at[p]))

        for p in range(3):
            pl.when(step > 0)(drained(p))
            for b in res[p][1:]:
                b[...] = jnp.zeros_like(b)
            _wait(_whole_waits(bufs[p], sem_in.at[p]))
            run(ATTN_PLANS[p], bufs[p], tabs[p], res[p])
            prefetch(p)
            _start(_class_scatter(res[p], outs[p], sem_out.at[p], _lanes_of(step)))
        for p in range(3):
            pl.when(step == n_grid - 1)(drained(p))
        if ns:
            pl.when(step == n_grid - 1)(finish_chips)

    padded = pltpu.VMEM((PAD + S, LANES), F32)
    shapes = [jax.ShapeDtypeStruct(views[0][p].shape, F32) for p in range(3) for _ in range(3)]
    out = pl.pallas_call(
        body, grid=(n_grid,), name="attn_bwd",
        in_specs=[ANY] * (3 * n_in + ns), out_specs=[ANY] * (9 + ns),
        out_shape=shapes + _chips_shapes(chip_sums),
        scratch_shapes=[padded] * 18
        + [pltpu.VMEM((4, 256, WIN), F32), pltpu.VMEM((4, 128, WIN), F32),
           pltpu.SemaphoreType.DMA((3, n_in)), pltpu.SemaphoreType.DMA((3, 3))]
        + (_chips_scratch(ns) if ns else []),
        compiler_params=_cparams(56))(*flat, *chip_sums)
    return [o.reshape(S, AW) for o in out[:9]] + list(out[9:])


def _in_proj_bwd(dqkv, dbcu, dqx, cos, sins, w16, x, g, dx1):
    tq = TQ // 2

    def body(*refs):
        parts = refs[0:9]
        dbcu_ref, dqx_ref, c_ref, s_ref, w_ref, x_ref, g_ref, dx1_ref, dp_ref, gx_ref, dg_ref = refs[9:]

        @pl.when(pl.program_id(0) == 0)
        def _():
            dg_ref[...] = jnp.zeros_like(dg_ref)

        dq, dk, dv = (parts[i][...] + parts[3 + i][...] + parts[6 + i][...] for i in range(3))
        cos, sn = _all_heads(c_ref[...]), _all_heads(s_ref[...])
        dqr = dq * SCALE
        dkr = dk
        dp = jnp.concatenate([(dqr * cos + _rot_half(dqr * sn)).astype(BF16),
                              (dkr * cos + _rot_half(dkr * sn)).astype(BF16), dv.astype(BF16),
                              dbcu_ref[...], dqx_ref[...]], axis=1)
        dp_ref[...] = dp
        dh = _dot_nt(dp, w_ref[...])
        g = g_ref[...]
        _, n, r = _rms(x_ref[...], g)
        dx, dg = _rms_bwd(dh, n, r, g)
        gx_ref[...] = dx1_ref[...] + dx
        dg_ref[0:1, :] += dg

    def tile(w):
        return pl.BlockSpec((tq, w), lambda i: (i, 0))

    return pl.pallas_call(
        body, grid=(S // tq,), name="in_proj_bwd",
        in_specs=[tile(AW)] * 9 + [tile(3 * CW), tile(XW), tile(LANES), tile(LANES), _const((D, PW)),
                                   tile(D), _const((1, D)), tile(D)],
        out_specs=[tile(PW), tile(D), _acc((SUBLANES, D))],
        out_shape=[jax.ShapeDtypeStruct((S, PW), BF16), jax.ShapeDtypeStruct((S, D), F32),
                   jax.ShapeDtypeStruct((SUBLANES, D), F32)],
        compiler_params=_cparams(56))(*dqkv, dbcu, dqx, cos, sins, w16, x, g, dx1)


def _mem_bwd(mem, g_mem, wkv16, dkv):
    def body(m_ref, g_ref, w_ref, dkv_ref, dkv16_ref, dg_ref):
        dkv16 = dkv_ref[...].astype(BF16)
        dkv16_ref[...] = dkv16
        _, n, _ = _rms(m_ref[...], g_ref[...])
        dg = jnp.sum(_dot_nt(dkv16, w_ref[...]) * n, axis=0, keepdims=True)
        dg_ref[...] = jnp.broadcast_to(dg, dg_ref.shape)

    return pl.pallas_call(
        body, name="mem_bwd",
        out_shape=[jax.ShapeDtypeStruct((N_MEM, 2 * XW), BF16), jax.ShapeDtypeStruct((SUBLANES, D), F32)],
        compiler_params=pltpu.CompilerParams(vmem_limit_bytes=32 << 20))(mem, g_mem, wkv16, dkv)


N_CHIPS = N_DEV // 2


def _transpose_into(at, a_ref):
    kk = a_ref.shape[0]
    chunk = min(kk, 512)
    for c in range(kk // chunk):
        at[:, c * chunk:(c + 1) * chunk] = a_ref[c * chunk:(c + 1) * chunk, :].T


def _pair_scratch(block):
    return [pltpu.VMEM((N_CHIPS,) + block, BF16), pltpu.VMEM((N_CHIPS,) + block, BF16),
            pltpu.SemaphoreType.DMA((N_CHIPS,)), pltpu.SemaphoreType.DMA((N_CHIPS,))]


def _swap_with_sibling(p, stage, land, send, recv):
    x, y, c = lax.axis_index("x"), lax.axis_index("y"), lax.axis_index("c")
    return pltpu.make_async_remote_copy(src_ref=stage.at[p], dst_ref=land.at[p], send_sem=send.at[p],
                                        recv_sem=recv.at[p], device_id=(x, y, 1 - c), device_id_type=MESH)


def _wgrad_cols(place, a16, b16, blk, name, square_b=False, transpose_out=False, to_chips=False, small=()):
    kk, m = a16.shape
    aligned = blk % LANES == 0
    wide = blk if aligned else -(-(blk + LANES // 2) // LANES) * LANES
    assert aligned or (transpose_out and blk % SUBLANES == 0)
    block = (blk, m) if transpose_out else (m, blk)

    def chip_of(step, my_chip):
        return jnp.bitwise_xor(my_chip, N_CHIPS - 1 - step) if to_chips else step

    def body(pl_ref, a_ref, *refs):
        b_refs, refs = refs[:2 if aligned else 1], refs[2 if aligned else 1:]
        accs, refs = refs[:len(small)], refs[len(small):]
        (cs_ref, own_ref), refs = refs[:2], refs[2:]
        if to_chips:
            landed, refs = refs[0], refs[1:]
        if small:
            tot_ref, refs = refs[0], refs[1:]
        (at, stage, land, send, recv), refs = refs[:5], refs[5:]
        if not aligned:
            (win, wsem), refs = refs[:2], refs[2:]
        if small:
            start_small, finish_small = _small_reduce_steps(accs, tot_ref, *refs[-4:])
            refs = refs[:-4]
        step = pl.program_id(0)
        if small:
            pl.when(step == 0)(start_small)
        x, y, c = lax.axis_index("x"), lax.axis_index("y"), lax.axis_index("c")
        my_chip = 2 * x + y
        p = chip_of(step, my_chip)

        def fetch(at_step, mine):
            j = 2 * chip_of(at_step, my_chip) + (c if mine else 1 - c)
            first = pl.multiple_of(((j * blk) >> 7) << 7, LANES)
            slot = 2 * (at_step & 1) + mine
            return pltpu.make_async_copy(b_refs[0].at[:, pl.ds(first, wide)], win.at[slot], wsem.at[slot])

        @pl.when(step == 0)
        def _():
            if not aligned:
                fetch(0, 0).start()
                fetch(0, 1).start()
            _transpose_into(at, a_ref)

        if not aligned:
            @pl.when(step + 1 < N_CHIPS)
            def _():
                fetch(step + 1, 0).start()
                fetch(step + 1, 1).start()

        def partial(mine):
            if aligned:
                b = b_refs[mine][...]
                if square_b:
                    b = b * b
                acc = _dot(at[...], b)
            else:
                fetch(step, mine).wait()
                acc = _dot(at[...], win[2 * (step & 1) + mine]).T
                odd = c if mine else 1 - c
                return jnp.where(odd == 0, acc[0:blk], acc[wide - blk:wide])
            return acc.T if transpose_out else acc

        stage[p] = partial(0).astype(BF16)
        swap = _swap_with_sibling(p, stage, land, send, recv)
        swap.start()
        mine = partial(1)
        swap.wait()
        total = mine + land[p].astype(F32)
        cs_ref[0] = total.astype(BF16)

        @pl.when(p == my_chip)
        def _():
            own_ref[...] = total

        if to_chips:
            stage2, send2, recv2 = refs
            flipped = jnp.bitwise_xor(p, my_chip)
            k = jnp.where(flipped == 2, 0, jnp.where(flipped == 1, 1, 2))

            def to_owner(src, k_, px, py):
                return pltpu.make_async_remote_copy(src_ref=src, dst_ref=landed.at[k_], send_sem=send2.at[k_],
                                                    recv_sem=recv2.at[k_], device_id=(px, py, c), device_id_type=MESH)

            @pl.when(p != my_chip)
            def _():
                stage2[p] = total.astype(BF16)
                to_owner(stage2.at[p], k, p >> 1, p & 1).start()

            @pl.when(step == N_CHIPS - 1)
            def _():
                for k_ in range(N_CHIPS - 1):
                    to_owner(stage2.at[0], k_, x, y).wait()

        if small:
            pl.when(step == N_CHIPS - 1)(finish_small)

    def b_spec(mine):
        return pl.BlockSpec((kk, blk), lambda i, s: (0, 2 * chip_of(i, s[1]) + (s[0] if mine else 1 - s[0])))

    b_specs, b_args = ([b_spec(0), b_spec(1)], (b16, b16)) if aligned else ([ANY], (b16,))
    scratch = [pltpu.VMEM((m, kk), BF16)] + _pair_scratch(block)
    if not aligned:
        scratch += [pltpu.VMEM((4, kk, wide), BF16), pltpu.SemaphoreType.DMA((4,))]
    out_specs = [pl.BlockSpec((1,) + block, lambda i, s: (chip_of(i, s[1]), 0, 0)), pl.BlockSpec(block, lambda i, s: (0, 0))]
    out_shape = [jax.ShapeDtypeStruct((N_CHIPS,) + block, BF16), jax.ShapeDtypeStruct(block, F32)]
    if to_chips:
        out_specs.append(ANY)
        out_shape.append(jax.ShapeDtypeStruct((N_CHIPS - 1,) + block, BF16))
        scratch += [pltpu.VMEM((N_CHIPS,) + block, BF16), pltpu.SemaphoreType.DMA((N_CHIPS - 1,)),
                    pltpu.SemaphoreType.DMA((N_CHIPS - 1,))]
    small_specs = [pl.BlockSpec(a.shape, lambda i, s: (0, 0)) for a in small]
    if small:
        out_specs.append(pl.BlockSpec((PACK_ROWS, D), lambda i, s: (0, 0)))
        out_shape.append(jax.ShapeDtypeStruct((PACK_ROWS, D), F32))
        scratch += _small_reduce_scratch()
    return pl.pallas_call(
        body, name=name,
        grid_spec=pltpu.PrefetchScalarGridSpec(
            num_scalar_prefetch=1, grid=(N_CHIPS,),
            in_specs=[pl.BlockSpec((kk, m), lambda i, s: (0, 0), pipeline_mode=pl.Buffered(1))] + b_specs + small_specs,
            out_specs=out_specs, scratch_shapes=scratch),
        out_shape=out_shape, compiler_params=_cparams(56))(place, a16, *b_args, *small)


def _wgrad_rows(place, a16, b16, name):
    kk, m = a16.shape
    n = b16.shape[1]
    block = (m // N_DEV, n)

    def body(pl_ref, a_ref, b_ref, cs_ref, own_ref, at, acc, stage, land, send, recv):
        c = pl_ref[0]
        _transpose_into(at, a_ref)
        acc[...] = _dot(at[...], b_ref[...])

        def rows(owner):
            return pl.ds(pl.multiple_of(owner * block[0], block[0]), block[0])

        swaps = []
        for p in range(N_CHIPS):
            stage[p] = acc[rows(2 * p + 1 - c), :].astype(BF16)
            swaps.append(_swap_with_sibling(p, stage, land, send, recv))
            swaps[-1].start()
        for p in range(N_CHIPS):
            swaps[p].wait()
            total = acc[rows(2 * p + c), :] + land[p].astype(F32)
            cs_ref[p] = total.astype(BF16)

            @pl.when(p == pl_ref[1])
            def _():
                own_ref[...] = total

    vmem = pl.BlockSpec(memory_space=pltpu.VMEM)
    return pl.pallas_call(
        body, name=name,
        in_specs=[pl.BlockSpec(memory_space=pltpu.SMEM), vmem, vmem], out_specs=[vmem, vmem],
        out_shape=[jax.ShapeDtypeStruct((N_CHIPS,) + block, BF16), jax.ShapeDtypeStruct(block, F32)],
        scratch_shapes=[pltpu.VMEM((m, kk), BF16), pltpu.VMEM((m, n), F32)] + _pair_scratch(block),
        compiler_params=pltpu.CompilerParams(vmem_limit_bytes=56 << 20))(place, a16, b16)


def _adamw_math(w, g, m, v):
    m = ADAM_B1 * m + (1.0 - ADAM_B1) * g
    v = ADAM_B2 * v + (1.0 - ADAM_B2) * jnp.square(g)
    m_hat = m / (1.0 - ADAM_B1 ** ADAM_STEP)
    v_hat = v / (1.0 - ADAM_B2 ** ADAM_STEP)
    delta = -ADAM_LR * (m_hat / (jnp.sqrt(v_hat) + ADAM_EPS) + ADAM_WD * w)
    return delta, m, v


def _adamw_shards(updates, name, chip_sums=()):
    names, nu, ns = list(updates), len(updates), len(chip_sums)

    def body(*refs):
        ins, sum_refs = refs[:5 * nu], refs[5 * nu:5 * nu + ns]
        outs = refs[5 * nu + ns:9 * nu + ns]
        landed_refs, scratch = refs[9 * nu + ns:9 * nu + 2 * ns], refs[9 * nu + 2 * ns:]
        if ns:
            start_chips, finish_chips = _chips_steps(sum_refs, landed_refs, *scratch)
            start_chips()
        for i in range(nu):
            o_ref, r_ref, w_ref, m_ref, v_ref = ins[5 * i:5 * i + 5]
            g_out, d_out, m_out, v_out = outs[4 * i:4 * i + 4]
            g = o_ref[...] + r_ref[0].astype(F32) + r_ref[1].astype(F32) + r_ref[2].astype(F32)
            g_out[...] = g
            d_out[...], m_out[...], v_out[...] = _adamw_math(w_ref[...], g, m_ref[...], v_ref[...])
        if ns:
            finish_chips()

    vmem = pl.BlockSpec(memory_space=pltpu.VMEM)
    out = pl.pallas_call(
        body, name=name,
        in_specs=[vmem] * (5 * nu) + [ANY] * ns, out_specs=[vmem] * (4 * nu) + [ANY] * ns,
        out_shape=[jax.ShapeDtypeStruct(updates[n][2].shape, F32) for n in names for _ in range(4)]
        + _chips_shapes(chip_sums),
        scratch_shapes=_chips_scratch(ns) if ns else [],
        compiler_params=pltpu.CompilerParams(vmem_limit_bytes=56 << 20),
    )(*[a for n in names for a in updates[n]], *chip_sums)
    return {n: out[4 * i:4 * i + 4] for i, n in enumerate(names)}, list(out[4 * nu:])


def _place():
    x, y, c = lax.axis_index("x"), lax.axis_index("y"), lax.axis_index("c")
    chips = [(1 - x, y), (x, 1 - y), (1 - x, 1 - y)]
    return x, y, c, chips


def _gather_steps(ins, outs, send, recv, lsem):
    nt = len(ins)
    x, y, c, (xn, yn, diag) = _place()
    me, sib = (x, y, c), (x, y, 1 - c)

    def slot(t, px, py, pc):
        return outs[t].at[4 * px + 2 * py + pc]

    def copy(t, k, block, to, src=None):
        return pltpu.make_async_remote_copy(
            src_ref=slot(t, *block) if src is None else src, dst_ref=slot(t, *block),
            send_sem=send.at[t, k], recv_sem=recv.at[t, k], device_id=to, device_id_type=MESH)

    mine = [pltpu.make_async_copy(ins[t], slot(t, *me), lsem.at[t]) for t in range(nt)]
    first = [copy(t, k, me, to, src=ins[t]) for t in range(nt) for k, to in ((0, sib), (1, (*xn, c)), (2, (*yn, c)))]

    def start():
        for cp in mine + first:
            cp.start()

    def landed(k, chip, also_to=None):
        for t in range(nt):
            copy(t, k, (*chip, c), me).wait_recv()
            if also_to is not None:
                copy(t, 3, (*chip, c), (*also_to, c)).start()
            copy(t, 3 + k, (*chip, c), sib).start()

    def relay():
        @pl.when(c == 0)
        def _():
            landed(1, xn, also_to=yn)
            landed(2, yn)

        @pl.when(c == 1)
        def _():
            landed(2, yn, also_to=xn)
            landed(1, xn)

    def finish():
        landed(3, diag)
        for t in range(nt):
            copy(t, 0, sib, me).wait_recv()
            for k, chip in ((4, xn), (5, yn), (6, diag)):
                copy(t, k, (*chip, 1 - c), me).wait_recv()
            for k in range(7):
                copy(t, k, me, sib).wait_send()
        for cp in mine:
            cp.wait()

    return start, relay, finish


def _gather_scratch(nt):
    return [pltpu.SemaphoreType.DMA((nt, 7)), pltpu.SemaphoreType.DMA((nt, 7)), pltpu.SemaphoreType.DMA((nt,))]


def _gathered_shapes(shards):
    return [jax.ShapeDtypeStruct((N_DEV,) + s.shape, s.dtype) for s in shards]


def _call_with_gather(body, n_grid, shards, *, name, in_specs, out_specs, out_shape, scratch_shapes, vmem_mb, args):
    ng, n_in, n_out = len(shards), len(in_specs), len(out_specs)

    def wrapped(*refs):
        ins, shard_refs = refs[:n_in], refs[n_in:n_in + ng]
        outs = refs[n_in + ng:n_in + ng + n_out]
        whole_refs = refs[n_in + ng + n_out:n_in + 2 * ng + n_out]
        scratch = refs[n_in + 2 * ng + n_out:]
        if ng:
            start, relay, finish = _gather_steps(shard_refs, whole_refs, *scratch[len(scratch_shapes):])
            pl.when(pl.program_id(0) == 0)(start)
            pl.when(pl.program_id(0) == n_grid // 2)(relay)
        body(*ins, *outs, *scratch[:len(scratch_shapes)])
        if ng:
            pl.when(pl.program_id(0) == n_grid - 1)(finish)

    return pl.pallas_call(
        wrapped, grid=(n_grid,), name=name,
        in_specs=list(in_specs) + [ANY] * ng, out_specs=list(out_specs) + [ANY] * ng,
        out_shape=list(out_shape) + _gathered_shapes(shards),
        scratch_shapes=list(scratch_shapes) + (_gather_scratch(ng) if ng else []),
        compiler_params=_cparams(vmem_mb))(*args, *shards)


def _chips_steps(ins, outs, send, recv):
    _, _, c, chips = _place()
    copies = [pltpu.make_async_remote_copy(
        src_ref=ins[t].at[2 * px + py], dst_ref=outs[t].at[j], send_sem=send.at[t, j], recv_sem=recv.at[t, j],
        device_id=(px, py, c), device_id_type=MESH) for t in range(len(ins)) for j, (px, py) in enumerate(chips)]

    def start():
        for cp in copies:
            cp.start()

    def finish():
        for cp in copies:
            cp.wait()

    return start, finish


def _chips_scratch(nt):
    return [pltpu.SemaphoreType.DMA((nt, 3)), pltpu.SemaphoreType.DMA((nt, 3))]


def _chips_shapes(cs16s):
    return [jax.ShapeDtypeStruct((3,) + g.shape[1:], g.dtype) for g in cs16s]


SMALL = (("g_pre_mix", 0, 0, D), ("g_mem", 1, 0, D), ("g_post_mix", 2, 0, D), ("g_attn_out", 3, 0, AW),
         ("g_conv_out", 3, AW, CW), ("g_xattn_out", 3, AW + CW, XW), ("g_post_mlp", 4, 0, D), ("g_pre_mlp", 5, 0, D))
CONV_ROW = 8
PACK_ROWS = 16


LOSS_ROW = 15


def _small_reduce_steps(accs, tot_ref, pack, land, send, recv):
    acc_in, acc_mem, acc_mix, acc_mlp, acc_cw, acc_loss = accs
    x, y, c, _ = _place()
    me = 4 * x + 2 * y + c
    copies = []
    for k in range(1, N_DEV):
        kx, ky, kc = (k >> 2) & 1, (k >> 1) & 1, k & 1
        peer = (1 - x if kx else x, 1 - y if ky else y, 1 - c if kc else c)
        copies.append(pltpu.make_async_remote_copy(
            src_ref=pack, dst_ref=land.at[me], send_sem=send.at[k - 1], recv_sem=recv.at[k - 1],
            device_id=peer, device_id_type=MESH))

    def start():
        pack[...] = jnp.zeros_like(pack)
        pack[0:1, :] = acc_in[0:1, :]
        pack[1:2, :] = acc_mem[0:1, :]
        pack[2:4, :] = acc_mix[0:2, :]
        pack[4:6, :] = acc_mlp[0:2, :]
        pack[CONV_ROW:CONV_ROW + 3, 0:CW] = acc_cw[0:3, :]
        pack[LOSS_ROW:LOSS_ROW + 1, 0:LANES] = acc_loss[0:1, :]
        land[me] = pack[...]
        for cp in copies:
            cp.start()

    def finish():
        for cp in copies:
            cp.wait()
        tot = land[0]
        for s in range(1, N_DEV):
            tot = tot + land[s]
        tot_ref[...] = tot

    return start, finish


def _small_reduce_scratch():
    return [pltpu.VMEM((PACK_ROWS, D), F32), pltpu.VMEM((N_DEV, PACK_ROWS, D), F32),
            pltpu.SemaphoreType.DMA((N_DEV - 1,)), pltpu.SemaphoreType.DMA((N_DEV - 1,))]


def _small_update(tot, me, params):
    flat = [a for n, _, _, _ in SMALL for a in params[n]] + list(params["conv_w"])
    n_par = len(SMALL) + 1
    tap_cols = CW // N_DEV

    def body(*refs):
        me_ref, tot_ref = refs[0:2]
        ins = refs[2:2 + 3 * n_par]
        loss_out = refs[2 + 3 * n_par]
        outs = refs[3 + 3 * n_par:]
        tot = tot_ref[...]
        loss_out[...] = jnp.broadcast_to(tot[LOSS_ROW:LOSS_ROW + 1, 0:LANES], loss_out.shape)

        def update(i, g):
            w_ref, m_ref, v_ref = ins[3 * i:3 * i + 3]
            for o_ref, res in zip(outs[4 * i:4 * i + 4], (g,) + _adamw_math(w_ref[...], g, m_ref[...], v_ref[...])):
                if len(o_ref.shape) == 3:
                    for t in range(o_ref.shape[0]):
                        o_ref[t] = res[t:t + 1, :]
                else:
                    o_ref[...] = res

        for i, (_, row, lane0, width) in enumerate(SMALL):
            update(i, tot[row:row + 1, lane0:lane0 + width])
        me = me_ref[0]
        taps = pltpu.roll(tot[CONV_ROW:CONV_ROW + SUBLANES, 0:CW], jnp.where(me == 0, 0, CW - me * tap_cols), 1)
        update(n_par - 1, taps[0:3, 0:tap_cols])

    shapes = [jax.ShapeDtypeStruct(params[n][0].shape, F32) for n, _, _, _ in SMALL] + [
        jax.ShapeDtypeStruct((3, 1, tap_cols), F32)]
    vmem = pl.BlockSpec(memory_space=pltpu.VMEM)
    loss, *out = pl.pallas_call(
        body, name="small_update",
        in_specs=[pl.BlockSpec(memory_space=pltpu.SMEM)] + [vmem] * (1 + 3 * n_par),
        out_shape=[jax.ShapeDtypeStruct((SUBLANES, LANES), F32)] + [s for s in shapes for _ in range(4)],
    )(me, tot, *flat)
    names = [n for n, _, _, _ in SMALL] + ["conv_w"]
    return loss[0, 0], {n: out[4 * i:4 * i + 4] for i, n in enumerate(names)}


def _local_step(x, mem, pos, gains, shards, tgt, place):
    half = HEAD // 2
    inv_freq = jnp.float32(ROPE_THETA) ** (-(jnp.arange(half, dtype=F32) * 2.0 / HEAD))
    invf = jnp.tile(inv_freq, LANES // half)[None, :]
    sgn = jnp.tile(jnp.concatenate([-jnp.ones((half,), F32), jnp.ones((half,), F32)]), LANES // HEAD)[None, :]
    cos, sins, win8 = _rope_table(pos.astype(F32).reshape(S, 1), invf, sgn, [shards["w_in"]])
    wdn_left, wdn_right = shards["w_down"][:, 0:D // 2], shards["w_down"][:, D // 2:]
    q, kvp, bcu, qx16, h16, win16, wout8, wkv8, conv8, wdn8_right = _in_proj(
        x, gains["g_pre_mix"], win8, cos, sins, [shards["w_out"], shards["w_mem_kv"], shards["conv_w"], wdn_right])
    wout16, wkv16 = wout8.reshape(D, D), wkv8.reshape(D, 2 * XW)
    cw_full = conv8[:, 0:3, 0:CW // N_DEV].transpose(1, 0, 2).reshape(3, CW)
    cw8 = jnp.zeros((SUBLANES, CW), F32).at[0:3].set(cw_full)
    y_attn, ltot, wup8, wdn8_left = _attn_fwd(q, kvp, [shards["w_up"], wdn_left])
    wdn_halves = (wdn8_left.reshape(FF, D // 2), wdn8_right.reshape(FF, D // 2))
    memn16, kv16 = _mem_fwd(mem, gains["g_mem"], wkv16)
    ypre, y16, y2, x1 = _mix_out(y_attn, bcu, qx16, kv16, cw8, gains["g_attn_out"], gains["g_conv_out"],
                                 gains["g_xattn_out"], gains["g_post_mix"], wout16, x, [])
    a16, du16, h2_16, df2_16, dx1, loss8, dg_mlp = _mlp(
        x1, tgt, gains["g_pre_mlp"], gains["g_post_mlp"], wup8, wdn_halves)

    sums = {"w_up": _wgrad_cols(place, h2_16, du16, FF_BLK, "wgrad_up"),
            "w_down": _wgrad_cols(place, df2_16, a16, FF_BLK, "wgrad_down", square_b=True, transpose_out=True)}

    head_id = jnp.arange(AW, dtype=jnp.int32) // HEAD
    head_ones = (head_id[:, None] == head_id[None, :]).astype(BF16)
    dy2_16, qdo, ld, dbcu, dqx, dgs, dcw, dkv = _mix_out_bwd(
        dx1, y2, ypre, ltot, head_ones, q, bcu, qx16, kv16, cw8, gains["g_post_mix"], gains["g_attn_out"],
        gains["g_conv_out"], gains["g_xattn_out"], wout16)
    dkv16, dg_mem = _mem_bwd(mem, gains["g_mem"], wkv16, dkv)
    sums["w_mem_kv"] = _wgrad_rows(place, memn16, dkv16, "wgrad_mem_kv")
    sums["w_out"] = _wgrad_rows(place, y16, dy2_16, "wgrad_out")
    out = _attn_bwd(qdo, kvp, ld, [s[0] for s in sums.values()])
    dqkv, landed = out[:9], out[9:]
    reduced = {n: (s[1], landed[t]) for t, (n, s) in enumerate(sums.items())}
    dproj16, grad_x, dg_in = _in_proj_bwd(dqkv, dbcu, dqx, cos, sins, win16, x, gains["g_pre_mix"], dx1)

    _, in_own, in_landed, small_tot = _wgrad_cols(place, h16, dproj16, PW // N_DEV, "wgrad_in", transpose_out=True,
                                                  to_chips=True, small=(dg_in, dg_mem, dgs, dg_mlp, dcw, loss8))
    reduced["w_in"] = (in_own, in_landed)
    return grad_x, reduced, small_tot


BIG = ("w_in", "w_mem_kv", "w_out", "w_up", "w_down")
ORDER = ("g_pre_mix", "g_mem", "w_in", "w_mem_kv", "conv_w", "g_attn_out", "g_conv_out", "g_xattn_out", "w_out",
         "g_post_mix", "g_pre_mlp", "w_up", "w_down", "g_post_mlp")


def kernel(x, mem, positions, g_pre_mix, g_mem, w_in, w_mem_kv, conv_w, g_attn_out, g_conv_out, g_xattn_out, w_out, g_post_mix, g_pre_mlp, w_up, w_down, g_post_mlp, loss_target, m_g_pre_mix, m_g_mem, m_w_in, m_w_mem_kv, m_conv_w, m_g_attn_out, m_g_conv_out, m_g_xattn_out, m_w_out, m_g_post_mix, m_g_pre_mlp, m_w_up, m_w_down, m_g_post_mlp, v_g_pre_mix, v_g_mem, v_w_in, v_w_mem_kv, v_conv_w, v_g_attn_out, v_g_conv_out, v_g_xattn_out, v_w_out, v_g_post_mix, v_g_pre_mlp, v_w_up, v_w_down, v_g_post_mlp):
    w = dict(g_pre_mix=g_pre_mix, g_mem=g_mem, w_in=w_in, w_mem_kv=w_mem_kv, conv_w=conv_w, g_attn_out=g_attn_out,
             g_conv_out=g_conv_out, g_xattn_out=g_xattn_out, w_out=w_out, g_post_mix=g_post_mix, g_pre_mlp=g_pre_mlp,
             w_up=w_up, w_down=w_down, g_post_mlp=g_post_mlp)
    mo = dict(g_pre_mix=m_g_pre_mix, g_mem=m_g_mem, w_in=m_w_in, w_mem_kv=m_w_mem_kv, conv_w=m_conv_w,
              g_attn_out=m_g_attn_out, g_conv_out=m_g_conv_out, g_xattn_out=m_g_xattn_out, w_out=m_w_out,
              g_post_mix=m_g_post_mix, g_pre_mlp=m_g_pre_mlp, w_up=m_w_up, w_down=m_w_down, g_post_mlp=m_g_post_mlp)
    vo = dict(g_pre_mix=v_g_pre_mix, g_mem=v_g_mem, w_in=v_w_in, w_mem_kv=v_w_mem_kv, conv_w=v_conv_w,
              g_attn_out=v_g_attn_out, g_conv_out=v_g_conv_out, g_xattn_out=v_g_xattn_out, w_out=v_w_out,
              g_post_mix=v_g_post_mix, g_pre_mlp=v_g_pre_mlp, w_up=v_w_up, w_down=v_w_down, g_post_mlp=v_g_post_mlp)

    xi, yi, ci = lax.axis_index("x"), lax.axis_index("y"), lax.axis_index("c")
    me = 4 * xi + 2 * yi + ci
    place = jnp.stack([ci, 2 * xi + yi]).astype(jnp.int32)

    shards = {n: w[n][0].astype(BF16) for n in BIG}
    shards["conv_w"] = jnp.zeros((SUBLANES, LANES), F32).at[0:3, 0:CW // N_DEV].set(conv_w[0])

    gains = {n: w[n] for n, _, _, _ in SMALL}
    grad_x, reduced, small_tot = _local_step(x[0], mem[0], positions[0], gains, shards, loss_target[0], place)

    def shard(n, a):
        return a[0].T if n == "w_in" else a[0]

    updated = {}
    for group in (("w_up", "w_down"), ("w_in", "w_out", "w_mem_kv")):
        updated.update(_adamw_shards({n: (*reduced[n], shard(n, w[n]), shard(n, mo[n]), shard(n, vo[n]))
                                      for n in group}, "adamw_" + "_".join(group))[0])
    grad, delta, new_m, new_v = {}, {}, {}, {}
    for n, res in updated.items():
        grad[n], delta[n], new_m[n], new_v[n] = [(a.T if n == "w_in" else a)[None] for a in res]

    params = {n: (w[n], mo[n], vo[n]) for n, _, _, _ in SMALL}
    params["conv_w"] = (w["conv_w"][0], mo["conv_w"][0], vo["conv_w"][0])
    loss, small = _small_update(small_tot, me.reshape(1).astype(jnp.int32), params)
    for n, (g, d_, m_, v_) in small.items():
        lead = (lambda a: a.reshape(conv_w.shape)) if n == "conv_w" else (lambda a: a)
        grad[n], delta[n], new_m[n], new_v[n] = lead(g), lead(d_), lead(m_), lead(v_)

    return (loss, grad_x[None], *[grad[n] for n in ORDER], *[delta[n] for n in ORDER],
            *[new_m[n] for n in ORDER], *[new_v[n] for n in ORDER])
```

```python
import jax
import jax.numpy as jnp
from jax import lax
from jax.experimental import pallas as pl
from jax.experimental.pallas import tpu as pltpu

F32, BF16 = jnp.float32, jnp.bfloat16
MESH = pl.DeviceIdType.MESH
ANY = pl.BlockSpec(memory_space=pl.ANY)

N_DEV = 8
D = 1024
S = 4096
N_MEM = 256
HEAD = 64
AW, CW, XW = 512, 256, 256
PW = 3 * AW + 3 * CW + XW
FF = 4096
FF_BLK = FF // N_DEV
EPS = 1e-6
NEG = -1e30
SCALE = HEAD ** -0.5
ROPE_THETA = 10000.0
LANES = 128
SUBLANES = 8

ADAM_LR, ADAM_B1, ADAM_B2, ADAM_EPS, ADAM_WD, ADAM_STEP = 0.001, 0.9, 0.999, 1e-08, 0.01, 10

TQ = 512
TQ_MLP = 512
NT = S // TQ


def _cparams(vmem_mb, n_grid=1):
    return pltpu.CompilerParams(dimension_semantics=("arbitrary",) * n_grid, vmem_limit_bytes=vmem_mb << 20)


def _const(shape):
    nd = len(shape)
    return pl.BlockSpec(shape, lambda *_: (0,) * nd, pipeline_mode=pl.Buffered(1))


def _acc(shape):
    nd = len(shape)
    return pl.BlockSpec(shape, lambda *_: (0,) * nd)


def _tokens_in_lanes(tq):
    return pl.BlockSpec((D, tq), lambda i: (0, i))


def _dot(a, b):
    return jnp.dot(a, b, preferred_element_type=F32)


def _dot_nt(a, b):
    return lax.dot_general(a, b, (((1,), (1,)), ((), ())), preferred_element_type=F32)


def _dot_tn(a, b):
    return lax.dot_general(a, b, (((0,), (0,)), ((), ())), preferred_element_type=F32)


def _rms(x, g):
    r = lax.rsqrt(jnp.mean(x * x, axis=-1, keepdims=True) + EPS)
    n = x * r
    return n * g, n, r


def _rms_bwd(dy, n, r, g):
    dn = dy * g
    dx = r * (dn - n * jnp.mean(dn * n, axis=-1, keepdims=True))
    return dx, jnp.sum(dy * n, axis=0, keepdims=True)


def _rot_half(t):
    lane = lax.broadcasted_iota(jnp.int32, t.shape, 1)
    n = t.shape[1]
    return jnp.where((lane % HEAD) < HEAD // 2, pltpu.roll(t, n - HEAD // 2, 1), pltpu.roll(t, HEAD // 2, 1))


def _rope_table(pos_col, invf, sgn, shards):
    def body(p_ref, f_ref, s_ref, c_out, s_out):
        ang = p_ref[...] * f_ref[...]
        c_out[...] = jnp.cos(ang)
        s_out[...] = jnp.sin(ang) * s_ref[...]

    tile = pl.BlockSpec((TQ, LANES), lambda i: (i, 0))
    return _call_with_gather(
        body, NT, shards, name="rope_table",
        in_specs=[pl.BlockSpec((TQ, 1), lambda i: (i, 0)), _const((1, LANES)), _const((1, LANES))],
        out_specs=[tile, tile], out_shape=[jax.ShapeDtypeStruct((S, LANES), F32)] * 2,
        scratch_shapes=[], vmem_mb=32, args=(pos_col, invf, sgn))


def _all_heads(t):
    return jnp.tile(t, (1, AW // LANES))


def _mem_fwd(mem, g_mem, wkv16):
    def body(m_ref, g_ref, w_ref, n16_ref, kv_ref):
        y, _, _ = _rms(m_ref[...], g_ref[...])
        y16 = y.astype(BF16)
        n16_ref[...] = y16.T
        kv_ref[...] = _dot(y16, w_ref[...]).astype(BF16)

    return pl.pallas_call(
        body, name="mem_fwd",
        out_shape=[jax.ShapeDtypeStruct((D, N_MEM), BF16), jax.ShapeDtypeStruct((N_MEM, 2 * XW), BF16)],
        compiler_params=pltpu.CompilerParams(vmem_limit_bytes=32 << 20))(mem, g_mem, wkv16)


def _in_proj(x, g, w8, cos, sins, shards):
    blk = PW // N_DEV

    def body(x_ref, g_ref, w8_ref, c_ref, s_ref, q_ref, kv_ref, bcu_ref, qx_ref, h_ref, w_out, w_ref):
        @pl.when(pl.program_id(0) == 0)
        def _():
            for j in range(N_DEV):
                w_ref[:, j * blk:(j + 1) * blk] = w8_ref[j]
            w_out[...] = w_ref[...]

        y, _, _ = _rms(x_ref[...], g_ref[...])
        h = y.astype(BF16)
        h_ref[...] = h.T
        proj = _dot(h, w_ref[...])
        cos, sn = _all_heads(c_ref[...]), _all_heads(s_ref[...])
        q, k = proj[:, 0:AW], proj[:, AW:2 * AW]
        q_ref[...] = (q * cos + _rot_half(q) * sn) * SCALE
        kv_ref[...] = _pack_pair(k * cos + _rot_half(k) * sn, proj[:, 2 * AW:3 * AW])
        bcu_ref[...] = proj[:, 3 * AW:3 * AW + 3 * CW]
        qx_ref[...] = (proj[:, 3 * AW + 3 * CW:] * SCALE).astype(BF16)

    def tile(w):
        return pl.BlockSpec((TQ, w), lambda i: (i, 0))

    return _call_with_gather(
        body, NT, shards, name="in_proj",
        in_specs=[tile(D), _const((1, D)), _const((N_DEV, D, blk)), tile(LANES), tile(LANES)],
        out_specs=[tile(AW), tile(AW), tile(3 * CW), tile(XW), _tokens_in_lanes(TQ), _acc((D, PW))],
        out_shape=[jax.ShapeDtypeStruct((S, AW), F32)] * 2 + [
            jax.ShapeDtypeStruct((S, 3 * CW), F32), jax.ShapeDtypeStruct((S, XW), BF16),
            jax.ShapeDtypeStruct((D, S), BF16), jax.ShapeDtypeStruct((D, PW), BF16)],
        scratch_shapes=[pltpu.VMEM((D, PW), BF16)], vmem_mb=56, args=(x, g, w8, cos, sins))


ATTN_PLANS = (("p1", 1, 128, 32), ("p4", 8, 64, 8), ("p16", 16, 128, 2))
PAD = 128
WIN = 256


ATTN_UNROLL = 16


def _fill_bias(tab, qblk, partner):
    qi = lax.broadcasted_iota(jnp.int32, (2 * qblk, WIN), 0) & (qblk - 1)
    kj = lax.broadcasted_iota(jnp.int32, (2 * qblk, WIN), 1)
    piece = kj >> (qblk.bit_length() - 1)
    kk = kj & (qblk - 1)
    prev = (piece & 1) == 0
    of_partner = piece >= 2
    for first in (0, 1):
        for par in (0, 1):
            lo = jnp.where(prev, (qblk if first else qi) + jnp.where(of_partner, par, 0), 0)
            hi = jnp.where(prev, qblk, qi + jnp.where(of_partner, par - 1, 0))
            tab[2 * first + par] = jnp.where((kk >= lo) & (kk <= hi), 0.0, NEG).astype(F32)


def _block_rows(g, qblk, nbc, partner):
    own = pl.ds(pl.multiple_of(PAD + g * qblk, qblk), qblk)
    first = ((g & (nbc - 1)) == 0).astype(jnp.int32)
    if partner:
        gp = jnp.bitwise_xor(g, 4 * nbc)
        wins = (pl.ds(pl.multiple_of(PAD + (g - 1) * qblk, qblk), 2 * qblk),
                pl.ds(pl.multiple_of(PAD + (gp - 1) * qblk, qblk), 2 * qblk))
        return own, wins, 2 * first + ((g >> ((4 * nbc).bit_length() - 1)) & 1)
    return own, (pl.ds(pl.multiple_of(PAD + (g - 1) * qblk, qblk), 2 * qblk),), 2 * first


def _pack_pair(lo, hi):
    lo_bits = lax.bitcast_convert_type(lo.astype(BF16).astype(F32), jnp.uint32) >> 16
    hi_bits = lax.bitcast_convert_type(hi.astype(BF16).astype(F32), jnp.uint32) & jnp.uint32(0xFFFF0000)
    return lax.bitcast_convert_type(hi_bits | lo_bits, F32)


def _unpack_pair(c):
    bits = lax.bitcast_convert_type(c, jnp.uint32)
    lo = lax.bitcast_convert_type(bits << 16, F32).astype(BF16)
    hi = lax.bitcast_convert_type(bits & jnp.uint32(0xFFFF0000), F32).astype(BF16)
    return lo, hi


def _window(ref, wins):
    parts = [ref[w, :] for w in wins]
    return parts[0] if len(parts) == 1 else jnp.concatenate(parts, axis=0)


def _stack_heads(t, lane):
    zero = jnp.zeros_like(t)
    return jnp.concatenate([jnp.where(lane < HEAD, t, zero), jnp.where(lane >= HEAD, t, zero)], axis=0)


def _unstack_heads(t2, lane):
    half = t2.shape[0] // 2
    return jnp.where(lane < HEAD, t2[0:half, :], t2[half:, :])


def _lanes_of(step):
    return pl.ds(pl.multiple_of(step * LANES, LANES), LANES)


def _whole_wait(buf, sem):
    whole = buf.at[pl.ds(PAD, S), :]
    return pltpu.make_async_copy(whole, whole, sem)


def _whole_waits(bufs, sems):
    return [_whole_wait(buf, sems.at[i]) for i, buf in enumerate(bufs)]


def _class_gather(views, bufs, sems, lanes):
    copies = []
    for i, (view, buf) in enumerate(zip(views, bufs)):
        if view.ndim == 2:
            copies.append(pltpu.make_async_copy(view.at[:, lanes], buf.at[pl.ds(PAD, S), :], sems.at[i]))
        else:
            per, n_cls = view.shape[0], view.shape[1]
            copies += [pltpu.make_async_copy(view.at[:, c, lanes], buf.at[pl.ds(PAD + c * per, per), :], sems.at[i])
                       for c in range(n_cls)]
    return copies


def _class_scatter(bufs, dsts, sems, lanes):
    copies = []
    for i, (buf, dst) in enumerate(zip(bufs, dsts)):
        if dst.ndim == 2:
            copies.append(pltpu.make_async_copy(buf.at[pl.ds(PAD, S), :], dst.at[:, lanes], sems.at[i]))
            continue
        per, n_cls = dst.shape[0], dst.shape[1]
        copies += [pltpu.make_async_copy(buf.at[pl.ds(PAD + c * per, per), :], dst.at[:, c, lanes], sems.at[i])
                   for c in range(n_cls)]
    return copies


def _start(copies):
    for cp in copies:
        cp.start()


def _wait(waits):
    for w in waits:
        w.wait()


def _attn_fwd(q, kvp, shards=()):
    views = [[a] + [a.reshape(S // n, n, AW) for _, n, _, _ in ATTN_PLANS[1:]] for a in (q, kvp)]
    flat = [views[a][p] for p in range(3) for a in range(2)]
    ng = len(shards)
    n_grid = AW // LANES

    def body(*refs):
        hbm = [refs[2 * p:2 * p + 2] for p in range(3)]
        refs = refs[6:]
        shard_refs, refs = refs[:ng], refs[ng:]
        y_ref, lt_ref = refs[0:2]
        whole_refs, refs = refs[2:2 + ng], refs[2 + ng:]
        bufs = [refs[2 * p:2 * p + 2] for p in range(3)]
        oc4, lc4, oc16, lc16, tab128, tab4, sem_in = refs[6:13]
        step = pl.program_id(0)
        if ng:
            start_gather, relay_gather, finish_gather = _gather_steps(shard_refs, whole_refs, *refs[13:])
            pl.when(step == 0)(start_gather)
            pl.when(step == n_grid // 2)(relay_gather)
        now = [_class_gather(hbm[p], bufs[p], sem_in.at[p], _lanes_of(step)) for p in range(3)]
        nxt = [_class_gather(hbm[p], bufs[p], sem_in.at[p], _lanes_of(step + 1)) for p in range(3)]

        @pl.when(step == 0)
        def _():
            for p in range(3):
                _start(now[p])
                for b in bufs[p]:
                    b[0:PAD, :] = jnp.zeros((PAD, LANES), F32)
            _fill_bias(tab128, 128, False)
            _fill_bias(tab4, 64, True)

        def prefetch(p):
            pl.when(step + 1 < n_grid)(lambda: _start(nxt[p]))

        lane = lax.broadcasted_iota(jnp.int32, (1, LANES), 1)
        ones = jnp.ones((WIN, LANES), BF16)

        def run(plan, bq, bkv, tab, o_dst, l_dst, dst_pad):
            _, n_cls, qblk, nbc = plan
            partner = n_cls == 8

            def block(g, carry):
                own, wins, mask = _block_rows(g, qblk, nbc, partner)
                q2 = _stack_heads(bq[own, :].astype(BF16), lane)
                kw, vwin = _unpack_pair(_window(bkv, wins))
                vw = jnp.concatenate([vwin, ones], axis=1)
                s = _dot_nt(q2, kw) + tab[mask]
                m = jnp.max(s, axis=1, keepdims=True)
                oe = _dot(jnp.exp(s - m).astype(BF16), vw)
                den = oe[:, LANES:]
                dst = pl.ds(pl.multiple_of(dst_pad + g * qblk, qblk), qblk)
                o_dst[dst, :] = _unstack_heads(oe[:, 0:LANES] / den, lane)
                l_dst[dst, :] = _unstack_heads(m + jnp.log(den), lane)
                return carry
            lax.fori_loop(0, n_cls * nbc, block, 0, unroll=ATTN_UNROLL)

        _wait(_whole_waits(bufs[0], sem_in.at[0]))
        run(ATTN_PLANS[0], *bufs[0], tab128, y_ref, lt_ref, 0)
        prefetch(0)
        _wait(_whole_waits(bufs[1], sem_in.at[1]))
        run(ATTN_PLANS[1], *bufs[1], tab4, oc4, lc4, PAD)
        prefetch(1)
        _wait(_whole_waits(bufs[2], sem_in.at[2]))
        run(ATTN_PLANS[2], *bufs[2], tab128, oc16, lc16, PAD)
        prefetch(2)

        n_rows = 64

        def token_order(buf, t, n_cls):
            per = S // n_cls
            first = PAD + t * (n_rows // n_cls)
            return jnp.concatenate([buf[pl.ds(first + jj, n_cls, stride=per), :] for jj in range(n_rows // n_cls)],
                                   axis=0)

        def combine(t, carry):
            rows = pl.ds(pl.multiple_of(t * n_rows, n_rows), n_rows)
            l0, l1, l2 = lt_ref[rows, :], token_order(lc4, t, 8), token_order(lc16, t, 16)
            lm = jnp.maximum(jnp.maximum(l0, l1), l2)
            e0, e1, e2 = jnp.exp(l0 - lm), jnp.exp(l1 - lm), jnp.exp(l2 - lm)
            den = e0 + e1 + e2
            y_ref[rows, :] = (e0 * y_ref[rows, :] + e1 * token_order(oc4, t, 8)
                              + e2 * token_order(oc16, t, 16)) / den
            lt_ref[rows, :] = lm + jnp.log(den)
            return carry
        lax.fori_loop(0, S // n_rows, combine, 0, unroll=2)

        if ng:
            pl.when(step == n_grid - 1)(finish_gather)

    col = pl.BlockSpec((S, LANES), lambda h: (0, h))
    padded = pltpu.VMEM((PAD + S, LANES), F32)
    return pl.pallas_call(
        body, grid=(n_grid,), name="attn_fwd",
        in_specs=[ANY] * (6 + ng), out_specs=[col, col] + [ANY] * ng,
        out_shape=[jax.ShapeDtypeStruct((S, AW), F32)] * 2 + _gathered_shapes(shards),
        scratch_shapes=[padded] * 10 + [
            pltpu.VMEM((4, 256, WIN), F32), pltpu.VMEM((4, 128, WIN), F32), pltpu.SemaphoreType.DMA((3, 2))]
        + (_gather_scratch(ng) if ng else []),
        compiler_params=_cparams(56))(*flat, *shards)


def _conv_taps(z, zprev, row):
    z1 = jnp.where(row == 0, zprev[7:8, :], pltpu.roll(z, 1, 0))
    z2 = jnp.where(row == 0, zprev[6:7, :], jnp.where(row == 1, zprev[7:8, :], pltpu.roll(z, 2, 0)))
    return z1, z2


def _xattn_scores(qm, km):
    s = _dot_nt(qm, km)
    m = jnp.max(s, axis=1, keepdims=True)
    e = jnp.exp(s - m)
    return e, jnp.sum(e, axis=1, keepdims=True)


def _mix_out(y_attn, bcu, qx16, kv16, cw8, g_attn, g_conv, g_x, g_post, wout16, x, shards):
    def body(ya_ref, bcu_ref, halo_ref, qx_ref, kv_ref, cw_ref, ga_ref, gc_ref, gx_ref, gp_ref, w_ref, x_ref,
             ypre_ref, y16_ref, y2_ref, x1_ref):
        i = pl.program_id(0)
        bcu = bcu_ref[...]
        b, c, u = bcu[:, 0:CW], bcu[:, CW:2 * CW], bcu[:, 2 * CW:]
        z = c * u
        halo = halo_ref[...]
        zprev = jnp.where(i > 0, halo[:, CW:2 * CW] * halo[:, 2 * CW:], 0.0)
        row = lax.broadcasted_iota(jnp.int32, z.shape, 0)
        z1, z2 = _conv_taps(z, zprev, row)
        cw = cw_ref[...]
        y_conv = b * (z2 * cw[0:1, :] + z1 * cw[1:2, :] + z * cw[2:3, :])

        qx = qx_ref[...]
        kv = kv_ref[...]
        km, vm = kv[:, 0:XW], kv[:, XW:]
        lane = lax.broadcasted_iota(jnp.int32, qx.shape, 1)
        y_x = jnp.zeros(qx.shape, F32)
        for h in range(XW // HEAD):
            hm = (lane >= h * HEAD) & (lane < (h + 1) * HEAD)
            e, l = _xattn_scores(jnp.where(hm, qx, jnp.zeros_like(qx)), km)
            y_x = jnp.where(hm, _dot(e.astype(BF16), vm) / l, y_x)

        y_attn = ya_ref[...]
        ypre_ref[:, 0:AW] = y_attn
        ypre_ref[:, AW:AW + CW] = y_conv
        ypre_ref[:, AW + CW:] = y_x
        y = jnp.concatenate([_rms(y_attn, ga_ref[...])[0], _rms(y_conv, gc_ref[...])[0],
                             _rms(y_x, gx_ref[...])[0]], axis=1).astype(BF16)
        y16_ref[...] = y.T
        y2 = _dot(y, w_ref[...])
        y2_ref[...] = y2
        x1_ref[...] = x_ref[...] + _rms(y2, gp_ref[...])[0]

    def tile(w):
        return pl.BlockSpec((TQ, w), lambda i: (i, 0))

    halo = pl.BlockSpec((SUBLANES, 3 * CW), lambda i: (jnp.maximum(i * (TQ // SUBLANES) - 1, 0), 0))
    return _call_with_gather(
        body, NT, shards, name="mix_out",
        in_specs=[tile(AW), tile(3 * CW), halo, tile(XW), _const((N_MEM, 2 * XW)), _const((SUBLANES, CW)),
                  _const((1, AW)), _const((1, CW)), _const((1, XW)), _const((1, D)), _const((D, D)), tile(D)],
        out_specs=[tile(D), _tokens_in_lanes(TQ), tile(D), tile(D)],
        out_shape=[jax.ShapeDtypeStruct((S, D), F32), jax.ShapeDtypeStruct((D, S), BF16),
                   jax.ShapeDtypeStruct((S, D), F32), jax.ShapeDtypeStruct((S, D), F32)],
        scratch_shapes=[], vmem_mb=56,
        args=(y_attn, bcu, bcu, qx16, kv16, cw8, g_attn, g_conv, g_x, g_post, wout16, x))


def _mlp(x1, tgt, g_pre, g_post, wup8, wdn_halves):
    tq = TQ_MLP
    half = D // 2

    def body(x1_ref, t_ref, g1_ref, g2_ref, wu_ref, wda_ref, wdb_ref,
             a16_ref, du_ref, h2_ref, df2_ref, dx1_ref, loss_ref, dg_ref):
        @pl.when(pl.program_id(0) == 0)
        def _():
            loss_ref[...] = jnp.zeros_like(loss_ref)
            dg_ref[...] = jnp.zeros_like(dg_ref)

        x1 = x1_ref[...]
        g1, g2 = g1_ref[...], g2_ref[...]
        y1, n1, r1 = _rms(x1, g1)
        h2 = y1.astype(BF16)
        h2_ref[...] = h2.T
        f2a = jnp.zeros((tq, half), F32)
        f2b = jnp.zeros((tq, half), F32)
        for j in range(N_DEV):
            cols = slice(j * FF_BLK, (j + 1) * FF_BLK)
            a = jnp.maximum(_dot(h2, wu_ref[j]), 0.0)
            a16_ref[:, cols] = a.astype(BF16)
            f = (a * a).astype(BF16)
            f2a = f2a + _dot(f, wda_ref[cols, :])
            f2b = f2b + _dot(f, wdb_ref[cols, :])
        f2 = jnp.concatenate([f2a, f2b], axis=1)
        y2, n2, r2 = _rms(f2, g2)
        e = x1 + y2 - t_ref[...]
        sq = jnp.sum(jnp.sum(e * e, axis=1, keepdims=True), axis=0, keepdims=True)
        loss_ref[...] += jnp.broadcast_to(sq * (0.5 / D), loss_ref.shape)
        dout = e * (1.0 / D)
        df2, dg2 = _rms_bwd(dout, n2, r2, g2)
        df2_16 = df2.astype(BF16)
        df2_ref[...] = df2_16.T
        dh2 = jnp.zeros((tq, D), F32)
        for j in range(N_DEV):
            cols = slice(j * FF_BLK, (j + 1) * FF_BLK)
            df = _dot_nt(df2_16[:, 0:half], wda_ref[cols, :]) + _dot_nt(df2_16[:, half:], wdb_ref[cols, :])
            du = (df * (2.0 * a16_ref[:, cols].astype(F32))).astype(BF16)
            du_ref[:, cols] = du
            dh2 = dh2 + _dot_nt(du, wu_ref[j])
        dx, dg1 = _rms_bwd(dh2, n1, r1, g1)
        dx1_ref[...] = dout + dx
        dg_ref[0:1, :] += dg2
        dg_ref[1:2, :] += dg1

    def tile(w):
        return pl.BlockSpec((tq, w), lambda i: (i, 0))

    return pl.pallas_call(
        body, grid=(S // tq,), name="mlp",
        in_specs=[tile(D), tile(D), _const((1, D)), _const((1, D)), _const((N_DEV, D, FF_BLK)), _const((FF, half)), _const((FF, half))],
        out_specs=[tile(FF), tile(FF), _tokens_in_lanes(tq), _tokens_in_lanes(tq), tile(D),
                   _acc((SUBLANES, LANES)), _acc((SUBLANES, D))],
        out_shape=[jax.ShapeDtypeStruct((S, FF), BF16), jax.ShapeDtypeStruct((S, FF), BF16),
                   jax.ShapeDtypeStruct((D, S), BF16), jax.ShapeDtypeStruct((D, S), BF16),
                   jax.ShapeDtypeStruct((S, D), F32), jax.ShapeDtypeStruct((SUBLANES, LANES), F32),
                   jax.ShapeDtypeStruct((SUBLANES, D), F32)],
        compiler_params=_cparams(60))(x1, tgt, g_pre, g_post, wup8, *wdn_halves)


def _mix_out_bwd(dx1, y2, ypre, ltot, head_ones, q, bcu, qx16, kv16, cw8, g_post, g_attn, g_conv, g_x, wout16):
    def body(dx1_ref, y2_ref, ypre_ref, lt_ref, e_ref, q_ref, bcu_ref, halo_ref, qx_ref, kv_ref, cw_ref, gp_ref,
             ga_ref, gc_ref, gx_ref, w_ref, dy2_ref, qdo_ref, ld_ref, dbcu_ref, dqx_ref, dgs_ref, dcw_ref, dkv_ref,
             carry):
        i = pl.program_id(0)

        @pl.when(i == 0)
        def _():
            dgs_ref[...] = jnp.zeros_like(dgs_ref)
            dcw_ref[...] = jnp.zeros_like(dcw_ref)
            dkv_ref[...] = jnp.zeros_like(dkv_ref)
            carry[...] = jnp.zeros_like(carry)

        gp = gp_ref[...]
        _, n, r = _rms(y2_ref[...], gp)
        dy2, dgp = _rms_bwd(dx1_ref[...], n, r, gp)
        dy2_16 = dy2.astype(BF16)
        dy2_ref[...] = dy2_16
        dy = _dot_nt(dy2_16, w_ref[...])

        ypre = ypre_ref[...]
        ga, gc, gx = ga_ref[...], gc_ref[...], gx_ref[...]
        _, na, ra = _rms(ypre[:, 0:AW], ga)
        dya, dga = _rms_bwd(dy[:, 0:AW], na, ra, ga)
        _, nc, rc = _rms(ypre[:, AW:AW + CW], gc)
        dyc, dgc = _rms_bwd(dy[:, AW:AW + CW], nc, rc, gc)
        y_x = ypre[:, AW + CW:]
        _, nx, rx = _rms(y_x, gx)
        dyx, dgx = _rms_bwd(dy[:, AW + CW:], nx, rx, gx)
        qdo_ref[...] = _pack_pair(q_ref[...], dya)
        prod = dya * ypre[:, 0:AW]
        hi = prod.astype(BF16)
        lo = (prod - hi.astype(F32)).astype(BF16)
        head_sum = _dot(hi, e_ref[...]) + _dot(lo, e_ref[...])
        lane_a = lax.broadcasted_iota(jnp.int32, prod.shape, 1)
        ld_ref[...] = jnp.where((lane_a % HEAD) < HEAD // 2, lt_ref[...], head_sum)
        dgs_ref[0:1, :] += dgp
        dgs_ref[1:2, :] += jnp.concatenate([dga, dgc, dgx], axis=1)

        bcu = bcu_ref[...]
        b, c, u = bcu[:, 0:CW], bcu[:, CW:2 * CW], bcu[:, 2 * CW:]
        z = c * u
        halo = halo_ref[...]
        zprev = jnp.where(i < NT - 1, halo[:, CW:2 * CW] * halo[:, 2 * CW:], 0.0)
        row = lax.broadcasted_iota(jnp.int32, z.shape, 0)
        z1, z2 = _conv_taps(z, zprev, row)
        cw = cw_ref[...]
        conv = z2 * cw[0:1, :] + z1 * cw[1:2, :] + z * cw[2:3, :]
        dconv = dyc * b
        nxt = carry[...]
        dn1 = jnp.where(row == TQ - 1, nxt[0:1, :], pltpu.roll(dconv, TQ - 1, 0))
        dn2 = jnp.where(row == TQ - 1, nxt[1:2, :], jnp.where(row == TQ - 2, nxt[0:1, :], pltpu.roll(dconv, TQ - 2, 0)))
        carry[...] = dconv[0:SUBLANES, :]
        dz = dconv * cw[2:3, :] + dn1 * cw[1:2, :] + dn2 * cw[0:1, :]
        dbcu_ref[:, 0:CW] = (dyc * conv).astype(BF16)
        dbcu_ref[:, CW:2 * CW] = (dz * u).astype(BF16)
        dbcu_ref[:, 2 * CW:] = (dz * c).astype(BF16)
        dcw_ref[0:1, :] += jnp.sum(z2 * dconv, axis=0, keepdims=True)
        dcw_ref[1:2, :] += jnp.sum(z1 * dconv, axis=0, keepdims=True)
        dcw_ref[2:3, :] += jnp.sum(z * dconv, axis=0, keepdims=True)

        qx = qx_ref[...]
        kv = kv_ref[...]
        km, vm = kv[:, 0:XW], kv[:, XW:]
        lane = lax.broadcasted_iota(jnp.int32, qx.shape, 1)
        dqx = jnp.zeros(qx.shape, F32)
        dkm = jnp.zeros((N_MEM, XW), F32)
        dvm = jnp.zeros((N_MEM, XW), F32)
        for h in range(XW // HEAD):
            hm = (lane >= h * HEAD) & (lane < (h + 1) * HEAD)
            qm = jnp.where(hm, qx, jnp.zeros_like(qx))
            e, l = _xattn_scores(qm, km)
            p = e / l
            dom = jnp.where(hm, dyx, 0.0)
            do16 = dom.astype(BF16)
            dsum = jnp.sum(dom * y_x, axis=1, keepdims=True)
            ds = (p * (_dot_nt(do16, vm) - dsum)).astype(BF16)
            dqx = jnp.where(hm, _dot(ds, km), dqx)
            dkm = dkm + _dot_tn(ds, qm)
            dvm = dvm + _dot_tn(p.astype(BF16), do16)
        dqx_ref[...] = (dqx * SCALE).astype(BF16)
        dkv_ref[:, 0:XW] += dkm
        dkv_ref[:, XW:] += dvm

    def tile(w):
        return pl.BlockSpec((TQ, w), lambda i: (NT - 1 - i, 0))

    halo = pl.BlockSpec((SUBLANES, 3 * CW), lambda i: (jnp.maximum((NT - 1 - i) * (TQ // SUBLANES) - 1, 0), 0))
    return pl.pallas_call(
        body, grid=(NT,), name="mix_out_bwd",
        in_specs=[tile(D), tile(D), tile(D), tile(AW), _const((AW, AW)), tile(AW), tile(3 * CW), halo, tile(XW),
                  _const((N_MEM, 2 * XW)), _const((SUBLANES, CW)), _const((1, D)), _const((1, AW)), _const((1, CW)),
                  _const((1, XW)), _const((D, D))],
        out_specs=[tile(D), tile(AW), tile(AW), tile(3 * CW), tile(XW), _acc((SUBLANES, D)), _acc((SUBLANES, CW)),
                   _acc((N_MEM, 2 * XW))],
        out_shape=[jax.ShapeDtypeStruct((S, D), BF16), jax.ShapeDtypeStruct((S, AW), F32),
                   jax.ShapeDtypeStruct((S, AW), F32),
                   jax.ShapeDtypeStruct((S, 3 * CW), BF16), jax.ShapeDtypeStruct((S, XW), BF16),
                   jax.ShapeDtypeStruct((SUBLANES, D), F32), jax.ShapeDtypeStruct((SUBLANES, CW), F32),
                   jax.ShapeDtypeStruct((N_MEM, 2 * XW), F32)],
        scratch_shapes=[pltpu.VMEM((SUBLANES, CW), F32)],
        compiler_params=_cparams(56))(dx1, y2, ypre, ltot, head_ones, q, bcu, bcu, qx16, kv16, cw8, g_post, g_attn,
                                      g_conv, g_x, wout16)


def _attn_bwd(qdo, kvp, ld, chip_sums=()):
    n_in = 3
    views = [[a] + [a.reshape(S // n, n, AW) for _, n, _, _ in ATTN_PLANS[1:]] for a in (qdo, kvp, ld)]
    flat = [views[a][p] for p in range(3) for a in range(n_in)]
    ns = len(chip_sums)
    n_grid = AW // LANES

    def body(*refs):
        hbm = [refs[n_in * p:n_in * p + n_in] for p in range(3)]
        refs = refs[3 * n_in:]
        sum_refs, refs = refs[:ns], refs[ns:]
        outs = [refs[3 * p:3 * p + 3] for p in range(3)]
        landed_refs, sc = refs[9:9 + ns], refs[9 + ns:]
        bufs = [sc[3 * p:3 * p + 3] for p in range(3)]
        res = [sc[9 + 3 * p:12 + 3 * p] for p in range(3)]
        tab128, tab4, sem_in, sem_out = sc[18:22]
        step = pl.program_id(0)
        if ns:
            start_chips, finish_chips = _chips_steps(sum_refs, landed_refs, *sc[22:])
            pl.when(step == 0)(start_chips)
        now = [_class_gather(hbm[p], bufs[p], sem_in.at[p], _lanes_of(step)) for p in range(3)]
        nxt = [_class_gather(hbm[p], bufs[p], sem_in.at[p], _lanes_of(step + 1)) for p in range(3)]

        @pl.when(step == 0)
        def _():
            for p in range(3):
                _start(now[p])
                for b in bufs[p]:
                    b[0:PAD, :] = jnp.zeros((PAD, LANES), F32)
            _fill_bias(tab128, 128, False)
            _fill_bias(tab4, 64, True)

        def prefetch(p):
            pl.when(step + 1 < n_grid)(lambda: _start(nxt[p]))

        lane = lax.broadcasted_iota(jnp.int32, (1, LANES), 1)

        def run(plan, plan_bufs, tab, dst):
            _, n_cls, qblk, nbc = plan
            partner = n_cls == 8
            bqdo, bkv, bld = plan_bufs
            rq, rk, rv = dst

            def block(g, carry):
                own, wins, mask = _block_rows(g, qblk, nbc, partner)
                qb, dob = _unpack_pair(bqdo[own, :])
                q2, do2 = _stack_heads(qb, lane), _stack_heads(dob, lane)
                kw, vw = _unpack_pair(_window(bkv, wins))
                ldv = bld[own, :]
                half = HEAD // 2
                lt2 = jnp.concatenate([ldv[:, 0:1], ldv[:, HEAD:HEAD + 1]], axis=0)
                dsum2 = jnp.concatenate([ldv[:, half:half + 1], ldv[:, HEAD + half:HEAD + half + 1]], axis=0)
                p = jnp.exp(_dot_nt(q2, kw) + tab[mask] - lt2)
                ds = (p * (_dot_nt(do2, vw) - dsum2)).astype(BF16)
                rq[own, :] = _unstack_heads(_dot(ds, kw), lane)
                dkw = _dot_tn(ds, q2)
                dvw = _dot_tn(p.astype(BF16), do2)
                n_w = WIN // len(wins)
                for i, w in enumerate(wins):
                    rk[w, :] += dkw[i * n_w:(i + 1) * n_w, :]
                    rv[w, :] += dvw[i * n_w:(i + 1) * n_w, :]
                return carry
            lax.fori_loop(0, n_cls * nbc, block, 0, unroll=ATTN_UNROLL)

        tabs = (tab128, tab4, tab128)
        def drained(p):
            return lambda: _wait(_whole_waits(res[p], sem_out.at[p]))

        for p in range(3):
            pl.when(step > 0)(drained(p))
            for b in res[p][1:]:
                b[...] = jnp.zeros_like(b)
            _wait(_whole_waits(bufs[p], sem_in.at[p]))
            run(ATTN_PLANS[p], bufs[p], tabs[p], res[p])
            prefetch(p)
            _start(_class_scatter(res[p], outs[p], sem_out.at[p], _lanes_of(step)))
        for p in range(3):
            pl.when(step == n_grid - 1)(drained(p))
        if ns:
            pl.when(step == n_grid - 1)(finish_chips)

    padded = pltpu.VMEM((PAD + S, LANES), F32)
    shapes = [jax.ShapeDtypeStruct(views[0][p].shape, F32) for p in range(3) for _ in range(3)]
    out = pl.pallas_call(
        body, grid=(n_grid,), name="attn_bwd",
        in_specs=[ANY] * (3 * n_in + ns), out_specs=[ANY] * (9 + ns),
        out_shape=shapes + _chips_shapes(chip_sums),
        scratch_shapes=[padded] * 18
        + [pltpu.VMEM((4, 256, WIN), F32), pltpu.VMEM((4, 128, WIN), F32),
           pltpu.SemaphoreType.DMA((3, n_in)), pltpu.SemaphoreType.DMA((3, 3))]
        + (_chips_scratch(ns) if ns else []),
        compiler_params=_cparams(56))(*flat, *chip_sums)
    return [o.reshape(S, AW) for o in out[:9]] + list(out[9:])


def _in_proj_bwd(dqkv, dbcu, dqx, cos, sins, w16, x, g, dx1):
    tq = TQ // 2

    def body(*refs):
        parts = refs[0:9]
        dbcu_ref, dqx_ref, c_ref, s_ref, w_ref, x_ref, g_ref, dx1_ref, dp_ref, gx_ref, dg_ref = refs[9:]

        @pl.when(pl.program_id(0) == 0)
        def _():
            dg_ref[...] = jnp.zeros_like(dg_ref)

        dq, dk, dv = (parts[i][...] + parts[3 + i][...] + parts[6 + i][...] for i in range(3))
        cos, sn = _all_heads(c_ref[...]), _all_heads(s_ref[...])
        dqr = dq * SCALE
        dkr = dk
        dp = jnp.concatenate([(dqr * cos + _rot_half(dqr * sn)).astype(BF16),
                              (dkr * cos + _rot_half(dkr * sn)).astype(BF16), dv.astype(BF16),
                              dbcu_ref[...], dqx_ref[...]], axis=1)
        dp_ref[...] = dp
        dh = _dot_nt(dp, w_ref[...])
        g = g_ref[...]
        _, n, r = _rms(x_ref[...], g)
        dx, dg = _rms_bwd(dh, n, r, g)
        gx_ref[...] = dx1_ref[...] + dx
        dg_ref[0:1, :] += dg

    def tile(w):
        return pl.BlockSpec((tq, w), lambda i: (i, 0))

    return pl.pallas_call(
        body, grid=(S // tq,), name="in_proj_bwd",
        in_specs=[tile(AW)] * 9 + [tile(3 * CW), tile(XW), tile(LANES), tile(LANES), _const((D, PW)),
                                   tile(D), _const((1, D)), tile(D)],
        out_specs=[tile(PW), tile(D), _acc((SUBLANES, D))],
        out_shape=[jax.ShapeDtypeStruct((S, PW), BF16), jax.ShapeDtypeStruct((S, D), F32),
                   jax.ShapeDtypeStruct((SUBLANES, D), F32)],
        compiler_params=_cparams(56))(*dqkv, dbcu, dqx, cos, sins, w16, x, g, dx1)


def _mem_bwd(mem, g_mem, wkv16, dkv):
    def body(m_ref, g_ref, w_ref, dkv_ref, dkv16_ref, dg_ref):
        dkv16 = dkv_ref[...].astype(BF16)
        dkv16_ref[...] = dkv16
        _, n, _ = _rms(m_ref[...], g_ref[...])
        dg = jnp.sum(_dot_nt(dkv16, w_ref[...]) * n, axis=0, keepdims=True)
        dg_ref[...] = jnp.broadcast_to(dg, dg_ref.shape)

    return pl.pallas_call(
        body, name="mem_bwd",
        out_shape=[jax.ShapeDtypeStruct((N_MEM, 2 * XW), BF16), jax.ShapeDtypeStruct((SUBLANES, D), F32)],
        compiler_params=pltpu.CompilerParams(vmem_limit_bytes=32 << 20))(mem, g_mem, wkv16, dkv)


N_CHIPS = N_DEV // 2


def _pair_scratch(block):
    return [pltpu.VMEM((N_CHIPS,) + block, BF16), pltpu.VMEM((N_CHIPS,) + block, BF16),
            pltpu.SemaphoreType.DMA((N_CHIPS,)), pltpu.SemaphoreType.DMA((N_CHIPS,))]


def _swap_with_sibling(p, stage, land, send, recv):
    x, y, c = lax.axis_index("x"), lax.axis_index("y"), lax.axis_index("c")
    return pltpu.make_async_remote_copy(src_ref=stage.at[p], dst_ref=land.at[p], send_sem=send.at[p],
                                        recv_sem=recv.at[p], device_id=(x, y, 1 - c), device_id_type=MESH)


def _wgrad_cols(place, at16, b16, blk, name, square_b=False, transpose_out=False, to_chips=False, small=()):
    m, kk = at16.shape
    aligned = blk % LANES == 0
    wide = blk if aligned else -(-(blk + LANES // 2) // LANES) * LANES
    assert aligned or (transpose_out and blk % SUBLANES == 0)
    block = (blk, m) if transpose_out else (m, blk)

    def chip_of(step, my_chip):
        return jnp.bitwise_xor(my_chip, N_CHIPS - 1 - step) if to_chips else step

    def body(pl_ref, a_ref, *refs):
        b_refs, refs = refs[:2 if aligned else 1], refs[2 if aligned else 1:]
        accs, refs = refs[:len(small)], refs[len(small):]
        (cs_ref, own_ref), refs = refs[:2], refs[2:]
        if to_chips:
            landed, refs = refs[0], refs[1:]
        if small:
            tot_ref, refs = refs[0], refs[1:]
        (stage, land, send, recv), refs = refs[:4], refs[4:]
        if not aligned:
            (win, wsem), refs = refs[:2], refs[2:]
        if small:
            start_small, finish_small = _small_reduce_steps(accs, tot_ref, *refs[-4:])
            refs = refs[:-4]
        step = pl.program_id(0)
        if small:
            pl.when(step == 0)(start_small)
        x, y, c = lax.axis_index("x"), lax.axis_index("y"), lax.axis_index("c")
        my_chip = 2 * x + y
        p = chip_of(step, my_chip)

        def fetch(at_step, mine):
            j = 2 * chip_of(at_step, my_chip) + (c if mine else 1 - c)
            first = pl.multiple_of(((j * blk) >> 7) << 7, LANES)
            slot = 2 * (at_step & 1) + mine
            return pltpu.make_async_copy(b_refs[0].at[:, pl.ds(first, wide)], win.at[slot], wsem.at[slot])

        if not aligned:
            @pl.when(step == 0)
            def _():
                fetch(0, 0).start()
                fetch(0, 1).start()

            @pl.when(step + 1 < N_CHIPS)
            def _():
                fetch(step + 1, 0).start()
                fetch(step + 1, 1).start()

        def partial(mine):
            if aligned:
                b = b_refs[mine][...]
                if square_b:
                    b = b * b
                acc = _dot(a_ref[...], b)
            else:
                fetch(step, mine).wait()
                acc = _dot(a_ref[...], win[2 * (step & 1) + mine]).T
                odd = c if mine else 1 - c
                return jnp.where(odd == 0, acc[0:blk], acc[wide - blk:wide])
            return acc.T if transpose_out else acc

        stage[p] = partial(0).astype(BF16)
        swap = _swap_with_sibling(p, stage, land, send, recv)
        swap.start()
        mine = partial(1)
        swap.wait()
        total = mine + land[p].astype(F32)
        cs_ref[0] = total.astype(BF16)

        @pl.when(p == my_chip)
        def _():
            own_ref[...] = total

        if to_chips:
            stage2, send2, recv2 = refs
            flipped = jnp.bitwise_xor(p, my_chip)
            k = jnp.where(flipped == 2, 0, jnp.where(flipped == 1, 1, 2))

            def to_owner(src, k_, px, py):
                return pltpu.make_async_remote_copy(src_ref=src, dst_ref=landed.at[k_], send_sem=send2.at[k_],
                                                    recv_sem=recv2.at[k_], device_id=(px, py, c), device_id_type=MESH)

            @pl.when(p != my_chip)
            def _():
                stage2[p] = total.astype(BF16)
                to_owner(stage2.at[p], k, p >> 1, p & 1).start()

            @pl.when(step == N_CHIPS - 1)
            def _():
                for k_ in range(N_CHIPS - 1):
                    to_owner(stage2.at[0], k_, x, y).wait()

        if small:
            pl.when(step == N_CHIPS - 1)(finish_small)

    def b_spec(mine):
        return pl.BlockSpec((kk, blk), lambda i, s: (0, 2 * chip_of(i, s[1]) + (s[0] if mine else 1 - s[0])))

    b_specs, b_args = ([b_spec(0), b_spec(1)], (b16, b16)) if aligned else ([ANY], (b16,))
    scratch = _pair_scratch(block)
    if not aligned:
        scratch += [pltpu.VMEM((4, kk, wide), BF16), pltpu.SemaphoreType.DMA((4,))]
    out_specs = [pl.BlockSpec((1,) + block, lambda i, s: (chip_of(i, s[1]), 0, 0)), pl.BlockSpec(block, lambda i, s: (0, 0))]
    out_shape = [jax.ShapeDtypeStruct((N_CHIPS,) + block, BF16), jax.ShapeDtypeStruct(block, F32)]
    if to_chips:
        out_specs.append(ANY)
        out_shape.append(jax.ShapeDtypeStruct((N_CHIPS - 1,) + block, BF16))
        scratch += [pltpu.VMEM((N_CHIPS,) + block, BF16), pltpu.SemaphoreType.DMA((N_CHIPS - 1,)),
                    pltpu.SemaphoreType.DMA((N_CHIPS - 1,))]
    small_specs = [pl.BlockSpec(a.shape, lambda i, s: (0, 0)) for a in small]
    if small:
        out_specs.append(pl.BlockSpec((PACK_ROWS, D), lambda i, s: (0, 0)))
        out_shape.append(jax.ShapeDtypeStruct((PACK_ROWS, D), F32))
        scratch += _small_reduce_scratch()
    return pl.pallas_call(
        body, name=name,
        grid_spec=pltpu.PrefetchScalarGridSpec(
            num_scalar_prefetch=1, grid=(N_CHIPS,),
            in_specs=[pl.BlockSpec((m, kk), lambda i, s: (0, 0), pipeline_mode=pl.Buffered(1))] + b_specs + small_specs,
            out_specs=out_specs, scratch_shapes=scratch),
        out_shape=out_shape, compiler_params=_cparams(56))(place, at16, *b_args, *small)


def _wgrad_rows(place, at16, b16, name):
    m, kk = at16.shape
    n = b16.shape[1]
    block = (m // N_DEV, n)

    def body(pl_ref, a_ref, b_ref, cs_ref, own_ref, acc, stage, land, send, recv):
        c = pl_ref[0]
        acc[...] = _dot(a_ref[...], b_ref[...])

        def rows(owner):
            return pl.ds(pl.multiple_of(owner * block[0], block[0]), block[0])

        swaps = []
        for p in range(N_CHIPS):
            stage[p] = acc[rows(2 * p + 1 - c), :].astype(BF16)
            swaps.append(_swap_with_sibling(p, stage, land, send, recv))
            swaps[-1].start()
        for p in range(N_CHIPS):
            swaps[p].wait()
            total = acc[rows(2 * p + c), :] + land[p].astype(F32)
            cs_ref[p] = total.astype(BF16)

            @pl.when(p == pl_ref[1])
            def _():
                own_ref[...] = total

    vmem = pl.BlockSpec(memory_space=pltpu.VMEM)
    return pl.pallas_call(
        body, name=name,
        in_specs=[pl.BlockSpec(memory_space=pltpu.SMEM), vmem, vmem], out_specs=[vmem, vmem],
        out_shape=[jax.ShapeDtypeStruct((N_CHIPS,) + block, BF16), jax.ShapeDtypeStruct(block, F32)],
        scratch_shapes=[pltpu.VMEM((m, n), F32)] + _pair_scratch(block),
        compiler_params=pltpu.CompilerParams(vmem_limit_bytes=56 << 20))(place, at16, b16)


def _adamw_math(w, g, m, v):
    m = ADAM_B1 * m + (1.0 - ADAM_B1) * g
    v = ADAM_B2 * v + (1.0 - ADAM_B2) * jnp.square(g)
    m_hat = m / (1.0 - ADAM_B1 ** ADAM_STEP)
    v_hat = v / (1.0 - ADAM_B2 ** ADAM_STEP)
    delta = -ADAM_LR * (m_hat / (jnp.sqrt(v_hat) + ADAM_EPS) + ADAM_WD * w)
    return delta, m, v


def _adamw_shards(updates, name, chip_sums=()):
    names, nu, ns = list(updates), len(updates), len(chip_sums)

    def body(*refs):
        ins, sum_refs = refs[:5 * nu], refs[5 * nu:5 * nu + ns]
        outs = refs[5 * nu + ns:9 * nu + ns]
        landed_refs, scratch = refs[9 * nu + ns:9 * nu + 2 * ns], refs[9 * nu + 2 * ns:]
        if ns:
            start_chips, finish_chips = _chips_steps(sum_refs, landed_refs, *scratch)
            start_chips()
        for i in range(nu):
            o_ref, r_ref, w_ref, m_ref, v_ref = ins[5 * i:5 * i + 5]
            g_out, d_out, m_out, v_out = outs[4 * i:4 * i + 4]
            g = o_ref[...] + r_ref[0].astype(F32) + r_ref[1].astype(F32) + r_ref[2].astype(F32)
            g_out[...] = g
            d_out[...], m_out[...], v_out[...] = _adamw_math(w_ref[...], g, m_ref[...], v_ref[...])
        if ns:
            finish_chips()

    vmem = pl.BlockSpec(memory_space=pltpu.VMEM)
    out = pl.pallas_call(
        body, name=name,
        in_specs=[vmem] * (5 * nu) + [ANY] * ns, out_specs=[vmem] * (4 * nu) + [ANY] * ns,
        out_shape=[jax.ShapeDtypeStruct(updates[n][2].shape, F32) for n in names for _ in range(4)]
        + _chips_shapes(chip_sums),
        scratch_shapes=_chips_scratch(ns) if ns else [],
        compiler_params=pltpu.CompilerParams(vmem_limit_bytes=56 << 20),
    )(*[a for n in names for a in updates[n]], *chip_sums)
    return {n: out[4 * i:4 * i + 4] for i, n in enumerate(names)}, list(out[4 * nu:])


def _place():
    x, y, c = lax.axis_index("x"), lax.axis_index("y"), lax.axis_index("c")
    chips = [(1 - x, y), (x, 1 - y), (1 - x, 1 - y)]
    return x, y, c, chips


def _gather_steps(ins, outs, send, recv, lsem):
    nt = len(ins)
    x, y, c, (xn, yn, diag) = _place()
    me, sib = (x, y, c), (x, y, 1 - c)

    def slot(t, px, py, pc):
        return outs[t].at[4 * px + 2 * py + pc]

    def copy(t, k, block, to, src=None):
        return pltpu.make_async_remote_copy(
            src_ref=slot(t, *block) if src is None else src, dst_ref=slot(t, *block),
            send_sem=send.at[t, k], recv_sem=recv.at[t, k], device_id=to, device_id_type=MESH)

    mine = [pltpu.make_async_copy(ins[t], slot(t, *me), lsem.at[t]) for t in range(nt)]
    first = [copy(t, k, me, to, src=ins[t]) for t in range(nt) for k, to in ((0, sib), (1, (*xn, c)), (2, (*yn, c)))]

    def start():
        for cp in mine + first:
            cp.start()

    def landed(k, chip, also_to=None):
        for t in range(nt):
            copy(t, k, (*chip, c), me).wait_recv()
            if also_to is not None:
                copy(t, 3, (*chip, c), (*also_to, c)).start()
            copy(t, 3 + k, (*chip, c), sib).start()

    def relay():
        @pl.when(c == 0)
        def _():
            landed(1, xn, also_to=yn)
            landed(2, yn)

        @pl.when(c == 1)
        def _():
            landed(2, yn, also_to=xn)
            landed(1, xn)

    def finish():
        landed(3, diag)
        for t in range(nt):
            copy(t, 0, sib, me).wait_recv()
            for k, chip in ((4, xn), (5, yn), (6, diag)):
                copy(t, k, (*chip, 1 - c), me).wait_recv()
            for k in range(7):
                copy(t, k, me, sib).wait_send()
        for cp in mine:
            cp.wait()

    return start, relay, finish


def _gather_scratch(nt):
    return [pltpu.SemaphoreType.DMA((nt, 7)), pltpu.SemaphoreType.DMA((nt, 7)), pltpu.SemaphoreType.DMA((nt,))]


def _gathered_shapes(shards):
    return [jax.ShapeDtypeStruct((N_DEV,) + s.shape, s.dtype) for s in shards]


def _call_with_gather(body, n_grid, shards, *, name, in_specs, out_specs, out_shape, scratch_shapes, vmem_mb, args):
    ng, n_in, n_out = len(shards), len(in_specs), len(out_specs)

    def wrapped(*refs):
        ins, shard_refs = refs[:n_in], refs[n_in:n_in + ng]
        outs = refs[n_in + ng:n_in + ng + n_out]
        whole_refs = refs[n_in + ng + n_out:n_in + 2 * ng + n_out]
        scratch = refs[n_in + 2 * ng + n_out:]
        if ng:
            start, relay, finish = _gather_steps(shard_refs, whole_refs, *scratch[len(scratch_shapes):])
            pl.when(pl.program_id(0) == 0)(start)
            pl.when(pl.program_id(0) == n_grid // 2)(relay)
        body(*ins, *outs, *scratch[:len(scratch_shapes)])
        if ng:
            pl.when(pl.program_id(0) == n_grid - 1)(finish)

    return pl.pallas_call(
        wrapped, grid=(n_grid,), name=name,
        in_specs=list(in_specs) + [ANY] * ng, out_specs=list(out_specs) + [ANY] * ng,
        out_shape=list(out_shape) + _gathered_shapes(shards),
        scratch_shapes=list(scratch_shapes) + (_gather_scratch(ng) if ng else []),
        compiler_params=_cparams(vmem_mb))(*args, *shards)


def _chips_steps(ins, outs, send, recv):
    _, _, c, chips = _place()
    copies = [pltpu.make_async_remote_copy(
        src_ref=ins[t].at[2 * px + py], dst_ref=outs[t].at[j], send_sem=send.at[t, j], recv_sem=recv.at[t, j],
        device_id=(px, py, c), device_id_type=MESH) for t in range(len(ins)) for j, (px, py) in enumerate(chips)]

    def start():
        for cp in copies:
            cp.start()

    def finish():
        for cp in copies:
            cp.wait()

    return start, finish


def _chips_scratch(nt):
    return [pltpu.SemaphoreType.DMA((nt, 3)), pltpu.SemaphoreType.DMA((nt, 3))]


def _chips_shapes(cs16s):
    return [jax.ShapeDtypeStruct((3,) + g.shape[1:], g.dtype) for g in cs16s]


SMALL = (("g_pre_mix", 0, 0, D), ("g_mem", 1, 0, D), ("g_post_mix", 2, 0, D), ("g_attn_out", 3, 0, AW),
         ("g_conv_out", 3, AW, CW), ("g_xattn_out", 3, AW + CW, XW), ("g_post_mlp", 4, 0, D), ("g_pre_mlp", 5, 0, D))
CONV_ROW = 8
PACK_ROWS = 16


LOSS_ROW = 15


def _small_reduce_steps(accs, tot_ref, pack, land, send, recv):
    acc_in, acc_mem, acc_mix, acc_mlp, acc_cw, acc_loss = accs
    x, y, c, _ = _place()
    me = 4 * x + 2 * y + c
    copies = []
    for k in range(1, N_DEV):
        kx, ky, kc = (k >> 2) & 1, (k >> 1) & 1, k & 1
        peer = (1 - x if kx else x, 1 - y if ky else y, 1 - c if kc else c)
        copies.append(pltpu.make_async_remote_copy(
            src_ref=pack, dst_ref=land.at[me], send_sem=send.at[k - 1], recv_sem=recv.at[k - 1],
            device_id=peer, device_id_type=MESH))

    def start():
        pack[...] = jnp.zeros_like(pack)
        pack[0:1, :] = acc_in[0:1, :]
        pack[1:2, :] = acc_mem[0:1, :]
        pack[2:4, :] = acc_mix[0:2, :]
        pack[4:6, :] = acc_mlp[0:2, :]
        pack[CONV_ROW:CONV_ROW + 3, 0:CW] = acc_cw[0:3, :]
        pack[LOSS_ROW:LOSS_ROW + 1, 0:LANES] = acc_loss[0:1, :]
        land[me] = pack[...]
        for cp in copies:
            cp.start()

    def finish():
        for cp in copies:
            cp.wait()
        tot = land[0]
        for s in range(1, N_DEV):
            tot = tot + land[s]
        tot_ref[...] = tot

    return start, finish


def _small_reduce_scratch():
    return [pltpu.VMEM((PACK_ROWS, D), F32), pltpu.VMEM((N_DEV, PACK_ROWS, D), F32),
            pltpu.SemaphoreType.DMA((N_DEV - 1,)), pltpu.SemaphoreType.DMA((N_DEV - 1,))]


def _small_update(tot, me, params):
    flat = [a for n, _, _, _ in SMALL for a in params[n]] + list(params["conv_w"])
    n_par = len(SMALL) + 1
    tap_cols = CW // N_DEV

    def body(*refs):
        me_ref, tot_ref = refs[0:2]
        ins = refs[2:2 + 3 * n_par]
        loss_out = refs[2 + 3 * n_par]
        outs = refs[3 + 3 * n_par:]
        tot = tot_ref[...]
        loss_out[...] = jnp.broadcast_to(tot[LOSS_ROW:LOSS_ROW + 1, 0:LANES], loss_out.shape)

        def update(i, g):
            w_ref, m_ref, v_ref = ins[3 * i:3 * i + 3]
            for o_ref, res in zip(outs[4 * i:4 * i + 4], (g,) + _adamw_math(w_ref[...], g, m_ref[...], v_ref[...])):
                if len(o_ref.shape) == 3:
                    for t in range(o_ref.shape[0]):
                        o_ref[t] = res[t:t + 1, :]
                else:
                    o_ref[...] = res

        for i, (_, row, lane0, width) in enumerate(SMALL):
            update(i, tot[row:row + 1, lane0:lane0 + width])
        me = me_ref[0]
        taps = pltpu.roll(tot[CONV_ROW:CONV_ROW + SUBLANES, 0:CW], jnp.where(me == 0, 0, CW - me * tap_cols), 1)
        update(n_par - 1, taps[0:3, 0:tap_cols])

    shapes = [jax.ShapeDtypeStruct(params[n][0].shape, F32) for n, _, _, _ in SMALL] + [
        jax.ShapeDtypeStruct((3, 1, tap_cols), F32)]
    vmem = pl.BlockSpec(memory_space=pltpu.VMEM)
    loss, *out = pl.pallas_call(
        body, name="small_update",
        in_specs=[pl.BlockSpec(memory_space=pltpu.SMEM)] + [vmem] * (1 + 3 * n_par),
        out_shape=[jax.ShapeDtypeStruct((SUBLANES, LANES), F32)] + [s for s in shapes for _ in range(4)],
    )(me, tot, *flat)
    names = [n for n, _, _, _ in SMALL] + ["conv_w"]
    return loss[0, 0], {n: out[4 * i:4 * i + 4] for i, n in enumerate(names)}


def _local_step(x, mem, pos, gains, shards, tgt, place):
    half = HEAD // 2
    inv_freq = jnp.float32(ROPE_THETA) ** (-(jnp.arange(half, dtype=F32) * 2.0 / HEAD))
    invf = jnp.tile(inv_freq, LANES // half)[None, :]
    sgn = jnp.tile(jnp.concatenate([-jnp.ones((half,), F32), jnp.ones((half,), F32)]), LANES // HEAD)[None, :]
    cos, sins, win8 = _rope_table(pos.astype(F32).reshape(S, 1), invf, sgn, [shards["w_in"]])
    wdn_left, wdn_right = shards["w_down"][:, 0:D // 2], shards["w_down"][:, D // 2:]
    q, kvp, bcu, qx16, h16, win16, wout8, wkv8, conv8, wdn8_right = _in_proj(
        x, gains["g_pre_mix"], win8, cos, sins, [shards["w_out"], shards["w_mem_kv"], shards["conv_w"], wdn_right])
    wout16, wkv16 = wout8.reshape(D, D), wkv8.reshape(D, 2 * XW)
    cw_full = conv8[:, 0:3, 0:CW // N_DEV].transpose(1, 0, 2).reshape(3, CW)
    cw8 = jnp.zeros((SUBLANES, CW), F32).at[0:3].set(cw_full)
    y_attn, ltot, wup8, wdn8_left = _attn_fwd(q, kvp, [shards["w_up"], wdn_left])
    wdn_halves = (wdn8_left.reshape(FF, D // 2), wdn8_right.reshape(FF, D // 2))
    memn16, kv16 = _mem_fwd(mem, gains["g_mem"], wkv16)
    ypre, y16, y2, x1 = _mix_out(y_attn, bcu, qx16, kv16, cw8, gains["g_attn_out"], gains["g_conv_out"],
                                 gains["g_xattn_out"], gains["g_post_mix"], wout16, x, [])
    a16, du16, h2_16, df2_16, dx1, loss8, dg_mlp = _mlp(
        x1, tgt, gains["g_pre_mlp"], gains["g_post_mlp"], wup8, wdn_halves)

    sums = {"w_up": _wgrad_cols(place, h2_16, du16, FF_BLK, "wgrad_up"),
            "w_down": _wgrad_cols(place, df2_16, a16, FF_BLK, "wgrad_down", square_b=True, transpose_out=True)}

    head_id = jnp.arange(AW, dtype=jnp.int32) // HEAD
    head_ones = (head_id[:, None] == head_id[None, :]).astype(BF16)
    dy2_16, qdo, ld, dbcu, dqx, dgs, dcw, dkv = _mix_out_bwd(
        dx1, y2, ypre, ltot, head_ones, q, bcu, qx16, kv16, cw8, gains["g_post_mix"], gains["g_attn_out"],
        gains["g_conv_out"], gains["g_xattn_out"], wout16)
    dkv16, dg_mem = _mem_bwd(mem, gains["g_mem"], wkv16, dkv)
    sums["w_mem_kv"] = _wgrad_rows(place, memn16, dkv16, "wgrad_mem_kv")
    sums["w_out"] = _wgrad_rows(place, y16, dy2_16, "wgrad_out")
    out = _attn_bwd(qdo, kvp, ld, [s[0] for s in sums.values()])
    dqkv, landed = out[:9], out[9:]
    reduced = {n: (s[1], landed[t]) for t, (n, s) in enumerate(sums.items())}
    dproj16, grad_x, dg_in = _in_proj_bwd(dqkv, dbcu, dqx, cos, sins, win16, x, gains["g_pre_mix"], dx1)

    _, in_own, in_landed, small_tot = _wgrad_cols(place, h16, dproj16, PW // N_DEV, "wgrad_in", transpose_out=True,
                                                  to_chips=True, small=(dg_in, dg_mem, dgs, dg_mlp, dcw, loss8))
    reduced["w_in"] = (in_own, in_landed)
    return grad_x, reduced, small_tot


BIG = ("w_in", "w_mem_kv", "w_out", "w_up", "w_down")
ORDER = ("g_pre_mix", "g_mem", "w_in", "w_mem_kv", "conv_w", "g_attn_out", "g_conv_out", "g_xattn_out", "w_out",
         "g_post_mix", "g_pre_mlp", "w_up", "w_down", "g_post_mlp")


def kernel(x, mem, positions, g_pre_mix, g_mem, w_in, w_mem_kv, conv_w, g_attn_out, g_conv_out, g_xattn_out, w_out, g_post_mix, g_pre_mlp, w_up, w_down, g_post_mlp, loss_target, m_g_pre_mix, m_g_mem, m_w_in, m_w_mem_kv, m_conv_w, m_g_attn_out, m_g_conv_out, m_g_xattn_out, m_w_out, m_g_post_mix, m_g_pre_mlp, m_w_up, m_w_down, m_g_post_mlp, v_g_pre_mix, v_g_mem, v_w_in, v_w_mem_kv, v_conv_w, v_g_attn_out, v_g_conv_out, v_g_xattn_out, v_w_out, v_g_post_mix, v_g_pre_mlp, v_w_up, v_w_down, v_g_post_mlp):
    w = dict(g_pre_mix=g_pre_mix, g_mem=g_mem, w_in=w_in, w_mem_kv=w_mem_kv, conv_w=conv_w, g_attn_out=g_attn_out,
             g_conv_out=g_conv_out, g_xattn_out=g_xattn_out, w_out=w_out, g_post_mix=g_post_mix, g_pre_mlp=g_pre_mlp,
             w_up=w_up, w_down=w_down, g_post_mlp=g_post_mlp)
    mo = dict(g_pre_mix=m_g_pre_mix, g_mem=m_g_mem, w_in=m_w_in, w_mem_kv=m_w_mem_kv, conv_w=m_conv_w,
              g_attn_out=m_g_attn_out, g_conv_out=m_g_conv_out, g_xattn_out=m_g_xattn_out, w_out=m_w_out,
              g_post_mix=m_g_post_mix, g_pre_mlp=m_g_pre_mlp, w_up=m_w_up, w_down=m_w_down, g_post_mlp=m_g_post_mlp)
    vo = dict(g_pre_mix=v_g_pre_mix, g_mem=v_g_mem, w_in=v_w_in, w_mem_kv=v_w_mem_kv, conv_w=v_conv_w,
              g_attn_out=v_g_attn_out, g_conv_out=v_g_conv_out, g_xattn_out=v_g_xattn_out, w_out=v_w_out,
              g_post_mix=v_g_post_mix, g_pre_mlp=v_g_pre_mlp, w_up=v_w_up, w_down=v_w_down, g_post_mlp=v_g_post_mlp)

    xi, yi, ci = lax.axis_index("x"), lax.axis_index("y"), lax.axis_index("c")
    me = 4 * xi + 2 * yi + ci
    place = jnp.stack([ci, 2 * xi + yi]).astype(jnp.int32)

    shards = {n: w[n][0].astype(BF16) for n in BIG}
    shards["conv_w"] = jnp.zeros((SUBLANES, LANES), F32).at[0:3, 0:CW // N_DEV].set(conv_w[0])

    gains = {n: w[n] for n, _, _, _ in SMALL}
    grad_x, reduced, small_tot = _local_step(x[0], mem[0], positions[0], gains, shards, loss_target[0], place)

    def shard(n, a):
        return a[0].T if n == "w_in" else a[0]

    updated = {}
    for group in (("w_up", "w_down"), ("w_in", "w_out", "w_mem_kv")):
        updated.update(_adamw_shards({n: (*reduced[n], shard(n, w[n]), shard(n, mo[n]), shard(n, vo[n]))
                                      for n in group}, "adamw_" + "_".join(group))[0])
    grad, delta, new_m, new_v = {}, {}, {}, {}
    for n, res in updated.items():
        grad[n], delta[n], new_m[n], new_v[n] = [(a.T if n == "w_in" else a)[None] for a in res]

    params = {n: (w[n], mo[n], vo[n]) for n, _, _, _ in SMALL}
    params["conv_w"] = (w["conv_w"][0], mo["conv_w"][0], vo["conv_w"][0])
    loss, small = _small_update(small_tot, me.reshape(1).astype(jnp.int32), params)
    for n, (g, d_, m_, v_) in small.items():
        lead = (lambda a: a.reshape(conv_w.shape)) if n == "conv_w" else (lambda a: a)
        grad[n], delta[n], new_m[n], new_v[n] = lead(g), lead(d_), lead(m_), lead(v_)

    return (loss, grad_x[None], *[grad[n] for n in ORDER], *[delta[n] for n in ORDER],
            *[new_m[n] for n in ORDER], *[new_v[n] for n in ORDER])
```

```python
import jax
import jax.numpy as jnp
from jax import lax
from jax.experimental import pallas as pl
from jax.experimental.pallas import tpu as pltpu

F32, BF16 = jnp.float32, jnp.bfloat16
MESH = pl.DeviceIdType.MESH
ANY = pl.BlockSpec(memory_space=pl.ANY)

N_DEV = 8
D = 1024
S = 4096
N_MEM = 256
HEAD = 64
AW, CW, XW = 512, 256, 256
PW = 3 * AW + 3 * CW + XW
FF = 4096
FF_BLK = FF // N_DEV
EPS = 1e-6
NEG = -1e30
SCALE = HEAD ** -0.5
ROPE_THETA = 10000.0
LANES = 128
SUBLANES = 8

ADAM_LR, ADAM_B1, ADAM_B2, ADAM_EPS, ADAM_WD, ADAM_STEP = 0.001, 0.9, 0.999, 1e-08, 0.01, 10

TQ = 512
TQ_MLP = 512
NT = S // TQ


def _cparams(vmem_mb, n_grid=1):
    return pltpu.CompilerParams(dimension_semantics=("arbitrary",) * n_grid, vmem_limit_bytes=vmem_mb << 20)


def _const(shape):
    nd = len(shape)
    return pl.BlockSpec(shape, lambda *_: (0,) * nd, pipeline_mode=pl.Buffered(1))


def _acc(shape):
    nd = len(shape)
    return pl.BlockSpec(shape, lambda *_: (0,) * nd)


def _tokens_in_lanes(tq):
    return pl.BlockSpec((D, tq), lambda i: (0, i))


def _dot(a, b):
    return jnp.dot(a, b, preferred_element_type=F32)


def _dot_nt(a, b):
    return lax.dot_general(a, b, (((1,), (1,)), ((), ())), preferred_element_type=F32)


def _dot_tn(a, b):
    return lax.dot_general(a, b, (((0,), (0,)), ((), ())), preferred_element_type=F32)


def _rms(x, g):
    r = lax.rsqrt(jnp.mean(x * x, axis=-1, keepdims=True) + EPS)
    n = x * r
    return n * g, n, r


def _rms_bwd(dy, n, r, g):
    dn = dy * g
    dx = r * (dn - n * jnp.mean(dn * n, axis=-1, keepdims=True))
    return dx, jnp.sum(dy * n, axis=0, keepdims=True)


def _rot_half(t):
    lane = lax.broadcasted_iota(jnp.int32, t.shape, 1)
    n = t.shape[1]
    return jnp.where((lane % HEAD) < HEAD // 2, pltpu.roll(t, n - HEAD // 2, 1), pltpu.roll(t, HEAD // 2, 1))


def _rope_table(pos_col, invf, sgn, shards):
    def body(p_ref, f_ref, s_ref, c_out, s_out):
        ang = p_ref[...] * f_ref[...]
        c_out[...] = jnp.cos(ang)
        s_out[...] = jnp.sin(ang) * s_ref[...]

    tile = pl.BlockSpec((TQ, LANES), lambda i: (i, 0))
    return _call_with_gather(
        body, NT, shards, name="rope_table",
        in_specs=[pl.BlockSpec((TQ, 1), lambda i: (i, 0)), _const((1, LANES)), _const((1, LANES))],
        out_specs=[tile, tile], out_shape=[jax.ShapeDtypeStruct((S, LANES), F32)] * 2,
        scratch_shapes=[], vmem_mb=32, args=(pos_col, invf, sgn))


def _all_heads(t):
    return jnp.tile(t, (1, AW // LANES))


def _mem_fwd(mem, g_mem, wkv16):
    def body(m_ref, g_ref, w_ref, n16_ref, kv_ref):
        y, _, _ = _rms(m_ref[...], g_ref[...])
        y16 = y.astype(BF16)
        n16_ref[...] = y16.T
        kv_ref[...] = _dot(y16, w_ref[...]).astype(BF16)

    return pl.pallas_call(
        body, name="mem_fwd",
        out_shape=[jax.ShapeDtypeStruct((D, N_MEM), BF16), jax.ShapeDtypeStruct((N_MEM, 2 * XW), BF16)],
        compiler_params=pltpu.CompilerParams(vmem_limit_bytes=32 << 20))(mem, g_mem, wkv16)


def _in_proj(x, g, w8, cos, sins, shards):
    blk = PW // N_DEV

    def body(x_ref, g_ref, w8_ref, c_ref, s_ref, q_ref, kv_ref, bcu_ref, qx_ref, h_ref, w_out, w_ref):
        @pl.when(pl.program_id(0) == 0)
        def _():
            for j in range(N_DEV):
                w_ref[:, j * blk:(j + 1) * blk] = w8_ref[j]
            w_out[...] = w_ref[...]

        y, _, _ = _rms(x_ref[...], g_ref[...])
        h = y.astype(BF16)
        h_ref[...] = h.T
        proj = _dot(h, w_ref[...])
        cos, sn = _all_heads(c_ref[...]), _all_heads(s_ref[...])
        q, k = proj[:, 0:AW], proj[:, AW:2 * AW]
        q_ref[...] = (q * cos + _rot_half(q) * sn) * SCALE
        kv_ref[...] = _pack_pair(k * cos + _rot_half(k) * sn, proj[:, 2 * AW:3 * AW])
        bcu_ref[...] = proj[:, 3 * AW:3 * AW + 3 * CW]
        qx_ref[...] = (proj[:, 3 * AW + 3 * CW:] * SCALE).astype(BF16)

    def tile(w):
        return pl.BlockSpec((TQ, w), lambda i: (i, 0))

    return _call_with_gather(
        body, NT, shards, name="in_proj",
        in_specs=[tile(D), _const((1, D)), _const((N_DEV, D, blk)), tile(LANES), tile(LANES)],
        out_specs=[tile(AW), tile(AW), tile(3 * CW), tile(XW), _tokens_in_lanes(TQ), _acc((D, PW))],
        out_shape=[jax.ShapeDtypeStruct((S, AW), F32)] * 2 + [
            jax.ShapeDtypeStruct((S, 3 * CW), F32), jax.ShapeDtypeStruct((S, XW), BF16),
            jax.ShapeDtypeStruct((D, S), BF16), jax.ShapeDtypeStruct((D, PW), BF16)],
        scratch_shapes=[pltpu.VMEM((D, PW), BF16)], vmem_mb=56, args=(x, g, w8, cos, sins))


ATTN_PLANS = (("p1", 1, 128, 32), ("p4", 8, 64, 8), ("p16", 16, 128, 2))
PAD = 128
WIN = 256


ATTN_UNROLL = 16


def _fill_bias(tab, qblk, partner):
    qi = lax.broadcasted_iota(jnp.int32, (2 * qblk, WIN), 0) & (qblk - 1)
    kj = lax.broadcasted_iota(jnp.int32, (2 * qblk, WIN), 1)
    piece = kj >> (qblk.bit_length() - 1)
    kk = kj & (qblk - 1)
    prev = (piece & 1) == 0
    of_partner = piece >= 2
    for first in (0, 1):
        for par in (0, 1):
            lo = jnp.where(prev, (qblk if first else qi) + jnp.where(of_partner, par, 0), 0)
            hi = jnp.where(prev, qblk, qi + jnp.where(of_partner, par - 1, 0))
            tab[2 * first + par] = jnp.where((kk >= lo) & (kk <= hi), 0.0, NEG).astype(F32)


def _block_rows(g, qblk, nbc, partner):
    own = pl.ds(pl.multiple_of(PAD + g * qblk, qblk), qblk)
    first = ((g & (nbc - 1)) == 0).astype(jnp.int32)
    if partner:
        gp = jnp.bitwise_xor(g, 4 * nbc)
        wins = (pl.ds(pl.multiple_of(PAD + (g - 1) * qblk, qblk), 2 * qblk),
                pl.ds(pl.multiple_of(PAD + (gp - 1) * qblk, qblk), 2 * qblk))
        return own, wins, 2 * first + ((g >> ((4 * nbc).bit_length() - 1)) & 1)
    return own, (pl.ds(pl.multiple_of(PAD + (g - 1) * qblk, qblk), 2 * qblk),), 2 * first


def _pack_pair(lo, hi):
    lo_bits = lax.bitcast_convert_type(lo.astype(BF16).astype(F32), jnp.uint32) >> 16
    hi_bits = lax.bitcast_convert_type(hi.astype(BF16).astype(F32), jnp.uint32) & jnp.uint32(0xFFFF0000)
    return lax.bitcast_convert_type(hi_bits | lo_bits, F32)


def _unpack_pair(c):
    bits = lax.bitcast_convert_type(c, jnp.uint32)
    lo = lax.bitcast_convert_type(bits << 16, F32).astype(BF16)
    hi = lax.bitcast_convert_type(bits & jnp.uint32(0xFFFF0000), F32).astype(BF16)
    return lo, hi


def _window(ref, wins):
    parts = [ref[w, :] for w in wins]
    return parts[0] if len(parts) == 1 else jnp.concatenate(parts, axis=0)


def _stack_heads(t, lane):
    zero = jnp.zeros_like(t)
    return jnp.concatenate([jnp.where(lane < HEAD, t, zero), jnp.where(lane >= HEAD, t, zero)], axis=0)


def _unstack_heads(t2, lane):
    half = t2.shape[0] // 2
    return jnp.where(lane < HEAD, t2[0:half, :], t2[half:, :])


def _lanes_of(step):
    return pl.ds(pl.multiple_of(step * LANES, LANES), LANES)


def _whole_wait(buf, sem):
    whole = buf.at[pl.ds(PAD, S), :]
    return pltpu.make_async_copy(whole, whole, sem)


def _whole_waits(bufs, sems):
    return [_whole_wait(buf, sems.at[i]) for i, buf in enumerate(bufs)]


def _class_gather(views, bufs, sems, lanes):
    copies = []
    for i, (view, buf) in enumerate(zip(views, bufs)):
        if view.ndim == 2:
            copies.append(pltpu.make_async_copy(view.at[:, lanes], buf.at[pl.ds(PAD, S), :], sems.at[i]))
        else:
            per, n_cls = view.shape[0], view.shape[1]
            copies += [pltpu.make_async_copy(view.at[:, c, lanes], buf.at[pl.ds(PAD + c * per, per), :], sems.at[i])
                       for c in range(n_cls)]
    return copies


def _class_scatter(bufs, dsts, sems, lanes):
    copies = []
    for i, (buf, dst) in enumerate(zip(bufs, dsts)):
        if dst.ndim == 2:
            copies.append(pltpu.make_async_copy(buf.at[pl.ds(PAD, S), :], dst.at[:, lanes], sems.at[i]))
            continue
        per, n_cls = dst.shape[0], dst.shape[1]
        copies += [pltpu.make_async_copy(buf.at[pl.ds(PAD + c * per, per), :], dst.at[:, c, lanes], sems.at[i])
                   for c in range(n_cls)]
    return copies


def _start(copies):
    for cp in copies:
        cp.start()


def _wait(waits):
    for w in waits:
        w.wait()


def _attn_fwd(q, kvp, shards=()):
    views = [[a] + [a.reshape(S // n, n, AW) for _, n, _, _ in ATTN_PLANS[1:]] for a in (q, kvp)]
    flat = [views[a][p] for p in range(3) for a in range(2)]
    ng = len(shards)
    n_grid = AW // LANES

    def body(*refs):
        hbm = [refs[2 * p:2 * p + 2] for p in range(3)]
        refs = refs[6:]
        shard_refs, refs = refs[:ng], refs[ng:]
        y_ref, lt_ref = refs[0:2]
        whole_refs, refs = refs[2:2 + ng], refs[2 + ng:]
        bufs = [refs[2 * p:2 * p + 2] for p in range(3)]
        oc4, lc4, oc16, lc16, tab128, tab4, sem_in = refs[6:13]
        step = pl.program_id(0)
        if ng:
            start_gather, relay_gather, finish_gather = _gather_steps(shard_refs, whole_refs, *refs[13:])
            pl.when(step == 0)(start_gather)
            pl.when(step == n_grid // 2)(relay_gather)
        now = [_class_gather(hbm[p], bufs[p], sem_in.at[p], _lanes_of(step)) for p in range(3)]
        nxt = [_class_gather(hbm[p], bufs[p], sem_in.at[p], _lanes_of(step + 1)) for p in range(3)]

        @pl.when(step == 0)
        def _():
            for p in range(3):
                _start(now[p])
                for b in bufs[p]:
                    b[0:PAD, :] = jnp.zeros((PAD, LANES), F32)
            _fill_bias(tab128, 128, False)
            _fill_bias(tab4, 64, True)

        def prefetch(p):
            pl.when(step + 1 < n_grid)(lambda: _start(nxt[p]))

        lane = lax.broadcasted_iota(jnp.int32, (1, LANES), 1)
        ones = jnp.ones((WIN, LANES), BF16)

        def run(plan, bq, bkv, tab, o_dst, l_dst, dst_pad):
            _, n_cls, qblk, nbc = plan
            partner = n_cls == 8

            def block(g, carry):
                own, wins, mask = _block_rows(g, qblk, nbc, partner)
                q2 = _stack_heads(bq[own, :].astype(BF16), lane)
                kw, vwin = _unpack_pair(_window(bkv, wins))
                vw = jnp.concatenate([vwin, ones], axis=1)
                s = _dot_nt(q2, kw) + tab[mask]
                m = jnp.max(s, axis=1, keepdims=True)
                oe = _dot(jnp.exp(s - m).astype(BF16), vw)
                den = oe[:, LANES:]
                dst = pl.ds(pl.multiple_of(dst_pad + g * qblk, qblk), qblk)
                o_dst[dst, :] = _unstack_heads(oe[:, 0:LANES] / den, lane)
                l_dst[dst, :] = _unstack_heads(m + jnp.log(den), lane)
                return carry
            lax.fori_loop(0, n_cls * nbc, block, 0, unroll=ATTN_UNROLL)

        _wait(_whole_waits(bufs[0], sem_in.at[0]))
        run(ATTN_PLANS[0], *bufs[0], tab128, y_ref, lt_ref, 0)
        prefetch(0)
        _wait(_whole_waits(bufs[1], sem_in.at[1]))
        run(ATTN_PLANS[1], *bufs[1], tab4, oc4, lc4, PAD)
        prefetch(1)
        _wait(_whole_waits(bufs[2], sem_in.at[2]))
        run(ATTN_PLANS[2], *bufs[2], tab128, oc16, lc16, PAD)
        prefetch(2)

        n_rows = 64

        def token_order(buf, t, n_cls):
            per = S // n_cls
            first = PAD + t * (n_rows // n_cls)
            return jnp.concatenate([buf[pl.ds(first + jj, n_cls, stride=per), :] for jj in range(n_rows // n_cls)],
                                   axis=0)

        def combine(t, carry):
            rows = pl.ds(pl.multiple_of(t * n_rows, n_rows), n_rows)
            l0, l1, l2 = lt_ref[rows, :], token_order(lc4, t, 8), token_order(lc16, t, 16)
            lm = jnp.maximum(jnp.maximum(l0, l1), l2)
            e0, e1, e2 = jnp.exp(l0 - lm), jnp.exp(l1 - lm), jnp.exp(l2 - lm)
            den = e0 + e1 + e2
            y_ref[rows, :] = (e0 * y_ref[rows, :] + e1 * token_order(oc4, t, 8)
                              + e2 * token_order(oc16, t, 16)) / den
            lt_ref[rows, :] = lm + jnp.log(den)
            return carry
        lax.fori_loop(0, S // n_rows, combine, 0, unroll=2)

        if ng:
            pl.when(step == n_grid - 1)(finish_gather)

    col = pl.BlockSpec((S, LANES), lambda h: (0, h))
    padded = pltpu.VMEM((PAD + S, LANES), F32)
    return pl.pallas_call(
        body, grid=(n_grid,), name="attn_fwd",
        in_specs=[ANY] * (6 + ng), out_specs=[col, col] + [ANY] * ng,
        out_shape=[jax.ShapeDtypeStruct((S, AW), F32)] * 2 + _gathered_shapes(shards),
        scratch_shapes=[padded] * 10 + [
            pltpu.VMEM((4, 256, WIN), F32), pltpu.VMEM((4, 128, WIN), F32), pltpu.SemaphoreType.DMA((3, 2))]
        + (_gather_scratch(ng) if ng else []),
        compiler_params=_cparams(56))(*flat, *shards)


def _conv_taps(z, zprev, row):
    z1 = jnp.where(row == 0, zprev[7:8, :], pltpu.roll(z, 1, 0))
    z2 = jnp.where(row == 0, zprev[6:7, :], jnp.where(row == 1, zprev[7:8, :], pltpu.roll(z, 2, 0)))
    return z1, z2


def _xattn_scores(qm, km):
    s = _dot_nt(qm, km)
    m = jnp.max(s, axis=1, keepdims=True)
    e = jnp.exp(s - m)
    return e, jnp.sum(e, axis=1, keepdims=True)


def _mix_out(y_attn, bcu, qx16, kv16, cw8, g_attn, g_conv, g_x, g_post, wout16, x, shards):
    def body(ya_ref, bcu_ref, halo_ref, qx_ref, kv_ref, cw_ref, ga_ref, gc_ref, gx_ref, gp_ref, w_ref, x_ref,
             ypre_ref, y16_ref, y2_ref, x1_ref):
        i = pl.program_id(0)
        bcu = bcu_ref[...]
        b, c, u = bcu[:, 0:CW], bcu[:, CW:2 * CW], bcu[:, 2 * CW:]
        z = c * u
        halo = halo_ref[...]
        zprev = jnp.where(i > 0, halo[:, CW:2 * CW] * halo[:, 2 * CW:], 0.0)
        row = lax.broadcasted_iota(jnp.int32, z.shape, 0)
        z1, z2 = _conv_taps(z, zprev, row)
        cw = cw_ref[...]
        y_conv = b * (z2 * cw[0:1, :] + z1 * cw[1:2, :] + z * cw[2:3, :])

        qx = qx_ref[...]
        kv = kv_ref[...]
        km, vm = kv[:, 0:XW], kv[:, XW:]
        lane = lax.broadcasted_iota(jnp.int32, qx.shape, 1)
        y_x = jnp.zeros(qx.shape, F32)
        for h in range(XW // HEAD):
            hm = (lane >= h * HEAD) & (lane < (h + 1) * HEAD)
            e, l = _xattn_scores(jnp.where(hm, qx, jnp.zeros_like(qx)), km)
            y_x = jnp.where(hm, _dot(e.astype(BF16), vm) / l, y_x)

        y_attn = ya_ref[...]
        ypre_ref[:, 0:AW] = y_attn
        ypre_ref[:, AW:AW + CW] = y_conv
        ypre_ref[:, AW + CW:] = y_x
        y = jnp.concatenate([_rms(y_attn, ga_ref[...])[0], _rms(y_conv, gc_ref[...])[0],
                             _rms(y_x, gx_ref[...])[0]], axis=1).astype(BF16)
        y16_ref[...] = y.T
        y2 = _dot(y, w_ref[...])
        y2_ref[...] = y2
        x1_ref[...] = x_ref[...] + _rms(y2, gp_ref[...])[0]

    def tile(w):
        return pl.BlockSpec((TQ, w), lambda i: (i, 0))

    halo = pl.BlockSpec((SUBLANES, 3 * CW), lambda i: (jnp.maximum(i * (TQ // SUBLANES) - 1, 0), 0))
    return _call_with_gather(
        body, NT, shards, name="mix_out",
        in_specs=[tile(AW), tile(3 * CW), halo, tile(XW), _const((N_MEM, 2 * XW)), _const((SUBLANES, CW)),
                  _const((1, AW)), _const((1, CW)), _const((1, XW)), _const((1, D)), _const((D, D)), tile(D)],
        out_specs=[tile(D), _tokens_in_lanes(TQ), tile(D), tile(D)],
        out_shape=[jax.ShapeDtypeStruct((S, D), F32), jax.ShapeDtypeStruct((D, S), BF16),
                   jax.ShapeDtypeStruct((S, D), F32), jax.ShapeDtypeStruct((S, D), F32)],
        scratch_shapes=[], vmem_mb=56,
        args=(y_attn, bcu, bcu, qx16, kv16, cw8, g_attn, g_conv, g_x, g_post, wout16, x))


def _mlp(x1, tgt, g_pre, g_post, wup8, wdn_halves):
    tq = TQ_MLP
    half = D // 2

    def body(x1_ref, t_ref, g1_ref, g2_ref, wu_ref, wda_ref, wdb_ref,
             a16_ref, du_ref, h2_ref, df2_ref, dx1_ref, loss_ref, dg_ref):
        @pl.when(pl.program_id(0) == 0)
        def _():
            loss_ref[...] = jnp.zeros_like(loss_ref)
            dg_ref[...] = jnp.zeros_like(dg_ref)

        x1 = x1_ref[...]
        g1, g2 = g1_ref[...], g2_ref[...]
        y1, n1, r1 = _rms(x1, g1)
        h2 = y1.astype(BF16)
        h2_ref[...] = h2.T
        f2a = jnp.zeros((tq, half), F32)
        f2b = jnp.zeros((tq, half), F32)
        for j in range(N_DEV):
            cols = slice(j * FF_BLK, (j + 1) * FF_BLK)
            a = jnp.maximum(_dot(h2, wu_ref[j]), 0.0)
            a16_ref[:, cols] = a.astype(BF16)
            f = (a * a).astype(BF16)
            f2a = f2a + _dot(f, wda_ref[cols, :])
            f2b = f2b + _dot(f, wdb_ref[cols, :])
        f2 = jnp.concatenate([f2a, f2b], axis=1)
        y2, n2, r2 = _rms(f2, g2)
        e = x1 + y2 - t_ref[...]
        sq = jnp.sum(jnp.sum(e * e, axis=1, keepdims=True), axis=0, keepdims=True)
        loss_ref[...] += jnp.broadcast_to(sq * (0.5 / D), loss_ref.shape)
        dout = e * (1.0 / D)
        df2, dg2 = _rms_bwd(dout, n2, r2, g2)
        df2_16 = df2.astype(BF16)
        df2_ref[...] = df2_16.T
        dh2 = jnp.zeros((tq, D), F32)
        for j in range(N_DEV):
            cols = slice(j * FF_BLK, (j + 1) * FF_BLK)
            df = _dot_nt(df2_16[:, 0:half], wda_ref[cols, :]) + _dot_nt(df2_16[:, half:], wdb_ref[cols, :])
            du = (df * (2.0 * a16_ref[:, cols].astype(F32))).astype(BF16)
            du_ref[:, cols] = du
            dh2 = dh2 + _dot_nt(du, wu_ref[j])
        dx, dg1 = _rms_bwd(dh2, n1, r1, g1)
        dx1_ref[...] = dout + dx
        dg_ref[0:1, :] += dg2
        dg_ref[1:2, :] += dg1

    def tile(w):
        return pl.BlockSpec((tq, w), lambda i: (i, 0))

    return pl.pallas_call(
        body, grid=(S // tq,), name="mlp",
        in_specs=[tile(D), tile(D), _const((1, D)), _const((1, D)), _const((N_DEV, D, FF_BLK)), _const((FF, half)), _const((FF, half))],
        out_specs=[tile(FF), tile(FF), _tokens_in_lanes(tq), _tokens_in_lanes(tq), tile(D),
                   _acc((SUBLANES, LANES)), _acc((SUBLANES, D))],
        out_shape=[jax.ShapeDtypeStruct((S, FF), BF16), jax.ShapeDtypeStruct((S, FF), BF16),
                   jax.ShapeDtypeStruct((D, S), BF16), jax.ShapeDtypeStruct((D, S), BF16),
                   jax.ShapeDtypeStruct((S, D), F32), jax.ShapeDtypeStruct((SUBLANES, LANES), F32),
                   jax.ShapeDtypeStruct((SUBLANES, D), F32)],
        compiler_params=_cparams(60))(x1, tgt, g_pre, g_post, wup8, *wdn_halves)


def _mix_out_bwd(dx1, y2, ypre, ltot, head_ones, q, bcu, qx16, kv16, cw8, g_post, g_attn, g_conv, g_x, wout16):
    def body(dx1_ref, y2_ref, ypre_ref, lt_ref, e_ref, q_ref, bcu_ref, halo_ref, qx_ref, kv_ref, cw_ref, gp_ref,
             ga_ref, gc_ref, gx_ref, w_ref, dy2_ref, qdo_ref, ld_ref, dbcu_ref, dqx_ref, dgs_ref, dcw_ref, dkv_ref,
             carry):
        i = pl.program_id(0)

        @pl.when(i == 0)
        def _():
            dgs_ref[...] = jnp.zeros_like(dgs_ref)
            dcw_ref[...] = jnp.zeros_like(dcw_ref)
            dkv_ref[...] = jnp.zeros_like(dkv_ref)
            carry[...] = jnp.zeros_like(carry)

        gp = gp_ref[...]
        _, n, r = _rms(y2_ref[...], gp)
        dy2, dgp = _rms_bwd(dx1_ref[...], n, r, gp)
        dy2_16 = dy2.astype(BF16)
        dy2_ref[...] = dy2_16
        dy = _dot_nt(dy2_16, w_ref[...])

        ypre = ypre_ref[...]
        ga, gc, gx = ga_ref[...], gc_ref[...], gx_ref[...]
        _, na, ra = _rms(ypre[:, 0:AW], ga)
        dya, dga = _rms_bwd(dy[:, 0:AW], na, ra, ga)
        _, nc, rc = _rms(ypre[:, AW:AW + CW], gc)
        dyc, dgc = _rms_bwd(dy[:, AW:AW + CW], nc, rc, gc)
        y_x = ypre[:, AW + CW:]
        _, nx, rx = _rms(y_x, gx)
        dyx, dgx = _rms_bwd(dy[:, AW + CW:], nx, rx, gx)
        qdo_ref[...] = _pack_pair(q_ref[...], dya)
        prod = dya * ypre[:, 0:AW]
        hi = prod.astype(BF16)
        lo = (prod - hi.astype(F32)).astype(BF16)
        head_sum = _dot(hi, e_ref[...]) + _dot(lo, e_ref[...])
        lane_a = lax.broadcasted_iota(jnp.int32, prod.shape, 1)
        ld_ref[...] = jnp.where((lane_a % HEAD) < HEAD // 2, lt_ref[...], head_sum)
        dgs_ref[0:1, :] += dgp
        dgs_ref[1:2, :] += jnp.concatenate([dga, dgc, dgx], axis=1)

        bcu = bcu_ref[...]
        b, c, u = bcu[:, 0:CW], bcu[:, CW:2 * CW], bcu[:, 2 * CW:]
        z = c * u
        halo = halo_ref[...]
        zprev = jnp.where(i < NT - 1, halo[:, CW:2 * CW] * halo[:, 2 * CW:], 0.0)
        row = lax.broadcasted_iota(jnp.int32, z.shape, 0)
        z1, z2 = _conv_taps(z, zprev, row)
        cw = cw_ref[...]
        conv = z2 * cw[0:1, :] + z1 * cw[1:2, :] + z * cw[2:3, :]
        dconv = dyc * b
        nxt = carry[...]
        dn1 = jnp.where(row == TQ - 1, nxt[0:1, :], pltpu.roll(dconv, TQ - 1, 0))
        dn2 = jnp.where(row == TQ - 1, nxt[1:2, :], jnp.where(row == TQ - 2, nxt[0:1, :], pltpu.roll(dconv, TQ - 2, 0)))
        carry[...] = dconv[0:SUBLANES, :]
        dz = dconv * cw[2:3, :] + dn1 * cw[1:2, :] + dn2 * cw[0:1, :]
        dbcu_ref[:, 0:CW] = (dyc * conv).astype(BF16)
        dbcu_ref[:, CW:2 * CW] = (dz * u).astype(BF16)
        dbcu_ref[:, 2 * CW:] = (dz * c).astype(BF16)
        dcw_ref[0:1, :] += jnp.sum(z2 * dconv, axis=0, keepdims=True)
        dcw_ref[1:2, :] += jnp.sum(z1 * dconv, axis=0, keepdims=True)
        dcw_ref[2:3, :] += jnp.sum(z * dconv, axis=0, keepdims=True)

        qx = qx_ref[...]
        kv = kv_ref[...]
        km, vm = kv[:, 0:XW], kv[:, XW:]
        lane = lax.broadcasted_iota(jnp.int32, qx.shape, 1)
        dqx = jnp.zeros(qx.shape, F32)
        dkm = jnp.zeros((N_MEM, XW), F32)
        dvm = jnp.zeros((N_MEM, XW), F32)
        for h in range(XW // HEAD):
            hm = (lane >= h * HEAD) & (lane < (h + 1) * HEAD)
            qm = jnp.where(hm, qx, jnp.zeros_like(qx))
            e, l = _xattn_scores(qm, km)
            p = e / l
            dom = jnp.where(hm, dyx, 0.0)
            do16 = dom.astype(BF16)
            dsum = jnp.sum(dom * y_x, axis=1, keepdims=True)
            ds = (p * (_dot_nt(do16, vm) - dsum)).astype(BF16)
            dqx = jnp.where(hm, _dot(ds, km), dqx)
            dkm = dkm + _dot_tn(ds, qm)
            dvm = dvm + _dot_tn(p.astype(BF16), do16)
        dqx_ref[...] = (dqx * SCALE).astype(BF16)
        dkv_ref[:, 0:XW] += dkm
        dkv_ref[:, XW:] += dvm

    def tile(w):
        return pl.BlockSpec((TQ, w), lambda i: (NT - 1 - i, 0))

    halo = pl.BlockSpec((SUBLANES, 3 * CW), lambda i: (jnp.maximum((NT - 1 - i) * (TQ // SUBLANES) - 1, 0), 0))
    return pl.pallas_call(
        body, grid=(NT,), name="mix_out_bwd",
        in_specs=[tile(D), tile(D), tile(D), tile(AW), _const((AW, AW)), tile(AW), tile(3 * CW), halo, tile(XW),
                  _const((N_MEM, 2 * XW)), _const((SUBLANES, CW)), _const((1, D)), _const((1, AW)), _const((1, CW)),
                  _const((1, XW)), _const((D, D))],
        out_specs=[tile(D), tile(AW), tile(AW), tile(3 * CW), tile(XW), _acc((SUBLANES, D)), _acc((SUBLANES, CW)),
                   _acc((N_MEM, 2 * XW))],
        out_shape=[jax.ShapeDtypeStruct((S, D), BF16), jax.ShapeDtypeStruct((S, AW), F32),
                   jax.ShapeDtypeStruct((S, AW), F32),
                   jax.ShapeDtypeStruct((S, 3 * CW), BF16), jax.ShapeDtypeStruct((S, XW), BF16),
                   jax.ShapeDtypeStruct((SUBLANES, D), F32), jax.ShapeDtypeStruct((SUBLANES, CW), F32),
                   jax.ShapeDtypeStruct((N_MEM, 2 * XW), F32)],
        scratch_shapes=[pltpu.VMEM((SUBLANES, CW), F32)],
        compiler_params=_cparams(56))(dx1, y2, ypre, ltot, head_ones, q, bcu, bcu, qx16, kv16, cw8, g_post, g_attn,
                                      g_conv, g_x, wout16)


def _attn_bwd(qdo, kvp, ld, chip_sums=()):
    n_in = 3
    views = [[a] + [a.reshape(S // n, n, AW) for _, n, _, _ in ATTN_PLANS[1:]] for a in (qdo, kvp, ld)]
    flat = [views[a][p] for p in range(3) for a in range(n_in)]
    ns = len(chip_sums)
    n_grid = AW // LANES

    def body(*refs):
        hbm = [refs[n_in * p:n_in * p + n_in] for p in range(3)]
        refs = refs[3 * n_in:]
        sum_refs, refs = refs[:ns], refs[ns:]
        outs = [refs[3 * p:3 * p + 3] for p in range(3)]
        landed_refs, sc = refs[9:9 + ns], refs[9 + ns:]
        bufs = [sc[3 * p:3 * p + 3] for p in range(3)]
        res = [sc[9 + 3 * p:12 + 3 * p] for p in range(3)]
        tab128, tab4, sem_in, sem_out = sc[18:22]
        step = pl.program_id(0)
        if ns:
            start_chips, finish_chips = _chips_steps(sum_refs, landed_refs, *sc[22:])
            pl.when(step == 0)(start_chips)
        now = [_class_gather(hbm[p], bufs[p], sem_in.at[p], _lanes_of(step)) for p in range(3)]
        nxt = [_class_gather(hbm[p], bufs[p], sem_in.at[p], _lanes_of(step + 1)) for p in range(3)]

        @pl.when(step == 0)
        def _():
            for p in range(3):
                _start(now[p])
                for b in bufs[p]:
                    b[0:PAD, :] = jnp.zeros((PAD, LANES), F32)
            _fill_bias(tab128, 128, False)
            _fill_bias(tab4, 64, True)

        def prefetch(p):
            pl.when(step + 1 < n_grid)(lambda: _start(nxt[p]))

        lane = lax.broadcasted_iota(jnp.int32, (1, LANES), 1)

        def run(plan, plan_bufs, tab, dst):
            _, n_cls, qblk, nbc = plan
            partner = n_cls == 8
            bqdo, bkv, bld = plan_bufs
            rq, rk, rv = dst

            def block(g, carry):
                own, wins, mask = _block_rows(g, qblk, nbc, partner)
                qb, dob = _unpack_pair(bqdo[own, :])
                q2, do2 = _stack_heads(qb, lane), _stack_heads(dob, lane)
                kw, vw = _unpack_pair(_window(bkv, wins))
                ldv = bld[own, :]
                half = HEAD // 2
                lt2 = jnp.concatenate([ldv[:, 0:1], ldv[:, HEAD:HEAD + 1]], axis=0)
                dsum2 = jnp.concatenate([ldv[:, half:half + 1], ldv[:, HEAD + half:HEAD + half + 1]], axis=0)
                p = jnp.exp(_dot_nt(q2, kw) + tab[mask] - lt2)
                ds = (p * (_dot_nt(do2, vw) - dsum2)).astype(BF16)
                rq[own, :] = _unstack_heads(_dot(ds, kw), lane)
                dkw = _dot_tn(ds, q2)
                dvw = _dot_tn(p.astype(BF16), do2)
                n_w = WIN // len(wins)
                for i, w in enumerate(wins):
                    rk[w, :] += dkw[i * n_w:(i + 1) * n_w, :]
                    rv[w, :] += dvw[i * n_w:(i + 1) * n_w, :]
                return carry
            lax.fori_loop(0, n_cls * nbc, block, 0, unroll=ATTN_UNROLL)

        tabs = (tab128, tab4, tab128)
        def drained(p):
            return lambda: _wait(_whole_waits(res[p], sem_out.at[p]))

        for p in range(3):
            pl.when(step > 0)(drained(p))
            for b in res[p][1:]:
                b[...] = jnp.zeros_like(b)
            _wait(_whole_waits(bufs[p], sem_in.at[p]))
            run(ATTN_PLANS[p], bufs[p], tabs[p], res[p])
            prefetch(p)
            _start(_class_scatter(res[p], outs[p], sem_out.at[p], _lanes_of(step)))
        for p in range(3):
            pl.when(step == n_grid - 1)(drained(p))
        if ns:
            pl.when(step == n_grid - 1)(finish_chips)

    padded = pltpu.VMEM((PAD + S, LANES), F32)
    shapes = [jax.ShapeDtypeStruct(views[0][p].shape, F32) for p in range(3) for _ in range(3)]
    out = pl.pallas_call(
        body, grid=(n_grid,), name="attn_bwd",
        in_specs=[ANY] * (3 * n_in + ns), out_specs=[ANY] * (9 + ns),
        out_shape=shapes + _chips_shapes(chip_sums),
        scratch_shapes=[padded] * 18
        + [pltpu.VMEM((4, 256, WIN), F32), pltpu.VMEM((4, 128, WIN), F32),
           pltpu.SemaphoreType.DMA((3, n_in)), pltpu.SemaphoreType.DMA((3, 3))]
        + (_chips_scratch(ns) if ns else []),
        compiler_params=_cparams(56))(*flat, *chip_sums)
    return [o.reshape(S, AW) for o in out[:9]] + list(out[9:])


def _in_proj_bwd(dqkv, dbcu, dqx, cos, sins, w16, x, g, dx1):
    tq = TQ // 2

    def body(*refs):
        parts = refs[0:9]
        dbcu_ref, dqx_ref, c_ref, s_ref, w_ref, x_ref, g_ref, dx1_ref, dp_ref, gx_ref, dg_ref = refs[9:]

        @pl.when(pl.program_id(0) == 0)
        def _():
            dg_ref[...] = jnp.zeros_like(dg_ref)

        dq, dk, dv = (parts[i][...] + parts[3 + i][...] + parts[6 + i][...] for i in range(3))
        cos, sn = _all_heads(c_ref[...]), _all_heads(s_ref[...])
        dqr = dq * SCALE
        dkr = dk
        dp = jnp.concatenate([(dqr * cos + _rot_half(dqr * sn)).astype(BF16),
                              (dkr * cos + _rot_half(dkr * sn)).astype(BF16), dv.astype(BF16),
                              dbcu_ref[...], dqx_ref[...]], axis=1)
        dp_ref[...] = dp
        dh = _dot_nt(dp, w_ref[...])
        g = g_ref[...]
        _, n, r = _rms(x_ref[...], g)
        dx, dg = _rms_bwd(dh, n, r, g)
        gx_ref[...] = dx1_ref[...] + dx
        dg_ref[0:1, :] += dg

    def tile(w):
        return pl.BlockSpec((tq, w), lambda i: (i, 0))

    return pl.pallas_call(
        body, grid=(S // tq,), name="in_proj_bwd",
        in_specs=[tile(AW)] * 9 + [tile(3 * CW), tile(XW), tile(LANES), tile(LANES), _const((D, PW)),
                                   tile(D), _const((1, D)), tile(D)],
        out_specs=[tile(PW), tile(D), _acc((SUBLANES, D))],
        out_shape=[jax.ShapeDtypeStruct((S, PW), BF16), jax.ShapeDtypeStruct((S, D), F32),
                   jax.ShapeDtypeStruct((SUBLANES, D), F32)],
        compiler_params=_cparams(56))(*dqkv, dbcu, dqx, cos, sins, w16, x, g, dx1)


def _mem_bwd(mem, g_mem, wkv16, dkv):
    def body(m_ref, g_ref, w_ref, dkv_ref, dkv16_ref, dg_ref):
        dkv16 = dkv_ref[...].astype(BF16)
        dkv16_ref[...] = dkv16
        _, n, _ = _rms(m_ref[...], g_ref[...])
        dg = jnp.sum(_dot_nt(dkv16, w_ref[...]) * n, axis=0, keepdims=True)
        dg_ref[...] = jnp.broadcast_to(dg, dg_ref.shape)

    return pl.pallas_call(
        body, name="mem_bwd",
        out_shape=[jax.ShapeDtypeStruct((N_MEM, 2 * XW), BF16), jax.ShapeDtypeStruct((SUBLANES, D), F32)],
        compiler_params=pltpu.CompilerParams(vmem_limit_bytes=32 << 20))(mem, g_mem, wkv16, dkv)


N_CHIPS = N_DEV // 2


def _pair_scratch(block):
    return [pltpu.VMEM((N_CHIPS,) + block, BF16), pltpu.VMEM((N_CHIPS,) + block, BF16),
            pltpu.SemaphoreType.DMA((N_CHIPS,)), pltpu.SemaphoreType.DMA((N_CHIPS,))]


def _swap_with_sibling(p, stage, land, send, recv):
    x, y, c = lax.axis_index("x"), lax.axis_index("y"), lax.axis_index("c")
    return pltpu.make_async_remote_copy(src_ref=stage.at[p], dst_ref=land.at[p], send_sem=send.at[p],
                                        recv_sem=recv.at[p], device_id=(x, y, 1 - c), device_id_type=MESH)


def _wgrad_cols(place, at16, b16, blk, name, square_b=False, transpose_out=False, to_chips=False, small=()):
    m, kk = at16.shape
    aligned = blk % LANES == 0
    wide = blk if aligned else -(-(blk + LANES // 2) // LANES) * LANES
    assert aligned or (transpose_out and blk % SUBLANES == 0)
    block = (blk, m) if transpose_out else (m, blk)

    def chip_of(step, my_chip):
        return jnp.bitwise_xor(my_chip, N_CHIPS - 1 - step) if to_chips else step

    def body(pl_ref, a_ref, *refs):
        b_refs, refs = refs[:2 if aligned else 1], refs[2 if aligned else 1:]
        accs, refs = refs[:len(small)], refs[len(small):]
        (cs_ref, own_ref), refs = refs[:2], refs[2:]
        if to_chips:
            landed, refs = refs[0], refs[1:]
        if small:
            tot_ref, refs = refs[0], refs[1:]
        (stage, land, send, recv), refs = refs[:4], refs[4:]
        if not aligned:
            (win, wsem), refs = refs[:2], refs[2:]
        if small:
            start_small, finish_small = _small_reduce_steps(accs, tot_ref, *refs[-4:])
            refs = refs[:-4]
        step = pl.program_id(0)
        if small:
            pl.when(step == 0)(start_small)
        x, y, c = lax.axis_index("x"), lax.axis_index("y"), lax.axis_index("c")
        my_chip = 2 * x + y
        p = chip_of(step, my_chip)

        def fetch(at_step, mine):
            j = 2 * chip_of(at_step, my_chip) + (c if mine else 1 - c)
            first = pl.multiple_of(((j * blk) >> 7) << 7, LANES)
            slot = 2 * (at_step & 1) + mine
            return pltpu.make_async_copy(b_refs[0].at[:, pl.ds(first, wide)], win.at[slot], wsem.at[slot])

        if not aligned:
            @pl.when(step == 0)
            def _():
                fetch(0, 0).start()
                fetch(0, 1).start()

            @pl.when(step + 1 < N_CHIPS)
            def _():
                fetch(step + 1, 0).start()
                fetch(step + 1, 1).start()

        def partial(mine):
            if aligned:
                b = b_refs[mine][...]
                if square_b:
                    b = b * b
                acc = _dot(a_ref[...], b)
            else:
                fetch(step, mine).wait()
                acc = _dot(a_ref[...], win[2 * (step & 1) + mine]).T
                odd = c if mine else 1 - c
                return jnp.where(odd == 0, acc[0:blk], acc[wide - blk:wide])
            return acc.T if transpose_out else acc

        stage[p] = partial(0).astype(BF16)
        swap = _swap_with_sibling(p, stage, land, send, recv)
        swap.start()
        mine = partial(1)
        swap.wait()
        total = mine + land[p].astype(F32)
        cs_ref[0] = total.astype(BF16)

        @pl.when(p == my_chip)
        def _():
            own_ref[...] = total

        if to_chips:
            stage2, send2, recv2 = refs
            flipped = jnp.bitwise_xor(p, my_chip)
            k = jnp.where(flipped == 2, 0, jnp.where(flipped == 1, 1, 2))

            def to_owner(src, k_, px, py):
                return pltpu.make_async_remote_copy(src_ref=src, dst_ref=landed.at[k_], send_sem=send2.at[k_],
                                                    recv_sem=recv2.at[k_], device_id=(px, py, c), device_id_type=MESH)

            @pl.when(p != my_chip)
            def _():
                stage2[p] = total.astype(BF16)
                to_owner(stage2.at[p], k, p >> 1, p & 1).start()

            @pl.when(step == N_CHIPS - 1)
            def _():
                for k_ in range(N_CHIPS - 1):
                    to_owner(stage2.at[0], k_, x, y).wait()

        if small:
            pl.when(step == N_CHIPS - 1)(finish_small)

    def b_spec(mine):
        return pl.BlockSpec((kk, blk), lambda i, s: (0, 2 * chip_of(i, s[1]) + (s[0] if mine else 1 - s[0])))

    b_specs, b_args = ([b_spec(0), b_spec(1)], (b16, b16)) if aligned else ([ANY], (b16,))
    scratch = _pair_scratch(block)
    if not aligned:
        scratch += [pltpu.VMEM((4, kk, wide), BF16), pltpu.SemaphoreType.DMA((4,))]
    out_specs = [pl.BlockSpec((1,) + block, lambda i, s: (chip_of(i, s[1]), 0, 0)), pl.BlockSpec(block, lambda i, s: (0, 0))]
    out_shape = [jax.ShapeDtypeStruct((N_CHIPS,) + block, BF16), jax.ShapeDtypeStruct(block, F32)]
    if to_chips:
        out_specs.append(ANY)
        out_shape.append(jax.ShapeDtypeStruct((N_CHIPS - 1,) + block, BF16))
        scratch += [pltpu.VMEM((N_CHIPS,) + block, BF16), pltpu.SemaphoreType.DMA((N_CHIPS - 1,)),
                    pltpu.SemaphoreType.DMA((N_CHIPS - 1,))]
    small_specs = [pl.BlockSpec(a.shape, lambda i, s: (0, 0)) for a in small]
    if small:
        out_specs.append(pl.BlockSpec((PACK_ROWS, D), lambda i, s: (0, 0)))
        out_shape.append(jax.ShapeDtypeStruct((PACK_ROWS, D), F32))
        scratch += _small_reduce_scratch()
    return pl.pallas_call(
        body, name=name,
        grid_spec=pltpu.PrefetchScalarGridSpec(
            num_scalar_prefetch=1, grid=(N_CHIPS,),
            in_specs=[pl.BlockSpec((m, kk), lambda i, s: (0, 0), pipeline_mode=pl.Buffered(1))] + b_specs + small_specs,
            out_specs=out_specs, scratch_shapes=scratch),
        out_shape=out_shape, compiler_params=_cparams(56))(place, at16, *b_args, *small)


ROWS_STEPS = 4


def _wgrad_rows(place, products, name):
    n_prod = len(products)
    dims = [(at16.shape[0], at16.shape[1], b16.shape[1]) for at16, b16 in products]
    cut = [kk % (ROWS_STEPS * LANES) == 0 for _, kk, _ in dims]
    blocks = [(m // N_DEV, n) for m, _, n in dims]

    def body(pl_ref, *refs):
        ins, outs, scratch = refs[:2 * n_prod], refs[2 * n_prod:4 * n_prod], refs[4 * n_prod:]
        c, step = pl_ref[0], pl.program_id(0)

        def multiply(i):
            a_ref, b_ref, acc = ins[2 * i], ins[2 * i + 1], scratch[5 * i]

            @pl.when(step == 0)
            def _():
                acc[...] = _dot(a_ref[...], b_ref[...])

            if cut[i]:
                @pl.when(step > 0)
                def _():
                    acc[...] += _dot(a_ref[...], b_ref[...])

        def rows(i, owner):
            return pl.ds(pl.multiple_of(owner * blocks[i][0], blocks[i][0]), blocks[i][0])

        def send_sibling_side(i):
            acc, stage, land, send, recv = scratch[5 * i:5 * i + 5]
            swaps = []
            for p in range(N_CHIPS):
                stage[p] = acc[rows(i, 2 * p + 1 - c), :].astype(BF16)
                swaps.append(_swap_with_sibling(p, stage, land, send, recv))
                swaps[-1].start()
            return swaps

        def add_my_side(i, swaps):
            acc, land = scratch[5 * i], scratch[5 * i + 2]
            cs_ref, own_ref = outs[2 * i:2 * i + 2]
            for p in range(N_CHIPS):
                swaps[p].wait()
                total = acc[rows(i, 2 * p + c), :] + land[p].astype(F32)
                cs_ref[p] = total.astype(BF16)

                @pl.when(p == pl_ref[1])
                def _():
                    own_ref[...] = total

        for i in range(n_prod):
            multiply(i)

        @pl.when(step == ROWS_STEPS - 1)
        def _():
            swaps = [send_sibling_side(i) for i in range(n_prod)]
            for i in range(n_prod):
                add_my_side(i, swaps[i])

    in_specs, out_specs, out_shape, scratch = [pl.BlockSpec(memory_space=pltpu.SMEM)], [], [], []
    for (m, kk, n), cut_i, block in zip(dims, cut, blocks):
        chunk = kk // ROWS_STEPS
        in_specs += ([pl.BlockSpec((m, chunk), lambda i: (0, i)), pl.BlockSpec((chunk, n), lambda i: (i, 0))]
                     if cut_i else [_const((m, kk)), _const((kk, n))])
        out_specs += [_acc((N_CHIPS,) + block), _acc(block)]
        out_shape += [jax.ShapeDtypeStruct((N_CHIPS,) + block, BF16), jax.ShapeDtypeStruct(block, F32)]
        scratch += [pltpu.VMEM((m, n), F32)] + _pair_scratch(block)
    out = pl.pallas_call(
        body, grid=(ROWS_STEPS,), name=name, in_specs=in_specs, out_specs=out_specs, out_shape=out_shape,
        scratch_shapes=scratch, compiler_params=_cparams(56))(place, *[a for pair in products for a in pair])
    return [tuple(out[2 * i:2 * i + 2]) for i in range(n_prod)]


def _adamw_math(w, g, m, v):
    m = ADAM_B1 * m + (1.0 - ADAM_B1) * g
    v = ADAM_B2 * v + (1.0 - ADAM_B2) * jnp.square(g)
    m_hat = m / (1.0 - ADAM_B1 ** ADAM_STEP)
    v_hat = v / (1.0 - ADAM_B2 ** ADAM_STEP)
    delta = -ADAM_LR * (m_hat / (jnp.sqrt(v_hat) + ADAM_EPS) + ADAM_WD * w)
    return delta, m, v


def _adamw_shards(updates, name, chip_sums=()):
    names, nu, ns = list(updates), len(updates), len(chip_sums)

    def body(*refs):
        ins, sum_refs = refs[:5 * nu], refs[5 * nu:5 * nu + ns]
        outs = refs[5 * nu + ns:9 * nu + ns]
        landed_refs, scratch = refs[9 * nu + ns:9 * nu + 2 * ns], refs[9 * nu + 2 * ns:]
        if ns:
            start_chips, finish_chips = _chips_steps(sum_refs, landed_refs, *scratch)
            start_chips()
        for i in range(nu):
            o_ref, r_ref, w_ref, m_ref, v_ref = ins[5 * i:5 * i + 5]
            g_out, d_out, m_out, v_out = outs[4 * i:4 * i + 4]
            g = o_ref[...] + r_ref[0].astype(F32) + r_ref[1].astype(F32) + r_ref[2].astype(F32)
            g_out[...] = g
            d_out[...], m_out[...], v_out[...] = _adamw_math(w_ref[...], g, m_ref[...], v_ref[...])
        if ns:
            finish_chips()

    vmem = pl.BlockSpec(memory_space=pltpu.VMEM)
    out = pl.pallas_call(
        body, name=name,
        in_specs=[vmem] * (5 * nu) + [ANY] * ns, out_specs=[vmem] * (4 * nu) + [ANY] * ns,
        out_shape=[jax.ShapeDtypeStruct(updates[n][2].shape, F32) for n in names for _ in range(4)]
        + _chips_shapes(chip_sums),
        scratch_shapes=_chips_scratch(ns) if ns else [],
        compiler_params=pltpu.CompilerParams(vmem_limit_bytes=56 << 20),
    )(*[a for n in names for a in updates[n]], *chip_sums)
    return {n: out[4 * i:4 * i + 4] for i, n in enumerate(names)}, list(out[4 * nu:])


def _place():
    x, y, c = lax.axis_index("x"), lax.axis_index("y"), lax.axis_index("c")
    chips = [(1 - x, y), (x, 1 - y), (1 - x, 1 - y)]
    return x, y, c, chips


def _gather_steps(ins, outs, send, recv, lsem):
    nt = len(ins)
    x, y, c, (xn, yn, diag) = _place()
    me, sib = (x, y, c), (x, y, 1 - c)

    def slot(t, px, py, pc):
        return outs[t].at[4 * px + 2 * py + pc]

    def copy(t, k, block, to, src=None):
        return pltpu.make_async_remote_copy(
            src_ref=slot(t, *block) if src is None else src, dst_ref=slot(t, *block),
            send_sem=send.at[t, k], recv_sem=recv.at[t, k], device_id=to, device_id_type=MESH)

    mine = [pltpu.make_async_copy(ins[t], slot(t, *me), lsem.at[t]) for t in range(nt)]
    first = [copy(t, k, me, to, src=ins[t]) for t in range(nt) for k, to in ((0, sib), (1, (*xn, c)), (2, (*yn, c)))]

    def start():
        for cp in mine + first:
            cp.start()

    def landed(k, chip, also_to=None):
        for t in range(nt):
            copy(t, k, (*chip, c), me).wait_recv()
            if also_to is not None:
                copy(t, 3, (*chip, c), (*also_to, c)).start()
            copy(t, 3 + k, (*chip, c), sib).start()

    def relay():
        @pl.when(c == 0)
        def _():
            landed(1, xn, also_to=yn)
            landed(2, yn)

        @pl.when(c == 1)
        def _():
            landed(2, yn, also_to=xn)
            landed(1, xn)

    def finish():
        landed(3, diag)
        for t in range(nt):
            copy(t, 0, sib, me).wait_recv()
            for k, chip in ((4, xn), (5, yn), (6, diag)):
                copy(t, k, (*chip, 1 - c), me).wait_recv()
            for k in range(7):
                copy(t, k, me, sib).wait_send()
        for cp in mine:
            cp.wait()

    return start, relay, finish


def _gather_scratch(nt):
    return [pltpu.SemaphoreType.DMA((nt, 7)), pltpu.SemaphoreType.DMA((nt, 7)), pltpu.SemaphoreType.DMA((nt,))]


def _gathered_shapes(shards):
    return [jax.ShapeDtypeStruct((N_DEV,) + s.shape, s.dtype) for s in shards]


def _call_with_gather(body, n_grid, shards, *, name, in_specs, out_specs, out_shape, scratch_shapes, vmem_mb, args):
    ng, n_in, n_out = len(shards), len(in_specs), len(out_specs)

    def wrapped(*refs):
        ins, shard_refs = refs[:n_in], refs[n_in:n_in + ng]
        outs = refs[n_in + ng:n_in + ng + n_out]
        whole_refs = refs[n_in + ng + n_out:n_in + 2 * ng + n_out]
        scratch = refs[n_in + 2 * ng + n_out:]
        if ng:
            start, relay, finish = _gather_steps(shard_refs, whole_refs, *scratch[len(scratch_shapes):])
            pl.when(pl.program_id(0) == 0)(start)
            pl.when(pl.program_id(0) == n_grid // 2)(relay)
        body(*ins, *outs, *scratch[:len(scratch_shapes)])
        if ng:
            pl.when(pl.program_id(0) == n_grid - 1)(finish)

    return pl.pallas_call(
        wrapped, grid=(n_grid,), name=name,
        in_specs=list(in_specs) + [ANY] * ng, out_specs=list(out_specs) + [ANY] * ng,
        out_shape=list(out_shape) + _gathered_shapes(shards),
        scratch_shapes=list(scratch_shapes) + (_gather_scratch(ng) if ng else []),
        compiler_params=_cparams(vmem_mb))(*args, *shards)


def _chips_steps(ins, outs, send, recv):
    _, _, c, chips = _place()
    copies = [pltpu.make_async_remote_copy(
        src_ref=ins[t].at[2 * px + py], dst_ref=outs[t].at[j], send_sem=send.at[t, j], recv_sem=recv.at[t, j],
        device_id=(px, py, c), device_id_type=MESH) for t in range(len(ins)) for j, (px, py) in enumerate(chips)]

    def start():
        for cp in copies:
            cp.start()

    def finish():
        for cp in copies:
            cp.wait()

    return start, finish


def _chips_scratch(nt):
    return [pltpu.SemaphoreType.DMA((nt, 3)), pltpu.SemaphoreType.DMA((nt, 3))]


def _chips_shapes(cs16s):
    return [jax.ShapeDtypeStruct((3,) + g.shape[1:], g.dtype) for g in cs16s]


SMALL = (("g_pre_mix", 0, 0, D), ("g_mem", 1, 0, D), ("g_post_mix", 2, 0, D), ("g_attn_out", 3, 0, AW),
         ("g_conv_out", 3, AW, CW), ("g_xattn_out", 3, AW + CW, XW), ("g_post_mlp", 4, 0, D), ("g_pre_mlp", 5, 0, D))
CONV_ROW = 8
PACK_ROWS = 16


LOSS_ROW = 15


def _small_reduce_steps(accs, tot_ref, pack, land, send, recv):
    acc_in, acc_mem, acc_mix, acc_mlp, acc_cw, acc_loss = accs
    x, y, c, _ = _place()
    me = 4 * x + 2 * y + c
    copies = []
    for k in range(1, N_DEV):
        kx, ky, kc = (k >> 2) & 1, (k >> 1) & 1, k & 1
        peer = (1 - x if kx else x, 1 - y if ky else y, 1 - c if kc else c)
        copies.append(pltpu.make_async_remote_copy(
            src_ref=pack, dst_ref=land.at[me], send_sem=send.at[k - 1], recv_sem=recv.at[k - 1],
            device_id=peer, device_id_type=MESH))

    def start():
        pack[...] = jnp.zeros_like(pack)
        pack[0:1, :] = acc_in[0:1, :]
        pack[1:2, :] = acc_mem[0:1, :]
        pack[2:4, :] = acc_mix[0:2, :]
        pack[4:6, :] = acc_mlp[0:2, :]
        pack[CONV_ROW:CONV_ROW + 3, 0:CW] = acc_cw[0:3, :]
        pack[LOSS_ROW:LOSS_ROW + 1, 0:LANES] = acc_loss[0:1, :]
        land[me] = pack[...]
        for cp in copies:
            cp.start()

    def finish():
        for cp in copies:
            cp.wait()
        tot = land[0]
        for s in range(1, N_DEV):
            tot = tot + land[s]
        tot_ref[...] = tot

    return start, finish


def _small_reduce_scratch():
    return [pltpu.VMEM((PACK_ROWS, D), F32), pltpu.VMEM((N_DEV, PACK_ROWS, D), F32),
            pltpu.SemaphoreType.DMA((N_DEV - 1,)), pltpu.SemaphoreType.DMA((N_DEV - 1,))]


def _small_update(tot, me, params):
    flat = [a for n, _, _, _ in SMALL for a in params[n]] + list(params["conv_w"])
    n_par = len(SMALL) + 1
    tap_cols = CW // N_DEV

    def body(*refs):
        me_ref, tot_ref = refs[0:2]
        ins = refs[2:2 + 3 * n_par]
        loss_out = refs[2 + 3 * n_par]
        outs = refs[3 + 3 * n_par:]
        tot = tot_ref[...]
        loss_out[...] = jnp.broadcast_to(tot[LOSS_ROW:LOSS_ROW + 1, 0:LANES], loss_out.shape)

        def update(i, g):
            w_ref, m_ref, v_ref = ins[3 * i:3 * i + 3]
            for o_ref, res in zip(outs[4 * i:4 * i + 4], (g,) + _adamw_math(w_ref[...], g, m_ref[...], v_ref[...])):
                if len(o_ref.shape) == 3:
                    for t in range(o_ref.shape[0]):
                        o_ref[t] = res[t:t + 1, :]
                else:
                    o_ref[...] = res

        for i, (_, row, lane0, width) in enumerate(SMALL):
            update(i, tot[row:row + 1, lane0:lane0 + width])
        me = me_ref[0]
        taps = pltpu.roll(tot[CONV_ROW:CONV_ROW + SUBLANES, 0:CW], jnp.where(me == 0, 0, CW - me * tap_cols), 1)
        update(n_par - 1, taps[0:3, 0:tap_cols])

    shapes = [jax.ShapeDtypeStruct(params[n][0].shape, F32) for n, _, _, _ in SMALL] + [
        jax.ShapeDtypeStruct((3, 1, tap_cols), F32)]
    vmem = pl.BlockSpec(memory_space=pltpu.VMEM)
    loss, *out = pl.pallas_call(
        body, name="small_update",
        in_specs=[pl.BlockSpec(memory_space=pltpu.SMEM)] + [vmem] * (1 + 3 * n_par),
        out_shape=[jax.ShapeDtypeStruct((SUBLANES, LANES), F32)] + [s for s in shapes for _ in range(4)],
    )(me, tot, *flat)
    names = [n for n, _, _, _ in SMALL] + ["conv_w"]
    return loss[0, 0], {n: out[4 * i:4 * i + 4] for i, n in enumerate(names)}


def _local_step(x, mem, pos, gains, shards, tgt, place):
    half = HEAD // 2
    inv_freq = jnp.float32(ROPE_THETA) ** (-(jnp.arange(half, dtype=F32) * 2.0 / HEAD))
    invf = jnp.tile(inv_freq, LANES // half)[None, :]
    sgn = jnp.tile(jnp.concatenate([-jnp.ones((half,), F32), jnp.ones((half,), F32)]), LANES // HEAD)[None, :]
    cos, sins, win8 = _rope_table(pos.astype(F32).reshape(S, 1), invf, sgn, [shards["w_in"]])
    wdn_left, wdn_right = shards["w_down"][:, 0:D // 2], shards["w_down"][:, D // 2:]
    q, kvp, bcu, qx16, h16, win16, wout8, wkv8, conv8, wdn8_right = _in_proj(
        x, gains["g_pre_mix"], win8, cos, sins, [shards["w_out"], shards["w_mem_kv"], shards["conv_w"], wdn_right])
    wout16, wkv16 = wout8.reshape(D, D), wkv8.reshape(D, 2 * XW)
    cw_full = conv8[:, 0:3, 0:CW // N_DEV].transpose(1, 0, 2).reshape(3, CW)
    cw8 = jnp.zeros((SUBLANES, CW), F32).at[0:3].set(cw_full)
    y_attn, ltot, wup8, wdn8_left = _attn_fwd(q, kvp, [shards["w_up"], wdn_left])
    wdn_halves = (wdn8_left.reshape(FF, D // 2), wdn8_right.reshape(FF, D // 2))
    memn16, kv16 = _mem_fwd(mem, gains["g_mem"], wkv16)
    ypre, y16, y2, x1 = _mix_out(y_attn, bcu, qx16, kv16, cw8, gains["g_attn_out"], gains["g_conv_out"],
                                 gains["g_xattn_out"], gains["g_post_mix"], wout16, x, [])
    a16, du16, h2_16, df2_16, dx1, loss8, dg_mlp = _mlp(
        x1, tgt, gains["g_pre_mlp"], gains["g_post_mlp"], wup8, wdn_halves)

    sums = {"w_up": _wgrad_cols(place, h2_16, du16, FF_BLK, "wgrad_up"),
            "w_down": _wgrad_cols(place, df2_16, a16, FF_BLK, "wgrad_down", square_b=True, transpose_out=True)}

    head_id = jnp.arange(AW, dtype=jnp.int32) // HEAD
    head_ones = (head_id[:, None] == head_id[None, :]).astype(BF16)
    dy2_16, qdo, ld, dbcu, dqx, dgs, dcw, dkv = _mix_out_bwd(
        dx1, y2, ypre, ltot, head_ones, q, bcu, qx16, kv16, cw8, gains["g_post_mix"], gains["g_attn_out"],
        gains["g_conv_out"], gains["g_xattn_out"], wout16)
    dkv16, dg_mem = _mem_bwd(mem, gains["g_mem"], wkv16, dkv)
    sums["w_mem_kv"], sums["w_out"] = _wgrad_rows(place, [(memn16, dkv16), (y16, dy2_16)], "wgrad_mem_kv_out")
    out = _attn_bwd(qdo, kvp, ld, [s[0] for s in sums.values()])
    dqkv, landed = out[:9], out[9:]
    reduced = {n: (s[1], landed[t]) for t, (n, s) in enumerate(sums.items())}
    dproj16, grad_x, dg_in = _in_proj_bwd(dqkv, dbcu, dqx, cos, sins, win16, x, gains["g_pre_mix"], dx1)

    _, in_own, in_landed, small_tot = _wgrad_cols(place, h16, dproj16, PW // N_DEV, "wgrad_in", transpose_out=True,
                                                  to_chips=True, small=(dg_in, dg_mem, dgs, dg_mlp, dcw, loss8))
    reduced["w_in"] = (in_own, in_landed)
    return grad_x, reduced, small_tot


BIG = ("w_in", "w_mem_kv", "w_out", "w_up", "w_down")
ORDER = ("g_pre_mix", "g_mem", "w_in", "w_mem_kv", "conv_w", "g_attn_out", "g_conv_out", "g_xattn_out", "w_out",
         "g_post_mix", "g_pre_mlp", "w_up", "w_down", "g_post_mlp")


def kernel(x, mem, positions, g_pre_mix, g_mem, w_in, w_mem_kv, conv_w, g_attn_out, g_conv_out, g_xattn_out, w_out, g_post_mix, g_pre_mlp, w_up, w_down, g_post_mlp, loss_target, m_g_pre_mix, m_g_mem, m_w_in, m_w_mem_kv, m_conv_w, m_g_attn_out, m_g_conv_out, m_g_xattn_out, m_w_out, m_g_post_mix, m_g_pre_mlp, m_w_up, m_w_down, m_g_post_mlp, v_g_pre_mix, v_g_mem, v_w_in, v_w_mem_kv, v_conv_w, v_g_attn_out, v_g_conv_out, v_g_xattn_out, v_w_out, v_g_post_mix, v_g_pre_mlp, v_w_up, v_w_down, v_g_post_mlp):
    w = dict(g_pre_mix=g_pre_mix, g_mem=g_mem, w_in=w_in, w_mem_kv=w_mem_kv, conv_w=conv_w, g_attn_out=g_attn_out,
             g_conv_out=g_conv_out, g_xattn_out=g_xattn_out, w_out=w_out, g_post_mix=g_post_mix, g_pre_mlp=g_pre_mlp,
             w_up=w_up, w_down=w_down, g_post_mlp=g_post_mlp)
    mo = dict(g_pre_mix=m_g_pre_mix, g_mem=m_g_mem, w_in=m_w_in, w_mem_kv=m_w_mem_kv, conv_w=m_conv_w,
              g_attn_out=m_g_attn_out, g_conv_out=m_g_conv_out, g_xattn_out=m_g_xattn_out, w_out=m_w_out,
              g_post_mix=m_g_post_mix, g_pre_mlp=m_g_pre_mlp, w_up=m_w_up, w_down=m_w_down, g_post_mlp=m_g_post_mlp)
    vo = dict(g_pre_mix=v_g_pre_mix, g_mem=v_g_mem, w_in=v_w_in, w_mem_kv=v_w_mem_kv, conv_w=v_conv_w,
              g_attn_out=v_g_attn_out, g_conv_out=v_g_conv_out, g_xattn_out=v_g_xattn_out, w_out=v_w_out,
              g_post_mix=v_g_post_mix, g_pre_mlp=v_g_pre_mlp, w_up=v_w_up, w_down=v_w_down, g_post_mlp=v_g_post_mlp)

    xi, yi, ci = lax.axis_index("x"), lax.axis_index("y"), lax.axis_index("c")
    me = 4 * xi + 2 * yi + ci
    place = jnp.stack([ci, 2 * xi + yi]).astype(jnp.int32)

    shards = {n: w[n][0].astype(BF16) for n in BIG}
    shards["conv_w"] = jnp.zeros((SUBLANES, LANES), F32).at[0:3, 0:CW // N_DEV].set(conv_w[0])

    gains = {n: w[n] for n, _, _, _ in SMALL}
    grad_x, reduced, small_tot = _local_step(x[0], mem[0], positions[0], gains, shards, loss_target[0], place)

    def shard(n, a):
        return a[0].T if n == "w_in" else a[0]

    updated = {}
    for group in (("w_up", "w_down"), ("w_in", "w_out", "w_mem_kv")):
        updated.update(_adamw_shards({n: (*reduced[n], shard(n, w[n]), shard(n, mo[n]), shard(n, vo[n]))
                                      for n in group}, "adamw_" + "_".join(group))[0])
    grad, delta, new_m, new_v = {}, {}, {}, {}
    for n, res in updated.items():
        grad[n], delta[n], new_m[n], new_v[n] = [(a.T if n == "w_in" else a)[None] for a in res]

    params = {n: (w[n], mo[n], vo[n]) for n, _, _, _ in SMALL}
    params["conv_w"] = (w["conv_w"][0], mo["conv_w"][0], vo["conv_w"][0])
    loss, small = _small_update(small_tot, me.reshape(1).astype(jnp.int32), params)
    for n, (g, d_, m_, v_) in small.items():
        lead = (lambda a: a.reshape(conv_w.shape)) if n == "conv_w" else (lambda a: a)
        grad[n], delta[n], new_m[n], new_v[n] = lead(g), lead(d_), lead(m_), lead(v_)

    return (loss, grad_x[None], *[grad[n] for n in ORDER], *[delta[n] for n in ORDER],
            *[new_m[n] for n in ORDER], *[new_v[n] for n in ORDER])
```

```python
import jax
import jax.numpy as jnp
from jax import lax
from jax.experimental import pallas as pl
from jax.experimental.pallas import tpu as pltpu

F32, BF16 = jnp.float32, jnp.bfloat16
MESH = pl.DeviceIdType.MESH
ANY = pl.BlockSpec(memory_space=pl.ANY)

N_DEV = 8
D = 1024
S = 4096
N_MEM = 256
HEAD = 64
AW, CW, XW = 512, 256, 256
PW = 3 * AW + 3 * CW + XW
FF = 4096
FF_BLK = FF // N_DEV
EPS = 1e-6
NEG = -1e30
SCALE = HEAD ** -0.5
ROPE_THETA = 10000.0
LANES = 128
SUBLANES = 8

ADAM_LR, ADAM_B1, ADAM_B2, ADAM_EPS, ADAM_WD, ADAM_STEP = 0.001, 0.9, 0.999, 1e-08, 0.01, 10

TQ = 512
TQ_MLP = 512
NT = S // TQ


def _cparams(vmem_mb, n_grid=1, **more):
    return pltpu.CompilerParams(dimension_semantics=("arbitrary",) * n_grid, vmem_limit_bytes=vmem_mb << 20, **more)


def _const(shape):
    nd = len(shape)
    return pl.BlockSpec(shape, lambda *_: (0,) * nd, pipeline_mode=pl.Buffered(1))


def _acc(shape):
    nd = len(shape)
    return pl.BlockSpec(shape, lambda *_: (0,) * nd)


def _tokens_in_lanes(tq):
    return pl.BlockSpec((D, tq), lambda i: (0, i))


def _dot(a, b):
    return jnp.dot(a, b, preferred_element_type=F32)


def _dot_nt(a, b):
    return lax.dot_general(a, b, (((1,), (1,)), ((), ())), preferred_element_type=F32)


def _dot_tn(a, b):
    return lax.dot_general(a, b, (((0,), (0,)), ((), ())), preferred_element_type=F32)


def _rms(x, g):
    r = lax.rsqrt(jnp.mean(x * x, axis=-1, keepdims=True) + EPS)
    n = x * r
    return n * g, n, r


def _rms_bwd(dy, n, r, g):
    dn = dy * g
    dx = r * (dn - n * jnp.mean(dn * n, axis=-1, keepdims=True))
    return dx, jnp.sum(dy * n, axis=0, keepdims=True)


def _rot_half(t):
    lane = lax.broadcasted_iota(jnp.int32, t.shape, 1)
    n = t.shape[1]
    return jnp.where((lane % HEAD) < HEAD // 2, pltpu.roll(t, n - HEAD // 2, 1), pltpu.roll(t, HEAD // 2, 1))


def _rope_table(pos_col, invf, sgn, shards):
    def body(p_ref, f_ref, s_ref, c_out, s_out):
        ang = p_ref[...] * f_ref[...]
        c_out[...] = jnp.cos(ang)
        s_out[...] = jnp.sin(ang) * s_ref[...]

    tile = pl.BlockSpec((TQ, LANES), lambda i: (i, 0))
    return _call_with_gather(
        body, NT, shards, name="rope_table",
        in_specs=[pl.BlockSpec((TQ, 1), lambda i: (i, 0)), _const((1, LANES)), _const((1, LANES))],
        out_specs=[tile, tile], out_shape=[jax.ShapeDtypeStruct((S, LANES), F32)] * 2,
        scratch_shapes=[], vmem_mb=32, args=(pos_col, invf, sgn))


def _all_heads(t):
    return jnp.tile(t, (1, AW // LANES))


def _mem_fwd(mem, g_mem, wkv16):
    def body(m_ref, g_ref, w_ref, n16_ref, kv_ref):
        y, _, _ = _rms(m_ref[...], g_ref[...])
        y16 = y.astype(BF16)
        n16_ref[...] = y16.T
        kv_ref[...] = _dot(y16, w_ref[...]).astype(BF16)

    return pl.pallas_call(
        body, name="mem_fwd",
        out_shape=[jax.ShapeDtypeStruct((D, N_MEM), BF16), jax.ShapeDtypeStruct((N_MEM, 2 * XW), BF16)],
        compiler_params=pltpu.CompilerParams(vmem_limit_bytes=32 << 20))(mem, g_mem, wkv16)


def _in_proj(x, g, w8, cos, sins, shards):
    blk = PW // N_DEV

    def body(x_ref, g_ref, w8_ref, c_ref, s_ref, q_ref, kv_ref, bcu_ref, qx_ref, h_ref, w_out, w_ref):
        @pl.when(pl.program_id(0) == 0)
        def _():
            for j in range(N_DEV):
                w_ref[:, j * blk:(j + 1) * blk] = w8_ref[j]
            w_out[...] = w_ref[...]

        y, _, _ = _rms(x_ref[...], g_ref[...])
        h = y.astype(BF16)
        h_ref[...] = h.T
        proj = _dot(h, w_ref[...])
        cos, sn = _all_heads(c_ref[...]), _all_heads(s_ref[...])
        q, k = proj[:, 0:AW], proj[:, AW:2 * AW]
        q_ref[...] = (q * cos + _rot_half(q) * sn) * SCALE
        kv_ref[...] = _pack_pair(k * cos + _rot_half(k) * sn, proj[:, 2 * AW:3 * AW])
        bcu_ref[...] = proj[:, 3 * AW:3 * AW + 3 * CW]
        qx_ref[...] = (proj[:, 3 * AW + 3 * CW:] * SCALE).astype(BF16)

    def tile(w):
        return pl.BlockSpec((TQ, w), lambda i: (i, 0))

    return _call_with_gather(
        body, NT, shards, name="in_proj",
        in_specs=[tile(D), _const((1, D)), _const((N_DEV, D, blk)), tile(LANES), tile(LANES)],
        out_specs=[tile(AW), tile(AW), tile(3 * CW), tile(XW), _tokens_in_lanes(TQ), _acc((D, PW))],
        out_shape=[jax.ShapeDtypeStruct((S, AW), F32)] * 2 + [
            jax.ShapeDtypeStruct((S, 3 * CW), F32), jax.ShapeDtypeStruct((S, XW), BF16),
            jax.ShapeDtypeStruct((D, S), BF16), jax.ShapeDtypeStruct((D, PW), BF16)],
        scratch_shapes=[pltpu.VMEM((D, PW), BF16)], vmem_mb=56, args=(x, g, w8, cos, sins))


ATTN_PLANS = (("p1", 1, 128, 32), ("p4", 8, 64, 8), ("p16", 16, 128, 2))
PAD = 128
WIN = 256


ATTN_UNROLL = 16


def _fill_bias(tab, qblk, partner):
    qi = lax.broadcasted_iota(jnp.int32, (2 * qblk, WIN), 0) & (qblk - 1)
    kj = lax.broadcasted_iota(jnp.int32, (2 * qblk, WIN), 1)
    piece = kj >> (qblk.bit_length() - 1)
    kk = kj & (qblk - 1)
    prev = (piece & 1) == 0
    of_partner = piece >= 2
    for first in (0, 1):
        for par in (0, 1):
            lo = jnp.where(prev, (qblk if first else qi) + jnp.where(of_partner, par, 0), 0)
            hi = jnp.where(prev, qblk, qi + jnp.where(of_partner, par - 1, 0))
            tab[2 * first + par] = jnp.where((kk >= lo) & (kk <= hi), 0.0, NEG).astype(F32)


def _block_rows(g, qblk, nbc, partner):
    own = pl.ds(pl.multiple_of(PAD + g * qblk, qblk), qblk)
    first = ((g & (nbc - 1)) == 0).astype(jnp.int32)
    if partner:
        gp = jnp.bitwise_xor(g, 4 * nbc)
        wins = (pl.ds(pl.multiple_of(PAD + (g - 1) * qblk, qblk), 2 * qblk),
                pl.ds(pl.multiple_of(PAD + (gp - 1) * qblk, qblk), 2 * qblk))
        return own, wins, 2 * first + ((g >> ((4 * nbc).bit_length() - 1)) & 1)
    return own, (pl.ds(pl.multiple_of(PAD + (g - 1) * qblk, qblk), 2 * qblk),), 2 * first


def _pack_pair(lo, hi):
    lo_bits = lax.bitcast_convert_type(lo.astype(BF16).astype(F32), jnp.uint32) >> 16
    hi_bits = lax.bitcast_convert_type(hi.astype(BF16).astype(F32), jnp.uint32) & jnp.uint32(0xFFFF0000)
    return lax.bitcast_convert_type(hi_bits | lo_bits, F32)


def _unpack_pair(c):
    bits = lax.bitcast_convert_type(c, jnp.uint32)
    lo = lax.bitcast_convert_type(bits << 16, F32).astype(BF16)
    hi = lax.bitcast_convert_type(bits & jnp.uint32(0xFFFF0000), F32).astype(BF16)
    return lo, hi


def _window(ref, wins):
    parts = [ref[w, :] for w in wins]
    return parts[0] if len(parts) == 1 else jnp.concatenate(parts, axis=0)


def _stack_heads(t, lane):
    zero = jnp.zeros_like(t)
    return jnp.concatenate([jnp.where(lane < HEAD, t, zero), jnp.where(lane >= HEAD, t, zero)], axis=0)


def _unstack_heads(t2, lane):
    half = t2.shape[0] // 2
    return jnp.where(lane < HEAD, t2[0:half, :], t2[half:, :])


def _lanes_of(step):
    return pl.ds(pl.multiple_of(step * LANES, LANES), LANES)


def _whole_wait(buf, sem):
    whole = buf.at[pl.ds(PAD, S), :]
    return pltpu.make_async_copy(whole, whole, sem)


def _whole_waits(bufs, sems):
    return [_whole_wait(buf, sems.at[i]) for i, buf in enumerate(bufs)]


def _class_gather(views, bufs, sems, lanes):
    copies = []
    for i, (view, buf) in enumerate(zip(views, bufs)):
        if view.ndim == 2:
            copies.append(pltpu.make_async_copy(view.at[:, lanes], buf.at[pl.ds(PAD, S), :], sems.at[i]))
        else:
            per, n_cls = view.shape[0], view.shape[1]
            copies += [pltpu.make_async_copy(view.at[:, c, lanes], buf.at[pl.ds(PAD + c * per, per), :], sems.at[i])
                       for c in range(n_cls)]
    return copies


def _class_scatter(bufs, dsts, sems, lanes):
    copies = []
    for i, (buf, dst) in enumerate(zip(bufs, dsts)):
        if dst.ndim == 2:
            copies.append(pltpu.make_async_copy(buf.at[pl.ds(PAD, S), :], dst.at[:, lanes], sems.at[i]))
            continue
        per, n_cls = dst.shape[0], dst.shape[1]
        copies += [pltpu.make_async_copy(buf.at[pl.ds(PAD + c * per, per), :], dst.at[:, c, lanes], sems.at[i])
                   for c in range(n_cls)]
    return copies


def _start(copies):
    for cp in copies:
        cp.start()


def _wait(waits):
    for w in waits:
        w.wait()


def _attn_fwd(q, kvp, shards=()):
    views = [[a] + [a.reshape(S // n, n, AW) for _, n, _, _ in ATTN_PLANS[1:]] for a in (q, kvp)]
    flat = [views[a][p] for p in range(3) for a in range(2)]
    ng = len(shards)
    n_grid = AW // LANES

    def body(*refs):
        hbm = [refs[2 * p:2 * p + 2] for p in range(3)]
        refs = refs[6:]
        shard_refs, refs = refs[:ng], refs[ng:]
        y_ref, lt_ref = refs[0:2]
        whole_refs, refs = refs[2:2 + ng], refs[2 + ng:]
        bufs = [refs[2 * p:2 * p + 2] for p in range(3)]
        oc4, lc4, oc16, lc16, tab128, tab4, sem_in = refs[6:13]
        step = pl.program_id(0)
        if ng:
            start_gather, relay_gather, finish_gather = _gather_steps(shard_refs, whole_refs, *refs[13:])
            pl.when(step == 0)(start_gather)
            pl.when(step == n_grid // 2)(relay_gather)
        now = [_class_gather(hbm[p], bufs[p], sem_in.at[p], _lanes_of(step)) for p in range(3)]
        nxt = [_class_gather(hbm[p], bufs[p], sem_in.at[p], _lanes_of(step + 1)) for p in range(3)]

        @pl.when(step == 0)
        def _():
            for p in range(3):
                _start(now[p])
                for b in bufs[p]:
                    b[0:PAD, :] = jnp.zeros((PAD, LANES), F32)
            _fill_bias(tab128, 128, False)
            _fill_bias(tab4, 64, True)

        def prefetch(p):
            pl.when(step + 1 < n_grid)(lambda: _start(nxt[p]))

        lane = lax.broadcasted_iota(jnp.int32, (1, LANES), 1)
        ones = jnp.ones((WIN, LANES), BF16)

        def run(plan, bq, bkv, tab, o_dst, l_dst, dst_pad):
            _, n_cls, qblk, nbc = plan
            partner = n_cls == 8

            def block(g, carry):
                own, wins, mask = _block_rows(g, qblk, nbc, partner)
                q2 = _stack_heads(bq[own, :].astype(BF16), lane)
                kw, vwin = _unpack_pair(_window(bkv, wins))
                vw = jnp.concatenate([vwin, ones], axis=1)
                s = _dot_nt(q2, kw) + tab[mask]
                m = jnp.max(s, axis=1, keepdims=True)
                oe = _dot(jnp.exp(s - m).astype(BF16), vw)
                den = oe[:, LANES:]
                dst = pl.ds(pl.multiple_of(dst_pad + g * qblk, qblk), qblk)
                o_dst[dst, :] = _unstack_heads(oe[:, 0:LANES] / den, lane)
                l_dst[dst, :] = _unstack_heads(m + jnp.log(den), lane)
                return carry
            lax.fori_loop(0, n_cls * nbc, block, 0, unroll=ATTN_UNROLL)

        _wait(_whole_waits(bufs[0], sem_in.at[0]))
        run(ATTN_PLANS[0], *bufs[0], tab128, y_ref, lt_ref, 0)
        prefetch(0)
        _wait(_whole_waits(bufs[1], sem_in.at[1]))
        run(ATTN_PLANS[1], *bufs[1], tab4, oc4, lc4, PAD)
        prefetch(1)
        _wait(_whole_waits(bufs[2], sem_in.at[2]))
        run(ATTN_PLANS[2], *bufs[2], tab128, oc16, lc16, PAD)
        prefetch(2)

        n_rows = 64

        def token_order(buf, t, n_cls):
            per = S // n_cls
            first = PAD + t * (n_rows // n_cls)
            return jnp.concatenate([buf[pl.ds(first + jj, n_cls, stride=per), :] for jj in range(n_rows // n_cls)],
                                   axis=0)

        def combine(t, carry):
            rows = pl.ds(pl.multiple_of(t * n_rows, n_rows), n_rows)
            l0, l1, l2 = lt_ref[rows, :], token_order(lc4, t, 8), token_order(lc16, t, 16)
            lm = jnp.maximum(jnp.maximum(l0, l1), l2)
            e0, e1, e2 = jnp.exp(l0 - lm), jnp.exp(l1 - lm), jnp.exp(l2 - lm)
            den = e0 + e1 + e2
            y_ref[rows, :] = (e0 * y_ref[rows, :] + e1 * token_order(oc4, t, 8)
                              + e2 * token_order(oc16, t, 16)) / den
            lt_ref[rows, :] = lm + jnp.log(den)
            return carry
        lax.fori_loop(0, S // n_rows, combine, 0, unroll=2)

        if ng:
            pl.when(step == n_grid - 1)(finish_gather)

    col = pl.BlockSpec((S, LANES), lambda h: (0, h))
    padded = pltpu.VMEM((PAD + S, LANES), F32)
    return pl.pallas_call(
        body, grid=(n_grid,), name="attn_fwd",
        in_specs=[ANY] * (6 + ng), out_specs=[col, col] + [ANY] * ng,
        out_shape=[jax.ShapeDtypeStruct((S, AW), F32)] * 2 + _gathered_shapes(shards),
        scratch_shapes=[padded] * 10 + [
            pltpu.VMEM((4, 256, WIN), F32), pltpu.VMEM((4, 128, WIN), F32), pltpu.SemaphoreType.DMA((3, 2))]
        + (_gather_scratch(ng) if ng else []),
        compiler_params=_cparams(56))(*flat, *shards)


def _conv_taps(z, zprev, row):
    z1 = jnp.where(row == 0, zprev[7:8, :], pltpu.roll(z, 1, 0))
    z2 = jnp.where(row == 0, zprev[6:7, :], jnp.where(row == 1, zprev[7:8, :], pltpu.roll(z, 2, 0)))
    return z1, z2


def _xattn_scores(qm, km):
    s = _dot_nt(qm, km)
    m = jnp.max(s, axis=1, keepdims=True)
    e = jnp.exp(s - m)
    return e, jnp.sum(e, axis=1, keepdims=True)


def _mix_out(y_attn, bcu, qx16, kv16, cw8, g_attn, g_conv, g_x, g_post, wout16, x, shards):
    def body(ya_ref, bcu_ref, halo_ref, qx_ref, kv_ref, cw_ref, ga_ref, gc_ref, gx_ref, gp_ref, w_ref, x_ref,
             ypre_ref, y16_ref, y2_ref, x1_ref):
        i = pl.program_id(0)
        bcu = bcu_ref[...]
        b, c, u = bcu[:, 0:CW], bcu[:, CW:2 * CW], bcu[:, 2 * CW:]
        z = c * u
        halo = halo_ref[...]
        zprev = jnp.where(i > 0, halo[:, CW:2 * CW] * halo[:, 2 * CW:], 0.0)
        row = lax.broadcasted_iota(jnp.int32, z.shape, 0)
        z1, z2 = _conv_taps(z, zprev, row)
        cw = cw_ref[...]
        y_conv = b * (z2 * cw[0:1, :] + z1 * cw[1:2, :] + z * cw[2:3, :])

        qx = qx_ref[...]
        kv = kv_ref[...]
        km, vm = kv[:, 0:XW], kv[:, XW:]
        lane = lax.broadcasted_iota(jnp.int32, qx.shape, 1)
        y_x = jnp.zeros(qx.shape, F32)
        for h in range(XW // HEAD):
            hm = (lane >= h * HEAD) & (lane < (h + 1) * HEAD)
            e, l = _xattn_scores(jnp.where(hm, qx, jnp.zeros_like(qx)), km)
            y_x = jnp.where(hm, _dot(e.astype(BF16), vm) / l, y_x)

        y_attn = ya_ref[...]
        ypre_ref[:, 0:AW] = y_attn
        ypre_ref[:, AW:AW + CW] = y_conv
        ypre_ref[:, AW + CW:] = y_x
        y = jnp.concatenate([_rms(y_attn, ga_ref[...])[0], _rms(y_conv, gc_ref[...])[0],
                             _rms(y_x, gx_ref[...])[0]], axis=1).astype(BF16)
        y16_ref[...] = y.T
        y2 = _dot(y, w_ref[...])
        y2_ref[...] = y2
        x1_ref[...] = x_ref[...] + _rms(y2, gp_ref[...])[0]

    def tile(w):
        return pl.BlockSpec((TQ, w), lambda i: (i, 0))

    halo = pl.BlockSpec((SUBLANES, 3 * CW), lambda i: (jnp.maximum(i * (TQ // SUBLANES) - 1, 0), 0))
    return _call_with_gather(
        body, NT, shards, name="mix_out",
        in_specs=[tile(AW), tile(3 * CW), halo, tile(XW), _const((N_MEM, 2 * XW)), _const((SUBLANES, CW)),
                  _const((1, AW)), _const((1, CW)), _const((1, XW)), _const((1, D)), _const((D, D)), tile(D)],
        out_specs=[tile(D), _tokens_in_lanes(TQ), tile(D), tile(D)],
        out_shape=[jax.ShapeDtypeStruct((S, D), F32), jax.ShapeDtypeStruct((D, S), BF16),
                   jax.ShapeDtypeStruct((S, D), F32), jax.ShapeDtypeStruct((S, D), F32)],
        scratch_shapes=[], vmem_mb=56,
        args=(y_attn, bcu, bcu, qx16, kv16, cw8, g_attn, g_conv, g_x, g_post, wout16, x))


def _mlp(x1, tgt, g_pre, g_post, wup8, wdn_halves):
    tq = TQ_MLP
    half = D // 2

    def body(x1_ref, t_ref, g1_ref, g2_ref, wu_ref, wda_ref, wdb_ref,
             a16_ref, du_ref, h2_ref, df2_ref, dx1_ref, loss_ref, dg_ref):
        @pl.when(pl.program_id(0) == 0)
        def _():
            loss_ref[...] = jnp.zeros_like(loss_ref)
            dg_ref[...] = jnp.zeros_like(dg_ref)

        x1 = x1_ref[...]
        g1, g2 = g1_ref[...], g2_ref[...]
        y1, n1, r1 = _rms(x1, g1)
        h2 = y1.astype(BF16)
        h2_ref[...] = h2.T
        f2a = jnp.zeros((tq, half), F32)
        f2b = jnp.zeros((tq, half), F32)
        for j in range(N_DEV):
            cols = slice(j * FF_BLK, (j + 1) * FF_BLK)
            a = jnp.maximum(_dot(h2, wu_ref[j]), 0.0)
            a16_ref[:, cols] = a.astype(BF16)
            f = (a * a).astype(BF16)
            f2a = f2a + _dot(f, wda_ref[cols, :])
            f2b = f2b + _dot(f, wdb_ref[cols, :])
        f2 = jnp.concatenate([f2a, f2b], axis=1)
        y2, n2, r2 = _rms(f2, g2)
        e = x1 + y2 - t_ref[...]
        sq = jnp.sum(jnp.sum(e * e, axis=1, keepdims=True), axis=0, keepdims=True)
        loss_ref[...] += jnp.broadcast_to(sq * (0.5 / D), loss_ref.shape)
        dout = e * (1.0 / D)
        df2, dg2 = _rms_bwd(dout, n2, r2, g2)
        df2_16 = df2.astype(BF16)
        df2_ref[...] = df2_16.T
        dh2 = jnp.zeros((tq, D), F32)
        for j in range(N_DEV):
            cols = slice(j * FF_BLK, (j + 1) * FF_BLK)
            df = _dot_nt(df2_16[:, 0:half], wda_ref[cols, :]) + _dot_nt(df2_16[:, half:], wdb_ref[cols, :])
            du = (df * (2.0 * a16_ref[:, cols].astype(F32))).astype(BF16)
            du_ref[:, cols] = du
            dh2 = dh2 + _dot_nt(du, wu_ref[j])
        dx, dg1 = _rms_bwd(dh2, n1, r1, g1)
        dx1_ref[...] = dout + dx
        dg_ref[0:1, :] += dg2
        dg_ref[1:2, :] += dg1

    def tile(w):
        return pl.BlockSpec((tq, w), lambda i: (i, 0))

    return pl.pallas_call(
        body, grid=(S // tq,), name="mlp",
        in_specs=[tile(D), tile(D), _const((1, D)), _const((1, D)), _const((N_DEV, D, FF_BLK)), _const((FF, half)), _const((FF, half))],
        out_specs=[tile(FF), tile(FF), _tokens_in_lanes(tq), _tokens_in_lanes(tq), tile(D),
                   _acc((SUBLANES, LANES)), _acc((SUBLANES, D))],
        out_shape=[jax.ShapeDtypeStruct((S, FF), BF16), jax.ShapeDtypeStruct((S, FF), BF16),
                   jax.ShapeDtypeStruct((D, S), BF16), jax.ShapeDtypeStruct((D, S), BF16),
                   jax.ShapeDtypeStruct((S, D), F32), jax.ShapeDtypeStruct((SUBLANES, LANES), F32),
                   jax.ShapeDtypeStruct((SUBLANES, D), F32)],
        compiler_params=_cparams(60))(x1, tgt, g_pre, g_post, wup8, *wdn_halves)


def _mix_out_bwd(dx1, y2, ypre, ltot, head_ones, q, bcu, qx16, kv16, cw8, g_post, g_attn, g_conv, g_x, wout16):
    def body(dx1_ref, y2_ref, ypre_ref, lt_ref, e_ref, q_ref, bcu_ref, halo_ref, qx_ref, kv_ref, cw_ref, gp_ref,
             ga_ref, gc_ref, gx_ref, w_ref, dy2_ref, qdo_ref, ld_ref, dbcu_ref, dqx_ref, dgs_ref, dcw_ref, dkv_ref,
             carry):
        i = pl.program_id(0)

        @pl.when(i == 0)
        def _():
            dgs_ref[...] = jnp.zeros_like(dgs_ref)
            dcw_ref[...] = jnp.zeros_like(dcw_ref)
            dkv_ref[...] = jnp.zeros_like(dkv_ref)
            carry[...] = jnp.zeros_like(carry)

        gp = gp_ref[...]
        _, n, r = _rms(y2_ref[...], gp)
        dy2, dgp = _rms_bwd(dx1_ref[...], n, r, gp)
        dy2_16 = dy2.astype(BF16)
        dy2_ref[...] = dy2_16
        dy = _dot_nt(dy2_16, w_ref[...])

        ypre = ypre_ref[...]
        ga, gc, gx = ga_ref[...], gc_ref[...], gx_ref[...]
        _, na, ra = _rms(ypre[:, 0:AW], ga)
        dya, dga = _rms_bwd(dy[:, 0:AW], na, ra, ga)
        _, nc, rc = _rms(ypre[:, AW:AW + CW], gc)
        dyc, dgc = _rms_bwd(dy[:, AW:AW + CW], nc, rc, gc)
        y_x = ypre[:, AW + CW:]
        _, nx, rx = _rms(y_x, gx)
        dyx, dgx = _rms_bwd(dy[:, AW + CW:], nx, rx, gx)
        qdo_ref[...] = _pack_pair(q_ref[...], dya)
        prod = dya * ypre[:, 0:AW]
        hi = prod.astype(BF16)
        lo = (prod - hi.astype(F32)).astype(BF16)
        head_sum = _dot(hi, e_ref[...]) + _dot(lo, e_ref[...])
        lane_a = lax.broadcasted_iota(jnp.int32, prod.shape, 1)
        ld_ref[...] = jnp.where((lane_a % HEAD) < HEAD // 2, lt_ref[...], head_sum)
        dgs_ref[0:1, :] += dgp
        dgs_ref[1:2, :] += jnp.concatenate([dga, dgc, dgx], axis=1)

        bcu = bcu_ref[...]
        b, c, u = bcu[:, 0:CW], bcu[:, CW:2 * CW], bcu[:, 2 * CW:]
        z = c * u
        halo = halo_ref[...]
        zprev = jnp.where(i < NT - 1, halo[:, CW:2 * CW] * halo[:, 2 * CW:], 0.0)
        row = lax.broadcasted_iota(jnp.int32, z.shape, 0)
        z1, z2 = _conv_taps(z, zprev, row)
        cw = cw_ref[...]
        conv = z2 * cw[0:1, :] + z1 * cw[1:2, :] + z * cw[2:3, :]
        dconv = dyc * b
        nxt = carry[...]
        dn1 = jnp.where(row == TQ - 1, nxt[0:1, :], pltpu.roll(dconv, TQ - 1, 0))
        dn2 = jnp.where(row == TQ - 1, nxt[1:2, :], jnp.where(row == TQ - 2, nxt[0:1, :], pltpu.roll(dconv, TQ - 2, 0)))
        carry[...] = dconv[0:SUBLANES, :]
        dz = dconv * cw[2:3, :] + dn1 * cw[1:2, :] + dn2 * cw[0:1, :]
        dbcu_ref[:, 0:CW] = (dyc * conv).astype(BF16)
        dbcu_ref[:, CW:2 * CW] = (dz * u).astype(BF16)
        dbcu_ref[:, 2 * CW:] = (dz * c).astype(BF16)
        dcw_ref[0:1, :] += jnp.sum(z2 * dconv, axis=0, keepdims=True)
        dcw_ref[1:2, :] += jnp.sum(z1 * dconv, axis=0, keepdims=True)
        dcw_ref[2:3, :] += jnp.sum(z * dconv, axis=0, keepdims=True)

        qx = qx_ref[...]
        kv = kv_ref[...]
        km, vm = kv[:, 0:XW], kv[:, XW:]
        lane = lax.broadcasted_iota(jnp.int32, qx.shape, 1)
        dqx = jnp.zeros(qx.shape, F32)
        dkm = jnp.zeros((N_MEM, XW), F32)
        dvm = jnp.zeros((N_MEM, XW), F32)
        for h in range(XW // HEAD):
            hm = (lane >= h * HEAD) & (lane < (h + 1) * HEAD)
            qm = jnp.where(hm, qx, jnp.zeros_like(qx))
            e, l = _xattn_scores(qm, km)
            p = e / l
            dom = jnp.where(hm, dyx, 0.0)
            do16 = dom.astype(BF16)
            dsum = jnp.sum(dom * y_x, axis=1, keepdims=True)
            ds = (p * (_dot_nt(do16, vm) - dsum)).astype(BF16)
            dqx = jnp.where(hm, _dot(ds, km), dqx)
            dkm = dkm + _dot_tn(ds, qm)
            dvm = dvm + _dot_tn(p.astype(BF16), do16)
        dqx_ref[...] = (dqx * SCALE).astype(BF16)
        dkv_ref[:, 0:XW] += dkm
        dkv_ref[:, XW:] += dvm

    def tile(w):
        return pl.BlockSpec((TQ, w), lambda i: (NT - 1 - i, 0))

    halo = pl.BlockSpec((SUBLANES, 3 * CW), lambda i: (jnp.maximum((NT - 1 - i) * (TQ // SUBLANES) - 1, 0), 0))
    return pl.pallas_call(
        body, grid=(NT,), name="mix_out_bwd",
        in_specs=[tile(D), tile(D), tile(D), tile(AW), _const((AW, AW)), tile(AW), tile(3 * CW), halo, tile(XW),
                  _const((N_MEM, 2 * XW)), _const((SUBLANES, CW)), _const((1, D)), _const((1, AW)), _const((1, CW)),
                  _const((1, XW)), _const((D, D))],
        out_specs=[tile(D), tile(AW), tile(AW), tile(3 * CW), tile(XW), _acc((SUBLANES, D)), _acc((SUBLANES, CW)),
                   _acc((N_MEM, 2 * XW))],
        out_shape=[jax.ShapeDtypeStruct((S, D), BF16), jax.ShapeDtypeStruct((S, AW), F32),
                   jax.ShapeDtypeStruct((S, AW), F32),
                   jax.ShapeDtypeStruct((S, 3 * CW), BF16), jax.ShapeDtypeStruct((S, XW), BF16),
                   jax.ShapeDtypeStruct((SUBLANES, D), F32), jax.ShapeDtypeStruct((SUBLANES, CW), F32),
                   jax.ShapeDtypeStruct((N_MEM, 2 * XW), F32)],
        scratch_shapes=[pltpu.VMEM((SUBLANES, CW), F32)],
        compiler_params=_cparams(56))(dx1, y2, ypre, ltot, head_ones, q, bcu, bcu, qx16, kv16, cw8, g_post, g_attn,
                                      g_conv, g_x, wout16)


def _attn_bwd(qdo, kvp, ld, chip_sums=()):
    n_in = 3
    views = [[a] + [a.reshape(S // n, n, AW) for _, n, _, _ in ATTN_PLANS[1:]] for a in (qdo, kvp, ld)]
    flat = [views[a][p] for p in range(3) for a in range(n_in)]
    ns = len(chip_sums)
    n_grid = AW // LANES

    def body(*refs):
        hbm = [refs[n_in * p:n_in * p + n_in] for p in range(3)]
        refs = refs[3 * n_in:]
        sum_refs, refs = refs[:ns], refs[ns:]
        outs = [refs[3 * p:3 * p + 3] for p in range(3)]
        landed_refs, sc = refs[9:9 + ns], refs[9 + ns:]
        bufs = [sc[3 * p:3 * p + 3] for p in range(3)]
        res = [sc[9 + 3 * p:12 + 3 * p] for p in range(3)]
        tab128, tab4, sem_in, sem_out = sc[18:22]
        step = pl.program_id(0)
        if ns:
            start_chips, finish_chips = _chips_steps(sum_refs, landed_refs, *sc[22:])
            pl.when(step == 0)(start_chips)
        now = [_class_gather(hbm[p], bufs[p], sem_in.at[p], _lanes_of(step)) for p in range(3)]
        nxt = [_class_gather(hbm[p], bufs[p], sem_in.at[p], _lanes_of(step + 1)) for p in range(3)]

        @pl.when(step == 0)
        def _():
            for p in range(3):
                _start(now[p])
                for b in bufs[p]:
                    b[0:PAD, :] = jnp.zeros((PAD, LANES), F32)
            _fill_bias(tab128, 128, False)
            _fill_bias(tab4, 64, True)

        def prefetch(p):
            pl.when(step + 1 < n_grid)(lambda: _start(nxt[p]))

        lane = lax.broadcasted_iota(jnp.int32, (1, LANES), 1)

        def run(plan, plan_bufs, tab, dst):
            _, n_cls, qblk, nbc = plan
            partner = n_cls == 8
            bqdo, bkv, bld = plan_bufs
            rq, rk, rv = dst

            def block(g, carry):
                own, wins, mask = _block_rows(g, qblk, nbc, partner)
                qb, dob = _unpack_pair(bqdo[own, :])
                q2, do2 = _stack_heads(qb, lane), _stack_heads(dob, lane)
                kw, vw = _unpack_pair(_window(bkv, wins))
                ldv = bld[own, :]
                half = HEAD // 2
                lt2 = jnp.concatenate([ldv[:, 0:1], ldv[:, HEAD:HEAD + 1]], axis=0)
                dsum2 = jnp.concatenate([ldv[:, half:half + 1], ldv[:, HEAD + half:HEAD + half + 1]], axis=0)
                p = jnp.exp(_dot_nt(q2, kw) + tab[mask] - lt2)
                ds = (p * (_dot_nt(do2, vw) - dsum2)).astype(BF16)
                rq[own, :] = _unstack_heads(_dot(ds, kw), lane)
                dkw = _dot_tn(ds, q2)
                dvw = _dot_tn(p.astype(BF16), do2)
                n_w = WIN // len(wins)
                for i, w in enumerate(wins):
                    rk[w, :] += dkw[i * n_w:(i + 1) * n_w, :]
                    rv[w, :] += dvw[i * n_w:(i + 1) * n_w, :]
                return carry
            lax.fori_loop(0, n_cls * nbc, block, 0, unroll=ATTN_UNROLL)

        tabs = (tab128, tab4, tab128)
        def drained(p):
            return lambda: _wait(_whole_waits(res[p], sem_out.at[p]))

        for p in range(3):
            pl.when(step > 0)(drained(p))
            for b in res[p][1:]:
                b[...] = jnp.zeros_like(b)
            _wait(_whole_waits(bufs[p], sem_in.at[p]))
            run(ATTN_PLANS[p], bufs[p], tabs[p], res[p])
            prefetch(p)
            _start(_class_scatter(res[p], outs[p], sem_out.at[p], _lanes_of(step)))
        for p in range(3):
            pl.when(step == n_grid - 1)(drained(p))
        if ns:
            pl.when(step == n_grid - 1)(finish_chips)

    padded = pltpu.VMEM((PAD + S, LANES), F32)
    shapes = [jax.ShapeDtypeStruct(views[0][p].shape, F32) for p in range(3) for _ in range(3)]
    out = pl.pallas_call(
        body, grid=(n_grid,), name="attn_bwd",
        in_specs=[ANY] * (3 * n_in + ns), out_specs=[ANY] * (9 + ns),
        out_shape=shapes + _chips_shapes(chip_sums),
        scratch_shapes=[padded] * 18
        + [pltpu.VMEM((4, 256, WIN), F32), pltpu.VMEM((4, 128, WIN), F32),
           pltpu.SemaphoreType.DMA((3, n_in)), pltpu.SemaphoreType.DMA((3, 3))]
        + (_chips_scratch(ns) if ns else []),
        compiler_params=_cparams(56))(*flat, *chip_sums)
    return [o.reshape(S, AW) for o in out[:9]] + list(out[9:])


def _in_proj_bwd(dqkv, dbcu, dqx, cos, sins, w16, x, g, dx1):
    tq = TQ // 2

    def body(*refs):
        parts = refs[0:9]
        dbcu_ref, dqx_ref, c_ref, s_ref, w_ref, x_ref, g_ref, dx1_ref, dp_ref, gx_ref, dg_ref = refs[9:]

        @pl.when(pl.program_id(0) == 0)
        def _():
            dg_ref[...] = jnp.zeros_like(dg_ref)

        dq, dk, dv = (parts[i][...] + parts[3 + i][...] + parts[6 + i][...] for i in range(3))
        cos, sn = _all_heads(c_ref[...]), _all_heads(s_ref[...])
        dqr = dq * SCALE
        dkr = dk
        dp = jnp.concatenate([(dqr * cos + _rot_half(dqr * sn)).astype(BF16),
                              (dkr * cos + _rot_half(dkr * sn)).astype(BF16), dv.astype(BF16),
                              dbcu_ref[...], dqx_ref[...]], axis=1)
        dp_ref[...] = dp
        dh = _dot_nt(dp, w_ref[...])
        g = g_ref[...]
        _, n, r = _rms(x_ref[...], g)
        dx, dg = _rms_bwd(dh, n, r, g)
        gx_ref[...] = dx1_ref[...] + dx
        dg_ref[0:1, :] += dg

    def tile(w):
        return pl.BlockSpec((tq, w), lambda i: (i, 0))

    return pl.pallas_call(
        body, grid=(S // tq,), name="in_proj_bwd",
        in_specs=[tile(AW)] * 9 + [tile(3 * CW), tile(XW), tile(LANES), tile(LANES), _const((D, PW)),
                                   tile(D), _const((1, D)), tile(D)],
        out_specs=[tile(PW), tile(D), _acc((SUBLANES, D))],
        out_shape=[jax.ShapeDtypeStruct((S, PW), BF16), jax.ShapeDtypeStruct((S, D), F32),
                   jax.ShapeDtypeStruct((SUBLANES, D), F32)],
        compiler_params=_cparams(56))(*dqkv, dbcu, dqx, cos, sins, w16, x, g, dx1)


def _mem_bwd(mem, g_mem, wkv16, dkv):
    def body(m_ref, g_ref, w_ref, dkv_ref, dkv16_ref, dg_ref):
        dkv16 = dkv_ref[...].astype(BF16)
        dkv16_ref[...] = dkv16
        _, n, _ = _rms(m_ref[...], g_ref[...])
        dg = jnp.sum(_dot_nt(dkv16, w_ref[...]) * n, axis=0, keepdims=True)
        dg_ref[...] = jnp.broadcast_to(dg, dg_ref.shape)

    return pl.pallas_call(
        body, name="mem_bwd",
        out_shape=[jax.ShapeDtypeStruct((N_MEM, 2 * XW), BF16), jax.ShapeDtypeStruct((SUBLANES, D), F32)],
        compiler_params=pltpu.CompilerParams(vmem_limit_bytes=32 << 20))(mem, g_mem, wkv16, dkv)


N_CHIPS = N_DEV // 2


def _pair_scratch(block):
    return [pltpu.VMEM((N_CHIPS,) + block, BF16), pltpu.VMEM((N_CHIPS,) + block, BF16),
            pltpu.SemaphoreType.DMA((N_CHIPS,)), pltpu.SemaphoreType.DMA((N_CHIPS,))]


def _swap_with_sibling(p, stage, land, send, recv):
    x, y, c = lax.axis_index("x"), lax.axis_index("y"), lax.axis_index("c")
    return pltpu.make_async_remote_copy(src_ref=stage.at[p], dst_ref=land.at[p], send_sem=send.at[p],
                                        recv_sem=recv.at[p], device_id=(x, y, 1 - c), device_id_type=MESH)


def _sibling_barrier():
    x, y, c = lax.axis_index("x"), lax.axis_index("y"), lax.axis_index("c")
    sem = pltpu.get_barrier_semaphore()
    return (lambda: pl.semaphore_signal(sem, inc=1, device_id=(x, y, 1 - c), device_id_type=MESH),
            lambda: pl.semaphore_wait(sem, 1))


ID_WGRAD_UP, ID_WGRAD_DOWN, ID_WGRAD_ROWS = 0, 1, 2


def _wgrad_cols(place, at16, b16, blk, name, square_b=False, transpose_out=False, to_chips=False, small=(),
                sibling_only_id=None):
    m, kk = at16.shape
    assert sibling_only_id is None or not (to_chips or small)
    aligned = blk % LANES == 0
    wide = blk if aligned else -(-(blk + LANES // 2) // LANES) * LANES
    assert aligned or (transpose_out and blk % SUBLANES == 0)
    block = (blk, m) if transpose_out else (m, blk)

    def chip_of(step, my_chip):
        return jnp.bitwise_xor(my_chip, N_CHIPS - 1 - step) if to_chips else step

    def body(pl_ref, a_ref, *refs):
        b_refs, refs = refs[:2 if aligned else 1], refs[2 if aligned else 1:]
        accs, refs = refs[:len(small)], refs[len(small):]
        (cs_ref, own_ref), refs = refs[:2], refs[2:]
        if to_chips:
            landed, refs = refs[0], refs[1:]
        if small:
            tot_ref, refs = refs[0], refs[1:]
        (stage, land, send, recv), refs = refs[:4], refs[4:]
        if not aligned:
            (win, wsem), refs = refs[:2], refs[2:]
        if small:
            start_small, finish_small = _small_reduce_steps(accs, tot_ref, *refs[-4:])
            refs = refs[:-4]
        step = pl.program_id(0)
        if small:
            pl.when(step == 0)(start_small)
        if sibling_only_id is not None:
            signal_sibling, sibling_is_in = _sibling_barrier()
            pl.when(step == 0)(signal_sibling)
        x, y, c = lax.axis_index("x"), lax.axis_index("y"), lax.axis_index("c")
        my_chip = 2 * x + y
        p = chip_of(step, my_chip)

        def fetch(at_step, mine):
            j = 2 * chip_of(at_step, my_chip) + (c if mine else 1 - c)
            first = pl.multiple_of(((j * blk) >> 7) << 7, LANES)
            slot = 2 * (at_step & 1) + mine
            return pltpu.make_async_copy(b_refs[0].at[:, pl.ds(first, wide)], win.at[slot], wsem.at[slot])

        if not aligned:
            @pl.when(step == 0)
            def _():
                fetch(0, 0).start()
                fetch(0, 1).start()

            @pl.when(step + 1 < N_CHIPS)
            def _():
                fetch(step + 1, 0).start()
                fetch(step + 1, 1).start()

        def partial(mine):
            if aligned:
                b = b_refs[mine][...]
                if square_b:
                    b = b * b
                acc = _dot(a_ref[...], b)
            else:
                fetch(step, mine).wait()
                acc = _dot(a_ref[...], win[2 * (step & 1) + mine]).T
                odd = c if mine else 1 - c
                return jnp.where(odd == 0, acc[0:blk], acc[wide - blk:wide])
            return acc.T if transpose_out else acc

        stage[p] = partial(0).astype(BF16)
        if sibling_only_id is not None:
            pl.when(step == 0)(sibling_is_in)
        swap = _swap_with_sibling(p, stage, land, send, recv)
        swap.start()
        mine = partial(1)
        swap.wait()
        total = mine + land[p].astype(F32)
        cs_ref[0] = total.astype(BF16)

        @pl.when(p == my_chip)
        def _():
            own_ref[...] = total

        if to_chips:
            stage2, send2, recv2 = refs
            flipped = jnp.bitwise_xor(p, my_chip)
            k = jnp.where(flipped == 2, 0, jnp.where(flipped == 1, 1, 2))

            def to_owner(src, k_, px, py):
                return pltpu.make_async_remote_copy(src_ref=src, dst_ref=landed.at[k_], send_sem=send2.at[k_],
                                                    recv_sem=recv2.at[k_], device_id=(px, py, c), device_id_type=MESH)

            @pl.when(p != my_chip)
            def _():
                stage2[p] = total.astype(BF16)
                to_owner(stage2.at[p], k, p >> 1, p & 1).start()

            @pl.when(step == N_CHIPS - 1)
            def _():
                for k_ in range(N_CHIPS - 1):
                    to_owner(stage2.at[0], k_, x, y).wait()

        if small:
            pl.when(step == N_CHIPS - 1)(finish_small)

    def b_spec(mine):
        return pl.BlockSpec((kk, blk), lambda i, s: (0, 2 * chip_of(i, s[1]) + (s[0] if mine else 1 - s[0])))

    b_specs, b_args = ([b_spec(0), b_spec(1)], (b16, b16)) if aligned else ([ANY], (b16,))
    scratch = _pair_scratch(block)
    if not aligned:
        scratch += [pltpu.VMEM((4, kk, wide), BF16), pltpu.SemaphoreType.DMA((4,))]
    out_specs = [pl.BlockSpec((1,) + block, lambda i, s: (chip_of(i, s[1]), 0, 0)), pl.BlockSpec(block, lambda i, s: (0, 0))]
    out_shape = [jax.ShapeDtypeStruct((N_CHIPS,) + block, BF16), jax.ShapeDtypeStruct(block, F32)]
    if to_chips:
        out_specs.append(ANY)
        out_shape.append(jax.ShapeDtypeStruct((N_CHIPS - 1,) + block, BF16))
        scratch += [pltpu.VMEM((N_CHIPS,) + block, BF16), pltpu.SemaphoreType.DMA((N_CHIPS - 1,)),
                    pltpu.SemaphoreType.DMA((N_CHIPS - 1,))]
    small_specs = [pl.BlockSpec(a.shape, lambda i, s: (0, 0)) for a in small]
    if small:
        out_specs.append(pl.BlockSpec((PACK_ROWS, D), lambda i, s: (0, 0)))
        out_shape.append(jax.ShapeDtypeStruct((PACK_ROWS, D), F32))
        scratch += _small_reduce_scratch()
    return pl.pallas_call(
        body, name=name,
        grid_spec=pltpu.PrefetchScalarGridSpec(
            num_scalar_prefetch=1, grid=(N_CHIPS,),
            in_specs=[pl.BlockSpec((m, kk), lambda i, s: (0, 0), pipeline_mode=pl.Buffered(1))] + b_specs + small_specs,
            out_specs=out_specs, scratch_shapes=scratch),
        out_shape=out_shape,
        compiler_params=_cparams(56, **({} if sibling_only_id is None else {"collective_id": sibling_only_id})),
    )(place, at16, *b_args, *small)


ROWS_STEPS = 4


def _wgrad_rows(place, products, name):
    n_prod = len(products)
    dims = [(at16.shape[0], at16.shape[1], b16.shape[1]) for at16, b16 in products]
    cut = [kk % (ROWS_STEPS * LANES) == 0 for _, kk, _ in dims]
    blocks = [(m // N_DEV, n) for m, _, n in dims]

    def body(pl_ref, *refs):
        ins, outs, scratch = refs[:2 * n_prod], refs[2 * n_prod:4 * n_prod], refs[4 * n_prod:]
        c, step = pl_ref[0], pl.program_id(0)

        def multiply(i):
            a_ref, b_ref, acc = ins[2 * i], ins[2 * i + 1], scratch[5 * i]

            @pl.when(step == 0)
            def _():
                acc[...] = _dot(a_ref[...], b_ref[...])

            if cut[i]:
                @pl.when(step > 0)
                def _():
                    acc[...] += _dot(a_ref[...], b_ref[...])

        def rows(i, owner):
            return pl.ds(pl.multiple_of(owner * blocks[i][0], blocks[i][0]), blocks[i][0])

        def send_sibling_side(i):
            acc, stage, land, send, recv = scratch[5 * i:5 * i + 5]
            swaps = []
            for p in range(N_CHIPS):
                stage[p] = acc[rows(i, 2 * p + 1 - c), :].astype(BF16)
                swaps.append(_swap_with_sibling(p, stage, land, send, recv))
                swaps[-1].start()
            return swaps

        def add_my_side(i, swaps):
            acc, land = scratch[5 * i], scratch[5 * i + 2]
            cs_ref, own_ref = outs[2 * i:2 * i + 2]
            for p in range(N_CHIPS):
                swaps[p].wait()
                total = acc[rows(i, 2 * p + c), :] + land[p].astype(F32)
                cs_ref[p] = total.astype(BF16)

                @pl.when(p == pl_ref[1])
                def _():
                    own_ref[...] = total

        signal_sibling, sibling_is_in = _sibling_barrier()
        pl.when(step == 0)(signal_sibling)
        for i in range(n_prod):
            multiply(i)

        @pl.when(step == ROWS_STEPS - 1)
        def _():
            sibling_is_in()
            swaps = [send_sibling_side(i) for i in range(n_prod)]
            for i in range(n_prod):
                add_my_side(i, swaps[i])

    in_specs, out_specs, out_shape, scratch = [pl.BlockSpec(memory_space=pltpu.SMEM)], [], [], []
    for (m, kk, n), cut_i, block in zip(dims, cut, blocks):
        chunk = kk // ROWS_STEPS
        in_specs += ([pl.BlockSpec((m, chunk), lambda i: (0, i)), pl.BlockSpec((chunk, n), lambda i: (i, 0))]
                     if cut_i else [_const((m, kk)), _const((kk, n))])
        out_specs += [_acc((N_CHIPS,) + block), _acc(block)]
        out_shape += [jax.ShapeDtypeStruct((N_CHIPS,) + block, BF16), jax.ShapeDtypeStruct(block, F32)]
        scratch += [pltpu.VMEM((m, n), F32)] + _pair_scratch(block)
    out = pl.pallas_call(
        body, grid=(ROWS_STEPS,), name=name, in_specs=in_specs, out_specs=out_specs, out_shape=out_shape,
        scratch_shapes=scratch, compiler_params=_cparams(56, collective_id=ID_WGRAD_ROWS),
    )(place, *[a for pair in products for a in pair])
    return [tuple(out[2 * i:2 * i + 2]) for i in range(n_prod)]


def _adamw_math(w, g, m, v):
    m = ADAM_B1 * m + (1.0 - ADAM_B1) * g
    v = ADAM_B2 * v + (1.0 - ADAM_B2) * jnp.square(g)
    m_hat = m / (1.0 - ADAM_B1 ** ADAM_STEP)
    v_hat = v / (1.0 - ADAM_B2 ** ADAM_STEP)
    delta = -ADAM_LR * (m_hat / (jnp.sqrt(v_hat) + ADAM_EPS) + ADAM_WD * w)
    return delta, m, v


def _adamw_shards(updates, name, chip_sums=()):
    names, nu, ns = list(updates), len(updates), len(chip_sums)

    def body(*refs):
        ins, sum_refs = refs[:5 * nu], refs[5 * nu:5 * nu + ns]
        outs = refs[5 * nu + ns:9 * nu + ns]
        landed_refs, scratch = refs[9 * nu + ns:9 * nu + 2 * ns], refs[9 * nu + 2 * ns:]
        if ns:
            start_chips, finish_chips = _chips_steps(sum_refs, landed_refs, *scratch)
            start_chips()
        for i in range(nu):
            o_ref, r_ref, w_ref, m_ref, v_ref = ins[5 * i:5 * i + 5]
            g_out, d_out, m_out, v_out = outs[4 * i:4 * i + 4]
            g = o_ref[...] + r_ref[0].astype(F32) + r_ref[1].astype(F32) + r_ref[2].astype(F32)
            g_out[...] = g
            d_out[...], m_out[...], v_out[...] = _adamw_math(w_ref[...], g, m_ref[...], v_ref[...])
        if ns:
            finish_chips()

    vmem = pl.BlockSpec(memory_space=pltpu.VMEM)
    out = pl.pallas_call(
        body, name=name,
        in_specs=[vmem] * (5 * nu) + [ANY] * ns, out_specs=[vmem] * (4 * nu) + [ANY] * ns,
        out_shape=[jax.ShapeDtypeStruct(updates[n][2].shape, F32) for n in names for _ in range(4)]
        + _chips_shapes(chip_sums),
        scratch_shapes=_chips_scratch(ns) if ns else [],
        compiler_params=pltpu.CompilerParams(vmem_limit_bytes=56 << 20),
    )(*[a for n in names for a in updates[n]], *chip_sums)
    return {n: out[4 * i:4 * i + 4] for i, n in enumerate(names)}, list(out[4 * nu:])


def _place():
    x, y, c = lax.axis_index("x"), lax.axis_index("y"), lax.axis_index("c")
    chips = [(1 - x, y), (x, 1 - y), (1 - x, 1 - y)]
    return x, y, c, chips


def _gather_steps(ins, outs, send, recv, lsem):
    nt = len(ins)
    x, y, c, (xn, yn, diag) = _place()
    me, sib = (x, y, c), (x, y, 1 - c)

    def slot(t, px, py, pc):
        return outs[t].at[4 * px + 2 * py + pc]

    def copy(t, k, block, to, src=None):
        return pltpu.make_async_remote_copy(
            src_ref=slot(t, *block) if src is None else src, dst_ref=slot(t, *block),
            send_sem=send.at[t, k], recv_sem=recv.at[t, k], device_id=to, device_id_type=MESH)

    mine = [pltpu.make_async_copy(ins[t], slot(t, *me), lsem.at[t]) for t in range(nt)]
    first = [copy(t, k, me, to, src=ins[t]) for t in range(nt) for k, to in ((0, sib), (1, (*xn, c)), (2, (*yn, c)))]

    def start():
        for cp in mine + first:
            cp.start()

    def landed(k, chip, also_to=None):
        for t in range(nt):
            copy(t, k, (*chip, c), me).wait_recv()
            if also_to is not None:
                copy(t, 3, (*chip, c), (*also_to, c)).start()
            copy(t, 3 + k, (*chip, c), sib).start()

    def relay():
        @pl.when(c == 0)
        def _():
            landed(1, xn, also_to=yn)
            landed(2, yn)

        @pl.when(c == 1)
        def _():
            landed(2, yn, also_to=xn)
            landed(1, xn)

    def finish():
        landed(3, diag)
        for t in range(nt):
            copy(t, 0, sib, me).wait_recv()
            for k, chip in ((4, xn), (5, yn), (6, diag)):
                copy(t, k, (*chip, 1 - c), me).wait_recv()
            for k in range(7):
                copy(t, k, me, sib).wait_send()
        for cp in mine:
            cp.wait()

    return start, relay, finish


def _gather_scratch(nt):
    return [pltpu.SemaphoreType.DMA((nt, 7)), pltpu.SemaphoreType.DMA((nt, 7)), pltpu.SemaphoreType.DMA((nt,))]


def _gathered_shapes(shards):
    return [jax.ShapeDtypeStruct((N_DEV,) + s.shape, s.dtype) for s in shards]


def _call_with_gather(body, n_grid, shards, *, name, in_specs, out_specs, out_shape, scratch_shapes, vmem_mb, args):
    ng, n_in, n_out = len(shards), len(in_specs), len(out_specs)

    def wrapped(*refs):
        ins, shard_refs = refs[:n_in], refs[n_in:n_in + ng]
        outs = refs[n_in + ng:n_in + ng + n_out]
        whole_refs = refs[n_in + ng + n_out:n_in + 2 * ng + n_out]
        scratch = refs[n_in + 2 * ng + n_out:]
        if ng:
            start, relay, finish = _gather_steps(shard_refs, whole_refs, *scratch[len(scratch_shapes):])
            pl.when(pl.program_id(0) == 0)(start)
            pl.when(pl.program_id(0) == n_grid // 2)(relay)
        body(*ins, *outs, *scratch[:len(scratch_shapes)])
        if ng:
            pl.when(pl.program_id(0) == n_grid - 1)(finish)

    return pl.pallas_call(
        wrapped, grid=(n_grid,), name=name,
        in_specs=list(in_specs) + [ANY] * ng, out_specs=list(out_specs) + [ANY] * ng,
        out_shape=list(out_shape) + _gathered_shapes(shards),
        scratch_shapes=list(scratch_shapes) + (_gather_scratch(ng) if ng else []),
        compiler_params=_cparams(vmem_mb))(*args, *shards)


def _chips_steps(ins, outs, send, recv):
    _, _, c, chips = _place()
    copies = [pltpu.make_async_remote_copy(
        src_ref=ins[t].at[2 * px + py], dst_ref=outs[t].at[j], send_sem=send.at[t, j], recv_sem=recv.at[t, j],
        device_id=(px, py, c), device_id_type=MESH) for t in range(len(ins)) for j, (px, py) in enumerate(chips)]

    def start():
        for cp in copies:
            cp.start()

    def finish():
        for cp in copies:
            cp.wait()

    return start, finish


def _chips_scratch(nt):
    return [pltpu.SemaphoreType.DMA((nt, 3)), pltpu.SemaphoreType.DMA((nt, 3))]


def _chips_shapes(cs16s):
    return [jax.ShapeDtypeStruct((3,) + g.shape[1:], g.dtype) for g in cs16s]


SMALL = (("g_pre_mix", 0, 0, D), ("g_mem", 1, 0, D), ("g_post_mix", 2, 0, D), ("g_attn_out", 3, 0, AW),
         ("g_conv_out", 3, AW, CW), ("g_xattn_out", 3, AW + CW, XW), ("g_post_mlp", 4, 0, D), ("g_pre_mlp", 5, 0, D))
CONV_ROW = 8
PACK_ROWS = 16


LOSS_ROW = 15


def _small_reduce_steps(accs, tot_ref, pack, land, send, recv):
    acc_in, acc_mem, acc_mix, acc_mlp, acc_cw, acc_loss = accs
    x, y, c, _ = _place()
    me = 4 * x + 2 * y + c
    copies = []
    for k in range(1, N_DEV):
        kx, ky, kc = (k >> 2) & 1, (k >> 1) & 1, k & 1
        peer = (1 - x if kx else x, 1 - y if ky else y, 1 - c if kc else c)
        copies.append(pltpu.make_async_remote_copy(
            src_ref=pack, dst_ref=land.at[me], send_sem=send.at[k - 1], recv_sem=recv.at[k - 1],
            device_id=peer, device_id_type=MESH))

    def start():
        pack[...] = jnp.zeros_like(pack)
        pack[0:1, :] = acc_in[0:1, :]
        pack[1:2, :] = acc_mem[0:1, :]
        pack[2:4, :] = acc_mix[0:2, :]
        pack[4:6, :] = acc_mlp[0:2, :]
        pack[CONV_ROW:CONV_ROW + 3, 0:CW] = acc_cw[0:3, :]
        pack[LOSS_ROW:LOSS_ROW + 1, 0:LANES] = acc_loss[0:1, :]
        land[me] = pack[...]
        for cp in copies:
            cp.start()

    def finish():
        for cp in copies:
            cp.wait()
        tot = land[0]
        for s in range(1, N_DEV):
            tot = tot + land[s]
        tot_ref[...] = tot

    return start, finish


def _small_reduce_scratch():
    return [pltpu.VMEM((PACK_ROWS, D), F32), pltpu.VMEM((N_DEV, PACK_ROWS, D), F32),
            pltpu.SemaphoreType.DMA((N_DEV - 1,)), pltpu.SemaphoreType.DMA((N_DEV - 1,))]


def _small_update(tot, me, params):
    flat = [a for n, _, _, _ in SMALL for a in params[n]] + list(params["conv_w"])
    n_par = len(SMALL) + 1
    tap_cols = CW // N_DEV

    def body(*refs):
        me_ref, tot_ref = refs[0:2]
        ins = refs[2:2 + 3 * n_par]
        loss_out = refs[2 + 3 * n_par]
        outs = refs[3 + 3 * n_par:]
        tot = tot_ref[...]
        loss_out[...] = jnp.broadcast_to(tot[LOSS_ROW:LOSS_ROW + 1, 0:LANES], loss_out.shape)

        def update(i, g):
            w_ref, m_ref, v_ref = ins[3 * i:3 * i + 3]
            for o_ref, res in zip(outs[4 * i:4 * i + 4], (g,) + _adamw_math(w_ref[...], g, m_ref[...], v_ref[...])):
                if len(o_ref.shape) == 3:
                    for t in range(o_ref.shape[0]):
                        o_ref[t] = res[t:t + 1, :]
                else:
                    o_ref[...] = res

        for i, (_, row, lane0, width) in enumerate(SMALL):
            update(i, tot[row:row + 1, lane0:lane0 + width])
        me = me_ref[0]
        taps = pltpu.roll(tot[CONV_ROW:CONV_ROW + SUBLANES, 0:CW], jnp.where(me == 0, 0, CW - me * tap_cols), 1)
        update(n_par - 1, taps[0:3, 0:tap_cols])

    shapes = [jax.ShapeDtypeStruct(params[n][0].shape, F32) for n, _, _, _ in SMALL] + [
        jax.ShapeDtypeStruct((3, 1, tap_cols), F32)]
    vmem = pl.BlockSpec(memory_space=pltpu.VMEM)
    loss, *out = pl.pallas_call(
        body, name="small_update",
        in_specs=[pl.BlockSpec(memory_space=pltpu.SMEM)] + [vmem] * (1 + 3 * n_par),
        out_shape=[jax.ShapeDtypeStruct((SUBLANES, LANES), F32)] + [s for s in shapes for _ in range(4)],
    )(me, tot, *flat)
    names = [n for n, _, _, _ in SMALL] + ["conv_w"]
    return loss[0, 0], {n: out[4 * i:4 * i + 4] for i, n in enumerate(names)}


def _local_step(x, mem, pos, gains, shards, tgt, place):
    half = HEAD // 2
    inv_freq = jnp.float32(ROPE_THETA) ** (-(jnp.arange(half, dtype=F32) * 2.0 / HEAD))
    invf = jnp.tile(inv_freq, LANES // half)[None, :]
    sgn = jnp.tile(jnp.concatenate([-jnp.ones((half,), F32), jnp.ones((half,), F32)]), LANES // HEAD)[None, :]
    cos, sins, win8 = _rope_table(pos.astype(F32).reshape(S, 1), invf, sgn, [shards["w_in"]])
    wdn_left, wdn_right = shards["w_down"][:, 0:D // 2], shards["w_down"][:, D // 2:]
    q, kvp, bcu, qx16, h16, win16, wout8, wkv8, conv8, wdn8_right = _in_proj(
        x, gains["g_pre_mix"], win8, cos, sins, [shards["w_out"], shards["w_mem_kv"], shards["conv_w"], wdn_right])
    wout16, wkv16 = wout8.reshape(D, D), wkv8.reshape(D, 2 * XW)
    cw_full = conv8[:, 0:3, 0:CW // N_DEV].transpose(1, 0, 2).reshape(3, CW)
    cw8 = jnp.zeros((SUBLANES, CW), F32).at[0:3].set(cw_full)
    y_attn, ltot, wup8, wdn8_left = _attn_fwd(q, kvp, [shards["w_up"], wdn_left])
    wdn_halves = (wdn8_left.reshape(FF, D // 2), wdn8_right.reshape(FF, D // 2))
    memn16, kv16 = _mem_fwd(mem, gains["g_mem"], wkv16)
    ypre, y16, y2, x1 = _mix_out(y_attn, bcu, qx16, kv16, cw8, gains["g_attn_out"], gains["g_conv_out"],
                                 gains["g_xattn_out"], gains["g_post_mix"], wout16, x, [])
    a16, du16, h2_16, df2_16, dx1, loss8, dg_mlp = _mlp(
        x1, tgt, gains["g_pre_mlp"], gains["g_post_mlp"], wup8, wdn_halves)

    sums = {"w_up": _wgrad_cols(place, h2_16, du16, FF_BLK, "wgrad_up", sibling_only_id=ID_WGRAD_UP),
            "w_down": _wgrad_cols(place, df2_16, a16, FF_BLK, "wgrad_down", square_b=True, transpose_out=True,
                                  sibling_only_id=ID_WGRAD_DOWN)}

    head_id = jnp.arange(AW, dtype=jnp.int32) // HEAD
    head_ones = (head_id[:, None] == head_id[None, :]).astype(BF16)
    dy2_16, qdo, ld, dbcu, dqx, dgs, dcw, dkv = _mix_out_bwd(
        dx1, y2, ypre, ltot, head_ones, q, bcu, qx16, kv16, cw8, gains["g_post_mix"], gains["g_attn_out"],
        gains["g_conv_out"], gains["g_xattn_out"], wout16)
    dkv16, dg_mem = _mem_bwd(mem, gains["g_mem"], wkv16, dkv)
    sums["w_mem_kv"], sums["w_out"] = _wgrad_rows(place, [(memn16, dkv16), (y16, dy2_16)], "wgrad_mem_kv_out")
    out = _attn_bwd(qdo, kvp, ld, [s[0] for s in sums.values()])
    dqkv, landed = out[:9], out[9:]
    reduced = {n: (s[1], landed[t]) for t, (n, s) in enumerate(sums.items())}
    dproj16, grad_x, dg_in = _in_proj_bwd(dqkv, dbcu, dqx, cos, sins, win16, x, gains["g_pre_mix"], dx1)

    _, in_own, in_landed, small_tot = _wgrad_cols(place, h16, dproj16, PW // N_DEV, "wgrad_in", transpose_out=True,
                                                  to_chips=True, small=(dg_in, dg_mem, dgs, dg_mlp, dcw, loss8))
    reduced["w_in"] = (in_own, in_landed)
    return grad_x, reduced, small_tot


BIG = ("w_in", "w_mem_kv", "w_out", "w_up", "w_down")
ORDER = ("g_pre_mix", "g_mem", "w_in", "w_mem_kv", "conv_w", "g_attn_out", "g_conv_out", "g_xattn_out", "w_out",
         "g_post_mix", "g_pre_mlp", "w_up", "w_down", "g_post_mlp")


def kernel(x, mem, positions, g_pre_mix, g_mem, w_in, w_mem_kv, conv_w, g_attn_out, g_conv_out, g_xattn_out, w_out, g_post_mix, g_pre_mlp, w_up, w_down, g_post_mlp, loss_target, m_g_pre_mix, m_g_mem, m_w_in, m_w_mem_kv, m_conv_w, m_g_attn_out, m_g_conv_out, m_g_xattn_out, m_w_out, m_g_post_mix, m_g_pre_mlp, m_w_up, m_w_down, m_g_post_mlp, v_g_pre_mix, v_g_mem, v_w_in, v_w_mem_kv, v_conv_w, v_g_attn_out, v_g_conv_out, v_g_xattn_out, v_w_out, v_g_post_mix, v_g_pre_mlp, v_w_up, v_w_down, v_g_post_mlp):
    w = dict(g_pre_mix=g_pre_mix, g_mem=g_mem, w_in=w_in, w_mem_kv=w_mem_kv, conv_w=conv_w, g_attn_out=g_attn_out,
             g_conv_out=g_conv_out, g_xattn_out=g_xattn_out, w_out=w_out, g_post_mix=g_post_mix, g_pre_mlp=g_pre_mlp,
             w_up=w_up, w_down=w_down, g_post_mlp=g_post_mlp)
    mo = dict(g_pre_mix=m_g_pre_mix, g_mem=m_g_mem, w_in=m_w_in, w_mem_kv=m_w_mem_kv, conv_w=m_conv_w,
              g_attn_out=m_g_attn_out, g_conv_out=m_g_conv_out, g_xattn_out=m_g_xattn_out, w_out=m_w_out,
              g_post_mix=m_g_post_mix, g_pre_mlp=m_g_pre_mlp, w_up=m_w_up, w_down=m_w_down, g_post_mlp=m_g_post_mlp)
    vo = dict(g_pre_mix=v_g_pre_mix, g_mem=v_g_mem, w_in=v_w_in, w_mem_kv=v_w_mem_kv, conv_w=v_conv_w,
              g_attn_out=v_g_attn_out, g_conv_out=v_g_conv_out, g_xattn_out=v_g_xattn_out, w_out=v_w_out,
              g_post_mix=v_g_post_mix, g_pre_mlp=v_g_pre_mlp, w_up=v_w_up, w_down=v_w_down, g_post_mlp=v_g_post_mlp)

    xi, yi, ci = lax.axis_index("x"), lax.axis_index("y"), lax.axis_index("c")
    me = 4 * xi + 2 * yi + ci
    place = jnp.stack([ci, 2 * xi + yi]).astype(jnp.int32)

    shards = {n: w[n][0].astype(BF16) for n in BIG}
    shards["conv_w"] = jnp.zeros((SUBLANES, LANES), F32).at[0:3, 0:CW // N_DEV].set(conv_w[0])

    gains = {n: w[n] for n, _, _, _ in SMALL}
    grad_x, reduced, small_tot = _local_step(x[0], mem[0], positions[0], gains, shards, loss_target[0], place)

    def shard(n, a):
        return a[0].T if n == "w_in" else a[0]

    updated = {}
    for group in (("w_up", "w_down"), ("w_in", "w_out", "w_mem_kv")):
        updated.update(_adamw_shards({n: (*reduced[n], shard(n, w[n]), shard(n, mo[n]), shard(n, vo[n]))
                                      for n in group}, "adamw_" + "_".join(group))[0])
    grad, delta, new_m, new_v = {}, {}, {}, {}
    for n, res in updated.items():
        grad[n], delta[n], new_m[n], new_v[n] = [(a.T if n == "w_in" else a)[None] for a in res]

    params = {n: (w[n], mo[n], vo[n]) for n, _, _, _ in SMALL}
    params["conv_w"] = (w["conv_w"][0], mo["conv_w"][0], vo["conv_w"][0])
    loss, small = _small_update(small_tot, me.reshape(1).astype(jnp.int32), params)
    for n, (g, d_, m_, v_) in small.items():
        lead = (lambda a: a.reshape(conv_w.shape)) if n == "conv_w" else (lambda a: a)
        grad[n], delta[n], new_m[n], new_v[n] = lead(g), lead(d_), lead(m_), lead(v_)

    return (loss, grad_x[None], *[grad[n] for n in ORDER], *[delta[n] for n in ORDER],
            *[new_m[n] for n in ORDER], *[new_v[n] for n in ORDER])
```

```python
import jax
import jax.numpy as jnp
from jax import lax
from jax.experimental import pallas as pl
from jax.experimental.pallas import tpu as pltpu

F32, BF16 = jnp.float32, jnp.bfloat16
MESH = pl.DeviceIdType.MESH
ANY = pl.BlockSpec(memory_space=pl.ANY)

N_DEV = 8
D = 1024
S = 4096
N_MEM = 256
HEAD = 64
AW, CW, XW = 512, 256, 256
PW = 3 * AW + 3 * CW + XW
FF = 4096
FF_BLK = FF // N_DEV
EPS = 1e-6
NEG = -1e30
SCALE = HEAD ** -0.5
ROPE_THETA = 10000.0
LANES = 128
SUBLANES = 8

ADAM_LR, ADAM_B1, ADAM_B2, ADAM_EPS, ADAM_WD, ADAM_STEP = 0.001, 0.9, 0.999, 1e-08, 0.01, 10

TQ = 512
TQ_MLP = 512
NT = S // TQ


def _cparams(vmem_mb, n_grid=1, **more):
    return pltpu.CompilerParams(dimension_semantics=("arbitrary",) * n_grid, vmem_limit_bytes=vmem_mb << 20, **more)


def _const(shape):
    nd = len(shape)
    return pl.BlockSpec(shape, lambda *_: (0,) * nd, pipeline_mode=pl.Buffered(1))


def _acc(shape):
    nd = len(shape)
    return pl.BlockSpec(shape, lambda *_: (0,) * nd)


def _tokens_in_lanes(tq):
    return pl.BlockSpec((D, tq), lambda i: (0, i))


def _dot(a, b):
    return jnp.dot(a, b, preferred_element_type=F32)


def _dot_nt(a, b):
    return lax.dot_general(a, b, (((1,), (1,)), ((), ())), preferred_element_type=F32)


def _dot_tn(a, b):
    return lax.dot_general(a, b, (((0,), (0,)), ((), ())), preferred_element_type=F32)


def _rms(x, g):
    r = lax.rsqrt(jnp.mean(x * x, axis=-1, keepdims=True) + EPS)
    n = x * r
    return n * g, n, r


def _rms_bwd(dy, n, r, g):
    dn = dy * g
    dx = r * (dn - n * jnp.mean(dn * n, axis=-1, keepdims=True))
    return dx, jnp.sum(dy * n, axis=0, keepdims=True)


def _rot_half(t):
    lane = lax.broadcasted_iota(jnp.int32, t.shape, 1)
    n = t.shape[1]
    return jnp.where((lane % HEAD) < HEAD // 2, pltpu.roll(t, n - HEAD // 2, 1), pltpu.roll(t, HEAD // 2, 1))


def _rope_table(pos_col, invf, sgn, shards):
    def body(p_ref, f_ref, s_ref, c_out, s_out):
        ang = p_ref[...] * f_ref[...]
        c_out[...] = jnp.cos(ang)
        s_out[...] = jnp.sin(ang) * s_ref[...]

    tile = pl.BlockSpec((TQ, LANES), lambda i: (i, 0))
    return _call_with_gather(
        body, NT, shards, name="rope_table",
        in_specs=[pl.BlockSpec((TQ, 1), lambda i: (i, 0)), _const((1, LANES)), _const((1, LANES))],
        out_specs=[tile, tile], out_shape=[jax.ShapeDtypeStruct((S, LANES), F32)] * 2,
        scratch_shapes=[], vmem_mb=32, args=(pos_col, invf, sgn), collective_id=ID_ROPE_TABLE)


def _all_heads(t):
    return jnp.tile(t, (1, AW // LANES))


def _mem_fwd(mem, g_mem, wkv16):
    def body(m_ref, g_ref, w_ref, n16_ref, kv_ref):
        y, _, _ = _rms(m_ref[...], g_ref[...])
        y16 = y.astype(BF16)
        n16_ref[...] = y16.T
        kv_ref[...] = _dot(y16, w_ref[...]).astype(BF16)

    return pl.pallas_call(
        body, name="mem_fwd",
        out_shape=[jax.ShapeDtypeStruct((D, N_MEM), BF16), jax.ShapeDtypeStruct((N_MEM, 2 * XW), BF16)],
        compiler_params=pltpu.CompilerParams(vmem_limit_bytes=32 << 20))(mem, g_mem, wkv16)


def _in_proj(x, g, w8, cos, sins, shards):
    blk = PW // N_DEV

    def body(x_ref, g_ref, w8_ref, c_ref, s_ref, q_ref, kv_ref, bcu_ref, qx_ref, h_ref, w_out, w_ref):
        @pl.when(pl.program_id(0) == 0)
        def _():
            for j in range(N_DEV):
                w_ref[:, j * blk:(j + 1) * blk] = w8_ref[j]
            w_out[...] = w_ref[...]

        y, _, _ = _rms(x_ref[...], g_ref[...])
        h = y.astype(BF16)
        h_ref[...] = h.T
        proj = _dot(h, w_ref[...])
        cos, sn = _all_heads(c_ref[...]), _all_heads(s_ref[...])
        q, k = proj[:, 0:AW], proj[:, AW:2 * AW]
        q_ref[...] = (q * cos + _rot_half(q) * sn) * SCALE
        kv_ref[...] = _pack_pair(k * cos + _rot_half(k) * sn, proj[:, 2 * AW:3 * AW])
        bcu_ref[...] = proj[:, 3 * AW:3 * AW + 3 * CW]
        qx_ref[...] = (proj[:, 3 * AW + 3 * CW:] * SCALE).astype(BF16)

    def tile(w):
        return pl.BlockSpec((TQ, w), lambda i: (i, 0))

    return _call_with_gather(
        body, NT, shards, name="in_proj",
        in_specs=[tile(D), _const((1, D)), _const((N_DEV, D, blk)), tile(LANES), tile(LANES)],
        out_specs=[tile(AW), tile(AW), tile(3 * CW), tile(XW), _tokens_in_lanes(TQ), _acc((D, PW))],
        out_shape=[jax.ShapeDtypeStruct((S, AW), F32)] * 2 + [
            jax.ShapeDtypeStruct((S, 3 * CW), F32), jax.ShapeDtypeStruct((S, XW), BF16),
            jax.ShapeDtypeStruct((D, S), BF16), jax.ShapeDtypeStruct((D, PW), BF16)],
        scratch_shapes=[pltpu.VMEM((D, PW), BF16)], vmem_mb=56, args=(x, g, w8, cos, sins),
        collective_id=ID_IN_PROJ)


ATTN_PLANS = (("p1", 1, 128, 32), ("p4", 8, 64, 8), ("p16", 16, 128, 2))
PAD = 128
WIN = 256


ATTN_UNROLL = 16


def _fill_bias(tab, qblk, partner):
    qi = lax.broadcasted_iota(jnp.int32, (2 * qblk, WIN), 0) & (qblk - 1)
    kj = lax.broadcasted_iota(jnp.int32, (2 * qblk, WIN), 1)
    piece = kj >> (qblk.bit_length() - 1)
    kk = kj & (qblk - 1)
    prev = (piece & 1) == 0
    of_partner = piece >= 2
    for first in (0, 1):
        for par in (0, 1):
            lo = jnp.where(prev, (qblk if first else qi) + jnp.where(of_partner, par, 0), 0)
            hi = jnp.where(prev, qblk, qi + jnp.where(of_partner, par - 1, 0))
            tab[2 * first + par] = jnp.where((kk >= lo) & (kk <= hi), 0.0, NEG).astype(F32)


def _block_rows(g, qblk, nbc, partner):
    own = pl.ds(pl.multiple_of(PAD + g * qblk, qblk), qblk)
    first = ((g & (nbc - 1)) == 0).astype(jnp.int32)
    if partner:
        gp = jnp.bitwise_xor(g, 4 * nbc)
        wins = (pl.ds(pl.multiple_of(PAD + (g - 1) * qblk, qblk), 2 * qblk),
                pl.ds(pl.multiple_of(PAD + (gp - 1) * qblk, qblk), 2 * qblk))
        return own, wins, 2 * first + ((g >> ((4 * nbc).bit_length() - 1)) & 1)
    return own, (pl.ds(pl.multiple_of(PAD + (g - 1) * qblk, qblk), 2 * qblk),), 2 * first


def _pack_pair(lo, hi):
    lo_bits = lax.bitcast_convert_type(lo.astype(BF16).astype(F32), jnp.uint32) >> 16
    hi_bits = lax.bitcast_convert_type(hi.astype(BF16).astype(F32), jnp.uint32) & jnp.uint32(0xFFFF0000)
    return lax.bitcast_convert_type(hi_bits | lo_bits, F32)


def _unpack_pair(c):
    bits = lax.bitcast_convert_type(c, jnp.uint32)
    lo = lax.bitcast_convert_type(bits << 16, F32).astype(BF16)
    hi = lax.bitcast_convert_type(bits & jnp.uint32(0xFFFF0000), F32).astype(BF16)
    return lo, hi


def _window(ref, wins):
    parts = [ref[w, :] for w in wins]
    return parts[0] if len(parts) == 1 else jnp.concatenate(parts, axis=0)


def _stack_heads(t, lane):
    zero = jnp.zeros_like(t)
    return jnp.concatenate([jnp.where(lane < HEAD, t, zero), jnp.where(lane >= HEAD, t, zero)], axis=0)


def _unstack_heads(t2, lane):
    half = t2.shape[0] // 2
    return jnp.where(lane < HEAD, t2[0:half, :], t2[half:, :])


def _lanes_of(step):
    return pl.ds(pl.multiple_of(step * LANES, LANES), LANES)


def _whole_wait(buf, sem):
    whole = buf.at[pl.ds(PAD, S), :]
    return pltpu.make_async_copy(whole, whole, sem)


def _whole_waits(bufs, sems):
    return [_whole_wait(buf, sems.at[i]) for i, buf in enumerate(bufs)]


def _class_gather(views, bufs, sems, lanes):
    copies = []
    for i, (view, buf) in enumerate(zip(views, bufs)):
        if view.ndim == 2:
            copies.append(pltpu.make_async_copy(view.at[:, lanes], buf.at[pl.ds(PAD, S), :], sems.at[i]))
        else:
            per, n_cls = view.shape[0], view.shape[1]
            copies += [pltpu.make_async_copy(view.at[:, c, lanes], buf.at[pl.ds(PAD + c * per, per), :], sems.at[i])
                       for c in range(n_cls)]
    return copies


def _class_scatter(bufs, dsts, sems, lanes):
    copies = []
    for i, (buf, dst) in enumerate(zip(bufs, dsts)):
        if dst.ndim == 2:
            copies.append(pltpu.make_async_copy(buf.at[pl.ds(PAD, S), :], dst.at[:, lanes], sems.at[i]))
            continue
        per, n_cls = dst.shape[0], dst.shape[1]
        copies += [pltpu.make_async_copy(buf.at[pl.ds(PAD + c * per, per), :], dst.at[:, c, lanes], sems.at[i])
                   for c in range(n_cls)]
    return copies


def _start(copies):
    for cp in copies:
        cp.start()


def _wait(waits):
    for w in waits:
        w.wait()


def _attn_fwd(q, kvp, shards=()):
    views = [[a] + [a.reshape(S // n, n, AW) for _, n, _, _ in ATTN_PLANS[1:]] for a in (q, kvp)]
    flat = [views[a][p] for p in range(3) for a in range(2)]
    ng = len(shards)
    n_grid = AW // LANES

    def body(*refs):
        hbm = [refs[2 * p:2 * p + 2] for p in range(3)]
        refs = refs[6:]
        shard_refs, refs = refs[:ng], refs[ng:]
        y_ref, lt_ref = refs[0:2]
        whole_refs, refs = refs[2:2 + ng], refs[2 + ng:]
        bufs = [refs[2 * p:2 * p + 2] for p in range(3)]
        oc4, lc4, oc16, lc16, tab128, tab4, sem_in = refs[6:13]
        step = pl.program_id(0)
        if ng:
            start_gather, relay_gather, finish_gather = _gather_steps(shard_refs, whole_refs, *refs[13:])
            pl.when(step == 0)(start_gather)
            pl.when(step == n_grid // 2)(relay_gather)
        now = [_class_gather(hbm[p], bufs[p], sem_in.at[p], _lanes_of(step)) for p in range(3)]
        nxt = [_class_gather(hbm[p], bufs[p], sem_in.at[p], _lanes_of(step + 1)) for p in range(3)]

        @pl.when(step == 0)
        def _():
            for p in range(3):
                _start(now[p])
                for b in bufs[p]:
                    b[0:PAD, :] = jnp.zeros((PAD, LANES), F32)
            _fill_bias(tab128, 128, False)
            _fill_bias(tab4, 64, True)

        def prefetch(p):
            pl.when(step + 1 < n_grid)(lambda: _start(nxt[p]))

        lane = lax.broadcasted_iota(jnp.int32, (1, LANES), 1)
        ones = jnp.ones((WIN, LANES), BF16)

        def run(plan, bq, bkv, tab, o_dst, l_dst, dst_pad):
            _, n_cls, qblk, nbc = plan
            partner = n_cls == 8

            def block(g, carry):
                own, wins, mask = _block_rows(g, qblk, nbc, partner)
                q2 = _stack_heads(bq[own, :].astype(BF16), lane)
                kw, vwin = _unpack_pair(_window(bkv, wins))
                vw = jnp.concatenate([vwin, ones], axis=1)
                s = _dot_nt(q2, kw) + tab[mask]
                m = jnp.max(s, axis=1, keepdims=True)
                oe = _dot(jnp.exp(s - m).astype(BF16), vw)
                den = oe[:, LANES:]
                dst = pl.ds(pl.multiple_of(dst_pad + g * qblk, qblk), qblk)
                o_dst[dst, :] = _unstack_heads(oe[:, 0:LANES] / den, lane)
                l_dst[dst, :] = _unstack_heads(m + jnp.log(den), lane)
                return carry
            lax.fori_loop(0, n_cls * nbc, block, 0, unroll=ATTN_UNROLL)

        _wait(_whole_waits(bufs[0], sem_in.at[0]))
        run(ATTN_PLANS[0], *bufs[0], tab128, y_ref, lt_ref, 0)
        prefetch(0)
        _wait(_whole_waits(bufs[1], sem_in.at[1]))
        run(ATTN_PLANS[1], *bufs[1], tab4, oc4, lc4, PAD)
        prefetch(1)
        _wait(_whole_waits(bufs[2], sem_in.at[2]))
        run(ATTN_PLANS[2], *bufs[2], tab128, oc16, lc16, PAD)
        prefetch(2)

        n_rows = 64

        def token_order(buf, t, n_cls):
            per = S // n_cls
            first = PAD + t * (n_rows // n_cls)
            return jnp.concatenate([buf[pl.ds(first + jj, n_cls, stride=per), :] for jj in range(n_rows // n_cls)],
                                   axis=0)

        def combine(t, carry):
            rows = pl.ds(pl.multiple_of(t * n_rows, n_rows), n_rows)
            l0, l1, l2 = lt_ref[rows, :], token_order(lc4, t, 8), token_order(lc16, t, 16)
            lm = jnp.maximum(jnp.maximum(l0, l1), l2)
            e0, e1, e2 = jnp.exp(l0 - lm), jnp.exp(l1 - lm), jnp.exp(l2 - lm)
            den = e0 + e1 + e2
            y_ref[rows, :] = (e0 * y_ref[rows, :] + e1 * token_order(oc4, t, 8)
                              + e2 * token_order(oc16, t, 16)) / den
            lt_ref[rows, :] = lm + jnp.log(den)
            return carry
        lax.fori_loop(0, S // n_rows, combine, 0, unroll=2)

        if ng:
            pl.when(step == n_grid - 1)(finish_gather)

    col = pl.BlockSpec((S, LANES), lambda h: (0, h))
    padded = pltpu.VMEM((PAD + S, LANES), F32)
    return pl.pallas_call(
        body, grid=(n_grid,), name="attn_fwd",
        in_specs=[ANY] * (6 + ng), out_specs=[col, col] + [ANY] * ng,
        out_shape=[jax.ShapeDtypeStruct((S, AW), F32)] * 2 + _gathered_shapes(shards),
        scratch_shapes=[padded] * 10 + [
            pltpu.VMEM((4, 256, WIN), F32), pltpu.VMEM((4, 128, WIN), F32), pltpu.SemaphoreType.DMA((3, 2))]
        + (_gather_scratch(ng) if ng else []),
        compiler_params=_cparams(56, **({"collective_id": ID_ATTN_FWD} if ng else {})))(*flat, *shards)


def _conv_taps(z, zprev, row):
    z1 = jnp.where(row == 0, zprev[7:8, :], pltpu.roll(z, 1, 0))
    z2 = jnp.where(row == 0, zprev[6:7, :], jnp.where(row == 1, zprev[7:8, :], pltpu.roll(z, 2, 0)))
    return z1, z2


def _xattn_scores(qm, km):
    s = _dot_nt(qm, km)
    m = jnp.max(s, axis=1, keepdims=True)
    e = jnp.exp(s - m)
    return e, jnp.sum(e, axis=1, keepdims=True)


def _mix_out(y_attn, bcu, qx16, kv16, cw8, g_attn, g_conv, g_x, g_post, wout16, x, shards):
    def body(ya_ref, bcu_ref, halo_ref, qx_ref, kv_ref, cw_ref, ga_ref, gc_ref, gx_ref, gp_ref, w_ref, x_ref,
             ypre_ref, y16_ref, y2_ref, x1_ref):
        i = pl.program_id(0)
        bcu = bcu_ref[...]
        b, c, u = bcu[:, 0:CW], bcu[:, CW:2 * CW], bcu[:, 2 * CW:]
        z = c * u
        halo = halo_ref[...]
        zprev = jnp.where(i > 0, halo[:, CW:2 * CW] * halo[:, 2 * CW:], 0.0)
        row = lax.broadcasted_iota(jnp.int32, z.shape, 0)
        z1, z2 = _conv_taps(z, zprev, row)
        cw = cw_ref[...]
        y_conv = b * (z2 * cw[0:1, :] + z1 * cw[1:2, :] + z * cw[2:3, :])

        qx = qx_ref[...]
        kv = kv_ref[...]
        km, vm = kv[:, 0:XW], kv[:, XW:]
        lane = lax.broadcasted_iota(jnp.int32, qx.shape, 1)
        y_x = jnp.zeros(qx.shape, F32)
        for h in range(XW // HEAD):
            hm = (lane >= h * HEAD) & (lane < (h + 1) * HEAD)
            e, l = _xattn_scores(jnp.where(hm, qx, jnp.zeros_like(qx)), km)
            y_x = jnp.where(hm, _dot(e.astype(BF16), vm) / l, y_x)

        y_attn = ya_ref[...]
        ypre_ref[:, 0:AW] = y_attn
        ypre_ref[:, AW:AW + CW] = y_conv
        ypre_ref[:, AW + CW:] = y_x
        y = jnp.concatenate([_rms(y_attn, ga_ref[...])[0], _rms(y_conv, gc_ref[...])[0],
                             _rms(y_x, gx_ref[...])[0]], axis=1).astype(BF16)
        y16_ref[...] = y.T
        y2 = _dot(y, w_ref[...])
        y2_ref[...] = y2
        x1_ref[...] = x_ref[...] + _rms(y2, gp_ref[...])[0]

    def tile(w):
        return pl.BlockSpec((TQ, w), lambda i: (i, 0))

    halo = pl.BlockSpec((SUBLANES, 3 * CW), lambda i: (jnp.maximum(i * (TQ // SUBLANES) - 1, 0), 0))
    return _call_with_gather(
        body, NT, shards, name="mix_out",
        in_specs=[tile(AW), tile(3 * CW), halo, tile(XW), _const((N_MEM, 2 * XW)), _const((SUBLANES, CW)),
                  _const((1, AW)), _const((1, CW)), _const((1, XW)), _const((1, D)), _const((D, D)), tile(D)],
        out_specs=[tile(D), _tokens_in_lanes(TQ), tile(D), tile(D)],
        out_shape=[jax.ShapeDtypeStruct((S, D), F32), jax.ShapeDtypeStruct((D, S), BF16),
                   jax.ShapeDtypeStruct((S, D), F32), jax.ShapeDtypeStruct((S, D), F32)],
        scratch_shapes=[], vmem_mb=56,
        args=(y_attn, bcu, bcu, qx16, kv16, cw8, g_attn, g_conv, g_x, g_post, wout16, x))


def _mlp(x1, tgt, g_pre, g_post, wup8, wdn_halves):
    tq = TQ_MLP
    half = D // 2

    def body(x1_ref, t_ref, g1_ref, g2_ref, wu_ref, wda_ref, wdb_ref,
             a16_ref, du_ref, h2_ref, df2_ref, dx1_ref, loss_ref, dg_ref):
        @pl.when(pl.program_id(0) == 0)
        def _():
            loss_ref[...] = jnp.zeros_like(loss_ref)
            dg_ref[...] = jnp.zeros_like(dg_ref)

        x1 = x1_ref[...]
        g1, g2 = g1_ref[...], g2_ref[...]
        y1, n1, r1 = _rms(x1, g1)
        h2 = y1.astype(BF16)
        h2_ref[...] = h2.T
        f2a = jnp.zeros((tq, half), F32)
        f2b = jnp.zeros((tq, half), F32)
        for j in range(N_DEV):
            cols = slice(j * FF_BLK, (j + 1) * FF_BLK)
            a = jnp.maximum(_dot(h2, wu_ref[j]), 0.0)
            a16_ref[:, cols] = a.astype(BF16)
            f = (a * a).astype(BF16)
            f2a = f2a + _dot(f, wda_ref[cols, :])
            f2b = f2b + _dot(f, wdb_ref[cols, :])
        f2 = jnp.concatenate([f2a, f2b], axis=1)
        y2, n2, r2 = _rms(f2, g2)
        e = x1 + y2 - t_ref[...]
        sq = jnp.sum(jnp.sum(e * e, axis=1, keepdims=True), axis=0, keepdims=True)
        loss_ref[...] += jnp.broadcast_to(sq * (0.5 / D), loss_ref.shape)
        dout = e * (1.0 / D)
        df2, dg2 = _rms_bwd(dout, n2, r2, g2)
        df2_16 = df2.astype(BF16)
        df2_ref[...] = df2_16.T
        dh2 = jnp.zeros((tq, D), F32)
        for j in range(N_DEV):
            cols = slice(j * FF_BLK, (j + 1) * FF_BLK)
            df = _dot_nt(df2_16[:, 0:half], wda_ref[cols, :]) + _dot_nt(df2_16[:, half:], wdb_ref[cols, :])
            du = (df * (2.0 * a16_ref[:, cols].astype(F32))).astype(BF16)
            du_ref[:, cols] = du
            dh2 = dh2 + _dot_nt(du, wu_ref[j])
        dx, dg1 = _rms_bwd(dh2, n1, r1, g1)
        dx1_ref[...] = dout + dx
        dg_ref[0:1, :] += dg2
        dg_ref[1:2, :] += dg1

    def tile(w):
        return pl.BlockSpec((tq, w), lambda i: (i, 0))

    return pl.pallas_call(
        body, grid=(S // tq,), name="mlp",
        in_specs=[tile(D), tile(D), _const((1, D)), _const((1, D)), _const((N_DEV, D, FF_BLK)), _const((FF, half)), _const((FF, half))],
        out_specs=[tile(FF), tile(FF), _tokens_in_lanes(tq), _tokens_in_lanes(tq), tile(D),
                   _acc((SUBLANES, LANES)), _acc((SUBLANES, D))],
        out_shape=[jax.ShapeDtypeStruct((S, FF), BF16), jax.ShapeDtypeStruct((S, FF), BF16),
                   jax.ShapeDtypeStruct((D, S), BF16), jax.ShapeDtypeStruct((D, S), BF16),
                   jax.ShapeDtypeStruct((S, D), F32), jax.ShapeDtypeStruct((SUBLANES, LANES), F32),
                   jax.ShapeDtypeStruct((SUBLANES, D), F32)],
        compiler_params=_cparams(60))(x1, tgt, g_pre, g_post, wup8, *wdn_halves)


def _mix_out_bwd(dx1, y2, ypre, ltot, head_ones, q, bcu, qx16, kv16, cw8, g_post, g_attn, g_conv, g_x, wout16):
    def body(dx1_ref, y2_ref, ypre_ref, lt_ref, e_ref, q_ref, bcu_ref, halo_ref, qx_ref, kv_ref, cw_ref, gp_ref,
             ga_ref, gc_ref, gx_ref, w_ref, dy2_ref, qdo_ref, ld_ref, dbcu_ref, dqx_ref, dgs_ref, dcw_ref, dkv_ref,
             carry):
        i = pl.program_id(0)

        @pl.when(i == 0)
        def _():
            dgs_ref[...] = jnp.zeros_like(dgs_ref)
            dcw_ref[...] = jnp.zeros_like(dcw_ref)
            dkv_ref[...] = jnp.zeros_like(dkv_ref)
            carry[...] = jnp.zeros_like(carry)

        gp = gp_ref[...]
        _, n, r = _rms(y2_ref[...], gp)
        dy2, dgp = _rms_bwd(dx1_ref[...], n, r, gp)
        dy2_16 = dy2.astype(BF16)
        dy2_ref[...] = dy2_16
        dy = _dot_nt(dy2_16, w_ref[...])

        ypre = ypre_ref[...]
        ga, gc, gx = ga_ref[...], gc_ref[...], gx_ref[...]
        _, na, ra = _rms(ypre[:, 0:AW], ga)
        dya, dga = _rms_bwd(dy[:, 0:AW], na, ra, ga)
        _, nc, rc = _rms(ypre[:, AW:AW + CW], gc)
        dyc, dgc = _rms_bwd(dy[:, AW:AW + CW], nc, rc, gc)
        y_x = ypre[:, AW + CW:]
        _, nx, rx = _rms(y_x, gx)
        dyx, dgx = _rms_bwd(dy[:, AW + CW:], nx, rx, gx)
        qdo_ref[...] = _pack_pair(q_ref[...], dya)
        prod = dya * ypre[:, 0:AW]
        hi = prod.astype(BF16)
        lo = (prod - hi.astype(F32)).astype(BF16)
        head_sum = _dot(hi, e_ref[...]) + _dot(lo, e_ref[...])
        lane_a = lax.broadcasted_iota(jnp.int32, prod.shape, 1)
        ld_ref[...] = jnp.where((lane_a % HEAD) < HEAD // 2, lt_ref[...], head_sum)
        dgs_ref[0:1, :] += dgp
        dgs_ref[1:2, :] += jnp.concatenate([dga, dgc, dgx], axis=1)

        bcu = bcu_ref[...]
        b, c, u = bcu[:, 0:CW], bcu[:, CW:2 * CW], bcu[:, 2 * CW:]
        z = c * u
        halo = halo_ref[...]
        zprev = jnp.where(i < NT - 1, halo[:, CW:2 * CW] * halo[:, 2 * CW:], 0.0)
        row = lax.broadcasted_iota(jnp.int32, z.shape, 0)
        z1, z2 = _conv_taps(z, zprev, row)
        cw = cw_ref[...]
        conv = z2 * cw[0:1, :] + z1 * cw[1:2, :] + z * cw[2:3, :]
        dconv = dyc * b
        nxt = carry[...]
        dn1 = jnp.where(row == TQ - 1, nxt[0:1, :], pltpu.roll(dconv, TQ - 1, 0))
        dn2 = jnp.where(row == TQ - 1, nxt[1:2, :], jnp.where(row == TQ - 2, nxt[0:1, :], pltpu.roll(dconv, TQ - 2, 0)))
        carry[...] = dconv[0:SUBLANES, :]
        dz = dconv * cw[2:3, :] + dn1 * cw[1:2, :] + dn2 * cw[0:1, :]
        dbcu_ref[:, 0:CW] = (dyc * conv).astype(BF16)
        dbcu_ref[:, CW:2 * CW] = (dz * u).astype(BF16)
        dbcu_ref[:, 2 * CW:] = (dz * c).astype(BF16)
        dcw_ref[0:1, :] += jnp.sum(z2 * dconv, axis=0, keepdims=True)
        dcw_ref[1:2, :] += jnp.sum(z1 * dconv, axis=0, keepdims=True)
        dcw_ref[2:3, :] += jnp.sum(z * dconv, axis=0, keepdims=True)

        qx = qx_ref[...]
        kv = kv_ref[...]
        km, vm = kv[:, 0:XW], kv[:, XW:]
        lane = lax.broadcasted_iota(jnp.int32, qx.shape, 1)
        dqx = jnp.zeros(qx.shape, F32)
        dkm = jnp.zeros((N_MEM, XW), F32)
        dvm = jnp.zeros((N_MEM, XW), F32)
        for h in range(XW // HEAD):
            hm = (lane >= h * HEAD) & (lane < (h + 1) * HEAD)
            qm = jnp.where(hm, qx, jnp.zeros_like(qx))
            e, l = _xattn_scores(qm, km)
            p = e / l
            dom = jnp.where(hm, dyx, 0.0)
            do16 = dom.astype(BF16)
            dsum = jnp.sum(dom * y_x, axis=1, keepdims=True)
            ds = (p * (_dot_nt(do16, vm) - dsum)).astype(BF16)
            dqx = jnp.where(hm, _dot(ds, km), dqx)
            dkm = dkm + _dot_tn(ds, qm)
            dvm = dvm + _dot_tn(p.astype(BF16), do16)
        dqx_ref[...] = (dqx * SCALE).astype(BF16)
        dkv_ref[:, 0:XW] += dkm
        dkv_ref[:, XW:] += dvm

    def tile(w):
        return pl.BlockSpec((TQ, w), lambda i: (NT - 1 - i, 0))

    halo = pl.BlockSpec((SUBLANES, 3 * CW), lambda i: (jnp.maximum((NT - 1 - i) * (TQ // SUBLANES) - 1, 0), 0))
    return pl.pallas_call(
        body, grid=(NT,), name="mix_out_bwd",
        in_specs=[tile(D), tile(D), tile(D), tile(AW), _const((AW, AW)), tile(AW), tile(3 * CW), halo, tile(XW),
                  _const((N_MEM, 2 * XW)), _const((SUBLANES, CW)), _const((1, D)), _const((1, AW)), _const((1, CW)),
                  _const((1, XW)), _const((D, D))],
        out_specs=[tile(D), tile(AW), tile(AW), tile(3 * CW), tile(XW), _acc((SUBLANES, D)), _acc((SUBLANES, CW)),
                   _acc((N_MEM, 2 * XW))],
        out_shape=[jax.ShapeDtypeStruct((S, D), BF16), jax.ShapeDtypeStruct((S, AW), F32),
                   jax.ShapeDtypeStruct((S, AW), F32),
                   jax.ShapeDtypeStruct((S, 3 * CW), BF16), jax.ShapeDtypeStruct((S, XW), BF16),
                   jax.ShapeDtypeStruct((SUBLANES, D), F32), jax.ShapeDtypeStruct((SUBLANES, CW), F32),
                   jax.ShapeDtypeStruct((N_MEM, 2 * XW), F32)],
        scratch_shapes=[pltpu.VMEM((SUBLANES, CW), F32)],
        compiler_params=_cparams(56))(dx1, y2, ypre, ltot, head_ones, q, bcu, bcu, qx16, kv16, cw8, g_post, g_attn,
                                      g_conv, g_x, wout16)


def _attn_bwd(qdo, kvp, ld, chip_sums=()):
    n_in = 3
    views = [[a] + [a.reshape(S // n, n, AW) for _, n, _, _ in ATTN_PLANS[1:]] for a in (qdo, kvp, ld)]
    flat = [views[a][p] for p in range(3) for a in range(n_in)]
    ns = len(chip_sums)
    n_grid = AW // LANES

    def body(*refs):
        hbm = [refs[n_in * p:n_in * p + n_in] for p in range(3)]
        refs = refs[3 * n_in:]
        sum_refs, refs = refs[:ns], refs[ns:]
        outs = [refs[3 * p:3 * p + 3] for p in range(3)]
        landed_refs, sc = refs[9:9 + ns], refs[9 + ns:]
        bufs = [sc[3 * p:3 * p + 3] for p in range(3)]
        res = [sc[9 + 3 * p:12 + 3 * p] for p in range(3)]
        tab128, tab4, sem_in, sem_out = sc[18:22]
        step = pl.program_id(0)
        if ns:
            start_chips, finish_chips = _chips_steps(sum_refs, landed_refs, *sc[22:])
            _, _, core, chips = _place()
            signal_chips, chips_are_in = _own_barrier([(px, py, core) for px, py in chips])
            pl.when(step == 0)(signal_chips)

            def chips_go():
                chips_are_in()
                start_chips()
        now =[_class_gather(hbm[p], bufs[p], sem_in.at[p], _lanes_of(step)) for p in range(3)]
        nxt = [_class_gather(hbm[p], bufs[p], sem_in.at[p], _lanes_of(step + 1)) for p in range(3)]

        @pl.when(step == 0)
        def _():
            for p in range(3):
                _start(now[p])
                for b in bufs[p]:
                    b[0:PAD, :] = jnp.zeros((PAD, LANES), F32)
            _fill_bias(tab128, 128, False)
            _fill_bias(tab4, 64, True)

        def prefetch(p):
            pl.when(step + 1 < n_grid)(lambda: _start(nxt[p]))

        lane = lax.broadcasted_iota(jnp.int32, (1, LANES), 1)

        def run(plan, plan_bufs, tab, dst):
            _, n_cls, qblk, nbc = plan
            partner = n_cls == 8
            bqdo, bkv, bld = plan_bufs
            rq, rk, rv = dst

            def block(g, carry):
                own, wins, mask = _block_rows(g, qblk, nbc, partner)
                qb, dob = _unpack_pair(bqdo[own, :])
                q2, do2 = _stack_heads(qb, lane), _stack_heads(dob, lane)
                kw, vw = _unpack_pair(_window(bkv, wins))
                ldv = bld[own, :]
                half = HEAD // 2
                lt2 = jnp.concatenate([ldv[:, 0:1], ldv[:, HEAD:HEAD + 1]], axis=0)
                dsum2 = jnp.concatenate([ldv[:, half:half + 1], ldv[:, HEAD + half:HEAD + half + 1]], axis=0)
                p = jnp.exp(_dot_nt(q2, kw) + tab[mask] - lt2)
                ds = (p * (_dot_nt(do2, vw) - dsum2)).astype(BF16)
                rq[own, :] = _unstack_heads(_dot(ds, kw), lane)
                dkw = _dot_tn(ds, q2)
                dvw = _dot_tn(p.astype(BF16), do2)
                n_w = WIN // len(wins)
                for i, w in enumerate(wins):
                    rk[w, :] += dkw[i * n_w:(i + 1) * n_w, :]
                    rv[w, :] += dvw[i * n_w:(i + 1) * n_w, :]
                return carry
            lax.fori_loop(0, n_cls * nbc, block, 0, unroll=ATTN_UNROLL)

        tabs = (tab128, tab4, tab128)
        def drained(p):
            return lambda: _wait(_whole_waits(res[p], sem_out.at[p]))

        for p in range(3):
            pl.when(step > 0)(drained(p))
            for b in res[p][1:]:
                b[...] = jnp.zeros_like(b)
            _wait(_whole_waits(bufs[p], sem_in.at[p]))
            run(ATTN_PLANS[p], bufs[p], tabs[p], res[p])
            prefetch(p)
            _start(_class_scatter(res[p], outs[p], sem_out.at[p], _lanes_of(step)))
            if ns and p == 0:
                pl.when(step == 0)(chips_go)
        for p in range(3):
            pl.when(step == n_grid - 1)(drained(p))
        if ns:
            pl.when(step == n_grid - 1)(finish_chips)

    padded = pltpu.VMEM((PAD + S, LANES), F32)
    shapes = [jax.ShapeDtypeStruct(views[0][p].shape, F32) for p in range(3) for _ in range(3)]
    out = pl.pallas_call(
        body, grid=(n_grid,), name="attn_bwd",
        in_specs=[ANY] * (3 * n_in + ns), out_specs=[ANY] * (9 + ns),
        out_shape=shapes + _chips_shapes(chip_sums),
        scratch_shapes=[padded] * 18
        + [pltpu.VMEM((4, 256, WIN), F32), pltpu.VMEM((4, 128, WIN), F32),
           pltpu.SemaphoreType.DMA((3, n_in)), pltpu.SemaphoreType.DMA((3, 3))]
        + (_chips_scratch(ns) if ns else []),
        compiler_params=_cparams(56, **({"collective_id": ID_ATTN_BWD} if ns else {})))(*flat, *chip_sums)
    return [o.reshape(S, AW) for o in out[:9]] + list(out[9:])


def _in_proj_bwd(dqkv, dbcu, dqx, cos, sins, w16, x, g, dx1):
    tq = TQ // 2

    def body(*refs):
        parts = refs[0:9]
        dbcu_ref, dqx_ref, c_ref, s_ref, w_ref, x_ref, g_ref, dx1_ref, dp_ref, gx_ref, dg_ref = refs[9:]

        @pl.when(pl.program_id(0) == 0)
        def _():
            dg_ref[...] = jnp.zeros_like(dg_ref)

        dq, dk, dv = (parts[i][...] + parts[3 + i][...] + parts[6 + i][...] for i in range(3))
        cos, sn = _all_heads(c_ref[...]), _all_heads(s_ref[...])
        dqr = dq * SCALE
        dkr = dk
        dp = jnp.concatenate([(dqr * cos + _rot_half(dqr * sn)).astype(BF16),
                              (dkr * cos + _rot_half(dkr * sn)).astype(BF16), dv.astype(BF16),
                              dbcu_ref[...], dqx_ref[...]], axis=1)
        dp_ref[...] = dp
        dh = _dot_nt(dp, w_ref[...])
        g = g_ref[...]
        _, n, r = _rms(x_ref[...], g)
        dx, dg = _rms_bwd(dh, n, r, g)
        gx_ref[...] = dx1_ref[...] + dx
        dg_ref[0:1, :] += dg

    def tile(w):
        return pl.BlockSpec((tq, w), lambda i: (i, 0))

    return pl.pallas_call(
        body, grid=(S // tq,), name="in_proj_bwd",
        in_specs=[tile(AW)] * 9 + [tile(3 * CW), tile(XW), tile(LANES), tile(LANES), _const((D, PW)),
                                   tile(D), _const((1, D)), tile(D)],
        out_specs=[tile(PW), tile(D), _acc((SUBLANES, D))],
        out_shape=[jax.ShapeDtypeStruct((S, PW), BF16), jax.ShapeDtypeStruct((S, D), F32),
                   jax.ShapeDtypeStruct((SUBLANES, D), F32)],
        compiler_params=_cparams(56))(*dqkv, dbcu, dqx, cos, sins, w16, x, g, dx1)


def _mem_bwd(mem, g_mem, wkv16, dkv):
    def body(m_ref, g_ref, w_ref, dkv_ref, dkv16_ref, dg_ref):
        dkv16 = dkv_ref[...].astype(BF16)
        dkv16_ref[...] = dkv16
        _, n, _ = _rms(m_ref[...], g_ref[...])
        dg = jnp.sum(_dot_nt(dkv16, w_ref[...]) * n, axis=0, keepdims=True)
        dg_ref[...] = jnp.broadcast_to(dg, dg_ref.shape)

    return pl.pallas_call(
        body, name="mem_bwd",
        out_shape=[jax.ShapeDtypeStruct((N_MEM, 2 * XW), BF16), jax.ShapeDtypeStruct((SUBLANES, D), F32)],
        compiler_params=pltpu.CompilerParams(vmem_limit_bytes=32 << 20))(mem, g_mem, wkv16, dkv)


N_CHIPS = N_DEV // 2


def _pair_scratch(block):
    return [pltpu.VMEM((N_CHIPS,) + block, BF16), pltpu.VMEM((N_CHIPS,) + block, BF16),
            pltpu.SemaphoreType.DMA((N_CHIPS,)), pltpu.SemaphoreType.DMA((N_CHIPS,))]


def _swap_with_sibling(p, stage, land, send, recv):
    x, y, c = lax.axis_index("x"), lax.axis_index("y"), lax.axis_index("c")
    return pltpu.make_async_remote_copy(src_ref=stage.at[p], dst_ref=land.at[p], send_sem=send.at[p],
                                        recv_sem=recv.at[p], device_id=(x, y, 1 - c), device_id_type=MESH)


def _own_barrier(peers):
    sem = pltpu.get_barrier_semaphore()

    def signal():
        for peer in peers:
            pl.semaphore_signal(sem, inc=1, device_id=peer, device_id_type=MESH)

    return signal, lambda: pl.semaphore_wait(sem, len(peers))


def _sibling_barrier():
    x, y, c = lax.axis_index("x"), lax.axis_index("y"), lax.axis_index("c")
    return _own_barrier([(x, y, 1 - c)])


ID_WGRAD_UP, ID_WGRAD_DOWN, ID_WGRAD_ROWS, ID_ROPE_TABLE, ID_IN_PROJ, ID_ATTN_FWD, ID_ATTN_BWD = range(7)


def _wgrad_cols(place, at16, b16, blk, name, square_b=False, transpose_out=False, to_chips=False, small=(),
                sibling_only_id=None):
    m, kk = at16.shape
    assert sibling_only_id is None or not (to_chips or small)
    aligned = blk % LANES == 0
    wide = blk if aligned else -(-(blk + LANES // 2) // LANES) * LANES
    assert aligned or (transpose_out and blk % SUBLANES == 0)
    block = (blk, m) if transpose_out else (m, blk)

    def chip_of(step, my_chip):
        return jnp.bitwise_xor(my_chip, N_CHIPS - 1 - step) if to_chips else step

    def body(pl_ref, a_ref, *refs):
        b_refs, refs = refs[:2 if aligned else 1], refs[2 if aligned else 1:]
        accs, refs = refs[:len(small)], refs[len(small):]
        (cs_ref, own_ref), refs = refs[:2], refs[2:]
        if to_chips:
            landed, refs = refs[0], refs[1:]
        if small:
            tot_ref, refs = refs[0], refs[1:]
        (stage, land, send, recv), refs = refs[:4], refs[4:]
        if not aligned:
            (win, wsem), refs = refs[:2], refs[2:]
        if small:
            start_small, finish_small = _small_reduce_steps(accs, tot_ref, *refs[-4:])
            refs = refs[:-4]
        step = pl.program_id(0)
        if small:
            pl.when(step == 0)(start_small)
        if sibling_only_id is not None:
            signal_sibling, sibling_is_in = _sibling_barrier()
            pl.when(step == 0)(signal_sibling)
        x, y, c = lax.axis_index("x"), lax.axis_index("y"), lax.axis_index("c")
        my_chip = 2 * x + y
        p = chip_of(step, my_chip)

        def fetch(at_step, mine):
            j = 2 * chip_of(at_step, my_chip) + (c if mine else 1 - c)
            first = pl.multiple_of(((j * blk) >> 7) << 7, LANES)
            slot = 2 * (at_step & 1) + mine
            return pltpu.make_async_copy(b_refs[0].at[:, pl.ds(first, wide)], win.at[slot], wsem.at[slot])

        if not aligned:
            @pl.when(step == 0)
            def _():
                fetch(0, 0).start()
                fetch(0, 1).start()

            @pl.when(step + 1 < N_CHIPS)
            def _():
                fetch(step + 1, 0).start()
                fetch(step + 1, 1).start()

        def partial(mine):
            if aligned:
                b = b_refs[mine][...]
                if square_b:
                    b = b * b
                acc = _dot(a_ref[...], b)
            else:
                fetch(step, mine).wait()
                acc = _dot(a_ref[...], win[2 * (step & 1) + mine]).T
                odd = c if mine else 1 - c
                return jnp.where(odd == 0, acc[0:blk], acc[wide - blk:wide])
            return acc.T if transpose_out else acc

        stage[p] = partial(0).astype(BF16)
        if sibling_only_id is not None:
            pl.when(step == 0)(sibling_is_in)
        swap = _swap_with_sibling(p, stage, land, send, recv)
        swap.start()
        mine = partial(1)
        swap.wait()
        total = mine + land[p].astype(F32)
        cs_ref[0] = total.astype(BF16)

        @pl.when(p == my_chip)
        def _():
            own_ref[...] = total

        if to_chips:
            stage2, send2, recv2 = refs
            flipped = jnp.bitwise_xor(p, my_chip)
            k = jnp.where(flipped == 2, 0, jnp.where(flipped == 1, 1, 2))

            def to_owner(src, k_, px, py):
                return pltpu.make_async_remote_copy(src_ref=src, dst_ref=landed.at[k_], send_sem=send2.at[k_],
                                                    recv_sem=recv2.at[k_], device_id=(px, py, c), device_id_type=MESH)

            @pl.when(p != my_chip)
            def _():
                stage2[p] = total.astype(BF16)
                to_owner(stage2.at[p], k, p >> 1, p & 1).start()

            @pl.when(step == N_CHIPS - 1)
            def _():
                for k_ in range(N_CHIPS - 1):
                    to_owner(stage2.at[0], k_, x, y).wait()

        if small:
            pl.when(step == N_CHIPS - 1)(finish_small)

    def b_spec(mine):
        return pl.BlockSpec((kk, blk), lambda i, s: (0, 2 * chip_of(i, s[1]) + (s[0] if mine else 1 - s[0])))

    b_specs, b_args = ([b_spec(0), b_spec(1)], (b16, b16)) if aligned else ([ANY], (b16,))
    scratch = _pair_scratch(block)
    if not aligned:
        scratch += [pltpu.VMEM((4, kk, wide), BF16), pltpu.SemaphoreType.DMA((4,))]
    out_specs = [pl.BlockSpec((1,) + block, lambda i, s: (chip_of(i, s[1]), 0, 0)), pl.BlockSpec(block, lambda i, s: (0, 0))]
    out_shape = [jax.ShapeDtypeStruct((N_CHIPS,) + block, BF16), jax.ShapeDtypeStruct(block, F32)]
    if to_chips:
        out_specs.append(ANY)
        out_shape.append(jax.ShapeDtypeStruct((N_CHIPS - 1,) + block, BF16))
        scratch += [pltpu.VMEM((N_CHIPS,) + block, BF16), pltpu.SemaphoreType.DMA((N_CHIPS - 1,)),
                    pltpu.SemaphoreType.DMA((N_CHIPS - 1,))]
    small_specs = [pl.BlockSpec(a.shape, lambda i, s: (0, 0)) for a in small]
    if small:
        out_specs.append(pl.BlockSpec((PACK_ROWS, D), lambda i, s: (0, 0)))
        out_shape.append(jax.ShapeDtypeStruct((PACK_ROWS, D), F32))
        scratch += _small_reduce_scratch()
    return pl.pallas_call(
        body, name=name,
        grid_spec=pltpu.PrefetchScalarGridSpec(
            num_scalar_prefetch=1, grid=(N_CHIPS,),
            in_specs=[pl.BlockSpec((m, kk), lambda i, s: (0, 0), pipeline_mode=pl.Buffered(1))] + b_specs + small_specs,
            out_specs=out_specs, scratch_shapes=scratch),
        out_shape=out_shape,
        compiler_params=_cparams(56, **({} if sibling_only_id is None else {"collective_id": sibling_only_id})),
    )(place, at16, *b_args, *small)


ROWS_STEPS = 4


def _wgrad_rows(place, products, name):
    n_prod = len(products)
    dims = [(at16.shape[0], at16.shape[1], b16.shape[1]) for at16, b16 in products]
    cut = [kk % (ROWS_STEPS * LANES) == 0 for _, kk, _ in dims]
    blocks = [(m // N_DEV, n) for m, _, n in dims]

    def body(pl_ref, *refs):
        ins, outs, scratch = refs[:2 * n_prod], refs[2 * n_prod:4 * n_prod], refs[4 * n_prod:]
        c, step = pl_ref[0], pl.program_id(0)

        def multiply(i):
            a_ref, b_ref, acc = ins[2 * i], ins[2 * i + 1], scratch[5 * i]

            @pl.when(step == 0)
            def _():
                acc[...] = _dot(a_ref[...], b_ref[...])

            if cut[i]:
                @pl.when(step > 0)
                def _():
                    acc[...] += _dot(a_ref[...], b_ref[...])

        def rows(i, owner):
            return pl.ds(pl.multiple_of(owner * blocks[i][0], blocks[i][0]), blocks[i][0])

        def send_sibling_side(i):
            acc, stage, land, send, recv = scratch[5 * i:5 * i + 5]
            swaps = []
            for p in range(N_CHIPS):
                stage[p] = acc[rows(i, 2 * p + 1 - c), :].astype(BF16)
                swaps.append(_swap_with_sibling(p, stage, land, send, recv))
                swaps[-1].start()
            return swaps

        def add_my_side(i, swaps):
            acc, land = scratch[5 * i], scratch[5 * i + 2]
            cs_ref, own_ref = outs[2 * i:2 * i + 2]
            for p in range(N_CHIPS):
                swaps[p].wait()
                total = acc[rows(i, 2 * p + c), :] + land[p].astype(F32)
                cs_ref[p] = total.astype(BF16)

                @pl.when(p == pl_ref[1])
                def _():
                    own_ref[...] = total

        signal_sibling, sibling_is_in = _sibling_barrier()
        pl.when(step == 0)(signal_sibling)
        for i in range(n_prod):
            multiply(i)

        @pl.when(step == ROWS_STEPS - 1)
        def _():
            sibling_is_in()
            swaps = [send_sibling_side(i) for i in range(n_prod)]
            for i in range(n_prod):
                add_my_side(i, swaps[i])

    in_specs, out_specs, out_shape, scratch = [pl.BlockSpec(memory_space=pltpu.SMEM)], [], [], []
    for (m, kk, n), cut_i, block in zip(dims, cut, blocks):
        chunk = kk // ROWS_STEPS
        in_specs += ([pl.BlockSpec((m, chunk), lambda i: (0, i)), pl.BlockSpec((chunk, n), lambda i: (i, 0))]
                     if cut_i else [_const((m, kk)), _const((kk, n))])
        out_specs += [_acc((N_CHIPS,) + block), _acc(block)]
        out_shape += [jax.ShapeDtypeStruct((N_CHIPS,) + block, BF16), jax.ShapeDtypeStruct(block, F32)]
        scratch += [pltpu.VMEM((m, n), F32)] + _pair_scratch(block)
    out = pl.pallas_call(
        body, grid=(ROWS_STEPS,), name=name, in_specs=in_specs, out_specs=out_specs, out_shape=out_shape,
        scratch_shapes=scratch, compiler_params=_cparams(56, collective_id=ID_WGRAD_ROWS),
    )(place, *[a for pair in products for a in pair])
    return [tuple(out[2 * i:2 * i + 2]) for i in range(n_prod)]


def _adamw_math(w, g, m, v):
    m = ADAM_B1 * m + (1.0 - ADAM_B1) * g
    v = ADAM_B2 * v + (1.0 - ADAM_B2) * jnp.square(g)
    m_hat = m / (1.0 - ADAM_B1 ** ADAM_STEP)
    v_hat = v / (1.0 - ADAM_B2 ** ADAM_STEP)
    delta = -ADAM_LR * (m_hat / (jnp.sqrt(v_hat) + ADAM_EPS) + ADAM_WD * w)
    return delta, m, v


def _adamw_shards(updates, name, chip_sums=()):
    names, nu, ns = list(updates), len(updates), len(chip_sums)

    def body(*refs):
        ins, sum_refs = refs[:5 * nu], refs[5 * nu:5 * nu + ns]
        outs = refs[5 * nu + ns:9 * nu + ns]
        landed_refs, scratch = refs[9 * nu + ns:9 * nu + 2 * ns], refs[9 * nu + 2 * ns:]
        if ns:
            start_chips, finish_chips = _chips_steps(sum_refs, landed_refs, *scratch)
            start_chips()
        for i in range(nu):
            o_ref, r_ref, w_ref, m_ref, v_ref = ins[5 * i:5 * i + 5]
            g_out, d_out, m_out, v_out = outs[4 * i:4 * i + 4]
            g = o_ref[...] + r_ref[0].astype(F32) + r_ref[1].astype(F32) + r_ref[2].astype(F32)
            g_out[...] = g
            d_out[...], m_out[...], v_out[...] = _adamw_math(w_ref[...], g, m_ref[...], v_ref[...])
        if ns:
            finish_chips()

    vmem = pl.BlockSpec(memory_space=pltpu.VMEM)
    out = pl.pallas_call(
        body, name=name,
        in_specs=[vmem] * (5 * nu) + [ANY] * ns, out_specs=[vmem] * (4 * nu) + [ANY] * ns,
        out_shape=[jax.ShapeDtypeStruct(updates[n][2].shape, F32) for n in names for _ in range(4)]
        + _chips_shapes(chip_sums),
        scratch_shapes=_chips_scratch(ns) if ns else [],
        compiler_params=pltpu.CompilerParams(vmem_limit_bytes=56 << 20),
    )(*[a for n in names for a in updates[n]], *chip_sums)
    return {n: out[4 * i:4 * i + 4] for i, n in enumerate(names)}, list(out[4 * nu:])


def _place():
    x, y, c = lax.axis_index("x"), lax.axis_index("y"), lax.axis_index("c")
    chips = [(1 - x, y), (x, 1 - y), (1 - x, 1 - y)]
    return x, y, c, chips


def _gather_steps(ins, outs, send, recv, lsem):
    nt = len(ins)
    x, y, c, (xn, yn, diag) = _place()
    me, sib = (x, y, c), (x, y, 1 - c)

    def slot(t, px, py, pc):
        return outs[t].at[4 * px + 2 * py + pc]

    def copy(t, k, block, to, src=None):
        return pltpu.make_async_remote_copy(
            src_ref=slot(t, *block) if src is None else src, dst_ref=slot(t, *block),
            send_sem=send.at[t, k], recv_sem=recv.at[t, k], device_id=to, device_id_type=MESH)

    mine = [pltpu.make_async_copy(ins[t], slot(t, *me), lsem.at[t]) for t in range(nt)]
    first = [copy(t, k, me, to, src=ins[t]) for t in range(nt) for k, to in ((0, sib), (1, (*xn, c)), (2, (*yn, c)))]

    signal_peers, peers_are_in = _own_barrier([sib, (*xn, c), (*yn, c)])

    def start():
        signal_peers()
        for cp in mine:
            cp.start()
        peers_are_in()
        for cp in first:
            cp.start()

    def landed(k, chip, also_to=None):
        for t in range(nt):
            copy(t, k, (*chip, c), me).wait_recv()
            if also_to is not None:
                copy(t, 3, (*chip, c), (*also_to, c)).start()
            copy(t, 3 + k, (*chip, c), sib).start()

    def relay():
        @pl.when(c == 0)
        def _():
            landed(1, xn, also_to=yn)
            landed(2, yn)

        @pl.when(c == 1)
        def _():
            landed(2, yn, also_to=xn)
            landed(1, xn)

    def finish():
        landed(3, diag)
        for t in range(nt):
            copy(t, 0, sib, me).wait_recv()
            for k, chip in ((4, xn), (5, yn), (6, diag)):
                copy(t, k, (*chip, 1 - c), me).wait_recv()
            for k in range(7):
                copy(t, k, me, sib).wait_send()
        for cp in mine:
            cp.wait()

    return start, relay, finish


def _gather_scratch(nt):
    return [pltpu.SemaphoreType.DMA((nt, 7)), pltpu.SemaphoreType.DMA((nt, 7)), pltpu.SemaphoreType.DMA((nt,))]


def _gathered_shapes(shards):
    return [jax.ShapeDtypeStruct((N_DEV,) + s.shape, s.dtype) for s in shards]


def _call_with_gather(body, n_grid, shards, *, name, in_specs, out_specs, out_shape, scratch_shapes, vmem_mb, args,
                      collective_id=None):
    assert (collective_id is None) == (not shards)
    ng, n_in, n_out = len(shards), len(in_specs), len(out_specs)

    def wrapped(*refs):
        ins, shard_refs = refs[:n_in], refs[n_in:n_in + ng]
        outs = refs[n_in + ng:n_in + ng + n_out]
        whole_refs = refs[n_in + ng + n_out:n_in + 2 * ng + n_out]
        scratch = refs[n_in + 2 * ng + n_out:]
        if ng:
            start, relay, finish = _gather_steps(shard_refs, whole_refs, *scratch[len(scratch_shapes):])
            pl.when(pl.program_id(0) == 0)(start)
            pl.when(pl.program_id(0) == n_grid // 2)(relay)
        body(*ins, *outs, *scratch[:len(scratch_shapes)])
        if ng:
            pl.when(pl.program_id(0) == n_grid - 1)(finish)

    return pl.pallas_call(
        wrapped, grid=(n_grid,), name=name,
        in_specs=list(in_specs) + [ANY] * ng, out_specs=list(out_specs) + [ANY] * ng,
        out_shape=list(out_shape) + _gathered_shapes(shards),
        scratch_shapes=list(scratch_shapes) + (_gather_scratch(ng) if ng else []),
        compiler_params=_cparams(vmem_mb, **({"collective_id": collective_id} if shards else {})))(*args, *shards)


def _chips_steps(ins, outs, send, recv):
    _, _, c, chips = _place()
    copies = [pltpu.make_async_remote_copy(
        src_ref=ins[t].at[2 * px + py], dst_ref=outs[t].at[j], send_sem=send.at[t, j], recv_sem=recv.at[t, j],
        device_id=(px, py, c), device_id_type=MESH) for t in range(len(ins)) for j, (px, py) in enumerate(chips)]

    def start():
        for cp in copies:
            cp.start()

    def finish():
        for cp in copies:
            cp.wait()

    return start, finish


def _chips_scratch(nt):
    return [pltpu.SemaphoreType.DMA((nt, 3)), pltpu.SemaphoreType.DMA((nt, 3))]


def _chips_shapes(cs16s):
    return [jax.ShapeDtypeStruct((3,) + g.shape[1:], g.dtype) for g in cs16s]


SMALL = (("g_pre_mix", 0, 0, D), ("g_mem", 1, 0, D), ("g_post_mix", 2, 0, D), ("g_attn_out", 3, 0, AW),
         ("g_conv_out", 3, AW, CW), ("g_xattn_out", 3, AW + CW, XW), ("g_post_mlp", 4, 0, D), ("g_pre_mlp", 5, 0, D))
CONV_ROW = 8
PACK_ROWS = 16


LOSS_ROW = 15


def _small_reduce_steps(accs, tot_ref, pack, land, send, recv):
    acc_in, acc_mem, acc_mix, acc_mlp, acc_cw, acc_loss = accs
    x, y, c, _ = _place()
    me = 4 * x + 2 * y + c
    copies = []
    for k in range(1, N_DEV):
        kx, ky, kc = (k >> 2) & 1, (k >> 1) & 1, k & 1
        peer = (1 - x if kx else x, 1 - y if ky else y, 1 - c if kc else c)
        copies.append(pltpu.make_async_remote_copy(
            src_ref=pack, dst_ref=land.at[me], send_sem=send.at[k - 1], recv_sem=recv.at[k - 1],
            device_id=peer, device_id_type=MESH))

    def start():
        pack[...] = jnp.zeros_like(pack)
        pack[0:1, :] = acc_in[0:1, :]
        pack[1:2, :] = acc_mem[0:1, :]
        pack[2:4, :] = acc_mix[0:2, :]
        pack[4:6, :] = acc_mlp[0:2, :]
        pack[CONV_ROW:CONV_ROW + 3, 0:CW] = acc_cw[0:3, :]
        pack[LOSS_ROW:LOSS_ROW + 1, 0:LANES] = acc_loss[0:1, :]
        land[me] = pack[...]
        for cp in copies:
            cp.start()

    def finish():
        for cp in copies:
            cp.wait()
        tot = land[0]
        for s in range(1, N_DEV):
            tot = tot + land[s]
        tot_ref[...] = tot

    return start, finish


def _small_reduce_scratch():
    return [pltpu.VMEM((PACK_ROWS, D), F32), pltpu.VMEM((N_DEV, PACK_ROWS, D), F32),
            pltpu.SemaphoreType.DMA((N_DEV - 1,)), pltpu.SemaphoreType.DMA((N_DEV - 1,))]


def _small_update(tot, me, params):
    flat = [a for n, _, _, _ in SMALL for a in params[n]] + list(params["conv_w"])
    n_par = len(SMALL) + 1
    tap_cols = CW // N_DEV

    def body(*refs):
        me_ref, tot_ref = refs[0:2]
        ins = refs[2:2 + 3 * n_par]
        loss_out = refs[2 + 3 * n_par]
        outs = refs[3 + 3 * n_par:]
        tot = tot_ref[...]
        loss_out[...] = jnp.broadcast_to(tot[LOSS_ROW:LOSS_ROW + 1, 0:LANES], loss_out.shape)

        def update(i, g):
            w_ref, m_ref, v_ref = ins[3 * i:3 * i + 3]
            for o_ref, res in zip(outs[4 * i:4 * i + 4], (g,) + _adamw_math(w_ref[...], g, m_ref[...], v_ref[...])):
                if len(o_ref.shape) == 3:
                    for t in range(o_ref.shape[0]):
                        o_ref[t] = res[t:t + 1, :]
                else:
                    o_ref[...] = res

        for i, (_, row, lane0, width) in enumerate(SMALL):
            update(i, tot[row:row + 1, lane0:lane0 + width])
        me = me_ref[0]
        taps = pltpu.roll(tot[CONV_ROW:CONV_ROW + SUBLANES, 0:CW], jnp.where(me == 0, 0, CW - me * tap_cols), 1)
        update(n_par - 1, taps[0:3, 0:tap_cols])

    shapes = [jax.ShapeDtypeStruct(params[n][0].shape, F32) for n, _, _, _ in SMALL] + [
        jax.ShapeDtypeStruct((3, 1, tap_cols), F32)]
    vmem = pl.BlockSpec(memory_space=pltpu.VMEM)
    loss, *out = pl.pallas_call(
        body, name="small_update",
        in_specs=[pl.BlockSpec(memory_space=pltpu.SMEM)] + [vmem] * (1 + 3 * n_par),
        out_shape=[jax.ShapeDtypeStruct((SUBLANES, LANES), F32)] + [s for s in shapes for _ in range(4)],
    )(me, tot, *flat)
    names = [n for n, _, _, _ in SMALL] + ["conv_w"]
    return loss[0, 0], {n: out[4 * i:4 * i + 4] for i, n in enumerate(names)}


def _local_step(x, mem, pos, gains, shards, tgt, place):
    half = HEAD // 2
    inv_freq = jnp.float32(ROPE_THETA) ** (-(jnp.arange(half, dtype=F32) * 2.0 / HEAD))
    invf = jnp.tile(inv_freq, LANES // half)[None, :]
    sgn = jnp.tile(jnp.concatenate([-jnp.ones((half,), F32), jnp.ones((half,), F32)]), LANES // HEAD)[None, :]
    cos, sins, win8 = _rope_table(pos.astype(F32).reshape(S, 1), invf, sgn, [shards["w_in"]])
    wdn_left, wdn_right = shards["w_down"][:, 0:D // 2], shards["w_down"][:, D // 2:]
    q, kvp, bcu, qx16, h16, win16, wout8, wkv8, conv8, wdn8_right = _in_proj(
        x, gains["g_pre_mix"], win8, cos, sins, [shards["w_out"], shards["w_mem_kv"], shards["conv_w"], wdn_right])
    wout16, wkv16 = wout8.reshape(D, D), wkv8.reshape(D, 2 * XW)
    cw_full = conv8[:, 0:3, 0:CW // N_DEV].transpose(1, 0, 2).reshape(3, CW)
    cw8 = jnp.zeros((SUBLANES, CW), F32).at[0:3].set(cw_full)
    y_attn, ltot, wup8, wdn8_left = _attn_fwd(q, kvp, [shards["w_up"], wdn_left])
    wdn_halves = (wdn8_left.reshape(FF, D // 2), wdn8_right.reshape(FF, D // 2))
    memn16, kv16 = _mem_fwd(mem, gains["g_mem"], wkv16)
    ypre, y16, y2, x1 = _mix_out(y_attn, bcu, qx16, kv16, cw8, gains["g_attn_out"], gains["g_conv_out"],
                                 gains["g_xattn_out"], gains["g_post_mix"], wout16, x, [])
    a16, du16, h2_16, df2_16, dx1, loss8, dg_mlp = _mlp(
        x1, tgt, gains["g_pre_mlp"], gains["g_post_mlp"], wup8, wdn_halves)

    sums = {"w_up": _wgrad_cols(place, h2_16, du16, FF_BLK, "wgrad_up", sibling_only_id=ID_WGRAD_UP),
            "w_down": _wgrad_cols(place, df2_16, a16, FF_BLK, "wgrad_down", square_b=True, transpose_out=True,
                                  sibling_only_id=ID_WGRAD_DOWN)}

    head_id = jnp.arange(AW, dtype=jnp.int32) // HEAD
    head_ones = (head_id[:, None] == head_id[None, :]).astype(BF16)
    dy2_16, qdo, ld, dbcu, dqx, dgs, dcw, dkv = _mix_out_bwd(
        dx1, y2, ypre, ltot, head_ones, q, bcu, qx16, kv16, cw8, gains["g_post_mix"], gains["g_attn_out"],
        gains["g_conv_out"], gains["g_xattn_out"], wout16)
    dkv16, dg_mem = _mem_bwd(mem, gains["g_mem"], wkv16, dkv)
    sums["w_mem_kv"], sums["w_out"] = _wgrad_rows(place, [(memn16, dkv16), (y16, dy2_16)], "wgrad_mem_kv_out")
    out = _attn_bwd(qdo, kvp, ld, [s[0] for s in sums.values()])
    dqkv, landed = out[:9], out[9:]
    reduced = {n: (s[1], landed[t]) for t, (n, s) in enumerate(sums.items())}
    dproj16, grad_x, dg_in = _in_proj_bwd(dqkv, dbcu, dqx, cos, sins, win16, x, gains["g_pre_mix"], dx1)

    _, in_own, in_landed, small_tot = _wgrad_cols(place, h16, dproj16, PW // N_DEV, "wgrad_in", transpose_out=True,
                                                  to_chips=True, small=(dg_in, dg_mem, dgs, dg_mlp, dcw, loss8))
    reduced["w_in"] = (in_own, in_landed)
    return grad_x, reduced, small_tot


BIG = ("w_in", "w_mem_kv", "w_out", "w_up", "w_down")
ORDER = ("g_pre_mix", "g_mem", "w_in", "w_mem_kv", "conv_w", "g_attn_out", "g_conv_out", "g_xattn_out", "w_out",
         "g_post_mix", "g_pre_mlp", "w_up", "w_down", "g_post_mlp")


def kernel(x, mem, positions, g_pre_mix, g_mem, w_in, w_mem_kv, conv_w, g_attn_out, g_conv_out, g_xattn_out, w_out, g_post_mix, g_pre_mlp, w_up, w_down, g_post_mlp, loss_target, m_g_pre_mix, m_g_mem, m_w_in, m_w_mem_kv, m_conv_w, m_g_attn_out, m_g_conv_out, m_g_xattn_out, m_w_out, m_g_post_mix, m_g_pre_mlp, m_w_up, m_w_down, m_g_post_mlp, v_g_pre_mix, v_g_mem, v_w_in, v_w_mem_kv, v_conv_w, v_g_attn_out, v_g_conv_out, v_g_xattn_out, v_w_out, v_g_post_mix, v_g_pre_mlp, v_w_up, v_w_down, v_g_post_mlp):
    w = dict(g_pre_mix=g_pre_mix, g_mem=g_mem, w_in=w_in, w_mem_kv=w_mem_kv, conv_w=conv_w, g_attn_out=g_attn_out,
             g_conv_out=g_conv_out, g_xattn_out=g_xattn_out, w_out=w_out, g_post_mix=g_post_mix, g_pre_mlp=g_pre_mlp,
             w_up=w_up, w_down=w_down, g_post_mlp=g_post_mlp)
    mo = dict(g_pre_mix=m_g_pre_mix, g_mem=m_g_mem, w_in=m_w_in, w_mem_kv=m_w_mem_kv, conv_w=m_conv_w,
              g_attn_out=m_g_attn_out, g_conv_out=m_g_conv_out, g_xattn_out=m_g_xattn_out, w_out=m_w_out,
              g_post_mix=m_g_post_mix, g_pre_mlp=m_g_pre_mlp, w_up=m_w_up, w_down=m_w_down, g_post_mlp=m_g_post_mlp)
    vo = dict(g_pre_mix=v_g_pre_mix, g_mem=v_g_mem, w_in=v_w_in, w_mem_kv=v_w_mem_kv, conv_w=v_conv_w,
              g_attn_out=v_g_attn_out, g_conv_out=v_g_conv_out, g_xattn_out=v_g_xattn_out, w_out=v_w_out,
              g_post_mix=v_g_post_mix, g_pre_mlp=v_g_pre_mlp, w_up=v_w_up, w_down=v_w_down, g_post_mlp=v_g_post_mlp)

    xi, yi, ci = lax.axis_index("x"), lax.axis_index("y"), lax.axis_index("c")
    me = 4 * xi + 2 * yi + ci
    place = jnp.stack([ci, 2 * xi + yi]).astype(jnp.int32)

    shards = {n: w[n][0].astype(BF16) for n in BIG}
    shards["conv_w"] = jnp.zeros((SUBLANES, LANES), F32).at[0:3, 0:CW // N_DEV].set(conv_w[0])

    gains = {n: w[n] for n, _, _, _ in SMALL}
    grad_x, reduced, small_tot = _local_step(x[0], mem[0], positions[0], gains, shards, loss_target[0], place)

    def shard(n, a):
        return a[0].T if n == "w_in" else a[0]

    updated = {}
    for group in (("w_up", "w_down"), ("w_in", "w_out", "w_mem_kv")):
        updated.update(_adamw_shards({n: (*reduced[n], shard(n, w[n]), shard(n, mo[n]), shard(n, vo[n]))
                                      for n in group}, "adamw_" + "_".join(group))[0])
    grad, delta, new_m, new_v = {}, {}, {}, {}
    for n, res in updated.items():
        grad[n], delta[n], new_m[n], new_v[n] = [(a.T if n == "w_in" else a)[None] for a in res]

    params = {n: (w[n], mo[n], vo[n]) for n, _, _, _ in SMALL}
    params["conv_w"] = (w["conv_w"][0], mo["conv_w"][0], vo["conv_w"][0])
    loss, small = _small_update(small_tot, me.reshape(1).astype(jnp.int32), params)
    for n, (g, d_, m_, v_) in small.items():
        lead = (lambda a: a.reshape(conv_w.shape)) if n == "conv_w" else (lambda a: a)
        grad[n], delta[n], new_m[n], new_v[n] = lead(g), lead(d_), lead(m_), lead(v_)

    return (loss, grad_x[None], *[grad[n] for n in ORDER], *[delta[n] for n in ORDER],
            *[new_m[n] for n in ORDER], *[new_v[n] for n in ORDER])
```

```python
import jax
import jax.numpy as jnp
from jax import lax
from jax.experimental import pallas as pl
from jax.experimental.pallas import tpu as pltpu

F32, BF16 = jnp.float32, jnp.bfloat16
MESH = pl.DeviceIdType.MESH
ANY = pl.BlockSpec(memory_space=pl.ANY)

N_DEV = 8
D = 1024
S = 4096
N_MEM = 256
HEAD = 64
AW, CW, XW = 512, 256, 256
PW = 3 * AW + 3 * CW + XW
FF = 4096
FF_BLK = FF // N_DEV
EPS = 1e-6
NEG = -1e30
SCALE = HEAD ** -0.5
ROPE_THETA = 10000.0
LANES = 128
SUBLANES = 8

ADAM_LR, ADAM_B1, ADAM_B2, ADAM_EPS, ADAM_WD, ADAM_STEP = 0.001, 0.9, 0.999, 1e-08, 0.01, 10

TQ = 512
TQ_MLP = 512
NT = S // TQ


def _cparams(vmem_mb, n_grid=1, **more):
    return pltpu.CompilerParams(dimension_semantics=("arbitrary",) * n_grid, vmem_limit_bytes=vmem_mb << 20, **more)


def _const(shape):
    nd = len(shape)
    return pl.BlockSpec(shape, lambda *_: (0,) * nd, pipeline_mode=pl.Buffered(1))


def _acc(shape):
    nd = len(shape)
    return pl.BlockSpec(shape, lambda *_: (0,) * nd)


def _tokens_in_lanes(tq):
    return pl.BlockSpec((D, tq), lambda i: (0, i))


def _dot(a, b):
    return jnp.dot(a, b, preferred_element_type=F32)


def _dot_nt(a, b):
    return lax.dot_general(a, b, (((1,), (1,)), ((), ())), preferred_element_type=F32)


def _dot_tn(a, b):
    return lax.dot_general(a, b, (((0,), (0,)), ((), ())), preferred_element_type=F32)


def _rms(x, g):
    r = lax.rsqrt(jnp.mean(x * x, axis=-1, keepdims=True) + EPS)
    n = x * r
    return n * g, n, r


def _rms_bwd(dy, n, r, g):
    dn = dy * g
    dx = r * (dn - n * jnp.mean(dn * n, axis=-1, keepdims=True))
    return dx, jnp.sum(dy * n, axis=0, keepdims=True)


def _rot_half(t):
    lane = lax.broadcasted_iota(jnp.int32, t.shape, 1)
    n = t.shape[1]
    return jnp.where((lane % HEAD) < HEAD // 2, pltpu.roll(t, n - HEAD // 2, 1), pltpu.roll(t, HEAD // 2, 1))


def _rope_table(pos_col, invf, sgn, shards):
    def body(p_ref, f_ref, s_ref, c_out, s_out):
        ang = p_ref[...] * f_ref[...]
        c_out[...] = jnp.cos(ang)
        s_out[...] = jnp.sin(ang) * s_ref[...]

    tile = pl.BlockSpec((TQ, LANES), lambda i: (i, 0))
    return _call_with_gather(
        body, NT, shards, name="rope_table",
        in_specs=[pl.BlockSpec((TQ, 1), lambda i: (i, 0)), _const((1, LANES)), _const((1, LANES))],
        out_specs=[tile, tile], out_shape=[jax.ShapeDtypeStruct((S, LANES), F32)] * 2,
        scratch_shapes=[], vmem_mb=32, args=(pos_col, invf, sgn), collective_id=ID_ROPE_TABLE)


def _all_heads(t):
    return jnp.tile(t, (1, AW // LANES))


def _mem_fwd(mem, g_mem, wkv16):
    def body(m_ref, g_ref, w_ref, n16_ref, kv_ref):
        y, _, _ = _rms(m_ref[...], g_ref[...])
        y16 = y.astype(BF16)
        n16_ref[...] = y16.T
        kv_ref[...] = _dot(y16, w_ref[...]).astype(BF16)

    return pl.pallas_call(
        body, name="mem_fwd",
        out_shape=[jax.ShapeDtypeStruct((D, N_MEM), BF16), jax.ShapeDtypeStruct((N_MEM, 2 * XW), BF16)],
        compiler_params=pltpu.CompilerParams(vmem_limit_bytes=32 << 20))(mem, g_mem, wkv16)


def _in_proj(x, g, w8, cos, sins, shards):
    blk = PW // N_DEV

    def body(x_ref, g_ref, w8_ref, c_ref, s_ref, q_ref, kv_ref, bcu_ref, qx_ref, h_ref, w_out, w_ref):
        @pl.when(pl.program_id(0) == 0)
        def _():
            for j in range(N_DEV):
                w_ref[:, j * blk:(j + 1) * blk] = w8_ref[j]
            w_out[...] = w_ref[...]

        y, _, _ = _rms(x_ref[...], g_ref[...])
        h = y.astype(BF16)
        h_ref[...] = h.T
        proj = _dot(h, w_ref[...])
        cos, sn = _all_heads(c_ref[...]), _all_heads(s_ref[...])
        q, k = proj[:, 0:AW], proj[:, AW:2 * AW]
        q_ref[...] = (q * cos + _rot_half(q) * sn) * SCALE
        kv_ref[...] = _pack_pair(k * cos + _rot_half(k) * sn, proj[:, 2 * AW:3 * AW])
        bcu_ref[...] = proj[:, 3 * AW:3 * AW + 3 * CW]
        qx_ref[...] = (proj[:, 3 * AW + 3 * CW:] * SCALE).astype(BF16)

    def tile(w):
        return pl.BlockSpec((TQ, w), lambda i: (i, 0))

    return _call_with_gather(
        body, NT, shards, name="in_proj",
        in_specs=[tile(D), _const((1, D)), _const((N_DEV, D, blk)), tile(LANES), tile(LANES)],
        out_specs=[tile(AW), tile(AW), tile(3 * CW), tile(XW), _tokens_in_lanes(TQ), _acc((D, PW))],
        out_shape=[jax.ShapeDtypeStruct((S, AW), F32)] * 2 + [
            jax.ShapeDtypeStruct((S, 3 * CW), F32), jax.ShapeDtypeStruct((S, XW), BF16),
            jax.ShapeDtypeStruct((D, S), BF16), jax.ShapeDtypeStruct((D, PW), BF16)],
        scratch_shapes=[pltpu.VMEM((D, PW), BF16)], vmem_mb=56, args=(x, g, w8, cos, sins),
        collective_id=ID_IN_PROJ)


ATTN_PLANS = (("p1", 1, 128, 32), ("p4", 8, 64, 8), ("p16", 16, 128, 2))
PAD = 128
WIN = 256


ATTN_UNROLL = 16


def _fill_bias(tab, qblk, partner):
    qi = lax.broadcasted_iota(jnp.int32, (2 * qblk, WIN), 0) & (qblk - 1)
    kj = lax.broadcasted_iota(jnp.int32, (2 * qblk, WIN), 1)
    piece = kj >> (qblk.bit_length() - 1)
    kk = kj & (qblk - 1)
    prev = (piece & 1) == 0
    of_partner = piece >= 2
    for first in (0, 1):
        for par in (0, 1):
            lo = jnp.where(prev, (qblk if first else qi) + jnp.where(of_partner, par, 0), 0)
            hi = jnp.where(prev, qblk, qi + jnp.where(of_partner, par - 1, 0))
            tab[2 * first + par] = jnp.where((kk >= lo) & (kk <= hi), 0.0, NEG).astype(F32)


def _block_rows(g, qblk, nbc, partner):
    own = pl.ds(pl.multiple_of(PAD + g * qblk, qblk), qblk)
    first = ((g & (nbc - 1)) == 0).astype(jnp.int32)
    if partner:
        gp = jnp.bitwise_xor(g, 4 * nbc)
        wins = (pl.ds(pl.multiple_of(PAD + (g - 1) * qblk, qblk), 2 * qblk),
                pl.ds(pl.multiple_of(PAD + (gp - 1) * qblk, qblk), 2 * qblk))
        return own, wins, 2 * first + ((g >> ((4 * nbc).bit_length() - 1)) & 1)
    return own, (pl.ds(pl.multiple_of(PAD + (g - 1) * qblk, qblk), 2 * qblk),), 2 * first


def _pack_pair(lo, hi):
    lo_bits = lax.bitcast_convert_type(lo.astype(BF16).astype(F32), jnp.uint32) >> 16
    hi_bits = lax.bitcast_convert_type(hi.astype(BF16).astype(F32), jnp.uint32) & jnp.uint32(0xFFFF0000)
    return lax.bitcast_convert_type(hi_bits | lo_bits, F32)


def _unpack_pair(c):
    bits = lax.bitcast_convert_type(c, jnp.uint32)
    lo = lax.bitcast_convert_type(bits << 16, F32).astype(BF16)
    hi = lax.bitcast_convert_type(bits & jnp.uint32(0xFFFF0000), F32).astype(BF16)
    return lo, hi


def _window(ref, wins):
    parts = [ref[w, :] for w in wins]
    return parts[0] if len(parts) == 1 else jnp.concatenate(parts, axis=0)


def _stack_heads(t, lane):
    zero = jnp.zeros_like(t)
    return jnp.concatenate([jnp.where(lane < HEAD, t, zero), jnp.where(lane >= HEAD, t, zero)], axis=0)


def _unstack_heads(t2, lane):
    half = t2.shape[0] // 2
    return jnp.where(lane < HEAD, t2[0:half, :], t2[half:, :])


def _lanes_of(step):
    return pl.ds(pl.multiple_of(step * LANES, LANES), LANES)


def _whole_wait(buf, sem):
    whole = buf.at[pl.ds(PAD, S), :]
    return pltpu.make_async_copy(whole, whole, sem)


def _whole_waits(bufs, sems):
    return [_whole_wait(buf, sems.at[i]) for i, buf in enumerate(bufs)]


def _class_gather(views, bufs, sems, lanes):
    copies = []
    for i, (view, buf) in enumerate(zip(views, bufs)):
        if view.ndim == 2:
            copies.append(pltpu.make_async_copy(view.at[:, lanes], buf.at[pl.ds(PAD, S), :], sems.at[i]))
        else:
            per, n_cls = view.shape[0], view.shape[1]
            copies += [pltpu.make_async_copy(view.at[:, c, lanes], buf.at[pl.ds(PAD + c * per, per), :], sems.at[i])
                       for c in range(n_cls)]
    return copies


def _class_scatter(bufs, dsts, sems, lanes):
    copies = []
    for i, (buf, dst) in enumerate(zip(bufs, dsts)):
        if dst.ndim == 2:
            copies.append(pltpu.make_async_copy(buf.at[pl.ds(PAD, S), :], dst.at[:, lanes], sems.at[i]))
            continue
        per, n_cls = dst.shape[0], dst.shape[1]
        copies += [pltpu.make_async_copy(buf.at[pl.ds(PAD + c * per, per), :], dst.at[:, c, lanes], sems.at[i])
                   for c in range(n_cls)]
    return copies


def _start(copies):
    for cp in copies:
        cp.start()


def _wait(waits):
    for w in waits:
        w.wait()


def _attn_fwd(q, kvp, shards=()):
    views = [[a] + [a.reshape(S // n, n, AW) for _, n, _, _ in ATTN_PLANS[1:]] for a in (q, kvp)]
    flat = [views[a][p] for p in range(3) for a in range(2)]
    ng = len(shards)
    n_grid = AW // LANES

    def body(*refs):
        hbm = [refs[2 * p:2 * p + 2] for p in range(3)]
        refs = refs[6:]
        shard_refs, refs = refs[:ng], refs[ng:]
        y_ref, lt_ref = refs[0:2]
        whole_refs, refs = refs[2:2 + ng], refs[2 + ng:]
        bufs = [refs[2 * p:2 * p + 2] for p in range(3)]
        oc4, lc4, oc16, lc16, tab128, tab4, sem_in = refs[6:13]
        step = pl.program_id(0)
        if ng:
            start_gather, relay_gather, finish_gather = _gather_steps(shard_refs, whole_refs, *refs[13:],
                                                                      own_barrier=False)
            pl.when(step == 0)(start_gather)
            pl.when(step == n_grid // 2)(relay_gather)
        now = [_class_gather(hbm[p], bufs[p], sem_in.at[p], _lanes_of(step)) for p in range(3)]
        nxt = [_class_gather(hbm[p], bufs[p], sem_in.at[p], _lanes_of(step + 1)) for p in range(3)]

        @pl.when(step == 0)
        def _():
            for p in range(3):
                _start(now[p])
                for b in bufs[p]:
                    b[0:PAD, :] = jnp.zeros((PAD, LANES), F32)
            _fill_bias(tab128, 128, False)
            _fill_bias(tab4, 64, True)

        def prefetch(p):
            pl.when(step + 1 < n_grid)(lambda: _start(nxt[p]))

        lane = lax.broadcasted_iota(jnp.int32, (1, LANES), 1)
        ones = jnp.ones((WIN, LANES), BF16)

        def run(plan, bq, bkv, tab, o_dst, l_dst, dst_pad):
            _, n_cls, qblk, nbc = plan
            partner = n_cls == 8

            def block(g, carry):
                own, wins, mask = _block_rows(g, qblk, nbc, partner)
                q2 = _stack_heads(bq[own, :].astype(BF16), lane)
                kw, vwin = _unpack_pair(_window(bkv, wins))
                vw = jnp.concatenate([vwin, ones], axis=1)
                s = _dot_nt(q2, kw) + tab[mask]
                m = jnp.max(s, axis=1, keepdims=True)
                oe = _dot(jnp.exp(s - m).astype(BF16), vw)
                den = oe[:, LANES:]
                dst = pl.ds(pl.multiple_of(dst_pad + g * qblk, qblk), qblk)
                o_dst[dst, :] = _unstack_heads(oe[:, 0:LANES] / den, lane)
                l_dst[dst, :] = _unstack_heads(m + jnp.log(den), lane)
                return carry
            lax.fori_loop(0, n_cls * nbc, block, 0, unroll=ATTN_UNROLL)

        _wait(_whole_waits(bufs[0], sem_in.at[0]))
        run(ATTN_PLANS[0], *bufs[0], tab128, y_ref, lt_ref, 0)
        prefetch(0)
        _wait(_whole_waits(bufs[1], sem_in.at[1]))
        run(ATTN_PLANS[1], *bufs[1], tab4, oc4, lc4, PAD)
        prefetch(1)
        _wait(_whole_waits(bufs[2], sem_in.at[2]))
        run(ATTN_PLANS[2], *bufs[2], tab128, oc16, lc16, PAD)
        prefetch(2)

        n_rows = 64

        def token_order(buf, t, n_cls):
            per = S // n_cls
            first = PAD + t * (n_rows // n_cls)
            return jnp.concatenate([buf[pl.ds(first + jj, n_cls, stride=per), :] for jj in range(n_rows // n_cls)],
                                   axis=0)

        def combine(t, carry):
            rows = pl.ds(pl.multiple_of(t * n_rows, n_rows), n_rows)
            l0, l1, l2 = lt_ref[rows, :], token_order(lc4, t, 8), token_order(lc16, t, 16)
            lm = jnp.maximum(jnp.maximum(l0, l1), l2)
            e0, e1, e2 = jnp.exp(l0 - lm), jnp.exp(l1 - lm), jnp.exp(l2 - lm)
            den = e0 + e1 + e2
            y_ref[rows, :] = (e0 * y_ref[rows, :] + e1 * token_order(oc4, t, 8)
                              + e2 * token_order(oc16, t, 16)) / den
            lt_ref[rows, :] = lm + jnp.log(den)
            return carry
        lax.fori_loop(0, S // n_rows, combine, 0, unroll=2)

        if ng:
            pl.when(step == n_grid - 1)(finish_gather)

    col = pl.BlockSpec((S, LANES), lambda h: (0, h))
    padded = pltpu.VMEM((PAD + S, LANES), F32)
    return pl.pallas_call(
        body, grid=(n_grid,), name="attn_fwd",
        in_specs=[ANY] * (6 + ng), out_specs=[col, col] + [ANY] * ng,
        out_shape=[jax.ShapeDtypeStruct((S, AW), F32)] * 2 + _gathered_shapes(shards),
        scratch_shapes=[padded] * 10 + [
            pltpu.VMEM((4, 256, WIN), F32), pltpu.VMEM((4, 128, WIN), F32), pltpu.SemaphoreType.DMA((3, 2))]
        + (_gather_scratch(ng) if ng else []),
        compiler_params=_cparams(56))(*flat, *shards)


def _conv_taps(z, zprev, row):
    z1 = jnp.where(row == 0, zprev[7:8, :], pltpu.roll(z, 1, 0))
    z2 = jnp.where(row == 0, zprev[6:7, :], jnp.where(row == 1, zprev[7:8, :], pltpu.roll(z, 2, 0)))
    return z1, z2


def _xattn_scores(qm, km):
    s = _dot_nt(qm, km)
    m = jnp.max(s, axis=1, keepdims=True)
    e = jnp.exp(s - m)
    return e, jnp.sum(e, axis=1, keepdims=True)


def _mix_out(y_attn, bcu, qx16, kv16, cw8, g_attn, g_conv, g_x, g_post, wout16, x, shards):
    def body(ya_ref, bcu_ref, halo_ref, qx_ref, kv_ref, cw_ref, ga_ref, gc_ref, gx_ref, gp_ref, w_ref, x_ref,
             ypre_ref, y16_ref, y2_ref, x1_ref):
        i = pl.program_id(0)
        bcu = bcu_ref[...]
        b, c, u = bcu[:, 0:CW], bcu[:, CW:2 * CW], bcu[:, 2 * CW:]
        z = c * u
        halo = halo_ref[...]
        zprev = jnp.where(i > 0, halo[:, CW:2 * CW] * halo[:, 2 * CW:], 0.0)
        row = lax.broadcasted_iota(jnp.int32, z.shape, 0)
        z1, z2 = _conv_taps(z, zprev, row)
        cw = cw_ref[...]
        y_conv = b * (z2 * cw[0:1, :] + z1 * cw[1:2, :] + z * cw[2:3, :])

        qx = qx_ref[...]
        kv = kv_ref[...]
        km, vm = kv[:, 0:XW], kv[:, XW:]
        lane = lax.broadcasted_iota(jnp.int32, qx.shape, 1)
        y_x = jnp.zeros(qx.shape, F32)
        for h in range(XW // HEAD):
            hm = (lane >= h * HEAD) & (lane < (h + 1) * HEAD)
            e, l = _xattn_scores(jnp.where(hm, qx, jnp.zeros_like(qx)), km)
            y_x = jnp.where(hm, _dot(e.astype(BF16), vm) / l, y_x)

        y_attn = ya_ref[...]
        ypre_ref[:, 0:AW] = y_attn
        ypre_ref[:, AW:AW + CW] = y_conv
        ypre_ref[:, AW + CW:] = y_x
        y = jnp.concatenate([_rms(y_attn, ga_ref[...])[0], _rms(y_conv, gc_ref[...])[0],
                             _rms(y_x, gx_ref[...])[0]], axis=1).astype(BF16)
        y16_ref[...] = y.T
        y2 = _dot(y, w_ref[...])
        y2_ref[...] = y2
        x1_ref[...] = x_ref[...] + _rms(y2, gp_ref[...])[0]

    def tile(w):
        return pl.BlockSpec((TQ, w), lambda i: (i, 0))

    halo = pl.BlockSpec((SUBLANES, 3 * CW), lambda i: (jnp.maximum(i * (TQ // SUBLANES) - 1, 0), 0))
    return _call_with_gather(
        body, NT, shards, name="mix_out",
        in_specs=[tile(AW), tile(3 * CW), halo, tile(XW), _const((N_MEM, 2 * XW)), _const((SUBLANES, CW)),
                  _const((1, AW)), _const((1, CW)), _const((1, XW)), _const((1, D)), _const((D, D)), tile(D)],
        out_specs=[tile(D), _tokens_in_lanes(TQ), tile(D), tile(D)],
        out_shape=[jax.ShapeDtypeStruct((S, D), F32), jax.ShapeDtypeStruct((D, S), BF16),
                   jax.ShapeDtypeStruct((S, D), F32), jax.ShapeDtypeStruct((S, D), F32)],
        scratch_shapes=[], vmem_mb=56,
        args=(y_attn, bcu, bcu, qx16, kv16, cw8, g_attn, g_conv, g_x, g_post, wout16, x))


def _mlp(x1, tgt, g_pre, g_post, wup8, wdn_halves):
    tq = TQ_MLP
    half = D // 2

    def body(x1_ref, t_ref, g1_ref, g2_ref, wu_ref, wda_ref, wdb_ref,
             a16_ref, du_ref, h2_ref, df2_ref, dx1_ref, loss_ref, dg_ref):
        @pl.when(pl.program_id(0) == 0)
        def _():
            loss_ref[...] = jnp.zeros_like(loss_ref)
            dg_ref[...] = jnp.zeros_like(dg_ref)

        x1 = x1_ref[...]
        g1, g2 = g1_ref[...], g2_ref[...]
        y1, n1, r1 = _rms(x1, g1)
        h2 = y1.astype(BF16)
        h2_ref[...] = h2.T
        f2a = jnp.zeros((tq, half), F32)
        f2b = jnp.zeros((tq, half), F32)
        for j in range(N_DEV):
            cols = slice(j * FF_BLK, (j + 1) * FF_BLK)
            a = jnp.maximum(_dot(h2, wu_ref[j]), 0.0)
            a16_ref[:, cols] = a.astype(BF16)
            f = (a * a).astype(BF16)
            f2a = f2a + _dot(f, wda_ref[cols, :])
            f2b = f2b + _dot(f, wdb_ref[cols, :])
        f2 = jnp.concatenate([f2a, f2b], axis=1)
        y2, n2, r2 = _rms(f2, g2)
        e = x1 + y2 - t_ref[...]
        sq = jnp.sum(jnp.sum(e * e, axis=1, keepdims=True), axis=0, keepdims=True)
        loss_ref[...] += jnp.broadcast_to(sq * (0.5 / D), loss_ref.shape)
        dout = e * (1.0 / D)
        df2, dg2 = _rms_bwd(dout, n2, r2, g2)
        df2_16 = df2.astype(BF16)
        df2_ref[...] = df2_16.T
        dh2 = jnp.zeros((tq, D), F32)
        for j in range(N_DEV):
            cols = slice(j * FF_BLK, (j + 1) * FF_BLK)
            df = _dot_nt(df2_16[:, 0:half], wda_ref[cols, :]) + _dot_nt(df2_16[:, half:], wdb_ref[cols, :])
            du = (df * (2.0 * a16_ref[:, cols].astype(F32))).astype(BF16)
            du_ref[:, cols] = du
            dh2 = dh2 + _dot_nt(du, wu_ref[j])
        dx, dg1 = _rms_bwd(dh2, n1, r1, g1)
        dx1_ref[...] = dout + dx
        dg_ref[0:1, :] += dg2
        dg_ref[1:2, :] += dg1

    def tile(w):
        return pl.BlockSpec((tq, w), lambda i: (i, 0))

    return pl.pallas_call(
        body, grid=(S // tq,), name="mlp",
        in_specs=[tile(D), tile(D), _const((1, D)), _const((1, D)), _const((N_DEV, D, FF_BLK)), _const((FF, half)), _const((FF, half))],
        out_specs=[tile(FF), tile(FF), _tokens_in_lanes(tq), _tokens_in_lanes(tq), tile(D),
                   _acc((SUBLANES, LANES)), _acc((SUBLANES, D))],
        out_shape=[jax.ShapeDtypeStruct((S, FF), BF16), jax.ShapeDtypeStruct((S, FF), BF16),
                   jax.ShapeDtypeStruct((D, S), BF16), jax.ShapeDtypeStruct((D, S), BF16),
                   jax.ShapeDtypeStruct((S, D), F32), jax.ShapeDtypeStruct((SUBLANES, LANES), F32),
                   jax.ShapeDtypeStruct((SUBLANES, D), F32)],
        compiler_params=_cparams(60))(x1, tgt, g_pre, g_post, wup8, *wdn_halves)


def _mix_out_bwd(dx1, y2, ypre, ltot, head_ones, q, bcu, qx16, kv16, cw8, g_post, g_attn, g_conv, g_x, wout16):
    def body(dx1_ref, y2_ref, ypre_ref, lt_ref, e_ref, q_ref, bcu_ref, halo_ref, qx_ref, kv_ref, cw_ref, gp_ref,
             ga_ref, gc_ref, gx_ref, w_ref, dy2_ref, qdo_ref, ld_ref, dbcu_ref, dqx_ref, dgs_ref, dcw_ref, dkv_ref,
             carry):
        i = pl.program_id(0)

        @pl.when(i == 0)
        def _():
            dgs_ref[...] = jnp.zeros_like(dgs_ref)
            dcw_ref[...] = jnp.zeros_like(dcw_ref)
            dkv_ref[...] = jnp.zeros_like(dkv_ref)
            carry[...] = jnp.zeros_like(carry)

        gp = gp_ref[...]
        _, n, r = _rms(y2_ref[...], gp)
        dy2, dgp = _rms_bwd(dx1_ref[...], n, r, gp)
        dy2_16 = dy2.astype(BF16)
        dy2_ref[...] = dy2_16
        dy = _dot_nt(dy2_16, w_ref[...])

        ypre = ypre_ref[...]
        ga, gc, gx = ga_ref[...], gc_ref[...], gx_ref[...]
        _, na, ra = _rms(ypre[:, 0:AW], ga)
        dya, dga = _rms_bwd(dy[:, 0:AW], na, ra, ga)
        _, nc, rc = _rms(ypre[:, AW:AW + CW], gc)
        dyc, dgc = _rms_bwd(dy[:, AW:AW + CW], nc, rc, gc)
        y_x = ypre[:, AW + CW:]
        _, nx, rx = _rms(y_x, gx)
        dyx, dgx = _rms_bwd(dy[:, AW + CW:], nx, rx, gx)
        qdo_ref[...] = _pack_pair(q_ref[...], dya)
        prod = dya * ypre[:, 0:AW]
        hi = prod.astype(BF16)
        lo = (prod - hi.astype(F32)).astype(BF16)
        head_sum = _dot(hi, e_ref[...]) + _dot(lo, e_ref[...])
        lane_a = lax.broadcasted_iota(jnp.int32, prod.shape, 1)
        ld_ref[...] = jnp.where((lane_a % HEAD) < HEAD // 2, lt_ref[...], head_sum)
        dgs_ref[0:1, :] += dgp
        dgs_ref[1:2, :] += jnp.concatenate([dga, dgc, dgx], axis=1)

        bcu = bcu_ref[...]
        b, c, u = bcu[:, 0:CW], bcu[:, CW:2 * CW], bcu[:, 2 * CW:]
        z = c * u
        halo = halo_ref[...]
        zprev = jnp.where(i < NT - 1, halo[:, CW:2 * CW] * halo[:, 2 * CW:], 0.0)
        row = lax.broadcasted_iota(jnp.int32, z.shape, 0)
        z1, z2 = _conv_taps(z, zprev, row)
        cw = cw_ref[...]
        conv = z2 * cw[0:1, :] + z1 * cw[1:2, :] + z * cw[2:3, :]
        dconv = dyc * b
        nxt = carry[...]
        dn1 = jnp.where(row == TQ - 1, nxt[0:1, :], pltpu.roll(dconv, TQ - 1, 0))
        dn2 = jnp.where(row == TQ - 1, nxt[1:2, :], jnp.where(row == TQ - 2, nxt[0:1, :], pltpu.roll(dconv, TQ - 2, 0)))
        carry[...] = dconv[0:SUBLANES, :]
        dz = dconv * cw[2:3, :] + dn1 * cw[1:2, :] + dn2 * cw[0:1, :]
        dbcu_ref[:, 0:CW] = (dyc * conv).astype(BF16)
        dbcu_ref[:, CW:2 * CW] = (dz * u).astype(BF16)
        dbcu_ref[:, 2 * CW:] = (dz * c).astype(BF16)
        dcw_ref[0:1, :] += jnp.sum(z2 * dconv, axis=0, keepdims=True)
        dcw_ref[1:2, :] += jnp.sum(z1 * dconv, axis=0, keepdims=True)
        dcw_ref[2:3, :] += jnp.sum(z * dconv, axis=0, keepdims=True)

        qx = qx_ref[...]
        kv = kv_ref[...]
        km, vm = kv[:, 0:XW], kv[:, XW:]
        lane = lax.broadcasted_iota(jnp.int32, qx.shape, 1)
        dqx = jnp.zeros(qx.shape, F32)
        dkm = jnp.zeros((N_MEM, XW), F32)
        dvm = jnp.zeros((N_MEM, XW), F32)
        for h in range(XW // HEAD):
            hm = (lane >= h * HEAD) & (lane < (h + 1) * HEAD)
            qm = jnp.where(hm, qx, jnp.zeros_like(qx))
            e, l = _xattn_scores(qm, km)
            p = e / l
            dom = jnp.where(hm, dyx, 0.0)
            do16 = dom.astype(BF16)
            dsum = jnp.sum(dom * y_x, axis=1, keepdims=True)
            ds = (p * (_dot_nt(do16, vm) - dsum)).astype(BF16)
            dqx = jnp.where(hm, _dot(ds, km), dqx)
            dkm = dkm + _dot_tn(ds, qm)
            dvm = dvm + _dot_tn(p.astype(BF16), do16)
        dqx_ref[...] = (dqx * SCALE).astype(BF16)
        dkv_ref[:, 0:XW] += dkm
        dkv_ref[:, XW:] += dvm

    def tile(w):
        return pl.BlockSpec((TQ, w), lambda i: (NT - 1 - i, 0))

    halo = pl.BlockSpec((SUBLANES, 3 * CW), lambda i: (jnp.maximum((NT - 1 - i) * (TQ // SUBLANES) - 1, 0), 0))
    return pl.pallas_call(
        body, grid=(NT,), name="mix_out_bwd",
        in_specs=[tile(D), tile(D), tile(D), tile(AW), _const((AW, AW)), tile(AW), tile(3 * CW), halo, tile(XW),
                  _const((N_MEM, 2 * XW)), _const((SUBLANES, CW)), _const((1, D)), _const((1, AW)), _const((1, CW)),
                  _const((1, XW)), _const((D, D))],
        out_specs=[tile(D), tile(AW), tile(AW), tile(3 * CW), tile(XW), _acc((SUBLANES, D)), _acc((SUBLANES, CW)),
                   _acc((N_MEM, 2 * XW))],
        out_shape=[jax.ShapeDtypeStruct((S, D), BF16), jax.ShapeDtypeStruct((S, AW), F32),
                   jax.ShapeDtypeStruct((S, AW), F32),
                   jax.ShapeDtypeStruct((S, 3 * CW), BF16), jax.ShapeDtypeStruct((S, XW), BF16),
                   jax.ShapeDtypeStruct((SUBLANES, D), F32), jax.ShapeDtypeStruct((SUBLANES, CW), F32),
                   jax.ShapeDtypeStruct((N_MEM, 2 * XW), F32)],
        scratch_shapes=[pltpu.VMEM((SUBLANES, CW), F32)],
        compiler_params=_cparams(56))(dx1, y2, ypre, ltot, head_ones, q, bcu, bcu, qx16, kv16, cw8, g_post, g_attn,
                                      g_conv, g_x, wout16)


def _attn_bwd(qdo, kvp, ld, chip_sums=()):
    n_in = 3
    views = [[a] + [a.reshape(S // n, n, AW) for _, n, _, _ in ATTN_PLANS[1:]] for a in (qdo, kvp, ld)]
    flat = [views[a][p] for p in range(3) for a in range(n_in)]
    ns = len(chip_sums)
    n_grid = AW // LANES

    def body(*refs):
        hbm = [refs[n_in * p:n_in * p + n_in] for p in range(3)]
        refs = refs[3 * n_in:]
        sum_refs, refs = refs[:ns], refs[ns:]
        outs = [refs[3 * p:3 * p + 3] for p in range(3)]
        landed_refs, sc = refs[9:9 + ns], refs[9 + ns:]
        bufs = [sc[3 * p:3 * p + 3] for p in range(3)]
        res = [sc[9 + 3 * p:12 + 3 * p] for p in range(3)]
        tab128, tab4, sem_in, sem_out = sc[18:22]
        step = pl.program_id(0)
        if ns:
            start_chips, finish_chips = _chips_steps(sum_refs, landed_refs, *sc[22:])
            _, _, core, chips = _place()
            signal_chips, chips_are_in = _own_barrier([(px, py, core) for px, py in chips])
            pl.when(step == 0)(signal_chips)

            def chips_go():
                chips_are_in()
                start_chips()
        now =[_class_gather(hbm[p], bufs[p], sem_in.at[p], _lanes_of(step)) for p in range(3)]
        nxt = [_class_gather(hbm[p], bufs[p], sem_in.at[p], _lanes_of(step + 1)) for p in range(3)]

        @pl.when(step == 0)
        def _():
            for p in range(3):
                _start(now[p])
                for b in bufs[p]:
                    b[0:PAD, :] = jnp.zeros((PAD, LANES), F32)
            _fill_bias(tab128, 128, False)
            _fill_bias(tab4, 64, True)

        def prefetch(p):
            pl.when(step + 1 < n_grid)(lambda: _start(nxt[p]))

        lane = lax.broadcasted_iota(jnp.int32, (1, LANES), 1)

        def run(plan, plan_bufs, tab, dst):
            _, n_cls, qblk, nbc = plan
            partner = n_cls == 8
            bqdo, bkv, bld = plan_bufs
            rq, rk, rv = dst

            def block(g, carry):
                own, wins, mask = _block_rows(g, qblk, nbc, partner)
                qb, dob = _unpack_pair(bqdo[own, :])
                q2, do2 = _stack_heads(qb, lane), _stack_heads(dob, lane)
                kw, vw = _unpack_pair(_window(bkv, wins))
                ldv = bld[own, :]
                half = HEAD // 2
                lt2 = jnp.concatenate([ldv[:, 0:1], ldv[:, HEAD:HEAD + 1]], axis=0)
                dsum2 = jnp.concatenate([ldv[:, half:half + 1], ldv[:, HEAD + half:HEAD + half + 1]], axis=0)
                p = jnp.exp(_dot_nt(q2, kw) + tab[mask] - lt2)
                ds = (p * (_dot_nt(do2, vw) - dsum2)).astype(BF16)
                rq[own, :] = _unstack_heads(_dot(ds, kw), lane)
                dkw = _dot_tn(ds, q2)
                dvw = _dot_tn(p.astype(BF16), do2)
                n_w = WIN // len(wins)
                for i, w in enumerate(wins):
                    rk[w, :] += dkw[i * n_w:(i + 1) * n_w, :]
                    rv[w, :] += dvw[i * n_w:(i + 1) * n_w, :]
                return carry
            lax.fori_loop(0, n_cls * nbc, block, 0, unroll=ATTN_UNROLL)

        tabs = (tab128, tab4, tab128)
        def drained(p):
            return lambda: _wait(_whole_waits(res[p], sem_out.at[p]))

        for p in range(3):
            pl.when(step > 0)(drained(p))
            for b in res[p][1:]:
                b[...] = jnp.zeros_like(b)
            _wait(_whole_waits(bufs[p], sem_in.at[p]))
            run(ATTN_PLANS[p], bufs[p], tabs[p], res[p])
            prefetch(p)
            _start(_class_scatter(res[p], outs[p], sem_out.at[p], _lanes_of(step)))
            if ns and p == 0:
                pl.when(step == 0)(chips_go)
        for p in range(3):
            pl.when(step == n_grid - 1)(drained(p))
        if ns:
            pl.when(step == n_grid - 1)(finish_chips)

    padded = pltpu.VMEM((PAD + S, LANES), F32)
    shapes = [jax.ShapeDtypeStruct(views[0][p].shape, F32) for p in range(3) for _ in range(3)]
    out = pl.pallas_call(
        body, grid=(n_grid,), name="attn_bwd",
        in_specs=[ANY] * (3 * n_in + ns), out_specs=[ANY] * (9 + ns),
        out_shape=shapes + _chips_shapes(chip_sums),
        scratch_shapes=[padded] * 18
        + [pltpu.VMEM((4, 256, WIN), F32), pltpu.VMEM((4, 128, WIN), F32),
           pltpu.SemaphoreType.DMA((3, n_in)), pltpu.SemaphoreType.DMA((3, 3))]
        + (_chips_scratch(ns) if ns else []),
        compiler_params=_cparams(56, **({"collective_id": ID_ATTN_BWD} if ns else {})))(*flat, *chip_sums)
    return [o.reshape(S, AW) for o in out[:9]] + list(out[9:])


def _in_proj_bwd(dqkv, dbcu, dqx, cos, sins, w16, x, g, dx1):
    tq = TQ // 2

    def body(*refs):
        parts = refs[0:9]
        dbcu_ref, dqx_ref, c_ref, s_ref, w_ref, x_ref, g_ref, dx1_ref, dp_ref, gx_ref, dg_ref = refs[9:]

        @pl.when(pl.program_id(0) == 0)
        def _():
            dg_ref[...] = jnp.zeros_like(dg_ref)

        dq, dk, dv = (parts[i][...] + parts[3 + i][...] + parts[6 + i][...] for i in range(3))
        cos, sn = _all_heads(c_ref[...]), _all_heads(s_ref[...])
        dqr = dq * SCALE
        dkr = dk
        dp = jnp.concatenate([(dqr * cos + _rot_half(dqr * sn)).astype(BF16),
                              (dkr * cos + _rot_half(dkr * sn)).astype(BF16), dv.astype(BF16),
                              dbcu_ref[...], dqx_ref[...]], axis=1)
        dp_ref[...] = dp
        dh = _dot_nt(dp, w_ref[...])
        g = g_ref[...]
        _, n, r = _rms(x_ref[...], g)
        dx, dg = _rms_bwd(dh, n, r, g)
        gx_ref[...] = dx1_ref[...] + dx
        dg_ref[0:1, :] += dg

    def tile(w):
        return pl.BlockSpec((tq, w), lambda i: (i, 0))

    return pl.pallas_call(
        body, grid=(S // tq,), name="in_proj_bwd",
        in_specs=[tile(AW)] * 9 + [tile(3 * CW), tile(XW), tile(LANES), tile(LANES), _const((D, PW)),
                                   tile(D), _const((1, D)), tile(D)],
        out_specs=[tile(PW), tile(D), _acc((SUBLANES, D))],
        out_shape=[jax.ShapeDtypeStruct((S, PW), BF16), jax.ShapeDtypeStruct((S, D), F32),
                   jax.ShapeDtypeStruct((SUBLANES, D), F32)],
        compiler_params=_cparams(56))(*dqkv, dbcu, dqx, cos, sins, w16, x, g, dx1)


def _mem_bwd(mem, g_mem, wkv16, dkv):
    def body(m_ref, g_ref, w_ref, dkv_ref, dkv16_ref, dg_ref):
        dkv16 = dkv_ref[...].astype(BF16)
        dkv16_ref[...] = dkv16
        _, n, _ = _rms(m_ref[...], g_ref[...])
        dg = jnp.sum(_dot_nt(dkv16, w_ref[...]) * n, axis=0, keepdims=True)
        dg_ref[...] = jnp.broadcast_to(dg, dg_ref.shape)

    return pl.pallas_call(
        body, name="mem_bwd",
        out_shape=[jax.ShapeDtypeStruct((N_MEM, 2 * XW), BF16), jax.ShapeDtypeStruct((SUBLANES, D), F32)],
        compiler_params=pltpu.CompilerParams(vmem_limit_bytes=32 << 20))(mem, g_mem, wkv16, dkv)


N_CHIPS = N_DEV // 2


def _pair_scratch(block):
    return [pltpu.VMEM((N_CHIPS,) + block, BF16), pltpu.VMEM((N_CHIPS,) + block, BF16),
            pltpu.SemaphoreType.DMA((N_CHIPS,)), pltpu.SemaphoreType.DMA((N_CHIPS,))]


def _swap_with_sibling(p, stage, land, send, recv):
    x, y, c = lax.axis_index("x"), lax.axis_index("y"), lax.axis_index("c")
    return pltpu.make_async_remote_copy(src_ref=stage.at[p], dst_ref=land.at[p], send_sem=send.at[p],
                                        recv_sem=recv.at[p], device_id=(x, y, 1 - c), device_id_type=MESH)


def _own_barrier(peers):
    sem = pltpu.get_barrier_semaphore()

    def signal():
        for peer in peers:
            pl.semaphore_signal(sem, inc=1, device_id=peer, device_id_type=MESH)

    return signal, lambda: pl.semaphore_wait(sem, len(peers))


def _sibling_barrier():
    x, y, c = lax.axis_index("x"), lax.axis_index("y"), lax.axis_index("c")
    return _own_barrier([(x, y, 1 - c)])


ID_WGRAD_UP, ID_WGRAD_DOWN, ID_WGRAD_ROWS, ID_ROPE_TABLE, ID_IN_PROJ, ID_ATTN_BWD = range(6)


def _wgrad_cols(place, at16, b16, blk, name, square_b=False, transpose_out=False, to_chips=False, small=(),
                sibling_only_id=None):
    m, kk = at16.shape
    assert sibling_only_id is None or not (to_chips or small)
    aligned = blk % LANES == 0
    wide = blk if aligned else -(-(blk + LANES // 2) // LANES) * LANES
    assert aligned or (transpose_out and blk % SUBLANES == 0)
    block = (blk, m) if transpose_out else (m, blk)

    def chip_of(step, my_chip):
        return jnp.bitwise_xor(my_chip, N_CHIPS - 1 - step) if to_chips else step

    def body(pl_ref, a_ref, *refs):
        b_refs, refs = refs[:2 if aligned else 1], refs[2 if aligned else 1:]
        accs, refs = refs[:len(small)], refs[len(small):]
        (cs_ref, own_ref), refs = refs[:2], refs[2:]
        if to_chips:
            landed, refs = refs[0], refs[1:]
        if small:
            tot_ref, refs = refs[0], refs[1:]
        (stage, land, send, recv), refs = refs[:4], refs[4:]
        if not aligned:
            (win, wsem), refs = refs[:2], refs[2:]
        if small:
            start_small, finish_small = _small_reduce_steps(accs, tot_ref, *refs[-4:])
            refs = refs[:-4]
        step = pl.program_id(0)
        if small:
            pl.when(step == 0)(start_small)
        if sibling_only_id is not None:
            signal_sibling, sibling_is_in = _sibling_barrier()
            pl.when(step == 0)(signal_sibling)
        x, y, c = lax.axis_index("x"), lax.axis_index("y"), lax.axis_index("c")
        my_chip = 2 * x + y
        p = chip_of(step, my_chip)

        def fetch(at_step, mine):
            j = 2 * chip_of(at_step, my_chip) + (c if mine else 1 - c)
            first = pl.multiple_of(((j * blk) >> 7) << 7, LANES)
            slot = 2 * (at_step & 1) + mine
            return pltpu.make_async_copy(b_refs[0].at[:, pl.ds(first, wide)], win.at[slot], wsem.at[slot])

        if not aligned:
            @pl.when(step == 0)
            def _():
                fetch(0, 0).start()
                fetch(0, 1).start()

            @pl.when(step + 1 < N_CHIPS)
            def _():
                fetch(step + 1, 0).start()
                fetch(step + 1, 1).start()

        def partial(mine):
            if aligned:
                b = b_refs[mine][...]
                if square_b:
                    b = b * b
                acc = _dot(a_ref[...], b)
            else:
                fetch(step, mine).wait()
                acc = _dot(a_ref[...], win[2 * (step & 1) + mine]).T
                odd = c if mine else 1 - c
                return jnp.where(odd == 0, acc[0:blk], acc[wide - blk:wide])
            return acc.T if transpose_out else acc

        stage[p] = partial(0).astype(BF16)
        if sibling_only_id is not None:
            pl.when(step == 0)(sibling_is_in)
        swap = _swap_with_sibling(p, stage, land, send, recv)
        swap.start()
        mine = partial(1)
        swap.wait()
        total = mine + land[p].astype(F32)
        cs_ref[0] = total.astype(BF16)

        @pl.when(p == my_chip)
        def _():
            own_ref[...] = total

        if to_chips:
            stage2, send2, recv2 = refs
            flipped = jnp.bitwise_xor(p, my_chip)
            k = jnp.where(flipped == 2, 0, jnp.where(flipped == 1, 1, 2))

            def to_owner(src, k_, px, py):
                return pltpu.make_async_remote_copy(src_ref=src, dst_ref=landed.at[k_], send_sem=send2.at[k_],
                                                    recv_sem=recv2.at[k_], device_id=(px, py, c), device_id_type=MESH)

            @pl.when(p != my_chip)
            def _():
                stage2[p] = total.astype(BF16)
                to_owner(stage2.at[p], k, p >> 1, p & 1).start()

            @pl.when(step == N_CHIPS - 1)
            def _():
                for k_ in range(N_CHIPS - 1):
                    to_owner(stage2.at[0], k_, x, y).wait()

        if small:
            pl.when(step == N_CHIPS - 1)(finish_small)

    def b_spec(mine):
        return pl.BlockSpec((kk, blk), lambda i, s: (0, 2 * chip_of(i, s[1]) + (s[0] if mine else 1 - s[0])))

    b_specs, b_args = ([b_spec(0), b_spec(1)], (b16, b16)) if aligned else ([ANY], (b16,))
    scratch = _pair_scratch(block)
    if not aligned:
        scratch += [pltpu.VMEM((4, kk, wide), BF16), pltpu.SemaphoreType.DMA((4,))]
    out_specs = [pl.BlockSpec((1,) + block, lambda i, s: (chip_of(i, s[1]), 0, 0)), pl.BlockSpec(block, lambda i, s: (0, 0))]
    out_shape = [jax.ShapeDtypeStruct((N_CHIPS,) + block, BF16), jax.ShapeDtypeStruct(block, F32)]
    if to_chips:
        out_specs.append(ANY)
        out_shape.append(jax.ShapeDtypeStruct((N_CHIPS - 1,) + block, BF16))
        scratch += [pltpu.VMEM((N_CHIPS,) + block, BF16), pltpu.SemaphoreType.DMA((N_CHIPS - 1,)),
                    pltpu.SemaphoreType.DMA((N_CHIPS - 1,))]
    small_specs = [pl.BlockSpec(a.shape, lambda i, s: (0, 0)) for a in small]
    if small:
        out_specs.append(pl.BlockSpec((PACK_ROWS, D), lambda i, s: (0, 0)))
        out_shape.append(jax.ShapeDtypeStruct((PACK_ROWS, D), F32))
        scratch += _small_reduce_scratch()
    return pl.pallas_call(
        body, name=name,
        grid_spec=pltpu.PrefetchScalarGridSpec(
            num_scalar_prefetch=1, grid=(N_CHIPS,),
            in_specs=[pl.BlockSpec((m, kk), lambda i, s: (0, 0), pipeline_mode=pl.Buffered(1))] + b_specs + small_specs,
            out_specs=out_specs, scratch_shapes=scratch),
        out_shape=out_shape,
        compiler_params=_cparams(56, **({} if sibling_only_id is None else {"collective_id": sibling_only_id})),
    )(place, at16, *b_args, *small)


ROWS_STEPS = 4


def _wgrad_rows(place, products, name):
    n_prod = len(products)
    dims = [(at16.shape[0], at16.shape[1], b16.shape[1]) for at16, b16 in products]
    cut = [kk % (ROWS_STEPS * LANES) == 0 for _, kk, _ in dims]
    blocks = [(m // N_DEV, n) for m, _, n in dims]

    def body(pl_ref, *refs):
        ins, outs, scratch = refs[:2 * n_prod], refs[2 * n_prod:4 * n_prod], refs[4 * n_prod:]
        c, step = pl_ref[0], pl.program_id(0)

        def multiply(i):
            a_ref, b_ref, acc = ins[2 * i], ins[2 * i + 1], scratch[5 * i]

            @pl.when(step == 0)
            def _():
                acc[...] = _dot(a_ref[...], b_ref[...])

            if cut[i]:
                @pl.when(step > 0)
                def _():
                    acc[...] += _dot(a_ref[...], b_ref[...])

        def rows(i, owner):
            return pl.ds(pl.multiple_of(owner * blocks[i][0], blocks[i][0]), blocks[i][0])

        def send_sibling_side(i):
            acc, stage, land, send, recv = scratch[5 * i:5 * i + 5]
            swaps = []
            for p in range(N_CHIPS):
                stage[p] = acc[rows(i, 2 * p + 1 - c), :].astype(BF16)
                swaps.append(_swap_with_sibling(p, stage, land, send, recv))
                swaps[-1].start()
            return swaps

        def add_my_side(i, swaps):
            acc, land = scratch[5 * i], scratch[5 * i + 2]
            cs_ref, own_ref = outs[2 * i:2 * i + 2]
            for p in range(N_CHIPS):
                swaps[p].wait()
                total = acc[rows(i, 2 * p + c), :] + land[p].astype(F32)
                cs_ref[p] = total.astype(BF16)

                @pl.when(p == pl_ref[1])
                def _():
                    own_ref[...] = total

        signal_sibling, sibling_is_in = _sibling_barrier()
        pl.when(step == 0)(signal_sibling)
        for i in range(n_prod):
            multiply(i)

        @pl.when(step == ROWS_STEPS - 1)
        def _():
            sibling_is_in()
            swaps = [send_sibling_side(i) for i in range(n_prod)]
            for i in range(n_prod):
                add_my_side(i, swaps[i])

    in_specs, out_specs, out_shape, scratch = [pl.BlockSpec(memory_space=pltpu.SMEM)], [], [], []
    for (m, kk, n), cut_i, block in zip(dims, cut, blocks):
        chunk = kk // ROWS_STEPS
        in_specs += ([pl.BlockSpec((m, chunk), lambda i: (0, i)), pl.BlockSpec((chunk, n), lambda i: (i, 0))]
                     if cut_i else [_const((m, kk)), _const((kk, n))])
        out_specs += [_acc((N_CHIPS,) + block), _acc(block)]
        out_shape += [jax.ShapeDtypeStruct((N_CHIPS,) + block, BF16), jax.ShapeDtypeStruct(block, F32)]
        scratch += [pltpu.VMEM((m, n), F32)] + _pair_scratch(block)
    out = pl.pallas_call(
        body, grid=(ROWS_STEPS,), name=name, in_specs=in_specs, out_specs=out_specs, out_shape=out_shape,
        scratch_shapes=scratch, compiler_params=_cparams(56, collective_id=ID_WGRAD_ROWS),
    )(place, *[a for pair in products for a in pair])
    return [tuple(out[2 * i:2 * i + 2]) for i in range(n_prod)]


def _adamw_math(w, g, m, v):
    m = ADAM_B1 * m + (1.0 - ADAM_B1) * g
    v = ADAM_B2 * v + (1.0 - ADAM_B2) * jnp.square(g)
    m_hat = m / (1.0 - ADAM_B1 ** ADAM_STEP)
    v_hat = v / (1.0 - ADAM_B2 ** ADAM_STEP)
    delta = -ADAM_LR * (m_hat / (jnp.sqrt(v_hat) + ADAM_EPS) + ADAM_WD * w)
    return delta, m, v


def _adamw_shards(updates, name, chip_sums=()):
    names, nu, ns = list(updates), len(updates), len(chip_sums)

    def body(*refs):
        ins, sum_refs = refs[:5 * nu], refs[5 * nu:5 * nu + ns]
        outs = refs[5 * nu + ns:9 * nu + ns]
        landed_refs, scratch = refs[9 * nu + ns:9 * nu + 2 * ns], refs[9 * nu + 2 * ns:]
        if ns:
            start_chips, finish_chips = _chips_steps(sum_refs, landed_refs, *scratch)
            start_chips()
        for i in range(nu):
            o_ref, r_ref, w_ref, m_ref, v_ref = ins[5 * i:5 * i + 5]
            g_out, d_out, m_out, v_out = outs[4 * i:4 * i + 4]
            g = o_ref[...] + r_ref[0].astype(F32) + r_ref[1].astype(F32) + r_ref[2].astype(F32)
            g_out[...] = g
            d_out[...], m_out[...], v_out[...] = _adamw_math(w_ref[...], g, m_ref[...], v_ref[...])
        if ns:
            finish_chips()

    vmem = pl.BlockSpec(memory_space=pltpu.VMEM)
    out = pl.pallas_call(
        body, name=name,
        in_specs=[vmem] * (5 * nu) + [ANY] * ns, out_specs=[vmem] * (4 * nu) + [ANY] * ns,
        out_shape=[jax.ShapeDtypeStruct(updates[n][2].shape, F32) for n in names for _ in range(4)]
        + _chips_shapes(chip_sums),
        scratch_shapes=_chips_scratch(ns) if ns else [],
        compiler_params=pltpu.CompilerParams(vmem_limit_bytes=56 << 20),
    )(*[a for n in names for a in updates[n]], *chip_sums)
    return {n: out[4 * i:4 * i + 4] for i, n in enumerate(names)}, list(out[4 * nu:])


def _place():
    x, y, c = lax.axis_index("x"), lax.axis_index("y"), lax.axis_index("c")
    chips = [(1 - x, y), (x, 1 - y), (1 - x, 1 - y)]
    return x, y, c, chips


def _gather_steps(ins, outs, send, recv, lsem, own_barrier=True):
    nt = len(ins)
    x, y, c, (xn, yn, diag) = _place()
    me, sib = (x, y, c), (x, y, 1 - c)

    def slot(t, px, py, pc):
        return outs[t].at[4 * px + 2 * py + pc]

    def copy(t, k, block, to, src=None):
        return pltpu.make_async_remote_copy(
            src_ref=slot(t, *block) if src is None else src, dst_ref=slot(t, *block),
            send_sem=send.at[t, k], recv_sem=recv.at[t, k], device_id=to, device_id_type=MESH)

    mine = [pltpu.make_async_copy(ins[t], slot(t, *me), lsem.at[t]) for t in range(nt)]
    first = [copy(t, k, me, to, src=ins[t]) for t in range(nt) for k, to in ((0, sib), (1, (*xn, c)), (2, (*yn, c)))]

    if own_barrier:
        signal_peers, peers_are_in = _own_barrier([sib, (*xn, c), (*yn, c)])

    def start():
        if own_barrier:
            signal_peers()
        for cp in mine:
            cp.start()
        if own_barrier:
            peers_are_in()
        for cp in first:
            cp.start()

    def landed(k, chip, also_to=None):
        for t in range(nt):
            copy(t, k, (*chip, c), me).wait_recv()
            if also_to is not None:
                copy(t, 3, (*chip, c), (*also_to, c)).start()
            copy(t, 3 + k, (*chip, c), sib).start()

    def relay():
        @pl.when(c == 0)
        def _():
            landed(1, xn, also_to=yn)
            landed(2, yn)

        @pl.when(c == 1)
        def _():
            landed(2, yn, also_to=xn)
            landed(1, xn)

    def finish():
        landed(3, diag)
        for t in range(nt):
            copy(t, 0, sib, me).wait_recv()
            for k, chip in ((4, xn), (5, yn), (6, diag)):
                copy(t, k, (*chip, 1 - c), me).wait_recv()
            for k in range(7):
                copy(t, k, me, sib).wait_send()
        for cp in mine:
            cp.wait()

    return start, relay, finish


def _gather_scratch(nt):
    return [pltpu.SemaphoreType.DMA((nt, 7)), pltpu.SemaphoreType.DMA((nt, 7)), pltpu.SemaphoreType.DMA((nt,))]


def _gathered_shapes(shards):
    return [jax.ShapeDtypeStruct((N_DEV,) + s.shape, s.dtype) for s in shards]


def _call_with_gather(body, n_grid, shards, *, name, in_specs, out_specs, out_shape, scratch_shapes, vmem_mb, args,
                      collective_id=None):
    assert (collective_id is None) == (not shards)
    ng, n_in, n_out = len(shards), len(in_specs), len(out_specs)

    def wrapped(*refs):
        ins, shard_refs = refs[:n_in], refs[n_in:n_in + ng]
        outs = refs[n_in + ng:n_in + ng + n_out]
        whole_refs = refs[n_in + ng + n_out:n_in + 2 * ng + n_out]
        scratch = refs[n_in + 2 * ng + n_out:]
        if ng:
            start, relay, finish = _gather_steps(shard_refs, whole_refs, *scratch[len(scratch_shapes):])
            pl.when(pl.program_id(0) == 0)(start)
            pl.when(pl.program_id(0) == n_grid // 2)(relay)
        body(*ins, *outs, *scratch[:len(scratch_shapes)])
        if ng:
            pl.when(pl.program_id(0) == n_grid - 1)(finish)

    return pl.pallas_call(
        wrapped, grid=(n_grid,), name=name,
        in_specs=list(in_specs) + [ANY] * ng, out_specs=list(out_specs) + [ANY] * ng,
        out_shape=list(out_shape) + _gathered_shapes(shards),
        scratch_shapes=list(scratch_shapes) + (_gather_scratch(ng) if ng else []),
        compiler_params=_cparams(vmem_mb, **({"collective_id": collective_id} if shards else {})))(*args, *shards)


def _chips_steps(ins, outs, send, recv):
    _, _, c, chips = _place()
    copies = [pltpu.make_async_remote_copy(
        src_ref=ins[t].at[2 * px + py], dst_ref=outs[t].at[j], send_sem=send.at[t, j], recv_sem=recv.at[t, j],
        device_id=(px, py, c), device_id_type=MESH) for t in range(len(ins)) for j, (px, py) in enumerate(chips)]

    def start():
        for cp in copies:
            cp.start()

    def finish():
        for cp in copies:
            cp.wait()

    return start, finish


def _chips_scratch(nt):
    return [pltpu.SemaphoreType.DMA((nt, 3)), pltpu.SemaphoreType.DMA((nt, 3))]


def _chips_shapes(cs16s):
    return [jax.ShapeDtypeStruct((3,) + g.shape[1:], g.dtype) for g in cs16s]


SMALL = (("g_pre_mix", 0, 0, D), ("g_mem", 1, 0, D), ("g_post_mix", 2, 0, D), ("g_attn_out", 3, 0, AW),
         ("g_conv_out", 3, AW, CW), ("g_xattn_out", 3, AW + CW, XW), ("g_post_mlp", 4, 0, D), ("g_pre_mlp", 5, 0, D))
CONV_ROW = 8
PACK_ROWS = 16


LOSS_ROW = 15


def _small_reduce_steps(accs, tot_ref, pack, land, send, recv):
    acc_in, acc_mem, acc_mix, acc_mlp, acc_cw, acc_loss = accs
    x, y, c, _ = _place()
    me = 4 * x + 2 * y + c
    copies = []
    for k in range(1, N_DEV):
        kx, ky, kc = (k >> 2) & 1, (k >> 1) & 1, k & 1
        peer = (1 - x if kx else x, 1 - y if ky else y, 1 - c if kc else c)
        copies.append(pltpu.make_async_remote_copy(
            src_ref=pack, dst_ref=land.at[me], send_sem=send.at[k - 1], recv_sem=recv.at[k - 1],
            device_id=peer, device_id_type=MESH))

    def start():
        pack[...] = jnp.zeros_like(pack)
        pack[0:1, :] = acc_in[0:1, :]
        pack[1:2, :] = acc_mem[0:1, :]
        pack[2:4, :] = acc_mix[0:2, :]
        pack[4:6, :] = acc_mlp[0:2, :]
        pack[CONV_ROW:CONV_ROW + 3, 0:CW] = acc_cw[0:3, :]
        pack[LOSS_ROW:LOSS_ROW + 1, 0:LANES] = acc_loss[0:1, :]
        land[me] = pack[...]
        for cp in copies:
            cp.start()

    def finish():
        for cp in copies:
            cp.wait()
        tot = land[0]
        for s in range(1, N_DEV):
            tot = tot + land[s]
        tot_ref[...] = tot

    return start, finish


def _small_reduce_scratch():
    return [pltpu.VMEM((PACK_ROWS, D), F32), pltpu.VMEM((N_DEV, PACK_ROWS, D), F32),
            pltpu.SemaphoreType.DMA((N_DEV - 1,)), pltpu.SemaphoreType.DMA((N_DEV - 1,))]


def _small_update(tot, me, params):
    flat = [a for n, _, _, _ in SMALL for a in params[n]] + list(params["conv_w"])
    n_par = len(SMALL) + 1
    tap_cols = CW // N_DEV

    def body(*refs):
        me_ref, tot_ref = refs[0:2]
        ins = refs[2:2 + 3 * n_par]
        loss_out = refs[2 + 3 * n_par]
        outs = refs[3 + 3 * n_par:]
        tot = tot_ref[...]
        loss_out[...] = jnp.broadcast_to(tot[LOSS_ROW:LOSS_ROW + 1, 0:LANES], loss_out.shape)

        def update(i, g):
            w_ref, m_ref, v_ref = ins[3 * i:3 * i + 3]
            for o_ref, res in zip(outs[4 * i:4 * i + 4], (g,) + _adamw_math(w_ref[...], g, m_ref[...], v_ref[...])):
                if len(o_ref.shape) == 3:
                    for t in range(o_ref.shape[0]):
                        o_ref[t] = res[t:t + 1, :]
                else:
                    o_ref[...] = res

        for i, (_, row, lane0, width) in enumerate(SMALL):
            update(i, tot[row:row + 1, lane0:lane0 + width])
        me = me_ref[0]
        taps = pltpu.roll(tot[CONV_ROW:CONV_ROW + SUBLANES, 0:CW], jnp.where(me == 0, 0, CW - me * tap_cols), 1)
        update(n_par - 1, taps[0:3, 0:tap_cols])

    shapes = [jax.ShapeDtypeStruct(params[n][0].shape, F32) for n, _, _, _ in SMALL] + [
        jax.ShapeDtypeStruct((3, 1, tap_cols), F32)]
    vmem = pl.BlockSpec(memory_space=pltpu.VMEM)
    loss, *out = pl.pallas_call(
        body, name="small_update",
        in_specs=[pl.BlockSpec(memory_space=pltpu.SMEM)] + [vmem] * (1 + 3 * n_par),
        out_shape=[jax.ShapeDtypeStruct((SUBLANES, LANES), F32)] + [s for s in shapes for _ in range(4)],
    )(me, tot, *flat)
    names = [n for n, _, _, _ in SMALL] + ["conv_w"]
    return loss[0, 0], {n: out[4 * i:4 * i + 4] for i, n in enumerate(names)}


def _local_step(x, mem, pos, gains, shards, tgt, place):
    half = HEAD // 2
    inv_freq = jnp.float32(ROPE_THETA) ** (-(jnp.arange(half, dtype=F32) * 2.0 / HEAD))
    invf = jnp.tile(inv_freq, LANES // half)[None, :]
    sgn = jnp.tile(jnp.concatenate([-jnp.ones((half,), F32), jnp.ones((half,), F32)]), LANES // HEAD)[None, :]
    cos, sins, win8 = _rope_table(pos.astype(F32).reshape(S, 1), invf, sgn, [shards["w_in"]])
    wdn_left, wdn_right = shards["w_down"][:, 0:D // 2], shards["w_down"][:, D // 2:]
    q, kvp, bcu, qx16, h16, win16, wout8, wkv8, conv8, wdn8_right = _in_proj(
        x, gains["g_pre_mix"], win8, cos, sins, [shards["w_out"], shards["w_mem_kv"], shards["conv_w"], wdn_right])
    wout16, wkv16 = wout8.reshape(D, D), wkv8.reshape(D, 2 * XW)
    cw_full = conv8[:, 0:3, 0:CW // N_DEV].transpose(1, 0, 2).reshape(3, CW)
    cw8 = jnp.zeros((SUBLANES, CW), F32).at[0:3].set(cw_full)
    y_attn, ltot, wup8, wdn8_left = _attn_fwd(q, kvp, [shards["w_up"], wdn_left])
    wdn_halves = (wdn8_left.reshape(FF, D // 2), wdn8_right.reshape(FF, D // 2))
    memn16, kv16 = _mem_fwd(mem, gains["g_mem"], wkv16)
    ypre, y16, y2, x1 = _mix_out(y_attn, bcu, qx16, kv16, cw8, gains["g_attn_out"], gains["g_conv_out"],
                                 gains["g_xattn_out"], gains["g_post_mix"], wout16, x, [])
    a16, du16, h2_16, df2_16, dx1, loss8, dg_mlp = _mlp(
        x1, tgt, gains["g_pre_mlp"], gains["g_post_mlp"], wup8, wdn_halves)

    sums = {"w_up": _wgrad_cols(place, h2_16, du16, FF_BLK, "wgrad_up", sibling_only_id=ID_WGRAD_UP),
            "w_down": _wgrad_cols(place, df2_16, a16, FF_BLK, "wgrad_down", square_b=True, transpose_out=True,
                                  sibling_only_id=ID_WGRAD_DOWN)}

    head_id = jnp.arange(AW, dtype=jnp.int32) // HEAD
    head_ones = (head_id[:, None] == head_id[None, :]).astype(BF16)
    dy2_16, qdo, ld, dbcu, dqx, dgs, dcw, dkv = _mix_out_bwd(
        dx1, y2, ypre, ltot, head_ones, q, bcu, qx16, kv16, cw8, gains["g_post_mix"], gains["g_attn_out"],
        gains["g_conv_out"], gains["g_xattn_out"], wout16)
    dkv16, dg_mem = _mem_bwd(mem, gains["g_mem"], wkv16, dkv)
    sums["w_mem_kv"], sums["w_out"] = _wgrad_rows(place, [(memn16, dkv16), (y16, dy2_16)], "wgrad_mem_kv_out")
    out = _attn_bwd(qdo, kvp, ld, [s[0] for s in sums.values()])
    dqkv, landed = out[:9], out[9:]
    reduced = {n: (s[1], landed[t]) for t, (n, s) in enumerate(sums.items())}
    dproj16, grad_x, dg_in = _in_proj_bwd(dqkv, dbcu, dqx, cos, sins, win16, x, gains["g_pre_mix"], dx1)

    _, in_own, in_landed, small_tot = _wgrad_cols(place, h16, dproj16, PW // N_DEV, "wgrad_in", transpose_out=True,
                                                  to_chips=True, small=(dg_in, dg_mem, dgs, dg_mlp, dcw, loss8))
    reduced["w_in"] = (in_own, in_landed)
    return grad_x, reduced, small_tot


BIG = ("w_in", "w_mem_kv", "w_out", "w_up", "w_down")
ORDER = ("g_pre_mix", "g_mem", "w_in", "w_mem_kv", "conv_w", "g_attn_out", "g_conv_out", "g_xattn_out", "w_out",
         "g_post_mix", "g_pre_mlp", "w_up", "w_down", "g_post_mlp")


def kernel(x, mem, positions, g_pre_mix, g_mem, w_in, w_mem_kv, conv_w, g_attn_out, g_conv_out, g_xattn_out, w_out, g_post_mix, g_pre_mlp, w_up, w_down, g_post_mlp, loss_target, m_g_pre_mix, m_g_mem, m_w_in, m_w_mem_kv, m_conv_w, m_g_attn_out, m_g_conv_out, m_g_xattn_out, m_w_out, m_g_post_mix, m_g_pre_mlp, m_w_up, m_w_down, m_g_post_mlp, v_g_pre_mix, v_g_mem, v_w_in, v_w_mem_kv, v_conv_w, v_g_attn_out, v_g_conv_out, v_g_xattn_out, v_w_out, v_g_post_mix, v_g_pre_mlp, v_w_up, v_w_down, v_g_post_mlp):
    w = dict(g_pre_mix=g_pre_mix, g_mem=g_mem, w_in=w_in, w_mem_kv=w_mem_kv, conv_w=conv_w, g_attn_out=g_attn_out,
             g_conv_out=g_conv_out, g_xattn_out=g_xattn_out, w_out=w_out, g_post_mix=g_post_mix, g_pre_mlp=g_pre_mlp,
             w_up=w_up, w_down=w_down, g_post_mlp=g_post_mlp)
    mo = dict(g_pre_mix=m_g_pre_mix, g_mem=m_g_mem, w_in=m_w_in, w_mem_kv=m_w_mem_kv, conv_w=m_conv_w,
              g_attn_out=m_g_attn_out, g_conv_out=m_g_conv_out, g_xattn_out=m_g_xattn_out, w_out=m_w_out,
              g_post_mix=m_g_post_mix, g_pre_mlp=m_g_pre_mlp, w_up=m_w_up, w_down=m_w_down, g_post_mlp=m_g_post_mlp)
    vo = dict(g_pre_mix=v_g_pre_mix, g_mem=v_g_mem, w_in=v_w_in, w_mem_kv=v_w_mem_kv, conv_w=v_conv_w,
              g_attn_out=v_g_attn_out, g_conv_out=v_g_conv_out, g_xattn_out=v_g_xattn_out, w_out=v_w_out,
              g_post_mix=v_g_post_mix, g_pre_mlp=v_g_pre_mlp, w_up=v_w_up, w_down=v_w_down, g_post_mlp=v_g_post_mlp)

    xi, yi, ci = lax.axis_index("x"), lax.axis_index("y"), lax.axis_index("c")
    me = 4 * xi + 2 * yi + ci
    place = jnp.stack([ci, 2 * xi + yi]).astype(jnp.int32)

    shards = {n: w[n][0].astype(BF16) for n in BIG}
    shards["conv_w"] = jnp.zeros((SUBLANES, LANES), F32).at[0:3, 0:CW // N_DEV].set(conv_w[0])

    gains = {n: w[n] for n, _, _, _ in SMALL}
    grad_x, reduced, small_tot = _local_step(x[0], mem[0], positions[0], gains, shards, loss_target[0], place)

    def shard(n, a):
        return a[0].T if n == "w_in" else a[0]

    updated = {}
    for group in (("w_up", "w_down"), ("w_in", "w_out", "w_mem_kv")):
        updated.update(_adamw_shards({n: (*reduced[n], shard(n, w[n]), shard(n, mo[n]), shard(n, vo[n]))
                                      for n in group}, "adamw_" + "_".join(group))[0])
    grad, delta, new_m, new_v = {}, {}, {}, {}
    for n, res in updated.items():
        grad[n], delta[n], new_m[n], new_v[n] = [(a.T if n == "w_in" else a)[None] for a in res]

    params = {n: (w[n], mo[n], vo[n]) for n, _, _, _ in SMALL}
    params["conv_w"] = (w["conv_w"][0], mo["conv_w"][0], vo["conv_w"][0])
    loss, small = _small_update(small_tot, me.reshape(1).astype(jnp.int32), params)
    for n, (g, d_, m_, v_) in small.items():
        lead = (lambda a: a.reshape(conv_w.shape)) if n == "conv_w" else (lambda a: a)
        grad[n], delta[n], new_m[n], new_v[n] = lead(g), lead(d_), lead(m_), lead(v_)

    return (loss, grad_x[None], *[grad[n] for n in ORDER], *[delta[n] for n in ORDER],
            *[new_m[n] for n in ORDER], *[new_v[n] for n in ORDER])
```

```python
import jax
import jax.numpy as jnp
from jax import lax
from jax.experimental import pallas as pl
from jax.experimental.pallas import tpu as pltpu

F32, BF16 = jnp.float32, jnp.bfloat16
MESH = pl.DeviceIdType.MESH
ANY = pl.BlockSpec(memory_space=pl.ANY)

N_DEV = 8
D = 1024
S = 4096
N_MEM = 256
HEAD = 64
AW, CW, XW = 512, 256, 256
PW = 3 * AW + 3 * CW + XW
FF = 4096
FF_BLK = FF // N_DEV
EPS = 1e-6
NEG = -1e30
SCALE = HEAD ** -0.5
ROPE_THETA = 10000.0
LANES = 128
SUBLANES = 8

ADAM_LR, ADAM_B1, ADAM_B2, ADAM_EPS, ADAM_WD, ADAM_STEP = 0.001, 0.9, 0.999, 1e-08, 0.01, 10

TQ = 512
TQ_MLP = 512
NT = S // TQ


def _cparams(vmem_mb, n_grid=1, **more):
    return pltpu.CompilerParams(dimension_semantics=("arbitrary",) * n_grid, vmem_limit_bytes=vmem_mb << 20, **more)


def _const(shape):
    nd = len(shape)
    return pl.BlockSpec(shape, lambda *_: (0,) * nd, pipeline_mode=pl.Buffered(1))


def _acc(shape):
    nd = len(shape)
    return pl.BlockSpec(shape, lambda *_: (0,) * nd)


def _tokens_in_lanes(tq):
    return pl.BlockSpec((D, tq), lambda i: (0, i))


def _dot(a, b):
    return jnp.dot(a, b, preferred_element_type=F32)


def _dot_nt(a, b):
    return lax.dot_general(a, b, (((1,), (1,)), ((), ())), preferred_element_type=F32)


def _dot_tn(a, b):
    return lax.dot_general(a, b, (((0,), (0,)), ((), ())), preferred_element_type=F32)


def _rms(x, g):
    r = lax.rsqrt(jnp.mean(x * x, axis=-1, keepdims=True) + EPS)
    n = x * r
    return n * g, n, r


def _rms_bwd(dy, n, r, g):
    dn = dy * g
    dx = r * (dn - n * jnp.mean(dn * n, axis=-1, keepdims=True))
    return dx, jnp.sum(dy * n, axis=0, keepdims=True)


def _rot_half(t):
    lane = lax.broadcasted_iota(jnp.int32, t.shape, 1)
    n = t.shape[1]
    return jnp.where((lane % HEAD) < HEAD // 2, pltpu.roll(t, n - HEAD // 2, 1), pltpu.roll(t, HEAD // 2, 1))


def _rope_table(pos_col, invf, sgn, shards):
    def body(p_ref, f_ref, s_ref, c_out, s_out):
        ang = p_ref[...] * f_ref[...]
        c_out[...] = jnp.cos(ang)
        s_out[...] = jnp.sin(ang) * s_ref[...]

    tile = pl.BlockSpec((TQ, LANES), lambda i: (i, 0))
    return _call_with_gather(
        body, NT, shards, name="rope_table",
        in_specs=[pl.BlockSpec((TQ, 1), lambda i: (i, 0)), _const((1, LANES)), _const((1, LANES))],
        out_specs=[tile, tile], out_shape=[jax.ShapeDtypeStruct((S, LANES), F32)] * 2,
        scratch_shapes=[], vmem_mb=32, args=(pos_col, invf, sgn), collective_id=ID_ROPE_TABLE)


def _all_heads(t):
    return jnp.tile(t, (1, AW // LANES))


def _mem_fwd(mem, g_mem, wkv16):
    def body(m_ref, g_ref, w_ref, n16_ref, kv_ref):
        y, _, _ = _rms(m_ref[...], g_ref[...])
        y16 = y.astype(BF16)
        n16_ref[...] = y16.T
        kv_ref[...] = _dot(y16, w_ref[...]).astype(BF16)

    return pl.pallas_call(
        body, name="mem_fwd",
        out_shape=[jax.ShapeDtypeStruct((D, N_MEM), BF16), jax.ShapeDtypeStruct((N_MEM, 2 * XW), BF16)],
        compiler_params=pltpu.CompilerParams(vmem_limit_bytes=32 << 20))(mem, g_mem, wkv16)


def _in_proj(x, g, w8, cos, sins, shards):
    blk = PW // N_DEV

    def body(x_ref, g_ref, w8_ref, c_ref, s_ref, q_ref, kv_ref, bcu_ref, qx_ref, h_ref, w_out, w_ref):
        @pl.when(pl.program_id(0) == 0)
        def _():
            for j in range(N_DEV):
                w_ref[:, j * blk:(j + 1) * blk] = w8_ref[j]
            w_out[...] = w_ref[...]

        y, _, _ = _rms(x_ref[...], g_ref[...])
        h = y.astype(BF16)
        h_ref[...] = h.T
        proj = _dot(h, w_ref[...])
        cos, sn = _all_heads(c_ref[...]), _all_heads(s_ref[...])
        q, k = proj[:, 0:AW], proj[:, AW:2 * AW]
        q_ref[...] = (q * cos + _rot_half(q) * sn) * SCALE
        kv_ref[...] = _pack_pair(k * cos + _rot_half(k) * sn, proj[:, 2 * AW:3 * AW])
        bcu_ref[...] = proj[:, 3 * AW:3 * AW + 3 * CW]
        qx_ref[...] = (proj[:, 3 * AW + 3 * CW:] * SCALE).astype(BF16)

    def tile(w):
        return pl.BlockSpec((TQ, w), lambda i: (i, 0))

    return _call_with_gather(
        body, NT, shards, name="in_proj",
        in_specs=[tile(D), _const((1, D)), _const((N_DEV, D, blk)), tile(LANES), tile(LANES)],
        out_specs=[tile(AW), tile(AW), tile(3 * CW), tile(XW), _tokens_in_lanes(TQ), _acc((D, PW))],
        out_shape=[jax.ShapeDtypeStruct((S, AW), F32)] * 2 + [
            jax.ShapeDtypeStruct((S, 3 * CW), F32), jax.ShapeDtypeStruct((S, XW), BF16),
            jax.ShapeDtypeStruct((D, S), BF16), jax.ShapeDtypeStruct((D, PW), BF16)],
        scratch_shapes=[pltpu.VMEM((D, PW), BF16)], vmem_mb=56, args=(x, g, w8, cos, sins),
        collective_id=ID_IN_PROJ)


ATTN_PLANS = (("p1", 1, 128, 32), ("p4", 8, 64, 8), ("p16", 16, 128, 2))
PAD = 128
WIN = 256


ATTN_UNROLL = 16


def _fill_bias(tab, qblk, partner):
    qi = lax.broadcasted_iota(jnp.int32, (2 * qblk, WIN), 0) & (qblk - 1)
    kj = lax.broadcasted_iota(jnp.int32, (2 * qblk, WIN), 1)
    piece = kj >> (qblk.bit_length() - 1)
    kk = kj & (qblk - 1)
    prev = (piece & 1) == 0
    of_partner = piece >= 2
    for first in (0, 1):
        for par in (0, 1):
            lo = jnp.where(prev, (qblk if first else qi) + jnp.where(of_partner, par, 0), 0)
            hi = jnp.where(prev, qblk, qi + jnp.where(of_partner, par - 1, 0))
            tab[2 * first + par] = jnp.where((kk >= lo) & (kk <= hi), 0.0, NEG).astype(F32)


def _block_rows(g, qblk, nbc, partner):
    own = pl.ds(pl.multiple_of(PAD + g * qblk, qblk), qblk)
    first = ((g & (nbc - 1)) == 0).astype(jnp.int32)
    if partner:
        gp = jnp.bitwise_xor(g, 4 * nbc)
        wins = (pl.ds(pl.multiple_of(PAD + (g - 1) * qblk, qblk), 2 * qblk),
                pl.ds(pl.multiple_of(PAD + (gp - 1) * qblk, qblk), 2 * qblk))
        return own, wins, 2 * first + ((g >> ((4 * nbc).bit_length() - 1)) & 1)
    return own, (pl.ds(pl.multiple_of(PAD + (g - 1) * qblk, qblk), 2 * qblk),), 2 * first


def _pack_pair(lo, hi):
    lo_bits = lax.bitcast_convert_type(lo.astype(BF16).astype(F32), jnp.uint32) >> 16
    hi_bits = lax.bitcast_convert_type(hi.astype(BF16).astype(F32), jnp.uint32) & jnp.uint32(0xFFFF0000)
    return lax.bitcast_convert_type(hi_bits | lo_bits, F32)


def _unpack_pair(c):
    bits = lax.bitcast_convert_type(c, jnp.uint32)
    lo = lax.bitcast_convert_type(bits << 16, F32).astype(BF16)
    hi = lax.bitcast_convert_type(bits & jnp.uint32(0xFFFF0000), F32).astype(BF16)
    return lo, hi


def _window(ref, wins):
    parts = [ref[w, :] for w in wins]
    return parts[0] if len(parts) == 1 else jnp.concatenate(parts, axis=0)


def _stack_heads(t, lane):
    zero = jnp.zeros_like(t)
    return jnp.concatenate([jnp.where(lane < HEAD, t, zero), jnp.where(lane >= HEAD, t, zero)], axis=0)


def _unstack_heads(t2, lane):
    half = t2.shape[0] // 2
    return jnp.where(lane < HEAD, t2[0:half, :], t2[half:, :])


def _lanes_of(step):
    return pl.ds(pl.multiple_of(step * LANES, LANES), LANES)


def _whole_wait(buf, sem):
    whole = buf.at[pl.ds(PAD, S), :]
    return pltpu.make_async_copy(whole, whole, sem)


def _whole_waits(bufs, sems):
    return [_whole_wait(buf, sems.at[i]) for i, buf in enumerate(bufs)]


def _class_gather(views, bufs, sems, lanes):
    copies = []
    for i, (view, buf) in enumerate(zip(views, bufs)):
        if view.ndim == 2:
            copies.append(pltpu.make_async_copy(view.at[:, lanes], buf.at[pl.ds(PAD, S), :], sems.at[i]))
        else:
            per, n_cls = view.shape[0], view.shape[1]
            copies += [pltpu.make_async_copy(view.at[:, c, lanes], buf.at[pl.ds(PAD + c * per, per), :], sems.at[i])
                       for c in range(n_cls)]
    return copies


def _class_scatter(bufs, dsts, sems, lanes):
    copies = []
    for i, (buf, dst) in enumerate(zip(bufs, dsts)):
        if dst.ndim == 2:
            copies.append(pltpu.make_async_copy(buf.at[pl.ds(PAD, S), :], dst.at[:, lanes], sems.at[i]))
            continue
        per, n_cls = dst.shape[0], dst.shape[1]
        copies += [pltpu.make_async_copy(buf.at[pl.ds(PAD + c * per, per), :], dst.at[:, c, lanes], sems.at[i])
                   for c in range(n_cls)]
    return copies


def _start(copies):
    for cp in copies:
        cp.start()


def _wait(waits):
    for w in waits:
        w.wait()


def _attn_fwd(q, kvp, shards=()):
    views = [[a] + [a.reshape(S // n, n, AW) for _, n, _, _ in ATTN_PLANS[1:]] for a in (q, kvp)]
    flat = [views[a][p] for p in range(3) for a in range(2)]
    ng = len(shards)
    n_grid = AW // LANES

    def body(*refs):
        hbm = [refs[2 * p:2 * p + 2] for p in range(3)]
        refs = refs[6:]
        shard_refs, refs = refs[:ng], refs[ng:]
        y_ref, lt_ref = refs[0:2]
        whole_refs, refs = refs[2:2 + ng], refs[2 + ng:]
        bufs = [refs[2 * p:2 * p + 2] for p in range(3)]
        oc4, lc4, oc16, lc16, tab128, tab4, sem_in = refs[6:13]
        step = pl.program_id(0)
        if ng:
            enter_gather, start_gather, relay_gather, finish_gather = _gather_steps(shard_refs, whole_refs, *refs[13:])
            pl.when(step == 0)(enter_gather)
            pl.when(step == n_grid // 2)(relay_gather)
        now = [_class_gather(hbm[p], bufs[p], sem_in.at[p], _lanes_of(step)) for p in range(3)]
        nxt = [_class_gather(hbm[p], bufs[p], sem_in.at[p], _lanes_of(step + 1)) for p in range(3)]

        @pl.when(step == 0)
        def _():
            for p in range(3):
                _start(now[p])
                for b in bufs[p]:
                    b[0:PAD, :] = jnp.zeros((PAD, LANES), F32)
            _fill_bias(tab128, 128, False)
            _fill_bias(tab4, 64, True)

        def prefetch(p):
            pl.when(step + 1 < n_grid)(lambda: _start(nxt[p]))

        lane = lax.broadcasted_iota(jnp.int32, (1, LANES), 1)
        ones = jnp.ones((WIN, LANES), BF16)

        def run(plan, bq, bkv, tab, o_dst, l_dst, dst_pad):
            _, n_cls, qblk, nbc = plan
            partner = n_cls == 8

            def block(g, carry):
                own, wins, mask = _block_rows(g, qblk, nbc, partner)
                q2 = _stack_heads(bq[own, :].astype(BF16), lane)
                kw, vwin = _unpack_pair(_window(bkv, wins))
                vw = jnp.concatenate([vwin, ones], axis=1)
                s = _dot_nt(q2, kw) + tab[mask]
                m = jnp.max(s, axis=1, keepdims=True)
                oe = _dot(jnp.exp(s - m).astype(BF16), vw)
                den = oe[:, LANES:]
                dst = pl.ds(pl.multiple_of(dst_pad + g * qblk, qblk), qblk)
                o_dst[dst, :] = _unstack_heads(oe[:, 0:LANES] / den, lane)
                l_dst[dst, :] = _unstack_heads(m + jnp.log(den), lane)
                return carry
            lax.fori_loop(0, n_cls * nbc, block, 0, unroll=ATTN_UNROLL)

        _wait(_whole_waits(bufs[0], sem_in.at[0]))
        run(ATTN_PLANS[0], *bufs[0], tab128, y_ref, lt_ref, 0)
        prefetch(0)
        if ng:
            pl.when(step == 0)(start_gather)
        _wait(_whole_waits(bufs[1], sem_in.at[1]))
        run(ATTN_PLANS[1], *bufs[1], tab4, oc4, lc4, PAD)
        prefetch(1)
        _wait(_whole_waits(bufs[2], sem_in.at[2]))
        run(ATTN_PLANS[2], *bufs[2], tab128, oc16, lc16, PAD)
        prefetch(2)

        n_rows = 64

        def token_order(buf, t, n_cls):
            per = S // n_cls
            first = PAD + t * (n_rows // n_cls)
            return jnp.concatenate([buf[pl.ds(first + jj, n_cls, stride=per), :] for jj in range(n_rows // n_cls)],
                                   axis=0)

        def combine(t, carry):
            rows = pl.ds(pl.multiple_of(t * n_rows, n_rows), n_rows)
            l0, l1, l2 = lt_ref[rows, :], token_order(lc4, t, 8), token_order(lc16, t, 16)
            lm = jnp.maximum(jnp.maximum(l0, l1), l2)
            e0, e1, e2 = jnp.exp(l0 - lm), jnp.exp(l1 - lm), jnp.exp(l2 - lm)
            den = e0 + e1 + e2
            y_ref[rows, :] = (e0 * y_ref[rows, :] + e1 * token_order(oc4, t, 8)
                              + e2 * token_order(oc16, t, 16)) / den
            lt_ref[rows, :] = lm + jnp.log(den)
            return carry
        lax.fori_loop(0, S // n_rows, combine, 0, unroll=2)

        if ng:
            pl.when(step == n_grid - 1)(finish_gather)

    col = pl.BlockSpec((S, LANES), lambda h: (0, h))
    padded = pltpu.VMEM((PAD + S, LANES), F32)
    return pl.pallas_call(
        body, grid=(n_grid,), name="attn_fwd",
        in_specs=[ANY] * (6 + ng), out_specs=[col, col] + [ANY] * ng,
        out_shape=[jax.ShapeDtypeStruct((S, AW), F32)] * 2 + _gathered_shapes(shards),
        scratch_shapes=[padded] * 10 + [
            pltpu.VMEM((4, 256, WIN), F32), pltpu.VMEM((4, 128, WIN), F32), pltpu.SemaphoreType.DMA((3, 2))]
        + (_gather_scratch(ng) if ng else []),
        compiler_params=_cparams(56, **({"collective_id": ID_ATTN_FWD} if ng else {})))(*flat, *shards)


def _conv_taps(z, zprev, row):
    z1 = jnp.where(row == 0, zprev[7:8, :], pltpu.roll(z, 1, 0))
    z2 = jnp.where(row == 0, zprev[6:7, :], jnp.where(row == 1, zprev[7:8, :], pltpu.roll(z, 2, 0)))
    return z1, z2


def _xattn_scores(qm, km):
    s = _dot_nt(qm, km)
    m = jnp.max(s, axis=1, keepdims=True)
    e = jnp.exp(s - m)
    return e, jnp.sum(e, axis=1, keepdims=True)


def _mix_out(y_attn, bcu, qx16, kv16, cw8, g_attn, g_conv, g_x, g_post, wout16, x, shards):
    def body(ya_ref, bcu_ref, halo_ref, qx_ref, kv_ref, cw_ref, ga_ref, gc_ref, gx_ref, gp_ref, w_ref, x_ref,
             ypre_ref, y16_ref, y2_ref, x1_ref):
        i = pl.program_id(0)
        bcu = bcu_ref[...]
        b, c, u = bcu[:, 0:CW], bcu[:, CW:2 * CW], bcu[:, 2 * CW:]
        z = c * u
        halo = halo_ref[...]
        zprev = jnp.where(i > 0, halo[:, CW:2 * CW] * halo[:, 2 * CW:], 0.0)
        row = lax.broadcasted_iota(jnp.int32, z.shape, 0)
        z1, z2 = _conv_taps(z, zprev, row)
        cw = cw_ref[...]
        y_conv = b * (z2 * cw[0:1, :] + z1 * cw[1:2, :] + z * cw[2:3, :])

        qx = qx_ref[...]
        kv = kv_ref[...]
        km, vm = kv[:, 0:XW], kv[:, XW:]
        lane = lax.broadcasted_iota(jnp.int32, qx.shape, 1)
        y_x = jnp.zeros(qx.shape, F32)
        for h in range(XW // HEAD):
            hm = (lane >= h * HEAD) & (lane < (h + 1) * HEAD)
            e, l = _xattn_scores(jnp.where(hm, qx, jnp.zeros_like(qx)), km)
            y_x = jnp.where(hm, _dot(e.astype(BF16), vm) / l, y_x)

        y_attn = ya_ref[...]
        ypre_ref[:, 0:AW] = y_attn
        ypre_ref[:, AW:AW + CW] = y_conv
        ypre_ref[:, AW + CW:] = y_x
        y = jnp.concatenate([_rms(y_attn, ga_ref[...])[0], _rms(y_conv, gc_ref[...])[0],
                             _rms(y_x, gx_ref[...])[0]], axis=1).astype(BF16)
        y16_ref[...] = y.T
        y2 = _dot(y, w_ref[...])
        y2_ref[...] = y2
        x1_ref[...] = x_ref[...] + _rms(y2, gp_ref[...])[0]

    def tile(w):
        return pl.BlockSpec((TQ, w), lambda i: (i, 0))

    halo = pl.BlockSpec((SUBLANES, 3 * CW), lambda i: (jnp.maximum(i * (TQ // SUBLANES) - 1, 0), 0))
    return _call_with_gather(
        body, NT, shards, name="mix_out",
        in_specs=[tile(AW), tile(3 * CW), halo, tile(XW), _const((N_MEM, 2 * XW)), _const((SUBLANES, CW)),
                  _const((1, AW)), _const((1, CW)), _const((1, XW)), _const((1, D)), _const((D, D)), tile(D)],
        out_specs=[tile(D), _tokens_in_lanes(TQ), tile(D), tile(D)],
        out_shape=[jax.ShapeDtypeStruct((S, D), F32), jax.ShapeDtypeStruct((D, S), BF16),
                   jax.ShapeDtypeStruct((S, D), F32), jax.ShapeDtypeStruct((S, D), F32)],
        scratch_shapes=[], vmem_mb=56,
        args=(y_attn, bcu, bcu, qx16, kv16, cw8, g_attn, g_conv, g_x, g_post, wout16, x))


def _mlp(x1, tgt, g_pre, g_post, wup8, wdn_halves):
    tq = TQ_MLP
    half = D // 2

    def body(x1_ref, t_ref, g1_ref, g2_ref, wu_ref, wda_ref, wdb_ref,
             a16_ref, du_ref, h2_ref, df2_ref, dx1_ref, loss_ref, dg_ref):
        @pl.when(pl.program_id(0) == 0)
        def _():
            loss_ref[...] = jnp.zeros_like(loss_ref)
            dg_ref[...] = jnp.zeros_like(dg_ref)

        x1 = x1_ref[...]
        g1, g2 = g1_ref[...], g2_ref[...]
        y1, n1, r1 = _rms(x1, g1)
        h2 = y1.astype(BF16)
        h2_ref[...] = h2.T
        f2a = jnp.zeros((tq, half), F32)
        f2b = jnp.zeros((tq, half), F32)
        for j in range(N_DEV):
            cols = slice(j * FF_BLK, (j + 1) * FF_BLK)
            a = jnp.maximum(_dot(h2, wu_ref[j]), 0.0)
            a16_ref[:, cols] = a.astype(BF16)
            f = (a * a).astype(BF16)
            f2a = f2a + _dot(f, wda_ref[cols, :])
            f2b = f2b + _dot(f, wdb_ref[cols, :])
        f2 = jnp.concatenate([f2a, f2b], axis=1)
        y2, n2, r2 = _rms(f2, g2)
        e = x1 + y2 - t_ref[...]
        sq = jnp.sum(jnp.sum(e * e, axis=1, keepdims=True), axis=0, keepdims=True)
        loss_ref[...] += jnp.broadcast_to(sq * (0.5 / D), loss_ref.shape)
        dout = e * (1.0 / D)
        df2, dg2 = _rms_bwd(dout, n2, r2, g2)
        df2_16 = df2.astype(BF16)
        df2_ref[...] = df2_16.T
        dh2 = jnp.zeros((tq, D), F32)
        for j in range(N_DEV):
            cols = slice(j * FF_BLK, (j + 1) * FF_BLK)
            df = _dot_nt(df2_16[:, 0:half], wda_ref[cols, :]) + _dot_nt(df2_16[:, half:], wdb_ref[cols, :])
            du = (df * (2.0 * a16_ref[:, cols].astype(F32))).astype(BF16)
            du_ref[:, cols] = du
            dh2 = dh2 + _dot_nt(du, wu_ref[j])
        dx, dg1 = _rms_bwd(dh2, n1, r1, g1)
        dx1_ref[...] = dout + dx
        dg_ref[0:1, :] += dg2
        dg_ref[1:2, :] += dg1

    def tile(w):
        return pl.BlockSpec((tq, w), lambda i: (i, 0))

    return pl.pallas_call(
        body, grid=(S // tq,), name="mlp",
        in_specs=[tile(D), tile(D), _const((1, D)), _const((1, D)), _const((N_DEV, D, FF_BLK)), _const((FF, half)), _const((FF, half))],
        out_specs=[tile(FF), tile(FF), _tokens_in_lanes(tq), _tokens_in_lanes(tq), tile(D),
                   _acc((SUBLANES, LANES)), _acc((SUBLANES, D))],
        out_shape=[jax.ShapeDtypeStruct((S, FF), BF16), jax.ShapeDtypeStruct((S, FF), BF16),
                   jax.ShapeDtypeStruct((D, S), BF16), jax.ShapeDtypeStruct((D, S), BF16),
                   jax.ShapeDtypeStruct((S, D), F32), jax.ShapeDtypeStruct((SUBLANES, LANES), F32),
                   jax.ShapeDtypeStruct((SUBLANES, D), F32)],
        compiler_params=_cparams(60))(x1, tgt, g_pre, g_post, wup8, *wdn_halves)


def _mix_out_bwd(dx1, y2, ypre, ltot, head_ones, q, bcu, qx16, kv16, cw8, g_post, g_attn, g_conv, g_x, wout16):
    def body(dx1_ref, y2_ref, ypre_ref, lt_ref, e_ref, q_ref, bcu_ref, halo_ref, qx_ref, kv_ref, cw_ref, gp_ref,
             ga_ref, gc_ref, gx_ref, w_ref, dy2_ref, qdo_ref, ld_ref, dbcu_ref, dqx_ref, dgs_ref, dcw_ref, dkv_ref,
             carry):
        i = pl.program_id(0)

        @pl.when(i == 0)
        def _():
            dgs_ref[...] = jnp.zeros_like(dgs_ref)
            dcw_ref[...] = jnp.zeros_like(dcw_ref)
            dkv_ref[...] = jnp.zeros_like(dkv_ref)
            carry[...] = jnp.zeros_like(carry)

        gp = gp_ref[...]
        _, n, r = _rms(y2_ref[...], gp)
        dy2, dgp = _rms_bwd(dx1_ref[...], n, r, gp)
        dy2_16 = dy2.astype(BF16)
        dy2_ref[...] = dy2_16
        dy = _dot_nt(dy2_16, w_ref[...])

        ypre = ypre_ref[...]
        ga, gc, gx = ga_ref[...], gc_ref[...], gx_ref[...]
        _, na, ra = _rms(ypre[:, 0:AW], ga)
        dya, dga = _rms_bwd(dy[:, 0:AW], na, ra, ga)
        _, nc, rc = _rms(ypre[:, AW:AW + CW], gc)
        dyc, dgc = _rms_bwd(dy[:, AW:AW + CW], nc, rc, gc)
        y_x = ypre[:, AW + CW:]
        _, nx, rx = _rms(y_x, gx)
        dyx, dgx = _rms_bwd(dy[:, AW + CW:], nx, rx, gx)
        qdo_ref[...] = _pack_pair(q_ref[...], dya)
        prod = dya * ypre[:, 0:AW]
        hi = prod.astype(BF16)
        lo = (prod - hi.astype(F32)).astype(BF16)
        head_sum = _dot(hi, e_ref[...]) + _dot(lo, e_ref[...])
        lane_a = lax.broadcasted_iota(jnp.int32, prod.shape, 1)
        ld_ref[...] = jnp.where((lane_a % HEAD) < HEAD // 2, lt_ref[...], head_sum)
        dgs_ref[0:1, :] += dgp
        dgs_ref[1:2, :] += jnp.concatenate([dga, dgc, dgx], axis=1)

        bcu = bcu_ref[...]
        b, c, u = bcu[:, 0:CW], bcu[:, CW:2 * CW], bcu[:, 2 * CW:]
        z = c * u
        halo = halo_ref[...]
        zprev = jnp.where(i < NT - 1, halo[:, CW:2 * CW] * halo[:, 2 * CW:], 0.0)
        row = lax.broadcasted_iota(jnp.int32, z.shape, 0)
        z1, z2 = _conv_taps(z, zprev, row)
        cw = cw_ref[...]
        conv = z2 * cw[0:1, :] + z1 * cw[1:2, :] + z * cw[2:3, :]
        dconv = dyc * b
        nxt = carry[...]
        dn1 = jnp.where(row == TQ - 1, nxt[0:1, :], pltpu.roll(dconv, TQ - 1, 0))
        dn2 = jnp.where(row == TQ - 1, nxt[1:2, :], jnp.where(row == TQ - 2, nxt[0:1, :], pltpu.roll(dconv, TQ - 2, 0)))
        carry[...] = dconv[0:SUBLANES, :]
        dz = dconv * cw[2:3, :] + dn1 * cw[1:2, :] + dn2 * cw[0:1, :]
        dbcu_ref[:, 0:CW] = (dyc * conv).astype(BF16)
        dbcu_ref[:, CW:2 * CW] = (dz * u).astype(BF16)
        dbcu_ref[:, 2 * CW:] = (dz * c).astype(BF16)
        dcw_ref[0:1, :] += jnp.sum(z2 * dconv, axis=0, keepdims=True)
        dcw_ref[1:2, :] += jnp.sum(z1 * dconv, axis=0, keepdims=True)
        dcw_ref[2:3, :] += jnp.sum(z * dconv, axis=0, keepdims=True)

        qx = qx_ref[...]
        kv = kv_ref[...]
        km, vm = kv[:, 0:XW], kv[:, XW:]
        lane = lax.broadcasted_iota(jnp.int32, qx.shape, 1)
        dqx = jnp.zeros(qx.shape, F32)
        dkm = jnp.zeros((N_MEM, XW), F32)
        dvm = jnp.zeros((N_MEM, XW), F32)
        for h in range(XW // HEAD):
            hm = (lane >= h * HEAD) & (lane < (h + 1) * HEAD)
            qm = jnp.where(hm, qx, jnp.zeros_like(qx))
            e, l = _xattn_scores(qm, km)
            p = e / l
            dom = jnp.where(hm, dyx, 0.0)
            do16 = dom.astype(BF16)
            dsum = jnp.sum(dom * y_x, axis=1, keepdims=True)
            ds = (p * (_dot_nt(do16, vm) - dsum)).astype(BF16)
            dqx = jnp.where(hm, _dot(ds, km), dqx)
            dkm = dkm + _dot_tn(ds, qm)
            dvm = dvm + _dot_tn(p.astype(BF16), do16)
        dqx_ref[...] = (dqx * SCALE).astype(BF16)
        dkv_ref[:, 0:XW] += dkm
        dkv_ref[:, XW:] += dvm

    def tile(w):
        return pl.BlockSpec((TQ, w), lambda i: (NT - 1 - i, 0))

    halo = pl.BlockSpec((SUBLANES, 3 * CW), lambda i: (jnp.maximum((NT - 1 - i) * (TQ // SUBLANES) - 1, 0), 0))
    return pl.pallas_call(
        body, grid=(NT,), name="mix_out_bwd",
        in_specs=[tile(D), tile(D), tile(D), tile(AW), _const((AW, AW)), tile(AW), tile(3 * CW), halo, tile(XW),
                  _const((N_MEM, 2 * XW)), _const((SUBLANES, CW)), _const((1, D)), _const((1, AW)), _const((1, CW)),
                  _const((1, XW)), _const((D, D))],
        out_specs=[tile(D), tile(AW), tile(AW), tile(3 * CW), tile(XW), _acc((SUBLANES, D)), _acc((SUBLANES, CW)),
                   _acc((N_MEM, 2 * XW))],
        out_shape=[jax.ShapeDtypeStruct((S, D), BF16), jax.ShapeDtypeStruct((S, AW), F32),
                   jax.ShapeDtypeStruct((S, AW), F32),
                   jax.ShapeDtypeStruct((S, 3 * CW), BF16), jax.ShapeDtypeStruct((S, XW), BF16),
                   jax.ShapeDtypeStruct((SUBLANES, D), F32), jax.ShapeDtypeStruct((SUBLANES, CW), F32),
                   jax.ShapeDtypeStruct((N_MEM, 2 * XW), F32)],
        scratch_shapes=[pltpu.VMEM((SUBLANES, CW), F32)],
        compiler_params=_cparams(56))(dx1, y2, ypre, ltot, head_ones, q, bcu, bcu, qx16, kv16, cw8, g_post, g_attn,
                                      g_conv, g_x, wout16)


def _attn_bwd(qdo, kvp, ld, chip_sums=()):
    n_in = 3
    views = [[a] + [a.reshape(S // n, n, AW) for _, n, _, _ in ATTN_PLANS[1:]] for a in (qdo, kvp, ld)]
    flat = [views[a][p] for p in range(3) for a in range(n_in)]
    ns = len(chip_sums)
    n_grid = AW // LANES

    def body(*refs):
        hbm = [refs[n_in * p:n_in * p + n_in] for p in range(3)]
        refs = refs[3 * n_in:]
        sum_refs, refs = refs[:ns], refs[ns:]
        outs = [refs[3 * p:3 * p + 3] for p in range(3)]
        landed_refs, sc = refs[9:9 + ns], refs[9 + ns:]
        bufs = [sc[3 * p:3 * p + 3] for p in range(3)]
        res = [sc[9 + 3 * p:12 + 3 * p] for p in range(3)]
        tab128, tab4, sem_in, sem_out = sc[18:22]
        step = pl.program_id(0)
        if ns:
            start_chips, finish_chips = _chips_steps(sum_refs, landed_refs, *sc[22:])
            _, _, core, chips = _place()
            signal_chips, chips_are_in = _own_barrier([(px, py, core) for px, py in chips])
            pl.when(step == 0)(signal_chips)

            def chips_go():
                chips_are_in()
                start_chips()
        now =[_class_gather(hbm[p], bufs[p], sem_in.at[p], _lanes_of(step)) for p in range(3)]
        nxt = [_class_gather(hbm[p], bufs[p], sem_in.at[p], _lanes_of(step + 1)) for p in range(3)]

        @pl.when(step == 0)
        def _():
            for p in range(3):
                _start(now[p])
                for b in bufs[p]:
                    b[0:PAD, :] = jnp.zeros((PAD, LANES), F32)
            _fill_bias(tab128, 128, False)
            _fill_bias(tab4, 64, True)

        def prefetch(p):
            pl.when(step + 1 < n_grid)(lambda: _start(nxt[p]))

        lane = lax.broadcasted_iota(jnp.int32, (1, LANES), 1)

        def run(plan, plan_bufs, tab, dst):
            _, n_cls, qblk, nbc = plan
            partner = n_cls == 8
            bqdo, bkv, bld = plan_bufs
            rq, rk, rv = dst

            def block(g, carry):
                own, wins, mask = _block_rows(g, qblk, nbc, partner)
                qb, dob = _unpack_pair(bqdo[own, :])
                q2, do2 = _stack_heads(qb, lane), _stack_heads(dob, lane)
                kw, vw = _unpack_pair(_window(bkv, wins))
                ldv = bld[own, :]
                half = HEAD // 2
                lt2 = jnp.concatenate([ldv[:, 0:1], ldv[:, HEAD:HEAD + 1]], axis=0)
                dsum2 = jnp.concatenate([ldv[:, half:half + 1], ldv[:, HEAD + half:HEAD + half + 1]], axis=0)
                p = jnp.exp(_dot_nt(q2, kw) + tab[mask] - lt2)
                ds = (p * (_dot_nt(do2, vw) - dsum2)).astype(BF16)
                rq[own, :] = _unstack_heads(_dot(ds, kw), lane)
                dkw = _dot_tn(ds, q2)
                dvw = _dot_tn(p.astype(BF16), do2)
                n_w = WIN // len(wins)
                for i, w in enumerate(wins):
                    rk[w, :] += dkw[i * n_w:(i + 1) * n_w, :]
                    rv[w, :] += dvw[i * n_w:(i + 1) * n_w, :]
                return carry
            lax.fori_loop(0, n_cls * nbc, block, 0, unroll=ATTN_UNROLL)

        tabs = (tab128, tab4, tab128)
        def drained(p):
            return lambda: _wait(_whole_waits(res[p], sem_out.at[p]))

        for p in range(3):
            pl.when(step > 0)(drained(p))
            for b in res[p][1:]:
                b[...] = jnp.zeros_like(b)
            _wait(_whole_waits(bufs[p], sem_in.at[p]))
            run(ATTN_PLANS[p], bufs[p], tabs[p], res[p])
            prefetch(p)
            _start(_class_scatter(res[p], outs[p], sem_out.at[p], _lanes_of(step)))
            if ns and p == 0:
                pl.when(step == 0)(chips_go)
        for p in range(3):
            pl.when(step == n_grid - 1)(drained(p))
        if ns:
            pl.when(step == n_grid - 1)(finish_chips)

    padded = pltpu.VMEM((PAD + S, LANES), F32)
    shapes = [jax.ShapeDtypeStruct(views[0][p].shape, F32) for p in range(3) for _ in range(3)]
    out = pl.pallas_call(
        body, grid=(n_grid,), name="attn_bwd",
        in_specs=[ANY] * (3 * n_in + ns), out_specs=[ANY] * (9 + ns),
        out_shape=shapes + _chips_shapes(chip_sums),
        scratch_shapes=[padded] * 18
        + [pltpu.VMEM((4, 256, WIN), F32), pltpu.VMEM((4, 128, WIN), F32),
           pltpu.SemaphoreType.DMA((3, n_in)), pltpu.SemaphoreType.DMA((3, 3))]
        + (_chips_scratch(ns) if ns else []),
        compiler_params=_cparams(56, **({"collective_id": ID_ATTN_BWD} if ns else {})))(*flat, *chip_sums)
    return [o.reshape(S, AW) for o in out[:9]] + list(out[9:])


def _in_proj_bwd(dqkv, dbcu, dqx, cos, sins, w16, x, g, dx1):
    tq = TQ // 2

    def body(*refs):
        parts = refs[0:9]
        dbcu_ref, dqx_ref, c_ref, s_ref, w_ref, x_ref, g_ref, dx1_ref, dp_ref, gx_ref, dg_ref = refs[9:]

        @pl.when(pl.program_id(0) == 0)
        def _():
            dg_ref[...] = jnp.zeros_like(dg_ref)

        dq, dk, dv = (parts[i][...] + parts[3 + i][...] + parts[6 + i][...] for i in range(3))
        cos, sn = _all_heads(c_ref[...]), _all_heads(s_ref[...])
        dqr = dq * SCALE
        dkr = dk
        dp = jnp.concatenate([(dqr * cos + _rot_half(dqr * sn)).astype(BF16),
                              (dkr * cos + _rot_half(dkr * sn)).astype(BF16), dv.astype(BF16),
                              dbcu_ref[...], dqx_ref[...]], axis=1)
        dp_ref[...] = dp
        dh = _dot_nt(dp, w_ref[...])
        g = g_ref[...]
        _, n, r = _rms(x_ref[...], g)
        dx, dg = _rms_bwd(dh, n, r, g)
        gx_ref[...] = dx1_ref[...] + dx
        dg_ref[0:1, :] += dg

    def tile(w):
        return pl.BlockSpec((tq, w), lambda i: (i, 0))

    return pl.pallas_call(
        body, grid=(S // tq,), name="in_proj_bwd",
        in_specs=[tile(AW)] * 9 + [tile(3 * CW), tile(XW), tile(LANES), tile(LANES), _const((D, PW)),
                                   tile(D), _const((1, D)), tile(D)],
        out_specs=[tile(PW), tile(D), _acc((SUBLANES, D))],
        out_shape=[jax.ShapeDtypeStruct((S, PW), BF16), jax.ShapeDtypeStruct((S, D), F32),
                   jax.ShapeDtypeStruct((SUBLANES, D), F32)],
        compiler_params=_cparams(56))(*dqkv, dbcu, dqx, cos, sins, w16, x, g, dx1)


def _mem_bwd(mem, g_mem, wkv16, dkv):
    def body(m_ref, g_ref, w_ref, dkv_ref, dkv16_ref, dg_ref):
        dkv16 = dkv_ref[...].astype(BF16)
        dkv16_ref[...] = dkv16
        _, n, _ = _rms(m_ref[...], g_ref[...])
        dg = jnp.sum(_dot_nt(dkv16, w_ref[...]) * n, axis=0, keepdims=True)
        dg_ref[...] = jnp.broadcast_to(dg, dg_ref.shape)

    return pl.pallas_call(
        body, name="mem_bwd",
        out_shape=[jax.ShapeDtypeStruct((N_MEM, 2 * XW), BF16), jax.ShapeDtypeStruct((SUBLANES, D), F32)],
        compiler_params=pltpu.CompilerParams(vmem_limit_bytes=32 << 20))(mem, g_mem, wkv16, dkv)


N_CHIPS = N_DEV // 2


def _pair_scratch(block):
    return [pltpu.VMEM((N_CHIPS,) + block, BF16), pltpu.VMEM((N_CHIPS,) + block, BF16),
            pltpu.SemaphoreType.DMA((N_CHIPS,)), pltpu.SemaphoreType.DMA((N_CHIPS,))]


def _swap_with_sibling(p, stage, land, send, recv):
    x, y, c = lax.axis_index("x"), lax.axis_index("y"), lax.axis_index("c")
    return pltpu.make_async_remote_copy(src_ref=stage.at[p], dst_ref=land.at[p], send_sem=send.at[p],
                                        recv_sem=recv.at[p], device_id=(x, y, 1 - c), device_id_type=MESH)


def _own_barrier(peers):
    sem = pltpu.get_barrier_semaphore()

    def signal():
        for peer in peers:
            pl.semaphore_signal(sem, inc=1, device_id=peer, device_id_type=MESH)

    return signal, lambda: pl.semaphore_wait(sem, len(peers))


def _sibling_barrier():
    x, y, c = lax.axis_index("x"), lax.axis_index("y"), lax.axis_index("c")
    return _own_barrier([(x, y, 1 - c)])


ID_WGRAD_UP, ID_WGRAD_DOWN, ID_WGRAD_ROWS, ID_ROPE_TABLE, ID_IN_PROJ, ID_ATTN_FWD, ID_ATTN_BWD, ID_WGRAD_IN = range(8)


def _wgrad_cols(place, at16, b16, blk, name, barrier_id, square_b=False, transpose_out=False, to_chips=False,
                small=()):
    m, kk = at16.shape
    assert to_chips == bool(small)
    aligned = blk % LANES == 0
    wide = blk if aligned else -(-(blk + LANES // 2) // LANES) * LANES
    assert aligned or (transpose_out and blk % SUBLANES == 0)
    block = (blk, m) if transpose_out else (m, blk)

    def chip_of(step, my_chip):
        return jnp.bitwise_xor(my_chip, N_CHIPS - 1 - step) if to_chips else step

    def body(pl_ref, a_ref, *refs):
        b_refs, refs = refs[:2 if aligned else 1], refs[2 if aligned else 1:]
        accs, refs = refs[:len(small)], refs[len(small):]
        (cs_ref, own_ref), refs = refs[:2], refs[2:]
        if to_chips:
            landed, refs = refs[0], refs[1:]
        if small:
            tot_ref, refs = refs[0], refs[1:]
        (stage, land, send, recv), refs = refs[:4], refs[4:]
        if not aligned:
            (win, wsem), refs = refs[:2], refs[2:]
        if small:
            start_small, finish_small = _small_reduce_steps(accs, tot_ref, *refs[-4:])
            refs = refs[:-4]
        step = pl.program_id(0)
        x, y, c = lax.axis_index("x"), lax.axis_index("y"), lax.axis_index("c")
        others = [(x ^ (k >> 2), y ^ ((k >> 1) & 1), c ^ (k & 1)) for k in range(1, N_DEV)]
        signal_peers, peers_are_in = _own_barrier(others if small else [(x, y, 1 - c)])
        pl.when(step == 0)(signal_peers)
        my_chip = 2 * x + y
        p = chip_of(step, my_chip)

        def fetch(at_step, mine):
            j = 2 * chip_of(at_step, my_chip) + (c if mine else 1 - c)
            first = pl.multiple_of(((j * blk) >> 7) << 7, LANES)
            slot = 2 * (at_step & 1) + mine
            return pltpu.make_async_copy(b_refs[0].at[:, pl.ds(first, wide)], win.at[slot], wsem.at[slot])

        if not aligned:
            @pl.when(step == 0)
            def _():
                fetch(0, 0).start()
                fetch(0, 1).start()

            @pl.when(step + 1 < N_CHIPS)
            def _():
                fetch(step + 1, 0).start()
                fetch(step + 1, 1).start()

        def partial(mine):
            if aligned:
                b = b_refs[mine][...]
                if square_b:
                    b = b * b
                acc = _dot(a_ref[...], b)
            else:
                fetch(step, mine).wait()
                acc = _dot(a_ref[...], win[2 * (step & 1) + mine]).T
                odd = c if mine else 1 - c
                return jnp.where(odd == 0, acc[0:blk], acc[wide - blk:wide])
            return acc.T if transpose_out else acc

        stage[p] = partial(0).astype(BF16)
        pl.when(step == 0)(peers_are_in)
        if small:
            pl.when(step == 0)(start_small)
        swap = _swap_with_sibling(p, stage, land, send, recv)
        swap.start()
        mine = partial(1)
        swap.wait()
        total = mine + land[p].astype(F32)
        cs_ref[0] = total.astype(BF16)

        @pl.when(p == my_chip)
        def _():
            own_ref[...] = total

        if to_chips:
            stage2, send2, recv2 = refs
            flipped = jnp.bitwise_xor(p, my_chip)
            k = jnp.where(flipped == 2, 0, jnp.where(flipped == 1, 1, 2))

            def to_owner(src, k_, px, py):
                return pltpu.make_async_remote_copy(src_ref=src, dst_ref=landed.at[k_], send_sem=send2.at[k_],
                                                    recv_sem=recv2.at[k_], device_id=(px, py, c), device_id_type=MESH)

            @pl.when(p != my_chip)
            def _():
                stage2[p] = total.astype(BF16)
                to_owner(stage2.at[p], k, p >> 1, p & 1).start()

            @pl.when(step == N_CHIPS - 1)
            def _():
                for k_ in range(N_CHIPS - 1):
                    to_owner(stage2.at[0], k_, x, y).wait()

        if small:
            pl.when(step == N_CHIPS - 1)(finish_small)

    def b_spec(mine):
        return pl.BlockSpec((kk, blk), lambda i, s: (0, 2 * chip_of(i, s[1]) + (s[0] if mine else 1 - s[0])))

    b_specs, b_args = ([b_spec(0), b_spec(1)], (b16, b16)) if aligned else ([ANY], (b16,))
    scratch = _pair_scratch(block)
    if not aligned:
        scratch += [pltpu.VMEM((4, kk, wide), BF16), pltpu.SemaphoreType.DMA((4,))]
    out_specs = [pl.BlockSpec((1,) + block, lambda i, s: (chip_of(i, s[1]), 0, 0)), pl.BlockSpec(block, lambda i, s: (0, 0))]
    out_shape = [jax.ShapeDtypeStruct((N_CHIPS,) + block, BF16), jax.ShapeDtypeStruct(block, F32)]
    if to_chips:
        out_specs.append(ANY)
        out_shape.append(jax.ShapeDtypeStruct((N_CHIPS - 1,) + block, BF16))
        scratch += [pltpu.VMEM((N_CHIPS,) + block, BF16), pltpu.SemaphoreType.DMA((N_CHIPS - 1,)),
                    pltpu.SemaphoreType.DMA((N_CHIPS - 1,))]
    small_specs = [pl.BlockSpec(a.shape, lambda i, s: (0, 0)) for a in small]
    if small:
        out_specs.append(pl.BlockSpec((PACK_ROWS, D), lambda i, s: (0, 0)))
        out_shape.append(jax.ShapeDtypeStruct((PACK_ROWS, D), F32))
        scratch += _small_reduce_scratch()
    return pl.pallas_call(
        body, name=name,
        grid_spec=pltpu.PrefetchScalarGridSpec(
            num_scalar_prefetch=1, grid=(N_CHIPS,),
            in_specs=[pl.BlockSpec((m, kk), lambda i, s: (0, 0), pipeline_mode=pl.Buffered(1))] + b_specs + small_specs,
            out_specs=out_specs, scratch_shapes=scratch),
        out_shape=out_shape,
        compiler_params=_cparams(56, collective_id=barrier_id),
    )(place, at16, *b_args, *small)


ROWS_STEPS = 4


def _wgrad_rows(place, products, name):
    n_prod = len(products)
    dims = [(at16.shape[0], at16.shape[1], b16.shape[1]) for at16, b16 in products]
    cut = [kk % (ROWS_STEPS * LANES) == 0 for _, kk, _ in dims]
    blocks = [(m // N_DEV, n) for m, _, n in dims]

    def body(pl_ref, *refs):
        ins, outs, scratch = refs[:2 * n_prod], refs[2 * n_prod:4 * n_prod], refs[4 * n_prod:]
        c, step = pl_ref[0], pl.program_id(0)

        def multiply(i):
            a_ref, b_ref, acc = ins[2 * i], ins[2 * i + 1], scratch[5 * i]

            @pl.when(step == 0)
            def _():
                acc[...] = _dot(a_ref[...], b_ref[...])

            if cut[i]:
                @pl.when(step > 0)
                def _():
                    acc[...] += _dot(a_ref[...], b_ref[...])

        def rows(i, owner):
            return pl.ds(pl.multiple_of(owner * blocks[i][0], blocks[i][0]), blocks[i][0])

        def send_sibling_side(i):
            acc, stage, land, send, recv = scratch[5 * i:5 * i + 5]
            swaps = []
            for p in range(N_CHIPS):
                stage[p] = acc[rows(i, 2 * p + 1 - c), :].astype(BF16)
                swaps.append(_swap_with_sibling(p, stage, land, send, recv))
                swaps[-1].start()
            return swaps

        def add_my_side(i, swaps):
            acc, land = scratch[5 * i], scratch[5 * i + 2]
            cs_ref, own_ref = outs[2 * i:2 * i + 2]
            for p in range(N_CHIPS):
                swaps[p].wait()
                total = acc[rows(i, 2 * p + c), :] + land[p].astype(F32)
                cs_ref[p] = total.astype(BF16)

                @pl.when(p == pl_ref[1])
                def _():
                    own_ref[...] = total

        signal_sibling, sibling_is_in = _sibling_barrier()
        pl.when(step == 0)(signal_sibling)
        for i in range(n_prod):
            multiply(i)

        @pl.when(step == ROWS_STEPS - 1)
        def _():
            sibling_is_in()
            swaps = [send_sibling_side(i) for i in range(n_prod)]
            for i in range(n_prod):
                add_my_side(i, swaps[i])

    in_specs, out_specs, out_shape, scratch = [pl.BlockSpec(memory_space=pltpu.SMEM)], [], [], []
    for (m, kk, n), cut_i, block in zip(dims, cut, blocks):
        chunk = kk // ROWS_STEPS
        in_specs += ([pl.BlockSpec((m, chunk), lambda i: (0, i)), pl.BlockSpec((chunk, n), lambda i: (i, 0))]
                     if cut_i else [_const((m, kk)), _const((kk, n))])
        out_specs += [_acc((N_CHIPS,) + block), _acc(block)]
        out_shape += [jax.ShapeDtypeStruct((N_CHIPS,) + block, BF16), jax.ShapeDtypeStruct(block, F32)]
        scratch += [pltpu.VMEM((m, n), F32)] + _pair_scratch(block)
    out = pl.pallas_call(
        body, grid=(ROWS_STEPS,), name=name, in_specs=in_specs, out_specs=out_specs, out_shape=out_shape,
        scratch_shapes=scratch, compiler_params=_cparams(56, collective_id=ID_WGRAD_ROWS),
    )(place, *[a for pair in products for a in pair])
    return [tuple(out[2 * i:2 * i + 2]) for i in range(n_prod)]


def _adamw_math(w, g, m, v):
    m = ADAM_B1 * m + (1.0 - ADAM_B1) * g
    v = ADAM_B2 * v + (1.0 - ADAM_B2) * jnp.square(g)
    m_hat = m / (1.0 - ADAM_B1 ** ADAM_STEP)
    v_hat = v / (1.0 - ADAM_B2 ** ADAM_STEP)
    delta = -ADAM_LR * (m_hat / (jnp.sqrt(v_hat) + ADAM_EPS) + ADAM_WD * w)
    return delta, m, v


def _adamw_shards(updates, name, chip_sums=()):
    names, nu, ns = list(updates), len(updates), len(chip_sums)

    def body(*refs):
        ins, sum_refs = refs[:5 * nu], refs[5 * nu:5 * nu + ns]
        outs = refs[5 * nu + ns:9 * nu + ns]
        landed_refs, scratch = refs[9 * nu + ns:9 * nu + 2 * ns], refs[9 * nu + 2 * ns:]
        if ns:
            start_chips, finish_chips = _chips_steps(sum_refs, landed_refs, *scratch)
            start_chips()
        for i in range(nu):
            o_ref, r_ref, w_ref, m_ref, v_ref = ins[5 * i:5 * i + 5]
            g_out, d_out, m_out, v_out = outs[4 * i:4 * i + 4]
            g = o_ref[...] + r_ref[0].astype(F32) + r_ref[1].astype(F32) + r_ref[2].astype(F32)
            g_out[...] = g
            d_out[...], m_out[...], v_out[...] = _adamw_math(w_ref[...], g, m_ref[...], v_ref[...])
        if ns:
            finish_chips()

    vmem = pl.BlockSpec(memory_space=pltpu.VMEM)
    out = pl.pallas_call(
        body, name=name,
        in_specs=[vmem] * (5 * nu) + [ANY] * ns, out_specs=[vmem] * (4 * nu) + [ANY] * ns,
        out_shape=[jax.ShapeDtypeStruct(updates[n][2].shape, F32) for n in names for _ in range(4)]
        + _chips_shapes(chip_sums),
        scratch_shapes=_chips_scratch(ns) if ns else [],
        compiler_params=pltpu.CompilerParams(vmem_limit_bytes=56 << 20),
    )(*[a for n in names for a in updates[n]], *chip_sums)
    return {n: out[4 * i:4 * i + 4] for i, n in enumerate(names)}, list(out[4 * nu:])


def _place():
    x, y, c = lax.axis_index("x"), lax.axis_index("y"), lax.axis_index("c")
    chips = [(1 - x, y), (x, 1 - y), (1 - x, 1 - y)]
    return x, y, c, chips


def _gather_steps(ins, outs, send, recv, lsem, own_barrier=True):
    nt = len(ins)
    x, y, c, (xn, yn, diag) = _place()
    me, sib = (x, y, c), (x, y, 1 - c)

    def slot(t, px, py, pc):
        return outs[t].at[4 * px + 2 * py + pc]

    def copy(t, k, block, to, src=None):
        return pltpu.make_async_remote_copy(
            src_ref=slot(t, *block) if src is None else src, dst_ref=slot(t, *block),
            send_sem=send.at[t, k], recv_sem=recv.at[t, k], device_id=to, device_id_type=MESH)

    mine = [pltpu.make_async_copy(ins[t], slot(t, *me), lsem.at[t]) for t in range(nt)]
    first = [copy(t, k, me, to, src=ins[t]) for t in range(nt) for k, to in ((0, sib), (1, (*xn, c)), (2, (*yn, c)))]

    if own_barrier:
        signal_peers, peers_are_in = _own_barrier([sib, (*xn, c), (*yn, c)])

    def enter():
        if own_barrier:
            signal_peers()
        for cp in mine:
            cp.start()

    def start():
        if own_barrier:
            peers_are_in()
        for cp in first:
            cp.start()

    def landed(k, chip, also_to=None):
        for t in range(nt):
            copy(t, k, (*chip, c), me).wait_recv()
            if also_to is not None:
                copy(t, 3, (*chip, c), (*also_to, c)).start()
            copy(t, 3 + k, (*chip, c), sib).start()

    def relay():
        @pl.when(c == 0)
        def _():
            landed(1, xn, also_to=yn)
            landed(2, yn)

        @pl.when(c == 1)
        def _():
            landed(2, yn, also_to=xn)
            landed(1, xn)

    def finish():
        landed(3, diag)
        for t in range(nt):
            copy(t, 0, sib, me).wait_recv()
            for k, chip in ((4, xn), (5, yn), (6, diag)):
                copy(t, k, (*chip, 1 - c), me).wait_recv()
            for k in range(7):
                copy(t, k, me, sib).wait_send()
        for cp in mine:
            cp.wait()

    return enter, start, relay, finish


def _gather_scratch(nt):
    return [pltpu.SemaphoreType.DMA((nt, 7)), pltpu.SemaphoreType.DMA((nt, 7)), pltpu.SemaphoreType.DMA((nt,))]


def _gathered_shapes(shards):
    return [jax.ShapeDtypeStruct((N_DEV,) + s.shape, s.dtype) for s in shards]


def _call_with_gather(body, n_grid, shards, *, name, in_specs, out_specs, out_shape, scratch_shapes, vmem_mb, args,
                      collective_id=None):
    assert (collective_id is None) == (not shards)
    ng, n_in, n_out = len(shards), len(in_specs), len(out_specs)

    def wrapped(*refs):
        ins, shard_refs = refs[:n_in], refs[n_in:n_in + ng]
        outs = refs[n_in + ng:n_in + ng + n_out]
        whole_refs = refs[n_in + ng + n_out:n_in + 2 * ng + n_out]
        scratch = refs[n_in + 2 * ng + n_out:]
        if ng:
            enter, start, relay, finish = _gather_steps(shard_refs, whole_refs, *scratch[len(scratch_shapes):])
            pl.when(pl.program_id(0) == 0)(enter)
            pl.when(pl.program_id(0) == 0)(start)
            pl.when(pl.program_id(0) == n_grid // 2)(relay)
        body(*ins, *outs, *scratch[:len(scratch_shapes)])
        if ng:
            pl.when(pl.program_id(0) == n_grid - 1)(finish)

    return pl.pallas_call(
        wrapped, grid=(n_grid,), name=name,
        in_specs=list(in_specs) + [ANY] * ng, out_specs=list(out_specs) + [ANY] * ng,
        out_shape=list(out_shape) + _gathered_shapes(shards),
        scratch_shapes=list(scratch_shapes) + (_gather_scratch(ng) if ng else []),
        compiler_params=_cparams(vmem_mb, **({"collective_id": collective_id} if shards else {})))(*args, *shards)


def _chips_steps(ins, outs, send, recv):
    _, _, c, chips = _place()
    copies = [pltpu.make_async_remote_copy(
        src_ref=ins[t].at[2 * px + py], dst_ref=outs[t].at[j], send_sem=send.at[t, j], recv_sem=recv.at[t, j],
        device_id=(px, py, c), device_id_type=MESH) for t in range(len(ins)) for j, (px, py) in enumerate(chips)]

    def start():
        for cp in copies:
            cp.start()

    def finish():
        for cp in copies:
            cp.wait()

    return start, finish


def _chips_scratch(nt):
    return [pltpu.SemaphoreType.DMA((nt, 3)), pltpu.SemaphoreType.DMA((nt, 3))]


def _chips_shapes(cs16s):
    return [jax.ShapeDtypeStruct((3,) + g.shape[1:], g.dtype) for g in cs16s]


SMALL = (("g_pre_mix", 0, 0, D), ("g_mem", 1, 0, D), ("g_post_mix", 2, 0, D), ("g_attn_out", 3, 0, AW),
         ("g_conv_out", 3, AW, CW), ("g_xattn_out", 3, AW + CW, XW), ("g_post_mlp", 4, 0, D), ("g_pre_mlp", 5, 0, D))
CONV_ROW = 8
PACK_ROWS = 16


LOSS_ROW = 15


def _small_reduce_steps(accs, tot_ref, pack, land, send, recv):
    acc_in, acc_mem, acc_mix, acc_mlp, acc_cw, acc_loss = accs
    x, y, c, _ = _place()
    me = 4 * x + 2 * y + c
    copies = []
    for k in range(1, N_DEV):
        kx, ky, kc = (k >> 2) & 1, (k >> 1) & 1, k & 1
        peer = (1 - x if kx else x, 1 - y if ky else y, 1 - c if kc else c)
        copies.append(pltpu.make_async_remote_copy(
            src_ref=pack, dst_ref=land.at[me], send_sem=send.at[k - 1], recv_sem=recv.at[k - 1],
            device_id=peer, device_id_type=MESH))

    def start():
        pack[...] = jnp.zeros_like(pack)
        pack[0:1, :] = acc_in[0:1, :]
        pack[1:2, :] = acc_mem[0:1, :]
        pack[2:4, :] = acc_mix[0:2, :]
        pack[4:6, :] = acc_mlp[0:2, :]
        pack[CONV_ROW:CONV_ROW + 3, 0:CW] = acc_cw[0:3, :]
        pack[LOSS_ROW:LOSS_ROW + 1, 0:LANES] = acc_loss[0:1, :]
        land[me] = pack[...]
        for cp in copies:
            cp.start()

    def finish():
        for cp in copies:
            cp.wait()
        tot = land[0]
        for s in range(1, N_DEV):
            tot = tot + land[s]
        tot_ref[...] = tot

    return start, finish


def _small_reduce_scratch():
    return [pltpu.VMEM((PACK_ROWS, D), F32), pltpu.VMEM((N_DEV, PACK_ROWS, D), F32),
            pltpu.SemaphoreType.DMA((N_DEV - 1,)), pltpu.SemaphoreType.DMA((N_DEV - 1,))]


def _small_update(tot, me, params):
    flat = [a for n, _, _, _ in SMALL for a in params[n]] + list(params["conv_w"])
    n_par = len(SMALL) + 1
    tap_cols = CW // N_DEV

    def body(*refs):
        me_ref, tot_ref = refs[0:2]
        ins = refs[2:2 + 3 * n_par]
        loss_out = refs[2 + 3 * n_par]
        outs = refs[3 + 3 * n_par:]
        tot = tot_ref[...]
        loss_out[...] = jnp.broadcast_to(tot[LOSS_ROW:LOSS_ROW + 1, 0:LANES], loss_out.shape)

        def update(i, g):
            w_ref, m_ref, v_ref = ins[3 * i:3 * i + 3]
            for o_ref, res in zip(outs[4 * i:4 * i + 4], (g,) + _adamw_math(w_ref[...], g, m_ref[...], v_ref[...])):
                if len(o_ref.shape) == 3:
                    for t in range(o_ref.shape[0]):
                        o_ref[t] = res[t:t + 1, :]
                else:
                    o_ref[...] = res

        for i, (_, row, lane0, width) in enumerate(SMALL):
            update(i, tot[row:row + 1, lane0:lane0 + width])
        me = me_ref[0]
        taps = pltpu.roll(tot[CONV_ROW:CONV_ROW + SUBLANES, 0:CW], jnp.where(me == 0, 0, CW - me * tap_cols), 1)
        update(n_par - 1, taps[0:3, 0:tap_cols])

    shapes = [jax.ShapeDtypeStruct(params[n][0].shape, F32) for n, _, _, _ in SMALL] + [
        jax.ShapeDtypeStruct((3, 1, tap_cols), F32)]
    vmem = pl.BlockSpec(memory_space=pltpu.VMEM)
    loss, *out = pl.pallas_call(
        body, name="small_update",
        in_specs=[pl.BlockSpec(memory_space=pltpu.SMEM)] + [vmem] * (1 + 3 * n_par),
        out_shape=[jax.ShapeDtypeStruct((SUBLANES, LANES), F32)] + [s for s in shapes for _ in range(4)],
    )(me, tot, *flat)
    names = [n for n, _, _, _ in SMALL] + ["conv_w"]
    return loss[0, 0], {n: out[4 * i:4 * i + 4] for i, n in enumerate(names)}


def _local_step(x, mem, pos, gains, shards, tgt, place):
    half = HEAD // 2
    inv_freq = jnp.float32(ROPE_THETA) ** (-(jnp.arange(half, dtype=F32) * 2.0 / HEAD))
    invf = jnp.tile(inv_freq, LANES // half)[None, :]
    sgn = jnp.tile(jnp.concatenate([-jnp.ones((half,), F32), jnp.ones((half,), F32)]), LANES // HEAD)[None, :]
    cos, sins, win8 = _rope_table(pos.astype(F32).reshape(S, 1), invf, sgn, [shards["w_in"]])
    wdn_left, wdn_right = shards["w_down"][:, 0:D // 2], shards["w_down"][:, D // 2:]
    q, kvp, bcu, qx16, h16, win16, wout8, wkv8, conv8, wdn8_right = _in_proj(
        x, gains["g_pre_mix"], win8, cos, sins, [shards["w_out"], shards["w_mem_kv"], shards["conv_w"], wdn_right])
    wout16, wkv16 = wout8.reshape(D, D), wkv8.reshape(D, 2 * XW)
    cw_full = conv8[:, 0:3, 0:CW // N_DEV].transpose(1, 0, 2).reshape(3, CW)
    cw8 = jnp.zeros((SUBLANES, CW), F32).at[0:3].set(cw_full)
    y_attn, ltot, wup8, wdn8_left = _attn_fwd(q, kvp, [shards["w_up"], wdn_left])
    wdn_halves = (wdn8_left.reshape(FF, D // 2), wdn8_right.reshape(FF, D // 2))
    memn16, kv16 = _mem_fwd(mem, gains["g_mem"], wkv16)
    ypre, y16, y2, x1 = _mix_out(y_attn, bcu, qx16, kv16, cw8, gains["g_attn_out"], gains["g_conv_out"],
                                 gains["g_xattn_out"], gains["g_post_mix"], wout16, x, [])
    a16, du16, h2_16, df2_16, dx1, loss8, dg_mlp = _mlp(
        x1, tgt, gains["g_pre_mlp"], gains["g_post_mlp"], wup8, wdn_halves)

    sums = {"w_up": _wgrad_cols(place, h2_16, du16, FF_BLK, "wgrad_up", ID_WGRAD_UP),
            "w_down": _wgrad_cols(place, df2_16, a16, FF_BLK, "wgrad_down", ID_WGRAD_DOWN, square_b=True,
                                  transpose_out=True)}

    head_id = jnp.arange(AW, dtype=jnp.int32) // HEAD
    head_ones = (head_id[:, None] == head_id[None, :]).astype(BF16)
    dy2_16, qdo, ld, dbcu, dqx, dgs, dcw, dkv = _mix_out_bwd(
        dx1, y2, ypre, ltot, head_ones, q, bcu, qx16, kv16, cw8, gains["g_post_mix"], gains["g_attn_out"],
        gains["g_conv_out"], gains["g_xattn_out"], wout16)
    dkv16, dg_mem = _mem_bwd(mem, gains["g_mem"], wkv16, dkv)
    sums["w_mem_kv"], sums["w_out"] = _wgrad_rows(place, [(memn16, dkv16), (y16, dy2_16)], "wgrad_mem_kv_out")
    out = _attn_bwd(qdo, kvp, ld, [s[0] for s in sums.values()])
    dqkv, landed = out[:9], out[9:]
    reduced = {n: (s[1], landed[t]) for t, (n, s) in enumerate(sums.items())}
    dproj16, grad_x, dg_in = _in_proj_bwd(dqkv, dbcu, dqx, cos, sins, win16, x, gains["g_pre_mix"], dx1)

    _, in_own, in_landed, small_tot = _wgrad_cols(place, h16, dproj16, PW // N_DEV, "wgrad_in", ID_WGRAD_IN,
                                                  transpose_out=True, to_chips=True,
                                                  small=(dg_in, dg_mem, dgs, dg_mlp, dcw, loss8))
    reduced["w_in"] = (in_own, in_landed)
    return grad_x, reduced, small_tot


BIG = ("w_in", "w_mem_kv", "w_out", "w_up", "w_down")
ORDER = ("g_pre_mix", "g_mem", "w_in", "w_mem_kv", "conv_w", "g_attn_out", "g_conv_out", "g_xattn_out", "w_out",
         "g_post_mix", "g_pre_mlp", "w_up", "w_down", "g_post_mlp")


def kernel(x, mem, positions, g_pre_mix, g_mem, w_in, w_mem_kv, conv_w, g_attn_out, g_conv_out, g_xattn_out, w_out, g_post_mix, g_pre_mlp, w_up, w_down, g_post_mlp, loss_target, m_g_pre_mix, m_g_mem, m_w_in, m_w_mem_kv, m_conv_w, m_g_attn_out, m_g_conv_out, m_g_xattn_out, m_w_out, m_g_post_mix, m_g_pre_mlp, m_w_up, m_w_down, m_g_post_mlp, v_g_pre_mix, v_g_mem, v_w_in, v_w_mem_kv, v_conv_w, v_g_attn_out, v_g_conv_out, v_g_xattn_out, v_w_out, v_g_post_mix, v_g_pre_mlp, v_w_up, v_w_down, v_g_post_mlp):
    w = dict(g_pre_mix=g_pre_mix, g_mem=g_mem, w_in=w_in, w_mem_kv=w_mem_kv, conv_w=conv_w, g_attn_out=g_attn_out,
             g_conv_out=g_conv_out, g_xattn_out=g_xattn_out, w_out=w_out, g_post_mix=g_post_mix, g_pre_mlp=g_pre_mlp,
             w_up=w_up, w_down=w_down, g_post_mlp=g_post_mlp)
    mo = dict(g_pre_mix=m_g_pre_mix, g_mem=m_g_mem, w_in=m_w_in, w_mem_kv=m_w_mem_kv, conv_w=m_conv_w,
              g_attn_out=m_g_attn_out, g_conv_out=m_g_conv_out, g_xattn_out=m_g_xattn_out, w_out=m_w_out,
              g_post_mix=m_g_post_mix, g_pre_mlp=m_g_pre_mlp, w_up=m_w_up, w_down=m_w_down, g_post_mlp=m_g_post_mlp)
    vo = dict(g_pre_mix=v_g_pre_mix, g_mem=v_g_mem, w_in=v_w_in, w_mem_kv=v_w_mem_kv, conv_w=v_conv_w,
              g_attn_out=v_g_attn_out, g_conv_out=v_g_conv_out, g_xattn_out=v_g_xattn_out, w_out=v_w_out,
              g_post_mix=v_g_post_mix, g_pre_mlp=v_g_pre_mlp, w_up=v_w_up, w_down=v_w_down, g_post_mlp=v_g_post_mlp)

    xi, yi, ci = lax.axis_index("x"), lax.axis_index("y"), lax.axis_index("c")
    me = 4 * xi + 2 * yi + ci
    place = jnp.stack([ci, 2 * xi + yi]).astype(jnp.int32)

    shards = {n: w[n][0].astype(BF16) for n in BIG}
    shards["conv_w"] = jnp.zeros((SUBLANES, LANES), F32).at[0:3, 0:CW // N_DEV].set(conv_w[0])

    gains = {n: w[n] for n, _, _, _ in SMALL}
    grad_x, reduced, small_tot = _local_step(x[0], mem[0], positions[0], gains, shards, loss_target[0], place)

    def shard(n, a):
        return a[0].T if n == "w_in" else a[0]

    updated = {}
    for group in (("w_up", "w_down"), ("w_in", "w_out", "w_mem_kv")):
        updated.update(_adamw_shards({n: (*reduced[n], shard(n, w[n]), shard(n, mo[n]), shard(n, vo[n]))
                                      for n in group}, "adamw_" + "_".join(group))[0])
    grad, delta, new_m, new_v = {}, {}, {}, {}
    for n, res in updated.items():
        grad[n], delta[n], new_m[n], new_v[n] = [(a.T if n == "w_in" else a)[None] for a in res]

    params = {n: (w[n], mo[n], vo[n]) for n, _, _, _ in SMALL}
    params["conv_w"] = (w["conv_w"][0], mo["conv_w"][0], vo["conv_w"][0])
    loss, small = _small_update(small_tot, me.reshape(1).astype(jnp.int32), params)
    for n, (g, d_, m_, v_) in small.items():
        lead = (lambda a: a.reshape(conv_w.shape)) if n == "conv_w" else (lambda a: a)
        grad[n], delta[n], new_m[n], new_v[n] = lead(g), lead(d_), lead(m_), lead(v_)

    return (loss, grad_x[None], *[grad[n] for n in ORDER], *[delta[n] for n in ORDER],
            *[new_m[n] for n in ORDER], *[new_v[n] for n in ORDER])
```

```python
import jax
import jax.numpy as jnp
from jax import lax
from jax.experimental import pallas as pl
from jax.experimental.pallas import tpu as pltpu

F32, BF16 = jnp.float32, jnp.bfloat16
MESH = pl.DeviceIdType.MESH
ANY = pl.BlockSpec(memory_space=pl.ANY)

N_DEV = 8
D = 1024
S = 4096
N_MEM = 256
HEAD = 64
AW, CW, XW = 512, 256, 256
PW = 3 * AW + 3 * CW + XW
FF = 4096
FF_BLK = FF // N_DEV
EPS = 1e-6
NEG = -1e30
SCALE = HEAD ** -0.5
ROPE_THETA = 10000.0
LANES = 128
SUBLANES = 8

ADAM_LR, ADAM_B1, ADAM_B2, ADAM_EPS, ADAM_WD, ADAM_STEP = 0.001, 0.9, 0.999, 1e-08, 0.01, 10

TQ = 512
TQ_MLP = 512
NT = S // TQ


def _cparams(vmem_mb, n_grid=1, **more):
    return pltpu.CompilerParams(dimension_semantics=("arbitrary",) * n_grid, vmem_limit_bytes=vmem_mb << 20, **more)


def _const(shape):
    nd = len(shape)
    return pl.BlockSpec(shape, lambda *_: (0,) * nd, pipeline_mode=pl.Buffered(1))


def _acc(shape):
    nd = len(shape)
    return pl.BlockSpec(shape, lambda *_: (0,) * nd)


def _tokens_in_lanes(tq):
    return pl.BlockSpec((D, tq), lambda i: (0, i))


def _dot(a, b):
    return jnp.dot(a, b, preferred_element_type=F32)


def _dot_nt(a, b):
    return lax.dot_general(a, b, (((1,), (1,)), ((), ())), preferred_element_type=F32)


def _dot_tn(a, b):
    return lax.dot_general(a, b, (((0,), (0,)), ((), ())), preferred_element_type=F32)


def _rms(x, g):
    r = lax.rsqrt(jnp.mean(x * x, axis=-1, keepdims=True) + EPS)
    n = x * r
    return n * g, n, r


def _rms_bwd(dy, n, r, g):
    dn = dy * g
    dx = r * (dn - n * jnp.mean(dn * n, axis=-1, keepdims=True))
    return dx, jnp.sum(dy * n, axis=0, keepdims=True)


def _rot_half(t):
    lane = lax.broadcasted_iota(jnp.int32, t.shape, 1)
    n = t.shape[1]
    return jnp.where((lane % HEAD) < HEAD // 2, pltpu.roll(t, n - HEAD // 2, 1), pltpu.roll(t, HEAD // 2, 1))


def _rope_table(pos_col, invf, sgn, shards):
    def body(p_ref, f_ref, s_ref, c_out, s_out):
        ang = p_ref[...] * f_ref[...]
        c_out[...] = jnp.cos(ang)
        s_out[...] = jnp.sin(ang) * s_ref[...]

    tile = pl.BlockSpec((TQ, LANES), lambda i: (i, 0))
    return _call_with_gather(
        body, NT, shards, name="rope_table",
        in_specs=[pl.BlockSpec((TQ, 1), lambda i: (i, 0)), _const((1, LANES)), _const((1, LANES))],
        out_specs=[tile, tile], out_shape=[jax.ShapeDtypeStruct((S, LANES), F32)] * 2,
        scratch_shapes=[], vmem_mb=32, args=(pos_col, invf, sgn), collective_id=ID_ROPE_TABLE)


def _all_heads(t):
    return jnp.tile(t, (1, AW // LANES))


def _mem_fwd(mem, g_mem, wkv16):
    def body(m_ref, g_ref, w_ref, n16_ref, kv_ref):
        y, _, _ = _rms(m_ref[...], g_ref[...])
        y16 = y.astype(BF16)
        n16_ref[...] = y16.T
        kv_ref[...] = _dot(y16, w_ref[...]).astype(BF16)

    return pl.pallas_call(
        body, name="mem_fwd",
        out_shape=[jax.ShapeDtypeStruct((D, N_MEM), BF16), jax.ShapeDtypeStruct((N_MEM, 2 * XW), BF16)],
        compiler_params=pltpu.CompilerParams(vmem_limit_bytes=32 << 20))(mem, g_mem, wkv16)


def _in_proj(x, g, w8, cos, sins, shards):
    blk = PW // N_DEV

    def body(x_ref, g_ref, w8_ref, c_ref, s_ref, q_ref, kv_ref, bcu_ref, qx_ref, h_ref, w_out, w_ref):
        @pl.when(pl.program_id(0) == 0)
        def _():
            for j in range(N_DEV):
                w_ref[:, j * blk:(j + 1) * blk] = w8_ref[j]
            w_out[...] = w_ref[...]

        y, _, _ = _rms(x_ref[...], g_ref[...])
        h = y.astype(BF16)
        h_ref[...] = h.T
        proj = _dot(h, w_ref[...])
        cos, sn = _all_heads(c_ref[...]), _all_heads(s_ref[...])
        q, k = proj[:, 0:AW], proj[:, AW:2 * AW]
        q_ref[...] = (q * cos + _rot_half(q) * sn) * SCALE
        kv_ref[...] = _pack_pair(k * cos + _rot_half(k) * sn, proj[:, 2 * AW:3 * AW])
        bcu_ref[...] = proj[:, 3 * AW:3 * AW + 3 * CW]
        qx_ref[...] = (proj[:, 3 * AW + 3 * CW:] * SCALE).astype(BF16)

    def tile(w):
        return pl.BlockSpec((TQ, w), lambda i: (i, 0))

    return _call_with_gather(
        body, NT, shards, name="in_proj",
        in_specs=[tile(D), _const((1, D)), _const((N_DEV, D, blk)), tile(LANES), tile(LANES)],
        out_specs=[tile(AW), tile(AW), tile(3 * CW), tile(XW), _tokens_in_lanes(TQ), _acc((D, PW))],
        out_shape=[jax.ShapeDtypeStruct((S, AW), F32)] * 2 + [
            jax.ShapeDtypeStruct((S, 3 * CW), F32), jax.ShapeDtypeStruct((S, XW), BF16),
            jax.ShapeDtypeStruct((D, S), BF16), jax.ShapeDtypeStruct((D, PW), BF16)],
        scratch_shapes=[pltpu.VMEM((D, PW), BF16)], vmem_mb=56, args=(x, g, w8, cos, sins),
        collective_id=ID_IN_PROJ)


ATTN_PLANS = (("p1", 1, 128, 32), ("p4", 8, 64, 8), ("p16", 16, 128, 2))
PAD = 128
WIN = 256


ATTN_UNROLL = 16


def _fill_bias(tab, qblk, partner):
    qi = lax.broadcasted_iota(jnp.int32, (2 * qblk, WIN), 0) & (qblk - 1)
    kj = lax.broadcasted_iota(jnp.int32, (2 * qblk, WIN), 1)
    piece = kj >> (qblk.bit_length() - 1)
    kk = kj & (qblk - 1)
    prev = (piece & 1) == 0
    of_partner = piece >= 2
    for first in (0, 1):
        for par in (0, 1):
            lo = jnp.where(prev, (qblk if first else qi) + jnp.where(of_partner, par, 0), 0)
            hi = jnp.where(prev, qblk, qi + jnp.where(of_partner, par - 1, 0))
            tab[2 * first + par] = jnp.where((kk >= lo) & (kk <= hi), 0.0, NEG).astype(F32)


def _block_rows(g, qblk, nbc, partner):
    own = pl.ds(pl.multiple_of(PAD + g * qblk, qblk), qblk)
    first = ((g & (nbc - 1)) == 0).astype(jnp.int32)
    if partner:
        gp = jnp.bitwise_xor(g, 4 * nbc)
        wins = (pl.ds(pl.multiple_of(PAD + (g - 1) * qblk, qblk), 2 * qblk),
                pl.ds(pl.multiple_of(PAD + (gp - 1) * qblk, qblk), 2 * qblk))
        return own, wins, 2 * first + ((g >> ((4 * nbc).bit_length() - 1)) & 1)
    return own, (pl.ds(pl.multiple_of(PAD + (g - 1) * qblk, qblk), 2 * qblk),), 2 * first


def _pack_pair(lo, hi):
    lo_bits = lax.bitcast_convert_type(lo.astype(BF16).astype(F32), jnp.uint32) >> 16
    hi_bits = lax.bitcast_convert_type(hi.astype(BF16).astype(F32), jnp.uint32) & jnp.uint32(0xFFFF0000)
    return lax.bitcast_convert_type(hi_bits | lo_bits, F32)


def _unpack_pair(c):
    bits = lax.bitcast_convert_type(c, jnp.uint32)
    lo = lax.bitcast_convert_type(bits << 16, F32).astype(BF16)
    hi = lax.bitcast_convert_type(bits & jnp.uint32(0xFFFF0000), F32).astype(BF16)
    return lo, hi


def _window(ref, wins):
    parts = [ref[w, :] for w in wins]
    return parts[0] if len(parts) == 1 else jnp.concatenate(parts, axis=0)


def _stack_heads(t, lane):
    zero = jnp.zeros_like(t)
    return jnp.concatenate([jnp.where(lane < HEAD, t, zero), jnp.where(lane >= HEAD, t, zero)], axis=0)


def _unstack_heads(t2, lane):
    half = t2.shape[0] // 2
    return jnp.where(lane < HEAD, t2[0:half, :], t2[half:, :])


def _lanes_of(step):
    return pl.ds(pl.multiple_of(step * LANES, LANES), LANES)


def _whole_wait(buf, sem):
    whole = buf.at[pl.ds(PAD, S), :]
    return pltpu.make_async_copy(whole, whole, sem)


def _whole_waits(bufs, sems):
    return [_whole_wait(buf, sems.at[i]) for i, buf in enumerate(bufs)]


def _class_gather(views, bufs, sems, lanes):
    copies = []
    for i, (view, buf) in enumerate(zip(views, bufs)):
        if view.ndim == 2:
            copies.append(pltpu.make_async_copy(view.at[:, lanes], buf.at[pl.ds(PAD, S), :], sems.at[i]))
        else:
            per, n_cls = view.shape[0], view.shape[1]
            copies += [pltpu.make_async_copy(view.at[:, c, lanes], buf.at[pl.ds(PAD + c * per, per), :], sems.at[i])
                       for c in range(n_cls)]
    return copies


def _class_scatter(bufs, dsts, sems, lanes):
    copies = []
    for i, (buf, dst) in enumerate(zip(bufs, dsts)):
        if dst.ndim == 2:
            copies.append(pltpu.make_async_copy(buf.at[pl.ds(PAD, S), :], dst.at[:, lanes], sems.at[i]))
            continue
        per, n_cls = dst.shape[0], dst.shape[1]
        copies += [pltpu.make_async_copy(buf.at[pl.ds(PAD + c * per, per), :], dst.at[:, c, lanes], sems.at[i])
                   for c in range(n_cls)]
    return copies


def _start(copies):
    for cp in copies:
        cp.start()


def _wait(waits):
    for w in waits:
        w.wait()


def _attn_fwd(q, kvp, shards=()):
    views = [[a] + [a.reshape(S // n, n, AW) for _, n, _, _ in ATTN_PLANS[1:]] for a in (q, kvp)]
    flat = [views[a][p] for p in range(3) for a in range(2)]
    ng = len(shards)
    n_grid = AW // LANES

    def body(*refs):
        hbm = [refs[2 * p:2 * p + 2] for p in range(3)]
        refs = refs[6:]
        shard_refs, refs = refs[:ng], refs[ng:]
        y_ref, lt_ref = refs[0:2]
        whole_refs, refs = refs[2:2 + ng], refs[2 + ng:]
        bufs = [refs[2 * p:2 * p + 2] for p in range(3)]
        oc4, lc4, oc16, lc16, tab128, tab4, sem_in = refs[6:13]
        step = pl.program_id(0)
        if ng:
            enter_gather, start_gather, relay_gather, finish_gather = _gather_steps(
                shard_refs, whole_refs, *refs[13:], own_barrier=False)
            pl.when(step == 0)(enter_gather)
            pl.when(step == 0)(start_gather)
            pl.when(step == n_grid // 2)(relay_gather)
        now = [_class_gather(hbm[p], bufs[p], sem_in.at[p], _lanes_of(step)) for p in range(3)]
        nxt = [_class_gather(hbm[p], bufs[p], sem_in.at[p], _lanes_of(step + 1)) for p in range(3)]

        @pl.when(step == 0)
        def _():
            for p in range(3):
                _start(now[p])
                for b in bufs[p]:
                    b[0:PAD, :] = jnp.zeros((PAD, LANES), F32)
            _fill_bias(tab128, 128, False)
            _fill_bias(tab4, 64, True)

        def prefetch(p):
            pl.when(step + 1 < n_grid)(lambda: _start(nxt[p]))

        lane = lax.broadcasted_iota(jnp.int32, (1, LANES), 1)
        ones = jnp.ones((WIN, LANES), BF16)

        def run(plan, bq, bkv, tab, o_dst, l_dst, dst_pad):
            _, n_cls, qblk, nbc = plan
            partner = n_cls == 8

            def block(g, carry):
                own, wins, mask = _block_rows(g, qblk, nbc, partner)
                q2 = _stack_heads(bq[own, :].astype(BF16), lane)
                kw, vwin = _unpack_pair(_window(bkv, wins))
                vw = jnp.concatenate([vwin, ones], axis=1)
                s = _dot_nt(q2, kw) + tab[mask]
                m = jnp.max(s, axis=1, keepdims=True)
                oe = _dot(jnp.exp(s - m).astype(BF16), vw)
                den = oe[:, LANES:]
                dst = pl.ds(pl.multiple_of(dst_pad + g * qblk, qblk), qblk)
                o_dst[dst, :] = _unstack_heads(oe[:, 0:LANES] / den, lane)
                l_dst[dst, :] = _unstack_heads(m + jnp.log(den), lane)
                return carry
            lax.fori_loop(0, n_cls * nbc, block, 0, unroll=ATTN_UNROLL)

        _wait(_whole_waits(bufs[0], sem_in.at[0]))
        run(ATTN_PLANS[0], *bufs[0], tab128, y_ref, lt_ref, 0)
        prefetch(0)
        _wait(_whole_waits(bufs[1], sem_in.at[1]))
        run(ATTN_PLANS[1], *bufs[1], tab4, oc4, lc4, PAD)
        prefetch(1)
        _wait(_whole_waits(bufs[2], sem_in.at[2]))
        run(ATTN_PLANS[2], *bufs[2], tab128, oc16, lc16, PAD)
        prefetch(2)

        n_rows = 64

        def token_order(buf, t, n_cls):
            per = S // n_cls
            first = PAD + t * (n_rows // n_cls)
            return jnp.concatenate([buf[pl.ds(first + jj, n_cls, stride=per), :] for jj in range(n_rows // n_cls)],
                                   axis=0)

        def combine(t, carry):
            rows = pl.ds(pl.multiple_of(t * n_rows, n_rows), n_rows)
            l0, l1, l2 = lt_ref[rows, :], token_order(lc4, t, 8), token_order(lc16, t, 16)
            lm = jnp.maximum(jnp.maximum(l0, l1), l2)
            e0, e1, e2 = jnp.exp(l0 - lm), jnp.exp(l1 - lm), jnp.exp(l2 - lm)
            den = e0 + e1 + e2
            y_ref[rows, :] = (e0 * y_ref[rows, :] + e1 * token_order(oc4, t, 8)
                              + e2 * token_order(oc16, t, 16)) / den
            lt_ref[rows, :] = lm + jnp.log(den)
            return carry
        lax.fori_loop(0, S // n_rows, combine, 0, unroll=2)

        if ng:
            pl.when(step == n_grid - 1)(finish_gather)

    col = pl.BlockSpec((S, LANES), lambda h: (0, h))
    padded = pltpu.VMEM((PAD + S, LANES), F32)
    return pl.pallas_call(
        body, grid=(n_grid,), name="attn_fwd",
        in_specs=[ANY] * (6 + ng), out_specs=[col, col] + [ANY] * ng,
        out_shape=[jax.ShapeDtypeStruct((S, AW), F32)] * 2 + _gathered_shapes(shards),
        scratch_shapes=[padded] * 10 + [
            pltpu.VMEM((4, 256, WIN), F32), pltpu.VMEM((4, 128, WIN), F32), pltpu.SemaphoreType.DMA((3, 2))]
        + (_gather_scratch(ng) if ng else []),
        compiler_params=_cparams(56))(*flat, *shards)


def _conv_taps(z, zprev, row):
    z1 = jnp.where(row == 0, zprev[7:8, :], pltpu.roll(z, 1, 0))
    z2 = jnp.where(row == 0, zprev[6:7, :], jnp.where(row == 1, zprev[7:8, :], pltpu.roll(z, 2, 0)))
    return z1, z2


def _xattn_scores(qm, km):
    s = _dot_nt(qm, km)
    m = jnp.max(s, axis=1, keepdims=True)
    e = jnp.exp(s - m)
    return e, jnp.sum(e, axis=1, keepdims=True)


def _mix_out(y_attn, bcu, qx16, kv16, cw8, g_attn, g_conv, g_x, g_post, wout16, x, shards):
    def body(ya_ref, bcu_ref, halo_ref, qx_ref, kv_ref, cw_ref, ga_ref, gc_ref, gx_ref, gp_ref, w_ref, x_ref,
             ypre_ref, y16_ref, y2_ref, x1_ref):
        i = pl.program_id(0)
        bcu = bcu_ref[...]
        b, c, u = bcu[:, 0:CW], bcu[:, CW:2 * CW], bcu[:, 2 * CW:]
        z = c * u
        halo = halo_ref[...]
        zprev = jnp.where(i > 0, halo[:, CW:2 * CW] * halo[:, 2 * CW:], 0.0)
        row = lax.broadcasted_iota(jnp.int32, z.shape, 0)
        z1, z2 = _conv_taps(z, zprev, row)
        cw = cw_ref[...]
        y_conv = b * (z2 * cw[0:1, :] + z1 * cw[1:2, :] + z * cw[2:3, :])

        qx = qx_ref[...]
        kv = kv_ref[...]
        km, vm = kv[:, 0:XW], kv[:, XW:]
        lane = lax.broadcasted_iota(jnp.int32, qx.shape, 1)
        y_x = jnp.zeros(qx.shape, F32)
        for h in range(XW // HEAD):
            hm = (lane >= h * HEAD) & (lane < (h + 1) * HEAD)
            e, l = _xattn_scores(jnp.where(hm, qx, jnp.zeros_like(qx)), km)
            y_x = jnp.where(hm, _dot(e.astype(BF16), vm) / l, y_x)

        y_attn = ya_ref[...]
        ypre_ref[:, 0:AW] = y_attn
        ypre_ref[:, AW:AW + CW] = y_conv
        ypre_ref[:, AW + CW:] = y_x
        y = jnp.concatenate([_rms(y_attn, ga_ref[...])[0], _rms(y_conv, gc_ref[...])[0],
                             _rms(y_x, gx_ref[...])[0]], axis=1).astype(BF16)
        y16_ref[...] = y.T
        y2 = _dot(y, w_ref[...])
        y2_ref[...] = y2
        x1_ref[...] = x_ref[...] + _rms(y2, gp_ref[...])[0]

    def tile(w):
        return pl.BlockSpec((TQ, w), lambda i: (i, 0))

    halo = pl.BlockSpec((SUBLANES, 3 * CW), lambda i: (jnp.maximum(i * (TQ // SUBLANES) - 1, 0), 0))
    return _call_with_gather(
        body, NT, shards, name="mix_out",
        in_specs=[tile(AW), tile(3 * CW), halo, tile(XW), _const((N_MEM, 2 * XW)), _const((SUBLANES, CW)),
                  _const((1, AW)), _const((1, CW)), _const((1, XW)), _const((1, D)), _const((D, D)), tile(D)],
        out_specs=[tile(D), _tokens_in_lanes(TQ), tile(D), tile(D)],
        out_shape=[jax.ShapeDtypeStruct((S, D), F32), jax.ShapeDtypeStruct((D, S), BF16),
                   jax.ShapeDtypeStruct((S, D), F32), jax.ShapeDtypeStruct((S, D), F32)],
        scratch_shapes=[], vmem_mb=56,
        args=(y_attn, bcu, bcu, qx16, kv16, cw8, g_attn, g_conv, g_x, g_post, wout16, x))


def _mlp(x1, tgt, g_pre, g_post, wup8, wdn_halves):
    tq = TQ_MLP
    half = D // 2

    def body(x1_ref, t_ref, g1_ref, g2_ref, wu_ref, wda_ref, wdb_ref,
             a16_ref, du_ref, h2_ref, df2_ref, dx1_ref, loss_ref, dg_ref):
        @pl.when(pl.program_id(0) == 0)
        def _():
            loss_ref[...] = jnp.zeros_like(loss_ref)
            dg_ref[...] = jnp.zeros_like(dg_ref)

        x1 = x1_ref[...]
        g1, g2 = g1_ref[...], g2_ref[...]
        y1, n1, r1 = _rms(x1, g1)
        h2 = y1.astype(BF16)
        h2_ref[...] = h2.T
        f2a = jnp.zeros((tq, half), F32)
        f2b = jnp.zeros((tq, half), F32)
        for j in range(N_DEV):
            cols = slice(j * FF_BLK, (j + 1) * FF_BLK)
            a = jnp.maximum(_dot(h2, wu_ref[j]), 0.0)
            a16_ref[:, cols] = a.astype(BF16)
            f = (a * a).astype(BF16)
            f2a = f2a + _dot(f, wda_ref[cols, :])
            f2b = f2b + _dot(f, wdb_ref[cols, :])
        f2 = jnp.concatenate([f2a, f2b], axis=1)
        y2, n2, r2 = _rms(f2, g2)
        e = x1 + y2 - t_ref[...]
        sq = jnp.sum(jnp.sum(e * e, axis=1, keepdims=True), axis=0, keepdims=True)
        loss_ref[...] += jnp.broadcast_to(sq * (0.5 / D), loss_ref.shape)
        dout = e * (1.0 / D)
        df2, dg2 = _rms_bwd(dout, n2, r2, g2)
        df2_16 = df2.astype(BF16)
        df2_ref[...] = df2_16.T
        dh2 = jnp.zeros((tq, D), F32)
        for j in range(N_DEV):
            cols = slice(j * FF_BLK, (j + 1) * FF_BLK)
            df = _dot_nt(df2_16[:, 0:half], wda_ref[cols, :]) + _dot_nt(df2_16[:, half:], wdb_ref[cols, :])
            du = (df * (2.0 * a16_ref[:, cols].astype(F32))).astype(BF16)
            du_ref[:, cols] = du
            dh2 = dh2 + _dot_nt(du, wu_ref[j])
        dx, dg1 = _rms_bwd(dh2, n1, r1, g1)
        dx1_ref[...] = dout + dx
        dg_ref[0:1, :] += dg2
        dg_ref[1:2, :] += dg1

    def tile(w):
        return pl.BlockSpec((tq, w), lambda i: (i, 0))

    return pl.pallas_call(
        body, grid=(S // tq,), name="mlp",
        in_specs=[tile(D), tile(D), _const((1, D)), _const((1, D)), _const((N_DEV, D, FF_BLK)), _const((FF, half)), _const((FF, half))],
        out_specs=[tile(FF), tile(FF), _tokens_in_lanes(tq), _tokens_in_lanes(tq), tile(D),
                   _acc((SUBLANES, LANES)), _acc((SUBLANES, D))],
        out_shape=[jax.ShapeDtypeStruct((S, FF), BF16), jax.ShapeDtypeStruct((S, FF), BF16),
                   jax.ShapeDtypeStruct((D, S), BF16), jax.ShapeDtypeStruct((D, S), BF16),
                   jax.ShapeDtypeStruct((S, D), F32), jax.ShapeDtypeStruct((SUBLANES, LANES), F32),
                   jax.ShapeDtypeStruct((SUBLANES, D), F32)],
        compiler_params=_cparams(60))(x1, tgt, g_pre, g_post, wup8, *wdn_halves)


def _mix_out_bwd(dx1, y2, ypre, ltot, head_ones, q, bcu, qx16, kv16, cw8, g_post, g_attn, g_conv, g_x, wout16):
    def body(dx1_ref, y2_ref, ypre_ref, lt_ref, e_ref, q_ref, bcu_ref, halo_ref, qx_ref, kv_ref, cw_ref, gp_ref,
             ga_ref, gc_ref, gx_ref, w_ref, dy2_ref, qdo_ref, ld_ref, dbcu_ref, dqx_ref, dgs_ref, dcw_ref, dkv_ref,
             carry):
        i = pl.program_id(0)

        @pl.when(i == 0)
        def _():
            dgs_ref[...] = jnp.zeros_like(dgs_ref)
            dcw_ref[...] = jnp.zeros_like(dcw_ref)
            dkv_ref[...] = jnp.zeros_like(dkv_ref)
            carry[...] = jnp.zeros_like(carry)

        gp = gp_ref[...]
        _, n, r = _rms(y2_ref[...], gp)
        dy2, dgp = _rms_bwd(dx1_ref[...], n, r, gp)
        dy2_16 = dy2.astype(BF16)
        dy2_ref[...] = dy2_16
        dy = _dot_nt(dy2_16, w_ref[...])

        ypre = ypre_ref[...]
        ga, gc, gx = ga_ref[...], gc_ref[...], gx_ref[...]
        _, na, ra = _rms(ypre[:, 0:AW], ga)
        dya, dga = _rms_bwd(dy[:, 0:AW], na, ra, ga)
        _, nc, rc = _rms(ypre[:, AW:AW + CW], gc)
        dyc, dgc = _rms_bwd(dy[:, AW:AW + CW], nc, rc, gc)
        y_x = ypre[:, AW + CW:]
        _, nx, rx = _rms(y_x, gx)
        dyx, dgx = _rms_bwd(dy[:, AW + CW:], nx, rx, gx)
        qdo_ref[...] = _pack_pair(q_ref[...], dya)
        prod = dya * ypre[:, 0:AW]
        hi = prod.astype(BF16)
        lo = (prod - hi.astype(F32)).astype(BF16)
        head_sum = _dot(hi, e_ref[...]) + _dot(lo, e_ref[...])
        lane_a = lax.broadcasted_iota(jnp.int32, prod.shape, 1)
        ld_ref[...] = jnp.where((lane_a % HEAD) < HEAD // 2, lt_ref[...], head_sum)
        dgs_ref[0:1, :] += dgp
        dgs_ref[1:2, :] += jnp.concatenate([dga, dgc, dgx], axis=1)

        bcu = bcu_ref[...]
        b, c, u = bcu[:, 0:CW], bcu[:, CW:2 * CW], bcu[:, 2 * CW:]
        z = c * u
        halo = halo_ref[...]
        zprev = jnp.where(i < NT - 1, halo[:, CW:2 * CW] * halo[:, 2 * CW:], 0.0)
        row = lax.broadcasted_iota(jnp.int32, z.shape, 0)
        z1, z2 = _conv_taps(z, zprev, row)
        cw = cw_ref[...]
        conv = z2 * cw[0:1, :] + z1 * cw[1:2, :] + z * cw[2:3, :]
        dconv = dyc * b
        nxt = carry[...]
        dn1 = jnp.where(row == TQ - 1, nxt[0:1, :], pltpu.roll(dconv, TQ - 1, 0))
        dn2 = jnp.where(row == TQ - 1, nxt[1:2, :], jnp.where(row == TQ - 2, nxt[0:1, :], pltpu.roll(dconv, TQ - 2, 0)))
        carry[...] = dconv[0:SUBLANES, :]
        dz = dconv * cw[2:3, :] + dn1 * cw[1:2, :] + dn2 * cw[0:1, :]
        dbcu_ref[:, 0:CW] = (dyc * conv).astype(BF16)
        dbcu_ref[:, CW:2 * CW] = (dz * u).astype(BF16)
        dbcu_ref[:, 2 * CW:] = (dz * c).astype(BF16)
        dcw_ref[0:1, :] += jnp.sum(z2 * dconv, axis=0, keepdims=True)
        dcw_ref[1:2, :] += jnp.sum(z1 * dconv, axis=0, keepdims=True)
        dcw_ref[2:3, :] += jnp.sum(z * dconv, axis=0, keepdims=True)

        qx = qx_ref[...]
        kv = kv_ref[...]
        km, vm = kv[:, 0:XW], kv[:, XW:]
        lane = lax.broadcasted_iota(jnp.int32, qx.shape, 1)
        dqx = jnp.zeros(qx.shape, F32)
        dkm = jnp.zeros((N_MEM, XW), F32)
        dvm = jnp.zeros((N_MEM, XW), F32)
        for h in range(XW // HEAD):
            hm = (lane >= h * HEAD) & (lane < (h + 1) * HEAD)
            qm = jnp.where(hm, qx, jnp.zeros_like(qx))
            e, l = _xattn_scores(qm, km)
            p = e / l
            dom = jnp.where(hm, dyx, 0.0)
            do16 = dom.astype(BF16)
            dsum = jnp.sum(dom * y_x, axis=1, keepdims=True)
            ds = (p * (_dot_nt(do16, vm) - dsum)).astype(BF16)
            dqx = jnp.where(hm, _dot(ds, km), dqx)
            dkm = dkm + _dot_tn(ds, qm)
            dvm = dvm + _dot_tn(p.astype(BF16), do16)
        dqx_ref[...] = (dqx * SCALE).astype(BF16)
        dkv_ref[:, 0:XW] += dkm
        dkv_ref[:, XW:] += dvm

    def tile(w):
        return pl.BlockSpec((TQ, w), lambda i: (NT - 1 - i, 0))

    halo = pl.BlockSpec((SUBLANES, 3 * CW), lambda i: (jnp.maximum((NT - 1 - i) * (TQ // SUBLANES) - 1, 0), 0))
    return pl.pallas_call(
        body, grid=(NT,), name="mix_out_bwd",
        in_specs=[tile(D), tile(D), tile(D), tile(AW), _const((AW, AW)), tile(AW), tile(3 * CW), halo, tile(XW),
                  _const((N_MEM, 2 * XW)), _const((SUBLANES, CW)), _const((1, D)), _const((1, AW)), _const((1, CW)),
                  _const((1, XW)), _const((D, D))],
        out_specs=[tile(D), tile(AW), tile(AW), tile(3 * CW), tile(XW), _acc((SUBLANES, D)), _acc((SUBLANES, CW)),
                   _acc((N_MEM, 2 * XW))],
        out_shape=[jax.ShapeDtypeStruct((S, D), BF16), jax.ShapeDtypeStruct((S, AW), F32),
                   jax.ShapeDtypeStruct((S, AW), F32),
                   jax.ShapeDtypeStruct((S, 3 * CW), BF16), jax.ShapeDtypeStruct((S, XW), BF16),
                   jax.ShapeDtypeStruct((SUBLANES, D), F32), jax.ShapeDtypeStruct((SUBLANES, CW), F32),
                   jax.ShapeDtypeStruct((N_MEM, 2 * XW), F32)],
        scratch_shapes=[pltpu.VMEM((SUBLANES, CW), F32)],
        compiler_params=_cparams(56))(dx1, y2, ypre, ltot, head_ones, q, bcu, bcu, qx16, kv16, cw8, g_post, g_attn,
                                      g_conv, g_x, wout16)


def _attn_bwd(qdo, kvp, ld, chip_sums=()):
    n_in = 3
    views = [[a] + [a.reshape(S // n, n, AW) for _, n, _, _ in ATTN_PLANS[1:]] for a in (qdo, kvp, ld)]
    flat = [views[a][p] for p in range(3) for a in range(n_in)]
    ns = len(chip_sums)
    n_grid = AW // LANES

    def body(*refs):
        hbm = [refs[n_in * p:n_in * p + n_in] for p in range(3)]
        refs = refs[3 * n_in:]
        sum_refs, refs = refs[:ns], refs[ns:]
        outs = [refs[3 * p:3 * p + 3] for p in range(3)]
        landed_refs, sc = refs[9:9 + ns], refs[9 + ns:]
        bufs = [sc[3 * p:3 * p + 3] for p in range(3)]
        res = [sc[9 + 3 * p:12 + 3 * p] for p in range(3)]
        tab128, tab4, sem_in, sem_out = sc[18:22]
        step = pl.program_id(0)
        if ns:
            start_chips, finish_chips = _chips_steps(sum_refs, landed_refs, *sc[22:])
            _, _, core, chips = _place()
            signal_chips, chips_are_in = _own_barrier([(px, py, core) for px, py in chips])
            pl.when(step == 0)(signal_chips)

            def chips_go():
                chips_are_in()
                start_chips()
        now =[_class_gather(hbm[p], bufs[p], sem_in.at[p], _lanes_of(step)) for p in range(3)]
        nxt = [_class_gather(hbm[p], bufs[p], sem_in.at[p], _lanes_of(step + 1)) for p in range(3)]

        @pl.when(step == 0)
        def _():
            for p in range(3):
                _start(now[p])
                for b in bufs[p]:
                    b[0:PAD, :] = jnp.zeros((PAD, LANES), F32)
            _fill_bias(tab128, 128, False)
            _fill_bias(tab4, 64, True)

        def prefetch(p):
            pl.when(step + 1 < n_grid)(lambda: _start(nxt[p]))

        lane = lax.broadcasted_iota(jnp.int32, (1, LANES), 1)

        def run(plan, plan_bufs, tab, dst):
            _, n_cls, qblk, nbc = plan
            partner = n_cls == 8
            bqdo, bkv, bld = plan_bufs
            rq, rk, rv = dst

            def block(g, carry):
                own, wins, mask = _block_rows(g, qblk, nbc, partner)
                qb, dob = _unpack_pair(bqdo[own, :])
                q2, do2 = _stack_heads(qb, lane), _stack_heads(dob, lane)
                kw, vw = _unpack_pair(_window(bkv, wins))
                ldv = bld[own, :]
                half = HEAD // 2
                lt2 = jnp.concatenate([ldv[:, 0:1], ldv[:, HEAD:HEAD + 1]], axis=0)
                dsum2 = jnp.concatenate([ldv[:, half:half + 1], ldv[:, HEAD + half:HEAD + half + 1]], axis=0)
                p = jnp.exp(_dot_nt(q2, kw) + tab[mask] - lt2)
                ds = (p * (_dot_nt(do2, vw) - dsum2)).astype(BF16)
                rq[own, :] = _unstack_heads(_dot(ds, kw), lane)
                dkw = _dot_tn(ds, q2)
                dvw = _dot_tn(p.astype(BF16), do2)
                n_w = WIN // len(wins)
                for i, w in enumerate(wins):
                    rk[w, :] += dkw[i * n_w:(i + 1) * n_w, :]
                    rv[w, :] += dvw[i * n_w:(i + 1) * n_w, :]
                return carry
            lax.fori_loop(0, n_cls * nbc, block, 0, unroll=ATTN_UNROLL)

        tabs = (tab128, tab4, tab128)
        def drained(p):
            return lambda: _wait(_whole_waits(res[p], sem_out.at[p]))

        for p in range(3):
            pl.when(step > 0)(drained(p))
            for b in res[p][1:]:
                b[...] = jnp.zeros_like(b)
            _wait(_whole_waits(bufs[p], sem_in.at[p]))
            run(ATTN_PLANS[p], bufs[p], tabs[p], res[p])
            prefetch(p)
            _start(_class_scatter(res[p], outs[p], sem_out.at[p], _lanes_of(step)))
            if ns and p == 0:
                pl.when(step == 0)(chips_go)
        for p in range(3):
            pl.when(step == n_grid - 1)(drained(p))
        if ns:
            pl.when(step == n_grid - 1)(finish_chips)

    padded = pltpu.VMEM((PAD + S, LANES), F32)
    shapes = [jax.ShapeDtypeStruct(views[0][p].shape, F32) for p in range(3) for _ in range(3)]
    out = pl.pallas_call(
        body, grid=(n_grid,), name="attn_bwd",
        in_specs=[ANY] * (3 * n_in + ns), out_specs=[ANY] * (9 + ns),
        out_shape=shapes + _chips_shapes(chip_sums),
        scratch_shapes=[padded] * 18
        + [pltpu.VMEM((4, 256, WIN), F32), pltpu.VMEM((4, 128, WIN), F32),
           pltpu.SemaphoreType.DMA((3, n_in)), pltpu.SemaphoreType.DMA((3, 3))]
        + (_chips_scratch(ns) if ns else []),
        compiler_params=_cparams(56, **({"collective_id": ID_ATTN_BWD} if ns else {})))(*flat, *chip_sums)
    return [o.reshape(S, AW) for o in out[:9]] + list(out[9:])


def _in_proj_bwd(dqkv, dbcu, dqx, cos, sins, w16, x, g, dx1):
    tq = TQ // 2

    def body(*refs):
        parts = refs[0:9]
        dbcu_ref, dqx_ref, c_ref, s_ref, w_ref, x_ref, g_ref, dx1_ref, dp_ref, gx_ref, dg_ref = refs[9:]

        @pl.when(pl.program_id(0) == 0)
        def _():
            dg_ref[...] = jnp.zeros_like(dg_ref)

        dq, dk, dv = (parts[i][...] + parts[3 + i][...] + parts[6 + i][...] for i in range(3))
        cos, sn = _all_heads(c_ref[...]), _all_heads(s_ref[...])
        dqr = dq * SCALE
        dkr = dk
        dp = jnp.concatenate([(dqr * cos + _rot_half(dqr * sn)).astype(BF16),
                              (dkr * cos + _rot_half(dkr * sn)).astype(BF16), dv.astype(BF16),
                              dbcu_ref[...], dqx_ref[...]], axis=1)
        dp_ref[...] = dp
        dh = _dot_nt(dp, w_ref[...])
        g = g_ref[...]
        _, n, r = _rms(x_ref[...], g)
        dx, dg = _rms_bwd(dh, n, r, g)
        gx_ref[...] = dx1_ref[...] + dx
        dg_ref[0:1, :] += dg

    def tile(w):
        return pl.BlockSpec((tq, w), lambda i: (i, 0))

    return pl.pallas_call(
        body, grid=(S // tq,), name="in_proj_bwd",
        in_specs=[tile(AW)] * 9 + [tile(3 * CW), tile(XW), tile(LANES), tile(LANES), _const((D, PW)),
                                   tile(D), _const((1, D)), tile(D)],
        out_specs=[tile(PW), tile(D), _acc((SUBLANES, D))],
        out_shape=[jax.ShapeDtypeStruct((S, PW), BF16), jax.ShapeDtypeStruct((S, D), F32),
                   jax.ShapeDtypeStruct((SUBLANES, D), F32)],
        compiler_params=_cparams(56))(*dqkv, dbcu, dqx, cos, sins, w16, x, g, dx1)


def _mem_bwd(mem, g_mem, wkv16, dkv):
    def body(m_ref, g_ref, w_ref, dkv_ref, dkv16_ref, dg_ref):
        dkv16 = dkv_ref[...].astype(BF16)
        dkv16_ref[...] = dkv16
        _, n, _ = _rms(m_ref[...], g_ref[...])
        dg = jnp.sum(_dot_nt(dkv16, w_ref[...]) * n, axis=0, keepdims=True)
        dg_ref[...] = jnp.broadcast_to(dg, dg_ref.shape)

    return pl.pallas_call(
        body, name="mem_bwd",
        out_shape=[jax.ShapeDtypeStruct((N_MEM, 2 * XW), BF16), jax.ShapeDtypeStruct((SUBLANES, D), F32)],
        compiler_params=pltpu.CompilerParams(vmem_limit_bytes=32 << 20))(mem, g_mem, wkv16, dkv)


N_CHIPS = N_DEV // 2


def _pair_scratch(block):
    return [pltpu.VMEM((N_CHIPS,) + block, BF16), pltpu.VMEM((N_CHIPS,) + block, BF16),
            pltpu.SemaphoreType.DMA((N_CHIPS,)), pltpu.SemaphoreType.DMA((N_CHIPS,))]


def _swap_with_sibling(p, stage, land, send, recv):
    x, y, c = lax.axis_index("x"), lax.axis_index("y"), lax.axis_index("c")
    return pltpu.make_async_remote_copy(src_ref=stage.at[p], dst_ref=land.at[p], send_sem=send.at[p],
                                        recv_sem=recv.at[p], device_id=(x, y, 1 - c), device_id_type=MESH)


def _own_barrier(peers):
    sem = pltpu.get_barrier_semaphore()

    def signal():
        for peer in peers:
            pl.semaphore_signal(sem, inc=1, device_id=peer, device_id_type=MESH)

    return signal, lambda: pl.semaphore_wait(sem, len(peers))


def _sibling_barrier():
    x, y, c = lax.axis_index("x"), lax.axis_index("y"), lax.axis_index("c")
    return _own_barrier([(x, y, 1 - c)])


ID_WGRAD_UP, ID_WGRAD_DOWN, ID_WGRAD_ROWS, ID_ROPE_TABLE, ID_IN_PROJ, ID_ATTN_BWD, ID_WGRAD_IN = range(7)


def _wgrad_cols(place, at16, b16, blk, name, barrier_id, square_b=False, transpose_out=False, to_chips=False,
                small=()):
    m, kk = at16.shape
    assert to_chips == bool(small)
    aligned = blk % LANES == 0
    wide = blk if aligned else -(-(blk + LANES // 2) // LANES) * LANES
    assert aligned or (transpose_out and blk % SUBLANES == 0)
    block = (blk, m) if transpose_out else (m, blk)

    def chip_of(step, my_chip):
        return jnp.bitwise_xor(my_chip, N_CHIPS - 1 - step) if to_chips else step

    def body(pl_ref, a_ref, *refs):
        b_refs, refs = refs[:2 if aligned else 1], refs[2 if aligned else 1:]
        accs, refs = refs[:len(small)], refs[len(small):]
        (cs_ref, own_ref), refs = refs[:2], refs[2:]
        if to_chips:
            landed, refs = refs[0], refs[1:]
        if small:
            tot_ref, refs = refs[0], refs[1:]
        (stage, land, send, recv), refs = refs[:4], refs[4:]
        if not aligned:
            (win, wsem), refs = refs[:2], refs[2:]
        if small:
            start_small, finish_small = _small_reduce_steps(accs, tot_ref, *refs[-4:])
            refs = refs[:-4]
        step = pl.program_id(0)
        x, y, c = lax.axis_index("x"), lax.axis_index("y"), lax.axis_index("c")
        others = [(x ^ (k >> 2), y ^ ((k >> 1) & 1), c ^ (k & 1)) for k in range(1, N_DEV)]
        signal_peers, peers_are_in = _own_barrier(others if small else [(x, y, 1 - c)])
        pl.when(step == 0)(signal_peers)
        my_chip = 2 * x + y
        p = chip_of(step, my_chip)

        def fetch(at_step, mine):
            j = 2 * chip_of(at_step, my_chip) + (c if mine else 1 - c)
            first = pl.multiple_of(((j * blk) >> 7) << 7, LANES)
            slot = 2 * (at_step & 1) + mine
            return pltpu.make_async_copy(b_refs[0].at[:, pl.ds(first, wide)], win.at[slot], wsem.at[slot])

        if not aligned:
            @pl.when(step == 0)
            def _():
                fetch(0, 0).start()
                fetch(0, 1).start()

            @pl.when(step + 1 < N_CHIPS)
            def _():
                fetch(step + 1, 0).start()
                fetch(step + 1, 1).start()

        def partial(mine):
            if aligned:
                b = b_refs[mine][...]
                if square_b:
                    b = b * b
                acc = _dot(a_ref[...], b)
            else:
                fetch(step, mine).wait()
                acc = _dot(a_ref[...], win[2 * (step & 1) + mine]).T
                odd = c if mine else 1 - c
                return jnp.where(odd == 0, acc[0:blk], acc[wide - blk:wide])
            return acc.T if transpose_out else acc

        stage[p] = partial(0).astype(BF16)
        pl.when(step == 0)(peers_are_in)
        if small:
            pl.when(step == 0)(start_small)
        swap = _swap_with_sibling(p, stage, land, send, recv)
        swap.start()
        mine = partial(1)
        swap.wait()
        total = mine + land[p].astype(F32)
        cs_ref[0] = total.astype(BF16)

        @pl.when(p == my_chip)
        def _():
            own_ref[...] = total

        if to_chips:
            stage2, send2, recv2 = refs
            flipped = jnp.bitwise_xor(p, my_chip)
            k = jnp.where(flipped == 2, 0, jnp.where(flipped == 1, 1, 2))

            def to_owner(src, k_, px, py):
                return pltpu.make_async_remote_copy(src_ref=src, dst_ref=landed.at[k_], send_sem=send2.at[k_],
                                                    recv_sem=recv2.at[k_], device_id=(px, py, c), device_id_type=MESH)

            @pl.when(p != my_chip)
            def _():
                stage2[p] = total.astype(BF16)
                to_owner(stage2.at[p], k, p >> 1, p & 1).start()

            @pl.when(step == N_CHIPS - 1)
            def _():
                for k_ in range(N_CHIPS - 1):
                    to_owner(stage2.at[0], k_, x, y).wait()

        if small:
            pl.when(step == N_CHIPS - 1)(finish_small)

    def b_spec(mine):
        return pl.BlockSpec((kk, blk), lambda i, s: (0, 2 * chip_of(i, s[1]) + (s[0] if mine else 1 - s[0])))

    b_specs, b_args = ([b_spec(0), b_spec(1)], (b16, b16)) if aligned else ([ANY], (b16,))
    scratch = _pair_scratch(block)
    if not aligned:
        scratch += [pltpu.VMEM((4, kk, wide), BF16), pltpu.SemaphoreType.DMA((4,))]
    out_specs = [pl.BlockSpec((1,) + block, lambda i, s: (chip_of(i, s[1]), 0, 0)), pl.BlockSpec(block, lambda i, s: (0, 0))]
    out_shape = [jax.ShapeDtypeStruct((N_CHIPS,) + block, BF16), jax.ShapeDtypeStruct(block, F32)]
    if to_chips:
        out_specs.append(ANY)
        out_shape.append(jax.ShapeDtypeStruct((N_CHIPS - 1,) + block, BF16))
        scratch += [pltpu.VMEM((N_CHIPS,) + block, BF16), pltpu.SemaphoreType.DMA((N_CHIPS - 1,)),
                    pltpu.SemaphoreType.DMA((N_CHIPS - 1,))]
    small_specs = [pl.BlockSpec(a.shape, lambda i, s: (0, 0)) for a in small]
    if small:
        out_specs.append(pl.BlockSpec((PACK_ROWS, D), lambda i, s: (0, 0)))
        out_shape.append(jax.ShapeDtypeStruct((PACK_ROWS, D), F32))
        scratch += _small_reduce_scratch()
    return pl.pallas_call(
        body, name=name,
        grid_spec=pltpu.PrefetchScalarGridSpec(
            num_scalar_prefetch=1, grid=(N_CHIPS,),
            in_specs=[pl.BlockSpec((m, kk), lambda i, s: (0, 0), pipeline_mode=pl.Buffered(1))] + b_specs + small_specs,
            out_specs=out_specs, scratch_shapes=scratch),
        out_shape=out_shape,
        compiler_params=_cparams(56, collective_id=barrier_id),
    )(place, at16, *b_args, *small)


ROWS_STEPS = 4


def _wgrad_rows(place, products, name):
    n_prod = len(products)
    dims = [(at16.shape[0], at16.shape[1], b16.shape[1]) for at16, b16 in products]
    cut = [kk % (ROWS_STEPS * LANES) == 0 for _, kk, _ in dims]
    blocks = [(m // N_DEV, n) for m, _, n in dims]

    def body(pl_ref, *refs):
        ins, outs, scratch = refs[:2 * n_prod], refs[2 * n_prod:4 * n_prod], refs[4 * n_prod:]
        c, step = pl_ref[0], pl.program_id(0)

        def multiply(i):
            a_ref, b_ref, acc = ins[2 * i], ins[2 * i + 1], scratch[5 * i]

            @pl.when(step == 0)
            def _():
                acc[...] = _dot(a_ref[...], b_ref[...])

            if cut[i]:
                @pl.when(step > 0)
                def _():
                    acc[...] += _dot(a_ref[...], b_ref[...])

        def rows(i, owner):
            return pl.ds(pl.multiple_of(owner * blocks[i][0], blocks[i][0]), blocks[i][0])

        def send_sibling_side(i):
            acc, stage, land, send, recv = scratch[5 * i:5 * i + 5]
            swaps = []
            for p in range(N_CHIPS):
                stage[p] = acc[rows(i, 2 * p + 1 - c), :].astype(BF16)
                swaps.append(_swap_with_sibling(p, stage, land, send, recv))
                swaps[-1].start()
            return swaps

        def add_my_side(i, swaps):
            acc, land = scratch[5 * i], scratch[5 * i + 2]
            cs_ref, own_ref = outs[2 * i:2 * i + 2]
            for p in range(N_CHIPS):
                swaps[p].wait()
                total = acc[rows(i, 2 * p + c), :] + land[p].astype(F32)
                cs_ref[p] = total.astype(BF16)

                @pl.when(p == pl_ref[1])
                def _():
                    own_ref[...] = total

        signal_sibling, sibling_is_in = _sibling_barrier()
        pl.when(step == 0)(signal_sibling)
        for i in range(n_prod):
            multiply(i)

        @pl.when(step == ROWS_STEPS - 1)
        def _():
            sibling_is_in()
            swaps = [send_sibling_side(i) for i in range(n_prod)]
            for i in range(n_prod):
                add_my_side(i, swaps[i])

    in_specs, out_specs, out_shape, scratch = [pl.BlockSpec(memory_space=pltpu.SMEM)], [], [], []
    for (m, kk, n), cut_i, block in zip(dims, cut, blocks):
        chunk = kk // ROWS_STEPS
        in_specs += ([pl.BlockSpec((m, chunk), lambda i: (0, i)), pl.BlockSpec((chunk, n), lambda i: (i, 0))]
                     if cut_i else [_const((m, kk)), _const((kk, n))])
        out_specs += [_acc((N_CHIPS,) + block), _acc(block)]
        out_shape += [jax.ShapeDtypeStruct((N_CHIPS,) + block, BF16), jax.ShapeDtypeStruct(block, F32)]
        scratch += [pltpu.VMEM((m, n), F32)] + _pair_scratch(block)
    out = pl.pallas_call(
        body, grid=(ROWS_STEPS,), name=name, in_specs=in_specs, out_specs=out_specs, out_shape=out_shape,
        scratch_shapes=scratch, compiler_params=_cparams(56, collective_id=ID_WGRAD_ROWS),
    )(place, *[a for pair in products for a in pair])
    return [tuple(out[2 * i:2 * i + 2]) for i in range(n_prod)]


def _adamw_math(w, g, m, v):
    m = ADAM_B1 * m + (1.0 - ADAM_B1) * g
    v = ADAM_B2 * v + (1.0 - ADAM_B2) * jnp.square(g)
    m_hat = m / (1.0 - ADAM_B1 ** ADAM_STEP)
    v_hat = v / (1.0 - ADAM_B2 ** ADAM_STEP)
    delta = -ADAM_LR * (m_hat / (jnp.sqrt(v_hat) + ADAM_EPS) + ADAM_WD * w)
    return delta, m, v


def _adamw_shards(updates, name, chip_sums=()):
    names, nu, ns = list(updates), len(updates), len(chip_sums)

    def body(*refs):
        ins, sum_refs = refs[:5 * nu], refs[5 * nu:5 * nu + ns]
        outs = refs[5 * nu + ns:9 * nu + ns]
        landed_refs, scratch = refs[9 * nu + ns:9 * nu + 2 * ns], refs[9 * nu + 2 * ns:]
        if ns:
            start_chips, finish_chips = _chips_steps(sum_refs, landed_refs, *scratch)
            start_chips()
        for i in range(nu):
            o_ref, r_ref, w_ref, m_ref, v_ref = ins[5 * i:5 * i + 5]
            g_out, d_out, m_out, v_out = outs[4 * i:4 * i + 4]
            g = o_ref[...] + r_ref[0].astype(F32) + r_ref[1].astype(F32) + r_ref[2].astype(F32)
            g_out[...] = g
            d_out[...], m_out[...], v_out[...] = _adamw_math(w_ref[...], g, m_ref[...], v_ref[...])
        if ns:
            finish_chips()

    vmem = pl.BlockSpec(memory_space=pltpu.VMEM)
    out = pl.pallas_call(
        body, name=name,
        in_specs=[vmem] * (5 * nu) + [ANY] * ns, out_specs=[vmem] * (4 * nu) + [ANY] * ns,
        out_shape=[jax.ShapeDtypeStruct(updates[n][2].shape, F32) for n in names for _ in range(4)]
        + _chips_shapes(chip_sums),
        scratch_shapes=_chips_scratch(ns) if ns else [],
        compiler_params=pltpu.CompilerParams(vmem_limit_bytes=56 << 20),
    )(*[a for n in names for a in updates[n]], *chip_sums)
    return {n: out[4 * i:4 * i + 4] for i, n in enumerate(names)}, list(out[4 * nu:])


def _place():
    x, y, c = lax.axis_index("x"), lax.axis_index("y"), lax.axis_index("c")
    chips = [(1 - x, y), (x, 1 - y), (1 - x, 1 - y)]
    return x, y, c, chips


def _gather_steps(ins, outs, send, recv, lsem, own_barrier=True):
    nt = len(ins)
    x, y, c, (xn, yn, diag) = _place()
    me, sib = (x, y, c), (x, y, 1 - c)

    def slot(t, px, py, pc):
        return outs[t].at[4 * px + 2 * py + pc]

    def copy(t, k, block, to, src=None):
        return pltpu.make_async_remote_copy(
            src_ref=slot(t, *block) if src is None else src, dst_ref=slot(t, *block),
            send_sem=send.at[t, k], recv_sem=recv.at[t, k], device_id=to, device_id_type=MESH)

    mine = [pltpu.make_async_copy(ins[t], slot(t, *me), lsem.at[t]) for t in range(nt)]
    first = [copy(t, k, me, to, src=ins[t]) for t in range(nt) for k, to in ((0, sib), (1, (*xn, c)), (2, (*yn, c)))]

    if own_barrier:
        signal_peers, peers_are_in = _own_barrier([sib, (*xn, c), (*yn, c)])

    def enter():
        if own_barrier:
            signal_peers()
        for cp in mine:
            cp.start()

    def start():
        if own_barrier:
            peers_are_in()
        for cp in first:
            cp.start()

    def landed(k, chip, also_to=None):
        for t in range(nt):
            copy(t, k, (*chip, c), me).wait_recv()
            if also_to is not None:
                copy(t, 3, (*chip, c), (*also_to, c)).start()
            copy(t, 3 + k, (*chip, c), sib).start()

    def relay():
        @pl.when(c == 0)
        def _():
            landed(1, xn, also_to=yn)
            landed(2, yn)

        @pl.when(c == 1)
        def _():
            landed(2, yn, also_to=xn)
            landed(1, xn)

    def finish():
        landed(3, diag)
        for t in range(nt):
            copy(t, 0, sib, me).wait_recv()
            for k, chip in ((4, xn), (5, yn), (6, diag)):
                copy(t, k, (*chip, 1 - c), me).wait_recv()
            for k in range(7):
                copy(t, k, me, sib).wait_send()
        for cp in mine:
            cp.wait()

    return enter, start, relay, finish


def _gather_scratch(nt):
    return [pltpu.SemaphoreType.DMA((nt, 7)), pltpu.SemaphoreType.DMA((nt, 7)), pltpu.SemaphoreType.DMA((nt,))]


def _gathered_shapes(shards):
    return [jax.ShapeDtypeStruct((N_DEV,) + s.shape, s.dtype) for s in shards]


def _call_with_gather(body, n_grid, shards, *, name, in_specs, out_specs, out_shape, scratch_shapes, vmem_mb, args,
                      collective_id=None):
    assert (collective_id is None) == (not shards)
    ng, n_in, n_out = len(shards), len(in_specs), len(out_specs)

    def wrapped(*refs):
        ins, shard_refs = refs[:n_in], refs[n_in:n_in + ng]
        outs = refs[n_in + ng:n_in + ng + n_out]
        whole_refs = refs[n_in + ng + n_out:n_in + 2 * ng + n_out]
        scratch = refs[n_in + 2 * ng + n_out:]
        if ng:
            enter, start, relay, finish = _gather_steps(shard_refs, whole_refs, *scratch[len(scratch_shapes):])
            pl.when(pl.program_id(0) == 0)(enter)
            pl.when(pl.program_id(0) == 0)(start)
            pl.when(pl.program_id(0) == n_grid // 2)(relay)
        body(*ins, *outs, *scratch[:len(scratch_shapes)])
        if ng:
            pl.when(pl.program_id(0) == n_grid - 1)(finish)

    return pl.pallas_call(
        wrapped, grid=(n_grid,), name=name,
        in_specs=list(in_specs) + [ANY] * ng, out_specs=list(out_specs) + [ANY] * ng,
        out_shape=list(out_shape) + _gathered_shapes(shards),
        scratch_shapes=list(scratch_shapes) + (_gather_scratch(ng) if ng else []),
        compiler_params=_cparams(vmem_mb, **({"collective_id": collective_id} if shards else {})))(*args, *shards)


def _chips_steps(ins, outs, send, recv):
    _, _, c, chips = _place()
    copies = [pltpu.make_async_remote_copy(
        src_ref=ins[t].at[2 * px + py], dst_ref=outs[t].at[j], send_sem=send.at[t, j], recv_sem=recv.at[t, j],
        device_id=(px, py, c), device_id_type=MESH) for t in range(len(ins)) for j, (px, py) in enumerate(chips)]

    def start():
        for cp in copies:
            cp.start()

    def finish():
        for cp in copies:
            cp.wait()

    return start, finish


def _chips_scratch(nt):
    return [pltpu.SemaphoreType.DMA((nt, 3)), pltpu.SemaphoreType.DMA((nt, 3))]


def _chips_shapes(cs16s):
    return [jax.ShapeDtypeStruct((3,) + g.shape[1:], g.dtype) for g in cs16s]


SMALL = (("g_pre_mix", 0, 0, D), ("g_mem", 1, 0, D), ("g_post_mix", 2, 0, D), ("g_attn_out", 3, 0, AW),
         ("g_conv_out", 3, AW, CW), ("g_xattn_out", 3, AW + CW, XW), ("g_post_mlp", 4, 0, D), ("g_pre_mlp", 5, 0, D))
CONV_ROW = 8
PACK_ROWS = 16


LOSS_ROW = 15


def _small_reduce_steps(accs, tot_ref, pack, land, send, recv):
    acc_in, acc_mem, acc_mix, acc_mlp, acc_cw, acc_loss = accs
    x, y, c, _ = _place()
    me = 4 * x + 2 * y + c
    copies = []
    for k in range(1, N_DEV):
        kx, ky, kc = (k >> 2) & 1, (k >> 1) & 1, k & 1
        peer = (1 - x if kx else x, 1 - y if ky else y, 1 - c if kc else c)
        copies.append(pltpu.make_async_remote_copy(
            src_ref=pack, dst_ref=land.at[me], send_sem=send.at[k - 1], recv_sem=recv.at[k - 1],
            device_id=peer, device_id_type=MESH))

    def start():
        pack[...] = jnp.zeros_like(pack)
        pack[0:1, :] = acc_in[0:1, :]
        pack[1:2, :] = acc_mem[0:1, :]
        pack[2:4, :] = acc_mix[0:2, :]
        pack[4:6, :] = acc_mlp[0:2, :]
        pack[CONV_ROW:CONV_ROW + 3, 0:CW] = acc_cw[0:3, :]
        pack[LOSS_ROW:LOSS_ROW + 1, 0:LANES] = acc_loss[0:1, :]
        land[me] = pack[...]
        for cp in copies:
            cp.start()

    def finish():
        for cp in copies:
            cp.wait()
        tot = land[0]
        for s in range(1, N_DEV):
            tot = tot + land[s]
        tot_ref[...] = tot

    return start, finish


def _small_reduce_scratch():
    return [pltpu.VMEM((PACK_ROWS, D), F32), pltpu.VMEM((N_DEV, PACK_ROWS, D), F32),
            pltpu.SemaphoreType.DMA((N_DEV - 1,)), pltpu.SemaphoreType.DMA((N_DEV - 1,))]


def _small_update(tot, me, params):
    flat = [a for n, _, _, _ in SMALL for a in params[n]] + list(params["conv_w"])
    n_par = len(SMALL) + 1
    tap_cols = CW // N_DEV

    def body(*refs):
        me_ref, tot_ref = refs[0:2]
        ins = refs[2:2 + 3 * n_par]
        loss_out = refs[2 + 3 * n_par]
        outs = refs[3 + 3 * n_par:]
        tot = tot_ref[...]
        loss_out[...] = jnp.broadcast_to(tot[LOSS_ROW:LOSS_ROW + 1, 0:LANES], loss_out.shape)

        def update(i, g):
            w_ref, m_ref, v_ref = ins[3 * i:3 * i + 3]
            for o_ref, res in zip(outs[4 * i:4 * i + 4], (g,) + _adamw_math(w_ref[...], g, m_ref[...], v_ref[...])):
                if len(o_ref.shape) == 3:
                    for t in range(o_ref.shape[0]):
                        o_ref[t] = res[t:t + 1, :]
                else:
                    o_ref[...] = res

        for i, (_, row, lane0, width) in enumerate(SMALL):
            update(i, tot[row:row + 1, lane0:lane0 + width])
        me = me_ref[0]
        taps = pltpu.roll(tot[CONV_ROW:CONV_ROW + SUBLANES, 0:CW], jnp.where(me == 0, 0, CW - me * tap_cols), 1)
        update(n_par - 1, taps[0:3, 0:tap_cols])

    shapes = [jax.ShapeDtypeStruct(params[n][0].shape, F32) for n, _, _, _ in SMALL] + [
        jax.ShapeDtypeStruct((3, 1, tap_cols), F32)]
    vmem = pl.BlockSpec(memory_space=pltpu.VMEM)
    loss, *out = pl.pallas_call(
        body, name="small_update",
        in_specs=[pl.BlockSpec(memory_space=pltpu.SMEM)] + [vmem] * (1 + 3 * n_par),
        out_shape=[jax.ShapeDtypeStruct((SUBLANES, LANES), F32)] + [s for s in shapes for _ in range(4)],
    )(me, tot, *flat)
    names = [n for n, _, _, _ in SMALL] + ["conv_w"]
    return loss[0, 0], {n: out[4 * i:4 * i + 4] for i, n in enumerate(names)}


def _local_step(x, mem, pos, gains, shards, tgt, place):
    half = HEAD // 2
    inv_freq = jnp.float32(ROPE_THETA) ** (-(jnp.arange(half, dtype=F32) * 2.0 / HEAD))
    invf = jnp.tile(inv_freq, LANES // half)[None, :]
    sgn = jnp.tile(jnp.concatenate([-jnp.ones((half,), F32), jnp.ones((half,), F32)]), LANES // HEAD)[None, :]
    cos, sins, win8 = _rope_table(pos.astype(F32).reshape(S, 1), invf, sgn, [shards["w_in"]])
    wdn_left, wdn_right = shards["w_down"][:, 0:D // 2], shards["w_down"][:, D // 2:]
    q, kvp, bcu, qx16, h16, win16, wout8, wkv8, conv8, wdn8_right = _in_proj(
        x, gains["g_pre_mix"], win8, cos, sins, [shards["w_out"], shards["w_mem_kv"], shards["conv_w"], wdn_right])
    wout16, wkv16 = wout8.reshape(D, D), wkv8.reshape(D, 2 * XW)
    cw_full = conv8[:, 0:3, 0:CW // N_DEV].transpose(1, 0, 2).reshape(3, CW)
    cw8 = jnp.zeros((SUBLANES, CW), F32).at[0:3].set(cw_full)
    y_attn, ltot, wup8, wdn8_left = _attn_fwd(q, kvp, [shards["w_up"], wdn_left])
    wdn_halves = (wdn8_left.reshape(FF, D // 2), wdn8_right.reshape(FF, D // 2))
    memn16, kv16 = _mem_fwd(mem, gains["g_mem"], wkv16)
    ypre, y16, y2, x1 = _mix_out(y_attn, bcu, qx16, kv16, cw8, gains["g_attn_out"], gains["g_conv_out"],
                                 gains["g_xattn_out"], gains["g_post_mix"], wout16, x, [])
    a16, du16, h2_16, df2_16, dx1, loss8, dg_mlp = _mlp(
        x1, tgt, gains["g_pre_mlp"], gains["g_post_mlp"], wup8, wdn_halves)

    sums = {"w_up": _wgrad_cols(place, h2_16, du16, FF_BLK, "wgrad_up", ID_WGRAD_UP),
            "w_down": _wgrad_cols(place, df2_16, a16, FF_BLK, "wgrad_down", ID_WGRAD_DOWN, square_b=True,
                                  transpose_out=True)}

    head_id = jnp.arange(AW, dtype=jnp.int32) // HEAD
    head_ones = (head_id[:, None] == head_id[None, :]).astype(BF16)
    dy2_16, qdo, ld, dbcu, dqx, dgs, dcw, dkv = _mix_out_bwd(
        dx1, y2, ypre, ltot, head_ones, q, bcu, qx16, kv16, cw8, gains["g_post_mix"], gains["g_attn_out"],
        gains["g_conv_out"], gains["g_xattn_out"], wout16)
    dkv16, dg_mem = _mem_bwd(mem, gains["g_mem"], wkv16, dkv)
    sums["w_mem_kv"], sums["w_out"] = _wgrad_rows(place, [(memn16, dkv16), (y16, dy2_16)], "wgrad_mem_kv_out")
    out = _attn_bwd(qdo, kvp, ld, [s[0] for s in sums.values()])
    dqkv, landed = out[:9], out[9:]
    reduced = {n: (s[1], landed[t]) for t, (n, s) in enumerate(sums.items())}
    dproj16, grad_x, dg_in = _in_proj_bwd(dqkv, dbcu, dqx, cos, sins, win16, x, gains["g_pre_mix"], dx1)

    _, in_own, in_landed, small_tot = _wgrad_cols(place, h16, dproj16, PW // N_DEV, "wgrad_in", ID_WGRAD_IN,
                                                  transpose_out=True, to_chips=True,
                                                  small=(dg_in, dg_mem, dgs, dg_mlp, dcw, loss8))
    reduced["w_in"] = (in_own, in_landed)
    return grad_x, reduced, small_tot


BIG = ("w_in", "w_mem_kv", "w_out", "w_up", "w_down")
ORDER = ("g_pre_mix", "g_mem", "w_in", "w_mem_kv", "conv_w", "g_attn_out", "g_conv_out", "g_xattn_out", "w_out",
         "g_post_mix", "g_pre_mlp", "w_up", "w_down", "g_post_mlp")


def kernel(x, mem, positions, g_pre_mix, g_mem, w_in, w_mem_kv, conv_w, g_attn_out, g_conv_out, g_xattn_out, w_out, g_post_mix, g_pre_mlp, w_up, w_down, g_post_mlp, loss_target, m_g_pre_mix, m_g_mem, m_w_in, m_w_mem_kv, m_conv_w, m_g_attn_out, m_g_conv_out, m_g_xattn_out, m_w_out, m_g_post_mix, m_g_pre_mlp, m_w_up, m_w_down, m_g_post_mlp, v_g_pre_mix, v_g_mem, v_w_in, v_w_mem_kv, v_conv_w, v_g_attn_out, v_g_conv_out, v_g_xattn_out, v_w_out, v_g_post_mix, v_g_pre_mlp, v_w_up, v_w_down, v_g_post_mlp):
    w = dict(g_pre_mix=g_pre_mix, g_mem=g_mem, w_in=w_in, w_mem_kv=w_mem_kv, conv_w=conv_w, g_attn_out=g_attn_out,
             g_conv_out=g_conv_out, g_xattn_out=g_xattn_out, w_out=w_out, g_post_mix=g_post_mix, g_pre_mlp=g_pre_mlp,
             w_up=w_up, w_down=w_down, g_post_mlp=g_post_mlp)
    mo = dict(g_pre_mix=m_g_pre_mix, g_mem=m_g_mem, w_in=m_w_in, w_mem_kv=m_w_mem_kv, conv_w=m_conv_w,
              g_attn_out=m_g_attn_out, g_conv_out=m_g_conv_out, g_xattn_out=m_g_xattn_out, w_out=m_w_out,
              g_post_mix=m_g_post_mix, g_pre_mlp=m_g_pre_mlp, w_up=m_w_up, w_down=m_w_down, g_post_mlp=m_g_post_mlp)
    vo = dict(g_pre_mix=v_g_pre_mix, g_mem=v_g_mem, w_in=v_w_in, w_mem_kv=v_w_mem_kv, conv_w=v_conv_w,
              g_attn_out=v_g_attn_out, g_conv_out=v_g_conv_out, g_xattn_out=v_g_xattn_out, w_out=v_w_out,
              g_post_mix=v_g_post_mix, g_pre_mlp=v_g_pre_mlp, w_up=v_w_up, w_down=v_w_down, g_post_mlp=v_g_post_mlp)

    xi, yi, ci = lax.axis_index("x"), lax.axis_index("y"), lax.axis_index("c")
    me = 4 * xi + 2 * yi + ci
    place = jnp.stack([ci, 2 * xi + yi]).astype(jnp.int32)

    shards = {n: w[n][0].astype(BF16) for n in BIG}
    shards["conv_w"] = jnp.zeros((SUBLANES, LANES), F32).at[0:3, 0:CW // N_DEV].set(conv_w[0])

    gains = {n: w[n] for n, _, _, _ in SMALL}
    grad_x, reduced, small_tot = _local_step(x[0], mem[0], positions[0], gains, shards, loss_target[0], place)

    def shard(n, a):
        return a[0].T if n == "w_in" else a[0]

    updated = {}
    for group in (("w_up", "w_down"), ("w_in", "w_out", "w_mem_kv")):
        updated.update(_adamw_shards({n: (*reduced[n], shard(n, w[n]), shard(n, mo[n]), shard(n, vo[n]))
                                      for n in group}, "adamw_" + "_".join(group))[0])
    grad, delta, new_m, new_v = {}, {}, {}, {}
    for n, res in updated.items():
        grad[n], delta[n], new_m[n], new_v[n] = [(a.T if n == "w_in" else a)[None] for a in res]

    params = {n: (w[n], mo[n], vo[n]) for n, _, _, _ in SMALL}
    params["conv_w"] = (w["conv_w"][0], mo["conv_w"][0], vo["conv_w"][0])
    loss, small = _small_update(small_tot, me.reshape(1).astype(jnp.int32), params)
    for n, (g, d_, m_, v_) in small.items():
        lead = (lambda a: a.reshape(conv_w.shape)) if n == "conv_w" else (lambda a: a)
        grad[n], delta[n], new_m[n], new_v[n] = lead(g), lead(d_), lead(m_), lead(v_)

    return (loss, grad_x[None], *[grad[n] for n in ORDER], *[delta[n] for n in ORDER],
            *[new_m[n] for n in ORDER], *[new_v[n] for n in ORDER])
```

```python
import jax
import jax.numpy as jnp
from jax import lax
from jax.experimental import pallas as pl
from jax.experimental.pallas import tpu as pltpu

F32, BF16 = jnp.float32, jnp.bfloat16
MESH = pl.DeviceIdType.MESH
ANY = pl.BlockSpec(memory_space=pl.ANY)

N_DEV = 8
D = 1024
S = 4096
N_MEM = 256
HEAD = 64
AW, CW, XW = 512, 256, 256
PW = 3 * AW + 3 * CW + XW
FF = 4096
FF_BLK = FF // N_DEV
EPS = 1e-6
NEG = -1e30
SCALE = HEAD ** -0.5
ROPE_THETA = 10000.0
LANES = 128
SUBLANES = 8

ADAM_LR, ADAM_B1, ADAM_B2, ADAM_EPS, ADAM_WD, ADAM_STEP = 0.001, 0.9, 0.999, 1e-08, 0.01, 10

TQ = 512
TQ_MLP = 512
NT = S // TQ


def _cparams(vmem_mb, n_grid=1, **more):
    return pltpu.CompilerParams(dimension_semantics=("arbitrary",) * n_grid, vmem_limit_bytes=vmem_mb << 20, **more)


def _const(shape):
    nd = len(shape)
    return pl.BlockSpec(shape, lambda *_: (0,) * nd, pipeline_mode=pl.Buffered(1))


def _acc(shape):
    nd = len(shape)
    return pl.BlockSpec(shape, lambda *_: (0,) * nd)


def _tokens_in_lanes(tq):
    return pl.BlockSpec((D, tq), lambda i: (0, i))


def _dot(a, b):
    return jnp.dot(a, b, preferred_element_type=F32)


def _dot_nt(a, b):
    return lax.dot_general(a, b, (((1,), (1,)), ((), ())), preferred_element_type=F32)


def _dot_tn(a, b):
    return lax.dot_general(a, b, (((0,), (0,)), ((), ())), preferred_element_type=F32)


def _rms(x, g):
    r = lax.rsqrt(jnp.mean(x * x, axis=-1, keepdims=True) + EPS)
    n = x * r
    return n * g, n, r


def _rms_bwd(dy, n, r, g):
    dn = dy * g
    dx = r * (dn - n * jnp.mean(dn * n, axis=-1, keepdims=True))
    return dx, jnp.sum(dy * n, axis=0, keepdims=True)


def _rot_half(t):
    lane = lax.broadcasted_iota(jnp.int32, t.shape, 1)
    n = t.shape[1]
    return jnp.where((lane % HEAD) < HEAD // 2, pltpu.roll(t, n - HEAD // 2, 1), pltpu.roll(t, HEAD // 2, 1))


def _rope_table(pos_col, invf, sgn, shards):
    def body(p_ref, f_ref, s_ref, c_out, s_out):
        ang = p_ref[...] * f_ref[...]
        c_out[...] = jnp.cos(ang)
        s_out[...] = jnp.sin(ang) * s_ref[...]

    tile = pl.BlockSpec((TQ, LANES), lambda i: (i, 0))
    return _call_with_gather(
        body, NT, shards, name="rope_table",
        in_specs=[pl.BlockSpec((TQ, 1), lambda i: (i, 0)), _const((1, LANES)), _const((1, LANES))],
        out_specs=[tile, tile], out_shape=[jax.ShapeDtypeStruct((S, LANES), F32)] * 2,
        scratch_shapes=[], vmem_mb=32, args=(pos_col, invf, sgn), collective_id=ID_ROPE_TABLE)


def _all_heads(t):
    return jnp.tile(t, (1, AW // LANES))


def _mem_fwd(mem, g_mem, wkv16):
    def body(m_ref, g_ref, w_ref, n16_ref, kv_ref):
        y, _, _ = _rms(m_ref[...], g_ref[...])
        y16 = y.astype(BF16)
        n16_ref[...] = y16.T
        kv_ref[...] = _dot(y16, w_ref[...]).astype(BF16)

    return pl.pallas_call(
        body, name="mem_fwd",
        out_shape=[jax.ShapeDtypeStruct((D, N_MEM), BF16), jax.ShapeDtypeStruct((N_MEM, 2 * XW), BF16)],
        compiler_params=pltpu.CompilerParams(vmem_limit_bytes=32 << 20))(mem, g_mem, wkv16)


def _in_proj(x, g, w8, cos, sins, shards):
    blk = PW // N_DEV

    def body(x_ref, g_ref, w8_ref, c_ref, s_ref, q_ref, kv_ref, bcu_ref, qx_ref, h_ref, w_out, w_ref):
        @pl.when(pl.program_id(0) == 0)
        def _():
            for j in range(N_DEV):
                w_ref[:, j * blk:(j + 1) * blk] = w8_ref[j]
            w_out[...] = w_ref[...]

        y, _, _ = _rms(x_ref[...], g_ref[...])
        h = y.astype(BF16)
        h_ref[...] = h.T
        proj = _dot(h, w_ref[...])
        cos, sn = _all_heads(c_ref[...]), _all_heads(s_ref[...])
        q, k = proj[:, 0:AW], proj[:, AW:2 * AW]
        q_ref[...] = (q * cos + _rot_half(q) * sn) * SCALE
        kv_ref[...] = _pack_pair(k * cos + _rot_half(k) * sn, proj[:, 2 * AW:3 * AW])
        bcu_ref[...] = proj[:, 3 * AW:3 * AW + 3 * CW]
        qx_ref[...] = (proj[:, 3 * AW + 3 * CW:] * SCALE).astype(BF16)

    def tile(w):
        return pl.BlockSpec((TQ, w), lambda i: (i, 0))

    return _call_with_gather(
        body, NT, shards, name="in_proj",
        in_specs=[tile(D), _const((1, D)), _const((N_DEV, D, blk)), tile(LANES), tile(LANES)],
        out_specs=[tile(AW), tile(AW), tile(3 * CW), tile(XW), _tokens_in_lanes(TQ), _acc((D, PW))],
        out_shape=[jax.ShapeDtypeStruct((S, AW), F32)] * 2 + [
            jax.ShapeDtypeStruct((S, 3 * CW), F32), jax.ShapeDtypeStruct((S, XW), BF16),
            jax.ShapeDtypeStruct((D, S), BF16), jax.ShapeDtypeStruct((D, PW), BF16)],
        scratch_shapes=[pltpu.VMEM((D, PW), BF16)], vmem_mb=56, args=(x, g, w8, cos, sins),
        collective_id=ID_IN_PROJ)


ATTN_PLANS = (("p1", 1, 128, 32), ("p4", 8, 64, 8), ("p16", 16, 128, 2))
PAD = 128
WIN = 256


ATTN_UNROLL = 16


def _fill_bias(tab, qblk, partner):
    qi = lax.broadcasted_iota(jnp.int32, (2 * qblk, WIN), 0) & (qblk - 1)
    kj = lax.broadcasted_iota(jnp.int32, (2 * qblk, WIN), 1)
    piece = kj >> (qblk.bit_length() - 1)
    kk = kj & (qblk - 1)
    prev = (piece & 1) == 0
    of_partner = piece >= 2
    for first in (0, 1):
        for par in (0, 1):
            lo = jnp.where(prev, (qblk if first else qi) + jnp.where(of_partner, par, 0), 0)
            hi = jnp.where(prev, qblk, qi + jnp.where(of_partner, par - 1, 0))
            tab[2 * first + par] = jnp.where((kk >= lo) & (kk <= hi), 0.0, NEG).astype(F32)


def _block_rows(g, qblk, nbc, partner):
    own = pl.ds(pl.multiple_of(PAD + g * qblk, qblk), qblk)
    first = ((g & (nbc - 1)) == 0).astype(jnp.int32)
    if partner:
        gp = jnp.bitwise_xor(g, 4 * nbc)
        wins = (pl.ds(pl.multiple_of(PAD + (g - 1) * qblk, qblk), 2 * qblk),
                pl.ds(pl.multiple_of(PAD + (gp - 1) * qblk, qblk), 2 * qblk))
        return own, wins, 2 * first + ((g >> ((4 * nbc).bit_length() - 1)) & 1)
    return own, (pl.ds(pl.multiple_of(PAD + (g - 1) * qblk, qblk), 2 * qblk),), 2 * first


def _pack_pair(lo, hi):
    lo_bits = lax.bitcast_convert_type(lo.astype(BF16).astype(F32), jnp.uint32) >> 16
    hi_bits = lax.bitcast_convert_type(hi.astype(BF16).astype(F32), jnp.uint32) & jnp.uint32(0xFFFF0000)
    return lax.bitcast_convert_type(hi_bits | lo_bits, F32)


def _unpack_pair(c):
    bits = lax.bitcast_convert_type(c, jnp.uint32)
    lo = lax.bitcast_convert_type(bits << 16, F32).astype(BF16)
    hi = lax.bitcast_convert_type(bits & jnp.uint32(0xFFFF0000), F32).astype(BF16)
    return lo, hi


def _window(ref, wins):
    parts = [ref[w, :] for w in wins]
    return parts[0] if len(parts) == 1 else jnp.concatenate(parts, axis=0)


def _stack_heads(t, lane):
    zero = jnp.zeros_like(t)
    return jnp.concatenate([jnp.where(lane < HEAD, t, zero), jnp.where(lane >= HEAD, t, zero)], axis=0)


def _unstack_heads(t2, lane):
    half = t2.shape[0] // 2
    return jnp.where(lane < HEAD, t2[0:half, :], t2[half:, :])


def _lanes_of(step):
    return pl.ds(pl.multiple_of(step * LANES, LANES), LANES)


def _whole_wait(buf, sem):
    whole = buf.at[pl.ds(PAD, S), :]
    return pltpu.make_async_copy(whole, whole, sem)


def _whole_waits(bufs, sems):
    return [_whole_wait(buf, sems.at[i]) for i, buf in enumerate(bufs)]


def _class_gather(views, bufs, sems, lanes):
    copies = []
    for i, (view, buf) in enumerate(zip(views, bufs)):
        if view.ndim == 2:
            copies.append(pltpu.make_async_copy(view.at[:, lanes], buf.at[pl.ds(PAD, S), :], sems.at[i]))
        else:
            per, n_cls = view.shape[0], view.shape[1]
            copies += [pltpu.make_async_copy(view.at[:, c, lanes], buf.at[pl.ds(PAD + c * per, per), :], sems.at[i])
                       for c in range(n_cls)]
    return copies


def _class_scatter(bufs, dsts, sems, lanes):
    copies = []
    for i, (buf, dst) in enumerate(zip(bufs, dsts)):
        if dst.ndim == 2:
            copies.append(pltpu.make_async_copy(buf.at[pl.ds(PAD, S), :], dst.at[:, lanes], sems.at[i]))
            continue
        per, n_cls = dst.shape[0], dst.shape[1]
        copies += [pltpu.make_async_copy(buf.at[pl.ds(PAD + c * per, per), :], dst.at[:, c, lanes], sems.at[i])
                   for c in range(n_cls)]
    return copies


def _start(copies):
    for cp in copies:
        cp.start()


def _wait(waits):
    for w in waits:
        w.wait()


def _attn_fwd(q, kvp, shards=()):
    views = [[a] + [a.reshape(S // n, n, AW) for _, n, _, _ in ATTN_PLANS[1:]] for a in (q, kvp)]
    flat = [views[a][p] for p in range(3) for a in range(2)]
    ng = len(shards)
    n_grid = AW // LANES

    def body(*refs):
        hbm = [refs[2 * p:2 * p + 2] for p in range(3)]
        refs = refs[6:]
        shard_refs, refs = refs[:ng], refs[ng:]
        y_ref, lt_ref = refs[0:2]
        whole_refs, refs = refs[2:2 + ng], refs[2 + ng:]
        bufs = [refs[2 * p:2 * p + 2] for p in range(3)]
        oc4, lc4, oc16, lc16, tab128, tab4, sem_in = refs[6:13]
        step = pl.program_id(0)
        if ng:
            enter_gather, start_gather, relay_gather, finish_gather = _gather_steps(
                shard_refs, whole_refs, *refs[13:], own_barrier=False)
            pl.when(step == 0)(enter_gather)
            pl.when(step == 0)(start_gather)
            pl.when(step == n_grid // 2)(relay_gather)
        now = [_class_gather(hbm[p], bufs[p], sem_in.at[p], _lanes_of(step)) for p in range(3)]
        nxt = [_class_gather(hbm[p], bufs[p], sem_in.at[p], _lanes_of(step + 1)) for p in range(3)]

        @pl.when(step == 0)
        def _():
            for p in range(3):
                _start(now[p])
                for b in bufs[p]:
                    b[0:PAD, :] = jnp.zeros((PAD, LANES), F32)
            _fill_bias(tab128, 128, False)
            _fill_bias(tab4, 64, True)

        def prefetch(p):
            pl.when(step + 1 < n_grid)(lambda: _start(nxt[p]))

        lane = lax.broadcasted_iota(jnp.int32, (1, LANES), 1)
        ones = jnp.ones((WIN, LANES), BF16)

        def run(plan, bq, bkv, tab, o_dst, l_dst, dst_pad):
            _, n_cls, qblk, nbc = plan
            partner = n_cls == 8

            def block(g, carry):
                own, wins, mask = _block_rows(g, qblk, nbc, partner)
                q2 = _stack_heads(bq[own, :].astype(BF16), lane)
                kw, vwin = _unpack_pair(_window(bkv, wins))
                vw = jnp.concatenate([vwin, ones], axis=1)
                s = _dot_nt(q2, kw) + tab[mask]
                m = jnp.max(s, axis=1, keepdims=True)
                oe = _dot(jnp.exp(s - m).astype(BF16), vw)
                den = oe[:, LANES:]
                dst = pl.ds(pl.multiple_of(dst_pad + g * qblk, qblk), qblk)
                o_dst[dst, :] = _unstack_heads(oe[:, 0:LANES] / den, lane)
                l_dst[dst, :] = _unstack_heads(m + jnp.log(den), lane)
                return carry
            lax.fori_loop(0, n_cls * nbc, block, 0, unroll=ATTN_UNROLL)

        _wait(_whole_waits(bufs[0], sem_in.at[0]))
        run(ATTN_PLANS[0], *bufs[0], tab128, y_ref, lt_ref, 0)
        prefetch(0)
        _wait(_whole_waits(bufs[1], sem_in.at[1]))
        run(ATTN_PLANS[1], *bufs[1], tab4, oc4, lc4, PAD)
        prefetch(1)
        _wait(_whole_waits(bufs[2], sem_in.at[2]))
        run(ATTN_PLANS[2], *bufs[2], tab128, oc16, lc16, PAD)
        prefetch(2)

        n_rows = 64

        def token_order(buf, t, n_cls):
            per = S // n_cls
            first = PAD + t * (n_rows // n_cls)
            return jnp.concatenate([buf[pl.ds(first + jj, n_cls, stride=per), :] for jj in range(n_rows // n_cls)],
                                   axis=0)

        def combine(t, carry):
            rows = pl.ds(pl.multiple_of(t * n_rows, n_rows), n_rows)
            l0, l1, l2 = lt_ref[rows, :], token_order(lc4, t, 8), token_order(lc16, t, 16)
            lm = jnp.maximum(jnp.maximum(l0, l1), l2)
            e0, e1, e2 = jnp.exp(l0 - lm), jnp.exp(l1 - lm), jnp.exp(l2 - lm)
            den = e0 + e1 + e2
            y_ref[rows, :] = (e0 * y_ref[rows, :] + e1 * token_order(oc4, t, 8)
                              + e2 * token_order(oc16, t, 16)) / den
            lt_ref[rows, :] = lm + jnp.log(den)
            return carry
        lax.fori_loop(0, S // n_rows, combine, 0, unroll=2)

        if ng:
            pl.when(step == n_grid - 1)(finish_gather)

    col = pl.BlockSpec((S, LANES), lambda h: (0, h))
    padded = pltpu.VMEM((PAD + S, LANES), F32)
    return pl.pallas_call(
        body, grid=(n_grid,), name="attn_fwd",
        in_specs=[ANY] * (6 + ng), out_specs=[col, col] + [ANY] * ng,
        out_shape=[jax.ShapeDtypeStruct((S, AW), F32)] * 2 + _gathered_shapes(shards),
        scratch_shapes=[padded] * 10 + [
            pltpu.VMEM((4, 256, WIN), F32), pltpu.VMEM((4, 128, WIN), F32), pltpu.SemaphoreType.DMA((3, 2))]
        + (_gather_scratch(ng) if ng else []),
        compiler_params=_cparams(56))(*flat, *shards)


def _conv_taps(z, zprev, row):
    z1 = jnp.where(row == 0, zprev[7:8, :], pltpu.roll(z, 1, 0))
    z2 = jnp.where(row == 0, zprev[6:7, :], jnp.where(row == 1, zprev[7:8, :], pltpu.roll(z, 2, 0)))
    return z1, z2


def _xattn_scores(qm, km):
    s = _dot_nt(qm, km)
    m = jnp.max(s, axis=1, keepdims=True)
    e = jnp.exp(s - m)
    return e, jnp.sum(e, axis=1, keepdims=True)


def _mix_out(y_attn, bcu, qx16, kv16, cw8, g_attn, g_conv, g_x, g_post, wout16, x, shards):
    def body(ya_ref, bcu_ref, halo_ref, qx_ref, kv_ref, cw_ref, ga_ref, gc_ref, gx_ref, gp_ref, w_ref, x_ref,
             ypre_ref, y16_ref, x1_ref):
        i = pl.program_id(0)
        bcu = bcu_ref[...]
        b, c, u = bcu[:, 0:CW], bcu[:, CW:2 * CW], bcu[:, 2 * CW:]
        z = c * u
        halo = halo_ref[...]
        zprev = jnp.where(i > 0, halo[:, CW:2 * CW] * halo[:, 2 * CW:], 0.0)
        row = lax.broadcasted_iota(jnp.int32, z.shape, 0)
        z1, z2 = _conv_taps(z, zprev, row)
        cw = cw_ref[...]
        y_conv = b * (z2 * cw[0:1, :] + z1 * cw[1:2, :] + z * cw[2:3, :])

        qx = qx_ref[...]
        kv = kv_ref[...]
        km, vm = kv[:, 0:XW], kv[:, XW:]
        lane = lax.broadcasted_iota(jnp.int32, qx.shape, 1)
        y_x = jnp.zeros(qx.shape, F32)
        for h in range(XW // HEAD):
            hm = (lane >= h * HEAD) & (lane < (h + 1) * HEAD)
            e, l = _xattn_scores(jnp.where(hm, qx, jnp.zeros_like(qx)), km)
            y_x = jnp.where(hm, _dot(e.astype(BF16), vm) / l, y_x)

        y_attn = ya_ref[...]
        ypre_ref[:, 0:AW] = y_attn
        ypre_ref[:, AW:AW + CW] = y_conv
        ypre_ref[:, AW + CW:] = y_x
        y = jnp.concatenate([_rms(y_attn, ga_ref[...])[0], _rms(y_conv, gc_ref[...])[0],
                             _rms(y_x, gx_ref[...])[0]], axis=1).astype(BF16)
        y16_ref[...] = y.T
        x1_ref[...] = x_ref[...] + _rms(_dot(y, w_ref[...]), gp_ref[...])[0]

    def tile(w):
        return pl.BlockSpec((TQ, w), lambda i: (i, 0))

    halo = pl.BlockSpec((SUBLANES, 3 * CW), lambda i: (jnp.maximum(i * (TQ // SUBLANES) - 1, 0), 0))
    return _call_with_gather(
        body, NT, shards, name="mix_out",
        in_specs=[tile(AW), tile(3 * CW), halo, tile(XW), _const((N_MEM, 2 * XW)), _const((SUBLANES, CW)),
                  _const((1, AW)), _const((1, CW)), _const((1, XW)), _const((1, D)), _const((D, D)), tile(D)],
        out_specs=[tile(D), _tokens_in_lanes(TQ), tile(D)],
        out_shape=[jax.ShapeDtypeStruct((S, D), F32), jax.ShapeDtypeStruct((D, S), BF16),
                   jax.ShapeDtypeStruct((S, D), F32)],
        scratch_shapes=[], vmem_mb=56,
        args=(y_attn, bcu, bcu, qx16, kv16, cw8, g_attn, g_conv, g_x, g_post, wout16, x))


def _mlp(x1, tgt, g_pre, g_post, wup8, wdn_halves):
    tq = TQ_MLP
    half = D // 2

    def body(x1_ref, t_ref, g1_ref, g2_ref, wu_ref, wda_ref, wdb_ref,
             a16_ref, du_ref, h2_ref, df2_ref, dx1_ref, loss_ref, dg_ref):
        @pl.when(pl.program_id(0) == 0)
        def _():
            loss_ref[...] = jnp.zeros_like(loss_ref)
            dg_ref[...] = jnp.zeros_like(dg_ref)

        x1 = x1_ref[...]
        g1, g2 = g1_ref[...], g2_ref[...]
        y1, n1, r1 = _rms(x1, g1)
        h2 = y1.astype(BF16)
        h2_ref[...] = h2.T
        f2a = jnp.zeros((tq, half), F32)
        f2b = jnp.zeros((tq, half), F32)
        for j in range(N_DEV):
            cols = slice(j * FF_BLK, (j + 1) * FF_BLK)
            a = jnp.maximum(_dot(h2, wu_ref[j]), 0.0)
            a16_ref[:, cols] = a.astype(BF16)
            f = (a * a).astype(BF16)
            f2a = f2a + _dot(f, wda_ref[cols, :])
            f2b = f2b + _dot(f, wdb_ref[cols, :])
        f2 = jnp.concatenate([f2a, f2b], axis=1)
        y2, n2, r2 = _rms(f2, g2)
        e = x1 + y2 - t_ref[...]
        sq = jnp.sum(jnp.sum(e * e, axis=1, keepdims=True), axis=0, keepdims=True)
        loss_ref[...] += jnp.broadcast_to(sq * (0.5 / D), loss_ref.shape)
        dout = e * (1.0 / D)
        df2, dg2 = _rms_bwd(dout, n2, r2, g2)
        df2_16 = df2.astype(BF16)
        df2_ref[...] = df2_16.T
        dh2 = jnp.zeros((tq, D), F32)
        for j in range(N_DEV):
            cols = slice(j * FF_BLK, (j + 1) * FF_BLK)
            df = _dot_nt(df2_16[:, 0:half], wda_ref[cols, :]) + _dot_nt(df2_16[:, half:], wdb_ref[cols, :])
            du = (df * (2.0 * a16_ref[:, cols].astype(F32))).astype(BF16)
            du_ref[:, cols] = du
            dh2 = dh2 + _dot_nt(du, wu_ref[j])
        dx, dg1 = _rms_bwd(dh2, n1, r1, g1)
        dx1_ref[...] = dout + dx
        dg_ref[0:1, :] += dg2
        dg_ref[1:2, :] += dg1

    def tile(w):
        return pl.BlockSpec((tq, w), lambda i: (i, 0))

    return pl.pallas_call(
        body, grid=(S // tq,), name="mlp",
        in_specs=[tile(D), tile(D), _const((1, D)), _const((1, D)), _const((N_DEV, D, FF_BLK)), _const((FF, half)), _const((FF, half))],
        out_specs=[tile(FF), tile(FF), _tokens_in_lanes(tq), _tokens_in_lanes(tq), tile(D),
                   _acc((SUBLANES, LANES)), _acc((SUBLANES, D))],
        out_shape=[jax.ShapeDtypeStruct((S, FF), BF16), jax.ShapeDtypeStruct((S, FF), BF16),
                   jax.ShapeDtypeStruct((D, S), BF16), jax.ShapeDtypeStruct((D, S), BF16),
                   jax.ShapeDtypeStruct((S, D), F32), jax.ShapeDtypeStruct((SUBLANES, LANES), F32),
                   jax.ShapeDtypeStruct((SUBLANES, D), F32)],
        compiler_params=_cparams(60))(x1, tgt, g_pre, g_post, wup8, *wdn_halves)


def _mix_out_bwd(dx1, ypre, ltot, head_ones, q, bcu, qx16, kv16, cw8, g_post, g_attn, g_conv, g_x, wout16):
    def body(dx1_ref, ypre_ref, lt_ref, e_ref, q_ref, bcu_ref, halo_ref, qx_ref, kv_ref, cw_ref, gp_ref,
             ga_ref, gc_ref, gx_ref, w_ref, dy2_ref, qdo_ref, ld_ref, dbcu_ref, dqx_ref, dgs_ref, dcw_ref, dkv_ref,
             carry):
        i = pl.program_id(0)

        @pl.when(i == 0)
        def _():
            dgs_ref[...] = jnp.zeros_like(dgs_ref)
            dcw_ref[...] = jnp.zeros_like(dcw_ref)
            dkv_ref[...] = jnp.zeros_like(dkv_ref)
            carry[...] = jnp.zeros_like(carry)

        ypre = ypre_ref[...]
        ga, gc, gx = ga_ref[...], gc_ref[...], gx_ref[...]
        ya, na, ra = _rms(ypre[:, 0:AW], ga)
        yc, nc, rc = _rms(ypre[:, AW:AW + CW], gc)
        y_x = ypre[:, AW + CW:]
        yx, nx, rx = _rms(y_x, gx)

        gp = gp_ref[...]
        _, n, r = _rms(_dot(jnp.concatenate([ya, yc, yx], axis=1).astype(BF16), w_ref[...]), gp)
        dy2, dgp = _rms_bwd(dx1_ref[...], n, r, gp)
        dy2_16 = dy2.astype(BF16)
        dy2_ref[...] = dy2_16
        dy = _dot_nt(dy2_16, w_ref[...])

        dya, dga = _rms_bwd(dy[:, 0:AW], na, ra, ga)
        dyc, dgc = _rms_bwd(dy[:, AW:AW + CW], nc, rc, gc)
        dyx, dgx = _rms_bwd(dy[:, AW + CW:], nx, rx, gx)
        qdo_ref[...] = _pack_pair(q_ref[...], dya)
        prod = dya * ypre[:, 0:AW]
        hi = prod.astype(BF16)
        lo = (prod - hi.astype(F32)).astype(BF16)
        head_sum = _dot(hi, e_ref[...]) + _dot(lo, e_ref[...])
        lane_a = lax.broadcasted_iota(jnp.int32, prod.shape, 1)
        ld_ref[...] = jnp.where((lane_a % HEAD) < HEAD // 2, lt_ref[...], head_sum)
        dgs_ref[0:1, :] += dgp
        dgs_ref[1:2, :] += jnp.concatenate([dga, dgc, dgx], axis=1)

        bcu = bcu_ref[...]
        b, c, u = bcu[:, 0:CW], bcu[:, CW:2 * CW], bcu[:, 2 * CW:]
        z = c * u
        halo = halo_ref[...]
        zprev = jnp.where(i < NT - 1, halo[:, CW:2 * CW] * halo[:, 2 * CW:], 0.0)
        row = lax.broadcasted_iota(jnp.int32, z.shape, 0)
        z1, z2 = _conv_taps(z, zprev, row)
        cw = cw_ref[...]
        conv = z2 * cw[0:1, :] + z1 * cw[1:2, :] + z * cw[2:3, :]
        dconv = dyc * b
        nxt = carry[...]
        dn1 = jnp.where(row == TQ - 1, nxt[0:1, :], pltpu.roll(dconv, TQ - 1, 0))
        dn2 = jnp.where(row == TQ - 1, nxt[1:2, :], jnp.where(row == TQ - 2, nxt[0:1, :], pltpu.roll(dconv, TQ - 2, 0)))
        carry[...] = dconv[0:SUBLANES, :]
        dz = dconv * cw[2:3, :] + dn1 * cw[1:2, :] + dn2 * cw[0:1, :]
        dbcu_ref[:, 0:CW] = (dyc * conv).astype(BF16)
        dbcu_ref[:, CW:2 * CW] = (dz * u).astype(BF16)
        dbcu_ref[:, 2 * CW:] = (dz * c).astype(BF16)
        dcw_ref[0:1, :] += jnp.sum(z2 * dconv, axis=0, keepdims=True)
        dcw_ref[1:2, :] += jnp.sum(z1 * dconv, axis=0, keepdims=True)
        dcw_ref[2:3, :] += jnp.sum(z * dconv, axis=0, keepdims=True)

        qx = qx_ref[...]
        kv = kv_ref[...]
        km, vm = kv[:, 0:XW], kv[:, XW:]
        lane = lax.broadcasted_iota(jnp.int32, qx.shape, 1)
        dqx = jnp.zeros(qx.shape, F32)
        dkm = jnp.zeros((N_MEM, XW), F32)
        dvm = jnp.zeros((N_MEM, XW), F32)
        for h in range(XW // HEAD):
            hm = (lane >= h * HEAD) & (lane < (h + 1) * HEAD)
            qm = jnp.where(hm, qx, jnp.zeros_like(qx))
            e, l = _xattn_scores(qm, km)
            p = e / l
            dom = jnp.where(hm, dyx, 0.0)
            do16 = dom.astype(BF16)
            dsum = jnp.sum(dom * y_x, axis=1, keepdims=True)
            ds = (p * (_dot_nt(do16, vm) - dsum)).astype(BF16)
            dqx = jnp.where(hm, _dot(ds, km), dqx)
            dkm = dkm + _dot_tn(ds, qm)
            dvm = dvm + _dot_tn(p.astype(BF16), do16)
        dqx_ref[...] = (dqx * SCALE).astype(BF16)
        dkv_ref[:, 0:XW] += dkm
        dkv_ref[:, XW:] += dvm

    def tile(w):
        return pl.BlockSpec((TQ, w), lambda i: (NT - 1 - i, 0))

    halo = pl.BlockSpec((SUBLANES, 3 * CW), lambda i: (jnp.maximum((NT - 1 - i) * (TQ // SUBLANES) - 1, 0), 0))
    return pl.pallas_call(
        body, grid=(NT,), name="mix_out_bwd",
        in_specs=[tile(D), tile(D), tile(AW), _const((AW, AW)), tile(AW), tile(3 * CW), halo, tile(XW),
                  _const((N_MEM, 2 * XW)), _const((SUBLANES, CW)), _const((1, D)), _const((1, AW)), _const((1, CW)),
                  _const((1, XW)), _const((D, D))],
        out_specs=[tile(D), tile(AW), tile(AW), tile(3 * CW), tile(XW), _acc((SUBLANES, D)), _acc((SUBLANES, CW)),
                   _acc((N_MEM, 2 * XW))],
        out_shape=[jax.ShapeDtypeStruct((S, D), BF16), jax.ShapeDtypeStruct((S, AW), F32),
                   jax.ShapeDtypeStruct((S, AW), F32),
                   jax.ShapeDtypeStruct((S, 3 * CW), BF16), jax.ShapeDtypeStruct((S, XW), BF16),
                   jax.ShapeDtypeStruct((SUBLANES, D), F32), jax.ShapeDtypeStruct((SUBLANES, CW), F32),
                   jax.ShapeDtypeStruct((N_MEM, 2 * XW), F32)],
        scratch_shapes=[pltpu.VMEM((SUBLANES, CW), F32)],
        compiler_params=_cparams(56))(dx1, ypre, ltot, head_ones, q, bcu, bcu, qx16, kv16, cw8, g_post, g_attn,
                                      g_conv, g_x, wout16)


def _attn_bwd(qdo, kvp, ld, chip_sums=()):
    n_in = 3
    views = [[a] + [a.reshape(S // n, n, AW) for _, n, _, _ in ATTN_PLANS[1:]] for a in (qdo, kvp, ld)]
    flat = [views[a][p] for p in range(3) for a in range(n_in)]
    ns = len(chip_sums)
    n_grid = AW // LANES

    def body(*refs):
        hbm = [refs[n_in * p:n_in * p + n_in] for p in range(3)]
        refs = refs[3 * n_in:]
        sum_refs, refs = refs[:ns], refs[ns:]
        outs = [refs[3 * p:3 * p + 3] for p in range(3)]
        landed_refs, sc = refs[9:9 + ns], refs[9 + ns:]
        bufs = [sc[3 * p:3 * p + 3] for p in range(3)]
        res = [sc[9 + 3 * p:12 + 3 * p] for p in range(3)]
        tab128, tab4, sem_in, sem_out = sc[18:22]
        step = pl.program_id(0)
        if ns:
            start_chips, finish_chips = _chips_steps(sum_refs, landed_refs, *sc[22:])
            _, _, core, chips = _place()
            signal_chips, chips_are_in = _own_barrier([(px, py, core) for px, py in chips])
            pl.when(step == 0)(signal_chips)

            def chips_go():
                chips_are_in()
                start_chips()
        now =[_class_gather(hbm[p], bufs[p], sem_in.at[p], _lanes_of(step)) for p in range(3)]
        nxt = [_class_gather(hbm[p], bufs[p], sem_in.at[p], _lanes_of(step + 1)) for p in range(3)]

        @pl.when(step == 0)
        def _():
            for p in range(3):
                _start(now[p])
                for b in bufs[p]:
                    b[0:PAD, :] = jnp.zeros((PAD, LANES), F32)
            _fill_bias(tab128, 128, False)
            _fill_bias(tab4, 64, True)

        def prefetch(p):
            pl.when(step + 1 < n_grid)(lambda: _start(nxt[p]))

        lane = lax.broadcasted_iota(jnp.int32, (1, LANES), 1)

        def run(plan, plan_bufs, tab, dst):
            _, n_cls, qblk, nbc = plan
            partner = n_cls == 8
            bqdo, bkv, bld = plan_bufs
            rq, rk, rv = dst

            def block(g, carry):
                own, wins, mask = _block_rows(g, qblk, nbc, partner)
                qb, dob = _unpack_pair(bqdo[own, :])
                q2, do2 = _stack_heads(qb, lane), _stack_heads(dob, lane)
                kw, vw = _unpack_pair(_window(bkv, wins))
                ldv = bld[own, :]
                half = HEAD // 2
                lt2 = jnp.concatenate([ldv[:, 0:1], ldv[:, HEAD:HEAD + 1]], axis=0)
                dsum2 = jnp.concatenate([ldv[:, half:half + 1], ldv[:, HEAD + half:HEAD + half + 1]], axis=0)
                p = jnp.exp(_dot_nt(q2, kw) + tab[mask] - lt2)
                ds = (p * (_dot_nt(do2, vw) - dsum2)).astype(BF16)
                rq[own, :] = _unstack_heads(_dot(ds, kw), lane)
                dkw = _dot_tn(ds, q2)
                dvw = _dot_tn(p.astype(BF16), do2)
                n_w = WIN // len(wins)
                for i, w in enumerate(wins):
                    rk[w, :] += dkw[i * n_w:(i + 1) * n_w, :]
                    rv[w, :] += dvw[i * n_w:(i + 1) * n_w, :]
                return carry
            lax.fori_loop(0, n_cls * nbc, block, 0, unroll=ATTN_UNROLL)

        tabs = (tab128, tab4, tab128)
        def drained(p):
            return lambda: _wait(_whole_waits(res[p], sem_out.at[p]))

        for p in range(3):
            pl.when(step > 0)(drained(p))
            for b in res[p][1:]:
                b[...] = jnp.zeros_like(b)
            _wait(_whole_waits(bufs[p], sem_in.at[p]))
            run(ATTN_PLANS[p], bufs[p], tabs[p], res[p])
            prefetch(p)
            _start(_class_scatter(res[p], outs[p], sem_out.at[p], _lanes_of(step)))
            if ns and p == 0:
                pl.when(step == 0)(chips_go)
        for p in range(3):
            pl.when(step == n_grid - 1)(drained(p))
        if ns:
            pl.when(step == n_grid - 1)(finish_chips)

    padded = pltpu.VMEM((PAD + S, LANES), F32)
    shapes = [jax.ShapeDtypeStruct(views[0][p].shape, F32) for p in range(3) for _ in range(3)]
    out = pl.pallas_call(
        body, grid=(n_grid,), name="attn_bwd",
        in_specs=[ANY] * (3 * n_in + ns), out_specs=[ANY] * (9 + ns),
        out_shape=shapes + _chips_shapes(chip_sums),
        scratch_shapes=[padded] * 18
        + [pltpu.VMEM((4, 256, WIN), F32), pltpu.VMEM((4, 128, WIN), F32),
           pltpu.SemaphoreType.DMA((3, n_in)), pltpu.SemaphoreType.DMA((3, 3))]
        + (_chips_scratch(ns) if ns else []),
        compiler_params=_cparams(56, **({"collective_id": ID_ATTN_BWD} if ns else {})))(*flat, *chip_sums)
    return [o.reshape(S, AW) for o in out[:9]] + list(out[9:])


def _in_proj_bwd(dqkv, dbcu, dqx, cos, sins, w16, x, g, dx1):
    tq = TQ // 2

    def body(*refs):
        parts = refs[0:9]
        dbcu_ref, dqx_ref, c_ref, s_ref, w_ref, x_ref, g_ref, dx1_ref, dp_ref, gx_ref, dg_ref = refs[9:]

        @pl.when(pl.program_id(0) == 0)
        def _():
            dg_ref[...] = jnp.zeros_like(dg_ref)

        dq, dk, dv = (parts[i][...] + parts[3 + i][...] + parts[6 + i][...] for i in range(3))
        cos, sn = _all_heads(c_ref[...]), _all_heads(s_ref[...])
        dqr = dq * SCALE
        dkr = dk
        dp = jnp.concatenate([(dqr * cos + _rot_half(dqr * sn)).astype(BF16),
                              (dkr * cos + _rot_half(dkr * sn)).astype(BF16), dv.astype(BF16),
                              dbcu_ref[...], dqx_ref[...]], axis=1)
        dp_ref[...] = dp
        dh = _dot_nt(dp, w_ref[...])
        g = g_ref[...]
        _, n, r = _rms(x_ref[...], g)
        dx, dg = _rms_bwd(dh, n, r, g)
        gx_ref[...] = dx1_ref[...] + dx
        dg_ref[0:1, :] += dg

    def tile(w):
        return pl.BlockSpec((tq, w), lambda i: (i, 0))

    return pl.pallas_call(
        body, grid=(S // tq,), name="in_proj_bwd",
        in_specs=[tile(AW)] * 9 + [tile(3 * CW), tile(XW), tile(LANES), tile(LANES), _const((D, PW)),
                                   tile(D), _const((1, D)), tile(D)],
        out_specs=[tile(PW), tile(D), _acc((SUBLANES, D))],
        out_shape=[jax.ShapeDtypeStruct((S, PW), BF16), jax.ShapeDtypeStruct((S, D), F32),
                   jax.ShapeDtypeStruct((SUBLANES, D), F32)],
        compiler_params=_cparams(56))(*dqkv, dbcu, dqx, cos, sins, w16, x, g, dx1)


def _mem_bwd(mem, g_mem, wkv16, dkv):
    def body(m_ref, g_ref, w_ref, dkv_ref, dkv16_ref, dg_ref):
        dkv16 = dkv_ref[...].astype(BF16)
        dkv16_ref[...] = dkv16
        _, n, _ = _rms(m_ref[...], g_ref[...])
        dg = jnp.sum(_dot_nt(dkv16, w_ref[...]) * n, axis=0, keepdims=True)
        dg_ref[...] = jnp.broadcast_to(dg, dg_ref.shape)

    return pl.pallas_call(
        body, name="mem_bwd",
        out_shape=[jax.ShapeDtypeStruct((N_MEM, 2 * XW), BF16), jax.ShapeDtypeStruct((SUBLANES, D), F32)],
        compiler_params=pltpu.CompilerParams(vmem_limit_bytes=32 << 20))(mem, g_mem, wkv16, dkv)


N_CHIPS = N_DEV // 2


def _pair_scratch(block):
    return [pltpu.VMEM((N_CHIPS,) + block, BF16), pltpu.VMEM((N_CHIPS,) + block, BF16),
            pltpu.SemaphoreType.DMA((N_CHIPS,)), pltpu.SemaphoreType.DMA((N_CHIPS,))]


def _swap_with_sibling(p, stage, land, send, recv):
    x, y, c = lax.axis_index("x"), lax.axis_index("y"), lax.axis_index("c")
    return pltpu.make_async_remote_copy(src_ref=stage.at[p], dst_ref=land.at[p], send_sem=send.at[p],
                                        recv_sem=recv.at[p], device_id=(x, y, 1 - c), device_id_type=MESH)


def _own_barrier(peers):
    sem = pltpu.get_barrier_semaphore()

    def signal():
        for peer in peers:
            pl.semaphore_signal(sem, inc=1, device_id=peer, device_id_type=MESH)

    return signal, lambda: pl.semaphore_wait(sem, len(peers))


def _sibling_barrier():
    x, y, c = lax.axis_index("x"), lax.axis_index("y"), lax.axis_index("c")
    return _own_barrier([(x, y, 1 - c)])


ID_WGRAD_UP, ID_WGRAD_DOWN, ID_WGRAD_ROWS, ID_ROPE_TABLE, ID_IN_PROJ, ID_ATTN_BWD, ID_WGRAD_IN = range(7)


def _wgrad_cols(place, at16, b16, blk, name, barrier_id, square_b=False, transpose_out=False, to_chips=False,
                small=()):
    m, kk = at16.shape
    assert to_chips == bool(small)
    aligned = blk % LANES == 0
    wide = blk if aligned else -(-(blk + LANES // 2) // LANES) * LANES
    assert aligned or (transpose_out and blk % SUBLANES == 0)
    block = (blk, m) if transpose_out else (m, blk)

    def chip_of(step, my_chip):
        return jnp.bitwise_xor(my_chip, N_CHIPS - 1 - step) if to_chips else step

    def body(pl_ref, a_ref, *refs):
        b_refs, refs = refs[:2 if aligned else 1], refs[2 if aligned else 1:]
        accs, refs = refs[:len(small)], refs[len(small):]
        (cs_ref, own_ref), refs = refs[:2], refs[2:]
        if to_chips:
            landed, refs = refs[0], refs[1:]
        if small:
            tot_ref, refs = refs[0], refs[1:]
        (stage, land, send, recv), refs = refs[:4], refs[4:]
        if not aligned:
            (win, wsem), refs = refs[:2], refs[2:]
        if small:
            start_small, finish_small = _small_reduce_steps(accs, tot_ref, *refs[-4:])
            refs = refs[:-4]
        step = pl.program_id(0)
        x, y, c = lax.axis_index("x"), lax.axis_index("y"), lax.axis_index("c")
        others = [(x ^ (k >> 2), y ^ ((k >> 1) & 1), c ^ (k & 1)) for k in range(1, N_DEV)]
        signal_peers, peers_are_in = _own_barrier(others if small else [(x, y, 1 - c)])
        pl.when(step == 0)(signal_peers)
        my_chip = 2 * x + y
        p = chip_of(step, my_chip)

        def fetch(at_step, mine):
            j = 2 * chip_of(at_step, my_chip) + (c if mine else 1 - c)
            first = pl.multiple_of(((j * blk) >> 7) << 7, LANES)
            slot = 2 * (at_step & 1) + mine
            return pltpu.make_async_copy(b_refs[0].at[:, pl.ds(first, wide)], win.at[slot], wsem.at[slot])

        if not aligned:
            @pl.when(step == 0)
            def _():
                fetch(0, 0).start()
                fetch(0, 1).start()

            @pl.when(step + 1 < N_CHIPS)
            def _():
                fetch(step + 1, 0).start()
                fetch(step + 1, 1).start()

        def partial(mine):
            if aligned:
                b = b_refs[mine][...]
                if square_b:
                    b = b * b
                acc = _dot(a_ref[...], b)
            else:
                fetch(step, mine).wait()
                acc = _dot(a_ref[...], win[2 * (step & 1) + mine]).T
                odd = c if mine else 1 - c
                return jnp.where(odd == 0, acc[0:blk], acc[wide - blk:wide])
            return acc.T if transpose_out else acc

        stage[p] = partial(0).astype(BF16)
        pl.when(step == 0)(peers_are_in)
        if small:
            pl.when(step == 0)(start_small)
        swap = _swap_with_sibling(p, stage, land, send, recv)
        swap.start()
        mine = partial(1)
        swap.wait()
        total = mine + land[p].astype(F32)
        cs_ref[0] = total.astype(BF16)

        @pl.when(p == my_chip)
        def _():
            own_ref[...] = total

        if to_chips:
            stage2, send2, recv2 = refs
            flipped = jnp.bitwise_xor(p, my_chip)
            k = jnp.where(flipped == 2, 0, jnp.where(flipped == 1, 1, 2))

            def to_owner(src, k_, px, py):
                return pltpu.make_async_remote_copy(src_ref=src, dst_ref=landed.at[k_], send_sem=send2.at[k_],
                                                    recv_sem=recv2.at[k_], device_id=(px, py, c), device_id_type=MESH)

            @pl.when(p != my_chip)
            def _():
                stage2[p] = total.astype(BF16)
                to_owner(stage2.at[p], k, p >> 1, p & 1).start()

            @pl.when(step == N_CHIPS - 1)
            def _():
                for k_ in range(N_CHIPS - 1):
                    to_owner(stage2.at[0], k_, x, y).wait()

        if small:
            pl.when(step == N_CHIPS - 1)(finish_small)

    def b_spec(mine):
        return pl.BlockSpec((kk, blk), lambda i, s: (0, 2 * chip_of(i, s[1]) + (s[0] if mine else 1 - s[0])))

    b_specs, b_args = ([b_spec(0), b_spec(1)], (b16, b16)) if aligned else ([ANY], (b16,))
    scratch = _pair_scratch(block)
    if not aligned:
        scratch += [pltpu.VMEM((4, kk, wide), BF16), pltpu.SemaphoreType.DMA((4,))]
    out_specs = [pl.BlockSpec((1,) + block, lambda i, s: (chip_of(i, s[1]), 0, 0)), pl.BlockSpec(block, lambda i, s: (0, 0))]
    out_shape = [jax.ShapeDtypeStruct((N_CHIPS,) + block, BF16), jax.ShapeDtypeStruct(block, F32)]
    if to_chips:
        out_specs.append(ANY)
        out_shape.append(jax.ShapeDtypeStruct((N_CHIPS - 1,) + block, BF16))
        scratch += [pltpu.VMEM((N_CHIPS,) + block, BF16), pltpu.SemaphoreType.DMA((N_CHIPS - 1,)),
                    pltpu.SemaphoreType.DMA((N_CHIPS - 1,))]
    small_specs = [pl.BlockSpec(a.shape, lambda i, s: (0, 0)) for a in small]
    if small:
        out_specs.append(pl.BlockSpec((PACK_ROWS, D), lambda i, s: (0, 0)))
        out_shape.append(jax.ShapeDtypeStruct((PACK_ROWS, D), F32))
        scratch += _small_reduce_scratch()
    return pl.pallas_call(
        body, name=name,
        grid_spec=pltpu.PrefetchScalarGridSpec(
            num_scalar_prefetch=1, grid=(N_CHIPS,),
            in_specs=[pl.BlockSpec((m, kk), lambda i, s: (0, 0), pipeline_mode=pl.Buffered(1))] + b_specs + small_specs,
            out_specs=out_specs, scratch_shapes=scratch),
        out_shape=out_shape,
        compiler_params=_cparams(56, collective_id=barrier_id),
    )(place, at16, *b_args, *small)


ROWS_STEPS = 4


def _wgrad_rows(place, products, name):
    n_prod = len(products)
    dims = [(at16.shape[0], at16.shape[1], b16.shape[1]) for at16, b16 in products]
    cut = [kk % (ROWS_STEPS * LANES) == 0 for _, kk, _ in dims]
    blocks = [(m // N_DEV, n) for m, _, n in dims]

    def body(pl_ref, *refs):
        ins, outs, scratch = refs[:2 * n_prod], refs[2 * n_prod:4 * n_prod], refs[4 * n_prod:]
        c, step = pl_ref[0], pl.program_id(0)

        def multiply(i):
            a_ref, b_ref, acc = ins[2 * i], ins[2 * i + 1], scratch[5 * i]

            @pl.when(step == 0)
            def _():
                acc[...] = _dot(a_ref[...], b_ref[...])

            if cut[i]:
                @pl.when(step > 0)
                def _():
                    acc[...] += _dot(a_ref[...], b_ref[...])

        def rows(i, owner):
            return pl.ds(pl.multiple_of(owner * blocks[i][0], blocks[i][0]), blocks[i][0])

        def send_sibling_side(i):
            acc, stage, land, send, recv = scratch[5 * i:5 * i + 5]
            swaps = []
            for p in range(N_CHIPS):
                stage[p] = acc[rows(i, 2 * p + 1 - c), :].astype(BF16)
                swaps.append(_swap_with_sibling(p, stage, land, send, recv))
                swaps[-1].start()
            return swaps

        def add_my_side(i, swaps):
            acc, land = scratch[5 * i], scratch[5 * i + 2]
            cs_ref, own_ref = outs[2 * i:2 * i + 2]
            for p in range(N_CHIPS):
                swaps[p].wait()
                total = acc[rows(i, 2 * p + c), :] + land[p].astype(F32)
                cs_ref[p] = total.astype(BF16)

                @pl.when(p == pl_ref[1])
                def _():
                    own_ref[...] = total

        signal_sibling, sibling_is_in = _sibling_barrier()
        pl.when(step == 0)(signal_sibling)
        for i in range(n_prod):
            multiply(i)

        @pl.when(step == ROWS_STEPS - 1)
        def _():
            sibling_is_in()
            swaps = [send_sibling_side(i) for i in range(n_prod)]
            for i in range(n_prod):
                add_my_side(i, swaps[i])

    in_specs, out_specs, out_shape, scratch = [pl.BlockSpec(memory_space=pltpu.SMEM)], [], [], []
    for (m, kk, n), cut_i, block in zip(dims, cut, blocks):
        chunk = kk // ROWS_STEPS
        in_specs += ([pl.BlockSpec((m, chunk), lambda i: (0, i)), pl.BlockSpec((chunk, n), lambda i: (i, 0))]
                     if cut_i else [_const((m, kk)), _const((kk, n))])
        out_specs += [_acc((N_CHIPS,) + block), _acc(block)]
        out_shape += [jax.ShapeDtypeStruct((N_CHIPS,) + block, BF16), jax.ShapeDtypeStruct(block, F32)]
        scratch += [pltpu.VMEM((m, n), F32)] + _pair_scratch(block)
    out = pl.pallas_call(
        body, grid=(ROWS_STEPS,), name=name, in_specs=in_specs, out_specs=out_specs, out_shape=out_shape,
        scratch_shapes=scratch, compiler_params=_cparams(56, collective_id=ID_WGRAD_ROWS),
    )(place, *[a for pair in products for a in pair])
    return [tuple(out[2 * i:2 * i + 2]) for i in range(n_prod)]


def _adamw_math(w, g, m, v):
    m = ADAM_B1 * m + (1.0 - ADAM_B1) * g
    v = ADAM_B2 * v + (1.0 - ADAM_B2) * jnp.square(g)
    m_hat = m / (1.0 - ADAM_B1 ** ADAM_STEP)
    v_hat = v / (1.0 - ADAM_B2 ** ADAM_STEP)
    delta = -ADAM_LR * (m_hat / (jnp.sqrt(v_hat) + ADAM_EPS) + ADAM_WD * w)
    return delta, m, v


def _adamw_shards(updates, name, chip_sums=()):
    names, nu, ns = list(updates), len(updates), len(chip_sums)

    def body(*refs):
        ins, sum_refs = refs[:5 * nu], refs[5 * nu:5 * nu + ns]
        outs = refs[5 * nu + ns:9 * nu + ns]
        landed_refs, scratch = refs[9 * nu + ns:9 * nu + 2 * ns], refs[9 * nu + 2 * ns:]
        if ns:
            start_chips, finish_chips = _chips_steps(sum_refs, landed_refs, *scratch)
            start_chips()
        for i in range(nu):
            o_ref, r_ref, w_ref, m_ref, v_ref = ins[5 * i:5 * i + 5]
            g_out, d_out, m_out, v_out = outs[4 * i:4 * i + 4]
            g = o_ref[...] + r_ref[0].astype(F32) + r_ref[1].astype(F32) + r_ref[2].astype(F32)
            g_out[...] = g
            d_out[...], m_out[...], v_out[...] = _adamw_math(w_ref[...], g, m_ref[...], v_ref[...])
        if ns:
            finish_chips()

    vmem = pl.BlockSpec(memory_space=pltpu.VMEM)
    out = pl.pallas_call(
        body, name=name,
        in_specs=[vmem] * (5 * nu) + [ANY] * ns, out_specs=[vmem] * (4 * nu) + [ANY] * ns,
        out_shape=[jax.ShapeDtypeStruct(updates[n][2].shape, F32) for n in names for _ in range(4)]
        + _chips_shapes(chip_sums),
        scratch_shapes=_chips_scratch(ns) if ns else [],
        compiler_params=pltpu.CompilerParams(vmem_limit_bytes=56 << 20),
    )(*[a for n in names for a in updates[n]], *chip_sums)
    return {n: out[4 * i:4 * i + 4] for i, n in enumerate(names)}, list(out[4 * nu:])


def _place():
    x, y, c = lax.axis_index("x"), lax.axis_index("y"), lax.axis_index("c")
    chips = [(1 - x, y), (x, 1 - y), (1 - x, 1 - y)]
    return x, y, c, chips


def _gather_steps(ins, outs, send, recv, lsem, own_barrier=True):
    nt = len(ins)
    x, y, c, (xn, yn, diag) = _place()
    me, sib = (x, y, c), (x, y, 1 - c)

    def slot(t, px, py, pc):
        return outs[t].at[4 * px + 2 * py + pc]

    def copy(t, k, block, to, src=None):
        return pltpu.make_async_remote_copy(
            src_ref=slot(t, *block) if src is None else src, dst_ref=slot(t, *block),
            send_sem=send.at[t, k], recv_sem=recv.at[t, k], device_id=to, device_id_type=MESH)

    mine = [pltpu.make_async_copy(ins[t], slot(t, *me), lsem.at[t]) for t in range(nt)]
    first = [copy(t, k, me, to, src=ins[t]) for t in range(nt) for k, to in ((0, sib), (1, (*xn, c)), (2, (*yn, c)))]

    if own_barrier:
        signal_peers, peers_are_in = _own_barrier([sib, (*xn, c), (*yn, c)])

    def enter():
        if own_barrier:
            signal_peers()
        for cp in mine:
            cp.start()

    def start():
        if own_barrier:
            peers_are_in()
        for cp in first:
            cp.start()

    def landed(k, chip, also_to=None):
        for t in range(nt):
            copy(t, k, (*chip, c), me).wait_recv()
            if also_to is not None:
                copy(t, 3, (*chip, c), (*also_to, c)).start()
            copy(t, 3 + k, (*chip, c), sib).start()

    def relay():
        @pl.when(c == 0)
        def _():
            landed(1, xn, also_to=yn)
            landed(2, yn)

        @pl.when(c == 1)
        def _():
            landed(2, yn, also_to=xn)
            landed(1, xn)

    def finish():
        landed(3, diag)
        for t in range(nt):
            copy(t, 0, sib, me).wait_recv()
            for k, chip in ((4, xn), (5, yn), (6, diag)):
                copy(t, k, (*chip, 1 - c), me).wait_recv()
            for k in range(7):
                copy(t, k, me, sib).wait_send()
        for cp in mine:
            cp.wait()

    return enter, start, relay, finish


def _gather_scratch(nt):
    return [pltpu.SemaphoreType.DMA((nt, 7)), pltpu.SemaphoreType.DMA((nt, 7)), pltpu.SemaphoreType.DMA((nt,))]


def _gathered_shapes(shards):
    return [jax.ShapeDtypeStruct((N_DEV,) + s.shape, s.dtype) for s in shards]


def _call_with_gather(body, n_grid, shards, *, name, in_specs, out_specs, out_shape, scratch_shapes, vmem_mb, args,
                      collective_id=None):
    assert (collective_id is None) == (not shards)
    ng, n_in, n_out = len(shards), len(in_specs), len(out_specs)

    def wrapped(*refs):
        ins, shard_refs = refs[:n_in], refs[n_in:n_in + ng]
        outs = refs[n_in + ng:n_in + ng + n_out]
        whole_refs = refs[n_in + ng + n_out:n_in + 2 * ng + n_out]
        scratch = refs[n_in + 2 * ng + n_out:]
        if ng:
            enter, start, relay, finish = _gather_steps(shard_refs, whole_refs, *scratch[len(scratch_shapes):])
            pl.when(pl.program_id(0) == 0)(enter)
            pl.when(pl.program_id(0) == 0)(start)
            pl.when(pl.program_id(0) == n_grid // 2)(relay)
        body(*ins, *outs, *scratch[:len(scratch_shapes)])
        if ng:
            pl.when(pl.program_id(0) == n_grid - 1)(finish)

    return pl.pallas_call(
        wrapped, grid=(n_grid,), name=name,
        in_specs=list(in_specs) + [ANY] * ng, out_specs=list(out_specs) + [ANY] * ng,
        out_shape=list(out_shape) + _gathered_shapes(shards),
        scratch_shapes=list(scratch_shapes) + (_gather_scratch(ng) if ng else []),
        compiler_params=_cparams(vmem_mb, **({"collective_id": collective_id} if shards else {})))(*args, *shards)


def _chips_steps(ins, outs, send, recv):
    _, _, c, chips = _place()
    copies = [pltpu.make_async_remote_copy(
        src_ref=ins[t].at[2 * px + py], dst_ref=outs[t].at[j], send_sem=send.at[t, j], recv_sem=recv.at[t, j],
        device_id=(px, py, c), device_id_type=MESH) for t in range(len(ins)) for j, (px, py) in enumerate(chips)]

    def start():
        for cp in copies:
            cp.start()

    def finish():
        for cp in copies:
            cp.wait()

    return start, finish


def _chips_scratch(nt):
    return [pltpu.SemaphoreType.DMA((nt, 3)), pltpu.SemaphoreType.DMA((nt, 3))]


def _chips_shapes(cs16s):
    return [jax.ShapeDtypeStruct((3,) + g.shape[1:], g.dtype) for g in cs16s]


SMALL = (("g_pre_mix", 0, 0, D), ("g_mem", 1, 0, D), ("g_post_mix", 2, 0, D), ("g_attn_out", 3, 0, AW),
         ("g_conv_out", 3, AW, CW), ("g_xattn_out", 3, AW + CW, XW), ("g_post_mlp", 4, 0, D), ("g_pre_mlp", 5, 0, D))
CONV_ROW = 8
PACK_ROWS = 16


LOSS_ROW = 15


def _small_reduce_steps(accs, tot_ref, pack, land, send, recv):
    acc_in, acc_mem, acc_mix, acc_mlp, acc_cw, acc_loss = accs
    x, y, c, _ = _place()
    me = 4 * x + 2 * y + c
    copies = []
    for k in range(1, N_DEV):
        kx, ky, kc = (k >> 2) & 1, (k >> 1) & 1, k & 1
        peer = (1 - x if kx else x, 1 - y if ky else y, 1 - c if kc else c)
        copies.append(pltpu.make_async_remote_copy(
            src_ref=pack, dst_ref=land.at[me], send_sem=send.at[k - 1], recv_sem=recv.at[k - 1],
            device_id=peer, device_id_type=MESH))

    def start():
        pack[...] = jnp.zeros_like(pack)
        pack[0:1, :] = acc_in[0:1, :]
        pack[1:2, :] = acc_mem[0:1, :]
        pack[2:4, :] = acc_mix[0:2, :]
        pack[4:6, :] = acc_mlp[0:2, :]
        pack[CONV_ROW:CONV_ROW + 3, 0:CW] = acc_cw[0:3, :]
        pack[LOSS_ROW:LOSS_ROW + 1, 0:LANES] = acc_loss[0:1, :]
        land[me] = pack[...]
        for cp in copies:
            cp.start()

    def finish():
        for cp in copies:
            cp.wait()
        tot = land[0]
        for s in range(1, N_DEV):
            tot = tot + land[s]
        tot_ref[...] = tot

    return start, finish


def _small_reduce_scratch():
    return [pltpu.VMEM((PACK_ROWS, D), F32), pltpu.VMEM((N_DEV, PACK_ROWS, D), F32),
            pltpu.SemaphoreType.DMA((N_DEV - 1,)), pltpu.SemaphoreType.DMA((N_DEV - 1,))]


def _small_update(tot, me, params):
    flat = [a for n, _, _, _ in SMALL for a in params[n]] + list(params["conv_w"])
    n_par = len(SMALL) + 1
    tap_cols = CW // N_DEV

    def body(*refs):
        me_ref, tot_ref = refs[0:2]
        ins = refs[2:2 + 3 * n_par]
        loss_out = refs[2 + 3 * n_par]
        outs = refs[3 + 3 * n_par:]
        tot = tot_ref[...]
        loss_out[...] = jnp.broadcast_to(tot[LOSS_ROW:LOSS_ROW + 1, 0:LANES], loss_out.shape)

        def update(i, g):
            w_ref, m_ref, v_ref = ins[3 * i:3 * i + 3]
            for o_ref, res in zip(outs[4 * i:4 * i + 4], (g,) + _adamw_math(w_ref[...], g, m_ref[...], v_ref[...])):
                if len(o_ref.shape) == 3:
                    for t in range(o_ref.shape[0]):
                        o_ref[t] = res[t:t + 1, :]
                else:
                    o_ref[...] = res

        for i, (_, row, lane0, width) in enumerate(SMALL):
            update(i, tot[row:row + 1, lane0:lane0 + width])
        me = me_ref[0]
        taps = pltpu.roll(tot[CONV_ROW:CONV_ROW + SUBLANES, 0:CW], jnp.where(me == 0, 0, CW - me * tap_cols), 1)
        update(n_par - 1, taps[0:3, 0:tap_cols])

    shapes = [jax.ShapeDtypeStruct(params[n][0].shape, F32) for n, _, _, _ in SMALL] + [
        jax.ShapeDtypeStruct((3, 1, tap_cols), F32)]
    vmem = pl.BlockSpec(memory_space=pltpu.VMEM)
    loss, *out = pl.pallas_call(
        body, name="small_update",
        in_specs=[pl.BlockSpec(memory_space=pltpu.SMEM)] + [vmem] * (1 + 3 * n_par),
        out_shape=[jax.ShapeDtypeStruct((SUBLANES, LANES), F32)] + [s for s in shapes for _ in range(4)],
    )(me, tot, *flat)
    names = [n for n, _, _, _ in SMALL] + ["conv_w"]
    return loss[0, 0], {n: out[4 * i:4 * i + 4] for i, n in enumerate(names)}


def _local_step(x, mem, pos, gains, shards, tgt, place):
    half = HEAD // 2
    inv_freq = jnp.float32(ROPE_THETA) ** (-(jnp.arange(half, dtype=F32) * 2.0 / HEAD))
    invf = jnp.tile(inv_freq, LANES // half)[None, :]
    sgn = jnp.tile(jnp.concatenate([-jnp.ones((half,), F32), jnp.ones((half,), F32)]), LANES // HEAD)[None, :]
    cos, sins, win8 = _rope_table(pos.astype(F32).reshape(S, 1), invf, sgn, [shards["w_in"]])
    wdn_left, wdn_right = shards["w_down"][:, 0:D // 2], shards["w_down"][:, D // 2:]
    q, kvp, bcu, qx16, h16, win16, wout8, wkv8, conv8, wdn8_right = _in_proj(
        x, gains["g_pre_mix"], win8, cos, sins, [shards["w_out"], shards["w_mem_kv"], shards["conv_w"], wdn_right])
    wout16, wkv16 = wout8.reshape(D, D), wkv8.reshape(D, 2 * XW)
    cw_full = conv8[:, 0:3, 0:CW // N_DEV].transpose(1, 0, 2).reshape(3, CW)
    cw8 = jnp.zeros((SUBLANES, CW), F32).at[0:3].set(cw_full)
    y_attn, ltot, wup8, wdn8_left = _attn_fwd(q, kvp, [shards["w_up"], wdn_left])
    wdn_halves = (wdn8_left.reshape(FF, D // 2), wdn8_right.reshape(FF, D // 2))
    memn16, kv16 = _mem_fwd(mem, gains["g_mem"], wkv16)
    ypre, y16, x1 = _mix_out(y_attn, bcu, qx16, kv16, cw8, gains["g_attn_out"], gains["g_conv_out"],
                                 gains["g_xattn_out"], gains["g_post_mix"], wout16, x, [])
    a16, du16, h2_16, df2_16, dx1, loss8, dg_mlp = _mlp(
        x1, tgt, gains["g_pre_mlp"], gains["g_post_mlp"], wup8, wdn_halves)

    sums = {"w_up": _wgrad_cols(place, h2_16, du16, FF_BLK, "wgrad_up", ID_WGRAD_UP),
            "w_down": _wgrad_cols(place, df2_16, a16, FF_BLK, "wgrad_down", ID_WGRAD_DOWN, square_b=True,
                                  transpose_out=True)}

    head_id = jnp.arange(AW, dtype=jnp.int32) // HEAD
    head_ones = (head_id[:, None] == head_id[None, :]).astype(BF16)
    dy2_16, qdo, ld, dbcu, dqx, dgs, dcw, dkv = _mix_out_bwd(
        dx1, ypre, ltot, head_ones, q, bcu, qx16, kv16, cw8, gains["g_post_mix"], gains["g_attn_out"],
        gains["g_conv_out"], gains["g_xattn_out"], wout16)
    dkv16, dg_mem = _mem_bwd(mem, gains["g_mem"], wkv16, dkv)
    sums["w_mem_kv"], sums["w_out"] = _wgrad_rows(place, [(memn16, dkv16), (y16, dy2_16)], "wgrad_mem_kv_out")
    out = _attn_bwd(qdo, kvp, ld, [s[0] for s in sums.values()])
    dqkv, landed = out[:9], out[9:]
    reduced = {n: (s[1], landed[t]) for t, (n, s) in enumerate(sums.items())}
    dproj16, grad_x, dg_in = _in_proj_bwd(dqkv, dbcu, dqx, cos, sins, win16, x, gains["g_pre_mix"], dx1)

    _, in_own, in_landed, small_tot = _wgrad_cols(place, h16, dproj16, PW // N_DEV, "wgrad_in", ID_WGRAD_IN,
                                                  transpose_out=True, to_chips=True,
                                                  small=(dg_in, dg_mem, dgs, dg_mlp, dcw, loss8))
    reduced["w_in"] = (in_own, in_landed)
    return grad_x, reduced, small_tot


BIG = ("w_in", "w_mem_kv", "w_out", "w_up", "w_down")
ORDER = ("g_pre_mix", "g_mem", "w_in", "w_mem_kv", "conv_w", "g_attn_out", "g_conv_out", "g_xattn_out", "w_out",
         "g_post_mix", "g_pre_mlp", "w_up", "w_down", "g_post_mlp")


def kernel(x, mem, positions, g_pre_mix, g_mem, w_in, w_mem_kv, conv_w, g_attn_out, g_conv_out, g_xattn_out, w_out, g_post_mix, g_pre_mlp, w_up, w_down, g_post_mlp, loss_target, m_g_pre_mix, m_g_mem, m_w_in, m_w_mem_kv, m_conv_w, m_g_attn_out, m_g_conv_out, m_g_xattn_out, m_w_out, m_g_post_mix, m_g_pre_mlp, m_w_up, m_w_down, m_g_post_mlp, v_g_pre_mix, v_g_mem, v_w_in, v_w_mem_kv, v_conv_w, v_g_attn_out, v_g_conv_out, v_g_xattn_out, v_w_out, v_g_post_mix, v_g_pre_mlp, v_w_up, v_w_down, v_g_post_mlp):
    w = dict(g_pre_mix=g_pre_mix, g_mem=g_mem, w_in=w_in, w_mem_kv=w_mem_kv, conv_w=conv_w, g_attn_out=g_attn_out,
             g_conv_out=g_conv_out, g_xattn_out=g_xattn_out, w_out=w_out, g_post_mix=g_post_mix, g_pre_mlp=g_pre_mlp,
             w_up=w_up, w_down=w_down, g_post_mlp=g_post_mlp)
    mo = dict(g_pre_mix=m_g_pre_mix, g_mem=m_g_mem, w_in=m_w_in, w_mem_kv=m_w_mem_kv, conv_w=m_conv_w,
              g_attn_out=m_g_attn_out, g_conv_out=m_g_conv_out, g_xattn_out=m_g_xattn_out, w_out=m_w_out,
              g_post_mix=m_g_post_mix, g_pre_mlp=m_g_pre_mlp, w_up=m_w_up, w_down=m_w_down, g_post_mlp=m_g_post_mlp)
    vo = dict(g_pre_mix=v_g_pre_mix, g_mem=v_g_mem, w_in=v_w_in, w_mem_kv=v_w_mem_kv, conv_w=v_conv_w,
              g_attn_out=v_g_attn_out, g_conv_out=v_g_conv_out, g_xattn_out=v_g_xattn_out, w_out=v_w_out,
              g_post_mix=v_g_post_mix, g_pre_mlp=v_g_pre_mlp, w_up=v_w_up, w_down=v_w_down, g_post_mlp=v_g_post_mlp)

    xi, yi, ci = lax.axis_index("x"), lax.axis_index("y"), lax.axis_index("c")
    me = 4 * xi + 2 * yi + ci
    place = jnp.stack([ci, 2 * xi + yi]).astype(jnp.int32)

    shards = {n: w[n][0].astype(BF16) for n in BIG}
    shards["conv_w"] = jnp.zeros((SUBLANES, LANES), F32).at[0:3, 0:CW // N_DEV].set(conv_w[0])

    gains = {n: w[n] for n, _, _, _ in SMALL}
    grad_x, reduced, small_tot = _local_step(x[0], mem[0], positions[0], gains, shards, loss_target[0], place)

    def shard(n, a):
        return a[0].T if n == "w_in" else a[0]

    updated = {}
    for group in (("w_up", "w_down"), ("w_in", "w_out", "w_mem_kv")):
        updated.update(_adamw_shards({n: (*reduced[n], shard(n, w[n]), shard(n, mo[n]), shard(n, vo[n]))
                                      for n in group}, "adamw_" + "_".join(group))[0])
    grad, delta, new_m, new_v = {}, {}, {}, {}
    for n, res in updated.items():
        grad[n], delta[n], new_m[n], new_v[n] = [(a.T if n == "w_in" else a)[None] for a in res]

    params = {n: (w[n], mo[n], vo[n]) for n, _, _, _ in SMALL}
    params["conv_w"] = (w["conv_w"][0], mo["conv_w"][0], vo["conv_w"][0])
    loss, small = _small_update(small_tot, me.reshape(1).astype(jnp.int32), params)
    for n, (g, d_, m_, v_) in small.items():
        lead = (lambda a: a.reshape(conv_w.shape)) if n == "conv_w" else (lambda a: a)
        grad[n], delta[n], new_m[n], new_v[n] = lead(g), lead(d_), lead(m_), lead(v_)

    return (loss, grad_x[None], *[grad[n] for n in ORDER], *[delta[n] for n in ORDER],
            *[new_m[n] for n in ORDER], *[new_v[n] for n in ORDER])
```

```python
import jax
import jax.numpy as jnp
from jax import lax
from jax.experimental import pallas as pl
from jax.experimental.pallas import tpu as pltpu

F32, BF16 = jnp.float32, jnp.bfloat16
MESH = pl.DeviceIdType.MESH
ANY = pl.BlockSpec(memory_space=pl.ANY)

N_DEV = 8
D = 1024
S = 4096
N_MEM = 256
HEAD = 64
AW, CW, XW = 512, 256, 256
PW = 3 * AW + 3 * CW + XW
FF = 4096
FF_BLK = FF // N_DEV
EPS = 1e-6
NEG = -1e30
SCALE = HEAD ** -0.5
ROPE_THETA = 10000.0
LANES = 128
SUBLANES = 8

ADAM_LR, ADAM_B1, ADAM_B2, ADAM_EPS, ADAM_WD, ADAM_STEP = 0.001, 0.9, 0.999, 1e-08, 0.01, 10

TQ = 512
TQ_MLP = 512
NT = S // TQ


def _cparams(vmem_mb, n_grid=1, **more):
    return pltpu.CompilerParams(dimension_semantics=("arbitrary",) * n_grid, vmem_limit_bytes=vmem_mb << 20, **more)


def _const(shape):
    nd = len(shape)
    return pl.BlockSpec(shape, lambda *_: (0,) * nd, pipeline_mode=pl.Buffered(1))


def _acc(shape):
    nd = len(shape)
    return pl.BlockSpec(shape, lambda *_: (0,) * nd)


def _tokens_in_lanes(tq):
    return pl.BlockSpec((D, tq), lambda i: (0, i))


def _dot(a, b):
    return jnp.dot(a, b, preferred_element_type=F32)


def _dot_nt(a, b):
    return lax.dot_general(a, b, (((1,), (1,)), ((), ())), preferred_element_type=F32)


def _dot_tn(a, b):
    return lax.dot_general(a, b, (((0,), (0,)), ((), ())), preferred_element_type=F32)


def _rms(x, g):
    r = lax.rsqrt(jnp.mean(x * x, axis=-1, keepdims=True) + EPS)
    n = x * r
    return n * g, n, r


def _rms_bwd(dy, n, r, g):
    dn = dy * g
    dx = r * (dn - n * jnp.mean(dn * n, axis=-1, keepdims=True))
    return dx, jnp.sum(dy * n, axis=0, keepdims=True)


def _rot_half(t):
    lane = lax.broadcasted_iota(jnp.int32, t.shape, 1)
    n = t.shape[1]
    return jnp.where((lane % HEAD) < HEAD // 2, pltpu.roll(t, n - HEAD // 2, 1), pltpu.roll(t, HEAD // 2, 1))


def _rope_table(pos_col, invf, sgn, shards):
    def body(p_ref, f_ref, s_ref, c_out, s_out):
        ang = p_ref[...] * f_ref[...]
        c_out[...] = jnp.cos(ang)
        s_out[...] = jnp.sin(ang) * s_ref[...]

    tile = pl.BlockSpec((TQ, LANES), lambda i: (i, 0))
    return _call_with_gather(
        body, NT, shards, name="rope_table",
        in_specs=[pl.BlockSpec((TQ, 1), lambda i: (i, 0)), _const((1, LANES)), _const((1, LANES))],
        out_specs=[tile, tile], out_shape=[jax.ShapeDtypeStruct((S, LANES), F32)] * 2,
        scratch_shapes=[], vmem_mb=32, args=(pos_col, invf, sgn), collective_id=ID_ROPE_TABLE)


def _all_heads(t):
    return jnp.tile(t, (1, AW // LANES))


def _mem_fwd(mem, g_mem, wkv16):
    def body(m_ref, g_ref, w_ref, n16_ref, kv_ref):
        y, _, _ = _rms(m_ref[...], g_ref[...])
        y16 = y.astype(BF16)
        n16_ref[...] = y16.T
        kv_ref[...] = _dot(y16, w_ref[...]).astype(BF16)

    return pl.pallas_call(
        body, name="mem_fwd",
        out_shape=[jax.ShapeDtypeStruct((D, N_MEM), BF16), jax.ShapeDtypeStruct((N_MEM, 2 * XW), BF16)],
        compiler_params=pltpu.CompilerParams(vmem_limit_bytes=32 << 20))(mem, g_mem, wkv16)


def _in_proj(x, g, w8, cos, sins, shards):
    blk = PW // N_DEV

    def body(x_ref, g_ref, w8_ref, c_ref, s_ref, q_ref, kv_ref, bcu_ref, qx_ref, h_ref, w_out, w_ref):
        @pl.when(pl.program_id(0) == 0)
        def _():
            for j in range(N_DEV):
                w_ref[:, j * blk:(j + 1) * blk] = w8_ref[j]
            w_out[...] = w_ref[...]

        y, _, _ = _rms(x_ref[...], g_ref[...])
        h = y.astype(BF16)
        h_ref[...] = h.T
        proj = _dot(h, w_ref[...])
        cos, sn = _all_heads(c_ref[...]), _all_heads(s_ref[...])
        q, k = proj[:, 0:AW], proj[:, AW:2 * AW]
        q_ref[...] = (q * cos + _rot_half(q) * sn) * SCALE
        kv_ref[...] = _pack_pair(k * cos + _rot_half(k) * sn, proj[:, 2 * AW:3 * AW])
        bcu_ref[...] = proj[:, 3 * AW:3 * AW + 3 * CW]
        qx_ref[...] = (proj[:, 3 * AW + 3 * CW:] * SCALE).astype(BF16)

    def tile(w):
        return pl.BlockSpec((TQ, w), lambda i: (i, 0))

    return _call_with_gather(
        body, NT, shards, name="in_proj",
        in_specs=[tile(D), _const((1, D)), _const((N_DEV, D, blk)), tile(LANES), tile(LANES)],
        out_specs=[tile(AW), tile(AW), tile(3 * CW), tile(XW), _tokens_in_lanes(TQ), _acc((D, PW))],
        out_shape=[jax.ShapeDtypeStruct((S, AW), F32)] * 2 + [
            jax.ShapeDtypeStruct((S, 3 * CW), F32), jax.ShapeDtypeStruct((S, XW), BF16),
            jax.ShapeDtypeStruct((D, S), BF16), jax.ShapeDtypeStruct((D, PW), BF16)],
        scratch_shapes=[pltpu.VMEM((D, PW), BF16)], vmem_mb=56, args=(x, g, w8, cos, sins),
        collective_id=ID_IN_PROJ)


ATTN_PLANS = (("p1", 1, 128, 32), ("p4", 8, 64, 8), ("p16", 16, 128, 2))
PAD = 128
WIN = 256


ATTN_UNROLL = 16


def _fill_bias(tab, qblk, partner):
    qi = lax.broadcasted_iota(jnp.int32, (2 * qblk, WIN), 0) & (qblk - 1)
    kj = lax.broadcasted_iota(jnp.int32, (2 * qblk, WIN), 1)
    piece = kj >> (qblk.bit_length() - 1)
    kk = kj & (qblk - 1)
    prev = (piece & 1) == 0
    of_partner = piece >= 2
    for first in (0, 1):
        for par in (0, 1):
            lo = jnp.where(prev, (qblk if first else qi) + jnp.where(of_partner, par, 0), 0)
            hi = jnp.where(prev, qblk, qi + jnp.where(of_partner, par - 1, 0))
            tab[2 * first + par] = jnp.where((kk >= lo) & (kk <= hi), 0.0, NEG).astype(F32)


def _block_rows(g, qblk, nbc, partner):
    own = pl.ds(pl.multiple_of(PAD + g * qblk, qblk), qblk)
    first = ((g & (nbc - 1)) == 0).astype(jnp.int32)
    if partner:
        gp = jnp.bitwise_xor(g, 4 * nbc)
        wins = (pl.ds(pl.multiple_of(PAD + (g - 1) * qblk, qblk), 2 * qblk),
                pl.ds(pl.multiple_of(PAD + (gp - 1) * qblk, qblk), 2 * qblk))
        return own, wins, 2 * first + ((g >> ((4 * nbc).bit_length() - 1)) & 1)
    return own, (pl.ds(pl.multiple_of(PAD + (g - 1) * qblk, qblk), 2 * qblk),), 2 * first


def _pack_pair(lo, hi):
    lo_bits = lax.bitcast_convert_type(lo.astype(BF16).astype(F32), jnp.uint32) >> 16
    hi_bits = lax.bitcast_convert_type(hi.astype(BF16).astype(F32), jnp.uint32) & jnp.uint32(0xFFFF0000)
    return lax.bitcast_convert_type(hi_bits | lo_bits, F32)


def _unpack_pair(c):
    bits = lax.bitcast_convert_type(c, jnp.uint32)
    lo = lax.bitcast_convert_type(bits << 16, F32).astype(BF16)
    hi = lax.bitcast_convert_type(bits & jnp.uint32(0xFFFF0000), F32).astype(BF16)
    return lo, hi


def _window(ref, wins):
    parts = [ref[w, :] for w in wins]
    return parts[0] if len(parts) == 1 else jnp.concatenate(parts, axis=0)


def _stack_heads(t, lane):
    zero = jnp.zeros_like(t)
    return jnp.concatenate([jnp.where(lane < HEAD, t, zero), jnp.where(lane >= HEAD, t, zero)], axis=0)


def _unstack_heads(t2, lane):
    half = t2.shape[0] // 2
    return jnp.where(lane < HEAD, t2[0:half, :], t2[half:, :])


def _lanes_of(step):
    return pl.ds(pl.multiple_of(step * LANES, LANES), LANES)


def _whole_wait(buf, sem):
    whole = buf.at[pl.ds(PAD, S), :]
    return pltpu.make_async_copy(whole, whole, sem)


def _whole_waits(bufs, sems):
    return [_whole_wait(buf, sems.at[i]) for i, buf in enumerate(bufs)]


def _class_gather(views, bufs, sems, lanes):
    copies = []
    for i, (view, buf) in enumerate(zip(views, bufs)):
        if view.ndim == 2:
            copies.append(pltpu.make_async_copy(view.at[:, lanes], buf.at[pl.ds(PAD, S), :], sems.at[i]))
        else:
            per, n_cls = view.shape[0], view.shape[1]
            copies += [pltpu.make_async_copy(view.at[:, c, lanes], buf.at[pl.ds(PAD + c * per, per), :], sems.at[i])
                       for c in range(n_cls)]
    return copies


def _class_scatter(bufs, dsts, sems, lanes):
    copies = []
    for i, (buf, dst) in enumerate(zip(bufs, dsts)):
        if dst.ndim == 2:
            copies.append(pltpu.make_async_copy(buf.at[pl.ds(PAD, S), :], dst.at[:, lanes], sems.at[i]))
            continue
        per, n_cls = dst.shape[0], dst.shape[1]
        copies += [pltpu.make_async_copy(buf.at[pl.ds(PAD + c * per, per), :], dst.at[:, c, lanes], sems.at[i])
                   for c in range(n_cls)]
    return copies


def _start(copies):
    for cp in copies:
        cp.start()


def _wait(waits):
    for w in waits:
        w.wait()


def _attn_fwd(q, kvp, shards=()):
    views = [[a] + [a.reshape(S // n, n, AW) for _, n, _, _ in ATTN_PLANS[1:]] for a in (q, kvp)]
    flat = [views[a][p] for p in range(3) for a in range(2)]
    ng = len(shards)
    n_grid = AW // LANES

    def body(*refs):
        hbm = [refs[2 * p:2 * p + 2] for p in range(3)]
        refs = refs[6:]
        shard_refs, refs = refs[:ng], refs[ng:]
        y_ref, lt_ref = refs[0:2]
        whole_refs, refs = refs[2:2 + ng], refs[2 + ng:]
        bufs = [refs[2 * p:2 * p + 2] for p in range(3)]
        oc4, lc4, oc16, lc16, tab128, tab4, sem_in = refs[6:13]
        step = pl.program_id(0)
        if ng:
            enter_gather, start_gather, relay_gather, finish_gather = _gather_steps(
                shard_refs, whole_refs, *refs[13:], own_barrier=False)
            pl.when(step == 0)(enter_gather)
            pl.when(step == 0)(start_gather)
            pl.when(step == n_grid // 2)(relay_gather)
        now = [_class_gather(hbm[p], bufs[p], sem_in.at[p], _lanes_of(step)) for p in range(3)]
        nxt = [_class_gather(hbm[p], bufs[p], sem_in.at[p], _lanes_of(step + 1)) for p in range(3)]

        @pl.when(step == 0)
        def _():
            for p in range(3):
                _start(now[p])
                for b in bufs[p]:
                    b[0:PAD, :] = jnp.zeros((PAD, LANES), F32)
            _fill_bias(tab128, 128, False)
            _fill_bias(tab4, 64, True)

        def prefetch(p):
            pl.when(step + 1 < n_grid)(lambda: _start(nxt[p]))

        lane = lax.broadcasted_iota(jnp.int32, (1, LANES), 1)
        ones = jnp.ones((WIN, LANES), BF16)

        def run(plan, bq, bkv, tab, o_dst, l_dst, dst_pad):
            _, n_cls, qblk, nbc = plan
            partner = n_cls == 8

            def block(g, carry):
                own, wins, mask = _block_rows(g, qblk, nbc, partner)
                q2 = _stack_heads(bq[own, :].astype(BF16), lane)
                kw, vwin = _unpack_pair(_window(bkv, wins))
                vw = jnp.concatenate([vwin, ones], axis=1)
                s = _dot_nt(q2, kw) + tab[mask]
                m = jnp.max(s, axis=1, keepdims=True)
                oe = _dot(jnp.exp(s - m).astype(BF16), vw)
                den = oe[:, LANES:]
                dst = pl.ds(pl.multiple_of(dst_pad + g * qblk, qblk), qblk)
                o_dst[dst, :] = _unstack_heads(oe[:, 0:LANES] / den, lane)
                l_dst[dst, :] = _unstack_heads(m + jnp.log(den), lane)
                return carry
            lax.fori_loop(0, n_cls * nbc, block, 0, unroll=ATTN_UNROLL)

        _wait(_whole_waits(bufs[0], sem_in.at[0]))
        run(ATTN_PLANS[0], *bufs[0], tab128, y_ref, lt_ref, 0)
        prefetch(0)
        _wait(_whole_waits(bufs[1], sem_in.at[1]))
        run(ATTN_PLANS[1], *bufs[1], tab4, oc4, lc4, PAD)
        prefetch(1)
        _wait(_whole_waits(bufs[2], sem_in.at[2]))
        run(ATTN_PLANS[2], *bufs[2], tab128, oc16, lc16, PAD)
        prefetch(2)

        n_rows = 64

        def token_order(buf, t, n_cls):
            per = S // n_cls
            first = PAD + t * (n_rows // n_cls)
            return jnp.concatenate([buf[pl.ds(first + jj, n_cls, stride=per), :] for jj in range(n_rows // n_cls)],
                                   axis=0)

        def combine(t, carry):
            rows = pl.ds(pl.multiple_of(t * n_rows, n_rows), n_rows)
            l0, l1, l2 = lt_ref[rows, :], token_order(lc4, t, 8), token_order(lc16, t, 16)
            lm = jnp.maximum(jnp.maximum(l0, l1), l2)
            e0, e1, e2 = jnp.exp(l0 - lm), jnp.exp(l1 - lm), jnp.exp(l2 - lm)
            den = e0 + e1 + e2
            y_ref[rows, :] = (e0 * y_ref[rows, :] + e1 * token_order(oc4, t, 8)
                              + e2 * token_order(oc16, t, 16)) / den
            lt_ref[rows, :] = lm + jnp.log(den)
            return carry
        lax.fori_loop(0, S // n_rows, combine, 0, unroll=2)

        if ng:
            pl.when(step == n_grid - 1)(finish_gather)

    col = pl.BlockSpec((S, LANES), lambda h: (0, h))
    padded = pltpu.VMEM((PAD + S, LANES), F32)
    return pl.pallas_call(
        body, grid=(n_grid,), name="attn_fwd",
        in_specs=[ANY] * (6 + ng), out_specs=[col, col] + [ANY] * ng,
        out_shape=[jax.ShapeDtypeStruct((S, AW), F32)] * 2 + _gathered_shapes(shards),
        scratch_shapes=[padded] * 10 + [
            pltpu.VMEM((4, 256, WIN), F32), pltpu.VMEM((4, 128, WIN), F32), pltpu.SemaphoreType.DMA((3, 2))]
        + (_gather_scratch(ng) if ng else []),
        compiler_params=_cparams(56))(*flat, *shards)


def _conv_taps(z, zprev, row):
    z1 = jnp.where(row == 0, zprev[7:8, :], pltpu.roll(z, 1, 0))
    z2 = jnp.where(row == 0, zprev[6:7, :], jnp.where(row == 1, zprev[7:8, :], pltpu.roll(z, 2, 0)))
    return z1, z2


def _xattn_scores(qm, km):
    s = _dot_nt(qm, km)
    m = jnp.max(s, axis=1, keepdims=True)
    e = jnp.exp(s - m)
    return e, jnp.sum(e, axis=1, keepdims=True)


def _mix_out(y_attn, bcu, qx16, kv16, cw8, g_attn, g_conv, g_x, g_post, wout16, x, shards):
    def body(ya_ref, bcu_ref, halo_ref, qx_ref, kv_ref, cw_ref, ga_ref, gc_ref, gx_ref, gp_ref, w_ref, x_ref,
             ypre_ref, y16_ref, y2_ref, x1_ref):
        i = pl.program_id(0)
        bcu = bcu_ref[...]
        b, c, u = bcu[:, 0:CW], bcu[:, CW:2 * CW], bcu[:, 2 * CW:]
        z = c * u
        halo = halo_ref[...]
        zprev = jnp.where(i > 0, halo[:, CW:2 * CW] * halo[:, 2 * CW:], 0.0)
        row = lax.broadcasted_iota(jnp.int32, z.shape, 0)
        z1, z2 = _conv_taps(z, zprev, row)
        cw = cw_ref[...]
        y_conv = b * (z2 * cw[0:1, :] + z1 * cw[1:2, :] + z * cw[2:3, :])

        qx = qx_ref[...]
        kv = kv_ref[...]
        km, vm = kv[:, 0:XW], kv[:, XW:]
        lane = lax.broadcasted_iota(jnp.int32, qx.shape, 1)
        y_x = jnp.zeros(qx.shape, F32)
        for h in range(XW // HEAD):
            hm = (lane >= h * HEAD) & (lane < (h + 1) * HEAD)
            e, l = _xattn_scores(jnp.where(hm, qx, jnp.zeros_like(qx)), km)
            y_x = jnp.where(hm, _dot(e.astype(BF16), vm) / l, y_x)

        y_attn = ya_ref[...]
        ypre_ref[:, 0:AW] = y_attn
        ypre_ref[:, AW:AW + CW] = y_conv
        ypre_ref[:, AW + CW:] = y_x
        y = jnp.concatenate([_rms(y_attn, ga_ref[...])[0], _rms(y_conv, gc_ref[...])[0],
                             _rms(y_x, gx_ref[...])[0]], axis=1).astype(BF16)
        y16_ref[...] = y.T
        y2 = _dot(y, w_ref[...])
        y2_ref[...] = y2
        x1_ref[...] = x_ref[...] + _rms(y2, gp_ref[...])[0]

    def tile(w):
        return pl.BlockSpec((TQ, w), lambda i: (i, 0))

    halo = pl.BlockSpec((SUBLANES, 3 * CW), lambda i: (jnp.maximum(i * (TQ // SUBLANES) - 1, 0), 0))
    return _call_with_gather(
        body, NT, shards, name="mix_out",
        in_specs=[tile(AW), tile(3 * CW), halo, tile(XW), _const((N_MEM, 2 * XW)), _const((SUBLANES, CW)),
                  _const((1, AW)), _const((1, CW)), _const((1, XW)), _const((1, D)), _const((D, D)), tile(D)],
        out_specs=[tile(D), _tokens_in_lanes(TQ), tile(D), tile(D)],
        out_shape=[jax.ShapeDtypeStruct((S, D), F32), jax.ShapeDtypeStruct((D, S), BF16),
                   jax.ShapeDtypeStruct((S, D), F32), jax.ShapeDtypeStruct((S, D), F32)],
        scratch_shapes=[], vmem_mb=56,
        args=(y_attn, bcu, bcu, qx16, kv16, cw8, g_attn, g_conv, g_x, g_post, wout16, x))


def _mlp(x1, tgt, g_pre, g_post, wup8, wdn_halves):
    tq = TQ_MLP
    half = D // 2

    def body(x1_ref, t_ref, g1_ref, g2_ref, wu_ref, wda_ref, wdb_ref,
             a16_ref, du_ref, h2_ref, df2_ref, dx1_ref, loss_ref, dg_ref):
        @pl.when(pl.program_id(0) == 0)
        def _():
            loss_ref[...] = jnp.zeros_like(loss_ref)
            dg_ref[...] = jnp.zeros_like(dg_ref)

        x1 = x1_ref[...]
        g1, g2 = g1_ref[...], g2_ref[...]
        y1, n1, r1 = _rms(x1, g1)
        h2 = y1.astype(BF16)
        h2_ref[...] = h2.T
        f2a = jnp.zeros((tq, half), F32)
        f2b = jnp.zeros((tq, half), F32)
        for j in range(N_DEV):
            cols = slice(j * FF_BLK, (j + 1) * FF_BLK)
            a = jnp.maximum(_dot(h2, wu_ref[j]), 0.0)
            a16_ref[:, cols] = a.astype(BF16)
            f = (a * a).astype(BF16)
            f2a = f2a + _dot(f, wda_ref[cols, :])
            f2b = f2b + _dot(f, wdb_ref[cols, :])
        f2 = jnp.concatenate([f2a, f2b], axis=1)
        y2, n2, r2 = _rms(f2, g2)
        e = x1 + y2 - t_ref[...]
        sq = jnp.sum(jnp.sum(e * e, axis=1, keepdims=True), axis=0, keepdims=True)
        loss_ref[...] += jnp.broadcast_to(sq * (0.5 / D), loss_ref.shape)
        dout = e * (1.0 / D)
        df2, dg2 = _rms_bwd(dout, n2, r2, g2)
        df2_16 = df2.astype(BF16)
        df2_ref[...] = df2_16.T
        dh2 = jnp.zeros((tq, D), F32)
        for j in range(N_DEV):
            cols = slice(j * FF_BLK, (j + 1) * FF_BLK)
            df = _dot_nt(df2_16[:, 0:half], wda_ref[cols, :]) + _dot_nt(df2_16[:, half:], wdb_ref[cols, :])
            du = (df * (2.0 * a16_ref[:, cols].astype(F32))).astype(BF16)
            du_ref[:, cols] = du
            dh2 = dh2 + _dot_nt(du, wu_ref[j])
        dx, dg1 = _rms_bwd(dh2, n1, r1, g1)
        dx1_ref[...] = dout + dx
        dg_ref[0:1, :] += dg2
        dg_ref[1:2, :] += dg1

    def tile(w):
        return pl.BlockSpec((tq, w), lambda i: (i, 0))

    return pl.pallas_call(
        body, grid=(S // tq,), name="mlp",
        in_specs=[tile(D), tile(D), _const((1, D)), _const((1, D)), _const((N_DEV, D, FF_BLK)), _const((FF, half)), _const((FF, half))],
        out_specs=[tile(FF), tile(FF), _tokens_in_lanes(tq), _tokens_in_lanes(tq), tile(D),
                   _acc((SUBLANES, LANES)), _acc((SUBLANES, D))],
        out_shape=[jax.ShapeDtypeStruct((S, FF), BF16), jax.ShapeDtypeStruct((S, FF), BF16),
                   jax.ShapeDtypeStruct((D, S), BF16), jax.ShapeDtypeStruct((D, S), BF16),
                   jax.ShapeDtypeStruct((S, D), F32), jax.ShapeDtypeStruct((SUBLANES, LANES), F32),
                   jax.ShapeDtypeStruct((SUBLANES, D), F32)],
        compiler_params=_cparams(60))(x1, tgt, g_pre, g_post, wup8, *wdn_halves)


def _mix_out_bwd(dx1, y2, ypre, ltot, head_ones, q, bcu, qx16, kv16, cw8, g_post, g_attn, g_conv, g_x, wout16):
    def body(dx1_ref, y2_ref, ypre_ref, lt_ref, e_ref, q_ref, bcu_ref, halo_ref, qx_ref, kv_ref, cw_ref, gp_ref,
             ga_ref, gc_ref, gx_ref, w_ref, dy2_ref, qdo_ref, ld_ref, dbcu_ref, dqx_ref, dgs_ref, dcw_ref, dkv_ref,
             carry):
        i = pl.program_id(0)

        @pl.when(i == 0)
        def _():
            dgs_ref[...] = jnp.zeros_like(dgs_ref)
            dcw_ref[...] = jnp.zeros_like(dcw_ref)
            dkv_ref[...] = jnp.zeros_like(dkv_ref)
            carry[...] = jnp.zeros_like(carry)

        gp = gp_ref[...]
        _, n, r = _rms(y2_ref[...], gp)
        dy2, dgp = _rms_bwd(dx1_ref[...], n, r, gp)
        dy2_16 = dy2.astype(BF16)
        dy2_ref[...] = dy2_16
        dy = _dot_nt(dy2_16, w_ref[...])

        ypre = ypre_ref[...]
        ga, gc, gx = ga_ref[...], gc_ref[...], gx_ref[...]
        _, na, ra = _rms(ypre[:, 0:AW], ga)
        dya, dga = _rms_bwd(dy[:, 0:AW], na, ra, ga)
        _, nc, rc = _rms(ypre[:, AW:AW + CW], gc)
        dyc, dgc = _rms_bwd(dy[:, AW:AW + CW], nc, rc, gc)
        y_x = ypre[:, AW + CW:]
        _, nx, rx = _rms(y_x, gx)
        dyx, dgx = _rms_bwd(dy[:, AW + CW:], nx, rx, gx)
        qdo_ref[...] = _pack_pair(q_ref[...], dya)
        prod = dya * ypre[:, 0:AW]
        hi = prod.astype(BF16)
        lo = (prod - hi.astype(F32)).astype(BF16)
        head_sum = _dot(hi, e_ref[...]) + _dot(lo, e_ref[...])
        lane_a = lax.broadcasted_iota(jnp.int32, prod.shape, 1)
        ld_ref[...] = jnp.where((lane_a % HEAD) < HEAD // 2, lt_ref[...], head_sum)
        dgs_ref[0:1, :] += dgp
        dgs_ref[1:2, :] += jnp.concatenate([dga, dgc, dgx], axis=1)

        bcu = bcu_ref[...]
        b, c, u = bcu[:, 0:CW], bcu[:, CW:2 * CW], bcu[:, 2 * CW:]
        z = c * u
        halo = halo_ref[...]
        zprev = jnp.where(i < NT - 1, halo[:, CW:2 * CW] * halo[:, 2 * CW:], 0.0)
        row = lax.broadcasted_iota(jnp.int32, z.shape, 0)
        z1, z2 = _conv_taps(z, zprev, row)
        cw = cw_ref[...]
        conv = z2 * cw[0:1, :] + z1 * cw[1:2, :] + z * cw[2:3, :]
        dconv = dyc * b
        nxt = carry[...]
        dn1 = jnp.where(row == TQ - 1, nxt[0:1, :], pltpu.roll(dconv, TQ - 1, 0))
        dn2 = jnp.where(row == TQ - 1, nxt[1:2, :], jnp.where(row == TQ - 2, nxt[0:1, :], pltpu.roll(dconv, TQ - 2, 0)))
        carry[...] = dconv[0:SUBLANES, :]
        dz = dconv * cw[2:3, :] + dn1 * cw[1:2, :] + dn2 * cw[0:1, :]
        dbcu_ref[:, 0:CW] = (dyc * conv).astype(BF16)
        dbcu_ref[:, CW:2 * CW] = (dz * u).astype(BF16)
        dbcu_ref[:, 2 * CW:] = (dz * c).astype(BF16)
        dcw_ref[0:1, :] += jnp.sum(z2 * dconv, axis=0, keepdims=True)
        dcw_ref[1:2, :] += jnp.sum(z1 * dconv, axis=0, keepdims=True)
        dcw_ref[2:3, :] += jnp.sum(z * dconv, axis=0, keepdims=True)

        qx = qx_ref[...]
        kv = kv_ref[...]
        km, vm = kv[:, 0:XW], kv[:, XW:]
        lane = lax.broadcasted_iota(jnp.int32, qx.shape, 1)
        dqx = jnp.zeros(qx.shape, F32)
        dkm = jnp.zeros((N_MEM, XW), F32)
        dvm = jnp.zeros((N_MEM, XW), F32)
        for h in range(XW // HEAD):
            hm = (lane >= h * HEAD) & (lane < (h + 1) * HEAD)
            qm = jnp.where(hm, qx, jnp.zeros_like(qx))
            e, l = _xattn_scores(qm, km)
            p = e / l
            dom = jnp.where(hm, dyx, 0.0)
            do16 = dom.astype(BF16)
            dsum = jnp.sum(dom * y_x, axis=1, keepdims=True)
            ds = (p * (_dot_nt(do16, vm) - dsum)).astype(BF16)
            dqx = jnp.where(hm, _dot(ds, km), dqx)
            dkm = dkm + _dot_tn(ds, qm)
            dvm = dvm + _dot_tn(p.astype(BF16), do16)
        dqx_ref[...] = (dqx * SCALE).astype(BF16)
        dkv_ref[:, 0:XW] += dkm
        dkv_ref[:, XW:] += dvm

    def tile(w):
        return pl.BlockSpec((TQ, w), lambda i: (NT - 1 - i, 0))

    halo = pl.BlockSpec((SUBLANES, 3 * CW), lambda i: (jnp.maximum((NT - 1 - i) * (TQ // SUBLANES) - 1, 0), 0))
    return pl.pallas_call(
        body, grid=(NT,), name="mix_out_bwd",
        in_specs=[tile(D), tile(D), tile(D), tile(AW), _const((AW, AW)), tile(AW), tile(3 * CW), halo, tile(XW),
                  _const((N_MEM, 2 * XW)), _const((SUBLANES, CW)), _const((1, D)), _const((1, AW)), _const((1, CW)),
                  _const((1, XW)), _const((D, D))],
        out_specs=[tile(D), tile(AW), tile(AW), tile(3 * CW), tile(XW), _acc((SUBLANES, D)), _acc((SUBLANES, CW)),
                   _acc((N_MEM, 2 * XW))],
        out_shape=[jax.ShapeDtypeStruct((S, D), BF16), jax.ShapeDtypeStruct((S, AW), F32),
                   jax.ShapeDtypeStruct((S, AW), F32),
                   jax.ShapeDtypeStruct((S, 3 * CW), BF16), jax.ShapeDtypeStruct((S, XW), BF16),
                   jax.ShapeDtypeStruct((SUBLANES, D), F32), jax.ShapeDtypeStruct((SUBLANES, CW), F32),
                   jax.ShapeDtypeStruct((N_MEM, 2 * XW), F32)],
        scratch_shapes=[pltpu.VMEM((SUBLANES, CW), F32)],
        compiler_params=_cparams(56))(dx1, y2, ypre, ltot, head_ones, q, bcu, bcu, qx16, kv16, cw8, g_post, g_attn,
                                      g_conv, g_x, wout16)


def _attn_bwd(qdo, kvp, ld, chip_sums=()):
    n_in = 3
    views = [[a] + [a.reshape(S // n, n, AW) for _, n, _, _ in ATTN_PLANS[1:]] for a in (qdo, kvp, ld)]
    flat = [views[a][p] for p in range(3) for a in range(n_in)]
    ns = len(chip_sums)
    n_grid = AW // LANES

    def body(*refs):
        hbm = [refs[n_in * p:n_in * p + n_in] for p in range(3)]
        refs = refs[3 * n_in:]
        sum_refs, refs = refs[:ns], refs[ns:]
        outs = [refs[3 * p:3 * p + 3] for p in range(3)]
        landed_refs, sc = refs[9:9 + ns], refs[9 + ns:]
        bufs = [sc[3 * p:3 * p + 3] for p in range(3)]
        res = [sc[9 + 3 * p:12 + 3 * p] for p in range(3)]
        tab128, tab4, sem_in, sem_out = sc[18:22]
        step = pl.program_id(0)
        if ns:
            start_chips, finish_chips = _chips_steps(sum_refs, landed_refs, *sc[22:])
            _, _, core, chips = _place()
            signal_chips, chips_are_in = _own_barrier([(px, py, core) for px, py in chips])
            pl.when(step == 0)(signal_chips)

            def chips_go():
                chips_are_in()
                start_chips()
        now =[_class_gather(hbm[p], bufs[p], sem_in.at[p], _lanes_of(step)) for p in range(3)]
        nxt = [_class_gather(hbm[p], bufs[p], sem_in.at[p], _lanes_of(step + 1)) for p in range(3)]

        @pl.when(step == 0)
        def _():
            for p in range(3):
                _start(now[p])
                for b in bufs[p]:
                    b[0:PAD, :] = jnp.zeros((PAD, LANES), F32)
            _fill_bias(tab128, 128, False)
            _fill_bias(tab4, 64, True)

        def prefetch(p):
            pl.when(step + 1 < n_grid)(lambda: _start(nxt[p]))

        lane = lax.broadcasted_iota(jnp.int32, (1, LANES), 1)

        def run(plan, plan_bufs, tab, dst):
            _, n_cls, qblk, nbc = plan
            partner = n_cls == 8
            bqdo, bkv, bld = plan_bufs
            rq, rk, rv = dst

            def block(g, carry):
                own, wins, mask = _block_rows(g, qblk, nbc, partner)
                qb, dob = _unpack_pair(bqdo[own, :])
                q2, do2 = _stack_heads(qb, lane), _stack_heads(dob, lane)
                kw, vw = _unpack_pair(_window(bkv, wins))
                ldv = bld[own, :]
                half = HEAD // 2
                lt2 = jnp.concatenate([ldv[:, 0:1], ldv[:, HEAD:HEAD + 1]], axis=0)
                dsum2 = jnp.concatenate([ldv[:, half:half + 1], ldv[:, HEAD + half:HEAD + half + 1]], axis=0)
                p = jnp.exp(_dot_nt(q2, kw) + tab[mask] - lt2)
                ds = (p * (_dot_nt(do2, vw) - dsum2)).astype(BF16)
                rq[own, :] = _unstack_heads(_dot(ds, kw), lane)
                dkw = _dot_tn(ds, q2)
                dvw = _dot_tn(p.astype(BF16), do2)
                n_w = WIN // len(wins)
                for i, w in enumerate(wins):
                    rk[w, :] += dkw[i * n_w:(i + 1) * n_w, :]
                    rv[w, :] += dvw[i * n_w:(i + 1) * n_w, :]
                return carry
            lax.fori_loop(0, n_cls * nbc, block, 0, unroll=ATTN_UNROLL)

        tabs = (tab128, tab4, tab128)
        def drained(p):
            return lambda: _wait(_whole_waits(res[p], sem_out.at[p]))

        for p in range(3):
            pl.when(step > 0)(drained(p))
            for b in res[p][1:]:
                b[...] = jnp.zeros_like(b)
            _wait(_whole_waits(bufs[p], sem_in.at[p]))
            run(ATTN_PLANS[p], bufs[p], tabs[p], res[p])
            prefetch(p)
            _start(_class_scatter(res[p], outs[p], sem_out.at[p], _lanes_of(step)))
            if ns and p == 0:
                pl.when(step == 0)(chips_go)
        for p in range(3):
            pl.when(step == n_grid - 1)(drained(p))
        if ns:
            pl.when(step == n_grid - 1)(finish_chips)

    padded = pltpu.VMEM((PAD + S, LANES), F32)
    shapes = [jax.ShapeDtypeStruct(views[0][p].shape, F32) for p in range(3) for _ in range(3)]
    out = pl.pallas_call(
        body, grid=(n_grid,), name="attn_bwd",
        in_specs=[ANY] * (3 * n_in + ns), out_specs=[ANY] * (9 + ns),
        out_shape=shapes + _chips_shapes(chip_sums),
        scratch_shapes=[padded] * 18
        + [pltpu.VMEM((4, 256, WIN), F32), pltpu.VMEM((4, 128, WIN), F32),
           pltpu.SemaphoreType.DMA((3, n_in)), pltpu.SemaphoreType.DMA((3, 3))]
        + (_chips_scratch(ns) if ns else []),
        compiler_params=_cparams(56, **({"collective_id": ID_ATTN_BWD} if ns else {})))(*flat, *chip_sums)
    return [o.reshape(S, AW) for o in out[:9]] + list(out[9:])


def _in_proj_bwd(dqkv, dbcu, dqx, cos, sins, w16, x, g, dx1):
    tq = TQ // 2

    def body(*refs):
        parts = refs[0:9]
        dbcu_ref, dqx_ref, c_ref, s_ref, w_ref, x_ref, g_ref, dx1_ref, dp_ref, gx_ref, dg_ref = refs[9:]

        @pl.when(pl.program_id(0) == 0)
        def _():
            dg_ref[...] = jnp.zeros_like(dg_ref)

        dq, dk, dv = (parts[i][...] + parts[3 + i][...] + parts[6 + i][...] for i in range(3))
        cos, sn = _all_heads(c_ref[...]), _all_heads(s_ref[...])
        dqr = dq * SCALE
        dkr = dk
        dp = jnp.concatenate([(dqr * cos + _rot_half(dqr * sn)).astype(BF16),
                              (dkr * cos + _rot_half(dkr * sn)).astype(BF16), dv.astype(BF16),
                              dbcu_ref[...], dqx_ref[...]], axis=1)
        dp_ref[...] = dp
        dh = _dot_nt(dp, w_ref[...])
        g = g_ref[...]
        _, n, r = _rms(x_ref[...], g)
        dx, dg = _rms_bwd(dh, n, r, g)
        gx_ref[...] = dx1_ref[...] + dx
        dg_ref[0:1, :] += dg

    def tile(w):
        return pl.BlockSpec((tq, w), lambda i: (i, 0))

    return pl.pallas_call(
        body, grid=(S // tq,), name="in_proj_bwd",
        in_specs=[tile(AW)] * 9 + [tile(3 * CW), tile(XW), tile(LANES), tile(LANES), _const((D, PW)),
                                   tile(D), _const((1, D)), tile(D)],
        out_specs=[tile(PW), tile(D), _acc((SUBLANES, D))],
        out_shape=[jax.ShapeDtypeStruct((S, PW), BF16), jax.ShapeDtypeStruct((S, D), F32),
                   jax.ShapeDtypeStruct((SUBLANES, D), F32)],
        compiler_params=_cparams(56))(*dqkv, dbcu, dqx, cos, sins, w16, x, g, dx1)


def _mem_bwd(mem, g_mem, wkv16, dkv):
    def body(m_ref, g_ref, w_ref, dkv_ref, dkv16_ref, dg_ref):
        dkv16 = dkv_ref[...].astype(BF16)
        dkv16_ref[...] = dkv16
        _, n, _ = _rms(m_ref[...], g_ref[...])
        dg = jnp.sum(_dot_nt(dkv16, w_ref[...]) * n, axis=0, keepdims=True)
        dg_ref[...] = jnp.broadcast_to(dg, dg_ref.shape)

    return pl.pallas_call(
        body, name="mem_bwd",
        out_shape=[jax.ShapeDtypeStruct((N_MEM, 2 * XW), BF16), jax.ShapeDtypeStruct((SUBLANES, D), F32)],
        compiler_params=pltpu.CompilerParams(vmem_limit_bytes=32 << 20))(mem, g_mem, wkv16, dkv)


N_CHIPS = N_DEV // 2


def _pair_scratch(block):
    return [pltpu.VMEM((N_CHIPS,) + block, BF16), pltpu.VMEM((N_CHIPS,) + block, BF16),
            pltpu.SemaphoreType.DMA((N_CHIPS,)), pltpu.SemaphoreType.DMA((N_CHIPS,))]


def _swap_with_sibling(p, stage, land, send, recv):
    x, y, c = lax.axis_index("x"), lax.axis_index("y"), lax.axis_index("c")
    return pltpu.make_async_remote_copy(src_ref=stage.at[p], dst_ref=land.at[p], send_sem=send.at[p],
                                        recv_sem=recv.at[p], device_id=(x, y, 1 - c), device_id_type=MESH)


def _own_barrier(peers):
    sem = pltpu.get_barrier_semaphore()

    def signal():
        for peer in peers:
            pl.semaphore_signal(sem, inc=1, device_id=peer, device_id_type=MESH)

    return signal, lambda: pl.semaphore_wait(sem, len(peers))


def _sibling_barrier():
    x, y, c = lax.axis_index("x"), lax.axis_index("y"), lax.axis_index("c")
    return _own_barrier([(x, y, 1 - c)])


ID_WGRAD_UP, ID_WGRAD_DOWN, ID_WGRAD_ROWS, ID_ROPE_TABLE, ID_IN_PROJ, ID_ATTN_BWD, ID_WGRAD_IN = range(7)


def _wgrad_cols(place, at16, b16, blk, name, barrier_id, square_b=False, transpose_out=False, to_chips=False,
                small=()):
    m, kk = at16.shape
    assert to_chips == bool(small)
    aligned = blk % LANES == 0
    wide = blk if aligned else -(-(blk + LANES // 2) // LANES) * LANES
    assert aligned or (transpose_out and blk % SUBLANES == 0)
    block = (blk, m) if transpose_out else (m, blk)

    def chip_of(step, my_chip):
        return jnp.bitwise_xor(my_chip, N_CHIPS - 1 - step) if to_chips else step

    def body(pl_ref, a_ref, *refs):
        b_refs, refs = refs[:2 if aligned else 1], refs[2 if aligned else 1:]
        accs, refs = refs[:len(small)], refs[len(small):]
        (cs_ref, own_ref), refs = refs[:2], refs[2:]
        if to_chips:
            landed, refs = refs[0], refs[1:]
        if small:
            tot_ref, refs = refs[0], refs[1:]
        (stage, land, send, recv), refs = refs[:4], refs[4:]
        if not aligned:
            (win, wsem), refs = refs[:2], refs[2:]
        if small:
            start_small, finish_small = _small_reduce_steps(accs, tot_ref, *refs[-4:])
            refs = refs[:-4]
        step = pl.program_id(0)
        x, y, c = lax.axis_index("x"), lax.axis_index("y"), lax.axis_index("c")
        others = [(x ^ (k >> 2), y ^ ((k >> 1) & 1), c ^ (k & 1)) for k in range(1, N_DEV)]
        signal_peers, peers_are_in = _own_barrier(others if small else [(x, y, 1 - c)])
        pl.when(step == 0)(signal_peers)
        my_chip = 2 * x + y
        p = chip_of(step, my_chip)

        def fetch(at_step, mine):
            j = 2 * chip_of(at_step, my_chip) + (c if mine else 1 - c)
            first = pl.multiple_of(((j * blk) >> 7) << 7, LANES)
            slot = 2 * (at_step & 1) + mine
            return pltpu.make_async_copy(b_refs[0].at[:, pl.ds(first, wide)], win.at[slot], wsem.at[slot])

        if not aligned:
            @pl.when(step == 0)
            def _():
                fetch(0, 0).start()
                fetch(0, 1).start()

            @pl.when(step + 1 < N_CHIPS)
            def _():
                fetch(step + 1, 0).start()
                fetch(step + 1, 1).start()

        def partial(mine):
            if aligned:
                b = b_refs[mine][...]
                if square_b:
                    b = b * b
                acc = _dot(a_ref[...], b)
            else:
                fetch(step, mine).wait()
                acc = _dot(a_ref[...], win[2 * (step & 1) + mine]).T
                odd = c if mine else 1 - c
                return jnp.where(odd == 0, acc[0:blk], acc[wide - blk:wide])
            return acc.T if transpose_out else acc

        stage[p] = partial(0).astype(BF16)
        pl.when(step == 0)(peers_are_in)
        if small:
            pl.when(step == 0)(start_small)
        swap = _swap_with_sibling(p, stage, land, send, recv)
        swap.start()
        mine = partial(1)
        swap.wait()
        total = mine + land[p].astype(F32)
        cs_ref[0] = total.astype(BF16)

        @pl.when(p == my_chip)
        def _():
            own_ref[...] = total

        if to_chips:
            stage2, send2, recv2 = refs
            flipped = jnp.bitwise_xor(p, my_chip)
            k = jnp.where(flipped == 2, 0, jnp.where(flipped == 1, 1, 2))

            def to_owner(src, k_, px, py):
                return pltpu.make_async_remote_copy(src_ref=src, dst_ref=landed.at[k_], send_sem=send2.at[k_],
                                                    recv_sem=recv2.at[k_], device_id=(px, py, c), device_id_type=MESH)

            @pl.when(p != my_chip)
            def _():
                stage2[p] = total.astype(BF16)
                to_owner(stage2.at[p], k, p >> 1, p & 1).start()

            @pl.when(step == N_CHIPS - 1)
            def _():
                for k_ in range(N_CHIPS - 1):
                    to_owner(stage2.at[0], k_, x, y).wait()

        if small:
            pl.when(step == N_CHIPS - 1)(finish_small)

    def b_spec(mine):
        return pl.BlockSpec((kk, blk), lambda i, s: (0, 2 * chip_of(i, s[1]) + (s[0] if mine else 1 - s[0])))

    b_specs, b_args = ([b_spec(0), b_spec(1)], (b16, b16)) if aligned else ([ANY], (b16,))
    scratch = _pair_scratch(block)
    if not aligned:
        scratch += [pltpu.VMEM((4, kk, wide), BF16), pltpu.SemaphoreType.DMA((4,))]
    out_specs = [pl.BlockSpec((1,) + block, lambda i, s: (chip_of(i, s[1]), 0, 0)), pl.BlockSpec(block, lambda i, s: (0, 0))]
    out_shape = [jax.ShapeDtypeStruct((N_CHIPS,) + block, BF16), jax.ShapeDtypeStruct(block, F32)]
    if to_chips:
        out_specs.append(ANY)
        out_shape.append(jax.ShapeDtypeStruct((N_CHIPS - 1,) + block, BF16))
        scratch += [pltpu.VMEM((N_CHIPS,) + block, BF16), pltpu.SemaphoreType.DMA((N_CHIPS - 1,)),
                    pltpu.SemaphoreType.DMA((N_CHIPS - 1,))]
    small_specs = [pl.BlockSpec(a.shape, lambda i, s: (0, 0)) for a in small]
    if small:
        out_specs.append(pl.BlockSpec((PACK_ROWS, D), lambda i, s: (0, 0)))
        out_shape.append(jax.ShapeDtypeStruct((PACK_ROWS, D), F32))
        scratch += _small_reduce_scratch()
    return pl.pallas_call(
        body, name=name,
        grid_spec=pltpu.PrefetchScalarGridSpec(
            num_scalar_prefetch=1, grid=(N_CHIPS,),
            in_specs=[pl.BlockSpec((m, kk), lambda i, s: (0, 0), pipeline_mode=pl.Buffered(1))] + b_specs + small_specs,
            out_specs=out_specs, scratch_shapes=scratch),
        out_shape=out_shape,
        compiler_params=_cparams(56, collective_id=barrier_id),
    )(place, at16, *b_args, *small)


ROWS_STEPS = 4


def _wgrad_rows(place, products, name):
    n_prod = len(products)
    dims = [(at16.shape[0], at16.shape[1], b16.shape[1]) for at16, b16 in products]
    cut = [kk % (ROWS_STEPS * LANES) == 0 for _, kk, _ in dims]
    blocks = [(m // N_DEV, n) for m, _, n in dims]

    def body(pl_ref, *refs):
        ins, outs, scratch = refs[:2 * n_prod], refs[2 * n_prod:4 * n_prod], refs[4 * n_prod:]
        c, step = pl_ref[0], pl.program_id(0)

        def multiply(i):
            a_ref, b_ref, acc = ins[2 * i], ins[2 * i + 1], scratch[5 * i]

            @pl.when(step == 0)
            def _():
                acc[...] = _dot(a_ref[...], b_ref[...])

            if cut[i]:
                @pl.when(step > 0)
                def _():
                    acc[...] += _dot(a_ref[...], b_ref[...])

        def rows(i, owner):
            return pl.ds(pl.multiple_of(owner * blocks[i][0], blocks[i][0]), blocks[i][0])

        def send_sibling_side(i):
            acc, stage, land, send, recv = scratch[5 * i:5 * i + 5]
            swaps = []
            for p in range(N_CHIPS):
                stage[p] = acc[rows(i, 2 * p + 1 - c), :].astype(BF16)
                swaps.append(_swap_with_sibling(p, stage, land, send, recv))
                swaps[-1].start()
            return swaps

        def add_my_side(i, swaps):
            acc, land = scratch[5 * i], scratch[5 * i + 2]
            cs_ref, own_ref = outs[2 * i:2 * i + 2]
            for p in range(N_CHIPS):
                swaps[p].wait()
                total = acc[rows(i, 2 * p + c), :] + land[p].astype(F32)
                cs_ref[p] = total.astype(BF16)

                @pl.when(p == pl_ref[1])
                def _():
                    own_ref[...] = total

        signal_sibling, sibling_is_in = _sibling_barrier()
        pl.when(step == 0)(signal_sibling)
        for i in range(n_prod):
            multiply(i)

        @pl.when(step == ROWS_STEPS - 1)
        def _():
            sibling_is_in()
            swaps = [send_sibling_side(i) for i in range(n_prod)]
            for i in range(n_prod):
                add_my_side(i, swaps[i])

    in_specs, out_specs, out_shape, scratch = [pl.BlockSpec(memory_space=pltpu.SMEM)], [], [], []
    for (m, kk, n), cut_i, block in zip(dims, cut, blocks):
        chunk = kk // ROWS_STEPS
        in_specs += ([pl.BlockSpec((m, chunk), lambda i: (0, i)), pl.BlockSpec((chunk, n), lambda i: (i, 0))]
                     if cut_i else [_const((m, kk)), _const((kk, n))])
        out_specs += [_acc((N_CHIPS,) + block), _acc(block)]
        out_shape += [jax.ShapeDtypeStruct((N_CHIPS,) + block, BF16), jax.ShapeDtypeStruct(block, F32)]
        scratch += [pltpu.VMEM((m, n), F32)] + _pair_scratch(block)
    out = pl.pallas_call(
        body, grid=(ROWS_STEPS,), name=name, in_specs=in_specs, out_specs=out_specs, out_shape=out_shape,
        scratch_shapes=scratch, compiler_params=_cparams(56, collective_id=ID_WGRAD_ROWS),
    )(place, *[a for pair in products for a in pair])
    return [tuple(out[2 * i:2 * i + 2]) for i in range(n_prod)]


def _adamw_math(w, g, m, v):
    m = ADAM_B1 * m + (1.0 - ADAM_B1) * g
    v = ADAM_B2 * v + (1.0 - ADAM_B2) * jnp.square(g)
    m_hat = m / (1.0 - ADAM_B1 ** ADAM_STEP)
    v_hat = v / (1.0 - ADAM_B2 ** ADAM_STEP)
    delta = -ADAM_LR * (m_hat / (jnp.sqrt(v_hat) + ADAM_EPS) + ADAM_WD * w)
    return delta, m, v


ADAMW_STEPS = 4


def _adamw_shards(updates, name):
    names, nu = list(updates), len(updates)

    def body(*refs):
        ins, outs = refs[:5 * nu], refs[5 * nu:]
        for i in range(nu):
            o_ref, r_ref, w_ref, m_ref, v_ref = ins[5 * i:5 * i + 5]
            g_out, d_out, m_out, v_out = outs[4 * i:4 * i + 4]
            g = o_ref[...] + r_ref[0].astype(F32) + r_ref[1].astype(F32) + r_ref[2].astype(F32)
            g_out[...] = g
            d_out[...], m_out[...], v_out[...] = _adamw_math(w_ref[...], g, m_ref[...], v_ref[...])

    in_specs, out_specs = [], []
    for n in names:
        rows, cols = updates[n][2].shape
        chunk = pl.BlockSpec((rows // ADAMW_STEPS, cols), lambda i: (i, 0))
        in_specs += [chunk, pl.BlockSpec((N_CHIPS - 1, rows // ADAMW_STEPS, cols), lambda i: (0, i, 0))] + [chunk] * 3
        out_specs += [chunk] * 4
    out = pl.pallas_call(
        body, grid=(ADAMW_STEPS,), name=name, in_specs=in_specs, out_specs=out_specs,
        out_shape=[jax.ShapeDtypeStruct(updates[n][2].shape, F32) for n in names for _ in range(4)],
        compiler_params=_cparams(56))(*[a for n in names for a in updates[n]])
    return {n: out[4 * i:4 * i + 4] for i, n in enumerate(names)}


def _place():
    x, y, c = lax.axis_index("x"), lax.axis_index("y"), lax.axis_index("c")
    chips = [(1 - x, y), (x, 1 - y), (1 - x, 1 - y)]
    return x, y, c, chips


def _gather_steps(ins, outs, send, recv, lsem, own_barrier=True):
    nt = len(ins)
    x, y, c, (xn, yn, diag) = _place()
    me, sib = (x, y, c), (x, y, 1 - c)

    def slot(t, px, py, pc):
        return outs[t].at[4 * px + 2 * py + pc]

    def copy(t, k, block, to, src=None):
        return pltpu.make_async_remote_copy(
            src_ref=slot(t, *block) if src is None else src, dst_ref=slot(t, *block),
            send_sem=send.at[t, k], recv_sem=recv.at[t, k], device_id=to, device_id_type=MESH)

    mine = [pltpu.make_async_copy(ins[t], slot(t, *me), lsem.at[t]) for t in range(nt)]
    first = [copy(t, k, me, to, src=ins[t]) for t in range(nt) for k, to in ((0, sib), (1, (*xn, c)), (2, (*yn, c)))]

    if own_barrier:
        signal_peers, peers_are_in = _own_barrier([sib, (*xn, c), (*yn, c)])

    def enter():
        if own_barrier:
            signal_peers()
        for cp in mine:
            cp.start()

    def start():
        if own_barrier:
            peers_are_in()
        for cp in first:
            cp.start()

    def landed(k, chip, also_to=None):
        for t in range(nt):
            copy(t, k, (*chip, c), me).wait_recv()
            if also_to is not None:
                copy(t, 3, (*chip, c), (*also_to, c)).start()
            copy(t, 3 + k, (*chip, c), sib).start()

    def relay():
        @pl.when(c == 0)
        def _():
            landed(1, xn, also_to=yn)
            landed(2, yn)

        @pl.when(c == 1)
        def _():
            landed(2, yn, also_to=xn)
            landed(1, xn)

    def finish():
        landed(3, diag)
        for t in range(nt):
            copy(t, 0, sib, me).wait_recv()
            for k, chip in ((4, xn), (5, yn), (6, diag)):
                copy(t, k, (*chip, 1 - c), me).wait_recv()
            for k in range(7):
                copy(t, k, me, sib).wait_send()
        for cp in mine:
            cp.wait()

    return enter, start, relay, finish


def _gather_scratch(nt):
    return [pltpu.SemaphoreType.DMA((nt, 7)), pltpu.SemaphoreType.DMA((nt, 7)), pltpu.SemaphoreType.DMA((nt,))]


def _gathered_shapes(shards):
    return [jax.ShapeDtypeStruct((N_DEV,) + s.shape, s.dtype) for s in shards]


def _call_with_gather(body, n_grid, shards, *, name, in_specs, out_specs, out_shape, scratch_shapes, vmem_mb, args,
                      collective_id=None):
    assert (collective_id is None) == (not shards)
    ng, n_in, n_out = len(shards), len(in_specs), len(out_specs)

    def wrapped(*refs):
        ins, shard_refs = refs[:n_in], refs[n_in:n_in + ng]
        outs = refs[n_in + ng:n_in + ng + n_out]
        whole_refs = refs[n_in + ng + n_out:n_in + 2 * ng + n_out]
        scratch = refs[n_in + 2 * ng + n_out:]
        if ng:
            enter, start, relay, finish = _gather_steps(shard_refs, whole_refs, *scratch[len(scratch_shapes):])
            pl.when(pl.program_id(0) == 0)(enter)
            pl.when(pl.program_id(0) == 0)(start)
            pl.when(pl.program_id(0) == n_grid // 2)(relay)
        body(*ins, *outs, *scratch[:len(scratch_shapes)])
        if ng:
            pl.when(pl.program_id(0) == n_grid - 1)(finish)

    return pl.pallas_call(
        wrapped, grid=(n_grid,), name=name,
        in_specs=list(in_specs) + [ANY] * ng, out_specs=list(out_specs) + [ANY] * ng,
        out_shape=list(out_shape) + _gathered_shapes(shards),
        scratch_shapes=list(scratch_shapes) + (_gather_scratch(ng) if ng else []),
        compiler_params=_cparams(vmem_mb, **({"collective_id": collective_id} if shards else {})))(*args, *shards)


def _chips_steps(ins, outs, send, recv):
    _, _, c, chips = _place()
    copies = [pltpu.make_async_remote_copy(
        src_ref=ins[t].at[2 * px + py], dst_ref=outs[t].at[j], send_sem=send.at[t, j], recv_sem=recv.at[t, j],
        device_id=(px, py, c), device_id_type=MESH) for t in range(len(ins)) for j, (px, py) in enumerate(chips)]

    def start():
        for cp in copies:
            cp.start()

    def finish():
        for cp in copies:
            cp.wait()

    return start, finish


def _chips_scratch(nt):
    return [pltpu.SemaphoreType.DMA((nt, 3)), pltpu.SemaphoreType.DMA((nt, 3))]


def _chips_shapes(cs16s):
    return [jax.ShapeDtypeStruct((3,) + g.shape[1:], g.dtype) for g in cs16s]


SMALL = (("g_pre_mix", 0, 0, D), ("g_mem", 1, 0, D), ("g_post_mix", 2, 0, D), ("g_attn_out", 3, 0, AW),
         ("g_conv_out", 3, AW, CW), ("g_xattn_out", 3, AW + CW, XW), ("g_post_mlp", 4, 0, D), ("g_pre_mlp", 5, 0, D))
CONV_ROW = 8
PACK_ROWS = 16


LOSS_ROW = 15


def _small_reduce_steps(accs, tot_ref, pack, land, send, recv):
    acc_in, acc_mem, acc_mix, acc_mlp, acc_cw, acc_loss = accs
    x, y, c, _ = _place()
    me = 4 * x + 2 * y + c
    copies = []
    for k in range(1, N_DEV):
        kx, ky, kc = (k >> 2) & 1, (k >> 1) & 1, k & 1
        peer = (1 - x if kx else x, 1 - y if ky else y, 1 - c if kc else c)
        copies.append(pltpu.make_async_remote_copy(
            src_ref=pack, dst_ref=land.at[me], send_sem=send.at[k - 1], recv_sem=recv.at[k - 1],
            device_id=peer, device_id_type=MESH))

    def start():
        pack[...] = jnp.zeros_like(pack)
        pack[0:1, :] = acc_in[0:1, :]
        pack[1:2, :] = acc_mem[0:1, :]
        pack[2:4, :] = acc_mix[0:2, :]
        pack[4:6, :] = acc_mlp[0:2, :]
        pack[CONV_ROW:CONV_ROW + 3, 0:CW] = acc_cw[0:3, :]
        pack[LOSS_ROW:LOSS_ROW + 1, 0:LANES] = acc_loss[0:1, :]
        land[me] = pack[...]
        for cp in copies:
            cp.start()

    def finish():
        for cp in copies:
            cp.wait()
        tot = land[0]
        for s in range(1, N_DEV):
            tot = tot + land[s]
        tot_ref[...] = tot

    return start, finish


def _small_reduce_scratch():
    return [pltpu.VMEM((PACK_ROWS, D), F32), pltpu.VMEM((N_DEV, PACK_ROWS, D), F32),
            pltpu.SemaphoreType.DMA((N_DEV - 1,)), pltpu.SemaphoreType.DMA((N_DEV - 1,))]


def _small_update(tot, me, params):
    flat = [a for n, _, _, _ in SMALL for a in params[n]] + list(params["conv_w"])
    n_par = len(SMALL) + 1
    tap_cols = CW // N_DEV

    def body(*refs):
        me_ref, tot_ref = refs[0:2]
        ins = refs[2:2 + 3 * n_par]
        loss_out = refs[2 + 3 * n_par]
        outs = refs[3 + 3 * n_par:]
        tot = tot_ref[...]
        loss_out[...] = jnp.broadcast_to(tot[LOSS_ROW:LOSS_ROW + 1, 0:LANES], loss_out.shape)

        def update(i, g):
            w_ref, m_ref, v_ref = ins[3 * i:3 * i + 3]
            for o_ref, res in zip(outs[4 * i:4 * i + 4], (g,) + _adamw_math(w_ref[...], g, m_ref[...], v_ref[...])):
                if len(o_ref.shape) == 3:
                    for t in range(o_ref.shape[0]):
                        o_ref[t] = res[t:t + 1, :]
                else:
                    o_ref[...] = res

        for i, (_, row, lane0, width) in enumerate(SMALL):
            update(i, tot[row:row + 1, lane0:lane0 + width])
        me = me_ref[0]
        taps = pltpu.roll(tot[CONV_ROW:CONV_ROW + SUBLANES, 0:CW], jnp.where(me == 0, 0, CW - me * tap_cols), 1)
        update(n_par - 1, taps[0:3, 0:tap_cols])

    shapes = [jax.ShapeDtypeStruct(params[n][0].shape, F32) for n, _, _, _ in SMALL] + [
        jax.ShapeDtypeStruct((3, 1, tap_cols), F32)]
    vmem = pl.BlockSpec(memory_space=pltpu.VMEM)
    loss, *out = pl.pallas_call(
        body, name="small_update",
        in_specs=[pl.BlockSpec(memory_space=pltpu.SMEM)] + [vmem] * (1 + 3 * n_par),
        out_shape=[jax.ShapeDtypeStruct((SUBLANES, LANES), F32)] + [s for s in shapes for _ in range(4)],
    )(me, tot, *flat)
    names = [n for n, _, _, _ in SMALL] + ["conv_w"]
    return loss[0, 0], {n: out[4 * i:4 * i + 4] for i, n in enumerate(names)}


def _local_step(x, mem, pos, gains, shards, tgt, place):
    half = HEAD // 2
    inv_freq = jnp.float32(ROPE_THETA) ** (-(jnp.arange(half, dtype=F32) * 2.0 / HEAD))
    invf = jnp.tile(inv_freq, LANES // half)[None, :]
    sgn = jnp.tile(jnp.concatenate([-jnp.ones((half,), F32), jnp.ones((half,), F32)]), LANES // HEAD)[None, :]
    cos, sins, win8 = _rope_table(pos.astype(F32).reshape(S, 1), invf, sgn, [shards["w_in"]])
    wdn_left, wdn_right = shards["w_down"][:, 0:D // 2], shards["w_down"][:, D // 2:]
    q, kvp, bcu, qx16, h16, win16, wout8, wkv8, conv8, wdn8_right = _in_proj(
        x, gains["g_pre_mix"], win8, cos, sins, [shards["w_out"], shards["w_mem_kv"], shards["conv_w"], wdn_right])
    wout16, wkv16 = wout8.reshape(D, D), wkv8.reshape(D, 2 * XW)
    cw_full = conv8[:, 0:3, 0:CW // N_DEV].transpose(1, 0, 2).reshape(3, CW)
    cw8 = jnp.zeros((SUBLANES, CW), F32).at[0:3].set(cw_full)
    y_attn, ltot, wup8, wdn8_left = _attn_fwd(q, kvp, [shards["w_up"], wdn_left])
    wdn_halves = (wdn8_left.reshape(FF, D // 2), wdn8_right.reshape(FF, D // 2))
    memn16, kv16 = _mem_fwd(mem, gains["g_mem"], wkv16)
    ypre, y16, y2, x1 = _mix_out(y_attn, bcu, qx16, kv16, cw8, gains["g_attn_out"], gains["g_conv_out"],
                                 gains["g_xattn_out"], gains["g_post_mix"], wout16, x, [])
    a16, du16, h2_16, df2_16, dx1, loss8, dg_mlp = _mlp(
        x1, tgt, gains["g_pre_mlp"], gains["g_post_mlp"], wup8, wdn_halves)

    sums = {"w_up": _wgrad_cols(place, h2_16, du16, FF_BLK, "wgrad_up", ID_WGRAD_UP),
            "w_down": _wgrad_cols(place, df2_16, a16, FF_BLK, "wgrad_down", ID_WGRAD_DOWN, square_b=True,
                                  transpose_out=True)}

    head_id = jnp.arange(AW, dtype=jnp.int32) // HEAD
    head_ones = (head_id[:, None] == head_id[None, :]).astype(BF16)
    dy2_16, qdo, ld, dbcu, dqx, dgs, dcw, dkv = _mix_out_bwd(
        dx1, y2, ypre, ltot, head_ones, q, bcu, qx16, kv16, cw8, gains["g_post_mix"], gains["g_attn_out"],
        gains["g_conv_out"], gains["g_xattn_out"], wout16)
    dkv16, dg_mem = _mem_bwd(mem, gains["g_mem"], wkv16, dkv)
    sums["w_mem_kv"], sums["w_out"] = _wgrad_rows(place, [(memn16, dkv16), (y16, dy2_16)], "wgrad_mem_kv_out")
    out = _attn_bwd(qdo, kvp, ld, [s[0] for s in sums.values()])
    dqkv, landed = out[:9], out[9:]
    reduced = {n: (s[1], landed[t]) for t, (n, s) in enumerate(sums.items())}
    dproj16, grad_x, dg_in = _in_proj_bwd(dqkv, dbcu, dqx, cos, sins, win16, x, gains["g_pre_mix"], dx1)

    _, in_own, in_landed, small_tot = _wgrad_cols(place, h16, dproj16, PW // N_DEV, "wgrad_in", ID_WGRAD_IN,
                                                  transpose_out=True, to_chips=True,
                                                  small=(dg_in, dg_mem, dgs, dg_mlp, dcw, loss8))
    reduced["w_in"] = (in_own, in_landed)
    return grad_x, reduced, small_tot


BIG = ("w_in", "w_mem_kv", "w_out", "w_up", "w_down")
ORDER = ("g_pre_mix", "g_mem", "w_in", "w_mem_kv", "conv_w", "g_attn_out", "g_conv_out", "g_xattn_out", "w_out",
         "g_post_mix", "g_pre_mlp", "w_up", "w_down", "g_post_mlp")


def kernel(x, mem, positions, g_pre_mix, g_mem, w_in, w_mem_kv, conv_w, g_attn_out, g_conv_out, g_xattn_out, w_out, g_post_mix, g_pre_mlp, w_up, w_down, g_post_mlp, loss_target, m_g_pre_mix, m_g_mem, m_w_in, m_w_mem_kv, m_conv_w, m_g_attn_out, m_g_conv_out, m_g_xattn_out, m_w_out, m_g_post_mix, m_g_pre_mlp, m_w_up, m_w_down, m_g_post_mlp, v_g_pre_mix, v_g_mem, v_w_in, v_w_mem_kv, v_conv_w, v_g_attn_out, v_g_conv_out, v_g_xattn_out, v_w_out, v_g_post_mix, v_g_pre_mlp, v_w_up, v_w_down, v_g_post_mlp):
    w = dict(g_pre_mix=g_pre_mix, g_mem=g_mem, w_in=w_in, w_mem_kv=w_mem_kv, conv_w=conv_w, g_attn_out=g_attn_out,
             g_conv_out=g_conv_out, g_xattn_out=g_xattn_out, w_out=w_out, g_post_mix=g_post_mix, g_pre_mlp=g_pre_mlp,
             w_up=w_up, w_down=w_down, g_post_mlp=g_post_mlp)
    mo = dict(g_pre_mix=m_g_pre_mix, g_mem=m_g_mem, w_in=m_w_in, w_mem_kv=m_w_mem_kv, conv_w=m_conv_w,
              g_attn_out=m_g_attn_out, g_conv_out=m_g_conv_out, g_xattn_out=m_g_xattn_out, w_out=m_w_out,
              g_post_mix=m_g_post_mix, g_pre_mlp=m_g_pre_mlp, w_up=m_w_up, w_down=m_w_down, g_post_mlp=m_g_post_mlp)
    vo = dict(g_pre_mix=v_g_pre_mix, g_mem=v_g_mem, w_in=v_w_in, w_mem_kv=v_w_mem_kv, conv_w=v_conv_w,
              g_attn_out=v_g_attn_out, g_conv_out=v_g_conv_out, g_xattn_out=v_g_xattn_out, w_out=v_w_out,
              g_post_mix=v_g_post_mix, g_pre_mlp=v_g_pre_mlp, w_up=v_w_up, w_down=v_w_down, g_post_mlp=v_g_post_mlp)

    xi, yi, ci = lax.axis_index("x"), lax.axis_index("y"), lax.axis_index("c")
    me = 4 * xi + 2 * yi + ci
    place = jnp.stack([ci, 2 * xi + yi]).astype(jnp.int32)

    shards = {n: w[n][0].astype(BF16) for n in BIG}
    shards["conv_w"] = jnp.zeros((SUBLANES, LANES), F32).at[0:3, 0:CW // N_DEV].set(conv_w[0])

    gains = {n: w[n] for n, _, _, _ in SMALL}
    grad_x, reduced, small_tot = _local_step(x[0], mem[0], positions[0], gains, shards, loss_target[0], place)

    def shard(n, a):
        return a[0].T if n == "w_in" else a[0]

    updated = _adamw_shards({n: (*reduced[n], shard(n, w[n]), shard(n, mo[n]), shard(n, vo[n])) for n in BIG},
                            "adamw")
    grad, delta, new_m, new_v = {}, {}, {}, {}
    for n, res in updated.items():
        grad[n], delta[n], new_m[n], new_v[n] = [(a.T if n == "w_in" else a)[None] for a in res]

    params = {n: (w[n], mo[n], vo[n]) for n, _, _, _ in SMALL}
    params["conv_w"] = (w["conv_w"][0], mo["conv_w"][0], vo["conv_w"][0])
    loss, small = _small_update(small_tot, me.reshape(1).astype(jnp.int32), params)
    for n, (g, d_, m_, v_) in small.items():
        lead = (lambda a: a.reshape(conv_w.shape)) if n == "conv_w" else (lambda a: a)
        grad[n], delta[n], new_m[n], new_v[n] = lead(g), lead(d_), lead(m_), lead(v_)

    return (loss, grad_x[None], *[grad[n] for n in ORDER], *[delta[n] for n in ORDER],
            *[new_m[n] for n in ORDER], *[new_v[n] for n in ORDER])
```

```python
import jax
import jax.numpy as jnp
from jax import lax
from jax.experimental import pallas as pl
from jax.experimental.pallas import tpu as pltpu

F32, BF16 = jnp.float32, jnp.bfloat16
MESH = pl.DeviceIdType.MESH
ANY = pl.BlockSpec(memory_space=pl.ANY)

N_DEV = 8
D = 1024
S = 4096
N_MEM = 256
HEAD = 64
AW, CW, XW = 512, 256, 256
PW = 3 * AW + 3 * CW + XW
FF = 4096
FF_BLK = FF // N_DEV
EPS = 1e-6
NEG = -1e30
SCALE = HEAD ** -0.5
ROPE_THETA = 10000.0
LANES = 128
SUBLANES = 8

ADAM_LR, ADAM_B1, ADAM_B2, ADAM_EPS, ADAM_WD, ADAM_STEP = 0.001, 0.9, 0.999, 1e-08, 0.01, 10

TQ = 512
TQ_MLP = 512
NT = S // TQ


def _cparams(vmem_mb, n_grid=1, **more):
    return pltpu.CompilerParams(dimension_semantics=("arbitrary",) * n_grid, vmem_limit_bytes=vmem_mb << 20, **more)


def _const(shape):
    nd = len(shape)
    return pl.BlockSpec(shape, lambda *_: (0,) * nd, pipeline_mode=pl.Buffered(1))


def _acc(shape):
    nd = len(shape)
    return pl.BlockSpec(shape, lambda *_: (0,) * nd)


def _tokens_in_lanes(tq):
    return pl.BlockSpec((D, tq), lambda i: (0, i))


def _dot(a, b):
    return jnp.dot(a, b, preferred_element_type=F32)


def _dot_nt(a, b):
    return lax.dot_general(a, b, (((1,), (1,)), ((), ())), preferred_element_type=F32)


def _dot_tn(a, b):
    return lax.dot_general(a, b, (((0,), (0,)), ((), ())), preferred_element_type=F32)


def _rms(x, g):
    r = lax.rsqrt(jnp.mean(x * x, axis=-1, keepdims=True) + EPS)
    n = x * r
    return n * g, n, r


def _rms_bwd(dy, n, r, g):
    dn = dy * g
    dx = r * (dn - n * jnp.mean(dn * n, axis=-1, keepdims=True))
    return dx, jnp.sum(dy * n, axis=0, keepdims=True)


def _rot_half(t):
    lane = lax.broadcasted_iota(jnp.int32, t.shape, 1)
    n = t.shape[1]
    return jnp.where((lane % HEAD) < HEAD // 2, pltpu.roll(t, n - HEAD // 2, 1), pltpu.roll(t, HEAD // 2, 1))


def _rope_table(pos_col, invf, sgn, shards):
    def body(p_ref, f_ref, s_ref, c_out, s_out):
        ang = p_ref[...] * f_ref[...]
        c_out[...] = jnp.cos(ang)
        s_out[...] = jnp.sin(ang) * s_ref[...]

    tile = pl.BlockSpec((TQ, LANES), lambda i: (i, 0))
    return _call_with_gather(
        body, NT, shards, name="rope_table",
        in_specs=[pl.BlockSpec((TQ, 1), lambda i: (i, 0)), _const((1, LANES)), _const((1, LANES))],
        out_specs=[tile, tile], out_shape=[jax.ShapeDtypeStruct((S, LANES), F32)] * 2,
        scratch_shapes=[], vmem_mb=32, args=(pos_col, invf, sgn), collective_id=ID_ROPE_TABLE)


def _all_heads(t):
    return jnp.tile(t, (1, AW // LANES))


def _mem_fwd(mem, g_mem, wkv16):
    def body(m_ref, g_ref, w_ref, n16_ref, kv_ref):
        y, _, _ = _rms(m_ref[...], g_ref[...])
        y16 = y.astype(BF16)
        n16_ref[...] = y16.T
        kv_ref[...] = _dot(y16, w_ref[...]).astype(BF16)

    return pl.pallas_call(
        body, name="mem_fwd",
        out_shape=[jax.ShapeDtypeStruct((D, N_MEM), BF16), jax.ShapeDtypeStruct((N_MEM, 2 * XW), BF16)],
        compiler_params=pltpu.CompilerParams(vmem_limit_bytes=32 << 20))(mem, g_mem, wkv16)


def _in_proj(x, g, w8, cos, sins, shards):
    blk = PW // N_DEV

    def body(x_ref, g_ref, w8_ref, c_ref, s_ref, q_ref, kv_ref, bcu_ref, qx_ref, h_ref, w_out, w_ref):
        @pl.when(pl.program_id(0) == 0)
        def _():
            for j in range(N_DEV):
                w_ref[:, j * blk:(j + 1) * blk] = w8_ref[j]
            w_out[...] = w_ref[...]

        y, _, _ = _rms(x_ref[...], g_ref[...])
        h = y.astype(BF16)
        h_ref[...] = h.T
        proj = _dot(h, w_ref[...])
        cos, sn = _all_heads(c_ref[...]), _all_heads(s_ref[...])
        q, k = proj[:, 0:AW], proj[:, AW:2 * AW]
        q_ref[...] = (q * cos + _rot_half(q) * sn) * SCALE
        kv_ref[...] = _pack_pair(k * cos + _rot_half(k) * sn, proj[:, 2 * AW:3 * AW])
        bcu_ref[...] = proj[:, 3 * AW:3 * AW + 3 * CW]
        qx_ref[...] = (proj[:, 3 * AW + 3 * CW:] * SCALE).astype(BF16)

    def tile(w):
        return pl.BlockSpec((TQ, w), lambda i: (i, 0))

    return _call_with_gather(
        body, NT, shards, name="in_proj",
        in_specs=[tile(D), _const((1, D)), _const((N_DEV, D, blk)), tile(LANES), tile(LANES)],
        out_specs=[tile(AW), tile(AW), tile(3 * CW), tile(XW), _tokens_in_lanes(TQ), _acc((D, PW))],
        out_shape=[jax.ShapeDtypeStruct((S, AW), F32)] * 2 + [
            jax.ShapeDtypeStruct((S, 3 * CW), F32), jax.ShapeDtypeStruct((S, XW), BF16),
            jax.ShapeDtypeStruct((D, S), BF16), jax.ShapeDtypeStruct((D, PW), BF16)],
        scratch_shapes=[pltpu.VMEM((D, PW), BF16)], vmem_mb=56, args=(x, g, w8, cos, sins),
        collective_id=ID_IN_PROJ)


ATTN_PLANS = (("p1", 1, 128, 32), ("p4", 8, 64, 8), ("p16", 16, 128, 2))
PAD = 128
WIN = 256


ATTN_UNROLL = 16


def _fill_bias(tab, qblk, partner):
    qi = lax.broadcasted_iota(jnp.int32, (2 * qblk, WIN), 0) & (qblk - 1)
    kj = lax.broadcasted_iota(jnp.int32, (2 * qblk, WIN), 1)
    piece = kj >> (qblk.bit_length() - 1)
    kk = kj & (qblk - 1)
    prev = (piece & 1) == 0
    of_partner = piece >= 2
    for first in (0, 1):
        for par in (0, 1):
            lo = jnp.where(prev, (qblk if first else qi) + jnp.where(of_partner, par, 0), 0)
            hi = jnp.where(prev, qblk, qi + jnp.where(of_partner, par - 1, 0))
            tab[2 * first + par] = jnp.where((kk >= lo) & (kk <= hi), 0.0, NEG).astype(F32)


def _block_rows(g, qblk, nbc, partner):
    own = pl.ds(pl.multiple_of(PAD + g * qblk, qblk), qblk)
    first = ((g & (nbc - 1)) == 0).astype(jnp.int32)
    if partner:
        gp = jnp.bitwise_xor(g, 4 * nbc)
        wins = (pl.ds(pl.multiple_of(PAD + (g - 1) * qblk, qblk), 2 * qblk),
                pl.ds(pl.multiple_of(PAD + (gp - 1) * qblk, qblk), 2 * qblk))
        return own, wins, 2 * first + ((g >> ((4 * nbc).bit_length() - 1)) & 1)
    return own, (pl.ds(pl.multiple_of(PAD + (g - 1) * qblk, qblk), 2 * qblk),), 2 * first


def _pack_pair(lo, hi):
    lo_bits = lax.bitcast_convert_type(lo.astype(BF16).astype(F32), jnp.uint32) >> 16
    hi_bits = lax.bitcast_convert_type(hi.astype(BF16).astype(F32), jnp.uint32) & jnp.uint32(0xFFFF0000)
    return lax.bitcast_convert_type(hi_bits | lo_bits, F32)


def _unpack_pair(c):
    bits = lax.bitcast_convert_type(c, jnp.uint32)
    lo = lax.bitcast_convert_type(bits << 16, F32).astype(BF16)
    hi = lax.bitcast_convert_type(bits & jnp.uint32(0xFFFF0000), F32).astype(BF16)
    return lo, hi


def _window(ref, wins):
    parts = [ref[w, :] for w in wins]
    return parts[0] if len(parts) == 1 else jnp.concatenate(parts, axis=0)


def _stack_heads(t, lane):
    zero = jnp.zeros_like(t)
    return jnp.concatenate([jnp.where(lane < HEAD, t, zero), jnp.where(lane >= HEAD, t, zero)], axis=0)


def _unstack_heads(t2, lane):
    half = t2.shape[0] // 2
    return jnp.where(lane < HEAD, t2[0:half, :], t2[half:, :])


def _lanes_of(step):
    return pl.ds(pl.multiple_of(step * LANES, LANES), LANES)


def _whole_wait(buf, sem):
    whole = buf.at[pl.ds(PAD, S), :]
    return pltpu.make_async_copy(whole, whole, sem)


def _whole_waits(bufs, sems):
    return [_whole_wait(buf, sems.at[i]) for i, buf in enumerate(bufs)]


def _class_gather(views, bufs, sems, lanes):
    copies = []
    for i, (view, buf) in enumerate(zip(views, bufs)):
        if view.ndim == 2:
            copies.append(pltpu.make_async_copy(view.at[:, lanes], buf.at[pl.ds(PAD, S), :], sems.at[i]))
        else:
            per, n_cls = view.shape[0], view.shape[1]
            copies += [pltpu.make_async_copy(view.at[:, c, lanes], buf.at[pl.ds(PAD + c * per, per), :], sems.at[i])
                       for c in range(n_cls)]
    return copies


def _class_scatter(bufs, dsts, sems, lanes):
    copies = []
    for i, (buf, dst) in enumerate(zip(bufs, dsts)):
        if dst.ndim == 2:
            copies.append(pltpu.make_async_copy(buf.at[pl.ds(PAD, S), :], dst.at[:, lanes], sems.at[i]))
            continue
        per, n_cls = dst.shape[0], dst.shape[1]
        copies += [pltpu.make_async_copy(buf.at[pl.ds(PAD + c * per, per), :], dst.at[:, c, lanes], sems.at[i])
                   for c in range(n_cls)]
    return copies


def _start(copies):
    for cp in copies:
        cp.start()


def _wait(waits):
    for w in waits:
        w.wait()


def _attn_fwd(q, kvp, shards=()):
    views = [[a] + [a.reshape(S // n, n, AW) for _, n, _, _ in ATTN_PLANS[1:]] for a in (q, kvp)]
    flat = [views[a][p] for p in range(3) for a in range(2)]
    ng = len(shards)
    n_grid = AW // LANES

    def body(*refs):
        hbm = [refs[2 * p:2 * p + 2] for p in range(3)]
        refs = refs[6:]
        shard_refs, refs = refs[:ng], refs[ng:]
        y_ref, lt_ref = refs[0:2]
        whole_refs, refs = refs[2:2 + ng], refs[2 + ng:]
        bufs = [refs[2 * p:2 * p + 2] for p in range(3)]
        oc4, lc4, oc16, lc16, tab128, tab4, sem_in = refs[6:13]
        step = pl.program_id(0)
        if ng:
            enter_gather, start_gather, relay_gather, finish_gather = _gather_steps(
                shard_refs, whole_refs, *refs[13:], own_barrier=False)
            pl.when(step == 0)(enter_gather)
            pl.when(step == 0)(start_gather)
            pl.when(step == n_grid // 2)(relay_gather)
        now = [_class_gather(hbm[p], bufs[p], sem_in.at[p], _lanes_of(step)) for p in range(3)]
        nxt = [_class_gather(hbm[p], bufs[p], sem_in.at[p], _lanes_of(step + 1)) for p in range(3)]

        @pl.when(step == 0)
        def _():
            for p in range(3):
                _start(now[p])
                for b in bufs[p]:
                    b[0:PAD, :] = jnp.zeros((PAD, LANES), F32)
            _fill_bias(tab128, 128, False)
            _fill_bias(tab4, 64, True)

        def prefetch(p):
            pl.when(step + 1 < n_grid)(lambda: _start(nxt[p]))

        lane = lax.broadcasted_iota(jnp.int32, (1, LANES), 1)
        ones = jnp.ones((WIN, LANES), BF16)

        def run(plan, bq, bkv, tab, o_dst, l_dst, dst_pad):
            _, n_cls, qblk, nbc = plan
            partner = n_cls == 8

            def block(g, carry):
                own, wins, mask = _block_rows(g, qblk, nbc, partner)
                q2 = _stack_heads(bq[own, :].astype(BF16), lane)
                kw, vwin = _unpack_pair(_window(bkv, wins))
                vw = jnp.concatenate([vwin, ones], axis=1)
                s = _dot_nt(q2, kw) + tab[mask]
                m = jnp.max(s, axis=1, keepdims=True)
                oe = _dot(jnp.exp(s - m).astype(BF16), vw)
                den = oe[:, LANES:]
                dst = pl.ds(pl.multiple_of(dst_pad + g * qblk, qblk), qblk)
                o_dst[dst, :] = _unstack_heads(oe[:, 0:LANES] / den, lane)
                l_dst[dst, :] = _unstack_heads(m + jnp.log(den), lane)
                return carry
            lax.fori_loop(0, n_cls * nbc, block, 0, unroll=ATTN_UNROLL)

        _wait(_whole_waits(bufs[0], sem_in.at[0]))
        run(ATTN_PLANS[0], *bufs[0], tab128, y_ref, lt_ref, 0)
        prefetch(0)
        _wait(_whole_waits(bufs[1], sem_in.at[1]))
        run(ATTN_PLANS[1], *bufs[1], tab4, oc4, lc4, PAD)
        prefetch(1)
        _wait(_whole_waits(bufs[2], sem_in.at[2]))
        run(ATTN_PLANS[2], *bufs[2], tab128, oc16, lc16, PAD)
        prefetch(2)

        n_rows = 64

        def token_order(buf, t, n_cls):
            per = S // n_cls
            first = PAD + t * (n_rows // n_cls)
            return jnp.concatenate([buf[pl.ds(first + jj, n_cls, stride=per), :] for jj in range(n_rows // n_cls)],
                                   axis=0)

        def combine(t, carry):
            rows = pl.ds(pl.multiple_of(t * n_rows, n_rows), n_rows)
            l0, l1, l2 = lt_ref[rows, :], token_order(lc4, t, 8), token_order(lc16, t, 16)
            lm = jnp.maximum(jnp.maximum(l0, l1), l2)
            e0, e1, e2 = jnp.exp(l0 - lm), jnp.exp(l1 - lm), jnp.exp(l2 - lm)
            den = e0 + e1 + e2
            y_ref[rows, :] = (e0 * y_ref[rows, :] + e1 * token_order(oc4, t, 8)
                              + e2 * token_order(oc16, t, 16)) / den
            lt_ref[rows, :] = lm + jnp.log(den)
            return carry
        lax.fori_loop(0, S // n_rows, combine, 0, unroll=2)

        if ng:
            pl.when(step == n_grid - 1)(finish_gather)

    col = pl.BlockSpec((S, LANES), lambda h: (0, h))
    padded = pltpu.VMEM((PAD + S, LANES), F32)
    return pl.pallas_call(
        body, grid=(n_grid,), name="attn_fwd",
        in_specs=[ANY] * (6 + ng), out_specs=[col, col] + [ANY] * ng,
        out_shape=[jax.ShapeDtypeStruct((S, AW), F32)] * 2 + _gathered_shapes(shards),
        scratch_shapes=[padded] * 10 + [
            pltpu.VMEM((4, 256, WIN), F32), pltpu.VMEM((4, 128, WIN), F32), pltpu.SemaphoreType.DMA((3, 2))]
        + (_gather_scratch(ng) if ng else []),
        compiler_params=_cparams(56))(*flat, *shards)


def _conv_taps(z, zprev, row):
    z1 = jnp.where(row == 0, zprev[7:8, :], pltpu.roll(z, 1, 0))
    z2 = jnp.where(row == 0, zprev[6:7, :], jnp.where(row == 1, zprev[7:8, :], pltpu.roll(z, 2, 0)))
    return z1, z2


def _xattn_scores(qm, km):
    s = _dot_nt(qm, km)
    m = jnp.max(s, axis=1, keepdims=True)
    e = jnp.exp(s - m)
    return e, jnp.sum(e, axis=1, keepdims=True)


def _mix_out(y_attn, bcu, qx16, kv16, cw8, g_attn, g_conv, g_x, g_post, wout16, x, shards):
    def body(ya_ref, bcu_ref, halo_ref, qx_ref, kv_ref, cw_ref, ga_ref, gc_ref, gx_ref, gp_ref, w_ref, x_ref,
             ypre_ref, y16_ref, y2_ref, x1_ref):
        i = pl.program_id(0)
        bcu = bcu_ref[...]
        b, c, u = bcu[:, 0:CW], bcu[:, CW:2 * CW], bcu[:, 2 * CW:]
        z = c * u
        halo = halo_ref[...]
        zprev = jnp.where(i > 0, halo[:, CW:2 * CW] * halo[:, 2 * CW:], 0.0)
        row = lax.broadcasted_iota(jnp.int32, z.shape, 0)
        z1, z2 = _conv_taps(z, zprev, row)
        cw = cw_ref[...]
        y_conv = b * (z2 * cw[0:1, :] + z1 * cw[1:2, :] + z * cw[2:3, :])

        qx = qx_ref[...]
        kv = kv_ref[...]
        km, vm = kv[:, 0:XW], kv[:, XW:]
        lane = lax.broadcasted_iota(jnp.int32, qx.shape, 1)
        y_x = jnp.zeros(qx.shape, F32)
        for h in range(XW // HEAD):
            hm = (lane >= h * HEAD) & (lane < (h + 1) * HEAD)
            e, l = _xattn_scores(jnp.where(hm, qx, jnp.zeros_like(qx)), km)
            y_x = jnp.where(hm, _dot(e.astype(BF16), vm) / l, y_x)

        y_attn = ya_ref[...]
        ypre_ref[:, 0:CW] = y_conv
        ypre_ref[:, CW:] = y_x
        y = jnp.concatenate([_rms(y_attn, ga_ref[...])[0], _rms(y_conv, gc_ref[...])[0],
                             _rms(y_x, gx_ref[...])[0]], axis=1).astype(BF16)
        y16_ref[...] = y.T
        y2 = _dot(y, w_ref[...])
        y2_ref[...] = y2
        x1_ref[...] = x_ref[...] + _rms(y2, gp_ref[...])[0]

    def tile(w):
        return pl.BlockSpec((TQ, w), lambda i: (i, 0))

    halo = pl.BlockSpec((SUBLANES, 3 * CW), lambda i: (jnp.maximum(i * (TQ // SUBLANES) - 1, 0), 0))
    return _call_with_gather(
        body, NT, shards, name="mix_out",
        in_specs=[tile(AW), tile(3 * CW), halo, tile(XW), _const((N_MEM, 2 * XW)), _const((SUBLANES, CW)),
                  _const((1, AW)), _const((1, CW)), _const((1, XW)), _const((1, D)), _const((D, D)), tile(D)],
        out_specs=[tile(CW + XW), _tokens_in_lanes(TQ), tile(D), tile(D)],
        out_shape=[jax.ShapeDtypeStruct((S, CW + XW), F32), jax.ShapeDtypeStruct((D, S), BF16),
                   jax.ShapeDtypeStruct((S, D), F32), jax.ShapeDtypeStruct((S, D), F32)],
        scratch_shapes=[], vmem_mb=56,
        args=(y_attn, bcu, bcu, qx16, kv16, cw8, g_attn, g_conv, g_x, g_post, wout16, x))


def _mlp(x1, tgt, g_pre, g_post, wup8, wdn_halves):
    tq = TQ_MLP
    half = D // 2

    def body(x1_ref, t_ref, g1_ref, g2_ref, wu_ref, wda_ref, wdb_ref,
             a16_ref, du_ref, h2_ref, df2_ref, dx1_ref, loss_ref, dg_ref):
        @pl.when(pl.program_id(0) == 0)
        def _():
            loss_ref[...] = jnp.zeros_like(loss_ref)
            dg_ref[...] = jnp.zeros_like(dg_ref)

        x1 = x1_ref[...]
        g1, g2 = g1_ref[...], g2_ref[...]
        y1, n1, r1 = _rms(x1, g1)
        h2 = y1.astype(BF16)
        h2_ref[...] = h2.T
        f2a = jnp.zeros((tq, half), F32)
        f2b = jnp.zeros((tq, half), F32)
        for j in range(N_DEV):
            cols = slice(j * FF_BLK, (j + 1) * FF_BLK)
            a = jnp.maximum(_dot(h2, wu_ref[j]), 0.0)
            a16_ref[:, cols] = a.astype(BF16)
            f = (a * a).astype(BF16)
            f2a = f2a + _dot(f, wda_ref[cols, :])
            f2b = f2b + _dot(f, wdb_ref[cols, :])
        f2 = jnp.concatenate([f2a, f2b], axis=1)
        y2, n2, r2 = _rms(f2, g2)
        e = x1 + y2 - t_ref[...]
        sq = jnp.sum(jnp.sum(e * e, axis=1, keepdims=True), axis=0, keepdims=True)
        loss_ref[...] += jnp.broadcast_to(sq * (0.5 / D), loss_ref.shape)
        dout = e * (1.0 / D)
        df2, dg2 = _rms_bwd(dout, n2, r2, g2)
        df2_16 = df2.astype(BF16)
        df2_ref[...] = df2_16.T
        dh2 = jnp.zeros((tq, D), F32)
        for j in range(N_DEV):
            cols = slice(j * FF_BLK, (j + 1) * FF_BLK)
            df = _dot_nt(df2_16[:, 0:half], wda_ref[cols, :]) + _dot_nt(df2_16[:, half:], wdb_ref[cols, :])
            du = (df * (2.0 * a16_ref[:, cols].astype(F32))).astype(BF16)
            du_ref[:, cols] = du
            dh2 = dh2 + _dot_nt(du, wu_ref[j])
        dx, dg1 = _rms_bwd(dh2, n1, r1, g1)
        dx1_ref[...] = dout + dx
        dg_ref[0:1, :] += dg2
        dg_ref[1:2, :] += dg1

    def tile(w):
        return pl.BlockSpec((tq, w), lambda i: (i, 0))

    return pl.pallas_call(
        body, grid=(S // tq,), name="mlp",
        in_specs=[tile(D), tile(D), _const((1, D)), _const((1, D)), _const((N_DEV, D, FF_BLK)), _const((FF, half)), _const((FF, half))],
        out_specs=[tile(FF), tile(FF), _tokens_in_lanes(tq), _tokens_in_lanes(tq), tile(D),
                   _acc((SUBLANES, LANES)), _acc((SUBLANES, D))],
        out_shape=[jax.ShapeDtypeStruct((S, FF), BF16), jax.ShapeDtypeStruct((S, FF), BF16),
                   jax.ShapeDtypeStruct((D, S), BF16), jax.ShapeDtypeStruct((D, S), BF16),
                   jax.ShapeDtypeStruct((S, D), F32), jax.ShapeDtypeStruct((SUBLANES, LANES), F32),
                   jax.ShapeDtypeStruct((SUBLANES, D), F32)],
        compiler_params=_cparams(60))(x1, tgt, g_pre, g_post, wup8, *wdn_halves)


def _mix_out_bwd(dx1, y2, ypre, y_attn, ltot, head_ones, q, bcu, qx16, kv16, cw8, g_post, g_attn, g_conv, g_x, wout16):
    def body(dx1_ref, y2_ref, ypre_ref, ya_ref, lt_ref, e_ref, q_ref, bcu_ref, halo_ref, qx_ref, kv_ref, cw_ref, gp_ref,
             ga_ref, gc_ref, gx_ref, w_ref, dy2_ref, qdo_ref, ld_ref, dbcu_ref, dqx_ref, dgs_ref, dcw_ref, dkv_ref,
             carry):
        i = pl.program_id(0)

        @pl.when(i == 0)
        def _():
            dgs_ref[...] = jnp.zeros_like(dgs_ref)
            dcw_ref[...] = jnp.zeros_like(dcw_ref)
            dkv_ref[...] = jnp.zeros_like(dkv_ref)
            carry[...] = jnp.zeros_like(carry)

        gp = gp_ref[...]
        _, n, r = _rms(y2_ref[...], gp)
        dy2, dgp = _rms_bwd(dx1_ref[...], n, r, gp)
        dy2_16 = dy2.astype(BF16)
        dy2_ref[...] = dy2_16
        dy = _dot_nt(dy2_16, w_ref[...])

        ypre, y_a = ypre_ref[...], ya_ref[...]
        ga, gc, gx = ga_ref[...], gc_ref[...], gx_ref[...]
        _, na, ra = _rms(y_a, ga)
        dya, dga = _rms_bwd(dy[:, 0:AW], na, ra, ga)
        _, nc, rc = _rms(ypre[:, 0:CW], gc)
        dyc, dgc = _rms_bwd(dy[:, AW:AW + CW], nc, rc, gc)
        y_x = ypre[:, CW:]
        _, nx, rx = _rms(y_x, gx)
        dyx, dgx = _rms_bwd(dy[:, AW + CW:], nx, rx, gx)
        qdo_ref[...] = _pack_pair(q_ref[...], dya)
        prod = dya * y_a
        hi = prod.astype(BF16)
        lo = (prod - hi.astype(F32)).astype(BF16)
        head_sum = _dot(hi, e_ref[...]) + _dot(lo, e_ref[...])
        lane_a = lax.broadcasted_iota(jnp.int32, prod.shape, 1)
        ld_ref[...] = jnp.where((lane_a % HEAD) < HEAD // 2, lt_ref[...], head_sum)
        dgs_ref[0:1, :] += dgp
        dgs_ref[1:2, :] += jnp.concatenate([dga, dgc, dgx], axis=1)

        bcu = bcu_ref[...]
        b, c, u = bcu[:, 0:CW], bcu[:, CW:2 * CW], bcu[:, 2 * CW:]
        z = c * u
        halo = halo_ref[...]
        zprev = jnp.where(i < NT - 1, halo[:, CW:2 * CW] * halo[:, 2 * CW:], 0.0)
        row = lax.broadcasted_iota(jnp.int32, z.shape, 0)
        z1, z2 = _conv_taps(z, zprev, row)
        cw = cw_ref[...]
        conv = z2 * cw[0:1, :] + z1 * cw[1:2, :] + z * cw[2:3, :]
        dconv = dyc * b
        nxt = carry[...]
        dn1 = jnp.where(row == TQ - 1, nxt[0:1, :], pltpu.roll(dconv, TQ - 1, 0))
        dn2 = jnp.where(row == TQ - 1, nxt[1:2, :], jnp.where(row == TQ - 2, nxt[0:1, :], pltpu.roll(dconv, TQ - 2, 0)))
        carry[...] = dconv[0:SUBLANES, :]
        dz = dconv * cw[2:3, :] + dn1 * cw[1:2, :] + dn2 * cw[0:1, :]
        dbcu_ref[:, 0:CW] = (dyc * conv).astype(BF16)
        dbcu_ref[:, CW:2 * CW] = (dz * u).astype(BF16)
        dbcu_ref[:, 2 * CW:] = (dz * c).astype(BF16)
        dcw_ref[0:1, :] += jnp.sum(z2 * dconv, axis=0, keepdims=True)
        dcw_ref[1:2, :] += jnp.sum(z1 * dconv, axis=0, keepdims=True)
        dcw_ref[2:3, :] += jnp.sum(z * dconv, axis=0, keepdims=True)

        qx = qx_ref[...]
        kv = kv_ref[...]
        km, vm = kv[:, 0:XW], kv[:, XW:]
        lane = lax.broadcasted_iota(jnp.int32, qx.shape, 1)
        dqx = jnp.zeros(qx.shape, F32)
        dkm = jnp.zeros((N_MEM, XW), F32)
        dvm = jnp.zeros((N_MEM, XW), F32)
        for h in range(XW // HEAD):
            hm = (lane >= h * HEAD) & (lane < (h + 1) * HEAD)
            qm = jnp.where(hm, qx, jnp.zeros_like(qx))
            e, l = _xattn_scores(qm, km)
            p = e / l
            dom = jnp.where(hm, dyx, 0.0)
            do16 = dom.astype(BF16)
            dsum = jnp.sum(dom * y_x, axis=1, keepdims=True)
            ds = (p * (_dot_nt(do16, vm) - dsum)).astype(BF16)
            dqx = jnp.where(hm, _dot(ds, km), dqx)
            dkm = dkm + _dot_tn(ds, qm)
            dvm = dvm + _dot_tn(p.astype(BF16), do16)
        dqx_ref[...] = (dqx * SCALE).astype(BF16)
        dkv_ref[:, 0:XW] += dkm
        dkv_ref[:, XW:] += dvm

    def tile(w):
        return pl.BlockSpec((TQ, w), lambda i: (NT - 1 - i, 0))

    halo = pl.BlockSpec((SUBLANES, 3 * CW), lambda i: (jnp.maximum((NT - 1 - i) * (TQ // SUBLANES) - 1, 0), 0))
    return pl.pallas_call(
        body, grid=(NT,), name="mix_out_bwd",
        in_specs=[tile(D), tile(D), tile(CW + XW), tile(AW), tile(AW), _const((AW, AW)), tile(AW), tile(3 * CW), halo,
                  tile(XW),
                  _const((N_MEM, 2 * XW)), _const((SUBLANES, CW)), _const((1, D)), _const((1, AW)), _const((1, CW)),
                  _const((1, XW)), _const((D, D))],
        out_specs=[tile(D), tile(AW), tile(AW), tile(3 * CW), tile(XW), _acc((SUBLANES, D)), _acc((SUBLANES, CW)),
                   _acc((N_MEM, 2 * XW))],
        out_shape=[jax.ShapeDtypeStruct((S, D), BF16), jax.ShapeDtypeStruct((S, AW), F32),
                   jax.ShapeDtypeStruct((S, AW), F32),
                   jax.ShapeDtypeStruct((S, 3 * CW), BF16), jax.ShapeDtypeStruct((S, XW), BF16),
                   jax.ShapeDtypeStruct((SUBLANES, D), F32), jax.ShapeDtypeStruct((SUBLANES, CW), F32),
                   jax.ShapeDtypeStruct((N_MEM, 2 * XW), F32)],
        scratch_shapes=[pltpu.VMEM((SUBLANES, CW), F32)],
        compiler_params=_cparams(56))(dx1, y2, ypre, y_attn, ltot, head_ones, q, bcu, bcu, qx16, kv16, cw8, g_post, g_attn,
                                      g_conv, g_x, wout16)


def _attn_bwd(qdo, kvp, ld, chip_sums=()):
    n_in = 3
    views = [[a] + [a.reshape(S // n, n, AW) for _, n, _, _ in ATTN_PLANS[1:]] for a in (qdo, kvp, ld)]
    flat = [views[a][p] for p in range(3) for a in range(n_in)]
    ns = len(chip_sums)
    n_grid = AW // LANES

    def body(*refs):
        hbm = [refs[n_in * p:n_in * p + n_in] for p in range(3)]
        refs = refs[3 * n_in:]
        sum_refs, refs = refs[:ns], refs[ns:]
        outs = [refs[3 * p:3 * p + 3] for p in range(3)]
        landed_refs, sc = refs[9:9 + ns], refs[9 + ns:]
        bufs = [sc[3 * p:3 * p + 3] for p in range(3)]
        res = [sc[9 + 3 * p:12 + 3 * p] for p in range(3)]
        tab128, tab4, sem_in, sem_out = sc[18:22]
        step = pl.program_id(0)
        if ns:
            start_chips, finish_chips = _chips_steps(sum_refs, landed_refs, *sc[22:])
            _, _, core, chips = _place()
            signal_chips, chips_are_in = _own_barrier([(px, py, core) for px, py in chips])
            pl.when(step == 0)(signal_chips)

            def chips_go():
                chips_are_in()
                start_chips()
        now =[_class_gather(hbm[p], bufs[p], sem_in.at[p], _lanes_of(step)) for p in range(3)]
        nxt = [_class_gather(hbm[p], bufs[p], sem_in.at[p], _lanes_of(step + 1)) for p in range(3)]

        @pl.when(step == 0)
        def _():
            for p in range(3):
                _start(now[p])
                for b in bufs[p]:
                    b[0:PAD, :] = jnp.zeros((PAD, LANES), F32)
            _fill_bias(tab128, 128, False)
            _fill_bias(tab4, 64, True)

        def prefetch(p):
            pl.when(step + 1 < n_grid)(lambda: _start(nxt[p]))

        lane = lax.broadcasted_iota(jnp.int32, (1, LANES), 1)

        def run(plan, plan_bufs, tab, dst):
            _, n_cls, qblk, nbc = plan
            partner = n_cls == 8
            bqdo, bkv, bld = plan_bufs
            rq, rk, rv = dst

            def block(g, carry):
                own, wins, mask = _block_rows(g, qblk, nbc, partner)
                qb, dob = _unpack_pair(bqdo[own, :])
                q2, do2 = _stack_heads(qb, lane), _stack_heads(dob, lane)
                kw, vw = _unpack_pair(_window(bkv, wins))
                ldv = bld[own, :]
                half = HEAD // 2
                lt2 = jnp.concatenate([ldv[:, 0:1], ldv[:, HEAD:HEAD + 1]], axis=0)
                dsum2 = jnp.concatenate([ldv[:, half:half + 1], ldv[:, HEAD + half:HEAD + half + 1]], axis=0)
                p = jnp.exp(_dot_nt(q2, kw) + tab[mask] - lt2)
                ds = (p * (_dot_nt(do2, vw) - dsum2)).astype(BF16)
                rq[own, :] = _unstack_heads(_dot(ds, kw), lane)
                dkw = _dot_tn(ds, q2)
                dvw = _dot_tn(p.astype(BF16), do2)
                n_w = WIN // len(wins)
                for i, w in enumerate(wins):
                    rk[w, :] += dkw[i * n_w:(i + 1) * n_w, :]
                    rv[w, :] += dvw[i * n_w:(i + 1) * n_w, :]
                return carry
            lax.fori_loop(0, n_cls * nbc, block, 0, unroll=ATTN_UNROLL)

        tabs = (tab128, tab4, tab128)
        def drained(p):
            return lambda: _wait(_whole_waits(res[p], sem_out.at[p]))

        for p in range(3):
            pl.when(step > 0)(drained(p))
            for b in res[p][1:]:
                b[...] = jnp.zeros_like(b)
            _wait(_whole_waits(bufs[p], sem_in.at[p]))
            run(ATTN_PLANS[p], bufs[p], tabs[p], res[p])
            prefetch(p)
            _start(_class_scatter(res[p], outs[p], sem_out.at[p], _lanes_of(step)))
            if ns and p == 0:
                pl.when(step == 0)(chips_go)
        for p in range(3):
            pl.when(step == n_grid - 1)(drained(p))
        if ns:
            pl.when(step == n_grid - 1)(finish_chips)

    padded = pltpu.VMEM((PAD + S, LANES), F32)
    shapes = [jax.ShapeDtypeStruct(views[0][p].shape, F32) for p in range(3) for _ in range(3)]
    out = pl.pallas_call(
        body, grid=(n_grid,), name="attn_bwd",
        in_specs=[ANY] * (3 * n_in + ns), out_specs=[ANY] * (9 + ns),
        out_shape=shapes + _chips_shapes(chip_sums),
        scratch_shapes=[padded] * 18
        + [pltpu.VMEM((4, 256, WIN), F32), pltpu.VMEM((4, 128, WIN), F32),
           pltpu.SemaphoreType.DMA((3, n_in)), pltpu.SemaphoreType.DMA((3, 3))]
        + (_chips_scratch(ns) if ns else []),
        compiler_params=_cparams(56, **({"collective_id": ID_ATTN_BWD} if ns else {})))(*flat, *chip_sums)
    return [o.reshape(S, AW) for o in out[:9]] + list(out[9:])


def _in_proj_bwd(dqkv, dbcu, dqx, cos, sins, w16, x, g, dx1):
    tq = TQ // 2

    def body(*refs):
        parts = refs[0:9]
        dbcu_ref, dqx_ref, c_ref, s_ref, w_ref, x_ref, g_ref, dx1_ref, dp_ref, gx_ref, dg_ref = refs[9:]

        @pl.when(pl.program_id(0) == 0)
        def _():
            dg_ref[...] = jnp.zeros_like(dg_ref)

        dq, dk, dv = (parts[i][...] + parts[3 + i][...] + parts[6 + i][...] for i in range(3))
        cos, sn = _all_heads(c_ref[...]), _all_heads(s_ref[...])
        dqr = dq * SCALE
        dkr = dk
        dp = jnp.concatenate([(dqr * cos + _rot_half(dqr * sn)).astype(BF16),
                              (dkr * cos + _rot_half(dkr * sn)).astype(BF16), dv.astype(BF16),
                              dbcu_ref[...], dqx_ref[...]], axis=1)
        dp_ref[...] = dp
        dh = _dot_nt(dp, w_ref[...])
        g = g_ref[...]
        _, n, r = _rms(x_ref[...], g)
        dx, dg = _rms_bwd(dh, n, r, g)
        gx_ref[...] = dx1_ref[...] + dx
        dg_ref[0:1, :] += dg

    def tile(w):
        return pl.BlockSpec((tq, w), lambda i: (i, 0))

    return pl.pallas_call(
        body, grid=(S // tq,), name="in_proj_bwd",
        in_specs=[tile(AW)] * 9 + [tile(3 * CW), tile(XW), tile(LANES), tile(LANES), _const((D, PW)),
                                   tile(D), _const((1, D)), tile(D)],
        out_specs=[tile(PW), tile(D), _acc((SUBLANES, D))],
        out_shape=[jax.ShapeDtypeStruct((S, PW), BF16), jax.ShapeDtypeStruct((S, D), F32),
                   jax.ShapeDtypeStruct((SUBLANES, D), F32)],
        compiler_params=_cparams(56))(*dqkv, dbcu, dqx, cos, sins, w16, x, g, dx1)


def _mem_bwd(mem, g_mem, wkv16, dkv):
    def body(m_ref, g_ref, w_ref, dkv_ref, dkv16_ref, dg_ref):
        dkv16 = dkv_ref[...].astype(BF16)
        dkv16_ref[...] = dkv16
        _, n, _ = _rms(m_ref[...], g_ref[...])
        dg = jnp.sum(_dot_nt(dkv16, w_ref[...]) * n, axis=0, keepdims=True)
        dg_ref[...] = jnp.broadcast_to(dg, dg_ref.shape)

    return pl.pallas_call(
        body, name="mem_bwd",
        out_shape=[jax.ShapeDtypeStruct((N_MEM, 2 * XW), BF16), jax.ShapeDtypeStruct((SUBLANES, D), F32)],
        compiler_params=pltpu.CompilerParams(vmem_limit_bytes=32 << 20))(mem, g_mem, wkv16, dkv)


N_CHIPS = N_DEV // 2


def _pair_scratch(block):
    return [pltpu.VMEM((N_CHIPS,) + block, BF16), pltpu.VMEM((N_CHIPS,) + block, BF16),
            pltpu.SemaphoreType.DMA((N_CHIPS,)), pltpu.SemaphoreType.DMA((N_CHIPS,))]


def _swap_with_sibling(p, stage, land, send, recv):
    x, y, c = lax.axis_index("x"), lax.axis_index("y"), lax.axis_index("c")
    return pltpu.make_async_remote_copy(src_ref=stage.at[p], dst_ref=land.at[p], send_sem=send.at[p],
                                        recv_sem=recv.at[p], device_id=(x, y, 1 - c), device_id_type=MESH)


def _own_barrier(peers):
    sem = pltpu.get_barrier_semaphore()

    def signal():
        for peer in peers:
            pl.semaphore_signal(sem, inc=1, device_id=peer, device_id_type=MESH)

    return signal, lambda: pl.semaphore_wait(sem, len(peers))


def _sibling_barrier():
    x, y, c = lax.axis_index("x"), lax.axis_index("y"), lax.axis_index("c")
    return _own_barrier([(x, y, 1 - c)])


ID_WGRAD_UP, ID_WGRAD_DOWN, ID_WGRAD_ROWS, ID_ROPE_TABLE, ID_IN_PROJ, ID_ATTN_BWD, ID_WGRAD_IN = range(7)


def _wgrad_cols(place, at16, b16, blk, name, barrier_id, square_b=False, transpose_out=False, to_chips=False,
                small=()):
    m, kk = at16.shape
    assert to_chips == bool(small)
    aligned = blk % LANES == 0
    wide = blk if aligned else -(-(blk + LANES // 2) // LANES) * LANES
    assert aligned or (transpose_out and blk % SUBLANES == 0)
    block = (blk, m) if transpose_out else (m, blk)

    def chip_of(step, my_chip):
        return jnp.bitwise_xor(my_chip, N_CHIPS - 1 - step) if to_chips else step

    def body(pl_ref, a_ref, *refs):
        b_refs, refs = refs[:2 if aligned else 1], refs[2 if aligned else 1:]
        accs, refs = refs[:len(small)], refs[len(small):]
        (cs_ref, own_ref), refs = refs[:2], refs[2:]
        if to_chips:
            landed, refs = refs[0], refs[1:]
        if small:
            tot_ref, refs = refs[0], refs[1:]
        (stage, land, send, recv), refs = refs[:4], refs[4:]
        if not aligned:
            (win, wsem), refs = refs[:2], refs[2:]
        if small:
            start_small, finish_small = _small_reduce_steps(accs, tot_ref, *refs[-4:])
            refs = refs[:-4]
        step = pl.program_id(0)
        x, y, c = lax.axis_index("x"), lax.axis_index("y"), lax.axis_index("c")
        others = [(x ^ (k >> 2), y ^ ((k >> 1) & 1), c ^ (k & 1)) for k in range(1, N_DEV)]
        signal_peers, peers_are_in = _own_barrier(others if small else [(x, y, 1 - c)])
        pl.when(step == 0)(signal_peers)
        my_chip = 2 * x + y
        p = chip_of(step, my_chip)

        def fetch(at_step, mine):
            j = 2 * chip_of(at_step, my_chip) + (c if mine else 1 - c)
            first = pl.multiple_of(((j * blk) >> 7) << 7, LANES)
            slot = 2 * (at_step & 1) + mine
            return pltpu.make_async_copy(b_refs[0].at[:, pl.ds(first, wide)], win.at[slot], wsem.at[slot])

        if not aligned:
            @pl.when(step == 0)
            def _():
                fetch(0, 0).start()
                fetch(0, 1).start()

            @pl.when(step + 1 < N_CHIPS)
            def _():
                fetch(step + 1, 0).start()
                fetch(step + 1, 1).start()

        def partial(mine):
            if aligned:
                b = b_refs[mine][...]
                if square_b:
                    b = b * b
                acc = _dot(a_ref[...], b)
            else:
                fetch(step, mine).wait()
                acc = _dot(a_ref[...], win[2 * (step & 1) + mine]).T
                odd = c if mine else 1 - c
                return jnp.where(odd == 0, acc[0:blk], acc[wide - blk:wide])
            return acc.T if transpose_out else acc

        stage[p] = partial(0).astype(BF16)
        pl.when(step == 0)(peers_are_in)
        if small:
            pl.when(step == 0)(start_small)
        swap = _swap_with_sibling(p, stage, land, send, recv)
        swap.start()
        mine = partial(1)
        swap.wait()
        total = mine + land[p].astype(F32)
        cs_ref[0] = total.astype(BF16)

        @pl.when(p == my_chip)
        def _():
            own_ref[...] = total

        if to_chips:
            stage2, send2, recv2 = refs
            flipped = jnp.bitwise_xor(p, my_chip)
            k = jnp.where(flipped == 2, 0, jnp.where(flipped == 1, 1, 2))

            def to_owner(src, k_, px, py):
                return pltpu.make_async_remote_copy(src_ref=src, dst_ref=landed.at[k_], send_sem=send2.at[k_],
                                                    recv_sem=recv2.at[k_], device_id=(px, py, c), device_id_type=MESH)

            @pl.when(p != my_chip)
            def _():
                stage2[p] = total.astype(BF16)
                to_owner(stage2.at[p], k, p >> 1, p & 1).start()

            @pl.when(step == N_CHIPS - 1)
            def _():
                for k_ in range(N_CHIPS - 1):
                    to_owner(stage2.at[0], k_, x, y).wait()

        if small:
            pl.when(step == N_CHIPS - 1)(finish_small)

    def b_spec(mine):
        return pl.BlockSpec((kk, blk), lambda i, s: (0, 2 * chip_of(i, s[1]) + (s[0] if mine else 1 - s[0])))

    b_specs, b_args = ([b_spec(0), b_spec(1)], (b16, b16)) if aligned else ([ANY], (b16,))
    scratch = _pair_scratch(block)
    if not aligned:
        scratch += [pltpu.VMEM((4, kk, wide), BF16), pltpu.SemaphoreType.DMA((4,))]
    out_specs = [pl.BlockSpec((1,) + block, lambda i, s: (chip_of(i, s[1]), 0, 0)), pl.BlockSpec(block, lambda i, s: (0, 0))]
    out_shape = [jax.ShapeDtypeStruct((N_CHIPS,) + block, BF16), jax.ShapeDtypeStruct(block, F32)]
    if to_chips:
        out_specs.append(ANY)
        out_shape.append(jax.ShapeDtypeStruct((N_CHIPS - 1,) + block, BF16))
        scratch += [pltpu.VMEM((N_CHIPS,) + block, BF16), pltpu.SemaphoreType.DMA((N_CHIPS - 1,)),
                    pltpu.SemaphoreType.DMA((N_CHIPS - 1,))]
    small_specs = [pl.BlockSpec(a.shape, lambda i, s: (0, 0)) for a in small]
    if small:
        out_specs.append(pl.BlockSpec((PACK_ROWS, D), lambda i, s: (0, 0)))
        out_shape.append(jax.ShapeDtypeStruct((PACK_ROWS, D), F32))
        scratch += _small_reduce_scratch()
    return pl.pallas_call(
        body, name=name,
        grid_spec=pltpu.PrefetchScalarGridSpec(
            num_scalar_prefetch=1, grid=(N_CHIPS,),
            in_specs=[pl.BlockSpec((m, kk), lambda i, s: (0, 0), pipeline_mode=pl.Buffered(1))] + b_specs + small_specs,
            out_specs=out_specs, scratch_shapes=scratch),
        out_shape=out_shape,
        compiler_params=_cparams(56, collective_id=barrier_id),
    )(place, at16, *b_args, *small)


ROWS_STEPS = 4


def _wgrad_rows(place, products, name):
    n_prod = len(products)
    dims = [(at16.shape[0], at16.shape[1], b16.shape[1]) for at16, b16 in products]
    cut = [kk % (ROWS_STEPS * LANES) == 0 for _, kk, _ in dims]
    blocks = [(m // N_DEV, n) for m, _, n in dims]

    def body(pl_ref, *refs):
        ins, outs, scratch = refs[:2 * n_prod], refs[2 * n_prod:4 * n_prod], refs[4 * n_prod:]
        c, step = pl_ref[0], pl.program_id(0)

        def multiply(i):
            a_ref, b_ref, acc = ins[2 * i], ins[2 * i + 1], scratch[5 * i]

            @pl.when(step == 0)
            def _():
                acc[...] = _dot(a_ref[...], b_ref[...])

            if cut[i]:
                @pl.when(step > 0)
                def _():
                    acc[...] += _dot(a_ref[...], b_ref[...])

        def rows(i, owner):
            return pl.ds(pl.multiple_of(owner * blocks[i][0], blocks[i][0]), blocks[i][0])

        def send_sibling_side(i):
            acc, stage, land, send, recv = scratch[5 * i:5 * i + 5]
            swaps = []
            for p in range(N_CHIPS):
                stage[p] = acc[rows(i, 2 * p + 1 - c), :].astype(BF16)
                swaps.append(_swap_with_sibling(p, stage, land, send, recv))
                swaps[-1].start()
            return swaps

        def add_my_side(i, swaps):
            acc, land = scratch[5 * i], scratch[5 * i + 2]
            cs_ref, own_ref = outs[2 * i:2 * i + 2]
            for p in range(N_CHIPS):
                swaps[p].wait()
                total = acc[rows(i, 2 * p + c), :] + land[p].astype(F32)
                cs_ref[p] = total.astype(BF16)

                @pl.when(p == pl_ref[1])
                def _():
                    own_ref[...] = total

        signal_sibling, sibling_is_in = _sibling_barrier()
        pl.when(step == 0)(signal_sibling)
        for i in range(n_prod):
            multiply(i)

        @pl.when(step == ROWS_STEPS - 1)
        def _():
            sibling_is_in()
            swaps = [send_sibling_side(i) for i in range(n_prod)]
            for i in range(n_prod):
                add_my_side(i, swaps[i])

    in_specs, out_specs, out_shape, scratch = [pl.BlockSpec(memory_space=pltpu.SMEM)], [], [], []
    for (m, kk, n), cut_i, block in zip(dims, cut, blocks):
        chunk = kk // ROWS_STEPS
        in_specs += ([pl.BlockSpec((m, chunk), lambda i: (0, i)), pl.BlockSpec((chunk, n), lambda i: (i, 0))]
                     if cut_i else [_const((m, kk)), _const((kk, n))])
        out_specs += [_acc((N_CHIPS,) + block), _acc(block)]
        out_shape += [jax.ShapeDtypeStruct((N_CHIPS,) + block, BF16), jax.ShapeDtypeStruct(block, F32)]
        scratch += [pltpu.VMEM((m, n), F32)] + _pair_scratch(block)
    out = pl.pallas_call(
        body, grid=(ROWS_STEPS,), name=name, in_specs=in_specs, out_specs=out_specs, out_shape=out_shape,
        scratch_shapes=scratch, compiler_params=_cparams(56, collective_id=ID_WGRAD_ROWS),
    )(place, *[a for pair in products for a in pair])
    return [tuple(out[2 * i:2 * i + 2]) for i in range(n_prod)]


def _adamw_math(w, g, m, v):
    m = ADAM_B1 * m + (1.0 - ADAM_B1) * g
    v = ADAM_B2 * v + (1.0 - ADAM_B2) * jnp.square(g)
    m_hat = m / (1.0 - ADAM_B1 ** ADAM_STEP)
    v_hat = v / (1.0 - ADAM_B2 ** ADAM_STEP)
    delta = -ADAM_LR * (m_hat / (jnp.sqrt(v_hat) + ADAM_EPS) + ADAM_WD * w)
    return delta, m, v


ADAMW_STEPS = 4


def _adamw_shards(updates, name):
    names, nu = list(updates), len(updates)

    def body(*refs):
        ins, outs = refs[:5 * nu], refs[5 * nu:]
        for i in range(nu):
            o_ref, r_ref, w_ref, m_ref, v_ref = ins[5 * i:5 * i + 5]
            g_out, d_out, m_out, v_out = outs[4 * i:4 * i + 4]
            g = o_ref[...] + r_ref[0].astype(F32) + r_ref[1].astype(F32) + r_ref[2].astype(F32)
            g_out[...] = g
            d_out[...], m_out[...], v_out[...] = _adamw_math(w_ref[...], g, m_ref[...], v_ref[...])

    in_specs, out_specs = [], []
    for n in names:
        rows, cols = updates[n][2].shape
        chunk = pl.BlockSpec((rows // ADAMW_STEPS, cols), lambda i: (i, 0))
        in_specs += [chunk, pl.BlockSpec((N_CHIPS - 1, rows // ADAMW_STEPS, cols), lambda i: (0, i, 0))] + [chunk] * 3
        out_specs += [chunk] * 4
    out = pl.pallas_call(
        body, grid=(ADAMW_STEPS,), name=name, in_specs=in_specs, out_specs=out_specs,
        out_shape=[jax.ShapeDtypeStruct(updates[n][2].shape, F32) for n in names for _ in range(4)],
        compiler_params=_cparams(56))(*[a for n in names for a in updates[n]])
    return {n: out[4 * i:4 * i + 4] for i, n in enumerate(names)}


def _place():
    x, y, c = lax.axis_index("x"), lax.axis_index("y"), lax.axis_index("c")
    chips = [(1 - x, y), (x, 1 - y), (1 - x, 1 - y)]
    return x, y, c, chips


def _gather_steps(ins, outs, send, recv, lsem, own_barrier=True):
    nt = len(ins)
    x, y, c, (xn, yn, diag) = _place()
    me, sib = (x, y, c), (x, y, 1 - c)

    def slot(t, px, py, pc):
        return outs[t].at[4 * px + 2 * py + pc]

    def copy(t, k, block, to, src=None):
        return pltpu.make_async_remote_copy(
            src_ref=slot(t, *block) if src is None else src, dst_ref=slot(t, *block),
            send_sem=send.at[t, k], recv_sem=recv.at[t, k], device_id=to, device_id_type=MESH)

    mine = [pltpu.make_async_copy(ins[t], slot(t, *me), lsem.at[t]) for t in range(nt)]
    first = [copy(t, k, me, to, src=ins[t]) for t in range(nt) for k, to in ((0, sib), (1, (*xn, c)), (2, (*yn, c)))]

    if own_barrier:
        signal_peers, peers_are_in = _own_barrier([sib, (*xn, c), (*yn, c)])

    def enter():
        if own_barrier:
            signal_peers()
        for cp in mine:
            cp.start()

    def start():
        if own_barrier:
            peers_are_in()
        for cp in first:
            cp.start()

    def landed(k, chip, also_to=None):
        for t in range(nt):
            copy(t, k, (*chip, c), me).wait_recv()
            if also_to is not None:
                copy(t, 3, (*chip, c), (*also_to, c)).start()
            copy(t, 3 + k, (*chip, c), sib).start()

    def relay():
        @pl.when(c == 0)
        def _():
            landed(1, xn, also_to=yn)
            landed(2, yn)

        @pl.when(c == 1)
        def _():
            landed(2, yn, also_to=xn)
            landed(1, xn)

    def finish():
        landed(3, diag)
        for t in range(nt):
            copy(t, 0, sib, me).wait_recv()
            for k, chip in ((4, xn), (5, yn), (6, diag)):
                copy(t, k, (*chip, 1 - c), me).wait_recv()
            for k in range(7):
                copy(t, k, me, sib).wait_send()
        for cp in mine:
            cp.wait()

    return enter, start, relay, finish


def _gather_scratch(nt):
    return [pltpu.SemaphoreType.DMA((nt, 7)), pltpu.SemaphoreType.DMA((nt, 7)), pltpu.SemaphoreType.DMA((nt,))]


def _gathered_shapes(shards):
    return [jax.ShapeDtypeStruct((N_DEV,) + s.shape, s.dtype) for s in shards]


def _call_with_gather(body, n_grid, shards, *, name, in_specs, out_specs, out_shape, scratch_shapes, vmem_mb, args,
                      collective_id=None):
    assert (collective_id is None) == (not shards)
    ng, n_in, n_out = len(shards), len(in_specs), len(out_specs)

    def wrapped(*refs):
        ins, shard_refs = refs[:n_in], refs[n_in:n_in + ng]
        outs = refs[n_in + ng:n_in + ng + n_out]
        whole_refs = refs[n_in + ng + n_out:n_in + 2 * ng + n_out]
        scratch = refs[n_in + 2 * ng + n_out:]
        if ng:
            enter, start, relay, finish = _gather_steps(shard_refs, whole_refs, *scratch[len(scratch_shapes):])
            pl.when(pl.program_id(0) == 0)(enter)
            pl.when(pl.program_id(0) == 0)(start)
            pl.when(pl.program_id(0) == n_grid // 2)(relay)
        body(*ins, *outs, *scratch[:len(scratch_shapes)])
        if ng:
            pl.when(pl.program_id(0) == n_grid - 1)(finish)

    return pl.pallas_call(
        wrapped, grid=(n_grid,), name=name,
        in_specs=list(in_specs) + [ANY] * ng, out_specs=list(out_specs) + [ANY] * ng,
        out_shape=list(out_shape) + _gathered_shapes(shards),
        scratch_shapes=list(scratch_shapes) + (_gather_scratch(ng) if ng else []),
        compiler_params=_cparams(vmem_mb, **({"collective_id": collective_id} if shards else {})))(*args, *shards)


def _chips_steps(ins, outs, send, recv):
    _, _, c, chips = _place()
    copies = [pltpu.make_async_remote_copy(
        src_ref=ins[t].at[2 * px + py], dst_ref=outs[t].at[j], send_sem=send.at[t, j], recv_sem=recv.at[t, j],
        device_id=(px, py, c), device_id_type=MESH) for t in range(len(ins)) for j, (px, py) in enumerate(chips)]

    def start():
        for cp in copies:
            cp.start()

    def finish():
        for cp in copies:
            cp.wait()

    return start, finish


def _chips_scratch(nt):
    return [pltpu.SemaphoreType.DMA((nt, 3)), pltpu.SemaphoreType.DMA((nt, 3))]


def _chips_shapes(cs16s):
    return [jax.ShapeDtypeStruct((3,) + g.shape[1:], g.dtype) for g in cs16s]


SMALL = (("g_pre_mix", 0, 0, D), ("g_mem", 1, 0, D), ("g_post_mix", 2, 0, D), ("g_attn_out", 3, 0, AW),
         ("g_conv_out", 3, AW, CW), ("g_xattn_out", 3, AW + CW, XW), ("g_post_mlp", 4, 0, D), ("g_pre_mlp", 5, 0, D))
CONV_ROW = 8
PACK_ROWS = 16


LOSS_ROW = 15


def _small_reduce_steps(accs, tot_ref, pack, land, send, recv):
    acc_in, acc_mem, acc_mix, acc_mlp, acc_cw, acc_loss = accs
    x, y, c, _ = _place()
    me = 4 * x + 2 * y + c
    copies = []
    for k in range(1, N_DEV):
        kx, ky, kc = (k >> 2) & 1, (k >> 1) & 1, k & 1
        peer = (1 - x if kx else x, 1 - y if ky else y, 1 - c if kc else c)
        copies.append(pltpu.make_async_remote_copy(
            src_ref=pack, dst_ref=land.at[me], send_sem=send.at[k - 1], recv_sem=recv.at[k - 1],
            device_id=peer, device_id_type=MESH))

    def start():
        pack[...] = jnp.zeros_like(pack)
        pack[0:1, :] = acc_in[0:1, :]
        pack[1:2, :] = acc_mem[0:1, :]
        pack[2:4, :] = acc_mix[0:2, :]
        pack[4:6, :] = acc_mlp[0:2, :]
        pack[CONV_ROW:CONV_ROW + 3, 0:CW] = acc_cw[0:3, :]
        pack[LOSS_ROW:LOSS_ROW + 1, 0:LANES] = acc_loss[0:1, :]
        land[me] = pack[...]
        for cp in copies:
            cp.start()

    def finish():
        for cp in copies:
            cp.wait()
        tot = land[0]
        for s in range(1, N_DEV):
            tot = tot + land[s]
        tot_ref[...] = tot

    return start, finish


def _small_reduce_scratch():
    return [pltpu.VMEM((PACK_ROWS, D), F32), pltpu.VMEM((N_DEV, PACK_ROWS, D), F32),
            pltpu.SemaphoreType.DMA((N_DEV - 1,)), pltpu.SemaphoreType.DMA((N_DEV - 1,))]


def _small_update(tot, me, params):
    flat = [a for n, _, _, _ in SMALL for a in params[n]] + list(params["conv_w"])
    n_par = len(SMALL) + 1
    tap_cols = CW // N_DEV

    def body(*refs):
        me_ref, tot_ref = refs[0:2]
        ins = refs[2:2 + 3 * n_par]
        loss_out = refs[2 + 3 * n_par]
        outs = refs[3 + 3 * n_par:]
        tot = tot_ref[...]
        loss_out[...] = jnp.broadcast_to(tot[LOSS_ROW:LOSS_ROW + 1, 0:LANES], loss_out.shape)

        def update(i, g):
            w_ref, m_ref, v_ref = ins[3 * i:3 * i + 3]
            for o_ref, res in zip(outs[4 * i:4 * i + 4], (g,) + _adamw_math(w_ref[...], g, m_ref[...], v_ref[...])):
                if len(o_ref.shape) == 3:
                    for t in range(o_ref.shape[0]):
                        o_ref[t] = res[t:t + 1, :]
                else:
                    o_ref[...] = res

        for i, (_, row, lane0, width) in enumerate(SMALL):
            update(i, tot[row:row + 1, lane0:lane0 + width])
        me = me_ref[0]
        taps = pltpu.roll(tot[CONV_ROW:CONV_ROW + SUBLANES, 0:CW], jnp.where(me == 0, 0, CW - me * tap_cols), 1)
        update(n_par - 1, taps[0:3, 0:tap_cols])

    shapes = [jax.ShapeDtypeStruct(params[n][0].shape, F32) for n, _, _, _ in SMALL] + [
        jax.ShapeDtypeStruct((3, 1, tap_cols), F32)]
    vmem = pl.BlockSpec(memory_space=pltpu.VMEM)
    loss, *out = pl.pallas_call(
        body, name="small_update",
        in_specs=[pl.BlockSpec(memory_space=pltpu.SMEM)] + [vmem] * (1 + 3 * n_par),
        out_shape=[jax.ShapeDtypeStruct((SUBLANES, LANES), F32)] + [s for s in shapes for _ in range(4)],
    )(me, tot, *flat)
    names = [n for n, _, _, _ in SMALL] + ["conv_w"]
    return loss[0, 0], {n: out[4 * i:4 * i + 4] for i, n in enumerate(names)}


def _local_step(x, mem, pos, gains, shards, tgt, place):
    half = HEAD // 2
    inv_freq = jnp.float32(ROPE_THETA) ** (-(jnp.arange(half, dtype=F32) * 2.0 / HEAD))
    invf = jnp.tile(inv_freq, LANES // half)[None, :]
    sgn = jnp.tile(jnp.concatenate([-jnp.ones((half,), F32), jnp.ones((half,), F32)]), LANES // HEAD)[None, :]
    cos, sins, win8 = _rope_table(pos.astype(F32).reshape(S, 1), invf, sgn, [shards["w_in"]])
    wdn_left, wdn_right = shards["w_down"][:, 0:D // 2], shards["w_down"][:, D // 2:]
    q, kvp, bcu, qx16, ht16, win16, wout8, wkv8, conv8, wdn8_right = _in_proj(
        x, gains["g_pre_mix"], win8, cos, sins, [shards["w_out"], shards["w_mem_kv"], shards["conv_w"], wdn_right])
    wout16, wkv16 = wout8.reshape(D, D), wkv8.reshape(D, 2 * XW)
    cw_full = conv8[:, 0:3, 0:CW // N_DEV].transpose(1, 0, 2).reshape(3, CW)
    cw8 = jnp.zeros((SUBLANES, CW), F32).at[0:3].set(cw_full)
    y_attn, ltot, wup8, wdn8_left = _attn_fwd(q, kvp, [shards["w_up"], wdn_left])
    wdn_halves = (wdn8_left.reshape(FF, D // 2), wdn8_right.reshape(FF, D // 2))
    memnt16, kv16 = _mem_fwd(mem, gains["g_mem"], wkv16)
    ypre, yt16, y2, x1 = _mix_out(y_attn, bcu, qx16, kv16, cw8, gains["g_attn_out"], gains["g_conv_out"],
                                 gains["g_xattn_out"], gains["g_post_mix"], wout16, x, [])
    a16, du16, h2t16, df2t16, dx1, loss8, dg_mlp = _mlp(
        x1, tgt, gains["g_pre_mlp"], gains["g_post_mlp"], wup8, wdn_halves)

    sums = {"w_up": _wgrad_cols(place, h2t16, du16, FF_BLK, "wgrad_up", ID_WGRAD_UP),
            "w_down": _wgrad_cols(place, df2t16, a16, FF_BLK, "wgrad_down", ID_WGRAD_DOWN, square_b=True,
                                  transpose_out=True)}

    head_id = jnp.arange(AW, dtype=jnp.int32) // HEAD
    head_ones = (head_id[:, None] == head_id[None, :]).astype(BF16)
    dy2_16, qdo, ld, dbcu, dqx, dgs, dcw, dkv = _mix_out_bwd(
        dx1, y2, ypre, y_attn, ltot, head_ones, q, bcu, qx16, kv16, cw8, gains["g_post_mix"], gains["g_attn_out"],
        gains["g_conv_out"], gains["g_xattn_out"], wout16)
    dkv16, dg_mem = _mem_bwd(mem, gains["g_mem"], wkv16, dkv)
    sums["w_mem_kv"], sums["w_out"] = _wgrad_rows(place, [(memnt16, dkv16), (yt16, dy2_16)], "wgrad_mem_kv_out")
    out = _attn_bwd(qdo, kvp, ld, [s[0] for s in sums.values()])
    dqkv, landed = out[:9], out[9:]
    reduced = {n: (s[1], landed[t]) for t, (n, s) in enumerate(sums.items())}
    dproj16, grad_x, dg_in = _in_proj_bwd(dqkv, dbcu, dqx, cos, sins, win16, x, gains["g_pre_mix"], dx1)

    _, in_own, in_landed, small_tot = _wgrad_cols(place, ht16, dproj16, PW // N_DEV, "wgrad_in", ID_WGRAD_IN,
                                                  transpose_out=True, to_chips=True,
                                                  small=(dg_in, dg_mem, dgs, dg_mlp, dcw, loss8))
    reduced["w_in"] = (in_own, in_landed)
    return grad_x, reduced, small_tot


BIG = ("w_in", "w_mem_kv", "w_out", "w_up", "w_down")
ORDER = ("g_pre_mix", "g_mem", "w_in", "w_mem_kv", "conv_w", "g_attn_out", "g_conv_out", "g_xattn_out", "w_out",
         "g_post_mix", "g_pre_mlp", "w_up", "w_down", "g_post_mlp")


def kernel(x, mem, positions, g_pre_mix, g_mem, w_in, w_mem_kv, conv_w, g_attn_out, g_conv_out, g_xattn_out, w_out, g_post_mix, g_pre_mlp, w_up, w_down, g_post_mlp, loss_target, m_g_pre_mix, m_g_mem, m_w_in, m_w_mem_kv, m_conv_w, m_g_attn_out, m_g_conv_out, m_g_xattn_out, m_w_out, m_g_post_mix, m_g_pre_mlp, m_w_up, m_w_down, m_g_post_mlp, v_g_pre_mix, v_g_mem, v_w_in, v_w_mem_kv, v_conv_w, v_g_attn_out, v_g_conv_out, v_g_xattn_out, v_w_out, v_g_post_mix, v_g_pre_mlp, v_w_up, v_w_down, v_g_post_mlp):
    w = dict(g_pre_mix=g_pre_mix, g_mem=g_mem, w_in=w_in, w_mem_kv=w_mem_kv, conv_w=conv_w, g_attn_out=g_attn_out,
             g_conv_out=g_conv_out, g_xattn_out=g_xattn_out, w_out=w_out, g_post_mix=g_post_mix, g_pre_mlp=g_pre_mlp,
             w_up=w_up, w_down=w_down, g_post_mlp=g_post_mlp)
    mo = dict(g_pre_mix=m_g_pre_mix, g_mem=m_g_mem, w_in=m_w_in, w_mem_kv=m_w_mem_kv, conv_w=m_conv_w,
              g_attn_out=m_g_attn_out, g_conv_out=m_g_conv_out, g_xattn_out=m_g_xattn_out, w_out=m_w_out,
              g_post_mix=m_g_post_mix, g_pre_mlp=m_g_pre_mlp, w_up=m_w_up, w_down=m_w_down, g_post_mlp=m_g_post_mlp)
    vo = dict(g_pre_mix=v_g_pre_mix, g_mem=v_g_mem, w_in=v_w_in, w_mem_kv=v_w_mem_kv, conv_w=v_conv_w,
              g_attn_out=v_g_attn_out, g_conv_out=v_g_conv_out, g_xattn_out=v_g_xattn_out, w_out=v_w_out,
              g_post_mix=v_g_post_mix, g_pre_mlp=v_g_pre_mlp, w_up=v_w_up, w_down=v_w_down, g_post_mlp=v_g_post_mlp)

    xi, yi, ci = lax.axis_index("x"), lax.axis_index("y"), lax.axis_index("c")
    me = 4 * xi + 2 * yi + ci
    place = jnp.stack([ci, 2 * xi + yi]).astype(jnp.int32)

    shards = {n: w[n][0].astype(BF16) for n in BIG}
    shards["conv_w"] = jnp.zeros((SUBLANES, LANES), F32).at[0:3, 0:CW // N_DEV].set(conv_w[0])

    gains = {n: w[n] for n, _, _, _ in SMALL}
    grad_x, reduced, small_tot = _local_step(x[0], mem[0], positions[0], gains, shards, loss_target[0], place)

    def shard(n, a):
        return a[0].T if n == "w_in" else a[0]

    updated = _adamw_shards({n: (*reduced[n], shard(n, w[n]), shard(n, mo[n]), shard(n, vo[n])) for n in BIG},
                            "adamw")
    grad, delta, new_m, new_v = {}, {}, {}, {}
    for n, res in updated.items():
        grad[n], delta[n], new_m[n], new_v[n] = [(a.T if n == "w_in" else a)[None] for a in res]

    params = {n: (w[n], mo[n], vo[n]) for n, _, _, _ in SMALL}
    params["conv_w"] = (w["conv_w"][0], mo["conv_w"][0], vo["conv_w"][0])
    loss, small = _small_update(small_tot, me.reshape(1).astype(jnp.int32), params)
    for n, (g, d_, m_, v_) in small.items():
        lead = (lambda a: a.reshape(conv_w.shape)) if n == "conv_w" else (lambda a: a)
        grad[n], delta[n], new_m[n], new_v[n] = lead(g), lead(d_), lead(m_), lead(v_)

    return (loss, grad_x[None], *[grad[n] for n in ORDER], *[delta[n] for n in ORDER],
            *[new_m[n] for n in ORDER], *[new_v[n] for n in ORDER])
```

```python
import jax
import jax.numpy as jnp
from jax import lax
from jax.experimental import pallas as pl
from jax.experimental.pallas import tpu as pltpu

F32, BF16 = jnp.float32, jnp.bfloat16
MESH = pl.DeviceIdType.MESH
ANY = pl.BlockSpec(memory_space=pl.ANY)

N_DEV = 8
D = 1024
S = 4096
N_MEM = 256
HEAD = 64
AW, CW, XW = 512, 256, 256
PW = 3 * AW + 3 * CW + XW
FF = 4096
FF_BLK = FF // N_DEV
EPS = 1e-6
NEG = -1e30
SCALE = HEAD ** -0.5
ROPE_THETA = 10000.0
LANES = 128
SUBLANES = 8

ADAM_LR, ADAM_B1, ADAM_B2, ADAM_EPS, ADAM_WD, ADAM_STEP = 0.001, 0.9, 0.999, 1e-08, 0.01, 10

TQ = 512
TQ_MLP = 512
NT = S // TQ


def _cparams(vmem_mb, n_grid=1, **more):
    return pltpu.CompilerParams(dimension_semantics=("arbitrary",) * n_grid, vmem_limit_bytes=vmem_mb << 20, **more)


def _const(shape):
    nd = len(shape)
    return pl.BlockSpec(shape, lambda *_: (0,) * nd, pipeline_mode=pl.Buffered(1))


def _acc(shape):
    nd = len(shape)
    return pl.BlockSpec(shape, lambda *_: (0,) * nd)


def _tokens_in_lanes(tq):
    return pl.BlockSpec((D, tq), lambda i: (0, i))


def _dot(a, b):
    return jnp.dot(a, b, preferred_element_type=F32)


def _dot_nt(a, b):
    return lax.dot_general(a, b, (((1,), (1,)), ((), ())), preferred_element_type=F32)


def _dot_tn(a, b):
    return lax.dot_general(a, b, (((0,), (0,)), ((), ())), preferred_element_type=F32)


def _rms(x, g):
    r = lax.rsqrt(jnp.mean(x * x, axis=-1, keepdims=True) + EPS)
    n = x * r
    return n * g, n, r


def _rms_bwd(dy, n, r, g):
    dn = dy * g
    dx = r * (dn - n * jnp.mean(dn * n, axis=-1, keepdims=True))
    return dx, jnp.sum(dy * n, axis=0, keepdims=True)


def _rot_half(t):
    lane = lax.broadcasted_iota(jnp.int32, t.shape, 1)
    n = t.shape[1]
    return jnp.where((lane % HEAD) < HEAD // 2, pltpu.roll(t, n - HEAD // 2, 1), pltpu.roll(t, HEAD // 2, 1))


def _rope_table(pos_col, invf, sgn, shards):
    def body(p_ref, f_ref, s_ref, c_out, s_out):
        ang = p_ref[...] * f_ref[...]
        c_out[...] = jnp.cos(ang)
        s_out[...] = jnp.sin(ang) * s_ref[...]

    tile = pl.BlockSpec((TQ, LANES), lambda i: (i, 0))
    return _call_with_gather(
        body, NT, shards, name="rope_table",
        in_specs=[pl.BlockSpec((TQ, 1), lambda i: (i, 0)), _const((1, LANES)), _const((1, LANES))],
        out_specs=[tile, tile], out_shape=[jax.ShapeDtypeStruct((S, LANES), F32)] * 2,
        scratch_shapes=[], vmem_mb=32, args=(pos_col, invf, sgn), collective_id=ID_ROPE_TABLE)


def _all_heads(t):
    return jnp.tile(t, (1, AW // LANES))


def _mem_fwd(mem, g_mem, wkv16):
    def body(m_ref, g_ref, w_ref, n16_ref, kv_ref):
        y, _, _ = _rms(m_ref[...], g_ref[...])
        y16 = y.astype(BF16)
        n16_ref[...] = y16.T
        kv_ref[...] = _dot(y16, w_ref[...]).astype(BF16)

    return pl.pallas_call(
        body, name="mem_fwd",
        out_shape=[jax.ShapeDtypeStruct((D, N_MEM), BF16), jax.ShapeDtypeStruct((N_MEM, 2 * XW), BF16)],
        compiler_params=pltpu.CompilerParams(vmem_limit_bytes=32 << 20))(mem, g_mem, wkv16)


def _in_proj(x, g, w8, cos, sins, shards):
    blk = PW // N_DEV

    def body(x_ref, g_ref, w8_ref, c_ref, s_ref, q_ref, kv_ref, bcu_ref, qx_ref, h_ref, w_out, w_ref):
        @pl.when(pl.program_id(0) == 0)
        def _():
            for j in range(N_DEV):
                w_ref[:, j * blk:(j + 1) * blk] = w8_ref[j]
            w_out[...] = w_ref[...]

        y, _, _ = _rms(x_ref[...], g_ref[...])
        h = y.astype(BF16)
        h_ref[...] = h.T
        proj = _dot(h, w_ref[...])
        cos, sn = _all_heads(c_ref[...]), _all_heads(s_ref[...])
        q, k = proj[:, 0:AW], proj[:, AW:2 * AW]
        q_ref[...] = (q * cos + _rot_half(q) * sn) * SCALE
        kv_ref[...] = _pack_pair(k * cos + _rot_half(k) * sn, proj[:, 2 * AW:3 * AW])
        bcu_ref[...] = proj[:, 3 * AW:3 * AW + 3 * CW]
        qx_ref[...] = (proj[:, 3 * AW + 3 * CW:] * SCALE).astype(BF16)

    def tile(w):
        return pl.BlockSpec((TQ, w), lambda i: (i, 0))

    return _call_with_gather(
        body, NT, shards, name="in_proj",
        in_specs=[tile(D), _const((1, D)), _const((N_DEV, D, blk)), tile(LANES), tile(LANES)],
        out_specs=[tile(AW), tile(AW), tile(3 * CW), tile(XW), _tokens_in_lanes(TQ), _acc((D, PW))],
        out_shape=[jax.ShapeDtypeStruct((S, AW), F32)] * 2 + [
            jax.ShapeDtypeStruct((S, 3 * CW), F32), jax.ShapeDtypeStruct((S, XW), BF16),
            jax.ShapeDtypeStruct((D, S), BF16), jax.ShapeDtypeStruct((D, PW), BF16)],
        scratch_shapes=[pltpu.VMEM((D, PW), BF16)], vmem_mb=56, args=(x, g, w8, cos, sins),
        collective_id=ID_IN_PROJ)


ATTN_PLANS = (("p1", 1, 128, 32), ("p4", 8, 64, 8), ("p16", 16, 128, 2))
PAD = 128
WIN = 256


ATTN_UNROLL = 16


def _fill_bias(tab, qblk, partner):
    qi = lax.broadcasted_iota(jnp.int32, (2 * qblk, WIN), 0) & (qblk - 1)
    kj = lax.broadcasted_iota(jnp.int32, (2 * qblk, WIN), 1)
    piece = kj >> (qblk.bit_length() - 1)
    kk = kj & (qblk - 1)
    prev = (piece & 1) == 0
    of_partner = piece >= 2
    for first in (0, 1):
        for par in (0, 1):
            lo = jnp.where(prev, (qblk if first else qi) + jnp.where(of_partner, par, 0), 0)
            hi = jnp.where(prev, qblk, qi + jnp.where(of_partner, par - 1, 0))
            tab[2 * first + par] = jnp.where((kk >= lo) & (kk <= hi), 0.0, NEG).astype(F32)


def _block_rows(g, qblk, nbc, partner):
    own = pl.ds(pl.multiple_of(PAD + g * qblk, qblk), qblk)
    first = ((g & (nbc - 1)) == 0).astype(jnp.int32)
    if partner:
        gp = jnp.bitwise_xor(g, 4 * nbc)
        wins = (pl.ds(pl.multiple_of(PAD + (g - 1) * qblk, qblk), 2 * qblk),
                pl.ds(pl.multiple_of(PAD + (gp - 1) * qblk, qblk), 2 * qblk))
        return own, wins, 2 * first + ((g >> ((4 * nbc).bit_length() - 1)) & 1)
    return own, (pl.ds(pl.multiple_of(PAD + (g - 1) * qblk, qblk), 2 * qblk),), 2 * first


def _pack_pair(lo, hi):
    lo_bits = lax.bitcast_convert_type(lo.astype(BF16).astype(F32), jnp.uint32) >> 16
    hi_bits = lax.bitcast_convert_type(hi.astype(BF16).astype(F32), jnp.uint32) & jnp.uint32(0xFFFF0000)
    return lax.bitcast_convert_type(hi_bits | lo_bits, F32)


def _unpack_pair(c):
    bits = lax.bitcast_convert_type(c, jnp.uint32)
    lo = lax.bitcast_convert_type(bits << 16, F32).astype(BF16)
    hi = lax.bitcast_convert_type(bits & jnp.uint32(0xFFFF0000), F32).astype(BF16)
    return lo, hi


def _window(ref, wins):
    parts = [ref[w, :] for w in wins]
    return parts[0] if len(parts) == 1 else jnp.concatenate(parts, axis=0)


def _stack_heads(t, lane):
    zero = jnp.zeros_like(t)
    return jnp.concatenate([jnp.where(lane < HEAD, t, zero), jnp.where(lane >= HEAD, t, zero)], axis=0)


def _unstack_heads(t2, lane):
    half = t2.shape[0] // 2
    return jnp.where(lane < HEAD, t2[0:half, :], t2[half:, :])


def _lanes_of(step):
    return pl.ds(pl.multiple_of(step * LANES, LANES), LANES)


def _whole_wait(buf, sem):
    whole = buf.at[pl.ds(PAD, S), :]
    return pltpu.make_async_copy(whole, whole, sem)


def _whole_waits(bufs, sems):
    return [_whole_wait(buf, sems.at[i]) for i, buf in enumerate(bufs)]


def _class_gather(views, bufs, sems, lanes):
    copies = []
    for i, (view, buf) in enumerate(zip(views, bufs)):
        if view.ndim == 2:
            copies.append(pltpu.make_async_copy(view.at[:, lanes], buf.at[pl.ds(PAD, S), :], sems.at[i]))
        else:
            per, n_cls = view.shape[0], view.shape[1]
            copies += [pltpu.make_async_copy(view.at[:, c, lanes], buf.at[pl.ds(PAD + c * per, per), :], sems.at[i])
                       for c in range(n_cls)]
    return copies


def _class_scatter(bufs, dsts, sems, lanes):
    copies = []
    for i, (buf, dst) in enumerate(zip(bufs, dsts)):
        if dst.ndim == 2:
            copies.append(pltpu.make_async_copy(buf.at[pl.ds(PAD, S), :], dst.at[:, lanes], sems.at[i]))
            continue
        per, n_cls = dst.shape[0], dst.shape[1]
        copies += [pltpu.make_async_copy(buf.at[pl.ds(PAD + c * per, per), :], dst.at[:, c, lanes], sems.at[i])
                   for c in range(n_cls)]
    return copies


def _start(copies):
    for cp in copies:
        cp.start()


def _wait(waits):
    for w in waits:
        w.wait()


def _attn_fwd(q, kvp, shards=()):
    views = [[a] + [a.reshape(S // n, n, AW) for _, n, _, _ in ATTN_PLANS[1:]] for a in (q, kvp)]
    flat = [views[a][p] for p in range(3) for a in range(2)]
    ng = len(shards)
    n_grid = AW // LANES

    def body(*refs):
        hbm = [refs[2 * p:2 * p + 2] for p in range(3)]
        refs = refs[6:]
        shard_refs, refs = refs[:ng], refs[ng:]
        y_ref, lt_ref = refs[0:2]
        whole_refs, refs = refs[2:2 + ng], refs[2 + ng:]
        bufs = [refs[2 * p:2 * p + 2] for p in range(3)]
        oc4, lc4, oc16, lc16, tab128, tab4, sem_in = refs[6:13]
        step = pl.program_id(0)
        if ng:
            enter_gather, start_gather, relay_gather, finish_gather = _gather_steps(
                shard_refs, whole_refs, *refs[13:], own_barrier=False)
            pl.when(step == 0)(enter_gather)
            pl.when(step == 0)(start_gather)
            pl.when(step == n_grid // 2)(relay_gather)
        now = [_class_gather(hbm[p], bufs[p], sem_in.at[p], _lanes_of(step)) for p in range(3)]
        nxt = [_class_gather(hbm[p], bufs[p], sem_in.at[p], _lanes_of(step + 1)) for p in range(3)]

        @pl.when(step == 0)
        def _():
            for p in range(3):
                _start(now[p])
                for b in bufs[p]:
                    b[0:PAD, :] = jnp.zeros((PAD, LANES), F32)
            _fill_bias(tab128, 128, False)
            _fill_bias(tab4, 64, True)

        def prefetch(p):
            pl.when(step + 1 < n_grid)(lambda: _start(nxt[p]))

        lane = lax.broadcasted_iota(jnp.int32, (1, LANES), 1)
        ones = jnp.ones((WIN, LANES), BF16)

        def run(plan, bq, bkv, tab, o_dst, l_dst, dst_pad):
            _, n_cls, qblk, nbc = plan
            partner = n_cls == 8

            def block(g, carry):
                own, wins, mask = _block_rows(g, qblk, nbc, partner)
                q2 = _stack_heads(bq[own, :].astype(BF16), lane)
                kw, vwin = _unpack_pair(_window(bkv, wins))
                vw = jnp.concatenate([vwin, ones], axis=1)
                s = _dot_nt(q2, kw) + tab[mask]
                m = jnp.max(s, axis=1, keepdims=True)
                oe = _dot(jnp.exp(s - m).astype(BF16), vw)
                den = oe[:, LANES:]
                dst = pl.ds(pl.multiple_of(dst_pad + g * qblk, qblk), qblk)
                o_dst[dst, :] = _unstack_heads(oe[:, 0:LANES] / den, lane)
                l_dst[dst, :] = _unstack_heads(m + jnp.log(den), lane)
                return carry
            lax.fori_loop(0, n_cls * nbc, block, 0, unroll=ATTN_UNROLL)

        _wait(_whole_waits(bufs[0], sem_in.at[0]))
        run(ATTN_PLANS[0], *bufs[0], tab128, y_ref, lt_ref, 0)
        prefetch(0)
        _wait(_whole_waits(bufs[1], sem_in.at[1]))
        run(ATTN_PLANS[1], *bufs[1], tab4, oc4, lc4, PAD)
        prefetch(1)
        _wait(_whole_waits(bufs[2], sem_in.at[2]))
        run(ATTN_PLANS[2], *bufs[2], tab128, oc16, lc16, PAD)
        prefetch(2)

        n_rows = 64

        def token_order(buf, t, n_cls):
            per = S // n_cls
            first = PAD + t * (n_rows // n_cls)
            return jnp.concatenate([buf[pl.ds(first + jj, n_cls, stride=per), :] for jj in range(n_rows // n_cls)],
                                   axis=0)

        def combine(t, carry):
            rows = pl.ds(pl.multiple_of(t * n_rows, n_rows), n_rows)
            l0, l1, l2 = lt_ref[rows, :], token_order(lc4, t, 8), token_order(lc16, t, 16)
            lm = jnp.maximum(jnp.maximum(l0, l1), l2)
            e0, e1, e2 = jnp.exp(l0 - lm), jnp.exp(l1 - lm), jnp.exp(l2 - lm)
            den = e0 + e1 + e2
            y_ref[rows, :] = (e0 * y_ref[rows, :] + e1 * token_order(oc4, t, 8)
                              + e2 * token_order(oc16, t, 16)) / den
            lt_ref[rows, :] = lm + jnp.log(den)
            return carry
        lax.fori_loop(0, S // n_rows, combine, 0, unroll=2)

        if ng:
            pl.when(step == n_grid - 1)(finish_gather)

    col = pl.BlockSpec((S, LANES), lambda h: (0, h))
    padded = pltpu.VMEM((PAD + S, LANES), F32)
    return pl.pallas_call(
        body, grid=(n_grid,), name="attn_fwd",
        in_specs=[ANY] * (6 + ng), out_specs=[col, col] + [ANY] * ng,
        out_shape=[jax.ShapeDtypeStruct((S, AW), F32)] * 2 + _gathered_shapes(shards),
        scratch_shapes=[padded] * 10 + [
            pltpu.VMEM((4, 256, WIN), F32), pltpu.VMEM((4, 128, WIN), F32), pltpu.SemaphoreType.DMA((3, 2))]
        + (_gather_scratch(ng) if ng else []),
        compiler_params=_cparams(56))(*flat, *shards)


def _conv_taps(z, zprev, row):
    z1 = jnp.where(row == 0, zprev[7:8, :], pltpu.roll(z, 1, 0))
    z2 = jnp.where(row == 0, zprev[6:7, :], jnp.where(row == 1, zprev[7:8, :], pltpu.roll(z, 2, 0)))
    return z1, z2


def _xattn_scores(qm, km):
    s = _dot_nt(qm, km)
    m = jnp.max(s, axis=1, keepdims=True)
    e = jnp.exp(s - m)
    return e, jnp.sum(e, axis=1, keepdims=True)


def _mix_out(y_attn, bcu, qx16, kv16, cw8, g_attn, g_conv, g_x, g_post, wout16, x, shards):
    def body(ya_ref, bcu_ref, halo_ref, qx_ref, kv_ref, cw_ref, ga_ref, gc_ref, gx_ref, gp_ref, w_ref, x_ref,
             ypre_ref, y16_ref, y2_ref, x1_ref):
        i = pl.program_id(0)
        bcu = bcu_ref[...]
        b, c, u = bcu[:, 0:CW], bcu[:, CW:2 * CW], bcu[:, 2 * CW:]
        z = c * u
        halo = halo_ref[...]
        zprev = jnp.where(i > 0, halo[:, CW:2 * CW] * halo[:, 2 * CW:], 0.0)
        row = lax.broadcasted_iota(jnp.int32, z.shape, 0)
        z1, z2 = _conv_taps(z, zprev, row)
        cw = cw_ref[...]
        y_conv = b * (z2 * cw[0:1, :] + z1 * cw[1:2, :] + z * cw[2:3, :])

        qx = qx_ref[...]
        kv = kv_ref[...]
        km, vm = kv[:, 0:XW], kv[:, XW:]
        lane = lax.broadcasted_iota(jnp.int32, qx.shape, 1)
        y_x = jnp.zeros(qx.shape, F32)
        for h in range(XW // HEAD):
            hm = (lane >= h * HEAD) & (lane < (h + 1) * HEAD)
            e, l = _xattn_scores(jnp.where(hm, qx, jnp.zeros_like(qx)), km)
            y_x = jnp.where(hm, _dot(e.astype(BF16), vm) / l, y_x)

        y_attn = ya_ref[...]
        ypre_ref[:, 0:CW] = y_conv
        ypre_ref[:, CW:] = y_x
        y = jnp.concatenate([_rms(y_attn, ga_ref[...])[0], _rms(y_conv, gc_ref[...])[0],
                             _rms(y_x, gx_ref[...])[0]], axis=1).astype(BF16)
        y16_ref[...] = y.T
        y2 = _dot(y, w_ref[...])
        y2_ref[...] = y2
        x1_ref[...] = x_ref[...] + _rms(y2, gp_ref[...])[0]

    def tile(w):
        return pl.BlockSpec((TQ, w), lambda i: (i, 0))

    halo = pl.BlockSpec((SUBLANES, 3 * CW), lambda i: (jnp.maximum(i * (TQ // SUBLANES) - 1, 0), 0))
    return _call_with_gather(
        body, NT, shards, name="mix_out",
        in_specs=[tile(AW), tile(3 * CW), halo, tile(XW), _const((N_MEM, 2 * XW)), _const((SUBLANES, CW)),
                  _const((1, AW)), _const((1, CW)), _const((1, XW)), _const((1, D)), _const((D, D)), tile(D)],
        out_specs=[tile(CW + XW), _tokens_in_lanes(TQ), tile(D), tile(D)],
        out_shape=[jax.ShapeDtypeStruct((S, CW + XW), F32), jax.ShapeDtypeStruct((D, S), BF16),
                   jax.ShapeDtypeStruct((S, D), F32), jax.ShapeDtypeStruct((S, D), F32)],
        scratch_shapes=[], vmem_mb=56,
        args=(y_attn, bcu, bcu, qx16, kv16, cw8, g_attn, g_conv, g_x, g_post, wout16, x))


def _mlp(x1, tgt, g_pre, g_post, wup8, wdn_halves):
    tq = TQ_MLP
    half = D // 2

    def body(x1_ref, t_ref, g1_ref, g2_ref, wu_ref, wda_ref, wdb_ref,
             a16_ref, du_ref, h2_ref, df2_ref, dx1_ref, loss_ref, dg_ref):
        @pl.when(pl.program_id(0) == 0)
        def _():
            loss_ref[...] = jnp.zeros_like(loss_ref)
            dg_ref[...] = jnp.zeros_like(dg_ref)

        x1 = x1_ref[...]
        g1, g2 = g1_ref[...], g2_ref[...]
        y1, n1, r1 = _rms(x1, g1)
        h2 = y1.astype(BF16)
        h2_ref[...] = h2.T
        f2a = jnp.zeros((tq, half), F32)
        f2b = jnp.zeros((tq, half), F32)
        for j in range(N_DEV):
            cols = slice(j * FF_BLK, (j + 1) * FF_BLK)
            a = jnp.maximum(_dot(h2, wu_ref[j]), 0.0)
            a16_ref[:, cols] = a.astype(BF16)
            f = (a * a).astype(BF16)
            f2a = f2a + _dot(f, wda_ref[cols, :])
            f2b = f2b + _dot(f, wdb_ref[cols, :])
        f2 = jnp.concatenate([f2a, f2b], axis=1)
        y2, n2, r2 = _rms(f2, g2)
        e = x1 + y2 - t_ref[...]
        sq = jnp.sum(jnp.sum(e * e, axis=1, keepdims=True), axis=0, keepdims=True)
        loss_ref[...] += jnp.broadcast_to(sq * (0.5 / D), loss_ref.shape)
        dout = e * (1.0 / D)
        df2, dg2 = _rms_bwd(dout, n2, r2, g2)
        df2_16 = df2.astype(BF16)
        df2_ref[...] = df2_16.T
        dh2 = jnp.zeros((tq, D), F32)
        for j in range(N_DEV):
            cols = slice(j * FF_BLK, (j + 1) * FF_BLK)
            df = _dot_nt(df2_16[:, 0:half], wda_ref[cols, :]) + _dot_nt(df2_16[:, half:], wdb_ref[cols, :])
            du = (df * (2.0 * a16_ref[:, cols].astype(F32))).astype(BF16)
            du_ref[:, cols] = du
            dh2 = dh2 + _dot_nt(du, wu_ref[j])
        dx, dg1 = _rms_bwd(dh2, n1, r1, g1)
        dx1_ref[...] = dout + dx
        dg_ref[0:1, :] += dg2
        dg_ref[1:2, :] += dg1

    def tile(w):
        return pl.BlockSpec((tq, w), lambda i: (i, 0))

    return pl.pallas_call(
        body, grid=(S // tq,), name="mlp",
        in_specs=[tile(D), tile(D), _const((1, D)), _const((1, D)), _const((N_DEV, D, FF_BLK)), _const((FF, half)), _const((FF, half))],
        out_specs=[tile(FF), tile(FF), _tokens_in_lanes(tq), _tokens_in_lanes(tq), tile(D),
                   _acc((SUBLANES, LANES)), _acc((SUBLANES, D))],
        out_shape=[jax.ShapeDtypeStruct((S, FF), BF16), jax.ShapeDtypeStruct((S, FF), BF16),
                   jax.ShapeDtypeStruct((D, S), BF16), jax.ShapeDtypeStruct((D, S), BF16),
                   jax.ShapeDtypeStruct((S, D), F32), jax.ShapeDtypeStruct((SUBLANES, LANES), F32),
                   jax.ShapeDtypeStruct((SUBLANES, D), F32)],
        compiler_params=_cparams(60))(x1, tgt, g_pre, g_post, wup8, *wdn_halves)


def _mix_out_bwd(dx1, y2, ypre, y_attn, ltot, head_ones, q, bcu, qx16, kv16, cw8, g_post, g_attn, g_conv, g_x, wout16):
    def body(dx1_ref, y2_ref, ypre_ref, ya_ref, lt_ref, e_ref, q_ref, bcu_ref, halo_ref, qx_ref, kv_ref, cw_ref, gp_ref,
             ga_ref, gc_ref, gx_ref, w_ref, dy2_ref, qdo_ref, ld_ref, dbcu_ref, dqx_ref, dgs_ref, dcw_ref, dkv_ref,
             carry):
        i = pl.program_id(0)

        @pl.when(i == 0)
        def _():
            dgs_ref[...] = jnp.zeros_like(dgs_ref)
            dcw_ref[...] = jnp.zeros_like(dcw_ref)
            dkv_ref[...] = jnp.zeros_like(dkv_ref)
            carry[...] = jnp.zeros_like(carry)

        gp = gp_ref[...]
        _, n, r = _rms(y2_ref[...], gp)
        dy2, dgp = _rms_bwd(dx1_ref[...], n, r, gp)
        dy2_16 = dy2.astype(BF16)
        dy2_ref[...] = dy2_16
        dy = _dot_nt(dy2_16, w_ref[...])

        ypre, y_a = ypre_ref[...], ya_ref[...]
        ga, gc, gx = ga_ref[...], gc_ref[...], gx_ref[...]
        _, na, ra = _rms(y_a, ga)
        dya, dga = _rms_bwd(dy[:, 0:AW], na, ra, ga)
        _, nc, rc = _rms(ypre[:, 0:CW], gc)
        dyc, dgc = _rms_bwd(dy[:, AW:AW + CW], nc, rc, gc)
        y_x = ypre[:, CW:]
        _, nx, rx = _rms(y_x, gx)
        dyx, dgx = _rms_bwd(dy[:, AW + CW:], nx, rx, gx)
        qdo_ref[...] = _pack_pair(q_ref[...], dya)
        prod = dya * y_a
        hi = prod.astype(BF16)
        lo = (prod - hi.astype(F32)).astype(BF16)
        head_sum = _dot(hi, e_ref[...]) + _dot(lo, e_ref[...])
        lane_a = lax.broadcasted_iota(jnp.int32, prod.shape, 1)
        ld_ref[...] = jnp.where((lane_a % HEAD) < HEAD // 2, lt_ref[...], head_sum)
        dgs_ref[0:1, :] += dgp
        dgs_ref[1:2, :] += jnp.concatenate([dga, dgc, dgx], axis=1)

        bcu = bcu_ref[...]
        b, c, u = bcu[:, 0:CW], bcu[:, CW:2 * CW], bcu[:, 2 * CW:]
        z = c * u
        halo = halo_ref[...]
        zprev = jnp.where(i < NT - 1, halo[:, CW:2 * CW] * halo[:, 2 * CW:], 0.0)
        row = lax.broadcasted_iota(jnp.int32, z.shape, 0)
        z1, z2 = _conv_taps(z, zprev, row)
        cw = cw_ref[...]
        conv = z2 * cw[0:1, :] + z1 * cw[1:2, :] + z * cw[2:3, :]
        dconv = dyc * b
        nxt = carry[...]
        dn1 = jnp.where(row == TQ - 1, nxt[0:1, :], pltpu.roll(dconv, TQ - 1, 0))
        dn2 = jnp.where(row == TQ - 1, nxt[1:2, :], jnp.where(row == TQ - 2, nxt[0:1, :], pltpu.roll(dconv, TQ - 2, 0)))
        carry[...] = dconv[0:SUBLANES, :]
        dz = dconv * cw[2:3, :] + dn1 * cw[1:2, :] + dn2 * cw[0:1, :]
        dbcu_ref[:, 0:CW] = (dyc * conv).astype(BF16)
        dbcu_ref[:, CW:2 * CW] = (dz * u).astype(BF16)
        dbcu_ref[:, 2 * CW:] = (dz * c).astype(BF16)
        dcw_ref[0:1, :] += jnp.sum(z2 * dconv, axis=0, keepdims=True)
        dcw_ref[1:2, :] += jnp.sum(z1 * dconv, axis=0, keepdims=True)
        dcw_ref[2:3, :] += jnp.sum(z * dconv, axis=0, keepdims=True)

        qx = qx_ref[...]
        kv = kv_ref[...]
        km, vm = kv[:, 0:XW], kv[:, XW:]
        lane = lax.broadcasted_iota(jnp.int32, qx.shape, 1)
        dqx = jnp.zeros(qx.shape, F32)
        dkm = jnp.zeros((N_MEM, XW), F32)
        dvm = jnp.zeros((N_MEM, XW), F32)
        for h in range(XW // HEAD):
            hm = (lane >= h * HEAD) & (lane < (h + 1) * HEAD)
            qm = jnp.where(hm, qx, jnp.zeros_like(qx))
            e, l = _xattn_scores(qm, km)
            p = e / l
            dom = jnp.where(hm, dyx, 0.0)
            do16 = dom.astype(BF16)
            dsum = jnp.sum(dom * y_x, axis=1, keepdims=True)
            ds = (p * (_dot_nt(do16, vm) - dsum)).astype(BF16)
            dqx = jnp.where(hm, _dot(ds, km), dqx)
            dkm = dkm + _dot_tn(ds, qm)
            dvm = dvm + _dot_tn(p.astype(BF16), do16)
        dqx_ref[...] = (dqx * SCALE).astype(BF16)
        dkv_ref[:, 0:XW] += dkm
        dkv_ref[:, XW:] += dvm

    def tile(w):
        return pl.BlockSpec((TQ, w), lambda i: (NT - 1 - i, 0))

    halo = pl.BlockSpec((SUBLANES, 3 * CW), lambda i: (jnp.maximum((NT - 1 - i) * (TQ // SUBLANES) - 1, 0), 0))
    return pl.pallas_call(
        body, grid=(NT,), name="mix_out_bwd",
        in_specs=[tile(D), tile(D), tile(CW + XW), tile(AW), tile(AW), _const((AW, AW)), tile(AW), tile(3 * CW), halo,
                  tile(XW),
                  _const((N_MEM, 2 * XW)), _const((SUBLANES, CW)), _const((1, D)), _const((1, AW)), _const((1, CW)),
                  _const((1, XW)), _const((D, D))],
        out_specs=[tile(D), tile(AW), tile(AW), tile(3 * CW), tile(XW), _acc((SUBLANES, D)), _acc((SUBLANES, CW)),
                   _acc((N_MEM, 2 * XW))],
        out_shape=[jax.ShapeDtypeStruct((S, D), BF16), jax.ShapeDtypeStruct((S, AW), F32),
                   jax.ShapeDtypeStruct((S, AW), F32),
                   jax.ShapeDtypeStruct((S, 3 * CW), BF16), jax.ShapeDtypeStruct((S, XW), BF16),
                   jax.ShapeDtypeStruct((SUBLANES, D), F32), jax.ShapeDtypeStruct((SUBLANES, CW), F32),
                   jax.ShapeDtypeStruct((N_MEM, 2 * XW), F32)],
        scratch_shapes=[pltpu.VMEM((SUBLANES, CW), F32)],
        compiler_params=_cparams(56))(dx1, y2, ypre, y_attn, ltot, head_ones, q, bcu, bcu, qx16, kv16, cw8, g_post, g_attn,
                                      g_conv, g_x, wout16)


def _attn_bwd(qdo, kvp, ld, chip_sums=()):
    n_in = 3
    views = [[a] + [a.reshape(S // n, n, AW) for _, n, _, _ in ATTN_PLANS[1:]] for a in (qdo, kvp, ld)]
    flat = [views[a][p] for p in range(3) for a in range(n_in)]
    ns = len(chip_sums)
    n_grid = AW // LANES

    def body(*refs):
        hbm = [refs[n_in * p:n_in * p + n_in] for p in range(3)]
        refs = refs[3 * n_in:]
        sum_refs, refs = refs[:ns], refs[ns:]
        outs = [refs[3 * p:3 * p + 3] for p in range(3)]
        landed_refs, sc = refs[9:9 + ns], refs[9 + ns:]
        bufs = [sc[3 * p:3 * p + 3] for p in range(3)]
        res = [sc[9 + 3 * p:12 + 3 * p] for p in range(3)]
        tab128, tab4, sem_in, sem_out = sc[18:22]
        step = pl.program_id(0)
        if ns:
            start_chips, finish_chips = _chips_steps(sum_refs, landed_refs, *sc[22:])
            _, _, core, chips = _place()
            signal_chips, chips_are_in = _own_barrier([(px, py, core) for px, py in chips])
            pl.when(step == 0)(signal_chips)

            def chips_go():
                chips_are_in()
                start_chips()
        now =[_class_gather(hbm[p], bufs[p], sem_in.at[p], _lanes_of(step)) for p in range(3)]
        nxt = [_class_gather(hbm[p], bufs[p], sem_in.at[p], _lanes_of(step + 1)) for p in range(3)]

        @pl.when(step == 0)
        def _():
            for p in range(3):
                _start(now[p])
                for b in bufs[p]:
                    b[0:PAD, :] = jnp.zeros((PAD, LANES), F32)
            _fill_bias(tab128, 128, False)
            _fill_bias(tab4, 64, True)

        def prefetch(p):
            pl.when(step + 1 < n_grid)(lambda: _start(nxt[p]))

        lane = lax.broadcasted_iota(jnp.int32, (1, LANES), 1)

        def run(plan, plan_bufs, tab, dst):
            _, n_cls, qblk, nbc = plan
            partner = n_cls == 8
            bqdo, bkv, bld = plan_bufs
            rq, rk, rv = dst

            def block(g, carry):
                own, wins, mask = _block_rows(g, qblk, nbc, partner)
                qb, dob = _unpack_pair(bqdo[own, :])
                q2, do2 = _stack_heads(qb, lane), _stack_heads(dob, lane)
                kw, vw = _unpack_pair(_window(bkv, wins))
                ldv = bld[own, :]
                half = HEAD // 2
                lt2 = jnp.concatenate([ldv[:, 0:1], ldv[:, HEAD:HEAD + 1]], axis=0)
                dsum2 = jnp.concatenate([ldv[:, half:half + 1], ldv[:, HEAD + half:HEAD + half + 1]], axis=0)
                p = jnp.exp(_dot_nt(q2, kw) + tab[mask] - lt2)
                ds = (p * (_dot_nt(do2, vw) - dsum2)).astype(BF16)
                rq[own, :] = _unstack_heads(_dot(ds, kw), lane)
                dkw = _dot_tn(ds, q2)
                dvw = _dot_tn(p.astype(BF16), do2)
                n_w = WIN // len(wins)
                for i, w in enumerate(wins):
                    rk[w, :] += dkw[i * n_w:(i + 1) * n_w, :]
                    rv[w, :] += dvw[i * n_w:(i + 1) * n_w, :]
                return carry
            lax.fori_loop(0, n_cls * nbc, block, 0, unroll=ATTN_UNROLL)

        tabs = (tab128, tab4, tab128)
        def drained(p):
            return lambda: _wait(_whole_waits(res[p], sem_out.at[p]))

        for p in range(3):
            pl.when(step > 0)(drained(p))
            for b in res[p][1:]:
                b[...] = jnp.zeros_like(b)
            _wait(_whole_waits(bufs[p], sem_in.at[p]))
            run(ATTN_PLANS[p], bufs[p], tabs[p], res[p])
            prefetch(p)
            _start(_class_scatter(res[p], outs[p], sem_out.at[p], _lanes_of(step)))
            if ns and p == 0:
                pl.when(step == 0)(chips_go)
        for p in range(3):
            pl.when(step == n_grid - 1)(drained(p))
        if ns:
            pl.when(step == n_grid - 1)(finish_chips)

    padded = pltpu.VMEM((PAD + S, LANES), F32)
    shapes = [jax.ShapeDtypeStruct(views[0][p].shape, F32) for p in range(3) for _ in range(3)]
    out = pl.pallas_call(
        body, grid=(n_grid,), name="attn_bwd",
        in_specs=[ANY] * (3 * n_in + ns), out_specs=[ANY] * (9 + ns),
        out_shape=shapes + _chips_shapes(chip_sums),
        scratch_shapes=[padded] * 18
        + [pltpu.VMEM((4, 256, WIN), F32), pltpu.VMEM((4, 128, WIN), F32),
           pltpu.SemaphoreType.DMA((3, n_in)), pltpu.SemaphoreType.DMA((3, 3))]
        + (_chips_scratch(ns) if ns else []),
        compiler_params=_cparams(56, **({"collective_id": ID_ATTN_BWD} if ns else {})))(*flat, *chip_sums)
    return [o.reshape(S, AW) for o in out[:9]] + list(out[9:])


def _in_proj_bwd(dqkv, dbcu, dqx, cos, sins, w16, x, g, dx1):
    tq = TQ // 2

    def body(*refs):
        parts = refs[0:9]
        dbcu_ref, dqx_ref, c_ref, s_ref, w_ref, x_ref, g_ref, dx1_ref, dp_ref, gx_ref, dg_ref = refs[9:]

        @pl.when(pl.program_id(0) == 0)
        def _():
            dg_ref[...] = jnp.zeros_like(dg_ref)

        dq, dk, dv = (parts[i][...] + parts[3 + i][...] + parts[6 + i][...] for i in range(3))
        cos, sn = _all_heads(c_ref[...]), _all_heads(s_ref[...])
        dqr = dq * SCALE
        dkr = dk
        dp = jnp.concatenate([(dqr * cos + _rot_half(dqr * sn)).astype(BF16),
                              (dkr * cos + _rot_half(dkr * sn)).astype(BF16), dv.astype(BF16),
                              dbcu_ref[...], dqx_ref[...]], axis=1)
        dp_ref[...] = dp
        dh = _dot_nt(dp, w_ref[...])
        g = g_ref[...]
        _, n, r = _rms(x_ref[...], g)
        dx, dg = _rms_bwd(dh, n, r, g)
        gx_ref[...] = dx1_ref[...] + dx
        dg_ref[0:1, :] += dg

    def tile(w):
        return pl.BlockSpec((tq, w), lambda i: (i, 0))

    return pl.pallas_call(
        body, grid=(S // tq,), name="in_proj_bwd",
        in_specs=[tile(AW)] * 9 + [tile(3 * CW), tile(XW), tile(LANES), tile(LANES), _const((D, PW)),
                                   tile(D), _const((1, D)), tile(D)],
        out_specs=[tile(PW), tile(D), _acc((SUBLANES, D))],
        out_shape=[jax.ShapeDtypeStruct((S, PW), BF16), jax.ShapeDtypeStruct((S, D), F32),
                   jax.ShapeDtypeStruct((SUBLANES, D), F32)],
        compiler_params=_cparams(56))(*dqkv, dbcu, dqx, cos, sins, w16, x, g, dx1)


def _mem_bwd(mem, g_mem, wkv16, dkv):
    def body(m_ref, g_ref, w_ref, dkv_ref, dkv16_ref, dg_ref):
        dkv16 = dkv_ref[...].astype(BF16)
        dkv16_ref[...] = dkv16
        _, n, _ = _rms(m_ref[...], g_ref[...])
        dg = jnp.sum(_dot_nt(dkv16, w_ref[...]) * n, axis=0, keepdims=True)
        dg_ref[...] = jnp.broadcast_to(dg, dg_ref.shape)

    return pl.pallas_call(
        body, name="mem_bwd",
        out_shape=[jax.ShapeDtypeStruct((N_MEM, 2 * XW), BF16), jax.ShapeDtypeStruct((SUBLANES, D), F32)],
        compiler_params=pltpu.CompilerParams(vmem_limit_bytes=32 << 20))(mem, g_mem, wkv16, dkv)


N_CHIPS = N_DEV // 2


def _pair_scratch(block):
    return [pltpu.VMEM((N_CHIPS,) + block, BF16), pltpu.VMEM((N_CHIPS,) + block, BF16),
            pltpu.SemaphoreType.DMA((N_CHIPS,)), pltpu.SemaphoreType.DMA((N_CHIPS,))]


def _swap_with_sibling(p, stage, land, send, recv):
    x, y, c = lax.axis_index("x"), lax.axis_index("y"), lax.axis_index("c")
    return pltpu.make_async_remote_copy(src_ref=stage.at[p], dst_ref=land.at[p], send_sem=send.at[p],
                                        recv_sem=recv.at[p], device_id=(x, y, 1 - c), device_id_type=MESH)


def _own_barrier(peers):
    sem = pltpu.get_barrier_semaphore()

    def signal():
        for peer in peers:
            pl.semaphore_signal(sem, inc=1, device_id=peer, device_id_type=MESH)

    return signal, lambda: pl.semaphore_wait(sem, len(peers))


def _sibling_barrier():
    x, y, c = lax.axis_index("x"), lax.axis_index("y"), lax.axis_index("c")
    return _own_barrier([(x, y, 1 - c)])


ID_WGRAD_UP, ID_WGRAD_DOWN, ID_WGRAD_ROWS, ID_ROPE_TABLE, ID_IN_PROJ, ID_ATTN_BWD, ID_WGRAD_IN = range(7)


def _wgrad_cols(place, at16, b16, blk, name, barrier_id, square_b=False, transpose_out=False, to_chips=False,
                small=()):
    m, kk = at16.shape
    assert to_chips == bool(small)
    aligned = blk % LANES == 0
    wide = blk if aligned else -(-(blk + LANES // 2) // LANES) * LANES
    assert aligned or (transpose_out and blk % SUBLANES == 0)
    block = (blk, m) if transpose_out else (m, blk)

    def chip_of(step, my_chip):
        return jnp.bitwise_xor(my_chip, N_CHIPS - 1 - step) if to_chips else step

    def body(pl_ref, a_hbm, *refs):
        (a_ref, asem), refs = refs[-2:], refs[:-2]
        b_refs, refs = refs[:2 if aligned else 1], refs[2 if aligned else 1:]
        accs, refs = refs[:len(small)], refs[len(small):]
        (cs_ref, own_ref), refs = refs[:2], refs[2:]
        if to_chips:
            landed, refs = refs[0], refs[1:]
        if small:
            tot_ref, refs = refs[0], refs[1:]
        (stage, land, send, recv), refs = refs[:4], refs[4:]
        if not aligned:
            (win, wsem), refs = refs[:2], refs[2:]
        if small:
            start_small, finish_small = _small_reduce_steps(accs, tot_ref, *refs[-4:])
            refs = refs[:-4]
        step = pl.program_id(0)
        x, y, c = lax.axis_index("x"), lax.axis_index("y"), lax.axis_index("c")
        others = [(x ^ (k >> 2), y ^ ((k >> 1) & 1), c ^ (k & 1)) for k in range(1, N_DEV)]
        signal_peers, peers_are_in = _own_barrier(others if small else [(x, y, 1 - c)])
        pl.when(step == 0)(signal_peers)
        my_chip = 2 * x + y
        p = chip_of(step, my_chip)
        kh = kk // 2

        def a_half(h):
            cols = pl.ds(h * kh, kh)
            return pltpu.make_async_copy(a_hbm.at[:, cols], a_ref.at[:, cols], asem.at[h])

        @pl.when(step == 0)
        def _():
            a_half(0).start()
            a_half(1).start()

        def fetch(at_step, mine):
            j = 2 * chip_of(at_step, my_chip) + (c if mine else 1 - c)
            first = pl.multiple_of(((j * blk) >> 7) << 7, LANES)
            slot = 2 * (at_step & 1) + mine
            return pltpu.make_async_copy(b_refs[0].at[:, pl.ds(first, wide)], win.at[slot], wsem.at[slot])

        if not aligned:
            @pl.when(step == 0)
            def _():
                fetch(0, 0).start()
                fetch(0, 1).start()

            @pl.when(step + 1 < N_CHIPS)
            def _():
                fetch(step + 1, 0).start()
                fetch(step + 1, 1).start()

        def product(b, first):
            if not first:
                return _dot(a_ref[...], b)
            pl.when(step == 0)(lambda: a_half(0).wait())
            acc = _dot(a_ref[:, 0:kh], b[0:kh, :])
            pl.when(step == 0)(lambda: a_half(1).wait())
            return acc + _dot(a_ref[:, kh:], b[kh:, :])

        def partial(mine):
            if aligned:
                b = b_refs[mine][...]
                if square_b:
                    b = b * b
                acc = product(b, mine == 0)
            else:
                fetch(step, mine).wait()
                acc = product(win[2 * (step & 1) + mine], mine == 0).T
                odd = c if mine else 1 - c
                return jnp.where(odd == 0, acc[0:blk], acc[wide - blk:wide])
            return acc.T if transpose_out else acc

        stage[p] = partial(0).astype(BF16)
        pl.when(step == 0)(peers_are_in)
        if small:
            pl.when(step == 0)(start_small)
        swap = _swap_with_sibling(p, stage, land, send, recv)
        swap.start()
        mine = partial(1)
        swap.wait()
        total = mine + land[p].astype(F32)
        cs_ref[0] = total.astype(BF16)

        @pl.when(p == my_chip)
        def _():
            own_ref[...] = total

        if to_chips:
            stage2, send2, recv2 = refs
            flipped = jnp.bitwise_xor(p, my_chip)
            k = jnp.where(flipped == 2, 0, jnp.where(flipped == 1, 1, 2))

            def to_owner(src, k_, px, py):
                return pltpu.make_async_remote_copy(src_ref=src, dst_ref=landed.at[k_], send_sem=send2.at[k_],
                                                    recv_sem=recv2.at[k_], device_id=(px, py, c), device_id_type=MESH)

            @pl.when(p != my_chip)
            def _():
                stage2[p] = total.astype(BF16)
                to_owner(stage2.at[p], k, p >> 1, p & 1).start()

            @pl.when(step == N_CHIPS - 1)
            def _():
                for k_ in range(N_CHIPS - 1):
                    to_owner(stage2.at[0], k_, x, y).wait()

        if small:
            pl.when(step == N_CHIPS - 1)(finish_small)

    def b_spec(mine):
        return pl.BlockSpec((kk, blk), lambda i, s: (0, 2 * chip_of(i, s[1]) + (s[0] if mine else 1 - s[0])))

    b_specs, b_args = ([b_spec(0), b_spec(1)], (b16, b16)) if aligned else ([ANY], (b16,))
    scratch = _pair_scratch(block)
    if not aligned:
        scratch += [pltpu.VMEM((4, kk, wide), BF16), pltpu.SemaphoreType.DMA((4,))]
    out_specs = [pl.BlockSpec((1,) + block, lambda i, s: (chip_of(i, s[1]), 0, 0)), pl.BlockSpec(block, lambda i, s: (0, 0))]
    out_shape = [jax.ShapeDtypeStruct((N_CHIPS,) + block, BF16), jax.ShapeDtypeStruct(block, F32)]
    if to_chips:
        out_specs.append(ANY)
        out_shape.append(jax.ShapeDtypeStruct((N_CHIPS - 1,) + block, BF16))
        scratch += [pltpu.VMEM((N_CHIPS,) + block, BF16), pltpu.SemaphoreType.DMA((N_CHIPS - 1,)),
                    pltpu.SemaphoreType.DMA((N_CHIPS - 1,))]
    small_specs = [pl.BlockSpec(a.shape, lambda i, s: (0, 0)) for a in small]
    if small:
        out_specs.append(pl.BlockSpec((PACK_ROWS, D), lambda i, s: (0, 0)))
        out_shape.append(jax.ShapeDtypeStruct((PACK_ROWS, D), F32))
        scratch += _small_reduce_scratch()
    return pl.pallas_call(
        body, name=name,
        grid_spec=pltpu.PrefetchScalarGridSpec(
            num_scalar_prefetch=1, grid=(N_CHIPS,),
            in_specs=[ANY] + b_specs + small_specs, out_specs=out_specs,
            scratch_shapes=scratch + [pltpu.VMEM((m, kk), BF16), pltpu.SemaphoreType.DMA((2,))]),
        out_shape=out_shape,
        compiler_params=_cparams(56, collective_id=barrier_id),
    )(place, at16, *b_args, *small)


ROWS_STEPS = 4


def _wgrad_rows(place, products, name):
    n_prod = len(products)
    dims = [(at16.shape[0], at16.shape[1], b16.shape[1]) for at16, b16 in products]
    cut = [kk % (ROWS_STEPS * LANES) == 0 for _, kk, _ in dims]
    blocks = [(m // N_DEV, n) for m, _, n in dims]

    def body(pl_ref, *refs):
        ins, outs, scratch = refs[:2 * n_prod], refs[2 * n_prod:4 * n_prod], refs[4 * n_prod:]
        c, step = pl_ref[0], pl.program_id(0)

        def multiply(i):
            a_ref, b_ref, acc = ins[2 * i], ins[2 * i + 1], scratch[5 * i]

            @pl.when(step == 0)
            def _():
                acc[...] = _dot(a_ref[...], b_ref[...])

            if cut[i]:
                @pl.when(step > 0)
                def _():
                    acc[...] += _dot(a_ref[...], b_ref[...])

        def rows(i, owner):
            return pl.ds(pl.multiple_of(owner * blocks[i][0], blocks[i][0]), blocks[i][0])

        def send_sibling_side(i):
            acc, stage, land, send, recv = scratch[5 * i:5 * i + 5]
            swaps = []
            for p in range(N_CHIPS):
                stage[p] = acc[rows(i, 2 * p + 1 - c), :].astype(BF16)
                swaps.append(_swap_with_sibling(p, stage, land, send, recv))
                swaps[-1].start()
            return swaps

        def add_my_side(i, swaps):
            acc, land = scratch[5 * i], scratch[5 * i + 2]
            cs_ref, own_ref = outs[2 * i:2 * i + 2]
            for p in range(N_CHIPS):
                swaps[p].wait()
                total = acc[rows(i, 2 * p + c), :] + land[p].astype(F32)
                cs_ref[p] = total.astype(BF16)

                @pl.when(p == pl_ref[1])
                def _():
                    own_ref[...] = total

        signal_sibling, sibling_is_in = _sibling_barrier()
        pl.when(step == 0)(signal_sibling)
        for i in range(n_prod):
            multiply(i)

        @pl.when(step == ROWS_STEPS - 1)
        def _():
            sibling_is_in()
            swaps = [send_sibling_side(i) for i in range(n_prod)]
            for i in range(n_prod):
                add_my_side(i, swaps[i])

    in_specs, out_specs, out_shape, scratch = [pl.BlockSpec(memory_space=pltpu.SMEM)], [], [], []
    for (m, kk, n), cut_i, block in zip(dims, cut, blocks):
        chunk = kk // ROWS_STEPS
        in_specs += ([pl.BlockSpec((m, chunk), lambda i: (0, i)), pl.BlockSpec((chunk, n), lambda i: (i, 0))]
                     if cut_i else [_const((m, kk)), _const((kk, n))])
        out_specs += [_acc((N_CHIPS,) + block), _acc(block)]
        out_shape += [jax.ShapeDtypeStruct((N_CHIPS,) + block, BF16), jax.ShapeDtypeStruct(block, F32)]
        scratch += [pltpu.VMEM((m, n), F32)] + _pair_scratch(block)
    out = pl.pallas_call(
        body, grid=(ROWS_STEPS,), name=name, in_specs=in_specs, out_specs=out_specs, out_shape=out_shape,
        scratch_shapes=scratch, compiler_params=_cparams(56, collective_id=ID_WGRAD_ROWS),
    )(place, *[a for pair in products for a in pair])
    return [tuple(out[2 * i:2 * i + 2]) for i in range(n_prod)]


def _adamw_math(w, g, m, v):
    m = ADAM_B1 * m + (1.0 - ADAM_B1) * g
    v = ADAM_B2 * v + (1.0 - ADAM_B2) * jnp.square(g)
    m_hat = m / (1.0 - ADAM_B1 ** ADAM_STEP)
    v_hat = v / (1.0 - ADAM_B2 ** ADAM_STEP)
    delta = -ADAM_LR * (m_hat / (jnp.sqrt(v_hat) + ADAM_EPS) + ADAM_WD * w)
    return delta, m, v


ADAMW_STEPS = 4


def _adamw_shards(updates, name):
    names, nu = list(updates), len(updates)

    def body(*refs):
        ins, outs = refs[:5 * nu], refs[5 * nu:]
        for i in range(nu):
            o_ref, r_ref, w_ref, m_ref, v_ref = ins[5 * i:5 * i + 5]
            g_out, d_out, m_out, v_out = outs[4 * i:4 * i + 4]
            g = o_ref[...] + r_ref[0].astype(F32) + r_ref[1].astype(F32) + r_ref[2].astype(F32)
            g_out[...] = g
            d_out[...], m_out[...], v_out[...] = _adamw_math(w_ref[...], g, m_ref[...], v_ref[...])

    in_specs, out_specs = [], []
    for n in names:
        rows, cols = updates[n][2].shape
        chunk = pl.BlockSpec((rows // ADAMW_STEPS, cols), lambda i: (i, 0))
        in_specs += [chunk, pl.BlockSpec((N_CHIPS - 1, rows // ADAMW_STEPS, cols), lambda i: (0, i, 0))] + [chunk] * 3
        out_specs += [chunk] * 4
    out = pl.pallas_call(
        body, grid=(ADAMW_STEPS,), name=name, in_specs=in_specs, out_specs=out_specs,
        out_shape=[jax.ShapeDtypeStruct(updates[n][2].shape, F32) for n in names for _ in range(4)],
        compiler_params=_cparams(56))(*[a for n in names for a in updates[n]])
    return {n: out[4 * i:4 * i + 4] for i, n in enumerate(names)}


def _place():
    x, y, c = lax.axis_index("x"), lax.axis_index("y"), lax.axis_index("c")
    chips = [(1 - x, y), (x, 1 - y), (1 - x, 1 - y)]
    return x, y, c, chips


def _gather_steps(ins, outs, send, recv, lsem, own_barrier=True):
    nt = len(ins)
    x, y, c, (xn, yn, diag) = _place()
    me, sib = (x, y, c), (x, y, 1 - c)

    def slot(t, px, py, pc):
        return outs[t].at[4 * px + 2 * py + pc]

    def copy(t, k, block, to, src=None):
        return pltpu.make_async_remote_copy(
            src_ref=slot(t, *block) if src is None else src, dst_ref=slot(t, *block),
            send_sem=send.at[t, k], recv_sem=recv.at[t, k], device_id=to, device_id_type=MESH)

    mine = [pltpu.make_async_copy(ins[t], slot(t, *me), lsem.at[t]) for t in range(nt)]
    first = [copy(t, k, me, to, src=ins[t]) for t in range(nt) for k, to in ((0, sib), (1, (*xn, c)), (2, (*yn, c)))]

    if own_barrier:
        signal_peers, peers_are_in = _own_barrier([sib, (*xn, c), (*yn, c)])

    def enter():
        if own_barrier:
            signal_peers()
        for cp in mine:
            cp.start()

    def start():
        if own_barrier:
            peers_are_in()
        for cp in first:
            cp.start()

    def landed(k, chip, also_to=None):
        for t in range(nt):
            copy(t, k, (*chip, c), me).wait_recv()
            if also_to is not None:
                copy(t, 3, (*chip, c), (*also_to, c)).start()
            copy(t, 3 + k, (*chip, c), sib).start()

    def relay():
        @pl.when(c == 0)
        def _():
            landed(1, xn, also_to=yn)
            landed(2, yn)

        @pl.when(c == 1)
        def _():
            landed(2, yn, also_to=xn)
            landed(1, xn)

    def finish():
        landed(3, diag)
        for t in range(nt):
            copy(t, 0, sib, me).wait_recv()
            for k, chip in ((4, xn), (5, yn), (6, diag)):
                copy(t, k, (*chip, 1 - c), me).wait_recv()
            for k in range(7):
                copy(t, k, me, sib).wait_send()
        for cp in mine:
            cp.wait()

    return enter, start, relay, finish


def _gather_scratch(nt):
    return [pltpu.SemaphoreType.DMA((nt, 7)), pltpu.SemaphoreType.DMA((nt, 7)), pltpu.SemaphoreType.DMA((nt,))]


def _gathered_shapes(shards):
    return [jax.ShapeDtypeStruct((N_DEV,) + s.shape, s.dtype) for s in shards]


def _call_with_gather(body, n_grid, shards, *, name, in_specs, out_specs, out_shape, scratch_shapes, vmem_mb, args,
                      collective_id=None):
    assert (collective_id is None) == (not shards)
    ng, n_in, n_out = len(shards), len(in_specs), len(out_specs)

    def wrapped(*refs):
        ins, shard_refs = refs[:n_in], refs[n_in:n_in + ng]
        outs = refs[n_in + ng:n_in + ng + n_out]
        whole_refs = refs[n_in + ng + n_out:n_in + 2 * ng + n_out]
        scratch = refs[n_in + 2 * ng + n_out:]
        if ng:
            enter, start, relay, finish = _gather_steps(shard_refs, whole_refs, *scratch[len(scratch_shapes):])
            pl.when(pl.program_id(0) == 0)(enter)
            pl.when(pl.program_id(0) == 0)(start)
            pl.when(pl.program_id(0) == n_grid // 2)(relay)
        body(*ins, *outs, *scratch[:len(scratch_shapes)])
        if ng:
            pl.when(pl.program_id(0) == n_grid - 1)(finish)

    return pl.pallas_call(
        wrapped, grid=(n_grid,), name=name,
        in_specs=list(in_specs) + [ANY] * ng, out_specs=list(out_specs) + [ANY] * ng,
        out_shape=list(out_shape) + _gathered_shapes(shards),
        scratch_shapes=list(scratch_shapes) + (_gather_scratch(ng) if ng else []),
        compiler_params=_cparams(vmem_mb, **({"collective_id": collective_id} if shards else {})))(*args, *shards)


def _chips_steps(ins, outs, send, recv):
    _, _, c, chips = _place()
    copies = [pltpu.make_async_remote_copy(
        src_ref=ins[t].at[2 * px + py], dst_ref=outs[t].at[j], send_sem=send.at[t, j], recv_sem=recv.at[t, j],
        device_id=(px, py, c), device_id_type=MESH) for t in range(len(ins)) for j, (px, py) in enumerate(chips)]

    def start():
        for cp in copies:
            cp.start()

    def finish():
        for cp in copies:
            cp.wait()

    return start, finish


def _chips_scratch(nt):
    return [pltpu.SemaphoreType.DMA((nt, 3)), pltpu.SemaphoreType.DMA((nt, 3))]


def _chips_shapes(cs16s):
    return [jax.ShapeDtypeStruct((3,) + g.shape[1:], g.dtype) for g in cs16s]


SMALL = (("g_pre_mix", 0, 0, D), ("g_mem", 1, 0, D), ("g_post_mix", 2, 0, D), ("g_attn_out", 3, 0, AW),
         ("g_conv_out", 3, AW, CW), ("g_xattn_out", 3, AW + CW, XW), ("g_post_mlp", 4, 0, D), ("g_pre_mlp", 5, 0, D))
CONV_ROW = 8
PACK_ROWS = 16


LOSS_ROW = 15


def _small_reduce_steps(accs, tot_ref, pack, land, send, recv):
    acc_in, acc_mem, acc_mix, acc_mlp, acc_cw, acc_loss = accs
    x, y, c, _ = _place()
    me = 4 * x + 2 * y + c
    copies = []
    for k in range(1, N_DEV):
        kx, ky, kc = (k >> 2) & 1, (k >> 1) & 1, k & 1
        peer = (1 - x if kx else x, 1 - y if ky else y, 1 - c if kc else c)
        copies.append(pltpu.make_async_remote_copy(
            src_ref=pack, dst_ref=land.at[me], send_sem=send.at[k - 1], recv_sem=recv.at[k - 1],
            device_id=peer, device_id_type=MESH))

    def start():
        pack[...] = jnp.zeros_like(pack)
        pack[0:1, :] = acc_in[0:1, :]
        pack[1:2, :] = acc_mem[0:1, :]
        pack[2:4, :] = acc_mix[0:2, :]
        pack[4:6, :] = acc_mlp[0:2, :]
        pack[CONV_ROW:CONV_ROW + 3, 0:CW] = acc_cw[0:3, :]
        pack[LOSS_ROW:LOSS_ROW + 1, 0:LANES] = acc_loss[0:1, :]
        land[me] = pack[...]
        for cp in copies:
            cp.start()

    def finish():
        for cp in copies:
            cp.wait()
        tot = land[0]
        for s in range(1, N_DEV):
            tot = tot + land[s]
        tot_ref[...] = tot

    return start, finish


def _small_reduce_scratch():
    return [pltpu.VMEM((PACK_ROWS, D), F32), pltpu.VMEM((N_DEV, PACK_ROWS, D), F32),
            pltpu.SemaphoreType.DMA((N_DEV - 1,)), pltpu.SemaphoreType.DMA((N_DEV - 1,))]


def _small_update(tot, me, params):
    flat = [a for n, _, _, _ in SMALL for a in params[n]] + list(params["conv_w"])
    n_par = len(SMALL) + 1
    tap_cols = CW // N_DEV

    def body(*refs):
        me_ref, tot_ref = refs[0:2]
        ins = refs[2:2 + 3 * n_par]
        loss_out = refs[2 + 3 * n_par]
        outs = refs[3 + 3 * n_par:]
        tot = tot_ref[...]
        loss_out[...] = jnp.broadcast_to(tot[LOSS_ROW:LOSS_ROW + 1, 0:LANES], loss_out.shape)

        def update(i, g):
            w_ref, m_ref, v_ref = ins[3 * i:3 * i + 3]
            for o_ref, res in zip(outs[4 * i:4 * i + 4], (g,) + _adamw_math(w_ref[...], g, m_ref[...], v_ref[...])):
                if len(o_ref.shape) == 3:
                    for t in range(o_ref.shape[0]):
                        o_ref[t] = res[t:t + 1, :]
                else:
                    o_ref[...] = res

        for i, (_, row, lane0, width) in enumerate(SMALL):
            update(i, tot[row:row + 1, lane0:lane0 + width])
        me = me_ref[0]
        taps = pltpu.roll(tot[CONV_ROW:CONV_ROW + SUBLANES, 0:CW], jnp.where(me == 0, 0, CW - me * tap_cols), 1)
        update(n_par - 1, taps[0:3, 0:tap_cols])

    shapes = [jax.ShapeDtypeStruct(params[n][0].shape, F32) for n, _, _, _ in SMALL] + [
        jax.ShapeDtypeStruct((3, 1, tap_cols), F32)]
    vmem = pl.BlockSpec(memory_space=pltpu.VMEM)
    loss, *out = pl.pallas_call(
        body, name="small_update",
        in_specs=[pl.BlockSpec(memory_space=pltpu.SMEM)] + [vmem] * (1 + 3 * n_par),
        out_shape=[jax.ShapeDtypeStruct((SUBLANES, LANES), F32)] + [s for s in shapes for _ in range(4)],
    )(me, tot, *flat)
    names = [n for n, _, _, _ in SMALL] + ["conv_w"]
    return loss[0, 0], {n: out[4 * i:4 * i + 4] for i, n in enumerate(names)}


def _local_step(x, mem, pos, gains, shards, tgt, place):
    half = HEAD // 2
    inv_freq = jnp.float32(ROPE_THETA) ** (-(jnp.arange(half, dtype=F32) * 2.0 / HEAD))
    invf = jnp.tile(inv_freq, LANES // half)[None, :]
    sgn = jnp.tile(jnp.concatenate([-jnp.ones((half,), F32), jnp.ones((half,), F32)]), LANES // HEAD)[None, :]
    cos, sins, win8 = _rope_table(pos.astype(F32).reshape(S, 1), invf, sgn, [shards["w_in"]])
    wdn_left, wdn_right = shards["w_down"][:, 0:D // 2], shards["w_down"][:, D // 2:]
    q, kvp, bcu, qx16, ht16, win16, wout8, wkv8, conv8, wdn8_right = _in_proj(
        x, gains["g_pre_mix"], win8, cos, sins, [shards["w_out"], shards["w_mem_kv"], shards["conv_w"], wdn_right])
    wout16, wkv16 = wout8.reshape(D, D), wkv8.reshape(D, 2 * XW)
    cw_full = conv8[:, 0:3, 0:CW // N_DEV].transpose(1, 0, 2).reshape(3, CW)
    cw8 = jnp.zeros((SUBLANES, CW), F32).at[0:3].set(cw_full)
    y_attn, ltot, wup8, wdn8_left = _attn_fwd(q, kvp, [shards["w_up"], wdn_left])
    wdn_halves = (wdn8_left.reshape(FF, D // 2), wdn8_right.reshape(FF, D // 2))
    memnt16, kv16 = _mem_fwd(mem, gains["g_mem"], wkv16)
    ypre, yt16, y2, x1 = _mix_out(y_attn, bcu, qx16, kv16, cw8, gains["g_attn_out"], gains["g_conv_out"],
                                 gains["g_xattn_out"], gains["g_post_mix"], wout16, x, [])
    a16, du16, h2t16, df2t16, dx1, loss8, dg_mlp = _mlp(
        x1, tgt, gains["g_pre_mlp"], gains["g_post_mlp"], wup8, wdn_halves)

    sums = {"w_up": _wgrad_cols(place, h2t16, du16, FF_BLK, "wgrad_up", ID_WGRAD_UP),
            "w_down": _wgrad_cols(place, df2t16, a16, FF_BLK, "wgrad_down", ID_WGRAD_DOWN, square_b=True,
                                  transpose_out=True)}

    head_id = jnp.arange(AW, dtype=jnp.int32) // HEAD
    head_ones = (head_id[:, None] == head_id[None, :]).astype(BF16)
    dy2_16, qdo, ld, dbcu, dqx, dgs, dcw, dkv = _mix_out_bwd(
        dx1, y2, ypre, y_attn, ltot, head_ones, q, bcu, qx16, kv16, cw8, gains["g_post_mix"], gains["g_attn_out"],
        gains["g_conv_out"], gains["g_xattn_out"], wout16)
    dkv16, dg_mem = _mem_bwd(mem, gains["g_mem"], wkv16, dkv)
    sums["w_mem_kv"], sums["w_out"] = _wgrad_rows(place, [(memnt16, dkv16), (yt16, dy2_16)], "wgrad_mem_kv_out")
    out = _attn_bwd(qdo, kvp, ld, [s[0] for s in sums.values()])
    dqkv, landed = out[:9], out[9:]
    reduced = {n: (s[1], landed[t]) for t, (n, s) in enumerate(sums.items())}
    dproj16, grad_x, dg_in = _in_proj_bwd(dqkv, dbcu, dqx, cos, sins, win16, x, gains["g_pre_mix"], dx1)

    _, in_own, in_landed, small_tot = _wgrad_cols(place, ht16, dproj16, PW // N_DEV, "wgrad_in", ID_WGRAD_IN,
                                                  transpose_out=True, to_chips=True,
                                                  small=(dg_in, dg_mem, dgs, dg_mlp, dcw, loss8))
    reduced["w_in"] = (in_own, in_landed)
    return grad_x, reduced, small_tot


BIG = ("w_in", "w_mem_kv", "w_out", "w_up", "w_down")
ORDER = ("g_pre_mix", "g_mem", "w_in", "w_mem_kv", "conv_w", "g_attn_out", "g_conv_out", "g_xattn_out", "w_out",
         "g_post_mix", "g_pre_mlp", "w_up", "w_down", "g_post_mlp")


def kernel(x, mem, positions, g_pre_mix, g_mem, w_in, w_mem_kv, conv_w, g_attn_out, g_conv_out, g_xattn_out, w_out, g_post_mix, g_pre_mlp, w_up, w_down, g_post_mlp, loss_target, m_g_pre_mix, m_g_mem, m_w_in, m_w_mem_kv, m_conv_w, m_g_attn_out, m_g_conv_out, m_g_xattn_out, m_w_out, m_g_post_mix, m_g_pre_mlp, m_w_up, m_w_down, m_g_post_mlp, v_g_pre_mix, v_g_mem, v_w_in, v_w_mem_kv, v_conv_w, v_g_attn_out, v_g_conv_out, v_g_xattn_out, v_w_out, v_g_post_mix, v_g_pre_mlp, v_w_up, v_w_down, v_g_post_mlp):
    w = dict(g_pre_mix=g_pre_mix, g_mem=g_mem, w_in=w_in, w_mem_kv=w_mem_kv, conv_w=conv_w, g_attn_out=g_attn_out,
             g_conv_out=g_conv_out, g_xattn_out=g_xattn_out, w_out=w_out, g_post_mix=g_post_mix, g_pre_mlp=g_pre_mlp,
             w_up=w_up, w_down=w_down, g_post_mlp=g_post_mlp)
    mo = dict(g_pre_mix=m_g_pre_mix, g_mem=m_g_mem, w_in=m_w_in, w_mem_kv=m_w_mem_kv, conv_w=m_conv_w,
              g_attn_out=m_g_attn_out, g_conv_out=m_g_conv_out, g_xattn_out=m_g_xattn_out, w_out=m_w_out,
              g_post_mix=m_g_post_mix, g_pre_mlp=m_g_pre_mlp, w_up=m_w_up, w_down=m_w_down, g_post_mlp=m_g_post_mlp)
    vo = dict(g_pre_mix=v_g_pre_mix, g_mem=v_g_mem, w_in=v_w_in, w_mem_kv=v_w_mem_kv, conv_w=v_conv_w,
              g_attn_out=v_g_attn_out, g_conv_out=v_g_conv_out, g_xattn_out=v_g_xattn_out, w_out=v_w_out,
              g_post_mix=v_g_post_mix, g_pre_mlp=v_g_pre_mlp, w_up=v_w_up, w_down=v_w_down, g_post_mlp=v_g_post_mlp)

    xi, yi, ci = lax.axis_index("x"), lax.axis_index("y"), lax.axis_index("c")
    me = 4 * xi + 2 * yi + ci
    place = jnp.stack([ci, 2 * xi + yi]).astype(jnp.int32)

    shards = {n: w[n][0].astype(BF16) for n in BIG}
    shards["conv_w"] = jnp.zeros((SUBLANES, LANES), F32).at[0:3, 0:CW // N_DEV].set(conv_w[0])

    gains = {n: w[n] for n, _, _, _ in SMALL}
    grad_x, reduced, small_tot = _local_step(x[0], mem[0], positions[0], gains, shards, loss_target[0], place)

    def shard(n, a):
        return a[0].T if n == "w_in" else a[0]

    updated = _adamw_shards({n: (*reduced[n], shard(n, w[n]), shard(n, mo[n]), shard(n, vo[n])) for n in BIG},
                            "adamw")
    grad, delta, new_m, new_v = {}, {}, {}, {}
    for n, res in updated.items():
        grad[n], delta[n], new_m[n], new_v[n] = [(a.T if n == "w_in" else a)[None] for a in res]

    params = {n: (w[n], mo[n], vo[n]) for n, _, _, _ in SMALL}
    params["conv_w"] = (w["conv_w"][0], mo["conv_w"][0], vo["conv_w"][0])
    loss, small = _small_update(small_tot, me.reshape(1).astype(jnp.int32), params)
    for n, (g, d_, m_, v_) in small.items():
        lead = (lambda a: a.reshape(conv_w.shape)) if n == "conv_w" else (lambda a: a)
        grad[n], delta[n], new_m[n], new_v[n] = lead(g), lead(d_), lead(m_), lead(v_)

    return (loss, grad_x[None], *[grad[n] for n in ORDER], *[delta[n] for n in ORDER],
            *[new_m[n] for n in ORDER], *[new_v[n] for n in ORDER])
```

```python
import jax
import jax.numpy as jnp
from jax import lax
from jax.experimental import pallas as pl
from jax.experimental.pallas import tpu as pltpu

F32, BF16 = jnp.float32, jnp.bfloat16
MESH = pl.DeviceIdType.MESH
ANY = pl.BlockSpec(memory_space=pl.ANY)

N_DEV = 8
D = 1024
S = 4096
N_MEM = 256
HEAD = 64
AW, CW, XW = 512, 256, 256
PW = 3 * AW + 3 * CW + XW
FF = 4096
FF_BLK = FF // N_DEV
EPS = 1e-6
NEG = -1e30
SCALE = HEAD ** -0.5
ROPE_THETA = 10000.0
LANES = 128
SUBLANES = 8

ADAM_LR, ADAM_B1, ADAM_B2, ADAM_EPS, ADAM_WD, ADAM_STEP = 0.001, 0.9, 0.999, 1e-08, 0.01, 10

TQ = 512
TQ_MLP = 512
NT = S // TQ


def _cparams(vmem_mb, n_grid=1, **more):
    return pltpu.CompilerParams(dimension_semantics=("arbitrary",) * n_grid, vmem_limit_bytes=vmem_mb << 20, **more)


def _const(shape):
    nd = len(shape)
    return pl.BlockSpec(shape, lambda *_: (0,) * nd, pipeline_mode=pl.Buffered(1))


def _acc(shape):
    nd = len(shape)
    return pl.BlockSpec(shape, lambda *_: (0,) * nd)


def _tokens_in_lanes(tq):
    return pl.BlockSpec((D, tq), lambda i: (0, i))


def _dot(a, b):
    return jnp.dot(a, b, preferred_element_type=F32)


def _dot_nt(a, b):
    return lax.dot_general(a, b, (((1,), (1,)), ((), ())), preferred_element_type=F32)


def _dot_tn(a, b):
    return lax.dot_general(a, b, (((0,), (0,)), ((), ())), preferred_element_type=F32)


def _rms(x, g):
    r = lax.rsqrt(jnp.mean(x * x, axis=-1, keepdims=True) + EPS)
    n = x * r
    return n * g, n, r


def _rms_bwd(dy, n, r, g):
    dn = dy * g
    dx = r * (dn - n * jnp.mean(dn * n, axis=-1, keepdims=True))
    return dx, jnp.sum(dy * n, axis=0, keepdims=True)


def _rot_half(t):
    lane = lax.broadcasted_iota(jnp.int32, t.shape, 1)
    n = t.shape[1]
    return jnp.where((lane % HEAD) < HEAD // 2, pltpu.roll(t, n - HEAD // 2, 1), pltpu.roll(t, HEAD // 2, 1))


def _rope_table(pos_col, invf, sgn, shards):
    def body(p_ref, f_ref, s_ref, c_out, s_out):
        ang = p_ref[...] * f_ref[...]
        c_out[...] = jnp.cos(ang)
        s_out[...] = jnp.sin(ang) * s_ref[...]

    tile = pl.BlockSpec((TQ, LANES), lambda i: (i, 0))
    return _call_with_gather(
        body, NT, shards, name="rope_table",
        in_specs=[pl.BlockSpec((TQ, 1), lambda i: (i, 0)), _const((1, LANES)), _const((1, LANES))],
        out_specs=[tile, tile], out_shape=[jax.ShapeDtypeStruct((S, LANES), F32)] * 2,
        scratch_shapes=[], vmem_mb=32, args=(pos_col, invf, sgn), collective_id=ID_ROPE_TABLE)


def _all_heads(t):
    return jnp.tile(t, (1, AW // LANES))


def _mem_fwd(mem, g_mem, wkv16):
    def body(m_ref, g_ref, w_ref, n16_ref, kv_ref):
        y, _, _ = _rms(m_ref[...], g_ref[...])
        y16 = y.astype(BF16)
        n16_ref[...] = y16.T
        kv_ref[...] = _dot(y16, w_ref[...]).astype(BF16)

    return pl.pallas_call(
        body, name="mem_fwd",
        out_shape=[jax.ShapeDtypeStruct((D, N_MEM), BF16), jax.ShapeDtypeStruct((N_MEM, 2 * XW), BF16)],
        compiler_params=pltpu.CompilerParams(vmem_limit_bytes=32 << 20))(mem, g_mem, wkv16)


def _in_proj(x, g, w8, cos, sins, shards):
    blk = PW // N_DEV

    def body(x_ref, g_ref, w8_ref, c_ref, s_ref, q_ref, kv_ref, bcu_ref, qx_ref, h_ref, w_out, w_ref):
        @pl.when(pl.program_id(0) == 0)
        def _():
            for j in range(N_DEV):
                w_ref[:, j * blk:(j + 1) * blk] = w8_ref[j]
            w_out[...] = w_ref[...]

        y, _, _ = _rms(x_ref[...], g_ref[...])
        h = y.astype(BF16)
        h_ref[...] = h.T
        proj = _dot(h, w_ref[...])
        cos, sn = _all_heads(c_ref[...]), _all_heads(s_ref[...])
        q, k = proj[:, 0:AW], proj[:, AW:2 * AW]
        q_ref[...] = (q * cos + _rot_half(q) * sn) * SCALE
        kv_ref[...] = _pack_pair(k * cos + _rot_half(k) * sn, proj[:, 2 * AW:3 * AW])
        bcu_ref[...] = proj[:, 3 * AW:3 * AW + 3 * CW]
        qx_ref[...] = (proj[:, 3 * AW + 3 * CW:] * SCALE).astype(BF16)

    def tile(w):
        return pl.BlockSpec((TQ, w), lambda i: (i, 0))

    return _call_with_gather(
        body, NT, shards, name="in_proj",
        in_specs=[tile(D), _const((1, D)), _const((N_DEV, D, blk)), tile(LANES), tile(LANES)],
        out_specs=[tile(AW), tile(AW), tile(3 * CW), tile(XW), _tokens_in_lanes(TQ), _acc((D, PW))],
        out_shape=[jax.ShapeDtypeStruct((S, AW), F32)] * 2 + [
            jax.ShapeDtypeStruct((S, 3 * CW), F32), jax.ShapeDtypeStruct((S, XW), BF16),
            jax.ShapeDtypeStruct((D, S), BF16), jax.ShapeDtypeStruct((D, PW), BF16)],
        scratch_shapes=[pltpu.VMEM((D, PW), BF16)], vmem_mb=56, args=(x, g, w8, cos, sins),
        collective_id=ID_IN_PROJ)


ATTN_PLANS = (("p1", 1, 128, 32), ("p4", 8, 64, 8), ("p16", 16, 128, 2))
PAD = 128
WIN = 256


ATTN_UNROLL = 16


def _fill_bias(tab, qblk, partner):
    qi = lax.broadcasted_iota(jnp.int32, (2 * qblk, WIN), 0) & (qblk - 1)
    kj = lax.broadcasted_iota(jnp.int32, (2 * qblk, WIN), 1)
    piece = kj >> (qblk.bit_length() - 1)
    kk = kj & (qblk - 1)
    prev = (piece & 1) == 0
    of_partner = piece >= 2
    for first in (0, 1):
        for par in (0, 1):
            lo = jnp.where(prev, (qblk if first else qi) + jnp.where(of_partner, par, 0), 0)
            hi = jnp.where(prev, qblk, qi + jnp.where(of_partner, par - 1, 0))
            tab[2 * first + par] = jnp.where((kk >= lo) & (kk <= hi), 0.0, NEG).astype(F32)


def _block_rows(g, qblk, nbc, partner):
    own = pl.ds(pl.multiple_of(PAD + g * qblk, qblk), qblk)
    first = ((g & (nbc - 1)) == 0).astype(jnp.int32)
    if partner:
        gp = jnp.bitwise_xor(g, 4 * nbc)
        wins = (pl.ds(pl.multiple_of(PAD + (g - 1) * qblk, qblk), 2 * qblk),
                pl.ds(pl.multiple_of(PAD + (gp - 1) * qblk, qblk), 2 * qblk))
        return own, wins, 2 * first + ((g >> ((4 * nbc).bit_length() - 1)) & 1)
    return own, (pl.ds(pl.multiple_of(PAD + (g - 1) * qblk, qblk), 2 * qblk),), 2 * first


def _pack_pair(lo, hi):
    lo_bits = lax.bitcast_convert_type(lo.astype(BF16).astype(F32), jnp.uint32) >> 16
    hi_bits = lax.bitcast_convert_type(hi.astype(BF16).astype(F32), jnp.uint32) & jnp.uint32(0xFFFF0000)
    return lax.bitcast_convert_type(hi_bits | lo_bits, F32)


def _unpack_pair(c):
    bits = lax.bitcast_convert_type(c, jnp.uint32)
    lo = lax.bitcast_convert_type(bits << 16, F32).astype(BF16)
    hi = lax.bitcast_convert_type(bits & jnp.uint32(0xFFFF0000), F32).astype(BF16)
    return lo, hi


def _window(ref, wins):
    parts = [ref[w, :] for w in wins]
    return parts[0] if len(parts) == 1 else jnp.concatenate(parts, axis=0)


def _stack_heads(t, lane):
    zero = jnp.zeros_like(t)
    return jnp.concatenate([jnp.where(lane < HEAD, t, zero), jnp.where(lane >= HEAD, t, zero)], axis=0)


def _unstack_heads(t2, lane):
    half = t2.shape[0] // 2
    return jnp.where(lane < HEAD, t2[0:half, :], t2[half:, :])


def _lanes_of(step):
    return pl.ds(pl.multiple_of(step * LANES, LANES), LANES)


def _whole_wait(buf, sem):
    whole = buf.at[pl.ds(PAD, S), :]
    return pltpu.make_async_copy(whole, whole, sem)


def _whole_waits(bufs, sems):
    return [_whole_wait(buf, sems.at[i]) for i, buf in enumerate(bufs)]


def _class_gather(views, bufs, sems, lanes):
    copies = []
    for i, (view, buf) in enumerate(zip(views, bufs)):
        if view.ndim == 2:
            copies.append(pltpu.make_async_copy(view.at[:, lanes], buf.at[pl.ds(PAD, S), :], sems.at[i]))
        else:
            per, n_cls = view.shape[0], view.shape[1]
            copies += [pltpu.make_async_copy(view.at[:, c, lanes], buf.at[pl.ds(PAD + c * per, per), :], sems.at[i])
                       for c in range(n_cls)]
    return copies


def _class_scatter(bufs, dsts, sems, lanes):
    copies = []
    for i, (buf, dst) in enumerate(zip(bufs, dsts)):
        if dst.ndim == 2:
            copies.append(pltpu.make_async_copy(buf.at[pl.ds(PAD, S), :], dst.at[:, lanes], sems.at[i]))
            continue
        per, n_cls = dst.shape[0], dst.shape[1]
        copies += [pltpu.make_async_copy(buf.at[pl.ds(PAD + c * per, per), :], dst.at[:, c, lanes], sems.at[i])
                   for c in range(n_cls)]
    return copies


def _start(copies):
    for cp in copies:
        cp.start()


def _wait(waits):
    for w in waits:
        w.wait()


def _attn_fwd(q, kvp, shards=()):
    views = [[a] + [a.reshape(S // n, n, AW) for _, n, _, _ in ATTN_PLANS[1:]] for a in (q, kvp)]
    flat = [views[a][p] for p in range(3) for a in range(2)]
    ng = len(shards)
    n_grid = AW // LANES

    def body(*refs):
        hbm = [refs[2 * p:2 * p + 2] for p in range(3)]
        refs = refs[6:]
        shard_refs, refs = refs[:ng], refs[ng:]
        y_ref, lt_ref = refs[0:2]
        whole_refs, refs = refs[2:2 + ng], refs[2 + ng:]
        bufs = [refs[2 * p:2 * p + 2] for p in range(3)]
        oc4, lc4, oc16, lc16, tab128, tab4, sem_in = refs[6:13]
        step = pl.program_id(0)
        if ng:
            enter_gather, start_gather, relay_gather, finish_gather = _gather_steps(
                shard_refs, whole_refs, *refs[13:], own_barrier=False)
            pl.when(step == 0)(enter_gather)
            pl.when(step == 0)(start_gather)
            pl.when(step == n_grid // 2)(relay_gather)
        now = [_class_gather(hbm[p], bufs[p], sem_in.at[p], _lanes_of(step)) for p in range(3)]
        nxt = [_class_gather(hbm[p], bufs[p], sem_in.at[p], _lanes_of(step + 1)) for p in range(3)]

        @pl.when(step == 0)
        def _():
            for p in range(3):
                _start(now[p])
                for b in bufs[p]:
                    b[0:PAD, :] = jnp.zeros((PAD, LANES), F32)
            _fill_bias(tab128, 128, False)
            _fill_bias(tab4, 64, True)

        def prefetch(p):
            pl.when(step + 1 < n_grid)(lambda: _start(nxt[p]))

        lane = lax.broadcasted_iota(jnp.int32, (1, LANES), 1)
        ones = jnp.ones((WIN, LANES), BF16)

        def run(plan, bq, bkv, tab, o_dst, l_dst, dst_pad):
            _, n_cls, qblk, nbc = plan
            partner = n_cls == 8

            def block(g, carry):
                own, wins, mask = _block_rows(g, qblk, nbc, partner)
                q2 = _stack_heads(bq[own, :].astype(BF16), lane)
                kw, vwin = _unpack_pair(_window(bkv, wins))
                vw = jnp.concatenate([vwin, ones], axis=1)
                s = _dot_nt(q2, kw) + tab[mask]
                m = jnp.max(s, axis=1, keepdims=True)
                oe = _dot(jnp.exp(s - m).astype(BF16), vw)
                den = oe[:, LANES:]
                dst = pl.ds(pl.multiple_of(dst_pad + g * qblk, qblk), qblk)
                o_dst[dst, :] = _unstack_heads(oe[:, 0:LANES] / den, lane)
                l_dst[dst, :] = _unstack_heads(m + jnp.log(den), lane)
                return carry
            lax.fori_loop(0, n_cls * nbc, block, 0, unroll=ATTN_UNROLL)

        _wait(_whole_waits(bufs[0], sem_in.at[0]))
        run(ATTN_PLANS[0], *bufs[0], tab128, y_ref, lt_ref, 0)
        prefetch(0)
        _wait(_whole_waits(bufs[1], sem_in.at[1]))
        run(ATTN_PLANS[1], *bufs[1], tab4, oc4, lc4, PAD)
        prefetch(1)
        _wait(_whole_waits(bufs[2], sem_in.at[2]))
        run(ATTN_PLANS[2], *bufs[2], tab128, oc16, lc16, PAD)
        prefetch(2)

        n_rows = 64

        def token_order(buf, t, n_cls):
            per = S // n_cls
            first = PAD + t * (n_rows // n_cls)
            return jnp.concatenate([buf[pl.ds(first + jj, n_cls, stride=per), :] for jj in range(n_rows // n_cls)],
                                   axis=0)

        def combine(t, carry):
            rows = pl.ds(pl.multiple_of(t * n_rows, n_rows), n_rows)
            l0, l1, l2 = lt_ref[rows, :], token_order(lc4, t, 8), token_order(lc16, t, 16)
            lm = jnp.maximum(jnp.maximum(l0, l1), l2)
            e0, e1, e2 = jnp.exp(l0 - lm), jnp.exp(l1 - lm), jnp.exp(l2 - lm)
            den = e0 + e1 + e2
            y_ref[rows, :] = (e0 * y_ref[rows, :] + e1 * token_order(oc4, t, 8)
                              + e2 * token_order(oc16, t, 16)) / den
            lt_ref[rows, :] = lm + jnp.log(den)
            return carry
        lax.fori_loop(0, S // n_rows, combine, 0, unroll=2)

        if ng:
            pl.when(step == n_grid - 1)(finish_gather)

    col = pl.BlockSpec((S, LANES), lambda h: (0, h))
    padded = pltpu.VMEM((PAD + S, LANES), F32)
    return pl.pallas_call(
        body, grid=(n_grid,), name="attn_fwd",
        in_specs=[ANY] * (6 + ng), out_specs=[col, col] + [ANY] * ng,
        out_shape=[jax.ShapeDtypeStruct((S, AW), F32)] * 2 + _gathered_shapes(shards),
        scratch_shapes=[padded] * 10 + [
            pltpu.VMEM((4, 256, WIN), F32), pltpu.VMEM((4, 128, WIN), F32), pltpu.SemaphoreType.DMA((3, 2))]
        + (_gather_scratch(ng) if ng else []),
        compiler_params=_cparams(56))(*flat, *shards)


def _conv_taps(z, zprev, row):
    z1 = jnp.where(row == 0, zprev[7:8, :], pltpu.roll(z, 1, 0))
    z2 = jnp.where(row == 0, zprev[6:7, :], jnp.where(row == 1, zprev[7:8, :], pltpu.roll(z, 2, 0)))
    return z1, z2


def _xattn_scores(qm, km):
    s = _dot_nt(qm, km)
    m = jnp.max(s, axis=1, keepdims=True)
    e = jnp.exp(s - m)
    return e, jnp.sum(e, axis=1, keepdims=True)


def _mix_out(y_attn, bcu, qx16, kv16, cw8, g_attn, g_conv, g_x, g_post, wout16, x, shards):
    def body(ya_ref, bcu_ref, halo_ref, qx_ref, kv_ref, cw_ref, ga_ref, gc_ref, gx_ref, gp_ref, w_ref, x_ref,
             ypre_ref, y16_ref, y2_ref, x1_ref):
        i = pl.program_id(0)
        bcu = bcu_ref[...]
        b, c, u = bcu[:, 0:CW], bcu[:, CW:2 * CW], bcu[:, 2 * CW:]
        z = c * u
        halo = halo_ref[...]
        zprev = jnp.where(i > 0, halo[:, CW:2 * CW] * halo[:, 2 * CW:], 0.0)
        row = lax.broadcasted_iota(jnp.int32, z.shape, 0)
        z1, z2 = _conv_taps(z, zprev, row)
        cw = cw_ref[...]
        y_conv = b * (z2 * cw[0:1, :] + z1 * cw[1:2, :] + z * cw[2:3, :])

        qx = qx_ref[...]
        kv = kv_ref[...]
        km, vm = kv[:, 0:XW], kv[:, XW:]
        lane = lax.broadcasted_iota(jnp.int32, qx.shape, 1)
        y_x = jnp.zeros(qx.shape, F32)
        for h in range(XW // HEAD):
            hm = (lane >= h * HEAD) & (lane < (h + 1) * HEAD)
            e, l = _xattn_scores(jnp.where(hm, qx, jnp.zeros_like(qx)), km)
            y_x = jnp.where(hm, _dot(e.astype(BF16), vm) / l, y_x)

        y_attn = ya_ref[...]
        ypre_ref[:, 0:CW] = y_conv
        ypre_ref[:, CW:] = y_x
        y = jnp.concatenate([_rms(y_attn, ga_ref[...])[0], _rms(y_conv, gc_ref[...])[0],
                             _rms(y_x, gx_ref[...])[0]], axis=1).astype(BF16)
        y16_ref[...] = y.T
        y2 = _dot(y, w_ref[...])
        y2_ref[...] = y2
        x1_ref[...] = x_ref[...] + _rms(y2, gp_ref[...])[0]

    def tile(w):
        return pl.BlockSpec((TQ, w), lambda i: (i, 0))

    halo = pl.BlockSpec((SUBLANES, 3 * CW), lambda i: (jnp.maximum(i * (TQ // SUBLANES) - 1, 0), 0))
    return _call_with_gather(
        body, NT, shards, name="mix_out",
        in_specs=[tile(AW), tile(3 * CW), halo, tile(XW), _const((N_MEM, 2 * XW)), _const((SUBLANES, CW)),
                  _const((1, AW)), _const((1, CW)), _const((1, XW)), _const((1, D)), _const((D, D)), tile(D)],
        out_specs=[tile(CW + XW), _tokens_in_lanes(TQ), tile(D), tile(D)],
        out_shape=[jax.ShapeDtypeStruct((S, CW + XW), F32), jax.ShapeDtypeStruct((D, S), BF16),
                   jax.ShapeDtypeStruct((S, D), F32), jax.ShapeDtypeStruct((S, D), F32)],
        scratch_shapes=[], vmem_mb=56,
        args=(y_attn, bcu, bcu, qx16, kv16, cw8, g_attn, g_conv, g_x, g_post, wout16, x))


def _mlp(x1, tgt, g_pre, g_post, wup8, wdn_halves):
    tq = TQ_MLP
    half = D // 2

    def body(x1_ref, t_ref, g1_ref, g2_ref, wu_ref, wda_ref, wdb_ref,
             a16_ref, du_ref, h2_ref, df2_ref, dx1_ref, loss_ref, dg_ref):
        @pl.when(pl.program_id(0) == 0)
        def _():
            loss_ref[...] = jnp.zeros_like(loss_ref)
            dg_ref[...] = jnp.zeros_like(dg_ref)

        x1 = x1_ref[...]
        g1, g2 = g1_ref[...], g2_ref[...]
        y1, n1, r1 = _rms(x1, g1)
        h2 = y1.astype(BF16)
        h2_ref[...] = h2.T
        f2a = jnp.zeros((tq, half), F32)
        f2b = jnp.zeros((tq, half), F32)
        for j in range(N_DEV):
            cols = slice(j * FF_BLK, (j + 1) * FF_BLK)
            a = jnp.maximum(_dot(h2, wu_ref[j]), 0.0)
            a16_ref[:, cols] = a.astype(BF16)
            f = (a * a).astype(BF16)
            f2a = f2a + _dot(f, wda_ref[cols, :])
            f2b = f2b + _dot(f, wdb_ref[cols, :])
        f2 = jnp.concatenate([f2a, f2b], axis=1)
        y2, n2, r2 = _rms(f2, g2)
        e = x1 + y2 - t_ref[...]
        sq = jnp.sum(jnp.sum(e * e, axis=1, keepdims=True), axis=0, keepdims=True)
        loss_ref[...] += jnp.broadcast_to(sq * (0.5 / D), loss_ref.shape)
        dout = e * (1.0 / D)
        df2, dg2 = _rms_bwd(dout, n2, r2, g2)
        df2_16 = df2.astype(BF16)
        df2_ref[...] = df2_16.T
        dh2 = jnp.zeros((tq, D), F32)
        for j in range(N_DEV):
            cols = slice(j * FF_BLK, (j + 1) * FF_BLK)
            df = _dot_nt(df2_16[:, 0:half], wda_ref[cols, :]) + _dot_nt(df2_16[:, half:], wdb_ref[cols, :])
            du = (df * (2.0 * a16_ref[:, cols].astype(F32))).astype(BF16)
            du_ref[:, cols] = du
            dh2 = dh2 + _dot_nt(du, wu_ref[j])
        dx, dg1 = _rms_bwd(dh2, n1, r1, g1)
        dx1_ref[...] = dout + dx
        dg_ref[0:1, :] += dg2
        dg_ref[1:2, :] += dg1

    def tile(w):
        return pl.BlockSpec((tq, w), lambda i: (i, 0))

    return pl.pallas_call(
        body, grid=(S // tq,), name="mlp",
        in_specs=[tile(D), tile(D), _const((1, D)), _const((1, D)), _const((N_DEV, D, FF_BLK)), _const((FF, half)), _const((FF, half))],
        out_specs=[tile(FF), tile(FF), _tokens_in_lanes(tq), _tokens_in_lanes(tq), tile(D),
                   _acc((SUBLANES, LANES)), _acc((SUBLANES, D))],
        out_shape=[jax.ShapeDtypeStruct((S, FF), BF16), jax.ShapeDtypeStruct((S, FF), BF16),
                   jax.ShapeDtypeStruct((D, S), BF16), jax.ShapeDtypeStruct((D, S), BF16),
                   jax.ShapeDtypeStruct((S, D), F32), jax.ShapeDtypeStruct((SUBLANES, LANES), F32),
                   jax.ShapeDtypeStruct((SUBLANES, D), F32)],
        compiler_params=_cparams(60))(x1, tgt, g_pre, g_post, wup8, *wdn_halves)


def _mix_out_bwd(dx1, y2, ypre, y_attn, ltot, head_ones, q, bcu, qx16, kv16, cw8, g_post, g_attn, g_conv, g_x, wout16):
    def body(dx1_ref, y2_ref, ypre_ref, ya_ref, lt_ref, e_ref, q_ref, bcu_ref, halo_ref, qx_ref, kv_ref, cw_ref, gp_ref,
             ga_ref, gc_ref, gx_ref, w_ref, dy2_ref, qdo_ref, ld_ref, dbcu_ref, dqx_ref, dgs_ref, dcw_ref, dkv_ref,
             carry):
        i = pl.program_id(0)

        @pl.when(i == 0)
        def _():
            dgs_ref[...] = jnp.zeros_like(dgs_ref)
            dcw_ref[...] = jnp.zeros_like(dcw_ref)
            dkv_ref[...] = jnp.zeros_like(dkv_ref)
            carry[...] = jnp.zeros_like(carry)

        gp = gp_ref[...]
        _, n, r = _rms(y2_ref[...], gp)
        dy2, dgp = _rms_bwd(dx1_ref[...], n, r, gp)
        dy2_16 = dy2.astype(BF16)
        dy2_ref[...] = dy2_16
        dy = _dot_nt(dy2_16, w_ref[...])

        ypre, y_a = ypre_ref[...], ya_ref[...]
        ga, gc, gx = ga_ref[...], gc_ref[...], gx_ref[...]
        _, na, ra = _rms(y_a, ga)
        dya, dga = _rms_bwd(dy[:, 0:AW], na, ra, ga)
        _, nc, rc = _rms(ypre[:, 0:CW], gc)
        dyc, dgc = _rms_bwd(dy[:, AW:AW + CW], nc, rc, gc)
        y_x = ypre[:, CW:]
        _, nx, rx = _rms(y_x, gx)
        dyx, dgx = _rms_bwd(dy[:, AW + CW:], nx, rx, gx)
        qdo_ref[...] = _pack_pair(q_ref[...], dya)
        prod = dya * y_a
        hi = prod.astype(BF16)
        lo = (prod - hi.astype(F32)).astype(BF16)
        head_sum = _dot(hi, e_ref[...]) + _dot(lo, e_ref[...])
        lane_a = lax.broadcasted_iota(jnp.int32, prod.shape, 1)
        ld_ref[...] = jnp.where((lane_a % HEAD) < HEAD // 2, lt_ref[...], head_sum)
        dgs_ref[0:1, :] += dgp
        dgs_ref[1:2, :] += jnp.concatenate([dga, dgc, dgx], axis=1)

        bcu = bcu_ref[...]
        b, c, u = bcu[:, 0:CW], bcu[:, CW:2 * CW], bcu[:, 2 * CW:]
        z = c * u
        halo = halo_ref[...]
        zprev = jnp.where(i < NT - 1, halo[:, CW:2 * CW] * halo[:, 2 * CW:], 0.0)
        row = lax.broadcasted_iota(jnp.int32, z.shape, 0)
        z1, z2 = _conv_taps(z, zprev, row)
        cw = cw_ref[...]
        conv = z2 * cw[0:1, :] + z1 * cw[1:2, :] + z * cw[2:3, :]
        dconv = dyc * b
        nxt = carry[...]
        dn1 = jnp.where(row == TQ - 1, nxt[0:1, :], pltpu.roll(dconv, TQ - 1, 0))
        dn2 = jnp.where(row == TQ - 1, nxt[1:2, :], jnp.where(row == TQ - 2, nxt[0:1, :], pltpu.roll(dconv, TQ - 2, 0)))
        carry[...] = dconv[0:SUBLANES, :]
        dz = dconv * cw[2:3, :] + dn1 * cw[1:2, :] + dn2 * cw[0:1, :]
        dbcu_ref[:, 0:CW] = (dyc * conv).astype(BF16)
        dbcu_ref[:, CW:2 * CW] = (dz * u).astype(BF16)
        dbcu_ref[:, 2 * CW:] = (dz * c).astype(BF16)
        dcw_ref[0:1, :] += jnp.sum(z2 * dconv, axis=0, keepdims=True)
        dcw_ref[1:2, :] += jnp.sum(z1 * dconv, axis=0, keepdims=True)
        dcw_ref[2:3, :] += jnp.sum(z * dconv, axis=0, keepdims=True)

        qx = qx_ref[...]
        kv = kv_ref[...]
        km, vm = kv[:, 0:XW], kv[:, XW:]
        lane = lax.broadcasted_iota(jnp.int32, qx.shape, 1)
        dqx = jnp.zeros(qx.shape, F32)
        dkm = jnp.zeros((N_MEM, XW), F32)
        dvm = jnp.zeros((N_MEM, XW), F32)
        for h in range(XW // HEAD):
            hm = (lane >= h * HEAD) & (lane < (h + 1) * HEAD)
            qm = jnp.where(hm, qx, jnp.zeros_like(qx))
            e, l = _xattn_scores(qm, km)
            p = e / l
            dom = jnp.where(hm, dyx, 0.0)
            do16 = dom.astype(BF16)
            dsum = jnp.sum(dom * y_x, axis=1, keepdims=True)
            ds = (p * (_dot_nt(do16, vm) - dsum)).astype(BF16)
            dqx = jnp.where(hm, _dot(ds, km), dqx)
            dkm = dkm + _dot_tn(ds, qm)
            dvm = dvm + _dot_tn(p.astype(BF16), do16)
        dqx_ref[...] = (dqx * SCALE).astype(BF16)
        dkv_ref[:, 0:XW] += dkm
        dkv_ref[:, XW:] += dvm

    def tile(w):
        return pl.BlockSpec((TQ, w), lambda i: (NT - 1 - i, 0))

    halo = pl.BlockSpec((SUBLANES, 3 * CW), lambda i: (jnp.maximum((NT - 1 - i) * (TQ // SUBLANES) - 1, 0), 0))
    return pl.pallas_call(
        body, grid=(NT,), name="mix_out_bwd",
        in_specs=[tile(D), tile(D), tile(CW + XW), tile(AW), tile(AW), _const((AW, AW)), tile(AW), tile(3 * CW), halo,
                  tile(XW),
                  _const((N_MEM, 2 * XW)), _const((SUBLANES, CW)), _const((1, D)), _const((1, AW)), _const((1, CW)),
                  _const((1, XW)), _const((D, D))],
        out_specs=[tile(D), tile(AW), tile(AW), tile(3 * CW), tile(XW), _acc((SUBLANES, D)), _acc((SUBLANES, CW)),
                   _acc((N_MEM, 2 * XW))],
        out_shape=[jax.ShapeDtypeStruct((S, D), BF16), jax.ShapeDtypeStruct((S, AW), F32),
                   jax.ShapeDtypeStruct((S, AW), F32),
                   jax.ShapeDtypeStruct((S, 3 * CW), BF16), jax.ShapeDtypeStruct((S, XW), BF16),
                   jax.ShapeDtypeStruct((SUBLANES, D), F32), jax.ShapeDtypeStruct((SUBLANES, CW), F32),
                   jax.ShapeDtypeStruct((N_MEM, 2 * XW), F32)],
        scratch_shapes=[pltpu.VMEM((SUBLANES, CW), F32)],
        compiler_params=_cparams(56))(dx1, y2, ypre, y_attn, ltot, head_ones, q, bcu, bcu, qx16, kv16, cw8, g_post, g_attn,
                                      g_conv, g_x, wout16)


def _attn_bwd(qdo, kvp, ld, chip_sums=()):
    n_in = 3
    views = [[a] + [a.reshape(S // n, n, AW) for _, n, _, _ in ATTN_PLANS[1:]] for a in (qdo, kvp, ld)]
    flat = [views[a][p] for p in range(3) for a in range(n_in)]
    ns = len(chip_sums)
    n_grid = AW // LANES

    def body(*refs):
        hbm = [refs[n_in * p:n_in * p + n_in] for p in range(3)]
        refs = refs[3 * n_in:]
        sum_refs, refs = refs[:ns], refs[ns:]
        outs = [refs[3 * p:3 * p + 3] for p in range(3)]
        landed_refs, sc = refs[9:9 + ns], refs[9 + ns:]
        bufs = [sc[3 * p:3 * p + 3] for p in range(3)]
        res = [sc[9 + 3 * p:12 + 3 * p] for p in range(3)]
        tab128, tab4, sem_in, sem_out = sc[18:22]
        step = pl.program_id(0)
        if ns:
            start_chips, finish_chips = _chips_steps(sum_refs, landed_refs, *sc[22:])
            _, _, core, chips = _place()
            signal_chips, chips_are_in = _own_barrier([(px, py, core) for px, py in chips])
            pl.when(step == 0)(signal_chips)

            def chips_go():
                chips_are_in()
                start_chips()
        now =[_class_gather(hbm[p], bufs[p], sem_in.at[p], _lanes_of(step)) for p in range(3)]
        nxt = [_class_gather(hbm[p], bufs[p], sem_in.at[p], _lanes_of(step + 1)) for p in range(3)]

        @pl.when(step == 0)
        def _():
            for p in range(3):
                _start(now[p])
                for b in bufs[p]:
                    b[0:PAD, :] = jnp.zeros((PAD, LANES), F32)
            _fill_bias(tab128, 128, False)
            _fill_bias(tab4, 64, True)

        def prefetch(p):
            pl.when(step + 1 < n_grid)(lambda: _start(nxt[p]))

        lane = lax.broadcasted_iota(jnp.int32, (1, LANES), 1)

        def run(plan, plan_bufs, tab, dst):
            _, n_cls, qblk, nbc = plan
            partner = n_cls == 8
            bqdo, bkv, bld = plan_bufs
            rq, rk, rv = dst

            def block(g, carry):
                own, wins, mask = _block_rows(g, qblk, nbc, partner)
                qb, dob = _unpack_pair(bqdo[own, :])
                q2, do2 = _stack_heads(qb, lane), _stack_heads(dob, lane)
                kw, vw = _unpack_pair(_window(bkv, wins))
                ldv = bld[own, :]
                half = HEAD // 2
                lt2 = jnp.concatenate([ldv[:, 0:1], ldv[:, HEAD:HEAD + 1]], axis=0)
                dsum2 = jnp.concatenate([ldv[:, half:half + 1], ldv[:, HEAD + half:HEAD + half + 1]], axis=0)
                p = jnp.exp(_dot_nt(q2, kw) + tab[mask] - lt2)
                ds = (p * (_dot_nt(do2, vw) - dsum2)).astype(BF16)
                rq[own, :] = _unstack_heads(_dot(ds, kw), lane)
                dkw = _dot_tn(ds, q2)
                dvw = _dot_tn(p.astype(BF16), do2)
                n_w = WIN // len(wins)
                for i, w in enumerate(wins):
                    rk[w, :] += dkw[i * n_w:(i + 1) * n_w, :]
                    rv[w, :] += dvw[i * n_w:(i + 1) * n_w, :]
                return carry
            lax.fori_loop(0, n_cls * nbc, block, 0, unroll=ATTN_UNROLL)

        tabs = (tab128, tab4, tab128)
        def drained(p):
            return lambda: _wait(_whole_waits(res[p], sem_out.at[p]))

        for p in range(3):
            pl.when(step > 0)(drained(p))
            for b in res[p][1:]:
                b[...] = jnp.zeros_like(b)
            _wait(_whole_waits(bufs[p], sem_in.at[p]))
            run(ATTN_PLANS[p], bufs[p], tabs[p], res[p])
            prefetch(p)
            _start(_class_scatter(res[p], outs[p], sem_out.at[p], _lanes_of(step)))
            if ns and p == 0:
                pl.when(step == 0)(chips_go)
        for p in range(3):
            pl.when(step == n_grid - 1)(drained(p))
        if ns:
            pl.when(step == n_grid - 1)(finish_chips)

    padded = pltpu.VMEM((PAD + S, LANES), F32)
    shapes = [jax.ShapeDtypeStruct(views[0][p].shape, F32) for p in range(3) for _ in range(3)]
    out = pl.pallas_call(
        body, grid=(n_grid,), name="attn_bwd",
        in_specs=[ANY] * (3 * n_in + ns), out_specs=[ANY] * (9 + ns),
        out_shape=shapes + _chips_shapes(chip_sums),
        scratch_shapes=[padded] * 18
        + [pltpu.VMEM((4, 256, WIN), F32), pltpu.VMEM((4, 128, WIN), F32),
           pltpu.SemaphoreType.DMA((3, n_in)), pltpu.SemaphoreType.DMA((3, 3))]
        + (_chips_scratch(ns) if ns else []),
        compiler_params=_cparams(56, **({"collective_id": ID_ATTN_BWD} if ns else {})))(*flat, *chip_sums)
    return [o.reshape(S, AW) for o in out[:9]] + list(out[9:])


def _in_proj_bwd(dqkv, dbcu, dqx, cos, sins, w16, x, g, dx1):
    tq = TQ // 2

    def body(*refs):
        parts = refs[0:9]
        dbcu_ref, dqx_ref, c_ref, s_ref, w_ref, x_ref, g_ref, dx1_ref, dp_ref, gx_ref, dg_ref = refs[9:]

        @pl.when(pl.program_id(0) == 0)
        def _():
            dg_ref[...] = jnp.zeros_like(dg_ref)

        dq, dk, dv = (parts[i][...] + parts[3 + i][...] + parts[6 + i][...] for i in range(3))
        cos, sn = _all_heads(c_ref[...]), _all_heads(s_ref[...])
        dqr = dq * SCALE
        dkr = dk
        dp = jnp.concatenate([(dqr * cos + _rot_half(dqr * sn)).astype(BF16),
                              (dkr * cos + _rot_half(dkr * sn)).astype(BF16), dv.astype(BF16),
                              dbcu_ref[...], dqx_ref[...]], axis=1)
        dp_ref[...] = dp
        dh = _dot_nt(dp, w_ref[...])
        g = g_ref[...]
        _, n, r = _rms(x_ref[...], g)
        dx, dg = _rms_bwd(dh, n, r, g)
        gx_ref[...] = dx1_ref[...] + dx
        dg_ref[0:1, :] += dg

    def tile(w):
        return pl.BlockSpec((tq, w), lambda i: (i, 0))

    return pl.pallas_call(
        body, grid=(S // tq,), name="in_proj_bwd",
        in_specs=[tile(AW)] * 9 + [tile(3 * CW), tile(XW), tile(LANES), tile(LANES), _const((D, PW)),
                                   tile(D), _const((1, D)), tile(D)],
        out_specs=[tile(PW), tile(D), _acc((SUBLANES, D))],
        out_shape=[jax.ShapeDtypeStruct((S, PW), BF16), jax.ShapeDtypeStruct((S, D), F32),
                   jax.ShapeDtypeStruct((SUBLANES, D), F32)],
        compiler_params=_cparams(56))(*dqkv, dbcu, dqx, cos, sins, w16, x, g, dx1)


def _mem_bwd(mem, g_mem, wkv16, dkv):
    def body(m_ref, g_ref, w_ref, dkv_ref, dkv16_ref, dg_ref):
        dkv16 = dkv_ref[...].astype(BF16)
        dkv16_ref[...] = dkv16
        _, n, _ = _rms(m_ref[...], g_ref[...])
        dg = jnp.sum(_dot_nt(dkv16, w_ref[...]) * n, axis=0, keepdims=True)
        dg_ref[...] = jnp.broadcast_to(dg, dg_ref.shape)

    return pl.pallas_call(
        body, name="mem_bwd",
        out_shape=[jax.ShapeDtypeStruct((N_MEM, 2 * XW), BF16), jax.ShapeDtypeStruct((SUBLANES, D), F32)],
        compiler_params=pltpu.CompilerParams(vmem_limit_bytes=32 << 20))(mem, g_mem, wkv16, dkv)


N_CHIPS = N_DEV // 2


def _pair_scratch(block):
    return [pltpu.VMEM((N_CHIPS,) + block, BF16), pltpu.VMEM((N_CHIPS,) + block, BF16),
            pltpu.SemaphoreType.DMA((N_CHIPS,)), pltpu.SemaphoreType.DMA((N_CHIPS,))]


def _swap_with_sibling(p, stage, land, send, recv):
    x, y, c = lax.axis_index("x"), lax.axis_index("y"), lax.axis_index("c")
    return pltpu.make_async_remote_copy(src_ref=stage.at[p], dst_ref=land.at[p], send_sem=send.at[p],
                                        recv_sem=recv.at[p], device_id=(x, y, 1 - c), device_id_type=MESH)


def _own_barrier(peers):
    sem = pltpu.get_barrier_semaphore()

    def signal():
        for peer in peers:
            pl.semaphore_signal(sem, inc=1, device_id=peer, device_id_type=MESH)

    return signal, lambda: pl.semaphore_wait(sem, len(peers))


def _sibling_barrier():
    x, y, c = lax.axis_index("x"), lax.axis_index("y"), lax.axis_index("c")
    return _own_barrier([(x, y, 1 - c)])


ID_WGRAD_UP, ID_WGRAD_DOWN, ID_WGRAD_ROWS, ID_ROPE_TABLE, ID_IN_PROJ, ID_ATTN_BWD, ID_WGRAD_IN = range(7)


def _wgrad_cols(place, at16, b16, blk, name, barrier_id, square_b=False, transpose_out=False, to_chips=False,
                small=()):
    m, kk = at16.shape
    assert to_chips == bool(small)
    aligned = blk % LANES == 0
    wide = blk if aligned else -(-(blk + LANES // 2) // LANES) * LANES
    assert aligned or (transpose_out and blk % SUBLANES == 0)
    block = (blk, m) if transpose_out else (m, blk)

    def chip_of(step, my_chip):
        return jnp.bitwise_xor(my_chip, N_CHIPS - 1 - step) if to_chips else step

    def body(pl_ref, a_ref, *refs):
        b_refs, refs = refs[:2 if aligned else 1], refs[2 if aligned else 1:]
        accs, refs = refs[:len(small)], refs[len(small):]
        (cs_ref, own_ref), refs = refs[:2], refs[2:]
        if to_chips:
            landed, refs = refs[0], refs[1:]
        if small:
            tot_ref, refs = refs[0], refs[1:]
        (stage, land, send, recv), refs = refs[:4], refs[4:]
        if not aligned:
            (win, wsem), refs = refs[:2], refs[2:]
        if small:
            start_small, finish_small = _small_reduce_steps(accs, tot_ref, *refs[-4:])
            refs = refs[:-4]
        step = pl.program_id(0)
        x, y, c = lax.axis_index("x"), lax.axis_index("y"), lax.axis_index("c")
        others = [(x ^ (k >> 2), y ^ ((k >> 1) & 1), c ^ (k & 1)) for k in range(1, N_DEV)]
        signal_peers, peers_are_in = _own_barrier(others if small else [(x, y, 1 - c)])
        pl.when(step == 0)(signal_peers)
        my_chip = 2 * x + y
        p = chip_of(step, my_chip)

        def fetch(at_step, mine):
            j = 2 * chip_of(at_step, my_chip) + (c if mine else 1 - c)
            first = pl.multiple_of(((j * blk) >> 7) << 7, LANES)
            slot = 2 * (at_step & 1) + mine
            return pltpu.make_async_copy(b_refs[0].at[:, pl.ds(first, wide)], win.at[slot], wsem.at[slot])

        if not aligned:
            @pl.when(step == 0)
            def _():
                fetch(0, 0).start()
                fetch(0, 1).start()

            @pl.when(step + 1 < N_CHIPS)
            def _():
                fetch(step + 1, 0).start()
                fetch(step + 1, 1).start()

        def partial(mine):
            if aligned:
                b = b_refs[mine][...]
                if square_b:
                    b = b * b
                acc = _dot(a_ref[...], b)
            else:
                fetch(step, mine).wait()
                acc = _dot(a_ref[...], win[2 * (step & 1) + mine]).T
                odd = c if mine else 1 - c
                return jnp.where(odd == 0, acc[0:blk], acc[wide - blk:wide])
            return acc.T if transpose_out else acc

        stage[p] = partial(0).astype(BF16)
        pl.when(step == 0)(peers_are_in)
        if small:
            pl.when(step == 0)(start_small)
        swap = _swap_with_sibling(p, stage, land, send, recv)
        swap.start()
        mine = partial(1)
        swap.wait()
        total = mine + land[p].astype(F32)
        cs_ref[0] = total.astype(BF16)

        @pl.when(p == my_chip)
        def _():
            own_ref[...] = total

        if to_chips:
            stage2, send2, recv2 = refs
            flipped = jnp.bitwise_xor(p, my_chip)
            k = jnp.where(flipped == 2, 0, jnp.where(flipped == 1, 1, 2))

            def to_owner(src, k_, px, py):
                return pltpu.make_async_remote_copy(src_ref=src, dst_ref=landed.at[k_], send_sem=send2.at[k_],
                                                    recv_sem=recv2.at[k_], device_id=(px, py, c), device_id_type=MESH)

            @pl.when(p != my_chip)
            def _():
                stage2[p] = total.astype(BF16)
                to_owner(stage2.at[p], k, p >> 1, p & 1).start()

            @pl.when(step == N_CHIPS - 1)
            def _():
                for k_ in range(N_CHIPS - 1):
                    to_owner(stage2.at[0], k_, x, y).wait()

        if small:
            pl.when(step == N_CHIPS - 1)(finish_small)

    def b_spec(mine):
        return pl.BlockSpec((kk, blk), lambda i, s: (0, 2 * chip_of(i, s[1]) + (s[0] if mine else 1 - s[0])))

    b_specs, b_args = ([b_spec(0), b_spec(1)], (b16, b16)) if aligned else ([ANY], (b16,))
    scratch = _pair_scratch(block)
    if not aligned:
        scratch += [pltpu.VMEM((4, kk, wide), BF16), pltpu.SemaphoreType.DMA((4,))]
    out_specs = [pl.BlockSpec((1,) + block, lambda i, s: (chip_of(i, s[1]), 0, 0)), pl.BlockSpec(block, lambda i, s: (0, 0))]
    out_shape = [jax.ShapeDtypeStruct((N_CHIPS,) + block, BF16), jax.ShapeDtypeStruct(block, F32)]
    if to_chips:
        out_specs.append(ANY)
        out_shape.append(jax.ShapeDtypeStruct((N_CHIPS - 1,) + block, BF16))
        scratch += [pltpu.VMEM((N_CHIPS,) + block, BF16), pltpu.SemaphoreType.DMA((N_CHIPS - 1,)),
                    pltpu.SemaphoreType.DMA((N_CHIPS - 1,))]
    small_specs = [pl.BlockSpec(a.shape, lambda i, s: (0, 0)) for a in small]
    if small:
        out_specs.append(pl.BlockSpec((PACK_ROWS, D), lambda i, s: (0, 0)))
        out_shape.append(jax.ShapeDtypeStruct((PACK_ROWS, D), F32))
        scratch += _small_reduce_scratch()
    return pl.pallas_call(
        body, name=name,
        grid_spec=pltpu.PrefetchScalarGridSpec(
            num_scalar_prefetch=1, grid=(N_CHIPS,),
            in_specs=[pl.BlockSpec((m, kk), lambda i, s: (0, 0), pipeline_mode=pl.Buffered(1))] + b_specs + small_specs,
            out_specs=out_specs, scratch_shapes=scratch),
        out_shape=out_shape,
        compiler_params=_cparams(56, collective_id=barrier_id),
    )(place, at16, *b_args, *small)


ROWS_STEPS = 4


def _wgrad_rows(place, products, name):
    n_prod = len(products)
    dims = [(at16.shape[0], at16.shape[1], b16.shape[1]) for at16, b16 in products]
    cut = [kk % (ROWS_STEPS * LANES) == 0 for _, kk, _ in dims]
    blocks = [(m // N_DEV, n) for m, _, n in dims]

    def body(pl_ref, *refs):
        ins, outs, scratch = refs[:2 * n_prod], refs[2 * n_prod:4 * n_prod], refs[4 * n_prod:]
        c, step = pl_ref[0], pl.program_id(0)

        def multiply(i):
            a_ref, b_ref, acc = ins[2 * i], ins[2 * i + 1], scratch[5 * i]

            @pl.when(step == 0)
            def _():
                acc[...] = _dot(a_ref[...], b_ref[...])

            if cut[i]:
                @pl.when(step > 0)
                def _():
                    acc[...] += _dot(a_ref[...], b_ref[...])

        def rows(i, owner):
            return pl.ds(pl.multiple_of(owner * blocks[i][0], blocks[i][0]), blocks[i][0])

        def send_sibling_side(i):
            acc, stage, land, send, recv = scratch[5 * i:5 * i + 5]
            for p in range(N_CHIPS):
                stage[p] = acc[rows(i, 2 * p + 1 - c), :].astype(BF16)
            swaps = [_swap_with_sibling(p, stage, land, send, recv) for p in range(N_CHIPS)]
            for swap in swaps:
                swap.start()
            return swaps

        def add_my_side(i, swaps):
            acc, land = scratch[5 * i], scratch[5 * i + 2]
            cs_ref, own_ref = outs[2 * i:2 * i + 2]
            for p in range(N_CHIPS):
                swaps[p].wait()
                total = acc[rows(i, 2 * p + c), :] + land[p].astype(F32)
                cs_ref[p] = total.astype(BF16)

                @pl.when(p == pl_ref[1])
                def _():
                    own_ref[...] = total

        signal_sibling, sibling_is_in = _sibling_barrier()
        pl.when(step == 0)(signal_sibling)
        for i in range(n_prod):
            multiply(i)

        @pl.when(step == ROWS_STEPS - 1)
        def _():
            sibling_is_in()
            swaps = [send_sibling_side(i) for i in range(n_prod)]
            for i in range(n_prod):
                add_my_side(i, swaps[i])

    in_specs, out_specs, out_shape, scratch = [pl.BlockSpec(memory_space=pltpu.SMEM)], [], [], []
    for (m, kk, n), cut_i, block in zip(dims, cut, blocks):
        chunk = kk // ROWS_STEPS
        in_specs += ([pl.BlockSpec((m, chunk), lambda i: (0, i)), pl.BlockSpec((chunk, n), lambda i: (i, 0))]
                     if cut_i else [_const((m, kk)), _const((kk, n))])
        out_specs += [_acc((N_CHIPS,) + block), _acc(block)]
        out_shape += [jax.ShapeDtypeStruct((N_CHIPS,) + block, BF16), jax.ShapeDtypeStruct(block, F32)]
        scratch += [pltpu.VMEM((m, n), F32)] + _pair_scratch(block)
    out = pl.pallas_call(
        body, grid=(ROWS_STEPS,), name=name, in_specs=in_specs, out_specs=out_specs, out_shape=out_shape,
        scratch_shapes=scratch, compiler_params=_cparams(56, collective_id=ID_WGRAD_ROWS),
    )(place, *[a for pair in products for a in pair])
    return [tuple(out[2 * i:2 * i + 2]) for i in range(n_prod)]


def _adamw_math(w, g, m, v):
    m = ADAM_B1 * m + (1.0 - ADAM_B1) * g
    v = ADAM_B2 * v + (1.0 - ADAM_B2) * jnp.square(g)
    m_hat = m / (1.0 - ADAM_B1 ** ADAM_STEP)
    v_hat = v / (1.0 - ADAM_B2 ** ADAM_STEP)
    delta = -ADAM_LR * (m_hat / (jnp.sqrt(v_hat) + ADAM_EPS) + ADAM_WD * w)
    return delta, m, v


ADAMW_STEPS = 4


def _adamw_shards(updates, name):
    names, nu = list(updates), len(updates)

    def body(*refs):
        ins, outs = refs[:5 * nu], refs[5 * nu:]
        for i in range(nu):
            o_ref, r_ref, w_ref, m_ref, v_ref = ins[5 * i:5 * i + 5]
            g_out, d_out, m_out, v_out = outs[4 * i:4 * i + 4]
            g = o_ref[...] + r_ref[0].astype(F32) + r_ref[1].astype(F32) + r_ref[2].astype(F32)
            g_out[...] = g
            d_out[...], m_out[...], v_out[...] = _adamw_math(w_ref[...], g, m_ref[...], v_ref[...])

    in_specs, out_specs = [], []
    for n in names:
        rows, cols = updates[n][2].shape
        chunk = pl.BlockSpec((rows // ADAMW_STEPS, cols), lambda i: (i, 0))
        in_specs += [chunk, pl.BlockSpec((N_CHIPS - 1, rows // ADAMW_STEPS, cols), lambda i: (0, i, 0))] + [chunk] * 3
        out_specs += [chunk] * 4
    out = pl.pallas_call(
        body, grid=(ADAMW_STEPS,), name=name, in_specs=in_specs, out_specs=out_specs,
        out_shape=[jax.ShapeDtypeStruct(updates[n][2].shape, F32) for n in names for _ in range(4)],
        compiler_params=_cparams(56))(*[a for n in names for a in updates[n]])
    return {n: out[4 * i:4 * i + 4] for i, n in enumerate(names)}


def _place():
    x, y, c = lax.axis_index("x"), lax.axis_index("y"), lax.axis_index("c")
    chips = [(1 - x, y), (x, 1 - y), (1 - x, 1 - y)]
    return x, y, c, chips


def _gather_steps(ins, outs, send, recv, lsem, own_barrier=True):
    nt = len(ins)
    x, y, c, (xn, yn, diag) = _place()
    me, sib = (x, y, c), (x, y, 1 - c)

    def slot(t, px, py, pc):
        return outs[t].at[4 * px + 2 * py + pc]

    def copy(t, k, block, to, src=None):
        return pltpu.make_async_remote_copy(
            src_ref=slot(t, *block) if src is None else src, dst_ref=slot(t, *block),
            send_sem=send.at[t, k], recv_sem=recv.at[t, k], device_id=to, device_id_type=MESH)

    mine = [pltpu.make_async_copy(ins[t], slot(t, *me), lsem.at[t]) for t in range(nt)]
    first = [copy(t, k, me, to, src=ins[t]) for t in range(nt) for k, to in ((0, sib), (1, (*xn, c)), (2, (*yn, c)))]

    if own_barrier:
        signal_peers, peers_are_in = _own_barrier([sib, (*xn, c), (*yn, c)])

    def enter():
        if own_barrier:
            signal_peers()
        for cp in mine:
            cp.start()

    def start():
        if own_barrier:
            peers_are_in()
        for cp in first:
            cp.start()

    def landed(k, chip, also_to=None):
        for t in range(nt):
            copy(t, k, (*chip, c), me).wait_recv()
            if also_to is not None:
                copy(t, 3, (*chip, c), (*also_to, c)).start()
            copy(t, 3 + k, (*chip, c), sib).start()

    def relay():
        @pl.when(c == 0)
        def _():
            landed(1, xn, also_to=yn)
            landed(2, yn)

        @pl.when(c == 1)
        def _():
            landed(2, yn, also_to=xn)
            landed(1, xn)

    def finish():
        landed(3, diag)
        for t in range(nt):
            copy(t, 0, sib, me).wait_recv()
            for k, chip in ((4, xn), (5, yn), (6, diag)):
                copy(t, k, (*chip, 1 - c), me).wait_recv()
            for k in range(7):
                copy(t, k, me, sib).wait_send()
        for cp in mine:
            cp.wait()

    return enter, start, relay, finish


def _gather_scratch(nt):
    return [pltpu.SemaphoreType.DMA((nt, 7)), pltpu.SemaphoreType.DMA((nt, 7)), pltpu.SemaphoreType.DMA((nt,))]


def _gathered_shapes(shards):
    return [jax.ShapeDtypeStruct((N_DEV,) + s.shape, s.dtype) for s in shards]


def _call_with_gather(body, n_grid, shards, *, name, in_specs, out_specs, out_shape, scratch_shapes, vmem_mb, args,
                      collective_id=None):
    assert (collective_id is None) == (not shards)
    ng, n_in, n_out = len(shards), len(in_specs), len(out_specs)

    def wrapped(*refs):
        ins, shard_refs = refs[:n_in], refs[n_in:n_in + ng]
        outs = refs[n_in + ng:n_in + ng + n_out]
        whole_refs = refs[n_in + ng + n_out:n_in + 2 * ng + n_out]
        scratch = refs[n_in + 2 * ng + n_out:]
        if ng:
            enter, start, relay, finish = _gather_steps(shard_refs, whole_refs, *scratch[len(scratch_shapes):])
            pl.when(pl.program_id(0) == 0)(enter)
            pl.when(pl.program_id(0) == 0)(start)
            pl.when(pl.program_id(0) == n_grid // 2)(relay)
        body(*ins, *outs, *scratch[:len(scratch_shapes)])
        if ng:
            pl.when(pl.program_id(0) == n_grid - 1)(finish)

    return pl.pallas_call(
        wrapped, grid=(n_grid,), name=name,
        in_specs=list(in_specs) + [ANY] * ng, out_specs=list(out_specs) + [ANY] * ng,
        out_shape=list(out_shape) + _gathered_shapes(shards),
        scratch_shapes=list(scratch_shapes) + (_gather_scratch(ng) if ng else []),
        compiler_params=_cparams(vmem_mb, **({"collective_id": collective_id} if shards else {})))(*args, *shards)


def _chips_steps(ins, outs, send, recv):
    _, _, c, chips = _place()
    copies = [pltpu.make_async_remote_copy(
        src_ref=ins[t].at[2 * px + py], dst_ref=outs[t].at[j], send_sem=send.at[t, j], recv_sem=recv.at[t, j],
        device_id=(px, py, c), device_id_type=MESH) for t in range(len(ins)) for j, (px, py) in enumerate(chips)]

    def start():
        for cp in copies:
            cp.start()

    def finish():
        for cp in copies:
            cp.wait()

    return start, finish


def _chips_scratch(nt):
    return [pltpu.SemaphoreType.DMA((nt, 3)), pltpu.SemaphoreType.DMA((nt, 3))]


def _chips_shapes(cs16s):
    return [jax.ShapeDtypeStruct((3,) + g.shape[1:], g.dtype) for g in cs16s]


SMALL = (("g_pre_mix", 0, 0, D), ("g_mem", 1, 0, D), ("g_post_mix", 2, 0, D), ("g_attn_out", 3, 0, AW),
         ("g_conv_out", 3, AW, CW), ("g_xattn_out", 3, AW + CW, XW), ("g_post_mlp", 4, 0, D), ("g_pre_mlp", 5, 0, D))
CONV_ROW = 8
PACK_ROWS = 16


LOSS_ROW = 15


def _small_reduce_steps(accs, tot_ref, pack, land, send, recv):
    acc_in, acc_mem, acc_mix, acc_mlp, acc_cw, acc_loss = accs
    x, y, c, _ = _place()
    me = 4 * x + 2 * y + c
    copies = []
    for k in range(1, N_DEV):
        kx, ky, kc = (k >> 2) & 1, (k >> 1) & 1, k & 1
        peer = (1 - x if kx else x, 1 - y if ky else y, 1 - c if kc else c)
        copies.append(pltpu.make_async_remote_copy(
            src_ref=pack, dst_ref=land.at[me], send_sem=send.at[k - 1], recv_sem=recv.at[k - 1],
            device_id=peer, device_id_type=MESH))

    def start():
        pack[...] = jnp.zeros_like(pack)
        pack[0:1, :] = acc_in[0:1, :]
        pack[1:2, :] = acc_mem[0:1, :]
        pack[2:4, :] = acc_mix[0:2, :]
        pack[4:6, :] = acc_mlp[0:2, :]
        pack[CONV_ROW:CONV_ROW + 3, 0:CW] = acc_cw[0:3, :]
        pack[LOSS_ROW:LOSS_ROW + 1, 0:LANES] = acc_loss[0:1, :]
        land[me] = pack[...]
        for cp in copies:
            cp.start()

    def finish():
        for cp in copies:
            cp.wait()
        tot = land[0]
        for s in range(1, N_DEV):
            tot = tot + land[s]
        tot_ref[...] = tot

    return start, finish


def _small_reduce_scratch():
    return [pltpu.VMEM((PACK_ROWS, D), F32), pltpu.VMEM((N_DEV, PACK_ROWS, D), F32),
            pltpu.SemaphoreType.DMA((N_DEV - 1,)), pltpu.SemaphoreType.DMA((N_DEV - 1,))]


def _small_update(tot, me, params):
    flat = [a for n, _, _, _ in SMALL for a in params[n]] + list(params["conv_w"])
    n_par = len(SMALL) + 1
    tap_cols = CW // N_DEV

    def body(*refs):
        me_ref, tot_ref = refs[0:2]
        ins = refs[2:2 + 3 * n_par]
        loss_out = refs[2 + 3 * n_par]
        outs = refs[3 + 3 * n_par:]
        tot = tot_ref[...]
        loss_out[...] = jnp.broadcast_to(tot[LOSS_ROW:LOSS_ROW + 1, 0:LANES], loss_out.shape)

        def update(i, g):
            w_ref, m_ref, v_ref = ins[3 * i:3 * i + 3]
            for o_ref, res in zip(outs[4 * i:4 * i + 4], (g,) + _adamw_math(w_ref[...], g, m_ref[...], v_ref[...])):
                if len(o_ref.shape) == 3:
                    for t in range(o_ref.shape[0]):
                        o_ref[t] = res[t:t + 1, :]
                else:
                    o_ref[...] = res

        for i, (_, row, lane0, width) in enumerate(SMALL):
            update(i, tot[row:row + 1, lane0:lane0 + width])
        me = me_ref[0]
        taps = pltpu.roll(tot[CONV_ROW:CONV_ROW + SUBLANES, 0:CW], jnp.where(me == 0, 0, CW - me * tap_cols), 1)
        update(n_par - 1, taps[0:3, 0:tap_cols])

    shapes = [jax.ShapeDtypeStruct(params[n][0].shape, F32) for n, _, _, _ in SMALL] + [
        jax.ShapeDtypeStruct((3, 1, tap_cols), F32)]
    vmem = pl.BlockSpec(memory_space=pltpu.VMEM)
    loss, *out = pl.pallas_call(
        body, name="small_update",
        in_specs=[pl.BlockSpec(memory_space=pltpu.SMEM)] + [vmem] * (1 + 3 * n_par),
        out_shape=[jax.ShapeDtypeStruct((SUBLANES, LANES), F32)] + [s for s in shapes for _ in range(4)],
    )(me, tot, *flat)
    names = [n for n, _, _, _ in SMALL] + ["conv_w"]
    return loss[0, 0], {n: out[4 * i:4 * i + 4] for i, n in enumerate(names)}


def _local_step(x, mem, pos, gains, shards, tgt, place):
    half = HEAD // 2
    inv_freq = jnp.float32(ROPE_THETA) ** (-(jnp.arange(half, dtype=F32) * 2.0 / HEAD))
    invf = jnp.tile(inv_freq, LANES // half)[None, :]
    sgn = jnp.tile(jnp.concatenate([-jnp.ones((half,), F32), jnp.ones((half,), F32)]), LANES // HEAD)[None, :]
    cos, sins, win8 = _rope_table(pos.astype(F32).reshape(S, 1), invf, sgn, [shards["w_in"]])
    wdn_left, wdn_right = shards["w_down"][:, 0:D // 2], shards["w_down"][:, D // 2:]
    q, kvp, bcu, qx16, ht16, win16, wout8, wkv8, conv8, wdn8_right = _in_proj(
        x, gains["g_pre_mix"], win8, cos, sins, [shards["w_out"], shards["w_mem_kv"], shards["conv_w"], wdn_right])
    wout16, wkv16 = wout8.reshape(D, D), wkv8.reshape(D, 2 * XW)
    cw_full = conv8[:, 0:3, 0:CW // N_DEV].transpose(1, 0, 2).reshape(3, CW)
    cw8 = jnp.zeros((SUBLANES, CW), F32).at[0:3].set(cw_full)
    y_attn, ltot, wup8, wdn8_left = _attn_fwd(q, kvp, [shards["w_up"], wdn_left])
    wdn_halves = (wdn8_left.reshape(FF, D // 2), wdn8_right.reshape(FF, D // 2))
    memnt16, kv16 = _mem_fwd(mem, gains["g_mem"], wkv16)
    ypre, yt16, y2, x1 = _mix_out(y_attn, bcu, qx16, kv16, cw8, gains["g_attn_out"], gains["g_conv_out"],
                                 gains["g_xattn_out"], gains["g_post_mix"], wout16, x, [])
    a16, du16, h2t16, df2t16, dx1, loss8, dg_mlp = _mlp(
        x1, tgt, gains["g_pre_mlp"], gains["g_post_mlp"], wup8, wdn_halves)

    sums = {"w_up": _wgrad_cols(place, h2t16, du16, FF_BLK, "wgrad_up", ID_WGRAD_UP),
            "w_down": _wgrad_cols(place, df2t16, a16, FF_BLK, "wgrad_down", ID_WGRAD_DOWN, square_b=True,
                                  transpose_out=True)}

    head_id = jnp.arange(AW, dtype=jnp.int32) // HEAD
    head_ones = (head_id[:, None] == head_id[None, :]).astype(BF16)
    dy2_16, qdo, ld, dbcu, dqx, dgs, dcw, dkv = _mix_out_bwd(
        dx1, y2, ypre, y_attn, ltot, head_ones, q, bcu, qx16, kv16, cw8, gains["g_post_mix"], gains["g_attn_out"],
        gains["g_conv_out"], gains["g_xattn_out"], wout16)
    dkv16, dg_mem = _mem_bwd(mem, gains["g_mem"], wkv16, dkv)
    sums["w_mem_kv"], sums["w_out"] = _wgrad_rows(place, [(memnt16, dkv16), (yt16, dy2_16)], "wgrad_mem_kv_out")
    out = _attn_bwd(qdo, kvp, ld, [s[0] for s in sums.values()])
    dqkv, landed = out[:9], out[9:]
    reduced = {n: (s[1], landed[t]) for t, (n, s) in enumerate(sums.items())}
    dproj16, grad_x, dg_in = _in_proj_bwd(dqkv, dbcu, dqx, cos, sins, win16, x, gains["g_pre_mix"], dx1)

    _, in_own, in_landed, small_tot = _wgrad_cols(place, ht16, dproj16, PW // N_DEV, "wgrad_in", ID_WGRAD_IN,
                                                  transpose_out=True, to_chips=True,
                                                  small=(dg_in, dg_mem, dgs, dg_mlp, dcw, loss8))
    reduced["w_in"] = (in_own, in_landed)
    return grad_x, reduced, small_tot


BIG = ("w_in", "w_mem_kv", "w_out", "w_up", "w_down")
ORDER = ("g_pre_mix", "g_mem", "w_in", "w_mem_kv", "conv_w", "g_attn_out", "g_conv_out", "g_xattn_out", "w_out",
         "g_post_mix", "g_pre_mlp", "w_up", "w_down", "g_post_mlp")


def kernel(x, mem, positions, g_pre_mix, g_mem, w_in, w_mem_kv, conv_w, g_attn_out, g_conv_out, g_xattn_out, w_out, g_post_mix, g_pre_mlp, w_up, w_down, g_post_mlp, loss_target, m_g_pre_mix, m_g_mem, m_w_in, m_w_mem_kv, m_conv_w, m_g_attn_out, m_g_conv_out, m_g_xattn_out, m_w_out, m_g_post_mix, m_g_pre_mlp, m_w_up, m_w_down, m_g_post_mlp, v_g_pre_mix, v_g_mem, v_w_in, v_w_mem_kv, v_conv_w, v_g_attn_out, v_g_conv_out, v_g_xattn_out, v_w_out, v_g_post_mix, v_g_pre_mlp, v_w_up, v_w_down, v_g_post_mlp):
    w = dict(g_pre_mix=g_pre_mix, g_mem=g_mem, w_in=w_in, w_mem_kv=w_mem_kv, conv_w=conv_w, g_attn_out=g_attn_out,
             g_conv_out=g_conv_out, g_xattn_out=g_xattn_out, w_out=w_out, g_post_mix=g_post_mix, g_pre_mlp=g_pre_mlp,
             w_up=w_up, w_down=w_down, g_post_mlp=g_post_mlp)
    mo = dict(g_pre_mix=m_g_pre_mix, g_mem=m_g_mem, w_in=m_w_in, w_mem_kv=m_w_mem_kv, conv_w=m_conv_w,
              g_attn_out=m_g_attn_out, g_conv_out=m_g_conv_out, g_xattn_out=m_g_xattn_out, w_out=m_w_out,
              g_post_mix=m_g_post_mix, g_pre_mlp=m_g_pre_mlp, w_up=m_w_up, w_down=m_w_down, g_post_mlp=m_g_post_mlp)
    vo = dict(g_pre_mix=v_g_pre_mix, g_mem=v_g_mem, w_in=v_w_in, w_mem_kv=v_w_mem_kv, conv_w=v_conv_w,
              g_attn_out=v_g_attn_out, g_conv_out=v_g_conv_out, g_xattn_out=v_g_xattn_out, w_out=v_w_out,
              g_post_mix=v_g_post_mix, g_pre_mlp=v_g_pre_mlp, w_up=v_w_up, w_down=v_w_down, g_post_mlp=v_g_post_mlp)

    xi, yi, ci = lax.axis_index("x"), lax.axis_index("y"), lax.axis_index("c")
    me = 4 * xi + 2 * yi + ci
    place = jnp.stack([ci, 2 * xi + yi]).astype(jnp.int32)

    shards = {n: w[n][0].astype(BF16) for n in BIG}
    shards["conv_w"] = jnp.zeros((SUBLANES, LANES), F32).at[0:3, 0:CW // N_DEV].set(conv_w[0])

    gains = {n: w[n] for n, _, _, _ in SMALL}
    grad_x, reduced, small_tot = _local_step(x[0], mem[0], positions[0], gains, shards, loss_target[0], place)

    def shard(n, a):
        return a[0].T if n == "w_in" else a[0]

    updated = _adamw_shards({n: (*reduced[n], shard(n, w[n]), shard(n, mo[n]), shard(n, vo[n])) for n in BIG},
                            "adamw")
    grad, delta, new_m, new_v = {}, {}, {}, {}
    for n, res in updated.items():
        grad[n], delta[n], new_m[n], new_v[n] = [(a.T if n == "w_in" else a)[None] for a in res]

    params = {n: (w[n], mo[n], vo[n]) for n, _, _, _ in SMALL}
    params["conv_w"] = (w["conv_w"][0], mo["conv_w"][0], vo["conv_w"][0])
    loss, small = _small_update(small_tot, me.reshape(1).astype(jnp.int32), params)
    for n, (g, d_, m_, v_) in small.items():
        lead = (lambda a: a.reshape(conv_w.shape)) if n == "conv_w" else (lambda a: a)
        grad[n], delta[n], new_m[n], new_v[n] = lead(g), lead(d_), lead(m_), lead(v_)

    return (loss, grad_x[None], *[grad[n] for n in ORDER], *[delta[n] for n in ORDER],
            *[new_m[n] for n in ORDER], *[new_v[n] for n in ORDER])
```
